```python
import math
import jax, jax.numpy as jnp
from jax import lax
import numpy as np

D_MODEL = 1024
BATCH = 8
SEQ = 8192
DEPTH = 1

HEAD_DIM = 64
FOX_HEADS = (D_MODEL // 2) // HEAD_DIM
SWA_HEADS = (D_MODEL // 2) // HEAD_DIM
SWA_KV_HEADS = max(1, SWA_HEADS // 4)
SWA_GROUP = SWA_HEADS // SWA_KV_HEADS
FOX_WIDTH = FOX_HEADS * HEAD_DIM
SWA_WIDTH = SWA_HEADS * HEAD_DIM
SWA_KV_WIDTH = SWA_KV_HEADS * HEAD_DIM
D_MIX = FOX_WIDTH + SWA_WIDTH
BLOCK = 128
WINDOW = 128
NUM_BUCKETS = 32
MAX_DISTANCE = 128
LN_EPS = 1e-5
NEG_INF = -1e30

_SPLIT_SIZES = (
    FOX_WIDTH,
    FOX_WIDTH,
    FOX_WIDTH,
    FOX_HEADS,
    FOX_WIDTH,
    SWA_WIDTH,
    SWA_KV_WIDTH,
    SWA_KV_WIDTH,
    SWA_WIDTH,
)
D_IN = sum(_SPLIT_SIZES)
SPLIT_POINTS = [int(v) for v in np.cumsum(_SPLIT_SIZES)[:-1]]

kernel_name = "hymba_fox_swa_sink_deepnorm"


def layer_norm(x, g, b):
    xf = x.astype(jnp.float32)
    mu = jnp.mean(xf, axis=-1, keepdims=True)
    var = jnp.mean(jnp.square(xf - mu), axis=-1, keepdims=True)
    y = (xf - mu) * lax.rsqrt(var + LN_EPS)
    return (y * g.astype(jnp.float32) + b.astype(jnp.float32)).astype(x.dtype)


def t5_causal_bucket(rel):
    max_exact = NUM_BUCKETS // 2
    is_small = rel < max_exact
    relf = jnp.maximum(rel, 1).astype(jnp.float32)
    large = max_exact + (jnp.log(relf / max_exact) / math.log(MAX_DISTANCE / max_exact)
                         * (NUM_BUCKETS - max_exact)).astype(jnp.int32)
    large = jnp.minimum(large, NUM_BUCKETS - 1)
    return jnp.where(is_small, rel, large)


def forgetting_attention(q, k, v, log_f):
    B, S, H, dh = q.shape
    nb = S // BLOCK
    scale = 1.0 / math.sqrt(dh)
    cum = lax.cumsum(log_f, axis=1)
    cum_k = jnp.transpose(cum, (0, 2, 1))
    q_blocks = jnp.transpose(q.reshape(B, nb, BLOCK, H, dh), (1, 0, 2, 3, 4))
    c_blocks = jnp.transpose(cum.reshape(B, nb, BLOCK, H), (1, 0, 3, 2))
    kpos = jnp.arange(S)

    def one_block(args):
        q_blk, c_blk, i = args
        s = jnp.einsum('bqhd,bkhd->bhqk', q_blk, k,
                       preferred_element_type=jnp.float32) * scale
        s = s + (c_blk[..., :, None] - cum_k[..., None, :])
        qpos = i * BLOCK + jnp.arange(BLOCK)
        causal = kpos[None, :] <= qpos[:, None]
        s = jnp.where(causal[None, None], s, NEG_INF)
        p = jax.nn.softmax(s, axis=-1)
        return jnp.einsum('bhqk,bkhd->bqhd', p.astype(v.dtype), v)

    out = lax.map(one_block, (q_blocks, c_blocks, jnp.arange(nb)))
    return jnp.transpose(out, (1, 0, 2, 3, 4)).reshape(B, S, H * dh)


def swa_sink_attention(q, k, v, rel_bias, sink):
    B, S, Hq, dh = q.shape
    nb = S // BLOCK
    scale = 1.0 / math.sqrt(dh)
    qb = q.reshape(B, nb, BLOCK, SWA_KV_HEADS, SWA_GROUP, dh)
    pad = jnp.zeros((B, BLOCK, SWA_KV_HEADS, dh), k.dtype)
    kp = jnp.concatenate([pad, k], axis=1).reshape(B, nb + 1, BLOCK, SWA_KV_HEADS, dh)
    vp = jnp.concatenate([pad, v], axis=1).reshape(B, nb + 1, BLOCK, SWA_KV_HEADS, dh)
    kb = jnp.concatenate([kp[:, :-1], kp[:, 1:]], axis=2)
    vb = jnp.concatenate([vp[:, :-1], vp[:, 1:]], axis=2)

    scores = jnp.einsum('bnqkgd,bnskd->bnkgqs', qb, kb,
                        preferred_element_type=jnp.float32) * scale

    qi = jnp.arange(BLOCK)[:, None]
    kj = jnp.arange(2 * BLOCK)[None, :]
    rel = qi + BLOCK - kj
    band = (rel >= 0) & (rel < WINDOW)
    s_abs = (jnp.arange(nb)[:, None] - 1) * BLOCK + jnp.arange(2 * BLOCK)[None, :]
    mask = band[None] & (s_abs >= 0)[:, None, :]

    bucket = t5_causal_bucket(jnp.maximum(rel, 0))
    bias = jnp.transpose(rel_bias.astype(jnp.float32)[bucket], (2, 0, 1))
    bias = bias.reshape(SWA_KV_HEADS, SWA_GROUP, BLOCK, 2 * BLOCK)
    scores = scores + bias[None, None]
    scores = jnp.where(mask[None, :, None, None], scores, NEG_INF)

    sink_logit = jnp.broadcast_to(
        sink.astype(jnp.float32).reshape(1, 1, SWA_KV_HEADS, SWA_GROUP, 1, 1),
        scores.shape[:-1] + (1,))
    probs = jax.nn.softmax(jnp.concatenate([scores, sink_logit], axis=-1), axis=-1)[..., :-1]
    out = jnp.einsum('bnkgqs,bnskd->bnqkgd', probs.astype(v.dtype), vb)
    return out.reshape(B, S, Hq * dh)


def _fwd_setup_inputs(seed: int = 0) -> dict:
    key = jax.random.key(seed)
    ks = jax.random.split(key, 8)
    beta = (8.0 * DEPTH) ** -0.25
    x = jax.random.normal(ks[0], (BATCH, SEQ, D_MODEL), jnp.float32)
    w_in = jax.random.normal(ks[1], (DEPTH, D_MODEL, D_IN), jnp.float32) * D_MODEL ** -0.5
    b_f = jax.random.uniform(ks[2], (DEPTH, FOX_HEADS), jnp.float32, 1.0, 5.0)
    rel_bias = jax.random.normal(ks[3], (NUM_BUCKETS, SWA_HEADS), jnp.float32) * 0.5
    sink = jax.random.normal(ks[4], (DEPTH, SWA_HEADS), jnp.float32) * 0.5
    w_o = jax.random.normal(ks[5], (DEPTH, D_MIX, D_MODEL), jnp.float32) * (D_MIX ** -0.5) * beta
    ln_g = 1.0 + 0.02 * jax.random.normal(ks[6], (DEPTH, D_MODEL), jnp.float32)
    ln_b = 0.02 * jax.random.normal(ks[7], (DEPTH, D_MODEL), jnp.float32)
    return {"x": x, "w_in": w_in, "b_f": b_f, "rel_bias": rel_bias, "sink": sink,
            "w_o": w_o, "ln_g": ln_g, "ln_b": ln_b}


def _fwd_reference(x, w_in, b_f, rel_bias, sink, w_o, ln_g, ln_b):
    alpha = (2.0 * DEPTH) ** 0.25
    B, S, _ = x.shape
    for l in range(DEPTH):
        proj = jnp.einsum('bsd,de->bse', x, w_in[l])
        fq, fk, fv, ff, fz, sq, sk, sv, sz = jnp.split(proj, SPLIT_POINTS, axis=-1)
        log_f = jax.nn.log_sigmoid(ff.astype(jnp.float32) + b_f[l].astype(jnp.float32))
        fox = forgetting_attention(
            fq.reshape(B, S, FOX_HEADS, HEAD_DIM),
            fk.reshape(B, S, FOX_HEADS, HEAD_DIM),
            fv.reshape(B, S, FOX_HEADS, HEAD_DIM),
            log_f)
        swa = swa_sink_attention(
            sq.reshape(B, S, SWA_HEADS, HEAD_DIM),
            sk.reshape(B, S, SWA_KV_HEADS, HEAD_DIM),
            sv.reshape(B, S, SWA_KV_HEADS, HEAD_DIM),
            rel_bias, sink[l])
        mixed = jnp.concatenate([fox * jax.nn.silu(fz), swa * jax.nn.silu(sz)], axis=-1)
        y = jnp.einsum('bse,ed->bsd', mixed, w_o[l])
        x = layer_norm(alpha * x + y, ln_g[l], ln_b[l])
    return x


import jax as _jax
import jax.numpy as _jnp

TWIN_FORMAT = 'train_step'
FWD_PARAMS = ['x', 'w_in', 'b_f', 'rel_bias', 'sink', 'w_o', 'ln_g', 'ln_b']
TWIN_WEIGHTS = ['w_in', 'b_f', 'rel_bias', 'sink', 'w_o', 'ln_g', 'ln_b']
TWIN_DIFF_INPUT = 'x'
TWIN_INPUTS = ['x', 'w_in', 'b_f', 'rel_bias', 'sink', 'w_o', 'ln_g', 'ln_b', 'loss_target', 'm_w_in', 'm_b_f', 'm_rel_bias', 'm_sink', 'm_w_o', 'm_ln_g', 'm_ln_b', 'v_w_in', 'v_b_f', 'v_rel_bias', 'v_sink', 'v_w_o', 'v_ln_g', 'v_ln_b']
TWIN_OUTPUTS = ['loss', 'grad_x', 'grad_w_in', 'grad_b_f', 'grad_rel_bias', 'grad_sink', 'grad_w_o', 'grad_ln_g', 'grad_ln_b', 'delta_w_in', 'delta_b_f', 'delta_rel_bias', 'delta_sink', 'delta_w_o', 'delta_ln_g', 'delta_ln_b', 'new_m_w_in', 'new_m_b_f', 'new_m_rel_bias', 'new_m_sink', 'new_m_w_o', 'new_m_ln_g', 'new_m_ln_b', 'new_v_w_in', 'new_v_b_f', 'new_v_rel_bias', 'new_v_sink', 'new_v_w_o', 'new_v_ln_g', 'new_v_ln_b']
TWIN_LEAF_KINDS = {'loss': 'loss', 'grad_x': 'grad_x', 'grad_w_in': 'grad_w', 'grad_b_f': 'grad_w', 'grad_rel_bias': 'grad_w', 'grad_sink': 'grad_w', 'grad_w_o': 'grad_w', 'grad_ln_g': 'grad_w', 'grad_ln_b': 'grad_w', 'delta_w_in': 'delta_w', 'delta_b_f': 'delta_w', 'delta_rel_bias': 'delta_w', 'delta_sink': 'delta_w', 'delta_w_o': 'delta_w', 'delta_ln_g': 'delta_w', 'delta_ln_b': 'delta_w', 'new_m_w_in': 'new_m', 'new_m_b_f': 'new_m', 'new_m_rel_bias': 'new_m', 'new_m_sink': 'new_m', 'new_m_w_o': 'new_m', 'new_m_ln_g': 'new_m', 'new_m_ln_b': 'new_m', 'new_v_w_in': 'new_v', 'new_v_b_f': 'new_v', 'new_v_rel_bias': 'new_v', 'new_v_sink': 'new_v', 'new_v_w_o': 'new_v', 'new_v_ln_g': 'new_v', 'new_v_ln_b': 'new_v'}


def _forward(args):
    return _fwd_reference(*[args[k] for k in FWD_PARAMS])


def _output_shape():
    def fwd():
        inp = _fwd_setup_inputs(0)
        return _fwd_reference(*[inp[k] for k in FWD_PARAMS])
    out = _jax.eval_shape(fwd)
    return out.shape, out.dtype

N_MICROBATCH = 1
ADAM_LR = 0.001
ADAM_B1 = 0.9
ADAM_B2 = 0.999
ADAM_EPS = 1e-08
ADAM_WD = 0.01
ADAM_STEP = 10
PER_EXAMPLE_BATCH_AXIS = {'x': 0, 'loss_target': 0}
SHARED_INPUTS = []
_WEIGHT_DTYPES = {'w_in': _jnp.float32, 'b_f': _jnp.float32, 'rel_bias': _jnp.float32, 'sink': _jnp.float32, 'w_o': _jnp.float32, 'ln_g': _jnp.float32, 'ln_b': _jnp.float32}
MOMENT_SCALE = {'w_in': 2.103967e-02, 'b_f': 8.050606e-02, 'rel_bias': 1.988650e-02, 'sink': 1.270819e-02, 'w_o': 3.458507e-02, 'ln_g': 6.397727e+01, 'ln_b': 9.251637e-01}


def _to_microbatches(a, axis):
    t = _jnp.moveaxis(a, axis, 0)
    t = t.reshape((N_MICROBATCH, t.shape[0] // N_MICROBATCH) + t.shape[1:])
    return _jnp.moveaxis(t, 1, axis + 1)


def setup_inputs(seed: int = 0) -> dict:
    inp = _fwd_setup_inputs(seed)
    key = _jax.random.fold_in(_jax.random.key(seed), 7919)
    shape, _ = _output_shape()
    out = dict(inp)
    out["loss_target"] = _jax.random.normal(_jax.random.fold_in(key, 0), shape, _jnp.float32)
    for i, name in enumerate(TWIN_WEIGHTS):
        w = inp[name].astype(_jnp.float32)
        if MOMENT_SCALE is None:
            s = _jnp.sqrt(_jnp.mean(_jnp.square(w)) + 1e-30)
        else:
            s = MOMENT_SCALE[name]
        km, kv = _jax.random.split(_jax.random.fold_in(key, i + 1))
        out[name] = w
        out["m_" + name] = s * _jax.random.normal(km, w.shape, _jnp.float32)
        out["v_" + name] = (s * s) * _jax.random.uniform(kv, w.shape, _jnp.float32, 0.5, 1.5)
    if N_MICROBATCH > 1:
        for name, axis in PER_EXAMPLE_BATCH_AXIS.items():
            out[name] = _to_microbatches(out[name], axis)
    return {'x': out['x'], 'w_in': out['w_in'], 'b_f': out['b_f'], 'rel_bias': out['rel_bias'], 'sink': out['sink'], 'w_o': out['w_o'], 'ln_g': out['ln_g'], 'ln_b': out['ln_b'], 'loss_target': out['loss_target'], 'm_w_in': out['m_w_in'], 'm_b_f': out['m_b_f'], 'm_rel_bias': out['m_rel_bias'], 'm_sink': out['m_sink'], 'm_w_o': out['m_w_o'], 'm_ln_g': out['m_ln_g'], 'm_ln_b': out['m_ln_b'], 'v_w_in': out['v_w_in'], 'v_b_f': out['v_b_f'], 'v_rel_bias': out['v_rel_bias'], 'v_sink': out['v_sink'], 'v_w_o': out['v_w_o'], 'v_ln_g': out['v_ln_g'], 'v_ln_b': out['v_ln_b']}


def _loss(weights, diff, rest, loss_target):
    with _jax.named_scope("forward"):
        args = {**rest, TWIN_DIFF_INPUT: diff, **{k: w.astype(_WEIGHT_DTYPES[k]) for k, w in weights.items()}}
        y = _forward(args)
    with _jax.named_scope("loss_head"):
        err = _jnp.square(y.astype(_jnp.float32) - loss_target)
        return 0.5 * _jnp.sum(_jnp.mean(err, axis=-1)) if err.ndim else 0.5 * err


def _adamw(w, g, m, v):
    m = ADAM_B1 * m + (1.0 - ADAM_B1) * g
    v = ADAM_B2 * v + (1.0 - ADAM_B2) * _jnp.square(g)
    m_hat = m / (1.0 - ADAM_B1 ** ADAM_STEP)
    v_hat = v / (1.0 - ADAM_B2 ** ADAM_STEP)
    delta = -ADAM_LR * (m_hat / (_jnp.sqrt(v_hat) + ADAM_EPS) + ADAM_WD * w)
    return delta, m, v


def reference(x, w_in, b_f, rel_bias, sink, w_o, ln_g, ln_b, loss_target, m_w_in, m_b_f, m_rel_bias, m_sink, m_w_o, m_ln_g, m_ln_b, v_w_in, v_b_f, v_rel_bias, v_sink, v_w_o, v_ln_g, v_ln_b):
    given = dict(x=x, w_in=w_in, b_f=b_f, rel_bias=rel_bias, sink=sink, w_o=w_o, ln_g=ln_g, ln_b=ln_b, loss_target=loss_target, m_w_in=m_w_in, m_b_f=m_b_f, m_rel_bias=m_rel_bias, m_sink=m_sink, m_w_o=m_w_o, m_ln_g=m_ln_g, m_ln_b=m_ln_b, v_w_in=v_w_in, v_b_f=v_b_f, v_rel_bias=v_rel_bias, v_sink=v_sink, v_w_o=v_w_o, v_ln_g=v_ln_g, v_ln_b=v_ln_b)
    weights = {n: given[n] for n in TWIN_WEIGHTS}
    shared = {n: given[n] for n in SHARED_INPUTS}
    per_example = {n: given[n] for n in ['x']}
    grad_fn = _jax.value_and_grad(_loss, argnums=(0, 1))

    def one_microbatch(ex, loss_target):
        ex = dict(ex)
        diff = ex.pop(TWIN_DIFF_INPUT)
        return grad_fn(weights, diff, {**shared, **ex}, loss_target)

    if N_MICROBATCH == 1:
        loss, (grad_w, grad_x) = one_microbatch(per_example, given["loss_target"])
    else:
        def body(carry, xs):
            loss_sum, grad_sum = carry
            l_k, (gw_k, gx_k) = one_microbatch(xs[0], xs[1])
            with _jax.named_scope("update"):
                return (loss_sum + l_k, _jax.tree.map(_jnp.add, grad_sum, gw_k)), gx_k

        init = (_jnp.zeros((), _jnp.float32), _jax.tree.map(_jnp.zeros_like, weights))
        (loss, grad_w), grad_x = _jax.lax.scan(body, init, (per_example, given["loss_target"]))
    with _jax.named_scope("update"):
        delta_w, new_m, new_v = {}, {}, {}
        for n in TWIN_WEIGHTS:
            delta_w[n], new_m[n], new_v[n] = _adamw(weights[n], grad_w[n], given["m_" + n], given["v_" + n])
    return (loss, grad_x, *[grad_w[n] for n in TWIN_WEIGHTS], *[delta_w[n] for n in TWIN_WEIGHTS],
            *[new_m[n] for n in TWIN_WEIGHTS], *[new_v[n] for n in TWIN_WEIGHTS])
```

```python
import functools
import math

import numpy as np
import jax
import jax.numpy as jnp
from jax import lax
from jax.experimental import pallas as pl
from jax.experimental.pallas import tpu as pltpu

f32 = jnp.float32
bf16 = jnp.bfloat16

D_MODEL = 1024
HEAD_DIM = 64
FOX_HEADS = 8
SWA_HEADS = 8
SWA_KV_HEADS = 2
SWA_GROUP = 4
BLOCK = 128
NUM_BUCKETS = 32
MAX_DISTANCE = 128
LN_EPS = 1e-5
NEG_INF = -1e30
ALPHA = 2.0 ** 0.25
SCALE = 1.0 / math.sqrt(HEAD_DIM)
D_IN = 3336

ADAM_LR = 0.001
ADAM_B1 = 0.9
ADAM_B2 = 0.999
ADAM_EPS = 1e-08
ADAM_WD = 0.01
ADAM_STEP = 10

N_DEV = 8
A_FQ, A_FK, A_FV, A_FZ, A_SQ, A_SK, A_SV, A_SZ, A_FF, A_W = 0, 512, 1024, 1536, 2048, 2560, 2688, 2816, 3328, 3456
O_FF0, O_FF1 = 1536, 1544

VMEM_LIMIT = 48 * 1024 * 1024
HIGHEST = lax.Precision.HIGHEST
NT = (((1,), (1,)), ((), ()))
TN = (((0,), (0,)), ((), ()))
MESH = pl.DeviceIdType.MESH
RELS = [(0, 0, 1), (0, 1, 0), (0, 1, 1), (1, 0, 0), (1, 0, 1), (1, 1, 0), (1, 1, 1)]


def _params(sem=None):
    return pltpu.CompilerParams(dimension_semantics=sem, vmem_limit_bytes=VMEM_LIMIT)


def _sds(shape, dtype):
    return jax.ShapeDtypeStruct(shape, dtype)


def _t5_bucket_table():
    qi = np.arange(BLOCK)[:, None]
    kj = np.arange(2 * BLOCK)[None, :]
    rel = qi + BLOCK - kj
    band = (rel >= 0) & (rel < BLOCK)
    relc = np.maximum(rel, 0)
    max_exact = NUM_BUCKETS // 2
    relf = np.maximum(relc, 1).astype(np.float32)
    large = max_exact + (np.log(relf / np.float32(max_exact)) / np.float32(math.log(MAX_DISTANCE / max_exact))
                         * np.float32(NUM_BUCKETS - max_exact)).astype(np.int32)
    large = np.minimum(large, NUM_BUCKETS - 1)
    bucket = np.where(relc < max_exact, relc, large).astype(np.int32)
    bucket = np.where(band, bucket, -1).astype(np.int32)
    return bucket


def _mesh_pos():
    return lax.axis_index("x"), lax.axis_index("y"), lax.axis_index("c")


def _dev_index(p):
    return 4 * p[0] + 2 * p[1] + p[2]


def _gather_call(xs):
    n = len(xs)

    def body(*refs):
        x_refs, o_refs = refs[:n], refs[n:2 * n]
        send_sems, recv_sems, local_sems = refs[2 * n:]
        x, y, c = _mesh_pos()
        me, sib = (x, y, c), (x, y, 1 - c)
        chips = [(1 - x, y), (x, 1 - y), (1 - x, 1 - y)]

        def copy(a, k, block, to, src=None):
            slot = o_refs[a].at[_dev_index(block)]
            return pltpu.make_async_remote_copy(
                src_ref=slot if src is None else src, dst_ref=slot,
                send_sem=send_sems.at[a * 7 + k], recv_sem=recv_sems.at[a * 7 + k],
                device_id=to, device_id_type=MESH)

        mine = [pltpu.make_async_copy(x_refs[a], o_refs[a].at[_dev_index(me)], local_sems.at[a]) for a in range(n)]
        for cp in mine:
            cp.start()
        first = []
        for a in range(n):
            first.append(copy(a, 0, me, sib, src=x_refs[a]))
            first += [copy(a, 1 + j, me, (*chip, c), src=x_refs[a]) for j, chip in enumerate(chips)]
        for cp in first:
            cp.start()
        passed = []
        for j, chip in enumerate(chips):
            for a in range(n):
                copy(a, 1 + j, (*chip, c), me).wait_recv()
                fwd = copy(a, 4 + j, (*chip, c), sib)
                fwd.start()
                passed.append(fwd)
        for a in range(n):
            copy(a, 0, sib, me).wait_recv()
            for j, chip in enumerate(chips):
                copy(a, 4 + j, (*chip, 1 - c), me).wait_recv()
        for cp in first + passed:
            cp.wait_send()
        for cp in mine:
            cp.wait()

    any_spec = pl.BlockSpec(memory_space=pl.ANY)
    return pl.pallas_call(
        body,
        name="gather_weights",
        out_shape=[_sds((N_DEV,) + a.shape, a.dtype) for a in xs],
        in_specs=[any_spec] * n,
        out_specs=[any_spec] * n,
        scratch_shapes=[pltpu.SemaphoreType.DMA((7 * n,)), pltpu.SemaphoreType.DMA((7 * n,)),
                        pltpu.SemaphoreType.DMA((n,))],
    )(*xs)


def _exchange_call(bs):
    n = len(bs)

    def body(*refs):
        b_refs, r_refs = refs[:n], refs[n:2 * n]
        send_sems, recv_sems, local_sems = refs[2 * n:]
        x, y, c = _mesh_pos()
        me_idx = _dev_index((x, y, c))
        mine = [pltpu.make_async_copy(b_refs[a].at[me_idx], r_refs[a].at[me_idx], local_sems.at[a]) for a in range(n)]
        for cp in mine:
            cp.start()
        sent = []
        for k, r in enumerate(RELS):
            peer = ((1 - x) if r[0] else x, (1 - y) if r[1] else y, (1 - c) if r[2] else c)
            pidx = _dev_index(peer)
            for a in range(n):
                out = pltpu.make_async_remote_copy(
                    src_ref=b_refs[a].at[pidx], dst_ref=r_refs[a].at[me_idx],
                    send_sem=send_sems.at[a * 7 + k], recv_sem=recv_sems.at[a * 7 + k],
                    device_id=peer, device_id_type=MESH)
                out.start()
                inc = pltpu.make_async_remote_copy(
                    src_ref=b_refs[a].at[pidx], dst_ref=r_refs[a].at[pidx],
                    send_sem=send_sems.at[a * 7 + k], recv_sem=recv_sems.at[a * 7 + k],
                    device_id=peer, device_id_type=MESH)
                sent.append((out, inc))
        for out, inc in sent:
            inc.wait_recv()
        for out, inc in sent:
            out.wait_send()
        for cp in mine:
            cp.wait()

    any_spec = pl.BlockSpec(memory_space=pl.ANY)
    return pl.pallas_call(
        body,
        name="exchange_grads",
        out_shape=[_sds(b.shape, b.dtype) for b in bs],
        in_specs=[any_spec] * n,
        out_specs=[any_spec] * n,
        scratch_shapes=[pltpu.SemaphoreType.DMA((7 * n,)), pltpu.SemaphoreType.DMA((7 * n,)),
                        pltpu.SemaphoreType.DMA((n,))],
    )(*bs)


def _proj_call(x2, w_al, tm):
    s_len = x2.shape[0]

    def body(x_ref, w_ref, qf_ref, kf_ref, vf_ref, fz_ref, qs_ref, ks_ref, vs_ref, sz_ref, fft_ref):
        xb = x_ref[...].astype(bf16)

        def seg(off, width):
            return jnp.dot(xb, w_ref[:, off:off + width], preferred_element_type=f32)

        def put_heads(ref, acc, nheads):
            for h in range(nheads):
                ref[h] = acc[:, h * HEAD_DIM:(h + 1) * HEAD_DIM].astype(bf16)

        put_heads(qf_ref, seg(A_FQ, 512) * SCALE, FOX_HEADS)
        put_heads(kf_ref, seg(A_FK, 512), FOX_HEADS)
        put_heads(vf_ref, seg(A_FV, 512), FOX_HEADS)
        fz_ref[...] = seg(A_FZ, 512)
        put_heads(qs_ref, seg(A_SQ, 512) * SCALE, SWA_HEADS)
        put_heads(ks_ref, seg(A_SK, 128), SWA_KV_HEADS)
        put_heads(vs_ref, seg(A_SV, 128), SWA_KV_HEADS)
        sz_ref[...] = seg(A_SZ, 512)
        fft_ref[...] = seg(A_FF, 128).T[:FOX_HEADS, :]

    def heads(nh):
        return pl.BlockSpec((nh, tm, HEAD_DIM), lambda i: (0, i, 0))

    wide = pl.BlockSpec((tm, 512), lambda i: (i, 0))
    return pl.pallas_call(
        body,
        name="proj_fwd",
        grid=(s_len // tm,),
        in_specs=[pl.BlockSpec((tm, D_MODEL), lambda i: (i, 0)), pl.BlockSpec((D_MODEL, A_W), lambda i: (0, 0))],
        out_specs=[heads(8), heads(8), heads(8), wide, heads(8), heads(2), heads(2), wide,
                   pl.BlockSpec((FOX_HEADS, tm), lambda i: (0, i))],
        out_shape=[_sds((8, s_len, HEAD_DIM), bf16)] * 3 + [_sds((s_len, 512), f32), _sds((8, s_len, HEAD_DIM), bf16),
                   _sds((2, s_len, HEAD_DIM), bf16), _sds((2, s_len, HEAD_DIM), bf16), _sds((s_len, 512), f32),
                   _sds((FOX_HEADS, s_len), f32)],
        compiler_params=_params(("arbitrary",)),
    )(x2, w_al)


CUM_CHUNK = 512


def _cum_call(fft, bf_col):
    s_len = fft.shape[1]
    ch = CUM_CHUNK

    def body(f_ref, b_ref, cum_ref, sg_ref):
        r = lax.broadcasted_iota(jnp.int32, (ch, ch), 0)
        c = lax.broadcasted_iota(jnp.int32, (ch, ch), 1)
        upper = (r <= c).astype(f32)
        carry = jnp.zeros((FOX_HEADS, 1), f32)
        for n in range(s_len // ch):
            z = f_ref[:, n * ch:(n + 1) * ch] + b_ref[...]
            logf = jnp.minimum(z, 0.0) - jnp.log1p(jnp.exp(-jnp.abs(z)))
            sg_ref[:, n * ch:(n + 1) * ch] = 1.0 / (1.0 + jnp.exp(z))
            cs = jnp.dot(logf, upper, precision=HIGHEST, preferred_element_type=f32) + carry
            cum_ref[:, n * ch:(n + 1) * ch] = cs
            carry = cs[:, ch - 1:ch]

    return pl.pallas_call(
        body,
        name="fox_cum_fwd",
        out_shape=[_sds((FOX_HEADS, s_len), f32)] * 2,
        compiler_params=_params(),
    )(fft, bf_col)


def _cum_bwd_call(dcq, dck, sg):
    s_len = sg.shape[1]
    ch = CUM_CHUNK
    nch = s_len // ch

    def body(q_ref, k_ref, sg_ref, dff_ref, dbf_ref):
        r = lax.broadcasted_iota(jnp.int32, (ch, ch), 0)
        c = lax.broadcasted_iota(jnp.int32, (ch, ch), 1)
        lower = (r >= c).astype(f32)
        dff_ref[...] = jnp.zeros_like(dff_ref)
        carry = jnp.zeros((FOX_HEADS, 1), f32)
        total = jnp.zeros((FOX_HEADS, 1), f32)
        for n in reversed(range(nch)):
            sl = slice(n * ch, (n + 1) * ch)
            dcum = q_ref[:, sl] - k_ref[:, sl]
            rs = jnp.dot(dcum, lower, precision=HIGHEST, preferred_element_type=f32) + carry
            carry = rs[:, 0:1]
            dff = rs * sg_ref[:, sl]
            dff_ref[0:FOX_HEADS, sl] = dff
            total = total + jnp.sum(dff, axis=1, keepdims=True)
        dbf_ref[...] = jnp.broadcast_to(total, (FOX_HEADS, 128))

    return pl.pallas_call(
        body,
        name="fox_cum_bwd",
        out_shape=[_sds((128, s_len), f32), _sds((FOX_HEADS, 128), f32)],
        compiler_params=_params(),
    )(dcq, dck, sg)


FOX_T = 512


def _fox_fwd_call(q, k, v, cum_col, cum_row):
    nh, s_len, _ = q.shape
    t = FOX_T

    def body(q_ref, k_ref, v_ref, cq_ref, ck_ref, o_ref, lse_ref):
        i = pl.program_id(1)
        qb = q_ref[0]
        cq = cq_ref[0]

        def step(j, carry, masked):
            m, l, acc = carry
            off = pl.multiple_of(j * t, t)
            kb = k_ref[0, pl.ds(off, t), :]
            vb = v_ref[0, pl.ds(off, t), :]
            s = lax.dot_general(qb, kb, NT, preferred_element_type=f32)
            s = s + (cq - ck_ref[0, :, pl.ds(off, t)])
            if masked:
                row = lax.broadcasted_iota(jnp.int32, (t, t), 0)
                col = lax.broadcasted_iota(jnp.int32, (t, t), 1)
                s = jnp.where(col <= row, s, NEG_INF)
            m_new = jnp.maximum(m, jnp.max(s, axis=1, keepdims=True))
            a = jnp.exp(m - m_new)
            p = jnp.exp(s - m_new)
            l = a * l + jnp.sum(p, axis=1, keepdims=True)
            acc = a * acc + jnp.dot(p.astype(bf16), vb, preferred_element_type=f32)
            return m_new, l, acc

        init = (jnp.full((t, 1), NEG_INF, f32), jnp.zeros((t, 1), f32), jnp.zeros((t, HEAD_DIM), f32))
        carry = lax.fori_loop(0, i, lambda j, cr: step(j, cr, False), init)
        m, l, acc = step(i, carry, True)
        o_ref[0] = acc / l
        lse_ref[0] = m + jnp.log(l)

    full = pl.BlockSpec((1, s_len, HEAD_DIM), lambda h, i: (h, 0, 0))
    return pl.pallas_call(
        body,
        name="fox_fwd",
        grid=(nh, s_len // t),
        in_specs=[pl.BlockSpec((1, t, HEAD_DIM), lambda h, i: (h, i, 0)), full, full,
                  pl.BlockSpec((1, t, 1), lambda h, i: (h, i, 0)),
                  pl.BlockSpec((1, 1, s_len), lambda h, i: (h, 0, 0))],
        out_specs=[pl.BlockSpec((1, t, HEAD_DIM), lambda h, i: (h, i, 0)),
                   pl.BlockSpec((1, t, 1), lambda h, i: (h, i, 0))],
        out_shape=[_sds((nh, s_len, HEAD_DIM), f32), _sds((nh, s_len, 1), f32)],
        compiler_params=_params(("arbitrary", "arbitrary")),
    )(q, k, v, cum_col, cum_row)


SWA_TS = 512


def _swa_bias_call(rel_bias, bucket):
    def body(rb_ref, bk_ref, b_ref, b0_ref):
        bk = bk_ref[...]
        col = lax.broadcasted_iota(jnp.int32, (BLOCK, 2 * BLOCK), 1)
        for h in range(SWA_HEADS):
            acc = jnp.full((BLOCK, 2 * BLOCK), NEG_INF, f32)
            for b in range(NUM_BUCKETS):
                acc = jnp.where(bk == b, rb_ref[b, h], acc)
            g, hh = divmod(h, SWA_GROUP)
            b_ref[g, hh * BLOCK:(hh + 1) * BLOCK, :] = acc
            b0_ref[g, hh * BLOCK:(hh + 1) * BLOCK, :] = jnp.where(col < BLOCK, NEG_INF, acc)

    return pl.pallas_call(
        body,
        name="swa_bias",
        in_specs=[pl.BlockSpec(memory_space=pltpu.SMEM), pl.BlockSpec(memory_space=pltpu.VMEM)],
        out_shape=[_sds((SWA_KV_HEADS, SWA_GROUP * BLOCK, 2 * BLOCK), f32)] * 2,
        compiler_params=_params(),
    )(rel_bias, bucket)


def _swa_bias_bwd_call(dbias, bucket):
    def body(d_ref, bk_ref, o_ref):
        bk = bk_ref[...]
        row = lax.broadcasted_iota(jnp.int32, (NUM_BUCKETS, 128), 0)
        col = lax.broadcasted_iota(jnp.int32, (NUM_BUCKETS, 128), 1)
        out = jnp.zeros((NUM_BUCKETS, 128), f32)
        for h in range(SWA_HEADS):
            g, hh = divmod(h, SWA_GROUP)
            d = d_ref[g, hh * BLOCK:(hh + 1) * BLOCK, :]
            for b in range(NUM_BUCKETS):
                val = jnp.sum(jnp.sum(jnp.where(bk == b, d, 0.0), axis=1, keepdims=True), axis=0, keepdims=True)
                out = jnp.where((row == b) & (col == h), val, out)
        o_ref[...] = out

    return pl.pallas_call(
        body,
        name="swa_bias_bwd",
        out_shape=_sds((NUM_BUCKETS, 128), f32),
        compiler_params=_params(),
    )(dbias, bucket)


def _swa_specs(ts):
    nb = ts // BLOCK
    qspec = pl.BlockSpec((SWA_HEADS, ts, HEAD_DIM), lambda n: (0, n, 0))
    cur = pl.BlockSpec((SWA_KV_HEADS, ts, HEAD_DIM), lambda n: (0, n, 0))
    prev = pl.BlockSpec((SWA_KV_HEADS, BLOCK, HEAD_DIM), lambda n: (0, jnp.maximum(n * nb - 1, 0), 0))
    return qspec, cur, prev


def _sink_col(sink_ref, g):
    return jnp.concatenate([jnp.full((BLOCK, 1), sink_ref[g * SWA_GROUP + hh], f32) for hh in range(SWA_GROUP)], axis=0)


def _swa_fwd_call(q, k, v, bias, bias0, sink):
    s_len = q.shape[1]
    ts = SWA_TS
    nb = ts // BLOCK

    def body(q_ref, kc_ref, kp_ref, vc_ref, vp_ref, b_ref, b0_ref, sink_ref, o_ref, lse_ref):
        first = pl.program_id(0) == 0
        for g in range(SWA_KV_HEADS):
            kall = jnp.concatenate([kp_ref[g], kc_ref[g]], axis=0)
            vall = jnp.concatenate([vp_ref[g], vc_ref[g]], axis=0)
            sink_c = _sink_col(sink_ref, g)
            for b in range(nb):
                rows = slice(b * BLOCK, (b + 1) * BLOCK)
                qg = jnp.concatenate([q_ref[g * SWA_GROUP + hh, rows, :] for hh in range(SWA_GROUP)], axis=0)
                kcat = kall[b * BLOCK:(b + 2) * BLOCK]
                vcat = vall[b * BLOCK:(b + 2) * BLOCK]
                bias_b = b_ref[g]
                if b == 0:
                    bias_b = jnp.where(first, b0_ref[g], bias_b)
                s = lax.dot_general(qg, kcat, NT, preferred_element_type=f32) + bias_b
                m = jnp.maximum(jnp.max(s, axis=1, keepdims=True), sink_c)
                p = jnp.exp(s - m)
                l = jnp.sum(p, axis=1, keepdims=True) + jnp.exp(sink_c - m)
                o = jnp.dot(p.astype(bf16), vcat, preferred_element_type=f32) / l
                lse = m + jnp.log(l)
                for hh in range(SWA_GROUP):
                    o_ref[g * SWA_GROUP + hh, rows, :] = o[hh * BLOCK:(hh + 1) * BLOCK]
                    lse_ref[g * SWA_GROUP + hh, rows, :] = lse[hh * BLOCK:(hh + 1) * BLOCK]

    qspec, cur, prev = _swa_specs(ts)
    bspec = pl.BlockSpec((SWA_KV_HEADS, SWA_GROUP * BLOCK, 2 * BLOCK), lambda n: (0, 0, 0))
    return pl.pallas_call(
        body,
        name="swa_fwd",
        grid=(s_len // ts,),
        in_specs=[qspec, cur, prev, cur, prev, bspec, bspec, pl.BlockSpec(memory_space=pltpu.SMEM)],
        out_specs=[pl.BlockSpec((SWA_HEADS, ts, HEAD_DIM), lambda n: (0, n, 0)),
                   pl.BlockSpec((SWA_HEADS, ts, 1), lambda n: (0, n, 0))],
        out_shape=[_sds((SWA_HEADS, s_len, HEAD_DIM), f32), _sds((SWA_HEADS, s_len, 1), f32)],
        compiler_params=_params(("arbitrary",)),
    )(q, k, k, v, v, bias, bias0, sink)


def _head_selector():
    sel = np.zeros((512, 128), np.float32)
    for h in range(8):
        sel[h * HEAD_DIM:(h + 1) * HEAD_DIM, h] = 1.0
    return sel


def _post_call(of, fz, osw, sz, x2, tgt, wo, ln_g, ln_b, sel, tm):
    s_len = x2.shape[0]

    def body(of_ref, fz_ref, os_ref, sz_ref, x_ref, t_ref, wo_ref, g_ref, b_ref, sel_ref,
             dh_ref, dof_ref, dfz_ref, dos_ref, dsz_ref, dlf_ref, dls_ref, dwo_ref, dg_ref, db_ref, loss_ref):
        n = pl.program_id(0)

        @pl.when(n == 0)
        def _():
            dwo_ref[...] = jnp.zeros_like(dwo_ref)
            dg_ref[...] = jnp.zeros_like(dg_ref)
            db_ref[...] = jnp.zeros_like(db_ref)
            loss_ref[...] = jnp.zeros_like(loss_ref)

        o_f = jnp.concatenate([of_ref[h] for h in range(FOX_HEADS)], axis=1)
        o_s = jnp.concatenate([os_ref[h] for h in range(SWA_HEADS)], axis=1)
        fz = fz_ref[...]
        sz = sz_ref[...]
        sg_f = jax.nn.sigmoid(fz)
        sg_s = jax.nn.sigmoid(sz)
        silu_f = fz * sg_f
        silu_s = sz * sg_s
        mixed = jnp.concatenate([o_f * silu_f, o_s * silu_s], axis=1).astype(bf16)
        y = jnp.dot(mixed, wo_ref[...], preferred_element_type=f32)
        h = ALPHA * x_ref[...] + y
        mu = jnp.mean(h, axis=1, keepdims=True)
        hc = h - mu
        var = jnp.mean(hc * hc, axis=1, keepdims=True)
        rstd = lax.rsqrt(var + LN_EPS)
        xhat = hc * rstd
        gam = g_ref[...]
        out = xhat * gam + b_ref[...]
        err = out - t_ref[...]
        tok_loss = jnp.mean(err * err, axis=1, keepdims=True)
        loss_ref[...] += 0.5 * jnp.sum(tok_loss, axis=0, keepdims=True)
        dout = err * (1.0 / D_MODEL)
        dg_ref[...] += jnp.sum(dout * xhat, axis=0, keepdims=True)
        db_ref[...] += jnp.sum(dout, axis=0, keepdims=True)
        dxh = dout * gam
        m1 = jnp.mean(dxh, axis=1, keepdims=True)
        m2 = jnp.mean(dxh * xhat, axis=1, keepdims=True)
        dh = rstd * (dxh - m1 - xhat * m2)
        dh_ref[...] = dh
        dyb = dh.astype(bf16)
        dwo_ref[...] += lax.dot_general(mixed, dyb, TN, preferred_element_type=f32)
        dmix = lax.dot_general(dyb, wo_ref[...], NT, preferred_element_type=f32)
        dm_f = dmix[:, :512]
        dm_s = dmix[:, 512:]
        do_f = dm_f * silu_f
        do_s = dm_s * silu_s
        dfz_ref[...] = (dm_f * o_f * (sg_f * (1.0 + fz * (1.0 - sg_f)))).astype(bf16)
        dsz_ref[...] = (dm_s * o_s * (sg_s * (1.0 + sz * (1.0 - sg_s)))).astype(bf16)
        for hd in range(FOX_HEADS):
            dof_ref[hd] = do_f[:, hd * HEAD_DIM:(hd + 1) * HEAD_DIM].astype(bf16)
            dos_ref[hd] = do_s[:, hd * HEAD_DIM:(hd + 1) * HEAD_DIM].astype(bf16)
        sel_m = sel_ref[...]
        dl_f = jnp.dot(do_f * o_f, sel_m, precision=HIGHEST, preferred_element_type=f32)
        dl_s = jnp.dot(do_s * o_s, sel_m, precision=HIGHEST, preferred_element_type=f32)
        dlf_ref[...] = dl_f.T[:FOX_HEADS, :]
        dls_ref[...] = dl_s

    heads_f32 = pl.BlockSpec((8, tm, HEAD_DIM), lambda n: (0, n, 0))
    half = pl.BlockSpec((tm, 512), lambda n: (n, 0))
    fullw = pl.BlockSpec((tm, D_MODEL), lambda n: (n, 0))
    vec = pl.BlockSpec((1, D_MODEL), lambda n: (0, 0))
    return pl.pallas_call(
        body,
        name="post_fwd_bwd",
        grid=(s_len // tm,),
        in_specs=[heads_f32, half, heads_f32, half, fullw, fullw,
                  pl.BlockSpec((D_MODEL, D_MODEL), lambda n: (0, 0)), vec, vec,
                  pl.BlockSpec((512, 128), lambda n: (0, 0))],
        out_specs=[fullw, heads_f32, half, heads_f32, half,
                   pl.BlockSpec((FOX_HEADS, tm), lambda n: (0, n)), pl.BlockSpec((tm, 128), lambda n: (n, 0)),
                   pl.BlockSpec((D_MODEL, D_MODEL), lambda n: (0, 0)), vec, vec,
                   pl.BlockSpec((1, 1), lambda n: (0, 0))],
        out_shape=[_sds((s_len, D_MODEL), f32), _sds((8, s_len, HEAD_DIM), bf16), _sds((s_len, 512), bf16),
                   _sds((8, s_len, HEAD_DIM), bf16), _sds((s_len, 512), bf16),
                   _sds((FOX_HEADS, s_len), f32), _sds((s_len, 128), f32),
                   _sds((D_MODEL, D_MODEL), f32), _sds((1, D_MODEL), f32), _sds((1, D_MODEL), f32),
                   _sds((1, 1), f32)],
        compiler_params=_params(("arbitrary",)),
    )(of, fz, osw, sz, x2, tgt, wo, ln_g, ln_b, sel)


def _fox_bwd_call(k, v, q, do, cq_row, lse_row, dl_row, ck_col):
    nh, s_len, _ = q.shape
    t = FOX_T
    nt = s_len // t

    def body(k_ref, v_ref, q_ref, do_ref, cq_ref, lse_ref, dl_ref, ck_ref,
             dk_ref, dv_ref, dq_ref, dcq_ref, dck_ref, dqt_s, dcq_s):
        j = pl.program_id(1)

        @pl.when(j == 0)
        def _():
            dqt_s[...] = jnp.zeros_like(dqt_s)
            dcq_s[...] = jnp.zeros_like(dcq_s)

        kb = k_ref[0]
        vb = v_ref[0]
        kt = kb.astype(f32).T.astype(bf16)
        nck = -ck_ref[0]

        def step(i, carry, masked):
            dk, dv, dck = carry
            off = pl.multiple_of(i * t, t)
            qb = q_ref[0, pl.ds(off, t), :]
            dob = do_ref[0, pl.ds(off, t), :]
            st = lax.dot_general(kb, qb, NT, preferred_element_type=f32)
            st = st + (nck + cq_ref[0, :, pl.ds(off, t)])
            if masked:
                row = lax.broadcasted_iota(jnp.int32, (t, t), 0)
                col = lax.broadcasted_iota(jnp.int32, (t, t), 1)
                st = jnp.where(row <= col, st, NEG_INF)
            pt = jnp.exp(st - lse_ref[0, :, pl.ds(off, t)])
            dpt = lax.dot_general(vb, dob, NT, preferred_element_type=f32)
            dst = pt * (dpt - dl_ref[0, :, pl.ds(off, t)])
            dsb = dst.astype(bf16)
            dv = dv + jnp.dot(pt.astype(bf16), dob, preferred_element_type=f32)
            dk = dk + jnp.dot(dsb, qb, preferred_element_type=f32)
            dqt_s[:, pl.ds(off, t)] += jnp.dot(kt, dsb, preferred_element_type=f32)
            dcq_s[:, pl.ds(off, t)] += jnp.sum(dst, axis=0, keepdims=True)
            dck = dck + jnp.sum(dst, axis=1, keepdims=True)
            return dk, dv, dck

        zero = (jnp.zeros((t, HEAD_DIM), f32), jnp.zeros((t, HEAD_DIM), f32), jnp.zeros((t, 1), f32))
        carry = step(j, zero, True)
        dk, dv, dck = lax.fori_loop(j + 1, nt, lambda i, cr: step(i, cr, False), carry)
        dk_ref[0] = dk.astype(bf16)
        dv_ref[0] = dv.astype(bf16)
        dck_ref[0] = dck

        @pl.when(j == nt - 1)
        def _():
            for cidx in range(nt):
                sl = slice(cidx * t, (cidx + 1) * t)
                dq_ref[0, sl, :] = (dqt_s[:, sl].T * SCALE).astype(bf16)
            dcq_ref[0] = dcq_s[...]

    tile = pl.BlockSpec((1, t, HEAD_DIM), lambda h, j: (h, j, 0))
    full = pl.BlockSpec((1, s_len, HEAD_DIM), lambda h, j: (h, 0, 0))
    rowv = pl.BlockSpec((1, 1, s_len), lambda h, j: (h, 0, 0))
    colt = pl.BlockSpec((1, t, 1), lambda h, j: (h, j, 0))
    return pl.pallas_call(
        body,
        name="fox_bwd",
        grid=(nh, nt),
        in_specs=[tile, tile, full, full, rowv, rowv, rowv, colt],
        out_specs=[tile, tile, full, rowv, colt],
        out_shape=[_sds((nh, s_len, HEAD_DIM), bf16)] * 3 + [_sds((nh, 1, s_len), f32), _sds((nh, s_len, 1), f32)],
        scratch_shapes=[pltpu.VMEM((HEAD_DIM, s_len), f32), pltpu.VMEM((1, s_len), f32)],
        compiler_params=_params(("arbitrary", "arbitrary")),
    )(k, v, q, do, cq_row, lse_row, dl_row, ck_col)


def _swa_bwd_call(q, k, v, do, lse, dl, bias, bias0, sink):
    s_len = q.shape[1]
    ts = SWA_TS
    nb = ts // BLOCK
    nsteps = s_len // ts

    def body(q_ref, kc_ref, kp_ref, vc_ref, vp_ref, do_ref, lse_ref, dl_ref, b_ref, b0_ref, sink_ref,
             dq_ref, dk_ref, dv_ref, dbias_ref, dsink_ref, dk_s, dv_s, tail_k, tail_v, sk_s):
        n = pl.program_id(0)

        @pl.when(n == 0)
        def _():
            dbias_ref[...] = jnp.zeros_like(dbias_ref)
            sk_s[...] = jnp.zeros_like(sk_s)

        @pl.when(n < nsteps)
        def _():
            first = n == 0
            dk_s[...] = jnp.zeros_like(dk_s)
            dv_s[...] = jnp.zeros_like(dv_s)
            for g in range(SWA_KV_HEADS):
                kall = jnp.concatenate([kp_ref[g], kc_ref[g]], axis=0)
                vall = jnp.concatenate([vp_ref[g], vc_ref[g]], axis=0)
                sink_c = _sink_col(sink_ref, g)
                for b in range(nb):
                    rows = slice(b * BLOCK, (b + 1) * BLOCK)
                    heads = [g * SWA_GROUP + hh for hh in range(SWA_GROUP)]
                    qg = jnp.concatenate([q_ref[h, rows, :] for h in heads], axis=0)
                    dog = jnp.concatenate([do_ref[h, rows, :] for h in heads], axis=0)
                    lse_c = jnp.concatenate([lse_ref[h, rows, :] for h in heads], axis=0)
                    dl_c = jnp.concatenate([dl_ref[rows, h:h + 1] for h in heads], axis=0)
                    kcat = kall[b * BLOCK:(b + 2) * BLOCK]
                    vcat = vall[b * BLOCK:(b + 2) * BLOCK]
                    bias_b = b_ref[g]
                    if b == 0:
                        bias_b = jnp.where(first, b0_ref[g], bias_b)
                    s = lax.dot_general(qg, kcat, NT, preferred_element_type=f32) + bias_b
                    p = jnp.exp(s - lse_c)
                    dp = lax.dot_general(dog, vcat, NT, preferred_element_type=f32)
                    ds = p * (dp - dl_c)
                    dsb = ds.astype(bf16)
                    dqg = jnp.dot(dsb, kcat, preferred_element_type=f32) * SCALE
                    for hh, h in enumerate(heads):
                        dq_ref[h, rows, :] = dqg[hh * BLOCK:(hh + 1) * BLOCK].astype(bf16)
                    win = slice(b * BLOCK, (b + 2) * BLOCK)
                    dk_s[g, win, :] += lax.dot_general(dsb, qg, TN, preferred_element_type=f32)
                    dv_s[g, win, :] += lax.dot_general(p.astype(bf16), dog, TN, preferred_element_type=f32)
                    dbias_ref[g] += ds
                    sk_s[g] += -jnp.exp(sink_c - lse_c) * dl_c

        @pl.when(n > 0)
        def _():
            last = slice(ts - BLOCK, ts)
            for g in range(SWA_KV_HEADS):
                add_k = jnp.where(n < nsteps, dk_s[g, 0:BLOCK, :], 0.0)
                add_v = jnp.where(n < nsteps, dv_s[g, 0:BLOCK, :], 0.0)
                dk_ref[g, 0:ts - BLOCK, :] = tail_k[g, 0:ts - BLOCK, :].astype(bf16)
                dv_ref[g, 0:ts - BLOCK, :] = tail_v[g, 0:ts - BLOCK, :].astype(bf16)
                dk_ref[g, last, :] = (tail_k[g, last, :] + add_k).astype(bf16)
                dv_ref[g, last, :] = (tail_v[g, last, :] + add_v).astype(bf16)

        @pl.when(n < nsteps)
        def _():
            tail_k[...] = dk_s[:, BLOCK:, :]
            tail_v[...] = dv_s[:, BLOCK:, :]

        @pl.when(n == nsteps)
        def _():
            row = lax.broadcasted_iota(jnp.int32, (SWA_HEADS, 128), 0)
            out = jnp.zeros((SWA_HEADS, 128), f32)
            for h in range(SWA_HEADS):
                g, hh = divmod(h, SWA_GROUP)
                val = jnp.sum(sk_s[g, hh * BLOCK:(hh + 1) * BLOCK, :], axis=0, keepdims=True)
                out = jnp.where(row == h, val, out)
            dsink_ref[...] = out

    last_step = nsteps - 1

    def cl(n):
        return jnp.minimum(n, last_step)

    qspec = pl.BlockSpec((SWA_HEADS, ts, HEAD_DIM), lambda n: (0, cl(n), 0))
    cur = pl.BlockSpec((SWA_KV_HEADS, ts, HEAD_DIM), lambda n: (0, cl(n), 0))
    prev = pl.BlockSpec((SWA_KV_HEADS, BLOCK, HEAD_DIM), lambda n: (0, jnp.maximum(cl(n) * nb - 1, 0), 0))
    lsespec = pl.BlockSpec((SWA_HEADS, ts, 1), lambda n: (0, cl(n), 0))
    dlspec = pl.BlockSpec((ts, 128), lambda n: (cl(n), 0))
    bspec = pl.BlockSpec((SWA_KV_HEADS, SWA_GROUP * BLOCK, 2 * BLOCK), lambda n: (0, 0, 0))
    kvout = pl.BlockSpec((SWA_KV_HEADS, ts, HEAD_DIM), lambda n: (0, jnp.maximum(n - 1, 0), 0))
    return pl.pallas_call(
        body,
        name="swa_bwd",
        grid=(nsteps + 1,),
        in_specs=[qspec, cur, prev, cur, prev, qspec, lsespec, dlspec, bspec, bspec,
                  pl.BlockSpec(memory_space=pltpu.SMEM)],
        out_specs=[qspec, kvout, kvout, bspec, pl.BlockSpec((SWA_HEADS, 128), lambda n: (0, 0))],
        out_shape=[_sds((SWA_HEADS, s_len, HEAD_DIM), bf16), _sds((SWA_KV_HEADS, s_len, HEAD_DIM), bf16),
                   _sds((SWA_KV_HEADS, s_len, HEAD_DIM), bf16),
                   _sds((SWA_KV_HEADS, SWA_GROUP * BLOCK, 2 * BLOCK), f32), _sds((SWA_HEADS, 128), f32)],
        scratch_shapes=[pltpu.VMEM((SWA_KV_HEADS, ts + BLOCK, HEAD_DIM), f32),
                        pltpu.VMEM((SWA_KV_HEADS, ts + BLOCK, HEAD_DIM), f32),
                        pltpu.VMEM((SWA_KV_HEADS, ts, HEAD_DIM), f32),
                        pltpu.VMEM((SWA_KV_HEADS, ts, HEAD_DIM), f32),
                        pltpu.VMEM((SWA_KV_HEADS, SWA_GROUP * BLOCK, 1), f32)],
        compiler_params=_params(("arbitrary",)),
    )(q, k, k, v, v, do, lse, dl, bias, bias0, sink)


def _dx_call(dh, dqf, dkf, dvf, dfz, dqs, dks, dvs, dsz, dfft, w_al, tm):
    s_len = dh.shape[0]

    def body(dh_ref, dqf_ref, dkf_ref, dvf_ref, dfz_ref, dqs_ref, dks_ref, dvs_ref, dsz_ref, dfft_ref, w_ref,
             dx_ref, dp_ref):
        def cat(ref, nheads):
            return jnp.concatenate([ref[h] for h in range(nheads)], axis=1)

        dp = jnp.concatenate([cat(dqf_ref, 8), cat(dkf_ref, 8), cat(dvf_ref, 8), dfz_ref[...], cat(dqs_ref, 8),
                              cat(dks_ref, 2), cat(dvs_ref, 2), dsz_ref[...], dfft_ref[...].T.astype(bf16)], axis=1)
        dp_ref[...] = dp
        dx_ref[...] = ALPHA * dh_ref[...] + lax.dot_general(dp, w_ref[...], NT, preferred_element_type=f32)

    def heads(nh):
        return pl.BlockSpec((nh, tm, HEAD_DIM), lambda i: (0, i, 0))

    half = pl.BlockSpec((tm, 512), lambda i: (i, 0))
    fullw = pl.BlockSpec((tm, D_MODEL), lambda i: (i, 0))
    return pl.pallas_call(
        body,
        name="dx_bwd",
        grid=(s_len // tm,),
        in_specs=[fullw, heads(8), heads(8), heads(8), half, heads(8), heads(2), heads(2), half,
                  pl.BlockSpec((128, tm), lambda i: (0, i)), pl.BlockSpec((D_MODEL, A_W), lambda i: (0, 0))],
        out_specs=[fullw, pl.BlockSpec((tm, A_W), lambda i: (i, 0))],
        out_shape=[_sds((s_len, D_MODEL), f32), _sds((s_len, A_W), bf16)],
        compiler_params=_params(("arbitrary",)),
    )(dh, dqf, dkf, dvf, dfz, dqs, dks, dvs, dsz, dfft, w_al)


DW_COLS = 1152


def _dw_call(x2, dproj, tm):
    s_len = x2.shape[0]
    nt = s_len // tm

    def body(x_ref, dp_ref, dw_ref):
        @pl.when(pl.program_id(1) == 0)
        def _():
            dw_ref[...] = jnp.zeros_like(dw_ref)

        dw_ref[...] += lax.dot_general(x_ref[...].astype(bf16), dp_ref[...], TN, preferred_element_type=f32)

    return pl.pallas_call(
        body,
        name="dw_in_bwd",
        grid=(A_W // DW_COLS, nt),
        in_specs=[pl.BlockSpec((tm, D_MODEL), lambda c, i: (i, 0)), pl.BlockSpec((tm, DW_COLS), lambda c, i: (i, c))],
        out_specs=pl.BlockSpec((D_MODEL, DW_COLS), lambda c, i: (0, c)),
        out_shape=_sds((D_MODEL, A_W), f32),
        compiler_params=_params(("arbitrary", "arbitrary")),
    )(x2, dproj)


def _adam_call(recv, w, m, v, tr, name):
    rows, cols = w.shape

    def body(r_ref, w_ref, m_ref, v_ref, g_ref, d_ref, mo_ref, vo_ref):
        g = r_ref[0]
        for p in range(1, N_DEV):
            g = g + r_ref[p]
        mn = ADAM_B1 * m_ref[...] + (1.0 - ADAM_B1) * g
        vn = ADAM_B2 * v_ref[...] + (1.0 - ADAM_B2) * (g * g)
        m_hat = mn / (1.0 - ADAM_B1 ** ADAM_STEP)
        v_hat = vn / (1.0 - ADAM_B2 ** ADAM_STEP)
        g_ref[...] = g
        d_ref[...] = -ADAM_LR * (m_hat / (jnp.sqrt(v_hat) + ADAM_EPS) + ADAM_WD * w_ref[...])
        mo_ref[...] = mn
        vo_ref[...] = vn

    blk = pl.BlockSpec((tr, cols), lambda i: (i, 0))
    return pl.pallas_call(
        body,
        name=name,
        grid=(rows // tr,),
        in_specs=[pl.BlockSpec((N_DEV, tr, cols), lambda i: (0, i, 0)), blk, blk, blk],
        out_specs=[blk] * 4,
        out_shape=[_sds((rows, cols), f32)] * 4,
        compiler_params=_params(("arbitrary",)),
    )(recv, w, m, v)


def _pad_cols(a, width=128):
    return jnp.pad(a, ((0, 0), (0, width - a.shape[1])))


def _pack_small(ln_g, ln_b, rel, b_f, sink):
    return jnp.concatenate([
        ln_g.reshape(8, 128), ln_b.reshape(8, 128), _pad_cols(rel),
        jnp.pad(_pad_cols(b_f), ((0, 7), (0, 0))), jnp.pad(_pad_cols(sink), ((0, 7), (0, 0)))], axis=0)


def _unpack_small(p):
    return (p[0:8].reshape(1, D_MODEL), p[8:16].reshape(1, D_MODEL), p[16:48, 0:8], p[48:49, 0:8], p[56:57, 0:8])


def kernel(x, w_in, b_f, rel_bias, sink, w_o, ln_g, ln_b, loss_target, m_w_in, m_b_f, m_rel_bias, m_sink, m_w_o, m_ln_g, m_ln_b, v_w_in, v_b_f, v_rel_bias, v_sink, v_w_o, v_ln_g, v_ln_b):
    x2 = x[0]
    tgt = loss_target[0]
    s_len = x2.shape[0]
    shard = w_in.shape[2]

    g_in, g_o = _gather_call([w_in[0].astype(bf16), w_o[0].astype(bf16)])
    w_full = jnp.transpose(g_in, (1, 0, 2)).reshape(D_MODEL, N_DEV * shard)
    w_al = jnp.concatenate([w_full[:, :O_FF0], w_full[:, O_FF1:], w_full[:, O_FF0:O_FF1],
                            jnp.zeros((D_MODEL, A_W - D_IN), bf16)], axis=1)
    wo_full = g_o.reshape(D_MODEL, D_MODEL)

    qf, kf, vf, fz, qs, ks, vs, sz, fft = _proj_call(x2, w_al, 512)
    cum, sgm = _cum_call(fft, b_f.reshape(FOX_HEADS, 1))
    cum_col = cum.reshape(FOX_HEADS, s_len, 1)
    cum_row = cum.reshape(FOX_HEADS, 1, s_len)
    o_f, lse_f = _fox_fwd_call(qf, kf, vf, cum_col, cum_row)
    bucket = jnp.asarray(_t5_bucket_table())
    bias, bias0 = _swa_bias_call(rel_bias, bucket)
    sink_v = sink.reshape(SWA_HEADS)
    o_s, lse_s = _swa_fwd_call(qs, ks, vs, bias, bias0, sink_v)

    (dh, do_f, dfz, do_s, dsz, dl_f, dl_s, dwo, dg, db, loss_part) = _post_call(
        o_f, fz, o_s, sz, x2, tgt, wo_full, ln_g, ln_b, jnp.asarray(_head_selector()), 256)

    dkf, dvf, dqf, dcq, dck = _fox_bwd_call(
        kf, vf, qf, do_f, cum_row, lse_f.reshape(FOX_HEADS, 1, s_len), dl_f.reshape(FOX_HEADS, 1, s_len), cum_col)
    dfft, dbf = _cum_bwd_call(dcq.reshape(FOX_HEADS, s_len), dck.reshape(FOX_HEADS, s_len), sgm)
    dqs, dks, dvs, dbias, dsink = _swa_bwd_call(qs, ks, vs, do_s, lse_s, dl_s, bias, bias0, sink_v)
    drel = _swa_bias_bwd_call(dbias, bucket)

    dx, dproj = _dx_call(dh, dqf, dkf, dvf, dfz, dqs, dks, dvs, dsz, dfft, w_al, 256)
    dw_al = _dw_call(x2, dproj, 1024)

    dw_full = jnp.concatenate([dw_al[:, :O_FF0], dw_al[:, A_FF:A_FF + (O_FF1 - O_FF0)], dw_al[:, O_FF0:A_FF]], axis=1)
    dw_blocks = jnp.transpose(dw_full.reshape(D_MODEL, N_DEV, shard), (1, 0, 2))
    dwo_blocks = dwo.reshape(N_DEV, D_MODEL // N_DEV, D_MODEL)
    small = _pack_small(dg, db, drel[:, 0:8], dbf[:, 0].reshape(1, 8), dsink[:, 0].reshape(1, 8))
    loss_slot = np.zeros((64, 128), bool)
    loss_slot[49, 0] = True
    small = jnp.where(jnp.asarray(loss_slot), loss_part[0, 0], small)
    small_blocks = jnp.broadcast_to(small[None], (N_DEV,) + small.shape)
    r_in, r_o, r_small = _exchange_call([dw_blocks, dwo_blocks, small_blocks])

    g_win, d_win, nm_win, nv_win = _adam_call(r_in, w_in[0], m_w_in[0], v_w_in[0], 128, "adam_w_in")
    g_wo, d_wo, nm_wo, nv_wo = _adam_call(r_o, w_o[0], m_w_o[0], v_w_o[0], 64, "adam_w_o")
    p_w = _pack_small(ln_g, ln_b, rel_bias, b_f, sink)
    p_m = _pack_small(m_ln_g, m_ln_b, m_rel_bias, m_b_f, m_sink)
    p_v = _pack_small(v_ln_g, v_ln_b, v_rel_bias, v_b_f, v_sink)
    g_p, d_p, nm_p, nv_p = _adam_call(r_small, p_w, p_m, p_v, 64, "adam_small")

    loss = g_p[49, 0]
    g_lng, g_lnb, g_rel, g_bf, g_sink = _unpack_small(g_p)
    d_lng, d_lnb, d_rel, d_bf, d_sink = _unpack_small(d_p)
    m_lng, m_lnb, m_rel, m_bf, m_sk = _unpack_small(nm_p)
    v_lng, v_lnb, v_rel, v_bf, v_sk = _unpack_small(nv_p)
    return (loss, dx[None], g_win[None], g_bf, g_rel, g_sink, g_wo[None], g_lng, g_lnb,
            d_win[None], d_bf, d_rel, d_sink, d_wo[None], d_lng, d_lnb,
            nm_win[None], m_bf, m_rel, m_sk, nm_wo[None], m_lng, m_lnb,
            nv_win[None], v_bf, v_rel, v_sk, nv_wo[None], v_lng, v_lnb)
```

```python
import functools
import math

import numpy as np
import jax
import jax.numpy as jnp
from jax import lax
from jax.experimental import pallas as pl
from jax.experimental.pallas import tpu as pltpu

f32 = jnp.float32
bf16 = jnp.bfloat16

D_MODEL = 1024
HEAD_DIM = 64
FOX_HEADS = 8
SWA_HEADS = 8
SWA_KV_HEADS = 2
SWA_GROUP = 4
BLOCK = 128
NUM_BUCKETS = 32
MAX_DISTANCE = 128
LN_EPS = 1e-5
NEG_INF = -1e30
ALPHA = 2.0 ** 0.25
SCALE = 1.0 / math.sqrt(HEAD_DIM)
D_IN = 3336

ADAM_LR = 0.001
ADAM_B1 = 0.9
ADAM_B2 = 0.999
ADAM_EPS = 1e-08
ADAM_WD = 0.01
ADAM_STEP = 10

N_DEV = 8
A_FQ, A_FK, A_FV, A_FZ, A_SQ, A_SK, A_SV, A_SZ, A_FF, A_W = 0, 512, 1024, 1536, 2048, 2560, 2688, 2816, 3328, 3456
O_FF0, O_FF1 = 1536, 1544

VMEM_LIMIT = 48 * 1024 * 1024
HIGHEST = lax.Precision.HIGHEST
NT = (((1,), (1,)), ((), ()))
TN = (((0,), (0,)), ((), ()))
MESH = pl.DeviceIdType.MESH
RELS = [(0, 0, 1), (0, 1, 0), (0, 1, 1), (1, 0, 0), (1, 0, 1), (1, 1, 0), (1, 1, 1)]


def _params(sem=None):
    return pltpu.CompilerParams(dimension_semantics=sem, vmem_limit_bytes=VMEM_LIMIT)


def _sds(shape, dtype):
    return jax.ShapeDtypeStruct(shape, dtype)


def _t5_bucket_table():
    qi = np.arange(BLOCK)[:, None]
    kj = np.arange(2 * BLOCK)[None, :]
    rel = qi + BLOCK - kj
    band = (rel >= 0) & (rel < BLOCK)
    relc = np.maximum(rel, 0)
    max_exact = NUM_BUCKETS // 2
    relf = np.maximum(relc, 1).astype(np.float32)
    large = max_exact + (np.log(relf / np.float32(max_exact)) / np.float32(math.log(MAX_DISTANCE / max_exact))
                         * np.float32(NUM_BUCKETS - max_exact)).astype(np.int32)
    large = np.minimum(large, NUM_BUCKETS - 1)
    bucket = np.where(relc < max_exact, relc, large).astype(np.int32)
    bucket = np.where(band, bucket, -1).astype(np.int32)
    return bucket


def _mesh_pos():
    return lax.axis_index("x"), lax.axis_index("y"), lax.axis_index("c")


def _dev_index(p):
    return 4 * p[0] + 2 * p[1] + p[2]


def _gather_call(xs):
    n = len(xs)

    def body(*refs):
        x_refs, o_refs = refs[:n], refs[n:2 * n]
        send_sems, recv_sems, local_sems = refs[2 * n:]
        x, y, c = _mesh_pos()
        me, sib = (x, y, c), (x, y, 1 - c)
        chips = [(1 - x, y), (x, 1 - y), (1 - x, 1 - y)]

        def copy(a, k, block, to, src=None):
            slot = o_refs[a].at[_dev_index(block)]
            return pltpu.make_async_remote_copy(
                src_ref=slot if src is None else src, dst_ref=slot,
                send_sem=send_sems.at[a * 7 + k], recv_sem=recv_sems.at[a * 7 + k],
                device_id=to, device_id_type=MESH)

        mine = [pltpu.make_async_copy(x_refs[a], o_refs[a].at[_dev_index(me)], local_sems.at[a]) for a in range(n)]
        for cp in mine:
            cp.start()
        first = []
        for a in range(n):
            first.append(copy(a, 0, me, sib, src=x_refs[a]))
            first += [copy(a, 1 + j, me, (*chip, c), src=x_refs[a]) for j, chip in enumerate(chips)]
        for cp in first:
            cp.start()
        passed = []
        for j, chip in enumerate(chips):
            for a in range(n):
                copy(a, 1 + j, (*chip, c), me).wait_recv()
                fwd = copy(a, 4 + j, (*chip, c), sib)
                fwd.start()
                passed.append(fwd)
        for a in range(n):
            copy(a, 0, sib, me).wait_recv()
            for j, chip in enumerate(chips):
                copy(a, 4 + j, (*chip, 1 - c), me).wait_recv()
        for cp in first + passed:
            cp.wait_send()
        for cp in mine:
            cp.wait()

    any_spec = pl.BlockSpec(memory_space=pl.ANY)
    return pl.pallas_call(
        body,
        name="gather_weights",
        out_shape=[_sds((N_DEV,) + a.shape, a.dtype) for a in xs],
        in_specs=[any_spec] * n,
        out_specs=[any_spec] * n,
        scratch_shapes=[pltpu.SemaphoreType.DMA((7 * n,)), pltpu.SemaphoreType.DMA((7 * n,)),
                        pltpu.SemaphoreType.DMA((n,))],
    )(*xs)


def _exchange_call(bs):
    n = len(bs)

    def body(*refs):
        b_refs, r_refs = refs[:n], refs[n:2 * n]
        send_sems, recv_sems, local_sems = refs[2 * n:]
        x, y, c = _mesh_pos()
        me_idx = _dev_index((x, y, c))
        mine = [pltpu.make_async_copy(b_refs[a].at[me_idx], r_refs[a].at[me_idx], local_sems.at[a]) for a in range(n)]
        for cp in mine:
            cp.start()
        sent = []
        for k, r in enumerate(RELS):
            peer = ((1 - x) if r[0] else x, (1 - y) if r[1] else y, (1 - c) if r[2] else c)
            pidx = _dev_index(peer)
            for a in range(n):
                out = pltpu.make_async_remote_copy(
                    src_ref=b_refs[a].at[pidx], dst_ref=r_refs[a].at[me_idx],
                    send_sem=send_sems.at[a * 7 + k], recv_sem=recv_sems.at[a * 7 + k],
                    device_id=peer, device_id_type=MESH)
                out.start()
                inc = pltpu.make_async_remote_copy(
                    src_ref=b_refs[a].at[pidx], dst_ref=r_refs[a].at[pidx],
                    send_sem=send_sems.at[a * 7 + k], recv_sem=recv_sems.at[a * 7 + k],
                    device_id=peer, device_id_type=MESH)
                sent.append((out, inc))
        for out, inc in sent:
            inc.wait_recv()
        for out, inc in sent:
            out.wait_send()
        for cp in mine:
            cp.wait()

    any_spec = pl.BlockSpec(memory_space=pl.ANY)
    return pl.pallas_call(
        body,
        name="exchange_grads",
        out_shape=[_sds(b.shape, b.dtype) for b in bs],
        in_specs=[any_spec] * n,
        out_specs=[any_spec] * n,
        scratch_shapes=[pltpu.SemaphoreType.DMA((7 * n,)), pltpu.SemaphoreType.DMA((7 * n,)),
                        pltpu.SemaphoreType.DMA((n,))],
    )(*bs)


def _proj_call(x2, w_al, wvt, tm):
    s_len = x2.shape[0]

    def body(x_ref, w_ref, wvt_ref, qf_ref, kf_ref, vf_ref, fz_ref, qs_ref, ks_ref, vs_ref, sz_ref, fft_ref, vat_ref):
        xb = x_ref[...].astype(bf16)
        vt = lax.dot_general(wvt_ref[...], xb, NT, preferred_element_type=f32)
        ones_row = jnp.where(lax.broadcasted_iota(jnp.int32, (HEAD_DIM, tm), 0) == 0, 1.0, 0.0).astype(bf16)
        for h in range(FOX_HEADS):
            vat_ref[h, 0:HEAD_DIM, :] = vt[h * HEAD_DIM:(h + 1) * HEAD_DIM, :].astype(bf16)
            vat_ref[h, HEAD_DIM:2 * HEAD_DIM, :] = ones_row

        def seg(off, width):
            return jnp.dot(xb, w_ref[:, off:off + width], preferred_element_type=f32)

        def put_heads(ref, acc, nheads):
            for h in range(nheads):
                ref[h] = acc[:, h * HEAD_DIM:(h + 1) * HEAD_DIM].astype(bf16)

        put_heads(qf_ref, seg(A_FQ, 512) * SCALE, FOX_HEADS)
        put_heads(kf_ref, seg(A_FK, 512), FOX_HEADS)
        put_heads(vf_ref, seg(A_FV, 512), FOX_HEADS)
        fz_ref[...] = seg(A_FZ, 512)
        put_heads(qs_ref, seg(A_SQ, 512) * SCALE, SWA_HEADS)
        put_heads(ks_ref, seg(A_SK, 128), SWA_KV_HEADS)
        put_heads(vs_ref, seg(A_SV, 128), SWA_KV_HEADS)
        sz_ref[...] = seg(A_SZ, 512)
        fft_ref[...] = seg(A_FF, 128).T[:FOX_HEADS, :]

    def heads(nh):
        return pl.BlockSpec((nh, tm, HEAD_DIM), lambda i: (0, i, 0))

    wide = pl.BlockSpec((tm, 512), lambda i: (i, 0))
    return pl.pallas_call(
        body,
        name="proj_fwd",
        grid=(s_len // tm,),
        in_specs=[pl.BlockSpec((tm, D_MODEL), lambda i: (i, 0)), pl.BlockSpec((D_MODEL, A_W), lambda i: (0, 0)),
                  pl.BlockSpec((512, D_MODEL), lambda i: (0, 0))],
        out_specs=[heads(8), heads(8), heads(8), wide, heads(8), heads(2), heads(2), wide,
                   pl.BlockSpec((FOX_HEADS, tm), lambda i: (0, i)),
                   pl.BlockSpec((FOX_HEADS, 2 * HEAD_DIM, tm), lambda i: (0, 0, i))],
        out_shape=[_sds((8, s_len, HEAD_DIM), bf16)] * 3 + [_sds((s_len, 512), f32), _sds((8, s_len, HEAD_DIM), bf16),
                   _sds((2, s_len, HEAD_DIM), bf16), _sds((2, s_len, HEAD_DIM), bf16), _sds((s_len, 512), f32),
                   _sds((FOX_HEADS, s_len), f32), _sds((FOX_HEADS, 2 * HEAD_DIM, s_len), bf16)],
        compiler_params=_params(("arbitrary",)),
    )(x2, w_al, wvt)


AUG = 2 * HEAD_DIM


def _augment_call(q, k, cum_col, tm):
    nh, s_len, _ = q.shape

    def body(q_ref, k_ref, c_ref, qa_ref, ka_ref):
        c = c_ref[0]
        hi = c.astype(bf16).astype(f32)
        r1 = c - hi
        mid = r1.astype(bf16).astype(f32)
        lo = (r1 - mid).astype(bf16).astype(f32)
        lane = lax.broadcasted_iota(jnp.int32, (tm, HEAD_DIM), 1)
        q_tail = jnp.where(lane == 0, hi, jnp.where(lane == 1, mid, jnp.where(lane == 2, lo,
                           jnp.where(lane < 6, 1.0, 0.0))))
        k_tail = jnp.where(lane < 3, 1.0, jnp.where(lane == 3, -hi, jnp.where(lane == 4, -mid,
                           jnp.where(lane == 5, -lo, 0.0))))
        qa_ref[0] = jnp.concatenate([q_ref[0], q_tail.astype(bf16)], axis=1)
        ka_ref[0] = jnp.concatenate([k_ref[0], k_tail.astype(bf16)], axis=1)

    tile = pl.BlockSpec((1, tm, HEAD_DIM), lambda h, i: (h, i, 0))
    wide = pl.BlockSpec((1, tm, AUG), lambda h, i: (h, i, 0))
    return pl.pallas_call(
        body,
        name="fox_augment",
        grid=(nh, s_len // tm),
        in_specs=[tile, tile, pl.BlockSpec((1, tm, 1), lambda h, i: (h, i, 0))],
        out_specs=[wide, wide],
        out_shape=[_sds((nh, s_len, AUG), bf16)] * 2,
        compiler_params=_params(("arbitrary", "arbitrary")),
    )(q, k, cum_col)


CUM_CHUNK = 512


def _cum_call(fft, bf_col):
    s_len = fft.shape[1]
    ch = CUM_CHUNK

    def body(f_ref, b_ref, cum_ref, sg_ref):
        r = lax.broadcasted_iota(jnp.int32, (ch, ch), 0)
        c = lax.broadcasted_iota(jnp.int32, (ch, ch), 1)
        upper = (r <= c).astype(f32)
        carry = jnp.zeros((FOX_HEADS, 1), f32)
        for n in range(s_len // ch):
            z = f_ref[:, n * ch:(n + 1) * ch] + b_ref[...]
            logf = jnp.minimum(z, 0.0) - jnp.log1p(jnp.exp(-jnp.abs(z)))
            sg_ref[:, n * ch:(n + 1) * ch] = 1.0 / (1.0 + jnp.exp(z))
            cs = jnp.dot(logf, upper, precision=HIGHEST, preferred_element_type=f32) + carry
            cum_ref[:, n * ch:(n + 1) * ch] = cs
            carry = cs[:, ch - 1:ch]

    return pl.pallas_call(
        body,
        name="fox_cum_fwd",
        out_shape=[_sds((FOX_HEADS, s_len), f32)] * 2,
        compiler_params=_params(),
    )(fft, bf_col)


def _cum_bwd_call(dcq, dck, sg):
    s_len = sg.shape[1]
    ch = CUM_CHUNK
    nch = s_len // ch

    def body(q_ref, k_ref, sg_ref, dff_ref, dbf_ref):
        r = lax.broadcasted_iota(jnp.int32, (ch, ch), 0)
        c = lax.broadcasted_iota(jnp.int32, (ch, ch), 1)
        lower = (r >= c).astype(f32)
        dff_ref[...] = jnp.zeros_like(dff_ref)
        carry = jnp.zeros((FOX_HEADS, 1), f32)
        total = jnp.zeros((FOX_HEADS, 1), f32)
        for n in reversed(range(nch)):
            sl = slice(n * ch, (n + 1) * ch)
            dcum = q_ref[:, sl] - k_ref[:, sl]
            rs = jnp.dot(dcum, lower, precision=HIGHEST, preferred_element_type=f32) + carry
            carry = rs[:, 0:1]
            dff = rs * sg_ref[:, sl]
            dff_ref[0:FOX_HEADS, sl] = dff
            total = total + jnp.sum(dff, axis=1, keepdims=True)
        dbf_ref[...] = jnp.broadcast_to(total, (FOX_HEADS, 128))

    return pl.pallas_call(
        body,
        name="fox_cum_bwd",
        out_shape=[_sds((128, s_len), f32), _sds((FOX_HEADS, 128), f32)],
        compiler_params=_params(),
    )(dcq, dck, sg)


FOX_T = 512
LANES = 128


def _causal_keep(t):
    return lax.broadcasted_iota(jnp.int32, (t, t), 0) <= lax.broadcasted_iota(jnp.int32, (t, t), 1)


def _fox_fwd_call(qa, ka, vat):
    nh, s_len, _ = qa.shape
    t = FOX_T

    def body(qa_ref, ka_ref, vat_ref, o_ref, lse_ref, s0, s1, p0, p1, a0, a1, m_ref, acc_ref):
        i = pl.program_id(1)
        qb = qa_ref[0]
        m_ref[...] = jnp.full((1, t), NEG_INF, f32)
        acc_ref[...] = jnp.zeros((AUG, t), f32)
        bufs = ((s0, p0, a0), (s1, p1, a1))

        def scores(j, b, masked):
            kb = ka_ref[0, pl.ds(pl.multiple_of(j * t, t), t), :]
            st = lax.dot_general(kb, qb, NT, preferred_element_type=f32)
            if masked:
                st = jnp.where(_causal_keep(t), st, NEG_INF)
            bufs[b][0][...] = st

        def softmax(b):
            s_ref, p_ref, a_ref = bufs[b]
            for c in range(t // LANES):
                cols = slice(c * LANES, (c + 1) * LANES)
                m_old = m_ref[:, cols]
                m_new = jnp.maximum(m_old, jnp.max(s_ref[:, cols], axis=0, keepdims=True))
                m_ref[:, cols] = m_new
                a_ref[:, cols] = jnp.exp(m_old - m_new)
                p_ref[:, cols] = jnp.exp(s_ref[:, cols] - m_new).astype(bf16)

        def accum(j, b):
            vt = vat_ref[0, :, pl.ds(pl.multiple_of(j * t, t), t)]
            acc_ref[...] = bufs[b][2][...] * acc_ref[...] + jnp.dot(vt, bufs[b][1][...], preferred_element_type=f32)

        def double(j, masked_second):
            scores(j, 0, False)
            scores(j + 1, 1, masked_second)
            softmax(0)
            accum(j, 0)
            softmax(1)
            accum(j + 1, 1)

        def off_diag(d, _):
            double(2 * d, False)
            return 0

        lax.fori_loop(0, i // 2, off_diag, 0)

        @pl.when(i % 2 == 1)
        def _():
            double(i - 1, True)

        @pl.when(i % 2 == 0)
        def _():
            scores(i, 0, True)
            softmax(0)
            accum(i, 0)

        l = acc_ref[HEAD_DIM:HEAD_DIM + 1, :]
        o_ref[0] = acc_ref[0:HEAD_DIM, :] / l
        lse_ref[0] = m_ref[...] + jnp.log(l)

    return pl.pallas_call(
        body,
        name="fox_fwd",
        grid=(nh, s_len // t),
        in_specs=[pl.BlockSpec((1, t, AUG), lambda h, i: (h, i, 0)),
                  pl.BlockSpec((1, s_len, AUG), lambda h, i: (h, 0, 0)),
                  pl.BlockSpec((1, AUG, s_len), lambda h, i: (h, 0, 0))],
        out_specs=[pl.BlockSpec((1, HEAD_DIM, t), lambda h, i: (h, 0, i)),
                   pl.BlockSpec((1, 1, t), lambda h, i: (h, 0, i))],
        out_shape=[_sds((nh, HEAD_DIM, s_len), f32), _sds((nh, 1, s_len), f32)],
        scratch_shapes=[pltpu.VMEM((t, t), f32), pltpu.VMEM((t, t), f32), pltpu.VMEM((t, t), bf16),
                        pltpu.VMEM((t, t), bf16), pltpu.VMEM((1, t), f32), pltpu.VMEM((1, t), f32),
                        pltpu.VMEM((1, t), f32), pltpu.VMEM((AUG, t), f32)],
        compiler_params=_params(("arbitrary", "arbitrary")),
    )(qa, ka, vat)


SWA_TS = 512


def _swa_bias_call(rel_bias, bucket):
    def body(rb_ref, bk_ref, b_ref, b0_ref):
        bk = bk_ref[...]
        col = lax.broadcasted_iota(jnp.int32, (BLOCK, 2 * BLOCK), 1)
        for h in range(SWA_HEADS):
            acc = jnp.full((BLOCK, 2 * BLOCK), NEG_INF, f32)
            for b in range(NUM_BUCKETS):
                acc = jnp.where(bk == b, rb_ref[b, h], acc)
            g, hh = divmod(h, SWA_GROUP)
            b_ref[g, hh * BLOCK:(hh + 1) * BLOCK, :] = acc
            b0_ref[g, hh * BLOCK:(hh + 1) * BLOCK, :] = jnp.where(col < BLOCK, NEG_INF, acc)

    return pl.pallas_call(
        body,
        name="swa_bias",
        in_specs=[pl.BlockSpec(memory_space=pltpu.SMEM), pl.BlockSpec(memory_space=pltpu.VMEM)],
        out_shape=[_sds((SWA_KV_HEADS, SWA_GROUP * BLOCK, 2 * BLOCK), f32)] * 2,
        compiler_params=_params(),
    )(rel_bias, bucket)


def _swa_bias_bwd_call(dbias, bucket):
    def body(d_ref, bk_ref, o_ref):
        bk = bk_ref[...]
        row = lax.broadcasted_iota(jnp.int32, (NUM_BUCKETS, 128), 0)
        col = lax.broadcasted_iota(jnp.int32, (NUM_BUCKETS, 128), 1)
        out = jnp.zeros((NUM_BUCKETS, 128), f32)
        for h in range(SWA_HEADS):
            g, hh = divmod(h, SWA_GROUP)
            d = d_ref[g, hh * BLOCK:(hh + 1) * BLOCK, :]
            for b in range(NUM_BUCKETS):
                val = jnp.sum(jnp.sum(jnp.where(bk == b, d, 0.0), axis=1, keepdims=True), axis=0, keepdims=True)
                out = jnp.where((row == b) & (col == h), val, out)
        o_ref[...] = out

    return pl.pallas_call(
        body,
        name="swa_bias_bwd",
        out_shape=_sds((NUM_BUCKETS, 128), f32),
        compiler_params=_params(),
    )(dbias, bucket)


def _swa_specs(ts):
    nb = ts // BLOCK
    qspec = pl.BlockSpec((SWA_HEADS, ts, HEAD_DIM), lambda n: (0, n, 0))
    cur = pl.BlockSpec((SWA_KV_HEADS, ts, HEAD_DIM), lambda n: (0, n, 0))
    prev = pl.BlockSpec((SWA_KV_HEADS, BLOCK, HEAD_DIM), lambda n: (0, jnp.maximum(n * nb - 1, 0), 0))
    return qspec, cur, prev


def _sink_col(sink_ref, g):
    return jnp.concatenate([jnp.full((BLOCK, 1), sink_ref[g * SWA_GROUP + hh], f32) for hh in range(SWA_GROUP)], axis=0)


def _swa_fwd_call(q, k, v, bias, bias0, sink):
    s_len = q.shape[1]
    ts = SWA_TS
    nb = ts // BLOCK

    def body(q_ref, kc_ref, kp_ref, vc_ref, vp_ref, b_ref, b0_ref, sink_ref, o_ref, lse_ref):
        first = pl.program_id(0) == 0
        for g in range(SWA_KV_HEADS):
            kall = jnp.concatenate([kp_ref[g], kc_ref[g]], axis=0)
            vall = jnp.concatenate([vp_ref[g], vc_ref[g]], axis=0)
            sink_c = _sink_col(sink_ref, g)
            for b in range(nb):
                rows = slice(b * BLOCK, (b + 1) * BLOCK)
                qg = jnp.concatenate([q_ref[g * SWA_GROUP + hh, rows, :] for hh in range(SWA_GROUP)], axis=0)
                kcat = kall[b * BLOCK:(b + 2) * BLOCK]
                vcat = vall[b * BLOCK:(b + 2) * BLOCK]
                bias_b = b_ref[g]
                if b == 0:
                    bias_b = jnp.where(first, b0_ref[g], bias_b)
                s = lax.dot_general(qg, kcat, NT, preferred_element_type=f32) + bias_b
                m = jnp.maximum(jnp.max(s, axis=1, keepdims=True), sink_c)
                p = jnp.exp(s - m)
                l = jnp.sum(p, axis=1, keepdims=True) + jnp.exp(sink_c - m)
                o = jnp.dot(p.astype(bf16), vcat, preferred_element_type=f32) / l
                lse = m + jnp.log(l)
                for hh in range(SWA_GROUP):
                    o_ref[g * SWA_GROUP + hh, rows, :] = o[hh * BLOCK:(hh + 1) * BLOCK]
                    lse_ref[g * SWA_GROUP + hh, rows, :] = lse[hh * BLOCK:(hh + 1) * BLOCK]

    qspec, cur, prev = _swa_specs(ts)
    bspec = pl.BlockSpec((SWA_KV_HEADS, SWA_GROUP * BLOCK, 2 * BLOCK), lambda n: (0, 0, 0))
    return pl.pallas_call(
        body,
        name="swa_fwd",
        grid=(s_len // ts,),
        in_specs=[qspec, cur, prev, cur, prev, bspec, bspec, pl.BlockSpec(memory_space=pltpu.SMEM)],
        out_specs=[pl.BlockSpec((SWA_HEADS, ts, HEAD_DIM), lambda n: (0, n, 0)),
                   pl.BlockSpec((SWA_HEADS, ts, 1), lambda n: (0, n, 0))],
        out_shape=[_sds((SWA_HEADS, s_len, HEAD_DIM), f32), _sds((SWA_HEADS, s_len, 1), f32)],
        compiler_params=_params(("arbitrary",)),
    )(q, k, k, v, v, bias, bias0, sink)


def _head_selector():
    sel = np.zeros((512, 128), np.float32)
    for h in range(8):
        sel[h * HEAD_DIM:(h + 1) * HEAD_DIM, h] = 1.0
    return sel


def _post_call(of, fz, osw, sz, x2, tgt, wo, ln_g, ln_b, sel, tm):
    s_len = x2.shape[0]

    def body(of_ref, fz_ref, os_ref, sz_ref, x_ref, t_ref, wo_ref, g_ref, b_ref, sel_ref,
             dh_ref, dof_ref, dfz_ref, dos_ref, dsz_ref, dlf_ref, dls_ref, dwo_ref, dg_ref, db_ref, loss_ref):
        n = pl.program_id(0)

        @pl.when(n == 0)
        def _():
            dwo_ref[...] = jnp.zeros_like(dwo_ref)
            dg_ref[...] = jnp.zeros_like(dg_ref)
            db_ref[...] = jnp.zeros_like(db_ref)
            loss_ref[...] = jnp.zeros_like(loss_ref)

        o_f = of_ref[...].T
        o_s = jnp.concatenate([os_ref[h] for h in range(SWA_HEADS)], axis=1)
        fz = fz_ref[...]
        sz = sz_ref[...]
        sg_f = jax.nn.sigmoid(fz)
        sg_s = jax.nn.sigmoid(sz)
        silu_f = fz * sg_f
        silu_s = sz * sg_s
        mixed = jnp.concatenate([o_f * silu_f, o_s * silu_s], axis=1).astype(bf16)
        y = jnp.dot(mixed, wo_ref[...], preferred_element_type=f32)
        h = ALPHA * x_ref[...] + y
        mu = jnp.mean(h, axis=1, keepdims=True)
        hc = h - mu
        var = jnp.mean(hc * hc, axis=1, keepdims=True)
        rstd = lax.rsqrt(var + LN_EPS)
        xhat = hc * rstd
        gam = g_ref[...]
        out = xhat * gam + b_ref[...]
        err = out - t_ref[...]
        tok_loss = jnp.mean(err * err, axis=1, keepdims=True)
        loss_ref[...] += 0.5 * jnp.sum(tok_loss, axis=0, keepdims=True)
        dout = err * (1.0 / D_MODEL)
        dg_ref[...] += jnp.sum(dout * xhat, axis=0, keepdims=True)
        db_ref[...] += jnp.sum(dout, axis=0, keepdims=True)
        dxh = dout * gam
        m1 = jnp.mean(dxh, axis=1, keepdims=True)
        m2 = jnp.mean(dxh * xhat, axis=1, keepdims=True)
        dh = rstd * (dxh - m1 - xhat * m2)
        dh_ref[...] = dh
        dyb = dh.astype(bf16)
        dwo_ref[...] += lax.dot_general(mixed, dyb, TN, preferred_element_type=f32)
        dmix = lax.dot_general(dyb, wo_ref[...], NT, preferred_element_type=f32)
        dm_f = dmix[:, :512]
        dm_s = dmix[:, 512:]
        do_f = dm_f * silu_f
        do_s = dm_s * silu_s
        dfz_ref[...] = (dm_f * o_f * (sg_f * (1.0 + fz * (1.0 - sg_f)))).astype(bf16)
        dsz_ref[...] = (dm_s * o_s * (sg_s * (1.0 + sz * (1.0 - sg_s)))).astype(bf16)
        for hd in range(FOX_HEADS):
            dof_ref[hd] = do_f[:, hd * HEAD_DIM:(hd + 1) * HEAD_DIM].astype(bf16)
            dos_ref[hd] = do_s[:, hd * HEAD_DIM:(hd + 1) * HEAD_DIM].astype(bf16)
        sel_m = sel_ref[...]
        dl_f = jnp.dot(do_f * o_f, sel_m, precision=HIGHEST, preferred_element_type=f32)
        dl_s = jnp.dot(do_s * o_s, sel_m, precision=HIGHEST, preferred_element_type=f32)
        dlf_ref[...] = dl_f.T[:FOX_HEADS, :]
        dls_ref[...] = dl_s

    heads_f32 = pl.BlockSpec((8, tm, HEAD_DIM), lambda n: (0, n, 0))
    half = pl.BlockSpec((tm, 512), lambda n: (n, 0))
    fullw = pl.BlockSpec((tm, D_MODEL), lambda n: (n, 0))
    vec = pl.BlockSpec((1, D_MODEL), lambda n: (0, 0))
    return pl.pallas_call(
        body,
        name="post_fwd_bwd",
        grid=(s_len // tm,),
        in_specs=[pl.BlockSpec((512, tm), lambda n: (0, n)), half, heads_f32, half, fullw, fullw,
                  pl.BlockSpec((D_MODEL, D_MODEL), lambda n: (0, 0)), vec, vec,
                  pl.BlockSpec((512, 128), lambda n: (0, 0))],
        out_specs=[fullw, heads_f32, half, heads_f32, half,
                   pl.BlockSpec((FOX_HEADS, tm), lambda n: (0, n)), pl.BlockSpec((tm, 128), lambda n: (n, 0)),
                   pl.BlockSpec((D_MODEL, D_MODEL), lambda n: (0, 0)), vec, vec,
                   pl.BlockSpec((1, 1), lambda n: (0, 0))],
        out_shape=[_sds((s_len, D_MODEL), f32), _sds((8, s_len, HEAD_DIM), bf16), _sds((s_len, 512), bf16),
                   _sds((8, s_len, HEAD_DIM), bf16), _sds((s_len, 512), bf16),
                   _sds((FOX_HEADS, s_len), f32), _sds((s_len, 128), f32),
                   _sds((D_MODEL, D_MODEL), f32), _sds((1, D_MODEL), f32), _sds((1, D_MODEL), f32),
                   _sds((1, 1), f32)],
        compiler_params=_params(("arbitrary",)),
    )(of, fz, osw, sz, x2, tgt, wo, ln_g, ln_b, sel)


def _fox_bwd_call(ka, v, qa, do, lse_row, dl_row):
    nh, s_len, _ = qa.shape
    t = FOX_T
    nt = s_len // t
    ck_slot = HEAD_DIM + 3
    cq_slot = HEAD_DIM

    def body(ka_ref, v_ref, qa_ref, do_ref, lse_ref, dl_ref,
             dk_ref, dv_ref, dq_ref, dcq_ref, dck_ref, dqt_s, dka_s, dv_s):
        j = pl.program_id(1)

        @pl.when(j == 0)
        def _():
            dqt_s[...] = jnp.zeros_like(dqt_s)

        kb = ka_ref[0]
        vb = v_ref[0]
        kt = kb.astype(f32).T.astype(bf16)
        dka_s[...] = jnp.zeros_like(dka_s)
        dv_s[...] = jnp.zeros_like(dv_s)

        def pair(i, masked):
            off = pl.multiple_of(i * t, t)
            qb = qa_ref[0, pl.ds(off, t), :]
            dob = do_ref[0, pl.ds(off, t), :]
            st = lax.dot_general(kb, qb, NT, preferred_element_type=f32)
            if masked:
                st = jnp.where(_causal_keep(t), st, NEG_INF)
            pt = jnp.exp(st - lse_ref[0, :, pl.ds(off, t)])
            dpt = lax.dot_general(vb, dob, NT, preferred_element_type=f32)
            dsb = (pt * (dpt - dl_ref[0, :, pl.ds(off, t)])).astype(bf16)
            dv_s[...] += jnp.dot(pt.astype(bf16), dob, preferred_element_type=f32)
            dka_s[...] += jnp.dot(dsb, qb, preferred_element_type=f32)
            dqt_s[:, pl.ds(off, t)] += jnp.dot(kt, dsb, preferred_element_type=f32)

        def off_diag(i, _):
            pair(i, False)
            return 0

        pair(j, True)
        lax.fori_loop(j + 1, nt, off_diag, 0)
        dk_ref[0] = dka_s[:, 0:HEAD_DIM].astype(bf16)
        dv_ref[0] = dv_s[...].astype(bf16)
        dck_ref[0] = dka_s[:, ck_slot:ck_slot + 1]

        @pl.when(j == nt - 1)
        def _():
            for cidx in range(nt):
                sl = slice(cidx * t, (cidx + 1) * t)
                dq_ref[0, sl, :] = (dqt_s[0:HEAD_DIM, sl].T * SCALE).astype(bf16)
            dcq_ref[0] = dqt_s[cq_slot:cq_slot + 1, :]

    ktile = pl.BlockSpec((1, t, AUG), lambda h, j: (h, j, 0))
    tile = pl.BlockSpec((1, t, HEAD_DIM), lambda h, j: (h, j, 0))
    qfull = pl.BlockSpec((1, s_len, AUG), lambda h, j: (h, 0, 0))
    full = pl.BlockSpec((1, s_len, HEAD_DIM), lambda h, j: (h, 0, 0))
    rowv = pl.BlockSpec((1, 1, s_len), lambda h, j: (h, 0, 0))
    colt = pl.BlockSpec((1, t, 1), lambda h, j: (h, j, 0))
    return pl.pallas_call(
        body,
        name="fox_bwd",
        grid=(nh, nt),
        in_specs=[ktile, tile, qfull, full, rowv, rowv],
        out_specs=[tile, tile, full, rowv, colt],
        out_shape=[_sds((nh, s_len, HEAD_DIM), bf16)] * 3 + [_sds((nh, 1, s_len), f32), _sds((nh, s_len, 1), f32)],
        scratch_shapes=[pltpu.VMEM((AUG, s_len), f32), pltpu.VMEM((t, AUG), f32), pltpu.VMEM((t, HEAD_DIM), f32)],
        compiler_params=_params(("arbitrary", "arbitrary")),
    )(ka, v, qa, do, lse_row, dl_row)


def _swa_bwd_call(q, k, v, do, lse, dl, bias, bias0, sink):
    s_len = q.shape[1]
    ts = SWA_TS
    nb = ts // BLOCK
    nsteps = s_len // ts

    def body(q_ref, kc_ref, kp_ref, vc_ref, vp_ref, do_ref, lse_ref, dl_ref, b_ref, b0_ref, sink_ref,
             dq_ref, dk_ref, dv_ref, dbias_ref, dsink_ref, dk_s, dv_s, tail_k, tail_v, sk_s):
        n = pl.program_id(0)

        @pl.when(n == 0)
        def _():
            dbias_ref[...] = jnp.zeros_like(dbias_ref)
            sk_s[...] = jnp.zeros_like(sk_s)

        @pl.when(n < nsteps)
        def _():
            first = n == 0
            dk_s[...] = jnp.zeros_like(dk_s)
            dv_s[...] = jnp.zeros_like(dv_s)
            for g in range(SWA_KV_HEADS):
                kall = jnp.concatenate([kp_ref[g], kc_ref[g]], axis=0)
                vall = jnp.concatenate([vp_ref[g], vc_ref[g]], axis=0)
                sink_c = _sink_col(sink_ref, g)
                for b in range(nb):
                    rows = slice(b * BLOCK, (b + 1) * BLOCK)
                    heads = [g * SWA_GROUP + hh for hh in range(SWA_GROUP)]
                    qg = jnp.concatenate([q_ref[h, rows, :] for h in heads], axis=0)
                    dog = jnp.concatenate([do_ref[h, rows, :] for h in heads], axis=0)
                    lse_c = jnp.concatenate([lse_ref[h, rows, :] for h in heads], axis=0)
                    dl_c = jnp.concatenate([dl_ref[rows, h:h + 1] for h in heads], axis=0)
                    kcat = kall[b * BLOCK:(b + 2) * BLOCK]
                    vcat = vall[b * BLOCK:(b + 2) * BLOCK]
                    bias_b = b_ref[g]
                    if b == 0:
                        bias_b = jnp.where(first, b0_ref[g], bias_b)
                    s = lax.dot_general(qg, kcat, NT, preferred_element_type=f32) + bias_b
                    p = jnp.exp(s - lse_c)
                    dp = lax.dot_general(dog, vcat, NT, preferred_element_type=f32)
                    ds = p * (dp - dl_c)
                    dsb = ds.astype(bf16)
                    dqg = jnp.dot(dsb, kcat, preferred_element_type=f32) * SCALE
                    for hh, h in enumerate(heads):
                        dq_ref[h, rows, :] = dqg[hh * BLOCK:(hh + 1) * BLOCK].astype(bf16)
                    win = slice(b * BLOCK, (b + 2) * BLOCK)
                    dk_s[g, win, :] += lax.dot_general(dsb, qg, TN, preferred_element_type=f32)
                    dv_s[g, win, :] += lax.dot_general(p.astype(bf16), dog, TN, preferred_element_type=f32)
                    dbias_ref[g] += ds
                    sk_s[g] += -jnp.exp(sink_c - lse_c) * dl_c

        @pl.when(n > 0)
        def _():
            last = slice(ts - BLOCK, ts)
            for g in range(SWA_KV_HEADS):
                add_k = jnp.where(n < nsteps, dk_s[g, 0:BLOCK, :], 0.0)
                add_v = jnp.where(n < nsteps, dv_s[g, 0:BLOCK, :], 0.0)
                dk_ref[g, 0:ts - BLOCK, :] = tail_k[g, 0:ts - BLOCK, :].astype(bf16)
                dv_ref[g, 0:ts - BLOCK, :] = tail_v[g, 0:ts - BLOCK, :].astype(bf16)
                dk_ref[g, last, :] = (tail_k[g, last, :] + add_k).astype(bf16)
                dv_ref[g, last, :] = (tail_v[g, last, :] + add_v).astype(bf16)

        @pl.when(n < nsteps)
        def _():
            tail_k[...] = dk_s[:, BLOCK:, :]
            tail_v[...] = dv_s[:, BLOCK:, :]

        @pl.when(n == nsteps)
        def _():
            row = lax.broadcasted_iota(jnp.int32, (SWA_HEADS, 128), 0)
            out = jnp.zeros((SWA_HEADS, 128), f32)
            for h in range(SWA_HEADS):
                g, hh = divmod(h, SWA_GROUP)
                val = jnp.sum(sk_s[g, hh * BLOCK:(hh + 1) * BLOCK, :], axis=0, keepdims=True)
                out = jnp.where(row == h, val, out)
            dsink_ref[...] = out

    last_step = nsteps - 1

    def cl(n):
        return jnp.minimum(n, last_step)

    qspec = pl.BlockSpec((SWA_HEADS, ts, HEAD_DIM), lambda n: (0, cl(n), 0))
    cur = pl.BlockSpec((SWA_KV_HEADS, ts, HEAD_DIM), lambda n: (0, cl(n), 0))
    prev = pl.BlockSpec((SWA_KV_HEADS, BLOCK, HEAD_DIM), lambda n: (0, jnp.maximum(cl(n) * nb - 1, 0), 0))
    lsespec = pl.BlockSpec((SWA_HEADS, ts, 1), lambda n: (0, cl(n), 0))
    dlspec = pl.BlockSpec((ts, 128), lambda n: (cl(n), 0))
    bspec = pl.BlockSpec((SWA_KV_HEADS, SWA_GROUP * BLOCK, 2 * BLOCK), lambda n: (0, 0, 0))
    kvout = pl.BlockSpec((SWA_KV_HEADS, ts, HEAD_DIM), lambda n: (0, jnp.maximum(n - 1, 0), 0))
    return pl.pallas_call(
        body,
        name="swa_bwd",
        grid=(nsteps + 1,),
        in_specs=[qspec, cur, prev, cur, prev, qspec, lsespec, dlspec, bspec, bspec,
                  pl.BlockSpec(memory_space=pltpu.SMEM)],
        out_specs=[qspec, kvout, kvout, bspec, pl.BlockSpec((SWA_HEADS, 128), lambda n: (0, 0))],
        out_shape=[_sds((SWA_HEADS, s_len, HEAD_DIM), bf16), _sds((SWA_KV_HEADS, s_len, HEAD_DIM), bf16),
                   _sds((SWA_KV_HEADS, s_len, HEAD_DIM), bf16),
                   _sds((SWA_KV_HEADS, SWA_GROUP * BLOCK, 2 * BLOCK), f32), _sds((SWA_HEADS, 128), f32)],
        scratch_shapes=[pltpu.VMEM((SWA_KV_HEADS, ts + BLOCK, HEAD_DIM), f32),
                        pltpu.VMEM((SWA_KV_HEADS, ts + BLOCK, HEAD_DIM), f32),
                        pltpu.VMEM((SWA_KV_HEADS, ts, HEAD_DIM), f32),
                        pltpu.VMEM((SWA_KV_HEADS, ts, HEAD_DIM), f32),
                        pltpu.VMEM((SWA_KV_HEADS, SWA_GROUP * BLOCK, 1), f32)],
        compiler_params=_params(("arbitrary",)),
    )(q, k, k, v, v, do, lse, dl, bias, bias0, sink)


def _dx_call(dh, dqf, dkf, dvf, dfz, dqs, dks, dvs, dsz, dfft, w_al, tm):
    s_len = dh.shape[0]

    def body(dh_ref, dqf_ref, dkf_ref, dvf_ref, dfz_ref, dqs_ref, dks_ref, dvs_ref, dsz_ref, dfft_ref, w_ref,
             dx_ref, dp_ref):
        def cat(ref, nheads):
            return jnp.concatenate([ref[h] for h in range(nheads)], axis=1)

        dp = jnp.concatenate([cat(dqf_ref, 8), cat(dkf_ref, 8), cat(dvf_ref, 8), dfz_ref[...], cat(dqs_ref, 8),
                              cat(dks_ref, 2), cat(dvs_ref, 2), dsz_ref[...], dfft_ref[...].T.astype(bf16)], axis=1)
        dp_ref[...] = dp
        dx_ref[...] = ALPHA * dh_ref[...] + lax.dot_general(dp, w_ref[...], NT, preferred_element_type=f32)

    def heads(nh):
        return pl.BlockSpec((nh, tm, HEAD_DIM), lambda i: (0, i, 0))

    half = pl.BlockSpec((tm, 512), lambda i: (i, 0))
    fullw = pl.BlockSpec((tm, D_MODEL), lambda i: (i, 0))
    return pl.pallas_call(
        body,
        name="dx_bwd",
        grid=(s_len // tm,),
        in_specs=[fullw, heads(8), heads(8), heads(8), half, heads(8), heads(2), heads(2), half,
                  pl.BlockSpec((128, tm), lambda i: (0, i)), pl.BlockSpec((D_MODEL, A_W), lambda i: (0, 0))],
        out_specs=[fullw, pl.BlockSpec((tm, A_W), lambda i: (i, 0))],
        out_shape=[_sds((s_len, D_MODEL), f32), _sds((s_len, A_W), bf16)],
        compiler_params=_params(("arbitrary",)),
    )(dh, dqf, dkf, dvf, dfz, dqs, dks, dvs, dsz, dfft, w_al)


DW_COLS = 1152


def _dw_call(x2, dproj, tm):
    s_len = x2.shape[0]
    nt = s_len // tm

    def body(x_ref, dp_ref, dw_ref):
        @pl.when(pl.program_id(1) == 0)
        def _():
            dw_ref[...] = jnp.zeros_like(dw_ref)

        dw_ref[...] += lax.dot_general(x_ref[...].astype(bf16), dp_ref[...], TN, preferred_element_type=f32)

    return pl.pallas_call(
        body,
        name="dw_in_bwd",
        grid=(A_W // DW_COLS, nt),
        in_specs=[pl.BlockSpec((tm, D_MODEL), lambda c, i: (i, 0)), pl.BlockSpec((tm, DW_COLS), lambda c, i: (i, c))],
        out_specs=pl.BlockSpec((D_MODEL, DW_COLS), lambda c, i: (0, c)),
        out_shape=_sds((D_MODEL, A_W), f32),
        compiler_params=_params(("arbitrary", "arbitrary")),
    )(x2, dproj)


def _adam_call(recv, w, m, v, tr, name):
    rows, cols = w.shape

    def body(r_ref, w_ref, m_ref, v_ref, g_ref, d_ref, mo_ref, vo_ref):
        g = r_ref[0]
        for p in range(1, N_DEV):
            g = g + r_ref[p]
        mn = ADAM_B1 * m_ref[...] + (1.0 - ADAM_B1) * g
        vn = ADAM_B2 * v_ref[...] + (1.0 - ADAM_B2) * (g * g)
        m_hat = mn / (1.0 - ADAM_B1 ** ADAM_STEP)
        v_hat = vn / (1.0 - ADAM_B2 ** ADAM_STEP)
        g_ref[...] = g
        d_ref[...] = -ADAM_LR * (m_hat / (jnp.sqrt(v_hat) + ADAM_EPS) + ADAM_WD * w_ref[...])
        mo_ref[...] = mn
        vo_ref[...] = vn

    blk = pl.BlockSpec((tr, cols), lambda i: (i, 0))
    return pl.pallas_call(
        body,
        name=name,
        grid=(rows // tr,),
        in_specs=[pl.BlockSpec((N_DEV, tr, cols), lambda i: (0, i, 0)), blk, blk, blk],
        out_specs=[blk] * 4,
        out_shape=[_sds((rows, cols), f32)] * 4,
        compiler_params=_params(("arbitrary",)),
    )(recv, w, m, v)


def _pad_cols(a, width=128):
    return jnp.pad(a, ((0, 0), (0, width - a.shape[1])))


def _pack_small(ln_g, ln_b, rel, b_f, sink):
    return jnp.concatenate([
        ln_g.reshape(8, 128), ln_b.reshape(8, 128), _pad_cols(rel),
        jnp.pad(_pad_cols(b_f), ((0, 7), (0, 0))), jnp.pad(_pad_cols(sink), ((0, 7), (0, 0)))], axis=0)


def _unpack_small(p):
    return (p[0:8].reshape(1, D_MODEL), p[8:16].reshape(1, D_MODEL), p[16:48, 0:8], p[48:49, 0:8], p[56:57, 0:8])


def kernel(x, w_in, b_f, rel_bias, sink, w_o, ln_g, ln_b, loss_target, m_w_in, m_b_f, m_rel_bias, m_sink, m_w_o, m_ln_g, m_ln_b, v_w_in, v_b_f, v_rel_bias, v_sink, v_w_o, v_ln_g, v_ln_b):
    x2 = x[0]
    tgt = loss_target[0]
    s_len = x2.shape[0]
    shard = w_in.shape[2]

    g_in, g_o = _gather_call([w_in[0].astype(bf16), w_o[0].astype(bf16)])
    w_full = jnp.transpose(g_in, (1, 0, 2)).reshape(D_MODEL, N_DEV * shard)
    w_al = jnp.concatenate([w_full[:, :O_FF0], w_full[:, O_FF1:], w_full[:, O_FF0:O_FF1],
                            jnp.zeros((D_MODEL, A_W - D_IN), bf16)], axis=1)
    wo_full = g_o.reshape(D_MODEL, D_MODEL)

    wvt = jnp.transpose(w_al[:, A_FV:A_FV + 512])
    qf, kf, vf, fz, qs, ks, vs, sz, fft, vat = _proj_call(x2, w_al, wvt, 512)
    cum, sgm = _cum_call(fft, b_f.reshape(FOX_HEADS, 1))
    qa, ka = _augment_call(qf, kf, cum.reshape(FOX_HEADS, s_len, 1), 1024)
    o_ft, lse_f = _fox_fwd_call(qa, ka, vat)
    bucket = jnp.asarray(_t5_bucket_table())
    bias, bias0 = _swa_bias_call(rel_bias, bucket)
    sink_v = sink.reshape(SWA_HEADS)
    o_s, lse_s = _swa_fwd_call(qs, ks, vs, bias, bias0, sink_v)

    (dh, do_f, dfz, do_s, dsz, dl_f, dl_s, dwo, dg, db, loss_part) = _post_call(
        o_ft.reshape(FOX_HEADS * HEAD_DIM, s_len), fz, o_s, sz, x2, tgt, wo_full, ln_g, ln_b,
        jnp.asarray(_head_selector()), 256)

    dkf, dvf, dqf, dcq, dck = _fox_bwd_call(ka, vf, qa, do_f, lse_f, dl_f.reshape(FOX_HEADS, 1, s_len))
    dfft, dbf = _cum_bwd_call(dcq.reshape(FOX_HEADS, s_len), dck.reshape(FOX_HEADS, s_len), sgm)
    dqs, dks, dvs, dbias, dsink = _swa_bwd_call(qs, ks, vs, do_s, lse_s, dl_s, bias, bias0, sink_v)
    drel = _swa_bias_bwd_call(dbias, bucket)

    dx, dproj = _dx_call(dh, dqf, dkf, dvf, dfz, dqs, dks, dvs, dsz, dfft, w_al, 256)
    dw_al = _dw_call(x2, dproj, 1024)

    dw_full = jnp.concatenate([dw_al[:, :O_FF0], dw_al[:, A_FF:A_FF + (O_FF1 - O_FF0)], dw_al[:, O_FF0:A_FF]], axis=1)
    dw_blocks = jnp.transpose(dw_full.reshape(D_MODEL, N_DEV, shard), (1, 0, 2))
    dwo_blocks = dwo.reshape(N_DEV, D_MODEL // N_DEV, D_MODEL)
    small = _pack_small(dg, db, drel[:, 0:8], dbf[:, 0].reshape(1, 8), dsink[:, 0].reshape(1, 8))
    loss_slot = np.zeros((64, 128), bool)
    loss_slot[49, 0] = True
    small = jnp.where(jnp.asarray(loss_slot), loss_part[0, 0], small)
    small_blocks = jnp.broadcast_to(small[None], (N_DEV,) + small.shape)
    r_in, r_o, r_small = _exchange_call([dw_blocks, dwo_blocks, small_blocks])

    g_win, d_win, nm_win, nv_win = _adam_call(r_in, w_in[0], m_w_in[0], v_w_in[0], 128, "adam_w_in")
    g_wo, d_wo, nm_wo, nv_wo = _adam_call(r_o, w_o[0], m_w_o[0], v_w_o[0], 64, "adam_w_o")
    p_w = _pack_small(ln_g, ln_b, rel_bias, b_f, sink)
    p_m = _pack_small(m_ln_g, m_ln_b, m_rel_bias, m_b_f, m_sink)
    p_v = _pack_small(v_ln_g, v_ln_b, v_rel_bias, v_b_f, v_sink)
    g_p, d_p, nm_p, nv_p = _adam_call(r_small, p_w, p_m, p_v, 64, "adam_small")

    loss = g_p[49, 0]
    g_lng, g_lnb, g_rel, g_bf, g_sink = _unpack_small(g_p)
    d_lng, d_lnb, d_rel, d_bf, d_sink = _unpack_small(d_p)
    m_lng, m_lnb, m_rel, m_bf, m_sk = _unpack_small(nm_p)
    v_lng, v_lnb, v_rel, v_bf, v_sk = _unpack_small(nv_p)
    return (loss, dx[None], g_win[None], g_bf, g_rel, g_sink, g_wo[None], g_lng, g_lnb,
            d_win[None], d_bf, d_rel, d_sink, d_wo[None], d_lng, d_lnb,
            nm_win[None], m_bf, m_rel, m_sk, nm_wo[None], m_lng, m_lnb,
            nv_win[None], v_bf, v_rel, v_sk, nv_wo[None], v_lng, v_lnb)
```

```python
import functools
import math

import numpy as np
import jax
import jax.numpy as jnp
from jax import lax
from jax.experimental import pallas as pl
from jax.experimental.pallas import tpu as pltpu

f32 = jnp.float32
bf16 = jnp.bfloat16

D_MODEL = 1024
HEAD_DIM = 64
FOX_HEADS = 8
SWA_HEADS = 8
SWA_KV_HEADS = 2
SWA_GROUP = 4
BLOCK = 128
NUM_BUCKETS = 32
MAX_DISTANCE = 128
LN_EPS = 1e-5
NEG_INF = -1e30
ALPHA = 2.0 ** 0.25
SCALE = 1.0 / math.sqrt(HEAD_DIM)
D_IN = 3336

ADAM_LR = 0.001
ADAM_B1 = 0.9
ADAM_B2 = 0.999
ADAM_EPS = 1e-08
ADAM_WD = 0.01
ADAM_STEP = 10

N_DEV = 8
A_FQ, A_FK, A_FV, A_FZ, A_SQ, A_SK, A_SV, A_SZ, A_FF, A_W = 0, 512, 1024, 1536, 2048, 2560, 2688, 2816, 3328, 3456
O_FF0, O_FF1 = 1536, 1544

VMEM_LIMIT = 48 * 1024 * 1024
HIGHEST = lax.Precision.HIGHEST
NT = (((1,), (1,)), ((), ()))
TN = (((0,), (0,)), ((), ()))
MESH = pl.DeviceIdType.MESH
RELS = [(0, 0, 1), (0, 1, 0), (0, 1, 1), (1, 0, 0), (1, 0, 1), (1, 1, 0), (1, 1, 1)]


def _params(sem=None):
    return pltpu.CompilerParams(dimension_semantics=sem, vmem_limit_bytes=VMEM_LIMIT)


def _sds(shape, dtype):
    return jax.ShapeDtypeStruct(shape, dtype)


def _t5_bucket_table():
    qi = np.arange(BLOCK)[:, None]
    kj = np.arange(2 * BLOCK)[None, :]
    rel = qi + BLOCK - kj
    band = (rel >= 0) & (rel < BLOCK)
    relc = np.maximum(rel, 0)
    max_exact = NUM_BUCKETS // 2
    relf = np.maximum(relc, 1).astype(np.float32)
    large = max_exact + (np.log(relf / np.float32(max_exact)) / np.float32(math.log(MAX_DISTANCE / max_exact))
                         * np.float32(NUM_BUCKETS - max_exact)).astype(np.int32)
    large = np.minimum(large, NUM_BUCKETS - 1)
    bucket = np.where(relc < max_exact, relc, large).astype(np.int32)
    bucket = np.where(band, bucket, -1).astype(np.int32)
    return bucket


def _mesh_pos():
    return lax.axis_index("x"), lax.axis_index("y"), lax.axis_index("c")


def _dev_index(p):
    return 4 * p[0] + 2 * p[1] + p[2]


def _gather_call(xs):
    n = len(xs)

    def body(*refs):
        x_refs, o_refs = refs[:n], refs[n:2 * n]
        send_sems, recv_sems, local_sems = refs[2 * n:]
        x, y, c = _mesh_pos()
        me, sib = (x, y, c), (x, y, 1 - c)
        chips = [(1 - x, y), (x, 1 - y), (1 - x, 1 - y)]

        def copy(a, k, block, to, src=None):
            slot = o_refs[a].at[_dev_index(block)]
            return pltpu.make_async_remote_copy(
                src_ref=slot if src is None else src, dst_ref=slot,
                send_sem=send_sems.at[a * 7 + k], recv_sem=recv_sems.at[a * 7 + k],
                device_id=to, device_id_type=MESH)

        mine = [pltpu.make_async_copy(x_refs[a], o_refs[a].at[_dev_index(me)], local_sems.at[a]) for a in range(n)]
        for cp in mine:
            cp.start()
        first = []
        for a in range(n):
            first.append(copy(a, 0, me, sib, src=x_refs[a]))
            first += [copy(a, 1 + j, me, (*chip, c), src=x_refs[a]) for j, chip in enumerate(chips)]
        for cp in first:
            cp.start()
        passed = []
        for j, chip in enumerate(chips):
            for a in range(n):
                copy(a, 1 + j, (*chip, c), me).wait_recv()
                fwd = copy(a, 4 + j, (*chip, c), sib)
                fwd.start()
                passed.append(fwd)
        for a in range(n):
            copy(a, 0, sib, me).wait_recv()
            for j, chip in enumerate(chips):
                copy(a, 4 + j, (*chip, 1 - c), me).wait_recv()
        for cp in first + passed:
            cp.wait_send()
        for cp in mine:
            cp.wait()

    any_spec = pl.BlockSpec(memory_space=pl.ANY)
    return pl.pallas_call(
        body,
        name="gather_weights",
        out_shape=[_sds((N_DEV,) + a.shape, a.dtype) for a in xs],
        in_specs=[any_spec] * n,
        out_specs=[any_spec] * n,
        scratch_shapes=[pltpu.SemaphoreType.DMA((7 * n,)), pltpu.SemaphoreType.DMA((7 * n,)),
                        pltpu.SemaphoreType.DMA((n,))],
    )(*xs)


def _exchange_call(bs):
    n = len(bs)

    def body(*refs):
        b_refs, r_refs = refs[:n], refs[n:2 * n]
        send_sems, recv_sems, local_sems = refs[2 * n:]
        x, y, c = _mesh_pos()
        me_idx = _dev_index((x, y, c))
        mine = [pltpu.make_async_copy(b_refs[a].at[me_idx], r_refs[a].at[me_idx], local_sems.at[a]) for a in range(n)]
        for cp in mine:
            cp.start()
        sent = []
        for k, r in enumerate(RELS):
            peer = ((1 - x) if r[0] else x, (1 - y) if r[1] else y, (1 - c) if r[2] else c)
            pidx = _dev_index(peer)
            for a in range(n):
                out = pltpu.make_async_remote_copy(
                    src_ref=b_refs[a].at[pidx], dst_ref=r_refs[a].at[me_idx],
                    send_sem=send_sems.at[a * 7 + k], recv_sem=recv_sems.at[a * 7 + k],
                    device_id=peer, device_id_type=MESH)
                out.start()
                inc = pltpu.make_async_remote_copy(
                    src_ref=b_refs[a].at[pidx], dst_ref=r_refs[a].at[pidx],
                    send_sem=send_sems.at[a * 7 + k], recv_sem=recv_sems.at[a * 7 + k],
                    device_id=peer, device_id_type=MESH)
                sent.append((out, inc))
        for out, inc in sent:
            inc.wait_recv()
        for out, inc in sent:
            out.wait_send()
        for cp in mine:
            cp.wait()

    any_spec = pl.BlockSpec(memory_space=pl.ANY)
    return pl.pallas_call(
        body,
        name="exchange_grads",
        out_shape=[_sds(b.shape, b.dtype) for b in bs],
        in_specs=[any_spec] * n,
        out_specs=[any_spec] * n,
        scratch_shapes=[pltpu.SemaphoreType.DMA((7 * n,)), pltpu.SemaphoreType.DMA((7 * n,)),
                        pltpu.SemaphoreType.DMA((n,))],
    )(*bs)


def _proj_call(x2, w_al, wvt, tm):
    s_len = x2.shape[0]

    def body(x_ref, w_ref, wvt_ref, qf_ref, kf_ref, vf_ref, fz_ref, qs_ref, ks_ref, vs_ref, sz_ref, fft_ref, vat_ref):
        xb = x_ref[...].astype(bf16)
        vt = lax.dot_general(wvt_ref[...], xb, NT, preferred_element_type=f32)
        ones_row = jnp.where(lax.broadcasted_iota(jnp.int32, (HEAD_DIM, tm), 0) == 0, 1.0, 0.0).astype(bf16)
        for h in range(FOX_HEADS):
            vat_ref[h, 0:HEAD_DIM, :] = vt[h * HEAD_DIM:(h + 1) * HEAD_DIM, :].astype(bf16)
            vat_ref[h, HEAD_DIM:2 * HEAD_DIM, :] = ones_row

        def seg(off, width):
            return jnp.dot(xb, w_ref[:, off:off + width], preferred_element_type=f32)

        def put_heads(ref, acc, nheads):
            for h in range(nheads):
                ref[h] = acc[:, h * HEAD_DIM:(h + 1) * HEAD_DIM].astype(bf16)

        put_heads(qf_ref, seg(A_FQ, 512) * SCALE, FOX_HEADS)
        put_heads(kf_ref, seg(A_FK, 512), FOX_HEADS)
        put_heads(vf_ref, seg(A_FV, 512), FOX_HEADS)
        fz_ref[...] = seg(A_FZ, 512)
        put_heads(qs_ref, seg(A_SQ, 512) * SCALE, SWA_HEADS)
        put_heads(ks_ref, seg(A_SK, 128), SWA_KV_HEADS)
        put_heads(vs_ref, seg(A_SV, 128), SWA_KV_HEADS)
        sz_ref[...] = seg(A_SZ, 512)
        fft_ref[...] = seg(A_FF, 128).T[:FOX_HEADS, :]

    def heads(nh):
        return pl.BlockSpec((nh, tm, HEAD_DIM), lambda i: (0, i, 0))

    wide = pl.BlockSpec((tm, 512), lambda i: (i, 0))
    return pl.pallas_call(
        body,
        name="proj_fwd",
        grid=(s_len // tm,),
        in_specs=[pl.BlockSpec((tm, D_MODEL), lambda i: (i, 0)), pl.BlockSpec((D_MODEL, A_W), lambda i: (0, 0)),
                  pl.BlockSpec((512, D_MODEL), lambda i: (0, 0))],
        out_specs=[heads(8), heads(8), heads(8), wide, heads(8), heads(2), heads(2), wide,
                   pl.BlockSpec((FOX_HEADS, tm), lambda i: (0, i)),
                   pl.BlockSpec((FOX_HEADS, 2 * HEAD_DIM, tm), lambda i: (0, 0, i))],
        out_shape=[_sds((8, s_len, HEAD_DIM), bf16)] * 3 + [_sds((s_len, 512), f32), _sds((8, s_len, HEAD_DIM), bf16),
                   _sds((2, s_len, HEAD_DIM), bf16), _sds((2, s_len, HEAD_DIM), bf16), _sds((s_len, 512), f32),
                   _sds((FOX_HEADS, s_len), f32), _sds((FOX_HEADS, 2 * HEAD_DIM, s_len), bf16)],
        compiler_params=_params(("arbitrary",)),
    )(x2, w_al, wvt)


AUG = 2 * HEAD_DIM


def _augment_call(q, k, cum_col, tm):
    nh, s_len, _ = q.shape

    def body(q_ref, k_ref, c_ref, qa_ref, ka_ref):
        c = c_ref[0]
        hi = c.astype(bf16).astype(f32)
        r1 = c - hi
        mid = r1.astype(bf16).astype(f32)
        lo = (r1 - mid).astype(bf16).astype(f32)
        lane = lax.broadcasted_iota(jnp.int32, (tm, HEAD_DIM), 1)
        q_tail = jnp.where(lane == 0, hi, jnp.where(lane == 1, mid, jnp.where(lane == 2, lo,
                           jnp.where(lane < 6, 1.0, 0.0))))
        k_tail = jnp.where(lane < 3, 1.0, jnp.where(lane == 3, -hi, jnp.where(lane == 4, -mid,
                           jnp.where(lane == 5, -lo, 0.0))))
        qa_ref[0] = jnp.concatenate([q_ref[0], q_tail.astype(bf16)], axis=1)
        ka_ref[0] = jnp.concatenate([k_ref[0], k_tail.astype(bf16)], axis=1)

    tile = pl.BlockSpec((1, tm, HEAD_DIM), lambda h, i: (h, i, 0))
    wide = pl.BlockSpec((1, tm, AUG), lambda h, i: (h, i, 0))
    return pl.pallas_call(
        body,
        name="fox_augment",
        grid=(nh, s_len // tm),
        in_specs=[tile, tile, pl.BlockSpec((1, tm, 1), lambda h, i: (h, i, 0))],
        out_specs=[wide, wide],
        out_shape=[_sds((nh, s_len, AUG), bf16)] * 2,
        compiler_params=_params(("arbitrary", "arbitrary")),
    )(q, k, cum_col)


CUM_CHUNK = 512


def _cum_call(fft, bf_col):
    s_len = fft.shape[1]
    ch = CUM_CHUNK

    def body(f_ref, b_ref, cum_ref, sg_ref):
        r = lax.broadcasted_iota(jnp.int32, (ch, ch), 0)
        c = lax.broadcasted_iota(jnp.int32, (ch, ch), 1)
        upper = (r <= c).astype(f32)
        carry = jnp.zeros((FOX_HEADS, 1), f32)
        for n in range(s_len // ch):
            z = f_ref[:, n * ch:(n + 1) * ch] + b_ref[...]
            logf = jnp.minimum(z, 0.0) - jnp.log1p(jnp.exp(-jnp.abs(z)))
            sg_ref[:, n * ch:(n + 1) * ch] = 1.0 / (1.0 + jnp.exp(z))
            cs = jnp.dot(logf, upper, precision=HIGHEST, preferred_element_type=f32) + carry
            cum_ref[:, n * ch:(n + 1) * ch] = cs
            carry = cs[:, ch - 1:ch]

    return pl.pallas_call(
        body,
        name="fox_cum_fwd",
        out_shape=[_sds((FOX_HEADS, s_len), f32)] * 2,
        compiler_params=_params(),
    )(fft, bf_col)


def _cum_bwd_call(dcq, dck, sg):
    s_len = sg.shape[1]
    ch = CUM_CHUNK
    nch = s_len // ch

    def body(q_ref, k_ref, sg_ref, dff_ref, dbf_ref):
        r = lax.broadcasted_iota(jnp.int32, (ch, ch), 0)
        c = lax.broadcasted_iota(jnp.int32, (ch, ch), 1)
        lower = (r >= c).astype(f32)
        dff_ref[...] = jnp.zeros_like(dff_ref)
        carry = jnp.zeros((FOX_HEADS, 1), f32)
        total = jnp.zeros((FOX_HEADS, 1), f32)
        for n in reversed(range(nch)):
            sl = slice(n * ch, (n + 1) * ch)
            dcum = q_ref[:, sl] - k_ref[:, sl]
            rs = jnp.dot(dcum, lower, precision=HIGHEST, preferred_element_type=f32) + carry
            carry = rs[:, 0:1]
            dff = rs * sg_ref[:, sl]
            dff_ref[0:FOX_HEADS, sl] = dff
            total = total + jnp.sum(dff, axis=1, keepdims=True)
        dbf_ref[...] = jnp.broadcast_to(total, (FOX_HEADS, 128))

    return pl.pallas_call(
        body,
        name="fox_cum_bwd",
        out_shape=[_sds((128, s_len), f32), _sds((FOX_HEADS, 128), f32)],
        compiler_params=_params(),
    )(dcq, dck, sg)


FOX_T = 512
LANES = 128


def _causal_keep(t):
    return lax.broadcasted_iota(jnp.int32, (t, t), 0) <= lax.broadcasted_iota(jnp.int32, (t, t), 1)


def _fox_fwd_call(qa, ka, vat):
    nh, s_len, _ = qa.shape
    t = FOX_T

    def body(qa_ref, ka_ref, vat_ref, o_ref, lse_ref, s0, s1, p0, p1, a0, a1, m_ref, acc_ref):
        i = pl.program_id(1)
        qb = qa_ref[0]
        m_ref[...] = jnp.full((1, t), NEG_INF, f32)
        acc_ref[...] = jnp.zeros((AUG, t), f32)
        bufs = ((s0, p0, a0), (s1, p1, a1))

        def scores(j, b, masked):
            kb = ka_ref[0, pl.ds(pl.multiple_of(j * t, t), t), :]
            st = lax.dot_general(kb, qb, NT, preferred_element_type=f32)
            if masked:
                st = jnp.where(_causal_keep(t), st, NEG_INF)
            bufs[b][0][...] = st

        def softmax(b):
            s_ref, p_ref, a_ref = bufs[b]
            for c in range(t // LANES):
                cols = slice(c * LANES, (c + 1) * LANES)
                m_old = m_ref[:, cols]
                m_new = jnp.maximum(m_old, jnp.max(s_ref[:, cols], axis=0, keepdims=True))
                m_ref[:, cols] = m_new
                a_ref[:, cols] = jnp.exp(m_old - m_new)
                p_ref[:, cols] = jnp.exp(s_ref[:, cols] - m_new).astype(bf16)

        def accum(j, b):
            vt = vat_ref[0, :, pl.ds(pl.multiple_of(j * t, t), t)]
            acc_ref[...] = bufs[b][2][...] * acc_ref[...] + jnp.dot(vt, bufs[b][1][...], preferred_element_type=f32)

        def key_tile(n):
            return jnp.where(n == 0, i, n - 1)

        def step(n, b):
            scores(n - 1, b, False)
            softmax(1 - b)
            accum(key_tile(n - 2), b)

        @pl.when(i == 0)
        def _():
            scores(0, 0, True)
            softmax(0)
            accum(0, 0)

        @pl.when(i >= 1)
        def _():
            scores(i, 0, True)
            scores(0, 1, False)
            softmax(0)

        def two_steps(d, _):
            n = 2 + 2 * d
            step(n, 0)
            step(n + 1, 1)
            return 0

        lax.fori_loop(0, (i - 1) // 2, two_steps, 0)

        @pl.when((i >= 2) & (i % 2 == 0))
        def _():
            step(i, 0)
            softmax(0)
            accum(i - 2, 1)
            accum(i - 1, 0)

        @pl.when(i % 2 == 1)
        def _():
            softmax(1)
            accum(key_tile(i - 1), 0)
            accum(i - 1, 1)

        l = acc_ref[HEAD_DIM:HEAD_DIM + 1, :]
        o_ref[0] = acc_ref[0:HEAD_DIM, :] / l
        lse_ref[0] = m_ref[...] + jnp.log(l)

    return pl.pallas_call(
        body,
        name="fox_fwd",
        grid=(nh, s_len // t),
        in_specs=[pl.BlockSpec((1, t, AUG), lambda h, i: (h, i, 0)),
                  pl.BlockSpec((1, s_len, AUG), lambda h, i: (h, 0, 0)),
                  pl.BlockSpec((1, AUG, s_len), lambda h, i: (h, 0, 0))],
        out_specs=[pl.BlockSpec((1, HEAD_DIM, t), lambda h, i: (h, 0, i)),
                   pl.BlockSpec((1, 1, t), lambda h, i: (h, 0, i))],
        out_shape=[_sds((nh, HEAD_DIM, s_len), f32), _sds((nh, 1, s_len), f32)],
        scratch_shapes=[pltpu.VMEM((t, t), f32), pltpu.VMEM((t, t), f32), pltpu.VMEM((t, t), bf16),
                        pltpu.VMEM((t, t), bf16), pltpu.VMEM((1, t), f32), pltpu.VMEM((1, t), f32),
                        pltpu.VMEM((1, t), f32), pltpu.VMEM((AUG, t), f32)],
        compiler_params=_params(("arbitrary", "arbitrary")),
    )(qa, ka, vat)


SWA_TS = 512


def _swa_bias_call(rel_bias, bucket):
    def body(rb_ref, bk_ref, b_ref, b0_ref):
        bk = bk_ref[...]
        col = lax.broadcasted_iota(jnp.int32, (BLOCK, 2 * BLOCK), 1)
        for h in range(SWA_HEADS):
            acc = jnp.full((BLOCK, 2 * BLOCK), NEG_INF, f32)
            for b in range(NUM_BUCKETS):
                acc = jnp.where(bk == b, rb_ref[b, h], acc)
            g, hh = divmod(h, SWA_GROUP)
            b_ref[g, hh * BLOCK:(hh + 1) * BLOCK, :] = acc
            b0_ref[g, hh * BLOCK:(hh + 1) * BLOCK, :] = jnp.where(col < BLOCK, NEG_INF, acc)

    return pl.pallas_call(
        body,
        name="swa_bias",
        in_specs=[pl.BlockSpec(memory_space=pltpu.SMEM), pl.BlockSpec(memory_space=pltpu.VMEM)],
        out_shape=[_sds((SWA_KV_HEADS, SWA_GROUP * BLOCK, 2 * BLOCK), f32)] * 2,
        compiler_params=_params(),
    )(rel_bias, bucket)


def _swa_bias_bwd_call(dbias, bucket):
    def body(d_ref, bk_ref, o_ref):
        bk = bk_ref[...]
        row = lax.broadcasted_iota(jnp.int32, (NUM_BUCKETS, 128), 0)
        col = lax.broadcasted_iota(jnp.int32, (NUM_BUCKETS, 128), 1)
        out = jnp.zeros((NUM_BUCKETS, 128), f32)
        for h in range(SWA_HEADS):
            g, hh = divmod(h, SWA_GROUP)
            d = d_ref[g, hh * BLOCK:(hh + 1) * BLOCK, :]
            for b in range(NUM_BUCKETS):
                val = jnp.sum(jnp.sum(jnp.where(bk == b, d, 0.0), axis=1, keepdims=True), axis=0, keepdims=True)
                out = jnp.where((row == b) & (col == h), val, out)
        o_ref[...] = out

    return pl.pallas_call(
        body,
        name="swa_bias_bwd",
        out_shape=_sds((NUM_BUCKETS, 128), f32),
        compiler_params=_params(),
    )(dbias, bucket)


def _swa_specs(ts):
    nb = ts // BLOCK
    qspec = pl.BlockSpec((SWA_HEADS, ts, HEAD_DIM), lambda n: (0, n, 0))
    cur = pl.BlockSpec((SWA_KV_HEADS, ts, HEAD_DIM), lambda n: (0, n, 0))
    prev = pl.BlockSpec((SWA_KV_HEADS, BLOCK, HEAD_DIM), lambda n: (0, jnp.maximum(n * nb - 1, 0), 0))
    return qspec, cur, prev


def _sink_col(sink_ref, g):
    return jnp.concatenate([jnp.full((BLOCK, 1), sink_ref[g * SWA_GROUP + hh], f32) for hh in range(SWA_GROUP)], axis=0)


def _swa_fwd_call(q, k, v, bias, bias0, sink):
    s_len = q.shape[1]
    ts = SWA_TS
    nb = ts // BLOCK

    def body(q_ref, kc_ref, kp_ref, vc_ref, vp_ref, b_ref, b0_ref, sink_ref, o_ref, lse_ref):
        first = pl.program_id(0) == 0
        for g in range(SWA_KV_HEADS):
            kall = jnp.concatenate([kp_ref[g], kc_ref[g]], axis=0)
            vall = jnp.concatenate([vp_ref[g], vc_ref[g]], axis=0)
            sink_c = _sink_col(sink_ref, g)
            for b in range(nb):
                rows = slice(b * BLOCK, (b + 1) * BLOCK)
                qg = jnp.concatenate([q_ref[g * SWA_GROUP + hh, rows, :] for hh in range(SWA_GROUP)], axis=0)
                kcat = kall[b * BLOCK:(b + 2) * BLOCK]
                vcat = vall[b * BLOCK:(b + 2) * BLOCK]
                bias_b = b_ref[g]
                if b == 0:
                    bias_b = jnp.where(first, b0_ref[g], bias_b)
                s = lax.dot_general(qg, kcat, NT, preferred_element_type=f32) + bias_b
                m = jnp.maximum(jnp.max(s, axis=1, keepdims=True), sink_c)
                p = jnp.exp(s - m)
                l = jnp.sum(p, axis=1, keepdims=True) + jnp.exp(sink_c - m)
                o = jnp.dot(p.astype(bf16), vcat, preferred_element_type=f32) / l
                lse = m + jnp.log(l)
                for hh in range(SWA_GROUP):
                    o_ref[g * SWA_GROUP + hh, rows, :] = o[hh * BLOCK:(hh + 1) * BLOCK]
                    lse_ref[g * SWA_GROUP + hh, rows, :] = lse[hh * BLOCK:(hh + 1) * BLOCK]

    qspec, cur, prev = _swa_specs(ts)
    bspec = pl.BlockSpec((SWA_KV_HEADS, SWA_GROUP * BLOCK, 2 * BLOCK), lambda n: (0, 0, 0))
    return pl.pallas_call(
        body,
        name="swa_fwd",
        grid=(s_len // ts,),
        in_specs=[qspec, cur, prev, cur, prev, bspec, bspec, pl.BlockSpec(memory_space=pltpu.SMEM)],
        out_specs=[pl.BlockSpec((SWA_HEADS, ts, HEAD_DIM), lambda n: (0, n, 0)),
                   pl.BlockSpec((SWA_HEADS, ts, 1), lambda n: (0, n, 0))],
        out_shape=[_sds((SWA_HEADS, s_len, HEAD_DIM), f32), _sds((SWA_HEADS, s_len, 1), f32)],
        compiler_params=_params(("arbitrary",)),
    )(q, k, k, v, v, bias, bias0, sink)


def _head_selector():
    sel = np.zeros((512, 128), np.float32)
    for h in range(8):
        sel[h * HEAD_DIM:(h + 1) * HEAD_DIM, h] = 1.0
    return sel


def _post_call(of, fz, osw, sz, x2, tgt, wo, ln_g, ln_b, sel, tm):
    s_len = x2.shape[0]

    def body(of_ref, fz_ref, os_ref, sz_ref, x_ref, t_ref, wo_ref, g_ref, b_ref, sel_ref,
             dh_ref, dof_ref, dfz_ref, dos_ref, dsz_ref, dlf_ref, dls_ref, dwo_ref, dg_ref, db_ref, loss_ref):
        n = pl.program_id(0)

        @pl.when(n == 0)
        def _():
            dwo_ref[...] = jnp.zeros_like(dwo_ref)
            dg_ref[...] = jnp.zeros_like(dg_ref)
            db_ref[...] = jnp.zeros_like(db_ref)
            loss_ref[...] = jnp.zeros_like(loss_ref)

        o_f = of_ref[...].T
        o_s = jnp.concatenate([os_ref[h] for h in range(SWA_HEADS)], axis=1)
        fz = fz_ref[...]
        sz = sz_ref[...]
        sg_f = jax.nn.sigmoid(fz)
        sg_s = jax.nn.sigmoid(sz)
        silu_f = fz * sg_f
        silu_s = sz * sg_s
        mixed = jnp.concatenate([o_f * silu_f, o_s * silu_s], axis=1).astype(bf16)
        y = jnp.dot(mixed, wo_ref[...], preferred_element_type=f32)
        h = ALPHA * x_ref[...] + y
        mu = jnp.mean(h, axis=1, keepdims=True)
        hc = h - mu
        var = jnp.mean(hc * hc, axis=1, keepdims=True)
        rstd = lax.rsqrt(var + LN_EPS)
        xhat = hc * rstd
        gam = g_ref[...]
        out = xhat * gam + b_ref[...]
        err = out - t_ref[...]
        tok_loss = jnp.mean(err * err, axis=1, keepdims=True)
        loss_ref[...] += 0.5 * jnp.sum(tok_loss, axis=0, keepdims=True)
        dout = err * (1.0 / D_MODEL)
        dg_ref[...] += jnp.sum(dout * xhat, axis=0, keepdims=True)
        db_ref[...] += jnp.sum(dout, axis=0, keepdims=True)
        dxh = dout * gam
        m1 = jnp.mean(dxh, axis=1, keepdims=True)
        m2 = jnp.mean(dxh * xhat, axis=1, keepdims=True)
        dh = rstd * (dxh - m1 - xhat * m2)
        dh_ref[...] = dh
        dyb = dh.astype(bf16)
        dwo_ref[...] += lax.dot_general(mixed, dyb, TN, preferred_element_type=f32)
        dmix = lax.dot_general(dyb, wo_ref[...], NT, preferred_element_type=f32)
        dm_f = dmix[:, :512]
        dm_s = dmix[:, 512:]
        do_f = dm_f * silu_f
        do_s = dm_s * silu_s
        dfz_ref[...] = (dm_f * o_f * (sg_f * (1.0 + fz * (1.0 - sg_f)))).astype(bf16)
        dsz_ref[...] = (dm_s * o_s * (sg_s * (1.0 + sz * (1.0 - sg_s)))).astype(bf16)
        for hd in range(FOX_HEADS):
            dof_ref[hd] = do_f[:, hd * HEAD_DIM:(hd + 1) * HEAD_DIM].astype(bf16)
            dos_ref[hd] = do_s[:, hd * HEAD_DIM:(hd + 1) * HEAD_DIM].astype(bf16)
        sel_m = sel_ref[...]
        dl_f = jnp.dot(do_f * o_f, sel_m, precision=HIGHEST, preferred_element_type=f32)
        dl_s = jnp.dot(do_s * o_s, sel_m, precision=HIGHEST, preferred_element_type=f32)
        dlf_ref[...] = dl_f.T[:FOX_HEADS, :]
        dls_ref[...] = dl_s

    heads_f32 = pl.BlockSpec((8, tm, HEAD_DIM), lambda n: (0, n, 0))
    half = pl.BlockSpec((tm, 512), lambda n: (n, 0))
    fullw = pl.BlockSpec((tm, D_MODEL), lambda n: (n, 0))
    vec = pl.BlockSpec((1, D_MODEL), lambda n: (0, 0))
    return pl.pallas_call(
        body,
        name="post_fwd_bwd",
        grid=(s_len // tm,),
        in_specs=[pl.BlockSpec((512, tm), lambda n: (0, n)), half, heads_f32, half, fullw, fullw,
                  pl.BlockSpec((D_MODEL, D_MODEL), lambda n: (0, 0)), vec, vec,
                  pl.BlockSpec((512, 128), lambda n: (0, 0))],
        out_specs=[fullw, heads_f32, half, heads_f32, half,
                   pl.BlockSpec((FOX_HEADS, tm), lambda n: (0, n)), pl.BlockSpec((tm, 128), lambda n: (n, 0)),
                   pl.BlockSpec((D_MODEL, D_MODEL), lambda n: (0, 0)), vec, vec,
                   pl.BlockSpec((1, 1), lambda n: (0, 0))],
        out_shape=[_sds((s_len, D_MODEL), f32), _sds((8, s_len, HEAD_DIM), bf16), _sds((s_len, 512), bf16),
                   _sds((8, s_len, HEAD_DIM), bf16), _sds((s_len, 512), bf16),
                   _sds((FOX_HEADS, s_len), f32), _sds((s_len, 128), f32),
                   _sds((D_MODEL, D_MODEL), f32), _sds((1, D_MODEL), f32), _sds((1, D_MODEL), f32),
                   _sds((1, 1), f32)],
        compiler_params=_params(("arbitrary",)),
    )(of, fz, osw, sz, x2, tgt, wo, ln_g, ln_b, sel)


def _fox_bwd_call(ka, v, qa, do, lse_row, dl_row):
    nh, s_len, _ = qa.shape
    t = FOX_T
    nt = s_len // t
    ck_slot = HEAD_DIM + 3
    cq_slot = HEAD_DIM

    def body(ka_ref, v_ref, qa_ref, do_ref, lse_ref, dl_ref,
             dk_ref, dv_ref, dq_ref, dcq_ref, dck_ref, dqt_s, dka_s, dv_s):
        j = pl.program_id(1)

        @pl.when(j == 0)
        def _():
            dqt_s[...] = jnp.zeros_like(dqt_s)

        kb = ka_ref[0]
        vb = v_ref[0]
        kt = kb.astype(f32).T.astype(bf16)
        dka_s[...] = jnp.zeros_like(dka_s)
        dv_s[...] = jnp.zeros_like(dv_s)

        def pair(i, masked):
            off = pl.multiple_of(i * t, t)
            qb = qa_ref[0, pl.ds(off, t), :]
            dob = do_ref[0, pl.ds(off, t), :]
            st = lax.dot_general(kb, qb, NT, preferred_element_type=f32)
            if masked:
                st = jnp.where(_causal_keep(t), st, NEG_INF)
            pt = jnp.exp(st - lse_ref[0, :, pl.ds(off, t)])
            dpt = lax.dot_general(vb, dob, NT, preferred_element_type=f32)
            dsb = (pt * (dpt - dl_ref[0, :, pl.ds(off, t)])).astype(bf16)
            dv_s[...] += jnp.dot(pt.astype(bf16), dob, preferred_element_type=f32)
            dka_s[...] += jnp.dot(dsb, qb, preferred_element_type=f32)
            dqt_s[:, pl.ds(off, t)] += jnp.dot(kt, dsb, preferred_element_type=f32)

        def off_diag(i, _):
            pair(i, False)
            return 0

        pair(j, True)
        lax.fori_loop(j + 1, nt, off_diag, 0)
        dk_ref[0] = dka_s[:, 0:HEAD_DIM].astype(bf16)
        dv_ref[0] = dv_s[...].astype(bf16)
        dck_ref[0] = dka_s[:, ck_slot:ck_slot + 1]

        @pl.when(j == nt - 1)
        def _():
            for cidx in range(nt):
                sl = slice(cidx * t, (cidx + 1) * t)
                dq_ref[0, sl, :] = (dqt_s[0:HEAD_DIM, sl].T * SCALE).astype(bf16)
            dcq_ref[0] = dqt_s[cq_slot:cq_slot + 1, :]

    ktile = pl.BlockSpec((1, t, AUG), lambda h, j: (h, j, 0))
    tile = pl.BlockSpec((1, t, HEAD_DIM), lambda h, j: (h, j, 0))
    qfull = pl.BlockSpec((1, s_len, AUG), lambda h, j: (h, 0, 0))
    full = pl.BlockSpec((1, s_len, HEAD_DIM), lambda h, j: (h, 0, 0))
    rowv = pl.BlockSpec((1, 1, s_len), lambda h, j: (h, 0, 0))
    colt = pl.BlockSpec((1, t, 1), lambda h, j: (h, j, 0))
    return pl.pallas_call(
        body,
        name="fox_bwd",
        grid=(nh, nt),
        in_specs=[ktile, tile, qfull, full, rowv, rowv],
        out_specs=[tile, tile, full, rowv, colt],
        out_shape=[_sds((nh, s_len, HEAD_DIM), bf16)] * 3 + [_sds((nh, 1, s_len), f32), _sds((nh, s_len, 1), f32)],
        scratch_shapes=[pltpu.VMEM((AUG, s_len), f32), pltpu.VMEM((t, AUG), f32), pltpu.VMEM((t, HEAD_DIM), f32)],
        compiler_params=_params(("arbitrary", "arbitrary")),
    )(ka, v, qa, do, lse_row, dl_row)


def _swa_bwd_call(q, k, v, do, lse, dl, bias, bias0, sink):
    s_len = q.shape[1]
    ts = SWA_TS
    nb = ts // BLOCK
    nsteps = s_len // ts

    def body(q_ref, kc_ref, kp_ref, vc_ref, vp_ref, do_ref, lse_ref, dl_ref, b_ref, b0_ref, sink_ref,
             dq_ref, dk_ref, dv_ref, dbias_ref, dsink_ref, dk_s, dv_s, tail_k, tail_v, sk_s):
        n = pl.program_id(0)

        @pl.when(n == 0)
        def _():
            dbias_ref[...] = jnp.zeros_like(dbias_ref)
            sk_s[...] = jnp.zeros_like(sk_s)

        @pl.when(n < nsteps)
        def _():
            first = n == 0
            dk_s[...] = jnp.zeros_like(dk_s)
            dv_s[...] = jnp.zeros_like(dv_s)
            for g in range(SWA_KV_HEADS):
                kall = jnp.concatenate([kp_ref[g], kc_ref[g]], axis=0)
                vall = jnp.concatenate([vp_ref[g], vc_ref[g]], axis=0)
                sink_c = _sink_col(sink_ref, g)
                for b in range(nb):
                    rows = slice(b * BLOCK, (b + 1) * BLOCK)
                    heads = [g * SWA_GROUP + hh for hh in range(SWA_GROUP)]
                    qg = jnp.concatenate([q_ref[h, rows, :] for h in heads], axis=0)
                    dog = jnp.concatenate([do_ref[h, rows, :] for h in heads], axis=0)
                    lse_c = jnp.concatenate([lse_ref[h, rows, :] for h in heads], axis=0)
                    dl_c = jnp.concatenate([dl_ref[rows, h:h + 1] for h in heads], axis=0)
                    kcat = kall[b * BLOCK:(b + 2) * BLOCK]
                    vcat = vall[b * BLOCK:(b + 2) * BLOCK]
                    bias_b = b_ref[g]
                    if b == 0:
                        bias_b = jnp.where(first, b0_ref[g], bias_b)
                    s = lax.dot_general(qg, kcat, NT, preferred_element_type=f32) + bias_b
                    p = jnp.exp(s - lse_c)
                    dp = lax.dot_general(dog, vcat, NT, preferred_element_type=f32)
                    ds = p * (dp - dl_c)
                    dsb = ds.astype(bf16)
                    dqg = jnp.dot(dsb, kcat, preferred_element_type=f32) * SCALE
                    for hh, h in enumerate(heads):
                        dq_ref[h, rows, :] = dqg[hh * BLOCK:(hh + 1) * BLOCK].astype(bf16)
                    win = slice(b * BLOCK, (b + 2) * BLOCK)
                    dk_s[g, win, :] += lax.dot_general(dsb, qg, TN, preferred_element_type=f32)
                    dv_s[g, win, :] += lax.dot_general(p.astype(bf16), dog, TN, preferred_element_type=f32)
                    dbias_ref[g] += ds
                    sk_s[g] += -jnp.exp(sink_c - lse_c) * dl_c

        @pl.when(n > 0)
        def _():
            last = slice(ts - BLOCK, ts)
            for g in range(SWA_KV_HEADS):
                add_k = jnp.where(n < nsteps, dk_s[g, 0:BLOCK, :], 0.0)
                add_v = jnp.where(n < nsteps, dv_s[g, 0:BLOCK, :], 0.0)
                dk_ref[g, 0:ts - BLOCK, :] = tail_k[g, 0:ts - BLOCK, :].astype(bf16)
                dv_ref[g, 0:ts - BLOCK, :] = tail_v[g, 0:ts - BLOCK, :].astype(bf16)
                dk_ref[g, last, :] = (tail_k[g, last, :] + add_k).astype(bf16)
                dv_ref[g, last, :] = (tail_v[g, last, :] + add_v).astype(bf16)

        @pl.when(n < nsteps)
        def _():
            tail_k[...] = dk_s[:, BLOCK:, :]
            tail_v[...] = dv_s[:, BLOCK:, :]

        @pl.when(n == nsteps)
        def _():
            row = lax.broadcasted_iota(jnp.int32, (SWA_HEADS, 128), 0)
            out = jnp.zeros((SWA_HEADS, 128), f32)
            for h in range(SWA_HEADS):
                g, hh = divmod(h, SWA_GROUP)
                val = jnp.sum(sk_s[g, hh * BLOCK:(hh + 1) * BLOCK, :], axis=0, keepdims=True)
                out = jnp.where(row == h, val, out)
            dsink_ref[...] = out

    last_step = nsteps - 1

    def cl(n):
        return jnp.minimum(n, last_step)

    qspec = pl.BlockSpec((SWA_HEADS, ts, HEAD_DIM), lambda n: (0, cl(n), 0))
    cur = pl.BlockSpec((SWA_KV_HEADS, ts, HEAD_DIM), lambda n: (0, cl(n), 0))
    prev = pl.BlockSpec((SWA_KV_HEADS, BLOCK, HEAD_DIM), lambda n: (0, jnp.maximum(cl(n) * nb - 1, 0), 0))
    lsespec = pl.BlockSpec((SWA_HEADS, ts, 1), lambda n: (0, cl(n), 0))
    dlspec = pl.BlockSpec((ts, 128), lambda n: (cl(n), 0))
    bspec = pl.BlockSpec((SWA_KV_HEADS, SWA_GROUP * BLOCK, 2 * BLOCK), lambda n: (0, 0, 0))
    kvout = pl.BlockSpec((SWA_KV_HEADS, ts, HEAD_DIM), lambda n: (0, jnp.maximum(n - 1, 0), 0))
    return pl.pallas_call(
        body,
        name="swa_bwd",
        grid=(nsteps + 1,),
        in_specs=[qspec, cur, prev, cur, prev, qspec, lsespec, dlspec, bspec, bspec,
                  pl.BlockSpec(memory_space=pltpu.SMEM)],
        out_specs=[qspec, kvout, kvout, bspec, pl.BlockSpec((SWA_HEADS, 128), lambda n: (0, 0))],
        out_shape=[_sds((SWA_HEADS, s_len, HEAD_DIM), bf16), _sds((SWA_KV_HEADS, s_len, HEAD_DIM), bf16),
                   _sds((SWA_KV_HEADS, s_len, HEAD_DIM), bf16),
                   _sds((SWA_KV_HEADS, SWA_GROUP * BLOCK, 2 * BLOCK), f32), _sds((SWA_HEADS, 128), f32)],
        scratch_shapes=[pltpu.VMEM((SWA_KV_HEADS, ts + BLOCK, HEAD_DIM), f32),
                        pltpu.VMEM((SWA_KV_HEADS, ts + BLOCK, HEAD_DIM), f32),
                        pltpu.VMEM((SWA_KV_HEADS, ts, HEAD_DIM), f32),
                        pltpu.VMEM((SWA_KV_HEADS, ts, HEAD_DIM), f32),
                        pltpu.VMEM((SWA_KV_HEADS, SWA_GROUP * BLOCK, 1), f32)],
        compiler_params=_params(("arbitrary",)),
    )(q, k, k, v, v, do, lse, dl, bias, bias0, sink)


def _dx_call(dh, dqf, dkf, dvf, dfz, dqs, dks, dvs, dsz, dfft, w_al, tm):
    s_len = dh.shape[0]

    def body(dh_ref, dqf_ref, dkf_ref, dvf_ref, dfz_ref, dqs_ref, dks_ref, dvs_ref, dsz_ref, dfft_ref, w_ref,
             dx_ref, dp_ref):
        def cat(ref, nheads):
            return jnp.concatenate([ref[h] for h in range(nheads)], axis=1)

        dp = jnp.concatenate([cat(dqf_ref, 8), cat(dkf_ref, 8), cat(dvf_ref, 8), dfz_ref[...], cat(dqs_ref, 8),
                              cat(dks_ref, 2), cat(dvs_ref, 2), dsz_ref[...], dfft_ref[...].T.astype(bf16)], axis=1)
        dp_ref[...] = dp
        dx_ref[...] = ALPHA * dh_ref[...] + lax.dot_general(dp, w_ref[...], NT, preferred_element_type=f32)

    def heads(nh):
        return pl.BlockSpec((nh, tm, HEAD_DIM), lambda i: (0, i, 0))

    half = pl.BlockSpec((tm, 512), lambda i: (i, 0))
    fullw = pl.BlockSpec((tm, D_MODEL), lambda i: (i, 0))
    return pl.pallas_call(
        body,
        name="dx_bwd",
        grid=(s_len // tm,),
        in_specs=[fullw, heads(8), heads(8), heads(8), half, heads(8), heads(2), heads(2), half,
                  pl.BlockSpec((128, tm), lambda i: (0, i)), pl.BlockSpec((D_MODEL, A_W), lambda i: (0, 0))],
        out_specs=[fullw, pl.BlockSpec((tm, A_W), lambda i: (i, 0))],
        out_shape=[_sds((s_len, D_MODEL), f32), _sds((s_len, A_W), bf16)],
        compiler_params=_params(("arbitrary",)),
    )(dh, dqf, dkf, dvf, dfz, dqs, dks, dvs, dsz, dfft, w_al)


DW_COLS = 1152


def _dw_call(x2, dproj, tm):
    s_len = x2.shape[0]
    nt = s_len // tm

    def body(x_ref, dp_ref, dw_ref):
        @pl.when(pl.program_id(1) == 0)
        def _():
            dw_ref[...] = jnp.zeros_like(dw_ref)

        dw_ref[...] += lax.dot_general(x_ref[...].astype(bf16), dp_ref[...], TN, preferred_element_type=f32)

    return pl.pallas_call(
        body,
        name="dw_in_bwd",
        grid=(A_W // DW_COLS, nt),
        in_specs=[pl.BlockSpec((tm, D_MODEL), lambda c, i: (i, 0)), pl.BlockSpec((tm, DW_COLS), lambda c, i: (i, c))],
        out_specs=pl.BlockSpec((D_MODEL, DW_COLS), lambda c, i: (0, c)),
        out_shape=_sds((D_MODEL, A_W), f32),
        compiler_params=_params(("arbitrary", "arbitrary")),
    )(x2, dproj)


def _adam_call(recv, w, m, v, tr, name):
    rows, cols = w.shape

    def body(r_ref, w_ref, m_ref, v_ref, g_ref, d_ref, mo_ref, vo_ref):
        g = r_ref[0]
        for p in range(1, N_DEV):
            g = g + r_ref[p]
        mn = ADAM_B1 * m_ref[...] + (1.0 - ADAM_B1) * g
        vn = ADAM_B2 * v_ref[...] + (1.0 - ADAM_B2) * (g * g)
        m_hat = mn / (1.0 - ADAM_B1 ** ADAM_STEP)
        v_hat = vn / (1.0 - ADAM_B2 ** ADAM_STEP)
        g_ref[...] = g
        d_ref[...] = -ADAM_LR * (m_hat / (jnp.sqrt(v_hat) + ADAM_EPS) + ADAM_WD * w_ref[...])
        mo_ref[...] = mn
        vo_ref[...] = vn

    blk = pl.BlockSpec((tr, cols), lambda i: (i, 0))
    return pl.pallas_call(
        body,
        name=name,
        grid=(rows // tr,),
        in_specs=[pl.BlockSpec((N_DEV, tr, cols), lambda i: (0, i, 0)), blk, blk, blk],
        out_specs=[blk] * 4,
        out_shape=[_sds((rows, cols), f32)] * 4,
        compiler_params=_params(("arbitrary",)),
    )(recv, w, m, v)


def _pad_cols(a, width=128):
    return jnp.pad(a, ((0, 0), (0, width - a.shape[1])))


def _pack_small(ln_g, ln_b, rel, b_f, sink):
    return jnp.concatenate([
        ln_g.reshape(8, 128), ln_b.reshape(8, 128), _pad_cols(rel),
        jnp.pad(_pad_cols(b_f), ((0, 7), (0, 0))), jnp.pad(_pad_cols(sink), ((0, 7), (0, 0)))], axis=0)


def _unpack_small(p):
    return (p[0:8].reshape(1, D_MODEL), p[8:16].reshape(1, D_MODEL), p[16:48, 0:8], p[48:49, 0:8], p[56:57, 0:8])


def kernel(x, w_in, b_f, rel_bias, sink, w_o, ln_g, ln_b, loss_target, m_w_in, m_b_f, m_rel_bias, m_sink, m_w_o, m_ln_g, m_ln_b, v_w_in, v_b_f, v_rel_bias, v_sink, v_w_o, v_ln_g, v_ln_b):
    x2 = x[0]
    tgt = loss_target[0]
    s_len = x2.shape[0]
    shard = w_in.shape[2]

    g_in, g_o = _gather_call([w_in[0].astype(bf16), w_o[0].astype(bf16)])
    w_full = jnp.transpose(g_in, (1, 0, 2)).reshape(D_MODEL, N_DEV * shard)
    w_al = jnp.concatenate([w_full[:, :O_FF0], w_full[:, O_FF1:], w_full[:, O_FF0:O_FF1],
                            jnp.zeros((D_MODEL, A_W - D_IN), bf16)], axis=1)
    wo_full = g_o.reshape(D_MODEL, D_MODEL)

    wvt = jnp.transpose(w_al[:, A_FV:A_FV + 512])
    qf, kf, vf, fz, qs, ks, vs, sz, fft, vat = _proj_call(x2, w_al, wvt, 512)
    cum, sgm = _cum_call(fft, b_f.reshape(FOX_HEADS, 1))
    qa, ka = _augment_call(qf, kf, cum.reshape(FOX_HEADS, s_len, 1), 1024)
    o_ft, lse_f = _fox_fwd_call(qa, ka, vat)
    bucket = jnp.asarray(_t5_bucket_table())
    bias, bias0 = _swa_bias_call(rel_bias, bucket)
    sink_v = sink.reshape(SWA_HEADS)
    o_s, lse_s = _swa_fwd_call(qs, ks, vs, bias, bias0, sink_v)

    (dh, do_f, dfz, do_s, dsz, dl_f, dl_s, dwo, dg, db, loss_part) = _post_call(
        o_ft.reshape(FOX_HEADS * HEAD_DIM, s_len), fz, o_s, sz, x2, tgt, wo_full, ln_g, ln_b,
        jnp.asarray(_head_selector()), 256)

    dkf, dvf, dqf, dcq, dck = _fox_bwd_call(ka, vf, qa, do_f, lse_f, dl_f.reshape(FOX_HEADS, 1, s_len))
    dfft, dbf = _cum_bwd_call(dcq.reshape(FOX_HEADS, s_len), dck.reshape(FOX_HEADS, s_len), sgm)
    dqs, dks, dvs, dbias, dsink = _swa_bwd_call(qs, ks, vs, do_s, lse_s, dl_s, bias, bias0, sink_v)
    drel = _swa_bias_bwd_call(dbias, bucket)

    dx, dproj = _dx_call(dh, dqf, dkf, dvf, dfz, dqs, dks, dvs, dsz, dfft, w_al, 256)
    dw_al = _dw_call(x2, dproj, 1024)

    dw_full = jnp.concatenate([dw_al[:, :O_FF0], dw_al[:, A_FF:A_FF + (O_FF1 - O_FF0)], dw_al[:, O_FF0:A_FF]], axis=1)
    dw_blocks = jnp.transpose(dw_full.reshape(D_MODEL, N_DEV, shard), (1, 0, 2))
    dwo_blocks = dwo.reshape(N_DEV, D_MODEL // N_DEV, D_MODEL)
    small = _pack_small(dg, db, drel[:, 0:8], dbf[:, 0].reshape(1, 8), dsink[:, 0].reshape(1, 8))
    loss_slot = np.zeros((64, 128), bool)
    loss_slot[49, 0] = True
    small = jnp.where(jnp.asarray(loss_slot), loss_part[0, 0], small)
    small_blocks = jnp.broadcast_to(small[None], (N_DEV,) + small.shape)
    r_in, r_o, r_small = _exchange_call([dw_blocks, dwo_blocks, small_blocks])

    g_win, d_win, nm_win, nv_win = _adam_call(r_in, w_in[0], m_w_in[0], v_w_in[0], 128, "adam_w_in")
    g_wo, d_wo, nm_wo, nv_wo = _adam_call(r_o, w_o[0], m_w_o[0], v_w_o[0], 64, "adam_w_o")
    p_w = _pack_small(ln_g, ln_b, rel_bias, b_f, sink)
    p_m = _pack_small(m_ln_g, m_ln_b, m_rel_bias, m_b_f, m_sink)
    p_v = _pack_small(v_ln_g, v_ln_b, v_rel_bias, v_b_f, v_sink)
    g_p, d_p, nm_p, nv_p = _adam_call(r_small, p_w, p_m, p_v, 64, "adam_small")

    loss = g_p[49, 0]
    g_lng, g_lnb, g_rel, g_bf, g_sink = _unpack_small(g_p)
    d_lng, d_lnb, d_rel, d_bf, d_sink = _unpack_small(d_p)
    m_lng, m_lnb, m_rel, m_bf, m_sk = _unpack_small(nm_p)
    v_lng, v_lnb, v_rel, v_bf, v_sk = _unpack_small(nv_p)
    return (loss, dx[None], g_win[None], g_bf, g_rel, g_sink, g_wo[None], g_lng, g_lnb,
            d_win[None], d_bf, d_rel, d_sink, d_wo[None], d_lng, d_lnb,
            nm_win[None], m_bf, m_rel, m_sk, nm_wo[None], m_lng, m_lnb,
            nv_win[None], v_bf, v_rel, v_sk, nv_wo[None], v_lng, v_lnb)
```

```python
import functools
import math

import numpy as np
import jax
import jax.numpy as jnp
from jax import lax
from jax.experimental import pallas as pl
from jax.experimental.pallas import tpu as pltpu

f32 = jnp.float32
bf16 = jnp.bfloat16

D_MODEL = 1024
HEAD_DIM = 64
FOX_HEADS = 8
SWA_HEADS = 8
SWA_KV_HEADS = 2
SWA_GROUP = 4
BLOCK = 128
NUM_BUCKETS = 32
MAX_DISTANCE = 128
LN_EPS = 1e-5
NEG_INF = -1e30
ALPHA = 2.0 ** 0.25
SCALE = 1.0 / math.sqrt(HEAD_DIM)
D_IN = 3336

ADAM_LR = 0.001
ADAM_B1 = 0.9
ADAM_B2 = 0.999
ADAM_EPS = 1e-08
ADAM_WD = 0.01
ADAM_STEP = 10

N_DEV = 8
A_FQ, A_FK, A_FV, A_FZ, A_SQ, A_SK, A_SV, A_SZ, A_FF, A_W = 0, 512, 1024, 1536, 2048, 2560, 2688, 2816, 3328, 3456
O_FF0, O_FF1 = 1536, 1544

VMEM_LIMIT = 48 * 1024 * 1024
HIGHEST = lax.Precision.HIGHEST
NT = (((1,), (1,)), ((), ()))
TN = (((0,), (0,)), ((), ()))
MESH = pl.DeviceIdType.MESH
RELS = [(0, 0, 1), (0, 1, 0), (0, 1, 1), (1, 0, 0), (1, 0, 1), (1, 1, 0), (1, 1, 1)]


def _params(sem=None):
    return pltpu.CompilerParams(dimension_semantics=sem, vmem_limit_bytes=VMEM_LIMIT)


def _sds(shape, dtype):
    return jax.ShapeDtypeStruct(shape, dtype)


def _t5_bucket_table():
    qi = np.arange(BLOCK)[:, None]
    kj = np.arange(2 * BLOCK)[None, :]
    rel = qi + BLOCK - kj
    band = (rel >= 0) & (rel < BLOCK)
    relc = np.maximum(rel, 0)
    max_exact = NUM_BUCKETS // 2
    relf = np.maximum(relc, 1).astype(np.float32)
    large = max_exact + (np.log(relf / np.float32(max_exact)) / np.float32(math.log(MAX_DISTANCE / max_exact))
                         * np.float32(NUM_BUCKETS - max_exact)).astype(np.int32)
    large = np.minimum(large, NUM_BUCKETS - 1)
    bucket = np.where(relc < max_exact, relc, large).astype(np.int32)
    bucket = np.where(band, bucket, -1).astype(np.int32)
    return bucket


def _mesh_pos():
    return lax.axis_index("x"), lax.axis_index("y"), lax.axis_index("c")


def _dev_index(p):
    return 4 * p[0] + 2 * p[1] + p[2]


def _gather_call(xs):
    n = len(xs)

    def body(*refs):
        x_refs, o_refs = refs[:n], refs[n:2 * n]
        send_sems, recv_sems, local_sems = refs[2 * n:]
        x, y, c = _mesh_pos()
        me, sib = (x, y, c), (x, y, 1 - c)
        chips = [(1 - x, y), (x, 1 - y), (1 - x, 1 - y)]

        def copy(a, k, block, to, src=None):
            slot = o_refs[a].at[_dev_index(block)]
            return pltpu.make_async_remote_copy(
                src_ref=slot if src is None else src, dst_ref=slot,
                send_sem=send_sems.at[a * 7 + k], recv_sem=recv_sems.at[a * 7 + k],
                device_id=to, device_id_type=MESH)

        mine = [pltpu.make_async_copy(x_refs[a], o_refs[a].at[_dev_index(me)], local_sems.at[a]) for a in range(n)]
        for cp in mine:
            cp.start()
        first = []
        for a in range(n):
            first.append(copy(a, 0, me, sib, src=x_refs[a]))
            first += [copy(a, 1 + j, me, (*chip, c), src=x_refs[a]) for j, chip in enumerate(chips)]
        for cp in first:
            cp.start()
        passed = []
        for j, chip in enumerate(chips):
            for a in range(n):
                copy(a, 1 + j, (*chip, c), me).wait_recv()
                fwd = copy(a, 4 + j, (*chip, c), sib)
                fwd.start()
                passed.append(fwd)
        for a in range(n):
            copy(a, 0, sib, me).wait_recv()
            for j, chip in enumerate(chips):
                copy(a, 4 + j, (*chip, 1 - c), me).wait_recv()
        for cp in first + passed:
            cp.wait_send()
        for cp in mine:
            cp.wait()

    any_spec = pl.BlockSpec(memory_space=pl.ANY)
    return pl.pallas_call(
        body,
        name="gather_weights",
        out_shape=[_sds((N_DEV,) + a.shape, a.dtype) for a in xs],
        in_specs=[any_spec] * n,
        out_specs=[any_spec] * n,
        scratch_shapes=[pltpu.SemaphoreType.DMA((7 * n,)), pltpu.SemaphoreType.DMA((7 * n,)),
                        pltpu.SemaphoreType.DMA((n,))],
    )(*xs)


def _exchange_call(bs):
    n = len(bs)

    def body(*refs):
        b_refs, r_refs = refs[:n], refs[n:2 * n]
        send_sems, recv_sems, local_sems = refs[2 * n:]
        x, y, c = _mesh_pos()
        me_idx = _dev_index((x, y, c))
        mine = [pltpu.make_async_copy(b_refs[a].at[me_idx], r_refs[a].at[me_idx], local_sems.at[a]) for a in range(n)]
        for cp in mine:
            cp.start()
        sent = []
        for k, r in enumerate(RELS):
            peer = ((1 - x) if r[0] else x, (1 - y) if r[1] else y, (1 - c) if r[2] else c)
            pidx = _dev_index(peer)
            for a in range(n):
                out = pltpu.make_async_remote_copy(
                    src_ref=b_refs[a].at[pidx], dst_ref=r_refs[a].at[me_idx],
                    send_sem=send_sems.at[a * 7 + k], recv_sem=recv_sems.at[a * 7 + k],
                    device_id=peer, device_id_type=MESH)
                out.start()
                inc = pltpu.make_async_remote_copy(
                    src_ref=b_refs[a].at[pidx], dst_ref=r_refs[a].at[pidx],
                    send_sem=send_sems.at[a * 7 + k], recv_sem=recv_sems.at[a * 7 + k],
                    device_id=peer, device_id_type=MESH)
                sent.append((out, inc))
        for out, inc in sent:
            inc.wait_recv()
        for out, inc in sent:
            out.wait_send()
        for cp in mine:
            cp.wait()

    any_spec = pl.BlockSpec(memory_space=pl.ANY)
    return pl.pallas_call(
        body,
        name="exchange_grads",
        out_shape=[_sds(b.shape, b.dtype) for b in bs],
        in_specs=[any_spec] * n,
        out_specs=[any_spec] * n,
        scratch_shapes=[pltpu.SemaphoreType.DMA((7 * n,)), pltpu.SemaphoreType.DMA((7 * n,)),
                        pltpu.SemaphoreType.DMA((n,))],
    )(*bs)


def _proj_call(x2, w_t, tm):
    s_len = x2.shape[0]

    def body(x_ref, w_ref, qf_ref, kf_ref, vf_ref, fz_ref, qs_ref, ks_ref, vs_ref, sz_ref, fft_ref, vat_ref):
        xb = x_ref[...].astype(bf16)
        vt = lax.dot_general(w_ref[A_FV:A_FV + 512, :], xb, NT, preferred_element_type=f32)
        ones_row = jnp.where(lax.broadcasted_iota(jnp.int32, (HEAD_DIM, tm), 0) == 0, 1.0, 0.0).astype(bf16)
        for h in range(FOX_HEADS):
            vat_ref[h, 0:HEAD_DIM, :] = vt[h * HEAD_DIM:(h + 1) * HEAD_DIM, :].astype(bf16)
            vat_ref[h, HEAD_DIM:2 * HEAD_DIM, :] = ones_row

        def seg(off, width):
            return lax.dot_general(xb, w_ref[off:off + width, :], NT, preferred_element_type=f32)

        def put_heads(ref, acc, nheads):
            for h in range(nheads):
                ref[h] = acc[:, h * HEAD_DIM:(h + 1) * HEAD_DIM].astype(bf16)

        put_heads(qf_ref, seg(A_FQ, 512) * SCALE, FOX_HEADS)
        put_heads(kf_ref, seg(A_FK, 512), FOX_HEADS)
        put_heads(vf_ref, seg(A_FV, 512), FOX_HEADS)
        fz_ref[...] = seg(A_FZ, 512)
        put_heads(qs_ref, seg(A_SQ, 512) * SCALE, SWA_HEADS)
        put_heads(ks_ref, seg(A_SK, 128), SWA_KV_HEADS)
        put_heads(vs_ref, seg(A_SV, 128), SWA_KV_HEADS)
        sz_ref[...] = seg(A_SZ, 512)
        fft_ref[...] = seg(A_FF, 128).T[:FOX_HEADS, :]

    def heads(nh):
        return pl.BlockSpec((nh, tm, HEAD_DIM), lambda i: (0, i, 0))

    wide = pl.BlockSpec((tm, 512), lambda i: (i, 0))
    return pl.pallas_call(
        body,
        name="proj_fwd",
        grid=(s_len // tm,),
        in_specs=[pl.BlockSpec((tm, D_MODEL), lambda i: (i, 0)), pl.BlockSpec((A_W, D_MODEL), lambda i: (0, 0))],
        out_specs=[heads(8), heads(8), heads(8), wide, heads(8), heads(2), heads(2), wide,
                   pl.BlockSpec((FOX_HEADS, tm), lambda i: (0, i)),
                   pl.BlockSpec((FOX_HEADS, 2 * HEAD_DIM, tm), lambda i: (0, 0, i))],
        out_shape=[_sds((8, s_len, HEAD_DIM), bf16)] * 3 + [_sds((s_len, 512), f32), _sds((8, s_len, HEAD_DIM), bf16),
                   _sds((2, s_len, HEAD_DIM), bf16), _sds((2, s_len, HEAD_DIM), bf16), _sds((s_len, 512), f32),
                   _sds((FOX_HEADS, s_len), f32), _sds((FOX_HEADS, 2 * HEAD_DIM, s_len), bf16)],
        compiler_params=_params(("arbitrary",)),
    )(x2, w_t)


AUG = 2 * HEAD_DIM


def _augment_call(q, k, cum_col, tm):
    nh, s_len, _ = q.shape

    def body(q_ref, k_ref, c_ref, qa_ref, ka_ref):
        c = c_ref[0]
        hi = c.astype(bf16).astype(f32)
        r1 = c - hi
        mid = r1.astype(bf16).astype(f32)
        lo = (r1 - mid).astype(bf16).astype(f32)
        lane = lax.broadcasted_iota(jnp.int32, (tm, HEAD_DIM), 1)
        q_tail = jnp.where(lane == 0, hi, jnp.where(lane == 1, mid, jnp.where(lane == 2, lo,
                           jnp.where(lane < 6, 1.0, 0.0))))
        k_tail = jnp.where(lane < 3, 1.0, jnp.where(lane == 3, -hi, jnp.where(lane == 4, -mid,
                           jnp.where(lane == 5, -lo, 0.0))))
        qa_ref[0] = jnp.concatenate([q_ref[0], q_tail.astype(bf16)], axis=1)
        ka_ref[0] = jnp.concatenate([k_ref[0], k_tail.astype(bf16)], axis=1)

    tile = pl.BlockSpec((1, tm, HEAD_DIM), lambda h, i: (h, i, 0))
    wide = pl.BlockSpec((1, tm, AUG), lambda h, i: (h, i, 0))
    return pl.pallas_call(
        body,
        name="fox_augment",
        grid=(nh, s_len // tm),
        in_specs=[tile, tile, pl.BlockSpec((1, tm, 1), lambda h, i: (h, i, 0))],
        out_specs=[wide, wide],
        out_shape=[_sds((nh, s_len, AUG), bf16)] * 2,
        compiler_params=_params(("arbitrary", "arbitrary")),
    )(q, k, cum_col)


CUM_CHUNK = 512


def _cum_call(fft, bf_col):
    s_len = fft.shape[1]
    ch = CUM_CHUNK

    def body(f_ref, b_ref, cum_ref, sg_ref):
        r = lax.broadcasted_iota(jnp.int32, (ch, ch), 0)
        c = lax.broadcasted_iota(jnp.int32, (ch, ch), 1)
        upper = (r <= c).astype(f32)
        carry = jnp.zeros((FOX_HEADS, 1), f32)
        for n in range(s_len // ch):
            z = f_ref[:, n * ch:(n + 1) * ch] + b_ref[...]
            logf = jnp.minimum(z, 0.0) - jnp.log1p(jnp.exp(-jnp.abs(z)))
            sg_ref[:, n * ch:(n + 1) * ch] = 1.0 / (1.0 + jnp.exp(z))
            cs = jnp.dot(logf, upper, precision=HIGHEST, preferred_element_type=f32) + carry
            cum_ref[:, n * ch:(n + 1) * ch] = cs
            carry = cs[:, ch - 1:ch]

    return pl.pallas_call(
        body,
        name="fox_cum_fwd",
        out_shape=[_sds((FOX_HEADS, s_len), f32)] * 2,
        compiler_params=_params(),
    )(fft, bf_col)


def _cum_bwd_call(dcq, dck, sg):
    s_len = sg.shape[1]
    ch = CUM_CHUNK
    nch = s_len // ch

    def body(q_ref, k_ref, sg_ref, dff_ref, dbf_ref):
        r = lax.broadcasted_iota(jnp.int32, (ch, ch), 0)
        c = lax.broadcasted_iota(jnp.int32, (ch, ch), 1)
        lower = (r >= c).astype(f32)
        dff_ref[...] = jnp.zeros_like(dff_ref)
        carry = jnp.zeros((FOX_HEADS, 1), f32)
        total = jnp.zeros((FOX_HEADS, 1), f32)
        for n in reversed(range(nch)):
            sl = slice(n * ch, (n + 1) * ch)
            dcum = q_ref[:, sl] - k_ref[:, sl]
            rs = jnp.dot(dcum, lower, precision=HIGHEST, preferred_element_type=f32) + carry
            carry = rs[:, 0:1]
            dff = rs * sg_ref[:, sl]
            dff_ref[0:FOX_HEADS, sl] = dff
            total = total + jnp.sum(dff, axis=1, keepdims=True)
        dbf_ref[...] = jnp.broadcast_to(total, (FOX_HEADS, 128))

    return pl.pallas_call(
        body,
        name="fox_cum_bwd",
        out_shape=[_sds((128, s_len), f32), _sds((FOX_HEADS, 128), f32)],
        compiler_params=_params(),
    )(dcq, dck, sg)


FOX_T = 512
LANES = 128


def _causal_keep(t):
    return lax.broadcasted_iota(jnp.int32, (t, t), 0) <= lax.broadcasted_iota(jnp.int32, (t, t), 1)


def _fox_fwd_call(qa, ka, vat):
    nh, s_len, _ = qa.shape
    t = FOX_T

    def body(qa_ref, ka_ref, vat_ref, o_ref, lse_ref, s0, s1, p0, p1, a0, a1, m_ref, acc_ref):
        i = pl.program_id(1)
        qb = qa_ref[0]
        m_ref[...] = jnp.full((1, t), NEG_INF, f32)
        acc_ref[...] = jnp.zeros((AUG, t), f32)
        bufs = ((s0, p0, a0), (s1, p1, a1))

        def scores(j, b, masked):
            kb = ka_ref[0, pl.ds(pl.multiple_of(j * t, t), t), :]
            st = lax.dot_general(kb, qb, NT, preferred_element_type=f32)
            if masked:
                st = jnp.where(_causal_keep(t), st, NEG_INF)
            bufs[b][0][...] = st

        def softmax(b):
            s_ref, p_ref, a_ref = bufs[b]
            for c in range(t // LANES):
                cols = slice(c * LANES, (c + 1) * LANES)
                m_old = m_ref[:, cols]
                m_new = jnp.maximum(m_old, jnp.max(s_ref[:, cols], axis=0, keepdims=True))
                m_ref[:, cols] = m_new
                a_ref[:, cols] = jnp.exp(m_old - m_new)
                p_ref[:, cols] = jnp.exp(s_ref[:, cols] - m_new).astype(bf16)

        def accum(j, b):
            vt = vat_ref[0, :, pl.ds(pl.multiple_of(j * t, t), t)]
            acc_ref[...] = bufs[b][2][...] * acc_ref[...] + jnp.dot(vt, bufs[b][1][...], preferred_element_type=f32)

        def key_tile(n):
            return jnp.where(n == 0, i, n - 1)

        def step(n, b):
            scores(n - 1, b, False)
            softmax(1 - b)
            accum(key_tile(n - 2), b)

        @pl.when(i == 0)
        def _():
            scores(0, 0, True)
            softmax(0)
            accum(0, 0)

        @pl.when(i >= 1)
        def _():
            scores(i, 0, True)
            scores(0, 1, False)
            softmax(0)

        def two_steps(d, _):
            n = 2 + 2 * d
            step(n, 0)
            step(n + 1, 1)
            return 0

        lax.fori_loop(0, (i - 1) // 2, two_steps, 0)

        @pl.when((i >= 2) & (i % 2 == 0))
        def _():
            step(i, 0)
            softmax(0)
            accum(i - 2, 1)
            accum(i - 1, 0)

        @pl.when(i % 2 == 1)
        def _():
            softmax(1)
            accum(key_tile(i - 1), 0)
            accum(i - 1, 1)

        l = acc_ref[HEAD_DIM:HEAD_DIM + 1, :]
        o_ref[0] = acc_ref[0:HEAD_DIM, :] / l
        lse_ref[0] = m_ref[...] + jnp.log(l)

    return pl.pallas_call(
        body,
        name="fox_fwd",
        grid=(nh, s_len // t),
        in_specs=[pl.BlockSpec((1, t, AUG), lambda h, i: (h, i, 0)),
                  pl.BlockSpec((1, s_len, AUG), lambda h, i: (h, 0, 0)),
                  pl.BlockSpec((1, AUG, s_len), lambda h, i: (h, 0, 0))],
        out_specs=[pl.BlockSpec((1, HEAD_DIM, t), lambda h, i: (h, 0, i)),
                   pl.BlockSpec((1, 1, t), lambda h, i: (h, 0, i))],
        out_shape=[_sds((nh, HEAD_DIM, s_len), f32), _sds((nh, 1, s_len), f32)],
        scratch_shapes=[pltpu.VMEM((t, t), f32), pltpu.VMEM((t, t), f32), pltpu.VMEM((t, t), bf16),
                        pltpu.VMEM((t, t), bf16), pltpu.VMEM((1, t), f32), pltpu.VMEM((1, t), f32),
                        pltpu.VMEM((1, t), f32), pltpu.VMEM((AUG, t), f32)],
        compiler_params=_params(("arbitrary", "arbitrary")),
    )(qa, ka, vat)


SWA_TS = 512


def _swa_bias_call(rel_bias, bucket):
    def body(rb_ref, bk_ref, b_ref, b0_ref):
        bk = bk_ref[...]
        col = lax.broadcasted_iota(jnp.int32, (BLOCK, 2 * BLOCK), 1)
        for h in range(SWA_HEADS):
            acc = jnp.full((BLOCK, 2 * BLOCK), NEG_INF, f32)
            for b in range(NUM_BUCKETS):
                acc = jnp.where(bk == b, rb_ref[b, h], acc)
            g, hh = divmod(h, SWA_GROUP)
            b_ref[g, hh * BLOCK:(hh + 1) * BLOCK, :] = acc
            b0_ref[g, hh * BLOCK:(hh + 1) * BLOCK, :] = jnp.where(col < BLOCK, NEG_INF, acc)

    return pl.pallas_call(
        body,
        name="swa_bias",
        in_specs=[pl.BlockSpec(memory_space=pltpu.SMEM), pl.BlockSpec(memory_space=pltpu.VMEM)],
        out_shape=[_sds((SWA_KV_HEADS, SWA_GROUP * BLOCK, 2 * BLOCK), f32)] * 2,
        compiler_params=_params(),
    )(rel_bias, bucket)


def _swa_bias_bwd_call(dbias, bucket):
    def body(d_ref, bk_ref, o_ref):
        bk = bk_ref[...]
        row = lax.broadcasted_iota(jnp.int32, (NUM_BUCKETS, 128), 0)
        col = lax.broadcasted_iota(jnp.int32, (NUM_BUCKETS, 128), 1)
        out = jnp.zeros((NUM_BUCKETS, 128), f32)
        for h in range(SWA_HEADS):
            g, hh = divmod(h, SWA_GROUP)
            d = d_ref[g, hh * BLOCK:(hh + 1) * BLOCK, :]
            for b in range(NUM_BUCKETS):
                val = jnp.sum(jnp.sum(jnp.where(bk == b, d, 0.0), axis=1, keepdims=True), axis=0, keepdims=True)
                out = jnp.where((row == b) & (col == h), val, out)
        o_ref[...] = out

    return pl.pallas_call(
        body,
        name="swa_bias_bwd",
        out_shape=_sds((NUM_BUCKETS, 128), f32),
        compiler_params=_params(),
    )(dbias, bucket)


def _swa_specs(ts):
    nb = ts // BLOCK
    qspec = pl.BlockSpec((SWA_HEADS, ts, HEAD_DIM), lambda n: (0, n, 0))
    cur = pl.BlockSpec((SWA_KV_HEADS, ts, HEAD_DIM), lambda n: (0, n, 0))
    prev = pl.BlockSpec((SWA_KV_HEADS, BLOCK, HEAD_DIM), lambda n: (0, jnp.maximum(n * nb - 1, 0), 0))
    return qspec, cur, prev


def _sink_col(sink_ref, g):
    return jnp.concatenate([jnp.full((BLOCK, 1), sink_ref[g * SWA_GROUP + hh], f32) for hh in range(SWA_GROUP)], axis=0)


def _swa_fwd_call(q, k, v, bias, bias0, sink):
    s_len = q.shape[1]
    ts = SWA_TS
    nb = ts // BLOCK

    def body(q_ref, kc_ref, kp_ref, vc_ref, vp_ref, b_ref, b0_ref, sink_ref, o_ref, lse_ref):
        first = pl.program_id(0) == 0
        for g in range(SWA_KV_HEADS):
            kall = jnp.concatenate([kp_ref[g], kc_ref[g]], axis=0)
            vall = jnp.concatenate([vp_ref[g], vc_ref[g]], axis=0)
            sink_c = _sink_col(sink_ref, g)
            for b in range(nb):
                rows = slice(b * BLOCK, (b + 1) * BLOCK)
                qg = jnp.concatenate([q_ref[g * SWA_GROUP + hh, rows, :] for hh in range(SWA_GROUP)], axis=0)
                kcat = kall[b * BLOCK:(b + 2) * BLOCK]
                vcat = vall[b * BLOCK:(b + 2) * BLOCK]
                bias_b = b_ref[g]
                if b == 0:
                    bias_b = jnp.where(first, b0_ref[g], bias_b)
                s = lax.dot_general(qg, kcat, NT, preferred_element_type=f32) + bias_b
                m = jnp.maximum(jnp.max(s, axis=1, keepdims=True), sink_c)
                p = jnp.exp(s - m)
                l = jnp.sum(p, axis=1, keepdims=True) + jnp.exp(sink_c - m)
                o = jnp.dot(p.astype(bf16), vcat, preferred_element_type=f32) / l
                lse = m + jnp.log(l)
                for hh in range(SWA_GROUP):
                    o_ref[g * SWA_GROUP + hh, rows, :] = o[hh * BLOCK:(hh + 1) * BLOCK]
                    lse_ref[g * SWA_GROUP + hh, rows, :] = lse[hh * BLOCK:(hh + 1) * BLOCK]

    qspec, cur, prev = _swa_specs(ts)
    bspec = pl.BlockSpec((SWA_KV_HEADS, SWA_GROUP * BLOCK, 2 * BLOCK), lambda n: (0, 0, 0))
    return pl.pallas_call(
        body,
        name="swa_fwd",
        grid=(s_len // ts,),
        in_specs=[qspec, cur, prev, cur, prev, bspec, bspec, pl.BlockSpec(memory_space=pltpu.SMEM)],
        out_specs=[pl.BlockSpec((SWA_HEADS, ts, HEAD_DIM), lambda n: (0, n, 0)),
                   pl.BlockSpec((SWA_HEADS, ts, 1), lambda n: (0, n, 0))],
        out_shape=[_sds((SWA_HEADS, s_len, HEAD_DIM), f32), _sds((SWA_HEADS, s_len, 1), f32)],
        compiler_params=_params(("arbitrary",)),
    )(q, k, k, v, v, bias, bias0, sink)


def _head_selector():
    sel = np.zeros((512, 128), np.float32)
    for h in range(8):
        sel[h * HEAD_DIM:(h + 1) * HEAD_DIM, h] = 1.0
    return sel


def _post_call(of, fz, osw, sz, x2, tgt, wo, ln_g, ln_b, sel, tm):
    s_len = x2.shape[0]

    def body(of_ref, fz_ref, os_ref, sz_ref, x_ref, t_ref, wo_ref, g_ref, b_ref, sel_ref,
             dh_ref, dof_ref, dfz_ref, dos_ref, dsz_ref, dlf_ref, dls_ref, dwo_ref, dg_ref, db_ref, loss_ref):
        n = pl.program_id(0)

        @pl.when(n == 0)
        def _():
            dwo_ref[...] = jnp.zeros_like(dwo_ref)
            dg_ref[...] = jnp.zeros_like(dg_ref)
            db_ref[...] = jnp.zeros_like(db_ref)
            loss_ref[...] = jnp.zeros_like(loss_ref)

        o_f = of_ref[...].T
        o_s = jnp.concatenate([os_ref[h] for h in range(SWA_HEADS)], axis=1)
        fz = fz_ref[...]
        sz = sz_ref[...]
        sg_f = jax.nn.sigmoid(fz)
        sg_s = jax.nn.sigmoid(sz)
        silu_f = fz * sg_f
        silu_s = sz * sg_s
        mixed = jnp.concatenate([o_f * silu_f, o_s * silu_s], axis=1).astype(bf16)
        y = jnp.dot(mixed, wo_ref[...], preferred_element_type=f32)
        h = ALPHA * x_ref[...] + y
        mu = jnp.mean(h, axis=1, keepdims=True)
        hc = h - mu
        var = jnp.mean(hc * hc, axis=1, keepdims=True)
        rstd = lax.rsqrt(var + LN_EPS)
        xhat = hc * rstd
        gam = g_ref[...]
        out = xhat * gam + b_ref[...]
        err = out - t_ref[...]
        tok_loss = jnp.mean(err * err, axis=1, keepdims=True)
        loss_ref[...] += 0.5 * jnp.sum(tok_loss, axis=0, keepdims=True)
        dout = err * (1.0 / D_MODEL)
        dg_ref[...] += jnp.sum(dout * xhat, axis=0, keepdims=True)
        db_ref[...] += jnp.sum(dout, axis=0, keepdims=True)
        dxh = dout * gam
        m1 = jnp.mean(dxh, axis=1, keepdims=True)
        m2 = jnp.mean(dxh * xhat, axis=1, keepdims=True)
        dh = rstd * (dxh - m1 - xhat * m2)
        dh_ref[...] = dh
        dyb = dh.astype(bf16)
        dwo_ref[...] += lax.dot_general(mixed, dyb, TN, preferred_element_type=f32)
        dmix = lax.dot_general(dyb, wo_ref[...], NT, preferred_element_type=f32)
        dm_f = dmix[:, :512]
        dm_s = dmix[:, 512:]
        do_f = dm_f * silu_f
        do_s = dm_s * silu_s
        dfz_ref[...] = (dm_f * o_f * (sg_f * (1.0 + fz * (1.0 - sg_f)))).astype(bf16)
        dsz_ref[...] = (dm_s * o_s * (sg_s * (1.0 + sz * (1.0 - sg_s)))).astype(bf16)
        for hd in range(FOX_HEADS):
            dof_ref[hd] = do_f[:, hd * HEAD_DIM:(hd + 1) * HEAD_DIM].astype(bf16)
            dos_ref[hd] = do_s[:, hd * HEAD_DIM:(hd + 1) * HEAD_DIM].astype(bf16)
        sel_m = sel_ref[...]
        dl_f = jnp.dot(do_f * o_f, sel_m, precision=HIGHEST, preferred_element_type=f32)
        dl_s = jnp.dot(do_s * o_s, sel_m, precision=HIGHEST, preferred_element_type=f32)
        dlf_ref[...] = dl_f.T[:FOX_HEADS, :]
        dls_ref[...] = dl_s

    heads_f32 = pl.BlockSpec((8, tm, HEAD_DIM), lambda n: (0, n, 0))
    half = pl.BlockSpec((tm, 512), lambda n: (n, 0))
    fullw = pl.BlockSpec((tm, D_MODEL), lambda n: (n, 0))
    vec = pl.BlockSpec((1, D_MODEL), lambda n: (0, 0))
    return pl.pallas_call(
        body,
        name="post_fwd_bwd",
        grid=(s_len // tm,),
        in_specs=[pl.BlockSpec((512, tm), lambda n: (0, n)), half, heads_f32, half, fullw, fullw,
                  pl.BlockSpec((D_MODEL, D_MODEL), lambda n: (0, 0)), vec, vec,
                  pl.BlockSpec((512, 128), lambda n: (0, 0))],
        out_specs=[fullw, heads_f32, half, heads_f32, half,
                   pl.BlockSpec((FOX_HEADS, tm), lambda n: (0, n)), pl.BlockSpec((tm, 128), lambda n: (n, 0)),
                   pl.BlockSpec((D_MODEL, D_MODEL), lambda n: (0, 0)), vec, vec,
                   pl.BlockSpec((1, 1), lambda n: (0, 0))],
        out_shape=[_sds((s_len, D_MODEL), f32), _sds((8, s_len, HEAD_DIM), bf16), _sds((s_len, 512), bf16),
                   _sds((8, s_len, HEAD_DIM), bf16), _sds((s_len, 512), bf16),
                   _sds((FOX_HEADS, s_len), f32), _sds((s_len, 128), f32),
                   _sds((D_MODEL, D_MODEL), f32), _sds((1, D_MODEL), f32), _sds((1, D_MODEL), f32),
                   _sds((1, 1), f32)],
        compiler_params=_params(("arbitrary",)),
    )(of, fz, osw, sz, x2, tgt, wo, ln_g, ln_b, sel)


def _fox_bwd_call(ka, v, qa, do, lse_row, dl_row):
    nh, s_len, _ = qa.shape
    t = FOX_T
    nt = s_len // t
    ck_slot = HEAD_DIM + 3
    cq_slot = HEAD_DIM

    def body(ka_ref, v_ref, qa_ref, do_ref, lse_ref, dl_ref,
             dk_ref, dv_ref, dq_ref, dcq_ref, dck_ref, dqt_s, dka_s, dv_s):
        j = pl.program_id(1)

        @pl.when(j == 0)
        def _():
            dqt_s[...] = jnp.zeros_like(dqt_s)

        kb = ka_ref[0]
        vb = v_ref[0]
        kt = kb.astype(f32).T.astype(bf16)
        dka_s[...] = jnp.zeros_like(dka_s)
        dv_s[...] = jnp.zeros_like(dv_s)

        def pair(i, masked):
            off = pl.multiple_of(i * t, t)
            qb = qa_ref[0, pl.ds(off, t), :]
            dob = do_ref[0, pl.ds(off, t), :]
            st = lax.dot_general(kb, qb, NT, preferred_element_type=f32)
            if masked:
                st = jnp.where(_causal_keep(t), st, NEG_INF)
            pt = jnp.exp(st - lse_ref[0, :, pl.ds(off, t)])
            dpt = lax.dot_general(vb, dob, NT, preferred_element_type=f32)
            dsb = (pt * (dpt - dl_ref[0, :, pl.ds(off, t)])).astype(bf16)
            dv_s[...] += jnp.dot(pt.astype(bf16), dob, preferred_element_type=f32)
            dka_s[...] += jnp.dot(dsb, qb, preferred_element_type=f32)
            dqt_s[:, pl.ds(off, t)] += jnp.dot(kt, dsb, preferred_element_type=f32)

        def off_diag(i, _):
            pair(i, False)
            return 0

        pair(j, True)
        lax.fori_loop(j + 1, nt, off_diag, 0)
        dk_ref[0] = dka_s[:, 0:HEAD_DIM].astype(bf16)
        dv_ref[0] = dv_s[...].astype(bf16)
        dck_ref[0] = dka_s[:, ck_slot:ck_slot + 1]

        @pl.when(j == nt - 1)
        def _():
            for cidx in range(nt):
                sl = slice(cidx * t, (cidx + 1) * t)
                dq_ref[0, sl, :] = (dqt_s[0:HEAD_DIM, sl].T * SCALE).astype(bf16)
            dcq_ref[0] = dqt_s[cq_slot:cq_slot + 1, :]

    ktile = pl.BlockSpec((1, t, AUG), lambda h, j: (h, j, 0))
    tile = pl.BlockSpec((1, t, HEAD_DIM), lambda h, j: (h, j, 0))
    qfull = pl.BlockSpec((1, s_len, AUG), lambda h, j: (h, 0, 0))
    full = pl.BlockSpec((1, s_len, HEAD_DIM), lambda h, j: (h, 0, 0))
    rowv = pl.BlockSpec((1, 1, s_len), lambda h, j: (h, 0, 0))
    colt = pl.BlockSpec((1, t, 1), lambda h, j: (h, j, 0))
    return pl.pallas_call(
        body,
        name="fox_bwd",
        grid=(nh, nt),
        in_specs=[ktile, tile, qfull, full, rowv, rowv],
        out_specs=[tile, tile, full, rowv, colt],
        out_shape=[_sds((nh, s_len, HEAD_DIM), bf16)] * 3 + [_sds((nh, 1, s_len), f32), _sds((nh, s_len, 1), f32)],
        scratch_shapes=[pltpu.VMEM((AUG, s_len), f32), pltpu.VMEM((t, AUG), f32), pltpu.VMEM((t, HEAD_DIM), f32)],
        compiler_params=_params(("arbitrary", "arbitrary")),
    )(ka, v, qa, do, lse_row, dl_row)


def _swa_bwd_call(q, k, v, do, lse, dl, bias, bias0, sink):
    s_len = q.shape[1]
    ts = SWA_TS
    nb = ts // BLOCK
    nsteps = s_len // ts

    def body(q_ref, kc_ref, kp_ref, vc_ref, vp_ref, do_ref, lse_ref, dl_ref, b_ref, b0_ref, sink_ref,
             dq_ref, dk_ref, dv_ref, dbias_ref, dsink_ref, dk_s, dv_s, tail_k, tail_v, sk_s):
        n = pl.program_id(0)

        @pl.when(n == 0)
        def _():
            dbias_ref[...] = jnp.zeros_like(dbias_ref)
            sk_s[...] = jnp.zeros_like(sk_s)

        @pl.when(n < nsteps)
        def _():
            first = n == 0
            dk_s[...] = jnp.zeros_like(dk_s)
            dv_s[...] = jnp.zeros_like(dv_s)
            for g in range(SWA_KV_HEADS):
                kall = jnp.concatenate([kp_ref[g], kc_ref[g]], axis=0)
                vall = jnp.concatenate([vp_ref[g], vc_ref[g]], axis=0)
                sink_c = _sink_col(sink_ref, g)
                for b in range(nb):
                    rows = slice(b * BLOCK, (b + 1) * BLOCK)
                    heads = [g * SWA_GROUP + hh for hh in range(SWA_GROUP)]
                    qg = jnp.concatenate([q_ref[h, rows, :] for h in heads], axis=0)
                    dog = jnp.concatenate([do_ref[h, rows, :] for h in heads], axis=0)
                    lse_c = jnp.concatenate([lse_ref[h, rows, :] for h in heads], axis=0)
                    dl_c = jnp.concatenate([dl_ref[rows, h:h + 1] for h in heads], axis=0)
                    kcat = kall[b * BLOCK:(b + 2) * BLOCK]
                    vcat = vall[b * BLOCK:(b + 2) * BLOCK]
                    bias_b = b_ref[g]
                    if b == 0:
                        bias_b = jnp.where(first, b0_ref[g], bias_b)
                    s = lax.dot_general(qg, kcat, NT, preferred_element_type=f32) + bias_b
                    p = jnp.exp(s - lse_c)
                    dp = lax.dot_general(dog, vcat, NT, preferred_element_type=f32)
                    ds = p * (dp - dl_c)
                    dsb = ds.astype(bf16)
                    dqg = jnp.dot(dsb, kcat, preferred_element_type=f32) * SCALE
                    for hh, h in enumerate(heads):
                        dq_ref[h, rows, :] = dqg[hh * BLOCK:(hh + 1) * BLOCK].astype(bf16)
                    win = slice(b * BLOCK, (b + 2) * BLOCK)
                    dk_s[g, win, :] += lax.dot_general(dsb, qg, TN, preferred_element_type=f32)
                    dv_s[g, win, :] += lax.dot_general(p.astype(bf16), dog, TN, preferred_element_type=f32)
                    dbias_ref[g] += ds
                    sk_s[g] += -jnp.exp(sink_c - lse_c) * dl_c

        @pl.when(n > 0)
        def _():
            last = slice(ts - BLOCK, ts)
            for g in range(SWA_KV_HEADS):
                add_k = jnp.where(n < nsteps, dk_s[g, 0:BLOCK, :], 0.0)
                add_v = jnp.where(n < nsteps, dv_s[g, 0:BLOCK, :], 0.0)
                dk_ref[g, 0:ts - BLOCK, :] = tail_k[g, 0:ts - BLOCK, :].astype(bf16)
                dv_ref[g, 0:ts - BLOCK, :] = tail_v[g, 0:ts - BLOCK, :].astype(bf16)
                dk_ref[g, last, :] = (tail_k[g, last, :] + add_k).astype(bf16)
                dv_ref[g, last, :] = (tail_v[g, last, :] + add_v).astype(bf16)

        @pl.when(n < nsteps)
        def _():
            tail_k[...] = dk_s[:, BLOCK:, :]
            tail_v[...] = dv_s[:, BLOCK:, :]

        @pl.when(n == nsteps)
        def _():
            row = lax.broadcasted_iota(jnp.int32, (SWA_HEADS, 128), 0)
            out = jnp.zeros((SWA_HEADS, 128), f32)
            for h in range(SWA_HEADS):
                g, hh = divmod(h, SWA_GROUP)
                val = jnp.sum(sk_s[g, hh * BLOCK:(hh + 1) * BLOCK, :], axis=0, keepdims=True)
                out = jnp.where(row == h, val, out)
            dsink_ref[...] = out

    last_step = nsteps - 1

    def cl(n):
        return jnp.minimum(n, last_step)

    qspec = pl.BlockSpec((SWA_HEADS, ts, HEAD_DIM), lambda n: (0, cl(n), 0))
    cur = pl.BlockSpec((SWA_KV_HEADS, ts, HEAD_DIM), lambda n: (0, cl(n), 0))
    prev = pl.BlockSpec((SWA_KV_HEADS, BLOCK, HEAD_DIM), lambda n: (0, jnp.maximum(cl(n) * nb - 1, 0), 0))
    lsespec = pl.BlockSpec((SWA_HEADS, ts, 1), lambda n: (0, cl(n), 0))
    dlspec = pl.BlockSpec((ts, 128), lambda n: (cl(n), 0))
    bspec = pl.BlockSpec((SWA_KV_HEADS, SWA_GROUP * BLOCK, 2 * BLOCK), lambda n: (0, 0, 0))
    kvout = pl.BlockSpec((SWA_KV_HEADS, ts, HEAD_DIM), lambda n: (0, jnp.maximum(n - 1, 0), 0))
    return pl.pallas_call(
        body,
        name="swa_bwd",
        grid=(nsteps + 1,),
        in_specs=[qspec, cur, prev, cur, prev, qspec, lsespec, dlspec, bspec, bspec,
                  pl.BlockSpec(memory_space=pltpu.SMEM)],
        out_specs=[qspec, kvout, kvout, bspec, pl.BlockSpec((SWA_HEADS, 128), lambda n: (0, 0))],
        out_shape=[_sds((SWA_HEADS, s_len, HEAD_DIM), bf16), _sds((SWA_KV_HEADS, s_len, HEAD_DIM), bf16),
                   _sds((SWA_KV_HEADS, s_len, HEAD_DIM), bf16),
                   _sds((SWA_KV_HEADS, SWA_GROUP * BLOCK, 2 * BLOCK), f32), _sds((SWA_HEADS, 128), f32)],
        scratch_shapes=[pltpu.VMEM((SWA_KV_HEADS, ts + BLOCK, HEAD_DIM), f32),
                        pltpu.VMEM((SWA_KV_HEADS, ts + BLOCK, HEAD_DIM), f32),
                        pltpu.VMEM((SWA_KV_HEADS, ts, HEAD_DIM), f32),
                        pltpu.VMEM((SWA_KV_HEADS, ts, HEAD_DIM), f32),
                        pltpu.VMEM((SWA_KV_HEADS, SWA_GROUP * BLOCK, 1), f32)],
        compiler_params=_params(("arbitrary",)),
    )(q, k, k, v, v, do, lse, dl, bias, bias0, sink)


def _dx_call(dh, dqf, dkf, dvf, dfz, dqs, dks, dvs, dsz, dfft, w_t, tm):
    s_len = dh.shape[0]

    def body(dh_ref, dqf_ref, dkf_ref, dvf_ref, dfz_ref, dqs_ref, dks_ref, dvs_ref, dsz_ref, dfft_ref, w_ref,
             dx_ref, dp_ref):
        def cat(ref, nheads):
            return jnp.concatenate([ref[h] for h in range(nheads)], axis=1)

        dp = jnp.concatenate([cat(dqf_ref, 8), cat(dkf_ref, 8), cat(dvf_ref, 8), dfz_ref[...], cat(dqs_ref, 8),
                              cat(dks_ref, 2), cat(dvs_ref, 2), dsz_ref[...], dfft_ref[...].T.astype(bf16)], axis=1)
        dp_ref[...] = dp
        dx_ref[...] = ALPHA * dh_ref[...] + jnp.dot(dp, w_ref[...], preferred_element_type=f32)

    def heads(nh):
        return pl.BlockSpec((nh, tm, HEAD_DIM), lambda i: (0, i, 0))

    half = pl.BlockSpec((tm, 512), lambda i: (i, 0))
    fullw = pl.BlockSpec((tm, D_MODEL), lambda i: (i, 0))
    return pl.pallas_call(
        body,
        name="dx_bwd",
        grid=(s_len // tm,),
        in_specs=[fullw, heads(8), heads(8), heads(8), half, heads(8), heads(2), heads(2), half,
                  pl.BlockSpec((128, tm), lambda i: (0, i)), pl.BlockSpec((A_W, D_MODEL), lambda i: (0, 0))],
        out_specs=[fullw, pl.BlockSpec((tm, A_W), lambda i: (i, 0))],
        out_shape=[_sds((s_len, D_MODEL), f32), _sds((s_len, A_W), bf16)],
        compiler_params=_params(("arbitrary",)),
    )(dh, dqf, dkf, dvf, dfz, dqs, dks, dvs, dsz, dfft, w_t)


DW_ROWS = 1152


def _dw_call(x2, dproj, tm):
    s_len = x2.shape[0]
    nt = s_len // tm

    def body(x_ref, dp_ref, dw_ref):
        @pl.when(pl.program_id(1) == 0)
        def _():
            dw_ref[...] = jnp.zeros_like(dw_ref)

        dw_ref[...] += lax.dot_general(dp_ref[...], x_ref[...].astype(bf16), TN, preferred_element_type=f32)

    return pl.pallas_call(
        body,
        name="dw_in_bwd",
        grid=(A_W // DW_ROWS, nt),
        in_specs=[pl.BlockSpec((tm, D_MODEL), lambda c, i: (i, 0)), pl.BlockSpec((tm, DW_ROWS), lambda c, i: (i, c))],
        out_specs=pl.BlockSpec((DW_ROWS, D_MODEL), lambda c, i: (c, 0)),
        out_shape=_sds((A_W, D_MODEL), f32),
        compiler_params=_params(("arbitrary", "arbitrary")),
    )(x2, dproj)


def _adam_call(recv, w, m, v, tc, name):
    rows, cols = w.shape

    def body(r_ref, w_ref, m_ref, v_ref, g_ref, d_ref, mo_ref, vo_ref):
        g = r_ref[0].astype(f32)
        for p in range(1, N_DEV):
            g = g + r_ref[p].astype(f32)
        mn = ADAM_B1 * m_ref[...] + (1.0 - ADAM_B1) * g
        vn = ADAM_B2 * v_ref[...] + (1.0 - ADAM_B2) * (g * g)
        m_hat = mn / (1.0 - ADAM_B1 ** ADAM_STEP)
        v_hat = vn / (1.0 - ADAM_B2 ** ADAM_STEP)
        g_ref[...] = g
        d_ref[...] = -ADAM_LR * (m_hat / (jnp.sqrt(v_hat) + ADAM_EPS) + ADAM_WD * w_ref[...])
        mo_ref[...] = mn
        vo_ref[...] = vn

    blk = pl.BlockSpec((rows, tc), lambda i: (0, i))
    return pl.pallas_call(
        body,
        name=name,
        grid=(cols // tc,),
        in_specs=[pl.BlockSpec((N_DEV, rows, tc), lambda i: (0, 0, i)), blk, blk, blk],
        out_specs=[blk] * 4,
        out_shape=[_sds((rows, cols), f32)] * 4,
        compiler_params=_params(("arbitrary",)),
    )(recv, w, m, v)


def _pad_cols(a, width=128):
    return jnp.pad(a, ((0, 0), (0, width - a.shape[1])))


def _pack_small(ln_g, ln_b, rel, b_f, sink):
    return jnp.concatenate([
        ln_g.reshape(8, 128), ln_b.reshape(8, 128), _pad_cols(rel),
        jnp.pad(_pad_cols(b_f), ((0, 7), (0, 0))), jnp.pad(_pad_cols(sink), ((0, 7), (0, 0)))], axis=0)


def _unpack_small(p):
    return (p[0:8].reshape(1, D_MODEL), p[8:16].reshape(1, D_MODEL), p[16:48, 0:8], p[48:49, 0:8], p[56:57, 0:8])


def kernel(x, w_in, b_f, rel_bias, sink, w_o, ln_g, ln_b, loss_target, m_w_in, m_b_f, m_rel_bias, m_sink, m_w_o, m_ln_g, m_ln_b, v_w_in, v_b_f, v_rel_bias, v_sink, v_w_o, v_ln_g, v_ln_b):
    x2 = x[0]
    tgt = loss_target[0]
    s_len = x2.shape[0]
    shard = w_in.shape[2]

    w_in_t = jnp.transpose(w_in[0])
    g_in, g_o = _gather_call([w_in_t.astype(bf16), w_o[0].astype(bf16)])
    wt_full = g_in.reshape(N_DEV * shard, D_MODEL)
    w_t = jnp.concatenate([wt_full[:O_FF0], wt_full[O_FF1:], wt_full[O_FF0:O_FF1],
                           jnp.zeros((A_W - D_IN, D_MODEL), bf16)], axis=0)
    wo_full = g_o.reshape(D_MODEL, D_MODEL)

    qf, kf, vf, fz, qs, ks, vs, sz, fft, vat = _proj_call(x2, w_t, 512)
    cum, sgm = _cum_call(fft, b_f.reshape(FOX_HEADS, 1))
    qa, ka = _augment_call(qf, kf, cum.reshape(FOX_HEADS, s_len, 1), 1024)
    o_ft, lse_f = _fox_fwd_call(qa, ka, vat)
    bucket = jnp.asarray(_t5_bucket_table())
    bias, bias0 = _swa_bias_call(rel_bias, bucket)
    sink_v = sink.reshape(SWA_HEADS)
    o_s, lse_s = _swa_fwd_call(qs, ks, vs, bias, bias0, sink_v)

    (dh, do_f, dfz, do_s, dsz, dl_f, dl_s, dwo, dg, db, loss_part) = _post_call(
        o_ft.reshape(FOX_HEADS * HEAD_DIM, s_len), fz, o_s, sz, x2, tgt, wo_full, ln_g, ln_b,
        jnp.asarray(_head_selector()), 256)

    dkf, dvf, dqf, dcq, dck = _fox_bwd_call(ka, vf, qa, do_f, lse_f, dl_f.reshape(FOX_HEADS, 1, s_len))
    dfft, dbf = _cum_bwd_call(dcq.reshape(FOX_HEADS, s_len), dck.reshape(FOX_HEADS, s_len), sgm)
    dqs, dks, dvs, dbias, dsink = _swa_bwd_call(qs, ks, vs, do_s, lse_s, dl_s, bias, bias0, sink_v)
    drel = _swa_bias_bwd_call(dbias, bucket)

    dx, dproj = _dx_call(dh, dqf, dkf, dvf, dfz, dqs, dks, dvs, dsz, dfft, w_t, 256)
    dw_t = _dw_call(x2, dproj, 1024)

    dwt_full = jnp.concatenate([dw_t[:O_FF0], dw_t[A_FF:A_FF + (O_FF1 - O_FF0)], dw_t[O_FF0:A_FF]], axis=0)
    dw_blocks = dwt_full.reshape(N_DEV, shard, D_MODEL).astype(bf16)
    dwo_blocks = dwo.reshape(N_DEV, D_MODEL // N_DEV, D_MODEL).astype(bf16)
    small = _pack_small(dg, db, drel[:, 0:8], dbf[:, 0].reshape(1, 8), dsink[:, 0].reshape(1, 8))
    loss_slot = np.zeros((64, 128), bool)
    loss_slot[49, 0] = True
    small = jnp.where(jnp.asarray(loss_slot), loss_part[0, 0], small)
    small_blocks = jnp.broadcast_to(small[None], (N_DEV,) + small.shape)
    r_in, r_o, r_small = _exchange_call([dw_blocks, dwo_blocks, small_blocks])

    win_t = [jnp.transpose(a) for a in _adam_call(
        r_in, w_in_t, jnp.transpose(m_w_in[0]), jnp.transpose(v_w_in[0]), 256, "adam_w_in")]
    g_win, d_win, nm_win, nv_win = win_t
    g_wo, d_wo, nm_wo, nv_wo = _adam_call(r_o, w_o[0], m_w_o[0], v_w_o[0], 256, "adam_w_o")
    p_w = _pack_small(ln_g, ln_b, rel_bias, b_f, sink)
    p_m = _pack_small(m_ln_g, m_ln_b, m_rel_bias, m_b_f, m_sink)
    p_v = _pack_small(v_ln_g, v_ln_b, v_rel_bias, v_b_f, v_sink)
    g_p, d_p, nm_p, nv_p = _adam_call(r_small, p_w, p_m, p_v, 128, "adam_small")

    loss = g_p[49, 0]
    g_lng, g_lnb, g_rel, g_bf, g_sink = _unpack_small(g_p)
    d_lng, d_lnb, d_rel, d_bf, d_sink = _unpack_small(d_p)
    m_lng, m_lnb, m_rel, m_bf, m_sk = _unpack_small(nm_p)
    v_lng, v_lnb, v_rel, v_bf, v_sk = _unpack_small(nv_p)
    return (loss, dx[None], g_win[None], g_bf, g_rel, g_sink, g_wo[None], g_lng, g_lnb,
            d_win[None], d_bf, d_rel, d_sink, d_wo[None], d_lng, d_lnb,
            nm_win[None], m_bf, m_rel, m_sk, nm_wo[None], m_lng, m_lnb,
            nv_win[None], v_bf, v_rel, v_sk, nv_wo[None], v_lng, v_lnb)
```

```python
import functools
import math

import numpy as np
import jax
import jax.numpy as jnp
from jax import lax
from jax.experimental import pallas as pl
from jax.experimental.pallas import tpu as pltpu

f32 = jnp.float32
bf16 = jnp.bfloat16

D_MODEL = 1024
HEAD_DIM = 64
FOX_HEADS = 8
SWA_HEADS = 8
SWA_KV_HEADS = 2
SWA_GROUP = 4
BLOCK = 128
NUM_BUCKETS = 32
MAX_DISTANCE = 128
LN_EPS = 1e-5
NEG_INF = -1e30
ALPHA = 2.0 ** 0.25
SCALE = 1.0 / math.sqrt(HEAD_DIM)
D_IN = 3336

ADAM_LR = 0.001
ADAM_B1 = 0.9
ADAM_B2 = 0.999
ADAM_EPS = 1e-08
ADAM_WD = 0.01
ADAM_STEP = 10

N_DEV = 8
A_FQ, A_FK, A_FV, A_FZ, A_SQ, A_SK, A_SV, A_SZ, A_FF, A_W = 0, 512, 1024, 1536, 2048, 2560, 2688, 2816, 3328, 3456
O_FF0, O_FF1 = 1536, 1544

VMEM_LIMIT = 48 * 1024 * 1024
HIGHEST = lax.Precision.HIGHEST
NT = (((1,), (1,)), ((), ()))
TN = (((0,), (0,)), ((), ()))
MESH = pl.DeviceIdType.MESH
RELS = [(0, 0, 1), (0, 1, 0), (0, 1, 1), (1, 0, 0), (1, 0, 1), (1, 1, 0), (1, 1, 1)]


def _params(sem=None):
    return pltpu.CompilerParams(dimension_semantics=sem, vmem_limit_bytes=VMEM_LIMIT)


def _sds(shape, dtype):
    return jax.ShapeDtypeStruct(shape, dtype)


def _t5_bucket_table():
    qi = np.arange(BLOCK)[:, None]
    kj = np.arange(2 * BLOCK)[None, :]
    rel = qi + BLOCK - kj
    band = (rel >= 0) & (rel < BLOCK)
    relc = np.maximum(rel, 0)
    max_exact = NUM_BUCKETS // 2
    relf = np.maximum(relc, 1).astype(np.float32)
    large = max_exact + (np.log(relf / np.float32(max_exact)) / np.float32(math.log(MAX_DISTANCE / max_exact))
                         * np.float32(NUM_BUCKETS - max_exact)).astype(np.int32)
    large = np.minimum(large, NUM_BUCKETS - 1)
    bucket = np.where(relc < max_exact, relc, large).astype(np.int32)
    bucket = np.where(band, bucket, -1).astype(np.int32)
    return bucket


def _mesh_pos():
    return lax.axis_index("x"), lax.axis_index("y"), lax.axis_index("c")


def _dev_index(p):
    return 4 * p[0] + 2 * p[1] + p[2]


def _gather_call(xs):
    n = len(xs)

    def body(*refs):
        x_refs, o_refs = refs[:n], refs[n:2 * n]
        send_sems, recv_sems, local_sems = refs[2 * n:]
        x, y, c = _mesh_pos()
        me, sib = (x, y, c), (x, y, 1 - c)
        chips = [(1 - x, y), (x, 1 - y), (1 - x, 1 - y)]

        def copy(a, k, block, to, src=None):
            slot = o_refs[a].at[_dev_index(block)]
            return pltpu.make_async_remote_copy(
                src_ref=slot if src is None else src, dst_ref=slot,
                send_sem=send_sems.at[a * 7 + k], recv_sem=recv_sems.at[a * 7 + k],
                device_id=to, device_id_type=MESH)

        mine = [pltpu.make_async_copy(x_refs[a], o_refs[a].at[_dev_index(me)], local_sems.at[a]) for a in range(n)]
        for cp in mine:
            cp.start()
        first = []
        for a in range(n):
            first.append(copy(a, 0, me, sib, src=x_refs[a]))
            first += [copy(a, 1 + j, me, (*chip, c), src=x_refs[a]) for j, chip in enumerate(chips)]
        for cp in first:
            cp.start()
        passed = []
        for j, chip in enumerate(chips):
            for a in range(n):
                copy(a, 1 + j, (*chip, c), me).wait_recv()
                fwd = copy(a, 4 + j, (*chip, c), sib)
                fwd.start()
                passed.append(fwd)
        for a in range(n):
            copy(a, 0, sib, me).wait_recv()
            for j, chip in enumerate(chips):
                copy(a, 4 + j, (*chip, 1 - c), me).wait_recv()
        for cp in first + passed:
            cp.wait_send()
        for cp in mine:
            cp.wait()

    any_spec = pl.BlockSpec(memory_space=pl.ANY)
    return pl.pallas_call(
        body,
        name="gather_weights",
        out_shape=[_sds((N_DEV,) + a.shape, a.dtype) for a in xs],
        in_specs=[any_spec] * n,
        out_specs=[any_spec] * n,
        scratch_shapes=[pltpu.SemaphoreType.DMA((7 * n,)), pltpu.SemaphoreType.DMA((7 * n,)),
                        pltpu.SemaphoreType.DMA((n,))],
    )(*xs)


def _exchange_call(bs):
    n = len(bs)

    def body(*refs):
        b_refs, r_refs = refs[:n], refs[n:2 * n]
        send_sems, recv_sems, local_sems = refs[2 * n:]
        x, y, c = _mesh_pos()
        me_idx = _dev_index((x, y, c))
        mine = [pltpu.make_async_copy(b_refs[a].at[me_idx], r_refs[a].at[me_idx], local_sems.at[a]) for a in range(n)]
        for cp in mine:
            cp.start()
        sent = []
        for k, r in enumerate(RELS):
            peer = ((1 - x) if r[0] else x, (1 - y) if r[1] else y, (1 - c) if r[2] else c)
            pidx = _dev_index(peer)
            for a in range(n):
                out = pltpu.make_async_remote_copy(
                    src_ref=b_refs[a].at[pidx], dst_ref=r_refs[a].at[me_idx],
                    send_sem=send_sems.at[a * 7 + k], recv_sem=recv_sems.at[a * 7 + k],
                    device_id=peer, device_id_type=MESH)
                out.start()
                inc = pltpu.make_async_remote_copy(
                    src_ref=b_refs[a].at[pidx], dst_ref=r_refs[a].at[pidx],
                    send_sem=send_sems.at[a * 7 + k], recv_sem=recv_sems.at[a * 7 + k],
                    device_id=peer, device_id_type=MESH)
                sent.append((out, inc))
        for out, inc in sent:
            inc.wait_recv()
        for out, inc in sent:
            out.wait_send()
        for cp in mine:
            cp.wait()

    any_spec = pl.BlockSpec(memory_space=pl.ANY)
    return pl.pallas_call(
        body,
        name="exchange_grads",
        out_shape=[_sds(b.shape, b.dtype) for b in bs],
        in_specs=[any_spec] * n,
        out_specs=[any_spec] * n,
        scratch_shapes=[pltpu.SemaphoreType.DMA((7 * n,)), pltpu.SemaphoreType.DMA((7 * n,)),
                        pltpu.SemaphoreType.DMA((n,))],
    )(*bs)


def _proj_call(x2, w_t, tm):
    s_len = x2.shape[0]

    def body(x_ref, w_ref, qf_ref, kf_ref, vf_ref, fz_ref, qs_ref, ks_ref, vs_ref, sz_ref, fft_ref, vat_ref):
        xb = x_ref[...].astype(bf16)
        vt = lax.dot_general(w_ref[A_FV:A_FV + 512, :], xb, NT, preferred_element_type=f32)
        ones_row = jnp.where(lax.broadcasted_iota(jnp.int32, (HEAD_DIM, tm), 0) == 0, 1.0, 0.0).astype(bf16)
        for h in range(FOX_HEADS):
            vat_ref[h, 0:HEAD_DIM, :] = vt[h * HEAD_DIM:(h + 1) * HEAD_DIM, :].astype(bf16)
            vat_ref[h, HEAD_DIM:2 * HEAD_DIM, :] = ones_row

        def seg(off, width):
            return lax.dot_general(xb, w_ref[off:off + width, :], NT, preferred_element_type=f32)

        def put_heads(ref, acc, nheads):
            for h in range(nheads):
                ref[h] = acc[:, h * HEAD_DIM:(h + 1) * HEAD_DIM].astype(bf16)

        put_heads(qf_ref, seg(A_FQ, 512) * SCALE, FOX_HEADS)
        put_heads(kf_ref, seg(A_FK, 512), FOX_HEADS)
        put_heads(vf_ref, seg(A_FV, 512), FOX_HEADS)
        fz_ref[...] = seg(A_FZ, 512)
        put_heads(qs_ref, seg(A_SQ, 512) * SCALE, SWA_HEADS)
        put_heads(ks_ref, seg(A_SK, 128), SWA_KV_HEADS)
        put_heads(vs_ref, seg(A_SV, 128), SWA_KV_HEADS)
        sz_ref[...] = seg(A_SZ, 512)
        fft_ref[...] = seg(A_FF, 128).T[:FOX_HEADS, :]

    def heads(nh):
        return pl.BlockSpec((nh, tm, HEAD_DIM), lambda i: (0, i, 0))

    wide = pl.BlockSpec((tm, 512), lambda i: (i, 0))
    return pl.pallas_call(
        body,
        name="proj_fwd",
        grid=(s_len // tm,),
        in_specs=[pl.BlockSpec((tm, D_MODEL), lambda i: (i, 0)), pl.BlockSpec((A_W, D_MODEL), lambda i: (0, 0))],
        out_specs=[heads(8), heads(8), heads(8), wide, heads(8), heads(2), heads(2), wide,
                   pl.BlockSpec((FOX_HEADS, tm), lambda i: (0, i)),
                   pl.BlockSpec((FOX_HEADS, 2 * HEAD_DIM, tm), lambda i: (0, 0, i))],
        out_shape=[_sds((8, s_len, HEAD_DIM), bf16)] * 3 + [_sds((s_len, 512), f32), _sds((8, s_len, HEAD_DIM), bf16),
                   _sds((2, s_len, HEAD_DIM), bf16), _sds((2, s_len, HEAD_DIM), bf16), _sds((s_len, 512), f32),
                   _sds((FOX_HEADS, s_len), f32), _sds((FOX_HEADS, 2 * HEAD_DIM, s_len), bf16)],
        compiler_params=_params(("arbitrary",)),
    )(x2, w_t)


AUG = 2 * HEAD_DIM


def _augment_call(q, k, cum_row, tm):
    nh, s_len, _ = q.shape

    def body(q_ref, k_ref, c_ref, qat_ref, ka_ref):
        c = c_ref[0]
        hi = c.astype(bf16).astype(f32)
        r1 = c - hi
        mid = r1.astype(bf16).astype(f32)
        lo = (r1 - mid).astype(bf16).astype(f32)
        row = lax.broadcasted_iota(jnp.int32, (HEAD_DIM, tm), 0)
        q_tail = jnp.where(row == 0, hi, jnp.where(row == 1, mid, jnp.where(row == 2, lo,
                           jnp.where(row < 6, 1.0, 0.0))))
        k_tail = jnp.where(row < 3, 1.0, jnp.where(row == 3, -hi, jnp.where(row == 4, -mid,
                           jnp.where(row == 5, -lo, 0.0))))
        qat_ref[0, 0:HEAD_DIM, :] = q_ref[0].astype(f32).T.astype(bf16)
        qat_ref[0, HEAD_DIM:AUG, :] = q_tail.astype(bf16)
        ka_ref[0] = jnp.concatenate([k_ref[0], k_tail.T.astype(bf16)], axis=1)

    tile = pl.BlockSpec((1, tm, HEAD_DIM), lambda h, i: (h, i, 0))
    return pl.pallas_call(
        body,
        name="fox_augment",
        grid=(nh, s_len // tm),
        in_specs=[tile, tile, pl.BlockSpec((1, 1, tm), lambda h, i: (h, 0, i))],
        out_specs=[pl.BlockSpec((1, AUG, tm), lambda h, i: (h, 0, i)),
                   pl.BlockSpec((1, tm, AUG), lambda h, i: (h, i, 0))],
        out_shape=[_sds((nh, AUG, s_len), bf16), _sds((nh, s_len, AUG), bf16)],
        compiler_params=_params(("arbitrary", "arbitrary")),
    )(q, k, cum_row)


CUM_CHUNK = 512


def _cum_call(fft, bf_col):
    s_len = fft.shape[1]
    ch = CUM_CHUNK

    def body(f_ref, b_ref, cum_ref, sg_ref):
        r = lax.broadcasted_iota(jnp.int32, (ch, ch), 0)
        c = lax.broadcasted_iota(jnp.int32, (ch, ch), 1)
        upper = (r <= c).astype(f32)
        carry = jnp.zeros((FOX_HEADS, 1), f32)
        for n in range(s_len // ch):
            z = f_ref[:, n * ch:(n + 1) * ch] + b_ref[...]
            logf = jnp.minimum(z, 0.0) - jnp.log1p(jnp.exp(-jnp.abs(z)))
            sg_ref[:, n * ch:(n + 1) * ch] = 1.0 / (1.0 + jnp.exp(z))
            cs = jnp.dot(logf, upper, precision=HIGHEST, preferred_element_type=f32) + carry
            cum_ref[:, n * ch:(n + 1) * ch] = cs
            carry = cs[:, ch - 1:ch]

    return pl.pallas_call(
        body,
        name="fox_cum_fwd",
        out_shape=[_sds((FOX_HEADS, s_len), f32)] * 2,
        compiler_params=_params(),
    )(fft, bf_col)


def _cum_bwd_call(dcq, dck, sg):
    s_len = sg.shape[1]
    ch = CUM_CHUNK
    nch = s_len // ch

    def body(q_ref, k_ref, sg_ref, dff_ref, dbf_ref):
        r = lax.broadcasted_iota(jnp.int32, (ch, ch), 0)
        c = lax.broadcasted_iota(jnp.int32, (ch, ch), 1)
        lower = (r >= c).astype(f32)
        dff_ref[...] = jnp.zeros_like(dff_ref)
        carry = jnp.zeros((FOX_HEADS, 1), f32)
        total = jnp.zeros((FOX_HEADS, 1), f32)
        for n in reversed(range(nch)):
            sl = slice(n * ch, (n + 1) * ch)
            dcum = q_ref[:, sl] - k_ref[:, sl]
            rs = jnp.dot(dcum, lower, precision=HIGHEST, preferred_element_type=f32) + carry
            carry = rs[:, 0:1]
            dff = rs * sg_ref[:, sl]
            dff_ref[0:FOX_HEADS, sl] = dff
            total = total + jnp.sum(dff, axis=1, keepdims=True)
        dbf_ref[...] = jnp.broadcast_to(total, (FOX_HEADS, 128))

    return pl.pallas_call(
        body,
        name="fox_cum_bwd",
        out_shape=[_sds((128, s_len), f32), _sds((FOX_HEADS, 128), f32)],
        compiler_params=_params(),
    )(dcq, dck, sg)


FOX_T = 512
LANES = 128


def _causal_keep(t):
    return lax.broadcasted_iota(jnp.int32, (t, t), 0) <= lax.broadcasted_iota(jnp.int32, (t, t), 1)


def _fox_fwd_call(qat, ka, vat):
    nh, s_len, _ = ka.shape
    t = FOX_T

    def body(qat_ref, ka_ref, vat_ref, o_ref, lse_ref, s0, s1, p0, p1, a0, a1, m_ref, acc_ref):
        i = pl.program_id(1)
        qt = qat_ref[0]
        m_ref[...] = jnp.full((1, t), NEG_INF, f32)
        acc_ref[...] = jnp.zeros((AUG, t), f32)
        bufs = ((s0, p0, a0), (s1, p1, a1))

        def scores(j, b, masked):
            kb = ka_ref[0, pl.ds(pl.multiple_of(j * t, t), t), :]
            st = jnp.dot(kb, qt, preferred_element_type=f32)
            if masked:
                st = jnp.where(_causal_keep(t), st, NEG_INF)
            bufs[b][0][...] = st

        def softmax(b):
            s_ref, p_ref, a_ref = bufs[b]
            for c in range(t // LANES):
                cols = slice(c * LANES, (c + 1) * LANES)
                m_old = m_ref[:, cols]
                m_new = jnp.maximum(m_old, jnp.max(s_ref[:, cols], axis=0, keepdims=True))
                m_ref[:, cols] = m_new
                a_ref[:, cols] = jnp.exp(m_old - m_new)
                p_ref[:, cols] = jnp.exp(s_ref[:, cols] - m_new).astype(bf16)

        def accum(j, b):
            vt = vat_ref[0, :, pl.ds(pl.multiple_of(j * t, t), t)]
            acc_ref[...] = bufs[b][2][...] * acc_ref[...] + jnp.dot(vt, bufs[b][1][...], preferred_element_type=f32)

        def key_tile(n):
            return jnp.where(n == 0, i, n - 1)

        def step(n, b):
            scores(n - 1, b, False)
            softmax(1 - b)
            accum(key_tile(n - 2), b)

        @pl.when(i == 0)
        def _():
            scores(0, 0, True)
            softmax(0)
            accum(0, 0)

        @pl.when(i >= 1)
        def _():
            scores(i, 0, True)
            scores(0, 1, False)
            softmax(0)

        def two_steps(d, _):
            n = 2 + 2 * d
            step(n, 0)
            step(n + 1, 1)
            return 0

        lax.fori_loop(0, (i - 1) // 2, two_steps, 0)

        @pl.when((i >= 2) & (i % 2 == 0))
        def _():
            step(i, 0)
            softmax(0)
            accum(i - 2, 1)
            accum(i - 1, 0)

        @pl.when(i % 2 == 1)
        def _():
            softmax(1)
            accum(key_tile(i - 1), 0)
            accum(i - 1, 1)

        l = acc_ref[HEAD_DIM:HEAD_DIM + 1, :]
        o_ref[0] = acc_ref[0:HEAD_DIM, :] / l
        lse_ref[0] = m_ref[...] + jnp.log(l)

    return pl.pallas_call(
        body,
        name="fox_fwd",
        grid=(nh, s_len // t),
        in_specs=[pl.BlockSpec((1, AUG, t), lambda h, i: (h, 0, i)),
                  pl.BlockSpec((1, s_len, AUG), lambda h, i: (h, 0, 0)),
                  pl.BlockSpec((1, AUG, s_len), lambda h, i: (h, 0, 0))],
        out_specs=[pl.BlockSpec((1, HEAD_DIM, t), lambda h, i: (h, 0, i)),
                   pl.BlockSpec((1, 1, t), lambda h, i: (h, 0, i))],
        out_shape=[_sds((nh, HEAD_DIM, s_len), f32), _sds((nh, 1, s_len), f32)],
        scratch_shapes=[pltpu.VMEM((t, t), f32), pltpu.VMEM((t, t), f32), pltpu.VMEM((t, t), bf16),
                        pltpu.VMEM((t, t), bf16), pltpu.VMEM((1, t), f32), pltpu.VMEM((1, t), f32),
                        pltpu.VMEM((1, t), f32), pltpu.VMEM((AUG, t), f32)],
        compiler_params=_params(("arbitrary", "arbitrary")),
    )(qat, ka, vat)


SWA_TS = 512


def _swa_bias_call(rel_bias, bucket):
    def body(rb_ref, bk_ref, b_ref, b0_ref):
        bk = bk_ref[...]
        col = lax.broadcasted_iota(jnp.int32, (BLOCK, 2 * BLOCK), 1)
        for h in range(SWA_HEADS):
            acc = jnp.full((BLOCK, 2 * BLOCK), NEG_INF, f32)
            for b in range(NUM_BUCKETS):
                acc = jnp.where(bk == b, rb_ref[b, h], acc)
            g, hh = divmod(h, SWA_GROUP)
            b_ref[g, hh * BLOCK:(hh + 1) * BLOCK, :] = acc
            b0_ref[g, hh * BLOCK:(hh + 1) * BLOCK, :] = jnp.where(col < BLOCK, NEG_INF, acc)

    return pl.pallas_call(
        body,
        name="swa_bias",
        in_specs=[pl.BlockSpec(memory_space=pltpu.SMEM), pl.BlockSpec(memory_space=pltpu.VMEM)],
        out_shape=[_sds((SWA_KV_HEADS, SWA_GROUP * BLOCK, 2 * BLOCK), f32)] * 2,
        compiler_params=_params(),
    )(rel_bias, bucket)


def _swa_bias_bwd_call(dbias, bucket):
    def body(d_ref, bk_ref, o_ref):
        bk = bk_ref[...]
        row = lax.broadcasted_iota(jnp.int32, (NUM_BUCKETS, 128), 0)
        col = lax.broadcasted_iota(jnp.int32, (NUM_BUCKETS, 128), 1)
        out = jnp.zeros((NUM_BUCKETS, 128), f32)
        for h in range(SWA_HEADS):
            g, hh = divmod(h, SWA_GROUP)
            d = d_ref[g, hh * BLOCK:(hh + 1) * BLOCK, :]
            for b in range(NUM_BUCKETS):
                val = jnp.sum(jnp.sum(jnp.where(bk == b, d, 0.0), axis=1, keepdims=True), axis=0, keepdims=True)
                out = jnp.where((row == b) & (col == h), val, out)
        o_ref[...] = out

    return pl.pallas_call(
        body,
        name="swa_bias_bwd",
        out_shape=_sds((NUM_BUCKETS, 128), f32),
        compiler_params=_params(),
    )(dbias, bucket)


def _swa_specs(ts):
    nb = ts // BLOCK
    qspec = pl.BlockSpec((SWA_HEADS, ts, HEAD_DIM), lambda n: (0, n, 0))
    cur = pl.BlockSpec((SWA_KV_HEADS, ts, HEAD_DIM), lambda n: (0, n, 0))
    prev = pl.BlockSpec((SWA_KV_HEADS, BLOCK, HEAD_DIM), lambda n: (0, jnp.maximum(n * nb - 1, 0), 0))
    return qspec, cur, prev


def _sink_col(sink_ref, g):
    return jnp.concatenate([jnp.full((BLOCK, 1), sink_ref[g * SWA_GROUP + hh], f32) for hh in range(SWA_GROUP)], axis=0)


def _swa_fwd_call(q, k, v, bias, bias0, sink):
    s_len = q.shape[1]
    ts = SWA_TS
    nb = ts // BLOCK

    def body(q_ref, kc_ref, kp_ref, vc_ref, vp_ref, b_ref, b0_ref, sink_ref, o_ref, lse_ref):
        first = pl.program_id(0) == 0
        for g in range(SWA_KV_HEADS):
            kall = jnp.concatenate([kp_ref[g], kc_ref[g]], axis=0)
            vall = jnp.concatenate([vp_ref[g], vc_ref[g]], axis=0)
            sink_c = _sink_col(sink_ref, g)
            for b in range(nb):
                rows = slice(b * BLOCK, (b + 1) * BLOCK)
                qg = jnp.concatenate([q_ref[g * SWA_GROUP + hh, rows, :] for hh in range(SWA_GROUP)], axis=0)
                kcat = kall[b * BLOCK:(b + 2) * BLOCK]
                vcat = vall[b * BLOCK:(b + 2) * BLOCK]
                bias_b = b_ref[g]
                if b == 0:
                    bias_b = jnp.where(first, b0_ref[g], bias_b)
                s = lax.dot_general(qg, kcat, NT, preferred_element_type=f32) + bias_b
                m = jnp.maximum(jnp.max(s, axis=1, keepdims=True), sink_c)
                p = jnp.exp(s - m)
                l = jnp.sum(p, axis=1, keepdims=True) + jnp.exp(sink_c - m)
                o = jnp.dot(p.astype(bf16), vcat, preferred_element_type=f32) / l
                lse = m + jnp.log(l)
                for hh in range(SWA_GROUP):
                    o_ref[g * SWA_GROUP + hh, rows, :] = o[hh * BLOCK:(hh + 1) * BLOCK]
                    lse_ref[g * SWA_GROUP + hh, rows, :] = lse[hh * BLOCK:(hh + 1) * BLOCK]

    qspec, cur, prev = _swa_specs(ts)
    bspec = pl.BlockSpec((SWA_KV_HEADS, SWA_GROUP * BLOCK, 2 * BLOCK), lambda n: (0, 0, 0))
    return pl.pallas_call(
        body,
        name="swa_fwd",
        grid=(s_len // ts,),
        in_specs=[qspec, cur, prev, cur, prev, bspec, bspec, pl.BlockSpec(memory_space=pltpu.SMEM)],
        out_specs=[pl.BlockSpec((SWA_HEADS, ts, HEAD_DIM), lambda n: (0, n, 0)),
                   pl.BlockSpec((SWA_HEADS, ts, 1), lambda n: (0, n, 0))],
        out_shape=[_sds((SWA_HEADS, s_len, HEAD_DIM), f32), _sds((SWA_HEADS, s_len, 1), f32)],
        compiler_params=_params(("arbitrary",)),
    )(q, k, k, v, v, bias, bias0, sink)


def _head_selector():
    sel = np.zeros((512, 128), np.float32)
    for h in range(8):
        sel[h * HEAD_DIM:(h + 1) * HEAD_DIM, h] = 1.0
    return sel


def _post_call(of, fz, osw, sz, x2, tgt, wo, ln_g, ln_b, sel, tm):
    s_len = x2.shape[0]

    def body(of_ref, fz_ref, os_ref, sz_ref, x_ref, t_ref, wo_ref, g_ref, b_ref, sel_ref,
             dh_ref, dof_ref, dfz_ref, dos_ref, dsz_ref, dlf_ref, dls_ref, dwo_ref, dg_ref, db_ref, loss_ref):
        n = pl.program_id(0)

        @pl.when(n == 0)
        def _():
            dwo_ref[...] = jnp.zeros_like(dwo_ref)
            dg_ref[...] = jnp.zeros_like(dg_ref)
            db_ref[...] = jnp.zeros_like(db_ref)
            loss_ref[...] = jnp.zeros_like(loss_ref)

        o_f = of_ref[...].T
        o_s = jnp.concatenate([os_ref[h] for h in range(SWA_HEADS)], axis=1)
        fz = fz_ref[...]
        sz = sz_ref[...]
        sg_f = jax.nn.sigmoid(fz)
        sg_s = jax.nn.sigmoid(sz)
        silu_f = fz * sg_f
        silu_s = sz * sg_s
        mixed = jnp.concatenate([o_f * silu_f, o_s * silu_s], axis=1).astype(bf16)
        y = jnp.dot(mixed, wo_ref[...], preferred_element_type=f32)
        h = ALPHA * x_ref[...] + y
        mu = jnp.mean(h, axis=1, keepdims=True)
        hc = h - mu
        var = jnp.mean(hc * hc, axis=1, keepdims=True)
        rstd = lax.rsqrt(var + LN_EPS)
        xhat = hc * rstd
        gam = g_ref[...]
        out = xhat * gam + b_ref[...]
        err = out - t_ref[...]
        tok_loss = jnp.mean(err * err, axis=1, keepdims=True)
        loss_ref[...] += 0.5 * jnp.sum(tok_loss, axis=0, keepdims=True)
        dout = err * (1.0 / D_MODEL)
        dg_ref[...] += jnp.sum(dout * xhat, axis=0, keepdims=True)
        db_ref[...] += jnp.sum(dout, axis=0, keepdims=True)
        dxh = dout * gam
        m1 = jnp.mean(dxh, axis=1, keepdims=True)
        m2 = jnp.mean(dxh * xhat, axis=1, keepdims=True)
        dh = rstd * (dxh - m1 - xhat * m2)
        dh_ref[...] = dh
        dyb = dh.astype(bf16)
        dwo_ref[...] += lax.dot_general(mixed, dyb, TN, preferred_element_type=f32)
        dmix = lax.dot_general(dyb, wo_ref[...], NT, preferred_element_type=f32)
        dm_f = dmix[:, :512]
        dm_s = dmix[:, 512:]
        do_f = dm_f * silu_f
        do_s = dm_s * silu_s
        dfz_ref[...] = (dm_f * o_f * (sg_f * (1.0 + fz * (1.0 - sg_f)))).astype(bf16)
        dsz_ref[...] = (dm_s * o_s * (sg_s * (1.0 + sz * (1.0 - sg_s)))).astype(bf16)
        dof_ref[...] = do_f.T.astype(bf16)
        for hd in range(SWA_HEADS):
            dos_ref[hd] = do_s[:, hd * HEAD_DIM:(hd + 1) * HEAD_DIM].astype(bf16)
        sel_m = sel_ref[...]
        dl_f = jnp.dot(do_f * o_f, sel_m, precision=HIGHEST, preferred_element_type=f32)
        dl_s = jnp.dot(do_s * o_s, sel_m, precision=HIGHEST, preferred_element_type=f32)
        dlf_ref[...] = dl_f.T[:FOX_HEADS, :]
        dls_ref[...] = dl_s

    heads_f32 = pl.BlockSpec((8, tm, HEAD_DIM), lambda n: (0, n, 0))
    half = pl.BlockSpec((tm, 512), lambda n: (n, 0))
    fullw = pl.BlockSpec((tm, D_MODEL), lambda n: (n, 0))
    vec = pl.BlockSpec((1, D_MODEL), lambda n: (0, 0))
    return pl.pallas_call(
        body,
        name="post_fwd_bwd",
        grid=(s_len // tm,),
        in_specs=[pl.BlockSpec((512, tm), lambda n: (0, n)), half, heads_f32, half, fullw, fullw,
                  pl.BlockSpec((D_MODEL, D_MODEL), lambda n: (0, 0)), vec, vec,
                  pl.BlockSpec((512, 128), lambda n: (0, 0))],
        out_specs=[fullw, pl.BlockSpec((512, tm), lambda n: (0, n)), half, heads_f32, half,
                   pl.BlockSpec((FOX_HEADS, tm), lambda n: (0, n)), pl.BlockSpec((tm, 128), lambda n: (n, 0)),
                   pl.BlockSpec((D_MODEL, D_MODEL), lambda n: (0, 0)), vec, vec,
                   pl.BlockSpec((1, 1), lambda n: (0, 0))],
        out_shape=[_sds((s_len, D_MODEL), f32), _sds((512, s_len), bf16), _sds((s_len, 512), bf16),
                   _sds((8, s_len, HEAD_DIM), bf16), _sds((s_len, 512), bf16),
                   _sds((FOX_HEADS, s_len), f32), _sds((s_len, 128), f32),
                   _sds((D_MODEL, D_MODEL), f32), _sds((1, D_MODEL), f32), _sds((1, D_MODEL), f32),
                   _sds((1, 1), f32)],
        compiler_params=_params(("arbitrary",)),
    )(of, fz, osw, sz, x2, tgt, wo, ln_g, ln_b, sel)


def _fox_bwd_call(ka, v, qat, dot, lse_row, dl_row):
    nh, s_len, _ = ka.shape
    t = FOX_T
    nt = s_len // t
    ck_slot = HEAD_DIM + 3
    cq_slot = HEAD_DIM

    def body(ka_ref, v_ref, qat_ref, dot_ref, lse_ref, dl_ref,
             dk_ref, dv_ref, dq_ref, dcq_ref, dck_ref, dqt_s, dkat_s, dvt_s, p0, p1, ds0, ds1):
        j = pl.program_id(1)

        @pl.when(j == 0)
        def _():
            dqt_s[...] = jnp.zeros_like(dqt_s)

        kb = ka_ref[0]
        vb = v_ref[0]
        kt = kb.astype(f32).T.astype(bf16)
        dkat_s[...] = jnp.zeros_like(dkat_s)
        dvt_s[...] = jnp.zeros_like(dvt_s)

        pbuf, dsbuf = (p0, p1), (ds0, ds1)

        def probs(i, b, masked):
            cols = pl.ds(pl.multiple_of(i * t, t), t)
            st = jnp.dot(kb, qat_ref[0, :, cols], preferred_element_type=f32)
            if masked:
                st = jnp.where(_causal_keep(t), st, NEG_INF)
            pt = jnp.exp(st - lse_ref[0, :, cols])
            dpt = jnp.dot(vb, dot_ref[0, :, cols], preferred_element_type=f32)
            pbuf[b][...] = pt.astype(bf16)
            dsbuf[b][...] = (pt * (dpt - dl_ref[0, :, cols])).astype(bf16)

        def grads(i, b):
            cols = pl.ds(pl.multiple_of(i * t, t), t)
            dvt_s[...] += lax.dot_general(dot_ref[0, :, cols], pbuf[b][...], NT, preferred_element_type=f32)
            dkat_s[...] += lax.dot_general(qat_ref[0, :, cols], dsbuf[b][...], NT, preferred_element_type=f32)
            dqt_s[:, cols] += jnp.dot(kt, dsbuf[b][...], preferred_element_type=f32)

        rest = nt - 1 - j
        probs(j, 0, True)

        def two_steps(d, _):
            i = j + 1 + 2 * d
            probs(i, 1, False)
            grads(i - 1, 0)
            probs(i + 1, 0, False)
            grads(i, 1)
            return 0

        lax.fori_loop(0, rest // 2, two_steps, 0)

        @pl.when(rest % 2 == 1)
        def _():
            probs(nt - 1, 1, False)
            grads(nt - 2, 0)
            grads(nt - 1, 1)

        @pl.when(rest % 2 == 0)
        def _():
            grads(nt - 1, 0)

        dk_ref[0] = dkat_s[0:HEAD_DIM, :].T.astype(bf16)
        dv_ref[0] = dvt_s[...].T.astype(bf16)
        dck_ref[0] = dkat_s[ck_slot:ck_slot + 1, :]

        @pl.when(j == nt - 1)
        def _():
            for cidx in range(nt):
                sl = slice(cidx * t, (cidx + 1) * t)
                dq_ref[0, sl, :] = (dqt_s[0:HEAD_DIM, sl].T * SCALE).astype(bf16)
            dcq_ref[0] = dqt_s[cq_slot:cq_slot + 1, :]

    ktile = pl.BlockSpec((1, t, AUG), lambda h, j: (h, j, 0))
    tile = pl.BlockSpec((1, t, HEAD_DIM), lambda h, j: (h, j, 0))
    full = pl.BlockSpec((1, s_len, HEAD_DIM), lambda h, j: (h, 0, 0))
    rowv = pl.BlockSpec((1, 1, s_len), lambda h, j: (h, 0, 0))
    return pl.pallas_call(
        body,
        name="fox_bwd",
        grid=(nh, nt),
        in_specs=[ktile, tile, pl.BlockSpec((1, AUG, s_len), lambda h, j: (h, 0, 0)),
                  pl.BlockSpec((1, HEAD_DIM, s_len), lambda h, j: (h, 0, 0)), rowv, rowv],
        out_specs=[tile, tile, full, rowv, pl.BlockSpec((1, 1, t), lambda h, j: (h, 0, j))],
        out_shape=[_sds((nh, s_len, HEAD_DIM), bf16)] * 3 + [_sds((nh, 1, s_len), f32), _sds((nh, 1, s_len), f32)],
        scratch_shapes=[pltpu.VMEM((AUG, s_len), f32), pltpu.VMEM((AUG, t), f32), pltpu.VMEM((HEAD_DIM, t), f32)]
                       + [pltpu.VMEM((t, t), bf16)] * 4,
        compiler_params=_params(("arbitrary", "arbitrary")),
    )(ka, v, qat, dot, lse_row, dl_row)


def _swa_bwd_call(q, k, v, do, lse, dl, bias, bias0, sink):
    s_len = q.shape[1]
    ts = SWA_TS
    nb = ts // BLOCK
    nsteps = s_len // ts

    def body(q_ref, kc_ref, kp_ref, vc_ref, vp_ref, do_ref, lse_ref, dl_ref, b_ref, b0_ref, sink_ref,
             dq_ref, dk_ref, dv_ref, dbias_ref, dsink_ref, dk_s, dv_s, tail_k, tail_v, sk_s):
        n = pl.program_id(0)

        @pl.when(n == 0)
        def _():
            dbias_ref[...] = jnp.zeros_like(dbias_ref)
            sk_s[...] = jnp.zeros_like(sk_s)

        @pl.when(n < nsteps)
        def _():
            first = n == 0
            dk_s[...] = jnp.zeros_like(dk_s)
            dv_s[...] = jnp.zeros_like(dv_s)
            for g in range(SWA_KV_HEADS):
                kall = jnp.concatenate([kp_ref[g], kc_ref[g]], axis=0)
                vall = jnp.concatenate([vp_ref[g], vc_ref[g]], axis=0)
                sink_c = _sink_col(sink_ref, g)
                for b in range(nb):
                    rows = slice(b * BLOCK, (b + 1) * BLOCK)
                    heads = [g * SWA_GROUP + hh for hh in range(SWA_GROUP)]
                    qg = jnp.concatenate([q_ref[h, rows, :] for h in heads], axis=0)
                    dog = jnp.concatenate([do_ref[h, rows, :] for h in heads], axis=0)
                    lse_c = jnp.concatenate([lse_ref[h, rows, :] for h in heads], axis=0)
                    dl_c = jnp.concatenate([dl_ref[rows, h:h + 1] for h in heads], axis=0)
                    kcat = kall[b * BLOCK:(b + 2) * BLOCK]
                    vcat = vall[b * BLOCK:(b + 2) * BLOCK]
                    bias_b = b_ref[g]
                    if b == 0:
                        bias_b = jnp.where(first, b0_ref[g], bias_b)
                    s = lax.dot_general(qg, kcat, NT, preferred_element_type=f32) + bias_b
                    p = jnp.exp(s - lse_c)
                    dp = lax.dot_general(dog, vcat, NT, preferred_element_type=f32)
                    ds = p * (dp - dl_c)
                    dsb = ds.astype(bf16)
                    dqg = jnp.dot(dsb, kcat, preferred_element_type=f32) * SCALE
                    for hh, h in enumerate(heads):
                        dq_ref[h, rows, :] = dqg[hh * BLOCK:(hh + 1) * BLOCK].astype(bf16)
                    win = slice(b * BLOCK, (b + 2) * BLOCK)
                    dk_s[g, win, :] += lax.dot_general(dsb, qg, TN, preferred_element_type=f32)
                    dv_s[g, win, :] += lax.dot_general(p.astype(bf16), dog, TN, preferred_element_type=f32)
                    dbias_ref[g] += ds
                    sk_s[g] += -jnp.exp(sink_c - lse_c) * dl_c

        @pl.when(n > 0)
        def _():
            last = slice(ts - BLOCK, ts)
            for g in range(SWA_KV_HEADS):
                add_k = jnp.where(n < nsteps, dk_s[g, 0:BLOCK, :], 0.0)
                add_v = jnp.where(n < nsteps, dv_s[g, 0:BLOCK, :], 0.0)
                dk_ref[g, 0:ts - BLOCK, :] = tail_k[g, 0:ts - BLOCK, :].astype(bf16)
                dv_ref[g, 0:ts - BLOCK, :] = tail_v[g, 0:ts - BLOCK, :].astype(bf16)
                dk_ref[g, last, :] = (tail_k[g, last, :] + add_k).astype(bf16)
                dv_ref[g, last, :] = (tail_v[g, last, :] + add_v).astype(bf16)

        @pl.when(n < nsteps)
        def _():
            tail_k[...] = dk_s[:, BLOCK:, :]
            tail_v[...] = dv_s[:, BLOCK:, :]

        @pl.when(n == nsteps)
        def _():
            row = lax.broadcasted_iota(jnp.int32, (SWA_HEADS, 128), 0)
            out = jnp.zeros((SWA_HEADS, 128), f32)
            for h in range(SWA_HEADS):
                g, hh = divmod(h, SWA_GROUP)
                val = jnp.sum(sk_s[g, hh * BLOCK:(hh + 1) * BLOCK, :], axis=0, keepdims=True)
                out = jnp.where(row == h, val, out)
            dsink_ref[...] = out

    last_step = nsteps - 1

    def cl(n):
        return jnp.minimum(n, last_step)

    qspec = pl.BlockSpec((SWA_HEADS, ts, HEAD_DIM), lambda n: (0, cl(n), 0))
    cur = pl.BlockSpec((SWA_KV_HEADS, ts, HEAD_DIM), lambda n: (0, cl(n), 0))
    prev = pl.BlockSpec((SWA_KV_HEADS, BLOCK, HEAD_DIM), lambda n: (0, jnp.maximum(cl(n) * nb - 1, 0), 0))
    lsespec = pl.BlockSpec((SWA_HEADS, ts, 1), lambda n: (0, cl(n), 0))
    dlspec = pl.BlockSpec((ts, 128), lambda n: (cl(n), 0))
    bspec = pl.BlockSpec((SWA_KV_HEADS, SWA_GROUP * BLOCK, 2 * BLOCK), lambda n: (0, 0, 0))
    kvout = pl.BlockSpec((SWA_KV_HEADS, ts, HEAD_DIM), lambda n: (0, jnp.maximum(n - 1, 0), 0))
    return pl.pallas_call(
        body,
        name="swa_bwd",
        grid=(nsteps + 1,),
        in_specs=[qspec, cur, prev, cur, prev, qspec, lsespec, dlspec, bspec, bspec,
                  pl.BlockSpec(memory_space=pltpu.SMEM)],
        out_specs=[qspec, kvout, kvout, bspec, pl.BlockSpec((SWA_HEADS, 128), lambda n: (0, 0))],
        out_shape=[_sds((SWA_HEADS, s_len, HEAD_DIM), bf16), _sds((SWA_KV_HEADS, s_len, HEAD_DIM), bf16),
                   _sds((SWA_KV_HEADS, s_len, HEAD_DIM), bf16),
                   _sds((SWA_KV_HEADS, SWA_GROUP * BLOCK, 2 * BLOCK), f32), _sds((SWA_HEADS, 128), f32)],
        scratch_shapes=[pltpu.VMEM((SWA_KV_HEADS, ts + BLOCK, HEAD_DIM), f32),
                        pltpu.VMEM((SWA_KV_HEADS, ts + BLOCK, HEAD_DIM), f32),
                        pltpu.VMEM((SWA_KV_HEADS, ts, HEAD_DIM), f32),
                        pltpu.VMEM((SWA_KV_HEADS, ts, HEAD_DIM), f32),
                        pltpu.VMEM((SWA_KV_HEADS, SWA_GROUP * BLOCK, 1), f32)],
        compiler_params=_params(("arbitrary",)),
    )(q, k, k, v, v, do, lse, dl, bias, bias0, sink)


def _dx_call(dh, dqf, dkf, dvf, dfz, dqs, dks, dvs, dsz, dfft, w_t, tm):
    s_len = dh.shape[0]

    def body(dh_ref, dqf_ref, dkf_ref, dvf_ref, dfz_ref, dqs_ref, dks_ref, dvs_ref, dsz_ref, dfft_ref, w_ref,
             dx_ref, dp_ref):
        def cat(ref, nheads):
            return jnp.concatenate([ref[h] for h in range(nheads)], axis=1)

        dp = jnp.concatenate([cat(dqf_ref, 8), cat(dkf_ref, 8), cat(dvf_ref, 8), dfz_ref[...], cat(dqs_ref, 8),
                              cat(dks_ref, 2), cat(dvs_ref, 2), dsz_ref[...], dfft_ref[...].T.astype(bf16)], axis=1)
        dp_ref[...] = dp
        dx_ref[...] = ALPHA * dh_ref[...] + jnp.dot(dp, w_ref[...], preferred_element_type=f32)

    def heads(nh):
        return pl.BlockSpec((nh, tm, HEAD_DIM), lambda i: (0, i, 0))

    half = pl.BlockSpec((tm, 512), lambda i: (i, 0))
    fullw = pl.BlockSpec((tm, D_MODEL), lambda i: (i, 0))
    return pl.pallas_call(
        body,
        name="dx_bwd",
        grid=(s_len // tm,),
        in_specs=[fullw, heads(8), heads(8), heads(8), half, heads(8), heads(2), heads(2), half,
                  pl.BlockSpec((128, tm), lambda i: (0, i)), pl.BlockSpec((A_W, D_MODEL), lambda i: (0, 0))],
        out_specs=[fullw, pl.BlockSpec((tm, A_W), lambda i: (i, 0))],
        out_shape=[_sds((s_len, D_MODEL), f32), _sds((s_len, A_W), bf16)],
        compiler_params=_params(("arbitrary",)),
    )(dh, dqf, dkf, dvf, dfz, dqs, dks, dvs, dsz, dfft, w_t)


DW_ROWS = 1152


def _dw_call(x2, dproj, tm):
    s_len = x2.shape[0]
    nt = s_len // tm

    def body(x_ref, dp_ref, dw_ref):
        @pl.when(pl.program_id(1) == 0)
        def _():
            dw_ref[...] = jnp.zeros_like(dw_ref)

        dw_ref[...] += lax.dot_general(dp_ref[...], x_ref[...].astype(bf16), TN, preferred_element_type=f32)

    return pl.pallas_call(
        body,
        name="dw_in_bwd",
        grid=(A_W // DW_ROWS, nt),
        in_specs=[pl.BlockSpec((tm, D_MODEL), lambda c, i: (i, 0)), pl.BlockSpec((tm, DW_ROWS), lambda c, i: (i, c))],
        out_specs=pl.BlockSpec((DW_ROWS, D_MODEL), lambda c, i: (c, 0)),
        out_shape=_sds((A_W, D_MODEL), f32),
        compiler_params=_params(("arbitrary", "arbitrary")),
    )(x2, dproj)


def _adam_call(recv, w, m, v, tc, name):
    rows, cols = w.shape

    def body(r_ref, w_ref, m_ref, v_ref, g_ref, d_ref, mo_ref, vo_ref):
        g = r_ref[0].astype(f32)
        for p in range(1, N_DEV):
            g = g + r_ref[p].astype(f32)
        mn = ADAM_B1 * m_ref[...] + (1.0 - ADAM_B1) * g
        vn = ADAM_B2 * v_ref[...] + (1.0 - ADAM_B2) * (g * g)
        m_hat = mn / (1.0 - ADAM_B1 ** ADAM_STEP)
        v_hat = vn / (1.0 - ADAM_B2 ** ADAM_STEP)
        g_ref[...] = g
        d_ref[...] = -ADAM_LR * (m_hat / (jnp.sqrt(v_hat) + ADAM_EPS) + ADAM_WD * w_ref[...])
        mo_ref[...] = mn
        vo_ref[...] = vn

    blk = pl.BlockSpec((rows, tc), lambda i: (0, i))
    return pl.pallas_call(
        body,
        name=name,
        grid=(cols // tc,),
        in_specs=[pl.BlockSpec((N_DEV, rows, tc), lambda i: (0, 0, i)), blk, blk, blk],
        out_specs=[blk] * 4,
        out_shape=[_sds((rows, cols), f32)] * 4,
        compiler_params=_params(("arbitrary",)),
    )(recv, w, m, v)


def _pad_cols(a, width=128):
    return jnp.pad(a, ((0, 0), (0, width - a.shape[1])))


def _pack_small(ln_g, ln_b, rel, b_f, sink):
    return jnp.concatenate([
        ln_g.reshape(8, 128), ln_b.reshape(8, 128), _pad_cols(rel),
        jnp.pad(_pad_cols(b_f), ((0, 7), (0, 0))), jnp.pad(_pad_cols(sink), ((0, 7), (0, 0)))], axis=0)


def _unpack_small(p):
    return (p[0:8].reshape(1, D_MODEL), p[8:16].reshape(1, D_MODEL), p[16:48, 0:8], p[48:49, 0:8], p[56:57, 0:8])


def kernel(x, w_in, b_f, rel_bias, sink, w_o, ln_g, ln_b, loss_target, m_w_in, m_b_f, m_rel_bias, m_sink, m_w_o, m_ln_g, m_ln_b, v_w_in, v_b_f, v_rel_bias, v_sink, v_w_o, v_ln_g, v_ln_b):
    x2 = x[0]
    tgt = loss_target[0]
    s_len = x2.shape[0]
    shard = w_in.shape[2]

    w_in_t = jnp.transpose(w_in[0])
    g_in, g_o = _gather_call([w_in_t.astype(bf16), w_o[0].astype(bf16)])
    wt_full = g_in.reshape(N_DEV * shard, D_MODEL)
    w_t = jnp.concatenate([wt_full[:O_FF0], wt_full[O_FF1:], wt_full[O_FF0:O_FF1],
                           jnp.zeros((A_W - D_IN, D_MODEL), bf16)], axis=0)
    wo_full = g_o.reshape(D_MODEL, D_MODEL)

    qf, kf, vf, fz, qs, ks, vs, sz, fft, vat = _proj_call(x2, w_t, 512)
    cum, sgm = _cum_call(fft, b_f.reshape(FOX_HEADS, 1))
    qat, ka = _augment_call(qf, kf, cum.reshape(FOX_HEADS, 1, s_len), 1024)
    o_ft, lse_f = _fox_fwd_call(qat, ka, vat)
    bucket = jnp.asarray(_t5_bucket_table())
    bias, bias0 = _swa_bias_call(rel_bias, bucket)
    sink_v = sink.reshape(SWA_HEADS)
    o_s, lse_s = _swa_fwd_call(qs, ks, vs, bias, bias0, sink_v)

    (dh, do_f, dfz, do_s, dsz, dl_f, dl_s, dwo, dg, db, loss_part) = _post_call(
        o_ft.reshape(FOX_HEADS * HEAD_DIM, s_len), fz, o_s, sz, x2, tgt, wo_full, ln_g, ln_b,
        jnp.asarray(_head_selector()), 256)

    dkf, dvf, dqf, dcq, dck = _fox_bwd_call(ka, vf, qat, do_f.reshape(FOX_HEADS, HEAD_DIM, s_len), lse_f,
                                            dl_f.reshape(FOX_HEADS, 1, s_len))
    dfft, dbf = _cum_bwd_call(dcq.reshape(FOX_HEADS, s_len), dck.reshape(FOX_HEADS, s_len), sgm)
    dqs, dks, dvs, dbias, dsink = _swa_bwd_call(qs, ks, vs, do_s, lse_s, dl_s, bias, bias0, sink_v)
    drel = _swa_bias_bwd_call(dbias, bucket)

    dx, dproj = _dx_call(dh, dqf, dkf, dvf, dfz, dqs, dks, dvs, dsz, dfft, w_t, 256)
    dw_t = _dw_call(x2, dproj, 1024)

    dwt_full = jnp.concatenate([dw_t[:O_FF0], dw_t[A_FF:A_FF + (O_FF1 - O_FF0)], dw_t[O_FF0:A_FF]], axis=0)
    dw_blocks = dwt_full.reshape(N_DEV, shard, D_MODEL).astype(bf16)
    dwo_blocks = dwo.reshape(N_DEV, D_MODEL // N_DEV, D_MODEL).astype(bf16)
    small = _pack_small(dg, db, drel[:, 0:8], dbf[:, 0].reshape(1, 8), dsink[:, 0].reshape(1, 8))
    loss_slot = np.zeros((64, 128), bool)
    loss_slot[49, 0] = True
    small = jnp.where(jnp.asarray(loss_slot), loss_part[0, 0], small)
    small_blocks = jnp.broadcast_to(small[None], (N_DEV,) + small.shape)
    r_in, r_o, r_small = _exchange_call([dw_blocks, dwo_blocks, small_blocks])

    win_t = [jnp.transpose(a) for a in _adam_call(
        r_in, w_in_t, jnp.transpose(m_w_in[0]), jnp.transpose(v_w_in[0]), 256, "adam_w_in")]
    g_win, d_win, nm_win, nv_win = win_t
    g_wo, d_wo, nm_wo, nv_wo = _adam_call(r_o, w_o[0], m_w_o[0], v_w_o[0], 256, "adam_w_o")
    p_w = _pack_small(ln_g, ln_b, rel_bias, b_f, sink)
    p_m = _pack_small(m_ln_g, m_ln_b, m_rel_bias, m_b_f, m_sink)
    p_v = _pack_small(v_ln_g, v_ln_b, v_rel_bias, v_b_f, v_sink)
    g_p, d_p, nm_p, nv_p = _adam_call(r_small, p_w, p_m, p_v, 128, "adam_small")

    loss = g_p[49, 0]
    g_lng, g_lnb, g_rel, g_bf, g_sink = _unpack_small(g_p)
    d_lng, d_lnb, d_rel, d_bf, d_sink = _unpack_small(d_p)
    m_lng, m_lnb, m_rel, m_bf, m_sk = _unpack_small(nm_p)
    v_lng, v_lnb, v_rel, v_bf, v_sk = _unpack_small(nv_p)
    return (loss, dx[None], g_win[None], g_bf, g_rel, g_sink, g_wo[None], g_lng, g_lnb,
            d_win[None], d_bf, d_rel, d_sink, d_wo[None], d_lng, d_lnb,
            nm_win[None], m_bf, m_rel, m_sk, nm_wo[None], m_lng, m_lnb,
            nv_win[None], v_bf, v_rel, v_sk, nv_wo[None], v_lng, v_lnb)
```

```python
import functools
import math

import numpy as np
import jax
import jax.numpy as jnp
from jax import lax
from jax.experimental import pallas as pl
from jax.experimental.pallas import tpu as pltpu

f32 = jnp.float32
bf16 = jnp.bfloat16

D_MODEL = 1024
HEAD_DIM = 64
FOX_HEADS = 8
SWA_HEADS = 8
SWA_KV_HEADS = 2
SWA_GROUP = 4
BLOCK = 128
NUM_BUCKETS = 32
MAX_DISTANCE = 128
LN_EPS = 1e-5
NEG_INF = -1e30
ALPHA = 2.0 ** 0.25
SCALE = 1.0 / math.sqrt(HEAD_DIM)
D_IN = 3336

ADAM_LR = 0.001
ADAM_B1 = 0.9
ADAM_B2 = 0.999
ADAM_EPS = 1e-08
ADAM_WD = 0.01
ADAM_STEP = 10

N_DEV = 8
A_FQ, A_FK, A_FV, A_FZ, A_SQ, A_SK, A_SV, A_SZ, A_FF, A_W = 0, 512, 1024, 1536, 2048, 2560, 2688, 2816, 3328, 3456
O_FF0, O_FF1 = 1536, 1544

VMEM_LIMIT = 48 * 1024 * 1024
HIGHEST = lax.Precision.HIGHEST
NT = (((1,), (1,)), ((), ()))
TN = (((0,), (0,)), ((), ()))
MESH = pl.DeviceIdType.MESH
RELS = [(0, 0, 1), (0, 1, 0), (0, 1, 1), (1, 0, 0), (1, 0, 1), (1, 1, 0), (1, 1, 1)]


def _params(sem=None):
    return pltpu.CompilerParams(dimension_semantics=sem, vmem_limit_bytes=VMEM_LIMIT)


def _sds(shape, dtype):
    return jax.ShapeDtypeStruct(shape, dtype)


def _t5_bucket_table():
    qi = np.arange(BLOCK)[:, None]
    kj = np.arange(2 * BLOCK)[None, :]
    rel = qi + BLOCK - kj
    band = (rel >= 0) & (rel < BLOCK)
    relc = np.maximum(rel, 0)
    max_exact = NUM_BUCKETS // 2
    relf = np.maximum(relc, 1).astype(np.float32)
    large = max_exact + (np.log(relf / np.float32(max_exact)) / np.float32(math.log(MAX_DISTANCE / max_exact))
                         * np.float32(NUM_BUCKETS - max_exact)).astype(np.int32)
    large = np.minimum(large, NUM_BUCKETS - 1)
    bucket = np.where(relc < max_exact, relc, large).astype(np.int32)
    bucket = np.where(band, bucket, -1).astype(np.int32)
    return bucket


def _mesh_pos():
    return lax.axis_index("x"), lax.axis_index("y"), lax.axis_index("c")


def _dev_index(p):
    return 4 * p[0] + 2 * p[1] + p[2]


def _gather_call(xs):
    n = len(xs)

    def body(*refs):
        x_refs, o_refs = refs[:n], refs[n:2 * n]
        send_sems, recv_sems, local_sems = refs[2 * n:]
        x, y, c = _mesh_pos()
        me, sib = (x, y, c), (x, y, 1 - c)
        chips = [(1 - x, y), (x, 1 - y), (1 - x, 1 - y)]

        def copy(a, k, block, to, src=None):
            slot = o_refs[a].at[_dev_index(block)]
            return pltpu.make_async_remote_copy(
                src_ref=slot if src is None else src, dst_ref=slot,
                send_sem=send_sems.at[a * 7 + k], recv_sem=recv_sems.at[a * 7 + k],
                device_id=to, device_id_type=MESH)

        mine = [pltpu.make_async_copy(x_refs[a], o_refs[a].at[_dev_index(me)], local_sems.at[a]) for a in range(n)]
        for cp in mine:
            cp.start()
        first = []
        for a in range(n):
            first.append(copy(a, 0, me, sib, src=x_refs[a]))
            first += [copy(a, 1 + j, me, (*chip, c), src=x_refs[a]) for j, chip in enumerate(chips)]
        for cp in first:
            cp.start()
        passed = []
        for j, chip in enumerate(chips):
            for a in range(n):
                copy(a, 1 + j, (*chip, c), me).wait_recv()
                fwd = copy(a, 4 + j, (*chip, c), sib)
                fwd.start()
                passed.append(fwd)
        for a in range(n):
            copy(a, 0, sib, me).wait_recv()
            for j, chip in enumerate(chips):
                copy(a, 4 + j, (*chip, 1 - c), me).wait_recv()
        for cp in first + passed:
            cp.wait_send()
        for cp in mine:
            cp.wait()

    any_spec = pl.BlockSpec(memory_space=pl.ANY)
    return pl.pallas_call(
        body,
        name="gather_weights",
        out_shape=[_sds((N_DEV,) + a.shape, a.dtype) for a in xs],
        in_specs=[any_spec] * n,
        out_specs=[any_spec] * n,
        scratch_shapes=[pltpu.SemaphoreType.DMA((7 * n,)), pltpu.SemaphoreType.DMA((7 * n,)),
                        pltpu.SemaphoreType.DMA((n,))],
    )(*xs)


def _exchange_call(bs):
    n = len(bs)

    def body(*refs):
        b_refs, r_refs = refs[:n], refs[n:2 * n]
        send_sems, recv_sems, local_sems = refs[2 * n:]
        x, y, c = _mesh_pos()
        me_idx = _dev_index((x, y, c))
        mine = [pltpu.make_async_copy(b_refs[a].at[me_idx], r_refs[a].at[me_idx], local_sems.at[a]) for a in range(n)]
        for cp in mine:
            cp.start()
        sent = []
        for k, r in enumerate(RELS):
            peer = ((1 - x) if r[0] else x, (1 - y) if r[1] else y, (1 - c) if r[2] else c)
            pidx = _dev_index(peer)
            for a in range(n):
                out = pltpu.make_async_remote_copy(
                    src_ref=b_refs[a].at[pidx], dst_ref=r_refs[a].at[me_idx],
                    send_sem=send_sems.at[a * 7 + k], recv_sem=recv_sems.at[a * 7 + k],
                    device_id=peer, device_id_type=MESH)
                out.start()
                inc = pltpu.make_async_remote_copy(
                    src_ref=b_refs[a].at[pidx], dst_ref=r_refs[a].at[pidx],
                    send_sem=send_sems.at[a * 7 + k], recv_sem=recv_sems.at[a * 7 + k],
                    device_id=peer, device_id_type=MESH)
                sent.append((out, inc))
        for out, inc in sent:
            inc.wait_recv()
        for out, inc in sent:
            out.wait_send()
        for cp in mine:
            cp.wait()

    any_spec = pl.BlockSpec(memory_space=pl.ANY)
    return pl.pallas_call(
        body,
        name="exchange_grads",
        out_shape=[_sds(b.shape, b.dtype) for b in bs],
        in_specs=[any_spec] * n,
        out_specs=[any_spec] * n,
        scratch_shapes=[pltpu.SemaphoreType.DMA((7 * n,)), pltpu.SemaphoreType.DMA((7 * n,)),
                        pltpu.SemaphoreType.DMA((n,))],
    )(*bs)


def _proj_call(x2, w_t, tm):
    s_len = x2.shape[0]

    def body(x_ref, w_ref, qf_ref, kf_ref, vf_ref, fz_ref, qs_ref, ks_ref, vs_ref, sz_ref, fft_ref, vat_ref):
        xb = x_ref[...].astype(bf16)
        vt = lax.dot_general(w_ref[A_FV:A_FV + 512, :], xb, NT, preferred_element_type=f32)
        ones_row = jnp.where(lax.broadcasted_iota(jnp.int32, (HEAD_DIM, tm), 0) == 0, 1.0, 0.0).astype(bf16)
        for h in range(FOX_HEADS):
            vat_ref[h, 0:HEAD_DIM, :] = vt[h * HEAD_DIM:(h + 1) * HEAD_DIM, :].astype(bf16)
            vat_ref[h, HEAD_DIM:2 * HEAD_DIM, :] = ones_row

        def seg(off, width):
            return lax.dot_general(xb, w_ref[off:off + width, :], NT, preferred_element_type=f32)

        def put_heads(ref, acc, nheads):
            for h in range(nheads):
                ref[h] = acc[:, h * HEAD_DIM:(h + 1) * HEAD_DIM].astype(bf16)

        put_heads(qf_ref, seg(A_FQ, 512) * SCALE, FOX_HEADS)
        put_heads(kf_ref, seg(A_FK, 512), FOX_HEADS)
        put_heads(vf_ref, seg(A_FV, 512), FOX_HEADS)
        fz_ref[...] = seg(A_FZ, 512)
        put_heads(qs_ref, seg(A_SQ, 512) * SCALE, SWA_HEADS)
        put_heads(ks_ref, seg(A_SK, 128), SWA_KV_HEADS)
        put_heads(vs_ref, seg(A_SV, 128), SWA_KV_HEADS)
        sz_ref[...] = seg(A_SZ, 512)
        fft_ref[...] = seg(A_FF, 128).T[:FOX_HEADS, :]

    def heads(nh):
        return pl.BlockSpec((nh, tm, HEAD_DIM), lambda i: (0, i, 0))

    wide = pl.BlockSpec((tm, 512), lambda i: (i, 0))
    return pl.pallas_call(
        body,
        name="proj_fwd",
        grid=(s_len // tm,),
        in_specs=[pl.BlockSpec((tm, D_MODEL), lambda i: (i, 0)), pl.BlockSpec((A_W, D_MODEL), lambda i: (0, 0))],
        out_specs=[heads(8), heads(8), heads(8), wide, heads(8), heads(2), heads(2), wide,
                   pl.BlockSpec((FOX_HEADS, tm), lambda i: (0, i)),
                   pl.BlockSpec((FOX_HEADS, 2 * HEAD_DIM, tm), lambda i: (0, 0, i))],
        out_shape=[_sds((8, s_len, HEAD_DIM), bf16)] * 3 + [_sds((s_len, 512), f32), _sds((8, s_len, HEAD_DIM), bf16),
                   _sds((2, s_len, HEAD_DIM), bf16), _sds((2, s_len, HEAD_DIM), bf16), _sds((s_len, 512), f32),
                   _sds((FOX_HEADS, s_len), f32), _sds((FOX_HEADS, 2 * HEAD_DIM, s_len), bf16)],
        compiler_params=_params(("arbitrary",)),
    )(x2, w_t)


AUG = 2 * HEAD_DIM


def _augment_call(q, k, cum_row, tm):
    nh, s_len, _ = q.shape
    per_step = tm // FOX_T

    def body(q_ref, k_ref, c_ref, qat_ref, ka_ref, st_ref):
        c = c_ref[0]
        hi = c.astype(bf16).astype(f32)
        r1 = c - hi
        mid = r1.astype(bf16).astype(f32)
        lo = (r1 - mid).astype(bf16).astype(f32)
        row = lax.broadcasted_iota(jnp.int32, (HEAD_DIM, tm), 0)
        q_tail = jnp.where(row == 0, hi, jnp.where(row == 1, mid, jnp.where(row == 2, lo,
                           jnp.where(row < 6, 1.0, 0.0))))
        k_tail = jnp.where(row < 3, 1.0, jnp.where(row == 3, -hi, jnp.where(row == 4, -mid,
                           jnp.where(row == 5, -lo, 0.0))))
        qt = q_ref[0].astype(f32).T
        kt = k_ref[0].astype(f32).T
        qat_ref[0, 0:HEAD_DIM, :] = qt.astype(bf16)
        qat_ref[0, HEAD_DIM:AUG, :] = q_tail.astype(bf16)
        ka_ref[0] = jnp.concatenate([k_ref[0], k_tail.T.astype(bf16)], axis=1)
        qn2 = jnp.sum(qt * qt, axis=0, keepdims=True)
        kn2 = jnp.sum(kt * kt, axis=0, keepdims=True)
        sd = jnp.sum(qt * kt, axis=0, keepdims=True)
        srow = lax.broadcasted_iota(jnp.int32, (8, LANES), 0)
        for part in range(per_step):
            sl = slice(part * FOX_T, (part + 1) * FOX_T)
            vals = [jnp.sqrt(jnp.max(qn2[:, sl], axis=1, keepdims=True)),
                    jnp.sqrt(jnp.max(kn2[:, sl], axis=1, keepdims=True)),
                    jnp.min(sd[:, sl], axis=1, keepdims=True),
                    jnp.max(c[:, sl], axis=1, keepdims=True), jnp.min(c[:, sl], axis=1, keepdims=True)]
            out = jnp.zeros((8, LANES), f32)
            for r, val in enumerate(vals):
                out = jnp.where(srow == r, val, out)
            st_ref[0, part] = out

    tile = pl.BlockSpec((1, tm, HEAD_DIM), lambda h, i: (h, i, 0))
    return pl.pallas_call(
        body,
        name="fox_augment",
        grid=(nh, s_len // tm),
        in_specs=[tile, tile, pl.BlockSpec((1, 1, tm), lambda h, i: (h, 0, i))],
        out_specs=[pl.BlockSpec((1, AUG, tm), lambda h, i: (h, 0, i)),
                   pl.BlockSpec((1, tm, AUG), lambda h, i: (h, i, 0)),
                   pl.BlockSpec((1, per_step, 8, LANES), lambda h, i: (h, i, 0, 0))],
        out_shape=[_sds((nh, AUG, s_len), bf16), _sds((nh, s_len, AUG), bf16),
                   _sds((nh, s_len // FOX_T, 8, LANES), f32)],
        compiler_params=_params(("arbitrary", "arbitrary")),
    )(q, k, cum_row)


EXP_ZERO_GAP = 110.0


def _fox_prune_tables(stats):
    s = stats[:, :, :, 0]
    qn, kn, sd, cmx, cmn = (s[:, :, r] for r in range(5))
    nt = s.shape[1]
    bound = qn[:, :, None] * kn[:, None, :] + (cmx[:, :, None] - cmn[:, None, :])
    margin = 2.0 + 1e-5 * (jnp.abs(cmx)[:, :, None] + jnp.abs(cmn)[:, None, :])
    qi = lax.broadcasted_iota(jnp.int32, (nt, nt), 0)
    kj = lax.broadcasted_iota(jnp.int32, (nt, nt), 1)
    skip = (bound + margin < sd[:, :, None] - EXP_ZERO_GAP) & (kj < qi)[None]
    first = jnp.sum(jnp.cumprod(skip.astype(jnp.int32), axis=2), axis=2)
    first = lax.cummin(first, axis=1, reverse=True)
    last = jnp.sum((first[:, :, None] <= kj[None]).astype(jnp.int32), axis=1) - 1
    return first.reshape(-1).astype(jnp.int32), last.reshape(-1).astype(jnp.int32)


CUM_CHUNK = 512


def _cum_call(fft, bf_col):
    s_len = fft.shape[1]
    ch = CUM_CHUNK

    def body(f_ref, b_ref, cum_ref, sg_ref):
        r = lax.broadcasted_iota(jnp.int32, (ch, ch), 0)
        c = lax.broadcasted_iota(jnp.int32, (ch, ch), 1)
        upper = (r <= c).astype(f32)
        carry = jnp.zeros((FOX_HEADS, 1), f32)
        for n in range(s_len // ch):
            z = f_ref[:, n * ch:(n + 1) * ch] + b_ref[...]
            logf = jnp.minimum(z, 0.0) - jnp.log1p(jnp.exp(-jnp.abs(z)))
            sg_ref[:, n * ch:(n + 1) * ch] = 1.0 / (1.0 + jnp.exp(z))
            cs = jnp.dot(logf, upper, precision=HIGHEST, preferred_element_type=f32) + carry
            cum_ref[:, n * ch:(n + 1) * ch] = cs
            carry = cs[:, ch - 1:ch]

    return pl.pallas_call(
        body,
        name="fox_cum_fwd",
        out_shape=[_sds((FOX_HEADS, s_len), f32)] * 2,
        compiler_params=_params(),
    )(fft, bf_col)


def _cum_bwd_call(dcq, dck, sg):
    s_len = sg.shape[1]
    ch = CUM_CHUNK
    nch = s_len // ch

    def body(q_ref, k_ref, sg_ref, dff_ref, dbf_ref):
        r = lax.broadcasted_iota(jnp.int32, (ch, ch), 0)
        c = lax.broadcasted_iota(jnp.int32, (ch, ch), 1)
        lower = (r >= c).astype(f32)
        dff_ref[...] = jnp.zeros_like(dff_ref)
        carry = jnp.zeros((FOX_HEADS, 1), f32)
        total = jnp.zeros((FOX_HEADS, 1), f32)
        for n in reversed(range(nch)):
            sl = slice(n * ch, (n + 1) * ch)
            dcum = q_ref[:, sl] - k_ref[:, sl]
            rs = jnp.dot(dcum, lower, precision=HIGHEST, preferred_element_type=f32) + carry
            carry = rs[:, 0:1]
            dff = rs * sg_ref[:, sl]
            dff_ref[0:FOX_HEADS, sl] = dff
            total = total + jnp.sum(dff, axis=1, keepdims=True)
        dbf_ref[...] = jnp.broadcast_to(total, (FOX_HEADS, 128))

    return pl.pallas_call(
        body,
        name="fox_cum_bwd",
        out_shape=[_sds((128, s_len), f32), _sds((FOX_HEADS, 128), f32)],
        compiler_params=_params(),
    )(dcq, dck, sg)


FOX_T = 512
LANES = 128


def _causal_keep(t):
    return lax.broadcasted_iota(jnp.int32, (t, t), 0) <= lax.broadcasted_iota(jnp.int32, (t, t), 1)


def _fox_fwd_call(qat, ka, vat, first_tile):
    nh, s_len, _ = ka.shape
    t = FOX_T
    nq = s_len // t

    def body(first_ref, qat_ref, ka_ref, vat_ref, o_ref, lse_ref, s0, s1, p0, p1, a0, a1, m_ref, acc_ref):
        i = pl.program_id(1)
        j0 = first_ref[pl.program_id(0) * nq + i]
        cnt = i - j0
        qt = qat_ref[0]
        m_ref[...] = jnp.full((1, t), NEG_INF, f32)
        acc_ref[...] = jnp.zeros((AUG, t), f32)
        bufs = ((s0, p0, a0), (s1, p1, a1))

        def scores(j, b, masked):
            kb = ka_ref[0, pl.ds(pl.multiple_of(j * t, t), t), :]
            st = jnp.dot(kb, qt, preferred_element_type=f32)
            if masked:
                st = jnp.where(_causal_keep(t), st, NEG_INF)
            bufs[b][0][...] = st

        def softmax(b):
            s_ref, p_ref, a_ref = bufs[b]
            for c in range(t // LANES):
                cols = slice(c * LANES, (c + 1) * LANES)
                m_old = m_ref[:, cols]
                m_new = jnp.maximum(m_old, jnp.max(s_ref[:, cols], axis=0, keepdims=True))
                m_ref[:, cols] = m_new
                a_ref[:, cols] = jnp.exp(m_old - m_new)
                p_ref[:, cols] = jnp.exp(s_ref[:, cols] - m_new).astype(bf16)

        def accum(j, b):
            vt = vat_ref[0, :, pl.ds(pl.multiple_of(j * t, t), t)]
            acc_ref[...] = bufs[b][2][...] * acc_ref[...] + jnp.dot(vt, bufs[b][1][...], preferred_element_type=f32)

        def key_tile(n):
            return jnp.where(n == 0, i, j0 + n - 1)

        def step(n, b):
            scores(key_tile(n), b, False)
            softmax(1 - b)
            accum(key_tile(n - 2), b)

        @pl.when(cnt == 0)
        def _():
            scores(i, 0, True)
            softmax(0)
            accum(i, 0)

        @pl.when(cnt >= 1)
        def _():
            scores(i, 0, True)
            scores(j0, 1, False)
            softmax(0)

        def two_steps(d, _):
            n = 2 + 2 * d
            step(n, 0)
            step(n + 1, 1)
            return 0

        lax.fori_loop(0, (cnt - 1) // 2, two_steps, 0)

        @pl.when((cnt >= 2) & (cnt % 2 == 0))
        def _():
            step(cnt, 0)
            softmax(0)
            accum(i - 2, 1)
            accum(i - 1, 0)

        @pl.when(cnt % 2 == 1)
        def _():
            softmax(1)
            accum(key_tile(cnt - 1), 0)
            accum(i - 1, 1)

        l = acc_ref[HEAD_DIM:HEAD_DIM + 1, :]
        o_ref[0] = acc_ref[0:HEAD_DIM, :] / l
        lse_ref[0] = m_ref[...] + jnp.log(l)

    return pl.pallas_call(
        body,
        name="fox_fwd",
        grid=(nh, nq),
        in_specs=[pl.BlockSpec(memory_space=pltpu.SMEM),
                  pl.BlockSpec((1, AUG, t), lambda h, i: (h, 0, i)),
                  pl.BlockSpec((1, s_len, AUG), lambda h, i: (h, 0, 0)),
                  pl.BlockSpec((1, AUG, s_len), lambda h, i: (h, 0, 0))],
        out_specs=[pl.BlockSpec((1, HEAD_DIM, t), lambda h, i: (h, 0, i)),
                   pl.BlockSpec((1, 1, t), lambda h, i: (h, 0, i))],
        out_shape=[_sds((nh, HEAD_DIM, s_len), f32), _sds((nh, 1, s_len), f32)],
        scratch_shapes=[pltpu.VMEM((t, t), f32), pltpu.VMEM((t, t), f32), pltpu.VMEM((t, t), bf16),
                        pltpu.VMEM((t, t), bf16), pltpu.VMEM((1, t), f32), pltpu.VMEM((1, t), f32),
                        pltpu.VMEM((1, t), f32), pltpu.VMEM((AUG, t), f32)],
        compiler_params=_params(("arbitrary", "arbitrary")),
    )(first_tile, qat, ka, vat)


SWA_TS = 512


def _swa_bias_call(rel_bias, bucket):
    def body(rb_ref, bk_ref, b_ref, b0_ref):
        bk = bk_ref[...]
        col = lax.broadcasted_iota(jnp.int32, (BLOCK, 2 * BLOCK), 1)
        for h in range(SWA_HEADS):
            acc = jnp.full((BLOCK, 2 * BLOCK), NEG_INF, f32)
            for b in range(NUM_BUCKETS):
                acc = jnp.where(bk == b, rb_ref[b, h], acc)
            g, hh = divmod(h, SWA_GROUP)
            b_ref[g, hh * BLOCK:(hh + 1) * BLOCK, :] = acc
            b0_ref[g, hh * BLOCK:(hh + 1) * BLOCK, :] = jnp.where(col < BLOCK, NEG_INF, acc)

    return pl.pallas_call(
        body,
        name="swa_bias",
        in_specs=[pl.BlockSpec(memory_space=pltpu.SMEM), pl.BlockSpec(memory_space=pltpu.VMEM)],
        out_shape=[_sds((SWA_KV_HEADS, SWA_GROUP * BLOCK, 2 * BLOCK), f32)] * 2,
        compiler_params=_params(),
    )(rel_bias, bucket)


def _swa_bias_bwd_call(dbias, bucket):
    def body(d_ref, bk_ref, o_ref):
        bk = bk_ref[...]
        row = lax.broadcasted_iota(jnp.int32, (NUM_BUCKETS, 128), 0)
        col = lax.broadcasted_iota(jnp.int32, (NUM_BUCKETS, 128), 1)
        out = jnp.zeros((NUM_BUCKETS, 128), f32)
        for h in range(SWA_HEADS):
            g, hh = divmod(h, SWA_GROUP)
            d = d_ref[g, hh * BLOCK:(hh + 1) * BLOCK, :]
            for b in range(NUM_BUCKETS):
                val = jnp.sum(jnp.sum(jnp.where(bk == b, d, 0.0), axis=1, keepdims=True), axis=0, keepdims=True)
                out = jnp.where((row == b) & (col == h), val, out)
        o_ref[...] = out

    return pl.pallas_call(
        body,
        name="swa_bias_bwd",
        out_shape=_sds((NUM_BUCKETS, 128), f32),
        compiler_params=_params(),
    )(dbias, bucket)


def _swa_specs(ts):
    nb = ts // BLOCK
    qspec = pl.BlockSpec((SWA_HEADS, ts, HEAD_DIM), lambda n: (0, n, 0))
    cur = pl.BlockSpec((SWA_KV_HEADS, ts, HEAD_DIM), lambda n: (0, n, 0))
    prev = pl.BlockSpec((SWA_KV_HEADS, BLOCK, HEAD_DIM), lambda n: (0, jnp.maximum(n * nb - 1, 0), 0))
    return qspec, cur, prev


def _sink_col(sink_ref, g):
    return jnp.concatenate([jnp.full((BLOCK, 1), sink_ref[g * SWA_GROUP + hh], f32) for hh in range(SWA_GROUP)], axis=0)


def _swa_fwd_call(q, k, v, bias, bias0, sink):
    s_len = q.shape[1]
    ts = SWA_TS
    nb = ts // BLOCK

    def body(q_ref, kc_ref, kp_ref, vc_ref, vp_ref, b_ref, b0_ref, sink_ref, o_ref, lse_ref):
        first = pl.program_id(0) == 0
        for g in range(SWA_KV_HEADS):
            kall = jnp.concatenate([kp_ref[g], kc_ref[g]], axis=0)
            vall = jnp.concatenate([vp_ref[g], vc_ref[g]], axis=0)
            sink_c = _sink_col(sink_ref, g)
            for b in range(nb):
                rows = slice(b * BLOCK, (b + 1) * BLOCK)
                qg = jnp.concatenate([q_ref[g * SWA_GROUP + hh, rows, :] for hh in range(SWA_GROUP)], axis=0)
                kcat = kall[b * BLOCK:(b + 2) * BLOCK]
                vcat = vall[b * BLOCK:(b + 2) * BLOCK]
                bias_b = b_ref[g]
                if b == 0:
                    bias_b = jnp.where(first, b0_ref[g], bias_b)
                s = lax.dot_general(qg, kcat, NT, preferred_element_type=f32) + bias_b
                m = jnp.maximum(jnp.max(s, axis=1, keepdims=True), sink_c)
                p = jnp.exp(s - m)
                l = jnp.sum(p, axis=1, keepdims=True) + jnp.exp(sink_c - m)
                o = jnp.dot(p.astype(bf16), vcat, preferred_element_type=f32) / l
                lse = m + jnp.log(l)
                for hh in range(SWA_GROUP):
                    o_ref[g * SWA_GROUP + hh, rows, :] = o[hh * BLOCK:(hh + 1) * BLOCK]
                    lse_ref[g * SWA_GROUP + hh, rows, :] = lse[hh * BLOCK:(hh + 1) * BLOCK]

    qspec, cur, prev = _swa_specs(ts)
    bspec = pl.BlockSpec((SWA_KV_HEADS, SWA_GROUP * BLOCK, 2 * BLOCK), lambda n: (0, 0, 0))
    return pl.pallas_call(
        body,
        name="swa_fwd",
        grid=(s_len // ts,),
        in_specs=[qspec, cur, prev, cur, prev, bspec, bspec, pl.BlockSpec(memory_space=pltpu.SMEM)],
        out_specs=[pl.BlockSpec((SWA_HEADS, ts, HEAD_DIM), lambda n: (0, n, 0)),
                   pl.BlockSpec((SWA_HEADS, ts, 1), lambda n: (0, n, 0))],
        out_shape=[_sds((SWA_HEADS, s_len, HEAD_DIM), f32), _sds((SWA_HEADS, s_len, 1), f32)],
        compiler_params=_params(("arbitrary",)),
    )(q, k, k, v, v, bias, bias0, sink)


def _head_selector():
    sel = np.zeros((512, 128), np.float32)
    for h in range(8):
        sel[h * HEAD_DIM:(h + 1) * HEAD_DIM, h] = 1.0
    return sel


def _post_call(of, fz, osw, sz, x2, tgt, wo, ln_g, ln_b, sel, tm):
    s_len = x2.shape[0]

    def body(of_ref, fz_ref, os_ref, sz_ref, x_ref, t_ref, wo_ref, g_ref, b_ref, sel_ref,
             dh_ref, dof_ref, dfz_ref, dos_ref, dsz_ref, dlf_ref, dls_ref, dwo_ref, dg_ref, db_ref, loss_ref):
        n = pl.program_id(0)

        @pl.when(n == 0)
        def _():
            dwo_ref[...] = jnp.zeros_like(dwo_ref)
            dg_ref[...] = jnp.zeros_like(dg_ref)
            db_ref[...] = jnp.zeros_like(db_ref)
            loss_ref[...] = jnp.zeros_like(loss_ref)

        o_f = of_ref[...].T
        o_s = jnp.concatenate([os_ref[h] for h in range(SWA_HEADS)], axis=1)
        fz = fz_ref[...]
        sz = sz_ref[...]
        sg_f = jax.nn.sigmoid(fz)
        sg_s = jax.nn.sigmoid(sz)
        silu_f = fz * sg_f
        silu_s = sz * sg_s
        mixed = jnp.concatenate([o_f * silu_f, o_s * silu_s], axis=1).astype(bf16)
        y = jnp.dot(mixed, wo_ref[...], preferred_element_type=f32)
        h = ALPHA * x_ref[...] + y
        mu = jnp.mean(h, axis=1, keepdims=True)
        hc = h - mu
        var = jnp.mean(hc * hc, axis=1, keepdims=True)
        rstd = lax.rsqrt(var + LN_EPS)
        xhat = hc * rstd
        gam = g_ref[...]
        out = xhat * gam + b_ref[...]
        err = out - t_ref[...]
        tok_loss = jnp.mean(err * err, axis=1, keepdims=True)
        loss_ref[...] += 0.5 * jnp.sum(tok_loss, axis=0, keepdims=True)
        dout = err * (1.0 / D_MODEL)
        dg_ref[...] += jnp.sum(dout * xhat, axis=0, keepdims=True)
        db_ref[...] += jnp.sum(dout, axis=0, keepdims=True)
        dxh = dout * gam
        m1 = jnp.mean(dxh, axis=1, keepdims=True)
        m2 = jnp.mean(dxh * xhat, axis=1, keepdims=True)
        dh = rstd * (dxh - m1 - xhat * m2)
        dh_ref[...] = dh
        dyb = dh.astype(bf16)
        dwo_ref[...] += lax.dot_general(mixed, dyb, TN, preferred_element_type=f32)
        dmix = lax.dot_general(dyb, wo_ref[...], NT, preferred_element_type=f32)
        dm_f = dmix[:, :512]
        dm_s = dmix[:, 512:]
        do_f = dm_f * silu_f
        do_s = dm_s * silu_s
        dfz_ref[...] = (dm_f * o_f * (sg_f * (1.0 + fz * (1.0 - sg_f)))).astype(bf16)
        dsz_ref[...] = (dm_s * o_s * (sg_s * (1.0 + sz * (1.0 - sg_s)))).astype(bf16)
        dof_ref[...] = do_f.T.astype(bf16)
        for hd in range(SWA_HEADS):
            dos_ref[hd] = do_s[:, hd * HEAD_DIM:(hd + 1) * HEAD_DIM].astype(bf16)
        sel_m = sel_ref[...]
        dl_f = jnp.dot(do_f * o_f, sel_m, precision=HIGHEST, preferred_element_type=f32)
        dl_s = jnp.dot(do_s * o_s, sel_m, precision=HIGHEST, preferred_element_type=f32)
        dlf_ref[...] = dl_f.T[:FOX_HEADS, :]
        dls_ref[...] = dl_s

    heads_f32 = pl.BlockSpec((8, tm, HEAD_DIM), lambda n: (0, n, 0))
    half = pl.BlockSpec((tm, 512), lambda n: (n, 0))
    fullw = pl.BlockSpec((tm, D_MODEL), lambda n: (n, 0))
    vec = pl.BlockSpec((1, D_MODEL), lambda n: (0, 0))
    return pl.pallas_call(
        body,
        name="post_fwd_bwd",
        grid=(s_len // tm,),
        in_specs=[pl.BlockSpec((512, tm), lambda n: (0, n)), half, heads_f32, half, fullw, fullw,
                  pl.BlockSpec((D_MODEL, D_MODEL), lambda n: (0, 0)), vec, vec,
                  pl.BlockSpec((512, 128), lambda n: (0, 0))],
        out_specs=[fullw, pl.BlockSpec((512, tm), lambda n: (0, n)), half, heads_f32, half,
                   pl.BlockSpec((FOX_HEADS, tm), lambda n: (0, n)), pl.BlockSpec((tm, 128), lambda n: (n, 0)),
                   pl.BlockSpec((D_MODEL, D_MODEL), lambda n: (0, 0)), vec, vec,
                   pl.BlockSpec((1, 1), lambda n: (0, 0))],
        out_shape=[_sds((s_len, D_MODEL), f32), _sds((512, s_len), bf16), _sds((s_len, 512), bf16),
                   _sds((8, s_len, HEAD_DIM), bf16), _sds((s_len, 512), bf16),
                   _sds((FOX_HEADS, s_len), f32), _sds((s_len, 128), f32),
                   _sds((D_MODEL, D_MODEL), f32), _sds((1, D_MODEL), f32), _sds((1, D_MODEL), f32),
                   _sds((1, 1), f32)],
        compiler_params=_params(("arbitrary",)),
    )(of, fz, osw, sz, x2, tgt, wo, ln_g, ln_b, sel)


def _fox_bwd_call(ka, v, qat, dot, lse_row, dl_row, last_tile):
    nh, s_len, _ = ka.shape
    t = FOX_T
    nt = s_len // t
    ck_slot = HEAD_DIM + 3
    cq_slot = HEAD_DIM

    def body(last_ref, ka_ref, v_ref, qat_ref, dot_ref, lse_ref, dl_ref,
             dk_ref, dv_ref, dq_ref, dcq_ref, dck_ref, dqt_s, dkat_s, dvt_s, p0, p1, ds0, ds1):
        j = pl.program_id(1)

        @pl.when(j == 0)
        def _():
            dqt_s[...] = jnp.zeros_like(dqt_s)

        kb = ka_ref[0]
        vb = v_ref[0]
        kt = kb.astype(f32).T.astype(bf16)
        dkat_s[...] = jnp.zeros_like(dkat_s)
        dvt_s[...] = jnp.zeros_like(dvt_s)

        pbuf, dsbuf = (p0, p1), (ds0, ds1)

        def probs(i, b, masked):
            cols = pl.ds(pl.multiple_of(i * t, t), t)
            st = jnp.dot(kb, qat_ref[0, :, cols], preferred_element_type=f32)
            if masked:
                st = jnp.where(_causal_keep(t), st, NEG_INF)
            pt = jnp.exp(st - lse_ref[0, :, cols])
            dpt = jnp.dot(vb, dot_ref[0, :, cols], preferred_element_type=f32)
            pbuf[b][...] = pt.astype(bf16)
            dsbuf[b][...] = (pt * (dpt - dl_ref[0, :, cols])).astype(bf16)

        def grads(i, b):
            cols = pl.ds(pl.multiple_of(i * t, t), t)
            dvt_s[...] += lax.dot_general(dot_ref[0, :, cols], pbuf[b][...], NT, preferred_element_type=f32)
            dkat_s[...] += lax.dot_general(qat_ref[0, :, cols], dsbuf[b][...], NT, preferred_element_type=f32)
            dqt_s[:, cols] += jnp.dot(kt, dsbuf[b][...], preferred_element_type=f32)

        i_last = last_ref[pl.program_id(0) * nt + j]
        rest = i_last - j
        probs(j, 0, True)

        def two_steps(d, _):
            i = j + 1 + 2 * d
            probs(i, 1, False)
            grads(i - 1, 0)
            probs(i + 1, 0, False)
            grads(i, 1)
            return 0

        lax.fori_loop(0, rest // 2, two_steps, 0)

        @pl.when(rest % 2 == 1)
        def _():
            probs(i_last, 1, False)
            grads(i_last - 1, 0)
            grads(i_last, 1)

        @pl.when(rest % 2 == 0)
        def _():
            grads(i_last, 0)

        dk_ref[0] = dkat_s[0:HEAD_DIM, :].T.astype(bf16)
        dv_ref[0] = dvt_s[...].T.astype(bf16)
        dck_ref[0] = dkat_s[ck_slot:ck_slot + 1, :]

        @pl.when(j == nt - 1)
        def _():
            for cidx in range(nt):
                sl = slice(cidx * t, (cidx + 1) * t)
                dq_ref[0, sl, :] = (dqt_s[0:HEAD_DIM, sl].T * SCALE).astype(bf16)
            dcq_ref[0] = dqt_s[cq_slot:cq_slot + 1, :]

    ktile = pl.BlockSpec((1, t, AUG), lambda h, j: (h, j, 0))
    tile = pl.BlockSpec((1, t, HEAD_DIM), lambda h, j: (h, j, 0))
    full = pl.BlockSpec((1, s_len, HEAD_DIM), lambda h, j: (h, 0, 0))
    rowv = pl.BlockSpec((1, 1, s_len), lambda h, j: (h, 0, 0))
    return pl.pallas_call(
        body,
        name="fox_bwd",
        grid=(nh, nt),
        in_specs=[pl.BlockSpec(memory_space=pltpu.SMEM), ktile, tile,
                  pl.BlockSpec((1, AUG, s_len), lambda h, j: (h, 0, 0)),
                  pl.BlockSpec((1, HEAD_DIM, s_len), lambda h, j: (h, 0, 0)), rowv, rowv],
        out_specs=[tile, tile, full, rowv, pl.BlockSpec((1, 1, t), lambda h, j: (h, 0, j))],
        out_shape=[_sds((nh, s_len, HEAD_DIM), bf16)] * 3 + [_sds((nh, 1, s_len), f32), _sds((nh, 1, s_len), f32)],
        scratch_shapes=[pltpu.VMEM((AUG, s_len), f32), pltpu.VMEM((AUG, t), f32), pltpu.VMEM((HEAD_DIM, t), f32)]
                       + [pltpu.VMEM((t, t), bf16)] * 4,
        compiler_params=_params(("arbitrary", "arbitrary")),
    )(last_tile, ka, v, qat, dot, lse_row, dl_row)


def _swa_bwd_call(q, k, v, do, lse, dl, bias, bias0, sink):
    s_len = q.shape[1]
    ts = SWA_TS
    nb = ts // BLOCK
    nsteps = s_len // ts

    def body(q_ref, kc_ref, kp_ref, vc_ref, vp_ref, do_ref, lse_ref, dl_ref, b_ref, b0_ref, sink_ref,
             dq_ref, dk_ref, dv_ref, dbias_ref, dsink_ref, dk_s, dv_s, tail_k, tail_v, sk_s):
        n = pl.program_id(0)

        @pl.when(n == 0)
        def _():
            dbias_ref[...] = jnp.zeros_like(dbias_ref)
            sk_s[...] = jnp.zeros_like(sk_s)

        @pl.when(n < nsteps)
        def _():
            first = n == 0
            dk_s[...] = jnp.zeros_like(dk_s)
            dv_s[...] = jnp.zeros_like(dv_s)
            for g in range(SWA_KV_HEADS):
                kall = jnp.concatenate([kp_ref[g], kc_ref[g]], axis=0)
                vall = jnp.concatenate([vp_ref[g], vc_ref[g]], axis=0)
                sink_c = _sink_col(sink_ref, g)
                for b in range(nb):
                    rows = slice(b * BLOCK, (b + 1) * BLOCK)
                    heads = [g * SWA_GROUP + hh for hh in range(SWA_GROUP)]
                    qg = jnp.concatenate([q_ref[h, rows, :] for h in heads], axis=0)
                    dog = jnp.concatenate([do_ref[h, rows, :] for h in heads], axis=0)
                    lse_c = jnp.concatenate([lse_ref[h, rows, :] for h in heads], axis=0)
                    dl_c = jnp.concatenate([dl_ref[rows, h:h + 1] for h in heads], axis=0)
                    kcat = kall[b * BLOCK:(b + 2) * BLOCK]
                    vcat = vall[b * BLOCK:(b + 2) * BLOCK]
                    bias_b = b_ref[g]
                    if b == 0:
                        bias_b = jnp.where(first, b0_ref[g], bias_b)
                    s = lax.dot_general(qg, kcat, NT, preferred_element_type=f32) + bias_b
                    p = jnp.exp(s - lse_c)
                    dp = lax.dot_general(dog, vcat, NT, preferred_element_type=f32)
                    ds = p * (dp - dl_c)
                    dsb = ds.astype(bf16)
                    dqg = jnp.dot(dsb, kcat, preferred_element_type=f32) * SCALE
                    for hh, h in enumerate(heads):
                        dq_ref[h, rows, :] = dqg[hh * BLOCK:(hh + 1) * BLOCK].astype(bf16)
                    win = slice(b * BLOCK, (b + 2) * BLOCK)
                    dk_s[g, win, :] += lax.dot_general(dsb, qg, TN, preferred_element_type=f32)
                    dv_s[g, win, :] += lax.dot_general(p.astype(bf16), dog, TN, preferred_element_type=f32)
                    dbias_ref[g] += ds
                    sk_s[g] += -jnp.exp(sink_c - lse_c) * dl_c

        @pl.when(n > 0)
        def _():
            last = slice(ts - BLOCK, ts)
            for g in range(SWA_KV_HEADS):
                add_k = jnp.where(n < nsteps, dk_s[g, 0:BLOCK, :], 0.0)
                add_v = jnp.where(n < nsteps, dv_s[g, 0:BLOCK, :], 0.0)
                dk_ref[g, 0:ts - BLOCK, :] = tail_k[g, 0:ts - BLOCK, :].astype(bf16)
                dv_ref[g, 0:ts - BLOCK, :] = tail_v[g, 0:ts - BLOCK, :].astype(bf16)
                dk_ref[g, last, :] = (tail_k[g, last, :] + add_k).astype(bf16)
                dv_ref[g, last, :] = (tail_v[g, last, :] + add_v).astype(bf16)

        @pl.when(n < nsteps)
        def _():
            tail_k[...] = dk_s[:, BLOCK:, :]
            tail_v[...] = dv_s[:, BLOCK:, :]

        @pl.when(n == nsteps)
        def _():
            row = lax.broadcasted_iota(jnp.int32, (SWA_HEADS, 128), 0)
            out = jnp.zeros((SWA_HEADS, 128), f32)
            for h in range(SWA_HEADS):
                g, hh = divmod(h, SWA_GROUP)
                val = jnp.sum(sk_s[g, hh * BLOCK:(hh + 1) * BLOCK, :], axis=0, keepdims=True)
                out = jnp.where(row == h, val, out)
            dsink_ref[...] = out

    last_step = nsteps - 1

    def cl(n):
        return jnp.minimum(n, last_step)

    qspec = pl.BlockSpec((SWA_HEADS, ts, HEAD_DIM), lambda n: (0, cl(n), 0))
    cur = pl.BlockSpec((SWA_KV_HEADS, ts, HEAD_DIM), lambda n: (0, cl(n), 0))
    prev = pl.BlockSpec((SWA_KV_HEADS, BLOCK, HEAD_DIM), lambda n: (0, jnp.maximum(cl(n) * nb - 1, 0), 0))
    lsespec = pl.BlockSpec((SWA_HEADS, ts, 1), lambda n: (0, cl(n), 0))
    dlspec = pl.BlockSpec((ts, 128), lambda n: (cl(n), 0))
    bspec = pl.BlockSpec((SWA_KV_HEADS, SWA_GROUP * BLOCK, 2 * BLOCK), lambda n: (0, 0, 0))
    kvout = pl.BlockSpec((SWA_KV_HEADS, ts, HEAD_DIM), lambda n: (0, jnp.maximum(n - 1, 0), 0))
    return pl.pallas_call(
        body,
        name="swa_bwd",
        grid=(nsteps + 1,),
        in_specs=[qspec, cur, prev, cur, prev, qspec, lsespec, dlspec, bspec, bspec,
                  pl.BlockSpec(memory_space=pltpu.SMEM)],
        out_specs=[qspec, kvout, kvout, bspec, pl.BlockSpec((SWA_HEADS, 128), lambda n: (0, 0))],
        out_shape=[_sds((SWA_HEADS, s_len, HEAD_DIM), bf16), _sds((SWA_KV_HEADS, s_len, HEAD_DIM), bf16),
                   _sds((SWA_KV_HEADS, s_len, HEAD_DIM), bf16),
                   _sds((SWA_KV_HEADS, SWA_GROUP * BLOCK, 2 * BLOCK), f32), _sds((SWA_HEADS, 128), f32)],
        scratch_shapes=[pltpu.VMEM((SWA_KV_HEADS, ts + BLOCK, HEAD_DIM), f32),
                        pltpu.VMEM((SWA_KV_HEADS, ts + BLOCK, HEAD_DIM), f32),
                        pltpu.VMEM((SWA_KV_HEADS, ts, HEAD_DIM), f32),
                        pltpu.VMEM((SWA_KV_HEADS, ts, HEAD_DIM), f32),
                        pltpu.VMEM((SWA_KV_HEADS, SWA_GROUP * BLOCK, 1), f32)],
        compiler_params=_params(("arbitrary",)),
    )(q, k, k, v, v, do, lse, dl, bias, bias0, sink)


def _dx_call(dh, dqf, dkf, dvf, dfz, dqs, dks, dvs, dsz, dfft, w_t, tm):
    s_len = dh.shape[0]

    def body(dh_ref, dqf_ref, dkf_ref, dvf_ref, dfz_ref, dqs_ref, dks_ref, dvs_ref, dsz_ref, dfft_ref, w_ref,
             dx_ref, dp_ref):
        def cat(ref, nheads):
            return jnp.concatenate([ref[h] for h in range(nheads)], axis=1)

        dp = jnp.concatenate([cat(dqf_ref, 8), cat(dkf_ref, 8), cat(dvf_ref, 8), dfz_ref[...], cat(dqs_ref, 8),
                              cat(dks_ref, 2), cat(dvs_ref, 2), dsz_ref[...], dfft_ref[...].T.astype(bf16)], axis=1)
        dp_ref[...] = dp
        dx_ref[...] = ALPHA * dh_ref[...] + jnp.dot(dp, w_ref[...], preferred_element_type=f32)

    def heads(nh):
        return pl.BlockSpec((nh, tm, HEAD_DIM), lambda i: (0, i, 0))

    half = pl.BlockSpec((tm, 512), lambda i: (i, 0))
    fullw = pl.BlockSpec((tm, D_MODEL), lambda i: (i, 0))
    return pl.pallas_call(
        body,
        name="dx_bwd",
        grid=(s_len // tm,),
        in_specs=[fullw, heads(8), heads(8), heads(8), half, heads(8), heads(2), heads(2), half,
                  pl.BlockSpec((128, tm), lambda i: (0, i)), pl.BlockSpec((A_W, D_MODEL), lambda i: (0, 0))],
        out_specs=[fullw, pl.BlockSpec((tm, A_W), lambda i: (i, 0))],
        out_shape=[_sds((s_len, D_MODEL), f32), _sds((s_len, A_W), bf16)],
        compiler_params=_params(("arbitrary",)),
    )(dh, dqf, dkf, dvf, dfz, dqs, dks, dvs, dsz, dfft, w_t)


DW_ROWS = 1152


def _dw_call(x2, dproj, tm):
    s_len = x2.shape[0]
    nt = s_len // tm

    def body(x_ref, dp_ref, dw_ref):
        @pl.when(pl.program_id(1) == 0)
        def _():
            dw_ref[...] = jnp.zeros_like(dw_ref)

        dw_ref[...] += lax.dot_general(dp_ref[...], x_ref[...].astype(bf16), TN, preferred_element_type=f32)

    return pl.pallas_call(
        body,
        name="dw_in_bwd",
        grid=(A_W // DW_ROWS, nt),
        in_specs=[pl.BlockSpec((tm, D_MODEL), lambda c, i: (i, 0)), pl.BlockSpec((tm, DW_ROWS), lambda c, i: (i, c))],
        out_specs=pl.BlockSpec((DW_ROWS, D_MODEL), lambda c, i: (c, 0)),
        out_shape=_sds((A_W, D_MODEL), f32),
        compiler_params=_params(("arbitrary", "arbitrary")),
    )(x2, dproj)


def _adam_call(recv, w, m, v, tc, name):
    rows, cols = w.shape

    def body(r_ref, w_ref, m_ref, v_ref, g_ref, d_ref, mo_ref, vo_ref):
        g = r_ref[0].astype(f32)
        for p in range(1, N_DEV):
            g = g + r_ref[p].astype(f32)
        mn = ADAM_B1 * m_ref[...] + (1.0 - ADAM_B1) * g
        vn = ADAM_B2 * v_ref[...] + (1.0 - ADAM_B2) * (g * g)
        m_hat = mn / (1.0 - ADAM_B1 ** ADAM_STEP)
        v_hat = vn / (1.0 - ADAM_B2 ** ADAM_STEP)
        g_ref[...] = g
        d_ref[...] = -ADAM_LR * (m_hat / (jnp.sqrt(v_hat) + ADAM_EPS) + ADAM_WD * w_ref[...])
        mo_ref[...] = mn
        vo_ref[...] = vn

    blk = pl.BlockSpec((rows, tc), lambda i: (0, i))
    return pl.pallas_call(
        body,
        name=name,
        grid=(cols // tc,),
        in_specs=[pl.BlockSpec((N_DEV, rows, tc), lambda i: (0, 0, i)), blk, blk, blk],
        out_specs=[blk] * 4,
        out_shape=[_sds((rows, cols), f32)] * 4,
        compiler_params=_params(("arbitrary",)),
    )(recv, w, m, v)


def _pad_cols(a, width=128):
    return jnp.pad(a, ((0, 0), (0, width - a.shape[1])))


def _pack_small(ln_g, ln_b, rel, b_f, sink):
    return jnp.concatenate([
        ln_g.reshape(8, 128), ln_b.reshape(8, 128), _pad_cols(rel),
        jnp.pad(_pad_cols(b_f), ((0, 7), (0, 0))), jnp.pad(_pad_cols(sink), ((0, 7), (0, 0)))], axis=0)


def _unpack_small(p):
    return (p[0:8].reshape(1, D_MODEL), p[8:16].reshape(1, D_MODEL), p[16:48, 0:8], p[48:49, 0:8], p[56:57, 0:8])


def kernel(x, w_in, b_f, rel_bias, sink, w_o, ln_g, ln_b, loss_target, m_w_in, m_b_f, m_rel_bias, m_sink, m_w_o, m_ln_g, m_ln_b, v_w_in, v_b_f, v_rel_bias, v_sink, v_w_o, v_ln_g, v_ln_b):
    x2 = x[0]
    tgt = loss_target[0]
    s_len = x2.shape[0]
    shard = w_in.shape[2]

    w_in_t = jnp.transpose(w_in[0])
    g_in, g_o = _gather_call([w_in_t.astype(bf16), w_o[0].astype(bf16)])
    wt_full = g_in.reshape(N_DEV * shard, D_MODEL)
    w_t = jnp.concatenate([wt_full[:O_FF0], wt_full[O_FF1:], wt_full[O_FF0:O_FF1],
                           jnp.zeros((A_W - D_IN, D_MODEL), bf16)], axis=0)
    wo_full = g_o.reshape(D_MODEL, D_MODEL)

    qf, kf, vf, fz, qs, ks, vs, sz, fft, vat = _proj_call(x2, w_t, 512)
    cum, sgm = _cum_call(fft, b_f.reshape(FOX_HEADS, 1))
    qat, ka, tile_stats = _augment_call(qf, kf, cum.reshape(FOX_HEADS, 1, s_len), 1024)
    first_tile, last_tile = _fox_prune_tables(tile_stats)
    o_ft, lse_f = _fox_fwd_call(qat, ka, vat, first_tile)
    bucket = jnp.asarray(_t5_bucket_table())
    bias, bias0 = _swa_bias_call(rel_bias, bucket)
    sink_v = sink.reshape(SWA_HEADS)
    o_s, lse_s = _swa_fwd_call(qs, ks, vs, bias, bias0, sink_v)

    (dh, do_f, dfz, do_s, dsz, dl_f, dl_s, dwo, dg, db, loss_part) = _post_call(
        o_ft.reshape(FOX_HEADS * HEAD_DIM, s_len), fz, o_s, sz, x2, tgt, wo_full, ln_g, ln_b,
        jnp.asarray(_head_selector()), 256)

    dkf, dvf, dqf, dcq, dck = _fox_bwd_call(ka, vf, qat, do_f.reshape(FOX_HEADS, HEAD_DIM, s_len), lse_f,
                                            dl_f.reshape(FOX_HEADS, 1, s_len), last_tile)
    dfft, dbf = _cum_bwd_call(dcq.reshape(FOX_HEADS, s_len), dck.reshape(FOX_HEADS, s_len), sgm)
    dqs, dks, dvs, dbias, dsink = _swa_bwd_call(qs, ks, vs, do_s, lse_s, dl_s, bias, bias0, sink_v)
    drel = _swa_bias_bwd_call(dbias, bucket)

    dx, dproj = _dx_call(dh, dqf, dkf, dvf, dfz, dqs, dks, dvs, dsz, dfft, w_t, 256)
    dw_t = _dw_call(x2, dproj, 1024)

    dwt_full = jnp.concatenate([dw_t[:O_FF0], dw_t[A_FF:A_FF + (O_FF1 - O_FF0)], dw_t[O_FF0:A_FF]], axis=0)
    dw_blocks = dwt_full.reshape(N_DEV, shard, D_MODEL).astype(bf16)
    dwo_blocks = dwo.reshape(N_DEV, D_MODEL // N_DEV, D_MODEL).astype(bf16)
    small = _pack_small(dg, db, drel[:, 0:8], dbf[:, 0].reshape(1, 8), dsink[:, 0].reshape(1, 8))
    loss_slot = np.zeros((64, 128), bool)
    loss_slot[49, 0] = True
    small = jnp.where(jnp.asarray(loss_slot), loss_part[0, 0], small)
    small_blocks = jnp.broadcast_to(small[None], (N_DEV,) + small.shape)
    r_in, r_o, r_small = _exchange_call([dw_blocks, dwo_blocks, small_blocks])

    win_t = [jnp.transpose(a) for a in _adam_call(
        r_in, w_in_t, jnp.transpose(m_w_in[0]), jnp.transpose(v_w_in[0]), 256, "adam_w_in")]
    g_win, d_win, nm_win, nv_win = win_t
    g_wo, d_wo, nm_wo, nv_wo = _adam_call(r_o, w_o[0], m_w_o[0], v_w_o[0], 256, "adam_w_o")
    p_w = _pack_small(ln_g, ln_b, rel_bias, b_f, sink)
    p_m = _pack_small(m_ln_g, m_ln_b, m_rel_bias, m_b_f, m_sink)
    p_v = _pack_small(v_ln_g, v_ln_b, v_rel_bias, v_b_f, v_sink)
    g_p, d_p, nm_p, nv_p = _adam_call(r_small, p_w, p_m, p_v, 128, "adam_small")

    loss = g_p[49, 0]
    g_lng, g_lnb, g_rel, g_bf, g_sink = _unpack_small(g_p)
    d_lng, d_lnb, d_rel, d_bf, d_sink = _unpack_small(d_p)
    m_lng, m_lnb, m_rel, m_bf, m_sk = _unpack_small(nm_p)
    v_lng, v_lnb, v_rel, v_bf, v_sk = _unpack_small(nv_p)
    return (loss, dx[None], g_win[None], g_bf, g_rel, g_sink, g_wo[None], g_lng, g_lnb,
            d_win[None], d_bf, d_rel, d_sink, d_wo[None], d_lng, d_lnb,
            nm_win[None], m_bf, m_rel, m_sk, nm_wo[None], m_lng, m_lnb,
            nv_win[None], v_bf, v_rel, v_sk, nv_wo[None], v_lng, v_lnb)
```

```python
import functools
import math

import numpy as np
import jax
import jax.numpy as jnp
from jax import lax
from jax.experimental import pallas as pl
from jax.experimental.pallas import tpu as pltpu

f32 = jnp.float32
bf16 = jnp.bfloat16

D_MODEL = 1024
HEAD_DIM = 64
FOX_HEADS = 8
SWA_HEADS = 8
SWA_KV_HEADS = 2
SWA_GROUP = 4
BLOCK = 128
NUM_BUCKETS = 32
MAX_DISTANCE = 128
LN_EPS = 1e-5
NEG_INF = -1e30
ALPHA = 2.0 ** 0.25
SCALE = 1.0 / math.sqrt(HEAD_DIM)
D_IN = 3336

ADAM_LR = 0.001
ADAM_B1 = 0.9
ADAM_B2 = 0.999
ADAM_EPS = 1e-08
ADAM_WD = 0.01
ADAM_STEP = 10

N_DEV = 8
A_FQ, A_FK, A_FV, A_FZ, A_SQ, A_SK, A_SV, A_SZ, A_FF, A_W = 0, 512, 1024, 1536, 2048, 2560, 2688, 2816, 3328, 3456
O_FF0, O_FF1 = 1536, 1544

VMEM_LIMIT = 48 * 1024 * 1024
HIGHEST = lax.Precision.HIGHEST
NT = (((1,), (1,)), ((), ()))
TN = (((0,), (0,)), ((), ()))
MESH = pl.DeviceIdType.MESH
RELS = [(0, 0, 1), (0, 1, 0), (0, 1, 1), (1, 0, 0), (1, 0, 1), (1, 1, 0), (1, 1, 1)]


def _params(sem=None):
    return pltpu.CompilerParams(dimension_semantics=sem, vmem_limit_bytes=VMEM_LIMIT)


def _sds(shape, dtype):
    return jax.ShapeDtypeStruct(shape, dtype)


def _t5_bucket_table():
    qi = np.arange(BLOCK)[:, None]
    kj = np.arange(2 * BLOCK)[None, :]
    rel = qi + BLOCK - kj
    band = (rel >= 0) & (rel < BLOCK)
    relc = np.maximum(rel, 0)
    max_exact = NUM_BUCKETS // 2
    relf = np.maximum(relc, 1).astype(np.float32)
    large = max_exact + (np.log(relf / np.float32(max_exact)) / np.float32(math.log(MAX_DISTANCE / max_exact))
                         * np.float32(NUM_BUCKETS - max_exact)).astype(np.int32)
    large = np.minimum(large, NUM_BUCKETS - 1)
    bucket = np.where(relc < max_exact, relc, large).astype(np.int32)
    bucket = np.where(band, bucket, -1).astype(np.int32)
    return bucket


def _mesh_pos():
    return lax.axis_index("x"), lax.axis_index("y"), lax.axis_index("c")


def _dev_index(p):
    return 4 * p[0] + 2 * p[1] + p[2]


def _gather_call(xs):
    n = len(xs)

    def body(*refs):
        x_refs, o_refs = refs[:n], refs[n:2 * n]
        send_sems, recv_sems, local_sems = refs[2 * n:]
        x, y, c = _mesh_pos()
        me, sib = (x, y, c), (x, y, 1 - c)
        chips = [(1 - x, y), (x, 1 - y), (1 - x, 1 - y)]

        def copy(a, k, block, to, src=None):
            slot = o_refs[a].at[_dev_index(block)]
            return pltpu.make_async_remote_copy(
                src_ref=slot if src is None else src, dst_ref=slot,
                send_sem=send_sems.at[a * 7 + k], recv_sem=recv_sems.at[a * 7 + k],
                device_id=to, device_id_type=MESH)

        mine = [pltpu.make_async_copy(x_refs[a], o_refs[a].at[_dev_index(me)], local_sems.at[a]) for a in range(n)]
        for cp in mine:
            cp.start()
        first = []
        for a in range(n):
            first.append(copy(a, 0, me, sib, src=x_refs[a]))
            first += [copy(a, 1 + j, me, (*chip, c), src=x_refs[a]) for j, chip in enumerate(chips)]
        for cp in first:
            cp.start()
        passed = []
        for j, chip in enumerate(chips):
            for a in range(n):
                copy(a, 1 + j, (*chip, c), me).wait_recv()
                fwd = copy(a, 4 + j, (*chip, c), sib)
                fwd.start()
                passed.append(fwd)
        for a in range(n):
            copy(a, 0, sib, me).wait_recv()
            for j, chip in enumerate(chips):
                copy(a, 4 + j, (*chip, 1 - c), me).wait_recv()
        for cp in first + passed:
            cp.wait_send()
        for cp in mine:
            cp.wait()

    any_spec = pl.BlockSpec(memory_space=pl.ANY)
    return pl.pallas_call(
        body,
        name="gather_weights",
        out_shape=[_sds((N_DEV,) + a.shape, a.dtype) for a in xs],
        in_specs=[any_spec] * n,
        out_specs=[any_spec] * n,
        scratch_shapes=[pltpu.SemaphoreType.DMA((7 * n,)), pltpu.SemaphoreType.DMA((7 * n,)),
                        pltpu.SemaphoreType.DMA((n,))],
    )(*xs)


def _exchange_call(bs):
    n = len(bs)

    def body(*refs):
        b_refs, r_refs = refs[:n], refs[n:2 * n]
        send_sems, recv_sems, local_sems = refs[2 * n:]
        x, y, c = _mesh_pos()
        me_idx = _dev_index((x, y, c))
        mine = [pltpu.make_async_copy(b_refs[a].at[me_idx], r_refs[a].at[me_idx], local_sems.at[a]) for a in range(n)]
        for cp in mine:
            cp.start()
        sent = []
        for k, r in enumerate(RELS):
            peer = ((1 - x) if r[0] else x, (1 - y) if r[1] else y, (1 - c) if r[2] else c)
            pidx = _dev_index(peer)
            for a in range(n):
                out = pltpu.make_async_remote_copy(
                    src_ref=b_refs[a].at[pidx], dst_ref=r_refs[a].at[me_idx],
                    send_sem=send_sems.at[a * 7 + k], recv_sem=recv_sems.at[a * 7 + k],
                    device_id=peer, device_id_type=MESH)
                out.start()
                inc = pltpu.make_async_remote_copy(
                    src_ref=b_refs[a].at[pidx], dst_ref=r_refs[a].at[pidx],
                    send_sem=send_sems.at[a * 7 + k], recv_sem=recv_sems.at[a * 7 + k],
                    device_id=peer, device_id_type=MESH)
                sent.append((out, inc))
        for out, inc in sent:
            inc.wait_recv()
        for out, inc in sent:
            out.wait_send()
        for cp in mine:
            cp.wait()

    any_spec = pl.BlockSpec(memory_space=pl.ANY)
    return pl.pallas_call(
        body,
        name="exchange_grads",
        out_shape=[_sds(b.shape, b.dtype) for b in bs],
        in_specs=[any_spec] * n,
        out_specs=[any_spec] * n,
        scratch_shapes=[pltpu.SemaphoreType.DMA((7 * n,)), pltpu.SemaphoreType.DMA((7 * n,)),
                        pltpu.SemaphoreType.DMA((n,))],
    )(*bs)


def _proj_call(x2, w_t, tm):
    s_len = x2.shape[0]

    def body(x_ref, w_ref, qf_ref, kf_ref, vf_ref, fz_ref, qs_ref, ks_ref, vs_ref, sz_ref, fft_ref, vat_ref):
        xb = x_ref[...].astype(bf16)
        vt = lax.dot_general(w_ref[A_FV:A_FV + 512, :], xb, NT, preferred_element_type=f32)
        ones_row = jnp.where(lax.broadcasted_iota(jnp.int32, (HEAD_DIM, tm), 0) == 0, 1.0, 0.0).astype(bf16)
        for h in range(FOX_HEADS):
            vat_ref[h, 0:HEAD_DIM, :] = vt[h * HEAD_DIM:(h + 1) * HEAD_DIM, :].astype(bf16)
            vat_ref[h, HEAD_DIM:2 * HEAD_DIM, :] = ones_row

        def seg(off, width):
            return lax.dot_general(xb, w_ref[off:off + width, :], NT, preferred_element_type=f32)

        def put_heads(ref, acc, nheads):
            for h in range(nheads):
                ref[h] = acc[:, h * HEAD_DIM:(h + 1) * HEAD_DIM].astype(bf16)

        put_heads(qf_ref, seg(A_FQ, 512) * SCALE, FOX_HEADS)
        put_heads(kf_ref, seg(A_FK, 512), FOX_HEADS)
        put_heads(vf_ref, seg(A_FV, 512), FOX_HEADS)
        fz_ref[...] = seg(A_FZ, 512)
        put_heads(qs_ref, seg(A_SQ, 512) * SCALE, SWA_HEADS)
        put_heads(ks_ref, seg(A_SK, 128), SWA_KV_HEADS)
        put_heads(vs_ref, seg(A_SV, 128), SWA_KV_HEADS)
        sz_ref[...] = seg(A_SZ, 512)
        fft_ref[...] = seg(A_FF, 128).T[:FOX_HEADS, :]

    def heads(nh):
        return pl.BlockSpec((nh, tm, HEAD_DIM), lambda i: (0, i, 0))

    wide = pl.BlockSpec((tm, 512), lambda i: (i, 0))
    return pl.pallas_call(
        body,
        name="proj_fwd",
        grid=(s_len // tm,),
        in_specs=[pl.BlockSpec((tm, D_MODEL), lambda i: (i, 0)), pl.BlockSpec((A_W, D_MODEL), lambda i: (0, 0))],
        out_specs=[heads(8), heads(8), heads(8), wide, heads(8), heads(2), heads(2), wide,
                   pl.BlockSpec((FOX_HEADS, tm), lambda i: (0, i)),
                   pl.BlockSpec((FOX_HEADS, 2 * HEAD_DIM, tm), lambda i: (0, 0, i))],
        out_shape=[_sds((8, s_len, HEAD_DIM), bf16)] * 3 + [_sds((s_len, 512), f32), _sds((8, s_len, HEAD_DIM), bf16),
                   _sds((2, s_len, HEAD_DIM), bf16), _sds((2, s_len, HEAD_DIM), bf16), _sds((s_len, 512), f32),
                   _sds((FOX_HEADS, s_len), f32), _sds((FOX_HEADS, 2 * HEAD_DIM, s_len), bf16)],
        compiler_params=_params(("arbitrary",)),
    )(x2, w_t)


AUG = 2 * HEAD_DIM


def _augment_call(q, k, cum_row, tm):
    nh, s_len, _ = q.shape
    per_step = tm // FOX_T

    def body(q_ref, k_ref, c_ref, qat_ref, ka_ref, kat_ref, st_ref):
        c = c_ref[0]
        hi = c.astype(bf16).astype(f32)
        r1 = c - hi
        mid = r1.astype(bf16).astype(f32)
        lo = (r1 - mid).astype(bf16).astype(f32)
        row = lax.broadcasted_iota(jnp.int32, (HEAD_DIM, tm), 0)
        q_tail = jnp.where(row == 0, hi, jnp.where(row == 1, mid, jnp.where(row == 2, lo,
                           jnp.where(row < 6, 1.0, 0.0))))
        k_tail = jnp.where(row < 3, 1.0, jnp.where(row == 3, -hi, jnp.where(row == 4, -mid,
                           jnp.where(row == 5, -lo, 0.0))))
        qt = q_ref[0].astype(f32).T
        kt = k_ref[0].astype(f32).T
        qat_ref[0, 0:HEAD_DIM, :] = qt.astype(bf16)
        qat_ref[0, HEAD_DIM:AUG, :] = q_tail.astype(bf16)
        ka_ref[0] = jnp.concatenate([k_ref[0], k_tail.T.astype(bf16)], axis=1)
        kat_ref[0, 0:HEAD_DIM, :] = kt.astype(bf16)
        kat_ref[0, HEAD_DIM:AUG, :] = k_tail.astype(bf16)
        qn2 =jnp.sum(qt * qt, axis=0, keepdims=True)
        kn2 = jnp.sum(kt * kt, axis=0, keepdims=True)
        sd = jnp.sum(qt * kt, axis=0, keepdims=True)
        srow = lax.broadcasted_iota(jnp.int32, (8, LANES), 0)
        for part in range(per_step):
            sl = slice(part * FOX_T, (part + 1) * FOX_T)
            vals = [jnp.sqrt(jnp.max(qn2[:, sl], axis=1, keepdims=True)),
                    jnp.sqrt(jnp.max(kn2[:, sl], axis=1, keepdims=True)),
                    jnp.min(sd[:, sl], axis=1, keepdims=True),
                    jnp.max(c[:, sl], axis=1, keepdims=True), jnp.min(c[:, sl], axis=1, keepdims=True)]
            out = jnp.zeros((8, LANES), f32)
            for r, val in enumerate(vals):
                out = jnp.where(srow == r, val, out)
            st_ref[0, part] = out

    tile = pl.BlockSpec((1, tm, HEAD_DIM), lambda h, i: (h, i, 0))
    return pl.pallas_call(
        body,
        name="fox_augment",
        grid=(nh, s_len // tm),
        in_specs=[tile, tile, pl.BlockSpec((1, 1, tm), lambda h, i: (h, 0, i))],
        out_specs=[pl.BlockSpec((1, AUG, tm), lambda h, i: (h, 0, i)),
                   pl.BlockSpec((1, tm, AUG), lambda h, i: (h, i, 0)),
                   pl.BlockSpec((1, AUG, tm), lambda h, i: (h, 0, i)),
                   pl.BlockSpec((1, per_step, 8, LANES), lambda h, i: (h, i, 0, 0))],
        out_shape=[_sds((nh, AUG, s_len), bf16), _sds((nh, s_len, AUG), bf16), _sds((nh, AUG, s_len), bf16),
                   _sds((nh, s_len // FOX_T, 8, LANES), f32)],
        compiler_params=_params(("arbitrary", "arbitrary")),
    )(q, k, cum_row)


EXP_ZERO_GAP = 110.0


def _fox_prune_tables(stats):
    s = stats[:, :, :, 0]
    qn, kn, sd, cmx, cmn = (s[:, :, r] for r in range(5))
    nt = s.shape[1]
    bound = qn[:, :, None] * kn[:, None, :] + (cmx[:, :, None] - cmn[:, None, :])
    margin = 2.0 + 1e-5 * (jnp.abs(cmx)[:, :, None] + jnp.abs(cmn)[:, None, :])
    qi = lax.broadcasted_iota(jnp.int32, (nt, nt), 0)
    kj = lax.broadcasted_iota(jnp.int32, (nt, nt), 1)
    skip = (bound + margin < sd[:, :, None] - EXP_ZERO_GAP) & (kj < qi)[None]
    first = jnp.sum(jnp.cumprod(skip.astype(jnp.int32), axis=2), axis=2)
    tiles = lax.broadcasted_iota(jnp.int32, (1, nt), 1)
    cnt = tiles - first
    ends = jnp.cumsum(cnt, axis=1)
    off = ends - cnt
    kmax = nt * (nt - 1) // 2
    k = lax.broadcasted_iota(jnp.int32, (1, kmax), 1)
    pair_q = jnp.minimum(jnp.sum((ends[:, None, :] <= k[:, :, None]).astype(jnp.int32), axis=2), nt - 1)
    hit = pair_q[:, :, None] == tiles[:, None, :]
    first_k = jnp.sum(jnp.where(hit, first[:, None, :], 0), axis=2)
    off_k = jnp.sum(jnp.where(hit, off[:, None, :], 0), axis=2)
    pair_k = jnp.clip(first_k + k - off_k, 0, nt - 1)
    return (ends[:, nt - 1].astype(jnp.int32), pair_q.reshape(-1).astype(jnp.int32),
            pair_k.reshape(-1).astype(jnp.int32))


CUM_CHUNK = 512


def _cum_call(fft, bf_col):
    s_len = fft.shape[1]
    ch = CUM_CHUNK

    def body(f_ref, b_ref, cum_ref, sg_ref):
        r = lax.broadcasted_iota(jnp.int32, (ch, ch), 0)
        c = lax.broadcasted_iota(jnp.int32, (ch, ch), 1)
        upper = (r <= c).astype(f32)
        carry = jnp.zeros((FOX_HEADS, 1), f32)
        for n in range(s_len // ch):
            z = f_ref[:, n * ch:(n + 1) * ch] + b_ref[...]
            logf = jnp.minimum(z, 0.0) - jnp.log1p(jnp.exp(-jnp.abs(z)))
            sg_ref[:, n * ch:(n + 1) * ch] = 1.0 / (1.0 + jnp.exp(z))
            cs = jnp.dot(logf, upper, precision=HIGHEST, preferred_element_type=f32) + carry
            cum_ref[:, n * ch:(n + 1) * ch] = cs
            carry = cs[:, ch - 1:ch]

    return pl.pallas_call(
        body,
        name="fox_cum_fwd",
        out_shape=[_sds((FOX_HEADS, s_len), f32)] * 2,
        compiler_params=_params(),
    )(fft, bf_col)


def _cum_bwd_call(dcq, dck, sg):
    s_len = sg.shape[1]
    ch = CUM_CHUNK
    nch = s_len // ch

    def body(q_ref, k_ref, sg_ref, dff_ref, dbf_ref):
        r = lax.broadcasted_iota(jnp.int32, (ch, ch), 0)
        c = lax.broadcasted_iota(jnp.int32, (ch, ch), 1)
        lower = (r >= c).astype(f32)
        dff_ref[...] = jnp.zeros_like(dff_ref)
        carry = jnp.zeros((FOX_HEADS, 1), f32)
        total = jnp.zeros((FOX_HEADS, 1), f32)
        for n in reversed(range(nch)):
            sl = slice(n * ch, (n + 1) * ch)
            dcum = q_ref[:, sl] - k_ref[:, sl]
            rs = jnp.dot(dcum, lower, precision=HIGHEST, preferred_element_type=f32) + carry
            carry = rs[:, 0:1]
            dff = rs * sg_ref[:, sl]
            dff_ref[0:FOX_HEADS, sl] = dff
            total = total + jnp.sum(dff, axis=1, keepdims=True)
        dbf_ref[...] = jnp.broadcast_to(total, (FOX_HEADS, 128))

    return pl.pallas_call(
        body,
        name="fox_cum_bwd",
        out_shape=[_sds((128, s_len), f32), _sds((FOX_HEADS, 128), f32)],
        compiler_params=_params(),
    )(dcq, dck, sg)


FOX_T = 512
LANES = 128


def _causal_keep(t):
    return lax.broadcasted_iota(jnp.int32, (t, t), 0) <= lax.broadcasted_iota(jnp.int32, (t, t), 1)


def _tile_cols(i, t):
    return pl.ds(pl.multiple_of(i * t, t), t)


def _fox_pair(n, nt, kmax, h, pq_ref, pk_ref):
    k = h * kmax + jnp.maximum(n - nt, 0)
    return jnp.where(n < nt, n, pq_ref[k]), jnp.where(n < nt, n, pk_ref[k])


def _fox_fwd_call(qat, ka, vat, npairs, pair_q, pair_k):
    nh, s_len, _ = ka.shape
    t = FOX_T
    nt = s_len // t
    kmax = nt * (nt - 1) // 2
    assert nt >= 2 and nt % 2 == 0

    def body(np_ref, pq_ref, pk_ref, qat_ref, ka_ref, vat_ref, o_ref, lse_ref, s0, s1, p0, p1, a0, a1, m_all, acc_all):
        h = pl.program_id(0)
        extra = np_ref[h]
        total = nt + extra
        m_all[...] = jnp.full(m_all.shape, NEG_INF, f32)
        acc_all[...] = jnp.zeros(acc_all.shape, f32)
        bufs = ((s0, p0, a0), (s1, p1, a1))

        def pair(n):
            return _fox_pair(n, nt, kmax, h, pq_ref, pk_ref)

        def scores(n, b, masked):
            i, j = pair(n)
            st = jnp.dot(ka_ref[0, _tile_cols(j, t), :], qat_ref[0, :, _tile_cols(i, t)], preferred_element_type=f32)
            if masked:
                st = jnp.where(_causal_keep(t), st, NEG_INF)
            bufs[b][0][...] = st

        def softmax(n, b):
            i, _ = pair(n)
            s_ref, p_ref, a_ref = bufs[b]
            for c in range(t // LANES):
                cols = slice(c * LANES, (c + 1) * LANES)
                mcols = pl.ds(pl.multiple_of(i * t + c * LANES, LANES), LANES)
                m_old = m_all[:, mcols]
                m_new = jnp.maximum(m_old, jnp.max(s_ref[:, cols], axis=0, keepdims=True))
                m_all[:, mcols] = m_new
                a_ref[:, cols] = jnp.exp(m_old - m_new)
                p_ref[:, cols] = jnp.exp(s_ref[:, cols] - m_new).astype(bf16)

        def accum(n, b):
            i, j = pair(n)
            cols = _tile_cols(i, t)
            acc_all[:, cols] = bufs[b][2][...] * acc_all[:, cols] + jnp.dot(
                vat_ref[0, :, _tile_cols(j, t)], bufs[b][1][...], preferred_element_type=f32)

        def step(n, b, masked):
            scores(n, b, masked)
            softmax(n - 1, 1 - b)
            accum(n - 2, b)

        scores(0, 0, True)
        scores(1, 1, True)
        softmax(0, 0)

        def diag_steps(d, _):
            n = 2 + 2 * d
            step(n, 0, True)
            step(n + 1, 1, True)
            return 0

        lax.fori_loop(0, (nt - 2) // 2, diag_steps, 0)

        def off_steps(d, _):
            n = nt + 2 * d
            step(n, 0, False)
            step(n + 1, 1, False)
            return 0

        lax.fori_loop(0, extra // 2, off_steps, 0)

        @pl.when(extra % 2 == 1)
        def _():
            step(total - 1, 0, False)
            softmax(total - 1, 0)
            accum(total - 2, 1)
            accum(total - 1, 0)

        @pl.when(extra % 2 == 0)
        def _():
            softmax(total - 1, 1)
            accum(total - 2, 0)
            accum(total - 1, 1)

        l = acc_all[HEAD_DIM:HEAD_DIM + 1, :]
        o_ref[0] = acc_all[0:HEAD_DIM, :] / l
        lse_ref[0] = m_all[...] + jnp.log(l)

    smem = pl.BlockSpec(memory_space=pltpu.SMEM)
    return pl.pallas_call(
        body,
        name="fox_fwd",
        grid=(nh,),
        in_specs=[smem, smem, smem,
                  pl.BlockSpec((1, AUG, s_len), lambda h: (h, 0, 0)),
                  pl.BlockSpec((1, s_len, AUG), lambda h: (h, 0, 0)),
                  pl.BlockSpec((1, AUG, s_len), lambda h: (h, 0, 0))],
        out_specs=[pl.BlockSpec((1, HEAD_DIM, s_len), lambda h: (h, 0, 0)),
                   pl.BlockSpec((1, 1, s_len), lambda h: (h, 0, 0))],
        out_shape=[_sds((nh, HEAD_DIM, s_len), f32), _sds((nh, 1, s_len), f32)],
        scratch_shapes=[pltpu.VMEM((t, t), f32), pltpu.VMEM((t, t), f32), pltpu.VMEM((t, t), bf16),
                        pltpu.VMEM((t, t), bf16), pltpu.VMEM((1, t), f32), pltpu.VMEM((1, t), f32),
                        pltpu.VMEM((1, s_len), f32), pltpu.VMEM((AUG, s_len), f32)],
        compiler_params=_params(("arbitrary",)),
    )(npairs, pair_q, pair_k, qat, ka, vat)


SWA_TS = 512


def _swa_bias_call(rel_bias, bucket):
    def body(rb_ref, bk_ref, b_ref, b0_ref):
        bk = bk_ref[...]
        col = lax.broadcasted_iota(jnp.int32, (BLOCK, 2 * BLOCK), 1)
        for h in range(SWA_HEADS):
            acc = jnp.full((BLOCK, 2 * BLOCK), NEG_INF, f32)
            for b in range(NUM_BUCKETS):
                acc = jnp.where(bk == b, rb_ref[b, h], acc)
            g, hh = divmod(h, SWA_GROUP)
            b_ref[g, hh * BLOCK:(hh + 1) * BLOCK, :] = acc
            b0_ref[g, hh * BLOCK:(hh + 1) * BLOCK, :] = jnp.where(col < BLOCK, NEG_INF, acc)

    return pl.pallas_call(
        body,
        name="swa_bias",
        in_specs=[pl.BlockSpec(memory_space=pltpu.SMEM), pl.BlockSpec(memory_space=pltpu.VMEM)],
        out_shape=[_sds((SWA_KV_HEADS, SWA_GROUP * BLOCK, 2 * BLOCK), f32)] * 2,
        compiler_params=_params(),
    )(rel_bias, bucket)


def _swa_bias_bwd_call(dbias, bucket):
    def body(d_ref, bk_ref, o_ref):
        bk = bk_ref[...]
        row = lax.broadcasted_iota(jnp.int32, (NUM_BUCKETS, 128), 0)
        col = lax.broadcasted_iota(jnp.int32, (NUM_BUCKETS, 128), 1)
        out = jnp.zeros((NUM_BUCKETS, 128), f32)
        for h in range(SWA_HEADS):
            g, hh = divmod(h, SWA_GROUP)
            d = d_ref[g, hh * BLOCK:(hh + 1) * BLOCK, :]
            for b in range(NUM_BUCKETS):
                val = jnp.sum(jnp.sum(jnp.where(bk == b, d, 0.0), axis=1, keepdims=True), axis=0, keepdims=True)
                out = jnp.where((row == b) & (col == h), val, out)
        o_ref[...] = out

    return pl.pallas_call(
        body,
        name="swa_bias_bwd",
        out_shape=_sds((NUM_BUCKETS, 128), f32),
        compiler_params=_params(),
    )(dbias, bucket)


def _swa_specs(ts):
    nb = ts // BLOCK
    qspec = pl.BlockSpec((SWA_HEADS, ts, HEAD_DIM), lambda n: (0, n, 0))
    cur = pl.BlockSpec((SWA_KV_HEADS, ts, HEAD_DIM), lambda n: (0, n, 0))
    prev = pl.BlockSpec((SWA_KV_HEADS, BLOCK, HEAD_DIM), lambda n: (0, jnp.maximum(n * nb - 1, 0), 0))
    return qspec, cur, prev


def _sink_col(sink_ref, g):
    return jnp.concatenate([jnp.full((BLOCK, 1), sink_ref[g * SWA_GROUP + hh], f32) for hh in range(SWA_GROUP)], axis=0)


def _swa_fwd_call(q, k, v, bias, bias0, sink):
    s_len = q.shape[1]
    ts = SWA_TS
    nb = ts // BLOCK

    def body(q_ref, kc_ref, kp_ref, vc_ref, vp_ref, b_ref, b0_ref, sink_ref, o_ref, lse_ref):
        first = pl.program_id(0) == 0
        for g in range(SWA_KV_HEADS):
            kall = jnp.concatenate([kp_ref[g], kc_ref[g]], axis=0)
            vall = jnp.concatenate([vp_ref[g], vc_ref[g]], axis=0)
            sink_c = _sink_col(sink_ref, g)
            for b in range(nb):
                rows = slice(b * BLOCK, (b + 1) * BLOCK)
                qg = jnp.concatenate([q_ref[g * SWA_GROUP + hh, rows, :] for hh in range(SWA_GROUP)], axis=0)
                kcat = kall[b * BLOCK:(b + 2) * BLOCK]
                vcat = vall[b * BLOCK:(b + 2) * BLOCK]
                bias_b = b_ref[g]
                if b == 0:
                    bias_b = jnp.where(first, b0_ref[g], bias_b)
                s = lax.dot_general(qg, kcat, NT, preferred_element_type=f32) + bias_b
                m = jnp.maximum(jnp.max(s, axis=1, keepdims=True), sink_c)
                p = jnp.exp(s - m)
                l = jnp.sum(p, axis=1, keepdims=True) + jnp.exp(sink_c - m)
                o = jnp.dot(p.astype(bf16), vcat, preferred_element_type=f32) / l
                lse = m + jnp.log(l)
                for hh in range(SWA_GROUP):
                    o_ref[g * SWA_GROUP + hh, rows, :] = o[hh * BLOCK:(hh + 1) * BLOCK]
                    lse_ref[g * SWA_GROUP + hh, rows, :] = lse[hh * BLOCK:(hh + 1) * BLOCK]

    qspec, cur, prev = _swa_specs(ts)
    bspec = pl.BlockSpec((SWA_KV_HEADS, SWA_GROUP * BLOCK, 2 * BLOCK), lambda n: (0, 0, 0))
    return pl.pallas_call(
        body,
        name="swa_fwd",
        grid=(s_len // ts,),
        in_specs=[qspec, cur, prev, cur, prev, bspec, bspec, pl.BlockSpec(memory_space=pltpu.SMEM)],
        out_specs=[pl.BlockSpec((SWA_HEADS, ts, HEAD_DIM), lambda n: (0, n, 0)),
                   pl.BlockSpec((SWA_HEADS, ts, 1), lambda n: (0, n, 0))],
        out_shape=[_sds((SWA_HEADS, s_len, HEAD_DIM), f32), _sds((SWA_HEADS, s_len, 1), f32)],
        compiler_params=_params(("arbitrary",)),
    )(q, k, k, v, v, bias, bias0, sink)


def _head_selector():
    sel = np.zeros((512, 128), np.float32)
    for h in range(8):
        sel[h * HEAD_DIM:(h + 1) * HEAD_DIM, h] = 1.0
    return sel


def _post_call(of, fz, osw, sz, x2, tgt, wo, ln_g, ln_b, sel, tm):
    s_len = x2.shape[0]

    def body(of_ref, fz_ref, os_ref, sz_ref, x_ref, t_ref, wo_ref, g_ref, b_ref, sel_ref,
             dh_ref, dof_ref, dfz_ref, dos_ref, dsz_ref, dlf_ref, dls_ref, dwo_ref, dg_ref, db_ref, loss_ref):
        n = pl.program_id(0)

        @pl.when(n == 0)
        def _():
            dwo_ref[...] = jnp.zeros_like(dwo_ref)
            dg_ref[...] = jnp.zeros_like(dg_ref)
            db_ref[...] = jnp.zeros_like(db_ref)
            loss_ref[...] = jnp.zeros_like(loss_ref)

        o_f = of_ref[...].T
        o_s = jnp.concatenate([os_ref[h] for h in range(SWA_HEADS)], axis=1)
        fz = fz_ref[...]
        sz = sz_ref[...]
        sg_f = jax.nn.sigmoid(fz)
        sg_s = jax.nn.sigmoid(sz)
        silu_f = fz * sg_f
        silu_s = sz * sg_s
        mixed = jnp.concatenate([o_f * silu_f, o_s * silu_s], axis=1).astype(bf16)
        y = jnp.dot(mixed, wo_ref[...], preferred_element_type=f32)
        h = ALPHA * x_ref[...] + y
        mu = jnp.mean(h, axis=1, keepdims=True)
        hc = h - mu
        var = jnp.mean(hc * hc, axis=1, keepdims=True)
        rstd = lax.rsqrt(var + LN_EPS)
        xhat = hc * rstd
        gam = g_ref[...]
        out = xhat * gam + b_ref[...]
        err = out - t_ref[...]
        tok_loss = jnp.mean(err * err, axis=1, keepdims=True)
        loss_ref[...] += 0.5 * jnp.sum(tok_loss, axis=0, keepdims=True)
        dout = err * (1.0 / D_MODEL)
        dg_ref[...] += jnp.sum(dout * xhat, axis=0, keepdims=True)
        db_ref[...] += jnp.sum(dout, axis=0, keepdims=True)
        dxh = dout * gam
        m1 = jnp.mean(dxh, axis=1, keepdims=True)
        m2 = jnp.mean(dxh * xhat, axis=1, keepdims=True)
        dh = rstd * (dxh - m1 - xhat * m2)
        dh_ref[...] = dh
        dyb = dh.astype(bf16)
        dwo_ref[...] += lax.dot_general(mixed, dyb, TN, preferred_element_type=f32)
        dmix = lax.dot_general(dyb, wo_ref[...], NT, preferred_element_type=f32)
        dm_f = dmix[:, :512]
        dm_s = dmix[:, 512:]
        do_f = dm_f * silu_f
        do_s = dm_s * silu_s
        dfz_ref[...] = (dm_f * o_f * (sg_f * (1.0 + fz * (1.0 - sg_f)))).astype(bf16)
        dsz_ref[...] = (dm_s * o_s * (sg_s * (1.0 + sz * (1.0 - sg_s)))).astype(bf16)
        dof_ref[...] = do_f.T.astype(bf16)
        for hd in range(SWA_HEADS):
            dos_ref[hd] = do_s[:, hd * HEAD_DIM:(hd + 1) * HEAD_DIM].astype(bf16)
        sel_m = sel_ref[...]
        dl_f = jnp.dot(do_f * o_f, sel_m, precision=HIGHEST, preferred_element_type=f32)
        dl_s = jnp.dot(do_s * o_s, sel_m, precision=HIGHEST, preferred_element_type=f32)
        dlf_ref[...] = dl_f.T[:FOX_HEADS, :]
        dls_ref[...] = dl_s

    heads_f32 = pl.BlockSpec((8, tm, HEAD_DIM), lambda n: (0, n, 0))
    half = pl.BlockSpec((tm, 512), lambda n: (n, 0))
    fullw = pl.BlockSpec((tm, D_MODEL), lambda n: (n, 0))
    vec = pl.BlockSpec((1, D_MODEL), lambda n: (0, 0))
    return pl.pallas_call(
        body,
        name="post_fwd_bwd",
        grid=(s_len // tm,),
        in_specs=[pl.BlockSpec((512, tm), lambda n: (0, n)), half, heads_f32, half, fullw, fullw,
                  pl.BlockSpec((D_MODEL, D_MODEL), lambda n: (0, 0)), vec, vec,
                  pl.BlockSpec((512, 128), lambda n: (0, 0))],
        out_specs=[fullw, pl.BlockSpec((512, tm), lambda n: (0, n)), half, heads_f32, half,
                   pl.BlockSpec((FOX_HEADS, tm), lambda n: (0, n)), pl.BlockSpec((tm, 128), lambda n: (n, 0)),
                   pl.BlockSpec((D_MODEL, D_MODEL), lambda n: (0, 0)), vec, vec,
                   pl.BlockSpec((1, 1), lambda n: (0, 0))],
        out_shape=[_sds((s_len, D_MODEL), f32), _sds((512, s_len), bf16), _sds((s_len, 512), bf16),
                   _sds((8, s_len, HEAD_DIM), bf16), _sds((s_len, 512), bf16),
                   _sds((FOX_HEADS, s_len), f32), _sds((s_len, 128), f32),
                   _sds((D_MODEL, D_MODEL), f32), _sds((1, D_MODEL), f32), _sds((1, D_MODEL), f32),
                   _sds((1, 1), f32)],
        compiler_params=_params(("arbitrary",)),
    )(of, fz, osw, sz, x2, tgt, wo, ln_g, ln_b, sel)


def _fox_bwd_call(ka, kat, v, qat, dot, lse_row, dl_row, npairs, pair_q, pair_k):
    nh, s_len, _ = ka.shape
    t = FOX_T
    nt = s_len // t
    kmax = nt * (nt - 1) // 2
    assert nt >= 2 and nt % 2 == 0
    ck_slot = HEAD_DIM + 3
    cq_slot = HEAD_DIM

    def body(np_ref, pq_ref, pk_ref, ka_ref, kat_ref, v_ref, qat_ref, dot_ref, lse_ref, dl_ref,
             dq_ref, dk_ref, dv_ref, dcq_ref, dck_ref, dqt_all, dkat_all, dvt_all, p0, p1, ds0, ds1):
        h = pl.program_id(0)
        extra = np_ref[h]
        total = nt + extra
        dqt_all[...] = jnp.zeros(dqt_all.shape, f32)
        dkat_all[...] = jnp.zeros(dkat_all.shape, f32)
        dvt_all[...] = jnp.zeros(dvt_all.shape, f32)
        pbuf, dsbuf = (p0, p1), (ds0, ds1)

        def pair(n):
            return _fox_pair(n, nt, kmax, h, pq_ref, pk_ref)

        def probs(n, b, masked):
            i, j = pair(n)
            qc, kr = _tile_cols(i, t), _tile_cols(j, t)
            st = jnp.dot(ka_ref[0, kr, :], qat_ref[0, :, qc], preferred_element_type=f32)
            if masked:
                st = jnp.where(_causal_keep(t), st, NEG_INF)
            pt = jnp.exp(st - lse_ref[0, :, qc])
            dpt = jnp.dot(v_ref[0, kr, :], dot_ref[0, :, qc], preferred_element_type=f32)
            pbuf[b][...] = pt.astype(bf16)
            dsbuf[b][...] = (pt * (dpt - dl_ref[0, :, qc])).astype(bf16)

        def grads(n, b):
            i, j = pair(n)
            qc, kc = _tile_cols(i, t), _tile_cols(j, t)
            dvt_all[:, kc] += lax.dot_general(dot_ref[0, :, qc], pbuf[b][...], NT, preferred_element_type=f32)
            dkat_all[:, kc] += lax.dot_general(qat_ref[0, :, qc], dsbuf[b][...], NT, preferred_element_type=f32)
            dqt_all[:, qc] += jnp.dot(kat_ref[0, :, kc], dsbuf[b][...], preferred_element_type=f32)

        def step(n, b, masked):
            probs(n, b, masked)
            grads(n - 1, 1 - b)

        probs(0, 0, True)
        step(1, 1, True)

        def diag_steps(d, _):
            n = 2 + 2 * d
            step(n, 0, True)
            step(n + 1, 1, True)
            return 0

        lax.fori_loop(0, (nt - 2) // 2, diag_steps, 0)

        def off_steps(d, _):
            n = nt + 2 * d
            step(n, 0, False)
            step(n + 1, 1, False)
            return 0

        lax.fori_loop(0, extra // 2, off_steps, 0)

        @pl.when(extra % 2 == 1)
        def _():
            step(total - 1, 0, False)
            grads(total - 1, 0)

        @pl.when(extra % 2 == 0)
        def _():
            grads(total - 1, 1)

        dq_ref[0] = (dqt_all[0:HEAD_DIM, :] * SCALE).astype(bf16)
        dk_ref[0] = dkat_all[0:HEAD_DIM, :].astype(bf16)
        dv_ref[0] = dvt_all[...].astype(bf16)
        dcq_ref[0] = dqt_all[cq_slot:cq_slot + 1, :]
        dck_ref[0] = dkat_all[ck_slot:ck_slot + 1, :]

    smem = pl.BlockSpec(memory_space=pltpu.SMEM)
    rows = pl.BlockSpec((1, s_len, AUG), lambda h: (h, 0, 0))
    feat = pl.BlockSpec((1, AUG, s_len), lambda h: (h, 0, 0))
    feat64 = pl.BlockSpec((1, HEAD_DIM, s_len), lambda h: (h, 0, 0))
    rowv = pl.BlockSpec((1, 1, s_len), lambda h: (h, 0, 0))
    return pl.pallas_call(
        body,
        name="fox_bwd",
        grid=(nh,),
        in_specs=[smem, smem, smem, rows, feat, pl.BlockSpec((1, s_len, HEAD_DIM), lambda h: (h, 0, 0)), feat, feat64,
                  rowv, rowv],
        out_specs=[feat64, feat64, feat64, rowv, rowv],
        out_shape=[_sds((nh, HEAD_DIM, s_len), bf16)] * 3 + [_sds((nh, 1, s_len), f32)] * 2,
        scratch_shapes=[pltpu.VMEM((AUG, s_len), f32), pltpu.VMEM((AUG, s_len), f32), pltpu.VMEM((HEAD_DIM, s_len), f32)]
                       + [pltpu.VMEM((t, t), bf16)] * 4,
        compiler_params=_params(("arbitrary",)),
    )(npairs, pair_q, pair_k, ka, kat, v, qat, dot, lse_row, dl_row)


def _swa_bwd_call(q, k, v, do, lse, dl, bias, bias0, sink):
    s_len = q.shape[1]
    ts = SWA_TS
    nb = ts // BLOCK
    nsteps = s_len // ts

    def body(q_ref, kc_ref, kp_ref, vc_ref, vp_ref, do_ref, lse_ref, dl_ref, b_ref, b0_ref, sink_ref,
             dq_ref, dk_ref, dv_ref, dbias_ref, dsink_ref, dk_s, dv_s, tail_k, tail_v, sk_s):
        n = pl.program_id(0)

        @pl.when(n == 0)
        def _():
            dbias_ref[...] = jnp.zeros_like(dbias_ref)
            sk_s[...] = jnp.zeros_like(sk_s)

        @pl.when(n < nsteps)
        def _():
            first = n == 0
            dk_s[...] = jnp.zeros_like(dk_s)
            dv_s[...] = jnp.zeros_like(dv_s)
            for g in range(SWA_KV_HEADS):
                kall = jnp.concatenate([kp_ref[g], kc_ref[g]], axis=0)
                vall = jnp.concatenate([vp_ref[g], vc_ref[g]], axis=0)
                sink_c = _sink_col(sink_ref, g)
                for b in range(nb):
                    rows = slice(b * BLOCK, (b + 1) * BLOCK)
                    heads = [g * SWA_GROUP + hh for hh in range(SWA_GROUP)]
                    qg = jnp.concatenate([q_ref[h, rows, :] for h in heads], axis=0)
                    dog = jnp.concatenate([do_ref[h, rows, :] for h in heads], axis=0)
                    lse_c = jnp.concatenate([lse_ref[h, rows, :] for h in heads], axis=0)
                    dl_c = jnp.concatenate([dl_ref[rows, h:h + 1] for h in heads], axis=0)
                    kcat = kall[b * BLOCK:(b + 2) * BLOCK]
                    vcat = vall[b * BLOCK:(b + 2) * BLOCK]
                    bias_b = b_ref[g]
                    if b == 0:
                        bias_b = jnp.where(first, b0_ref[g], bias_b)
                    s = lax.dot_general(qg, kcat, NT, preferred_element_type=f32) + bias_b
                    p = jnp.exp(s - lse_c)
                    dp = lax.dot_general(dog, vcat, NT, preferred_element_type=f32)
                    ds = p * (dp - dl_c)
                    dsb = ds.astype(bf16)
                    dqg = jnp.dot(dsb, kcat, preferred_element_type=f32) * SCALE
                    for hh, h in enumerate(heads):
                        dq_ref[h, rows, :] = dqg[hh * BLOCK:(hh + 1) * BLOCK].astype(bf16)
                    win = slice(b * BLOCK, (b + 2) * BLOCK)
                    dk_s[g, win, :] += lax.dot_general(dsb, qg, TN, preferred_element_type=f32)
                    dv_s[g, win, :] += lax.dot_general(p.astype(bf16), dog, TN, preferred_element_type=f32)
                    dbias_ref[g] += ds
                    sk_s[g] += -jnp.exp(sink_c - lse_c) * dl_c

        @pl.when(n > 0)
        def _():
            last = slice(ts - BLOCK, ts)
            for g in range(SWA_KV_HEADS):
                add_k = jnp.where(n < nsteps, dk_s[g, 0:BLOCK, :], 0.0)
                add_v = jnp.where(n < nsteps, dv_s[g, 0:BLOCK, :], 0.0)
                dk_ref[g, 0:ts - BLOCK, :] = tail_k[g, 0:ts - BLOCK, :].astype(bf16)
                dv_ref[g, 0:ts - BLOCK, :] = tail_v[g, 0:ts - BLOCK, :].astype(bf16)
                dk_ref[g, last, :] = (tail_k[g, last, :] + add_k).astype(bf16)
                dv_ref[g, last, :] = (tail_v[g, last, :] + add_v).astype(bf16)

        @pl.when(n < nsteps)
        def _():
            tail_k[...] = dk_s[:, BLOCK:, :]
            tail_v[...] = dv_s[:, BLOCK:, :]

        @pl.when(n == nsteps)
        def _():
            row = lax.broadcasted_iota(jnp.int32, (SWA_HEADS, 128), 0)
            out = jnp.zeros((SWA_HEADS, 128), f32)
            for h in range(SWA_HEADS):
                g, hh = divmod(h, SWA_GROUP)
                val = jnp.sum(sk_s[g, hh * BLOCK:(hh + 1) * BLOCK, :], axis=0, keepdims=True)
                out = jnp.where(row == h, val, out)
            dsink_ref[...] = out

    last_step = nsteps - 1

    def cl(n):
        return jnp.minimum(n, last_step)

    qspec = pl.BlockSpec((SWA_HEADS, ts, HEAD_DIM), lambda n: (0, cl(n), 0))
    cur = pl.BlockSpec((SWA_KV_HEADS, ts, HEAD_DIM), lambda n: (0, cl(n), 0))
    prev = pl.BlockSpec((SWA_KV_HEADS, BLOCK, HEAD_DIM), lambda n: (0, jnp.maximum(cl(n) * nb - 1, 0), 0))
    lsespec = pl.BlockSpec((SWA_HEADS, ts, 1), lambda n: (0, cl(n), 0))
    dlspec = pl.BlockSpec((ts, 128), lambda n: (cl(n), 0))
    bspec = pl.BlockSpec((SWA_KV_HEADS, SWA_GROUP * BLOCK, 2 * BLOCK), lambda n: (0, 0, 0))
    kvout = pl.BlockSpec((SWA_KV_HEADS, ts, HEAD_DIM), lambda n: (0, jnp.maximum(n - 1, 0), 0))
    return pl.pallas_call(
        body,
        name="swa_bwd",
        grid=(nsteps + 1,),
        in_specs=[qspec, cur, prev, cur, prev, qspec, lsespec, dlspec, bspec, bspec,
                  pl.BlockSpec(memory_space=pltpu.SMEM)],
        out_specs=[qspec, kvout, kvout, bspec, pl.BlockSpec((SWA_HEADS, 128), lambda n: (0, 0))],
        out_shape=[_sds((SWA_HEADS, s_len, HEAD_DIM), bf16), _sds((SWA_KV_HEADS, s_len, HEAD_DIM), bf16),
                   _sds((SWA_KV_HEADS, s_len, HEAD_DIM), bf16),
                   _sds((SWA_KV_HEADS, SWA_GROUP * BLOCK, 2 * BLOCK), f32), _sds((SWA_HEADS, 128), f32)],
        scratch_shapes=[pltpu.VMEM((SWA_KV_HEADS, ts + BLOCK, HEAD_DIM), f32),
                        pltpu.VMEM((SWA_KV_HEADS, ts + BLOCK, HEAD_DIM), f32),
                        pltpu.VMEM((SWA_KV_HEADS, ts, HEAD_DIM), f32),
                        pltpu.VMEM((SWA_KV_HEADS, ts, HEAD_DIM), f32),
                        pltpu.VMEM((SWA_KV_HEADS, SWA_GROUP * BLOCK, 1), f32)],
        compiler_params=_params(("arbitrary",)),
    )(q, k, k, v, v, do, lse, dl, bias, bias0, sink)


def _dx_call(dh, dqf, dkf, dvf, dfz, dqs, dks, dvs, dsz, dfft, w_t, tm):
    s_len = dh.shape[0]

    def body(dh_ref, dqf_ref, dkf_ref, dvf_ref, dfz_ref, dqs_ref, dks_ref, dvs_ref, dsz_ref, dfft_ref, w_ref,
             dx_ref, dp_ref):
        def cat(ref, nheads):
            return jnp.concatenate([ref[h] for h in range(nheads)], axis=1)

        def tr(ref):
            return ref[...].astype(f32).T.astype(bf16)

        dp = jnp.concatenate([tr(dqf_ref), tr(dkf_ref), tr(dvf_ref), dfz_ref[...], cat(dqs_ref, 8),
                              cat(dks_ref, 2), cat(dvs_ref, 2), dsz_ref[...], dfft_ref[...].T.astype(bf16)], axis=1)
        dp_ref[...] = dp
        dx_ref[...] = ALPHA * dh_ref[...] + jnp.dot(dp, w_ref[...], preferred_element_type=f32)

    def heads(nh):
        return pl.BlockSpec((nh, tm, HEAD_DIM), lambda i: (0, i, 0))

    half = pl.BlockSpec((tm, 512), lambda i: (i, 0))
    feat = pl.BlockSpec((512, tm), lambda i: (0, i))
    fullw = pl.BlockSpec((tm, D_MODEL), lambda i: (i, 0))
    return pl.pallas_call(
        body,
        name="dx_bwd",
        grid=(s_len // tm,),
        in_specs=[fullw, feat, feat, feat, half, heads(8), heads(2), heads(2), half,
                  pl.BlockSpec((128, tm), lambda i: (0, i)), pl.BlockSpec((A_W, D_MODEL), lambda i: (0, 0))],
        out_specs=[fullw, pl.BlockSpec((tm, A_W), lambda i: (i, 0))],
        out_shape=[_sds((s_len, D_MODEL), f32), _sds((s_len, A_W), bf16)],
        compiler_params=_params(("arbitrary",)),
    )(dh, dqf, dkf, dvf, dfz, dqs, dks, dvs, dsz, dfft, w_t)


DW_ROWS = 1152


def _dw_call(x2, dproj, tm):
    s_len = x2.shape[0]
    nt = s_len // tm

    def body(x_ref, dp_ref, dw_ref, acc_ref):
        i = pl.program_id(1)

        @pl.when(i == 0)
        def _():
            acc_ref[...] = jnp.zeros_like(acc_ref)

        acc_ref[...] += lax.dot_general(dp_ref[...], x_ref[...].astype(bf16), TN, preferred_element_type=f32)

        @pl.when(i == nt - 1)
        def _():
            dw_ref[...] = acc_ref[...].astype(bf16)

    return pl.pallas_call(
        body,
        name="dw_in_bwd",
        grid=(A_W // DW_ROWS, nt),
        in_specs=[pl.BlockSpec((tm, D_MODEL), lambda c, i: (i, 0)), pl.BlockSpec((tm, DW_ROWS), lambda c, i: (i, c))],
        out_specs=pl.BlockSpec((DW_ROWS, D_MODEL), lambda c, i: (c, 0)),
        out_shape=_sds((A_W, D_MODEL), bf16),
        scratch_shapes=[pltpu.VMEM((DW_ROWS, D_MODEL), f32)],
        compiler_params=_params(("arbitrary", "arbitrary")),
    )(x2, dproj)


def _adam_call(recv, w, m, v, tc, name):
    rows, cols = w.shape

    def body(r_ref, w_ref, m_ref, v_ref, g_ref, d_ref, mo_ref, vo_ref):
        g = r_ref[0].astype(f32)
        for p in range(1, N_DEV):
            g = g + r_ref[p].astype(f32)
        mn = ADAM_B1 * m_ref[...] + (1.0 - ADAM_B1) * g
        vn = ADAM_B2 * v_ref[...] + (1.0 - ADAM_B2) * (g * g)
        m_hat = mn / (1.0 - ADAM_B1 ** ADAM_STEP)
        v_hat = vn / (1.0 - ADAM_B2 ** ADAM_STEP)
        g_ref[...] = g
        d_ref[...] = -ADAM_LR * (m_hat / (jnp.sqrt(v_hat) + ADAM_EPS) + ADAM_WD * w_ref[...])
        mo_ref[...] = mn
        vo_ref[...] = vn

    blk = pl.BlockSpec((rows, tc), lambda i: (0, i))
    return pl.pallas_call(
        body,
        name=name,
        grid=(cols // tc,),
        in_specs=[pl.BlockSpec((N_DEV, rows, tc), lambda i: (0, 0, i)), blk, blk, blk],
        out_specs=[blk] * 4,
        out_shape=[_sds((rows, cols), f32)] * 4,
        compiler_params=_params(("arbitrary",)),
    )(recv, w, m, v)


def _pad_cols(a, width=128):
    return jnp.pad(a, ((0, 0), (0, width - a.shape[1])))


def _pack_small(ln_g, ln_b, rel, b_f, sink):
    return jnp.concatenate([
        ln_g.reshape(8, 128), ln_b.reshape(8, 128), _pad_cols(rel),
        jnp.pad(_pad_cols(b_f), ((0, 7), (0, 0))), jnp.pad(_pad_cols(sink), ((0, 7), (0, 0)))], axis=0)


def _unpack_small(p):
    return (p[0:8].reshape(1, D_MODEL), p[8:16].reshape(1, D_MODEL), p[16:48, 0:8], p[48:49, 0:8], p[56:57, 0:8])


def kernel(x, w_in, b_f, rel_bias, sink, w_o, ln_g, ln_b, loss_target, m_w_in, m_b_f, m_rel_bias, m_sink, m_w_o, m_ln_g, m_ln_b, v_w_in, v_b_f, v_rel_bias, v_sink, v_w_o, v_ln_g, v_ln_b):
    x2 = x[0]
    tgt = loss_target[0]
    s_len = x2.shape[0]
    shard = w_in.shape[2]

    w_in_t = jnp.transpose(w_in[0])
    g_in, g_o = _gather_call([w_in_t.astype(bf16), w_o[0].astype(bf16)])
    wt_full = g_in.reshape(N_DEV * shard, D_MODEL)
    w_t = jnp.concatenate([wt_full[:O_FF0], wt_full[O_FF1:], wt_full[O_FF0:O_FF1],
                           jnp.zeros((A_W - D_IN, D_MODEL), bf16)], axis=0)
    wo_full = g_o.reshape(D_MODEL, D_MODEL)

    qf, kf, vf, fz, qs, ks, vs, sz, fft, vat = _proj_call(x2, w_t, 512)
    cum, sgm = _cum_call(fft, b_f.reshape(FOX_HEADS, 1))
    qat, ka, kat, tile_stats = _augment_call(qf, kf, cum.reshape(FOX_HEADS, 1, s_len), 2048)
    npairs, pair_q, pair_k = _fox_prune_tables(tile_stats)
    o_ft, lse_f = _fox_fwd_call(qat, ka, vat, npairs, pair_q, pair_k)
    bucket = jnp.asarray(_t5_bucket_table())
    bias, bias0 = _swa_bias_call(rel_bias, bucket)
    sink_v = sink.reshape(SWA_HEADS)
    o_s, lse_s = _swa_fwd_call(qs, ks, vs, bias, bias0, sink_v)

    (dh, do_f, dfz, do_s, dsz, dl_f, dl_s, dwo, dg, db, loss_part) = _post_call(
        o_ft.reshape(FOX_HEADS * HEAD_DIM, s_len), fz, o_s, sz, x2, tgt, wo_full, ln_g, ln_b,
        jnp.asarray(_head_selector()), 256)

    dqf, dkf, dvf, dcq, dck = _fox_bwd_call(ka, kat, vf, qat, do_f.reshape(FOX_HEADS, HEAD_DIM, s_len), lse_f,
                                            dl_f.reshape(FOX_HEADS, 1, s_len), npairs, pair_q, pair_k)
    dqf, dkf, dvf = (a.reshape(FOX_HEADS * HEAD_DIM, s_len) for a in (dqf, dkf, dvf))
    dfft, dbf = _cum_bwd_call(dcq.reshape(FOX_HEADS, s_len), dck.reshape(FOX_HEADS, s_len), sgm)
    dqs, dks, dvs, dbias, dsink = _swa_bwd_call(qs, ks, vs, do_s, lse_s, dl_s, bias, bias0, sink_v)
    drel = _swa_bias_bwd_call(dbias, bucket)

    dx, dproj = _dx_call(dh, dqf, dkf, dvf, dfz, dqs, dks, dvs, dsz, dfft, w_t, 256)
    dw_t = _dw_call(x2, dproj, 1024)

    dwt_full = jnp.concatenate([dw_t[:O_FF0], dw_t[A_FF:A_FF + (O_FF1 - O_FF0)], dw_t[O_FF0:A_FF]], axis=0)
    dw_blocks = dwt_full.reshape(N_DEV, shard, D_MODEL)
    dwo_blocks = dwo.reshape(N_DEV, D_MODEL // N_DEV, D_MODEL).astype(bf16)
    small = _pack_small(dg, db, drel[:, 0:8], dbf[:, 0].reshape(1, 8), dsink[:, 0].reshape(1, 8))
    loss_slot = np.zeros((64, 128), bool)
    loss_slot[49, 0] = True
    small = jnp.where(jnp.asarray(loss_slot), loss_part[0, 0], small)
    small_blocks = jnp.broadcast_to(small[None], (N_DEV,) + small.shape)
    r_in, r_o, r_small = _exchange_call([dw_blocks, dwo_blocks, small_blocks])

    win_t = [jnp.transpose(a) for a in _adam_call(
        r_in, w_in_t, jnp.transpose(m_w_in[0]), jnp.transpose(v_w_in[0]), 256, "adam_w_in")]
    g_win, d_win, nm_win, nv_win = win_t
    g_wo, d_wo, nm_wo, nv_wo = _adam_call(r_o, w_o[0], m_w_o[0], v_w_o[0], 256, "adam_w_o")
    p_w = _pack_small(ln_g, ln_b, rel_bias, b_f, sink)
    p_m = _pack_small(m_ln_g, m_ln_b, m_rel_bias, m_b_f, m_sink)
    p_v = _pack_small(v_ln_g, v_ln_b, v_rel_bias, v_b_f, v_sink)
    g_p, d_p, nm_p, nv_p = _adam_call(r_small, p_w, p_m, p_v, 128, "adam_small")

    loss = g_p[49, 0]
    g_lng, g_lnb, g_rel, g_bf, g_sink = _unpack_small(g_p)
    d_lng, d_lnb, d_rel, d_bf, d_sink = _unpack_small(d_p)
    m_lng, m_lnb, m_rel, m_bf, m_sk = _unpack_small(nm_p)
    v_lng, v_lnb, v_rel, v_bf, v_sk = _unpack_small(nv_p)
    return (loss, dx[None], g_win[None], g_bf, g_rel, g_sink, g_wo[None], g_lng, g_lnb,
            d_win[None], d_bf, d_rel, d_sink, d_wo[None], d_lng, d_lnb,
            nm_win[None], m_bf, m_rel, m_sk, nm_wo[None], m_lng, m_lnb,
            nv_win[None], v_bf, v_rel, v_sk, nv_wo[None], v_lng, v_lnb)
```

```python
import functools
import math

import numpy as np
import jax
import jax.numpy as jnp
from jax import lax
from jax.experimental import pallas as pl
from jax.experimental.pallas import tpu as pltpu

f32 = jnp.float32
bf16 = jnp.bfloat16

D_MODEL = 1024
HEAD_DIM = 64
FOX_HEADS = 8
SWA_HEADS = 8
SWA_KV_HEADS = 2
SWA_GROUP = 4
BLOCK = 128
NUM_BUCKETS = 32
MAX_DISTANCE = 128
LN_EPS = 1e-5
NEG_INF = -1e30
ALPHA = 2.0 ** 0.25
SCALE = 1.0 / math.sqrt(HEAD_DIM)
D_IN = 3336

ADAM_LR = 0.001
ADAM_B1 = 0.9
ADAM_B2 = 0.999
ADAM_EPS = 1e-08
ADAM_WD = 0.01
ADAM_STEP = 10

N_DEV = 8
A_FQ, A_FK, A_FV, A_FZ, A_SQ, A_SK, A_SV, A_SZ, A_FF, A_W = 0, 512, 1024, 1536, 2048, 2560, 2688, 2816, 3328, 3456
O_FF0, O_FF1 = 1536, 1544

VMEM_LIMIT = 48 * 1024 * 1024
HIGHEST = lax.Precision.HIGHEST
NT = (((1,), (1,)), ((), ()))
TN = (((0,), (0,)), ((), ()))
MESH = pl.DeviceIdType.MESH
RELS = [(0, 0, 1), (0, 1, 0), (0, 1, 1), (1, 0, 0), (1, 0, 1), (1, 1, 0), (1, 1, 1)]


VMEM_LIMIT_BIG = 60 * 1024 * 1024


def _params(sem=None, vmem=VMEM_LIMIT):
    return pltpu.CompilerParams(dimension_semantics=sem, vmem_limit_bytes=vmem)


def _sds(shape, dtype):
    return jax.ShapeDtypeStruct(shape, dtype)


def _t5_bucket_table():
    qi = np.arange(BLOCK)[:, None]
    kj = np.arange(2 * BLOCK)[None, :]
    rel = qi + BLOCK - kj
    band = (rel >= 0) & (rel < BLOCK)
    relc = np.maximum(rel, 0)
    max_exact = NUM_BUCKETS // 2
    relf = np.maximum(relc, 1).astype(np.float32)
    large = max_exact + (np.log(relf / np.float32(max_exact)) / np.float32(math.log(MAX_DISTANCE / max_exact))
                         * np.float32(NUM_BUCKETS - max_exact)).astype(np.int32)
    large = np.minimum(large, NUM_BUCKETS - 1)
    bucket = np.where(relc < max_exact, relc, large).astype(np.int32)
    bucket = np.where(band, bucket, -1).astype(np.int32)
    return bucket


def _mesh_pos():
    return lax.axis_index("x"), lax.axis_index("y"), lax.axis_index("c")


def _dev_index(p):
    return 4 * p[0] + 2 * p[1] + p[2]


def _gather_call(xs):
    n = len(xs)

    def body(*refs):
        x_refs, o_refs = refs[:n], refs[n:2 * n]
        send_sems, recv_sems, local_sems = refs[2 * n:]
        x, y, c = _mesh_pos()
        me, sib = (x, y, c), (x, y, 1 - c)
        chips = [(1 - x, y), (x, 1 - y), (1 - x, 1 - y)]

        def copy(a, k, block, to, src=None):
            slot = o_refs[a].at[_dev_index(block)]
            return pltpu.make_async_remote_copy(
                src_ref=slot if src is None else src, dst_ref=slot,
                send_sem=send_sems.at[a * 7 + k], recv_sem=recv_sems.at[a * 7 + k],
                device_id=to, device_id_type=MESH)

        mine = [pltpu.make_async_copy(x_refs[a], o_refs[a].at[_dev_index(me)], local_sems.at[a]) for a in range(n)]
        for cp in mine:
            cp.start()
        first = []
        for a in range(n):
            first.append(copy(a, 0, me, sib, src=x_refs[a]))
            first += [copy(a, 1 + j, me, (*chip, c), src=x_refs[a]) for j, chip in enumerate(chips)]
        for cp in first:
            cp.start()
        passed = []
        for j, chip in enumerate(chips):
            for a in range(n):
                copy(a, 1 + j, (*chip, c), me).wait_recv()
                fwd = copy(a, 4 + j, (*chip, c), sib)
                fwd.start()
                passed.append(fwd)
        for a in range(n):
            copy(a, 0, sib, me).wait_recv()
            for j, chip in enumerate(chips):
                copy(a, 4 + j, (*chip, 1 - c), me).wait_recv()
        for cp in first + passed:
            cp.wait_send()
        for cp in mine:
            cp.wait()

    any_spec = pl.BlockSpec(memory_space=pl.ANY)
    return pl.pallas_call(
        body,
        name="gather_weights",
        out_shape=[_sds((N_DEV,) + a.shape, a.dtype) for a in xs],
        in_specs=[any_spec] * n,
        out_specs=[any_spec] * n,
        scratch_shapes=[pltpu.SemaphoreType.DMA((7 * n,)), pltpu.SemaphoreType.DMA((7 * n,)),
                        pltpu.SemaphoreType.DMA((n,))],
    )(*xs)


def _exchange_copies(b_refs, r_refs, send_sems, recv_sems, local_sems, incoming):
    n = len(b_refs)
    x, y, c = _mesh_pos()
    me_idx = _dev_index((x, y, c))
    mine = [pltpu.make_async_copy(b_refs[a].at[me_idx], r_refs[a].at[me_idx], local_sems.at[a]) for a in range(n)]
    remote = []
    for k, r in enumerate(RELS):
        peer = ((1 - x) if r[0] else x, (1 - y) if r[1] else y, (1 - c) if r[2] else c)
        pidx = _dev_index(peer)
        for a in range(n):
            remote.append(pltpu.make_async_remote_copy(
                src_ref=b_refs[a].at[pidx], dst_ref=r_refs[a].at[pidx if incoming else me_idx],
                send_sem=send_sems.at[a * 7 + k], recv_sem=recv_sems.at[a * 7 + k],
                device_id=peer, device_id_type=MESH))
    return mine, remote


def _exchange_start(b_refs, r_refs, sems):
    mine, out = _exchange_copies(b_refs, r_refs, *sems, incoming=False)
    for cp in mine + out:
        cp.start()


def _exchange_wait(b_refs, r_refs, sems):
    mine, inc = _exchange_copies(b_refs, r_refs, *sems, incoming=True)
    for cp in inc:
        cp.wait_recv()
    for cp in inc:
        cp.wait_send()
    for cp in mine:
        cp.wait()


def _proj_call(x2, w_t, tm):
    s_len = x2.shape[0]

    def body(x_ref, w_ref, qf_ref, kf_ref, vf_ref, fz_ref, qs_ref, ks_ref, vs_ref, sz_ref, fft_ref, vat_ref):
        xb = x_ref[...].astype(bf16)
        vt = lax.dot_general(w_ref[A_FV:A_FV + 512, :], xb, NT, preferred_element_type=f32)
        ones_row = jnp.where(lax.broadcasted_iota(jnp.int32, (HEAD_DIM, tm), 0) == 0, 1.0, 0.0).astype(bf16)
        for h in range(FOX_HEADS):
            vat_ref[h, 0:HEAD_DIM, :] = vt[h * HEAD_DIM:(h + 1) * HEAD_DIM, :].astype(bf16)
            vat_ref[h, HEAD_DIM:2 * HEAD_DIM, :] = ones_row

        def seg(off, width):
            return lax.dot_general(xb, w_ref[off:off + width, :], NT, preferred_element_type=f32)

        def put_heads(ref, acc, nheads):
            for h in range(nheads):
                ref[h] = acc[:, h * HEAD_DIM:(h + 1) * HEAD_DIM].astype(bf16)

        put_heads(qf_ref, seg(A_FQ, 512) * SCALE, FOX_HEADS)
        put_heads(kf_ref, seg(A_FK, 512), FOX_HEADS)
        put_heads(vf_ref, seg(A_FV, 512), FOX_HEADS)
        fz_ref[...] = seg(A_FZ, 512)
        put_heads(qs_ref, seg(A_SQ, 512) * SCALE, SWA_HEADS)
        put_heads(ks_ref, seg(A_SK, 128), SWA_KV_HEADS)
        put_heads(vs_ref, seg(A_SV, 128), SWA_KV_HEADS)
        sz_ref[...] = seg(A_SZ, 512)
        fft_ref[...] = seg(A_FF, 128).T[:FOX_HEADS, :]

    def heads(nh):
        return pl.BlockSpec((nh, tm, HEAD_DIM), lambda i: (0, i, 0))

    wide = pl.BlockSpec((tm, 512), lambda i: (i, 0))
    return pl.pallas_call(
        body,
        name="proj_fwd",
        grid=(s_len // tm,),
        in_specs=[pl.BlockSpec((tm, D_MODEL), lambda i: (i, 0)), pl.BlockSpec((A_W, D_MODEL), lambda i: (0, 0))],
        out_specs=[heads(8), heads(8), heads(8), wide, heads(8), heads(2), heads(2), wide,
                   pl.BlockSpec((FOX_HEADS, tm), lambda i: (0, i)),
                   pl.BlockSpec((FOX_HEADS, 2 * HEAD_DIM, tm), lambda i: (0, 0, i))],
        out_shape=[_sds((8, s_len, HEAD_DIM), bf16)] * 3 + [_sds((s_len, 512), f32), _sds((8, s_len, HEAD_DIM), bf16),
                   _sds((2, s_len, HEAD_DIM), bf16), _sds((2, s_len, HEAD_DIM), bf16), _sds((s_len, 512), f32),
                   _sds((FOX_HEADS, s_len), f32), _sds((FOX_HEADS, 2 * HEAD_DIM, s_len), bf16)],
        compiler_params=_params(("arbitrary",)),
    )(x2, w_t)


AUG = 2 * HEAD_DIM


def _augment_call(q, k, cum_row, tm):
    nh, s_len, _ = q.shape
    per_step = tm // FOX_T

    def body(q_ref, k_ref, c_ref, qat_ref, ka_ref, kat_ref, st_ref):
        c = c_ref[0]
        hi = c.astype(bf16).astype(f32)
        r1 = c - hi
        mid = r1.astype(bf16).astype(f32)
        lo = (r1 - mid).astype(bf16).astype(f32)
        row = lax.broadcasted_iota(jnp.int32, (HEAD_DIM, tm), 0)
        q_tail = jnp.where(row == 0, hi, jnp.where(row == 1, mid, jnp.where(row == 2, lo,
                           jnp.where(row < 6, 1.0, 0.0))))
        k_tail = jnp.where(row < 3, 1.0, jnp.where(row == 3, -hi, jnp.where(row == 4, -mid,
                           jnp.where(row == 5, -lo, 0.0))))
        qt = q_ref[0].astype(f32).T
        kt = k_ref[0].astype(f32).T
        qat_ref[0, 0:HEAD_DIM, :] = qt.astype(bf16)
        qat_ref[0, HEAD_DIM:AUG, :] = q_tail.astype(bf16)
        ka_ref[0] = jnp.concatenate([k_ref[0], k_tail.T.astype(bf16)], axis=1)
        kat_ref[0, 0:HEAD_DIM, :] = kt.astype(bf16)
        kat_ref[0, HEAD_DIM:AUG, :] = k_tail.astype(bf16)
        qn2 =jnp.sum(qt * qt, axis=0, keepdims=True)
        kn2 = jnp.sum(kt * kt, axis=0, keepdims=True)
        sd = jnp.sum(qt * kt, axis=0, keepdims=True)
        srow = lax.broadcasted_iota(jnp.int32, (8, LANES), 0)
        for part in range(per_step):
            sl = slice(part * FOX_T, (part + 1) * FOX_T)
            vals = [jnp.sqrt(jnp.max(qn2[:, sl], axis=1, keepdims=True)),
                    jnp.sqrt(jnp.max(kn2[:, sl], axis=1, keepdims=True)),
                    jnp.min(sd[:, sl], axis=1, keepdims=True),
                    jnp.max(c[:, sl], axis=1, keepdims=True), jnp.min(c[:, sl], axis=1, keepdims=True)]
            out = jnp.zeros((8, LANES), f32)
            for r, val in enumerate(vals):
                out = jnp.where(srow == r, val, out)
            st_ref[0, part] = out

    tile = pl.BlockSpec((1, tm, HEAD_DIM), lambda h, i: (h, i, 0))
    return pl.pallas_call(
        body,
        name="fox_augment",
        grid=(nh, s_len // tm),
        in_specs=[tile, tile, pl.BlockSpec((1, 1, tm), lambda h, i: (h, 0, i))],
        out_specs=[pl.BlockSpec((1, AUG, tm), lambda h, i: (h, 0, i)),
                   pl.BlockSpec((1, tm, AUG), lambda h, i: (h, i, 0)),
                   pl.BlockSpec((1, AUG, tm), lambda h, i: (h, 0, i)),
                   pl.BlockSpec((1, per_step, 8, LANES), lambda h, i: (h, i, 0, 0))],
        out_shape=[_sds((nh, AUG, s_len), bf16), _sds((nh, s_len, AUG), bf16), _sds((nh, AUG, s_len), bf16),
                   _sds((nh, s_len // FOX_T, 8, LANES), f32)],
        compiler_params=_params(("arbitrary", "arbitrary")),
    )(q, k, cum_row)


EXP_ZERO_GAP = 110.0


def _fox_prune_tables(stats):
    s = stats[:, :, :, 0]
    qn, kn, sd, cmx, cmn = (s[:, :, r] for r in range(5))
    nt = s.shape[1]
    bound = qn[:, :, None] * kn[:, None, :] + (cmx[:, :, None] - cmn[:, None, :])
    margin = 2.0 + 1e-5 * (jnp.abs(cmx)[:, :, None] + jnp.abs(cmn)[:, None, :])
    qi = lax.broadcasted_iota(jnp.int32, (nt, nt), 0)
    kj = lax.broadcasted_iota(jnp.int32, (nt, nt), 1)
    skip = (bound + margin < sd[:, :, None] - EXP_ZERO_GAP) & (kj < qi)[None]
    first = jnp.sum(jnp.cumprod(skip.astype(jnp.int32), axis=2), axis=2)
    tiles = lax.broadcasted_iota(jnp.int32, (1, nt), 1)
    cnt = tiles - first
    ends = jnp.cumsum(cnt, axis=1)
    off = ends - cnt
    kmax = nt * (nt - 1) // 2
    k = lax.broadcasted_iota(jnp.int32, (1, kmax), 1)
    pair_q = jnp.minimum(jnp.sum((ends[:, None, :] <= k[:, :, None]).astype(jnp.int32), axis=2), nt - 1)
    hit = pair_q[:, :, None] == tiles[:, None, :]
    first_k = jnp.sum(jnp.where(hit, first[:, None, :], 0), axis=2)
    off_k = jnp.sum(jnp.where(hit, off[:, None, :], 0), axis=2)
    pair_k = jnp.clip(first_k + k - off_k, 0, nt - 1)
    return (ends[:, nt - 1].astype(jnp.int32), pair_q.reshape(-1).astype(jnp.int32),
            pair_k.reshape(-1).astype(jnp.int32))


CUM_CHUNK = 512


def _cum_call(fft, bf_col):
    s_len = fft.shape[1]
    ch = CUM_CHUNK

    def body(f_ref, b_ref, cum_ref, sg_ref):
        r = lax.broadcasted_iota(jnp.int32, (ch, ch), 0)
        c = lax.broadcasted_iota(jnp.int32, (ch, ch), 1)
        upper = (r <= c).astype(f32)
        carry = jnp.zeros((FOX_HEADS, 1), f32)
        for n in range(s_len // ch):
            z = f_ref[:, n * ch:(n + 1) * ch] + b_ref[...]
            logf = jnp.minimum(z, 0.0) - jnp.log1p(jnp.exp(-jnp.abs(z)))
            sg_ref[:, n * ch:(n + 1) * ch] = 1.0 / (1.0 + jnp.exp(z))
            cs = jnp.dot(logf, upper, precision=HIGHEST, preferred_element_type=f32) + carry
            cum_ref[:, n * ch:(n + 1) * ch] = cs
            carry = cs[:, ch - 1:ch]

    return pl.pallas_call(
        body,
        name="fox_cum_fwd",
        out_shape=[_sds((FOX_HEADS, s_len), f32)] * 2,
        compiler_params=_params(),
    )(fft, bf_col)


def _cum_bwd_call(dcq, dck, sg):
    s_len = sg.shape[1]
    ch = CUM_CHUNK
    nch = s_len // ch

    def body(q_ref, k_ref, sg_ref, dff_ref, dbf_ref):
        r = lax.broadcasted_iota(jnp.int32, (ch, ch), 0)
        c = lax.broadcasted_iota(jnp.int32, (ch, ch), 1)
        lower = (r >= c).astype(f32)
        dff_ref[...] = jnp.zeros_like(dff_ref)
        carry = jnp.zeros((FOX_HEADS, 1), f32)
        total = jnp.zeros((FOX_HEADS, 1), f32)
        for n in reversed(range(nch)):
            sl = slice(n * ch, (n + 1) * ch)
            dcum = q_ref[:, sl] - k_ref[:, sl]
            rs = jnp.dot(dcum, lower, precision=HIGHEST, preferred_element_type=f32) + carry
            carry = rs[:, 0:1]
            dff = rs * sg_ref[:, sl]
            dff_ref[0:FOX_HEADS, sl] = dff
            total = total + jnp.sum(dff, axis=1, keepdims=True)
        dbf_ref[...] = jnp.broadcast_to(total, (FOX_HEADS, 128))

    return pl.pallas_call(
        body,
        name="fox_cum_bwd",
        out_shape=[_sds((128, s_len), f32), _sds((FOX_HEADS, 128), f32)],
        compiler_params=_params(),
    )(dcq, dck, sg)


FOX_T = 512
LANES = 128


def _causal_keep(t):
    return lax.broadcasted_iota(jnp.int32, (t, t), 0) <= lax.broadcasted_iota(jnp.int32, (t, t), 1)


def _tile_cols(i, t):
    return pl.ds(pl.multiple_of(i * t, t), t)


def _fox_pair(n, nt, kmax, h, pq_ref, pk_ref):
    k = h * kmax + jnp.maximum(n - nt, 0)
    return jnp.where(n < nt, n, pq_ref[k]), jnp.where(n < nt, n, pk_ref[k])


def _fox_fwd_call(qat, ka, vat, npairs, pair_q, pair_k):
    nh, s_len, _ = ka.shape
    t = FOX_T
    nt = s_len // t
    kmax = nt * (nt - 1) // 2
    assert nt >= 2 and nt % 2 == 0

    def body(np_ref, pq_ref, pk_ref, qat_ref, ka_ref, vat_ref, o_ref, lse_ref, s0, s1, p0, p1, a0, a1, m_all, acc_all):
        h = pl.program_id(0)
        extra = np_ref[h]
        total = nt + extra
        m_all[...] = jnp.full(m_all.shape, NEG_INF, f32)
        acc_all[...] = jnp.zeros(acc_all.shape, f32)
        bufs = ((s0, p0, a0), (s1, p1, a1))

        def pair(n):
            return _fox_pair(n, nt, kmax, h, pq_ref, pk_ref)

        def scores(n, b, masked):
            i, j = pair(n)
            st = jnp.dot(ka_ref[0, _tile_cols(j, t), :], qat_ref[0, :, _tile_cols(i, t)], preferred_element_type=f32)
            if masked:
                st = jnp.where(_causal_keep(t), st, NEG_INF)
            bufs[b][0][...] = st

        def softmax(n, b):
            i, _ = pair(n)
            s_ref, p_ref, a_ref = bufs[b]
            for c in range(t // LANES):
                cols = slice(c * LANES, (c + 1) * LANES)
                mcols = pl.ds(pl.multiple_of(i * t + c * LANES, LANES), LANES)
                m_old = m_all[:, mcols]
                m_new = jnp.maximum(m_old, jnp.max(s_ref[:, cols], axis=0, keepdims=True))
                m_all[:, mcols] = m_new
                a_ref[:, cols] = jnp.exp(m_old - m_new)
                p_ref[:, cols] = jnp.exp(s_ref[:, cols] - m_new).astype(bf16)

        def accum(n, b):
            i, j = pair(n)
            cols = _tile_cols(i, t)
            acc_all[:, cols] = bufs[b][2][...] * acc_all[:, cols] + jnp.dot(
                vat_ref[0, :, _tile_cols(j, t)], bufs[b][1][...], preferred_element_type=f32)

        def step(n, b, masked):
            scores(n, b, masked)
            softmax(n - 1, 1 - b)
            accum(n - 2, b)

        scores(0, 0, True)
        scores(1, 1, True)
        softmax(0, 0)

        def diag_steps(d, _):
            n = 2 + 2 * d
            step(n, 0, True)
            step(n + 1, 1, True)
            return 0

        lax.fori_loop(0, (nt - 2) // 2, diag_steps, 0)

        def off_steps(d, _):
            n = nt + 2 * d
            step(n, 0, False)
            step(n + 1, 1, False)
            return 0

        lax.fori_loop(0, extra // 2, off_steps, 0)

        @pl.when(extra % 2 == 1)
        def _():
            step(total - 1, 0, False)
            softmax(total - 1, 0)
            accum(total - 2, 1)
            accum(total - 1, 0)

        @pl.when(extra % 2 == 0)
        def _():
            softmax(total - 1, 1)
            accum(total - 2, 0)
            accum(total - 1, 1)

        l = acc_all[HEAD_DIM:HEAD_DIM + 1, :]
        o_ref[0] = acc_all[0:HEAD_DIM, :] / l
        lse_ref[0] = m_all[...] + jnp.log(l)

    smem = pl.BlockSpec(memory_space=pltpu.SMEM)
    return pl.pallas_call(
        body,
        name="fox_fwd",
        grid=(nh,),
        in_specs=[smem, smem, smem,
                  pl.BlockSpec((1, AUG, s_len), lambda h: (h, 0, 0)),
                  pl.BlockSpec((1, s_len, AUG), lambda h: (h, 0, 0)),
                  pl.BlockSpec((1, AUG, s_len), lambda h: (h, 0, 0))],
        out_specs=[pl.BlockSpec((1, HEAD_DIM, s_len), lambda h: (h, 0, 0)),
                   pl.BlockSpec((1, 1, s_len), lambda h: (h, 0, 0))],
        out_shape=[_sds((nh, HEAD_DIM, s_len), f32), _sds((nh, 1, s_len), f32)],
        scratch_shapes=[pltpu.VMEM((t, t), f32), pltpu.VMEM((t, t), f32), pltpu.VMEM((t, t), bf16),
                        pltpu.VMEM((t, t), bf16), pltpu.VMEM((1, t), f32), pltpu.VMEM((1, t), f32),
                        pltpu.VMEM((1, s_len), f32), pltpu.VMEM((AUG, s_len), f32)],
        compiler_params=_params(("arbitrary",)),
    )(npairs, pair_q, pair_k, qat, ka, vat)


SWA_TS = 512


def _swa_bias_call(rel_bias, bucket):
    def body(rb_ref, bk_ref, b_ref, b0_ref):
        bk = bk_ref[...]
        col = lax.broadcasted_iota(jnp.int32, (BLOCK, 2 * BLOCK), 1)
        for h in range(SWA_HEADS):
            acc = jnp.full((BLOCK, 2 * BLOCK), NEG_INF, f32)
            for b in range(NUM_BUCKETS):
                acc = jnp.where(bk == b, rb_ref[b, h], acc)
            g, hh = divmod(h, SWA_GROUP)
            b_ref[g, hh * BLOCK:(hh + 1) * BLOCK, :] = acc
            b0_ref[g, hh * BLOCK:(hh + 1) * BLOCK, :] = jnp.where(col < BLOCK, NEG_INF, acc)

    return pl.pallas_call(
        body,
        name="swa_bias",
        in_specs=[pl.BlockSpec(memory_space=pltpu.SMEM), pl.BlockSpec(memory_space=pltpu.VMEM)],
        out_shape=[_sds((SWA_KV_HEADS, SWA_GROUP * BLOCK, 2 * BLOCK), f32)] * 2,
        compiler_params=_params(),
    )(rel_bias, bucket)


def _swa_bias_bwd_call(dbias, bucket):
    def body(d_ref, bk_ref, o_ref):
        bk = bk_ref[...]
        row = lax.broadcasted_iota(jnp.int32, (NUM_BUCKETS, 128), 0)
        col = lax.broadcasted_iota(jnp.int32, (NUM_BUCKETS, 128), 1)
        out = jnp.zeros((NUM_BUCKETS, 128), f32)
        for h in range(SWA_HEADS):
            g, hh = divmod(h, SWA_GROUP)
            d = d_ref[g, hh * BLOCK:(hh + 1) * BLOCK, :]
            for b in range(NUM_BUCKETS):
                val = jnp.sum(jnp.sum(jnp.where(bk == b, d, 0.0), axis=1, keepdims=True), axis=0, keepdims=True)
                out = jnp.where((row == b) & (col == h), val, out)
        o_ref[...] = out

    return pl.pallas_call(
        body,
        name="swa_bias_bwd",
        out_shape=_sds((NUM_BUCKETS, 128), f32),
        compiler_params=_params(),
    )(dbias, bucket)


def _swa_specs(ts):
    nb = ts // BLOCK
    qspec = pl.BlockSpec((SWA_HEADS, ts, HEAD_DIM), lambda n: (0, n, 0))
    cur = pl.BlockSpec((SWA_KV_HEADS, ts, HEAD_DIM), lambda n: (0, n, 0))
    prev = pl.BlockSpec((SWA_KV_HEADS, BLOCK, HEAD_DIM), lambda n: (0, jnp.maximum(n * nb - 1, 0), 0))
    return qspec, cur, prev


def _sink_col(sink_ref, g):
    return jnp.concatenate([jnp.full((BLOCK, 1), sink_ref[g * SWA_GROUP + hh], f32) for hh in range(SWA_GROUP)], axis=0)


def _swa_fwd_call(q, k, v, bias, bias0, sink):
    s_len = q.shape[1]
    ts = SWA_TS
    nb = ts // BLOCK

    def body(q_ref, kc_ref, kp_ref, vc_ref, vp_ref, b_ref, b0_ref, sink_ref, o_ref, lse_ref):
        first = pl.program_id(0) == 0
        for g in range(SWA_KV_HEADS):
            kall = jnp.concatenate([kp_ref[g], kc_ref[g]], axis=0)
            vall = jnp.concatenate([vp_ref[g], vc_ref[g]], axis=0)
            sink_c = _sink_col(sink_ref, g)
            for b in range(nb):
                rows = slice(b * BLOCK, (b + 1) * BLOCK)
                qg = jnp.concatenate([q_ref[g * SWA_GROUP + hh, rows, :] for hh in range(SWA_GROUP)], axis=0)
                kcat = kall[b * BLOCK:(b + 2) * BLOCK]
                vcat = vall[b * BLOCK:(b + 2) * BLOCK]
                bias_b = b_ref[g]
                if b == 0:
                    bias_b = jnp.where(first, b0_ref[g], bias_b)
                s = lax.dot_general(qg, kcat, NT, preferred_element_type=f32) + bias_b
                m = jnp.maximum(jnp.max(s, axis=1, keepdims=True), sink_c)
                p = jnp.exp(s - m)
                l = jnp.sum(p, axis=1, keepdims=True) + jnp.exp(sink_c - m)
                o = jnp.dot(p.astype(bf16), vcat, preferred_element_type=f32) / l
                lse = m + jnp.log(l)
                for hh in range(SWA_GROUP):
                    o_ref[g * SWA_GROUP + hh, rows, :] = o[hh * BLOCK:(hh + 1) * BLOCK]
                    lse_ref[g * SWA_GROUP + hh, rows, :] = lse[hh * BLOCK:(hh + 1) * BLOCK]

    qspec, cur, prev = _swa_specs(ts)
    bspec = pl.BlockSpec((SWA_KV_HEADS, SWA_GROUP * BLOCK, 2 * BLOCK), lambda n: (0, 0, 0))
    return pl.pallas_call(
        body,
        name="swa_fwd",
        grid=(s_len // ts,),
        in_specs=[qspec, cur, prev, cur, prev, bspec, bspec, pl.BlockSpec(memory_space=pltpu.SMEM)],
        out_specs=[pl.BlockSpec((SWA_HEADS, ts, HEAD_DIM), lambda n: (0, n, 0)),
                   pl.BlockSpec((SWA_HEADS, ts, 1), lambda n: (0, n, 0))],
        out_shape=[_sds((SWA_HEADS, s_len, HEAD_DIM), f32), _sds((SWA_HEADS, s_len, 1), f32)],
        compiler_params=_params(("arbitrary",)),
    )(q, k, k, v, v, bias, bias0, sink)


def _head_selector():
    sel = np.zeros((512, 128), np.float32)
    for h in range(8):
        sel[h * HEAD_DIM:(h + 1) * HEAD_DIM, h] = 1.0
    return sel


def _post_call(of, fz, osw, sz, x2, tgt, wo, ln_g, ln_b, sel, tm):
    s_len = x2.shape[0]

    def body(of_ref, fz_ref, os_ref, sz_ref, x_ref, t_ref, wo_ref, g_ref, b_ref, sel_ref,
             dh_ref, dof_ref, dfz_ref, dos_ref, dsz_ref, dlf_ref, dls_ref, dwo_ref, dg_ref, db_ref, loss_ref):
        n = pl.program_id(0)

        @pl.when(n == 0)
        def _():
            dwo_ref[...] = jnp.zeros_like(dwo_ref)
            dg_ref[...] = jnp.zeros_like(dg_ref)
            db_ref[...] = jnp.zeros_like(db_ref)
            loss_ref[...] = jnp.zeros_like(loss_ref)

        o_f = of_ref[...].T
        o_s = jnp.concatenate([os_ref[h] for h in range(SWA_HEADS)], axis=1)
        fz = fz_ref[...]
        sz = sz_ref[...]
        sg_f = jax.nn.sigmoid(fz)
        sg_s = jax.nn.sigmoid(sz)
        silu_f = fz * sg_f
        silu_s = sz * sg_s
        mixed = jnp.concatenate([o_f * silu_f, o_s * silu_s], axis=1).astype(bf16)
        y = jnp.dot(mixed, wo_ref[...], preferred_element_type=f32)
        h = ALPHA * x_ref[...] + y
        mu = jnp.mean(h, axis=1, keepdims=True)
        hc = h - mu
        var = jnp.mean(hc * hc, axis=1, keepdims=True)
        rstd = lax.rsqrt(var + LN_EPS)
        xhat = hc * rstd
        gam = g_ref[...]
        out = xhat * gam + b_ref[...]
        err = out - t_ref[...]
        tok_loss = jnp.mean(err * err, axis=1, keepdims=True)
        loss_ref[...] += 0.5 * jnp.sum(tok_loss, axis=0, keepdims=True)
        dout = err * (1.0 / D_MODEL)
        dg_ref[...] += jnp.sum(dout * xhat, axis=0, keepdims=True)
        db_ref[...] += jnp.sum(dout, axis=0, keepdims=True)
        dxh = dout * gam
        m1 = jnp.mean(dxh, axis=1, keepdims=True)
        m2 = jnp.mean(dxh * xhat, axis=1, keepdims=True)
        dh = rstd * (dxh - m1 - xhat * m2)
        dh_ref[...] = dh
        dyb = dh.astype(bf16)
        dwo_ref[...] += lax.dot_general(mixed, dyb, TN, preferred_element_type=f32)
        dmix = lax.dot_general(dyb, wo_ref[...], NT, preferred_element_type=f32)
        dm_f = dmix[:, :512]
        dm_s = dmix[:, 512:]
        do_f = dm_f * silu_f
        do_s = dm_s * silu_s
        dfz_ref[...] = (dm_f * o_f * (sg_f * (1.0 + fz * (1.0 - sg_f)))).astype(bf16)
        dsz_ref[...] = (dm_s * o_s * (sg_s * (1.0 + sz * (1.0 - sg_s)))).astype(bf16)
        dof_ref[...] = do_f.T.astype(bf16)
        for hd in range(SWA_HEADS):
            dos_ref[hd] = do_s[:, hd * HEAD_DIM:(hd + 1) * HEAD_DIM].astype(bf16)
        sel_m = sel_ref[...]
        dl_f = jnp.dot(do_f * o_f, sel_m, precision=HIGHEST, preferred_element_type=f32)
        dl_s = jnp.dot(do_s * o_s, sel_m, precision=HIGHEST, preferred_element_type=f32)
        dlf_ref[...] = dl_f.T[:FOX_HEADS, :]
        dls_ref[...] = dl_s

    heads_f32 = pl.BlockSpec((8, tm, HEAD_DIM), lambda n: (0, n, 0))
    half = pl.BlockSpec((tm, 512), lambda n: (n, 0))
    fullw = pl.BlockSpec((tm, D_MODEL), lambda n: (n, 0))
    vec = pl.BlockSpec((1, D_MODEL), lambda n: (0, 0))
    return pl.pallas_call(
        body,
        name="post_fwd_bwd",
        grid=(s_len // tm,),
        in_specs=[pl.BlockSpec((512, tm), lambda n: (0, n)), half, heads_f32, half, fullw, fullw,
                  pl.BlockSpec((D_MODEL, D_MODEL), lambda n: (0, 0)), vec, vec,
                  pl.BlockSpec((512, 128), lambda n: (0, 0))],
        out_specs=[fullw, pl.BlockSpec((512, tm), lambda n: (0, n)), half, heads_f32, half,
                   pl.BlockSpec((FOX_HEADS, tm), lambda n: (0, n)), pl.BlockSpec((tm, 128), lambda n: (n, 0)),
                   pl.BlockSpec((D_MODEL, D_MODEL), lambda n: (0, 0)), vec, vec,
                   pl.BlockSpec((1, 1), lambda n: (0, 0))],
        out_shape=[_sds((s_len, D_MODEL), f32), _sds((512, s_len), bf16), _sds((s_len, 512), bf16),
                   _sds((8, s_len, HEAD_DIM), bf16), _sds((s_len, 512), bf16),
                   _sds((FOX_HEADS, s_len), f32), _sds((s_len, 128), f32),
                   _sds((D_MODEL, D_MODEL), f32), _sds((1, D_MODEL), f32), _sds((1, D_MODEL), f32),
                   _sds((1, 1), f32)],
        compiler_params=_params(("arbitrary",)),
    )(of, fz, osw, sz, x2, tgt, wo, ln_g, ln_b, sel)


def _fox_bwd_call(ka, kat, v, qat, dot, lse_row, dl_row, npairs, pair_q, pair_k):
    nh, s_len, _ = ka.shape
    t = FOX_T
    nt = s_len // t
    kmax = nt * (nt - 1) // 2
    assert nt >= 2 and nt % 2 == 0
    ck_slot = HEAD_DIM + 3
    cq_slot = HEAD_DIM

    def body(np_ref, pq_ref, pk_ref, ka_ref, kat_ref, v_ref, qat_ref, dot_ref, lse_ref, dl_ref,
             dq_ref, dk_ref, dv_ref, dcq_ref, dck_ref, dqt_all, dkat_all, dvt_all, p0, p1, ds0, ds1):
        h = pl.program_id(0)
        extra = np_ref[h]
        total = nt + extra
        dqt_all[...] = jnp.zeros(dqt_all.shape, f32)
        dkat_all[...] = jnp.zeros(dkat_all.shape, f32)
        dvt_all[...] = jnp.zeros(dvt_all.shape, f32)
        pbuf, dsbuf = (p0, p1), (ds0, ds1)

        def pair(n):
            return _fox_pair(n, nt, kmax, h, pq_ref, pk_ref)

        def probs(n, b, masked):
            i, j = pair(n)
            qc, kr = _tile_cols(i, t), _tile_cols(j, t)
            st = jnp.dot(ka_ref[0, kr, :], qat_ref[0, :, qc], preferred_element_type=f32)
            if masked:
                st = jnp.where(_causal_keep(t), st, NEG_INF)
            pt = jnp.exp(st - lse_ref[0, :, qc])
            dpt = jnp.dot(v_ref[0, kr, :], dot_ref[0, :, qc], preferred_element_type=f32)
            pbuf[b][...] = pt.astype(bf16)
            dsbuf[b][...] = (pt * (dpt - dl_ref[0, :, qc])).astype(bf16)

        def grads(n, b):
            i, j = pair(n)
            qc, kc = _tile_cols(i, t), _tile_cols(j, t)
            dvt_all[:, kc] += lax.dot_general(dot_ref[0, :, qc], pbuf[b][...], NT, preferred_element_type=f32)
            dkat_all[:, kc] += lax.dot_general(qat_ref[0, :, qc], dsbuf[b][...], NT, preferred_element_type=f32)
            dqt_all[:, qc] += jnp.dot(kat_ref[0, :, kc], dsbuf[b][...], preferred_element_type=f32)

        def step(n, b, masked):
            probs(n, b, masked)
            grads(n - 1, 1 - b)

        probs(0, 0, True)
        step(1, 1, True)

        def diag_steps(d, _):
            n = 2 + 2 * d
            step(n, 0, True)
            step(n + 1, 1, True)
            return 0

        lax.fori_loop(0, (nt - 2) // 2, diag_steps, 0)

        def off_steps(d, _):
            n = nt + 2 * d
            step(n, 0, False)
            step(n + 1, 1, False)
            return 0

        lax.fori_loop(0, extra // 2, off_steps, 0)

        @pl.when(extra % 2 == 1)
        def _():
            step(total - 1, 0, False)
            grads(total - 1, 0)

        @pl.when(extra % 2 == 0)
        def _():
            grads(total - 1, 1)

        dq_ref[0] = (dqt_all[0:HEAD_DIM, :] * SCALE).astype(bf16)
        dk_ref[0] = dkat_all[0:HEAD_DIM, :].astype(bf16)
        dv_ref[0] = dvt_all[...].astype(bf16)
        dcq_ref[0] = dqt_all[cq_slot:cq_slot + 1, :]
        dck_ref[0] = dkat_all[ck_slot:ck_slot + 1, :]

    smem = pl.BlockSpec(memory_space=pltpu.SMEM)
    rows = pl.BlockSpec((1, s_len, AUG), lambda h: (h, 0, 0))
    feat = pl.BlockSpec((1, AUG, s_len), lambda h: (h, 0, 0))
    feat64 = pl.BlockSpec((1, HEAD_DIM, s_len), lambda h: (h, 0, 0))
    rowv = pl.BlockSpec((1, 1, s_len), lambda h: (h, 0, 0))
    return pl.pallas_call(
        body,
        name="fox_bwd",
        grid=(nh,),
        in_specs=[smem, smem, smem, rows, feat, pl.BlockSpec((1, s_len, HEAD_DIM), lambda h: (h, 0, 0)), feat, feat64,
                  rowv, rowv],
        out_specs=[feat64, feat64, feat64, rowv, rowv],
        out_shape=[_sds((nh, HEAD_DIM, s_len), bf16)] * 3 + [_sds((nh, 1, s_len), f32)] * 2,
        scratch_shapes=[pltpu.VMEM((AUG, s_len), f32), pltpu.VMEM((AUG, s_len), f32), pltpu.VMEM((HEAD_DIM, s_len), f32)]
                       + [pltpu.VMEM((t, t), bf16)] * 4,
        compiler_params=_params(("arbitrary",)),
    )(npairs, pair_q, pair_k, ka, kat, v, qat, dot, lse_row, dl_row)


def _swa_bwd_call(q, k, v, do, lse, dl, bias, bias0, sink):
    s_len = q.shape[1]
    ts = SWA_TS
    nb = ts // BLOCK
    nsteps = s_len // ts

    def body(q_ref, kc_ref, kp_ref, vc_ref, vp_ref, do_ref, lse_ref, dl_ref, b_ref, b0_ref, sink_ref,
             dq_ref, dk_ref, dv_ref, dbias_ref, dsink_ref, dk_s, dv_s, tail_k, tail_v, sk_s):
        n = pl.program_id(0)

        @pl.when(n == 0)
        def _():
            dbias_ref[...] = jnp.zeros_like(dbias_ref)
            sk_s[...] = jnp.zeros_like(sk_s)

        @pl.when(n < nsteps)
        def _():
            first = n == 0
            dk_s[...] = jnp.zeros_like(dk_s)
            dv_s[...] = jnp.zeros_like(dv_s)
            for g in range(SWA_KV_HEADS):
                kall = jnp.concatenate([kp_ref[g], kc_ref[g]], axis=0)
                vall = jnp.concatenate([vp_ref[g], vc_ref[g]], axis=0)
                sink_c = _sink_col(sink_ref, g)
                for b in range(nb):
                    rows = slice(b * BLOCK, (b + 1) * BLOCK)
                    heads = [g * SWA_GROUP + hh for hh in range(SWA_GROUP)]
                    qg = jnp.concatenate([q_ref[h, rows, :] for h in heads], axis=0)
                    dog = jnp.concatenate([do_ref[h, rows, :] for h in heads], axis=0)
                    lse_c = jnp.concatenate([lse_ref[h, rows, :] for h in heads], axis=0)
                    dl_c = jnp.concatenate([dl_ref[rows, h:h + 1] for h in heads], axis=0)
                    kcat = kall[b * BLOCK:(b + 2) * BLOCK]
                    vcat = vall[b * BLOCK:(b + 2) * BLOCK]
                    bias_b = b_ref[g]
                    if b == 0:
                        bias_b = jnp.where(first, b0_ref[g], bias_b)
                    s = lax.dot_general(qg, kcat, NT, preferred_element_type=f32) + bias_b
                    p = jnp.exp(s - lse_c)
                    dp = lax.dot_general(dog, vcat, NT, preferred_element_type=f32)
                    ds = p * (dp - dl_c)
                    dsb = ds.astype(bf16)
                    dqg = jnp.dot(dsb, kcat, preferred_element_type=f32) * SCALE
                    for hh, h in enumerate(heads):
                        dq_ref[h, rows, :] = dqg[hh * BLOCK:(hh + 1) * BLOCK].astype(bf16)
                    win = slice(b * BLOCK, (b + 2) * BLOCK)
                    dk_s[g, win, :] += lax.dot_general(dsb, qg, TN, preferred_element_type=f32)
                    dv_s[g, win, :] += lax.dot_general(p.astype(bf16), dog, TN, preferred_element_type=f32)
                    dbias_ref[g] += ds
                    sk_s[g] += -jnp.exp(sink_c - lse_c) * dl_c

        @pl.when(n > 0)
        def _():
            last = slice(ts - BLOCK, ts)
            for g in range(SWA_KV_HEADS):
                add_k = jnp.where(n < nsteps, dk_s[g, 0:BLOCK, :], 0.0)
                add_v = jnp.where(n < nsteps, dv_s[g, 0:BLOCK, :], 0.0)
                dk_ref[g, 0:ts - BLOCK, :] = tail_k[g, 0:ts - BLOCK, :].astype(bf16)
                dv_ref[g, 0:ts - BLOCK, :] = tail_v[g, 0:ts - BLOCK, :].astype(bf16)
                dk_ref[g, last, :] = (tail_k[g, last, :] + add_k).astype(bf16)
                dv_ref[g, last, :] = (tail_v[g, last, :] + add_v).astype(bf16)

        @pl.when(n < nsteps)
        def _():
            tail_k[...] = dk_s[:, BLOCK:, :]
            tail_v[...] = dv_s[:, BLOCK:, :]

        @pl.when(n == nsteps)
        def _():
            row = lax.broadcasted_iota(jnp.int32, (SWA_HEADS, 128), 0)
            out = jnp.zeros((SWA_HEADS, 128), f32)
            for h in range(SWA_HEADS):
                g, hh = divmod(h, SWA_GROUP)
                val = jnp.sum(sk_s[g, hh * BLOCK:(hh + 1) * BLOCK, :], axis=0, keepdims=True)
                out = jnp.where(row == h, val, out)
            dsink_ref[...] = out

    last_step = nsteps - 1

    def cl(n):
        return jnp.minimum(n, last_step)

    qspec = pl.BlockSpec((SWA_HEADS, ts, HEAD_DIM), lambda n: (0, cl(n), 0))
    cur = pl.BlockSpec((SWA_KV_HEADS, ts, HEAD_DIM), lambda n: (0, cl(n), 0))
    prev = pl.BlockSpec((SWA_KV_HEADS, BLOCK, HEAD_DIM), lambda n: (0, jnp.maximum(cl(n) * nb - 1, 0), 0))
    lsespec = pl.BlockSpec((SWA_HEADS, ts, 1), lambda n: (0, cl(n), 0))
    dlspec = pl.BlockSpec((ts, 128), lambda n: (cl(n), 0))
    bspec = pl.BlockSpec((SWA_KV_HEADS, SWA_GROUP * BLOCK, 2 * BLOCK), lambda n: (0, 0, 0))
    kvout = pl.BlockSpec((SWA_KV_HEADS, ts, HEAD_DIM), lambda n: (0, jnp.maximum(n - 1, 0), 0))
    return pl.pallas_call(
        body,
        name="swa_bwd",
        grid=(nsteps + 1,),
        in_specs=[qspec, cur, prev, cur, prev, qspec, lsespec, dlspec, bspec, bspec,
                  pl.BlockSpec(memory_space=pltpu.SMEM)],
        out_specs=[qspec, kvout, kvout, bspec, pl.BlockSpec((SWA_HEADS, 128), lambda n: (0, 0))],
        out_shape=[_sds((SWA_HEADS, s_len, HEAD_DIM), bf16), _sds((SWA_KV_HEADS, s_len, HEAD_DIM), bf16),
                   _sds((SWA_KV_HEADS, s_len, HEAD_DIM), bf16),
                   _sds((SWA_KV_HEADS, SWA_GROUP * BLOCK, 2 * BLOCK), f32), _sds((SWA_HEADS, 128), f32)],
        scratch_shapes=[pltpu.VMEM((SWA_KV_HEADS, ts + BLOCK, HEAD_DIM), f32),
                        pltpu.VMEM((SWA_KV_HEADS, ts + BLOCK, HEAD_DIM), f32),
                        pltpu.VMEM((SWA_KV_HEADS, ts, HEAD_DIM), f32),
                        pltpu.VMEM((SWA_KV_HEADS, ts, HEAD_DIM), f32),
                        pltpu.VMEM((SWA_KV_HEADS, SWA_GROUP * BLOCK, 1), f32)],
        compiler_params=_params(("arbitrary",)),
    )(q, k, k, v, v, do, lse, dl, bias, bias0, sink)


def _dproj_specs(tm):
    def heads(nh):
        return pl.BlockSpec((nh, tm, HEAD_DIM), lambda i: (0, i, 0))

    half = pl.BlockSpec((tm, 512), lambda i: (i, 0))
    feat = pl.BlockSpec((512, tm), lambda i: (0, i))
    return [feat, feat, feat, half, heads(8), heads(2), heads(2), half, pl.BlockSpec((128, tm), lambda i: (0, i))]


def _cat_heads(ref, nheads):
    return jnp.concatenate([ref[h] for h in range(nheads)], axis=1)


def _dx_exchange_call(dh, pieces, w_t, bs, tm):
    s_len = dh.shape[0]
    n = len(bs)
    last = s_len // tm - 1

    def body(*refs):
        dh_ref, dqf_ref, dkf_ref, dvf_ref, dfz_ref, dqs_ref, dks_ref, dvs_ref, dsz_ref, dfft_ref, w_ref = refs[:11]
        b_refs = refs[11:11 + n]
        dx_ref = refs[11 + n]
        r_refs = refs[12 + n:12 + 2 * n]
        sems = refs[12 + 2 * n:]
        i = pl.program_id(0)

        @pl.when(i == 0)
        def _():
            _exchange_start(b_refs, r_refs, sems)

        def tr(ref):
            return ref[...].astype(f32).T.astype(bf16)

        dp = jnp.concatenate([tr(dqf_ref), tr(dkf_ref), tr(dvf_ref), dfz_ref[...], _cat_heads(dqs_ref, 8),
                              _cat_heads(dks_ref, 2), _cat_heads(dvs_ref, 2), dsz_ref[...],
                              dfft_ref[...].T.astype(bf16)], axis=1)
        dx_ref[...] = ALPHA * dh_ref[...] + jnp.dot(dp, w_ref[...], preferred_element_type=f32)

        @pl.when(i == last)
        def _():
            _exchange_wait(b_refs, r_refs, sems)

    fullw = pl.BlockSpec((tm, D_MODEL), lambda i: (i, 0))
    any_spec = pl.BlockSpec(memory_space=pl.ANY)
    out = pl.pallas_call(
        body,
        name="dx_bwd_exchange",
        grid=(s_len // tm,),
        in_specs=[fullw] + _dproj_specs(tm) + [pl.BlockSpec((A_W, D_MODEL), lambda i: (0, 0))] + [any_spec] * n,
        out_specs=[fullw] + [any_spec] * n,
        out_shape=[_sds((s_len, D_MODEL), f32)] + [_sds(b.shape, b.dtype) for b in bs],
        scratch_shapes=[pltpu.SemaphoreType.DMA((7 * n,)), pltpu.SemaphoreType.DMA((7 * n,)),
                        pltpu.SemaphoreType.DMA((n,))],
        compiler_params=_params(("arbitrary",)),
    )(dh, *pieces, w_t, *bs)
    return out[0], out[1:]


DW_STAGE_ROWS = 384


def _dw_call(x2, pieces, tm):
    s_len = x2.shape[0]
    nt = s_len // tm

    def body(x_ref, dqf_ref, dkf_ref, dvf_ref, dfz_ref, dqs_ref, dks_ref, dvs_ref, dsz_ref, dfft_ref, dw_ref,
             acc_ref, stage_ref, sem):
        i = pl.program_id(0)

        @pl.when(i == 0)
        def _():
            acc_ref[...] = jnp.zeros_like(acc_ref)

        xb = x_ref[...].astype(bf16)

        def add_feat(off, lhs):
            acc_ref[off:off + lhs.shape[0], :] += jnp.dot(lhs, xb, preferred_element_type=f32)

        def add_rows(off, piece):
            acc_ref[off:off + piece.shape[1], :] += lax.dot_general(piece, xb, TN, preferred_element_type=f32)

        add_feat(A_FQ, dqf_ref[...])
        add_feat(A_FK, dkf_ref[...])
        add_feat(A_FV, dvf_ref[...])
        add_rows(A_FZ, dfz_ref[...])
        add_rows(A_SQ, _cat_heads(dqs_ref, 8))
        add_rows(A_SK, _cat_heads(dks_ref, 2))
        add_rows(A_SV, _cat_heads(dvs_ref, 2))
        add_rows(A_SZ, dsz_ref[...])
        add_feat(A_FF, dfft_ref[...].astype(bf16))

        @pl.when(i == nt - 1)
        def _():
            for r in range(A_W // DW_STAGE_ROWS):
                rows = slice(r * DW_STAGE_ROWS, (r + 1) * DW_STAGE_ROWS)
                stage_ref[...] = acc_ref[rows, :].astype(bf16)
                cp = pltpu.make_async_copy(stage_ref, dw_ref.at[rows, :], sem)
                cp.start()
                cp.wait()

    return pl.pallas_call(
        body,
        name="dw_in_bwd",
        grid=(nt,),
        in_specs=[pl.BlockSpec((tm, D_MODEL), lambda i: (i, 0))] + _dproj_specs(tm),
        out_specs=pl.BlockSpec(memory_space=pl.ANY),
        out_shape=_sds((A_W, D_MODEL), bf16),
        scratch_shapes=[pltpu.VMEM((A_W, D_MODEL), f32), pltpu.VMEM((DW_STAGE_ROWS, D_MODEL), bf16),
                        pltpu.SemaphoreType.DMA],
        compiler_params=_params(("arbitrary",), VMEM_LIMIT_BIG),
    )(x2, *pieces)


def _adam_call(recv, w, m, v, tc, name):
    rows, cols = w.shape

    def body(r_ref, w_ref, m_ref, v_ref, g_ref, d_ref, mo_ref, vo_ref):
        g = r_ref[0].astype(f32)
        for p in range(1, N_DEV):
            g = g + r_ref[p].astype(f32)
        mn = ADAM_B1 * m_ref[...] + (1.0 - ADAM_B1) * g
        vn = ADAM_B2 * v_ref[...] + (1.0 - ADAM_B2) * (g * g)
        m_hat = mn / (1.0 - ADAM_B1 ** ADAM_STEP)
        v_hat = vn / (1.0 - ADAM_B2 ** ADAM_STEP)
        g_ref[...] = g
        d_ref[...] = -ADAM_LR * (m_hat / (jnp.sqrt(v_hat) + ADAM_EPS) + ADAM_WD * w_ref[...])
        mo_ref[...] = mn
        vo_ref[...] = vn

    blk = pl.BlockSpec((rows, tc), lambda i: (0, i))
    return pl.pallas_call(
        body,
        name=name,
        grid=(cols // tc,),
        in_specs=[pl.BlockSpec((N_DEV, rows, tc), lambda i: (0, 0, i)), blk, blk, blk],
        out_specs=[blk] * 4,
        out_shape=[_sds((rows, cols), f32)] * 4,
        compiler_params=_params(("arbitrary",)),
    )(recv, w, m, v)


def _pad_cols(a, width=128):
    return jnp.pad(a, ((0, 0), (0, width - a.shape[1])))


def _pack_small(ln_g, ln_b, rel, b_f, sink):
    return jnp.concatenate([
        ln_g.reshape(8, 128), ln_b.reshape(8, 128), _pad_cols(rel),
        jnp.pad(_pad_cols(b_f), ((0, 7), (0, 0))), jnp.pad(_pad_cols(sink), ((0, 7), (0, 0)))], axis=0)


def _unpack_small(p):
    return (p[0:8].reshape(1, D_MODEL), p[8:16].reshape(1, D_MODEL), p[16:48, 0:8], p[48:49, 0:8], p[56:57, 0:8])


def kernel(x, w_in, b_f, rel_bias, sink, w_o, ln_g, ln_b, loss_target, m_w_in, m_b_f, m_rel_bias, m_sink, m_w_o, m_ln_g, m_ln_b, v_w_in, v_b_f, v_rel_bias, v_sink, v_w_o, v_ln_g, v_ln_b):
    x2 = x[0]
    tgt = loss_target[0]
    s_len = x2.shape[0]
    shard = w_in.shape[2]

    w_in_t = jnp.transpose(w_in[0])
    g_in, g_o = _gather_call([w_in_t.astype(bf16), w_o[0].astype(bf16)])
    wt_full = g_in.reshape(N_DEV * shard, D_MODEL)
    w_t = jnp.concatenate([wt_full[:O_FF0], wt_full[O_FF1:], wt_full[O_FF0:O_FF1],
                           jnp.zeros((A_W - D_IN, D_MODEL), bf16)], axis=0)
    wo_full = g_o.reshape(D_MODEL, D_MODEL)

    qf, kf, vf, fz, qs, ks, vs, sz, fft, vat = _proj_call(x2, w_t, 512)
    cum, sgm = _cum_call(fft, b_f.reshape(FOX_HEADS, 1))
    qat, ka, kat, tile_stats = _augment_call(qf, kf, cum.reshape(FOX_HEADS, 1, s_len), 2048)
    npairs, pair_q, pair_k = _fox_prune_tables(tile_stats)
    o_ft, lse_f = _fox_fwd_call(qat, ka, vat, npairs, pair_q, pair_k)
    bucket = jnp.asarray(_t5_bucket_table())
    bias, bias0 = _swa_bias_call(rel_bias, bucket)
    sink_v = sink.reshape(SWA_HEADS)
    o_s, lse_s = _swa_fwd_call(qs, ks, vs, bias, bias0, sink_v)

    (dh, do_f, dfz, do_s, dsz, dl_f, dl_s, dwo, dg, db, loss_part) = _post_call(
        o_ft.reshape(FOX_HEADS * HEAD_DIM, s_len), fz, o_s, sz, x2, tgt, wo_full, ln_g, ln_b,
        jnp.asarray(_head_selector()), 256)

    dqf, dkf, dvf, dcq, dck = _fox_bwd_call(ka, kat, vf, qat, do_f.reshape(FOX_HEADS, HEAD_DIM, s_len), lse_f,
                                            dl_f.reshape(FOX_HEADS, 1, s_len), npairs, pair_q, pair_k)
    dqf, dkf, dvf = (a.reshape(FOX_HEADS * HEAD_DIM, s_len) for a in (dqf, dkf, dvf))
    dfft, dbf = _cum_bwd_call(dcq.reshape(FOX_HEADS, s_len), dck.reshape(FOX_HEADS, s_len), sgm)
    dqs, dks, dvs, dbias, dsink = _swa_bwd_call(qs, ks, vs, do_s, lse_s, dl_s, bias, bias0, sink_v)
    drel = _swa_bias_bwd_call(dbias, bucket)

    pieces = (dqf, dkf, dvf, dfz, dqs, dks, dvs, dsz, dfft)
    dw_t = _dw_call(x2, pieces, 1024)

    dwt_full = jnp.concatenate([dw_t[:O_FF0], dw_t[A_FF:A_FF + (O_FF1 - O_FF0)], dw_t[O_FF0:A_FF]], axis=0)
    dw_blocks = dwt_full.reshape(N_DEV, shard, D_MODEL)
    dwo_blocks = dwo.reshape(N_DEV, D_MODEL // N_DEV, D_MODEL).astype(bf16)
    small = _pack_small(dg, db, drel[:, 0:8], dbf[:, 0].reshape(1, 8), dsink[:, 0].reshape(1, 8))
    loss_slot = np.zeros((64, 128), bool)
    loss_slot[49, 0] = True
    small = jnp.where(jnp.asarray(loss_slot), loss_part[0, 0], small)
    small_blocks = jnp.broadcast_to(small[None], (N_DEV,) + small.shape)
    dx, (r_in, r_o, r_small) = _dx_exchange_call(dh, pieces, w_t, [dw_blocks, dwo_blocks, small_blocks], 256)

    win_t = [jnp.transpose(a) for a in _adam_call(
        r_in, w_in_t, jnp.transpose(m_w_in[0]), jnp.transpose(v_w_in[0]), 256, "adam_w_in")]
    g_win, d_win, nm_win, nv_win = win_t
    g_wo, d_wo, nm_wo, nv_wo = _adam_call(r_o, w_o[0], m_w_o[0], v_w_o[0], 256, "adam_w_o")
    p_w = _pack_small(ln_g, ln_b, rel_bias, b_f, sink)
    p_m = _pack_small(m_ln_g, m_ln_b, m_rel_bias, m_b_f, m_sink)
    p_v = _pack_small(v_ln_g, v_ln_b, v_rel_bias, v_b_f, v_sink)
    g_p, d_p, nm_p, nv_p = _adam_call(r_small, p_w, p_m, p_v, 128, "adam_small")

    loss = g_p[49, 0]
    g_lng, g_lnb, g_rel, g_bf, g_sink = _unpack_small(g_p)
    d_lng, d_lnb, d_rel, d_bf, d_sink = _unpack_small(d_p)
    m_lng, m_lnb, m_rel, m_bf, m_sk = _unpack_small(nm_p)
    v_lng, v_lnb, v_rel, v_bf, v_sk = _unpack_small(nv_p)
    return (loss, dx[None], g_win[None], g_bf, g_rel, g_sink, g_wo[None], g_lng, g_lnb,
            d_win[None], d_bf, d_rel, d_sink, d_wo[None], d_lng, d_lnb,
            nm_win[None], m_bf, m_rel, m_sk, nm_wo[None], m_lng, m_lnb,
            nv_win[None], v_bf, v_rel, v_sk, nv_wo[None], v_lng, v_lnb)
```

```python
import functools
import math

import numpy as np
import jax
import jax.numpy as jnp
from jax import lax
from jax.experimental import pallas as pl
from jax.experimental.pallas import tpu as pltpu

f32 = jnp.float32
bf16 = jnp.bfloat16

D_MODEL = 1024
HEAD_DIM = 64
FOX_HEADS = 8
SWA_HEADS = 8
SWA_KV_HEADS = 2
SWA_GROUP = 4
BLOCK = 128
NUM_BUCKETS = 32
MAX_DISTANCE = 128
LN_EPS = 1e-5
NEG_INF = -1e30
ALPHA = 2.0 ** 0.25
SCALE = 1.0 / math.sqrt(HEAD_DIM)
D_IN = 3336

ADAM_LR = 0.001
ADAM_B1 = 0.9
ADAM_B2 = 0.999
ADAM_EPS = 1e-08
ADAM_WD = 0.01
ADAM_STEP = 10

N_DEV = 8
A_FQ, A_FK, A_FV, A_FZ, A_SQ, A_SK, A_SV, A_SZ, A_FF, A_W = 0, 512, 1024, 1536, 2048, 2560, 2688, 2816, 3328, 3456
O_FF0, O_FF1 = 1536, 1544

VMEM_LIMIT = 48 * 1024 * 1024
HIGHEST = lax.Precision.HIGHEST
NT = (((1,), (1,)), ((), ()))
TN = (((0,), (0,)), ((), ()))
MESH = pl.DeviceIdType.MESH
RELS = [(0, 0, 1), (0, 1, 0), (0, 1, 1), (1, 0, 0), (1, 0, 1), (1, 1, 0), (1, 1, 1)]


VMEM_LIMIT_BIG = 60 * 1024 * 1024


def _params(sem=None, vmem=VMEM_LIMIT):
    return pltpu.CompilerParams(dimension_semantics=sem, vmem_limit_bytes=vmem)


def _sds(shape, dtype):
    return jax.ShapeDtypeStruct(shape, dtype)


def _t5_bucket_table():
    qi = np.arange(BLOCK)[:, None]
    kj = np.arange(2 * BLOCK)[None, :]
    rel = qi + BLOCK - kj
    band = (rel >= 0) & (rel < BLOCK)
    relc = np.maximum(rel, 0)
    max_exact = NUM_BUCKETS // 2
    relf = np.maximum(relc, 1).astype(np.float32)
    large = max_exact + (np.log(relf / np.float32(max_exact)) / np.float32(math.log(MAX_DISTANCE / max_exact))
                         * np.float32(NUM_BUCKETS - max_exact)).astype(np.int32)
    large = np.minimum(large, NUM_BUCKETS - 1)
    bucket = np.where(relc < max_exact, relc, large).astype(np.int32)
    bucket = np.where(band, bucket, -1).astype(np.int32)
    return bucket


def _mesh_pos():
    return lax.axis_index("x"), lax.axis_index("y"), lax.axis_index("c")


def _dev_index(p):
    return 4 * p[0] + 2 * p[1] + p[2]


def _gather_call(xs):
    n = len(xs)

    def body(*refs):
        x_refs, o_refs = refs[:n], refs[n:2 * n]
        send_sems, recv_sems, local_sems = refs[2 * n:]
        x, y, c = _mesh_pos()
        me, sib = (x, y, c), (x, y, 1 - c)
        chips = [(1 - x, y), (x, 1 - y), (1 - x, 1 - y)]

        def copy(a, k, block, to, src=None):
            slot = o_refs[a].at[_dev_index(block)]
            return pltpu.make_async_remote_copy(
                src_ref=slot if src is None else src, dst_ref=slot,
                send_sem=send_sems.at[a * 7 + k], recv_sem=recv_sems.at[a * 7 + k],
                device_id=to, device_id_type=MESH)

        mine = [pltpu.make_async_copy(x_refs[a], o_refs[a].at[_dev_index(me)], local_sems.at[a]) for a in range(n)]
        for cp in mine:
            cp.start()
        first = []
        for a in range(n):
            first.append(copy(a, 0, me, sib, src=x_refs[a]))
            first += [copy(a, 1 + j, me, (*chip, c), src=x_refs[a]) for j, chip in enumerate(chips)]
        for cp in first:
            cp.start()
        passed = []
        for j, chip in enumerate(chips):
            for a in range(n):
                copy(a, 1 + j, (*chip, c), me).wait_recv()
                fwd = copy(a, 4 + j, (*chip, c), sib)
                fwd.start()
                passed.append(fwd)
        for a in range(n):
            copy(a, 0, sib, me).wait_recv()
            for j, chip in enumerate(chips):
                copy(a, 4 + j, (*chip, 1 - c), me).wait_recv()
        for cp in first + passed:
            cp.wait_send()
        for cp in mine:
            cp.wait()

    any_spec = pl.BlockSpec(memory_space=pl.ANY)
    return pl.pallas_call(
        body,
        name="gather_weights",
        out_shape=[_sds((N_DEV,) + a.shape, a.dtype) for a in xs],
        in_specs=[any_spec] * n,
        out_specs=[any_spec] * n,
        scratch_shapes=[pltpu.SemaphoreType.DMA((7 * n,)), pltpu.SemaphoreType.DMA((7 * n,)),
                        pltpu.SemaphoreType.DMA((n,))],
    )(*xs)


def _exchange_copies(b_refs, r_refs, send_sems, recv_sems, local_sems, incoming):
    n = len(b_refs)
    x, y, c = _mesh_pos()
    me_idx = _dev_index((x, y, c))
    mine = [pltpu.make_async_copy(b_refs[a].at[me_idx], r_refs[a].at[me_idx], local_sems.at[a]) for a in range(n)]
    remote = []
    for k, r in enumerate(RELS):
        peer = ((1 - x) if r[0] else x, (1 - y) if r[1] else y, (1 - c) if r[2] else c)
        pidx = _dev_index(peer)
        for a in range(n):
            remote.append(pltpu.make_async_remote_copy(
                src_ref=b_refs[a].at[pidx], dst_ref=r_refs[a].at[pidx if incoming else me_idx],
                send_sem=send_sems.at[a * 7 + k], recv_sem=recv_sems.at[a * 7 + k],
                device_id=peer, device_id_type=MESH))
    return mine, remote


def _exchange_start(b_refs, r_refs, sems):
    mine, out = _exchange_copies(b_refs, r_refs, *sems, incoming=False)
    for cp in mine + out:
        cp.start()


def _exchange_wait(b_refs, r_refs, sems):
    mine, inc = _exchange_copies(b_refs, r_refs, *sems, incoming=True)
    for cp in inc:
        cp.wait_recv()
    for cp in inc:
        cp.wait_send()
    for cp in mine:
        cp.wait()


def _proj_call(x2, w_t, tm):
    s_len = x2.shape[0]

    def body(x_ref, w_ref, qf_ref, kf_ref, vf_ref, fz_ref, qs_ref, ks_ref, vs_ref, sz_ref, fft_ref, vat_ref):
        xb = x_ref[...].astype(bf16)
        vt = lax.dot_general(w_ref[A_FV:A_FV + 512, :], xb, NT, preferred_element_type=f32)
        ones_row = jnp.where(lax.broadcasted_iota(jnp.int32, (HEAD_DIM, tm), 0) == 0, 1.0, 0.0).astype(bf16)
        for h in range(FOX_HEADS):
            vat_ref[h, 0:HEAD_DIM, :] = vt[h * HEAD_DIM:(h + 1) * HEAD_DIM, :].astype(bf16)
            vat_ref[h, HEAD_DIM:2 * HEAD_DIM, :] = ones_row

        def seg(off, width):
            return lax.dot_general(xb, w_ref[off:off + width, :], NT, preferred_element_type=f32)

        def put_heads(ref, acc, nheads):
            for h in range(nheads):
                ref[h] = acc[:, h * HEAD_DIM:(h + 1) * HEAD_DIM].astype(bf16)

        put_heads(qf_ref, seg(A_FQ, 512) * SCALE, FOX_HEADS)
        put_heads(kf_ref, seg(A_FK, 512), FOX_HEADS)
        put_heads(vf_ref, seg(A_FV, 512), FOX_HEADS)
        fz_ref[...] = seg(A_FZ, 512)
        put_heads(qs_ref, seg(A_SQ, 512) * SCALE, SWA_HEADS)
        put_heads(ks_ref, seg(A_SK, 128), SWA_KV_HEADS)
        put_heads(vs_ref, seg(A_SV, 128), SWA_KV_HEADS)
        sz_ref[...] = seg(A_SZ, 512)
        fft_ref[...] = seg(A_FF, 128).T[:FOX_HEADS, :]

    def heads(nh):
        return pl.BlockSpec((nh, tm, HEAD_DIM), lambda i: (0, i, 0))

    wide = pl.BlockSpec((tm, 512), lambda i: (i, 0))
    return pl.pallas_call(
        body,
        name="proj_fwd",
        grid=(s_len // tm,),
        in_specs=[pl.BlockSpec((tm, D_MODEL), lambda i: (i, 0)), pl.BlockSpec((A_W, D_MODEL), lambda i: (0, 0))],
        out_specs=[heads(8), heads(8), heads(8), wide, heads(8), heads(2), heads(2), wide,
                   pl.BlockSpec((FOX_HEADS, tm), lambda i: (0, i)),
                   pl.BlockSpec((FOX_HEADS, 2 * HEAD_DIM, tm), lambda i: (0, 0, i))],
        out_shape=[_sds((8, s_len, HEAD_DIM), bf16)] * 3 + [_sds((s_len, 512), f32), _sds((8, s_len, HEAD_DIM), bf16),
                   _sds((2, s_len, HEAD_DIM), bf16), _sds((2, s_len, HEAD_DIM), bf16), _sds((s_len, 512), f32),
                   _sds((FOX_HEADS, s_len), f32), _sds((FOX_HEADS, 2 * HEAD_DIM, s_len), bf16)],
        compiler_params=_params(("arbitrary",)),
    )(x2, w_t)


AUG = 2 * HEAD_DIM


def _augment_call(q, k, cum_row, tm):
    nh, s_len, _ = q.shape
    per_step = tm // FOX_T

    def body(q_ref, k_ref, c_ref, qat_ref, ka_ref, kat_ref, st_ref):
        c = c_ref[0]
        hi = c.astype(bf16).astype(f32)
        r1 = c - hi
        mid = r1.astype(bf16).astype(f32)
        lo = (r1 - mid).astype(bf16).astype(f32)
        row = lax.broadcasted_iota(jnp.int32, (HEAD_DIM, tm), 0)
        q_tail = jnp.where(row == 0, hi, jnp.where(row == 1, mid, jnp.where(row == 2, lo,
                           jnp.where(row < 6, 1.0, 0.0))))
        k_tail = jnp.where(row < 3, 1.0, jnp.where(row == 3, -hi, jnp.where(row == 4, -mid,
                           jnp.where(row == 5, -lo, 0.0))))
        qt = q_ref[0].astype(f32).T
        kt = k_ref[0].astype(f32).T
        qat_ref[0, 0:HEAD_DIM, :] = qt.astype(bf16)
        qat_ref[0, HEAD_DIM:AUG, :] = q_tail.astype(bf16)
        ka_ref[0] = jnp.concatenate([k_ref[0], k_tail.T.astype(bf16)], axis=1)
        kat_ref[0, 0:HEAD_DIM, :] = kt.astype(bf16)
        kat_ref[0, HEAD_DIM:AUG, :] = k_tail.astype(bf16)
        qn2 =jnp.sum(qt * qt, axis=0, keepdims=True)
        kn2 = jnp.sum(kt * kt, axis=0, keepdims=True)
        sd = jnp.sum(qt * kt, axis=0, keepdims=True)
        srow = lax.broadcasted_iota(jnp.int32, (8, LANES), 0)
        for part in range(per_step):
            sl = slice(part * FOX_T, (part + 1) * FOX_T)
            vals = [jnp.sqrt(jnp.max(qn2[:, sl], axis=1, keepdims=True)),
                    jnp.sqrt(jnp.max(kn2[:, sl], axis=1, keepdims=True)),
                    jnp.min(sd[:, sl], axis=1, keepdims=True),
                    jnp.max(c[:, sl], axis=1, keepdims=True), jnp.min(c[:, sl], axis=1, keepdims=True)]
            out = jnp.zeros((8, LANES), f32)
            for r, val in enumerate(vals):
                out = jnp.where(srow == r, val, out)
            st_ref[0, part] = out

    tile = pl.BlockSpec((1, tm, HEAD_DIM), lambda h, i: (h, i, 0))
    return pl.pallas_call(
        body,
        name="fox_augment",
        grid=(nh, s_len // tm),
        in_specs=[tile, tile, pl.BlockSpec((1, 1, tm), lambda h, i: (h, 0, i))],
        out_specs=[pl.BlockSpec((1, AUG, tm), lambda h, i: (h, 0, i)),
                   pl.BlockSpec((1, tm, AUG), lambda h, i: (h, i, 0)),
                   pl.BlockSpec((1, AUG, tm), lambda h, i: (h, 0, i)),
                   pl.BlockSpec((1, per_step, 8, LANES), lambda h, i: (h, i, 0, 0))],
        out_shape=[_sds((nh, AUG, s_len), bf16), _sds((nh, s_len, AUG), bf16), _sds((nh, AUG, s_len), bf16),
                   _sds((nh, s_len // FOX_T, 8, LANES), f32)],
        compiler_params=_params(("arbitrary", "arbitrary")),
    )(q, k, cum_row)


EXP_ZERO_GAP = 110.0


def _fox_prune_tables(stats):
    s = stats[:, :, :, 0]
    qn, kn, sd, cmx, cmn = (s[:, :, r] for r in range(5))
    nt = s.shape[1]
    bound = qn[:, :, None] * kn[:, None, :] + (cmx[:, :, None] - cmn[:, None, :])
    margin = 2.0 + 1e-5 * (jnp.abs(cmx)[:, :, None] + jnp.abs(cmn)[:, None, :])
    qi = lax.broadcasted_iota(jnp.int32, (nt, nt), 0)
    kj = lax.broadcasted_iota(jnp.int32, (nt, nt), 1)
    skip = jnp.zeros_like(bound + margin < sd[:, :, None] - EXP_ZERO_GAP) & (kj < qi)[None]
    first = jnp.sum(jnp.cumprod(skip.astype(jnp.int32), axis=2), axis=2)
    tiles = lax.broadcasted_iota(jnp.int32, (1, nt), 1)
    cnt = tiles - first
    ends = jnp.cumsum(cnt, axis=1)
    off = ends - cnt
    kmax = nt * (nt - 1) // 2
    k = lax.broadcasted_iota(jnp.int32, (1, kmax), 1)
    pair_q = jnp.minimum(jnp.sum((ends[:, None, :] <= k[:, :, None]).astype(jnp.int32), axis=2), nt - 1)
    hit = pair_q[:, :, None] == tiles[:, None, :]
    first_k = jnp.sum(jnp.where(hit, first[:, None, :], 0), axis=2)
    off_k = jnp.sum(jnp.where(hit, off[:, None, :], 0), axis=2)
    pair_k = jnp.clip(first_k + k - off_k, 0, nt - 1)
    return (ends[:, nt - 1].astype(jnp.int32), pair_q.reshape(-1).astype(jnp.int32),
            pair_k.reshape(-1).astype(jnp.int32))


CUM_CHUNK = 512


def _cum_call(fft, bf_col):
    s_len = fft.shape[1]
    ch = CUM_CHUNK

    def body(f_ref, b_ref, cum_ref, sg_ref):
        r = lax.broadcasted_iota(jnp.int32, (ch, ch), 0)
        c = lax.broadcasted_iota(jnp.int32, (ch, ch), 1)
        upper = (r <= c).astype(f32)
        carry = jnp.zeros((FOX_HEADS, 1), f32)
        for n in range(s_len // ch):
            z = f_ref[:, n * ch:(n + 1) * ch] + b_ref[...]
            logf = jnp.minimum(z, 0.0) - jnp.log1p(jnp.exp(-jnp.abs(z)))
            sg_ref[:, n * ch:(n + 1) * ch] = 1.0 / (1.0 + jnp.exp(z))
            cs = jnp.dot(logf, upper, precision=HIGHEST, preferred_element_type=f32) + carry
            cum_ref[:, n * ch:(n + 1) * ch] = cs
            carry = cs[:, ch - 1:ch]

    return pl.pallas_call(
        body,
        name="fox_cum_fwd",
        out_shape=[_sds((FOX_HEADS, s_len), f32)] * 2,
        compiler_params=_params(),
    )(fft, bf_col)


def _cum_bwd_call(dcq, dck, sg):
    s_len = sg.shape[1]
    ch = CUM_CHUNK
    nch = s_len // ch

    def body(q_ref, k_ref, sg_ref, dff_ref, dbf_ref):
        r = lax.broadcasted_iota(jnp.int32, (ch, ch), 0)
        c = lax.broadcasted_iota(jnp.int32, (ch, ch), 1)
        lower = (r >= c).astype(f32)
        dff_ref[...] = jnp.zeros_like(dff_ref)
        carry = jnp.zeros((FOX_HEADS, 1), f32)
        total = jnp.zeros((FOX_HEADS, 1), f32)
        for n in reversed(range(nch)):
            sl = slice(n * ch, (n + 1) * ch)
            dcum = q_ref[:, sl] - k_ref[:, sl]
            rs = jnp.dot(dcum, lower, precision=HIGHEST, preferred_element_type=f32) + carry
            carry = rs[:, 0:1]
            dff = rs * sg_ref[:, sl]
            dff_ref[0:FOX_HEADS, sl] = dff
            total = total + jnp.sum(dff, axis=1, keepdims=True)
        dbf_ref[...] = jnp.broadcast_to(total, (FOX_HEADS, 128))

    return pl.pallas_call(
        body,
        name="fox_cum_bwd",
        out_shape=[_sds((128, s_len), f32), _sds((FOX_HEADS, 128), f32)],
        compiler_params=_params(),
    )(dcq, dck, sg)


FOX_T = 512
LANES = 128


def _causal_keep(t):
    return lax.broadcasted_iota(jnp.int32, (t, t), 0) <= lax.broadcasted_iota(jnp.int32, (t, t), 1)


def _tile_cols(i, t):
    return pl.ds(pl.multiple_of(i * t, t), t)


def _fox_pair(n, nt, kmax, h, pq_ref, pk_ref):
    k = h * kmax + jnp.maximum(n - nt, 0)
    return jnp.where(n < nt, n, pq_ref[k]), jnp.where(n < nt, n, pk_ref[k])


def _fox_fwd_call(qat, ka, vat, npairs, pair_q, pair_k):
    nh, s_len, _ = ka.shape
    t = FOX_T
    nt = s_len // t
    kmax = nt * (nt - 1) // 2
    assert nt >= 2 and nt % 2 == 0

    def body(np_ref, pq_ref, pk_ref, qat_ref, ka_ref, vat_ref, o_ref, lse_ref, s0, s1, p0, p1, a0, a1, m_all, acc_all):
        h = pl.program_id(0)
        extra = np_ref[h]
        total = nt + extra
        m_all[...] = jnp.full(m_all.shape, NEG_INF, f32)
        acc_all[...] = jnp.zeros(acc_all.shape, f32)
        bufs = ((s0, p0, a0), (s1, p1, a1))

        def pair(n):
            return _fox_pair(n, nt, kmax, h, pq_ref, pk_ref)

        def scores(n, b, masked):
            i, j = pair(n)
            st = jnp.dot(ka_ref[0, _tile_cols(j, t), :], qat_ref[0, :, _tile_cols(i, t)], preferred_element_type=f32)
            if masked:
                st = jnp.where(_causal_keep(t), st, NEG_INF)
            bufs[b][0][...] = st

        def softmax(n, b):
            i, _ = pair(n)
            s_ref, p_ref, a_ref = bufs[b]
            for c in range(t // LANES):
                cols = slice(c * LANES, (c + 1) * LANES)
                mcols = pl.ds(pl.multiple_of(i * t + c * LANES, LANES), LANES)
                m_old = m_all[:, mcols]
                m_new = jnp.maximum(m_old, jnp.max(s_ref[:, cols], axis=0, keepdims=True))
                m_all[:, mcols] = m_new
                a_ref[:, cols] = jnp.exp(m_old - m_new)
                p_ref[:, cols] = jnp.exp(s_ref[:, cols] - m_new).astype(bf16)

        def accum(n, b):
            i, j = pair(n)
            cols = _tile_cols(i, t)
            acc_all[:, cols] = bufs[b][2][...] * acc_all[:, cols] + jnp.dot(
                vat_ref[0, :, _tile_cols(j, t)], bufs[b][1][...], preferred_element_type=f32)

        def step(n, b, masked):
            scores(n, b, masked)
            softmax(n - 1, 1 - b)
            accum(n - 2, b)

        scores(0, 0, True)
        scores(1, 1, True)
        softmax(0, 0)

        def diag_steps(d, _):
            n = 2 + 2 * d
            step(n, 0, True)
            step(n + 1, 1, True)
            return 0

        lax.fori_loop(0, (nt - 2) // 2, diag_steps, 0)

        def off_steps(d, _):
            n = nt + 2 * d
            step(n, 0, False)
            step(n + 1, 1, False)
            return 0

        lax.fori_loop(0, extra // 2, off_steps, 0)

        @pl.when(extra % 2 == 1)
        def _():
            step(total - 1, 0, False)
            softmax(total - 1, 0)
            accum(total - 2, 1)
            accum(total - 1, 0)

        @pl.when(extra % 2 == 0)
        def _():
            softmax(total - 1, 1)
            accum(total - 2, 0)
            accum(total - 1, 1)

        l = acc_all[HEAD_DIM:HEAD_DIM + 1, :]
        o_ref[0] = acc_all[0:HEAD_DIM, :] / l
        lse_ref[0] = m_all[...] + jnp.log(l)

    smem = pl.BlockSpec(memory_space=pltpu.SMEM)
    return pl.pallas_call(
        body,
        name="fox_fwd",
        grid=(nh,),
        in_specs=[smem, smem, smem,
                  pl.BlockSpec((1, AUG, s_len), lambda h: (h, 0, 0)),
                  pl.BlockSpec((1, s_len, AUG), lambda h: (h, 0, 0)),
                  pl.BlockSpec((1, AUG, s_len), lambda h: (h, 0, 0))],
        out_specs=[pl.BlockSpec((1, HEAD_DIM, s_len), lambda h: (h, 0, 0)),
                   pl.BlockSpec((1, 1, s_len), lambda h: (h, 0, 0))],
        out_shape=[_sds((nh, HEAD_DIM, s_len), f32), _sds((nh, 1, s_len), f32)],
        scratch_shapes=[pltpu.VMEM((t, t), f32), pltpu.VMEM((t, t), f32), pltpu.VMEM((t, t), bf16),
                        pltpu.VMEM((t, t), bf16), pltpu.VMEM((1, t), f32), pltpu.VMEM((1, t), f32),
                        pltpu.VMEM((1, s_len), f32), pltpu.VMEM((AUG, s_len), f32)],
        compiler_params=_params(("arbitrary",)),
    )(npairs, pair_q, pair_k, qat, ka, vat)


SWA_TS = 512


def _swa_bias_call(rel_bias, bucket):
    def body(rb_ref, bk_ref, b_ref, b0_ref):
        bk = bk_ref[...]
        col = lax.broadcasted_iota(jnp.int32, (BLOCK, 2 * BLOCK), 1)
        for h in range(SWA_HEADS):
            acc = jnp.full((BLOCK, 2 * BLOCK), NEG_INF, f32)
            for b in range(NUM_BUCKETS):
                acc = jnp.where(bk == b, rb_ref[b, h], acc)
            g, hh = divmod(h, SWA_GROUP)
            b_ref[g, hh * BLOCK:(hh + 1) * BLOCK, :] = acc
            b0_ref[g, hh * BLOCK:(hh + 1) * BLOCK, :] = jnp.where(col < BLOCK, NEG_INF, acc)

    return pl.pallas_call(
        body,
        name="swa_bias",
        in_specs=[pl.BlockSpec(memory_space=pltpu.SMEM), pl.BlockSpec(memory_space=pltpu.VMEM)],
        out_shape=[_sds((SWA_KV_HEADS, SWA_GROUP * BLOCK, 2 * BLOCK), f32)] * 2,
        compiler_params=_params(),
    )(rel_bias, bucket)


def _swa_bias_bwd_call(dbias, bucket):
    def body(d_ref, bk_ref, o_ref):
        bk = bk_ref[...]
        row = lax.broadcasted_iota(jnp.int32, (NUM_BUCKETS, 128), 0)
        col = lax.broadcasted_iota(jnp.int32, (NUM_BUCKETS, 128), 1)
        out = jnp.zeros((NUM_BUCKETS, 128), f32)
        for h in range(SWA_HEADS):
            g, hh = divmod(h, SWA_GROUP)
            d = d_ref[g, hh * BLOCK:(hh + 1) * BLOCK, :]
            for b in range(NUM_BUCKETS):
                val = jnp.sum(jnp.sum(jnp.where(bk == b, d, 0.0), axis=1, keepdims=True), axis=0, keepdims=True)
                out = jnp.where((row == b) & (col == h), val, out)
        o_ref[...] = out

    return pl.pallas_call(
        body,
        name="swa_bias_bwd",
        out_shape=_sds((NUM_BUCKETS, 128), f32),
        compiler_params=_params(),
    )(dbias, bucket)


def _swa_specs(ts):
    nb = ts // BLOCK
    qspec = pl.BlockSpec((SWA_HEADS, ts, HEAD_DIM), lambda n: (0, n, 0))
    cur = pl.BlockSpec((SWA_KV_HEADS, ts, HEAD_DIM), lambda n: (0, n, 0))
    prev = pl.BlockSpec((SWA_KV_HEADS, BLOCK, HEAD_DIM), lambda n: (0, jnp.maximum(n * nb - 1, 0), 0))
    return qspec, cur, prev


def _sink_col(sink_ref, g):
    return jnp.concatenate([jnp.full((BLOCK, 1), sink_ref[g * SWA_GROUP + hh], f32) for hh in range(SWA_GROUP)], axis=0)


def _swa_fwd_call(q, k, v, bias, bias0, sink):
    s_len = q.shape[1]
    ts = SWA_TS
    nb = ts // BLOCK

    def body(q_ref, kc_ref, kp_ref, vc_ref, vp_ref, b_ref, b0_ref, sink_ref, o_ref, lse_ref):
        first = pl.program_id(0) == 0
        for g in range(SWA_KV_HEADS):
            kall = jnp.concatenate([kp_ref[g], kc_ref[g]], axis=0)
            vall = jnp.concatenate([vp_ref[g], vc_ref[g]], axis=0)
            sink_c = _sink_col(sink_ref, g)
            for b in range(nb):
                rows = slice(b * BLOCK, (b + 1) * BLOCK)
                qg = jnp.concatenate([q_ref[g * SWA_GROUP + hh, rows, :] for hh in range(SWA_GROUP)], axis=0)
                kcat = kall[b * BLOCK:(b + 2) * BLOCK]
                vcat = vall[b * BLOCK:(b + 2) * BLOCK]
                bias_b = b_ref[g]
                if b == 0:
                    bias_b = jnp.where(first, b0_ref[g], bias_b)
                s = lax.dot_general(qg, kcat, NT, preferred_element_type=f32) + bias_b
                m = jnp.maximum(jnp.max(s, axis=1, keepdims=True), sink_c)
                p = jnp.exp(s - m)
                l = jnp.sum(p, axis=1, keepdims=True) + jnp.exp(sink_c - m)
                o = jnp.dot(p.astype(bf16), vcat, preferred_element_type=f32) / l
                lse = m + jnp.log(l)
                for hh in range(SWA_GROUP):
                    o_ref[g * SWA_GROUP + hh, rows, :] = o[hh * BLOCK:(hh + 1) * BLOCK]
                    lse_ref[g * SWA_GROUP + hh, rows, :] = lse[hh * BLOCK:(hh + 1) * BLOCK]

    qspec, cur, prev = _swa_specs(ts)
    bspec = pl.BlockSpec((SWA_KV_HEADS, SWA_GROUP * BLOCK, 2 * BLOCK), lambda n: (0, 0, 0))
    return pl.pallas_call(
        body,
        name="swa_fwd",
        grid=(s_len // ts,),
        in_specs=[qspec, cur, prev, cur, prev, bspec, bspec, pl.BlockSpec(memory_space=pltpu.SMEM)],
        out_specs=[pl.BlockSpec((SWA_HEADS, ts, HEAD_DIM), lambda n: (0, n, 0)),
                   pl.BlockSpec((SWA_HEADS, ts, 1), lambda n: (0, n, 0))],
        out_shape=[_sds((SWA_HEADS, s_len, HEAD_DIM), f32), _sds((SWA_HEADS, s_len, 1), f32)],
        compiler_params=_params(("arbitrary",)),
    )(q, k, k, v, v, bias, bias0, sink)


def _head_selector():
    sel = np.zeros((512, 128), np.float32)
    for h in range(8):
        sel[h * HEAD_DIM:(h + 1) * HEAD_DIM, h] = 1.0
    return sel


def _post_call(of, fz, osw, sz, x2, tgt, wo, ln_g, ln_b, sel, tm):
    s_len = x2.shape[0]

    def body(of_ref, fz_ref, os_ref, sz_ref, x_ref, t_ref, wo_ref, g_ref, b_ref, sel_ref,
             dh_ref, dof_ref, dfz_ref, dos_ref, dsz_ref, dlf_ref, dls_ref, dwo_ref, dg_ref, db_ref, loss_ref):
        n = pl.program_id(0)

        @pl.when(n == 0)
        def _():
            dwo_ref[...] = jnp.zeros_like(dwo_ref)
            dg_ref[...] = jnp.zeros_like(dg_ref)
            db_ref[...] = jnp.zeros_like(db_ref)
            loss_ref[...] = jnp.zeros_like(loss_ref)

        o_f = of_ref[...].T
        o_s = jnp.concatenate([os_ref[h] for h in range(SWA_HEADS)], axis=1)
        fz = fz_ref[...]
        sz = sz_ref[...]
        sg_f = jax.nn.sigmoid(fz)
        sg_s = jax.nn.sigmoid(sz)
        silu_f = fz * sg_f
        silu_s = sz * sg_s
        mixed = jnp.concatenate([o_f * silu_f, o_s * silu_s], axis=1).astype(bf16)
        y = jnp.dot(mixed, wo_ref[...], preferred_element_type=f32)
        h = ALPHA * x_ref[...] + y
        mu = jnp.mean(h, axis=1, keepdims=True)
        hc = h - mu
        var = jnp.mean(hc * hc, axis=1, keepdims=True)
        rstd = lax.rsqrt(var + LN_EPS)
        xhat = hc * rstd
        gam = g_ref[...]
        out = xhat * gam + b_ref[...]
        err = out - t_ref[...]
        tok_loss = jnp.mean(err * err, axis=1, keepdims=True)
        loss_ref[...] += 0.5 * jnp.sum(tok_loss, axis=0, keepdims=True)
        dout = err * (1.0 / D_MODEL)
        dg_ref[...] += jnp.sum(dout * xhat, axis=0, keepdims=True)
        db_ref[...] += jnp.sum(dout, axis=0, keepdims=True)
        dxh = dout * gam
        m1 = jnp.mean(dxh, axis=1, keepdims=True)
        m2 = jnp.mean(dxh * xhat, axis=1, keepdims=True)
        dh = rstd * (dxh - m1 - xhat * m2)
        dh_ref[...] = dh
        dyb = dh.astype(bf16)
        dwo_ref[...] += lax.dot_general(mixed, dyb, TN, preferred_element_type=f32)
        dmix = lax.dot_general(dyb, wo_ref[...], NT, preferred_element_type=f32)
        dm_f = dmix[:, :512]
        dm_s = dmix[:, 512:]
        do_f = dm_f * silu_f
        do_s = dm_s * silu_s
        dfz_ref[...] = (dm_f * o_f * (sg_f * (1.0 + fz * (1.0 - sg_f)))).astype(bf16)
        dsz_ref[...] = (dm_s * o_s * (sg_s * (1.0 + sz * (1.0 - sg_s)))).astype(bf16)
        dof_ref[...] = do_f.T.astype(bf16)
        for hd in range(SWA_HEADS):
            dos_ref[hd] = do_s[:, hd * HEAD_DIM:(hd + 1) * HEAD_DIM].astype(bf16)
        sel_m = sel_ref[...]
        dl_f = jnp.dot(do_f * o_f, sel_m, precision=HIGHEST, preferred_element_type=f32)
        dl_s = jnp.dot(do_s * o_s, sel_m, precision=HIGHEST, preferred_element_type=f32)
        dlf_ref[...] = dl_f.T[:FOX_HEADS, :]
        dls_ref[...] = dl_s

    heads_f32 = pl.BlockSpec((8, tm, HEAD_DIM), lambda n: (0, n, 0))
    half = pl.BlockSpec((tm, 512), lambda n: (n, 0))
    fullw = pl.BlockSpec((tm, D_MODEL), lambda n: (n, 0))
    vec = pl.BlockSpec((1, D_MODEL), lambda n: (0, 0))
    return pl.pallas_call(
        body,
        name="post_fwd_bwd",
        grid=(s_len // tm,),
        in_specs=[pl.BlockSpec((512, tm), lambda n: (0, n)), half, heads_f32, half, fullw, fullw,
                  pl.BlockSpec((D_MODEL, D_MODEL), lambda n: (0, 0)), vec, vec,
                  pl.BlockSpec((512, 128), lambda n: (0, 0))],
        out_specs=[fullw, pl.BlockSpec((512, tm), lambda n: (0, n)), half, heads_f32, half,
                   pl.BlockSpec((FOX_HEADS, tm), lambda n: (0, n)), pl.BlockSpec((tm, 128), lambda n: (n, 0)),
                   pl.BlockSpec((D_MODEL, D_MODEL), lambda n: (0, 0)), vec, vec,
                   pl.BlockSpec((1, 1), lambda n: (0, 0))],
        out_shape=[_sds((s_len, D_MODEL), f32), _sds((512, s_len), bf16), _sds((s_len, 512), bf16),
                   _sds((8, s_len, HEAD_DIM), bf16), _sds((s_len, 512), bf16),
                   _sds((FOX_HEADS, s_len), f32), _sds((s_len, 128), f32),
                   _sds((D_MODEL, D_MODEL), f32), _sds((1, D_MODEL), f32), _sds((1, D_MODEL), f32),
                   _sds((1, 1), f32)],
        compiler_params=_params(("arbitrary",)),
    )(of, fz, osw, sz, x2, tgt, wo, ln_g, ln_b, sel)


def _fox_bwd_call(ka, kat, v, qat, dot, lse_row, dl_row, npairs, pair_q, pair_k):
    nh, s_len, _ = ka.shape
    t = FOX_T
    nt = s_len // t
    kmax = nt * (nt - 1) // 2
    assert nt >= 2 and nt % 2 == 0
    ck_slot = HEAD_DIM + 3
    cq_slot = HEAD_DIM

    def body(np_ref, pq_ref, pk_ref, ka_ref, kat_ref, v_ref, qat_ref, dot_ref, lse_ref, dl_ref,
             dq_ref, dk_ref, dv_ref, dcq_ref, dck_ref, dqt_all, dkat_all, dvt_all, p0, p1, ds0, ds1):
        h = pl.program_id(0)
        extra = np_ref[h]
        total = nt + extra
        dqt_all[...] = jnp.zeros(dqt_all.shape, f32)
        dkat_all[...] = jnp.zeros(dkat_all.shape, f32)
        dvt_all[...] = jnp.zeros(dvt_all.shape, f32)
        pbuf, dsbuf = (p0, p1), (ds0, ds1)

        def pair(n):
            return _fox_pair(n, nt, kmax, h, pq_ref, pk_ref)

        def probs(n, b, masked):
            i, j = pair(n)
            qc, kr = _tile_cols(i, t), _tile_cols(j, t)
            st = jnp.dot(ka_ref[0, kr, :], qat_ref[0, :, qc], preferred_element_type=f32)
            if masked:
                st = jnp.where(_causal_keep(t), st, NEG_INF)
            pt = jnp.exp(st - lse_ref[0, :, qc])
            dpt = jnp.dot(v_ref[0, kr, :], dot_ref[0, :, qc], preferred_element_type=f32)
            pbuf[b][...] = pt.astype(bf16)
            dsbuf[b][...] = (pt * (dpt - dl_ref[0, :, qc])).astype(bf16)

        def grads(n, b):
            i, j = pair(n)
            qc, kc = _tile_cols(i, t), _tile_cols(j, t)
            dvt_all[:, kc] += lax.dot_general(dot_ref[0, :, qc], pbuf[b][...], NT, preferred_element_type=f32)
            dkat_all[:, kc] += lax.dot_general(qat_ref[0, :, qc], dsbuf[b][...], NT, preferred_element_type=f32)
            dqt_all[:, qc] += jnp.dot(kat_ref[0, :, kc], dsbuf[b][...], preferred_element_type=f32)

        def step(n, b, masked):
            probs(n, b, masked)
            grads(n - 1, 1 - b)

        probs(0, 0, True)
        step(1, 1, True)

        def diag_steps(d, _):
            n = 2 + 2 * d
            step(n, 0, True)
            step(n + 1, 1, True)
            return 0

        lax.fori_loop(0, (nt - 2) // 2, diag_steps, 0)

        def off_steps(d, _):
            n = nt + 2 * d
            step(n, 0, False)
            step(n + 1, 1, False)
            return 0

        lax.fori_loop(0, extra // 2, off_steps, 0)

        @pl.when(extra % 2 == 1)
        def _():
            step(total - 1, 0, False)
            grads(total - 1, 0)

        @pl.when(extra % 2 == 0)
        def _():
            grads(total - 1, 1)

        dq_ref[0] = (dqt_all[0:HEAD_DIM, :] * SCALE).astype(bf16)
        dk_ref[0] = dkat_all[0:HEAD_DIM, :].astype(bf16)
        dv_ref[0] = dvt_all[...].astype(bf16)
        dcq_ref[0] = dqt_all[cq_slot:cq_slot + 1, :]
        dck_ref[0] = dkat_all[ck_slot:ck_slot + 1, :]

    smem = pl.BlockSpec(memory_space=pltpu.SMEM)
    rows = pl.BlockSpec((1, s_len, AUG), lambda h: (h, 0, 0))
    feat = pl.BlockSpec((1, AUG, s_len), lambda h: (h, 0, 0))
    feat64 = pl.BlockSpec((1, HEAD_DIM, s_len), lambda h: (h, 0, 0))
    rowv = pl.BlockSpec((1, 1, s_len), lambda h: (h, 0, 0))
    return pl.pallas_call(
        body,
        name="fox_bwd",
        grid=(nh,),
        in_specs=[smem, smem, smem, rows, feat, pl.BlockSpec((1, s_len, HEAD_DIM), lambda h: (h, 0, 0)), feat, feat64,
                  rowv, rowv],
        out_specs=[feat64, feat64, feat64, rowv, rowv],
        out_shape=[_sds((nh, HEAD_DIM, s_len), bf16)] * 3 + [_sds((nh, 1, s_len), f32)] * 2,
        scratch_shapes=[pltpu.VMEM((AUG, s_len), f32), pltpu.VMEM((AUG, s_len), f32), pltpu.VMEM((HEAD_DIM, s_len), f32)]
                       + [pltpu.VMEM((t, t), bf16)] * 4,
        compiler_params=_params(("arbitrary",)),
    )(npairs, pair_q, pair_k, ka, kat, v, qat, dot, lse_row, dl_row)


def _swa_bwd_call(q, k, v, do, lse, dl, bias, bias0, sink):
    s_len = q.shape[1]
    ts = SWA_TS
    nb = ts // BLOCK
    nsteps = s_len // ts

    def body(q_ref, kc_ref, kp_ref, vc_ref, vp_ref, do_ref, lse_ref, dl_ref, b_ref, b0_ref, sink_ref,
             dq_ref, dk_ref, dv_ref, dbias_ref, dsink_ref, dk_s, dv_s, tail_k, tail_v, sk_s):
        n = pl.program_id(0)

        @pl.when(n == 0)
        def _():
            dbias_ref[...] = jnp.zeros_like(dbias_ref)
            sk_s[...] = jnp.zeros_like(sk_s)

        @pl.when(n < nsteps)
        def _():
            first = n == 0
            dk_s[...] = jnp.zeros_like(dk_s)
            dv_s[...] = jnp.zeros_like(dv_s)
            for g in range(SWA_KV_HEADS):
                kall = jnp.concatenate([kp_ref[g], kc_ref[g]], axis=0)
                vall = jnp.concatenate([vp_ref[g], vc_ref[g]], axis=0)
                sink_c = _sink_col(sink_ref, g)
                for b in range(nb):
                    rows = slice(b * BLOCK, (b + 1) * BLOCK)
                    heads = [g * SWA_GROUP + hh for hh in range(SWA_GROUP)]
                    qg = jnp.concatenate([q_ref[h, rows, :] for h in heads], axis=0)
                    dog = jnp.concatenate([do_ref[h, rows, :] for h in heads], axis=0)
                    lse_c = jnp.concatenate([lse_ref[h, rows, :] for h in heads], axis=0)
                    dl_c = jnp.concatenate([dl_ref[rows, h:h + 1] for h in heads], axis=0)
                    kcat = kall[b * BLOCK:(b + 2) * BLOCK]
                    vcat = vall[b * BLOCK:(b + 2) * BLOCK]
                    bias_b = b_ref[g]
                    if b == 0:
                        bias_b = jnp.where(first, b0_ref[g], bias_b)
                    s = lax.dot_general(qg, kcat, NT, preferred_element_type=f32) + bias_b
                    p = jnp.exp(s - lse_c)
                    dp = lax.dot_general(dog, vcat, NT, preferred_element_type=f32)
                    ds = p * (dp - dl_c)
                    dsb = ds.astype(bf16)
                    dqg = jnp.dot(dsb, kcat, preferred_element_type=f32) * SCALE
                    for hh, h in enumerate(heads):
                        dq_ref[h, rows, :] = dqg[hh * BLOCK:(hh + 1) * BLOCK].astype(bf16)
                    win = slice(b * BLOCK, (b + 2) * BLOCK)
                    dk_s[g, win, :] += lax.dot_general(dsb, qg, TN, preferred_element_type=f32)
                    dv_s[g, win, :] += lax.dot_general(p.astype(bf16), dog, TN, preferred_element_type=f32)
                    dbias_ref[g] += ds
                    sk_s[g] += -jnp.exp(sink_c - lse_c) * dl_c

        @pl.when(n > 0)
        def _():
            last = slice(ts - BLOCK, ts)
            for g in range(SWA_KV_HEADS):
                add_k = jnp.where(n < nsteps, dk_s[g, 0:BLOCK, :], 0.0)
                add_v = jnp.where(n < nsteps, dv_s[g, 0:BLOCK, :], 0.0)
                dk_ref[g, 0:ts - BLOCK, :] = tail_k[g, 0:ts - BLOCK, :].astype(bf16)
                dv_ref[g, 0:ts - BLOCK, :] = tail_v[g, 0:ts - BLOCK, :].astype(bf16)
                dk_ref[g, last, :] = (tail_k[g, last, :] + add_k).astype(bf16)
                dv_ref[g, last, :] = (tail_v[g, last, :] + add_v).astype(bf16)

        @pl.when(n < nsteps)
        def _():
            tail_k[...] = dk_s[:, BLOCK:, :]
            tail_v[...] = dv_s[:, BLOCK:, :]

        @pl.when(n == nsteps)
        def _():
            row = lax.broadcasted_iota(jnp.int32, (SWA_HEADS, 128), 0)
            out = jnp.zeros((SWA_HEADS, 128), f32)
            for h in range(SWA_HEADS):
                g, hh = divmod(h, SWA_GROUP)
                val = jnp.sum(sk_s[g, hh * BLOCK:(hh + 1) * BLOCK, :], axis=0, keepdims=True)
                out = jnp.where(row == h, val, out)
            dsink_ref[...] = out

    last_step = nsteps - 1

    def cl(n):
        return jnp.minimum(n, last_step)

    qspec = pl.BlockSpec((SWA_HEADS, ts, HEAD_DIM), lambda n: (0, cl(n), 0))
    cur = pl.BlockSpec((SWA_KV_HEADS, ts, HEAD_DIM), lambda n: (0, cl(n), 0))
    prev = pl.BlockSpec((SWA_KV_HEADS, BLOCK, HEAD_DIM), lambda n: (0, jnp.maximum(cl(n) * nb - 1, 0), 0))
    lsespec = pl.BlockSpec((SWA_HEADS, ts, 1), lambda n: (0, cl(n), 0))
    dlspec = pl.BlockSpec((ts, 128), lambda n: (cl(n), 0))
    bspec = pl.BlockSpec((SWA_KV_HEADS, SWA_GROUP * BLOCK, 2 * BLOCK), lambda n: (0, 0, 0))
    kvout = pl.BlockSpec((SWA_KV_HEADS, ts, HEAD_DIM), lambda n: (0, jnp.maximum(n - 1, 0), 0))
    return pl.pallas_call(
        body,
        name="swa_bwd",
        grid=(nsteps + 1,),
        in_specs=[qspec, cur, prev, cur, prev, qspec, lsespec, dlspec, bspec, bspec,
                  pl.BlockSpec(memory_space=pltpu.SMEM)],
        out_specs=[qspec, kvout, kvout, bspec, pl.BlockSpec((SWA_HEADS, 128), lambda n: (0, 0))],
        out_shape=[_sds((SWA_HEADS, s_len, HEAD_DIM), bf16), _sds((SWA_KV_HEADS, s_len, HEAD_DIM), bf16),
                   _sds((SWA_KV_HEADS, s_len, HEAD_DIM), bf16),
                   _sds((SWA_KV_HEADS, SWA_GROUP * BLOCK, 2 * BLOCK), f32), _sds((SWA_HEADS, 128), f32)],
        scratch_shapes=[pltpu.VMEM((SWA_KV_HEADS, ts + BLOCK, HEAD_DIM), f32),
                        pltpu.VMEM((SWA_KV_HEADS, ts + BLOCK, HEAD_DIM), f32),
                        pltpu.VMEM((SWA_KV_HEADS, ts, HEAD_DIM), f32),
                        pltpu.VMEM((SWA_KV_HEADS, ts, HEAD_DIM), f32),
                        pltpu.VMEM((SWA_KV_HEADS, SWA_GROUP * BLOCK, 1), f32)],
        compiler_params=_params(("arbitrary",)),
    )(q, k, k, v, v, do, lse, dl, bias, bias0, sink)


def _dproj_specs(tm):
    def heads(nh):
        return pl.BlockSpec((nh, tm, HEAD_DIM), lambda i: (0, i, 0))

    half = pl.BlockSpec((tm, 512), lambda i: (i, 0))
    feat = pl.BlockSpec((512, tm), lambda i: (0, i))
    return [feat, feat, feat, half, heads(8), heads(2), heads(2), half, pl.BlockSpec((128, tm), lambda i: (0, i))]


def _cat_heads(ref, nheads):
    return jnp.concatenate([ref[h] for h in range(nheads)], axis=1)


def _dx_exchange_call(dh, pieces, w_t, bs, tm):
    s_len = dh.shape[0]
    n = len(bs)
    last = s_len // tm - 1

    def body(*refs):
        dh_ref, dqf_ref, dkf_ref, dvf_ref, dfz_ref, dqs_ref, dks_ref, dvs_ref, dsz_ref, dfft_ref, w_ref = refs[:11]
        b_refs = refs[11:11 + n]
        dx_ref = refs[11 + n]
        r_refs = refs[12 + n:12 + 2 * n]
        sems = refs[12 + 2 * n:]
        i = pl.program_id(0)

        @pl.when(i == 0)
        def _():
            _exchange_start(b_refs, r_refs, sems)

        def tr(ref):
            return ref[...].astype(f32).T.astype(bf16)

        dp = jnp.concatenate([tr(dqf_ref), tr(dkf_ref), tr(dvf_ref), dfz_ref[...], _cat_heads(dqs_ref, 8),
                              _cat_heads(dks_ref, 2), _cat_heads(dvs_ref, 2), dsz_ref[...],
                              dfft_ref[...].T.astype(bf16)], axis=1)
        dx_ref[...] = ALPHA * dh_ref[...] + jnp.dot(dp, w_ref[...], preferred_element_type=f32)

        @pl.when(i == last)
        def _():
            _exchange_wait(b_refs, r_refs, sems)

    fullw = pl.BlockSpec((tm, D_MODEL), lambda i: (i, 0))
    any_spec = pl.BlockSpec(memory_space=pl.ANY)
    out = pl.pallas_call(
        body,
        name="dx_bwd_exchange",
        grid=(s_len // tm,),
        in_specs=[fullw] + _dproj_specs(tm) + [pl.BlockSpec((A_W, D_MODEL), lambda i: (0, 0))] + [any_spec] * n,
        out_specs=[fullw] + [any_spec] * n,
        out_shape=[_sds((s_len, D_MODEL), f32)] + [_sds(b.shape, b.dtype) for b in bs],
        scratch_shapes=[pltpu.SemaphoreType.DMA((7 * n,)), pltpu.SemaphoreType.DMA((7 * n,)),
                        pltpu.SemaphoreType.DMA((n,))],
        compiler_params=_params(("arbitrary",)),
    )(dh, *pieces, w_t, *bs)
    return out[0], out[1:]


DW_STAGE_ROWS = 384


def _dw_call(x2, pieces, tm):
    s_len = x2.shape[0]
    nt = s_len // tm

    def body(x_ref, dqf_ref, dkf_ref, dvf_ref, dfz_ref, dqs_ref, dks_ref, dvs_ref, dsz_ref, dfft_ref, dw_ref,
             acc_ref, stage_ref, sem):
        i = pl.program_id(0)

        @pl.when(i == 0)
        def _():
            acc_ref[...] = jnp.zeros_like(acc_ref)

        xb = x_ref[...].astype(bf16)

        def add_feat(off, lhs):
            acc_ref[off:off + lhs.shape[0], :] += jnp.dot(lhs, xb, preferred_element_type=f32)

        def add_rows(off, piece):
            acc_ref[off:off + piece.shape[1], :] += lax.dot_general(piece, xb, TN, preferred_element_type=f32)

        add_feat(A_FQ, dqf_ref[...])
        add_feat(A_FK, dkf_ref[...])
        add_feat(A_FV, dvf_ref[...])
        add_rows(A_FZ, dfz_ref[...])
        add_rows(A_SQ, _cat_heads(dqs_ref, 8))
        add_rows(A_SK, _cat_heads(dks_ref, 2))
        add_rows(A_SV, _cat_heads(dvs_ref, 2))
        add_rows(A_SZ, dsz_ref[...])
        add_feat(A_FF, dfft_ref[...].astype(bf16))

        @pl.when(i == nt - 1)
        def _():
            for r in range(A_W // DW_STAGE_ROWS):
                rows = slice(r * DW_STAGE_ROWS, (r + 1) * DW_STAGE_ROWS)
                stage_ref[...] = acc_ref[rows, :].astype(bf16)
                cp = pltpu.make_async_copy(stage_ref, dw_ref.at[rows, :], sem)
                cp.start()
                cp.wait()

    return pl.pallas_call(
        body,
        name="dw_in_bwd",
        grid=(nt,),
        in_specs=[pl.BlockSpec((tm, D_MODEL), lambda i: (i, 0))] + _dproj_specs(tm),
        out_specs=pl.BlockSpec(memory_space=pl.ANY),
        out_shape=_sds((A_W, D_MODEL), bf16),
        scratch_shapes=[pltpu.VMEM((A_W, D_MODEL), f32), pltpu.VMEM((DW_STAGE_ROWS, D_MODEL), bf16),
                        pltpu.SemaphoreType.DMA],
        compiler_params=_params(("arbitrary",), VMEM_LIMIT_BIG),
    )(x2, *pieces)


def _adam_call(recv, w, m, v, tc, name):
    rows, cols = w.shape

    def body(r_ref, w_ref, m_ref, v_ref, g_ref, d_ref, mo_ref, vo_ref):
        g = r_ref[0].astype(f32)
        for p in range(1, N_DEV):
            g = g + r_ref[p].astype(f32)
        mn = ADAM_B1 * m_ref[...] + (1.0 - ADAM_B1) * g
        vn = ADAM_B2 * v_ref[...] + (1.0 - ADAM_B2) * (g * g)
        m_hat = mn / (1.0 - ADAM_B1 ** ADAM_STEP)
        v_hat = vn / (1.0 - ADAM_B2 ** ADAM_STEP)
        g_ref[...] = g
        d_ref[...] = -ADAM_LR * (m_hat / (jnp.sqrt(v_hat) + ADAM_EPS) + ADAM_WD * w_ref[...])
        mo_ref[...] = mn
        vo_ref[...] = vn

    blk = pl.BlockSpec((rows, tc), lambda i: (0, i))
    return pl.pallas_call(
        body,
        name=name,
        grid=(cols // tc,),
        in_specs=[pl.BlockSpec((N_DEV, rows, tc), lambda i: (0, 0, i)), blk, blk, blk],
        out_specs=[blk] * 4,
        out_shape=[_sds((rows, cols), f32)] * 4,
        compiler_params=_params(("arbitrary",)),
    )(recv, w, m, v)


def _pad_cols(a, width=128):
    return jnp.pad(a, ((0, 0), (0, width - a.shape[1])))


def _pack_small(ln_g, ln_b, rel, b_f, sink):
    return jnp.concatenate([
        ln_g.reshape(8, 128), ln_b.reshape(8, 128), _pad_cols(rel),
        jnp.pad(_pad_cols(b_f), ((0, 7), (0, 0))), jnp.pad(_pad_cols(sink), ((0, 7), (0, 0)))], axis=0)


def _unpack_small(p):
    return (p[0:8].reshape(1, D_MODEL), p[8:16].reshape(1, D_MODEL), p[16:48, 0:8], p[48:49, 0:8], p[56:57, 0:8])


def kernel(x, w_in, b_f, rel_bias, sink, w_o, ln_g, ln_b, loss_target, m_w_in, m_b_f, m_rel_bias, m_sink, m_w_o, m_ln_g, m_ln_b, v_w_in, v_b_f, v_rel_bias, v_sink, v_w_o, v_ln_g, v_ln_b):
    x2 = x[0]
    tgt = loss_target[0]
    s_len = x2.shape[0]
    shard = w_in.shape[2]

    w_in_t = jnp.transpose(w_in[0])
    g_in, g_o = _gather_call([w_in_t.astype(bf16), w_o[0].astype(bf16)])
    wt_full = g_in.reshape(N_DEV * shard, D_MODEL)
    w_t = jnp.concatenate([wt_full[:O_FF0], wt_full[O_FF1:], wt_full[O_FF0:O_FF1],
                           jnp.zeros((A_W - D_IN, D_MODEL), bf16)], axis=0)
    wo_full = g_o.reshape(D_MODEL, D_MODEL)

    qf, kf, vf, fz, qs, ks, vs, sz, fft, vat = _proj_call(x2, w_t, 512)
    cum, sgm = _cum_call(fft, b_f.reshape(FOX_HEADS, 1))
    qat, ka, kat, tile_stats = _augment_call(qf, kf, cum.reshape(FOX_HEADS, 1, s_len), 2048)
    npairs, pair_q, pair_k = _fox_prune_tables(tile_stats)
    o_ft, lse_f = _fox_fwd_call(qat, ka, vat, npairs, pair_q, pair_k)
    bucket = jnp.asarray(_t5_bucket_table())
    bias, bias0 = _swa_bias_call(rel_bias, bucket)
    sink_v = sink.reshape(SWA_HEADS)
    o_s, lse_s = _swa_fwd_call(qs, ks, vs, bias, bias0, sink_v)

    (dh, do_f, dfz, do_s, dsz, dl_f, dl_s, dwo, dg, db, loss_part) = _post_call(
        o_ft.reshape(FOX_HEADS * HEAD_DIM, s_len), fz, o_s, sz, x2, tgt, wo_full, ln_g, ln_b,
        jnp.asarray(_head_selector()), 256)

    dqf, dkf, dvf, dcq, dck = _fox_bwd_call(ka, kat, vf, qat, do_f.reshape(FOX_HEADS, HEAD_DIM, s_len), lse_f,
                                            dl_f.reshape(FOX_HEADS, 1, s_len), npairs, pair_q, pair_k)
    dqf, dkf, dvf = (a.reshape(FOX_HEADS * HEAD_DIM, s_len) for a in (dqf, dkf, dvf))
    dfft, dbf = _cum_bwd_call(dcq.reshape(FOX_HEADS, s_len), dck.reshape(FOX_HEADS, s_len), sgm)
    dqs, dks, dvs, dbias, dsink = _swa_bwd_call(qs, ks, vs, do_s, lse_s, dl_s, bias, bias0, sink_v)
    drel = _swa_bias_bwd_call(dbias, bucket)

    pieces = (dqf, dkf, dvf, dfz, dqs, dks, dvs, dsz, dfft)
    dw_t = _dw_call(x2, pieces, 1024)

    dwt_full = jnp.concatenate([dw_t[:O_FF0], dw_t[A_FF:A_FF + (O_FF1 - O_FF0)], dw_t[O_FF0:A_FF]], axis=0)
    dw_blocks = dwt_full.reshape(N_DEV, shard, D_MODEL)
    dwo_blocks = dwo.reshape(N_DEV, D_MODEL // N_DEV, D_MODEL).astype(bf16)
    small = _pack_small(dg, db, drel[:, 0:8], dbf[:, 0].reshape(1, 8), dsink[:, 0].reshape(1, 8))
    loss_slot = np.zeros((64, 128), bool)
    loss_slot[49, 0] = True
    small = jnp.where(jnp.asarray(loss_slot), loss_part[0, 0], small)
    small_blocks = jnp.broadcast_to(small[None], (N_DEV,) + small.shape)
    dx, (r_in, r_o, r_small) = _dx_exchange_call(dh, pieces, w_t, [dw_blocks, dwo_blocks, small_blocks], 256)

    win_t = [jnp.transpose(a) for a in _adam_call(
        r_in, w_in_t, jnp.transpose(m_w_in[0]), jnp.transpose(v_w_in[0]), 256, "adam_w_in")]
    g_win, d_win, nm_win, nv_win = win_t
    g_wo, d_wo, nm_wo, nv_wo = _adam_call(r_o, w_o[0], m_w_o[0], v_w_o[0], 256, "adam_w_o")
    p_w = _pack_small(ln_g, ln_b, rel_bias, b_f, sink)
    p_m = _pack_small(m_ln_g, m_ln_b, m_rel_bias, m_b_f, m_sink)
    p_v = _pack_small(v_ln_g, v_ln_b, v_rel_bias, v_b_f, v_sink)
    g_p, d_p, nm_p, nv_p = _adam_call(r_small, p_w, p_m, p_v, 128, "adam_small")

    loss = g_p[49, 0]
    g_lng, g_lnb, g_rel, g_bf, g_sink = _unpack_small(g_p)
    d_lng, d_lnb, d_rel, d_bf, d_sink = _unpack_small(d_p)
    m_lng, m_lnb, m_rel, m_bf, m_sk = _unpack_small(nm_p)
    v_lng, v_lnb, v_rel, v_bf, v_sk = _unpack_small(nv_p)
    return (loss, dx[None], g_win[None], g_bf, g_rel, g_sink, g_wo[None], g_lng, g_lnb,
            d_win[None], d_bf, d_rel, d_sink, d_wo[None], d_lng, d_lnb,
            nm_win[None], m_bf, m_rel, m_sk, nm_wo[None], m_lng, m_lnb,
            nv_win[None], v_bf, v_rel, v_sk, nv_wo[None], v_lng, v_lnb)
```

```python
import functools
import math

import numpy as np
import jax
import jax.numpy as jnp
from jax import lax
from jax.experimental import pallas as pl
from jax.experimental.pallas import tpu as pltpu

f32 = jnp.float32
bf16 = jnp.bfloat16

D_MODEL = 1024
HEAD_DIM = 64
FOX_HEADS = 8
SWA_HEADS = 8
SWA_KV_HEADS = 2
SWA_GROUP = 4
BLOCK = 128
NUM_BUCKETS = 32
MAX_DISTANCE = 128
LN_EPS = 1e-5
NEG_INF = -1e30
ALPHA = 2.0 ** 0.25
SCALE = 1.0 / math.sqrt(HEAD_DIM)
D_IN = 3336

ADAM_LR = 0.001
ADAM_B1 = 0.9
ADAM_B2 = 0.999
ADAM_EPS = 1e-08
ADAM_WD = 0.01
ADAM_STEP = 10

N_DEV = 8
A_FQ, A_FK, A_FV, A_FZ, A_SQ, A_SK, A_SV, A_SZ, A_FF, A_W = 0, 512, 1024, 1536, 2048, 2560, 2688, 2816, 3328, 3456
O_FF0, O_FF1 = 1536, 1544

VMEM_LIMIT = 48 * 1024 * 1024
HIGHEST = lax.Precision.HIGHEST
NT = (((1,), (1,)), ((), ()))
TN = (((0,), (0,)), ((), ()))
MESH = pl.DeviceIdType.MESH
RELS = [(0, 0, 1), (0, 1, 0), (0, 1, 1), (1, 0, 0), (1, 0, 1), (1, 1, 0), (1, 1, 1)]


VMEM_LIMIT_BIG = 60 * 1024 * 1024


def _params(sem=None, vmem=VMEM_LIMIT):
    return pltpu.CompilerParams(dimension_semantics=sem, vmem_limit_bytes=vmem)


def _sds(shape, dtype):
    return jax.ShapeDtypeStruct(shape, dtype)


def _t5_bucket_table():
    qi = np.arange(BLOCK)[:, None]
    kj = np.arange(2 * BLOCK)[None, :]
    rel = qi + BLOCK - kj
    band = (rel >= 0) & (rel < BLOCK)
    relc = np.maximum(rel, 0)
    max_exact = NUM_BUCKETS // 2
    relf = np.maximum(relc, 1).astype(np.float32)
    large = max_exact + (np.log(relf / np.float32(max_exact)) / np.float32(math.log(MAX_DISTANCE / max_exact))
                         * np.float32(NUM_BUCKETS - max_exact)).astype(np.int32)
    large = np.minimum(large, NUM_BUCKETS - 1)
    bucket = np.where(relc < max_exact, relc, large).astype(np.int32)
    bucket = np.where(band, bucket, -1).astype(np.int32)
    return bucket


def _mesh_pos():
    return lax.axis_index("x"), lax.axis_index("y"), lax.axis_index("c")


def _dev_index(p):
    return 4 * p[0] + 2 * p[1] + p[2]


def _gather_call(xs):
    n = len(xs)

    def body(*refs):
        x_refs, o_refs = refs[:n], refs[n:2 * n]
        send_sems, recv_sems, local_sems = refs[2 * n:]
        x, y, c = _mesh_pos()
        me, sib = (x, y, c), (x, y, 1 - c)
        chips = [(1 - x, y), (x, 1 - y), (1 - x, 1 - y)]

        def copy(a, k, block, to, src=None):
            slot = o_refs[a].at[_dev_index(block)]
            return pltpu.make_async_remote_copy(
                src_ref=slot if src is None else src, dst_ref=slot,
                send_sem=send_sems.at[a * 7 + k], recv_sem=recv_sems.at[a * 7 + k],
                device_id=to, device_id_type=MESH)

        mine = [pltpu.make_async_copy(x_refs[a], o_refs[a].at[_dev_index(me)], local_sems.at[a]) for a in range(n)]
        for cp in mine:
            cp.start()
        first = []
        for a in range(n):
            first.append(copy(a, 0, me, sib, src=x_refs[a]))
            first += [copy(a, 1 + j, me, (*chip, c), src=x_refs[a]) for j, chip in enumerate(chips)]
        for cp in first:
            cp.start()
        passed = []
        for j, chip in enumerate(chips):
            for a in range(n):
                copy(a, 1 + j, (*chip, c), me).wait_recv()
                fwd = copy(a, 4 + j, (*chip, c), sib)
                fwd.start()
                passed.append(fwd)
        for a in range(n):
            copy(a, 0, sib, me).wait_recv()
            for j, chip in enumerate(chips):
                copy(a, 4 + j, (*chip, 1 - c), me).wait_recv()
        for cp in first + passed:
            cp.wait_send()
        for cp in mine:
            cp.wait()

    any_spec = pl.BlockSpec(memory_space=pl.ANY)
    return pl.pallas_call(
        body,
        name="gather_weights",
        out_shape=[_sds((N_DEV,) + a.shape, a.dtype) for a in xs],
        in_specs=[any_spec] * n,
        out_specs=[any_spec] * n,
        scratch_shapes=[pltpu.SemaphoreType.DMA((7 * n,)), pltpu.SemaphoreType.DMA((7 * n,)),
                        pltpu.SemaphoreType.DMA((n,))],
    )(*xs)


def _exchange_copies(b_refs, r_refs, send_sems, recv_sems, local_sems, incoming):
    n = len(b_refs)
    x, y, c = _mesh_pos()
    me_idx = _dev_index((x, y, c))
    mine = [pltpu.make_async_copy(b_refs[a].at[me_idx], r_refs[a].at[me_idx], local_sems.at[a]) for a in range(n)]
    remote = []
    for k, r in enumerate(RELS):
        peer = ((1 - x) if r[0] else x, (1 - y) if r[1] else y, (1 - c) if r[2] else c)
        pidx = _dev_index(peer)
        for a in range(n):
            remote.append(pltpu.make_async_remote_copy(
                src_ref=b_refs[a].at[pidx], dst_ref=r_refs[a].at[pidx if incoming else me_idx],
                send_sem=send_sems.at[a * 7 + k], recv_sem=recv_sems.at[a * 7 + k],
                device_id=peer, device_id_type=MESH))
    return mine, remote


def _exchange_start(b_refs, r_refs, sems):
    mine, out = _exchange_copies(b_refs, r_refs, *sems, incoming=False)
    for cp in mine + out:
        cp.start()


def _exchange_wait(b_refs, r_refs, sems):
    mine, inc = _exchange_copies(b_refs, r_refs, *sems, incoming=True)
    for cp in inc:
        cp.wait_recv()
    for cp in inc:
        cp.wait_send()
    for cp in mine:
        cp.wait()


def _proj_call(x2, w_t, tm):
    s_len = x2.shape[0]

    def body(x_ref, w_ref, qf_ref, kf_ref, vf_ref, fz_ref, qst_ref, ks_ref, vs_ref, sz_ref, fft_ref, vat_ref,
             kst_ref, vsta_ref):
        xb = x_ref[...].astype(bf16)

        def seg_t(off, width):
            return lax.dot_general(w_ref[off:off + width, :], xb, NT, preferred_element_type=f32)

        def seg(off, width):
            return lax.dot_general(xb, w_ref[off:off + width, :], NT, preferred_element_type=f32)

        def put_heads(ref, acc, nheads):
            for h in range(nheads):
                ref[h] = acc[:, h * HEAD_DIM:(h + 1) * HEAD_DIM].astype(bf16)

        def put_heads_t(ref, acc_t, nheads, augment):
            for h in range(nheads):
                ref[h, 0:HEAD_DIM, :] = acc_t[h * HEAD_DIM:(h + 1) * HEAD_DIM, :].astype(bf16)
                if augment:
                    ref[h, HEAD_DIM:2 * HEAD_DIM, :] = ones_row

        ones_row = jnp.where(lax.broadcasted_iota(jnp.int32, (HEAD_DIM, tm), 0) == 0, 1.0, 0.0).astype(bf16)
        put_heads_t(vat_ref, seg_t(A_FV, 512), FOX_HEADS, True)
        put_heads(qf_ref, seg(A_FQ, 512) * SCALE, FOX_HEADS)
        put_heads(kf_ref, seg(A_FK, 512), FOX_HEADS)
        put_heads(vf_ref, seg(A_FV, 512), FOX_HEADS)
        fz_ref[...] = seg(A_FZ, 512)
        put_heads_t(qst_ref, seg_t(A_SQ, 512) * SCALE, SWA_HEADS, False)
        put_heads(ks_ref, seg(A_SK, 128), SWA_KV_HEADS)
        put_heads(vs_ref, seg(A_SV, 128), SWA_KV_HEADS)
        put_heads_t(kst_ref, seg_t(A_SK, 128), SWA_KV_HEADS, False)
        put_heads_t(vsta_ref, seg_t(A_SV, 128), SWA_KV_HEADS, True)
        sz_ref[...] = seg(A_SZ, 512)
        fft_ref[...] = seg(A_FF, 128).T[:FOX_HEADS, :]

    def heads(nh):
        return pl.BlockSpec((nh, tm, HEAD_DIM), lambda i: (0, i, 0))

    def feat(nh, rows):
        return pl.BlockSpec((nh, rows, tm), lambda i: (0, 0, i))

    wide = pl.BlockSpec((tm, 512), lambda i: (i, 0))
    return pl.pallas_call(
        body,
        name="proj_fwd",
        grid=(s_len // tm,),
        in_specs=[pl.BlockSpec((tm, D_MODEL), lambda i: (i, 0)), pl.BlockSpec((A_W, D_MODEL), lambda i: (0, 0))],
        out_specs=[heads(8), heads(8), heads(8), wide, feat(8, HEAD_DIM), heads(2), heads(2), wide,
                   pl.BlockSpec((FOX_HEADS, tm), lambda i: (0, i)),
                   feat(FOX_HEADS, 2 * HEAD_DIM), feat(2, HEAD_DIM), feat(2, 2 * HEAD_DIM)],
        out_shape=[_sds((8, s_len, HEAD_DIM), bf16)] * 3 + [_sds((s_len, 512), f32), _sds((8, HEAD_DIM, s_len), bf16),
                   _sds((2, s_len, HEAD_DIM), bf16), _sds((2, s_len, HEAD_DIM), bf16), _sds((s_len, 512), f32),
                   _sds((FOX_HEADS, s_len), f32), _sds((FOX_HEADS, 2 * HEAD_DIM, s_len), bf16),
                   _sds((2, HEAD_DIM, s_len), bf16), _sds((2, 2 * HEAD_DIM, s_len), bf16)],
        compiler_params=_params(("arbitrary",)),
    )(x2, w_t)


AUG = 2 * HEAD_DIM


def _augment_call(q, k, cum_row, tm):
    nh, s_len, _ = q.shape
    per_step = tm // FOX_T

    def body(q_ref, k_ref, c_ref, qat_ref, ka_ref, kat_ref, st_ref):
        c = c_ref[0]
        hi = c.astype(bf16).astype(f32)
        r1 = c - hi
        mid = r1.astype(bf16).astype(f32)
        lo = (r1 - mid).astype(bf16).astype(f32)
        row = lax.broadcasted_iota(jnp.int32, (HEAD_DIM, tm), 0)
        q_tail = jnp.where(row == 0, hi, jnp.where(row == 1, mid, jnp.where(row == 2, lo,
                           jnp.where(row < 6, 1.0, 0.0))))
        k_tail = jnp.where(row < 3, 1.0, jnp.where(row == 3, -hi, jnp.where(row == 4, -mid,
                           jnp.where(row == 5, -lo, 0.0))))
        qt = q_ref[0].astype(f32).T
        kt = k_ref[0].astype(f32).T
        qat_ref[0, 0:HEAD_DIM, :] = qt.astype(bf16)
        qat_ref[0, HEAD_DIM:AUG, :] = q_tail.astype(bf16)
        ka_ref[0] = jnp.concatenate([k_ref[0], k_tail.T.astype(bf16)], axis=1)
        kat_ref[0, 0:HEAD_DIM, :] = kt.astype(bf16)
        kat_ref[0, HEAD_DIM:AUG, :] = k_tail.astype(bf16)
        qn2 =jnp.sum(qt * qt, axis=0, keepdims=True)
        kn2 = jnp.sum(kt * kt, axis=0, keepdims=True)
        sd = jnp.sum(qt * kt, axis=0, keepdims=True)
        srow = lax.broadcasted_iota(jnp.int32, (8, LANES), 0)
        for part in range(per_step):
            sl = slice(part * FOX_T, (part + 1) * FOX_T)
            vals = [jnp.sqrt(jnp.max(qn2[:, sl], axis=1, keepdims=True)),
                    jnp.sqrt(jnp.max(kn2[:, sl], axis=1, keepdims=True)),
                    jnp.min(sd[:, sl], axis=1, keepdims=True),
                    jnp.max(c[:, sl], axis=1, keepdims=True), jnp.min(c[:, sl], axis=1, keepdims=True)]
            out = jnp.zeros((8, LANES), f32)
            for r, val in enumerate(vals):
                out = jnp.where(srow == r, val, out)
            st_ref[0, part] = out

    tile = pl.BlockSpec((1, tm, HEAD_DIM), lambda h, i: (h, i, 0))
    return pl.pallas_call(
        body,
        name="fox_augment",
        grid=(nh, s_len // tm),
        in_specs=[tile, tile, pl.BlockSpec((1, 1, tm), lambda h, i: (h, 0, i))],
        out_specs=[pl.BlockSpec((1, AUG, tm), lambda h, i: (h, 0, i)),
                   pl.BlockSpec((1, tm, AUG), lambda h, i: (h, i, 0)),
                   pl.BlockSpec((1, AUG, tm), lambda h, i: (h, 0, i)),
                   pl.BlockSpec((1, per_step, 8, LANES), lambda h, i: (h, i, 0, 0))],
        out_shape=[_sds((nh, AUG, s_len), bf16), _sds((nh, s_len, AUG), bf16), _sds((nh, AUG, s_len), bf16),
                   _sds((nh, s_len // FOX_T, 8, LANES), f32)],
        compiler_params=_params(("arbitrary", "arbitrary")),
    )(q, k, cum_row)


EXP_ZERO_GAP = 110.0


def _fox_prune_tables(stats):
    s = stats[:, :, :, 0]
    qn, kn, sd, cmx, cmn = (s[:, :, r] for r in range(5))
    nt = s.shape[1]
    bound = qn[:, :, None] * kn[:, None, :] + (cmx[:, :, None] - cmn[:, None, :])
    margin = 2.0 + 1e-5 * (jnp.abs(cmx)[:, :, None] + jnp.abs(cmn)[:, None, :])
    qi = lax.broadcasted_iota(jnp.int32, (nt, nt), 0)
    kj = lax.broadcasted_iota(jnp.int32, (nt, nt), 1)
    skip = (bound + margin < sd[:, :, None] - EXP_ZERO_GAP) & (kj < qi)[None]
    first = jnp.sum(jnp.cumprod(skip.astype(jnp.int32), axis=2), axis=2)
    tiles = lax.broadcasted_iota(jnp.int32, (1, nt), 1)
    cnt = tiles - first
    ends = jnp.cumsum(cnt, axis=1)
    off = ends - cnt
    kmax = nt * (nt - 1) // 2
    k = lax.broadcasted_iota(jnp.int32, (1, kmax), 1)
    pair_q = jnp.minimum(jnp.sum((ends[:, None, :] <= k[:, :, None]).astype(jnp.int32), axis=2), nt - 1)
    hit = pair_q[:, :, None] == tiles[:, None, :]
    first_k = jnp.sum(jnp.where(hit, first[:, None, :], 0), axis=2)
    off_k = jnp.sum(jnp.where(hit, off[:, None, :], 0), axis=2)
    pair_k = jnp.clip(first_k + k - off_k, 0, nt - 1)
    return (ends[:, nt - 1].astype(jnp.int32), pair_q.reshape(-1).astype(jnp.int32),
            pair_k.reshape(-1).astype(jnp.int32))


CUM_CHUNK = 512


def _cum_call(fft, bf_col):
    s_len = fft.shape[1]
    ch = CUM_CHUNK

    def body(f_ref, b_ref, cum_ref, sg_ref):
        r = lax.broadcasted_iota(jnp.int32, (ch, ch), 0)
        c = lax.broadcasted_iota(jnp.int32, (ch, ch), 1)
        upper = (r <= c).astype(f32)
        carry = jnp.zeros((FOX_HEADS, 1), f32)
        for n in range(s_len // ch):
            z = f_ref[:, n * ch:(n + 1) * ch] + b_ref[...]
            logf = jnp.minimum(z, 0.0) - jnp.log1p(jnp.exp(-jnp.abs(z)))
            sg_ref[:, n * ch:(n + 1) * ch] = 1.0 / (1.0 + jnp.exp(z))
            cs = jnp.dot(logf, upper, precision=HIGHEST, preferred_element_type=f32) + carry
            cum_ref[:, n * ch:(n + 1) * ch] = cs
            carry = cs[:, ch - 1:ch]

    return pl.pallas_call(
        body,
        name="fox_cum_fwd",
        out_shape=[_sds((FOX_HEADS, s_len), f32)] * 2,
        compiler_params=_params(),
    )(fft, bf_col)


def _cum_bwd_call(dcq, dck, sg):
    s_len = sg.shape[1]
    ch = CUM_CHUNK
    nch = s_len // ch

    def body(q_ref, k_ref, sg_ref, dff_ref, dbf_ref):
        r = lax.broadcasted_iota(jnp.int32, (ch, ch), 0)
        c = lax.broadcasted_iota(jnp.int32, (ch, ch), 1)
        lower = (r >= c).astype(f32)
        dff_ref[...] = jnp.zeros_like(dff_ref)
        carry = jnp.zeros((FOX_HEADS, 1), f32)
        total = jnp.zeros((FOX_HEADS, 1), f32)
        for n in reversed(range(nch)):
            sl = slice(n * ch, (n + 1) * ch)
            dcum = q_ref[:, sl] - k_ref[:, sl]
            rs = jnp.dot(dcum, lower, precision=HIGHEST, preferred_element_type=f32) + carry
            carry = rs[:, 0:1]
            dff = rs * sg_ref[:, sl]
            dff_ref[0:FOX_HEADS, sl] = dff
            total = total + jnp.sum(dff, axis=1, keepdims=True)
        dbf_ref[...] = jnp.broadcast_to(total, (FOX_HEADS, 128))

    return pl.pallas_call(
        body,
        name="fox_cum_bwd",
        out_shape=[_sds((128, s_len), f32), _sds((FOX_HEADS, 128), f32)],
        compiler_params=_params(),
    )(dcq, dck, sg)


FOX_T = 512
LANES = 128


def _causal_keep(t):
    return lax.broadcasted_iota(jnp.int32, (t, t), 0) <= lax.broadcasted_iota(jnp.int32, (t, t), 1)


def _tile_cols(i, t):
    return pl.ds(pl.multiple_of(i * t, t), t)


def _fox_pair(n, nt, kmax, h, pq_ref, pk_ref):
    k = h * kmax + jnp.maximum(n - nt, 0)
    return jnp.where(n < nt, n, pq_ref[k]), jnp.where(n < nt, n, pk_ref[k])


def _fox_fwd_call(qat, ka, vat, npairs, pair_q, pair_k):
    nh, s_len, _ = ka.shape
    t = FOX_T
    nt = s_len // t
    kmax = nt * (nt - 1) // 2
    assert nt >= 2 and nt % 2 == 0

    def body(np_ref, pq_ref, pk_ref, qat_ref, ka_ref, vat_ref, o_ref, lse_ref, s0, s1, p0, p1, a0, a1, m_all, acc_all):
        h = pl.program_id(0)
        extra = np_ref[h]
        total = nt + extra
        m_all[...] = jnp.full(m_all.shape, NEG_INF, f32)
        acc_all[...] = jnp.zeros(acc_all.shape, f32)
        bufs = ((s0, p0, a0), (s1, p1, a1))

        def pair(n):
            return _fox_pair(n, nt, kmax, h, pq_ref, pk_ref)

        def scores(n, b, masked):
            i, j = pair(n)
            st = jnp.dot(ka_ref[0, _tile_cols(j, t), :], qat_ref[0, :, _tile_cols(i, t)], preferred_element_type=f32)
            if masked:
                st = jnp.where(_causal_keep(t), st, NEG_INF)
            bufs[b][0][...] = st

        def softmax(n, b):
            i, _ = pair(n)
            s_ref, p_ref, a_ref = bufs[b]
            for c in range(t // LANES):
                cols = slice(c * LANES, (c + 1) * LANES)
                mcols = pl.ds(pl.multiple_of(i * t + c * LANES, LANES), LANES)
                m_old = m_all[:, mcols]
                m_new = jnp.maximum(m_old, jnp.max(s_ref[:, cols], axis=0, keepdims=True))
                m_all[:, mcols] = m_new
                a_ref[:, cols] = jnp.exp(m_old - m_new)
                p_ref[:, cols] = jnp.exp(s_ref[:, cols] - m_new).astype(bf16)

        def accum(n, b):
            i, j = pair(n)
            cols = _tile_cols(i, t)
            acc_all[:, cols] = bufs[b][2][...] * acc_all[:, cols] + jnp.dot(
                vat_ref[0, :, _tile_cols(j, t)], bufs[b][1][...], preferred_element_type=f32)

        def step(n, b, masked):
            scores(n, b, masked)
            softmax(n - 1, 1 - b)
            accum(n - 2, b)

        scores(0, 0, True)
        scores(1, 1, True)
        softmax(0, 0)

        def diag_steps(d, _):
            n = 2 + 2 * d
            step(n, 0, True)
            step(n + 1, 1, True)
            return 0

        lax.fori_loop(0, (nt - 2) // 2, diag_steps, 0)

        def off_steps(d, _):
            n = nt + 2 * d
            step(n, 0, False)
            step(n + 1, 1, False)
            return 0

        lax.fori_loop(0, extra // 2, off_steps, 0)

        @pl.when(extra % 2 == 1)
        def _():
            step(total - 1, 0, False)
            softmax(total - 1, 0)
            accum(total - 2, 1)
            accum(total - 1, 0)

        @pl.when(extra % 2 == 0)
        def _():
            softmax(total - 1, 1)
            accum(total - 2, 0)
            accum(total - 1, 1)

        l = acc_all[HEAD_DIM:HEAD_DIM + 1, :]
        o_ref[0] = acc_all[0:HEAD_DIM, :] / l
        lse_ref[0] = m_all[...] + jnp.log(l)

    smem = pl.BlockSpec(memory_space=pltpu.SMEM)
    return pl.pallas_call(
        body,
        name="fox_fwd",
        grid=(nh,),
        in_specs=[smem, smem, smem,
                  pl.BlockSpec((1, AUG, s_len), lambda h: (h, 0, 0)),
                  pl.BlockSpec((1, s_len, AUG), lambda h: (h, 0, 0)),
                  pl.BlockSpec((1, AUG, s_len), lambda h: (h, 0, 0))],
        out_specs=[pl.BlockSpec((1, HEAD_DIM, s_len), lambda h: (h, 0, 0)),
                   pl.BlockSpec((1, 1, s_len), lambda h: (h, 0, 0))],
        out_shape=[_sds((nh, HEAD_DIM, s_len), f32), _sds((nh, 1, s_len), f32)],
        scratch_shapes=[pltpu.VMEM((t, t), f32), pltpu.VMEM((t, t), f32), pltpu.VMEM((t, t), bf16),
                        pltpu.VMEM((t, t), bf16), pltpu.VMEM((1, t), f32), pltpu.VMEM((1, t), f32),
                        pltpu.VMEM((1, s_len), f32), pltpu.VMEM((AUG, s_len), f32)],
        compiler_params=_params(("arbitrary",)),
    )(npairs, pair_q, pair_k, qat, ka, vat)


SWA_TS = 512


SWA_W = SWA_GROUP * BLOCK


def _swa_bias_call(rel_bias, bucket_t):
    def body(rb_ref, bk_ref, b_ref, b0_ref):
        bk = bk_ref[...]
        row = lax.broadcasted_iota(jnp.int32, (2 * BLOCK, BLOCK), 0)
        for h in range(SWA_HEADS):
            acc = jnp.full((2 * BLOCK, BLOCK), NEG_INF, f32)
            for b in range(NUM_BUCKETS):
                acc = jnp.where(bk == b, rb_ref[b, h], acc)
            g, hh = divmod(h, SWA_GROUP)
            b_ref[g, :, hh * BLOCK:(hh + 1) * BLOCK] = acc
            b0_ref[g, :, hh * BLOCK:(hh + 1) * BLOCK] = jnp.where(row < BLOCK, NEG_INF, acc)

    return pl.pallas_call(
        body,
        name="swa_bias",
        in_specs=[pl.BlockSpec(memory_space=pltpu.SMEM), pl.BlockSpec(memory_space=pltpu.VMEM)],
        out_shape=[_sds((SWA_KV_HEADS, 2 * BLOCK, SWA_W), f32)] * 2,
        compiler_params=_params(),
    )(rel_bias, bucket_t)


def _swa_bias_bwd_call(dbias, bucket_t):
    def body(d_ref, bk_ref, o_ref):
        bk = bk_ref[...]
        row = lax.broadcasted_iota(jnp.int32, (NUM_BUCKETS, 128), 0)
        col = lax.broadcasted_iota(jnp.int32, (NUM_BUCKETS, 128), 1)
        out = jnp.zeros((NUM_BUCKETS, 128), f32)
        for h in range(SWA_HEADS):
            g, hh = divmod(h, SWA_GROUP)
            d = d_ref[g, :, hh * BLOCK:(hh + 1) * BLOCK]
            for b in range(NUM_BUCKETS):
                val = jnp.sum(jnp.sum(jnp.where(bk == b, d, 0.0), axis=1, keepdims=True), axis=0, keepdims=True)
                out = jnp.where((row == b) & (col == h), val, out)
        o_ref[...] = out

    return pl.pallas_call(
        body,
        name="swa_bias_bwd",
        out_shape=_sds((NUM_BUCKETS, 128), f32),
        compiler_params=_params(),
    )(dbias, bucket_t)


def _sink_row(sink_ref, g):
    return jnp.concatenate([jnp.full((1, BLOCK), sink_ref[g * SWA_GROUP + hh], f32) for hh in range(SWA_GROUP)], axis=1)


def _group_lanes(ref, g, cols):
    return jnp.concatenate([ref[g * SWA_GROUP + hh, :, cols] for hh in range(SWA_GROUP)], axis=1)


def _swa_fwd_call(qt, k, vta, bias_t, bias0_t, sink):
    s_len = qt.shape[2]
    ts = SWA_TS
    nb = ts // BLOCK

    def body(qt_ref, kc_ref, kp_ref, vc_ref, vp_ref, b_ref, b0_ref, sink_ref, o_ref, lse_ref):
        first = pl.program_id(0) == 0
        kall = [jnp.concatenate([kp_ref[g], kc_ref[g]], axis=0) for g in range(SWA_KV_HEADS)]
        vall = [jnp.concatenate([vp_ref[g], vc_ref[g]], axis=1) for g in range(SWA_KV_HEADS)]
        sinks = [_sink_row(sink_ref, g) for g in range(SWA_KV_HEADS)]
        items = [(g, b) for g in range(SWA_KV_HEADS) for b in range(nb)]

        def scores(g, b):
            qg = _group_lanes(qt_ref, g, slice(b * BLOCK, (b + 1) * BLOCK))
            bias_b = b_ref[g]
            if b == 0:
                bias_b = jnp.where(first, b0_ref[g], bias_b)
            return jnp.dot(kall[g][b * BLOCK:(b + 2) * BLOCK], qg, preferred_element_type=f32) + bias_b

        def finish(g, b, st):
            m = jnp.maximum(jnp.max(st, axis=0, keepdims=True), sinks[g])
            pt = jnp.exp(st - m)
            acc = jnp.dot(vall[g][:, b * BLOCK:(b + 2) * BLOCK], pt.astype(bf16), preferred_element_type=f32)
            l = acc[HEAD_DIM:HEAD_DIM + 1, :] + jnp.exp(sinks[g] - m)
            return acc[0:HEAD_DIM, :] / l, m + jnp.log(l)

        outs, lses = {}, {}
        st_next = scores(*items[0])
        for idx, (g, b) in enumerate(items):
            st = st_next
            if idx + 1 < len(items):
                st_next = scores(*items[idx + 1])
            outs[g, b], lses[g, b] = finish(g, b, st)
        for g in range(SWA_KV_HEADS):
            for hh in range(SWA_GROUP):
                lanes = slice(hh * BLOCK, (hh + 1) * BLOCK)
                o_ref[g * SWA_GROUP + hh] = jnp.concatenate([outs[g, b][:, lanes] for b in range(nb)], axis=1)
                lse_ref[g * SWA_GROUP + hh] = jnp.concatenate([lses[g, b][:, lanes] for b in range(nb)], axis=1)

    def prev_blk(n):
        return jnp.maximum(n * nb - 1, 0)

    bspec = pl.BlockSpec((SWA_KV_HEADS, 2 * BLOCK, SWA_W), lambda n: (0, 0, 0))
    return pl.pallas_call(
        body,
        name="swa_fwd",
        grid=(s_len // ts,),
        in_specs=[pl.BlockSpec((SWA_HEADS, HEAD_DIM, ts), lambda n: (0, 0, n)),
                  pl.BlockSpec((SWA_KV_HEADS, ts, HEAD_DIM), lambda n: (0, n, 0)),
                  pl.BlockSpec((SWA_KV_HEADS, BLOCK, HEAD_DIM), lambda n: (0, prev_blk(n), 0)),
                  pl.BlockSpec((SWA_KV_HEADS, AUG, ts), lambda n: (0, 0, n)),
                  pl.BlockSpec((SWA_KV_HEADS, AUG, BLOCK), lambda n: (0, 0, prev_blk(n))),
                  bspec, bspec, pl.BlockSpec(memory_space=pltpu.SMEM)],
        out_specs=[pl.BlockSpec((SWA_HEADS, HEAD_DIM, ts), lambda n: (0, 0, n)),
                   pl.BlockSpec((SWA_HEADS, 1, ts), lambda n: (0, 0, n))],
        out_shape=[_sds((SWA_HEADS, HEAD_DIM, s_len), f32), _sds((SWA_HEADS, 1, s_len), f32)],
        compiler_params=_params(("arbitrary",)),
    )(qt, k, k, vta, vta, bias_t, bias0_t, sink)


def _head_selector():
    sel = np.zeros((512, 128), np.float32)
    for h in range(8):
        sel[h * HEAD_DIM:(h + 1) * HEAD_DIM, h] = 1.0
    return sel


def _post_call(of, fz, osw, sz, x2, tgt, wo, ln_g, ln_b, sel, tm):
    s_len = x2.shape[0]

    def body(of_ref, fz_ref, os_ref, sz_ref, x_ref, t_ref, wo_ref, g_ref, b_ref, sel_ref,
             dh_ref, dof_ref, dfz_ref, dos_ref, dsz_ref, dlf_ref, dls_ref, dwo_ref, dg_ref, db_ref, loss_ref):
        n = pl.program_id(0)

        @pl.when(n == 0)
        def _():
            dwo_ref[...] = jnp.zeros_like(dwo_ref)
            dg_ref[...] = jnp.zeros_like(dg_ref)
            db_ref[...] = jnp.zeros_like(db_ref)
            loss_ref[...] = jnp.zeros_like(loss_ref)

        o_f = of_ref[...].T
        o_s = os_ref[...].T
        fz = fz_ref[...]
        sz = sz_ref[...]
        sg_f = jax.nn.sigmoid(fz)
        sg_s = jax.nn.sigmoid(sz)
        silu_f = fz * sg_f
        silu_s = sz * sg_s
        mixed = jnp.concatenate([o_f * silu_f, o_s * silu_s], axis=1).astype(bf16)
        y = jnp.dot(mixed, wo_ref[...], preferred_element_type=f32)
        h = ALPHA * x_ref[...] + y
        mu = jnp.mean(h, axis=1, keepdims=True)
        hc = h - mu
        var = jnp.mean(hc * hc, axis=1, keepdims=True)
        rstd = lax.rsqrt(var + LN_EPS)
        xhat = hc * rstd
        gam = g_ref[...]
        out = xhat * gam + b_ref[...]
        err = out - t_ref[...]
        tok_loss = jnp.mean(err * err, axis=1, keepdims=True)
        loss_ref[...] += 0.5 * jnp.sum(tok_loss, axis=0, keepdims=True)
        dout = err * (1.0 / D_MODEL)
        dg_ref[...] += jnp.sum(dout * xhat, axis=0, keepdims=True)
        db_ref[...] += jnp.sum(dout, axis=0, keepdims=True)
        dxh = dout * gam
        m1 = jnp.mean(dxh, axis=1, keepdims=True)
        m2 = jnp.mean(dxh * xhat, axis=1, keepdims=True)
        dh = rstd * (dxh - m1 - xhat * m2)
        dh_ref[...] = dh
        dyb = dh.astype(bf16)
        dwo_ref[...] += lax.dot_general(mixed, dyb, TN, preferred_element_type=f32)
        dmix = lax.dot_general(dyb, wo_ref[...], NT, preferred_element_type=f32)
        dm_f = dmix[:, :512]
        dm_s = dmix[:, 512:]
        do_f = dm_f * silu_f
        do_s = dm_s * silu_s
        dfz_ref[...] = (dm_f * o_f * (sg_f * (1.0 + fz * (1.0 - sg_f)))).astype(bf16)
        dsz_ref[...] = (dm_s * o_s * (sg_s * (1.0 + sz * (1.0 - sg_s)))).astype(bf16)
        dof_ref[...] = do_f.T.astype(bf16)
        dos_ref[...] = do_s.T.astype(bf16)
        sel_m = sel_ref[...]
        dl_f = jnp.dot(do_f * o_f, sel_m, precision=HIGHEST, preferred_element_type=f32)
        dl_s = jnp.dot(do_s * o_s, sel_m, precision=HIGHEST, preferred_element_type=f32)
        dlf_ref[...] = dl_f.T[:FOX_HEADS, :]
        dls_ref[...] = dl_s.T[:SWA_HEADS, :]

    feat = pl.BlockSpec((512, tm), lambda n: (0, n))
    rows8 = pl.BlockSpec((8, tm), lambda n: (0, n))
    half = pl.BlockSpec((tm, 512), lambda n: (n, 0))
    fullw = pl.BlockSpec((tm, D_MODEL), lambda n: (n, 0))
    vec = pl.BlockSpec((1, D_MODEL), lambda n: (0, 0))
    return pl.pallas_call(
        body,
        name="post_fwd_bwd",
        grid=(s_len // tm,),
        in_specs=[feat, half, feat, half, fullw, fullw,
                  pl.BlockSpec((D_MODEL, D_MODEL), lambda n: (0, 0)), vec, vec,
                  pl.BlockSpec((512, 128), lambda n: (0, 0))],
        out_specs=[fullw, feat, half, feat, half, rows8, rows8,
                   pl.BlockSpec((D_MODEL, D_MODEL), lambda n: (0, 0)), vec, vec,
                   pl.BlockSpec((1, 1), lambda n: (0, 0))],
        out_shape=[_sds((s_len, D_MODEL), f32), _sds((512, s_len), bf16), _sds((s_len, 512), bf16),
                   _sds((512, s_len), bf16), _sds((s_len, 512), bf16),
                   _sds((FOX_HEADS, s_len), f32), _sds((SWA_HEADS, s_len), f32),
                   _sds((D_MODEL, D_MODEL), f32), _sds((1, D_MODEL), f32), _sds((1, D_MODEL), f32),
                   _sds((1, 1), f32)],
        compiler_params=_params(("arbitrary",)),
    )(of, fz, osw, sz, x2, tgt, wo, ln_g, ln_b, sel)


def _fox_bwd_call(ka, kat, v, qat, dot, lse_row, dl_row, npairs, pair_q, pair_k):
    nh, s_len, _ = ka.shape
    t = FOX_T
    nt = s_len // t
    kmax = nt * (nt - 1) // 2
    assert nt >= 2 and nt % 2 == 0
    ck_slot = HEAD_DIM + 3
    cq_slot = HEAD_DIM

    def body(np_ref, pq_ref, pk_ref, ka_ref, kat_ref, v_ref, qat_ref, dot_ref, lse_ref, dl_ref,
             dq_ref, dk_ref, dv_ref, dcq_ref, dck_ref, dqt_all, dkat_all, dvt_all, p0, p1, ds0, ds1):
        h = pl.program_id(0)
        extra = np_ref[h]
        total = nt + extra
        dqt_all[...] = jnp.zeros(dqt_all.shape, f32)
        dkat_all[...] = jnp.zeros(dkat_all.shape, f32)
        dvt_all[...] = jnp.zeros(dvt_all.shape, f32)
        pbuf, dsbuf = (p0, p1), (ds0, ds1)

        def pair(n):
            return _fox_pair(n, nt, kmax, h, pq_ref, pk_ref)

        def probs(n, b, masked):
            i, j = pair(n)
            qc, kr = _tile_cols(i, t), _tile_cols(j, t)
            st = jnp.dot(ka_ref[0, kr, :], qat_ref[0, :, qc], preferred_element_type=f32)
            if masked:
                st = jnp.where(_causal_keep(t), st, NEG_INF)
            pt = jnp.exp(st - lse_ref[0, :, qc])
            dpt = jnp.dot(v_ref[0, kr, :], dot_ref[0, :, qc], preferred_element_type=f32)
            pbuf[b][...] = pt.astype(bf16)
            dsbuf[b][...] = (pt * (dpt - dl_ref[0, :, qc])).astype(bf16)

        def grads(n, b):
            i, j = pair(n)
            qc, kc = _tile_cols(i, t), _tile_cols(j, t)
            dvt_all[:, kc] += lax.dot_general(dot_ref[0, :, qc], pbuf[b][...], NT, preferred_element_type=f32)
            dkat_all[:, kc] += lax.dot_general(qat_ref[0, :, qc], dsbuf[b][...], NT, preferred_element_type=f32)
            dqt_all[:, qc] += jnp.dot(kat_ref[0, :, kc], dsbuf[b][...], preferred_element_type=f32)

        def step(n, b, masked):
            probs(n, b, masked)
            grads(n - 1, 1 - b)

        probs(0, 0, True)
        step(1, 1, True)

        def diag_steps(d, _):
            n = 2 + 2 * d
            step(n, 0, True)
            step(n + 1, 1, True)
            return 0

        lax.fori_loop(0, (nt - 2) // 2, diag_steps, 0)

        def off_steps(d, _):
            n = nt + 2 * d
            step(n, 0, False)
            step(n + 1, 1, False)
            return 0

        lax.fori_loop(0, extra // 2, off_steps, 0)

        @pl.when(extra % 2 == 1)
        def _():
            step(total - 1, 0, False)
            grads(total - 1, 0)

        @pl.when(extra % 2 == 0)
        def _():
            grads(total - 1, 1)

        dq_ref[0] = (dqt_all[0:HEAD_DIM, :] * SCALE).astype(bf16)
        dk_ref[0] = dkat_all[0:HEAD_DIM, :].astype(bf16)
        dv_ref[0] = dvt_all[...].astype(bf16)
        dcq_ref[0] = dqt_all[cq_slot:cq_slot + 1, :]
        dck_ref[0] = dkat_all[ck_slot:ck_slot + 1, :]

    smem = pl.BlockSpec(memory_space=pltpu.SMEM)
    rows = pl.BlockSpec((1, s_len, AUG), lambda h: (h, 0, 0))
    feat = pl.BlockSpec((1, AUG, s_len), lambda h: (h, 0, 0))
    feat64 = pl.BlockSpec((1, HEAD_DIM, s_len), lambda h: (h, 0, 0))
    rowv = pl.BlockSpec((1, 1, s_len), lambda h: (h, 0, 0))
    return pl.pallas_call(
        body,
        name="fox_bwd",
        grid=(nh,),
        in_specs=[smem, smem, smem, rows, feat, pl.BlockSpec((1, s_len, HEAD_DIM), lambda h: (h, 0, 0)), feat, feat64,
                  rowv, rowv],
        out_specs=[feat64, feat64, feat64, rowv, rowv],
        out_shape=[_sds((nh, HEAD_DIM, s_len), bf16)] * 3 + [_sds((nh, 1, s_len), f32)] * 2,
        scratch_shapes=[pltpu.VMEM((AUG, s_len), f32), pltpu.VMEM((AUG, s_len), f32), pltpu.VMEM((HEAD_DIM, s_len), f32)]
                       + [pltpu.VMEM((t, t), bf16)] * 4,
        compiler_params=_params(("arbitrary",)),
    )(npairs, pair_q, pair_k, ka, kat, v, qat, dot, lse_row, dl_row)


def _swa_bwd_call(qt, k, kt, v, dot, lse, dl, bias_t, bias0_t, sink):
    s_len = qt.shape[2]
    ts = SWA_TS
    nb = ts // BLOCK
    nsteps = s_len // ts

    def body(qt_ref, kc_ref, kp_ref, ktc_ref, ktp_ref, vc_ref, vp_ref, dot_ref, lse_ref, dl_ref, b_ref, b0_ref,
             sink_ref, dq_ref, dk_ref, dv_ref, dbias_ref, dsink_ref, dk_s, dv_s, tail_k, tail_v, sk_s):
        n = pl.program_id(0)

        @pl.when(n == 0)
        def _():
            dbias_ref[...] = jnp.zeros_like(dbias_ref)
            sk_s[...] = jnp.zeros_like(sk_s)

        @pl.when(n < nsteps)
        def _():
            first = n == 0
            dk_s[...] = jnp.zeros_like(dk_s)
            dv_s[...] = jnp.zeros_like(dv_s)
            groups = range(SWA_KV_HEADS)
            kall = [jnp.concatenate([kp_ref[g], kc_ref[g]], axis=0) for g in groups]
            vall = [jnp.concatenate([vp_ref[g], vc_ref[g]], axis=0) for g in groups]
            ktall = [jnp.concatenate([ktp_ref[g], ktc_ref[g]], axis=1) for g in groups]
            sinks = [_sink_row(sink_ref, g) for g in groups]
            items = [(g, b) for g in groups for b in range(nb)]

            def products(g, b):
                cols = slice(b * BLOCK, (b + 1) * BLOCK)
                win = slice(b * BLOCK, (b + 2) * BLOCK)
                qg = _group_lanes(qt_ref, g, cols)
                dog = _group_lanes(dot_ref, g, cols)
                bias_b = b_ref[g]
                if b == 0:
                    bias_b = jnp.where(first, b0_ref[g], bias_b)
                st = jnp.dot(kall[g][win], qg, preferred_element_type=f32) + bias_b
                dpt = jnp.dot(vall[g][win], dog, preferred_element_type=f32)
                return qg, dog, st, dpt

            def finish(g, b, qg, dog, st, dpt):
                cols = slice(b * BLOCK, (b + 1) * BLOCK)
                win = slice(b * BLOCK, (b + 2) * BLOCK)
                lse_r = _group_lanes(lse_ref, g, cols)
                dl_r = _group_lanes(dl_ref, g, cols)
                pt = jnp.exp(st - lse_r)
                dst = pt * (dpt - dl_r)
                dsb = dst.astype(bf16)
                dk_s[g, :, win] += lax.dot_general(qg, dsb, NT, preferred_element_type=f32)
                dv_s[g, :, win] += lax.dot_general(dog, pt.astype(bf16), NT, preferred_element_type=f32)
                dqg = jnp.dot(ktall[g][:, win], dsb, preferred_element_type=f32) * SCALE
                return dqg, dst, -jnp.exp(sinks[g] - lse_r) * dl_r

            dqs, dsts, sks = {}, {}, {}
            nxt = products(*items[0])
            for idx, (g, b) in enumerate(items):
                cur = nxt
                if idx + 1 < len(items):
                    nxt = products(*items[idx + 1])
                dqs[g, b], dsts[g, b], sks[g, b] = finish(g, b, *cur)
            for g in groups:
                dbias_ref[g] += functools.reduce(lambda a, c: a + c, [dsts[g, b] for b in range(nb)])
                sk_s[g] += functools.reduce(lambda a, c: a + c, [sks[g, b] for b in range(nb)])
                for hh in range(SWA_GROUP):
                    lanes = slice(hh * BLOCK, (hh + 1) * BLOCK)
                    dq_ref[g * SWA_GROUP + hh] = jnp.concatenate(
                        [dqs[g, b][:, lanes] for b in range(nb)], axis=1).astype(bf16)

        @pl.when(n > 0)
        def _():
            last = slice(ts - BLOCK, ts)
            for g in range(SWA_KV_HEADS):
                add_k = jnp.where(n < nsteps, dk_s[g, :, 0:BLOCK], 0.0)
                add_v = jnp.where(n < nsteps, dv_s[g, :, 0:BLOCK], 0.0)
                dk_ref[g, :, 0:ts - BLOCK] = tail_k[g, :, 0:ts - BLOCK].astype(bf16)
                dv_ref[g, :, 0:ts - BLOCK] = tail_v[g, :, 0:ts - BLOCK].astype(bf16)
                dk_ref[g, :, last] = (tail_k[g, :, last] + add_k).astype(bf16)
                dv_ref[g, :, last] = (tail_v[g, :, last] + add_v).astype(bf16)

        @pl.when(n < nsteps)
        def _():
            tail_k[...] = dk_s[:, :, BLOCK:]
            tail_v[...] = dv_s[:, :, BLOCK:]

        @pl.when(n == nsteps)
        def _():
            row = lax.broadcasted_iota(jnp.int32, (SWA_HEADS, 128), 0)
            out = jnp.zeros((SWA_HEADS, 128), f32)
            for h in range(SWA_HEADS):
                g, hh = divmod(h, SWA_GROUP)
                val = jnp.sum(sk_s[g, :, hh * BLOCK:(hh + 1) * BLOCK], axis=1, keepdims=True)
                out = jnp.where(row == h, val, out)
            dsink_ref[...] = out

    last_step = nsteps - 1

    def cl(n):
        return jnp.minimum(n, last_step)

    def prev_blk(n):
        return jnp.maximum(cl(n) * nb - 1, 0)

    feat8 = pl.BlockSpec((SWA_HEADS, HEAD_DIM, ts), lambda n: (0, 0, cl(n)))
    rows8 = pl.BlockSpec((SWA_HEADS, 1, ts), lambda n: (0, 0, cl(n)))
    cur = pl.BlockSpec((SWA_KV_HEADS, ts, HEAD_DIM), lambda n: (0, cl(n), 0))
    prev = pl.BlockSpec((SWA_KV_HEADS, BLOCK, HEAD_DIM), lambda n: (0, prev_blk(n), 0))
    curt = pl.BlockSpec((SWA_KV_HEADS, HEAD_DIM, ts), lambda n: (0, 0, cl(n)))
    prevt = pl.BlockSpec((SWA_KV_HEADS, HEAD_DIM, BLOCK), lambda n: (0, 0, prev_blk(n)))
    bspec = pl.BlockSpec((SWA_KV_HEADS, 2 * BLOCK, SWA_W), lambda n: (0, 0, 0))
    kvout = pl.BlockSpec((SWA_KV_HEADS, HEAD_DIM, ts), lambda n: (0, 0, jnp.maximum(n - 1, 0)))
    return pl.pallas_call(
        body,
        name="swa_bwd",
        grid=(nsteps + 1,),
        in_specs=[feat8, cur, prev, curt, prevt, cur, prev, feat8, rows8, rows8, bspec, bspec,
                  pl.BlockSpec(memory_space=pltpu.SMEM)],
        out_specs=[feat8, kvout, kvout, bspec, pl.BlockSpec((SWA_HEADS, 128), lambda n: (0, 0))],
        out_shape=[_sds((SWA_HEADS, HEAD_DIM, s_len), bf16), _sds((SWA_KV_HEADS, HEAD_DIM, s_len), bf16),
                   _sds((SWA_KV_HEADS, HEAD_DIM, s_len), bf16),
                   _sds((SWA_KV_HEADS, 2 * BLOCK, SWA_W), f32), _sds((SWA_HEADS, 128), f32)],
        scratch_shapes=[pltpu.VMEM((SWA_KV_HEADS, HEAD_DIM, ts + BLOCK), f32),
                        pltpu.VMEM((SWA_KV_HEADS, HEAD_DIM, ts + BLOCK), f32),
                        pltpu.VMEM((SWA_KV_HEADS, HEAD_DIM, ts), f32),
                        pltpu.VMEM((SWA_KV_HEADS, HEAD_DIM, ts), f32),
                        pltpu.VMEM((SWA_KV_HEADS, 1, SWA_W), f32)],
        compiler_params=_params(("arbitrary",)),
    )(qt, k, k, kt, kt, v, v, dot, lse, dl, bias_t, bias0_t, sink)


def _dproj_specs(tm):
    half = pl.BlockSpec((tm, 512), lambda i: (i, 0))
    feat = pl.BlockSpec((512, tm), lambda i: (0, i))
    feat_kv = pl.BlockSpec((128, tm), lambda i: (0, i))
    return [feat, feat, feat, half, feat, feat_kv, feat_kv, half, feat_kv]


def _dx_exchange_call(dh, pieces, w_t, bs, tm):
    s_len = dh.shape[0]
    n = len(bs)
    last = s_len // tm - 1

    def body(*refs):
        dh_ref, dqf_ref, dkf_ref, dvf_ref, dfz_ref, dqs_ref, dks_ref, dvs_ref, dsz_ref, dfft_ref, w_ref = refs[:11]
        b_refs = refs[11:11 + n]
        dx_ref = refs[11 + n]
        r_refs = refs[12 + n:12 + 2 * n]
        sems = refs[12 + 2 * n:]
        i = pl.program_id(0)

        @pl.when(i == 0)
        def _():
            _exchange_start(b_refs, r_refs, sems)

        def tr(ref):
            return ref[...].astype(f32).T.astype(bf16)

        dp = jnp.concatenate([tr(dqf_ref), tr(dkf_ref), tr(dvf_ref), dfz_ref[...], tr(dqs_ref), tr(dks_ref),
                              tr(dvs_ref), dsz_ref[...], tr(dfft_ref)], axis=1)
        dx_ref[...] = ALPHA * dh_ref[...] + jnp.dot(dp, w_ref[...], preferred_element_type=f32)

        @pl.when(i == last)
        def _():
            _exchange_wait(b_refs, r_refs, sems)

    fullw = pl.BlockSpec((tm, D_MODEL), lambda i: (i, 0))
    any_spec = pl.BlockSpec(memory_space=pl.ANY)
    out = pl.pallas_call(
        body,
        name="dx_bwd_exchange",
        grid=(s_len // tm,),
        in_specs=[fullw] + _dproj_specs(tm) + [pl.BlockSpec((A_W, D_MODEL), lambda i: (0, 0))] + [any_spec] * n,
        out_specs=[fullw] + [any_spec] * n,
        out_shape=[_sds((s_len, D_MODEL), f32)] + [_sds(b.shape, b.dtype) for b in bs],
        scratch_shapes=[pltpu.SemaphoreType.DMA((7 * n,)), pltpu.SemaphoreType.DMA((7 * n,)),
                        pltpu.SemaphoreType.DMA((n,))],
        compiler_params=_params(("arbitrary",)),
    )(dh, *pieces, w_t, *bs)
    return out[0], out[1:]


DW_STAGE_ROWS = 384


def _dw_call(x2, pieces, tm):
    s_len = x2.shape[0]
    nt = s_len // tm

    def body(x_ref, dqf_ref, dkf_ref, dvf_ref, dfz_ref, dqs_ref, dks_ref, dvs_ref, dsz_ref, dfft_ref, dw_ref,
             acc_ref, stage_ref, sem):
        i = pl.program_id(0)

        @pl.when(i == 0)
        def _():
            acc_ref[...] = jnp.zeros_like(acc_ref)

        xb = x_ref[...].astype(bf16)

        def add_feat(off, lhs):
            acc_ref[off:off + lhs.shape[0], :] += jnp.dot(lhs, xb, preferred_element_type=f32)

        def add_rows(off, piece):
            acc_ref[off:off + piece.shape[1], :] += lax.dot_general(piece, xb, TN, preferred_element_type=f32)

        add_feat(A_FQ, dqf_ref[...])
        add_feat(A_FK, dkf_ref[...])
        add_feat(A_FV, dvf_ref[...])
        add_rows(A_FZ, dfz_ref[...])
        add_feat(A_SQ, dqs_ref[...])
        add_feat(A_SK, dks_ref[...])
        add_feat(A_SV, dvs_ref[...])
        add_rows(A_SZ, dsz_ref[...])
        add_feat(A_FF, dfft_ref[...].astype(bf16))

        @pl.when(i == nt - 1)
        def _():
            for r in range(A_W // DW_STAGE_ROWS):
                rows = slice(r * DW_STAGE_ROWS, (r + 1) * DW_STAGE_ROWS)
                stage_ref[...] = acc_ref[rows, :].astype(bf16)
                cp = pltpu.make_async_copy(stage_ref, dw_ref.at[rows, :], sem)
                cp.start()
                cp.wait()

    return pl.pallas_call(
        body,
        name="dw_in_bwd",
        grid=(nt,),
        in_specs=[pl.BlockSpec((tm, D_MODEL), lambda i: (i, 0))] + _dproj_specs(tm),
        out_specs=pl.BlockSpec(memory_space=pl.ANY),
        out_shape=_sds((A_W, D_MODEL), bf16),
        scratch_shapes=[pltpu.VMEM((A_W, D_MODEL), f32), pltpu.VMEM((DW_STAGE_ROWS, D_MODEL), bf16),
                        pltpu.SemaphoreType.DMA],
        compiler_params=_params(("arbitrary",), VMEM_LIMIT_BIG),
    )(x2, *pieces)


def _adam_call(recv, w, m, v, tc, name):
    rows, cols = w.shape

    def body(r_ref, w_ref, m_ref, v_ref, g_ref, d_ref, mo_ref, vo_ref):
        g = r_ref[0].astype(f32)
        for p in range(1, N_DEV):
            g = g + r_ref[p].astype(f32)
        mn = ADAM_B1 * m_ref[...] + (1.0 - ADAM_B1) * g
        vn = ADAM_B2 * v_ref[...] + (1.0 - ADAM_B2) * (g * g)
        m_hat = mn / (1.0 - ADAM_B1 ** ADAM_STEP)
        v_hat = vn / (1.0 - ADAM_B2 ** ADAM_STEP)
        g_ref[...] = g
        d_ref[...] = -ADAM_LR * (m_hat / (jnp.sqrt(v_hat) + ADAM_EPS) + ADAM_WD * w_ref[...])
        mo_ref[...] = mn
        vo_ref[...] = vn

    blk = pl.BlockSpec((rows, tc), lambda i: (0, i))
    return pl.pallas_call(
        body,
        name=name,
        grid=(cols // tc,),
        in_specs=[pl.BlockSpec((N_DEV, rows, tc), lambda i: (0, 0, i)), blk, blk, blk],
        out_specs=[blk] * 4,
        out_shape=[_sds((rows, cols), f32)] * 4,
        compiler_params=_params(("arbitrary",)),
    )(recv, w, m, v)


def _pad_cols(a, width=128):
    return jnp.pad(a, ((0, 0), (0, width - a.shape[1])))


def _pack_small(ln_g, ln_b, rel, b_f, sink):
    return jnp.concatenate([
        ln_g.reshape(8, 128), ln_b.reshape(8, 128), _pad_cols(rel),
        jnp.pad(_pad_cols(b_f), ((0, 7), (0, 0))), jnp.pad(_pad_cols(sink), ((0, 7), (0, 0)))], axis=0)


def _unpack_small(p):
    return (p[0:8].reshape(1, D_MODEL), p[8:16].reshape(1, D_MODEL), p[16:48, 0:8], p[48:49, 0:8], p[56:57, 0:8])


def kernel(x, w_in, b_f, rel_bias, sink, w_o, ln_g, ln_b, loss_target, m_w_in, m_b_f, m_rel_bias, m_sink, m_w_o, m_ln_g, m_ln_b, v_w_in, v_b_f, v_rel_bias, v_sink, v_w_o, v_ln_g, v_ln_b):
    x2 = x[0]
    tgt = loss_target[0]
    s_len = x2.shape[0]
    shard = w_in.shape[2]

    w_in_t = jnp.transpose(w_in[0])
    g_in, g_o = _gather_call([w_in_t.astype(bf16), w_o[0].astype(bf16)])
    wt_full = g_in.reshape(N_DEV * shard, D_MODEL)
    w_t = jnp.concatenate([wt_full[:O_FF0], wt_full[O_FF1:], wt_full[O_FF0:O_FF1],
                           jnp.zeros((A_W - D_IN, D_MODEL), bf16)], axis=0)
    wo_full = g_o.reshape(D_MODEL, D_MODEL)

    qf, kf, vf, fz, qst, ks, vs, sz, fft, vat, kst, vsta = _proj_call(x2, w_t, 512)
    cum, sgm = _cum_call(fft, b_f.reshape(FOX_HEADS, 1))
    qat, ka, kat, tile_stats = _augment_call(qf, kf, cum.reshape(FOX_HEADS, 1, s_len), 2048)
    npairs, pair_q, pair_k = _fox_prune_tables(tile_stats)
    o_ft, lse_f = _fox_fwd_call(qat, ka, vat, npairs, pair_q, pair_k)
    bucket_t = jnp.asarray(_t5_bucket_table().T)
    bias_t, bias0_t = _swa_bias_call(rel_bias, bucket_t)
    sink_v = sink.reshape(SWA_HEADS)
    o_st, lse_s = _swa_fwd_call(qst, ks, vsta, bias_t, bias0_t, sink_v)

    (dh, do_f, dfz, do_s, dsz, dl_f, dl_s, dwo, dg, db, loss_part) = _post_call(
        o_ft.reshape(FOX_HEADS * HEAD_DIM, s_len), fz, o_st.reshape(SWA_HEADS * HEAD_DIM, s_len), sz, x2, tgt,
        wo_full, ln_g, ln_b, jnp.asarray(_head_selector()), 256)

    dqf, dkf, dvf, dcq, dck = _fox_bwd_call(ka, kat, vf, qat, do_f.reshape(FOX_HEADS, HEAD_DIM, s_len), lse_f,
                                            dl_f.reshape(FOX_HEADS, 1, s_len), npairs, pair_q, pair_k)
    dqf, dkf, dvf = (a.reshape(FOX_HEADS * HEAD_DIM, s_len) for a in (dqf, dkf, dvf))
    dfft, dbf = _cum_bwd_call(dcq.reshape(FOX_HEADS, s_len), dck.reshape(FOX_HEADS, s_len), sgm)
    dqs, dks, dvs, dbias, dsink = _swa_bwd_call(
        qst, ks, kst, vs, do_s.reshape(SWA_HEADS, HEAD_DIM, s_len), lse_s, dl_s.reshape(SWA_HEADS, 1, s_len),
        bias_t, bias0_t, sink_v)
    dqs = dqs.reshape(SWA_HEADS * HEAD_DIM, s_len)
    dks, dvs = (a.reshape(SWA_KV_HEADS * HEAD_DIM, s_len) for a in (dks, dvs))
    drel = _swa_bias_bwd_call(dbias, bucket_t)

    pieces = (dqf, dkf, dvf, dfz, dqs, dks, dvs, dsz, dfft)
    dw_t = _dw_call(x2, pieces, 1024)

    dwt_full = jnp.concatenate([dw_t[:O_FF0], dw_t[A_FF:A_FF + (O_FF1 - O_FF0)], dw_t[O_FF0:A_FF]], axis=0)
    dw_blocks = dwt_full.reshape(N_DEV, shard, D_MODEL)
    dwo_blocks = dwo.reshape(N_DEV, D_MODEL // N_DEV, D_MODEL).astype(bf16)
    small = _pack_small(dg, db, drel[:, 0:8], dbf[:, 0].reshape(1, 8), dsink[:, 0].reshape(1, 8))
    loss_slot = np.zeros((64, 128), bool)
    loss_slot[49, 0] = True
    small = jnp.where(jnp.asarray(loss_slot), loss_part[0, 0], small)
    small_blocks = jnp.broadcast_to(small[None], (N_DEV,) + small.shape)
    dx, (r_in, r_o, r_small) = _dx_exchange_call(dh, pieces, w_t, [dw_blocks, dwo_blocks, small_blocks], 256)

    win_t = [jnp.transpose(a) for a in _adam_call(
        r_in, w_in_t, jnp.transpose(m_w_in[0]), jnp.transpose(v_w_in[0]), 256, "adam_w_in")]
    g_win, d_win, nm_win, nv_win = win_t
    g_wo, d_wo, nm_wo, nv_wo = _adam_call(r_o, w_o[0], m_w_o[0], v_w_o[0], 256, "adam_w_o")
    p_w = _pack_small(ln_g, ln_b, rel_bias, b_f, sink)
    p_m = _pack_small(m_ln_g, m_ln_b, m_rel_bias, m_b_f, m_sink)
    p_v = _pack_small(v_ln_g, v_ln_b, v_rel_bias, v_b_f, v_sink)
    g_p, d_p, nm_p, nv_p = _adam_call(r_small, p_w, p_m, p_v, 128, "adam_small")

    loss = g_p[49, 0]
    g_lng, g_lnb, g_rel, g_bf, g_sink = _unpack_small(g_p)
    d_lng, d_lnb, d_rel, d_bf, d_sink = _unpack_small(d_p)
    m_lng, m_lnb, m_rel, m_bf, m_sk = _unpack_small(nm_p)
    v_lng, v_lnb, v_rel, v_bf, v_sk = _unpack_small(nv_p)
    return (loss, dx[None], g_win[None], g_bf, g_rel, g_sink, g_wo[None], g_lng, g_lnb,
            d_win[None], d_bf, d_rel, d_sink, d_wo[None], d_lng, d_lnb,
            nm_win[None], m_bf, m_rel, m_sk, nm_wo[None], m_lng, m_lnb,
            nv_win[None], v_bf, v_rel, v_sk, nv_wo[None], v_lng, v_lnb)
```

```python
import functools
import math

import numpy as np
import jax
import jax.numpy as jnp
from jax import lax
from jax.experimental import pallas as pl
from jax.experimental.pallas import tpu as pltpu

f32 = jnp.float32
bf16 = jnp.bfloat16

D_MODEL = 1024
HEAD_DIM = 64
FOX_HEADS = 8
SWA_HEADS = 8
SWA_KV_HEADS = 2
SWA_GROUP = 4
BLOCK = 128
NUM_BUCKETS = 32
MAX_DISTANCE = 128
LN_EPS = 1e-5
NEG_INF = -1e30
ALPHA = 2.0 ** 0.25
SCALE = 1.0 / math.sqrt(HEAD_DIM)
D_IN = 3336

ADAM_LR = 0.001
ADAM_B1 = 0.9
ADAM_B2 = 0.999
ADAM_EPS = 1e-08
ADAM_WD = 0.01
ADAM_STEP = 10

N_DEV = 8
A_FQ, A_FK, A_FV, A_FZ, A_SQ, A_SK, A_SV, A_SZ, A_FF, A_W = 0, 512, 1024, 1536, 2048, 2560, 2688, 2816, 3328, 3456
O_FF0, O_FF1 = 1536, 1544

VMEM_LIMIT = 48 * 1024 * 1024
HIGHEST = lax.Precision.HIGHEST
NT = (((1,), (1,)), ((), ()))
TN = (((0,), (0,)), ((), ()))
MESH = pl.DeviceIdType.MESH
RELS = [(0, 0, 1), (0, 1, 0), (0, 1, 1), (1, 0, 0), (1, 0, 1), (1, 1, 0), (1, 1, 1)]


VMEM_LIMIT_BIG = 60 * 1024 * 1024


def _params(sem=None, vmem=VMEM_LIMIT):
    return pltpu.CompilerParams(dimension_semantics=sem, vmem_limit_bytes=vmem)


def _sds(shape, dtype):
    return jax.ShapeDtypeStruct(shape, dtype)


def _t5_bucket_table():
    qi = np.arange(BLOCK)[:, None]
    kj = np.arange(2 * BLOCK)[None, :]
    rel = qi + BLOCK - kj
    band = (rel >= 0) & (rel < BLOCK)
    relc = np.maximum(rel, 0)
    max_exact = NUM_BUCKETS // 2
    relf = np.maximum(relc, 1).astype(np.float32)
    large = max_exact + (np.log(relf / np.float32(max_exact)) / np.float32(math.log(MAX_DISTANCE / max_exact))
                         * np.float32(NUM_BUCKETS - max_exact)).astype(np.int32)
    large = np.minimum(large, NUM_BUCKETS - 1)
    bucket = np.where(relc < max_exact, relc, large).astype(np.int32)
    bucket = np.where(band, bucket, -1).astype(np.int32)
    return bucket


def _mesh_pos():
    return lax.axis_index("x"), lax.axis_index("y"), lax.axis_index("c")


def _dev_index(p):
    return 4 * p[0] + 2 * p[1] + p[2]


def _gather_call(xs):
    n = len(xs)

    def body(*refs):
        x_refs, o_refs = refs[:n], refs[n:2 * n]
        send_sems, recv_sems, local_sems = refs[2 * n:]
        x, y, c = _mesh_pos()
        me, sib = (x, y, c), (x, y, 1 - c)
        chips = [(1 - x, y), (x, 1 - y), (1 - x, 1 - y)]

        def copy(a, k, block, to, src=None):
            slot = o_refs[a].at[_dev_index(block)]
            return pltpu.make_async_remote_copy(
                src_ref=slot if src is None else src, dst_ref=slot,
                send_sem=send_sems.at[a * 7 + k], recv_sem=recv_sems.at[a * 7 + k],
                device_id=to, device_id_type=MESH)

        mine = [pltpu.make_async_copy(x_refs[a], o_refs[a].at[_dev_index(me)], local_sems.at[a]) for a in range(n)]
        for cp in mine:
            cp.start()
        first = []
        for a in range(n):
            first.append(copy(a, 0, me, sib, src=x_refs[a]))
            first += [copy(a, 1 + j, me, (*chip, c), src=x_refs[a]) for j, chip in enumerate(chips)]
        for cp in first:
            cp.start()
        passed = []
        for j, chip in enumerate(chips):
            for a in range(n):
                copy(a, 1 + j, (*chip, c), me).wait_recv()
                fwd = copy(a, 4 + j, (*chip, c), sib)
                fwd.start()
                passed.append(fwd)
        for a in range(n):
            copy(a, 0, sib, me).wait_recv()
            for j, chip in enumerate(chips):
                copy(a, 4 + j, (*chip, 1 - c), me).wait_recv()
        for cp in first + passed:
            cp.wait_send()
        for cp in mine:
            cp.wait()

    any_spec = pl.BlockSpec(memory_space=pl.ANY)
    return pl.pallas_call(
        body,
        name="gather_weights",
        out_shape=[_sds((N_DEV,) + a.shape, a.dtype) for a in xs],
        in_specs=[any_spec] * n,
        out_specs=[any_spec] * n,
        scratch_shapes=[pltpu.SemaphoreType.DMA((7 * n,)), pltpu.SemaphoreType.DMA((7 * n,)),
                        pltpu.SemaphoreType.DMA((n,))],
    )(*xs)


def _exchange_copies(b_refs, r_refs, send_sems, recv_sems, local_sems, incoming):
    n = len(b_refs)
    x, y, c = _mesh_pos()
    me_idx = _dev_index((x, y, c))
    mine = [pltpu.make_async_copy(b_refs[a].at[me_idx], r_refs[a].at[me_idx], local_sems.at[a]) for a in range(n)]
    remote = []
    for k, r in enumerate(RELS):
        peer = ((1 - x) if r[0] else x, (1 - y) if r[1] else y, (1 - c) if r[2] else c)
        pidx = _dev_index(peer)
        for a in range(n):
            remote.append(pltpu.make_async_remote_copy(
                src_ref=b_refs[a].at[pidx], dst_ref=r_refs[a].at[pidx if incoming else me_idx],
                send_sem=send_sems.at[a * 7 + k], recv_sem=recv_sems.at[a * 7 + k],
                device_id=peer, device_id_type=MESH))
    return mine, remote


def _exchange_start(b_refs, r_refs, sems):
    mine, out = _exchange_copies(b_refs, r_refs, *sems, incoming=False)
    for cp in mine + out:
        cp.start()


def _exchange_wait(b_refs, r_refs, sems):
    mine, inc = _exchange_copies(b_refs, r_refs, *sems, incoming=True)
    for cp in inc:
        cp.wait_recv()
    for cp in inc:
        cp.wait_send()
    for cp in mine:
        cp.wait()


def _proj_call(x2, w_t, tm):
    s_len = x2.shape[0]

    def body(x_ref, w_ref, qft_ref, kf_ref, vf_ref, fz_ref, qst_ref, ks_ref, vs_ref, sz_ref, fft_ref, vat_ref,
             kst_ref, vsta_ref, kft_ref):
        xb = x_ref[...].astype(bf16)

        def seg_t(off, width):
            return lax.dot_general(w_ref[off:off + width, :], xb, NT, preferred_element_type=f32)

        def seg(off, width):
            return lax.dot_general(xb, w_ref[off:off + width, :], NT, preferred_element_type=f32)

        def put_heads(ref, acc, nheads):
            for h in range(nheads):
                ref[h] = acc[:, h * HEAD_DIM:(h + 1) * HEAD_DIM].astype(bf16)

        def put_heads_t(ref, acc_t, nheads, augment):
            for h in range(nheads):
                ref[h, 0:HEAD_DIM, :] = acc_t[h * HEAD_DIM:(h + 1) * HEAD_DIM, :].astype(bf16)
                if augment:
                    ref[h, HEAD_DIM:2 * HEAD_DIM, :] = ones_row

        ones_row = jnp.where(lax.broadcasted_iota(jnp.int32, (HEAD_DIM, tm), 0) == 0, 1.0, 0.0).astype(bf16)
        put_heads_t(vat_ref, seg_t(A_FV, 512), FOX_HEADS, True)
        put_heads_t(qft_ref, seg_t(A_FQ, 512) * SCALE, FOX_HEADS, False)
        put_heads(kf_ref, seg(A_FK, 512), FOX_HEADS)
        put_heads_t(kft_ref, seg_t(A_FK, 512), FOX_HEADS, False)
        put_heads(vf_ref, seg(A_FV, 512), FOX_HEADS)
        fz_ref[...] = seg(A_FZ, 512)
        put_heads_t(qst_ref, seg_t(A_SQ, 512) * SCALE, SWA_HEADS, False)
        put_heads(ks_ref, seg(A_SK, 128), SWA_KV_HEADS)
        put_heads(vs_ref, seg(A_SV, 128), SWA_KV_HEADS)
        put_heads_t(kst_ref, seg_t(A_SK, 128), SWA_KV_HEADS, False)
        put_heads_t(vsta_ref, seg_t(A_SV, 128), SWA_KV_HEADS, True)
        sz_ref[...] = seg(A_SZ, 512)
        fft_ref[...] = seg(A_FF, 128).T[:FOX_HEADS, :]

    def heads(nh):
        return pl.BlockSpec((nh, tm, HEAD_DIM), lambda i: (0, i, 0))

    def feat(nh, rows):
        return pl.BlockSpec((nh, rows, tm), lambda i: (0, 0, i))

    wide = pl.BlockSpec((tm, 512), lambda i: (i, 0))
    return pl.pallas_call(
        body,
        name="proj_fwd",
        grid=(s_len // tm,),
        in_specs=[pl.BlockSpec((tm, D_MODEL), lambda i: (i, 0)), pl.BlockSpec((A_W, D_MODEL), lambda i: (0, 0))],
        out_specs=[feat(8, HEAD_DIM), heads(8), heads(8), wide, feat(8, HEAD_DIM), heads(2), heads(2), wide,
                   pl.BlockSpec((FOX_HEADS, tm), lambda i: (0, i)),
                   feat(FOX_HEADS, 2 * HEAD_DIM), feat(2, HEAD_DIM), feat(2, 2 * HEAD_DIM), feat(8, HEAD_DIM)],
        out_shape=[_sds((8, HEAD_DIM, s_len), bf16)] + [_sds((8, s_len, HEAD_DIM), bf16)] * 2
                  + [_sds((s_len, 512), f32), _sds((8, HEAD_DIM, s_len), bf16),
                     _sds((2, s_len, HEAD_DIM), bf16), _sds((2, s_len, HEAD_DIM), bf16), _sds((s_len, 512), f32),
                     _sds((FOX_HEADS, s_len), f32), _sds((FOX_HEADS, 2 * HEAD_DIM, s_len), bf16),
                     _sds((2, HEAD_DIM, s_len), bf16), _sds((2, 2 * HEAD_DIM, s_len), bf16),
                     _sds((8, HEAD_DIM, s_len), bf16)],
        compiler_params=_params(("arbitrary",)),
    )(x2, w_t)


AUG = 2 * HEAD_DIM


def _augment_call(q_t, k, k_t, cum_row, tm):
    nh, s_len, _ = k.shape
    per_step = tm // FOX_T

    def body(qt_ref, k_ref, kt_ref, c_ref, qat_ref, ka_ref, kat_ref, st_ref):
        c = c_ref[0]
        hi = c.astype(bf16).astype(f32)
        r1 = c - hi
        mid = r1.astype(bf16).astype(f32)
        lo = (r1 - mid).astype(bf16).astype(f32)
        row = lax.broadcasted_iota(jnp.int32, (HEAD_DIM, tm), 0)
        q_tail = jnp.where(row == 0, hi, jnp.where(row == 1, mid, jnp.where(row == 2, lo,
                           jnp.where(row < 6, 1.0, 0.0))))
        k_tail = jnp.where(row < 3, 1.0, jnp.where(row == 3, -hi, jnp.where(row == 4, -mid,
                           jnp.where(row == 5, -lo, 0.0))))
        qat_ref[0, 0:HEAD_DIM, :] = qt_ref[0]
        qat_ref[0, HEAD_DIM:AUG, :] = q_tail.astype(bf16)
        ka_ref[0] = jnp.concatenate([k_ref[0], k_tail.T.astype(bf16)], axis=1)
        kat_ref[0, 0:HEAD_DIM, :] = kt_ref[0]
        kat_ref[0, HEAD_DIM:AUG, :] = k_tail.astype(bf16)
        qt = qt_ref[0].astype(f32)
        kt = kt_ref[0].astype(f32)
        qn2 = jnp.sum(qt * qt, axis=0, keepdims=True)
        kn2 = jnp.sum(kt * kt, axis=0, keepdims=True)
        sd = jnp.sum(qt * kt, axis=0, keepdims=True)
        srow = lax.broadcasted_iota(jnp.int32, (8, LANES), 0)
        for part in range(per_step):
            sl = slice(part * FOX_T, (part + 1) * FOX_T)
            vals = [jnp.sqrt(jnp.max(qn2[:, sl], axis=1, keepdims=True)),
                    jnp.sqrt(jnp.max(kn2[:, sl], axis=1, keepdims=True)),
                    jnp.min(sd[:, sl], axis=1, keepdims=True),
                    jnp.max(c[:, sl], axis=1, keepdims=True), jnp.min(c[:, sl], axis=1, keepdims=True)]
            out = jnp.zeros((8, LANES), f32)
            for r, val in enumerate(vals):
                out = jnp.where(srow == r, val, out)
            st_ref[0, part] = out

    tile = pl.BlockSpec((1, tm, HEAD_DIM), lambda h, i: (h, i, 0))
    tile_t = pl.BlockSpec((1, HEAD_DIM, tm), lambda h, i: (h, 0, i))
    return pl.pallas_call(
        body,
        name="fox_augment",
        grid=(nh, s_len // tm),
        in_specs=[tile_t, tile, tile_t, pl.BlockSpec((1, 1, tm), lambda h, i: (h, 0, i))],
        out_specs=[pl.BlockSpec((1, AUG, tm), lambda h, i: (h, 0, i)),
                   pl.BlockSpec((1, tm, AUG), lambda h, i: (h, i, 0)),
                   pl.BlockSpec((1, AUG, tm), lambda h, i: (h, 0, i)),
                   pl.BlockSpec((1, per_step, 8, LANES), lambda h, i: (h, i, 0, 0))],
        out_shape=[_sds((nh, AUG, s_len), bf16), _sds((nh, s_len, AUG), bf16), _sds((nh, AUG, s_len), bf16),
                   _sds((nh, s_len // FOX_T, 8, LANES), f32)],
        compiler_params=_params(("arbitrary", "arbitrary")),
    )(q_t, k, k_t, cum_row)


EXP_ZERO_GAP = 110.0


def _fox_prune_tables(stats):
    s = stats[:, :, :, 0]
    qn, kn, sd, cmx, cmn = (s[:, :, r] for r in range(5))
    nt = s.shape[1]
    bound = qn[:, :, None] * kn[:, None, :] + (cmx[:, :, None] - cmn[:, None, :])
    margin = 2.0 + 1e-5 * (jnp.abs(cmx)[:, :, None] + jnp.abs(cmn)[:, None, :])
    qi = lax.broadcasted_iota(jnp.int32, (nt, nt), 0)
    kj = lax.broadcasted_iota(jnp.int32, (nt, nt), 1)
    skip = (bound + margin < sd[:, :, None] - EXP_ZERO_GAP) & (kj < qi)[None]
    first = jnp.sum(jnp.cumprod(skip.astype(jnp.int32), axis=2), axis=2)
    tiles = lax.broadcasted_iota(jnp.int32, (1, nt), 1)
    cnt = tiles - first
    ends = jnp.cumsum(cnt, axis=1)
    off = ends - cnt
    kmax = nt * (nt - 1) // 2
    k = lax.broadcasted_iota(jnp.int32, (1, kmax), 1)
    pair_q = jnp.minimum(jnp.sum((ends[:, None, :] <= k[:, :, None]).astype(jnp.int32), axis=2), nt - 1)
    hit = pair_q[:, :, None] == tiles[:, None, :]
    first_k = jnp.sum(jnp.where(hit, first[:, None, :], 0), axis=2)
    off_k = jnp.sum(jnp.where(hit, off[:, None, :], 0), axis=2)
    pair_k = jnp.clip(first_k + k - off_k, 0, nt - 1)
    return (ends[:, nt - 1].astype(jnp.int32), pair_q.reshape(-1).astype(jnp.int32),
            pair_k.reshape(-1).astype(jnp.int32))


CUM_CHUNK = 512


def _cum_call(fft, bf_col):
    s_len = fft.shape[1]
    ch = CUM_CHUNK

    def body(f_ref, b_ref, cum_ref, sg_ref):
        r = lax.broadcasted_iota(jnp.int32, (ch, ch), 0)
        c = lax.broadcasted_iota(jnp.int32, (ch, ch), 1)
        upper = (r <= c).astype(f32)
        carry = jnp.zeros((FOX_HEADS, 1), f32)
        for n in range(s_len // ch):
            z = f_ref[:, n * ch:(n + 1) * ch] + b_ref[...]
            logf = jnp.minimum(z, 0.0) - jnp.log1p(jnp.exp(-jnp.abs(z)))
            sg_ref[:, n * ch:(n + 1) * ch] = 1.0 / (1.0 + jnp.exp(z))
            cs = jnp.dot(logf, upper, precision=HIGHEST, preferred_element_type=f32) + carry
            cum_ref[:, n * ch:(n + 1) * ch] = cs
            carry = cs[:, ch - 1:ch]

    return pl.pallas_call(
        body,
        name="fox_cum_fwd",
        out_shape=[_sds((FOX_HEADS, s_len), f32)] * 2,
        compiler_params=_params(),
    )(fft, bf_col)


def _cum_bwd_call(dcq, dck, sg):
    s_len = sg.shape[1]
    ch = CUM_CHUNK
    nch = s_len // ch

    def body(q_ref, k_ref, sg_ref, dff_ref, dbf_ref):
        r = lax.broadcasted_iota(jnp.int32, (ch, ch), 0)
        c = lax.broadcasted_iota(jnp.int32, (ch, ch), 1)
        lower = (r >= c).astype(f32)
        dff_ref[...] = jnp.zeros_like(dff_ref)
        carry = jnp.zeros((FOX_HEADS, 1), f32)
        total = jnp.zeros((FOX_HEADS, 1), f32)
        for n in reversed(range(nch)):
            sl = slice(n * ch, (n + 1) * ch)
            dcum = q_ref[:, sl] - k_ref[:, sl]
            rs = jnp.dot(dcum, lower, precision=HIGHEST, preferred_element_type=f32) + carry
            carry = rs[:, 0:1]
            dff = rs * sg_ref[:, sl]
            dff_ref[0:FOX_HEADS, sl] = dff
            total = total + jnp.sum(dff, axis=1, keepdims=True)
        dbf_ref[...] = jnp.broadcast_to(total, (FOX_HEADS, 128))

    return pl.pallas_call(
        body,
        name="fox_cum_bwd",
        out_shape=[_sds((128, s_len), f32), _sds((FOX_HEADS, 128), f32)],
        compiler_params=_params(),
    )(dcq, dck, sg)


FOX_T = 512
LANES = 128


def _causal_keep(t):
    return lax.broadcasted_iota(jnp.int32, (t, t), 0) <= lax.broadcasted_iota(jnp.int32, (t, t), 1)


def _tile_cols(i, t):
    return pl.ds(pl.multiple_of(i * t, t), t)


def _fox_pair(n, nt, kmax, h, pq_ref, pk_ref):
    k = h * kmax + jnp.maximum(n - nt, 0)
    return jnp.where(n < nt, n, pq_ref[k]), jnp.where(n < nt, n, pk_ref[k])


def _fox_fwd_call(qat, ka, vat, npairs, pair_q, pair_k):
    nh, s_len, _ = ka.shape
    t = FOX_T
    nt = s_len // t
    kmax = nt * (nt - 1) // 2
    assert nt >= 2 and nt % 2 == 0

    def body(np_ref, pq_ref, pk_ref, qat_ref, ka_ref, vat_ref, o_ref, lse_ref, s0, s1, p0, p1, a0, a1, m_all, acc_all):
        h = pl.program_id(0)
        extra = np_ref[h]
        total = nt + extra
        m_all[...] = jnp.full(m_all.shape, NEG_INF, f32)
        acc_all[...] = jnp.zeros(acc_all.shape, f32)
        bufs = ((s0, p0, a0), (s1, p1, a1))

        def pair(n):
            return _fox_pair(n, nt, kmax, h, pq_ref, pk_ref)

        def scores(n, b, masked):
            i, j = pair(n)
            st = jnp.dot(ka_ref[0, _tile_cols(j, t), :], qat_ref[0, :, _tile_cols(i, t)], preferred_element_type=f32)
            if masked:
                st = jnp.where(_causal_keep(t), st, NEG_INF)
            bufs[b][0][...] = st

        def softmax(n, b):
            i, _ = pair(n)
            s_ref, p_ref, a_ref = bufs[b]
            for c in range(t // LANES):
                cols = slice(c * LANES, (c + 1) * LANES)
                mcols = pl.ds(pl.multiple_of(i * t + c * LANES, LANES), LANES)
                m_old = m_all[:, mcols]
                m_new = jnp.maximum(m_old, jnp.max(s_ref[:, cols], axis=0, keepdims=True))
                m_all[:, mcols] = m_new
                a_ref[:, cols] = jnp.exp(m_old - m_new)
                p_ref[:, cols] = jnp.exp(s_ref[:, cols] - m_new).astype(bf16)

        def accum(n, b):
            i, j = pair(n)
            cols = _tile_cols(i, t)
            acc_all[:, cols] = bufs[b][2][...] * acc_all[:, cols] + jnp.dot(
                vat_ref[0, :, _tile_cols(j, t)], bufs[b][1][...], preferred_element_type=f32)

        def step(n, b, masked):
            accum(n - 2, b)
            softmax(n - 1, 1 - b)
            scores(n, b, masked)

        scores(0, 0, True)
        scores(1, 1, True)
        softmax(0, 0)

        def diag_steps(d, _):
            n = 2 + 2 * d
            step(n, 0, True)
            step(n + 1, 1, True)
            return 0

        lax.fori_loop(0, (nt - 2) // 2, diag_steps, 0)

        def off_steps(d, _):
            n = nt + 2 * d
            step(n, 0, False)
            step(n + 1, 1, False)
            return 0

        lax.fori_loop(0, extra // 2, off_steps, 0)

        @pl.when(extra % 2 == 1)
        def _():
            step(total - 1, 0, False)
            softmax(total - 1, 0)
            accum(total - 2, 1)
            accum(total - 1, 0)

        @pl.when(extra % 2 == 0)
        def _():
            softmax(total - 1, 1)
            accum(total - 2, 0)
            accum(total - 1, 1)

        l = acc_all[HEAD_DIM:HEAD_DIM + 1, :]
        o_ref[0] = acc_all[0:HEAD_DIM, :] / l
        lse_ref[0] = m_all[...] + jnp.log(l)

    smem = pl.BlockSpec(memory_space=pltpu.SMEM)
    return pl.pallas_call(
        body,
        name="fox_fwd",
        grid=(nh,),
        in_specs=[smem, smem, smem,
                  pl.BlockSpec((1, AUG, s_len), lambda h: (h, 0, 0)),
                  pl.BlockSpec((1, s_len, AUG), lambda h: (h, 0, 0)),
                  pl.BlockSpec((1, AUG, s_len), lambda h: (h, 0, 0))],
        out_specs=[pl.BlockSpec((1, HEAD_DIM, s_len), lambda h: (h, 0, 0)),
                   pl.BlockSpec((1, 1, s_len), lambda h: (h, 0, 0))],
        out_shape=[_sds((nh, HEAD_DIM, s_len), f32), _sds((nh, 1, s_len), f32)],
        scratch_shapes=[pltpu.VMEM((t, t), f32), pltpu.VMEM((t, t), f32), pltpu.VMEM((t, t), bf16),
                        pltpu.VMEM((t, t), bf16), pltpu.VMEM((1, t), f32), pltpu.VMEM((1, t), f32),
                        pltpu.VMEM((1, s_len), f32), pltpu.VMEM((AUG, s_len), f32)],
        compiler_params=_params(("arbitrary",)),
    )(npairs, pair_q, pair_k, qat, ka, vat)


SWA_TS = 512


SWA_W = SWA_GROUP * BLOCK


def _swa_bias_call(rel_bias, bucket_t):
    def body(rb_ref, bk_ref, b_ref, b0_ref):
        bk = bk_ref[...]
        row = lax.broadcasted_iota(jnp.int32, (2 * BLOCK, BLOCK), 0)
        for h in range(SWA_HEADS):
            acc = jnp.full((2 * BLOCK, BLOCK), NEG_INF, f32)
            for b in range(NUM_BUCKETS):
                acc = jnp.where(bk == b, rb_ref[b, h], acc)
            g, hh = divmod(h, SWA_GROUP)
            b_ref[g, :, hh * BLOCK:(hh + 1) * BLOCK] = acc
            b0_ref[g, :, hh * BLOCK:(hh + 1) * BLOCK] = jnp.where(row < BLOCK, NEG_INF, acc)

    return pl.pallas_call(
        body,
        name="swa_bias",
        in_specs=[pl.BlockSpec(memory_space=pltpu.SMEM), pl.BlockSpec(memory_space=pltpu.VMEM)],
        out_shape=[_sds((SWA_KV_HEADS, 2 * BLOCK, SWA_W), f32)] * 2,
        compiler_params=_params(),
    )(rel_bias, bucket_t)


def _swa_bias_bwd_call(dbias, bucket_t):
    def body(d_ref, bk_ref, o_ref):
        bk = bk_ref[...]
        row = lax.broadcasted_iota(jnp.int32, (NUM_BUCKETS, 128), 0)
        col = lax.broadcasted_iota(jnp.int32, (NUM_BUCKETS, 128), 1)
        out = jnp.zeros((NUM_BUCKETS, 128), f32)
        for h in range(SWA_HEADS):
            g, hh = divmod(h, SWA_GROUP)
            d = d_ref[g, :, hh * BLOCK:(hh + 1) * BLOCK]
            for b in range(NUM_BUCKETS):
                val = jnp.sum(jnp.sum(jnp.where(bk == b, d, 0.0), axis=1, keepdims=True), axis=0, keepdims=True)
                out = jnp.where((row == b) & (col == h), val, out)
        o_ref[...] = out

    return pl.pallas_call(
        body,
        name="swa_bias_bwd",
        out_shape=_sds((NUM_BUCKETS, 128), f32),
        compiler_params=_params(),
    )(dbias, bucket_t)


def _sink_row(sink_ref, g):
    return jnp.concatenate([jnp.full((1, BLOCK), sink_ref[g * SWA_GROUP + hh], f32) for hh in range(SWA_GROUP)], axis=1)


def _group_lanes(ref, g, cols):
    return jnp.concatenate([ref[g * SWA_GROUP + hh, :, cols] for hh in range(SWA_GROUP)], axis=1)


def _swa_fwd_call(qt, k, vta, bias_t, bias0_t, sink):
    s_len = qt.shape[2]
    ts = SWA_TS
    nb = ts // BLOCK

    def body(qt_ref, kc_ref, kp_ref, vc_ref, vp_ref, b_ref, b0_ref, sink_ref, o_ref, lse_ref):
        first = pl.program_id(0) == 0
        kall = [jnp.concatenate([kp_ref[g], kc_ref[g]], axis=0) for g in range(SWA_KV_HEADS)]
        vall = [jnp.concatenate([vp_ref[g], vc_ref[g]], axis=1) for g in range(SWA_KV_HEADS)]
        sinks = [_sink_row(sink_ref, g) for g in range(SWA_KV_HEADS)]
        items = [(g, b) for g in range(SWA_KV_HEADS) for b in range(nb)]

        def scores(g, b):
            qg = _group_lanes(qt_ref, g, slice(b * BLOCK, (b + 1) * BLOCK))
            bias_b = b_ref[g]
            if b == 0:
                bias_b = jnp.where(first, b0_ref[g], bias_b)
            return jnp.dot(kall[g][b * BLOCK:(b + 2) * BLOCK], qg, preferred_element_type=f32) + bias_b

        def finish(g, b, st):
            m = jnp.maximum(jnp.max(st, axis=0, keepdims=True), sinks[g])
            pt = jnp.exp(st - m)
            acc = jnp.dot(vall[g][:, b * BLOCK:(b + 2) * BLOCK], pt.astype(bf16), preferred_element_type=f32)
            l = acc[HEAD_DIM:HEAD_DIM + 1, :] + jnp.exp(sinks[g] - m)
            return acc[0:HEAD_DIM, :] / l, m + jnp.log(l)

        outs, lses = {}, {}
        st_next = scores(*items[0])
        for idx, (g, b) in enumerate(items):
            st = st_next
            if idx + 1 < len(items):
                st_next = scores(*items[idx + 1])
            outs[g, b], lses[g, b] = finish(g, b, st)
        for g in range(SWA_KV_HEADS):
            for hh in range(SWA_GROUP):
                lanes = slice(hh * BLOCK, (hh + 1) * BLOCK)
                o_ref[g * SWA_GROUP + hh] = jnp.concatenate([outs[g, b][:, lanes] for b in range(nb)], axis=1)
                lse_ref[g * SWA_GROUP + hh] = jnp.concatenate([lses[g, b][:, lanes] for b in range(nb)], axis=1)

    def prev_blk(n):
        return jnp.maximum(n * nb - 1, 0)

    bspec = pl.BlockSpec((SWA_KV_HEADS, 2 * BLOCK, SWA_W), lambda n: (0, 0, 0))
    return pl.pallas_call(
        body,
        name="swa_fwd",
        grid=(s_len // ts,),
        in_specs=[pl.BlockSpec((SWA_HEADS, HEAD_DIM, ts), lambda n: (0, 0, n)),
                  pl.BlockSpec((SWA_KV_HEADS, ts, HEAD_DIM), lambda n: (0, n, 0)),
                  pl.BlockSpec((SWA_KV_HEADS, BLOCK, HEAD_DIM), lambda n: (0, prev_blk(n), 0)),
                  pl.BlockSpec((SWA_KV_HEADS, AUG, ts), lambda n: (0, 0, n)),
                  pl.BlockSpec((SWA_KV_HEADS, AUG, BLOCK), lambda n: (0, 0, prev_blk(n))),
                  bspec, bspec, pl.BlockSpec(memory_space=pltpu.SMEM)],
        out_specs=[pl.BlockSpec((SWA_HEADS, HEAD_DIM, ts), lambda n: (0, 0, n)),
                   pl.BlockSpec((SWA_HEADS, 1, ts), lambda n: (0, 0, n))],
        out_shape=[_sds((SWA_HEADS, HEAD_DIM, s_len), f32), _sds((SWA_HEADS, 1, s_len), f32)],
        compiler_params=_params(("arbitrary",)),
    )(qt, k, k, vta, vta, bias_t, bias0_t, sink)


def _head_selector():
    sel = np.zeros((512, 128), np.float32)
    for h in range(8):
        sel[h * HEAD_DIM:(h + 1) * HEAD_DIM, h] = 1.0
    return sel


def _post_call(of, fz, osw, sz, x2, tgt, wo, ln_g, ln_b, sel, tm):
    s_len = x2.shape[0]

    def body(of_ref, fz_ref, os_ref, sz_ref, x_ref, t_ref, wo_ref, g_ref, b_ref, sel_ref,
             dh_ref, dof_ref, dfz_ref, dos_ref, dsz_ref, dlf_ref, dls_ref, dwo_ref, dg_ref, db_ref, loss_ref):
        n = pl.program_id(0)

        @pl.when(n == 0)
        def _():
            dwo_ref[...] = jnp.zeros_like(dwo_ref)
            dg_ref[...] = jnp.zeros_like(dg_ref)
            db_ref[...] = jnp.zeros_like(db_ref)
            loss_ref[...] = jnp.zeros_like(loss_ref)

        o_f = of_ref[...].T
        o_s = os_ref[...].T
        fz = fz_ref[...]
        sz = sz_ref[...]
        sg_f = jax.nn.sigmoid(fz)
        sg_s = jax.nn.sigmoid(sz)
        silu_f = fz * sg_f
        silu_s = sz * sg_s
        mixed = jnp.concatenate([o_f * silu_f, o_s * silu_s], axis=1).astype(bf16)
        y = jnp.dot(mixed, wo_ref[...], preferred_element_type=f32)
        h = ALPHA * x_ref[...] + y
        mu = jnp.mean(h, axis=1, keepdims=True)
        hc = h - mu
        var = jnp.mean(hc * hc, axis=1, keepdims=True)
        rstd = lax.rsqrt(var + LN_EPS)
        xhat = hc * rstd
        gam = g_ref[...]
        out = xhat * gam + b_ref[...]
        err = out - t_ref[...]
        tok_loss = jnp.mean(err * err, axis=1, keepdims=True)
        loss_ref[...] += 0.5 * jnp.sum(tok_loss, axis=0, keepdims=True)
        dout = err * (1.0 / D_MODEL)
        dg_ref[...] += jnp.sum(dout * xhat, axis=0, keepdims=True)
        db_ref[...] += jnp.sum(dout, axis=0, keepdims=True)
        dxh = dout * gam
        m1 = jnp.mean(dxh, axis=1, keepdims=True)
        m2 = jnp.mean(dxh * xhat, axis=1, keepdims=True)
        dh = rstd * (dxh - m1 - xhat * m2)
        dh_ref[...] = dh
        dyb = dh.astype(bf16)
        dwo_ref[...] += lax.dot_general(mixed, dyb, TN, preferred_element_type=f32)
        dmix = lax.dot_general(dyb, wo_ref[...], NT, preferred_element_type=f32)
        dm_f = dmix[:, :512]
        dm_s = dmix[:, 512:]
        do_f = dm_f * silu_f
        do_s = dm_s * silu_s
        dfz_ref[...] = (dm_f * o_f * (sg_f * (1.0 + fz * (1.0 - sg_f)))).astype(bf16)
        dsz_ref[...] = (dm_s * o_s * (sg_s * (1.0 + sz * (1.0 - sg_s)))).astype(bf16)
        dof_ref[...] = do_f.T.astype(bf16)
        dos_ref[...] = do_s.T.astype(bf16)
        sel_m = sel_ref[...]

        def head_sums(prod):
            hi = prod.astype(bf16)
            lo = (prod - hi.astype(f32)).astype(bf16)
            return (jnp.dot(hi, sel_m, preferred_element_type=f32) + jnp.dot(lo, sel_m, preferred_element_type=f32))

        dl_f = head_sums(do_f * o_f)
        dl_s = head_sums(do_s * o_s)
        dlf_ref[...] = dl_f.T[:FOX_HEADS, :]
        dls_ref[...] = dl_s.T[:SWA_HEADS, :]

    feat = pl.BlockSpec((512, tm), lambda n: (0, n))
    rows8 = pl.BlockSpec((8, tm), lambda n: (0, n))
    half = pl.BlockSpec((tm, 512), lambda n: (n, 0))
    fullw = pl.BlockSpec((tm, D_MODEL), lambda n: (n, 0))
    vec = pl.BlockSpec((1, D_MODEL), lambda n: (0, 0))
    return pl.pallas_call(
        body,
        name="post_fwd_bwd",
        grid=(s_len // tm,),
        in_specs=[feat, half, feat, half, fullw, fullw,
                  pl.BlockSpec((D_MODEL, D_MODEL), lambda n: (0, 0)), vec, vec,
                  pl.BlockSpec((512, 128), lambda n: (0, 0))],
        out_specs=[fullw, feat, half, feat, half, rows8, rows8,
                   pl.BlockSpec((D_MODEL, D_MODEL), lambda n: (0, 0)), vec, vec,
                   pl.BlockSpec((1, 1), lambda n: (0, 0))],
        out_shape=[_sds((s_len, D_MODEL), f32), _sds((512, s_len), bf16), _sds((s_len, 512), bf16),
                   _sds((512, s_len), bf16), _sds((s_len, 512), bf16),
                   _sds((FOX_HEADS, s_len), f32), _sds((SWA_HEADS, s_len), f32),
                   _sds((D_MODEL, D_MODEL), f32), _sds((1, D_MODEL), f32), _sds((1, D_MODEL), f32),
                   _sds((1, 1), f32)],
        compiler_params=_params(("arbitrary",)),
    )(of, fz, osw, sz, x2, tgt, wo, ln_g, ln_b, sel)


def _fox_bwd_call(ka, kat, v, qat, dot, lse_row, dl_row, npairs, pair_q, pair_k):
    nh, s_len, _ = ka.shape
    t = FOX_T
    nt = s_len // t
    kmax = nt * (nt - 1) // 2
    assert nt >= 2 and nt % 2 == 0
    ck_slot = HEAD_DIM + 3
    cq_slot = HEAD_DIM

    def body(np_ref, pq_ref, pk_ref, ka_ref, kat_ref, v_ref, qat_ref, dot_ref, lse_ref, dl_ref,
             dq_ref, dk_ref, dv_ref, dcq_ref, dck_ref, dqt_all, dkat_all, dvt_all, p0, p1, ds0, ds1):
        h = pl.program_id(0)
        extra = np_ref[h]
        total = nt + extra
        dqt_all[...] = jnp.zeros(dqt_all.shape, f32)
        dkat_all[...] = jnp.zeros(dkat_all.shape, f32)
        dvt_all[...] = jnp.zeros(dvt_all.shape, f32)
        pbuf, dsbuf = (p0, p1), (ds0, ds1)

        def pair(n):
            return _fox_pair(n, nt, kmax, h, pq_ref, pk_ref)

        def probs(n, b, masked):
            i, j = pair(n)
            qc, kr = _tile_cols(i, t), _tile_cols(j, t)
            st = jnp.dot(ka_ref[0, kr, :], qat_ref[0, :, qc], preferred_element_type=f32)
            dpt = jnp.dot(v_ref[0, kr, :], dot_ref[0, :, qc], preferred_element_type=f32)
            if masked:
                st = jnp.where(_causal_keep(t), st, NEG_INF)
            pt = jnp.exp(st - lse_ref[0, :, qc])
            pbuf[b][...] = pt.astype(bf16)
            dsbuf[b][...] = (pt * (dpt - dl_ref[0, :, qc])).astype(bf16)

        def grads(n, b):
            i, j = pair(n)
            qc, kc = _tile_cols(i, t), _tile_cols(j, t)
            dvt_all[:, kc] += lax.dot_general(dot_ref[0, :, qc], pbuf[b][...], NT, preferred_element_type=f32)
            dkat_all[:, kc] += lax.dot_general(qat_ref[0, :, qc], dsbuf[b][...], NT, preferred_element_type=f32)
            dqt_all[:, qc] += jnp.dot(kat_ref[0, :, kc], dsbuf[b][...], preferred_element_type=f32)

        def step(n, b, masked):
            probs(n, b, masked)
            grads(n - 1, 1 - b)

        probs(0, 0, True)
        step(1, 1, True)

        def diag_steps(d, _):
            n = 2 + 2 * d
            step(n, 0, True)
            step(n + 1, 1, True)
            return 0

        lax.fori_loop(0, (nt - 2) // 2, diag_steps, 0)

        def off_steps(d, _):
            n = nt + 2 * d
            step(n, 0, False)
            step(n + 1, 1, False)
            return 0

        lax.fori_loop(0, extra // 2, off_steps, 0)

        @pl.when(extra % 2 == 1)
        def _():
            step(total - 1, 0, False)
            grads(total - 1, 0)

        @pl.when(extra % 2 == 0)
        def _():
            grads(total - 1, 1)

        dq_ref[0] = (dqt_all[0:HEAD_DIM, :] * SCALE).astype(bf16)
        dk_ref[0] = dkat_all[0:HEAD_DIM, :].astype(bf16)
        dv_ref[0] = dvt_all[...].astype(bf16)
        dcq_ref[0] = dqt_all[cq_slot:cq_slot + 1, :]
        dck_ref[0] = dkat_all[ck_slot:ck_slot + 1, :]

    smem = pl.BlockSpec(memory_space=pltpu.SMEM)
    rows = pl.BlockSpec((1, s_len, AUG), lambda h: (h, 0, 0))
    feat = pl.BlockSpec((1, AUG, s_len), lambda h: (h, 0, 0))
    feat64 = pl.BlockSpec((1, HEAD_DIM, s_len), lambda h: (h, 0, 0))
    rowv = pl.BlockSpec((1, 1, s_len), lambda h: (h, 0, 0))
    return pl.pallas_call(
        body,
        name="fox_bwd",
        grid=(nh,),
        in_specs=[smem, smem, smem, rows, feat, pl.BlockSpec((1, s_len, HEAD_DIM), lambda h: (h, 0, 0)), feat, feat64,
                  rowv, rowv],
        out_specs=[feat64, feat64, feat64, rowv, rowv],
        out_shape=[_sds((nh, HEAD_DIM, s_len), bf16)] * 3 + [_sds((nh, 1, s_len), f32)] * 2,
        scratch_shapes=[pltpu.VMEM((AUG, s_len), f32), pltpu.VMEM((AUG, s_len), f32), pltpu.VMEM((HEAD_DIM, s_len), f32)]
                       + [pltpu.VMEM((t, t), bf16)] * 4,
        compiler_params=_params(("arbitrary",)),
    )(npairs, pair_q, pair_k, ka, kat, v, qat, dot, lse_row, dl_row)


def _swa_bwd_call(qt, k, kt, v, dot, lse, dl, bias_t, bias0_t, sink):
    s_len = qt.shape[2]
    ts = SWA_TS
    nb = ts // BLOCK
    nsteps = s_len // ts

    def body(qt_ref, kc_ref, kp_ref, ktc_ref, ktp_ref, vc_ref, vp_ref, dot_ref, lse_ref, dl_ref, b_ref, b0_ref,
             sink_ref, dq_ref, dk_ref, dv_ref, dbias_ref, dsink_ref, dk_s, dv_s, tail_k, tail_v, sk_s):
        n = pl.program_id(0)

        @pl.when(n == 0)
        def _():
            dbias_ref[...] = jnp.zeros_like(dbias_ref)
            sk_s[...] = jnp.zeros_like(sk_s)

        @pl.when(n < nsteps)
        def _():
            first = n == 0
            dk_s[...] = jnp.zeros_like(dk_s)
            dv_s[...] = jnp.zeros_like(dv_s)
            groups = range(SWA_KV_HEADS)
            kall = [jnp.concatenate([kp_ref[g], kc_ref[g]], axis=0) for g in groups]
            vall = [jnp.concatenate([vp_ref[g], vc_ref[g]], axis=0) for g in groups]
            ktall = [jnp.concatenate([ktp_ref[g], ktc_ref[g]], axis=1) for g in groups]
            sinks = [_sink_row(sink_ref, g) for g in groups]
            items = [(g, b) for g in groups for b in range(nb)]

            def products(g, b):
                cols = slice(b * BLOCK, (b + 1) * BLOCK)
                win = slice(b * BLOCK, (b + 2) * BLOCK)
                qg = _group_lanes(qt_ref, g, cols)
                dog = _group_lanes(dot_ref, g, cols)
                bias_b = b_ref[g]
                if b == 0:
                    bias_b = jnp.where(first, b0_ref[g], bias_b)
                st = jnp.dot(kall[g][win], qg, preferred_element_type=f32) + bias_b
                dpt = jnp.dot(vall[g][win], dog, preferred_element_type=f32)
                return qg, dog, st, dpt

            def finish(g, b, qg, dog, st, dpt):
                cols = slice(b * BLOCK, (b + 1) * BLOCK)
                win = slice(b * BLOCK, (b + 2) * BLOCK)
                lse_r = _group_lanes(lse_ref, g, cols)
                dl_r = _group_lanes(dl_ref, g, cols)
                pt = jnp.exp(st - lse_r)
                dst = pt * (dpt - dl_r)
                dsb = dst.astype(bf16)
                dk_s[g, :, win] += lax.dot_general(qg, dsb, NT, preferred_element_type=f32)
                dv_s[g, :, win] += lax.dot_general(dog, pt.astype(bf16), NT, preferred_element_type=f32)
                dqg = jnp.dot(ktall[g][:, win], dsb, preferred_element_type=f32) * SCALE
                return dqg, dst, -jnp.exp(sinks[g] - lse_r) * dl_r

            dqs, dsts, sks = {}, {}, {}
            nxt = products(*items[0])
            for idx, (g, b) in enumerate(items):
                cur = nxt
                if idx + 1 < len(items):
                    nxt = products(*items[idx + 1])
                dqs[g, b], dsts[g, b], sks[g, b] = finish(g, b, *cur)
            for g in groups:
                dbias_ref[g] += functools.reduce(lambda a, c: a + c, [dsts[g, b] for b in range(nb)])
                sk_s[g] += functools.reduce(lambda a, c: a + c, [sks[g, b] for b in range(nb)])
                for hh in range(SWA_GROUP):
                    lanes = slice(hh * BLOCK, (hh + 1) * BLOCK)
                    dq_ref[g * SWA_GROUP + hh] = jnp.concatenate(
                        [dqs[g, b][:, lanes] for b in range(nb)], axis=1).astype(bf16)

        @pl.when(n > 0)
        def _():
            last = slice(ts - BLOCK, ts)
            for g in range(SWA_KV_HEADS):
                add_k = jnp.where(n < nsteps, dk_s[g, :, 0:BLOCK], 0.0)
                add_v = jnp.where(n < nsteps, dv_s[g, :, 0:BLOCK], 0.0)
                dk_ref[g, :, 0:ts - BLOCK] = tail_k[g, :, 0:ts - BLOCK].astype(bf16)
                dv_ref[g, :, 0:ts - BLOCK] = tail_v[g, :, 0:ts - BLOCK].astype(bf16)
                dk_ref[g, :, last] = (tail_k[g, :, last] + add_k).astype(bf16)
                dv_ref[g, :, last] = (tail_v[g, :, last] + add_v).astype(bf16)

        @pl.when(n < nsteps)
        def _():
            tail_k[...] = dk_s[:, :, BLOCK:]
            tail_v[...] = dv_s[:, :, BLOCK:]

        @pl.when(n == nsteps)
        def _():
            row = lax.broadcasted_iota(jnp.int32, (SWA_HEADS, 128), 0)
            out = jnp.zeros((SWA_HEADS, 128), f32)
            for h in range(SWA_HEADS):
                g, hh = divmod(h, SWA_GROUP)
                val = jnp.sum(sk_s[g, :, hh * BLOCK:(hh + 1) * BLOCK], axis=1, keepdims=True)
                out = jnp.where(row == h, val, out)
            dsink_ref[...] = out

    last_step = nsteps - 1

    def cl(n):
        return jnp.minimum(n, last_step)

    def prev_blk(n):
        return jnp.maximum(cl(n) * nb - 1, 0)

    feat8 = pl.BlockSpec((SWA_HEADS, HEAD_DIM, ts), lambda n: (0, 0, cl(n)))
    rows8 = pl.BlockSpec((SWA_HEADS, 1, ts), lambda n: (0, 0, cl(n)))
    cur = pl.BlockSpec((SWA_KV_HEADS, ts, HEAD_DIM), lambda n: (0, cl(n), 0))
    prev = pl.BlockSpec((SWA_KV_HEADS, BLOCK, HEAD_DIM), lambda n: (0, prev_blk(n), 0))
    curt = pl.BlockSpec((SWA_KV_HEADS, HEAD_DIM, ts), lambda n: (0, 0, cl(n)))
    prevt = pl.BlockSpec((SWA_KV_HEADS, HEAD_DIM, BLOCK), lambda n: (0, 0, prev_blk(n)))
    bspec = pl.BlockSpec((SWA_KV_HEADS, 2 * BLOCK, SWA_W), lambda n: (0, 0, 0))
    kvout = pl.BlockSpec((SWA_KV_HEADS, HEAD_DIM, ts), lambda n: (0, 0, jnp.maximum(n - 1, 0)))
    return pl.pallas_call(
        body,
        name="swa_bwd",
        grid=(nsteps + 1,),
        in_specs=[feat8, cur, prev, curt, prevt, cur, prev, feat8, rows8, rows8, bspec, bspec,
                  pl.BlockSpec(memory_space=pltpu.SMEM)],
        out_specs=[feat8, kvout, kvout, bspec, pl.BlockSpec((SWA_HEADS, 128), lambda n: (0, 0))],
        out_shape=[_sds((SWA_HEADS, HEAD_DIM, s_len), bf16), _sds((SWA_KV_HEADS, HEAD_DIM, s_len), bf16),
                   _sds((SWA_KV_HEADS, HEAD_DIM, s_len), bf16),
                   _sds((SWA_KV_HEADS, 2 * BLOCK, SWA_W), f32), _sds((SWA_HEADS, 128), f32)],
        scratch_shapes=[pltpu.VMEM((SWA_KV_HEADS, HEAD_DIM, ts + BLOCK), f32),
                        pltpu.VMEM((SWA_KV_HEADS, HEAD_DIM, ts + BLOCK), f32),
                        pltpu.VMEM((SWA_KV_HEADS, HEAD_DIM, ts), f32),
                        pltpu.VMEM((SWA_KV_HEADS, HEAD_DIM, ts), f32),
                        pltpu.VMEM((SWA_KV_HEADS, 1, SWA_W), f32)],
        compiler_params=_params(("arbitrary",)),
    )(qt, k, k, kt, kt, v, v, dot, lse, dl, bias_t, bias0_t, sink)


def _dproj_specs(tm):
    half = pl.BlockSpec((tm, 512), lambda i: (i, 0))
    feat = pl.BlockSpec((512, tm), lambda i: (0, i))
    feat_kv = pl.BlockSpec((128, tm), lambda i: (0, i))
    return [feat, feat, feat, half, feat, feat_kv, feat_kv, half, feat_kv]


def _dx_exchange_call(dh, pieces, w_t, bs, tm):
    s_len = dh.shape[0]
    n = len(bs)
    last = s_len // tm - 1

    def body(*refs):
        dh_ref, dqf_ref, dkf_ref, dvf_ref, dfz_ref, dqs_ref, dks_ref, dvs_ref, dsz_ref, dfft_ref, w_ref = refs[:11]
        b_refs = refs[11:11 + n]
        dx_ref = refs[11 + n]
        r_refs = refs[12 + n:12 + 2 * n]
        sems = refs[12 + 2 * n:]
        i = pl.program_id(0)

        @pl.when(i == 0)
        def _():
            _exchange_start(b_refs, r_refs, sems)

        def tr(ref):
            return ref[...].astype(f32).T.astype(bf16)

        dp = jnp.concatenate([tr(dqf_ref), tr(dkf_ref), tr(dvf_ref), dfz_ref[...], tr(dqs_ref), tr(dks_ref),
                              tr(dvs_ref), dsz_ref[...], tr(dfft_ref)], axis=1)
        dx_ref[...] = ALPHA * dh_ref[...] + jnp.dot(dp, w_ref[...], preferred_element_type=f32)

        @pl.when(i == last)
        def _():
            _exchange_wait(b_refs, r_refs, sems)

    fullw = pl.BlockSpec((tm, D_MODEL), lambda i: (i, 0))
    any_spec = pl.BlockSpec(memory_space=pl.ANY)
    out = pl.pallas_call(
        body,
        name="dx_bwd_exchange",
        grid=(s_len // tm,),
        in_specs=[fullw] + _dproj_specs(tm) + [pl.BlockSpec((A_W, D_MODEL), lambda i: (0, 0))] + [any_spec] * n,
        out_specs=[fullw] + [any_spec] * n,
        out_shape=[_sds((s_len, D_MODEL), f32)] + [_sds(b.shape, b.dtype) for b in bs],
        scratch_shapes=[pltpu.SemaphoreType.DMA((7 * n,)), pltpu.SemaphoreType.DMA((7 * n,)),
                        pltpu.SemaphoreType.DMA((n,))],
        compiler_params=_params(("arbitrary",)),
    )(dh, *pieces, w_t, *bs)
    return out[0], out[1:]


DW_STAGE_ROWS = 384


def _dw_call(x2, pieces, tm):
    s_len = x2.shape[0]
    nt = s_len // tm

    def body(x_ref, dqf_ref, dkf_ref, dvf_ref, dfz_ref, dqs_ref, dks_ref, dvs_ref, dsz_ref, dfft_ref, dw_ref,
             acc_ref, stage_ref, sem):
        i = pl.program_id(0)

        @pl.when(i == 0)
        def _():
            acc_ref[...] = jnp.zeros_like(acc_ref)

        xb = x_ref[...].astype(bf16)

        def add_feat(off, lhs):
            acc_ref[off:off + lhs.shape[0], :] += jnp.dot(lhs, xb, preferred_element_type=f32)

        def add_rows(off, piece):
            acc_ref[off:off + piece.shape[1], :] += lax.dot_general(piece, xb, TN, preferred_element_type=f32)

        add_feat(A_FQ, dqf_ref[...])
        add_feat(A_FK, dkf_ref[...])
        add_feat(A_FV, dvf_ref[...])
        add_rows(A_FZ, dfz_ref[...])
        add_feat(A_SQ, dqs_ref[...])
        add_feat(A_SK, dks_ref[...])
        add_feat(A_SV, dvs_ref[...])
        add_rows(A_SZ, dsz_ref[...])
        add_feat(A_FF, dfft_ref[...].astype(bf16))

        @pl.when(i == nt - 1)
        def _():
            for r in range(A_W // DW_STAGE_ROWS):
                rows = slice(r * DW_STAGE_ROWS, (r + 1) * DW_STAGE_ROWS)
                stage_ref[...] = acc_ref[rows, :].astype(bf16)
                cp = pltpu.make_async_copy(stage_ref, dw_ref.at[rows, :], sem)
                cp.start()
                cp.wait()

    return pl.pallas_call(
        body,
        name="dw_in_bwd",
        grid=(nt,),
        in_specs=[pl.BlockSpec((tm, D_MODEL), lambda i: (i, 0))] + _dproj_specs(tm),
        out_specs=pl.BlockSpec(memory_space=pl.ANY),
        out_shape=_sds((A_W, D_MODEL), bf16),
        scratch_shapes=[pltpu.VMEM((A_W, D_MODEL), f32), pltpu.VMEM((DW_STAGE_ROWS, D_MODEL), bf16),
                        pltpu.SemaphoreType.DMA],
        compiler_params=_params(("arbitrary",), VMEM_LIMIT_BIG),
    )(x2, *pieces)


def _adam_call(recv, w, m, v, tc, name):
    rows, cols = w.shape

    def body(r_ref, w_ref, m_ref, v_ref, g_ref, d_ref, mo_ref, vo_ref):
        g = r_ref[0].astype(f32)
        for p in range(1, N_DEV):
            g = g + r_ref[p].astype(f32)
        mn = ADAM_B1 * m_ref[...] + (1.0 - ADAM_B1) * g
        vn = ADAM_B2 * v_ref[...] + (1.0 - ADAM_B2) * (g * g)
        m_hat = mn / (1.0 - ADAM_B1 ** ADAM_STEP)
        v_hat = vn / (1.0 - ADAM_B2 ** ADAM_STEP)
        g_ref[...] = g
        d_ref[...] = -ADAM_LR * (m_hat / (jnp.sqrt(v_hat) + ADAM_EPS) + ADAM_WD * w_ref[...])
        mo_ref[...] = mn
        vo_ref[...] = vn

    blk = pl.BlockSpec((rows, tc), lambda i: (0, i))
    return pl.pallas_call(
        body,
        name=name,
        grid=(cols // tc,),
        in_specs=[pl.BlockSpec((N_DEV, rows, tc), lambda i: (0, 0, i)), blk, blk, blk],
        out_specs=[blk] * 4,
        out_shape=[_sds((rows, cols), f32)] * 4,
        compiler_params=_params(("arbitrary",)),
    )(recv, w, m, v)


def _pad_cols(a, width=128):
    return jnp.pad(a, ((0, 0), (0, width - a.shape[1])))


def _pack_small(ln_g, ln_b, rel, b_f, sink):
    return jnp.concatenate([
        ln_g.reshape(8, 128), ln_b.reshape(8, 128), _pad_cols(rel),
        jnp.pad(_pad_cols(b_f), ((0, 7), (0, 0))), jnp.pad(_pad_cols(sink), ((0, 7), (0, 0)))], axis=0)


def _unpack_small(p):
    return (p[0:8].reshape(1, D_MODEL), p[8:16].reshape(1, D_MODEL), p[16:48, 0:8], p[48:49, 0:8], p[56:57, 0:8])


def kernel(x, w_in, b_f, rel_bias, sink, w_o, ln_g, ln_b, loss_target, m_w_in, m_b_f, m_rel_bias, m_sink, m_w_o, m_ln_g, m_ln_b, v_w_in, v_b_f, v_rel_bias, v_sink, v_w_o, v_ln_g, v_ln_b):
    x2 = x[0]
    tgt = loss_target[0]
    s_len = x2.shape[0]
    shard = w_in.shape[2]

    w_in_t = jnp.transpose(w_in[0])
    g_in, g_o = _gather_call([w_in_t.astype(bf16), w_o[0].astype(bf16)])
    wt_full = g_in.reshape(N_DEV * shard, D_MODEL)
    w_t = jnp.concatenate([wt_full[:O_FF0], wt_full[O_FF1:], wt_full[O_FF0:O_FF1],
                           jnp.zeros((A_W - D_IN, D_MODEL), bf16)], axis=0)
    wo_full = g_o.reshape(D_MODEL, D_MODEL)

    qft, kf, vf, fz, qst, ks, vs, sz, fft, vat, kst, vsta, kft = _proj_call(x2, w_t, 512)
    cum, sgm = _cum_call(fft, b_f.reshape(FOX_HEADS, 1))
    qat, ka, kat, tile_stats = _augment_call(qft, kf, kft, cum.reshape(FOX_HEADS, 1, s_len), 2048)
    npairs, pair_q, pair_k = _fox_prune_tables(tile_stats)
    o_ft, lse_f = _fox_fwd_call(qat, ka, vat, npairs, pair_q, pair_k)
    bucket_t = jnp.asarray(_t5_bucket_table().T)
    bias_t, bias0_t = _swa_bias_call(rel_bias, bucket_t)
    sink_v = sink.reshape(SWA_HEADS)
    o_st, lse_s = _swa_fwd_call(qst, ks, vsta, bias_t, bias0_t, sink_v)

    (dh, do_f, dfz, do_s, dsz, dl_f, dl_s, dwo, dg, db, loss_part) = _post_call(
        o_ft.reshape(FOX_HEADS * HEAD_DIM, s_len), fz, o_st.reshape(SWA_HEADS * HEAD_DIM, s_len), sz, x2, tgt,
        wo_full, ln_g, ln_b, jnp.asarray(_head_selector()).astype(bf16), 256)

    dqf, dkf, dvf, dcq, dck = _fox_bwd_call(ka, kat, vf, qat, do_f.reshape(FOX_HEADS, HEAD_DIM, s_len), lse_f,
                                            dl_f.reshape(FOX_HEADS, 1, s_len), npairs, pair_q, pair_k)
    dqf, dkf, dvf = (a.reshape(FOX_HEADS * HEAD_DIM, s_len) for a in (dqf, dkf, dvf))
    dfft, dbf = _cum_bwd_call(dcq.reshape(FOX_HEADS, s_len), dck.reshape(FOX_HEADS, s_len), sgm)
    dqs, dks, dvs, dbias, dsink = _swa_bwd_call(
        qst, ks, kst, vs, do_s.reshape(SWA_HEADS, HEAD_DIM, s_len), lse_s, dl_s.reshape(SWA_HEADS, 1, s_len),
        bias_t, bias0_t, sink_v)
    dqs = dqs.reshape(SWA_HEADS * HEAD_DIM, s_len)
    dks, dvs = (a.reshape(SWA_KV_HEADS * HEAD_DIM, s_len) for a in (dks, dvs))
    drel = _swa_bias_bwd_call(dbias, bucket_t)

    pieces = (dqf, dkf, dvf, dfz, dqs, dks, dvs, dsz, dfft)
    dw_t = _dw_call(x2, pieces, 1024)

    dwt_full = jnp.concatenate([dw_t[:O_FF0], dw_t[A_FF:A_FF + (O_FF1 - O_FF0)], dw_t[O_FF0:A_FF]], axis=0)
    dw_blocks = dwt_full.reshape(N_DEV, shard, D_MODEL)
    dwo_blocks = dwo.reshape(N_DEV, D_MODEL // N_DEV, D_MODEL).astype(bf16)
    small = _pack_small(dg, db, drel[:, 0:8], dbf[:, 0].reshape(1, 8), dsink[:, 0].reshape(1, 8))
    loss_slot = np.zeros((64, 128), bool)
    loss_slot[49, 0] = True
    small = jnp.where(jnp.asarray(loss_slot), loss_part[0, 0], small)
    small_blocks = jnp.broadcast_to(small[None], (N_DEV,) + small.shape)
    dx, (r_in, r_o, r_small) = _dx_exchange_call(dh, pieces, w_t, [dw_blocks, dwo_blocks, small_blocks], 256)

    win_t = [jnp.transpose(a) for a in _adam_call(
        r_in, w_in_t, jnp.transpose(m_w_in[0]), jnp.transpose(v_w_in[0]), 256, "adam_w_in")]
    g_win, d_win, nm_win, nv_win = win_t
    g_wo, d_wo, nm_wo, nv_wo = _adam_call(r_o, w_o[0], m_w_o[0], v_w_o[0], 256, "adam_w_o")
    p_w = _pack_small(ln_g, ln_b, rel_bias, b_f, sink)
    p_m = _pack_small(m_ln_g, m_ln_b, m_rel_bias, m_b_f, m_sink)
    p_v = _pack_small(v_ln_g, v_ln_b, v_rel_bias, v_b_f, v_sink)
    g_p, d_p, nm_p, nv_p = _adam_call(r_small, p_w, p_m, p_v, 128, "adam_small")

    loss = g_p[49, 0]
    g_lng, g_lnb, g_rel, g_bf, g_sink = _unpack_small(g_p)
    d_lng, d_lnb, d_rel, d_bf, d_sink = _unpack_small(d_p)
    m_lng, m_lnb, m_rel, m_bf, m_sk = _unpack_small(nm_p)
    v_lng, v_lnb, v_rel, v_bf, v_sk = _unpack_small(nv_p)
    return (loss, dx[None], g_win[None], g_bf, g_rel, g_sink, g_wo[None], g_lng, g_lnb,
            d_win[None], d_bf, d_rel, d_sink, d_wo[None], d_lng, d_lnb,
            nm_win[None], m_bf, m_rel, m_sk, nm_wo[None], m_lng, m_lnb,
            nv_win[None], v_bf, v_rel, v_sk, nv_wo[None], v_lng, v_lnb)
```

```python
import functools
import math

import numpy as np
import jax
import jax.numpy as jnp
from jax import lax
from jax.experimental import pallas as pl
from jax.experimental.pallas import tpu as pltpu

f32 = jnp.float32
bf16 = jnp.bfloat16

D_MODEL = 1024
HEAD_DIM = 64
FOX_HEADS = 8
SWA_HEADS = 8
SWA_KV_HEADS = 2
SWA_GROUP = 4
BLOCK = 128
NUM_BUCKETS = 32
MAX_DISTANCE = 128
LN_EPS = 1e-5
NEG_INF = -1e30
ALPHA = 2.0 ** 0.25
SCALE = 1.0 / math.sqrt(HEAD_DIM)
D_IN = 3336

ADAM_LR = 0.001
ADAM_B1 = 0.9
ADAM_B2 = 0.999
ADAM_EPS = 1e-08
ADAM_WD = 0.01
ADAM_STEP = 10

N_DEV = 8
A_FQ, A_FK, A_FV, A_FZ, A_SQ, A_SK, A_SV, A_SZ, A_FF, A_W = 0, 512, 1024, 1536, 2048, 2560, 2688, 2816, 3328, 3456
O_FF0, O_FF1 = 1536, 1544

VMEM_LIMIT = 48 * 1024 * 1024
HIGHEST = lax.Precision.HIGHEST
NT = (((1,), (1,)), ((), ()))
TN = (((0,), (0,)), ((), ()))
MESH = pl.DeviceIdType.MESH
RELS = [(0, 0, 1), (0, 1, 0), (0, 1, 1), (1, 0, 0), (1, 0, 1), (1, 1, 0), (1, 1, 1)]


VMEM_LIMIT_BIG = 60 * 1024 * 1024


def _params(sem=None, vmem=VMEM_LIMIT):
    return pltpu.CompilerParams(dimension_semantics=sem, vmem_limit_bytes=vmem)


def _sds(shape, dtype):
    return jax.ShapeDtypeStruct(shape, dtype)


def _t5_bucket_table():
    qi = np.arange(BLOCK)[:, None]
    kj = np.arange(2 * BLOCK)[None, :]
    rel = qi + BLOCK - kj
    band = (rel >= 0) & (rel < BLOCK)
    relc = np.maximum(rel, 0)
    max_exact = NUM_BUCKETS // 2
    relf = np.maximum(relc, 1).astype(np.float32)
    large = max_exact + (np.log(relf / np.float32(max_exact)) / np.float32(math.log(MAX_DISTANCE / max_exact))
                         * np.float32(NUM_BUCKETS - max_exact)).astype(np.int32)
    large = np.minimum(large, NUM_BUCKETS - 1)
    bucket = np.where(relc < max_exact, relc, large).astype(np.int32)
    bucket = np.where(band, bucket, -1).astype(np.int32)
    return bucket


def _mesh_pos():
    return lax.axis_index("x"), lax.axis_index("y"), lax.axis_index("c")


def _dev_index(p):
    return 4 * p[0] + 2 * p[1] + p[2]


def _gather_call(xs):
    n = len(xs)

    def body(*refs):
        x_refs, o_refs = refs[:n], refs[n:2 * n]
        send_sems, recv_sems, local_sems = refs[2 * n:]
        x, y, c = _mesh_pos()
        me, sib = (x, y, c), (x, y, 1 - c)
        chips = [(1 - x, y), (x, 1 - y), (1 - x, 1 - y)]

        def copy(a, k, block, to, src=None):
            slot = o_refs[a].at[_dev_index(block)]
            return pltpu.make_async_remote_copy(
                src_ref=slot if src is None else src, dst_ref=slot,
                send_sem=send_sems.at[a * 7 + k], recv_sem=recv_sems.at[a * 7 + k],
                device_id=to, device_id_type=MESH)

        mine = [pltpu.make_async_copy(x_refs[a], o_refs[a].at[_dev_index(me)], local_sems.at[a]) for a in range(n)]
        for cp in mine:
            cp.start()
        first = []
        for a in range(n):
            first.append(copy(a, 0, me, sib, src=x_refs[a]))
            first += [copy(a, 1 + j, me, (*chip, c), src=x_refs[a]) for j, chip in enumerate(chips)]
        for cp in first:
            cp.start()
        passed = []
        for j, chip in enumerate(chips):
            for a in range(n):
                copy(a, 1 + j, (*chip, c), me).wait_recv()
                fwd = copy(a, 4 + j, (*chip, c), sib)
                fwd.start()
                passed.append(fwd)
        for a in range(n):
            copy(a, 0, sib, me).wait_recv()
            for j, chip in enumerate(chips):
                copy(a, 4 + j, (*chip, 1 - c), me).wait_recv()
        for cp in first + passed:
            cp.wait_send()
        for cp in mine:
            cp.wait()

    any_spec = pl.BlockSpec(memory_space=pl.ANY)
    return pl.pallas_call(
        body,
        name="gather_weights",
        out_shape=[_sds((N_DEV,) + a.shape, a.dtype) for a in xs],
        in_specs=[any_spec] * n,
        out_specs=[any_spec] * n,
        scratch_shapes=[pltpu.SemaphoreType.DMA((7 * n,)), pltpu.SemaphoreType.DMA((7 * n,)),
                        pltpu.SemaphoreType.DMA((n,))],
    )(*xs)


def _exchange_copies(b_refs, r_refs, send_sems, recv_sems, local_sems, incoming):
    n = len(b_refs)
    x, y, c = _mesh_pos()
    me_idx = _dev_index((x, y, c))
    mine = [pltpu.make_async_copy(b_refs[a].at[me_idx], r_refs[a].at[me_idx], local_sems.at[a]) for a in range(n)]
    remote = []
    for k, r in enumerate(RELS):
        peer = ((1 - x) if r[0] else x, (1 - y) if r[1] else y, (1 - c) if r[2] else c)
        pidx = _dev_index(peer)
        for a in range(n):
            remote.append(pltpu.make_async_remote_copy(
                src_ref=b_refs[a].at[pidx], dst_ref=r_refs[a].at[pidx if incoming else me_idx],
                send_sem=send_sems.at[a * 7 + k], recv_sem=recv_sems.at[a * 7 + k],
                device_id=peer, device_id_type=MESH))
    return mine, remote


def _exchange_start(b_refs, r_refs, sems):
    mine, out = _exchange_copies(b_refs, r_refs, *sems, incoming=False)
    for cp in mine + out:
        cp.start()


def _exchange_wait(b_refs, r_refs, sems):
    mine, inc = _exchange_copies(b_refs, r_refs, *sems, incoming=True)
    for cp in inc:
        cp.wait_recv()
    for cp in inc:
        cp.wait_send()
    for cp in mine:
        cp.wait()


def _proj_call(x2, w_t, tm):
    s_len = x2.shape[0]

    def body(x_ref, w_ref, qft_ref, kf_ref, vf_ref, fz_ref, qst_ref, ks_ref, vs_ref, sz_ref, fft_ref, vat_ref,
             kst_ref, vsta_ref, kft_ref):
        xb = x_ref[...].astype(bf16)

        def seg_t(off, width):
            return lax.dot_general(w_ref[off:off + width, :], xb, NT, preferred_element_type=f32)

        def seg(off, width):
            return lax.dot_general(xb, w_ref[off:off + width, :], NT, preferred_element_type=f32)

        def put_heads(ref, acc, nheads):
            for h in range(nheads):
                ref[h] = acc[:, h * HEAD_DIM:(h + 1) * HEAD_DIM].astype(bf16)

        def put_heads_t(ref, acc_t, nheads, augment):
            for h in range(nheads):
                ref[h, 0:HEAD_DIM, :] = acc_t[h * HEAD_DIM:(h + 1) * HEAD_DIM, :].astype(bf16)
                if augment:
                    ref[h, HEAD_DIM:2 * HEAD_DIM, :] = ones_row

        ones_row = jnp.where(lax.broadcasted_iota(jnp.int32, (HEAD_DIM, tm), 0) == 0, 1.0, 0.0).astype(bf16)
        put_heads_t(vat_ref, seg_t(A_FV, 512), FOX_HEADS, True)
        put_heads_t(qft_ref, seg_t(A_FQ, 512) * SCALE, FOX_HEADS, False)
        put_heads(kf_ref, seg(A_FK, 512), FOX_HEADS)
        put_heads_t(kft_ref, seg_t(A_FK, 512), FOX_HEADS, False)
        put_heads(vf_ref, seg(A_FV, 512), FOX_HEADS)
        fz_ref[...] = seg(A_FZ, 512)
        put_heads_t(qst_ref, seg_t(A_SQ, 512) * SCALE, SWA_HEADS, False)
        put_heads(ks_ref, seg(A_SK, 128), SWA_KV_HEADS)
        put_heads(vs_ref, seg(A_SV, 128), SWA_KV_HEADS)
        put_heads_t(kst_ref, seg_t(A_SK, 128), SWA_KV_HEADS, False)
        put_heads_t(vsta_ref, seg_t(A_SV, 128), SWA_KV_HEADS, True)
        sz_ref[...] = seg(A_SZ, 512)
        fft_ref[...] = seg(A_FF, 128).T[:FOX_HEADS, :]

    def heads(nh):
        return pl.BlockSpec((nh, tm, HEAD_DIM), lambda i: (0, i, 0))

    def feat(nh, rows):
        return pl.BlockSpec((nh, rows, tm), lambda i: (0, 0, i))

    wide = pl.BlockSpec((tm, 512), lambda i: (i, 0))
    return pl.pallas_call(
        body,
        name="proj_fwd",
        grid=(s_len // tm,),
        in_specs=[pl.BlockSpec((tm, D_MODEL), lambda i: (i, 0)), pl.BlockSpec((A_W, D_MODEL), lambda i: (0, 0))],
        out_specs=[feat(8, HEAD_DIM), heads(8), heads(8), wide, feat(8, HEAD_DIM), heads(2), heads(2), wide,
                   pl.BlockSpec((FOX_HEADS, tm), lambda i: (0, i)),
                   feat(FOX_HEADS, 2 * HEAD_DIM), feat(2, HEAD_DIM), feat(2, 2 * HEAD_DIM), feat(8, HEAD_DIM)],
        out_shape=[_sds((8, HEAD_DIM, s_len), bf16)] + [_sds((8, s_len, HEAD_DIM), bf16)] * 2
                  + [_sds((s_len, 512), f32), _sds((8, HEAD_DIM, s_len), bf16),
                     _sds((2, s_len, HEAD_DIM), bf16), _sds((2, s_len, HEAD_DIM), bf16), _sds((s_len, 512), f32),
                     _sds((FOX_HEADS, s_len), f32), _sds((FOX_HEADS, 2 * HEAD_DIM, s_len), bf16),
                     _sds((2, HEAD_DIM, s_len), bf16), _sds((2, 2 * HEAD_DIM, s_len), bf16),
                     _sds((8, HEAD_DIM, s_len), bf16)],
        compiler_params=_params(("arbitrary",)),
    )(x2, w_t)


AUG = 2 * HEAD_DIM


def _augment_call(q_t, k, k_t, cum_row, tm):
    nh, s_len, _ = k.shape
    per_step = tm // FOX_T

    def body(qt_ref, k_ref, kt_ref, c_ref, qat_ref, ka_ref, kat_ref, st_ref):
        c = c_ref[0]
        hi = c.astype(bf16).astype(f32)
        r1 = c - hi
        mid = r1.astype(bf16).astype(f32)
        lo = (r1 - mid).astype(bf16).astype(f32)
        row = lax.broadcasted_iota(jnp.int32, (HEAD_DIM, tm), 0)
        q_tail = jnp.where(row == 0, hi, jnp.where(row == 1, mid, jnp.where(row == 2, lo,
                           jnp.where(row < 6, 1.0, 0.0))))
        k_tail = jnp.where(row < 3, 1.0, jnp.where(row == 3, -hi, jnp.where(row == 4, -mid,
                           jnp.where(row == 5, -lo, 0.0))))
        qat_ref[0, 0:HEAD_DIM, :] = qt_ref[0]
        qat_ref[0, HEAD_DIM:AUG, :] = q_tail.astype(bf16)
        ka_ref[0] = jnp.concatenate([k_ref[0], k_tail.T.astype(bf16)], axis=1)
        kat_ref[0, 0:HEAD_DIM, :] = kt_ref[0]
        kat_ref[0, HEAD_DIM:AUG, :] = k_tail.astype(bf16)
        qt = qt_ref[0].astype(f32)
        kt = kt_ref[0].astype(f32)
        qn2 = jnp.sum(qt * qt, axis=0, keepdims=True)
        kn2 = jnp.sum(kt * kt, axis=0, keepdims=True)
        sd = jnp.sum(qt * kt, axis=0, keepdims=True)
        srow = lax.broadcasted_iota(jnp.int32, (8, LANES), 0)
        for part in range(per_step):
            sl = slice(part * FOX_T, (part + 1) * FOX_T)
            vals = [jnp.sqrt(jnp.max(qn2[:, sl], axis=1, keepdims=True)),
                    jnp.sqrt(jnp.max(kn2[:, sl], axis=1, keepdims=True)),
                    jnp.min(sd[:, sl], axis=1, keepdims=True),
                    jnp.max(c[:, sl], axis=1, keepdims=True), jnp.min(c[:, sl], axis=1, keepdims=True)]
            out = jnp.zeros((8, LANES), f32)
            for r, val in enumerate(vals):
                out = jnp.where(srow == r, val, out)
            st_ref[0, part] = out

    tile = pl.BlockSpec((1, tm, HEAD_DIM), lambda h, i: (h, i, 0))
    tile_t = pl.BlockSpec((1, HEAD_DIM, tm), lambda h, i: (h, 0, i))
    return pl.pallas_call(
        body,
        name="fox_augment",
        grid=(nh, s_len // tm),
        in_specs=[tile_t, tile, tile_t, pl.BlockSpec((1, 1, tm), lambda h, i: (h, 0, i))],
        out_specs=[pl.BlockSpec((1, AUG, tm), lambda h, i: (h, 0, i)),
                   pl.BlockSpec((1, tm, AUG), lambda h, i: (h, i, 0)),
                   pl.BlockSpec((1, AUG, tm), lambda h, i: (h, 0, i)),
                   pl.BlockSpec((1, per_step, 8, LANES), lambda h, i: (h, i, 0, 0))],
        out_shape=[_sds((nh, AUG, s_len), bf16), _sds((nh, s_len, AUG), bf16), _sds((nh, AUG, s_len), bf16),
                   _sds((nh, s_len // FOX_T, 8, LANES), f32)],
        compiler_params=_params(("arbitrary", "arbitrary")),
    )(q_t, k, k_t, cum_row)


EXP_ZERO_GAP = 110.0


def _fox_prune_tables(stats):
    s = stats[:, :, :, 0]
    qn, kn, sd, cmx, cmn = (s[:, :, r] for r in range(5))
    nt = s.shape[1]
    bound = qn[:, :, None] * kn[:, None, :] + (cmx[:, :, None] - cmn[:, None, :])
    margin = 2.0 + 1e-5 * (jnp.abs(cmx)[:, :, None] + jnp.abs(cmn)[:, None, :])
    qi = lax.broadcasted_iota(jnp.int32, (nt, nt), 0)
    kj = lax.broadcasted_iota(jnp.int32, (nt, nt), 1)
    skip = (bound + margin < sd[:, :, None] - EXP_ZERO_GAP) & (kj < qi)[None]
    first = jnp.sum(jnp.cumprod(skip.astype(jnp.int32), axis=2), axis=2)
    tiles = lax.broadcasted_iota(jnp.int32, (1, nt), 1)
    cnt = tiles - first
    ends = jnp.cumsum(cnt, axis=1)
    off = ends - cnt
    kmax = nt * (nt - 1) // 2
    k = lax.broadcasted_iota(jnp.int32, (1, kmax), 1)
    pair_q = jnp.minimum(jnp.sum((ends[:, None, :] <= k[:, :, None]).astype(jnp.int32), axis=2), nt - 1)
    hit = pair_q[:, :, None] == tiles[:, None, :]
    first_k = jnp.sum(jnp.where(hit, first[:, None, :], 0), axis=2)
    off_k = jnp.sum(jnp.where(hit, off[:, None, :], 0), axis=2)
    pair_k = jnp.clip(first_k + k - off_k, 0, nt - 1)
    return (ends[:, nt - 1].astype(jnp.int32), pair_q.reshape(-1).astype(jnp.int32),
            pair_k.reshape(-1).astype(jnp.int32))


CUM_CHUNK = 512


def _cum_call(fft, bf_col):
    s_len = fft.shape[1]
    ch = CUM_CHUNK

    def body(f_ref, b_ref, cum_ref, sg_ref):
        r = lax.broadcasted_iota(jnp.int32, (ch, ch), 0)
        c = lax.broadcasted_iota(jnp.int32, (ch, ch), 1)
        upper = (r <= c).astype(f32)
        carry = jnp.zeros((FOX_HEADS, 1), f32)
        for n in range(s_len // ch):
            z = f_ref[:, n * ch:(n + 1) * ch] + b_ref[...]
            logf = jnp.minimum(z, 0.0) - jnp.log1p(jnp.exp(-jnp.abs(z)))
            sg_ref[:, n * ch:(n + 1) * ch] = 1.0 / (1.0 + jnp.exp(z))
            cs = jnp.dot(logf, upper, precision=HIGHEST, preferred_element_type=f32) + carry
            cum_ref[:, n * ch:(n + 1) * ch] = cs
            carry = cs[:, ch - 1:ch]

    return pl.pallas_call(
        body,
        name="fox_cum_fwd",
        out_shape=[_sds((FOX_HEADS, s_len), f32)] * 2,
        compiler_params=_params(),
    )(fft, bf_col)


def _cum_bwd_call(dcq, dck, sg):
    s_len = sg.shape[1]
    ch = CUM_CHUNK
    nch = s_len // ch

    def body(q_ref, k_ref, sg_ref, dff_ref, dbf_ref):
        r = lax.broadcasted_iota(jnp.int32, (ch, ch), 0)
        c = lax.broadcasted_iota(jnp.int32, (ch, ch), 1)
        lower = (r >= c).astype(f32)
        dff_ref[...] = jnp.zeros_like(dff_ref)
        carry = jnp.zeros((FOX_HEADS, 1), f32)
        total = jnp.zeros((FOX_HEADS, 1), f32)
        for n in reversed(range(nch)):
            sl = slice(n * ch, (n + 1) * ch)
            dcum = q_ref[:, sl] - k_ref[:, sl]
            rs = jnp.dot(dcum, lower, precision=HIGHEST, preferred_element_type=f32) + carry
            carry = rs[:, 0:1]
            dff = rs * sg_ref[:, sl]
            dff_ref[0:FOX_HEADS, sl] = dff
            total = total + jnp.sum(dff, axis=1, keepdims=True)
        dbf_ref[...] = jnp.broadcast_to(total, (FOX_HEADS, 128))

    return pl.pallas_call(
        body,
        name="fox_cum_bwd",
        out_shape=[_sds((128, s_len), f32), _sds((FOX_HEADS, 128), f32)],
        compiler_params=_params(),
    )(dcq, dck, sg)


FOX_T = 512
LANES = 128


def _causal_keep(t):
    return lax.broadcasted_iota(jnp.int32, (t, t), 0) <= lax.broadcasted_iota(jnp.int32, (t, t), 1)


def _tile_cols(i, t):
    return pl.ds(pl.multiple_of(i * t, t), t)


def _fox_pair(n, nt, kmax, h, pq_ref, pk_ref):
    k = h * kmax + jnp.maximum(n - nt, 0)
    return jnp.where(n < nt, n, pq_ref[k]), jnp.where(n < nt, n, pk_ref[k])


def _fox_fwd_call(qat, ka, vat, npairs, pair_q, pair_k):
    nh, s_len, _ = ka.shape
    t = FOX_T
    nt = s_len // t
    kmax = nt * (nt - 1) // 2
    assert nt >= 2 and nt % 2 == 0

    def body(np_ref, pq_ref, pk_ref, qat_ref, ka_ref, vat_ref, o_ref, lse_ref, s0, s1, p0, p1, a0, a1, m_all, acc_all):
        h = pl.program_id(0)
        extra = np_ref[h]
        total = nt + extra
        m_all[...] = jnp.full(m_all.shape, NEG_INF, f32)
        acc_all[...] = jnp.zeros(acc_all.shape, f32)
        bufs = ((s0, p0, a0), (s1, p1, a1))

        def pair(n):
            return _fox_pair(n, nt, kmax, h, pq_ref, pk_ref)

        def scores(n, b, masked):
            i, j = pair(n)
            st = jnp.dot(ka_ref[0, _tile_cols(j, t), :], qat_ref[0, :, _tile_cols(i, t)], preferred_element_type=f32)
            if masked:
                st = jnp.where(_causal_keep(t), st, NEG_INF)
            bufs[b][0][...] = st

        def softmax(n, b):
            i, _ = pair(n)
            s_ref, p_ref, a_ref = bufs[b]
            for c in range(t // LANES):
                cols = slice(c * LANES, (c + 1) * LANES)
                mcols = pl.ds(pl.multiple_of(i * t + c * LANES, LANES), LANES)
                m_old = m_all[:, mcols]
                m_new = jnp.maximum(m_old, jnp.max(s_ref[:, cols], axis=0, keepdims=True))
                m_all[:, mcols] = m_new
                a_ref[:, cols] = jnp.exp(m_old - m_new)
                p_ref[:, cols] = jnp.exp(s_ref[:, cols] - m_new).astype(bf16)

        def accum(n, b):
            i, j = pair(n)
            cols = _tile_cols(i, t)
            acc_all[:, cols] = bufs[b][2][...] * acc_all[:, cols] + jnp.dot(
                vat_ref[0, :, _tile_cols(j, t)], bufs[b][1][...], preferred_element_type=f32)

        def step(n, b, masked):
            accum(n - 2, b)
            softmax(n - 1, 1 - b)
            scores(n, b, masked)

        scores(0, 0, True)
        scores(1, 1, True)
        softmax(0, 0)

        def diag_steps(d, _):
            n = 2 + 2 * d
            step(n, 0, True)
            step(n + 1, 1, True)
            return 0

        lax.fori_loop(0, (nt - 2) // 2, diag_steps, 0)

        def off_steps(d, _):
            n = nt + 2 * d
            step(n, 0, False)
            step(n + 1, 1, False)
            return 0

        lax.fori_loop(0, extra // 2, off_steps, 0)

        @pl.when(extra % 2 == 1)
        def _():
            step(total - 1, 0, False)
            softmax(total - 1, 0)
            accum(total - 2, 1)
            accum(total - 1, 0)

        @pl.when(extra % 2 == 0)
        def _():
            softmax(total - 1, 1)
            accum(total - 2, 0)
            accum(total - 1, 1)

        l = acc_all[HEAD_DIM:HEAD_DIM + 1, :]
        o_ref[0] = acc_all[0:HEAD_DIM, :] / l
        lse_ref[0] = m_all[...] + jnp.log(l)

    smem = pl.BlockSpec(memory_space=pltpu.SMEM)
    return pl.pallas_call(
        body,
        name="fox_fwd",
        grid=(nh,),
        in_specs=[smem, smem, smem,
                  pl.BlockSpec((1, AUG, s_len), lambda h: (h, 0, 0)),
                  pl.BlockSpec((1, s_len, AUG), lambda h: (h, 0, 0)),
                  pl.BlockSpec((1, AUG, s_len), lambda h: (h, 0, 0))],
        out_specs=[pl.BlockSpec((1, HEAD_DIM, s_len), lambda h: (h, 0, 0)),
                   pl.BlockSpec((1, 1, s_len), lambda h: (h, 0, 0))],
        out_shape=[_sds((nh, HEAD_DIM, s_len), f32), _sds((nh, 1, s_len), f32)],
        scratch_shapes=[pltpu.VMEM((t, t), f32), pltpu.VMEM((t, t), f32), pltpu.VMEM((t, t), bf16),
                        pltpu.VMEM((t, t), bf16), pltpu.VMEM((1, t), f32), pltpu.VMEM((1, t), f32),
                        pltpu.VMEM((1, s_len), f32), pltpu.VMEM((AUG, s_len), f32)],
        compiler_params=_params(("arbitrary",)),
    )(npairs, pair_q, pair_k, qat, ka, vat)


SWA_TS = 512


SWA_W = SWA_GROUP * BLOCK


def _swa_bias_call(rel_bias, bucket_t):
    def body(rb_ref, bk_ref, b_ref, b0_ref):
        bk = bk_ref[...]
        row = lax.broadcasted_iota(jnp.int32, (2 * BLOCK, BLOCK), 0)
        for h in range(SWA_HEADS):
            acc = jnp.full((2 * BLOCK, BLOCK), NEG_INF, f32)
            for b in range(NUM_BUCKETS):
                acc = jnp.where(bk == b, rb_ref[b, h], acc)
            g, hh = divmod(h, SWA_GROUP)
            b_ref[g, :, hh * BLOCK:(hh + 1) * BLOCK] = acc
            b0_ref[g, :, hh * BLOCK:(hh + 1) * BLOCK] = jnp.where(row < BLOCK, NEG_INF, acc)

    return pl.pallas_call(
        body,
        name="swa_bias",
        in_specs=[pl.BlockSpec(memory_space=pltpu.SMEM), pl.BlockSpec(memory_space=pltpu.VMEM)],
        out_shape=[_sds((SWA_KV_HEADS, 2 * BLOCK, SWA_W), f32)] * 2,
        compiler_params=_params(),
    )(rel_bias, bucket_t)


def _swa_bias_bwd_call(dbias, bucket_t):
    def body(d_ref, bk_ref, o_ref):
        bk = bk_ref[...]
        row = lax.broadcasted_iota(jnp.int32, (NUM_BUCKETS, 128), 0)
        col = lax.broadcasted_iota(jnp.int32, (NUM_BUCKETS, 128), 1)
        out = jnp.zeros((NUM_BUCKETS, 128), f32)
        for h in range(SWA_HEADS):
            g, hh = divmod(h, SWA_GROUP)
            d = d_ref[g, :, hh * BLOCK:(hh + 1) * BLOCK]
            for b in range(NUM_BUCKETS):
                val = jnp.sum(jnp.sum(jnp.where(bk == b, d, 0.0), axis=1, keepdims=True), axis=0, keepdims=True)
                out = jnp.where((row == b) & (col == h), val, out)
        o_ref[...] = out

    return pl.pallas_call(
        body,
        name="swa_bias_bwd",
        out_shape=_sds((NUM_BUCKETS, 128), f32),
        compiler_params=_params(),
    )(dbias, bucket_t)


def _sink_row(sink_ref, g):
    return jnp.concatenate([jnp.full((1, BLOCK), sink_ref[g * SWA_GROUP + hh], f32) for hh in range(SWA_GROUP)], axis=1)


def _group_lanes(ref, g, cols):
    return jnp.concatenate([ref[g * SWA_GROUP + hh, :, cols] for hh in range(SWA_GROUP)], axis=1)


def _swa_fwd_call(qt, k, vta, bias_t, bias0_t, sink):
    s_len = qt.shape[2]
    ts = SWA_TS
    nb = ts // BLOCK

    def body(qt_ref, kc_ref, kp_ref, vc_ref, vp_ref, b_ref, b0_ref, sink_ref, o_ref, lse_ref):
        first = pl.program_id(0) == 0
        kall = [jnp.concatenate([kp_ref[g], kc_ref[g]], axis=0) for g in range(SWA_KV_HEADS)]
        vall = [jnp.concatenate([vp_ref[g], vc_ref[g]], axis=1) for g in range(SWA_KV_HEADS)]
        sinks = [_sink_row(sink_ref, g) for g in range(SWA_KV_HEADS)]
        items = [(g, b) for g in range(SWA_KV_HEADS) for b in range(nb)]

        def scores(g, b):
            qg = _group_lanes(qt_ref, g, slice(b * BLOCK, (b + 1) * BLOCK))
            bias_b = b_ref[g]
            if b == 0:
                bias_b = jnp.where(first, b0_ref[g], bias_b)
            return jnp.dot(kall[g][b * BLOCK:(b + 2) * BLOCK], qg, preferred_element_type=f32) + bias_b

        def finish(g, b, st):
            m = jnp.maximum(jnp.max(st, axis=0, keepdims=True), sinks[g])
            pt = jnp.exp(st - m)
            acc = jnp.dot(vall[g][:, b * BLOCK:(b + 2) * BLOCK], pt.astype(bf16), preferred_element_type=f32)
            l = acc[HEAD_DIM:HEAD_DIM + 1, :] + jnp.exp(sinks[g] - m)
            return acc[0:HEAD_DIM, :] / l, m + jnp.log(l)

        outs, lses = {}, {}
        st_next = scores(*items[0])
        for idx, (g, b) in enumerate(items):
            st = st_next
            if idx + 1 < len(items):
                st_next = scores(*items[idx + 1])
            outs[g, b], lses[g, b] = finish(g, b, st)
        for g in range(SWA_KV_HEADS):
            for hh in range(SWA_GROUP):
                lanes = slice(hh * BLOCK, (hh + 1) * BLOCK)
                o_ref[g * SWA_GROUP + hh] = jnp.concatenate([outs[g, b][:, lanes] for b in range(nb)], axis=1)
                lse_ref[g * SWA_GROUP + hh] = jnp.concatenate([lses[g, b][:, lanes] for b in range(nb)], axis=1)

    def prev_blk(n):
        return jnp.maximum(n * nb - 1, 0)

    bspec = pl.BlockSpec((SWA_KV_HEADS, 2 * BLOCK, SWA_W), lambda n: (0, 0, 0))
    return pl.pallas_call(
        body,
        name="swa_fwd",
        grid=(s_len // ts,),
        in_specs=[pl.BlockSpec((SWA_HEADS, HEAD_DIM, ts), lambda n: (0, 0, n)),
                  pl.BlockSpec((SWA_KV_HEADS, ts, HEAD_DIM), lambda n: (0, n, 0)),
                  pl.BlockSpec((SWA_KV_HEADS, BLOCK, HEAD_DIM), lambda n: (0, prev_blk(n), 0)),
                  pl.BlockSpec((SWA_KV_HEADS, AUG, ts), lambda n: (0, 0, n)),
                  pl.BlockSpec((SWA_KV_HEADS, AUG, BLOCK), lambda n: (0, 0, prev_blk(n))),
                  bspec, bspec, pl.BlockSpec(memory_space=pltpu.SMEM)],
        out_specs=[pl.BlockSpec((SWA_HEADS, HEAD_DIM, ts), lambda n: (0, 0, n)),
                   pl.BlockSpec((SWA_HEADS, 1, ts), lambda n: (0, 0, n))],
        out_shape=[_sds((SWA_HEADS, HEAD_DIM, s_len), f32), _sds((SWA_HEADS, 1, s_len), f32)],
        compiler_params=_params(("arbitrary",)),
    )(qt, k, k, vta, vta, bias_t, bias0_t, sink)


def _head_selector():
    sel = np.zeros((512, 128), np.float32)
    for h in range(8):
        sel[h * HEAD_DIM:(h + 1) * HEAD_DIM, h] = 1.0
    return sel


def _post_call(of, fz, osw, sz, x2, tgt, wo, ln_g, ln_b, sel, tm):
    s_len = x2.shape[0]

    def body(of_ref, fz_ref, os_ref, sz_ref, x_ref, t_ref, wo_ref, g_ref, b_ref, sel_ref,
             dh_ref, dof_ref, dfz_ref, dos_ref, dsz_ref, dlf_ref, dls_ref, dwo_ref, dg_ref, db_ref, loss_ref):
        n = pl.program_id(0)

        @pl.when(n == 0)
        def _():
            dwo_ref[...] = jnp.zeros_like(dwo_ref)
            dg_ref[...] = jnp.zeros_like(dg_ref)
            db_ref[...] = jnp.zeros_like(db_ref)
            loss_ref[...] = jnp.zeros_like(loss_ref)

        o_f = of_ref[...].T
        o_s = os_ref[...].T
        fz = fz_ref[...]
        sz = sz_ref[...]
        sg_f = jax.nn.sigmoid(fz)
        sg_s = jax.nn.sigmoid(sz)
        silu_f = fz * sg_f
        silu_s = sz * sg_s
        mixed = jnp.concatenate([o_f * silu_f, o_s * silu_s], axis=1).astype(bf16)
        y = jnp.dot(mixed, wo_ref[...], preferred_element_type=f32)
        h = ALPHA * x_ref[...] + y
        mu = jnp.mean(h, axis=1, keepdims=True)
        hc = h - mu
        var = jnp.mean(hc * hc, axis=1, keepdims=True)
        rstd = lax.rsqrt(var + LN_EPS)
        xhat = hc * rstd
        gam = g_ref[...]
        out = xhat * gam + b_ref[...]
        err = out - t_ref[...]
        tok_loss = jnp.mean(err * err, axis=1, keepdims=True)
        loss_ref[...] += 0.5 * jnp.sum(tok_loss, axis=0, keepdims=True)
        dout = err * (1.0 / D_MODEL)
        dg_ref[...] += jnp.sum(dout * xhat, axis=0, keepdims=True)
        db_ref[...] += jnp.sum(dout, axis=0, keepdims=True)
        dxh = dout * gam
        m1 = jnp.mean(dxh, axis=1, keepdims=True)
        m2 = jnp.mean(dxh * xhat, axis=1, keepdims=True)
        dh = rstd * (dxh - m1 - xhat * m2)
        dh_ref[...] = dh
        dyb = dh.astype(bf16)
        dmix = lax.dot_general(dyb, wo_ref[...], NT, preferred_element_type=f32)
        dwo_ref[...] += lax.dot_general(mixed, dyb, TN, preferred_element_type=f32)
        dm_f = dmix[:, :512]
        dm_s = dmix[:, 512:]
        do_f = dm_f * silu_f
        do_s = dm_s * silu_s
        dfz_ref[...] = (dm_f * o_f * (sg_f * (1.0 + fz * (1.0 - sg_f)))).astype(bf16)
        dsz_ref[...] = (dm_s * o_s * (sg_s * (1.0 + sz * (1.0 - sg_s)))).astype(bf16)
        dof_ref[...] = do_f.T.astype(bf16)
        dos_ref[...] = do_s.T.astype(bf16)
        sel_m = sel_ref[...]

        def head_sums(prod):
            hi = prod.astype(bf16)
            lo = (prod - hi.astype(f32)).astype(bf16)
            return (jnp.dot(hi, sel_m, preferred_element_type=f32) + jnp.dot(lo, sel_m, preferred_element_type=f32))

        dl_f = head_sums(do_f * o_f)
        dl_s = head_sums(do_s * o_s)
        dlf_ref[...] = dl_f.T[:FOX_HEADS, :]
        dls_ref[...] = dl_s.T[:SWA_HEADS, :]

    feat = pl.BlockSpec((512, tm), lambda n: (0, n))
    rows8 = pl.BlockSpec((8, tm), lambda n: (0, n))
    half = pl.BlockSpec((tm, 512), lambda n: (n, 0))
    fullw = pl.BlockSpec((tm, D_MODEL), lambda n: (n, 0))
    vec = pl.BlockSpec((1, D_MODEL), lambda n: (0, 0))
    return pl.pallas_call(
        body,
        name="post_fwd_bwd",
        grid=(s_len // tm,),
        in_specs=[feat, half, feat, half, fullw, fullw,
                  pl.BlockSpec((D_MODEL, D_MODEL), lambda n: (0, 0)), vec, vec,
                  pl.BlockSpec((512, 128), lambda n: (0, 0))],
        out_specs=[fullw, feat, half, feat, half, rows8, rows8,
                   pl.BlockSpec((D_MODEL, D_MODEL), lambda n: (0, 0)), vec, vec,
                   pl.BlockSpec((1, 1), lambda n: (0, 0))],
        out_shape=[_sds((s_len, D_MODEL), f32), _sds((512, s_len), bf16), _sds((s_len, 512), bf16),
                   _sds((512, s_len), bf16), _sds((s_len, 512), bf16),
                   _sds((FOX_HEADS, s_len), f32), _sds((SWA_HEADS, s_len), f32),
                   _sds((D_MODEL, D_MODEL), f32), _sds((1, D_MODEL), f32), _sds((1, D_MODEL), f32),
                   _sds((1, 1), f32)],
        compiler_params=_params(("arbitrary",)),
    )(of, fz, osw, sz, x2, tgt, wo, ln_g, ln_b, sel)


def _fox_bwd_call(ka, kat, v, qat, dot, lse_row, dl_row, npairs, pair_q, pair_k):
    nh, s_len, _ = ka.shape
    t = FOX_T
    nt = s_len // t
    kmax = nt * (nt - 1) // 2
    assert nt >= 2 and nt % 2 == 0
    ck_slot = HEAD_DIM + 3
    cq_slot = HEAD_DIM

    def body(np_ref, pq_ref, pk_ref, ka_ref, kat_ref, v_ref, qat_ref, dot_ref, lse_ref, dl_ref,
             dq_ref, dk_ref, dv_ref, dcq_ref, dck_ref, dqt_all, dkat_all, dvt_all, p0, p1, ds0, ds1):
        h = pl.program_id(0)
        extra = np_ref[h]
        total = nt + extra
        dqt_all[...] = jnp.zeros(dqt_all.shape, f32)
        dkat_all[...] = jnp.zeros(dkat_all.shape, f32)
        dvt_all[...] = jnp.zeros(dvt_all.shape, f32)
        pbuf, dsbuf = (p0, p1), (ds0, ds1)

        def pair(n):
            return _fox_pair(n, nt, kmax, h, pq_ref, pk_ref)

        def probs(n, b, masked):
            i, j = pair(n)
            qc, kr = _tile_cols(i, t), _tile_cols(j, t)
            st = jnp.dot(ka_ref[0, kr, :], qat_ref[0, :, qc], preferred_element_type=f32)
            dpt = jnp.dot(v_ref[0, kr, :], dot_ref[0, :, qc], preferred_element_type=f32)
            if masked:
                st = jnp.where(_causal_keep(t), st, NEG_INF)
            pt = jnp.exp(st - lse_ref[0, :, qc])
            pbuf[b][...] = pt.astype(bf16)
            dsbuf[b][...] = (pt * (dpt - dl_ref[0, :, qc])).astype(bf16)

        def grads(n, b):
            i, j = pair(n)
            qc, kc = _tile_cols(i, t), _tile_cols(j, t)
            dvt_all[:, kc] += lax.dot_general(dot_ref[0, :, qc], pbuf[b][...], NT, preferred_element_type=f32)
            dkat_all[:, kc] += lax.dot_general(qat_ref[0, :, qc], dsbuf[b][...], NT, preferred_element_type=f32)
            dqt_all[:, qc] += jnp.dot(kat_ref[0, :, kc], dsbuf[b][...], preferred_element_type=f32)

        def step(n, b, masked):
            i, j = pair(n)
            qc, kr = _tile_cols(i, t), _tile_cols(j, t)
            i1, j1 = pair(n - 1)
            qc1, kc1 = _tile_cols(i1, t), _tile_cols(j1, t)
            c = 1 - b
            st = jnp.dot(ka_ref[0, kr, :], qat_ref[0, :, qc], preferred_element_type=f32)
            dvt_all[:, kc1] += lax.dot_general(dot_ref[0, :, qc1], pbuf[c][...], NT, preferred_element_type=f32)
            if masked:
                st = jnp.where(_causal_keep(t), st, NEG_INF)
            pt = jnp.exp(st - lse_ref[0, :, qc])
            pbuf[b][...] = pt.astype(bf16)
            dpt = jnp.dot(v_ref[0, kr, :], dot_ref[0, :, qc], preferred_element_type=f32)
            dkat_all[:, kc1] += lax.dot_general(qat_ref[0, :, qc1], dsbuf[c][...], NT, preferred_element_type=f32)
            dqt_all[:, qc1] += jnp.dot(kat_ref[0, :, kc1], dsbuf[c][...], preferred_element_type=f32)
            dsbuf[b][...] = (pt * (dpt - dl_ref[0, :, qc])).astype(bf16)

        probs(0, 0, True)
        step(1, 1, True)

        def diag_steps(d, _):
            n = 2 + 2 * d
            step(n, 0, True)
            step(n + 1, 1, True)
            return 0

        lax.fori_loop(0, (nt - 2) // 2, diag_steps, 0)

        def off_steps(d, _):
            n = nt + 2 * d
            step(n, 0, False)
            step(n + 1, 1, False)
            return 0

        lax.fori_loop(0, extra // 2, off_steps, 0)

        @pl.when(extra % 2 == 1)
        def _():
            step(total - 1, 0, False)
            grads(total - 1, 0)

        @pl.when(extra % 2 == 0)
        def _():
            grads(total - 1, 1)

        dq_ref[0] = (dqt_all[0:HEAD_DIM, :] * SCALE).astype(bf16)
        dk_ref[0] = dkat_all[0:HEAD_DIM, :].astype(bf16)
        dv_ref[0] = dvt_all[...].astype(bf16)
        dcq_ref[0] = dqt_all[cq_slot:cq_slot + 1, :]
        dck_ref[0] = dkat_all[ck_slot:ck_slot + 1, :]

    smem = pl.BlockSpec(memory_space=pltpu.SMEM)
    rows = pl.BlockSpec((1, s_len, AUG), lambda h: (h, 0, 0))
    feat = pl.BlockSpec((1, AUG, s_len), lambda h: (h, 0, 0))
    feat64 = pl.BlockSpec((1, HEAD_DIM, s_len), lambda h: (h, 0, 0))
    rowv = pl.BlockSpec((1, 1, s_len), lambda h: (h, 0, 0))
    return pl.pallas_call(
        body,
        name="fox_bwd",
        grid=(nh,),
        in_specs=[smem, smem, smem, rows, feat, pl.BlockSpec((1, s_len, HEAD_DIM), lambda h: (h, 0, 0)), feat, feat64,
                  rowv, rowv],
        out_specs=[feat64, feat64, feat64, rowv, rowv],
        out_shape=[_sds((nh, HEAD_DIM, s_len), bf16)] * 3 + [_sds((nh, 1, s_len), f32)] * 2,
        scratch_shapes=[pltpu.VMEM((AUG, s_len), f32), pltpu.VMEM((AUG, s_len), f32), pltpu.VMEM((HEAD_DIM, s_len), f32)]
                       + [pltpu.VMEM((t, t), bf16)] * 4,
        compiler_params=_params(("arbitrary",)),
    )(npairs, pair_q, pair_k, ka, kat, v, qat, dot, lse_row, dl_row)


def _swa_bwd_call(qt, k, kt, v, dot, lse, dl, bias_t, bias0_t, sink):
    s_len = qt.shape[2]
    ts = SWA_TS
    nb = ts // BLOCK
    nsteps = s_len // ts

    def body(qt_ref, kc_ref, kp_ref, ktc_ref, ktp_ref, vc_ref, vp_ref, dot_ref, lse_ref, dl_ref, b_ref, b0_ref,
             sink_ref, dq_ref, dk_ref, dv_ref, dbias_ref, dsink_ref, dk_s, dv_s, tail_k, tail_v, sk_s):
        n = pl.program_id(0)

        @pl.when(n == 0)
        def _():
            dbias_ref[...] = jnp.zeros_like(dbias_ref)
            sk_s[...] = jnp.zeros_like(sk_s)

        @pl.when(n < nsteps)
        def _():
            first = n == 0
            dk_s[...] = jnp.zeros_like(dk_s)
            dv_s[...] = jnp.zeros_like(dv_s)
            groups = range(SWA_KV_HEADS)
            kall = [jnp.concatenate([kp_ref[g], kc_ref[g]], axis=0) for g in groups]
            vall = [jnp.concatenate([vp_ref[g], vc_ref[g]], axis=0) for g in groups]
            ktall = [jnp.concatenate([ktp_ref[g], ktc_ref[g]], axis=1) for g in groups]
            sinks = [_sink_row(sink_ref, g) for g in groups]
            items = [(g, b) for g in groups for b in range(nb)]

            def products(g, b):
                cols = slice(b * BLOCK, (b + 1) * BLOCK)
                win = slice(b * BLOCK, (b + 2) * BLOCK)
                qg = _group_lanes(qt_ref, g, cols)
                dog = _group_lanes(dot_ref, g, cols)
                bias_b = b_ref[g]
                if b == 0:
                    bias_b = jnp.where(first, b0_ref[g], bias_b)
                st = jnp.dot(kall[g][win], qg, preferred_element_type=f32) + bias_b
                dpt = jnp.dot(vall[g][win], dog, preferred_element_type=f32)
                return qg, dog, st, dpt

            def finish(g, b, qg, dog, st, dpt):
                cols = slice(b * BLOCK, (b + 1) * BLOCK)
                win = slice(b * BLOCK, (b + 2) * BLOCK)
                lse_r = _group_lanes(lse_ref, g, cols)
                dl_r = _group_lanes(dl_ref, g, cols)
                pt = jnp.exp(st - lse_r)
                dst = pt * (dpt - dl_r)
                dsb = dst.astype(bf16)
                dk_s[g, :, win] += lax.dot_general(qg, dsb, NT, preferred_element_type=f32)
                dv_s[g, :, win] += lax.dot_general(dog, pt.astype(bf16), NT, preferred_element_type=f32)
                dqg = jnp.dot(ktall[g][:, win], dsb, preferred_element_type=f32) * SCALE
                return dqg, dst, -jnp.exp(sinks[g] - lse_r) * dl_r

            dqs, dsts, sks = {}, {}, {}
            nxt = products(*items[0])
            for idx, (g, b) in enumerate(items):
                cur = nxt
                if idx + 1 < len(items):
                    nxt = products(*items[idx + 1])
                dqs[g, b], dsts[g, b], sks[g, b] = finish(g, b, *cur)
            for g in groups:
                dbias_ref[g] += functools.reduce(lambda a, c: a + c, [dsts[g, b] for b in range(nb)])
                sk_s[g] += functools.reduce(lambda a, c: a + c, [sks[g, b] for b in range(nb)])
                for hh in range(SWA_GROUP):
                    lanes = slice(hh * BLOCK, (hh + 1) * BLOCK)
                    dq_ref[g * SWA_GROUP + hh] = jnp.concatenate(
                        [dqs[g, b][:, lanes] for b in range(nb)], axis=1).astype(bf16)

        @pl.when(n > 0)
        def _():
            last = slice(ts - BLOCK, ts)
            for g in range(SWA_KV_HEADS):
                add_k = jnp.where(n < nsteps, dk_s[g, :, 0:BLOCK], 0.0)
                add_v = jnp.where(n < nsteps, dv_s[g, :, 0:BLOCK], 0.0)
                dk_ref[g, :, 0:ts - BLOCK] = tail_k[g, :, 0:ts - BLOCK].astype(bf16)
                dv_ref[g, :, 0:ts - BLOCK] = tail_v[g, :, 0:ts - BLOCK].astype(bf16)
                dk_ref[g, :, last] = (tail_k[g, :, last] + add_k).astype(bf16)
                dv_ref[g, :, last] = (tail_v[g, :, last] + add_v).astype(bf16)

        @pl.when(n < nsteps)
        def _():
            tail_k[...] = dk_s[:, :, BLOCK:]
            tail_v[...] = dv_s[:, :, BLOCK:]

        @pl.when(n == nsteps)
        def _():
            row = lax.broadcasted_iota(jnp.int32, (SWA_HEADS, 128), 0)
            out = jnp.zeros((SWA_HEADS, 128), f32)
            for h in range(SWA_HEADS):
                g, hh = divmod(h, SWA_GROUP)
                val = jnp.sum(sk_s[g, :, hh * BLOCK:(hh + 1) * BLOCK], axis=1, keepdims=True)
                out = jnp.where(row == h, val, out)
            dsink_ref[...] = out

    last_step = nsteps - 1

    def cl(n):
        return jnp.minimum(n, last_step)

    def prev_blk(n):
        return jnp.maximum(cl(n) * nb - 1, 0)

    feat8 = pl.BlockSpec((SWA_HEADS, HEAD_DIM, ts), lambda n: (0, 0, cl(n)))
    rows8 = pl.BlockSpec((SWA_HEADS, 1, ts), lambda n: (0, 0, cl(n)))
    cur = pl.BlockSpec((SWA_KV_HEADS, ts, HEAD_DIM), lambda n: (0, cl(n), 0))
    prev = pl.BlockSpec((SWA_KV_HEADS, BLOCK, HEAD_DIM), lambda n: (0, prev_blk(n), 0))
    curt = pl.BlockSpec((SWA_KV_HEADS, HEAD_DIM, ts), lambda n: (0, 0, cl(n)))
    prevt = pl.BlockSpec((SWA_KV_HEADS, HEAD_DIM, BLOCK), lambda n: (0, 0, prev_blk(n)))
    bspec = pl.BlockSpec((SWA_KV_HEADS, 2 * BLOCK, SWA_W), lambda n: (0, 0, 0))
    kvout = pl.BlockSpec((SWA_KV_HEADS, HEAD_DIM, ts), lambda n: (0, 0, jnp.maximum(n - 1, 0)))
    return pl.pallas_call(
        body,
        name="swa_bwd",
        grid=(nsteps + 1,),
        in_specs=[feat8, cur, prev, curt, prevt, cur, prev, feat8, rows8, rows8, bspec, bspec,
                  pl.BlockSpec(memory_space=pltpu.SMEM)],
        out_specs=[feat8, kvout, kvout, bspec, pl.BlockSpec((SWA_HEADS, 128), lambda n: (0, 0))],
        out_shape=[_sds((SWA_HEADS, HEAD_DIM, s_len), bf16), _sds((SWA_KV_HEADS, HEAD_DIM, s_len), bf16),
                   _sds((SWA_KV_HEADS, HEAD_DIM, s_len), bf16),
                   _sds((SWA_KV_HEADS, 2 * BLOCK, SWA_W), f32), _sds((SWA_HEADS, 128), f32)],
        scratch_shapes=[pltpu.VMEM((SWA_KV_HEADS, HEAD_DIM, ts + BLOCK), f32),
                        pltpu.VMEM((SWA_KV_HEADS, HEAD_DIM, ts + BLOCK), f32),
                        pltpu.VMEM((SWA_KV_HEADS, HEAD_DIM, ts), f32),
                        pltpu.VMEM((SWA_KV_HEADS, HEAD_DIM, ts), f32),
                        pltpu.VMEM((SWA_KV_HEADS, 1, SWA_W), f32)],
        compiler_params=_params(("arbitrary",)),
    )(qt, k, k, kt, kt, v, v, dot, lse, dl, bias_t, bias0_t, sink)


def _dproj_specs(tm):
    half = pl.BlockSpec((tm, 512), lambda i: (i, 0))
    feat = pl.BlockSpec((512, tm), lambda i: (0, i))
    feat_kv = pl.BlockSpec((128, tm), lambda i: (0, i))
    return [feat, feat, feat, half, feat, feat_kv, feat_kv, half, feat_kv]


def _dx_exchange_call(dh, pieces, w_t, bs, tm):
    s_len = dh.shape[0]
    n = len(bs)
    last = s_len // tm - 1

    def body(*refs):
        dh_ref, dqf_ref, dkf_ref, dvf_ref, dfz_ref, dqs_ref, dks_ref, dvs_ref, dsz_ref, dfft_ref, w_ref = refs[:11]
        b_refs = refs[11:11 + n]
        dx_ref = refs[11 + n]
        r_refs = refs[12 + n:12 + 2 * n]
        sems = refs[12 + 2 * n:]
        i = pl.program_id(0)

        @pl.when(i == 0)
        def _():
            _exchange_start(b_refs, r_refs, sems)

        def tr(ref):
            return ref[...].astype(f32).T.astype(bf16)

        dp = jnp.concatenate([tr(dqf_ref), tr(dkf_ref), tr(dvf_ref), dfz_ref[...], tr(dqs_ref), tr(dks_ref),
                              tr(dvs_ref), dsz_ref[...], tr(dfft_ref)], axis=1)
        dx_ref[...] = ALPHA * dh_ref[...] + jnp.dot(dp, w_ref[...], preferred_element_type=f32)

        @pl.when(i == last)
        def _():
            _exchange_wait(b_refs, r_refs, sems)

    fullw = pl.BlockSpec((tm, D_MODEL), lambda i: (i, 0))
    any_spec = pl.BlockSpec(memory_space=pl.ANY)
    out = pl.pallas_call(
        body,
        name="dx_bwd_exchange",
        grid=(s_len // tm,),
        in_specs=[fullw] + _dproj_specs(tm) + [pl.BlockSpec((A_W, D_MODEL), lambda i: (0, 0))] + [any_spec] * n,
        out_specs=[fullw] + [any_spec] * n,
        out_shape=[_sds((s_len, D_MODEL), f32)] + [_sds(b.shape, b.dtype) for b in bs],
        scratch_shapes=[pltpu.SemaphoreType.DMA((7 * n,)), pltpu.SemaphoreType.DMA((7 * n,)),
                        pltpu.SemaphoreType.DMA((n,))],
        compiler_params=_params(("arbitrary",)),
    )(dh, *pieces, w_t, *bs)
    return out[0], out[1:]


DW_STAGE_ROWS = 384


def _dw_call(x2, pieces, tm):
    s_len = x2.shape[0]
    nt = s_len // tm

    def body(x_ref, dqf_ref, dkf_ref, dvf_ref, dfz_ref, dqs_ref, dks_ref, dvs_ref, dsz_ref, dfft_ref, dw_ref,
             acc_ref, stage_ref, sem):
        i = pl.program_id(0)

        @pl.when(i == 0)
        def _():
            acc_ref[...] = jnp.zeros_like(acc_ref)

        xb = x_ref[...].astype(bf16)

        def add_feat(off, lhs):
            acc_ref[off:off + lhs.shape[0], :] += jnp.dot(lhs, xb, preferred_element_type=f32)

        def add_rows(off, piece):
            acc_ref[off:off + piece.shape[1], :] += lax.dot_general(piece, xb, TN, preferred_element_type=f32)

        add_feat(A_FQ, dqf_ref[...])
        add_feat(A_FK, dkf_ref[...])
        add_feat(A_FV, dvf_ref[...])
        add_rows(A_FZ, dfz_ref[...])
        add_feat(A_SQ, dqs_ref[...])
        add_feat(A_SK, dks_ref[...])
        add_feat(A_SV, dvs_ref[...])
        add_rows(A_SZ, dsz_ref[...])
        add_feat(A_FF, dfft_ref[...].astype(bf16))

        @pl.when(i == nt - 1)
        def _():
            for r in range(A_W // DW_STAGE_ROWS):
                rows = slice(r * DW_STAGE_ROWS, (r + 1) * DW_STAGE_ROWS)
                stage_ref[...] = acc_ref[rows, :].astype(bf16)
                cp = pltpu.make_async_copy(stage_ref, dw_ref.at[rows, :], sem)
                cp.start()
                cp.wait()

    return pl.pallas_call(
        body,
        name="dw_in_bwd",
        grid=(nt,),
        in_specs=[pl.BlockSpec((tm, D_MODEL), lambda i: (i, 0))] + _dproj_specs(tm),
        out_specs=pl.BlockSpec(memory_space=pl.ANY),
        out_shape=_sds((A_W, D_MODEL), bf16),
        scratch_shapes=[pltpu.VMEM((A_W, D_MODEL), f32), pltpu.VMEM((DW_STAGE_ROWS, D_MODEL), bf16),
                        pltpu.SemaphoreType.DMA],
        compiler_params=_params(("arbitrary",), VMEM_LIMIT_BIG),
    )(x2, *pieces)


def _adam_call(recv, w, m, v, tc, name):
    rows, cols = w.shape

    def body(r_ref, w_ref, m_ref, v_ref, g_ref, d_ref, mo_ref, vo_ref):
        g = r_ref[0].astype(f32)
        for p in range(1, N_DEV):
            g = g + r_ref[p].astype(f32)
        mn = ADAM_B1 * m_ref[...] + (1.0 - ADAM_B1) * g
        vn = ADAM_B2 * v_ref[...] + (1.0 - ADAM_B2) * (g * g)
        m_hat = mn / (1.0 - ADAM_B1 ** ADAM_STEP)
        v_hat = vn / (1.0 - ADAM_B2 ** ADAM_STEP)
        g_ref[...] = g
        d_ref[...] = -ADAM_LR * (m_hat / (jnp.sqrt(v_hat) + ADAM_EPS) + ADAM_WD * w_ref[...])
        mo_ref[...] = mn
        vo_ref[...] = vn

    blk = pl.BlockSpec((rows, tc), lambda i: (0, i))
    return pl.pallas_call(
        body,
        name=name,
        grid=(cols // tc,),
        in_specs=[pl.BlockSpec((N_DEV, rows, tc), lambda i: (0, 0, i)), blk, blk, blk],
        out_specs=[blk] * 4,
        out_shape=[_sds((rows, cols), f32)] * 4,
        compiler_params=_params(("arbitrary",)),
    )(recv, w, m, v)


def _pad_cols(a, width=128):
    return jnp.pad(a, ((0, 0), (0, width - a.shape[1])))


def _pack_small(ln_g, ln_b, rel, b_f, sink):
    return jnp.concatenate([
        ln_g.reshape(8, 128), ln_b.reshape(8, 128), _pad_cols(rel),
        jnp.pad(_pad_cols(b_f), ((0, 7), (0, 0))), jnp.pad(_pad_cols(sink), ((0, 7), (0, 0)))], axis=0)


def _unpack_small(p):
    return (p[0:8].reshape(1, D_MODEL), p[8:16].reshape(1, D_MODEL), p[16:48, 0:8], p[48:49, 0:8], p[56:57, 0:8])


def kernel(x, w_in, b_f, rel_bias, sink, w_o, ln_g, ln_b, loss_target, m_w_in, m_b_f, m_rel_bias, m_sink, m_w_o, m_ln_g, m_ln_b, v_w_in, v_b_f, v_rel_bias, v_sink, v_w_o, v_ln_g, v_ln_b):
    x2 = x[0]
    tgt = loss_target[0]
    s_len = x2.shape[0]
    shard = w_in.shape[2]

    w_in_t = jnp.transpose(w_in[0])
    g_in, g_o = _gather_call([w_in_t.astype(bf16), w_o[0].astype(bf16)])
    wt_full = g_in.reshape(N_DEV * shard, D_MODEL)
    w_t = jnp.concatenate([wt_full[:O_FF0], wt_full[O_FF1:], wt_full[O_FF0:O_FF1],
                           jnp.zeros((A_W - D_IN, D_MODEL), bf16)], axis=0)
    wo_full = g_o.reshape(D_MODEL, D_MODEL)

    qft, kf, vf, fz, qst, ks, vs, sz, fft, vat, kst, vsta, kft = _proj_call(x2, w_t, 512)
    cum, sgm = _cum_call(fft, b_f.reshape(FOX_HEADS, 1))
    qat, ka, kat, tile_stats = _augment_call(qft, kf, kft, cum.reshape(FOX_HEADS, 1, s_len), 2048)
    npairs, pair_q, pair_k = _fox_prune_tables(tile_stats)
    o_ft, lse_f = _fox_fwd_call(qat, ka, vat, npairs, pair_q, pair_k)
    bucket_t = jnp.asarray(_t5_bucket_table().T)
    bias_t, bias0_t = _swa_bias_call(rel_bias, bucket_t)
    sink_v = sink.reshape(SWA_HEADS)
    o_st, lse_s = _swa_fwd_call(qst, ks, vsta, bias_t, bias0_t, sink_v)

    (dh, do_f, dfz, do_s, dsz, dl_f, dl_s, dwo, dg, db, loss_part) = _post_call(
        o_ft.reshape(FOX_HEADS * HEAD_DIM, s_len), fz, o_st.reshape(SWA_HEADS * HEAD_DIM, s_len), sz, x2, tgt,
        wo_full, ln_g, ln_b, jnp.asarray(_head_selector()).astype(bf16), 256)

    dqf, dkf, dvf, dcq, dck = _fox_bwd_call(ka, kat, vf, qat, do_f.reshape(FOX_HEADS, HEAD_DIM, s_len), lse_f,
                                            dl_f.reshape(FOX_HEADS, 1, s_len), npairs, pair_q, pair_k)
    dqf, dkf, dvf = (a.reshape(FOX_HEADS * HEAD_DIM, s_len) for a in (dqf, dkf, dvf))
    dfft, dbf = _cum_bwd_call(dcq.reshape(FOX_HEADS, s_len), dck.reshape(FOX_HEADS, s_len), sgm)
    dqs, dks, dvs, dbias, dsink = _swa_bwd_call(
        qst, ks, kst, vs, do_s.reshape(SWA_HEADS, HEAD_DIM, s_len), lse_s, dl_s.reshape(SWA_HEADS, 1, s_len),
        bias_t, bias0_t, sink_v)
    dqs = dqs.reshape(SWA_HEADS * HEAD_DIM, s_len)
    dks, dvs = (a.reshape(SWA_KV_HEADS * HEAD_DIM, s_len) for a in (dks, dvs))
    drel = _swa_bias_bwd_call(dbias, bucket_t)

    pieces = (dqf, dkf, dvf, dfz, dqs, dks, dvs, dsz, dfft)
    dw_t = _dw_call(x2, pieces, 1024)

    dwt_full = jnp.concatenate([dw_t[:O_FF0], dw_t[A_FF:A_FF + (O_FF1 - O_FF0)], dw_t[O_FF0:A_FF]], axis=0)
    dw_blocks = dwt_full.reshape(N_DEV, shard, D_MODEL)
    dwo_blocks = dwo.reshape(N_DEV, D_MODEL // N_DEV, D_MODEL).astype(bf16)
    small = _pack_small(dg, db, drel[:, 0:8], dbf[:, 0].reshape(1, 8), dsink[:, 0].reshape(1, 8))
    loss_slot = np.zeros((64, 128), bool)
    loss_slot[49, 0] = True
    small = jnp.where(jnp.asarray(loss_slot), loss_part[0, 0], small)
    small_blocks = jnp.broadcast_to(small[None], (N_DEV,) + small.shape)
    dx, (r_in, r_o, r_small) = _dx_exchange_call(dh, pieces, w_t, [dw_blocks, dwo_blocks, small_blocks], 256)

    win_t = [jnp.transpose(a) for a in _adam_call(
        r_in, w_in_t, jnp.transpose(m_w_in[0]), jnp.transpose(v_w_in[0]), 256, "adam_w_in")]
    g_win, d_win, nm_win, nv_win = win_t
    g_wo, d_wo, nm_wo, nv_wo = _adam_call(r_o, w_o[0], m_w_o[0], v_w_o[0], 256, "adam_w_o")
    p_w = _pack_small(ln_g, ln_b, rel_bias, b_f, sink)
    p_m = _pack_small(m_ln_g, m_ln_b, m_rel_bias, m_b_f, m_sink)
    p_v = _pack_small(v_ln_g, v_ln_b, v_rel_bias, v_b_f, v_sink)
    g_p, d_p, nm_p, nv_p = _adam_call(r_small, p_w, p_m, p_v, 128, "adam_small")

    loss = g_p[49, 0]
    g_lng, g_lnb, g_rel, g_bf, g_sink = _unpack_small(g_p)
    d_lng, d_lnb, d_rel, d_bf, d_sink = _unpack_small(d_p)
    m_lng, m_lnb, m_rel, m_bf, m_sk = _unpack_small(nm_p)
    v_lng, v_lnb, v_rel, v_bf, v_sk = _unpack_small(nv_p)
    return (loss, dx[None], g_win[None], g_bf, g_rel, g_sink, g_wo[None], g_lng, g_lnb,
            d_win[None], d_bf, d_rel, d_sink, d_wo[None], d_lng, d_lnb,
            nm_win[None], m_bf, m_rel, m_sk, nm_wo[None], m_lng, m_lnb,
            nv_win[None], v_bf, v_rel, v_sk, nv_wo[None], v_lng, v_lnb)
```

```python
import functools
import math

import numpy as np
import jax
import jax.numpy as jnp
from jax import lax
from jax.experimental import pallas as pl
from jax.experimental.pallas import tpu as pltpu

f32 = jnp.float32
bf16 = jnp.bfloat16

D_MODEL = 1024
HEAD_DIM = 64
FOX_HEADS = 8
SWA_HEADS = 8
SWA_KV_HEADS = 2
SWA_GROUP = 4
BLOCK = 128
NUM_BUCKETS = 32
MAX_DISTANCE = 128
LN_EPS = 1e-5
NEG_INF = -1e30
ALPHA = 2.0 ** 0.25
SCALE = 1.0 / math.sqrt(HEAD_DIM)
D_IN = 3336

ADAM_LR = 0.001
ADAM_B1 = 0.9
ADAM_B2 = 0.999
ADAM_EPS = 1e-08
ADAM_WD = 0.01
ADAM_STEP = 10

N_DEV = 8
A_FQ, A_FK, A_FV, A_FZ, A_SQ, A_SK, A_SV, A_SZ, A_FF, A_W = 0, 512, 1024, 1536, 2048, 2560, 2688, 2816, 3328, 3456
O_FF0, O_FF1 = 1536, 1544

VMEM_LIMIT = 48 * 1024 * 1024
HIGHEST = lax.Precision.HIGHEST
NT = (((1,), (1,)), ((), ()))
TN = (((0,), (0,)), ((), ()))
MESH = pl.DeviceIdType.MESH
RELS = [(0, 0, 1), (0, 1, 0), (0, 1, 1), (1, 0, 0), (1, 0, 1), (1, 1, 0), (1, 1, 1)]


VMEM_LIMIT_BIG = 60 * 1024 * 1024


def _params(sem=None, vmem=VMEM_LIMIT):
    return pltpu.CompilerParams(dimension_semantics=sem, vmem_limit_bytes=vmem)


def _sds(shape, dtype):
    return jax.ShapeDtypeStruct(shape, dtype)


def _t5_bucket_table():
    qi = np.arange(BLOCK)[:, None]
    kj = np.arange(2 * BLOCK)[None, :]
    rel = qi + BLOCK - kj
    band = (rel >= 0) & (rel < BLOCK)
    relc = np.maximum(rel, 0)
    max_exact = NUM_BUCKETS // 2
    relf = np.maximum(relc, 1).astype(np.float32)
    large = max_exact + (np.log(relf / np.float32(max_exact)) / np.float32(math.log(MAX_DISTANCE / max_exact))
                         * np.float32(NUM_BUCKETS - max_exact)).astype(np.int32)
    large = np.minimum(large, NUM_BUCKETS - 1)
    bucket = np.where(relc < max_exact, relc, large).astype(np.int32)
    bucket = np.where(band, bucket, -1).astype(np.int32)
    return bucket


def _mesh_pos():
    return lax.axis_index("x"), lax.axis_index("y"), lax.axis_index("c")


def _dev_index(p):
    return 4 * p[0] + 2 * p[1] + p[2]


def _gather_call(xs):
    n = len(xs)

    def body(*refs):
        x_refs, o_refs = refs[:n], refs[n:2 * n]
        send_sems, recv_sems, local_sems = refs[2 * n:]
        x, y, c = _mesh_pos()
        me, sib = (x, y, c), (x, y, 1 - c)
        chips = [(1 - x, y), (x, 1 - y), (1 - x, 1 - y)]

        def copy(a, k, block, to, src=None):
            slot = o_refs[a].at[_dev_index(block)]
            return pltpu.make_async_remote_copy(
                src_ref=slot if src is None else src, dst_ref=slot,
                send_sem=send_sems.at[a * 7 + k], recv_sem=recv_sems.at[a * 7 + k],
                device_id=to, device_id_type=MESH)

        mine = [pltpu.make_async_copy(x_refs[a], o_refs[a].at[_dev_index(me)], local_sems.at[a]) for a in range(n)]
        for cp in mine:
            cp.start()
        first = []
        for a in range(n):
            first.append(copy(a, 0, me, sib, src=x_refs[a]))
            first += [copy(a, 1 + j, me, (*chip, c), src=x_refs[a]) for j, chip in enumerate(chips)]
        for cp in first:
            cp.start()
        passed = []
        for j, chip in enumerate(chips):
            for a in range(n):
                copy(a, 1 + j, (*chip, c), me).wait_recv()
                fwd = copy(a, 4 + j, (*chip, c), sib)
                fwd.start()
                passed.append(fwd)
        for a in range(n):
            copy(a, 0, sib, me).wait_recv()
            for j, chip in enumerate(chips):
                copy(a, 4 + j, (*chip, 1 - c), me).wait_recv()
        for cp in first + passed:
            cp.wait_send()
        for cp in mine:
            cp.wait()

    any_spec = pl.BlockSpec(memory_space=pl.ANY)
    return pl.pallas_call(
        body,
        name="gather_weights",
        out_shape=[_sds((N_DEV,) + a.shape, a.dtype) for a in xs],
        in_specs=[any_spec] * n,
        out_specs=[any_spec] * n,
        scratch_shapes=[pltpu.SemaphoreType.DMA((7 * n,)), pltpu.SemaphoreType.DMA((7 * n,)),
                        pltpu.SemaphoreType.DMA((n,))],
    )(*xs)


def _exchange_copies(b_refs, r_refs, send_sems, recv_sems, local_sems, incoming):
    n = len(b_refs)
    x, y, c = _mesh_pos()
    me_idx = _dev_index((x, y, c))
    mine = [pltpu.make_async_copy(b_refs[a].at[me_idx], r_refs[a].at[me_idx], local_sems.at[a]) for a in range(n)]
    remote = []
    for k, r in enumerate(RELS):
        peer = ((1 - x) if r[0] else x, (1 - y) if r[1] else y, (1 - c) if r[2] else c)
        pidx = _dev_index(peer)
        for a in range(n):
            remote.append(pltpu.make_async_remote_copy(
                src_ref=b_refs[a].at[pidx], dst_ref=r_refs[a].at[pidx if incoming else me_idx],
                send_sem=send_sems.at[a * 7 + k], recv_sem=recv_sems.at[a * 7 + k],
                device_id=peer, device_id_type=MESH))
    return mine, remote


def _exchange_start(b_refs, r_refs, sems):
    mine, out = _exchange_copies(b_refs, r_refs, *sems, incoming=False)
    for cp in mine + out:
        cp.start()


def _exchange_wait(b_refs, r_refs, sems):
    mine, inc = _exchange_copies(b_refs, r_refs, *sems, incoming=True)
    for cp in inc:
        cp.wait_recv()
    for cp in inc:
        cp.wait_send()
    for cp in mine:
        cp.wait()


def _proj_call(x2, w_t, tm):
    s_len = x2.shape[0]

    def body(x_ref, w_ref, qft_ref, kft_ref, vf_ref, fz_ref, qst_ref, ks_ref, vs_ref, sz_ref, fft_ref, vat_ref,
             kst_ref, vsta_ref):
        xb = x_ref[...].astype(bf16)

        def seg_t(off, width):
            return lax.dot_general(w_ref[off:off + width, :], xb, NT, preferred_element_type=f32)

        def seg(off, width):
            return lax.dot_general(xb, w_ref[off:off + width, :], NT, preferred_element_type=f32)

        def put_heads(ref, acc, nheads):
            for h in range(nheads):
                ref[h] = acc[:, h * HEAD_DIM:(h + 1) * HEAD_DIM].astype(bf16)

        def put_heads_t(ref, acc_t, nheads, augment):
            for h in range(nheads):
                ref[h, 0:HEAD_DIM, :] = acc_t[h * HEAD_DIM:(h + 1) * HEAD_DIM, :].astype(bf16)
                if augment:
                    ref[h, HEAD_DIM:2 * HEAD_DIM, :] = ones_row

        ones_row = jnp.where(lax.broadcasted_iota(jnp.int32, (HEAD_DIM, tm), 0) == 0, 1.0, 0.0).astype(bf16)
        put_heads_t(vat_ref, seg_t(A_FV, 512), FOX_HEADS, True)
        put_heads_t(qft_ref, seg_t(A_FQ, 512) * SCALE, FOX_HEADS, False)
        put_heads_t(kft_ref, seg_t(A_FK, 512), FOX_HEADS, False)
        put_heads(vf_ref, seg(A_FV, 512), FOX_HEADS)
        fz_ref[...] = seg(A_FZ, 512)
        put_heads_t(qst_ref, seg_t(A_SQ, 512) * SCALE, SWA_HEADS, False)
        put_heads(ks_ref, seg(A_SK, 128), SWA_KV_HEADS)
        put_heads(vs_ref, seg(A_SV, 128), SWA_KV_HEADS)
        put_heads_t(kst_ref, seg_t(A_SK, 128), SWA_KV_HEADS, False)
        put_heads_t(vsta_ref, seg_t(A_SV, 128), SWA_KV_HEADS, True)
        sz_ref[...] = seg(A_SZ, 512)
        fft_ref[...] = seg(A_FF, 128).T[:FOX_HEADS, :]

    def heads(nh):
        return pl.BlockSpec((nh, tm, HEAD_DIM), lambda i: (0, i, 0))

    def feat(nh, rows):
        return pl.BlockSpec((nh, rows, tm), lambda i: (0, 0, i))

    wide = pl.BlockSpec((tm, 512), lambda i: (i, 0))
    return pl.pallas_call(
        body,
        name="proj_fwd",
        grid=(s_len // tm,),
        in_specs=[pl.BlockSpec((tm, D_MODEL), lambda i: (i, 0)), pl.BlockSpec((A_W, D_MODEL), lambda i: (0, 0))],
        out_specs=[feat(8, HEAD_DIM), feat(8, HEAD_DIM), heads(8), wide, feat(8, HEAD_DIM), heads(2), heads(2), wide,
                   pl.BlockSpec((FOX_HEADS, tm), lambda i: (0, i)),
                   feat(FOX_HEADS, 2 * HEAD_DIM), feat(2, HEAD_DIM), feat(2, 2 * HEAD_DIM)],
        out_shape=[_sds((8, HEAD_DIM, s_len), bf16)] * 2 + [_sds((8, s_len, HEAD_DIM), bf16)]
                  + [_sds((s_len, 512), f32), _sds((8, HEAD_DIM, s_len), bf16),
                     _sds((2, s_len, HEAD_DIM), bf16), _sds((2, s_len, HEAD_DIM), bf16), _sds((s_len, 512), f32),
                     _sds((FOX_HEADS, s_len), f32), _sds((FOX_HEADS, 2 * HEAD_DIM, s_len), bf16),
                     _sds((2, HEAD_DIM, s_len), bf16), _sds((2, 2 * HEAD_DIM, s_len), bf16)],
        compiler_params=_params(("arbitrary",)),
    )(x2, w_t)


AUG = 2 * HEAD_DIM


def _augment_call(q_t, k_t, cum_row, tm):
    nh, _, s_len = k_t.shape
    per_step = tm // FOX_T

    def body(qt_ref, kt_ref, c_ref, qat_ref, ka_ref, kat_ref, st_ref):
        c = c_ref[0]
        hi = c.astype(bf16).astype(f32)
        r1 = c - hi
        mid = r1.astype(bf16).astype(f32)
        lo = (r1 - mid).astype(bf16).astype(f32)
        row = lax.broadcasted_iota(jnp.int32, (HEAD_DIM, tm), 0)
        q_tail = jnp.where(row == 0, hi, jnp.where(row == 1, mid, jnp.where(row == 2, lo,
                           jnp.where(row < 6, 1.0, 0.0))))
        k_tail = jnp.where(row < 3, 1.0, jnp.where(row == 3, -hi, jnp.where(row == 4, -mid,
                           jnp.where(row == 5, -lo, 0.0))))
        qat_ref[0, 0:HEAD_DIM, :] = qt_ref[0]
        qat_ref[0, HEAD_DIM:AUG, :] = q_tail.astype(bf16)
        kat_ref[0, 0:HEAD_DIM, :] = kt_ref[0]
        kat_ref[0, HEAD_DIM:AUG, :] = k_tail.astype(bf16)
        qt = qt_ref[0].astype(f32)
        kt = kt_ref[0].astype(f32)
        ka_ref[0] = jnp.concatenate([kt, k_tail], axis=0).T.astype(bf16)
        qn2 = jnp.sum(qt * qt, axis=0, keepdims=True)
        kn2 = jnp.sum(kt * kt, axis=0, keepdims=True)
        sd = jnp.sum(qt * kt, axis=0, keepdims=True)
        srow = lax.broadcasted_iota(jnp.int32, (8, LANES), 0)
        for part in range(per_step):
            sl = slice(part * FOX_T, (part + 1) * FOX_T)
            vals = [jnp.sqrt(jnp.max(qn2[:, sl], axis=1, keepdims=True)),
                    jnp.sqrt(jnp.max(kn2[:, sl], axis=1, keepdims=True)),
                    jnp.min(sd[:, sl], axis=1, keepdims=True),
                    jnp.max(c[:, sl], axis=1, keepdims=True), jnp.min(c[:, sl], axis=1, keepdims=True)]
            out = jnp.zeros((8, LANES), f32)
            for r, val in enumerate(vals):
                out = jnp.where(srow == r, val, out)
            st_ref[0, part] = out

    tile_t = pl.BlockSpec((1, HEAD_DIM, tm), lambda h, i: (h, 0, i))
    return pl.pallas_call(
        body,
        name="fox_augment",
        grid=(nh, s_len // tm),
        in_specs=[tile_t, tile_t, pl.BlockSpec((1, 1, tm), lambda h, i: (h, 0, i))],
        out_specs=[pl.BlockSpec((1, AUG, tm), lambda h, i: (h, 0, i)),
                   pl.BlockSpec((1, tm, AUG), lambda h, i: (h, i, 0)),
                   pl.BlockSpec((1, AUG, tm), lambda h, i: (h, 0, i)),
                   pl.BlockSpec((1, per_step, 8, LANES), lambda h, i: (h, i, 0, 0))],
        out_shape=[_sds((nh, AUG, s_len), bf16), _sds((nh, s_len, AUG), bf16), _sds((nh, AUG, s_len), bf16),
                   _sds((nh, s_len // FOX_T, 8, LANES), f32)],
        compiler_params=_params(("arbitrary", "arbitrary")),
    )(q_t, k_t, cum_row)


EXP_ZERO_GAP = 110.0


def _fox_prune_tables(stats):
    s = stats[:, :, :, 0]
    qn, kn, sd, cmx, cmn = (s[:, :, r] for r in range(5))
    nt = s.shape[1]
    bound = qn[:, :, None] * kn[:, None, :] + (cmx[:, :, None] - cmn[:, None, :])
    margin = 2.0 + 1e-5 * (jnp.abs(cmx)[:, :, None] + jnp.abs(cmn)[:, None, :])
    qi = lax.broadcasted_iota(jnp.int32, (nt, nt), 0)
    kj = lax.broadcasted_iota(jnp.int32, (nt, nt), 1)
    skip = (bound + margin < sd[:, :, None] - EXP_ZERO_GAP) & (kj < qi)[None]
    first = jnp.sum(jnp.cumprod(skip.astype(jnp.int32), axis=2), axis=2)
    tiles = lax.broadcasted_iota(jnp.int32, (1, nt), 1)
    cnt = tiles - first
    ends = jnp.cumsum(cnt, axis=1)
    off = ends - cnt
    kmax = nt * (nt - 1) // 2
    k = lax.broadcasted_iota(jnp.int32, (1, kmax), 1)
    pair_q = jnp.minimum(jnp.sum((ends[:, None, :] <= k[:, :, None]).astype(jnp.int32), axis=2), nt - 1)
    hit = pair_q[:, :, None] == tiles[:, None, :]
    first_k = jnp.sum(jnp.where(hit, first[:, None, :], 0), axis=2)
    off_k = jnp.sum(jnp.where(hit, off[:, None, :], 0), axis=2)
    pair_k = jnp.clip(first_k + k - off_k, 0, nt - 1)
    return (ends[:, nt - 1].astype(jnp.int32), pair_q.reshape(-1).astype(jnp.int32),
            pair_k.reshape(-1).astype(jnp.int32))


CUM_CHUNK = 512


def _cum_call(fft, bf_col):
    s_len = fft.shape[1]
    ch = CUM_CHUNK

    def body(f_ref, b_ref, cum_ref, sg_ref):
        r = lax.broadcasted_iota(jnp.int32, (ch, ch), 0)
        c = lax.broadcasted_iota(jnp.int32, (ch, ch), 1)
        upper = (r <= c).astype(f32)
        carry = jnp.zeros((FOX_HEADS, 1), f32)
        for n in range(s_len // ch):
            z = f_ref[:, n * ch:(n + 1) * ch] + b_ref[...]
            logf = jnp.minimum(z, 0.0) - jnp.log1p(jnp.exp(-jnp.abs(z)))
            sg_ref[:, n * ch:(n + 1) * ch] = 1.0 / (1.0 + jnp.exp(z))
            cs = jnp.dot(logf, upper, precision=HIGHEST, preferred_element_type=f32) + carry
            cum_ref[:, n * ch:(n + 1) * ch] = cs
            carry = cs[:, ch - 1:ch]

    return pl.pallas_call(
        body,
        name="fox_cum_fwd",
        out_shape=[_sds((FOX_HEADS, s_len), f32)] * 2,
        compiler_params=_params(),
    )(fft, bf_col)


def _cum_bwd_call(dcq, dck, sg):
    s_len = sg.shape[1]
    ch = CUM_CHUNK
    nch = s_len // ch

    def body(q_ref, k_ref, sg_ref, dff_ref, dbf_ref):
        r = lax.broadcasted_iota(jnp.int32, (ch, ch), 0)
        c = lax.broadcasted_iota(jnp.int32, (ch, ch), 1)
        lower = (r >= c).astype(f32)
        dff_ref[...] = jnp.zeros_like(dff_ref)
        carry = jnp.zeros((FOX_HEADS, 1), f32)
        total = jnp.zeros((FOX_HEADS, 1), f32)
        for n in reversed(range(nch)):
            sl = slice(n * ch, (n + 1) * ch)
            dcum = q_ref[:, sl] - k_ref[:, sl]
            rs = jnp.dot(dcum, lower, precision=HIGHEST, preferred_element_type=f32) + carry
            carry = rs[:, 0:1]
            dff = rs * sg_ref[:, sl]
            dff_ref[0:FOX_HEADS, sl] = dff
            total = total + jnp.sum(dff, axis=1, keepdims=True)
        dbf_ref[...] = jnp.broadcast_to(total, (FOX_HEADS, 128))

    return pl.pallas_call(
        body,
        name="fox_cum_bwd",
        out_shape=[_sds((128, s_len), f32), _sds((FOX_HEADS, 128), f32)],
        compiler_params=_params(),
    )(dcq, dck, sg)


FOX_T = 512
LANES = 128


def _causal_keep(t):
    return lax.broadcasted_iota(jnp.int32, (t, t), 0) <= lax.broadcasted_iota(jnp.int32, (t, t), 1)


def _tile_cols(i, t):
    return pl.ds(pl.multiple_of(i * t, t), t)


def _fox_pair(n, nt, kmax, h, pq_ref, pk_ref):
    k = h * kmax + jnp.maximum(n - nt, 0)
    return jnp.where(n < nt, n, pq_ref[k]), jnp.where(n < nt, n, pk_ref[k])


def _fox_fwd_call(qat, ka, vat, npairs, pair_q, pair_k):
    nh, s_len, _ = ka.shape
    t = FOX_T
    nt = s_len // t
    kmax = nt * (nt - 1) // 2
    assert nt >= 2 and nt % 2 == 0

    def body(np_ref, pq_ref, pk_ref, qat_ref, ka_ref, vat_ref, o_ref, lse_ref, s0, s1, p0, p1, a0, a1, m_all, acc_all):
        h = pl.program_id(0)
        extra = np_ref[h]
        total = nt + extra
        m_all[...] = jnp.full(m_all.shape, NEG_INF, f32)
        acc_all[...] = jnp.zeros(acc_all.shape, f32)
        bufs = ((s0, p0, a0), (s1, p1, a1))

        def pair(n):
            return _fox_pair(n, nt, kmax, h, pq_ref, pk_ref)

        def scores(n, b, masked):
            i, j = pair(n)
            st = jnp.dot(ka_ref[0, _tile_cols(j, t), :], qat_ref[0, :, _tile_cols(i, t)], preferred_element_type=f32)
            if masked:
                st = jnp.where(_causal_keep(t), st, NEG_INF)
            bufs[b][0][...] = st

        def softmax(n, b):
            i, _ = pair(n)
            s_ref, p_ref, a_ref = bufs[b]
            for c in range(t // LANES):
                cols = slice(c * LANES, (c + 1) * LANES)
                mcols = pl.ds(pl.multiple_of(i * t + c * LANES, LANES), LANES)
                m_old = m_all[:, mcols]
                m_new = jnp.maximum(m_old, jnp.max(s_ref[:, cols], axis=0, keepdims=True))
                m_all[:, mcols] = m_new
                a_ref[:, cols] = jnp.exp(m_old - m_new)
                p_ref[:, cols] = jnp.exp(s_ref[:, cols] - m_new).astype(bf16)

        def accum(n, b):
            i, j = pair(n)
            cols = _tile_cols(i, t)
            acc_all[:, cols] = bufs[b][2][...] * acc_all[:, cols] + jnp.dot(
                vat_ref[0, :, _tile_cols(j, t)], bufs[b][1][...], preferred_element_type=f32)

        def step(n, b, masked):
            accum(n - 2, b)
            softmax(n - 1, 1 - b)
            scores(n, b, masked)

        scores(0, 0, True)
        scores(1, 1, True)
        softmax(0, 0)

        def diag_steps(d, _):
            n = 2 + 2 * d
            step(n, 0, True)
            step(n + 1, 1, True)
            return 0

        lax.fori_loop(0, (nt - 2) // 2, diag_steps, 0)

        def off_steps(d, _):
            n = nt + 2 * d
            step(n, 0, False)
            step(n + 1, 1, False)
            return 0

        lax.fori_loop(0, extra // 2, off_steps, 0)

        @pl.when(extra % 2 == 1)
        def _():
            step(total - 1, 0, False)
            softmax(total - 1, 0)
            accum(total - 2, 1)
            accum(total - 1, 0)

        @pl.when(extra % 2 == 0)
        def _():
            softmax(total - 1, 1)
            accum(total - 2, 0)
            accum(total - 1, 1)

        l = acc_all[HEAD_DIM:HEAD_DIM + 1, :]
        o_ref[0] = acc_all[0:HEAD_DIM, :] / l
        lse_ref[0] = m_all[...] + jnp.log(l)

    smem = pl.BlockSpec(memory_space=pltpu.SMEM)
    return pl.pallas_call(
        body,
        name="fox_fwd",
        grid=(nh,),
        in_specs=[smem, smem, smem,
                  pl.BlockSpec((1, AUG, s_len), lambda h: (h, 0, 0)),
                  pl.BlockSpec((1, s_len, AUG), lambda h: (h, 0, 0)),
                  pl.BlockSpec((1, AUG, s_len), lambda h: (h, 0, 0))],
        out_specs=[pl.BlockSpec((1, HEAD_DIM, s_len), lambda h: (h, 0, 0)),
                   pl.BlockSpec((1, 1, s_len), lambda h: (h, 0, 0))],
        out_shape=[_sds((nh, HEAD_DIM, s_len), f32), _sds((nh, 1, s_len), f32)],
        scratch_shapes=[pltpu.VMEM((t, t), f32), pltpu.VMEM((t, t), f32), pltpu.VMEM((t, t), bf16),
                        pltpu.VMEM((t, t), bf16), pltpu.VMEM((1, t), f32), pltpu.VMEM((1, t), f32),
                        pltpu.VMEM((1, s_len), f32), pltpu.VMEM((AUG, s_len), f32)],
        compiler_params=_params(("arbitrary",)),
    )(npairs, pair_q, pair_k, qat, ka, vat)


SWA_TS = 512


SWA_W = SWA_GROUP * BLOCK


def _swa_bias_call(rel_bias, bucket_t):
    def body(rb_ref, bk_ref, b_ref, b0_ref):
        bk = bk_ref[...]
        row = lax.broadcasted_iota(jnp.int32, (2 * BLOCK, BLOCK), 0)
        for h in range(SWA_HEADS):
            acc = jnp.full((2 * BLOCK, BLOCK), NEG_INF, f32)
            for b in range(NUM_BUCKETS):
                acc = jnp.where(bk == b, rb_ref[b, h], acc)
            g, hh = divmod(h, SWA_GROUP)
            b_ref[g, :, hh * BLOCK:(hh + 1) * BLOCK] = acc
            b0_ref[g, :, hh * BLOCK:(hh + 1) * BLOCK] = jnp.where(row < BLOCK, NEG_INF, acc)

    return pl.pallas_call(
        body,
        name="swa_bias",
        in_specs=[pl.BlockSpec(memory_space=pltpu.SMEM), pl.BlockSpec(memory_space=pltpu.VMEM)],
        out_shape=[_sds((SWA_KV_HEADS, 2 * BLOCK, SWA_W), f32)] * 2,
        compiler_params=_params(),
    )(rel_bias, bucket_t)


def _swa_bias_bwd_call(dbias, bucket_t):
    def body(d_ref, bk_ref, o_ref):
        bk = bk_ref[...]
        row = lax.broadcasted_iota(jnp.int32, (NUM_BUCKETS, 128), 0)
        col = lax.broadcasted_iota(jnp.int32, (NUM_BUCKETS, 128), 1)
        out = jnp.zeros((NUM_BUCKETS, 128), f32)
        for h in range(SWA_HEADS):
            g, hh = divmod(h, SWA_GROUP)
            d = d_ref[g, :, hh * BLOCK:(hh + 1) * BLOCK]
            for b in range(NUM_BUCKETS):
                val = jnp.sum(jnp.sum(jnp.where(bk == b, d, 0.0), axis=1, keepdims=True), axis=0, keepdims=True)
                out = jnp.where((row == b) & (col == h), val, out)
        o_ref[...] = out

    return pl.pallas_call(
        body,
        name="swa_bias_bwd",
        out_shape=_sds((NUM_BUCKETS, 128), f32),
        compiler_params=_params(),
    )(dbias, bucket_t)


def _sink_row(sink_ref, g):
    return jnp.concatenate([jnp.full((1, BLOCK), sink_ref[g * SWA_GROUP + hh], f32) for hh in range(SWA_GROUP)], axis=1)


def _group_lanes(ref, g, cols):
    return jnp.concatenate([ref[g * SWA_GROUP + hh, :, cols] for hh in range(SWA_GROUP)], axis=1)


def _swa_fwd_call(qt, k, vta, bias_t, bias0_t, sink):
    s_len = qt.shape[2]
    ts = SWA_TS
    nb = ts // BLOCK

    def body(qt_ref, kc_ref, kp_ref, vc_ref, vp_ref, b_ref, b0_ref, sink_ref, o_ref, lse_ref):
        first = pl.program_id(0) == 0
        kall = [jnp.concatenate([kp_ref[g], kc_ref[g]], axis=0) for g in range(SWA_KV_HEADS)]
        vall = [jnp.concatenate([vp_ref[g], vc_ref[g]], axis=1) for g in range(SWA_KV_HEADS)]
        sinks = [_sink_row(sink_ref, g) for g in range(SWA_KV_HEADS)]
        items = [(g, b) for g in range(SWA_KV_HEADS) for b in range(nb)]

        def scores(g, b):
            qg = _group_lanes(qt_ref, g, slice(b * BLOCK, (b + 1) * BLOCK))
            bias_b = b_ref[g]
            if b == 0:
                bias_b = jnp.where(first, b0_ref[g], bias_b)
            return jnp.dot(kall[g][b * BLOCK:(b + 2) * BLOCK], qg, preferred_element_type=f32) + bias_b

        def finish(g, b, st):
            m = jnp.maximum(jnp.max(st, axis=0, keepdims=True), sinks[g])
            pt = jnp.exp(st - m)
            acc = jnp.dot(vall[g][:, b * BLOCK:(b + 2) * BLOCK], pt.astype(bf16), preferred_element_type=f32)
            l = acc[HEAD_DIM:HEAD_DIM + 1, :] + jnp.exp(sinks[g] - m)
            return acc[0:HEAD_DIM, :] / l, m + jnp.log(l)

        outs, lses = {}, {}
        st_next = scores(*items[0])
        for idx, (g, b) in enumerate(items):
            st = st_next
            if idx + 1 < len(items):
                st_next = scores(*items[idx + 1])
            outs[g, b], lses[g, b] = finish(g, b, st)
        for g in range(SWA_KV_HEADS):
            for hh in range(SWA_GROUP):
                lanes = slice(hh * BLOCK, (hh + 1) * BLOCK)
                o_ref[g * SWA_GROUP + hh] = jnp.concatenate([outs[g, b][:, lanes] for b in range(nb)], axis=1)
                lse_ref[g * SWA_GROUP + hh] = jnp.concatenate([lses[g, b][:, lanes] for b in range(nb)], axis=1)

    def prev_blk(n):
        return jnp.maximum(n * nb - 1, 0)

    bspec = pl.BlockSpec((SWA_KV_HEADS, 2 * BLOCK, SWA_W), lambda n: (0, 0, 0))
    return pl.pallas_call(
        body,
        name="swa_fwd",
        grid=(s_len // ts,),
        in_specs=[pl.BlockSpec((SWA_HEADS, HEAD_DIM, ts), lambda n: (0, 0, n)),
                  pl.BlockSpec((SWA_KV_HEADS, ts, HEAD_DIM), lambda n: (0, n, 0)),
                  pl.BlockSpec((SWA_KV_HEADS, BLOCK, HEAD_DIM), lambda n: (0, prev_blk(n), 0)),
                  pl.BlockSpec((SWA_KV_HEADS, AUG, ts), lambda n: (0, 0, n)),
                  pl.BlockSpec((SWA_KV_HEADS, AUG, BLOCK), lambda n: (0, 0, prev_blk(n))),
                  bspec, bspec, pl.BlockSpec(memory_space=pltpu.SMEM)],
        out_specs=[pl.BlockSpec((SWA_HEADS, HEAD_DIM, ts), lambda n: (0, 0, n)),
                   pl.BlockSpec((SWA_HEADS, 1, ts), lambda n: (0, 0, n))],
        out_shape=[_sds((SWA_HEADS, HEAD_DIM, s_len), f32), _sds((SWA_HEADS, 1, s_len), f32)],
        compiler_params=_params(("arbitrary",)),
    )(qt, k, k, vta, vta, bias_t, bias0_t, sink)


def _head_selector():
    sel = np.zeros((512, 128), np.float32)
    for h in range(8):
        sel[h * HEAD_DIM:(h + 1) * HEAD_DIM, h] = 1.0
    return sel


def _post_call(of, fz, osw, sz, x2, tgt, wo, ln_g, ln_b, sel, tm):
    s_len = x2.shape[0]

    def body(of_ref, fz_ref, os_ref, sz_ref, x_ref, t_ref, wo_ref, g_ref, b_ref, sel_ref,
             dh_ref, dof_ref, dfz_ref, dos_ref, dsz_ref, dlf_ref, dls_ref, dwo_ref, dg_ref, db_ref, loss_ref):
        n = pl.program_id(0)

        @pl.when(n == 0)
        def _():
            dwo_ref[...] = jnp.zeros_like(dwo_ref)
            dg_ref[...] = jnp.zeros_like(dg_ref)
            db_ref[...] = jnp.zeros_like(db_ref)
            loss_ref[...] = jnp.zeros_like(loss_ref)

        gam = g_ref[...]
        sel_m = sel_ref[...]

        def forward(r):
            o_f = of_ref[:, r].T
            o_s = os_ref[:, r].T
            fz = fz_ref[r, :]
            sz = sz_ref[r, :]
            sg_f = jax.nn.sigmoid(fz)
            sg_s = jax.nn.sigmoid(sz)
            silu_f = fz * sg_f
            silu_s = sz * sg_s
            mixed = jnp.concatenate([o_f * silu_f, o_s * silu_s], axis=1).astype(bf16)
            y = jnp.dot(mixed, wo_ref[...], preferred_element_type=f32)
            return o_f, o_s, fz, sz, sg_f, sg_s, silu_f, silu_s, mixed, y

        def norm_and_back(r, fwd):
            mixed, y = fwd[8], fwd[9]
            h = ALPHA * x_ref[r, :] + y
            mu = jnp.mean(h, axis=1, keepdims=True)
            hc = h - mu
            var = jnp.mean(hc * hc, axis=1, keepdims=True)
            rstd = lax.rsqrt(var + LN_EPS)
            xhat = hc * rstd
            out = xhat * gam + b_ref[...]
            err = out - t_ref[r, :]
            tok_loss = jnp.mean(err * err, axis=1, keepdims=True)
            loss_ref[...] += 0.5 * jnp.sum(tok_loss, axis=0, keepdims=True)
            dout = err * (1.0 / D_MODEL)
            dg_ref[...] += jnp.sum(dout * xhat, axis=0, keepdims=True)
            db_ref[...] += jnp.sum(dout, axis=0, keepdims=True)
            dxh = dout * gam
            m1 = jnp.mean(dxh, axis=1, keepdims=True)
            m2 = jnp.mean(dxh * xhat, axis=1, keepdims=True)
            dh = rstd * (dxh - m1 - xhat * m2)
            dh_ref[r, :] = dh
            dyb = dh.astype(bf16)
            dmix = lax.dot_general(dyb, wo_ref[...], NT, preferred_element_type=f32)
            dwo_ref[...] += lax.dot_general(mixed, dyb, TN, preferred_element_type=f32)
            return dmix

        def head_sums(prod):
            hi = prod.astype(bf16)
            lo = (prod - hi.astype(f32)).astype(bf16)
            return (jnp.dot(hi, sel_m, preferred_element_type=f32) + jnp.dot(lo, sel_m, preferred_element_type=f32))

        def gates_back(r, fwd, dmix):
            o_f, o_s, fz, sz, sg_f, sg_s, silu_f, silu_s = fwd[:8]
            dm_f = dmix[:, :512]
            dm_s = dmix[:, 512:]
            do_f = dm_f * silu_f
            do_s = dm_s * silu_s
            dfz_ref[r, :] = (dm_f * o_f * (sg_f * (1.0 + fz * (1.0 - sg_f)))).astype(bf16)
            dsz_ref[r, :] = (dm_s * o_s * (sg_s * (1.0 + sz * (1.0 - sg_s)))).astype(bf16)
            dof_ref[:, r] = do_f.T.astype(bf16)
            dos_ref[:, r] = do_s.T.astype(bf16)
            dlf_ref[:, r] = head_sums(do_f * o_f).T[:FOX_HEADS, :]
            dls_ref[:, r] = head_sums(do_s * o_s).T[:SWA_HEADS, :]

        halves = [slice(k * (tm // 2), (k + 1) * (tm // 2)) for k in range(2)]
        fwds = [forward(r) for r in halves]
        dmixes = [norm_and_back(r, f) for r, f in zip(halves, fwds)]
        for r, f, d in zip(halves, fwds, dmixes):
            gates_back(r, f, d)

    feat = pl.BlockSpec((512, tm), lambda n: (0, n))
    rows8 = pl.BlockSpec((8, tm), lambda n: (0, n))
    half = pl.BlockSpec((tm, 512), lambda n: (n, 0))
    fullw = pl.BlockSpec((tm, D_MODEL), lambda n: (n, 0))
    vec = pl.BlockSpec((1, D_MODEL), lambda n: (0, 0))
    return pl.pallas_call(
        body,
        name="post_fwd_bwd",
        grid=(s_len // tm,),
        in_specs=[feat, half, feat, half, fullw, fullw,
                  pl.BlockSpec((D_MODEL, D_MODEL), lambda n: (0, 0)), vec, vec,
                  pl.BlockSpec((512, 128), lambda n: (0, 0))],
        out_specs=[fullw, feat, half, feat, half, rows8, rows8,
                   pl.BlockSpec((D_MODEL, D_MODEL), lambda n: (0, 0)), vec, vec,
                   pl.BlockSpec((1, 1), lambda n: (0, 0))],
        out_shape=[_sds((s_len, D_MODEL), f32), _sds((512, s_len), bf16), _sds((s_len, 512), bf16),
                   _sds((512, s_len), bf16), _sds((s_len, 512), bf16),
                   _sds((FOX_HEADS, s_len), f32), _sds((SWA_HEADS, s_len), f32),
                   _sds((D_MODEL, D_MODEL), f32), _sds((1, D_MODEL), f32), _sds((1, D_MODEL), f32),
                   _sds((1, 1), f32)],
        compiler_params=_params(("arbitrary",), VMEM_LIMIT_BIG),
    )(of, fz, osw, sz, x2, tgt, wo, ln_g, ln_b, sel)


def _fox_bwd_call(ka, kat, v, qat, dot, lse_row, dl_row, npairs, pair_q, pair_k):
    nh, s_len, _ = ka.shape
    t = FOX_T
    nt = s_len // t
    kmax = nt * (nt - 1) // 2
    assert nt >= 2 and nt % 2 == 0
    ck_slot = HEAD_DIM + 3
    cq_slot = HEAD_DIM

    def body(np_ref, pq_ref, pk_ref, ka_ref, kat_ref, v_ref, qat_ref, dot_ref, lse_ref, dl_ref,
             dq_ref, dk_ref, dv_ref, dcq_ref, dck_ref, dqt_all, dkat_all, dvt_all, p0, p1, ds0, ds1):
        h = pl.program_id(0)
        extra = np_ref[h]
        total = nt + extra
        dqt_all[...] = jnp.zeros(dqt_all.shape, f32)
        dkat_all[...] = jnp.zeros(dkat_all.shape, f32)
        dvt_all[...] = jnp.zeros(dvt_all.shape, f32)
        pbuf, dsbuf = (p0, p1), (ds0, ds1)

        def pair(n):
            return _fox_pair(n, nt, kmax, h, pq_ref, pk_ref)

        def probs(n, b, masked):
            i, j = pair(n)
            qc, kr = _tile_cols(i, t), _tile_cols(j, t)
            st = jnp.dot(ka_ref[0, kr, :], qat_ref[0, :, qc], preferred_element_type=f32)
            dpt = jnp.dot(v_ref[0, kr, :], dot_ref[0, :, qc], preferred_element_type=f32)
            if masked:
                st = jnp.where(_causal_keep(t), st, NEG_INF)
            pt = jnp.exp(st - lse_ref[0, :, qc])
            pbuf[b][...] = pt.astype(bf16)
            dsbuf[b][...] = (pt * (dpt - dl_ref[0, :, qc])).astype(bf16)

        def grads(n, b):
            i, j = pair(n)
            qc, kc = _tile_cols(i, t), _tile_cols(j, t)
            dvt_all[:, kc] += lax.dot_general(dot_ref[0, :, qc], pbuf[b][...], NT, preferred_element_type=f32)
            dkat_all[:, kc] += lax.dot_general(qat_ref[0, :, qc], dsbuf[b][...], NT, preferred_element_type=f32)
            dqt_all[:, qc] += jnp.dot(kat_ref[0, :, kc], dsbuf[b][...], preferred_element_type=f32)

        def step(n, b, masked):
            i, j = pair(n)
            qc, kr = _tile_cols(i, t), _tile_cols(j, t)
            i1, j1 = pair(n - 1)
            qc1, kc1 = _tile_cols(i1, t), _tile_cols(j1, t)
            c = 1 - b
            st = jnp.dot(ka_ref[0, kr, :], qat_ref[0, :, qc], preferred_element_type=f32)
            dvt_all[:, kc1] += lax.dot_general(dot_ref[0, :, qc1], pbuf[c][...], NT, preferred_element_type=f32)
            if masked:
                st = jnp.where(_causal_keep(t), st, NEG_INF)
            pt = jnp.exp(st - lse_ref[0, :, qc])
            pbuf[b][...] = pt.astype(bf16)
            dpt = jnp.dot(v_ref[0, kr, :], dot_ref[0, :, qc], preferred_element_type=f32)
            dkat_all[:, kc1] += lax.dot_general(qat_ref[0, :, qc1], dsbuf[c][...], NT, preferred_element_type=f32)
            dqt_all[:, qc1] += jnp.dot(kat_ref[0, :, kc1], dsbuf[c][...], preferred_element_type=f32)
            dsbuf[b][...] = (pt * (dpt - dl_ref[0, :, qc])).astype(bf16)

        probs(0, 0, True)
        step(1, 1, True)

        def diag_steps(d, _):
            n = 2 + 2 * d
            step(n, 0, True)
            step(n + 1, 1, True)
            return 0

        lax.fori_loop(0, (nt - 2) // 2, diag_steps, 0)

        def off_steps(d, _):
            n = nt + 2 * d
            step(n, 0, False)
            step(n + 1, 1, False)
            return 0

        lax.fori_loop(0, extra // 2, off_steps, 0)

        @pl.when(extra % 2 == 1)
        def _():
            step(total - 1, 0, False)
            grads(total - 1, 0)

        @pl.when(extra % 2 == 0)
        def _():
            grads(total - 1, 1)

        dq_ref[0] = (dqt_all[0:HEAD_DIM, :] * SCALE).astype(bf16)
        dk_ref[0] = dkat_all[0:HEAD_DIM, :].astype(bf16)
        dv_ref[0] = dvt_all[...].astype(bf16)
        dcq_ref[0] = dqt_all[cq_slot:cq_slot + 1, :]
        dck_ref[0] = dkat_all[ck_slot:ck_slot + 1, :]

    smem = pl.BlockSpec(memory_space=pltpu.SMEM)
    rows = pl.BlockSpec((1, s_len, AUG), lambda h: (h, 0, 0))
    feat = pl.BlockSpec((1, AUG, s_len), lambda h: (h, 0, 0))
    feat64 = pl.BlockSpec((1, HEAD_DIM, s_len), lambda h: (h, 0, 0))
    rowv = pl.BlockSpec((1, 1, s_len), lambda h: (h, 0, 0))
    return pl.pallas_call(
        body,
        name="fox_bwd",
        grid=(nh,),
        in_specs=[smem, smem, smem, rows, feat, pl.BlockSpec((1, s_len, HEAD_DIM), lambda h: (h, 0, 0)), feat, feat64,
                  rowv, rowv],
        out_specs=[feat64, feat64, feat64, rowv, rowv],
        out_shape=[_sds((nh, HEAD_DIM, s_len), bf16)] * 3 + [_sds((nh, 1, s_len), f32)] * 2,
        scratch_shapes=[pltpu.VMEM((AUG, s_len), f32), pltpu.VMEM((AUG, s_len), f32), pltpu.VMEM((HEAD_DIM, s_len), f32)]
                       + [pltpu.VMEM((t, t), bf16)] * 4,
        compiler_params=_params(("arbitrary",)),
    )(npairs, pair_q, pair_k, ka, kat, v, qat, dot, lse_row, dl_row)


def _swa_bwd_call(qt, k, kt, v, dot, lse, dl, bias_t, bias0_t, sink):
    s_len = qt.shape[2]
    ts = SWA_TS
    nb = ts // BLOCK
    nsteps = s_len // ts

    def body(qt_ref, kc_ref, kp_ref, ktc_ref, ktp_ref, vc_ref, vp_ref, dot_ref, lse_ref, dl_ref, b_ref, b0_ref,
             sink_ref, dq_ref, dk_ref, dv_ref, dbias_ref, dsink_ref, dk_s, dv_s, tail_k, tail_v, sk_s):
        n = pl.program_id(0)

        @pl.when(n == 0)
        def _():
            dbias_ref[...] = jnp.zeros_like(dbias_ref)
            sk_s[...] = jnp.zeros_like(sk_s)

        @pl.when(n < nsteps)
        def _():
            first = n == 0
            dk_s[...] = jnp.zeros_like(dk_s)
            dv_s[...] = jnp.zeros_like(dv_s)
            groups = range(SWA_KV_HEADS)
            kall = [jnp.concatenate([kp_ref[g], kc_ref[g]], axis=0) for g in groups]
            vall = [jnp.concatenate([vp_ref[g], vc_ref[g]], axis=0) for g in groups]
            ktall = [jnp.concatenate([ktp_ref[g], ktc_ref[g]], axis=1) for g in groups]
            sinks = [_sink_row(sink_ref, g) for g in groups]
            items = [(g, b) for g in groups for b in range(nb)]

            def products(g, b):
                cols = slice(b * BLOCK, (b + 1) * BLOCK)
                win = slice(b * BLOCK, (b + 2) * BLOCK)
                qg = _group_lanes(qt_ref, g, cols)
                dog = _group_lanes(dot_ref, g, cols)
                bias_b = b_ref[g]
                if b == 0:
                    bias_b = jnp.where(first, b0_ref[g], bias_b)
                st = jnp.dot(kall[g][win], qg, preferred_element_type=f32) + bias_b
                dpt = jnp.dot(vall[g][win], dog, preferred_element_type=f32)
                return qg, dog, st, dpt

            def finish(g, b, qg, dog, st, dpt):
                cols = slice(b * BLOCK, (b + 1) * BLOCK)
                win = slice(b * BLOCK, (b + 2) * BLOCK)
                lse_r = _group_lanes(lse_ref, g, cols)
                dl_r = _group_lanes(dl_ref, g, cols)
                pt = jnp.exp(st - lse_r)
                dst = pt * (dpt - dl_r)
                dsb = dst.astype(bf16)
                dk_s[g, :, win] += lax.dot_general(qg, dsb, NT, preferred_element_type=f32)
                dv_s[g, :, win] += lax.dot_general(dog, pt.astype(bf16), NT, preferred_element_type=f32)
                dqg = jnp.dot(ktall[g][:, win], dsb, preferred_element_type=f32) * SCALE
                return dqg, dst, -jnp.exp(sinks[g] - lse_r) * dl_r

            dqs, dsts, sks = {}, {}, {}
            nxt = products(*items[0])
            for idx, (g, b) in enumerate(items):
                cur = nxt
                if idx + 1 < len(items):
                    nxt = products(*items[idx + 1])
                dqs[g, b], dsts[g, b], sks[g, b] = finish(g, b, *cur)
            for g in groups:
                dbias_ref[g] += functools.reduce(lambda a, c: a + c, [dsts[g, b] for b in range(nb)])
                sk_s[g] += functools.reduce(lambda a, c: a + c, [sks[g, b] for b in range(nb)])
                for hh in range(SWA_GROUP):
                    lanes = slice(hh * BLOCK, (hh + 1) * BLOCK)
                    dq_ref[g * SWA_GROUP + hh] = jnp.concatenate(
                        [dqs[g, b][:, lanes] for b in range(nb)], axis=1).astype(bf16)

        @pl.when(n > 0)
        def _():
            last = slice(ts - BLOCK, ts)
            for g in range(SWA_KV_HEADS):
                add_k = jnp.where(n < nsteps, dk_s[g, :, 0:BLOCK], 0.0)
                add_v = jnp.where(n < nsteps, dv_s[g, :, 0:BLOCK], 0.0)
                dk_ref[g, :, 0:ts - BLOCK] = tail_k[g, :, 0:ts - BLOCK].astype(bf16)
                dv_ref[g, :, 0:ts - BLOCK] = tail_v[g, :, 0:ts - BLOCK].astype(bf16)
                dk_ref[g, :, last] = (tail_k[g, :, last] + add_k).astype(bf16)
                dv_ref[g, :, last] = (tail_v[g, :, last] + add_v).astype(bf16)

        @pl.when(n < nsteps)
        def _():
            tail_k[...] = dk_s[:, :, BLOCK:]
            tail_v[...] = dv_s[:, :, BLOCK:]

        @pl.when(n == nsteps)
        def _():
            row = lax.broadcasted_iota(jnp.int32, (SWA_HEADS, 128), 0)
            out = jnp.zeros((SWA_HEADS, 128), f32)
            for h in range(SWA_HEADS):
                g, hh = divmod(h, SWA_GROUP)
                val = jnp.sum(sk_s[g, :, hh * BLOCK:(hh + 1) * BLOCK], axis=1, keepdims=True)
                out = jnp.where(row == h, val, out)
            dsink_ref[...] = out

    last_step = nsteps - 1

    def cl(n):
        return jnp.minimum(n, last_step)

    def prev_blk(n):
        return jnp.maximum(cl(n) * nb - 1, 0)

    feat8 = pl.BlockSpec((SWA_HEADS, HEAD_DIM, ts), lambda n: (0, 0, cl(n)))
    rows8 = pl.BlockSpec((SWA_HEADS, 1, ts), lambda n: (0, 0, cl(n)))
    cur = pl.BlockSpec((SWA_KV_HEADS, ts, HEAD_DIM), lambda n: (0, cl(n), 0))
    prev = pl.BlockSpec((SWA_KV_HEADS, BLOCK, HEAD_DIM), lambda n: (0, prev_blk(n), 0))
    curt = pl.BlockSpec((SWA_KV_HEADS, HEAD_DIM, ts), lambda n: (0, 0, cl(n)))
    prevt = pl.BlockSpec((SWA_KV_HEADS, HEAD_DIM, BLOCK), lambda n: (0, 0, prev_blk(n)))
    bspec = pl.BlockSpec((SWA_KV_HEADS, 2 * BLOCK, SWA_W), lambda n: (0, 0, 0))
    kvout = pl.BlockSpec((SWA_KV_HEADS, HEAD_DIM, ts), lambda n: (0, 0, jnp.maximum(n - 1, 0)))
    return pl.pallas_call(
        body,
        name="swa_bwd",
        grid=(nsteps + 1,),
        in_specs=[feat8, cur, prev, curt, prevt, cur, prev, feat8, rows8, rows8, bspec, bspec,
                  pl.BlockSpec(memory_space=pltpu.SMEM)],
        out_specs=[feat8, kvout, kvout, bspec, pl.BlockSpec((SWA_HEADS, 128), lambda n: (0, 0))],
        out_shape=[_sds((SWA_HEADS, HEAD_DIM, s_len), bf16), _sds((SWA_KV_HEADS, HEAD_DIM, s_len), bf16),
                   _sds((SWA_KV_HEADS, HEAD_DIM, s_len), bf16),
                   _sds((SWA_KV_HEADS, 2 * BLOCK, SWA_W), f32), _sds((SWA_HEADS, 128), f32)],
        scratch_shapes=[pltpu.VMEM((SWA_KV_HEADS, HEAD_DIM, ts + BLOCK), f32),
                        pltpu.VMEM((SWA_KV_HEADS, HEAD_DIM, ts + BLOCK), f32),
                        pltpu.VMEM((SWA_KV_HEADS, HEAD_DIM, ts), f32),
                        pltpu.VMEM((SWA_KV_HEADS, HEAD_DIM, ts), f32),
                        pltpu.VMEM((SWA_KV_HEADS, 1, SWA_W), f32)],
        compiler_params=_params(("arbitrary",)),
    )(qt, k, k, kt, kt, v, v, dot, lse, dl, bias_t, bias0_t, sink)


def _dproj_specs(tm):
    half = pl.BlockSpec((tm, 512), lambda i: (i, 0))
    feat = pl.BlockSpec((512, tm), lambda i: (0, i))
    feat_kv = pl.BlockSpec((128, tm), lambda i: (0, i))
    return [feat, feat, feat, half, feat, feat_kv, feat_kv, half, feat_kv]


def _dx_exchange_call(dh, pieces, w_t, bs, tm):
    s_len = dh.shape[0]
    n = len(bs)
    last = s_len // tm - 1

    def body(*refs):
        dh_ref, dqf_ref, dkf_ref, dvf_ref, dfz_ref, dqs_ref, dks_ref, dvs_ref, dsz_ref, dfft_ref, w_ref = refs[:11]
        b_refs = refs[11:11 + n]
        dx_ref = refs[11 + n]
        r_refs = refs[12 + n:12 + 2 * n]
        sems = refs[12 + 2 * n:]
        i = pl.program_id(0)

        @pl.when(i == 0)
        def _():
            _exchange_start(b_refs, r_refs, sems)

        def tr(ref):
            return ref[...].astype(f32).T.astype(bf16)

        dp = jnp.concatenate([tr(dqf_ref), tr(dkf_ref), tr(dvf_ref), dfz_ref[...], tr(dqs_ref), tr(dks_ref),
                              tr(dvs_ref), dsz_ref[...], tr(dfft_ref)], axis=1)
        dx_ref[...] = ALPHA * dh_ref[...] + jnp.dot(dp, w_ref[...], preferred_element_type=f32)

        @pl.when(i == last)
        def _():
            _exchange_wait(b_refs, r_refs, sems)

    fullw = pl.BlockSpec((tm, D_MODEL), lambda i: (i, 0))
    any_spec = pl.BlockSpec(memory_space=pl.ANY)
    out = pl.pallas_call(
        body,
        name="dx_bwd_exchange",
        grid=(s_len // tm,),
        in_specs=[fullw] + _dproj_specs(tm) + [pl.BlockSpec((A_W, D_MODEL), lambda i: (0, 0))] + [any_spec] * n,
        out_specs=[fullw] + [any_spec] * n,
        out_shape=[_sds((s_len, D_MODEL), f32)] + [_sds(b.shape, b.dtype) for b in bs],
        scratch_shapes=[pltpu.SemaphoreType.DMA((7 * n,)), pltpu.SemaphoreType.DMA((7 * n,)),
                        pltpu.SemaphoreType.DMA((n,))],
        compiler_params=_params(("arbitrary",)),
    )(dh, *pieces, w_t, *bs)
    return out[0], out[1:]


DW_STAGE_ROWS = 384


def _dw_call(x2, pieces, tm):
    s_len = x2.shape[0]
    nt = s_len // tm

    def body(x_ref, dqf_ref, dkf_ref, dvf_ref, dfz_ref, dqs_ref, dks_ref, dvs_ref, dsz_ref, dfft_ref, dw_ref,
             acc_ref, stage_ref, sem):
        i = pl.program_id(0)

        @pl.when(i == 0)
        def _():
            acc_ref[...] = jnp.zeros_like(acc_ref)

        xb = x_ref[...].astype(bf16)

        def add_feat(off, lhs):
            acc_ref[off:off + lhs.shape[0], :] += jnp.dot(lhs, xb, preferred_element_type=f32)

        def add_rows(off, piece):
            acc_ref[off:off + piece.shape[1], :] += lax.dot_general(piece, xb, TN, preferred_element_type=f32)

        add_feat(A_FQ, dqf_ref[...])
        add_feat(A_FK, dkf_ref[...])
        add_feat(A_FV, dvf_ref[...])
        add_rows(A_FZ, dfz_ref[...])
        add_feat(A_SQ, dqs_ref[...])
        add_feat(A_SK, dks_ref[...])
        add_feat(A_SV, dvs_ref[...])
        add_rows(A_SZ, dsz_ref[...])
        add_feat(A_FF, dfft_ref[...].astype(bf16))

        @pl.when(i == nt - 1)
        def _():
            for r in range(A_W // DW_STAGE_ROWS):
                rows = slice(r * DW_STAGE_ROWS, (r + 1) * DW_STAGE_ROWS)
                stage_ref[...] = acc_ref[rows, :].astype(bf16)
                cp = pltpu.make_async_copy(stage_ref, dw_ref.at[rows, :], sem)
                cp.start()
                cp.wait()

    return pl.pallas_call(
        body,
        name="dw_in_bwd",
        grid=(nt,),
        in_specs=[pl.BlockSpec((tm, D_MODEL), lambda i: (i, 0))] + _dproj_specs(tm),
        out_specs=pl.BlockSpec(memory_space=pl.ANY),
        out_shape=_sds((A_W, D_MODEL), bf16),
        scratch_shapes=[pltpu.VMEM((A_W, D_MODEL), f32), pltpu.VMEM((DW_STAGE_ROWS, D_MODEL), bf16),
                        pltpu.SemaphoreType.DMA],
        compiler_params=_params(("arbitrary",), VMEM_LIMIT_BIG),
    )(x2, *pieces)


def _adam_call(recv, w, m, v, tc, name):
    rows, cols = w.shape

    def body(r_ref, w_ref, m_ref, v_ref, g_ref, d_ref, mo_ref, vo_ref):
        g = r_ref[0].astype(f32)
        for p in range(1, N_DEV):
            g = g + r_ref[p].astype(f32)
        mn = ADAM_B1 * m_ref[...] + (1.0 - ADAM_B1) * g
        vn = ADAM_B2 * v_ref[...] + (1.0 - ADAM_B2) * (g * g)
        m_hat = mn / (1.0 - ADAM_B1 ** ADAM_STEP)
        v_hat = vn / (1.0 - ADAM_B2 ** ADAM_STEP)
        g_ref[...] = g
        d_ref[...] = -ADAM_LR * (m_hat / (jnp.sqrt(v_hat) + ADAM_EPS) + ADAM_WD * w_ref[...])
        mo_ref[...] = mn
        vo_ref[...] = vn

    blk = pl.BlockSpec((rows, tc), lambda i: (0, i))
    return pl.pallas_call(
        body,
        name=name,
        grid=(cols // tc,),
        in_specs=[pl.BlockSpec((N_DEV, rows, tc), lambda i: (0, 0, i)), blk, blk, blk],
        out_specs=[blk] * 4,
        out_shape=[_sds((rows, cols), f32)] * 4,
        compiler_params=_params(("arbitrary",)),
    )(recv, w, m, v)


def _pad_cols(a, width=128):
    return jnp.pad(a, ((0, 0), (0, width - a.shape[1])))


def _pack_small(ln_g, ln_b, rel, b_f, sink):
    return jnp.concatenate([
        ln_g.reshape(8, 128), ln_b.reshape(8, 128), _pad_cols(rel),
        jnp.pad(_pad_cols(b_f), ((0, 7), (0, 0))), jnp.pad(_pad_cols(sink), ((0, 7), (0, 0)))], axis=0)


def _unpack_small(p):
    return (p[0:8].reshape(1, D_MODEL), p[8:16].reshape(1, D_MODEL), p[16:48, 0:8], p[48:49, 0:8], p[56:57, 0:8])


def kernel(x, w_in, b_f, rel_bias, sink, w_o, ln_g, ln_b, loss_target, m_w_in, m_b_f, m_rel_bias, m_sink, m_w_o, m_ln_g, m_ln_b, v_w_in, v_b_f, v_rel_bias, v_sink, v_w_o, v_ln_g, v_ln_b):
    x2 = x[0]
    tgt = loss_target[0]
    s_len = x2.shape[0]
    shard = w_in.shape[2]

    w_in_t = jnp.transpose(w_in[0])
    g_in, g_o = _gather_call([w_in_t.astype(bf16), w_o[0].astype(bf16)])
    wt_full = g_in.reshape(N_DEV * shard, D_MODEL)
    w_t = jnp.concatenate([wt_full[:O_FF0], wt_full[O_FF1:], wt_full[O_FF0:O_FF1],
                           jnp.zeros((A_W - D_IN, D_MODEL), bf16)], axis=0)
    wo_full = g_o.reshape(D_MODEL, D_MODEL)

    qft, kft, vf, fz, qst, ks, vs, sz, fft, vat, kst, vsta = _proj_call(x2, w_t, 512)
    cum, sgm = _cum_call(fft, b_f.reshape(FOX_HEADS, 1))
    qat, ka, kat, tile_stats = _augment_call(qft, kft, cum.reshape(FOX_HEADS, 1, s_len), 2048)
    npairs, pair_q, pair_k = _fox_prune_tables(tile_stats)
    o_ft, lse_f = _fox_fwd_call(qat, ka, vat, npairs, pair_q, pair_k)
    bucket_t = jnp.asarray(_t5_bucket_table().T)
    bias_t, bias0_t = _swa_bias_call(rel_bias, bucket_t)
    sink_v = sink.reshape(SWA_HEADS)
    o_st, lse_s = _swa_fwd_call(qst, ks, vsta, bias_t, bias0_t, sink_v)

    (dh, do_f, dfz, do_s, dsz, dl_f, dl_s, dwo, dg, db, loss_part) = _post_call(
        o_ft.reshape(FOX_HEADS * HEAD_DIM, s_len), fz, o_st.reshape(SWA_HEADS * HEAD_DIM, s_len), sz, x2, tgt,
        wo_full, ln_g, ln_b, jnp.asarray(_head_selector()).astype(bf16), 512)

    dqf, dkf, dvf, dcq, dck = _fox_bwd_call(ka, kat, vf, qat, do_f.reshape(FOX_HEADS, HEAD_DIM, s_len), lse_f,
                                            dl_f.reshape(FOX_HEADS, 1, s_len), npairs, pair_q, pair_k)
    dqf, dkf, dvf = (a.reshape(FOX_HEADS * HEAD_DIM, s_len) for a in (dqf, dkf, dvf))
    dfft, dbf = _cum_bwd_call(dcq.reshape(FOX_HEADS, s_len), dck.reshape(FOX_HEADS, s_len), sgm)
    dqs, dks, dvs, dbias, dsink = _swa_bwd_call(
        qst, ks, kst, vs, do_s.reshape(SWA_HEADS, HEAD_DIM, s_len), lse_s, dl_s.reshape(SWA_HEADS, 1, s_len),
        bias_t, bias0_t, sink_v)
    dqs = dqs.reshape(SWA_HEADS * HEAD_DIM, s_len)
    dks, dvs = (a.reshape(SWA_KV_HEADS * HEAD_DIM, s_len) for a in (dks, dvs))
    drel = _swa_bias_bwd_call(dbias, bucket_t)

    pieces = (dqf, dkf, dvf, dfz, dqs, dks, dvs, dsz, dfft)
    dw_t = _dw_call(x2, pieces, 1024)

    dwt_full = jnp.concatenate([dw_t[:O_FF0], dw_t[A_FF:A_FF + (O_FF1 - O_FF0)], dw_t[O_FF0:A_FF]], axis=0)
    dw_blocks = dwt_full.reshape(N_DEV, shard, D_MODEL)
    dwo_blocks = dwo.reshape(N_DEV, D_MODEL // N_DEV, D_MODEL).astype(bf16)
    small = _pack_small(dg, db, drel[:, 0:8], dbf[:, 0].reshape(1, 8), dsink[:, 0].reshape(1, 8))
    loss_slot = np.zeros((64, 128), bool)
    loss_slot[49, 0] = True
    small = jnp.where(jnp.asarray(loss_slot), loss_part[0, 0], small)
    small_blocks = jnp.broadcast_to(small[None], (N_DEV,) + small.shape)
    dx, (r_in, r_o, r_small) = _dx_exchange_call(dh, pieces, w_t, [dw_blocks, dwo_blocks, small_blocks], 256)

    win_t = [jnp.transpose(a) for a in _adam_call(
        r_in, w_in_t, jnp.transpose(m_w_in[0]), jnp.transpose(v_w_in[0]), 256, "adam_w_in")]
    g_win, d_win, nm_win, nv_win = win_t
    g_wo, d_wo, nm_wo, nv_wo = _adam_call(r_o, w_o[0], m_w_o[0], v_w_o[0], 256, "adam_w_o")
    p_w = _pack_small(ln_g, ln_b, rel_bias, b_f, sink)
    p_m = _pack_small(m_ln_g, m_ln_b, m_rel_bias, m_b_f, m_sink)
    p_v = _pack_small(v_ln_g, v_ln_b, v_rel_bias, v_b_f, v_sink)
    g_p, d_p, nm_p, nv_p = _adam_call(r_small, p_w, p_m, p_v, 128, "adam_small")

    loss = g_p[49, 0]
    g_lng, g_lnb, g_rel, g_bf, g_sink = _unpack_small(g_p)
    d_lng, d_lnb, d_rel, d_bf, d_sink = _unpack_small(d_p)
    m_lng, m_lnb, m_rel, m_bf, m_sk = _unpack_small(nm_p)
    v_lng, v_lnb, v_rel, v_bf, v_sk = _unpack_small(nv_p)
    return (loss, dx[None], g_win[None], g_bf, g_rel, g_sink, g_wo[None], g_lng, g_lnb,
            d_win[None], d_bf, d_rel, d_sink, d_wo[None], d_lng, d_lnb,
            nm_win[None], m_bf, m_rel, m_sk, nm_wo[None], m_lng, m_lnb,
            nv_win[None], v_bf, v_rel, v_sk, nv_wo[None], v_lng, v_lnb)
```

```python
import functools
import math

import numpy as np
import jax
import jax.numpy as jnp
from jax import lax
from jax.experimental import pallas as pl
from jax.experimental.pallas import tpu as pltpu

f32 = jnp.float32
bf16 = jnp.bfloat16

D_MODEL = 1024
HEAD_DIM = 64
FOX_HEADS = 8
SWA_HEADS = 8
SWA_KV_HEADS = 2
SWA_GROUP = 4
BLOCK = 128
NUM_BUCKETS = 32
MAX_DISTANCE = 128
LN_EPS = 1e-5
NEG_INF = -1e30
ALPHA = 2.0 ** 0.25
SCALE = 1.0 / math.sqrt(HEAD_DIM)
D_IN = 3336

ADAM_LR = 0.001
ADAM_B1 = 0.9
ADAM_B2 = 0.999
ADAM_EPS = 1e-08
ADAM_WD = 0.01
ADAM_STEP = 10

N_DEV = 8
A_FQ, A_FK, A_FV, A_FZ, A_SQ, A_SK, A_SV, A_SZ, A_FF, A_W = 0, 512, 1024, 1536, 2048, 2560, 2688, 2816, 3328, 3456
O_FF0, O_FF1 = 1536, 1544

VMEM_LIMIT = 48 * 1024 * 1024
HIGHEST = lax.Precision.HIGHEST
NT = (((1,), (1,)), ((), ()))
TN = (((0,), (0,)), ((), ()))
MESH = pl.DeviceIdType.MESH
RELS = [(0, 0, 1), (0, 1, 0), (0, 1, 1), (1, 0, 0), (1, 0, 1), (1, 1, 0), (1, 1, 1)]


VMEM_LIMIT_BIG = 60 * 1024 * 1024


def _params(sem=None, vmem=VMEM_LIMIT):
    return pltpu.CompilerParams(dimension_semantics=sem, vmem_limit_bytes=vmem)


def _sds(shape, dtype):
    return jax.ShapeDtypeStruct(shape, dtype)


def _t5_bucket_table():
    qi = np.arange(BLOCK)[:, None]
    kj = np.arange(2 * BLOCK)[None, :]
    rel = qi + BLOCK - kj
    band = (rel >= 0) & (rel < BLOCK)
    relc = np.maximum(rel, 0)
    max_exact = NUM_BUCKETS // 2
    relf = np.maximum(relc, 1).astype(np.float32)
    large = max_exact + (np.log(relf / np.float32(max_exact)) / np.float32(math.log(MAX_DISTANCE / max_exact))
                         * np.float32(NUM_BUCKETS - max_exact)).astype(np.int32)
    large = np.minimum(large, NUM_BUCKETS - 1)
    bucket = np.where(relc < max_exact, relc, large).astype(np.int32)
    bucket = np.where(band, bucket, -1).astype(np.int32)
    return bucket


def _mesh_pos():
    return lax.axis_index("x"), lax.axis_index("y"), lax.axis_index("c")


def _dev_index(p):
    return 4 * p[0] + 2 * p[1] + p[2]


def _gather_call(xs):
    n = len(xs)

    def body(*refs):
        x_refs, o_refs = refs[:n], refs[n:2 * n]
        send_sems, recv_sems, local_sems = refs[2 * n:]
        x, y, c = _mesh_pos()
        me, sib = (x, y, c), (x, y, 1 - c)
        chips = [(1 - x, y), (x, 1 - y), (1 - x, 1 - y)]

        def copy(a, k, block, to, src=None):
            slot = o_refs[a].at[_dev_index(block)]
            return pltpu.make_async_remote_copy(
                src_ref=slot if src is None else src, dst_ref=slot,
                send_sem=send_sems.at[a * 7 + k], recv_sem=recv_sems.at[a * 7 + k],
                device_id=to, device_id_type=MESH)

        mine = [pltpu.make_async_copy(x_refs[a], o_refs[a].at[_dev_index(me)], local_sems.at[a]) for a in range(n)]
        for cp in mine:
            cp.start()
        first = []
        for a in range(n):
            first.append(copy(a, 0, me, sib, src=x_refs[a]))
            first += [copy(a, 1 + j, me, (*chip, c), src=x_refs[a]) for j, chip in enumerate(chips)]
        for cp in first:
            cp.start()
        passed = []
        for j, chip in enumerate(chips):
            for a in range(n):
                copy(a, 1 + j, (*chip, c), me).wait_recv()
                fwd = copy(a, 4 + j, (*chip, c), sib)
                fwd.start()
                passed.append(fwd)
        for a in range(n):
            copy(a, 0, sib, me).wait_recv()
            for j, chip in enumerate(chips):
                copy(a, 4 + j, (*chip, 1 - c), me).wait_recv()
        for cp in first + passed:
            cp.wait_send()
        for cp in mine:
            cp.wait()

    any_spec = pl.BlockSpec(memory_space=pl.ANY)
    return pl.pallas_call(
        body,
        name="gather_weights",
        out_shape=[_sds((N_DEV,) + a.shape, a.dtype) for a in xs],
        in_specs=[any_spec] * n,
        out_specs=[any_spec] * n,
        scratch_shapes=[pltpu.SemaphoreType.DMA((7 * n,)), pltpu.SemaphoreType.DMA((7 * n,)),
                        pltpu.SemaphoreType.DMA((n,))],
    )(*xs)


def _exchange_copies(b_refs, r_refs, send_sems, recv_sems, local_sems, incoming):
    n = len(b_refs)
    x, y, c = _mesh_pos()
    me_idx = _dev_index((x, y, c))
    mine = [pltpu.make_async_copy(b_refs[a].at[me_idx], r_refs[a].at[me_idx], local_sems.at[a]) for a in range(n)]
    remote = []
    for k, r in enumerate(RELS):
        peer = ((1 - x) if r[0] else x, (1 - y) if r[1] else y, (1 - c) if r[2] else c)
        pidx = _dev_index(peer)
        for a in range(n):
            remote.append(pltpu.make_async_remote_copy(
                src_ref=b_refs[a].at[pidx], dst_ref=r_refs[a].at[pidx if incoming else me_idx],
                send_sem=send_sems.at[a * 7 + k], recv_sem=recv_sems.at[a * 7 + k],
                device_id=peer, device_id_type=MESH))
    return mine, remote


def _exchange_start(b_refs, r_refs, sems):
    mine, out = _exchange_copies(b_refs, r_refs, *sems, incoming=False)
    for cp in mine + out:
        cp.start()


def _exchange_wait(b_refs, r_refs, sems):
    mine, inc = _exchange_copies(b_refs, r_refs, *sems, incoming=True)
    for cp in inc:
        cp.wait_recv()
    for cp in inc:
        cp.wait_send()
    for cp in mine:
        cp.wait()


def _proj_call(x2, w_t, tm):
    s_len = x2.shape[0]

    def body(x_ref, w_ref, qft_ref, kft_ref, vf_ref, fz_ref, qst_ref, ks_ref, vs_ref, sz_ref, fft_ref, vat_ref,
             kst_ref, vsta_ref):
        xb = x_ref[...].astype(bf16)

        def seg_t(off, width):
            return lax.dot_general(w_ref[off:off + width, :], xb, NT, preferred_element_type=f32)

        def seg(off, width):
            return lax.dot_general(xb, w_ref[off:off + width, :], NT, preferred_element_type=f32)

        def put_heads(ref, acc, nheads):
            for h in range(nheads):
                ref[h] = acc[:, h * HEAD_DIM:(h + 1) * HEAD_DIM].astype(bf16)

        def put_heads_t(ref, acc_t, nheads, augment):
            for h in range(nheads):
                ref[h, 0:HEAD_DIM, :] = acc_t[h * HEAD_DIM:(h + 1) * HEAD_DIM, :].astype(bf16)
                if augment:
                    ref[h, HEAD_DIM:2 * HEAD_DIM, :] = ones_row

        ones_row = jnp.where(lax.broadcasted_iota(jnp.int32, (HEAD_DIM, tm), 0) == 0, 1.0, 0.0).astype(bf16)
        put_heads_t(vat_ref, seg_t(A_FV, 512), FOX_HEADS, True)
        put_heads_t(qft_ref, seg_t(A_FQ, 512) * SCALE, FOX_HEADS, False)
        put_heads_t(kft_ref, seg_t(A_FK, 512), FOX_HEADS, False)
        put_heads(vf_ref, seg(A_FV, 512), FOX_HEADS)
        fz_ref[...] = seg(A_FZ, 512)
        put_heads_t(qst_ref, seg_t(A_SQ, 512) * SCALE, SWA_HEADS, False)
        put_heads(ks_ref, seg(A_SK, 128), SWA_KV_HEADS)
        put_heads(vs_ref, seg(A_SV, 128), SWA_KV_HEADS)
        put_heads_t(kst_ref, seg_t(A_SK, 128), SWA_KV_HEADS, False)
        put_heads_t(vsta_ref, seg_t(A_SV, 128), SWA_KV_HEADS, True)
        sz_ref[...] = seg(A_SZ, 512)
        fft_ref[...] = seg(A_FF, 128).T[:FOX_HEADS, :]

    def heads(nh):
        return pl.BlockSpec((nh, tm, HEAD_DIM), lambda i: (0, i, 0))

    def feat(nh, rows):
        return pl.BlockSpec((nh, rows, tm), lambda i: (0, 0, i))

    wide = pl.BlockSpec((tm, 512), lambda i: (i, 0))
    return pl.pallas_call(
        body,
        name="proj_fwd",
        grid=(s_len // tm,),
        in_specs=[pl.BlockSpec((tm, D_MODEL), lambda i: (i, 0)), pl.BlockSpec((A_W, D_MODEL), lambda i: (0, 0))],
        out_specs=[feat(8, HEAD_DIM), feat(8, HEAD_DIM), heads(8), wide, feat(8, HEAD_DIM), heads(2), heads(2), wide,
                   pl.BlockSpec((FOX_HEADS, tm), lambda i: (0, i)),
                   feat(FOX_HEADS, 2 * HEAD_DIM), feat(2, HEAD_DIM), feat(2, 2 * HEAD_DIM)],
        out_shape=[_sds((8, HEAD_DIM, s_len), bf16)] * 2 + [_sds((8, s_len, HEAD_DIM), bf16)]
                  + [_sds((s_len, 512), f32), _sds((8, HEAD_DIM, s_len), bf16),
                     _sds((2, s_len, HEAD_DIM), bf16), _sds((2, s_len, HEAD_DIM), bf16), _sds((s_len, 512), f32),
                     _sds((FOX_HEADS, s_len), f32), _sds((FOX_HEADS, 2 * HEAD_DIM, s_len), bf16),
                     _sds((2, HEAD_DIM, s_len), bf16), _sds((2, 2 * HEAD_DIM, s_len), bf16)],
        compiler_params=_params(("arbitrary",)),
    )(x2, w_t)


AUG = 2 * HEAD_DIM


def _augment_call(q_t, k_t, cum_row, tm):
    nh, _, s_len = k_t.shape
    per_step = tm // FOX_T

    def body(qt_ref, kt_ref, c_ref, qat_ref, ka_ref, kat_ref, st_ref):
        c = c_ref[0]
        hi = c.astype(bf16).astype(f32)
        r1 = c - hi
        mid = r1.astype(bf16).astype(f32)
        lo = (r1 - mid).astype(bf16).astype(f32)
        row = lax.broadcasted_iota(jnp.int32, (HEAD_DIM, tm), 0)
        q_tail = jnp.where(row == 0, hi, jnp.where(row == 1, mid, jnp.where(row == 2, lo,
                           jnp.where(row < 6, 1.0, 0.0))))
        k_tail = jnp.where(row < 3, 1.0, jnp.where(row == 3, -hi, jnp.where(row == 4, -mid,
                           jnp.where(row == 5, -lo, 0.0))))
        qat_ref[0, 0:HEAD_DIM, :] = qt_ref[0]
        qat_ref[0, HEAD_DIM:AUG, :] = q_tail.astype(bf16)
        kat_ref[0, 0:HEAD_DIM, :] = kt_ref[0]
        kat_ref[0, HEAD_DIM:AUG, :] = k_tail.astype(bf16)
        qt = qt_ref[0].astype(f32)
        kt = kt_ref[0].astype(f32)
        ka_ref[0] = jnp.concatenate([kt, k_tail], axis=0).T.astype(bf16)
        qn2 = jnp.sum(qt * qt, axis=0, keepdims=True)
        kn2 = jnp.sum(kt * kt, axis=0, keepdims=True)
        sd = jnp.sum(qt * kt, axis=0, keepdims=True)
        srow = lax.broadcasted_iota(jnp.int32, (8, LANES), 0)
        for part in range(per_step):
            sl = slice(part * FOX_T, (part + 1) * FOX_T)
            vals = [jnp.sqrt(jnp.max(qn2[:, sl], axis=1, keepdims=True)),
                    jnp.sqrt(jnp.max(kn2[:, sl], axis=1, keepdims=True)),
                    jnp.min(sd[:, sl], axis=1, keepdims=True),
                    jnp.max(c[:, sl], axis=1, keepdims=True), jnp.min(c[:, sl], axis=1, keepdims=True)]
            out = jnp.zeros((8, LANES), f32)
            for r, val in enumerate(vals):
                out = jnp.where(srow == r, val, out)
            st_ref[0, part] = out

    tile_t = pl.BlockSpec((1, HEAD_DIM, tm), lambda h, i: (h, 0, i))
    return pl.pallas_call(
        body,
        name="fox_augment",
        grid=(nh, s_len // tm),
        in_specs=[tile_t, tile_t, pl.BlockSpec((1, 1, tm), lambda h, i: (h, 0, i))],
        out_specs=[pl.BlockSpec((1, AUG, tm), lambda h, i: (h, 0, i)),
                   pl.BlockSpec((1, tm, AUG), lambda h, i: (h, i, 0)),
                   pl.BlockSpec((1, AUG, tm), lambda h, i: (h, 0, i)),
                   pl.BlockSpec((1, per_step, 8, LANES), lambda h, i: (h, i, 0, 0))],
        out_shape=[_sds((nh, AUG, s_len), bf16), _sds((nh, s_len, AUG), bf16), _sds((nh, AUG, s_len), bf16),
                   _sds((nh, s_len // FOX_T, 8, LANES), f32)],
        compiler_params=_params(("arbitrary", "arbitrary")),
    )(q_t, k_t, cum_row)


EXP_ZERO_GAP = 110.0


def _fox_prune_tables(stats):
    s = stats[:, :, :, 0]
    qn, kn, sd, cmx, cmn = (s[:, :, r] for r in range(5))
    nt = s.shape[1]
    bound = qn[:, :, None] * kn[:, None, :] + (cmx[:, :, None] - cmn[:, None, :])
    margin = 2.0 + 1e-5 * (jnp.abs(cmx)[:, :, None] + jnp.abs(cmn)[:, None, :])
    qi = lax.broadcasted_iota(jnp.int32, (nt, nt), 0)
    kj = lax.broadcasted_iota(jnp.int32, (nt, nt), 1)
    skip = (bound + margin < sd[:, :, None] - EXP_ZERO_GAP) & (kj < qi)[None]
    first = jnp.sum(jnp.cumprod(skip.astype(jnp.int32), axis=2), axis=2)
    tiles = lax.broadcasted_iota(jnp.int32, (1, nt), 1)
    cnt = tiles - first
    ends = jnp.cumsum(cnt, axis=1)
    off = ends - cnt
    kmax = nt * (nt - 1) // 2
    k = lax.broadcasted_iota(jnp.int32, (1, kmax), 1)
    pair_q = jnp.minimum(jnp.sum((ends[:, None, :] <= k[:, :, None]).astype(jnp.int32), axis=2), nt - 1)
    hit = pair_q[:, :, None] == tiles[:, None, :]
    first_k = jnp.sum(jnp.where(hit, first[:, None, :], 0), axis=2)
    off_k = jnp.sum(jnp.where(hit, off[:, None, :], 0), axis=2)
    pair_k = jnp.clip(first_k + k - off_k, 0, nt - 1)
    return (ends[:, nt - 1].astype(jnp.int32), pair_q.reshape(-1).astype(jnp.int32),
            pair_k.reshape(-1).astype(jnp.int32))


CUM_CHUNK = 512


def _cum_call(fft, bf_col):
    s_len = fft.shape[1]
    ch = CUM_CHUNK

    def body(f_ref, b_ref, cum_ref, sg_ref):
        r = lax.broadcasted_iota(jnp.int32, (ch, ch), 0)
        c = lax.broadcasted_iota(jnp.int32, (ch, ch), 1)
        upper = (r <= c).astype(f32)
        carry = jnp.zeros((FOX_HEADS, 1), f32)
        for n in range(s_len // ch):
            z = f_ref[:, n * ch:(n + 1) * ch] + b_ref[...]
            logf = jnp.minimum(z, 0.0) - jnp.log1p(jnp.exp(-jnp.abs(z)))
            sg_ref[:, n * ch:(n + 1) * ch] = 1.0 / (1.0 + jnp.exp(z))
            cs = jnp.dot(logf, upper, precision=HIGHEST, preferred_element_type=f32) + carry
            cum_ref[:, n * ch:(n + 1) * ch] = cs
            carry = cs[:, ch - 1:ch]

    return pl.pallas_call(
        body,
        name="fox_cum_fwd",
        out_shape=[_sds((FOX_HEADS, s_len), f32)] * 2,
        compiler_params=_params(),
    )(fft, bf_col)


def _cum_bwd_call(dcq, dck, sg):
    s_len = sg.shape[1]
    ch = CUM_CHUNK
    nch = s_len // ch

    def body(q_ref, k_ref, sg_ref, dff_ref, dbf_ref):
        r = lax.broadcasted_iota(jnp.int32, (ch, ch), 0)
        c = lax.broadcasted_iota(jnp.int32, (ch, ch), 1)
        lower = (r >= c).astype(f32)
        dff_ref[...] = jnp.zeros_like(dff_ref)
        carry = jnp.zeros((FOX_HEADS, 1), f32)
        total = jnp.zeros((FOX_HEADS, 1), f32)
        for n in reversed(range(nch)):
            sl = slice(n * ch, (n + 1) * ch)
            dcum = q_ref[:, sl] - k_ref[:, sl]
            rs = jnp.dot(dcum, lower, precision=HIGHEST, preferred_element_type=f32) + carry
            carry = rs[:, 0:1]
            dff = rs * sg_ref[:, sl]
            dff_ref[0:FOX_HEADS, sl] = dff
            total = total + jnp.sum(dff, axis=1, keepdims=True)
        dbf_ref[...] = jnp.broadcast_to(total, (FOX_HEADS, 128))

    return pl.pallas_call(
        body,
        name="fox_cum_bwd",
        out_shape=[_sds((128, s_len), f32), _sds((FOX_HEADS, 128), f32)],
        compiler_params=_params(),
    )(dcq, dck, sg)


FOX_T = 512
FOX_RING = 8
LANES = 128


def _causal_keep(t):
    return lax.broadcasted_iota(jnp.int32, (t, t), 0) <= lax.broadcasted_iota(jnp.int32, (t, t), 1)


def _tile_cols(i, t):
    return pl.ds(pl.multiple_of(i * t, t), t)


def _fox_pair(n, nt, kmax, h, pq_ref, pk_ref):
    k = h * kmax + jnp.maximum(n - nt, 0)
    return jnp.where(n < nt, n, pq_ref[k]), jnp.where(n < nt, n, pk_ref[k])


def _fox_fwd_call(qat, ka, vat, npairs, pair_q, pair_k):
    nh, s_len, _ = ka.shape
    t = FOX_T
    nt = s_len // t
    kmax = nt * (nt - 1) // 2
    nslots = nt + kmax
    assert nt >= 4 and nt % 2 == 0

    def body(np_ref, pq_ref, pk_ref, qat_ref, ka_ref, vat_ref, o_ref, lse_ref, psave_ref, msave_ref,
             s0, s1, p0, p1, a0, a1, m_all, acc_all, ring, save_sems):
        h = pl.program_id(0)
        extra = np_ref[h]
        total = nt + extra
        m_all[...] = jnp.full(m_all.shape, NEG_INF, f32)
        acc_all[...] = jnp.zeros(acc_all.shape, f32)
        bufs = ((s0, p0, a0), (s1, p1, a1))

        def pair(n):
            return _fox_pair(n, nt, kmax, h, pq_ref, pk_ref)

        def scores(n, b, masked):
            i, j = pair(n)
            st = jnp.dot(ka_ref[0, _tile_cols(j, t), :], qat_ref[0, :, _tile_cols(i, t)], preferred_element_type=f32)
            if masked:
                st = jnp.where(_causal_keep(t), st, NEG_INF)
            bufs[b][0][...] = st

        def save(q):
            slot = lax.rem(q, FOX_RING)
            return pltpu.make_async_copy(ring.at[slot], psave_ref.at[h, q], save_sems.at[slot])

        def ring_turn(n):
            for q in (n - 3, n - 2):
                @pl.when(q >= 0)
                def _():
                    save(q).start()
            for q in (n - 1 - FOX_RING, n - FOX_RING):
                @pl.when(q >= 0)
                def _():
                    save(q).wait()

        def softmax(n, b):
            i, _ = pair(n)
            s_ref, p_ref, a_ref = bufs[b]
            slot = lax.rem(n, FOX_RING)
            for c in range(t // LANES):
                cols = slice(c * LANES, (c + 1) * LANES)
                mcols = pl.ds(pl.multiple_of(i * t + c * LANES, LANES), LANES)
                m_old = m_all[:, mcols]
                m_new = jnp.maximum(m_old, jnp.max(s_ref[:, cols], axis=0, keepdims=True))
                m_all[:, mcols] = m_new
                msave_ref[0, n, :, cols] = m_new
                a_ref[:, cols] = jnp.exp(m_old - m_new)
                p = jnp.exp(s_ref[:, cols] - m_new).astype(bf16)
                p_ref[:, cols] = p
                ring[slot, :, cols] = p

        def accum(n, b):
            i, j = pair(n)
            cols = _tile_cols(i, t)
            acc_all[:, cols] = bufs[b][2][...] * acc_all[:, cols] + jnp.dot(
                vat_ref[0, :, _tile_cols(j, t)], bufs[b][1][...], preferred_element_type=f32)

        def step(n, b, masked):
            accum(n - 2, b)
            softmax(n - 1, 1 - b)
            scores(n, b, masked)

        scores(0, 0, True)
        scores(1, 1, True)
        softmax(0, 0)

        def diag_steps(d, _):
            n = 2 + 2 * d
            ring_turn(n)
            step(n, 0, True)
            step(n + 1, 1, True)
            return 0

        lax.fori_loop(0, (nt - 2) // 2, diag_steps, 0)

        def off_steps(d, _):
            n = nt + 2 * d
            ring_turn(n)
            step(n, 0, False)
            step(n + 1, 1, False)
            return 0

        lax.fori_loop(0, extra // 2, off_steps, 0)
        n_end = total - extra % 2
        ring_turn(n_end)

        @pl.when(extra % 2 == 1)
        def _():
            step(total - 1, 0, False)
            softmax(total - 1, 0)
            accum(total - 2, 1)
            accum(total - 1, 0)

        @pl.when(extra % 2 == 0)
        def _():
            softmax(total - 1, 1)
            accum(total - 2, 0)
            accum(total - 1, 1)

        def start_rest(q, _):
            save(q).start()
            return 0

        def wait_rest(q, _):
            save(q).wait()
            return 0

        lax.fori_loop(n_end - 1, total, start_rest, 0)
        lax.fori_loop(jnp.maximum(n_end - FOX_RING + 1, 0), total, wait_rest, 0)
        l = acc_all[HEAD_DIM:HEAD_DIM + 1, :]
        o_ref[0] = acc_all[0:HEAD_DIM, :] / l
        lse_ref[0] = m_all[...] + jnp.log(l)

    smem = pl.BlockSpec(memory_space=pltpu.SMEM)
    return pl.pallas_call(
        body,
        name="fox_fwd",
        grid=(nh,),
        in_specs=[smem, smem, smem,
                  pl.BlockSpec((1, AUG, s_len), lambda h: (h, 0, 0)),
                  pl.BlockSpec((1, s_len, AUG), lambda h: (h, 0, 0)),
                  pl.BlockSpec((1, AUG, s_len), lambda h: (h, 0, 0))],
        out_specs=[pl.BlockSpec((1, HEAD_DIM, s_len), lambda h: (h, 0, 0)),
                   pl.BlockSpec((1, 1, s_len), lambda h: (h, 0, 0)),
                   pl.BlockSpec(memory_space=pl.ANY),
                   pl.BlockSpec((1, nslots, 1, t), lambda h: (h, 0, 0, 0))],
        out_shape=[_sds((nh, HEAD_DIM, s_len), f32), _sds((nh, 1, s_len), f32), _sds((nh, nslots, t, t), bf16),
                   _sds((nh, nslots, 1, t), f32)],
        scratch_shapes=[pltpu.VMEM((t, t), f32), pltpu.VMEM((t, t), f32), pltpu.VMEM((t, t), bf16),
                        pltpu.VMEM((t, t), bf16), pltpu.VMEM((1, t), f32), pltpu.VMEM((1, t), f32),
                        pltpu.VMEM((1, s_len), f32), pltpu.VMEM((AUG, s_len), f32), pltpu.VMEM((FOX_RING, t, t), bf16),
                        pltpu.SemaphoreType.DMA((FOX_RING,))],
        compiler_params=_params(("arbitrary",)),
    )(npairs, pair_q, pair_k, qat, ka, vat)


SWA_TS = 512


SWA_W = SWA_GROUP * BLOCK


def _swa_bias_call(rel_bias, bucket_t):
    def body(rb_ref, bk_ref, b_ref, b0_ref):
        bk = bk_ref[...]
        row = lax.broadcasted_iota(jnp.int32, (2 * BLOCK, BLOCK), 0)
        for h in range(SWA_HEADS):
            acc = jnp.full((2 * BLOCK, BLOCK), NEG_INF, f32)
            for b in range(NUM_BUCKETS):
                acc = jnp.where(bk == b, rb_ref[b, h], acc)
            g, hh = divmod(h, SWA_GROUP)
            b_ref[g, :, hh * BLOCK:(hh + 1) * BLOCK] = acc
            b0_ref[g, :, hh * BLOCK:(hh + 1) * BLOCK] = jnp.where(row < BLOCK, NEG_INF, acc)

    return pl.pallas_call(
        body,
        name="swa_bias",
        in_specs=[pl.BlockSpec(memory_space=pltpu.SMEM), pl.BlockSpec(memory_space=pltpu.VMEM)],
        out_shape=[_sds((SWA_KV_HEADS, 2 * BLOCK, SWA_W), f32)] * 2,
        compiler_params=_params(),
    )(rel_bias, bucket_t)


def _swa_bias_bwd_call(dbias, bucket_t):
    def body(d_ref, bk_ref, o_ref):
        bk = bk_ref[...]
        row = lax.broadcasted_iota(jnp.int32, (NUM_BUCKETS, 128), 0)
        col = lax.broadcasted_iota(jnp.int32, (NUM_BUCKETS, 128), 1)
        out = jnp.zeros((NUM_BUCKETS, 128), f32)
        for h in range(SWA_HEADS):
            g, hh = divmod(h, SWA_GROUP)
            d = d_ref[g, :, hh * BLOCK:(hh + 1) * BLOCK]
            for b in range(NUM_BUCKETS):
                val = jnp.sum(jnp.sum(jnp.where(bk == b, d, 0.0), axis=1, keepdims=True), axis=0, keepdims=True)
                out = jnp.where((row == b) & (col == h), val, out)
        o_ref[...] = out

    return pl.pallas_call(
        body,
        name="swa_bias_bwd",
        out_shape=_sds((NUM_BUCKETS, 128), f32),
        compiler_params=_params(),
    )(dbias, bucket_t)


def _sink_row(sink_ref, g):
    return jnp.concatenate([jnp.full((1, BLOCK), sink_ref[g * SWA_GROUP + hh], f32) for hh in range(SWA_GROUP)], axis=1)


def _group_lanes(ref, g, cols):
    return jnp.concatenate([ref[g * SWA_GROUP + hh, :, cols] for hh in range(SWA_GROUP)], axis=1)


def _swa_fwd_call(qt, k, vta, bias_t, bias0_t, sink):
    s_len = qt.shape[2]
    ts = SWA_TS
    nb = ts // BLOCK

    def body(qt_ref, kc_ref, kp_ref, vc_ref, vp_ref, b_ref, b0_ref, sink_ref, o_ref, lse_ref):
        first = pl.program_id(0) == 0
        kall = [jnp.concatenate([kp_ref[g], kc_ref[g]], axis=0) for g in range(SWA_KV_HEADS)]
        vall = [jnp.concatenate([vp_ref[g], vc_ref[g]], axis=1) for g in range(SWA_KV_HEADS)]
        sinks = [_sink_row(sink_ref, g) for g in range(SWA_KV_HEADS)]
        items = [(g, b) for g in range(SWA_KV_HEADS) for b in range(nb)]

        def scores(g, b):
            qg = _group_lanes(qt_ref, g, slice(b * BLOCK, (b + 1) * BLOCK))
            bias_b = b_ref[g]
            if b == 0:
                bias_b = jnp.where(first, b0_ref[g], bias_b)
            return jnp.dot(kall[g][b * BLOCK:(b + 2) * BLOCK], qg, preferred_element_type=f32) + bias_b

        def finish(g, b, st):
            m = jnp.maximum(jnp.max(st, axis=0, keepdims=True), sinks[g])
            pt = jnp.exp(st - m)
            acc = jnp.dot(vall[g][:, b * BLOCK:(b + 2) * BLOCK], pt.astype(bf16), preferred_element_type=f32)
            l = acc[HEAD_DIM:HEAD_DIM + 1, :] + jnp.exp(sinks[g] - m)
            return acc[0:HEAD_DIM, :] / l, m + jnp.log(l)

        outs, lses = {}, {}
        st_next = scores(*items[0])
        for idx, (g, b) in enumerate(items):
            st = st_next
            if idx + 1 < len(items):
                st_next = scores(*items[idx + 1])
            outs[g, b], lses[g, b] = finish(g, b, st)
        for g in range(SWA_KV_HEADS):
            for hh in range(SWA_GROUP):
                lanes = slice(hh * BLOCK, (hh + 1) * BLOCK)
                o_ref[g * SWA_GROUP + hh] = jnp.concatenate([outs[g, b][:, lanes] for b in range(nb)], axis=1)
                lse_ref[g * SWA_GROUP + hh] = jnp.concatenate([lses[g, b][:, lanes] for b in range(nb)], axis=1)

    def prev_blk(n):
        return jnp.maximum(n * nb - 1, 0)

    bspec = pl.BlockSpec((SWA_KV_HEADS, 2 * BLOCK, SWA_W), lambda n: (0, 0, 0))
    return pl.pallas_call(
        body,
        name="swa_fwd",
        grid=(s_len // ts,),
        in_specs=[pl.BlockSpec((SWA_HEADS, HEAD_DIM, ts), lambda n: (0, 0, n)),
                  pl.BlockSpec((SWA_KV_HEADS, ts, HEAD_DIM), lambda n: (0, n, 0)),
                  pl.BlockSpec((SWA_KV_HEADS, BLOCK, HEAD_DIM), lambda n: (0, prev_blk(n), 0)),
                  pl.BlockSpec((SWA_KV_HEADS, AUG, ts), lambda n: (0, 0, n)),
                  pl.BlockSpec((SWA_KV_HEADS, AUG, BLOCK), lambda n: (0, 0, prev_blk(n))),
                  bspec, bspec, pl.BlockSpec(memory_space=pltpu.SMEM)],
        out_specs=[pl.BlockSpec((SWA_HEADS, HEAD_DIM, ts), lambda n: (0, 0, n)),
                   pl.BlockSpec((SWA_HEADS, 1, ts), lambda n: (0, 0, n))],
        out_shape=[_sds((SWA_HEADS, HEAD_DIM, s_len), f32), _sds((SWA_HEADS, 1, s_len), f32)],
        compiler_params=_params(("arbitrary",)),
    )(qt, k, k, vta, vta, bias_t, bias0_t, sink)


def _head_selector():
    sel = np.zeros((512, 128), np.float32)
    for h in range(8):
        sel[h * HEAD_DIM:(h + 1) * HEAD_DIM, h] = 1.0
    return sel


def _post_call(of, fz, osw, sz, x2, tgt, wo, ln_g, ln_b, sel, tm):
    s_len = x2.shape[0]

    def body(of_ref, fz_ref, os_ref, sz_ref, x_ref, t_ref, wo_ref, g_ref, b_ref, sel_ref,
             dh_ref, dof_ref, dfz_ref, dos_ref, dsz_ref, dlf_ref, dls_ref, dwo_ref, dg_ref, db_ref, loss_ref):
        n = pl.program_id(0)

        @pl.when(n == 0)
        def _():
            dwo_ref[...] = jnp.zeros_like(dwo_ref)
            dg_ref[...] = jnp.zeros_like(dg_ref)
            db_ref[...] = jnp.zeros_like(db_ref)
            loss_ref[...] = jnp.zeros_like(loss_ref)

        gam = g_ref[...]
        sel_m = sel_ref[...]

        def forward(r):
            o_f = of_ref[:, r].T
            o_s = os_ref[:, r].T
            fz = fz_ref[r, :]
            sz = sz_ref[r, :]
            sg_f = jax.nn.sigmoid(fz)
            sg_s = jax.nn.sigmoid(sz)
            silu_f = fz * sg_f
            silu_s = sz * sg_s
            mixed = jnp.concatenate([o_f * silu_f, o_s * silu_s], axis=1).astype(bf16)
            y = jnp.dot(mixed, wo_ref[...], preferred_element_type=f32)
            return o_f, o_s, fz, sz, sg_f, sg_s, silu_f, silu_s, mixed, y

        def norm_and_back(r, fwd):
            mixed, y = fwd[8], fwd[9]
            h = ALPHA * x_ref[r, :] + y
            mu = jnp.mean(h, axis=1, keepdims=True)
            hc = h - mu
            var = jnp.mean(hc * hc, axis=1, keepdims=True)
            rstd = lax.rsqrt(var + LN_EPS)
            xhat = hc * rstd
            out = xhat * gam + b_ref[...]
            err = out - t_ref[r, :]
            tok_loss = jnp.mean(err * err, axis=1, keepdims=True)
            loss_ref[...] += 0.5 * jnp.sum(tok_loss, axis=0, keepdims=True)
            dout = err * (1.0 / D_MODEL)
            dg_ref[...] += jnp.sum(dout * xhat, axis=0, keepdims=True)
            db_ref[...] += jnp.sum(dout, axis=0, keepdims=True)
            dxh = dout * gam
            m1 = jnp.mean(dxh, axis=1, keepdims=True)
            m2 = jnp.mean(dxh * xhat, axis=1, keepdims=True)
            dh = rstd * (dxh - m1 - xhat * m2)
            dh_ref[r, :] = dh
            dyb = dh.astype(bf16)
            dmix = lax.dot_general(dyb, wo_ref[...], NT, preferred_element_type=f32)
            dwo_ref[...] += lax.dot_general(mixed, dyb, TN, preferred_element_type=f32)
            return dmix

        def head_sums(prod):
            hi = prod.astype(bf16)
            lo = (prod - hi.astype(f32)).astype(bf16)
            return (jnp.dot(hi, sel_m, preferred_element_type=f32) + jnp.dot(lo, sel_m, preferred_element_type=f32))

        def gates_back(r, fwd, dmix):
            o_f, o_s, fz, sz, sg_f, sg_s, silu_f, silu_s = fwd[:8]
            dm_f = dmix[:, :512]
            dm_s = dmix[:, 512:]
            do_f = dm_f * silu_f
            do_s = dm_s * silu_s
            dfz_ref[r, :] = (dm_f * o_f * (sg_f * (1.0 + fz * (1.0 - sg_f)))).astype(bf16)
            dsz_ref[r, :] = (dm_s * o_s * (sg_s * (1.0 + sz * (1.0 - sg_s)))).astype(bf16)
            dof_ref[:, r] = do_f.T.astype(bf16)
            dos_ref[:, r] = do_s.T.astype(bf16)
            dlf_ref[:, r] = head_sums(do_f * o_f).T[:FOX_HEADS, :]
            dls_ref[:, r] = head_sums(do_s * o_s).T[:SWA_HEADS, :]

        halves = [slice(k * (tm // 2), (k + 1) * (tm // 2)) for k in range(2)]
        fwds = [forward(r) for r in halves]
        dmixes = [norm_and_back(r, f) for r, f in zip(halves, fwds)]
        for r, f, d in zip(halves, fwds, dmixes):
            gates_back(r, f, d)

    feat = pl.BlockSpec((512, tm), lambda n: (0, n))
    rows8 = pl.BlockSpec((8, tm), lambda n: (0, n))
    half = pl.BlockSpec((tm, 512), lambda n: (n, 0))
    fullw = pl.BlockSpec((tm, D_MODEL), lambda n: (n, 0))
    vec = pl.BlockSpec((1, D_MODEL), lambda n: (0, 0))
    return pl.pallas_call(
        body,
        name="post_fwd_bwd",
        grid=(s_len // tm,),
        in_specs=[feat, half, feat, half, fullw, fullw,
                  pl.BlockSpec((D_MODEL, D_MODEL), lambda n: (0, 0)), vec, vec,
                  pl.BlockSpec((512, 128), lambda n: (0, 0))],
        out_specs=[fullw, feat, half, feat, half, rows8, rows8,
                   pl.BlockSpec((D_MODEL, D_MODEL), lambda n: (0, 0)), vec, vec,
                   pl.BlockSpec((1, 1), lambda n: (0, 0))],
        out_shape=[_sds((s_len, D_MODEL), f32), _sds((512, s_len), bf16), _sds((s_len, 512), bf16),
                   _sds((512, s_len), bf16), _sds((s_len, 512), bf16),
                   _sds((FOX_HEADS, s_len), f32), _sds((SWA_HEADS, s_len), f32),
                   _sds((D_MODEL, D_MODEL), f32), _sds((1, D_MODEL), f32), _sds((1, D_MODEL), f32),
                   _sds((1, 1), f32)],
        compiler_params=_params(("arbitrary",), VMEM_LIMIT_BIG),
    )(of, fz, osw, sz, x2, tgt, wo, ln_g, ln_b, sel)


def _fox_bwd_call(kat, v, qat, dot, lse_row, dl_row, psave, msave, npairs, pair_q, pair_k):
    nh, _, s_len = kat.shape
    t = FOX_T
    nt = s_len // t
    nslots = psave.shape[1]
    assert nt >= 2 and nt % 2 == 0
    ck_slot = HEAD_DIM + 3
    cq_slot = HEAD_DIM

    def body(np_ref, pq_ref, pk_ref, kat_ref, v_ref, qat_ref, dot_ref, lse_ref, dl_ref, psave_ref, msave_ref,
             dq_ref, dk_ref, dv_ref, dcq_ref, dck_ref, dqt_all, dkat_all, dvt_all, ring, p0, p1, ds0, ds1,
             do0, do1, load_sems):
        h = pl.program_id(0)
        total = nt + np_ref[h]
        dqt_all[...] = jnp.zeros(dqt_all.shape, f32)
        dkat_all[...] = jnp.zeros(dkat_all.shape, f32)
        dvt_all[...] = jnp.zeros(dvt_all.shape, f32)
        pbuf, dsbuf, dobuf = (p0, p1), (ds0, ds1), (do0, do1)

        def pair(n):
            return _fox_pair(n, nt, nslots - nt, h, pq_ref, pk_ref)

        def load(q):
            slot = lax.rem(q, FOX_RING)
            return pltpu.make_async_copy(psave_ref.at[h, q], ring.at[slot], load_sems.at[slot])

        def ring_turn(n):
            for q in (n, n + 1):
                @pl.when(q < total)
                def _():
                    load(q).wait()
            for q in (n + FOX_RING - 2, n + FOX_RING - 1):
                @pl.when(q < total)
                def _():
                    load(q).start()

        def scaled_do(n, qc):
            corr = jnp.exp(msave_ref[0, n] - lse_ref[0, :, qc])
            return (dot_ref[0, :, qc].astype(f32) * corr).astype(bf16), dl_ref[0, :, qc] * corr

        def take_tile(n, b, doc):
            tile = ring[lax.rem(n, FOX_RING)]
            pbuf[b][...] = tile
            dobuf[b][...] = doc
            return tile

        def probs(n, b):
            i, j = pair(n)
            qc, kr = _tile_cols(i, t), _tile_cols(j, t)
            doc, dlc = scaled_do(n, qc)
            dpt = jnp.dot(v_ref[0, kr, :], doc, preferred_element_type=f32)
            tile = take_tile(n, b, doc)
            dsbuf[b][...] = (tile.astype(f32) * (dpt - dlc)).astype(bf16)

        def grads(n, b):
            i, j = pair(n)
            qc, kc = _tile_cols(i, t), _tile_cols(j, t)
            dvt_all[:, kc] += lax.dot_general(dobuf[b][...], pbuf[b][...], NT, preferred_element_type=f32)
            dkat_all[:, kc] += lax.dot_general(qat_ref[0, :, qc], dsbuf[b][...], NT, preferred_element_type=f32)
            dqt_all[:, qc] += jnp.dot(kat_ref[0, :, kc], dsbuf[b][...], preferred_element_type=f32)

        def step(n, b):
            i, j = pair(n)
            qc, kr = _tile_cols(i, t), _tile_cols(j, t)
            i1, j1 = pair(n - 1)
            qc1, kc1 = _tile_cols(i1, t), _tile_cols(j1, t)
            c = 1 - b
            doc, dlc = scaled_do(n, qc)
            dpt = jnp.dot(v_ref[0, kr, :], doc, preferred_element_type=f32)
            dvt_all[:, kc1] += lax.dot_general(dobuf[c][...], pbuf[c][...], NT, preferred_element_type=f32)
            tile = take_tile(n, b, doc)
            dkat_all[:, kc1] += lax.dot_general(qat_ref[0, :, qc1], dsbuf[c][...], NT, preferred_element_type=f32)
            dqt_all[:, qc1] += jnp.dot(kat_ref[0, :, kc1], dsbuf[c][...], preferred_element_type=f32)
            dsbuf[b][...] = (tile.astype(f32) * (dpt - dlc)).astype(bf16)

        for q in range(FOX_RING - 2):
            @pl.when(q < total)
            def _():
                load(q).start()
        ring_turn(0)
        probs(0, 0)
        step(1, 1)

        def two_steps(d, _):
            n = 2 + 2 * d
            ring_turn(n)
            step(n, 0)
            step(n + 1, 1)
            return 0

        lax.fori_loop(0, (total - 2) // 2, two_steps, 0)
        ring_turn(total - total % 2)

        @pl.when(total % 2 == 1)
        def _():
            step(total - 1, 0)
            grads(total - 1, 0)

        @pl.when(total % 2 == 0)
        def _():
            grads(total - 1, 1)

        dq_ref[0] = (dqt_all[0:HEAD_DIM, :] * SCALE).astype(bf16)
        dk_ref[0] = dkat_all[0:HEAD_DIM, :].astype(bf16)
        dv_ref[0] = dvt_all[...].astype(bf16)
        dcq_ref[0] = dqt_all[cq_slot:cq_slot + 1, :]
        dck_ref[0] = dkat_all[ck_slot:ck_slot + 1, :]

    smem = pl.BlockSpec(memory_space=pltpu.SMEM)
    feat = pl.BlockSpec((1, AUG, s_len), lambda h: (h, 0, 0))
    feat64 = pl.BlockSpec((1, HEAD_DIM, s_len), lambda h: (h, 0, 0))
    rowv = pl.BlockSpec((1, 1, s_len), lambda h: (h, 0, 0))
    return pl.pallas_call(
        body,
        name="fox_bwd",
        grid=(nh,),
        in_specs=[smem, smem, smem, feat, pl.BlockSpec((1, s_len, HEAD_DIM), lambda h: (h, 0, 0)), feat, feat64,
                  rowv, rowv, pl.BlockSpec(memory_space=pl.ANY),
                  pl.BlockSpec((1, nslots, 1, t), lambda h: (h, 0, 0, 0))],
        out_specs=[feat64, feat64, feat64, rowv, rowv],
        out_shape=[_sds((nh, HEAD_DIM, s_len), bf16)] * 3 + [_sds((nh, 1, s_len), f32)] * 2,
        scratch_shapes=[pltpu.VMEM((AUG, s_len), f32), pltpu.VMEM((AUG, s_len), f32), pltpu.VMEM((HEAD_DIM, s_len), f32)]
                       + [pltpu.VMEM((FOX_RING, t, t), bf16)] + [pltpu.VMEM((t, t), bf16)] * 4
                       + [pltpu.VMEM((HEAD_DIM, t), bf16)] * 2 + [pltpu.SemaphoreType.DMA((FOX_RING,))],
        compiler_params=_params(("arbitrary",)),
    )(npairs, pair_q, pair_k, kat, v, qat, dot, lse_row, dl_row, psave, msave)


def _swa_bwd_call(qt, k, kt, v, dot, lse, dl, bias_t, bias0_t, sink):
    s_len = qt.shape[2]
    ts = SWA_TS
    nb = ts // BLOCK
    nsteps = s_len // ts

    def body(qt_ref, kc_ref, kp_ref, ktc_ref, ktp_ref, vc_ref, vp_ref, dot_ref, lse_ref, dl_ref, b_ref, b0_ref,
             sink_ref, dq_ref, dk_ref, dv_ref, dbias_ref, dsink_ref, dk_s, dv_s, tail_k, tail_v, sk_s):
        n = pl.program_id(0)

        @pl.when(n == 0)
        def _():
            dbias_ref[...] = jnp.zeros_like(dbias_ref)
            sk_s[...] = jnp.zeros_like(sk_s)

        @pl.when(n < nsteps)
        def _():
            first = n == 0
            dk_s[...] = jnp.zeros_like(dk_s)
            dv_s[...] = jnp.zeros_like(dv_s)
            groups = range(SWA_KV_HEADS)
            kall = [jnp.concatenate([kp_ref[g], kc_ref[g]], axis=0) for g in groups]
            vall = [jnp.concatenate([vp_ref[g], vc_ref[g]], axis=0) for g in groups]
            ktall = [jnp.concatenate([ktp_ref[g], ktc_ref[g]], axis=1) for g in groups]
            sinks = [_sink_row(sink_ref, g) for g in groups]
            items = [(g, b) for g in groups for b in range(nb)]

            def products(g, b):
                cols = slice(b * BLOCK, (b + 1) * BLOCK)
                win = slice(b * BLOCK, (b + 2) * BLOCK)
                qg = _group_lanes(qt_ref, g, cols)
                dog = _group_lanes(dot_ref, g, cols)
                bias_b = b_ref[g]
                if b == 0:
                    bias_b = jnp.where(first, b0_ref[g], bias_b)
                st = jnp.dot(kall[g][win], qg, preferred_element_type=f32) + bias_b
                dpt = jnp.dot(vall[g][win], dog, preferred_element_type=f32)
                return qg, dog, st, dpt

            def finish(g, b, qg, dog, st, dpt):
                cols = slice(b * BLOCK, (b + 1) * BLOCK)
                win = slice(b * BLOCK, (b + 2) * BLOCK)
                lse_r = _group_lanes(lse_ref, g, cols)
                dl_r = _group_lanes(dl_ref, g, cols)
                pt = jnp.exp(st - lse_r)
                dst = pt * (dpt - dl_r)
                dsb = dst.astype(bf16)
                dk_s[g, :, win] += lax.dot_general(qg, dsb, NT, preferred_element_type=f32)
                dv_s[g, :, win] += lax.dot_general(dog, pt.astype(bf16), NT, preferred_element_type=f32)
                dqg = jnp.dot(ktall[g][:, win], dsb, preferred_element_type=f32) * SCALE
                return dqg, dst, -jnp.exp(sinks[g] - lse_r) * dl_r

            dqs, dsts, sks = {}, {}, {}
            nxt = products(*items[0])
            for idx, (g, b) in enumerate(items):
                cur = nxt
                if idx + 1 < len(items):
                    nxt = products(*items[idx + 1])
                dqs[g, b], dsts[g, b], sks[g, b] = finish(g, b, *cur)
            for g in groups:
                dbias_ref[g] += functools.reduce(lambda a, c: a + c, [dsts[g, b] for b in range(nb)])
                sk_s[g] += functools.reduce(lambda a, c: a + c, [sks[g, b] for b in range(nb)])
                for hh in range(SWA_GROUP):
                    lanes = slice(hh * BLOCK, (hh + 1) * BLOCK)
                    dq_ref[g * SWA_GROUP + hh] = jnp.concatenate(
                        [dqs[g, b][:, lanes] for b in range(nb)], axis=1).astype(bf16)

        @pl.when(n > 0)
        def _():
            last = slice(ts - BLOCK, ts)
            for g in range(SWA_KV_HEADS):
                add_k = jnp.where(n < nsteps, dk_s[g, :, 0:BLOCK], 0.0)
                add_v = jnp.where(n < nsteps, dv_s[g, :, 0:BLOCK], 0.0)
                dk_ref[g, :, 0:ts - BLOCK] = tail_k[g, :, 0:ts - BLOCK].astype(bf16)
                dv_ref[g, :, 0:ts - BLOCK] = tail_v[g, :, 0:ts - BLOCK].astype(bf16)
                dk_ref[g, :, last] = (tail_k[g, :, last] + add_k).astype(bf16)
                dv_ref[g, :, last] = (tail_v[g, :, last] + add_v).astype(bf16)

        @pl.when(n < nsteps)
        def _():
            tail_k[...] = dk_s[:, :, BLOCK:]
            tail_v[...] = dv_s[:, :, BLOCK:]

        @pl.when(n == nsteps)
        def _():
            row = lax.broadcasted_iota(jnp.int32, (SWA_HEADS, 128), 0)
            out = jnp.zeros((SWA_HEADS, 128), f32)
            for h in range(SWA_HEADS):
                g, hh = divmod(h, SWA_GROUP)
                val = jnp.sum(sk_s[g, :, hh * BLOCK:(hh + 1) * BLOCK], axis=1, keepdims=True)
                out = jnp.where(row == h, val, out)
            dsink_ref[...] = out

    last_step = nsteps - 1

    def cl(n):
        return jnp.minimum(n, last_step)

    def prev_blk(n):
        return jnp.maximum(cl(n) * nb - 1, 0)

    feat8 = pl.BlockSpec((SWA_HEADS, HEAD_DIM, ts), lambda n: (0, 0, cl(n)))
    rows8 = pl.BlockSpec((SWA_HEADS, 1, ts), lambda n: (0, 0, cl(n)))
    cur = pl.BlockSpec((SWA_KV_HEADS, ts, HEAD_DIM), lambda n: (0, cl(n), 0))
    prev = pl.BlockSpec((SWA_KV_HEADS, BLOCK, HEAD_DIM), lambda n: (0, prev_blk(n), 0))
    curt = pl.BlockSpec((SWA_KV_HEADS, HEAD_DIM, ts), lambda n: (0, 0, cl(n)))
    prevt = pl.BlockSpec((SWA_KV_HEADS, HEAD_DIM, BLOCK), lambda n: (0, 0, prev_blk(n)))
    bspec = pl.BlockSpec((SWA_KV_HEADS, 2 * BLOCK, SWA_W), lambda n: (0, 0, 0))
    kvout = pl.BlockSpec((SWA_KV_HEADS, HEAD_DIM, ts), lambda n: (0, 0, jnp.maximum(n - 1, 0)))
    return pl.pallas_call(
        body,
        name="swa_bwd",
        grid=(nsteps + 1,),
        in_specs=[feat8, cur, prev, curt, prevt, cur, prev, feat8, rows8, rows8, bspec, bspec,
                  pl.BlockSpec(memory_space=pltpu.SMEM)],
        out_specs=[feat8, kvout, kvout, bspec, pl.BlockSpec((SWA_HEADS, 128), lambda n: (0, 0))],
        out_shape=[_sds((SWA_HEADS, HEAD_DIM, s_len), bf16), _sds((SWA_KV_HEADS, HEAD_DIM, s_len), bf16),
                   _sds((SWA_KV_HEADS, HEAD_DIM, s_len), bf16),
                   _sds((SWA_KV_HEADS, 2 * BLOCK, SWA_W), f32), _sds((SWA_HEADS, 128), f32)],
        scratch_shapes=[pltpu.VMEM((SWA_KV_HEADS, HEAD_DIM, ts + BLOCK), f32),
                        pltpu.VMEM((SWA_KV_HEADS, HEAD_DIM, ts + BLOCK), f32),
                        pltpu.VMEM((SWA_KV_HEADS, HEAD_DIM, ts), f32),
                        pltpu.VMEM((SWA_KV_HEADS, HEAD_DIM, ts), f32),
                        pltpu.VMEM((SWA_KV_HEADS, 1, SWA_W), f32)],
        compiler_params=_params(("arbitrary",)),
    )(qt, k, k, kt, kt, v, v, dot, lse, dl, bias_t, bias0_t, sink)


def _dproj_specs(tm):
    half = pl.BlockSpec((tm, 512), lambda i: (i, 0))
    feat = pl.BlockSpec((512, tm), lambda i: (0, i))
    feat_kv = pl.BlockSpec((128, tm), lambda i: (0, i))
    return [feat, feat, feat, half, feat, feat_kv, feat_kv, half, feat_kv]


def _dx_exchange_call(dh, pieces, w_t, bs, tm):
    s_len = dh.shape[0]
    n = len(bs)
    last = s_len // tm - 1

    def body(*refs):
        dh_ref, dqf_ref, dkf_ref, dvf_ref, dfz_ref, dqs_ref, dks_ref, dvs_ref, dsz_ref, dfft_ref, w_ref = refs[:11]
        b_refs = refs[11:11 + n]
        dx_ref = refs[11 + n]
        r_refs = refs[12 + n:12 + 2 * n]
        sems = refs[12 + 2 * n:]
        i = pl.program_id(0)

        @pl.when(i == 0)
        def _():
            _exchange_start(b_refs, r_refs, sems)

        def tr(ref):
            return ref[...].astype(f32).T.astype(bf16)

        dp = jnp.concatenate([tr(dqf_ref), tr(dkf_ref), tr(dvf_ref), dfz_ref[...], tr(dqs_ref), tr(dks_ref),
                              tr(dvs_ref), dsz_ref[...], tr(dfft_ref)], axis=1)
        dx_ref[...] = ALPHA * dh_ref[...] + jnp.dot(dp, w_ref[...], preferred_element_type=f32)

        @pl.when(i == last)
        def _():
            _exchange_wait(b_refs, r_refs, sems)

    fullw = pl.BlockSpec((tm, D_MODEL), lambda i: (i, 0))
    any_spec = pl.BlockSpec(memory_space=pl.ANY)
    out = pl.pallas_call(
        body,
        name="dx_bwd_exchange",
        grid=(s_len // tm,),
        in_specs=[fullw] + _dproj_specs(tm) + [pl.BlockSpec((A_W, D_MODEL), lambda i: (0, 0))] + [any_spec] * n,
        out_specs=[fullw] + [any_spec] * n,
        out_shape=[_sds((s_len, D_MODEL), f32)] + [_sds(b.shape, b.dtype) for b in bs],
        scratch_shapes=[pltpu.SemaphoreType.DMA((7 * n,)), pltpu.SemaphoreType.DMA((7 * n,)),
                        pltpu.SemaphoreType.DMA((n,))],
        compiler_params=_params(("arbitrary",)),
    )(dh, *pieces, w_t, *bs)
    return out[0], out[1:]


DW_STAGE_ROWS = 384


def _dw_call(x2, pieces, tm):
    s_len = x2.shape[0]
    nt = s_len // tm

    def body(x_ref, dqf_ref, dkf_ref, dvf_ref, dfz_ref, dqs_ref, dks_ref, dvs_ref, dsz_ref, dfft_ref, dw_ref,
             acc_ref, stage_ref, sem):
        i = pl.program_id(0)

        @pl.when(i == 0)
        def _():
            acc_ref[...] = jnp.zeros_like(acc_ref)

        xb = x_ref[...].astype(bf16)

        def add_feat(off, lhs):
            acc_ref[off:off + lhs.shape[0], :] += jnp.dot(lhs, xb, preferred_element_type=f32)

        def add_rows(off, piece):
            acc_ref[off:off + piece.shape[1], :] += lax.dot_general(piece, xb, TN, preferred_element_type=f32)

        add_feat(A_FQ, dqf_ref[...])
        add_feat(A_FK, dkf_ref[...])
        add_feat(A_FV, dvf_ref[...])
        add_rows(A_FZ, dfz_ref[...])
        add_feat(A_SQ, dqs_ref[...])
        add_feat(A_SK, dks_ref[...])
        add_feat(A_SV, dvs_ref[...])
        add_rows(A_SZ, dsz_ref[...])
        add_feat(A_FF, dfft_ref[...].astype(bf16))

        @pl.when(i == nt - 1)
        def _():
            for r in range(A_W // DW_STAGE_ROWS):
                rows = slice(r * DW_STAGE_ROWS, (r + 1) * DW_STAGE_ROWS)
                stage_ref[...] = acc_ref[rows, :].astype(bf16)
                cp = pltpu.make_async_copy(stage_ref, dw_ref.at[rows, :], sem)
                cp.start()
                cp.wait()

    return pl.pallas_call(
        body,
        name="dw_in_bwd",
        grid=(nt,),
        in_specs=[pl.BlockSpec((tm, D_MODEL), lambda i: (i, 0))] + _dproj_specs(tm),
        out_specs=pl.BlockSpec(memory_space=pl.ANY),
        out_shape=_sds((A_W, D_MODEL), bf16),
        scratch_shapes=[pltpu.VMEM((A_W, D_MODEL), f32), pltpu.VMEM((DW_STAGE_ROWS, D_MODEL), bf16),
                        pltpu.SemaphoreType.DMA],
        compiler_params=_params(("arbitrary",), VMEM_LIMIT_BIG),
    )(x2, *pieces)


def _adam_call(recv, w, m, v, tc, name):
    rows, cols = w.shape

    def body(r_ref, w_ref, m_ref, v_ref, g_ref, d_ref, mo_ref, vo_ref):
        g = r_ref[0].astype(f32)
        for p in range(1, N_DEV):
            g = g + r_ref[p].astype(f32)
        mn = ADAM_B1 * m_ref[...] + (1.0 - ADAM_B1) * g
        vn = ADAM_B2 * v_ref[...] + (1.0 - ADAM_B2) * (g * g)
        m_hat = mn / (1.0 - ADAM_B1 ** ADAM_STEP)
        v_hat = vn / (1.0 - ADAM_B2 ** ADAM_STEP)
        g_ref[...] = g
        d_ref[...] = -ADAM_LR * (m_hat / (jnp.sqrt(v_hat) + ADAM_EPS) + ADAM_WD * w_ref[...])
        mo_ref[...] = mn
        vo_ref[...] = vn

    blk = pl.BlockSpec((rows, tc), lambda i: (0, i))
    return pl.pallas_call(
        body,
        name=name,
        grid=(cols // tc,),
        in_specs=[pl.BlockSpec((N_DEV, rows, tc), lambda i: (0, 0, i)), blk, blk, blk],
        out_specs=[blk] * 4,
        out_shape=[_sds((rows, cols), f32)] * 4,
        compiler_params=_params(("arbitrary",)),
    )(recv, w, m, v)


def _pad_cols(a, width=128):
    return jnp.pad(a, ((0, 0), (0, width - a.shape[1])))


def _pack_small(ln_g, ln_b, rel, b_f, sink):
    return jnp.concatenate([
        ln_g.reshape(8, 128), ln_b.reshape(8, 128), _pad_cols(rel),
        jnp.pad(_pad_cols(b_f), ((0, 7), (0, 0))), jnp.pad(_pad_cols(sink), ((0, 7), (0, 0)))], axis=0)


def _unpack_small(p):
    return (p[0:8].reshape(1, D_MODEL), p[8:16].reshape(1, D_MODEL), p[16:48, 0:8], p[48:49, 0:8], p[56:57, 0:8])


def kernel(x, w_in, b_f, rel_bias, sink, w_o, ln_g, ln_b, loss_target, m_w_in, m_b_f, m_rel_bias, m_sink, m_w_o, m_ln_g, m_ln_b, v_w_in, v_b_f, v_rel_bias, v_sink, v_w_o, v_ln_g, v_ln_b):
    x2 = x[0]
    tgt = loss_target[0]
    s_len = x2.shape[0]
    shard = w_in.shape[2]

    w_in_t = jnp.transpose(w_in[0])
    g_in, g_o = _gather_call([w_in_t.astype(bf16), w_o[0].astype(bf16)])
    wt_full = g_in.reshape(N_DEV * shard, D_MODEL)
    w_t = jnp.concatenate([wt_full[:O_FF0], wt_full[O_FF1:], wt_full[O_FF0:O_FF1],
                           jnp.zeros((A_W - D_IN, D_MODEL), bf16)], axis=0)
    wo_full = g_o.reshape(D_MODEL, D_MODEL)

    qft, kft, vf, fz, qst, ks, vs, sz, fft, vat, kst, vsta = _proj_call(x2, w_t, 512)
    cum, sgm = _cum_call(fft, b_f.reshape(FOX_HEADS, 1))
    qat, ka, kat, tile_stats = _augment_call(qft, kft, cum.reshape(FOX_HEADS, 1, s_len), 2048)
    npairs, pair_q, pair_k = _fox_prune_tables(tile_stats)
    o_ft, lse_f, p_tiles, m_rows = _fox_fwd_call(qat, ka, vat, npairs, pair_q, pair_k)
    bucket_t = jnp.asarray(_t5_bucket_table().T)
    bias_t, bias0_t = _swa_bias_call(rel_bias, bucket_t)
    sink_v = sink.reshape(SWA_HEADS)
    o_st, lse_s = _swa_fwd_call(qst, ks, vsta, bias_t, bias0_t, sink_v)

    (dh, do_f, dfz, do_s, dsz, dl_f, dl_s, dwo, dg, db, loss_part) = _post_call(
        o_ft.reshape(FOX_HEADS * HEAD_DIM, s_len), fz, o_st.reshape(SWA_HEADS * HEAD_DIM, s_len), sz, x2, tgt,
        wo_full, ln_g, ln_b, jnp.asarray(_head_selector()).astype(bf16), 512)

    dqf, dkf, dvf, dcq, dck = _fox_bwd_call(kat, vf, qat, do_f.reshape(FOX_HEADS, HEAD_DIM, s_len), lse_f,
                                            dl_f.reshape(FOX_HEADS, 1, s_len), p_tiles, m_rows, npairs, pair_q, pair_k)
    dqf, dkf, dvf = (a.reshape(FOX_HEADS * HEAD_DIM, s_len) for a in (dqf, dkf, dvf))
    dfft, dbf = _cum_bwd_call(dcq.reshape(FOX_HEADS, s_len), dck.reshape(FOX_HEADS, s_len), sgm)
    dqs, dks, dvs, dbias, dsink = _swa_bwd_call(
        qst, ks, kst, vs, do_s.reshape(SWA_HEADS, HEAD_DIM, s_len), lse_s, dl_s.reshape(SWA_HEADS, 1, s_len),
        bias_t, bias0_t, sink_v)
    dqs = dqs.reshape(SWA_HEADS * HEAD_DIM, s_len)
    dks, dvs = (a.reshape(SWA_KV_HEADS * HEAD_DIM, s_len) for a in (dks, dvs))
    drel = _swa_bias_bwd_call(dbias, bucket_t)

    pieces = (dqf, dkf, dvf, dfz, dqs, dks, dvs, dsz, dfft)
    dw_t = _dw_call(x2, pieces, 1024)

    dwt_full = jnp.concatenate([dw_t[:O_FF0], dw_t[A_FF:A_FF + (O_FF1 - O_FF0)], dw_t[O_FF0:A_FF]], axis=0)
    dw_blocks = dwt_full.reshape(N_DEV, shard, D_MODEL)
    dwo_blocks = dwo.reshape(N_DEV, D_MODEL // N_DEV, D_MODEL).astype(bf16)
    small = _pack_small(dg, db, drel[:, 0:8], dbf[:, 0].reshape(1, 8), dsink[:, 0].reshape(1, 8))
    loss_slot = np.zeros((64, 128), bool)
    loss_slot[49, 0] = True
    small = jnp.where(jnp.asarray(loss_slot), loss_part[0, 0], small)
    small_blocks = jnp.broadcast_to(small[None], (N_DEV,) + small.shape)
    dx, (r_in, r_o, r_small) = _dx_exchange_call(dh, pieces, w_t, [dw_blocks, dwo_blocks, small_blocks], 256)

    win_t = [jnp.transpose(a) for a in _adam_call(
        r_in, w_in_t, jnp.transpose(m_w_in[0]), jnp.transpose(v_w_in[0]), 256, "adam_w_in")]
    g_win, d_win, nm_win, nv_win = win_t
    g_wo, d_wo, nm_wo, nv_wo = _adam_call(r_o, w_o[0], m_w_o[0], v_w_o[0], 256, "adam_w_o")
    p_w = _pack_small(ln_g, ln_b, rel_bias, b_f, sink)
    p_m = _pack_small(m_ln_g, m_ln_b, m_rel_bias, m_b_f, m_sink)
    p_v = _pack_small(v_ln_g, v_ln_b, v_rel_bias, v_b_f, v_sink)
    g_p, d_p, nm_p, nv_p = _adam_call(r_small, p_w, p_m, p_v, 128, "adam_small")

    loss = g_p[49, 0]
    g_lng, g_lnb, g_rel, g_bf, g_sink = _unpack_small(g_p)
    d_lng, d_lnb, d_rel, d_bf, d_sink = _unpack_small(d_p)
    m_lng, m_lnb, m_rel, m_bf, m_sk = _unpack_small(nm_p)
    v_lng, v_lnb, v_rel, v_bf, v_sk = _unpack_small(nv_p)
    return (loss, dx[None], g_win[None], g_bf, g_rel, g_sink, g_wo[None], g_lng, g_lnb,
            d_win[None], d_bf, d_rel, d_sink, d_wo[None], d_lng, d_lnb,
            nm_win[None], m_bf, m_rel, m_sk, nm_wo[None], m_lng, m_lnb,
            nv_win[None], v_bf, v_rel, v_sk, nv_wo[None], v_lng, v_lnb)
```

```python
import functools
import math

import numpy as np
import jax
import jax.numpy as jnp
from jax import lax
from jax.experimental import pallas as pl
from jax.experimental.pallas import tpu as pltpu

f32 = jnp.float32
bf16 = jnp.bfloat16

D_MODEL = 1024
HEAD_DIM = 64
FOX_HEADS = 8
SWA_HEADS = 8
SWA_KV_HEADS = 2
SWA_GROUP = 4
BLOCK = 128
NUM_BUCKETS = 32
MAX_DISTANCE = 128
LN_EPS = 1e-5
NEG_INF = -1e30
ALPHA = 2.0 ** 0.25
SCALE = 1.0 / math.sqrt(HEAD_DIM)
D_IN = 3336

ADAM_LR = 0.001
ADAM_B1 = 0.9
ADAM_B2 = 0.999
ADAM_EPS = 1e-08
ADAM_WD = 0.01
ADAM_STEP = 10

N_DEV = 8
A_FQ, A_FK, A_FV, A_FZ, A_SQ, A_SK, A_SV, A_SZ, A_FF, A_W = 0, 512, 1024, 1536, 2048, 2560, 2688, 2816, 3328, 3456
O_FF0, O_FF1 = 1536, 1544

VMEM_LIMIT = 48 * 1024 * 1024
HIGHEST = lax.Precision.HIGHEST
NT = (((1,), (1,)), ((), ()))
TN = (((0,), (0,)), ((), ()))
MESH = pl.DeviceIdType.MESH
RELS = [(0, 0, 1), (0, 1, 0), (0, 1, 1), (1, 0, 0), (1, 0, 1), (1, 1, 0), (1, 1, 1)]


VMEM_LIMIT_BIG = 60 * 1024 * 1024


def _params(sem=None, vmem=VMEM_LIMIT):
    return pltpu.CompilerParams(dimension_semantics=sem, vmem_limit_bytes=vmem)


def _sds(shape, dtype):
    return jax.ShapeDtypeStruct(shape, dtype)


def _t5_bucket_table():
    qi = np.arange(BLOCK)[:, None]
    kj = np.arange(2 * BLOCK)[None, :]
    rel = qi + BLOCK - kj
    band = (rel >= 0) & (rel < BLOCK)
    relc = np.maximum(rel, 0)
    max_exact = NUM_BUCKETS // 2
    relf = np.maximum(relc, 1).astype(np.float32)
    large = max_exact + (np.log(relf / np.float32(max_exact)) / np.float32(math.log(MAX_DISTANCE / max_exact))
                         * np.float32(NUM_BUCKETS - max_exact)).astype(np.int32)
    large = np.minimum(large, NUM_BUCKETS - 1)
    bucket = np.where(relc < max_exact, relc, large).astype(np.int32)
    bucket = np.where(band, bucket, -1).astype(np.int32)
    return bucket


def _mesh_pos():
    return lax.axis_index("x"), lax.axis_index("y"), lax.axis_index("c")


def _dev_index(p):
    return 4 * p[0] + 2 * p[1] + p[2]


def _gather_call(xs):
    n = len(xs)

    def body(*refs):
        x_refs, o_refs = refs[:n], refs[n:2 * n]
        send_sems, recv_sems, local_sems = refs[2 * n:]
        x, y, c = _mesh_pos()
        me, sib = (x, y, c), (x, y, 1 - c)
        chips = [(1 - x, y), (x, 1 - y), (1 - x, 1 - y)]

        def copy(a, k, block, to, src=None):
            slot = o_refs[a].at[_dev_index(block)]
            return pltpu.make_async_remote_copy(
                src_ref=slot if src is None else src, dst_ref=slot,
                send_sem=send_sems.at[a * 7 + k], recv_sem=recv_sems.at[a * 7 + k],
                device_id=to, device_id_type=MESH)

        mine = [pltpu.make_async_copy(x_refs[a], o_refs[a].at[_dev_index(me)], local_sems.at[a]) for a in range(n)]
        for cp in mine:
            cp.start()
        first = []
        for a in range(n):
            first.append(copy(a, 0, me, sib, src=x_refs[a]))
            first += [copy(a, 1 + j, me, (*chip, c), src=x_refs[a]) for j, chip in enumerate(chips)]
        for cp in first:
            cp.start()
        passed = []
        for j, chip in enumerate(chips):
            for a in range(n):
                copy(a, 1 + j, (*chip, c), me).wait_recv()
                fwd = copy(a, 4 + j, (*chip, c), sib)
                fwd.start()
                passed.append(fwd)
        for a in range(n):
            copy(a, 0, sib, me).wait_recv()
            for j, chip in enumerate(chips):
                copy(a, 4 + j, (*chip, 1 - c), me).wait_recv()
        for cp in first + passed:
            cp.wait_send()
        for cp in mine:
            cp.wait()

    any_spec = pl.BlockSpec(memory_space=pl.ANY)
    return pl.pallas_call(
        body,
        name="gather_weights",
        out_shape=[_sds((N_DEV,) + a.shape, a.dtype) for a in xs],
        in_specs=[any_spec] * n,
        out_specs=[any_spec] * n,
        scratch_shapes=[pltpu.SemaphoreType.DMA((7 * n,)), pltpu.SemaphoreType.DMA((7 * n,)),
                        pltpu.SemaphoreType.DMA((n,))],
    )(*xs)


def _exchange_copies(b_refs, r_refs, send_sems, recv_sems, local_sems, incoming):
    n = len(b_refs)
    x, y, c = _mesh_pos()
    me_idx = _dev_index((x, y, c))
    mine = [pltpu.make_async_copy(b_refs[a].at[me_idx], r_refs[a].at[me_idx], local_sems.at[a]) for a in range(n)]
    remote = []
    for k, r in enumerate(RELS):
        peer = ((1 - x) if r[0] else x, (1 - y) if r[1] else y, (1 - c) if r[2] else c)
        pidx = _dev_index(peer)
        for a in range(n):
            remote.append(pltpu.make_async_remote_copy(
                src_ref=b_refs[a].at[pidx], dst_ref=r_refs[a].at[pidx if incoming else me_idx],
                send_sem=send_sems.at[a * 7 + k], recv_sem=recv_sems.at[a * 7 + k],
                device_id=peer, device_id_type=MESH))
    return mine, remote


def _exchange_start(b_refs, r_refs, sems):
    mine, out = _exchange_copies(b_refs, r_refs, *sems, incoming=False)
    for cp in mine + out:
        cp.start()


def _exchange_wait(b_refs, r_refs, sems):
    mine, inc = _exchange_copies(b_refs, r_refs, *sems, incoming=True)
    for cp in inc:
        cp.wait_recv()
    for cp in inc:
        cp.wait_send()
    for cp in mine:
        cp.wait()


def _proj_call(x2, w_t, tm):
    s_len = x2.shape[0]

    def body(x_ref, w_ref, qft_ref, kft_ref, vf_ref, fz_ref, qst_ref, ks_ref, vs_ref, sz_ref, fft_ref, vat_ref,
             kst_ref, vsta_ref):
        xb = x_ref[...].astype(bf16)

        def seg_t(off, width):
            return lax.dot_general(w_ref[off:off + width, :], xb, NT, preferred_element_type=f32)

        def seg(off, width):
            return lax.dot_general(xb, w_ref[off:off + width, :], NT, preferred_element_type=f32)

        def put_heads(ref, acc, nheads):
            for h in range(nheads):
                ref[h] = acc[:, h * HEAD_DIM:(h + 1) * HEAD_DIM].astype(bf16)

        def put_heads_t(ref, acc_t, nheads, augment):
            for h in range(nheads):
                ref[h, 0:HEAD_DIM, :] = acc_t[h * HEAD_DIM:(h + 1) * HEAD_DIM, :].astype(bf16)
                if augment:
                    ref[h, HEAD_DIM:2 * HEAD_DIM, :] = ones_row

        ones_row = jnp.where(lax.broadcasted_iota(jnp.int32, (HEAD_DIM, tm), 0) == 0, 1.0, 0.0).astype(bf16)
        put_heads_t(vat_ref, seg_t(A_FV, 512), FOX_HEADS, True)
        put_heads_t(qft_ref, seg_t(A_FQ, 512) * SCALE, FOX_HEADS, False)
        put_heads_t(kft_ref, seg_t(A_FK, 512), FOX_HEADS, False)
        put_heads(vf_ref, seg(A_FV, 512), FOX_HEADS)
        fz_ref[...] = seg(A_FZ, 512)
        put_heads_t(qst_ref, seg_t(A_SQ, 512) * SCALE, SWA_HEADS, False)
        put_heads(ks_ref, seg(A_SK, 128), SWA_KV_HEADS)
        put_heads(vs_ref, seg(A_SV, 128), SWA_KV_HEADS)
        put_heads_t(kst_ref, seg_t(A_SK, 128), SWA_KV_HEADS, False)
        put_heads_t(vsta_ref, seg_t(A_SV, 128), SWA_KV_HEADS, True)
        sz_ref[...] = seg(A_SZ, 512)
        fft_ref[...] = seg(A_FF, 128).T[:FOX_HEADS, :]

    def heads(nh):
        return pl.BlockSpec((nh, tm, HEAD_DIM), lambda i: (0, i, 0))

    def feat(nh, rows):
        return pl.BlockSpec((nh, rows, tm), lambda i: (0, 0, i))

    wide = pl.BlockSpec((tm, 512), lambda i: (i, 0))
    return pl.pallas_call(
        body,
        name="proj_fwd",
        grid=(s_len // tm,),
        in_specs=[pl.BlockSpec((tm, D_MODEL), lambda i: (i, 0)), pl.BlockSpec((A_W, D_MODEL), lambda i: (0, 0))],
        out_specs=[feat(8, HEAD_DIM), feat(8, HEAD_DIM), heads(8), wide, feat(8, HEAD_DIM), heads(2), heads(2), wide,
                   pl.BlockSpec((FOX_HEADS, tm), lambda i: (0, i)),
                   feat(FOX_HEADS, 2 * HEAD_DIM), feat(2, HEAD_DIM), feat(2, 2 * HEAD_DIM)],
        out_shape=[_sds((8, HEAD_DIM, s_len), bf16)] * 2 + [_sds((8, s_len, HEAD_DIM), bf16)]
                  + [_sds((s_len, 512), f32), _sds((8, HEAD_DIM, s_len), bf16),
                     _sds((2, s_len, HEAD_DIM), bf16), _sds((2, s_len, HEAD_DIM), bf16), _sds((s_len, 512), f32),
                     _sds((FOX_HEADS, s_len), f32), _sds((FOX_HEADS, 2 * HEAD_DIM, s_len), bf16),
                     _sds((2, HEAD_DIM, s_len), bf16), _sds((2, 2 * HEAD_DIM, s_len), bf16)],
        compiler_params=_params(("arbitrary",)),
    )(x2, w_t)


AUG = 2 * HEAD_DIM


def _augment_call(q_t, k_t, cum_row, tm):
    nh, _, s_len = k_t.shape
    per_step = tm // FOX_T

    def body(qt_ref, kt_ref, c_ref, qat_ref, ka_ref, kat_ref, st_ref):
        c = c_ref[0]
        hi = c.astype(bf16).astype(f32)
        r1 = c - hi
        mid = r1.astype(bf16).astype(f32)
        lo = (r1 - mid).astype(bf16).astype(f32)
        row = lax.broadcasted_iota(jnp.int32, (HEAD_DIM, tm), 0)
        q_tail = jnp.where(row == 0, hi, jnp.where(row == 1, mid, jnp.where(row == 2, lo,
                           jnp.where(row < 6, 1.0, 0.0))))
        k_tail = jnp.where(row < 3, 1.0, jnp.where(row == 3, -hi, jnp.where(row == 4, -mid,
                           jnp.where(row == 5, -lo, 0.0))))
        qat_ref[0, 0:HEAD_DIM, :] = qt_ref[0]
        qat_ref[0, HEAD_DIM:AUG, :] = q_tail.astype(bf16)
        kat_ref[0, 0:HEAD_DIM, :] = kt_ref[0]
        kat_ref[0, HEAD_DIM:AUG, :] = k_tail.astype(bf16)
        qt = qt_ref[0].astype(f32)
        kt = kt_ref[0].astype(f32)
        ka_ref[0] = jnp.concatenate([kt, k_tail], axis=0).T.astype(bf16)
        qn2 = jnp.sum(qt * qt, axis=0, keepdims=True)
        kn2 = jnp.sum(kt * kt, axis=0, keepdims=True)
        sd = jnp.sum(qt * kt, axis=0, keepdims=True)
        srow = lax.broadcasted_iota(jnp.int32, (8, LANES), 0)
        for part in range(per_step):
            sl = slice(part * FOX_T, (part + 1) * FOX_T)
            vals = [jnp.sqrt(jnp.max(qn2[:, sl], axis=1, keepdims=True)),
                    jnp.sqrt(jnp.max(kn2[:, sl], axis=1, keepdims=True)),
                    jnp.min(sd[:, sl], axis=1, keepdims=True),
                    jnp.max(c[:, sl], axis=1, keepdims=True), jnp.min(c[:, sl], axis=1, keepdims=True)]
            out = jnp.zeros((8, LANES), f32)
            for r, val in enumerate(vals):
                out = jnp.where(srow == r, val, out)
            st_ref[0, part] = out

    tile_t = pl.BlockSpec((1, HEAD_DIM, tm), lambda h, i: (h, 0, i))
    return pl.pallas_call(
        body,
        name="fox_augment",
        grid=(nh, s_len // tm),
        in_specs=[tile_t, tile_t, pl.BlockSpec((1, 1, tm), lambda h, i: (h, 0, i))],
        out_specs=[pl.BlockSpec((1, AUG, tm), lambda h, i: (h, 0, i)),
                   pl.BlockSpec((1, tm, AUG), lambda h, i: (h, i, 0)),
                   pl.BlockSpec((1, AUG, tm), lambda h, i: (h, 0, i)),
                   pl.BlockSpec((1, per_step, 8, LANES), lambda h, i: (h, i, 0, 0))],
        out_shape=[_sds((nh, AUG, s_len), bf16), _sds((nh, s_len, AUG), bf16), _sds((nh, AUG, s_len), bf16),
                   _sds((nh, s_len // FOX_T, 8, LANES), f32)],
        compiler_params=_params(("arbitrary", "arbitrary")),
    )(q_t, k_t, cum_row)


EXP_ZERO_GAP = 110.0


def _fox_prune_tables(stats):
    s = stats[:, :, :, 0]
    qn, kn, sd, cmx, cmn = (s[:, :, r] for r in range(5))
    nt = s.shape[1]
    bound = qn[:, :, None] * kn[:, None, :] + (cmx[:, :, None] - cmn[:, None, :])
    margin = 2.0 + 1e-5 * (jnp.abs(cmx)[:, :, None] + jnp.abs(cmn)[:, None, :])
    qi = lax.broadcasted_iota(jnp.int32, (nt, nt), 0)
    kj = lax.broadcasted_iota(jnp.int32, (nt, nt), 1)
    skip = (bound + margin < sd[:, :, None] - EXP_ZERO_GAP) & (kj < qi)[None]
    first = jnp.sum(jnp.cumprod(skip.astype(jnp.int32), axis=2), axis=2)
    tiles = lax.broadcasted_iota(jnp.int32, (1, nt), 1)
    cnt = tiles - first
    ends = jnp.cumsum(cnt, axis=1)
    off = ends - cnt
    kmax = nt * (nt - 1) // 2
    k = lax.broadcasted_iota(jnp.int32, (1, kmax), 1)
    pair_q = jnp.minimum(jnp.sum((ends[:, None, :] <= k[:, :, None]).astype(jnp.int32), axis=2), nt - 1)
    hit = pair_q[:, :, None] == tiles[:, None, :]
    first_k = jnp.sum(jnp.where(hit, first[:, None, :], 0), axis=2)
    off_k = jnp.sum(jnp.where(hit, off[:, None, :], 0), axis=2)
    pair_k = jnp.clip(first_k + k - off_k, 0, nt - 1)
    return (ends[:, nt - 1].astype(jnp.int32), pair_q.reshape(-1).astype(jnp.int32),
            pair_k.reshape(-1).astype(jnp.int32))


CUM_CHUNK = 512


def _cum_call(fft, bf_col):
    s_len = fft.shape[1]
    ch = CUM_CHUNK

    def body(f_ref, b_ref, cum_ref, sg_ref):
        r = lax.broadcasted_iota(jnp.int32, (ch, ch), 0)
        c = lax.broadcasted_iota(jnp.int32, (ch, ch), 1)
        upper = (r <= c).astype(f32)
        carry = jnp.zeros((FOX_HEADS, 1), f32)
        for n in range(s_len // ch):
            z = f_ref[:, n * ch:(n + 1) * ch] + b_ref[...]
            logf = jnp.minimum(z, 0.0) - jnp.log1p(jnp.exp(-jnp.abs(z)))
            sg_ref[:, n * ch:(n + 1) * ch] = 1.0 / (1.0 + jnp.exp(z))
            cs = jnp.dot(logf, upper, precision=HIGHEST, preferred_element_type=f32) + carry
            cum_ref[:, n * ch:(n + 1) * ch] = cs
            carry = cs[:, ch - 1:ch]

    return pl.pallas_call(
        body,
        name="fox_cum_fwd",
        out_shape=[_sds((FOX_HEADS, s_len), f32)] * 2,
        compiler_params=_params(),
    )(fft, bf_col)


def _cum_bwd_call(dcq, dck, sg):
    s_len = sg.shape[1]
    ch = CUM_CHUNK
    nch = s_len // ch

    def body(q_ref, k_ref, sg_ref, dff_ref, dbf_ref):
        r = lax.broadcasted_iota(jnp.int32, (ch, ch), 0)
        c = lax.broadcasted_iota(jnp.int32, (ch, ch), 1)
        lower = (r >= c).astype(f32)
        dff_ref[...] = jnp.zeros_like(dff_ref)
        carry = jnp.zeros((FOX_HEADS, 1), f32)
        total = jnp.zeros((FOX_HEADS, 1), f32)
        for n in reversed(range(nch)):
            sl = slice(n * ch, (n + 1) * ch)
            dcum = q_ref[:, sl] - k_ref[:, sl]
            rs = jnp.dot(dcum, lower, precision=HIGHEST, preferred_element_type=f32) + carry
            carry = rs[:, 0:1]
            dff = rs * sg_ref[:, sl]
            dff_ref[0:FOX_HEADS, sl] = dff
            total = total + jnp.sum(dff, axis=1, keepdims=True)
        dbf_ref[...] = jnp.broadcast_to(total, (FOX_HEADS, 128))

    return pl.pallas_call(
        body,
        name="fox_cum_bwd",
        out_shape=[_sds((128, s_len), f32), _sds((FOX_HEADS, 128), f32)],
        compiler_params=_params(),
    )(dcq, dck, sg)


FOX_T = 512
FOX_RING = 16
LANES = 128


def _causal_keep(t):
    return lax.broadcasted_iota(jnp.int32, (t, t), 0) <= lax.broadcasted_iota(jnp.int32, (t, t), 1)


def _tile_cols(i, t):
    return pl.ds(pl.multiple_of(i * t, t), t)


def _fox_pair(n, nt, kmax, h, pq_ref, pk_ref):
    k = h * kmax + jnp.maximum(n - nt, 0)
    return jnp.where(n < nt, n, pq_ref[k]), jnp.where(n < nt, n, pk_ref[k])


def _fox_fwd_call(qat, ka, vat, npairs, pair_q, pair_k):
    nh, s_len, _ = ka.shape
    t = FOX_T
    nt = s_len // t
    kmax = nt * (nt - 1) // 2
    nslots = nt + kmax
    assert nt >= 4 and nt % 2 == 0

    def body(np_ref, pq_ref, pk_ref, qat_ref, ka_ref, vat_ref, o_ref, lse_ref, psave_ref, msave_ref,
             s0, s1, p0, p1, a0, a1, m_all, acc_all, ring, save_sems):
        h = pl.program_id(0)
        extra = np_ref[h]
        total = nt + extra
        m_all[...] = jnp.full(m_all.shape, NEG_INF, f32)
        acc_all[...] = jnp.zeros(acc_all.shape, f32)
        bufs = ((s0, p0, a0), (s1, p1, a1))

        def pair(n):
            return _fox_pair(n, nt, kmax, h, pq_ref, pk_ref)

        def scores(n, b, masked):
            i, j = pair(n)
            st = jnp.dot(ka_ref[0, _tile_cols(j, t), :], qat_ref[0, :, _tile_cols(i, t)], preferred_element_type=f32)
            if masked:
                st = jnp.where(_causal_keep(t), st, NEG_INF)
            bufs[b][0][...] = st

        def save(q):
            slot = lax.rem(q, FOX_RING)
            return pltpu.make_async_copy(ring.at[slot], psave_ref.at[h, q], save_sems.at[slot])

        def ring_turn(n):
            for q in (n - 3, n - 2):
                @pl.when(q >= 0)
                def _():
                    save(q).start()
            for q in (n - 1 - FOX_RING, n - FOX_RING):
                @pl.when(q >= 0)
                def _():
                    save(q).wait()

        def softmax(n, b):
            i, _ = pair(n)
            s_ref, p_ref, a_ref = bufs[b]
            slot = lax.rem(n, FOX_RING)
            for c in range(t // LANES):
                cols = slice(c * LANES, (c + 1) * LANES)
                mcols = pl.ds(pl.multiple_of(i * t + c * LANES, LANES), LANES)
                m_old = m_all[:, mcols]
                m_new = jnp.maximum(m_old, jnp.max(s_ref[:, cols], axis=0, keepdims=True))
                m_all[:, mcols] = m_new
                msave_ref[0, n, :, cols] = m_new
                a_ref[:, cols] = jnp.exp(m_old - m_new)
                p = jnp.exp(s_ref[:, cols] - m_new).astype(bf16)
                p_ref[:, cols] = p
                ring[slot, :, cols] = p

        def accum(n, b):
            i, j = pair(n)
            cols = _tile_cols(i, t)
            acc_all[:, cols] = bufs[b][2][...] * acc_all[:, cols] + jnp.dot(
                vat_ref[0, :, _tile_cols(j, t)], bufs[b][1][...], preferred_element_type=f32)

        def step(n, b, masked):
            accum(n - 2, b)
            softmax(n - 1, 1 - b)
            scores(n, b, masked)

        scores(0, 0, True)
        scores(1, 1, True)
        softmax(0, 0)

        def diag_steps(d, _):
            n = 2 + 2 * d
            ring_turn(n)
            step(n, 0, True)
            step(n + 1, 1, True)
            return 0

        lax.fori_loop(0, (nt - 2) // 2, diag_steps, 0)

        def off_steps(d, _):
            n = nt + 2 * d
            ring_turn(n)
            step(n, 0, False)
            step(n + 1, 1, False)
            return 0

        lax.fori_loop(0, extra // 2, off_steps, 0)
        n_end = total - extra % 2
        ring_turn(n_end)

        @pl.when(extra % 2 == 1)
        def _():
            step(total - 1, 0, False)
            softmax(total - 1, 0)
            accum(total - 2, 1)
            accum(total - 1, 0)

        @pl.when(extra % 2 == 0)
        def _():
            softmax(total - 1, 1)
            accum(total - 2, 0)
            accum(total - 1, 1)

        def start_rest(q, _):
            save(q).start()
            return 0

        def wait_rest(q, _):
            save(q).wait()
            return 0

        lax.fori_loop(n_end - 1, total, start_rest, 0)
        lax.fori_loop(jnp.maximum(n_end - FOX_RING + 1, 0), total, wait_rest, 0)
        l = acc_all[HEAD_DIM:HEAD_DIM + 1, :]
        o_ref[0] = acc_all[0:HEAD_DIM, :] / l
        lse_ref[0] = m_all[...] + jnp.log(l)

    smem = pl.BlockSpec(memory_space=pltpu.SMEM)
    return pl.pallas_call(
        body,
        name="fox_fwd",
        grid=(nh,),
        in_specs=[smem, smem, smem,
                  pl.BlockSpec((1, AUG, s_len), lambda h: (h, 0, 0)),
                  pl.BlockSpec((1, s_len, AUG), lambda h: (h, 0, 0)),
                  pl.BlockSpec((1, AUG, s_len), lambda h: (h, 0, 0))],
        out_specs=[pl.BlockSpec((1, HEAD_DIM, s_len), lambda h: (h, 0, 0)),
                   pl.BlockSpec((1, 1, s_len), lambda h: (h, 0, 0)),
                   pl.BlockSpec(memory_space=pl.ANY),
                   pl.BlockSpec((1, nslots, 1, t), lambda h: (h, 0, 0, 0))],
        out_shape=[_sds((nh, HEAD_DIM, s_len), f32), _sds((nh, 1, s_len), f32), _sds((nh, nslots, t, t), bf16),
                   _sds((nh, nslots, 1, t), f32)],
        scratch_shapes=[pltpu.VMEM((t, t), f32), pltpu.VMEM((t, t), f32), pltpu.VMEM((t, t), bf16),
                        pltpu.VMEM((t, t), bf16), pltpu.VMEM((1, t), f32), pltpu.VMEM((1, t), f32),
                        pltpu.VMEM((1, s_len), f32), pltpu.VMEM((AUG, s_len), f32), pltpu.VMEM((FOX_RING, t, t), bf16),
                        pltpu.SemaphoreType.DMA((FOX_RING,))],
        compiler_params=_params(("arbitrary",)),
    )(npairs, pair_q, pair_k, qat, ka, vat)


SWA_TS = 512


SWA_W = SWA_GROUP * BLOCK


def _swa_bias_call(rel_bias, bucket_t):
    def body(rb_ref, bk_ref, b_ref, b0_ref):
        bk = bk_ref[...]
        row = lax.broadcasted_iota(jnp.int32, (2 * BLOCK, BLOCK), 0)
        for h in range(SWA_HEADS):
            acc = jnp.full((2 * BLOCK, BLOCK), NEG_INF, f32)
            for b in range(NUM_BUCKETS):
                acc = jnp.where(bk == b, rb_ref[b, h], acc)
            g, hh = divmod(h, SWA_GROUP)
            b_ref[g, :, hh * BLOCK:(hh + 1) * BLOCK] = acc
            b0_ref[g, :, hh * BLOCK:(hh + 1) * BLOCK] = jnp.where(row < BLOCK, NEG_INF, acc)

    return pl.pallas_call(
        body,
        name="swa_bias",
        in_specs=[pl.BlockSpec(memory_space=pltpu.SMEM), pl.BlockSpec(memory_space=pltpu.VMEM)],
        out_shape=[_sds((SWA_KV_HEADS, 2 * BLOCK, SWA_W), f32)] * 2,
        compiler_params=_params(),
    )(rel_bias, bucket_t)


def _swa_bias_bwd_call(dbias, bucket_t):
    def body(d_ref, bk_ref, o_ref):
        bk = bk_ref[...]
        row = lax.broadcasted_iota(jnp.int32, (NUM_BUCKETS, 128), 0)
        col = lax.broadcasted_iota(jnp.int32, (NUM_BUCKETS, 128), 1)
        out = jnp.zeros((NUM_BUCKETS, 128), f32)
        for h in range(SWA_HEADS):
            g, hh = divmod(h, SWA_GROUP)
            d = d_ref[g, :, hh * BLOCK:(hh + 1) * BLOCK]
            for b in range(NUM_BUCKETS):
                val = jnp.sum(jnp.sum(jnp.where(bk == b, d, 0.0), axis=1, keepdims=True), axis=0, keepdims=True)
                out = jnp.where((row == b) & (col == h), val, out)
        o_ref[...] = out

    return pl.pallas_call(
        body,
        name="swa_bias_bwd",
        out_shape=_sds((NUM_BUCKETS, 128), f32),
        compiler_params=_params(),
    )(dbias, bucket_t)


def _sink_row(sink_ref, g):
    return jnp.concatenate([jnp.full((1, BLOCK), sink_ref[g * SWA_GROUP + hh], f32) for hh in range(SWA_GROUP)], axis=1)


def _group_lanes(ref, g, cols):
    return jnp.concatenate([ref[g * SWA_GROUP + hh, :, cols] for hh in range(SWA_GROUP)], axis=1)


def _swa_fwd_call(qt, k, vta, bias_t, bias0_t, sink):
    s_len = qt.shape[2]
    ts = SWA_TS
    nb = ts // BLOCK

    def body(qt_ref, kc_ref, kp_ref, vc_ref, vp_ref, b_ref, b0_ref, sink_ref, o_ref, lse_ref):
        first = pl.program_id(0) == 0
        kall = [jnp.concatenate([kp_ref[g], kc_ref[g]], axis=0) for g in range(SWA_KV_HEADS)]
        vall = [jnp.concatenate([vp_ref[g], vc_ref[g]], axis=1) for g in range(SWA_KV_HEADS)]
        sinks = [_sink_row(sink_ref, g) for g in range(SWA_KV_HEADS)]
        items = [(g, b) for g in range(SWA_KV_HEADS) for b in range(nb)]

        def scores(g, b):
            qg = _group_lanes(qt_ref, g, slice(b * BLOCK, (b + 1) * BLOCK))
            bias_b = b_ref[g]
            if b == 0:
                bias_b = jnp.where(first, b0_ref[g], bias_b)
            return jnp.dot(kall[g][b * BLOCK:(b + 2) * BLOCK], qg, preferred_element_type=f32) + bias_b

        def finish(g, b, st):
            m = jnp.maximum(jnp.max(st, axis=0, keepdims=True), sinks[g])
            pt = jnp.exp(st - m)
            acc = jnp.dot(vall[g][:, b * BLOCK:(b + 2) * BLOCK], pt.astype(bf16), preferred_element_type=f32)
            l = acc[HEAD_DIM:HEAD_DIM + 1, :] + jnp.exp(sinks[g] - m)
            return acc[0:HEAD_DIM, :] / l, m + jnp.log(l)

        outs, lses = {}, {}
        st_next = scores(*items[0])
        for idx, (g, b) in enumerate(items):
            st = st_next
            if idx + 1 < len(items):
                st_next = scores(*items[idx + 1])
            outs[g, b], lses[g, b] = finish(g, b, st)
        for g in range(SWA_KV_HEADS):
            for hh in range(SWA_GROUP):
                lanes = slice(hh * BLOCK, (hh + 1) * BLOCK)
                o_ref[g * SWA_GROUP + hh] = jnp.concatenate([outs[g, b][:, lanes] for b in range(nb)], axis=1)
                lse_ref[g * SWA_GROUP + hh] = jnp.concatenate([lses[g, b][:, lanes] for b in range(nb)], axis=1)

    def prev_blk(n):
        return jnp.maximum(n * nb - 1, 0)

    bspec = pl.BlockSpec((SWA_KV_HEADS, 2 * BLOCK, SWA_W), lambda n: (0, 0, 0))
    return pl.pallas_call(
        body,
        name="swa_fwd",
        grid=(s_len // ts,),
        in_specs=[pl.BlockSpec((SWA_HEADS, HEAD_DIM, ts), lambda n: (0, 0, n)),
                  pl.BlockSpec((SWA_KV_HEADS, ts, HEAD_DIM), lambda n: (0, n, 0)),
                  pl.BlockSpec((SWA_KV_HEADS, BLOCK, HEAD_DIM), lambda n: (0, prev_blk(n), 0)),
                  pl.BlockSpec((SWA_KV_HEADS, AUG, ts), lambda n: (0, 0, n)),
                  pl.BlockSpec((SWA_KV_HEADS, AUG, BLOCK), lambda n: (0, 0, prev_blk(n))),
                  bspec, bspec, pl.BlockSpec(memory_space=pltpu.SMEM)],
        out_specs=[pl.BlockSpec((SWA_HEADS, HEAD_DIM, ts), lambda n: (0, 0, n)),
                   pl.BlockSpec((SWA_HEADS, 1, ts), lambda n: (0, 0, n))],
        out_shape=[_sds((SWA_HEADS, HEAD_DIM, s_len), f32), _sds((SWA_HEADS, 1, s_len), f32)],
        compiler_params=_params(("arbitrary",)),
    )(qt, k, k, vta, vta, bias_t, bias0_t, sink)


def _head_selector():
    sel = np.zeros((512, 128), np.float32)
    for h in range(8):
        sel[h * HEAD_DIM:(h + 1) * HEAD_DIM, h] = 1.0
    return sel


def _post_call(of, fz, osw, sz, x2, tgt, wo, ln_g, ln_b, sel, tm):
    s_len = x2.shape[0]

    def body(of_ref, fz_ref, os_ref, sz_ref, x_ref, t_ref, wo_ref, g_ref, b_ref, sel_ref,
             dh_ref, dof_ref, dfz_ref, dos_ref, dsz_ref, dlf_ref, dls_ref, dwo_ref, dg_ref, db_ref, loss_ref):
        n = pl.program_id(0)

        @pl.when(n == 0)
        def _():
            dwo_ref[...] = jnp.zeros_like(dwo_ref)
            dg_ref[...] = jnp.zeros_like(dg_ref)
            db_ref[...] = jnp.zeros_like(db_ref)
            loss_ref[...] = jnp.zeros_like(loss_ref)

        gam = g_ref[...]
        sel_m = sel_ref[...]

        def forward(r):
            o_f = of_ref[:, r].T
            o_s = os_ref[:, r].T
            fz = fz_ref[r, :]
            sz = sz_ref[r, :]
            sg_f = jax.nn.sigmoid(fz)
            sg_s = jax.nn.sigmoid(sz)
            silu_f = fz * sg_f
            silu_s = sz * sg_s
            mixed = jnp.concatenate([o_f * silu_f, o_s * silu_s], axis=1).astype(bf16)
            y = jnp.dot(mixed, wo_ref[...], preferred_element_type=f32)
            return o_f, o_s, fz, sz, sg_f, sg_s, silu_f, silu_s, mixed, y

        def norm_and_back(r, fwd):
            mixed, y = fwd[8], fwd[9]
            h = ALPHA * x_ref[r, :] + y
            mu = jnp.mean(h, axis=1, keepdims=True)
            hc = h - mu
            var = jnp.mean(hc * hc, axis=1, keepdims=True)
            rstd = lax.rsqrt(var + LN_EPS)
            xhat = hc * rstd
            out = xhat * gam + b_ref[...]
            err = out - t_ref[r, :]
            tok_loss = jnp.mean(err * err, axis=1, keepdims=True)
            loss_ref[...] += 0.5 * jnp.sum(tok_loss, axis=0, keepdims=True)
            dout = err * (1.0 / D_MODEL)
            dg_ref[...] += jnp.sum(dout * xhat, axis=0, keepdims=True)
            db_ref[...] += jnp.sum(dout, axis=0, keepdims=True)
            dxh = dout * gam
            m1 = jnp.mean(dxh, axis=1, keepdims=True)
            m2 = jnp.mean(dxh * xhat, axis=1, keepdims=True)
            dh = rstd * (dxh - m1 - xhat * m2)
            dh_ref[r, :] = dh
            dyb = dh.astype(bf16)
            dmix = lax.dot_general(dyb, wo_ref[...], NT, preferred_element_type=f32)
            dwo_ref[...] += lax.dot_general(mixed, dyb, TN, preferred_element_type=f32)
            return dmix

        def head_sums(prod):
            hi = prod.astype(bf16)
            lo = (prod - hi.astype(f32)).astype(bf16)
            return (jnp.dot(hi, sel_m, preferred_element_type=f32) + jnp.dot(lo, sel_m, preferred_element_type=f32))

        def gates_back(r, fwd, dmix):
            o_f, o_s, fz, sz, sg_f, sg_s, silu_f, silu_s = fwd[:8]
            dm_f = dmix[:, :512]
            dm_s = dmix[:, 512:]
            do_f = dm_f * silu_f
            do_s = dm_s * silu_s
            dfz_ref[r, :] = (dm_f * o_f * (sg_f * (1.0 + fz * (1.0 - sg_f)))).astype(bf16)
            dsz_ref[r, :] = (dm_s * o_s * (sg_s * (1.0 + sz * (1.0 - sg_s)))).astype(bf16)
            dof_ref[:, r] = do_f.T.astype(bf16)
            dos_ref[:, r] = do_s.T.astype(bf16)
            dlf_ref[:, r] = head_sums(do_f * o_f).T[:FOX_HEADS, :]
            dls_ref[:, r] = head_sums(do_s * o_s).T[:SWA_HEADS, :]

        halves = [slice(k * (tm // 2), (k + 1) * (tm // 2)) for k in range(2)]
        fwds = [forward(r) for r in halves]
        dmixes = [norm_and_back(r, f) for r, f in zip(halves, fwds)]
        for r, f, d in zip(halves, fwds, dmixes):
            gates_back(r, f, d)

    feat = pl.BlockSpec((512, tm), lambda n: (0, n))
    rows8 = pl.BlockSpec((8, tm), lambda n: (0, n))
    half = pl.BlockSpec((tm, 512), lambda n: (n, 0))
    fullw = pl.BlockSpec((tm, D_MODEL), lambda n: (n, 0))
    vec = pl.BlockSpec((1, D_MODEL), lambda n: (0, 0))
    return pl.pallas_call(
        body,
        name="post_fwd_bwd",
        grid=(s_len // tm,),
        in_specs=[feat, half, feat, half, fullw, fullw,
                  pl.BlockSpec((D_MODEL, D_MODEL), lambda n: (0, 0)), vec, vec,
                  pl.BlockSpec((512, 128), lambda n: (0, 0))],
        out_specs=[fullw, feat, half, feat, half, rows8, rows8,
                   pl.BlockSpec((D_MODEL, D_MODEL), lambda n: (0, 0)), vec, vec,
                   pl.BlockSpec((1, 1), lambda n: (0, 0))],
        out_shape=[_sds((s_len, D_MODEL), f32), _sds((512, s_len), bf16), _sds((s_len, 512), bf16),
                   _sds((512, s_len), bf16), _sds((s_len, 512), bf16),
                   _sds((FOX_HEADS, s_len), f32), _sds((SWA_HEADS, s_len), f32),
                   _sds((D_MODEL, D_MODEL), f32), _sds((1, D_MODEL), f32), _sds((1, D_MODEL), f32),
                   _sds((1, 1), f32)],
        compiler_params=_params(("arbitrary",), VMEM_LIMIT_BIG),
    )(of, fz, osw, sz, x2, tgt, wo, ln_g, ln_b, sel)


def _fox_bwd_call(kat, v, qat, dot, lse_row, dl_row, psave, msave, npairs, pair_q, pair_k):
    nh, _, s_len = kat.shape
    t = FOX_T
    nt = s_len // t
    nslots = psave.shape[1]
    assert nt >= 2 and nt % 2 == 0
    ck_slot = HEAD_DIM + 3
    cq_slot = HEAD_DIM

    def body(np_ref, pq_ref, pk_ref, kat_ref, v_ref, qat_ref, dot_ref, lse_ref, dl_ref, psave_ref, msave_ref,
             dq_ref, dk_ref, dv_ref, dcq_ref, dck_ref, dqt_all, dkat_all, dvt_all, ring, p0, p1, ds0, ds1,
             do0, do1, load_sems):
        h = pl.program_id(0)
        total = nt + np_ref[h]
        dqt_all[...] = jnp.zeros(dqt_all.shape, f32)
        dkat_all[...] = jnp.zeros(dkat_all.shape, f32)
        dvt_all[...] = jnp.zeros(dvt_all.shape, f32)
        pbuf, dsbuf, dobuf = (p0, p1), (ds0, ds1), (do0, do1)

        def pair(n):
            return _fox_pair(n, nt, nslots - nt, h, pq_ref, pk_ref)

        def load(q):
            slot = lax.rem(q, FOX_RING)
            return pltpu.make_async_copy(psave_ref.at[h, q], ring.at[slot], load_sems.at[slot])

        def ring_turn(n):
            for q in (n, n + 1):
                @pl.when(q < total)
                def _():
                    load(q).wait()
            for q in (n + FOX_RING - 2, n + FOX_RING - 1):
                @pl.when(q < total)
                def _():
                    load(q).start()

        def scaled_do(n, qc):
            corr = jnp.exp(msave_ref[0, n] - lse_ref[0, :, qc])
            return (dot_ref[0, :, qc].astype(f32) * corr).astype(bf16), dl_ref[0, :, qc] * corr

        def take_tile(n, b, doc):
            tile = ring[lax.rem(n, FOX_RING)]
            pbuf[b][...] = tile
            dobuf[b][...] = doc
            return tile

        def probs(n, b):
            i, j = pair(n)
            qc, kr = _tile_cols(i, t), _tile_cols(j, t)
            doc, dlc = scaled_do(n, qc)
            dpt = jnp.dot(v_ref[0, kr, :], doc, preferred_element_type=f32)
            tile = take_tile(n, b, doc)
            dsbuf[b][...] = (tile.astype(f32) * (dpt - dlc)).astype(bf16)

        def grads(n, b):
            i, j = pair(n)
            qc, kc = _tile_cols(i, t), _tile_cols(j, t)
            dvt_all[:, kc] += lax.dot_general(dobuf[b][...], pbuf[b][...], NT, preferred_element_type=f32)
            dkat_all[:, kc] += lax.dot_general(qat_ref[0, :, qc], dsbuf[b][...], NT, preferred_element_type=f32)
            dqt_all[:, qc] += jnp.dot(kat_ref[0, :, kc], dsbuf[b][...], preferred_element_type=f32)

        def step(n, b):
            i, j = pair(n)
            qc, kr = _tile_cols(i, t), _tile_cols(j, t)
            i1, j1 = pair(n - 1)
            qc1, kc1 = _tile_cols(i1, t), _tile_cols(j1, t)
            c = 1 - b
            doc, dlc = scaled_do(n, qc)
            dpt = jnp.dot(v_ref[0, kr, :], doc, preferred_element_type=f32)
            dvt_all[:, kc1] += lax.dot_general(dobuf[c][...], pbuf[c][...], NT, preferred_element_type=f32)
            tile = take_tile(n, b, doc)
            dkat_all[:, kc1] += lax.dot_general(qat_ref[0, :, qc1], dsbuf[c][...], NT, preferred_element_type=f32)
            dqt_all[:, qc1] += jnp.dot(kat_ref[0, :, kc1], dsbuf[c][...], preferred_element_type=f32)
            dsbuf[b][...] = (tile.astype(f32) * (dpt - dlc)).astype(bf16)

        for q in range(FOX_RING - 2):
            @pl.when(q < total)
            def _():
                load(q).start()
        ring_turn(0)
        probs(0, 0)
        step(1, 1)

        def two_steps(d, _):
            n = 2 + 2 * d
            ring_turn(n)
            step(n, 0)
            step(n + 1, 1)
            return 0

        lax.fori_loop(0, (total - 2) // 2, two_steps, 0)
        ring_turn(total - total % 2)

        @pl.when(total % 2 == 1)
        def _():
            step(total - 1, 0)
            grads(total - 1, 0)

        @pl.when(total % 2 == 0)
        def _():
            grads(total - 1, 1)

        dq_ref[0] = (dqt_all[0:HEAD_DIM, :] * SCALE).astype(bf16)
        dk_ref[0] = dkat_all[0:HEAD_DIM, :].astype(bf16)
        dv_ref[0] = dvt_all[...].astype(bf16)
        dcq_ref[0] = dqt_all[cq_slot:cq_slot + 1, :]
        dck_ref[0] = dkat_all[ck_slot:ck_slot + 1, :]

    smem = pl.BlockSpec(memory_space=pltpu.SMEM)
    feat = pl.BlockSpec((1, AUG, s_len), lambda h: (h, 0, 0))
    feat64 = pl.BlockSpec((1, HEAD_DIM, s_len), lambda h: (h, 0, 0))
    rowv = pl.BlockSpec((1, 1, s_len), lambda h: (h, 0, 0))
    return pl.pallas_call(
        body,
        name="fox_bwd",
        grid=(nh,),
        in_specs=[smem, smem, smem, feat, pl.BlockSpec((1, s_len, HEAD_DIM), lambda h: (h, 0, 0)), feat, feat64,
                  rowv, rowv, pl.BlockSpec(memory_space=pl.ANY),
                  pl.BlockSpec((1, nslots, 1, t), lambda h: (h, 0, 0, 0))],
        out_specs=[feat64, feat64, feat64, rowv, rowv],
        out_shape=[_sds((nh, HEAD_DIM, s_len), bf16)] * 3 + [_sds((nh, 1, s_len), f32)] * 2,
        scratch_shapes=[pltpu.VMEM((AUG, s_len), f32), pltpu.VMEM((AUG, s_len), f32), pltpu.VMEM((HEAD_DIM, s_len), f32)]
                       + [pltpu.VMEM((FOX_RING, t, t), bf16)] + [pltpu.VMEM((t, t), bf16)] * 4
                       + [pltpu.VMEM((HEAD_DIM, t), bf16)] * 2 + [pltpu.SemaphoreType.DMA((FOX_RING,))],
        compiler_params=_params(("arbitrary",)),
    )(npairs, pair_q, pair_k, kat, v, qat, dot, lse_row, dl_row, psave, msave)


def _swa_bwd_call(qt, k, kt, v, dot, lse, dl, bias_t, bias0_t, sink):
    s_len = qt.shape[2]
    ts = SWA_TS
    nb = ts // BLOCK
    nsteps = s_len // ts

    def body(qt_ref, kc_ref, kp_ref, ktc_ref, ktp_ref, vc_ref, vp_ref, dot_ref, lse_ref, dl_ref, b_ref, b0_ref,
             sink_ref, dq_ref, dk_ref, dv_ref, dbias_ref, dsink_ref, dk_s, dv_s, tail_k, tail_v, sk_s):
        n = pl.program_id(0)

        @pl.when(n == 0)
        def _():
            dbias_ref[...] = jnp.zeros_like(dbias_ref)
            sk_s[...] = jnp.zeros_like(sk_s)

        @pl.when(n < nsteps)
        def _():
            first = n == 0
            dk_s[...] = jnp.zeros_like(dk_s)
            dv_s[...] = jnp.zeros_like(dv_s)
            groups = range(SWA_KV_HEADS)
            kall = [jnp.concatenate([kp_ref[g], kc_ref[g]], axis=0) for g in groups]
            vall = [jnp.concatenate([vp_ref[g], vc_ref[g]], axis=0) for g in groups]
            ktall = [jnp.concatenate([ktp_ref[g], ktc_ref[g]], axis=1) for g in groups]
            sinks = [_sink_row(sink_ref, g) for g in groups]
            items = [(g, b) for g in groups for b in range(nb)]

            def products(g, b):
                cols = slice(b * BLOCK, (b + 1) * BLOCK)
                win = slice(b * BLOCK, (b + 2) * BLOCK)
                qg = _group_lanes(qt_ref, g, cols)
                dog = _group_lanes(dot_ref, g, cols)
                bias_b = b_ref[g]
                if b == 0:
                    bias_b = jnp.where(first, b0_ref[g], bias_b)
                st = jnp.dot(kall[g][win], qg, preferred_element_type=f32) + bias_b
                dpt = jnp.dot(vall[g][win], dog, preferred_element_type=f32)
                return qg, dog, st, dpt

            def finish(g, b, qg, dog, st, dpt):
                cols = slice(b * BLOCK, (b + 1) * BLOCK)
                win = slice(b * BLOCK, (b + 2) * BLOCK)
                lse_r = _group_lanes(lse_ref, g, cols)
                dl_r = _group_lanes(dl_ref, g, cols)
                pt = jnp.exp(st - lse_r)
                dst = pt * (dpt - dl_r)
                dsb = dst.astype(bf16)
                dk_s[g, :, win] += lax.dot_general(qg, dsb, NT, preferred_element_type=f32)
                dv_s[g, :, win] += lax.dot_general(dog, pt.astype(bf16), NT, preferred_element_type=f32)
                dqg = jnp.dot(ktall[g][:, win], dsb, preferred_element_type=f32) * SCALE
                return dqg, dst, -jnp.exp(sinks[g] - lse_r) * dl_r

            dqs, dsts, sks = {}, {}, {}
            nxt = products(*items[0])
            for idx, (g, b) in enumerate(items):
                cur = nxt
                if idx + 1 < len(items):
                    nxt = products(*items[idx + 1])
                dqs[g, b], dsts[g, b], sks[g, b] = finish(g, b, *cur)
            for g in groups:
                dbias_ref[g] += functools.reduce(lambda a, c: a + c, [dsts[g, b] for b in range(nb)])
                sk_s[g] += functools.reduce(lambda a, c: a + c, [sks[g, b] for b in range(nb)])
                for hh in range(SWA_GROUP):
                    lanes = slice(hh * BLOCK, (hh + 1) * BLOCK)
                    dq_ref[g * SWA_GROUP + hh] = jnp.concatenate(
                        [dqs[g, b][:, lanes] for b in range(nb)], axis=1).astype(bf16)

        @pl.when(n > 0)
        def _():
            last = slice(ts - BLOCK, ts)
            for g in range(SWA_KV_HEADS):
                add_k = jnp.where(n < nsteps, dk_s[g, :, 0:BLOCK], 0.0)
                add_v = jnp.where(n < nsteps, dv_s[g, :, 0:BLOCK], 0.0)
                dk_ref[g, :, 0:ts - BLOCK] = tail_k[g, :, 0:ts - BLOCK].astype(bf16)
                dv_ref[g, :, 0:ts - BLOCK] = tail_v[g, :, 0:ts - BLOCK].astype(bf16)
                dk_ref[g, :, last] = (tail_k[g, :, last] + add_k).astype(bf16)
                dv_ref[g, :, last] = (tail_v[g, :, last] + add_v).astype(bf16)

        @pl.when(n < nsteps)
        def _():
            tail_k[...] = dk_s[:, :, BLOCK:]
            tail_v[...] = dv_s[:, :, BLOCK:]

        @pl.when(n == nsteps)
        def _():
            row = lax.broadcasted_iota(jnp.int32, (SWA_HEADS, 128), 0)
            out = jnp.zeros((SWA_HEADS, 128), f32)
            for h in range(SWA_HEADS):
                g, hh = divmod(h, SWA_GROUP)
                val = jnp.sum(sk_s[g, :, hh * BLOCK:(hh + 1) * BLOCK], axis=1, keepdims=True)
                out = jnp.where(row == h, val, out)
            dsink_ref[...] = out

    last_step = nsteps - 1

    def cl(n):
        return jnp.minimum(n, last_step)

    def prev_blk(n):
        return jnp.maximum(cl(n) * nb - 1, 0)

    feat8 = pl.BlockSpec((SWA_HEADS, HEAD_DIM, ts), lambda n: (0, 0, cl(n)))
    rows8 = pl.BlockSpec((SWA_HEADS, 1, ts), lambda n: (0, 0, cl(n)))
    cur = pl.BlockSpec((SWA_KV_HEADS, ts, HEAD_DIM), lambda n: (0, cl(n), 0))
    prev = pl.BlockSpec((SWA_KV_HEADS, BLOCK, HEAD_DIM), lambda n: (0, prev_blk(n), 0))
    curt = pl.BlockSpec((SWA_KV_HEADS, HEAD_DIM, ts), lambda n: (0, 0, cl(n)))
    prevt = pl.BlockSpec((SWA_KV_HEADS, HEAD_DIM, BLOCK), lambda n: (0, 0, prev_blk(n)))
    bspec = pl.BlockSpec((SWA_KV_HEADS, 2 * BLOCK, SWA_W), lambda n: (0, 0, 0))
    kvout = pl.BlockSpec((SWA_KV_HEADS, HEAD_DIM, ts), lambda n: (0, 0, jnp.maximum(n - 1, 0)))
    return pl.pallas_call(
        body,
        name="swa_bwd",
        grid=(nsteps + 1,),
        in_specs=[feat8, cur, prev, curt, prevt, cur, prev, feat8, rows8, rows8, bspec, bspec,
                  pl.BlockSpec(memory_space=pltpu.SMEM)],
        out_specs=[feat8, kvout, kvout, bspec, pl.BlockSpec((SWA_HEADS, 128), lambda n: (0, 0))],
        out_shape=[_sds((SWA_HEADS, HEAD_DIM, s_len), bf16), _sds((SWA_KV_HEADS, HEAD_DIM, s_len), bf16),
                   _sds((SWA_KV_HEADS, HEAD_DIM, s_len), bf16),
                   _sds((SWA_KV_HEADS, 2 * BLOCK, SWA_W), f32), _sds((SWA_HEADS, 128), f32)],
        scratch_shapes=[pltpu.VMEM((SWA_KV_HEADS, HEAD_DIM, ts + BLOCK), f32),
                        pltpu.VMEM((SWA_KV_HEADS, HEAD_DIM, ts + BLOCK), f32),
                        pltpu.VMEM((SWA_KV_HEADS, HEAD_DIM, ts), f32),
                        pltpu.VMEM((SWA_KV_HEADS, HEAD_DIM, ts), f32),
                        pltpu.VMEM((SWA_KV_HEADS, 1, SWA_W), f32)],
        compiler_params=_params(("arbitrary",)),
    )(qt, k, k, kt, kt, v, v, dot, lse, dl, bias_t, bias0_t, sink)


def _dproj_specs(tm):
    half = pl.BlockSpec((tm, 512), lambda i: (i, 0))
    feat = pl.BlockSpec((512, tm), lambda i: (0, i))
    feat_kv = pl.BlockSpec((128, tm), lambda i: (0, i))
    return [feat, feat, feat, half, feat, feat_kv, feat_kv, half, feat_kv]


def _dx_exchange_call(dh, pieces, w_t, bs, tm):
    s_len = dh.shape[0]
    n = len(bs)
    last = s_len // tm - 1

    def body(*refs):
        dh_ref, dqf_ref, dkf_ref, dvf_ref, dfz_ref, dqs_ref, dks_ref, dvs_ref, dsz_ref, dfft_ref, w_ref = refs[:11]
        b_refs = refs[11:11 + n]
        dx_ref = refs[11 + n]
        r_refs = refs[12 + n:12 + 2 * n]
        sems = refs[12 + 2 * n:]
        i = pl.program_id(0)

        @pl.when(i == 0)
        def _():
            _exchange_start(b_refs, r_refs, sems)

        def tr(ref):
            return ref[...].astype(f32).T.astype(bf16)

        dp = jnp.concatenate([tr(dqf_ref), tr(dkf_ref), tr(dvf_ref), dfz_ref[...], tr(dqs_ref), tr(dks_ref),
                              tr(dvs_ref), dsz_ref[...], tr(dfft_ref)], axis=1)
        dx_ref[...] = ALPHA * dh_ref[...] + jnp.dot(dp, w_ref[...], preferred_element_type=f32)

        @pl.when(i == last)
        def _():
            _exchange_wait(b_refs, r_refs, sems)

    fullw = pl.BlockSpec((tm, D_MODEL), lambda i: (i, 0))
    any_spec = pl.BlockSpec(memory_space=pl.ANY)
    out = pl.pallas_call(
        body,
        name="dx_bwd_exchange",
        grid=(s_len // tm,),
        in_specs=[fullw] + _dproj_specs(tm) + [pl.BlockSpec((A_W, D_MODEL), lambda i: (0, 0))] + [any_spec] * n,
        out_specs=[fullw] + [any_spec] * n,
        out_shape=[_sds((s_len, D_MODEL), f32)] + [_sds(b.shape, b.dtype) for b in bs],
        scratch_shapes=[pltpu.SemaphoreType.DMA((7 * n,)), pltpu.SemaphoreType.DMA((7 * n,)),
                        pltpu.SemaphoreType.DMA((n,))],
        compiler_params=_params(("arbitrary",)),
    )(dh, *pieces, w_t, *bs)
    return out[0], out[1:]


DW_STAGE_ROWS = 384


def _dw_call(x2, pieces, tm):
    s_len = x2.shape[0]
    nt = s_len // tm

    def body(x_ref, dqf_ref, dkf_ref, dvf_ref, dfz_ref, dqs_ref, dks_ref, dvs_ref, dsz_ref, dfft_ref, dw_ref,
             acc_ref, stage_ref, sem):
        i = pl.program_id(0)

        @pl.when(i == 0)
        def _():
            acc_ref[...] = jnp.zeros_like(acc_ref)

        xb = x_ref[...].astype(bf16)

        def add_feat(off, lhs):
            acc_ref[off:off + lhs.shape[0], :] += jnp.dot(lhs, xb, preferred_element_type=f32)

        def add_rows(off, piece):
            acc_ref[off:off + piece.shape[1], :] += lax.dot_general(piece, xb, TN, preferred_element_type=f32)

        add_feat(A_FQ, dqf_ref[...])
        add_feat(A_FK, dkf_ref[...])
        add_feat(A_FV, dvf_ref[...])
        add_rows(A_FZ, dfz_ref[...])
        add_feat(A_SQ, dqs_ref[...])
        add_feat(A_SK, dks_ref[...])
        add_feat(A_SV, dvs_ref[...])
        add_rows(A_SZ, dsz_ref[...])
        add_feat(A_FF, dfft_ref[...].astype(bf16))

        @pl.when(i == nt - 1)
        def _():
            for r in range(A_W // DW_STAGE_ROWS):
                rows = slice(r * DW_STAGE_ROWS, (r + 1) * DW_STAGE_ROWS)
                stage_ref[...] = acc_ref[rows, :].astype(bf16)
                cp = pltpu.make_async_copy(stage_ref, dw_ref.at[rows, :], sem)
                cp.start()
                cp.wait()

    return pl.pallas_call(
        body,
        name="dw_in_bwd",
        grid=(nt,),
        in_specs=[pl.BlockSpec((tm, D_MODEL), lambda i: (i, 0))] + _dproj_specs(tm),
        out_specs=pl.BlockSpec(memory_space=pl.ANY),
        out_shape=_sds((A_W, D_MODEL), bf16),
        scratch_shapes=[pltpu.VMEM((A_W, D_MODEL), f32), pltpu.VMEM((DW_STAGE_ROWS, D_MODEL), bf16),
                        pltpu.SemaphoreType.DMA],
        compiler_params=_params(("arbitrary",), VMEM_LIMIT_BIG),
    )(x2, *pieces)


def _adam_call(recv, w, m, v, tc, name):
    rows, cols = w.shape

    def body(r_ref, w_ref, m_ref, v_ref, g_ref, d_ref, mo_ref, vo_ref):
        g = r_ref[0].astype(f32)
        for p in range(1, N_DEV):
            g = g + r_ref[p].astype(f32)
        mn = ADAM_B1 * m_ref[...] + (1.0 - ADAM_B1) * g
        vn = ADAM_B2 * v_ref[...] + (1.0 - ADAM_B2) * (g * g)
        m_hat = mn / (1.0 - ADAM_B1 ** ADAM_STEP)
        v_hat = vn / (1.0 - ADAM_B2 ** ADAM_STEP)
        g_ref[...] = g
        d_ref[...] = -ADAM_LR * (m_hat / (jnp.sqrt(v_hat) + ADAM_EPS) + ADAM_WD * w_ref[...])
        mo_ref[...] = mn
        vo_ref[...] = vn

    blk = pl.BlockSpec((rows, tc), lambda i: (0, i))
    return pl.pallas_call(
        body,
        name=name,
        grid=(cols // tc,),
        in_specs=[pl.BlockSpec((N_DEV, rows, tc), lambda i: (0, 0, i)), blk, blk, blk],
        out_specs=[blk] * 4,
        out_shape=[_sds((rows, cols), f32)] * 4,
        compiler_params=_params(("arbitrary",)),
    )(recv, w, m, v)


def _pad_cols(a, width=128):
    return jnp.pad(a, ((0, 0), (0, width - a.shape[1])))


def _pack_small(ln_g, ln_b, rel, b_f, sink):
    return jnp.concatenate([
        ln_g.reshape(8, 128), ln_b.reshape(8, 128), _pad_cols(rel),
        jnp.pad(_pad_cols(b_f), ((0, 7), (0, 0))), jnp.pad(_pad_cols(sink), ((0, 7), (0, 0)))], axis=0)


def _unpack_small(p):
    return (p[0:8].reshape(1, D_MODEL), p[8:16].reshape(1, D_MODEL), p[16:48, 0:8], p[48:49, 0:8], p[56:57, 0:8])


def kernel(x, w_in, b_f, rel_bias, sink, w_o, ln_g, ln_b, loss_target, m_w_in, m_b_f, m_rel_bias, m_sink, m_w_o, m_ln_g, m_ln_b, v_w_in, v_b_f, v_rel_bias, v_sink, v_w_o, v_ln_g, v_ln_b):
    x2 = x[0]
    tgt = loss_target[0]
    s_len = x2.shape[0]
    shard = w_in.shape[2]

    w_in_t = jnp.transpose(w_in[0])
    g_in, g_o = _gather_call([w_in_t.astype(bf16), w_o[0].astype(bf16)])
    wt_full = g_in.reshape(N_DEV * shard, D_MODEL)
    w_t = jnp.concatenate([wt_full[:O_FF0], wt_full[O_FF1:], wt_full[O_FF0:O_FF1],
                           jnp.zeros((A_W - D_IN, D_MODEL), bf16)], axis=0)
    wo_full = g_o.reshape(D_MODEL, D_MODEL)

    qft, kft, vf, fz, qst, ks, vs, sz, fft, vat, kst, vsta = _proj_call(x2, w_t, 512)
    cum, sgm = _cum_call(fft, b_f.reshape(FOX_HEADS, 1))
    qat, ka, kat, tile_stats = _augment_call(qft, kft, cum.reshape(FOX_HEADS, 1, s_len), 2048)
    npairs, pair_q, pair_k = _fox_prune_tables(tile_stats)
    o_ft, lse_f, p_tiles, m_rows = _fox_fwd_call(qat, ka, vat, npairs, pair_q, pair_k)
    bucket_t = jnp.asarray(_t5_bucket_table().T)
    bias_t, bias0_t = _swa_bias_call(rel_bias, bucket_t)
    sink_v = sink.reshape(SWA_HEADS)
    o_st, lse_s = _swa_fwd_call(qst, ks, vsta, bias_t, bias0_t, sink_v)

    (dh, do_f, dfz, do_s, dsz, dl_f, dl_s, dwo, dg, db, loss_part) = _post_call(
        o_ft.reshape(FOX_HEADS * HEAD_DIM, s_len), fz, o_st.reshape(SWA_HEADS * HEAD_DIM, s_len), sz, x2, tgt,
        wo_full, ln_g, ln_b, jnp.asarray(_head_selector()).astype(bf16), 512)

    dqf, dkf, dvf, dcq, dck = _fox_bwd_call(kat, vf, qat, do_f.reshape(FOX_HEADS, HEAD_DIM, s_len), lse_f,
                                            dl_f.reshape(FOX_HEADS, 1, s_len), p_tiles, m_rows, npairs, pair_q, pair_k)
    dqf, dkf, dvf = (a.reshape(FOX_HEADS * HEAD_DIM, s_len) for a in (dqf, dkf, dvf))
    dfft, dbf = _cum_bwd_call(dcq.reshape(FOX_HEADS, s_len), dck.reshape(FOX_HEADS, s_len), sgm)
    dqs, dks, dvs, dbias, dsink = _swa_bwd_call(
        qst, ks, kst, vs, do_s.reshape(SWA_HEADS, HEAD_DIM, s_len), lse_s, dl_s.reshape(SWA_HEADS, 1, s_len),
        bias_t, bias0_t, sink_v)
    dqs = dqs.reshape(SWA_HEADS * HEAD_DIM, s_len)
    dks, dvs = (a.reshape(SWA_KV_HEADS * HEAD_DIM, s_len) for a in (dks, dvs))
    drel = _swa_bias_bwd_call(dbias, bucket_t)

    pieces = (dqf, dkf, dvf, dfz, dqs, dks, dvs, dsz, dfft)
    dw_t = _dw_call(x2, pieces, 1024)

    dwt_full = jnp.concatenate([dw_t[:O_FF0], dw_t[A_FF:A_FF + (O_FF1 - O_FF0)], dw_t[O_FF0:A_FF]], axis=0)
    dw_blocks = dwt_full.reshape(N_DEV, shard, D_MODEL)
    dwo_blocks = dwo.reshape(N_DEV, D_MODEL // N_DEV, D_MODEL).astype(bf16)
    small = _pack_small(dg, db, drel[:, 0:8], dbf[:, 0].reshape(1, 8), dsink[:, 0].reshape(1, 8))
    loss_slot = np.zeros((64, 128), bool)
    loss_slot[49, 0] = True
    small = jnp.where(jnp.asarray(loss_slot), loss_part[0, 0], small)
    small_blocks = jnp.broadcast_to(small[None], (N_DEV,) + small.shape)
    dx, (r_in, r_o, r_small) = _dx_exchange_call(dh, pieces, w_t, [dw_blocks, dwo_blocks, small_blocks], 256)

    win_t = [jnp.transpose(a) for a in _adam_call(
        r_in, w_in_t, jnp.transpose(m_w_in[0]), jnp.transpose(v_w_in[0]), 256, "adam_w_in")]
    g_win, d_win, nm_win, nv_win = win_t
    g_wo, d_wo, nm_wo, nv_wo = _adam_call(r_o, w_o[0], m_w_o[0], v_w_o[0], 256, "adam_w_o")
    p_w = _pack_small(ln_g, ln_b, rel_bias, b_f, sink)
    p_m = _pack_small(m_ln_g, m_ln_b, m_rel_bias, m_b_f, m_sink)
    p_v = _pack_small(v_ln_g, v_ln_b, v_rel_bias, v_b_f, v_sink)
    g_p, d_p, nm_p, nv_p = _adam_call(r_small, p_w, p_m, p_v, 128, "adam_small")

    loss = g_p[49, 0]
    g_lng, g_lnb, g_rel, g_bf, g_sink = _unpack_small(g_p)
    d_lng, d_lnb, d_rel, d_bf, d_sink = _unpack_small(d_p)
    m_lng, m_lnb, m_rel, m_bf, m_sk = _unpack_small(nm_p)
    v_lng, v_lnb, v_rel, v_bf, v_sk = _unpack_small(nv_p)
    return (loss, dx[None], g_win[None], g_bf, g_rel, g_sink, g_wo[None], g_lng, g_lnb,
            d_win[None], d_bf, d_rel, d_sink, d_wo[None], d_lng, d_lnb,
            nm_win[None], m_bf, m_rel, m_sk, nm_wo[None], m_lng, m_lnb,
            nv_win[None], v_bf, v_rel, v_sk, nv_wo[None], v_lng, v_lnb)
```

```python
import functools
import math

import numpy as np
import jax
import jax.numpy as jnp
from jax import lax
from jax.experimental import pallas as pl
from jax.experimental.pallas import tpu as pltpu

f32 = jnp.float32
bf16 = jnp.bfloat16

D_MODEL = 1024
HEAD_DIM = 64
FOX_HEADS = 8
SWA_HEADS = 8
SWA_KV_HEADS = 2
SWA_GROUP = 4
BLOCK = 128
NUM_BUCKETS = 32
MAX_DISTANCE = 128
LN_EPS = 1e-5
NEG_INF = -1e30
ALPHA = 2.0 ** 0.25
SCALE = 1.0 / math.sqrt(HEAD_DIM)
D_IN = 3336

ADAM_LR = 0.001
ADAM_B1 = 0.9
ADAM_B2 = 0.999
ADAM_EPS = 1e-08
ADAM_WD = 0.01
ADAM_STEP = 10

N_DEV = 8
A_FQ, A_FK, A_FV, A_FZ, A_SQ, A_SK, A_SV, A_SZ, A_FF, A_W = 0, 512, 1024, 1536, 2048, 2560, 2688, 2816, 3328, 3456
O_FF0, O_FF1 = 1536, 1544

VMEM_LIMIT = 48 * 1024 * 1024
HIGHEST = lax.Precision.HIGHEST
NT = (((1,), (1,)), ((), ()))
TN = (((0,), (0,)), ((), ()))
MESH = pl.DeviceIdType.MESH
RELS = [(0, 0, 1), (0, 1, 0), (0, 1, 1), (1, 0, 0), (1, 0, 1), (1, 1, 0), (1, 1, 1)]


VMEM_LIMIT_BIG = 60 * 1024 * 1024


def _params(sem=None, vmem=VMEM_LIMIT):
    return pltpu.CompilerParams(dimension_semantics=sem, vmem_limit_bytes=vmem)


def _sds(shape, dtype):
    return jax.ShapeDtypeStruct(shape, dtype)


def _t5_bucket_table():
    qi = np.arange(BLOCK)[:, None]
    kj = np.arange(2 * BLOCK)[None, :]
    rel = qi + BLOCK - kj
    band = (rel >= 0) & (rel < BLOCK)
    relc = np.maximum(rel, 0)
    max_exact = NUM_BUCKETS // 2
    relf = np.maximum(relc, 1).astype(np.float32)
    large = max_exact + (np.log(relf / np.float32(max_exact)) / np.float32(math.log(MAX_DISTANCE / max_exact))
                         * np.float32(NUM_BUCKETS - max_exact)).astype(np.int32)
    large = np.minimum(large, NUM_BUCKETS - 1)
    bucket = np.where(relc < max_exact, relc, large).astype(np.int32)
    bucket = np.where(band, bucket, -1).astype(np.int32)
    return bucket


def _mesh_pos():
    return lax.axis_index("x"), lax.axis_index("y"), lax.axis_index("c")


def _dev_index(p):
    return 4 * p[0] + 2 * p[1] + p[2]


def _gather_call(xs):
    n = len(xs)

    def body(*refs):
        x_refs, o_refs = refs[:n], refs[n:2 * n]
        send_sems, recv_sems, local_sems = refs[2 * n:]
        x, y, c = _mesh_pos()
        me, sib = (x, y, c), (x, y, 1 - c)
        chips = [(1 - x, y), (x, 1 - y), (1 - x, 1 - y)]

        def copy(a, k, block, to, src=None):
            slot = o_refs[a].at[_dev_index(block)]
            return pltpu.make_async_remote_copy(
                src_ref=slot if src is None else src, dst_ref=slot,
                send_sem=send_sems.at[a * 7 + k], recv_sem=recv_sems.at[a * 7 + k],
                device_id=to, device_id_type=MESH)

        mine = [pltpu.make_async_copy(x_refs[a], o_refs[a].at[_dev_index(me)], local_sems.at[a]) for a in range(n)]
        for cp in mine:
            cp.start()
        first = []
        for a in range(n):
            first.append(copy(a, 0, me, sib, src=x_refs[a]))
            first += [copy(a, 1 + j, me, (*chip, c), src=x_refs[a]) for j, chip in enumerate(chips)]
        for cp in first:
            cp.start()
        passed = []
        for j, chip in enumerate(chips):
            for a in range(n):
                copy(a, 1 + j, (*chip, c), me).wait_recv()
                fwd = copy(a, 4 + j, (*chip, c), sib)
                fwd.start()
                passed.append(fwd)
        for a in range(n):
            copy(a, 0, sib, me).wait_recv()
            for j, chip in enumerate(chips):
                copy(a, 4 + j, (*chip, 1 - c), me).wait_recv()
        for cp in first + passed:
            cp.wait_send()
        for cp in mine:
            cp.wait()

    any_spec = pl.BlockSpec(memory_space=pl.ANY)
    return pl.pallas_call(
        body,
        name="gather_weights",
        out_shape=[_sds((N_DEV,) + a.shape, a.dtype) for a in xs],
        in_specs=[any_spec] * n,
        out_specs=[any_spec] * n,
        scratch_shapes=[pltpu.SemaphoreType.DMA((7 * n,)), pltpu.SemaphoreType.DMA((7 * n,)),
                        pltpu.SemaphoreType.DMA((n,))],
    )(*xs)


def _exchange_copies(b_refs, r_refs, send_sems, recv_sems, local_sems, incoming):
    n = len(b_refs)
    x, y, c = _mesh_pos()
    me_idx = _dev_index((x, y, c))
    mine = [pltpu.make_async_copy(b_refs[a].at[me_idx], r_refs[a].at[me_idx], local_sems.at[a]) for a in range(n)]
    remote = []
    for k, r in enumerate(RELS):
        peer = ((1 - x) if r[0] else x, (1 - y) if r[1] else y, (1 - c) if r[2] else c)
        pidx = _dev_index(peer)
        for a in range(n):
            remote.append(pltpu.make_async_remote_copy(
                src_ref=b_refs[a].at[pidx], dst_ref=r_refs[a].at[pidx if incoming else me_idx],
                send_sem=send_sems.at[a * 7 + k], recv_sem=recv_sems.at[a * 7 + k],
                device_id=peer, device_id_type=MESH))
    return mine, remote


def _exchange_start(b_refs, r_refs, sems):
    mine, out = _exchange_copies(b_refs, r_refs, *sems, incoming=False)
    for cp in mine + out:
        cp.start()


def _exchange_wait(b_refs, r_refs, sems):
    mine, inc = _exchange_copies(b_refs, r_refs, *sems, incoming=True)
    for cp in inc:
        cp.wait_recv()
    for cp in inc:
        cp.wait_send()
    for cp in mine:
        cp.wait()


def _proj_call(x2, w_t, tm):
    s_len = x2.shape[0]

    def body(x_ref, w_ref, qft_ref, kft_ref, vf_ref, fz_ref, qst_ref, ks_ref, vs_ref, sz_ref, fft_ref, vat_ref,
             kst_ref, vsta_ref):
        xb = x_ref[...].astype(bf16)

        def seg_t(off, width):
            return lax.dot_general(w_ref[off:off + width, :], xb, NT, preferred_element_type=f32)

        def seg(off, width):
            return lax.dot_general(xb, w_ref[off:off + width, :], NT, preferred_element_type=f32)

        def put_heads(ref, acc, nheads):
            for h in range(nheads):
                ref[h] = acc[:, h * HEAD_DIM:(h + 1) * HEAD_DIM].astype(bf16)

        def put_heads_t(ref, acc_t, nheads, augment):
            for h in range(nheads):
                ref[h, 0:HEAD_DIM, :] = acc_t[h * HEAD_DIM:(h + 1) * HEAD_DIM, :].astype(bf16)
                if augment:
                    ref[h, HEAD_DIM:2 * HEAD_DIM, :] = ones_row

        ones_row = jnp.where(lax.broadcasted_iota(jnp.int32, (HEAD_DIM, tm), 0) == 0, 1.0, 0.0).astype(bf16)
        put_heads_t(vat_ref, seg_t(A_FV, 512), FOX_HEADS, True)
        put_heads_t(qft_ref, seg_t(A_FQ, 512) * SCALE, FOX_HEADS, False)
        put_heads_t(kft_ref, seg_t(A_FK, 512), FOX_HEADS, False)
        put_heads(vf_ref, seg(A_FV, 512), FOX_HEADS)
        fz_ref[...] = seg(A_FZ, 512)
        put_heads_t(qst_ref, seg_t(A_SQ, 512) * SCALE, SWA_HEADS, False)
        put_heads(ks_ref, seg(A_SK, 128), SWA_KV_HEADS)
        put_heads(vs_ref, seg(A_SV, 128), SWA_KV_HEADS)
        put_heads_t(kst_ref, seg_t(A_SK, 128), SWA_KV_HEADS, False)
        put_heads_t(vsta_ref, seg_t(A_SV, 128), SWA_KV_HEADS, True)
        sz_ref[...] = seg(A_SZ, 512)
        fft_ref[...] = seg(A_FF, 128).T[:FOX_HEADS, :]

    def heads(nh):
        return pl.BlockSpec((nh, tm, HEAD_DIM), lambda i: (0, i, 0))

    def feat(nh, rows):
        return pl.BlockSpec((nh, rows, tm), lambda i: (0, 0, i))

    wide = pl.BlockSpec((tm, 512), lambda i: (i, 0))
    return pl.pallas_call(
        body,
        name="proj_fwd",
        grid=(s_len // tm,),
        in_specs=[pl.BlockSpec((tm, D_MODEL), lambda i: (i, 0)), pl.BlockSpec((A_W, D_MODEL), lambda i: (0, 0))],
        out_specs=[feat(8, HEAD_DIM), feat(8, HEAD_DIM), heads(8), wide, feat(8, HEAD_DIM), heads(2), heads(2), wide,
                   pl.BlockSpec((FOX_HEADS, tm), lambda i: (0, i)),
                   feat(FOX_HEADS, 2 * HEAD_DIM), feat(2, HEAD_DIM), feat(2, 2 * HEAD_DIM)],
        out_shape=[_sds((8, HEAD_DIM, s_len), bf16)] * 2 + [_sds((8, s_len, HEAD_DIM), bf16)]
                  + [_sds((s_len, 512), f32), _sds((8, HEAD_DIM, s_len), bf16),
                     _sds((2, s_len, HEAD_DIM), bf16), _sds((2, s_len, HEAD_DIM), bf16), _sds((s_len, 512), f32),
                     _sds((FOX_HEADS, s_len), f32), _sds((FOX_HEADS, 2 * HEAD_DIM, s_len), bf16),
                     _sds((2, HEAD_DIM, s_len), bf16), _sds((2, 2 * HEAD_DIM, s_len), bf16)],
        compiler_params=_params(("arbitrary",)),
    )(x2, w_t)


AUG = 2 * HEAD_DIM


def _augment_call(q_t, k_t, cum_row, tm):
    nh, _, s_len = k_t.shape
    per_step = tm // FOX_T

    def body(qt_ref, kt_ref, c_ref, qat_ref, ka_ref, kat_ref, st_ref):
        c = c_ref[0]
        hi = c.astype(bf16).astype(f32)
        r1 = c - hi
        mid = r1.astype(bf16).astype(f32)
        lo = (r1 - mid).astype(bf16).astype(f32)
        row = lax.broadcasted_iota(jnp.int32, (HEAD_DIM, tm), 0)
        q_tail = jnp.where(row == 0, hi, jnp.where(row == 1, mid, jnp.where(row == 2, lo,
                           jnp.where(row < 6, 1.0, 0.0))))
        k_tail = jnp.where(row < 3, 1.0, jnp.where(row == 3, -hi, jnp.where(row == 4, -mid,
                           jnp.where(row == 5, -lo, 0.0))))
        qat_ref[0, 0:HEAD_DIM, :] = qt_ref[0]
        qat_ref[0, HEAD_DIM:AUG, :] = q_tail.astype(bf16)
        kat_ref[0, 0:HEAD_DIM, :] = kt_ref[0]
        kat_ref[0, HEAD_DIM:AUG, :] = k_tail.astype(bf16)
        qt = qt_ref[0].astype(f32)
        kt = kt_ref[0].astype(f32)
        ka_ref[0] = jnp.concatenate([kt, k_tail], axis=0).T.astype(bf16)
        qn2 = jnp.sum(qt * qt, axis=0, keepdims=True)
        kn2 = jnp.sum(kt * kt, axis=0, keepdims=True)
        sd = jnp.sum(qt * kt, axis=0, keepdims=True)
        srow = lax.broadcasted_iota(jnp.int32, (8, LANES), 0)
        for part in range(per_step):
            sl = slice(part * FOX_T, (part + 1) * FOX_T)
            vals = [jnp.sqrt(jnp.max(qn2[:, sl], axis=1, keepdims=True)),
                    jnp.sqrt(jnp.max(kn2[:, sl], axis=1, keepdims=True)),
                    jnp.min(sd[:, sl], axis=1, keepdims=True),
                    jnp.max(c[:, sl], axis=1, keepdims=True), jnp.min(c[:, sl], axis=1, keepdims=True)]
            out = jnp.zeros((8, LANES), f32)
            for r, val in enumerate(vals):
                out = jnp.where(srow == r, val, out)
            st_ref[0, part] = out

    tile_t = pl.BlockSpec((1, HEAD_DIM, tm), lambda h, i: (h, 0, i))
    return pl.pallas_call(
        body,
        name="fox_augment",
        grid=(nh, s_len // tm),
        in_specs=[tile_t, tile_t, pl.BlockSpec((1, 1, tm), lambda h, i: (h, 0, i))],
        out_specs=[pl.BlockSpec((1, AUG, tm), lambda h, i: (h, 0, i)),
                   pl.BlockSpec((1, tm, AUG), lambda h, i: (h, i, 0)),
                   pl.BlockSpec((1, AUG, tm), lambda h, i: (h, 0, i)),
                   pl.BlockSpec((1, per_step, 8, LANES), lambda h, i: (h, i, 0, 0))],
        out_shape=[_sds((nh, AUG, s_len), bf16), _sds((nh, s_len, AUG), bf16), _sds((nh, AUG, s_len), bf16),
                   _sds((nh, s_len // FOX_T, 8, LANES), f32)],
        compiler_params=_params(("arbitrary", "arbitrary")),
    )(q_t, k_t, cum_row)


FOX_PRUNE_GAP = 40.0


def _fox_prune_tables(stats):
    s = stats[:, :, :, 0]
    qn, kn, sd, cmx, cmn = (s[:, :, r] for r in range(5))
    nt = s.shape[1]
    bound = qn[:, :, None] * kn[:, None, :] + (cmx[:, :, None] - cmn[:, None, :])
    margin = 2.0 + 1e-5 * (jnp.abs(cmx)[:, :, None] + jnp.abs(cmn)[:, None, :])
    qi = lax.broadcasted_iota(jnp.int32, (nt, nt), 0)
    kj = lax.broadcasted_iota(jnp.int32, (nt, nt), 1)
    skip = (bound + margin < sd[:, :, None] - FOX_PRUNE_GAP) & (kj < qi)[None]
    first = jnp.sum(jnp.cumprod(skip.astype(jnp.int32), axis=2), axis=2)
    tiles = lax.broadcasted_iota(jnp.int32, (1, nt), 1)
    cnt = tiles - first
    ends = jnp.cumsum(cnt, axis=1)
    off = ends - cnt
    kmax = nt * (nt - 1) // 2
    k = lax.broadcasted_iota(jnp.int32, (1, kmax), 1)
    pair_q = jnp.minimum(jnp.sum((ends[:, None, :] <= k[:, :, None]).astype(jnp.int32), axis=2), nt - 1)
    hit = pair_q[:, :, None] == tiles[:, None, :]
    first_k = jnp.sum(jnp.where(hit, first[:, None, :], 0), axis=2)
    off_k = jnp.sum(jnp.where(hit, off[:, None, :], 0), axis=2)
    pair_k = jnp.clip(first_k + k - off_k, 0, nt - 1)
    return (ends[:, nt - 1].astype(jnp.int32), pair_q.reshape(-1).astype(jnp.int32),
            pair_k.reshape(-1).astype(jnp.int32))


CUM_CHUNK = 512


def _cum_call(fft, bf_col):
    s_len = fft.shape[1]
    ch = CUM_CHUNK

    def body(f_ref, b_ref, cum_ref, sg_ref):
        r = lax.broadcasted_iota(jnp.int32, (ch, ch), 0)
        c = lax.broadcasted_iota(jnp.int32, (ch, ch), 1)
        upper = (r <= c).astype(f32)
        carry = jnp.zeros((FOX_HEADS, 1), f32)
        for n in range(s_len // ch):
            z = f_ref[:, n * ch:(n + 1) * ch] + b_ref[...]
            logf = jnp.minimum(z, 0.0) - jnp.log1p(jnp.exp(-jnp.abs(z)))
            sg_ref[:, n * ch:(n + 1) * ch] = 1.0 / (1.0 + jnp.exp(z))
            cs = jnp.dot(logf, upper, precision=HIGHEST, preferred_element_type=f32) + carry
            cum_ref[:, n * ch:(n + 1) * ch] = cs
            carry = cs[:, ch - 1:ch]

    return pl.pallas_call(
        body,
        name="fox_cum_fwd",
        out_shape=[_sds((FOX_HEADS, s_len), f32)] * 2,
        compiler_params=_params(),
    )(fft, bf_col)


def _cum_bwd_call(dcq, dck, sg):
    s_len = sg.shape[1]
    ch = CUM_CHUNK
    nch = s_len // ch

    def body(q_ref, k_ref, sg_ref, dff_ref, dbf_ref):
        r = lax.broadcasted_iota(jnp.int32, (ch, ch), 0)
        c = lax.broadcasted_iota(jnp.int32, (ch, ch), 1)
        lower = (r >= c).astype(f32)
        dff_ref[...] = jnp.zeros_like(dff_ref)
        carry = jnp.zeros((FOX_HEADS, 1), f32)
        total = jnp.zeros((FOX_HEADS, 1), f32)
        for n in reversed(range(nch)):
            sl = slice(n * ch, (n + 1) * ch)
            dcum = q_ref[:, sl] - k_ref[:, sl]
            rs = jnp.dot(dcum, lower, precision=HIGHEST, preferred_element_type=f32) + carry
            carry = rs[:, 0:1]
            dff = rs * sg_ref[:, sl]
            dff_ref[0:FOX_HEADS, sl] = dff
            total = total + jnp.sum(dff, axis=1, keepdims=True)
        dbf_ref[...] = jnp.broadcast_to(total, (FOX_HEADS, 128))

    return pl.pallas_call(
        body,
        name="fox_cum_bwd",
        out_shape=[_sds((128, s_len), f32), _sds((FOX_HEADS, 128), f32)],
        compiler_params=_params(),
    )(dcq, dck, sg)


FOX_T = 512
LANES = 128


def _causal_keep(t):
    return lax.broadcasted_iota(jnp.int32, (t, t), 0) <= lax.broadcasted_iota(jnp.int32, (t, t), 1)


def _tile_cols(i, t):
    return pl.ds(pl.multiple_of(i * t, t), t)


def _fox_pair(n, nt, kmax, h, pq_ref, pk_ref):
    k = h * kmax + jnp.maximum(n - nt, 0)
    return jnp.where(n < nt, n, pq_ref[k]), jnp.where(n < nt, n, pk_ref[k])


def _fox_fwd_call(qat, ka, vat, npairs, pair_q, pair_k):
    nh, s_len, _ = ka.shape
    t = FOX_T
    nt = s_len // t
    kmax = nt * (nt - 1) // 2
    assert nt >= 2 and nt % 2 == 0

    def body(np_ref, pq_ref, pk_ref, qat_ref, ka_ref, vat_ref, o_ref, lse_ref, s0, s1, p0, p1, a0, a1, m_all, acc_all):
        h = pl.program_id(0)
        extra = np_ref[h]
        total = nt + extra
        m_all[...] = jnp.full(m_all.shape, NEG_INF, f32)
        acc_all[...] = jnp.zeros(acc_all.shape, f32)
        bufs = ((s0, p0, a0), (s1, p1, a1))

        def pair(n):
            return _fox_pair(n, nt, kmax, h, pq_ref, pk_ref)

        def scores(n, b, masked):
            i, j = pair(n)
            st = jnp.dot(ka_ref[0, _tile_cols(j, t), :], qat_ref[0, :, _tile_cols(i, t)], preferred_element_type=f32)
            if masked:
                st = jnp.where(_causal_keep(t), st, NEG_INF)
            bufs[b][0][...] = st

        def softmax(n, b):
            i, _ = pair(n)
            s_ref, p_ref, a_ref = bufs[b]
            for c in range(t // LANES):
                cols = slice(c * LANES, (c + 1) * LANES)
                mcols = pl.ds(pl.multiple_of(i * t + c * LANES, LANES), LANES)
                m_old = m_all[:, mcols]
                m_new = jnp.maximum(m_old, jnp.max(s_ref[:, cols], axis=0, keepdims=True))
                m_all[:, mcols] = m_new
                a_ref[:, cols] = jnp.exp(m_old - m_new)
                p_ref[:, cols] = jnp.exp(s_ref[:, cols] - m_new).astype(bf16)

        def accum(n, b):
            i, j = pair(n)
            cols = _tile_cols(i, t)
            acc_all[:, cols] = bufs[b][2][...] * acc_all[:, cols] + jnp.dot(
                vat_ref[0, :, _tile_cols(j, t)], bufs[b][1][...], preferred_element_type=f32)

        def step(n, b, masked):
            accum(n - 2, b)
            softmax(n - 1, 1 - b)
            scores(n, b, masked)

        scores(0, 0, True)
        scores(1, 1, True)
        softmax(0, 0)

        def diag_steps(d, _):
            n = 2 + 2 * d
            step(n, 0, True)
            step(n + 1, 1, True)
            return 0

        lax.fori_loop(0, (nt - 2) // 2, diag_steps, 0)

        def off_steps(d, _):
            n = nt + 2 * d
            step(n, 0, False)
            step(n + 1, 1, False)
            return 0

        lax.fori_loop(0, extra // 2, off_steps, 0)

        @pl.when(extra % 2 == 1)
        def _():
            step(total - 1, 0, False)
            softmax(total - 1, 0)
            accum(total - 2, 1)
            accum(total - 1, 0)

        @pl.when(extra % 2 == 0)
        def _():
            softmax(total - 1, 1)
            accum(total - 2, 0)
            accum(total - 1, 1)

        l = acc_all[HEAD_DIM:HEAD_DIM + 1, :]
        o_ref[0] = acc_all[0:HEAD_DIM, :] / l
        lse_ref[0] = m_all[...] + jnp.log(l)

    smem = pl.BlockSpec(memory_space=pltpu.SMEM)
    return pl.pallas_call(
        body,
        name="fox_fwd",
        grid=(nh,),
        in_specs=[smem, smem, smem,
                  pl.BlockSpec((1, AUG, s_len), lambda h: (h, 0, 0)),
                  pl.BlockSpec((1, s_len, AUG), lambda h: (h, 0, 0)),
                  pl.BlockSpec((1, AUG, s_len), lambda h: (h, 0, 0))],
        out_specs=[pl.BlockSpec((1, HEAD_DIM, s_len), lambda h: (h, 0, 0)),
                   pl.BlockSpec((1, 1, s_len), lambda h: (h, 0, 0))],
        out_shape=[_sds((nh, HEAD_DIM, s_len), f32), _sds((nh, 1, s_len), f32)],
        scratch_shapes=[pltpu.VMEM((t, t), f32), pltpu.VMEM((t, t), f32), pltpu.VMEM((t, t), bf16),
                        pltpu.VMEM((t, t), bf16), pltpu.VMEM((1, t), f32), pltpu.VMEM((1, t), f32),
                        pltpu.VMEM((1, s_len), f32), pltpu.VMEM((AUG, s_len), f32)],
        compiler_params=_params(("arbitrary",)),
    )(npairs, pair_q, pair_k, qat, ka, vat)


SWA_TS = 512


SWA_W = SWA_GROUP * BLOCK


def _swa_bias_call(rel_bias, bucket_t):
    def body(rb_ref, bk_ref, b_ref, b0_ref):
        bk = bk_ref[...]
        row = lax.broadcasted_iota(jnp.int32, (2 * BLOCK, BLOCK), 0)
        for h in range(SWA_HEADS):
            acc = jnp.full((2 * BLOCK, BLOCK), NEG_INF, f32)
            for b in range(NUM_BUCKETS):
                acc = jnp.where(bk == b, rb_ref[b, h], acc)
            g, hh = divmod(h, SWA_GROUP)
            b_ref[g, :, hh * BLOCK:(hh + 1) * BLOCK] = acc
            b0_ref[g, :, hh * BLOCK:(hh + 1) * BLOCK] = jnp.where(row < BLOCK, NEG_INF, acc)

    return pl.pallas_call(
        body,
        name="swa_bias",
        in_specs=[pl.BlockSpec(memory_space=pltpu.SMEM), pl.BlockSpec(memory_space=pltpu.VMEM)],
        out_shape=[_sds((SWA_KV_HEADS, 2 * BLOCK, SWA_W), f32)] * 2,
        compiler_params=_params(),
    )(rel_bias, bucket_t)


def _swa_bias_bwd_call(dbias, bucket_t):
    def body(d_ref, bk_ref, o_ref):
        bk = bk_ref[...]
        row = lax.broadcasted_iota(jnp.int32, (NUM_BUCKETS, 128), 0)
        col = lax.broadcasted_iota(jnp.int32, (NUM_BUCKETS, 128), 1)
        out = jnp.zeros((NUM_BUCKETS, 128), f32)
        for h in range(SWA_HEADS):
            g, hh = divmod(h, SWA_GROUP)
            d = d_ref[g, :, hh * BLOCK:(hh + 1) * BLOCK]
            for b in range(NUM_BUCKETS):
                val = jnp.sum(jnp.sum(jnp.where(bk == b, d, 0.0), axis=1, keepdims=True), axis=0, keepdims=True)
                out = jnp.where((row == b) & (col == h), val, out)
        o_ref[...] = out

    return pl.pallas_call(
        body,
        name="swa_bias_bwd",
        out_shape=_sds((NUM_BUCKETS, 128), f32),
        compiler_params=_params(),
    )(dbias, bucket_t)


def _sink_row(sink_ref, g):
    return jnp.concatenate([jnp.full((1, BLOCK), sink_ref[g * SWA_GROUP + hh], f32) for hh in range(SWA_GROUP)], axis=1)


def _group_lanes(ref, g, cols):
    return jnp.concatenate([ref[g * SWA_GROUP + hh, :, cols] for hh in range(SWA_GROUP)], axis=1)


def _swa_fwd_call(qt, k, vta, bias_t, bias0_t, sink):
    s_len = qt.shape[2]
    ts = SWA_TS
    nb = ts // BLOCK

    def body(qt_ref, kc_ref, kp_ref, vc_ref, vp_ref, b_ref, b0_ref, sink_ref, o_ref, lse_ref):
        first = pl.program_id(0) == 0
        kall = [jnp.concatenate([kp_ref[g], kc_ref[g]], axis=0) for g in range(SWA_KV_HEADS)]
        vall = [jnp.concatenate([vp_ref[g], vc_ref[g]], axis=1) for g in range(SWA_KV_HEADS)]
        sinks = [_sink_row(sink_ref, g) for g in range(SWA_KV_HEADS)]
        items = [(g, b) for g in range(SWA_KV_HEADS) for b in range(nb)]

        def scores(g, b):
            qg = _group_lanes(qt_ref, g, slice(b * BLOCK, (b + 1) * BLOCK))
            bias_b = b_ref[g]
            if b == 0:
                bias_b = jnp.where(first, b0_ref[g], bias_b)
            return jnp.dot(kall[g][b * BLOCK:(b + 2) * BLOCK], qg, preferred_element_type=f32) + bias_b

        def finish(g, b, st):
            m = jnp.maximum(jnp.max(st, axis=0, keepdims=True), sinks[g])
            pt = jnp.exp(st - m)
            acc = jnp.dot(vall[g][:, b * BLOCK:(b + 2) * BLOCK], pt.astype(bf16), preferred_element_type=f32)
            l = acc[HEAD_DIM:HEAD_DIM + 1, :] + jnp.exp(sinks[g] - m)
            return acc[0:HEAD_DIM, :] / l, m + jnp.log(l)

        outs, lses = {}, {}
        st_next = scores(*items[0])
        for idx, (g, b) in enumerate(items):
            st = st_next
            if idx + 1 < len(items):
                st_next = scores(*items[idx + 1])
            outs[g, b], lses[g, b] = finish(g, b, st)
        for g in range(SWA_KV_HEADS):
            for hh in range(SWA_GROUP):
                lanes = slice(hh * BLOCK, (hh + 1) * BLOCK)
                o_ref[g * SWA_GROUP + hh] = jnp.concatenate([outs[g, b][:, lanes] for b in range(nb)], axis=1)
                lse_ref[g * SWA_GROUP + hh] = jnp.concatenate([lses[g, b][:, lanes] for b in range(nb)], axis=1)

    def prev_blk(n):
        return jnp.maximum(n * nb - 1, 0)

    bspec = pl.BlockSpec((SWA_KV_HEADS, 2 * BLOCK, SWA_W), lambda n: (0, 0, 0))
    return pl.pallas_call(
        body,
        name="swa_fwd",
        grid=(s_len // ts,),
        in_specs=[pl.BlockSpec((SWA_HEADS, HEAD_DIM, ts), lambda n: (0, 0, n)),
                  pl.BlockSpec((SWA_KV_HEADS, ts, HEAD_DIM), lambda n: (0, n, 0)),
                  pl.BlockSpec((SWA_KV_HEADS, BLOCK, HEAD_DIM), lambda n: (0, prev_blk(n), 0)),
                  pl.BlockSpec((SWA_KV_HEADS, AUG, ts), lambda n: (0, 0, n)),
                  pl.BlockSpec((SWA_KV_HEADS, AUG, BLOCK), lambda n: (0, 0, prev_blk(n))),
                  bspec, bspec, pl.BlockSpec(memory_space=pltpu.SMEM)],
        out_specs=[pl.BlockSpec((SWA_HEADS, HEAD_DIM, ts), lambda n: (0, 0, n)),
                   pl.BlockSpec((SWA_HEADS, 1, ts), lambda n: (0, 0, n))],
        out_shape=[_sds((SWA_HEADS, HEAD_DIM, s_len), f32), _sds((SWA_HEADS, 1, s_len), f32)],
        compiler_params=_params(("arbitrary",)),
    )(qt, k, k, vta, vta, bias_t, bias0_t, sink)


def _head_selector():
    sel = np.zeros((512, 128), np.float32)
    for h in range(8):
        sel[h * HEAD_DIM:(h + 1) * HEAD_DIM, h] = 1.0
    return sel


def _post_call(of, fz, osw, sz, x2, tgt, wo, ln_g, ln_b, sel, tm):
    s_len = x2.shape[0]

    def body(of_ref, fz_ref, os_ref, sz_ref, x_ref, t_ref, wo_ref, g_ref, b_ref, sel_ref,
             dh_ref, dof_ref, dfz_ref, dos_ref, dsz_ref, dlf_ref, dls_ref, dwo_ref, dg_ref, db_ref, loss_ref):
        n = pl.program_id(0)

        @pl.when(n == 0)
        def _():
            dwo_ref[...] = jnp.zeros_like(dwo_ref)
            dg_ref[...] = jnp.zeros_like(dg_ref)
            db_ref[...] = jnp.zeros_like(db_ref)
            loss_ref[...] = jnp.zeros_like(loss_ref)

        gam = g_ref[...]
        sel_m = sel_ref[...]

        def forward(r):
            o_f = of_ref[:, r].T
            o_s = os_ref[:, r].T
            fz = fz_ref[r, :]
            sz = sz_ref[r, :]
            sg_f = jax.nn.sigmoid(fz)
            sg_s = jax.nn.sigmoid(sz)
            silu_f = fz * sg_f
            silu_s = sz * sg_s
            mixed = jnp.concatenate([o_f * silu_f, o_s * silu_s], axis=1).astype(bf16)
            y = jnp.dot(mixed, wo_ref[...], preferred_element_type=f32)
            return o_f, o_s, fz, sz, sg_f, sg_s, silu_f, silu_s, mixed, y

        def norm_and_back(r, fwd):
            mixed, y = fwd[8], fwd[9]
            h = ALPHA * x_ref[r, :] + y
            mu = jnp.mean(h, axis=1, keepdims=True)
            hc = h - mu
            var = jnp.mean(hc * hc, axis=1, keepdims=True)
            rstd = lax.rsqrt(var + LN_EPS)
            xhat = hc * rstd
            out = xhat * gam + b_ref[...]
            err = out - t_ref[r, :]
            tok_loss = jnp.mean(err * err, axis=1, keepdims=True)
            loss_ref[...] += 0.5 * jnp.sum(tok_loss, axis=0, keepdims=True)
            dout = err * (1.0 / D_MODEL)
            dg_ref[...] += jnp.sum(dout * xhat, axis=0, keepdims=True)
            db_ref[...] += jnp.sum(dout, axis=0, keepdims=True)
            dxh = dout * gam
            m1 = jnp.mean(dxh, axis=1, keepdims=True)
            m2 = jnp.mean(dxh * xhat, axis=1, keepdims=True)
            dh = rstd * (dxh - m1 - xhat * m2)
            dh_ref[r, :] = dh
            dyb = dh.astype(bf16)
            dmix = lax.dot_general(dyb, wo_ref[...], NT, preferred_element_type=f32)
            dwo_ref[...] += lax.dot_general(mixed, dyb, TN, preferred_element_type=f32)
            return dmix

        def head_sums(prod):
            hi = prod.astype(bf16)
            lo = (prod - hi.astype(f32)).astype(bf16)
            return (jnp.dot(hi, sel_m, preferred_element_type=f32) + jnp.dot(lo, sel_m, preferred_element_type=f32))

        def gates_back(r, fwd, dmix):
            o_f, o_s, fz, sz, sg_f, sg_s, silu_f, silu_s = fwd[:8]
            dm_f = dmix[:, :512]
            dm_s = dmix[:, 512:]
            do_f = dm_f * silu_f
            do_s = dm_s * silu_s
            dfz_ref[r, :] = (dm_f * o_f * (sg_f * (1.0 + fz * (1.0 - sg_f)))).astype(bf16)
            dsz_ref[r, :] = (dm_s * o_s * (sg_s * (1.0 + sz * (1.0 - sg_s)))).astype(bf16)
            dof_ref[:, r] = do_f.T.astype(bf16)
            dos_ref[:, r] = do_s.T.astype(bf16)
            dlf_ref[:, r] = head_sums(do_f * o_f).T[:FOX_HEADS, :]
            dls_ref[:, r] = head_sums(do_s * o_s).T[:SWA_HEADS, :]

        halves = [slice(k * (tm // 2), (k + 1) * (tm // 2)) for k in range(2)]
        fwds = [forward(r) for r in halves]
        dmixes = [norm_and_back(r, f) for r, f in zip(halves, fwds)]
        for r, f, d in zip(halves, fwds, dmixes):
            gates_back(r, f, d)

    feat = pl.BlockSpec((512, tm), lambda n: (0, n))
    rows8 = pl.BlockSpec((8, tm), lambda n: (0, n))
    half = pl.BlockSpec((tm, 512), lambda n: (n, 0))
    fullw = pl.BlockSpec((tm, D_MODEL), lambda n: (n, 0))
    vec = pl.BlockSpec((1, D_MODEL), lambda n: (0, 0))
    return pl.pallas_call(
        body,
        name="post_fwd_bwd",
        grid=(s_len // tm,),
        in_specs=[feat, half, feat, half, fullw, fullw,
                  pl.BlockSpec((D_MODEL, D_MODEL), lambda n: (0, 0)), vec, vec,
                  pl.BlockSpec((512, 128), lambda n: (0, 0))],
        out_specs=[fullw, feat, half, feat, half, rows8, rows8,
                   pl.BlockSpec((D_MODEL, D_MODEL), lambda n: (0, 0)), vec, vec,
                   pl.BlockSpec((1, 1), lambda n: (0, 0))],
        out_shape=[_sds((s_len, D_MODEL), f32), _sds((512, s_len), bf16), _sds((s_len, 512), bf16),
                   _sds((512, s_len), bf16), _sds((s_len, 512), bf16),
                   _sds((FOX_HEADS, s_len), f32), _sds((SWA_HEADS, s_len), f32),
                   _sds((D_MODEL, D_MODEL), f32), _sds((1, D_MODEL), f32), _sds((1, D_MODEL), f32),
                   _sds((1, 1), f32)],
        compiler_params=_params(("arbitrary",), VMEM_LIMIT_BIG),
    )(of, fz, osw, sz, x2, tgt, wo, ln_g, ln_b, sel)


def _fox_bwd_call(ka, kat, v, qat, dot, lse_row, dl_row, npairs, pair_q, pair_k):
    nh, s_len, _ = ka.shape
    t = FOX_T
    nt = s_len // t
    kmax = nt * (nt - 1) // 2
    assert nt >= 2 and nt % 2 == 0
    ck_slot = HEAD_DIM + 3
    cq_slot = HEAD_DIM

    def body(np_ref, pq_ref, pk_ref, ka_ref, kat_ref, v_ref, qat_ref, dot_ref, lse_ref, dl_ref,
             dq_ref, dk_ref, dv_ref, dcq_ref, dck_ref, dqt_all, dkat_all, dvt_all, p0, p1, ds0, ds1):
        h = pl.program_id(0)
        extra = np_ref[h]
        total = nt + extra
        dqt_all[...] = jnp.zeros(dqt_all.shape, f32)
        dkat_all[...] = jnp.zeros(dkat_all.shape, f32)
        dvt_all[...] = jnp.zeros(dvt_all.shape, f32)
        pbuf, dsbuf = (p0, p1), (ds0, ds1)

        def pair(n):
            return _fox_pair(n, nt, kmax, h, pq_ref, pk_ref)

        def probs(n, b, masked):
            i, j = pair(n)
            qc, kr = _tile_cols(i, t), _tile_cols(j, t)
            st = jnp.dot(ka_ref[0, kr, :], qat_ref[0, :, qc], preferred_element_type=f32)
            dpt = jnp.dot(v_ref[0, kr, :], dot_ref[0, :, qc], preferred_element_type=f32)
            if masked:
                st = jnp.where(_causal_keep(t), st, NEG_INF)
            pt = jnp.exp(st - lse_ref[0, :, qc])
            pbuf[b][...] = pt.astype(bf16)
            dsbuf[b][...] = (pt * (dpt - dl_ref[0, :, qc])).astype(bf16)

        def grads(n, b):
            i, j = pair(n)
            qc, kc = _tile_cols(i, t), _tile_cols(j, t)
            dvt_all[:, kc] += lax.dot_general(dot_ref[0, :, qc], pbuf[b][...], NT, preferred_element_type=f32)
            dkat_all[:, kc] += lax.dot_general(qat_ref[0, :, qc], dsbuf[b][...], NT, preferred_element_type=f32)
            dqt_all[:, qc] += jnp.dot(kat_ref[0, :, kc], dsbuf[b][...], preferred_element_type=f32)

        def step(n, b, masked):
            i, j = pair(n)
            qc, kr = _tile_cols(i, t), _tile_cols(j, t)
            i1, j1 = pair(n - 1)
            qc1, kc1 = _tile_cols(i1, t), _tile_cols(j1, t)
            c = 1 - b
            st = jnp.dot(ka_ref[0, kr, :], qat_ref[0, :, qc], preferred_element_type=f32)
            dvt_all[:, kc1] += lax.dot_general(dot_ref[0, :, qc1], pbuf[c][...], NT, preferred_element_type=f32)
            if masked:
                st = jnp.where(_causal_keep(t), st, NEG_INF)
            pt = jnp.exp(st - lse_ref[0, :, qc])
            pbuf[b][...] = pt.astype(bf16)
            dpt = jnp.dot(v_ref[0, kr, :], dot_ref[0, :, qc], preferred_element_type=f32)
            dkat_all[:, kc1] += lax.dot_general(qat_ref[0, :, qc1], dsbuf[c][...], NT, preferred_element_type=f32)
            dqt_all[:, qc1] += jnp.dot(kat_ref[0, :, kc1], dsbuf[c][...], preferred_element_type=f32)
            dsbuf[b][...] = (pt * (dpt - dl_ref[0, :, qc])).astype(bf16)

        probs(0, 0, True)
        step(1, 1, True)

        def diag_steps(d, _):
            n = 2 + 2 * d
            step(n, 0, True)
            step(n + 1, 1, True)
            return 0

        lax.fori_loop(0, (nt - 2) // 2, diag_steps, 0)

        def off_steps(d, _):
            n = nt + 2 * d
            step(n, 0, False)
            step(n + 1, 1, False)
            return 0

        lax.fori_loop(0, extra // 2, off_steps, 0)

        @pl.when(extra % 2 == 1)
        def _():
            step(total - 1, 0, False)
            grads(total - 1, 0)

        @pl.when(extra % 2 == 0)
        def _():
            grads(total - 1, 1)

        dq_ref[0] = (dqt_all[0:HEAD_DIM, :] * SCALE).astype(bf16)
        dk_ref[0] = dkat_all[0:HEAD_DIM, :].astype(bf16)
        dv_ref[0] = dvt_all[...].astype(bf16)
        dcq_ref[0] = dqt_all[cq_slot:cq_slot + 1, :]
        dck_ref[0] = dkat_all[ck_slot:ck_slot + 1, :]

    smem = pl.BlockSpec(memory_space=pltpu.SMEM)
    rows = pl.BlockSpec((1, s_len, AUG), lambda h: (h, 0, 0))
    feat = pl.BlockSpec((1, AUG, s_len), lambda h: (h, 0, 0))
    feat64 = pl.BlockSpec((1, HEAD_DIM, s_len), lambda h: (h, 0, 0))
    rowv = pl.BlockSpec((1, 1, s_len), lambda h: (h, 0, 0))
    return pl.pallas_call(
        body,
        name="fox_bwd",
        grid=(nh,),
        in_specs=[smem, smem, smem, rows, feat, pl.BlockSpec((1, s_len, HEAD_DIM), lambda h: (h, 0, 0)), feat, feat64,
                  rowv, rowv],
        out_specs=[feat64, feat64, feat64, rowv, rowv],
        out_shape=[_sds((nh, HEAD_DIM, s_len), bf16)] * 3 + [_sds((nh, 1, s_len), f32)] * 2,
        scratch_shapes=[pltpu.VMEM((AUG, s_len), f32), pltpu.VMEM((AUG, s_len), f32), pltpu.VMEM((HEAD_DIM, s_len), f32)]
                       + [pltpu.VMEM((t, t), bf16)] * 4,
        compiler_params=_params(("arbitrary",)),
    )(npairs, pair_q, pair_k, ka, kat, v, qat, dot, lse_row, dl_row)


def _swa_bwd_call(qt, k, kt, v, dot, lse, dl, bias_t, bias0_t, sink):
    s_len = qt.shape[2]
    ts = SWA_TS
    nb = ts // BLOCK
    nsteps = s_len // ts

    def body(qt_ref, kc_ref, kp_ref, ktc_ref, ktp_ref, vc_ref, vp_ref, dot_ref, lse_ref, dl_ref, b_ref, b0_ref,
             sink_ref, dq_ref, dk_ref, dv_ref, dbias_ref, dsink_ref, dk_s, dv_s, tail_k, tail_v, sk_s):
        n = pl.program_id(0)

        @pl.when(n == 0)
        def _():
            dbias_ref[...] = jnp.zeros_like(dbias_ref)
            sk_s[...] = jnp.zeros_like(sk_s)

        @pl.when(n < nsteps)
        def _():
            first = n == 0
            dk_s[...] = jnp.zeros_like(dk_s)
            dv_s[...] = jnp.zeros_like(dv_s)
            groups = range(SWA_KV_HEADS)
            kall = [jnp.concatenate([kp_ref[g], kc_ref[g]], axis=0) for g in groups]
            vall = [jnp.concatenate([vp_ref[g], vc_ref[g]], axis=0) for g in groups]
            ktall = [jnp.concatenate([ktp_ref[g], ktc_ref[g]], axis=1) for g in groups]
            sinks = [_sink_row(sink_ref, g) for g in groups]
            items = [(g, b) for g in groups for b in range(nb)]

            def products(g, b):
                cols = slice(b * BLOCK, (b + 1) * BLOCK)
                win = slice(b * BLOCK, (b + 2) * BLOCK)
                qg = _group_lanes(qt_ref, g, cols)
                dog = _group_lanes(dot_ref, g, cols)
                bias_b = b_ref[g]
                if b == 0:
                    bias_b = jnp.where(first, b0_ref[g], bias_b)
                st = jnp.dot(kall[g][win], qg, preferred_element_type=f32) + bias_b
                dpt = jnp.dot(vall[g][win], dog, preferred_element_type=f32)
                return qg, dog, st, dpt

            def finish(g, b, qg, dog, st, dpt):
                cols = slice(b * BLOCK, (b + 1) * BLOCK)
                win = slice(b * BLOCK, (b + 2) * BLOCK)
                lse_r = _group_lanes(lse_ref, g, cols)
                dl_r = _group_lanes(dl_ref, g, cols)
                pt = jnp.exp(st - lse_r)
                dst = pt * (dpt - dl_r)
                dsb = dst.astype(bf16)
                dk_s[g, :, win] += lax.dot_general(qg, dsb, NT, preferred_element_type=f32)
                dv_s[g, :, win] += lax.dot_general(dog, pt.astype(bf16), NT, preferred_element_type=f32)
                dqg = jnp.dot(ktall[g][:, win], dsb, preferred_element_type=f32) * SCALE
                return dqg, dst, -jnp.exp(sinks[g] - lse_r) * dl_r

            dqs, dsts, sks = {}, {}, {}
            nxt = products(*items[0])
            for idx, (g, b) in enumerate(items):
                cur = nxt
                if idx + 1 < len(items):
                    nxt = products(*items[idx + 1])
                dqs[g, b], dsts[g, b], sks[g, b] = finish(g, b, *cur)
            for g in groups:
                dbias_ref[g] += functools.reduce(lambda a, c: a + c, [dsts[g, b] for b in range(nb)])
                sk_s[g] += functools.reduce(lambda a, c: a + c, [sks[g, b] for b in range(nb)])
                for hh in range(SWA_GROUP):
                    lanes = slice(hh * BLOCK, (hh + 1) * BLOCK)
                    dq_ref[g * SWA_GROUP + hh] = jnp.concatenate(
                        [dqs[g, b][:, lanes] for b in range(nb)], axis=1).astype(bf16)

        @pl.when(n > 0)
        def _():
            last = slice(ts - BLOCK, ts)
            for g in range(SWA_KV_HEADS):
                add_k = jnp.where(n < nsteps, dk_s[g, :, 0:BLOCK], 0.0)
                add_v = jnp.where(n < nsteps, dv_s[g, :, 0:BLOCK], 0.0)
                dk_ref[g, :, 0:ts - BLOCK] = tail_k[g, :, 0:ts - BLOCK].astype(bf16)
                dv_ref[g, :, 0:ts - BLOCK] = tail_v[g, :, 0:ts - BLOCK].astype(bf16)
                dk_ref[g, :, last] = (tail_k[g, :, last] + add_k).astype(bf16)
                dv_ref[g, :, last] = (tail_v[g, :, last] + add_v).astype(bf16)

        @pl.when(n < nsteps)
        def _():
            tail_k[...] = dk_s[:, :, BLOCK:]
            tail_v[...] = dv_s[:, :, BLOCK:]

        @pl.when(n == nsteps)
        def _():
            row = lax.broadcasted_iota(jnp.int32, (SWA_HEADS, 128), 0)
            out = jnp.zeros((SWA_HEADS, 128), f32)
            for h in range(SWA_HEADS):
                g, hh = divmod(h, SWA_GROUP)
                val = jnp.sum(sk_s[g, :, hh * BLOCK:(hh + 1) * BLOCK], axis=1, keepdims=True)
                out = jnp.where(row == h, val, out)
            dsink_ref[...] = out

    last_step = nsteps - 1

    def cl(n):
        return jnp.minimum(n, last_step)

    def prev_blk(n):
        return jnp.maximum(cl(n) * nb - 1, 0)

    feat8 = pl.BlockSpec((SWA_HEADS, HEAD_DIM, ts), lambda n: (0, 0, cl(n)))
    rows8 = pl.BlockSpec((SWA_HEADS, 1, ts), lambda n: (0, 0, cl(n)))
    cur = pl.BlockSpec((SWA_KV_HEADS, ts, HEAD_DIM), lambda n: (0, cl(n), 0))
    prev = pl.BlockSpec((SWA_KV_HEADS, BLOCK, HEAD_DIM), lambda n: (0, prev_blk(n), 0))
    curt = pl.BlockSpec((SWA_KV_HEADS, HEAD_DIM, ts), lambda n: (0, 0, cl(n)))
    prevt = pl.BlockSpec((SWA_KV_HEADS, HEAD_DIM, BLOCK), lambda n: (0, 0, prev_blk(n)))
    bspec = pl.BlockSpec((SWA_KV_HEADS, 2 * BLOCK, SWA_W), lambda n: (0, 0, 0))
    kvout = pl.BlockSpec((SWA_KV_HEADS, HEAD_DIM, ts), lambda n: (0, 0, jnp.maximum(n - 1, 0)))
    return pl.pallas_call(
        body,
        name="swa_bwd",
        grid=(nsteps + 1,),
        in_specs=[feat8, cur, prev, curt, prevt, cur, prev, feat8, rows8, rows8, bspec, bspec,
                  pl.BlockSpec(memory_space=pltpu.SMEM)],
        out_specs=[feat8, kvout, kvout, bspec, pl.BlockSpec((SWA_HEADS, 128), lambda n: (0, 0))],
        out_shape=[_sds((SWA_HEADS, HEAD_DIM, s_len), bf16), _sds((SWA_KV_HEADS, HEAD_DIM, s_len), bf16),
                   _sds((SWA_KV_HEADS, HEAD_DIM, s_len), bf16),
                   _sds((SWA_KV_HEADS, 2 * BLOCK, SWA_W), f32), _sds((SWA_HEADS, 128), f32)],
        scratch_shapes=[pltpu.VMEM((SWA_KV_HEADS, HEAD_DIM, ts + BLOCK), f32),
                        pltpu.VMEM((SWA_KV_HEADS, HEAD_DIM, ts + BLOCK), f32),
                        pltpu.VMEM((SWA_KV_HEADS, HEAD_DIM, ts), f32),
                        pltpu.VMEM((SWA_KV_HEADS, HEAD_DIM, ts), f32),
                        pltpu.VMEM((SWA_KV_HEADS, 1, SWA_W), f32)],
        compiler_params=_params(("arbitrary",)),
    )(qt, k, k, kt, kt, v, v, dot, lse, dl, bias_t, bias0_t, sink)


def _dproj_specs(tm):
    half = pl.BlockSpec((tm, 512), lambda i: (i, 0))
    feat = pl.BlockSpec((512, tm), lambda i: (0, i))
    feat_kv = pl.BlockSpec((128, tm), lambda i: (0, i))
    return [feat, feat, feat, half, feat, feat_kv, feat_kv, half, feat_kv]


def _dx_exchange_call(dh, pieces, w_t, bs, tm):
    s_len = dh.shape[0]
    n = len(bs)
    last = s_len // tm - 1

    def body(*refs):
        dh_ref, dqf_ref, dkf_ref, dvf_ref, dfz_ref, dqs_ref, dks_ref, dvs_ref, dsz_ref, dfft_ref, w_ref = refs[:11]
        b_refs = refs[11:11 + n]
        dx_ref = refs[11 + n]
        r_refs = refs[12 + n:12 + 2 * n]
        sems = refs[12 + 2 * n:]
        i = pl.program_id(0)

        @pl.when(i == 0)
        def _():
            _exchange_start(b_refs, r_refs, sems)

        def tr(ref):
            return ref[...].astype(f32).T.astype(bf16)

        dp = jnp.concatenate([tr(dqf_ref), tr(dkf_ref), tr(dvf_ref), dfz_ref[...], tr(dqs_ref), tr(dks_ref),
                              tr(dvs_ref), dsz_ref[...], tr(dfft_ref)], axis=1)
        dx_ref[...] = ALPHA * dh_ref[...] + jnp.dot(dp, w_ref[...], preferred_element_type=f32)

        @pl.when(i == last)
        def _():
            _exchange_wait(b_refs, r_refs, sems)

    fullw = pl.BlockSpec((tm, D_MODEL), lambda i: (i, 0))
    any_spec = pl.BlockSpec(memory_space=pl.ANY)
    out = pl.pallas_call(
        body,
        name="dx_bwd_exchange",
        grid=(s_len // tm,),
        in_specs=[fullw] + _dproj_specs(tm) + [pl.BlockSpec((A_W, D_MODEL), lambda i: (0, 0))] + [any_spec] * n,
        out_specs=[fullw] + [any_spec] * n,
        out_shape=[_sds((s_len, D_MODEL), f32)] + [_sds(b.shape, b.dtype) for b in bs],
        scratch_shapes=[pltpu.SemaphoreType.DMA((7 * n,)), pltpu.SemaphoreType.DMA((7 * n,)),
                        pltpu.SemaphoreType.DMA((n,))],
        compiler_params=_params(("arbitrary",)),
    )(dh, *pieces, w_t, *bs)
    return out[0], out[1:]


DW_STAGE_ROWS = 384


def _dw_call(x2, pieces, tm):
    s_len = x2.shape[0]
    nt = s_len // tm

    def body(x_ref, dqf_ref, dkf_ref, dvf_ref, dfz_ref, dqs_ref, dks_ref, dvs_ref, dsz_ref, dfft_ref, dw_ref,
             acc_ref, stage_ref, sem):
        i = pl.program_id(0)

        @pl.when(i == 0)
        def _():
            acc_ref[...] = jnp.zeros_like(acc_ref)

        xb = x_ref[...].astype(bf16)

        def add_feat(off, lhs):
            acc_ref[off:off + lhs.shape[0], :] += jnp.dot(lhs, xb, preferred_element_type=f32)

        def add_rows(off, piece):
            acc_ref[off:off + piece.shape[1], :] += lax.dot_general(piece, xb, TN, preferred_element_type=f32)

        add_feat(A_FQ, dqf_ref[...])
        add_feat(A_FK, dkf_ref[...])
        add_feat(A_FV, dvf_ref[...])
        add_rows(A_FZ, dfz_ref[...])
        add_feat(A_SQ, dqs_ref[...])
        add_feat(A_SK, dks_ref[...])
        add_feat(A_SV, dvs_ref[...])
        add_rows(A_SZ, dsz_ref[...])
        add_feat(A_FF, dfft_ref[...].astype(bf16))

        @pl.when(i == nt - 1)
        def _():
            for r in range(A_W // DW_STAGE_ROWS):
                rows = slice(r * DW_STAGE_ROWS, (r + 1) * DW_STAGE_ROWS)
                stage_ref[...] = acc_ref[rows, :].astype(bf16)
                cp = pltpu.make_async_copy(stage_ref, dw_ref.at[rows, :], sem)
                cp.start()
                cp.wait()

    return pl.pallas_call(
        body,
        name="dw_in_bwd",
        grid=(nt,),
        in_specs=[pl.BlockSpec((tm, D_MODEL), lambda i: (i, 0))] + _dproj_specs(tm),
        out_specs=pl.BlockSpec(memory_space=pl.ANY),
        out_shape=_sds((A_W, D_MODEL), bf16),
        scratch_shapes=[pltpu.VMEM((A_W, D_MODEL), f32), pltpu.VMEM((DW_STAGE_ROWS, D_MODEL), bf16),
                        pltpu.SemaphoreType.DMA],
        compiler_params=_params(("arbitrary",), VMEM_LIMIT_BIG),
    )(x2, *pieces)


def _adam_call(recv, w, m, v, tc, name):
    rows, cols = w.shape

    def body(r_ref, w_ref, m_ref, v_ref, g_ref, d_ref, mo_ref, vo_ref):
        g = r_ref[0].astype(f32)
        for p in range(1, N_DEV):
            g = g + r_ref[p].astype(f32)
        mn = ADAM_B1 * m_ref[...] + (1.0 - ADAM_B1) * g
        vn = ADAM_B2 * v_ref[...] + (1.0 - ADAM_B2) * (g * g)
        m_hat = mn / (1.0 - ADAM_B1 ** ADAM_STEP)
        v_hat = vn / (1.0 - ADAM_B2 ** ADAM_STEP)
        g_ref[...] = g
        d_ref[...] = -ADAM_LR * (m_hat / (jnp.sqrt(v_hat) + ADAM_EPS) + ADAM_WD * w_ref[...])
        mo_ref[...] = mn
        vo_ref[...] = vn

    blk = pl.BlockSpec((rows, tc), lambda i: (0, i))
    return pl.pallas_call(
        body,
        name=name,
        grid=(cols // tc,),
        in_specs=[pl.BlockSpec((N_DEV, rows, tc), lambda i: (0, 0, i)), blk, blk, blk],
        out_specs=[blk] * 4,
        out_shape=[_sds((rows, cols), f32)] * 4,
        compiler_params=_params(("arbitrary",)),
    )(recv, w, m, v)


def _pad_cols(a, width=128):
    return jnp.pad(a, ((0, 0), (0, width - a.shape[1])))


def _pack_small(ln_g, ln_b, rel, b_f, sink):
    return jnp.concatenate([
        ln_g.reshape(8, 128), ln_b.reshape(8, 128), _pad_cols(rel),
        jnp.pad(_pad_cols(b_f), ((0, 7), (0, 0))), jnp.pad(_pad_cols(sink), ((0, 7), (0, 0)))], axis=0)


def _unpack_small(p):
    return (p[0:8].reshape(1, D_MODEL), p[8:16].reshape(1, D_MODEL), p[16:48, 0:8], p[48:49, 0:8], p[56:57, 0:8])


def kernel(x, w_in, b_f, rel_bias, sink, w_o, ln_g, ln_b, loss_target, m_w_in, m_b_f, m_rel_bias, m_sink, m_w_o, m_ln_g, m_ln_b, v_w_in, v_b_f, v_rel_bias, v_sink, v_w_o, v_ln_g, v_ln_b):
    x2 = x[0]
    tgt = loss_target[0]
    s_len = x2.shape[0]
    shard = w_in.shape[2]

    w_in_t = jnp.transpose(w_in[0])
    g_in, g_o = _gather_call([w_in_t.astype(bf16), w_o[0].astype(bf16)])
    wt_full = g_in.reshape(N_DEV * shard, D_MODEL)
    w_t = jnp.concatenate([wt_full[:O_FF0], wt_full[O_FF1:], wt_full[O_FF0:O_FF1],
                           jnp.zeros((A_W - D_IN, D_MODEL), bf16)], axis=0)
    wo_full = g_o.reshape(D_MODEL, D_MODEL)

    qft, kft, vf, fz, qst, ks, vs, sz, fft, vat, kst, vsta = _proj_call(x2, w_t, 512)
    cum, sgm = _cum_call(fft, b_f.reshape(FOX_HEADS, 1))
    qat, ka, kat, tile_stats = _augment_call(qft, kft, cum.reshape(FOX_HEADS, 1, s_len), 2048)
    npairs, pair_q, pair_k = _fox_prune_tables(tile_stats)
    o_ft, lse_f = _fox_fwd_call(qat, ka, vat, npairs, pair_q, pair_k)
    bucket_t = jnp.asarray(_t5_bucket_table().T)
    bias_t, bias0_t = _swa_bias_call(rel_bias, bucket_t)
    sink_v = sink.reshape(SWA_HEADS)
    o_st, lse_s = _swa_fwd_call(qst, ks, vsta, bias_t, bias0_t, sink_v)

    (dh, do_f, dfz, do_s, dsz, dl_f, dl_s, dwo, dg, db, loss_part) = _post_call(
        o_ft.reshape(FOX_HEADS * HEAD_DIM, s_len), fz, o_st.reshape(SWA_HEADS * HEAD_DIM, s_len), sz, x2, tgt,
        wo_full, ln_g, ln_b, jnp.asarray(_head_selector()).astype(bf16), 512)

    dqf, dkf, dvf, dcq, dck = _fox_bwd_call(ka, kat, vf, qat, do_f.reshape(FOX_HEADS, HEAD_DIM, s_len), lse_f,
                                            dl_f.reshape(FOX_HEADS, 1, s_len), npairs, pair_q, pair_k)
    dqf, dkf, dvf = (a.reshape(FOX_HEADS * HEAD_DIM, s_len) for a in (dqf, dkf, dvf))
    dfft, dbf = _cum_bwd_call(dcq.reshape(FOX_HEADS, s_len), dck.reshape(FOX_HEADS, s_len), sgm)
    dqs, dks, dvs, dbias, dsink = _swa_bwd_call(
        qst, ks, kst, vs, do_s.reshape(SWA_HEADS, HEAD_DIM, s_len), lse_s, dl_s.reshape(SWA_HEADS, 1, s_len),
        bias_t, bias0_t, sink_v)
    dqs = dqs.reshape(SWA_HEADS * HEAD_DIM, s_len)
    dks, dvs = (a.reshape(SWA_KV_HEADS * HEAD_DIM, s_len) for a in (dks, dvs))
    drel = _swa_bias_bwd_call(dbias, bucket_t)

    pieces = (dqf, dkf, dvf, dfz, dqs, dks, dvs, dsz, dfft)
    dw_t = _dw_call(x2, pieces, 1024)

    dwt_full = jnp.concatenate([dw_t[:O_FF0], dw_t[A_FF:A_FF + (O_FF1 - O_FF0)], dw_t[O_FF0:A_FF]], axis=0)
    dw_blocks = dwt_full.reshape(N_DEV, shard, D_MODEL)
    dwo_blocks = dwo.reshape(N_DEV, D_MODEL // N_DEV, D_MODEL).astype(bf16)
    small = _pack_small(dg, db, drel[:, 0:8], dbf[:, 0].reshape(1, 8), dsink[:, 0].reshape(1, 8))
    loss_slot = np.zeros((64, 128), bool)
    loss_slot[49, 0] = True
    small = jnp.where(jnp.asarray(loss_slot), loss_part[0, 0], small)
    small_blocks = jnp.broadcast_to(small[None], (N_DEV,) + small.shape)
    dx, (r_in, r_o, r_small) = _dx_exchange_call(dh, pieces, w_t, [dw_blocks, dwo_blocks, small_blocks], 256)

    win_t = [jnp.transpose(a) for a in _adam_call(
        r_in, w_in_t, jnp.transpose(m_w_in[0]), jnp.transpose(v_w_in[0]), 256, "adam_w_in")]
    g_win, d_win, nm_win, nv_win = win_t
    g_wo, d_wo, nm_wo, nv_wo = _adam_call(r_o, w_o[0], m_w_o[0], v_w_o[0], 256, "adam_w_o")
    p_w = _pack_small(ln_g, ln_b, rel_bias, b_f, sink)
    p_m = _pack_small(m_ln_g, m_ln_b, m_rel_bias, m_b_f, m_sink)
    p_v = _pack_small(v_ln_g, v_ln_b, v_rel_bias, v_b_f, v_sink)
    g_p, d_p, nm_p, nv_p = _adam_call(r_small, p_w, p_m, p_v, 128, "adam_small")

    loss = g_p[49, 0]
    g_lng, g_lnb, g_rel, g_bf, g_sink = _unpack_small(g_p)
    d_lng, d_lnb, d_rel, d_bf, d_sink = _unpack_small(d_p)
    m_lng, m_lnb, m_rel, m_bf, m_sk = _unpack_small(nm_p)
    v_lng, v_lnb, v_rel, v_bf, v_sk = _unpack_small(nv_p)
    return (loss, dx[None], g_win[None], g_bf, g_rel, g_sink, g_wo[None], g_lng, g_lnb,
            d_win[None], d_bf, d_rel, d_sink, d_wo[None], d_lng, d_lnb,
            nm_win[None], m_bf, m_rel, m_sk, nm_wo[None], m_lng, m_lnb,
            nv_win[None], v_bf, v_rel, v_sk, nv_wo[None], v_lng, v_lnb)
```

```python
import functools
import math

import numpy as np
import jax
import jax.numpy as jnp
from jax import lax
from jax.experimental import pallas as pl
from jax.experimental.pallas import tpu as pltpu

f32 = jnp.float32
bf16 = jnp.bfloat16

D_MODEL = 1024
HEAD_DIM = 64
FOX_HEADS = 8
SWA_HEADS = 8
SWA_KV_HEADS = 2
SWA_GROUP = 4
BLOCK = 128
NUM_BUCKETS = 32
MAX_DISTANCE = 128
LN_EPS = 1e-5
NEG_INF = -1e30
ALPHA = 2.0 ** 0.25
SCALE = 1.0 / math.sqrt(HEAD_DIM)
D_IN = 3336

ADAM_LR = 0.001
ADAM_B1 = 0.9
ADAM_B2 = 0.999
ADAM_EPS = 1e-08
ADAM_WD = 0.01
ADAM_STEP = 10

N_DEV = 8
A_FQ, A_FK, A_FV, A_FZ, A_SQ, A_SK, A_SV, A_SZ, A_FF, A_W = 0, 512, 1024, 1536, 2048, 2560, 2688, 2816, 3328, 3456
O_FF0, O_FF1 = 1536, 1544

VMEM_LIMIT = 48 * 1024 * 1024
HIGHEST = lax.Precision.HIGHEST
NT = (((1,), (1,)), ((), ()))
TN = (((0,), (0,)), ((), ()))
MESH = pl.DeviceIdType.MESH
RELS = [(0, 0, 1), (0, 1, 0), (0, 1, 1), (1, 0, 0), (1, 0, 1), (1, 1, 0), (1, 1, 1)]


VMEM_LIMIT_BIG = 60 * 1024 * 1024


def _params(sem=None, vmem=VMEM_LIMIT):
    return pltpu.CompilerParams(dimension_semantics=sem, vmem_limit_bytes=vmem)


def _sds(shape, dtype):
    return jax.ShapeDtypeStruct(shape, dtype)


def _t5_bucket_table():
    qi = np.arange(BLOCK)[:, None]
    kj = np.arange(2 * BLOCK)[None, :]
    rel = qi + BLOCK - kj
    band = (rel >= 0) & (rel < BLOCK)
    relc = np.maximum(rel, 0)
    max_exact = NUM_BUCKETS // 2
    relf = np.maximum(relc, 1).astype(np.float32)
    large = max_exact + (np.log(relf / np.float32(max_exact)) / np.float32(math.log(MAX_DISTANCE / max_exact))
                         * np.float32(NUM_BUCKETS - max_exact)).astype(np.int32)
    large = np.minimum(large, NUM_BUCKETS - 1)
    bucket = np.where(relc < max_exact, relc, large).astype(np.int32)
    bucket = np.where(band, bucket, -1).astype(np.int32)
    return bucket


def _mesh_pos():
    return lax.axis_index("x"), lax.axis_index("y"), lax.axis_index("c")


def _dev_index(p):
    return 4 * p[0] + 2 * p[1] + p[2]


def _gather_call(xs):
    n = len(xs)

    def body(*refs):
        x_refs, o_refs = refs[:n], refs[n:2 * n]
        send_sems, recv_sems, local_sems = refs[2 * n:]
        x, y, c = _mesh_pos()
        me, sib = (x, y, c), (x, y, 1 - c)
        chips = [(1 - x, y), (x, 1 - y), (1 - x, 1 - y)]

        def copy(a, k, block, to, src=None):
            slot = o_refs[a].at[_dev_index(block)]
            return pltpu.make_async_remote_copy(
                src_ref=slot if src is None else src, dst_ref=slot,
                send_sem=send_sems.at[a * 7 + k], recv_sem=recv_sems.at[a * 7 + k],
                device_id=to, device_id_type=MESH)

        mine = [pltpu.make_async_copy(x_refs[a], o_refs[a].at[_dev_index(me)], local_sems.at[a]) for a in range(n)]
        for cp in mine:
            cp.start()
        first = []
        for a in range(n):
            first.append(copy(a, 0, me, sib, src=x_refs[a]))
            first += [copy(a, 1 + j, me, (*chip, c), src=x_refs[a]) for j, chip in enumerate(chips)]
        for cp in first:
            cp.start()
        passed = []
        for j, chip in enumerate(chips):
            for a in range(n):
                copy(a, 1 + j, (*chip, c), me).wait_recv()
                fwd = copy(a, 4 + j, (*chip, c), sib)
                fwd.start()
                passed.append(fwd)
        for a in range(n):
            copy(a, 0, sib, me).wait_recv()
            for j, chip in enumerate(chips):
                copy(a, 4 + j, (*chip, 1 - c), me).wait_recv()
        for cp in first + passed:
            cp.wait_send()
        for cp in mine:
            cp.wait()

    any_spec = pl.BlockSpec(memory_space=pl.ANY)
    return pl.pallas_call(
        body,
        name="gather_weights",
        out_shape=[_sds((N_DEV,) + a.shape, a.dtype) for a in xs],
        in_specs=[any_spec] * n,
        out_specs=[any_spec] * n,
        scratch_shapes=[pltpu.SemaphoreType.DMA((7 * n,)), pltpu.SemaphoreType.DMA((7 * n,)),
                        pltpu.SemaphoreType.DMA((n,))],
    )(*xs)


def _exchange_copies(b_refs, r_refs, send_sems, recv_sems, local_sems, incoming):
    n = len(b_refs)
    x, y, c = _mesh_pos()
    me_idx = _dev_index((x, y, c))
    mine = [pltpu.make_async_copy(b_refs[a].at[me_idx], r_refs[a].at[me_idx], local_sems.at[a]) for a in range(n)]
    remote = []
    for k, r in enumerate(RELS):
        peer = ((1 - x) if r[0] else x, (1 - y) if r[1] else y, (1 - c) if r[2] else c)
        pidx = _dev_index(peer)
        for a in range(n):
            remote.append(pltpu.make_async_remote_copy(
                src_ref=b_refs[a].at[pidx], dst_ref=r_refs[a].at[pidx if incoming else me_idx],
                send_sem=send_sems.at[a * 7 + k], recv_sem=recv_sems.at[a * 7 + k],
                device_id=peer, device_id_type=MESH))
    return mine, remote


def _exchange_start(b_refs, r_refs, sems):
    mine, out = _exchange_copies(b_refs, r_refs, *sems, incoming=False)
    for cp in mine + out:
        cp.start()


def _exchange_wait(b_refs, r_refs, sems):
    mine, inc = _exchange_copies(b_refs, r_refs, *sems, incoming=True)
    for cp in inc:
        cp.wait_recv()
    for cp in inc:
        cp.wait_send()
    for cp in mine:
        cp.wait()


def _proj_call(x2, w_t, tm):
    s_len = x2.shape[0]

    def body(x_ref, w_ref, qft_ref, kft_ref, vf_ref, fz_ref, qst_ref, ks_ref, vs_ref, sz_ref, fft_ref, vat_ref,
             kst_ref, vsta_ref):
        xb = x_ref[...].astype(bf16)

        def seg_t(off, width):
            return lax.dot_general(w_ref[off:off + width, :], xb, NT, preferred_element_type=f32)

        def seg(off, width):
            return lax.dot_general(xb, w_ref[off:off + width, :], NT, preferred_element_type=f32)

        def put_heads(ref, acc, nheads):
            for h in range(nheads):
                ref[h] = acc[:, h * HEAD_DIM:(h + 1) * HEAD_DIM].astype(bf16)

        def put_heads_t(ref, acc_t, nheads, augment):
            for h in range(nheads):
                ref[h, 0:HEAD_DIM, :] = acc_t[h * HEAD_DIM:(h + 1) * HEAD_DIM, :].astype(bf16)
                if augment:
                    ref[h, HEAD_DIM:2 * HEAD_DIM, :] = ones_row

        ones_row = jnp.where(lax.broadcasted_iota(jnp.int32, (HEAD_DIM, tm), 0) == 0, 1.0, 0.0).astype(bf16)
        put_heads_t(vat_ref, seg_t(A_FV, 512), FOX_HEADS, True)
        put_heads_t(qft_ref, seg_t(A_FQ, 512) * SCALE, FOX_HEADS, False)
        put_heads_t(kft_ref, seg_t(A_FK, 512), FOX_HEADS, False)
        put_heads(vf_ref, seg(A_FV, 512), FOX_HEADS)
        fz_ref[...] = seg(A_FZ, 512)
        put_heads_t(qst_ref, seg_t(A_SQ, 512) * SCALE, SWA_HEADS, False)
        put_heads(ks_ref, seg(A_SK, 128), SWA_KV_HEADS)
        put_heads(vs_ref, seg(A_SV, 128), SWA_KV_HEADS)
        put_heads_t(kst_ref, seg_t(A_SK, 128), SWA_KV_HEADS, False)
        put_heads_t(vsta_ref, seg_t(A_SV, 128), SWA_KV_HEADS, True)
        sz_ref[...] = seg(A_SZ, 512)
        fft_ref[...] = seg(A_FF, 128).T[:FOX_HEADS, :]

    def heads(nh):
        return pl.BlockSpec((nh, tm, HEAD_DIM), lambda i: (0, i, 0))

    def feat(nh, rows):
        return pl.BlockSpec((nh, rows, tm), lambda i: (0, 0, i))

    wide = pl.BlockSpec((tm, 512), lambda i: (i, 0))
    return pl.pallas_call(
        body,
        name="proj_fwd",
        grid=(s_len // tm,),
        in_specs=[pl.BlockSpec((tm, D_MODEL), lambda i: (i, 0)), pl.BlockSpec((A_W, D_MODEL), lambda i: (0, 0))],
        out_specs=[feat(8, HEAD_DIM), feat(8, HEAD_DIM), heads(8), wide, feat(8, HEAD_DIM), heads(2), heads(2), wide,
                   pl.BlockSpec((FOX_HEADS, tm), lambda i: (0, i)),
                   feat(FOX_HEADS, 2 * HEAD_DIM), feat(2, HEAD_DIM), feat(2, 2 * HEAD_DIM)],
        out_shape=[_sds((8, HEAD_DIM, s_len), bf16)] * 2 + [_sds((8, s_len, HEAD_DIM), bf16)]
                  + [_sds((s_len, 512), f32), _sds((8, HEAD_DIM, s_len), bf16),
                     _sds((2, s_len, HEAD_DIM), bf16), _sds((2, s_len, HEAD_DIM), bf16), _sds((s_len, 512), f32),
                     _sds((FOX_HEADS, s_len), f32), _sds((FOX_HEADS, 2 * HEAD_DIM, s_len), bf16),
                     _sds((2, HEAD_DIM, s_len), bf16), _sds((2, 2 * HEAD_DIM, s_len), bf16)],
        compiler_params=_params(("arbitrary",)),
    )(x2, w_t)


AUG = 2 * HEAD_DIM


def _augment_call(q_t, k_t, cum_row, tm):
    nh, _, s_len = k_t.shape
    per_step = tm // FOX_T

    def body(qt_ref, kt_ref, c_ref, qat_ref, ka_ref, kat_ref, st_ref):
        c = c_ref[0]
        hi = c.astype(bf16).astype(f32)
        r1 = c - hi
        mid = r1.astype(bf16).astype(f32)
        lo = (r1 - mid).astype(bf16).astype(f32)
        row = lax.broadcasted_iota(jnp.int32, (HEAD_DIM, tm), 0)
        q_tail = jnp.where(row == 0, hi, jnp.where(row == 1, mid, jnp.where(row == 2, lo,
                           jnp.where(row < 6, 1.0, 0.0))))
        k_tail = jnp.where(row < 3, 1.0, jnp.where(row == 3, -hi, jnp.where(row == 4, -mid,
                           jnp.where(row == 5, -lo, 0.0))))
        qat_ref[0, 0:HEAD_DIM, :] = qt_ref[0]
        qat_ref[0, HEAD_DIM:AUG, :] = q_tail.astype(bf16)
        kat_ref[0, 0:HEAD_DIM, :] = kt_ref[0]
        kat_ref[0, HEAD_DIM:AUG, :] = k_tail.astype(bf16)
        qt = qt_ref[0].astype(f32)
        kt = kt_ref[0].astype(f32)
        ka_ref[0] = jnp.concatenate([kt, k_tail], axis=0).T.astype(bf16)
        qn2 = jnp.sum(qt * qt, axis=0, keepdims=True)
        kn2 = jnp.sum(kt * kt, axis=0, keepdims=True)
        sd = jnp.sum(qt * kt, axis=0, keepdims=True)
        srow = lax.broadcasted_iota(jnp.int32, (8, LANES), 0)
        for part in range(per_step):
            sl = slice(part * FOX_T, (part + 1) * FOX_T)
            vals = [jnp.sqrt(jnp.max(qn2[:, sl], axis=1, keepdims=True)),
                    jnp.sqrt(jnp.max(kn2[:, sl], axis=1, keepdims=True)),
                    jnp.min(sd[:, sl], axis=1, keepdims=True),
                    jnp.max(c[:, sl], axis=1, keepdims=True), jnp.min(c[:, sl], axis=1, keepdims=True)]
            out = jnp.zeros((8, LANES), f32)
            for r, val in enumerate(vals):
                out = jnp.where(srow == r, val, out)
            st_ref[0, part] = out

    tile_t = pl.BlockSpec((1, HEAD_DIM, tm), lambda h, i: (h, 0, i))
    return pl.pallas_call(
        body,
        name="fox_augment",
        grid=(nh, s_len // tm),
        in_specs=[tile_t, tile_t, pl.BlockSpec((1, 1, tm), lambda h, i: (h, 0, i))],
        out_specs=[pl.BlockSpec((1, AUG, tm), lambda h, i: (h, 0, i)),
                   pl.BlockSpec((1, tm, AUG), lambda h, i: (h, i, 0)),
                   pl.BlockSpec((1, AUG, tm), lambda h, i: (h, 0, i)),
                   pl.BlockSpec((1, per_step, 8, LANES), lambda h, i: (h, i, 0, 0))],
        out_shape=[_sds((nh, AUG, s_len), bf16), _sds((nh, s_len, AUG), bf16), _sds((nh, AUG, s_len), bf16),
                   _sds((nh, s_len // FOX_T, 8, LANES), f32)],
        compiler_params=_params(("arbitrary", "arbitrary")),
    )(q_t, k_t, cum_row)


FOX_PRUNE_GAP = 32.0


def _fox_prune_tables(stats):
    s = stats[:, :, :, 0]
    qn, kn, sd, cmx, cmn = (s[:, :, r] for r in range(5))
    nt = s.shape[1]
    bound = qn[:, :, None] * kn[:, None, :] + (cmx[:, :, None] - cmn[:, None, :])
    margin = 0.01 + 1e-5 * (jnp.abs(cmx)[:, :, None] + jnp.abs(cmn)[:, None, :])
    qi = lax.broadcasted_iota(jnp.int32, (nt, nt), 0)
    kj = lax.broadcasted_iota(jnp.int32, (nt, nt), 1)
    skip = (bound + margin < sd[:, :, None] - FOX_PRUNE_GAP) & (kj < qi)[None]
    first = jnp.sum(jnp.cumprod(skip.astype(jnp.int32), axis=2), axis=2)
    tiles = lax.broadcasted_iota(jnp.int32, (1, nt), 1)
    cnt = tiles - first
    ends = jnp.cumsum(cnt, axis=1)
    off = ends - cnt
    kmax = nt * (nt - 1) // 2
    k = lax.broadcasted_iota(jnp.int32, (1, kmax), 1)
    pair_q = jnp.minimum(jnp.sum((ends[:, None, :] <= k[:, :, None]).astype(jnp.int32), axis=2), nt - 1)
    hit = pair_q[:, :, None] == tiles[:, None, :]
    first_k = jnp.sum(jnp.where(hit, first[:, None, :], 0), axis=2)
    off_k = jnp.sum(jnp.where(hit, off[:, None, :], 0), axis=2)
    pair_k = jnp.clip(first_k + k - off_k, 0, nt - 1)
    return (ends[:, nt - 1].astype(jnp.int32), pair_q.reshape(-1).astype(jnp.int32),
            pair_k.reshape(-1).astype(jnp.int32))


CUM_CHUNK = 512


def _cum_call(fft, bf_col):
    s_len = fft.shape[1]
    ch = CUM_CHUNK

    def body(f_ref, b_ref, cum_ref, sg_ref):
        r = lax.broadcasted_iota(jnp.int32, (ch, ch), 0)
        c = lax.broadcasted_iota(jnp.int32, (ch, ch), 1)
        upper = (r <= c).astype(f32)
        carry = jnp.zeros((FOX_HEADS, 1), f32)
        for n in range(s_len // ch):
            z = f_ref[:, n * ch:(n + 1) * ch] + b_ref[...]
            logf = jnp.minimum(z, 0.0) - jnp.log1p(jnp.exp(-jnp.abs(z)))
            sg_ref[:, n * ch:(n + 1) * ch] = 1.0 / (1.0 + jnp.exp(z))
            cs = jnp.dot(logf, upper, precision=HIGHEST, preferred_element_type=f32) + carry
            cum_ref[:, n * ch:(n + 1) * ch] = cs
            carry = cs[:, ch - 1:ch]

    return pl.pallas_call(
        body,
        name="fox_cum_fwd",
        out_shape=[_sds((FOX_HEADS, s_len), f32)] * 2,
        compiler_params=_params(),
    )(fft, bf_col)


def _cum_bwd_call(dcq, dck, sg):
    s_len = sg.shape[1]
    ch = CUM_CHUNK
    nch = s_len // ch

    def body(q_ref, k_ref, sg_ref, dff_ref, dbf_ref):
        r = lax.broadcasted_iota(jnp.int32, (ch, ch), 0)
        c = lax.broadcasted_iota(jnp.int32, (ch, ch), 1)
        lower = (r >= c).astype(f32)
        dff_ref[...] = jnp.zeros_like(dff_ref)
        carry = jnp.zeros((FOX_HEADS, 1), f32)
        total = jnp.zeros((FOX_HEADS, 1), f32)
        for n in reversed(range(nch)):
            sl = slice(n * ch, (n + 1) * ch)
            dcum = q_ref[:, sl] - k_ref[:, sl]
            rs = jnp.dot(dcum, lower, precision=HIGHEST, preferred_element_type=f32) + carry
            carry = rs[:, 0:1]
            dff = rs * sg_ref[:, sl]
            dff_ref[0:FOX_HEADS, sl] = dff
            total = total + jnp.sum(dff, axis=1, keepdims=True)
        dbf_ref[...] = jnp.broadcast_to(total, (FOX_HEADS, 128))

    return pl.pallas_call(
        body,
        name="fox_cum_bwd",
        out_shape=[_sds((128, s_len), f32), _sds((FOX_HEADS, 128), f32)],
        compiler_params=_params(),
    )(dcq, dck, sg)


FOX_T = 512
LANES = 128


def _causal_keep(t):
    return lax.broadcasted_iota(jnp.int32, (t, t), 0) <= lax.broadcasted_iota(jnp.int32, (t, t), 1)


def _tile_cols(i, t):
    return pl.ds(pl.multiple_of(i * t, t), t)


def _fox_pair(n, nt, kmax, h, pq_ref, pk_ref):
    k = h * kmax + jnp.maximum(n - nt, 0)
    return jnp.where(n < nt, n, pq_ref[k]), jnp.where(n < nt, n, pk_ref[k])


def _fox_fwd_call(qat, ka, vat, npairs, pair_q, pair_k):
    nh, s_len, _ = ka.shape
    t = FOX_T
    nt = s_len // t
    kmax = nt * (nt - 1) // 2
    assert nt >= 2 and nt % 2 == 0

    def body(np_ref, pq_ref, pk_ref, qat_ref, ka_ref, vat_ref, o_ref, lse_ref, s0, s1, p0, p1, a0, a1, m_all, acc_all):
        h = pl.program_id(0)
        extra = np_ref[h]
        total = nt + extra
        m_all[...] = jnp.full(m_all.shape, NEG_INF, f32)
        acc_all[...] = jnp.zeros(acc_all.shape, f32)
        bufs = ((s0, p0, a0), (s1, p1, a1))

        def pair(n):
            return _fox_pair(n, nt, kmax, h, pq_ref, pk_ref)

        def scores(n, b, masked):
            i, j = pair(n)
            st = jnp.dot(ka_ref[0, _tile_cols(j, t), :], qat_ref[0, :, _tile_cols(i, t)], preferred_element_type=f32)
            if masked:
                st = jnp.where(_causal_keep(t), st, NEG_INF)
            bufs[b][0][...] = st

        def softmax(n, b):
            i, _ = pair(n)
            s_ref, p_ref, a_ref = bufs[b]
            for c in range(t // LANES):
                cols = slice(c * LANES, (c + 1) * LANES)
                mcols = pl.ds(pl.multiple_of(i * t + c * LANES, LANES), LANES)
                m_old = m_all[:, mcols]
                m_new = jnp.maximum(m_old, jnp.max(s_ref[:, cols], axis=0, keepdims=True))
                m_all[:, mcols] = m_new
                a_ref[:, cols] = jnp.exp(m_old - m_new)
                p_ref[:, cols] = jnp.exp(s_ref[:, cols] - m_new).astype(bf16)

        def accum(n, b):
            i, j = pair(n)
            cols = _tile_cols(i, t)
            acc_all[:, cols] = bufs[b][2][...] * acc_all[:, cols] + jnp.dot(
                vat_ref[0, :, _tile_cols(j, t)], bufs[b][1][...], preferred_element_type=f32)

        def step(n, b, masked):
            accum(n - 2, b)
            softmax(n - 1, 1 - b)
            scores(n, b, masked)

        scores(0, 0, True)
        scores(1, 1, True)
        softmax(0, 0)

        def diag_steps(d, _):
            n = 2 + 2 * d
            step(n, 0, True)
            step(n + 1, 1, True)
            return 0

        lax.fori_loop(0, (nt - 2) // 2, diag_steps, 0)

        def off_steps(d, _):
            n = nt + 2 * d
            step(n, 0, False)
            step(n + 1, 1, False)
            return 0

        lax.fori_loop(0, extra // 2, off_steps, 0)

        @pl.when(extra % 2 == 1)
        def _():
            step(total - 1, 0, False)
            softmax(total - 1, 0)
            accum(total - 2, 1)
            accum(total - 1, 0)

        @pl.when(extra % 2 == 0)
        def _():
            softmax(total - 1, 1)
            accum(total - 2, 0)
            accum(total - 1, 1)

        l = acc_all[HEAD_DIM:HEAD_DIM + 1, :]
        o_ref[0] = acc_all[0:HEAD_DIM, :] / l
        lse_ref[0] = m_all[...] + jnp.log(l)

    smem = pl.BlockSpec(memory_space=pltpu.SMEM)
    return pl.pallas_call(
        body,
        name="fox_fwd",
        grid=(nh,),
        in_specs=[smem, smem, smem,
                  pl.BlockSpec((1, AUG, s_len), lambda h: (h, 0, 0)),
                  pl.BlockSpec((1, s_len, AUG), lambda h: (h, 0, 0)),
                  pl.BlockSpec((1, AUG, s_len), lambda h: (h, 0, 0))],
        out_specs=[pl.BlockSpec((1, HEAD_DIM, s_len), lambda h: (h, 0, 0)),
                   pl.BlockSpec((1, 1, s_len), lambda h: (h, 0, 0))],
        out_shape=[_sds((nh, HEAD_DIM, s_len), f32), _sds((nh, 1, s_len), f32)],
        scratch_shapes=[pltpu.VMEM((t, t), f32), pltpu.VMEM((t, t), f32), pltpu.VMEM((t, t), bf16),
                        pltpu.VMEM((t, t), bf16), pltpu.VMEM((1, t), f32), pltpu.VMEM((1, t), f32),
                        pltpu.VMEM((1, s_len), f32), pltpu.VMEM((AUG, s_len), f32)],
        compiler_params=_params(("arbitrary",)),
    )(npairs, pair_q, pair_k, qat, ka, vat)


SWA_TS = 512


SWA_W = SWA_GROUP * BLOCK


def _swa_bias_call(rel_bias, bucket_t):
    def body(rb_ref, bk_ref, b_ref, b0_ref):
        bk = bk_ref[...]
        row = lax.broadcasted_iota(jnp.int32, (2 * BLOCK, BLOCK), 0)
        for h in range(SWA_HEADS):
            acc = jnp.full((2 * BLOCK, BLOCK), NEG_INF, f32)
            for b in range(NUM_BUCKETS):
                acc = jnp.where(bk == b, rb_ref[b, h], acc)
            g, hh = divmod(h, SWA_GROUP)
            b_ref[g, :, hh * BLOCK:(hh + 1) * BLOCK] = acc
            b0_ref[g, :, hh * BLOCK:(hh + 1) * BLOCK] = jnp.where(row < BLOCK, NEG_INF, acc)

    return pl.pallas_call(
        body,
        name="swa_bias",
        in_specs=[pl.BlockSpec(memory_space=pltpu.SMEM), pl.BlockSpec(memory_space=pltpu.VMEM)],
        out_shape=[_sds((SWA_KV_HEADS, 2 * BLOCK, SWA_W), f32)] * 2,
        compiler_params=_params(),
    )(rel_bias, bucket_t)


def _swa_bias_bwd_call(dbias, bucket_t):
    def body(d_ref, bk_ref, o_ref):
        bk = bk_ref[...]
        row = lax.broadcasted_iota(jnp.int32, (NUM_BUCKETS, 128), 0)
        col = lax.broadcasted_iota(jnp.int32, (NUM_BUCKETS, 128), 1)
        out = jnp.zeros((NUM_BUCKETS, 128), f32)
        for h in range(SWA_HEADS):
            g, hh = divmod(h, SWA_GROUP)
            d = d_ref[g, :, hh * BLOCK:(hh + 1) * BLOCK]
            for b in range(NUM_BUCKETS):
                val = jnp.sum(jnp.sum(jnp.where(bk == b, d, 0.0), axis=1, keepdims=True), axis=0, keepdims=True)
                out = jnp.where((row == b) & (col == h), val, out)
        o_ref[...] = out

    return pl.pallas_call(
        body,
        name="swa_bias_bwd",
        out_shape=_sds((NUM_BUCKETS, 128), f32),
        compiler_params=_params(),
    )(dbias, bucket_t)


def _sink_row(sink_ref, g):
    return jnp.concatenate([jnp.full((1, BLOCK), sink_ref[g * SWA_GROUP + hh], f32) for hh in range(SWA_GROUP)], axis=1)


def _group_lanes(ref, g, cols):
    return jnp.concatenate([ref[g * SWA_GROUP + hh, :, cols] for hh in range(SWA_GROUP)], axis=1)


def _swa_fwd_call(qt, k, vta, bias_t, bias0_t, sink):
    s_len = qt.shape[2]
    ts = SWA_TS
    nb = ts // BLOCK

    def body(qt_ref, kc_ref, kp_ref, vc_ref, vp_ref, b_ref, b0_ref, sink_ref, o_ref, lse_ref):
        first = pl.program_id(0) == 0
        kall = [jnp.concatenate([kp_ref[g], kc_ref[g]], axis=0) for g in range(SWA_KV_HEADS)]
        vall = [jnp.concatenate([vp_ref[g], vc_ref[g]], axis=1) for g in range(SWA_KV_HEADS)]
        sinks = [_sink_row(sink_ref, g) for g in range(SWA_KV_HEADS)]
        items = [(g, b) for g in range(SWA_KV_HEADS) for b in range(nb)]

        def scores(g, b):
            qg = _group_lanes(qt_ref, g, slice(b * BLOCK, (b + 1) * BLOCK))
            bias_b = b_ref[g]
            if b == 0:
                bias_b = jnp.where(first, b0_ref[g], bias_b)
            return jnp.dot(kall[g][b * BLOCK:(b + 2) * BLOCK], qg, preferred_element_type=f32) + bias_b

        def finish(g, b, st):
            m = jnp.maximum(jnp.max(st, axis=0, keepdims=True), sinks[g])
            pt = jnp.exp(st - m)
            acc = jnp.dot(vall[g][:, b * BLOCK:(b + 2) * BLOCK], pt.astype(bf16), preferred_element_type=f32)
            l = acc[HEAD_DIM:HEAD_DIM + 1, :] + jnp.exp(sinks[g] - m)
            return acc[0:HEAD_DIM, :] / l, m + jnp.log(l)

        outs, lses = {}, {}
        st_next = scores(*items[0])
        for idx, (g, b) in enumerate(items):
            st = st_next
            if idx + 1 < len(items):
                st_next = scores(*items[idx + 1])
            outs[g, b], lses[g, b] = finish(g, b, st)
        for g in range(SWA_KV_HEADS):
            for hh in range(SWA_GROUP):
                lanes = slice(hh * BLOCK, (hh + 1) * BLOCK)
                o_ref[g * SWA_GROUP + hh] = jnp.concatenate([outs[g, b][:, lanes] for b in range(nb)], axis=1)
                lse_ref[g * SWA_GROUP + hh] = jnp.concatenate([lses[g, b][:, lanes] for b in range(nb)], axis=1)

    def prev_blk(n):
        return jnp.maximum(n * nb - 1, 0)

    bspec = pl.BlockSpec((SWA_KV_HEADS, 2 * BLOCK, SWA_W), lambda n: (0, 0, 0))
    return pl.pallas_call(
        body,
        name="swa_fwd",
        grid=(s_len // ts,),
        in_specs=[pl.BlockSpec((SWA_HEADS, HEAD_DIM, ts), lambda n: (0, 0, n)),
                  pl.BlockSpec((SWA_KV_HEADS, ts, HEAD_DIM), lambda n: (0, n, 0)),
                  pl.BlockSpec((SWA_KV_HEADS, BLOCK, HEAD_DIM), lambda n: (0, prev_blk(n), 0)),
                  pl.BlockSpec((SWA_KV_HEADS, AUG, ts), lambda n: (0, 0, n)),
                  pl.BlockSpec((SWA_KV_HEADS, AUG, BLOCK), lambda n: (0, 0, prev_blk(n))),
                  bspec, bspec, pl.BlockSpec(memory_space=pltpu.SMEM)],
        out_specs=[pl.BlockSpec((SWA_HEADS, HEAD_DIM, ts), lambda n: (0, 0, n)),
                   pl.BlockSpec((SWA_HEADS, 1, ts), lambda n: (0, 0, n))],
        out_shape=[_sds((SWA_HEADS, HEAD_DIM, s_len), f32), _sds((SWA_HEADS, 1, s_len), f32)],
        compiler_params=_params(("arbitrary",)),
    )(qt, k, k, vta, vta, bias_t, bias0_t, sink)


def _head_selector():
    sel = np.zeros((512, 128), np.float32)
    for h in range(8):
        sel[h * HEAD_DIM:(h + 1) * HEAD_DIM, h] = 1.0
    return sel


def _post_call(of, fz, osw, sz, x2, tgt, wo, ln_g, ln_b, sel, tm):
    s_len = x2.shape[0]

    def body(of_ref, fz_ref, os_ref, sz_ref, x_ref, t_ref, wo_ref, g_ref, b_ref, sel_ref,
             dh_ref, dof_ref, dfz_ref, dos_ref, dsz_ref, dlf_ref, dls_ref, dwo_ref, dg_ref, db_ref, loss_ref):
        n = pl.program_id(0)

        @pl.when(n == 0)
        def _():
            dwo_ref[...] = jnp.zeros_like(dwo_ref)
            dg_ref[...] = jnp.zeros_like(dg_ref)
            db_ref[...] = jnp.zeros_like(db_ref)
            loss_ref[...] = jnp.zeros_like(loss_ref)

        gam = g_ref[...]
        sel_m = sel_ref[...]

        def forward(r):
            o_f = of_ref[:, r].T
            o_s = os_ref[:, r].T
            fz = fz_ref[r, :]
            sz = sz_ref[r, :]
            sg_f = jax.nn.sigmoid(fz)
            sg_s = jax.nn.sigmoid(sz)
            silu_f = fz * sg_f
            silu_s = sz * sg_s
            mixed = jnp.concatenate([o_f * silu_f, o_s * silu_s], axis=1).astype(bf16)
            y = jnp.dot(mixed, wo_ref[...], preferred_element_type=f32)
            return o_f, o_s, fz, sz, sg_f, sg_s, silu_f, silu_s, mixed, y

        def norm_and_back(r, fwd):
            mixed, y = fwd[8], fwd[9]
            h = ALPHA * x_ref[r, :] + y
            mu = jnp.mean(h, axis=1, keepdims=True)
            hc = h - mu
            var = jnp.mean(hc * hc, axis=1, keepdims=True)
            rstd = lax.rsqrt(var + LN_EPS)
            xhat = hc * rstd
            out = xhat * gam + b_ref[...]
            err = out - t_ref[r, :]
            tok_loss = jnp.mean(err * err, axis=1, keepdims=True)
            loss_ref[...] += 0.5 * jnp.sum(tok_loss, axis=0, keepdims=True)
            dout = err * (1.0 / D_MODEL)
            dg_ref[...] += jnp.sum(dout * xhat, axis=0, keepdims=True)
            db_ref[...] += jnp.sum(dout, axis=0, keepdims=True)
            dxh = dout * gam
            m1 = jnp.mean(dxh, axis=1, keepdims=True)
            m2 = jnp.mean(dxh * xhat, axis=1, keepdims=True)
            dh = rstd * (dxh - m1 - xhat * m2)
            dh_ref[r, :] = dh
            dyb = dh.astype(bf16)
            dmix = lax.dot_general(dyb, wo_ref[...], NT, preferred_element_type=f32)
            dwo_ref[...] += lax.dot_general(mixed, dyb, TN, preferred_element_type=f32)
            return dmix

        def head_sums(prod):
            hi = prod.astype(bf16)
            lo = (prod - hi.astype(f32)).astype(bf16)
            return (jnp.dot(hi, sel_m, preferred_element_type=f32) + jnp.dot(lo, sel_m, preferred_element_type=f32))

        def gates_back(r, fwd, dmix):
            o_f, o_s, fz, sz, sg_f, sg_s, silu_f, silu_s = fwd[:8]
            dm_f = dmix[:, :512]
            dm_s = dmix[:, 512:]
            do_f = dm_f * silu_f
            do_s = dm_s * silu_s
            dfz_ref[r, :] = (dm_f * o_f * (sg_f * (1.0 + fz * (1.0 - sg_f)))).astype(bf16)
            dsz_ref[r, :] = (dm_s * o_s * (sg_s * (1.0 + sz * (1.0 - sg_s)))).astype(bf16)
            dof_ref[:, r] = do_f.T.astype(bf16)
            dos_ref[:, r] = do_s.T.astype(bf16)
            dlf_ref[:, r] = head_sums(do_f * o_f).T[:FOX_HEADS, :]
            dls_ref[:, r] = head_sums(do_s * o_s).T[:SWA_HEADS, :]

        halves = [slice(k * (tm // 2), (k + 1) * (tm // 2)) for k in range(2)]
        fwds = [forward(r) for r in halves]
        dmixes = [norm_and_back(r, f) for r, f in zip(halves, fwds)]
        for r, f, d in zip(halves, fwds, dmixes):
            gates_back(r, f, d)

    feat = pl.BlockSpec((512, tm), lambda n: (0, n))
    rows8 = pl.BlockSpec((8, tm), lambda n: (0, n))
    half = pl.BlockSpec((tm, 512), lambda n: (n, 0))
    fullw = pl.BlockSpec((tm, D_MODEL), lambda n: (n, 0))
    vec = pl.BlockSpec((1, D_MODEL), lambda n: (0, 0))
    return pl.pallas_call(
        body,
        name="post_fwd_bwd",
        grid=(s_len // tm,),
        in_specs=[feat, half, feat, half, fullw, fullw,
                  pl.BlockSpec((D_MODEL, D_MODEL), lambda n: (0, 0)), vec, vec,
                  pl.BlockSpec((512, 128), lambda n: (0, 0))],
        out_specs=[fullw, feat, half, feat, half, rows8, rows8,
                   pl.BlockSpec((D_MODEL, D_MODEL), lambda n: (0, 0)), vec, vec,
                   pl.BlockSpec((1, 1), lambda n: (0, 0))],
        out_shape=[_sds((s_len, D_MODEL), f32), _sds((512, s_len), bf16), _sds((s_len, 512), bf16),
                   _sds((512, s_len), bf16), _sds((s_len, 512), bf16),
                   _sds((FOX_HEADS, s_len), f32), _sds((SWA_HEADS, s_len), f32),
                   _sds((D_MODEL, D_MODEL), f32), _sds((1, D_MODEL), f32), _sds((1, D_MODEL), f32),
                   _sds((1, 1), f32)],
        compiler_params=_params(("arbitrary",), VMEM_LIMIT_BIG),
    )(of, fz, osw, sz, x2, tgt, wo, ln_g, ln_b, sel)


def _fox_bwd_call(ka, kat, v, qat, dot, lse_row, dl_row, npairs, pair_q, pair_k):
    nh, s_len, _ = ka.shape
    t = FOX_T
    nt = s_len // t
    kmax = nt * (nt - 1) // 2
    assert nt >= 2 and nt % 2 == 0
    ck_slot = HEAD_DIM + 3
    cq_slot = HEAD_DIM

    def body(np_ref, pq_ref, pk_ref, ka_ref, kat_ref, v_ref, qat_ref, dot_ref, lse_ref, dl_ref,
             dq_ref, dk_ref, dv_ref, dcq_ref, dck_ref, dqt_all, dkat_all, dvt_all, p0, p1, ds0, ds1):
        h = pl.program_id(0)
        extra = np_ref[h]
        total = nt + extra
        dqt_all[...] = jnp.zeros(dqt_all.shape, f32)
        dkat_all[...] = jnp.zeros(dkat_all.shape, f32)
        dvt_all[...] = jnp.zeros(dvt_all.shape, f32)
        pbuf, dsbuf = (p0, p1), (ds0, ds1)

        def pair(n):
            return _fox_pair(n, nt, kmax, h, pq_ref, pk_ref)

        def probs(n, b, masked):
            i, j = pair(n)
            qc, kr = _tile_cols(i, t), _tile_cols(j, t)
            st = jnp.dot(ka_ref[0, kr, :], qat_ref[0, :, qc], preferred_element_type=f32)
            dpt = jnp.dot(v_ref[0, kr, :], dot_ref[0, :, qc], preferred_element_type=f32)
            if masked:
                st = jnp.where(_causal_keep(t), st, NEG_INF)
            pt = jnp.exp(st - lse_ref[0, :, qc])
            pbuf[b][...] = pt.astype(bf16)
            dsbuf[b][...] = (pt * (dpt - dl_ref[0, :, qc])).astype(bf16)

        def grads(n, b):
            i, j = pair(n)
            qc, kc = _tile_cols(i, t), _tile_cols(j, t)
            dvt_all[:, kc] += lax.dot_general(dot_ref[0, :, qc], pbuf[b][...], NT, preferred_element_type=f32)
            dkat_all[:, kc] += lax.dot_general(qat_ref[0, :, qc], dsbuf[b][...], NT, preferred_element_type=f32)
            dqt_all[:, qc] += jnp.dot(kat_ref[0, :, kc], dsbuf[b][...], preferred_element_type=f32)

        def step(n, b, masked):
            i, j = pair(n)
            qc, kr = _tile_cols(i, t), _tile_cols(j, t)
            i1, j1 = pair(n - 1)
            qc1, kc1 = _tile_cols(i1, t), _tile_cols(j1, t)
            c = 1 - b
            st = jnp.dot(ka_ref[0, kr, :], qat_ref[0, :, qc], preferred_element_type=f32)
            dvt_all[:, kc1] += lax.dot_general(dot_ref[0, :, qc1], pbuf[c][...], NT, preferred_element_type=f32)
            if masked:
                st = jnp.where(_causal_keep(t), st, NEG_INF)
            pt = jnp.exp(st - lse_ref[0, :, qc])
            pbuf[b][...] = pt.astype(bf16)
            dpt = jnp.dot(v_ref[0, kr, :], dot_ref[0, :, qc], preferred_element_type=f32)
            dkat_all[:, kc1] += lax.dot_general(qat_ref[0, :, qc1], dsbuf[c][...], NT, preferred_element_type=f32)
            dqt_all[:, qc1] += jnp.dot(kat_ref[0, :, kc1], dsbuf[c][...], preferred_element_type=f32)
            dsbuf[b][...] = (pt * (dpt - dl_ref[0, :, qc])).astype(bf16)

        probs(0, 0, True)
        step(1, 1, True)

        def diag_steps(d, _):
            n = 2 + 2 * d
            step(n, 0, True)
            step(n + 1, 1, True)
            return 0

        lax.fori_loop(0, (nt - 2) // 2, diag_steps, 0)

        def off_steps(d, _):
            n = nt + 2 * d
            step(n, 0, False)
            step(n + 1, 1, False)
            return 0

        lax.fori_loop(0, extra // 2, off_steps, 0)

        @pl.when(extra % 2 == 1)
        def _():
            step(total - 1, 0, False)
            grads(total - 1, 0)

        @pl.when(extra % 2 == 0)
        def _():
            grads(total - 1, 1)

        dq_ref[0] = (dqt_all[0:HEAD_DIM, :] * SCALE).astype(bf16)
        dk_ref[0] = dkat_all[0:HEAD_DIM, :].astype(bf16)
        dv_ref[0] = dvt_all[...].astype(bf16)
        dcq_ref[0] = dqt_all[cq_slot:cq_slot + 1, :]
        dck_ref[0] = dkat_all[ck_slot:ck_slot + 1, :]

    smem = pl.BlockSpec(memory_space=pltpu.SMEM)
    rows = pl.BlockSpec((1, s_len, AUG), lambda h: (h, 0, 0))
    feat = pl.BlockSpec((1, AUG, s_len), lambda h: (h, 0, 0))
    feat64 = pl.BlockSpec((1, HEAD_DIM, s_len), lambda h: (h, 0, 0))
    rowv = pl.BlockSpec((1, 1, s_len), lambda h: (h, 0, 0))
    return pl.pallas_call(
        body,
        name="fox_bwd",
        grid=(nh,),
        in_specs=[smem, smem, smem, rows, feat, pl.BlockSpec((1, s_len, HEAD_DIM), lambda h: (h, 0, 0)), feat, feat64,
                  rowv, rowv],
        out_specs=[feat64, feat64, feat64, rowv, rowv],
        out_shape=[_sds((nh, HEAD_DIM, s_len), bf16)] * 3 + [_sds((nh, 1, s_len), f32)] * 2,
        scratch_shapes=[pltpu.VMEM((AUG, s_len), f32), pltpu.VMEM((AUG, s_len), f32), pltpu.VMEM((HEAD_DIM, s_len), f32)]
                       + [pltpu.VMEM((t, t), bf16)] * 4,
        compiler_params=_params(("arbitrary",)),
    )(npairs, pair_q, pair_k, ka, kat, v, qat, dot, lse_row, dl_row)


def _swa_bwd_call(qt, k, kt, v, dot, lse, dl, bias_t, bias0_t, sink):
    s_len = qt.shape[2]
    ts = SWA_TS
    nb = ts // BLOCK
    nsteps = s_len // ts

    def body(qt_ref, kc_ref, kp_ref, ktc_ref, ktp_ref, vc_ref, vp_ref, dot_ref, lse_ref, dl_ref, b_ref, b0_ref,
             sink_ref, dq_ref, dk_ref, dv_ref, dbias_ref, dsink_ref, dk_s, dv_s, tail_k, tail_v, sk_s):
        n = pl.program_id(0)

        @pl.when(n == 0)
        def _():
            dbias_ref[...] = jnp.zeros_like(dbias_ref)
            sk_s[...] = jnp.zeros_like(sk_s)

        @pl.when(n < nsteps)
        def _():
            first = n == 0
            dk_s[...] = jnp.zeros_like(dk_s)
            dv_s[...] = jnp.zeros_like(dv_s)
            groups = range(SWA_KV_HEADS)
            kall = [jnp.concatenate([kp_ref[g], kc_ref[g]], axis=0) for g in groups]
            vall = [jnp.concatenate([vp_ref[g], vc_ref[g]], axis=0) for g in groups]
            ktall = [jnp.concatenate([ktp_ref[g], ktc_ref[g]], axis=1) for g in groups]
            sinks = [_sink_row(sink_ref, g) for g in groups]
            items = [(g, b) for g in groups for b in range(nb)]

            def products(g, b):
                cols = slice(b * BLOCK, (b + 1) * BLOCK)
                win = slice(b * BLOCK, (b + 2) * BLOCK)
                qg = _group_lanes(qt_ref, g, cols)
                dog = _group_lanes(dot_ref, g, cols)
                bias_b = b_ref[g]
                if b == 0:
                    bias_b = jnp.where(first, b0_ref[g], bias_b)
                st = jnp.dot(kall[g][win], qg, preferred_element_type=f32) + bias_b
                dpt = jnp.dot(vall[g][win], dog, preferred_element_type=f32)
                return qg, dog, st, dpt

            def finish(g, b, qg, dog, st, dpt):
                cols = slice(b * BLOCK, (b + 1) * BLOCK)
                win = slice(b * BLOCK, (b + 2) * BLOCK)
                lse_r = _group_lanes(lse_ref, g, cols)
                dl_r = _group_lanes(dl_ref, g, cols)
                pt = jnp.exp(st - lse_r)
                dst = pt * (dpt - dl_r)
                dsb = dst.astype(bf16)
                dk_s[g, :, win] += lax.dot_general(qg, dsb, NT, preferred_element_type=f32)
                dv_s[g, :, win] += lax.dot_general(dog, pt.astype(bf16), NT, preferred_element_type=f32)
                dqg = jnp.dot(ktall[g][:, win], dsb, preferred_element_type=f32) * SCALE
                return dqg, dst, -jnp.exp(sinks[g] - lse_r) * dl_r

            dqs, dsts, sks = {}, {}, {}
            nxt = products(*items[0])
            for idx, (g, b) in enumerate(items):
                cur = nxt
                if idx + 1 < len(items):
                    nxt = products(*items[idx + 1])
                dqs[g, b], dsts[g, b], sks[g, b] = finish(g, b, *cur)
            for g in groups:
                dbias_ref[g] += functools.reduce(lambda a, c: a + c, [dsts[g, b] for b in range(nb)])
                sk_s[g] += functools.reduce(lambda a, c: a + c, [sks[g, b] for b in range(nb)])
                for hh in range(SWA_GROUP):
                    lanes = slice(hh * BLOCK, (hh + 1) * BLOCK)
                    dq_ref[g * SWA_GROUP + hh] = jnp.concatenate(
                        [dqs[g, b][:, lanes] for b in range(nb)], axis=1).astype(bf16)

        @pl.when(n > 0)
        def _():
            last = slice(ts - BLOCK, ts)
            for g in range(SWA_KV_HEADS):
                add_k = jnp.where(n < nsteps, dk_s[g, :, 0:BLOCK], 0.0)
                add_v = jnp.where(n < nsteps, dv_s[g, :, 0:BLOCK], 0.0)
                dk_ref[g, :, 0:ts - BLOCK] = tail_k[g, :, 0:ts - BLOCK].astype(bf16)
                dv_ref[g, :, 0:ts - BLOCK] = tail_v[g, :, 0:ts - BLOCK].astype(bf16)
                dk_ref[g, :, last] = (tail_k[g, :, last] + add_k).astype(bf16)
                dv_ref[g, :, last] = (tail_v[g, :, last] + add_v).astype(bf16)

        @pl.when(n < nsteps)
        def _():
            tail_k[...] = dk_s[:, :, BLOCK:]
            tail_v[...] = dv_s[:, :, BLOCK:]

        @pl.when(n == nsteps)
        def _():
            row = lax.broadcasted_iota(jnp.int32, (SWA_HEADS, 128), 0)
            out = jnp.zeros((SWA_HEADS, 128), f32)
            for h in range(SWA_HEADS):
                g, hh = divmod(h, SWA_GROUP)
                val = jnp.sum(sk_s[g, :, hh * BLOCK:(hh + 1) * BLOCK], axis=1, keepdims=True)
                out = jnp.where(row == h, val, out)
            dsink_ref[...] = out

    last_step = nsteps - 1

    def cl(n):
        return jnp.minimum(n, last_step)

    def prev_blk(n):
        return jnp.maximum(cl(n) * nb - 1, 0)

    feat8 = pl.BlockSpec((SWA_HEADS, HEAD_DIM, ts), lambda n: (0, 0, cl(n)))
    rows8 = pl.BlockSpec((SWA_HEADS, 1, ts), lambda n: (0, 0, cl(n)))
    cur = pl.BlockSpec((SWA_KV_HEADS, ts, HEAD_DIM), lambda n: (0, cl(n), 0))
    prev = pl.BlockSpec((SWA_KV_HEADS, BLOCK, HEAD_DIM), lambda n: (0, prev_blk(n), 0))
    curt = pl.BlockSpec((SWA_KV_HEADS, HEAD_DIM, ts), lambda n: (0, 0, cl(n)))
    prevt = pl.BlockSpec((SWA_KV_HEADS, HEAD_DIM, BLOCK), lambda n: (0, 0, prev_blk(n)))
    bspec = pl.BlockSpec((SWA_KV_HEADS, 2 * BLOCK, SWA_W), lambda n: (0, 0, 0))
    kvout = pl.BlockSpec((SWA_KV_HEADS, HEAD_DIM, ts), lambda n: (0, 0, jnp.maximum(n - 1, 0)))
    return pl.pallas_call(
        body,
        name="swa_bwd",
        grid=(nsteps + 1,),
        in_specs=[feat8, cur, prev, curt, prevt, cur, prev, feat8, rows8, rows8, bspec, bspec,
                  pl.BlockSpec(memory_space=pltpu.SMEM)],
        out_specs=[feat8, kvout, kvout, bspec, pl.BlockSpec((SWA_HEADS, 128), lambda n: (0, 0))],
        out_shape=[_sds((SWA_HEADS, HEAD_DIM, s_len), bf16), _sds((SWA_KV_HEADS, HEAD_DIM, s_len), bf16),
                   _sds((SWA_KV_HEADS, HEAD_DIM, s_len), bf16),
                   _sds((SWA_KV_HEADS, 2 * BLOCK, SWA_W), f32), _sds((SWA_HEADS, 128), f32)],
        scratch_shapes=[pltpu.VMEM((SWA_KV_HEADS, HEAD_DIM, ts + BLOCK), f32),
                        pltpu.VMEM((SWA_KV_HEADS, HEAD_DIM, ts + BLOCK), f32),
                        pltpu.VMEM((SWA_KV_HEADS, HEAD_DIM, ts), f32),
                        pltpu.VMEM((SWA_KV_HEADS, HEAD_DIM, ts), f32),
                        pltpu.VMEM((SWA_KV_HEADS, 1, SWA_W), f32)],
        compiler_params=_params(("arbitrary",)),
    )(qt, k, k, kt, kt, v, v, dot, lse, dl, bias_t, bias0_t, sink)


def _dproj_specs(tm):
    half = pl.BlockSpec((tm, 512), lambda i: (i, 0))
    feat = pl.BlockSpec((512, tm), lambda i: (0, i))
    feat_kv = pl.BlockSpec((128, tm), lambda i: (0, i))
    return [feat, feat, feat, half, feat, feat_kv, feat_kv, half, feat_kv]


def _dx_exchange_call(dh, pieces, w_t, bs, tm):
    s_len = dh.shape[0]
    n = len(bs)
    last = s_len // tm - 1

    def body(*refs):
        dh_ref, dqf_ref, dkf_ref, dvf_ref, dfz_ref, dqs_ref, dks_ref, dvs_ref, dsz_ref, dfft_ref, w_ref = refs[:11]
        b_refs = refs[11:11 + n]
        dx_ref = refs[11 + n]
        r_refs = refs[12 + n:12 + 2 * n]
        sems = refs[12 + 2 * n:]
        i = pl.program_id(0)

        @pl.when(i == 0)
        def _():
            _exchange_start(b_refs, r_refs, sems)

        def tr(ref):
            return ref[...].astype(f32).T.astype(bf16)

        dp = jnp.concatenate([tr(dqf_ref), tr(dkf_ref), tr(dvf_ref), dfz_ref[...], tr(dqs_ref), tr(dks_ref),
                              tr(dvs_ref), dsz_ref[...], tr(dfft_ref)], axis=1)
        dx_ref[...] = ALPHA * dh_ref[...] + jnp.dot(dp, w_ref[...], preferred_element_type=f32)

        @pl.when(i == last)
        def _():
            _exchange_wait(b_refs, r_refs, sems)

    fullw = pl.BlockSpec((tm, D_MODEL), lambda i: (i, 0))
    any_spec = pl.BlockSpec(memory_space=pl.ANY)
    out = pl.pallas_call(
        body,
        name="dx_bwd_exchange",
        grid=(s_len // tm,),
        in_specs=[fullw] + _dproj_specs(tm) + [pl.BlockSpec((A_W, D_MODEL), lambda i: (0, 0))] + [any_spec] * n,
        out_specs=[fullw] + [any_spec] * n,
        out_shape=[_sds((s_len, D_MODEL), f32)] + [_sds(b.shape, b.dtype) for b in bs],
        scratch_shapes=[pltpu.SemaphoreType.DMA((7 * n,)), pltpu.SemaphoreType.DMA((7 * n,)),
                        pltpu.SemaphoreType.DMA((n,))],
        compiler_params=_params(("arbitrary",)),
    )(dh, *pieces, w_t, *bs)
    return out[0], out[1:]


DW_STAGE_ROWS = 384


def _dw_call(x2, pieces, tm):
    s_len = x2.shape[0]
    nt = s_len // tm

    def body(x_ref, dqf_ref, dkf_ref, dvf_ref, dfz_ref, dqs_ref, dks_ref, dvs_ref, dsz_ref, dfft_ref, dw_ref,
             acc_ref, stage_ref, sem):
        i = pl.program_id(0)

        @pl.when(i == 0)
        def _():
            acc_ref[...] = jnp.zeros_like(acc_ref)

        xb = x_ref[...].astype(bf16)

        def add_feat(off, lhs):
            acc_ref[off:off + lhs.shape[0], :] += jnp.dot(lhs, xb, preferred_element_type=f32)

        def add_rows(off, piece):
            acc_ref[off:off + piece.shape[1], :] += lax.dot_general(piece, xb, TN, preferred_element_type=f32)

        add_feat(A_FQ, dqf_ref[...])
        add_feat(A_FK, dkf_ref[...])
        add_feat(A_FV, dvf_ref[...])
        add_rows(A_FZ, dfz_ref[...])
        add_feat(A_SQ, dqs_ref[...])
        add_feat(A_SK, dks_ref[...])
        add_feat(A_SV, dvs_ref[...])
        add_rows(A_SZ, dsz_ref[...])
        add_feat(A_FF, dfft_ref[...].astype(bf16))

        @pl.when(i == nt - 1)
        def _():
            for r in range(A_W // DW_STAGE_ROWS):
                rows = slice(r * DW_STAGE_ROWS, (r + 1) * DW_STAGE_ROWS)
                stage_ref[...] = acc_ref[rows, :].astype(bf16)
                cp = pltpu.make_async_copy(stage_ref, dw_ref.at[rows, :], sem)
                cp.start()
                cp.wait()

    return pl.pallas_call(
        body,
        name="dw_in_bwd",
        grid=(nt,),
        in_specs=[pl.BlockSpec((tm, D_MODEL), lambda i: (i, 0))] + _dproj_specs(tm),
        out_specs=pl.BlockSpec(memory_space=pl.ANY),
        out_shape=_sds((A_W, D_MODEL), bf16),
        scratch_shapes=[pltpu.VMEM((A_W, D_MODEL), f32), pltpu.VMEM((DW_STAGE_ROWS, D_MODEL), bf16),
                        pltpu.SemaphoreType.DMA],
        compiler_params=_params(("arbitrary",), VMEM_LIMIT_BIG),
    )(x2, *pieces)


def _adam_call(recv, w, m, v, tc, name):
    rows, cols = w.shape

    def body(r_ref, w_ref, m_ref, v_ref, g_ref, d_ref, mo_ref, vo_ref):
        g = r_ref[0].astype(f32)
        for p in range(1, N_DEV):
            g = g + r_ref[p].astype(f32)
        mn = ADAM_B1 * m_ref[...] + (1.0 - ADAM_B1) * g
        vn = ADAM_B2 * v_ref[...] + (1.0 - ADAM_B2) * (g * g)
        m_hat = mn / (1.0 - ADAM_B1 ** ADAM_STEP)
        v_hat = vn / (1.0 - ADAM_B2 ** ADAM_STEP)
        g_ref[...] = g
        d_ref[...] = -ADAM_LR * (m_hat / (jnp.sqrt(v_hat) + ADAM_EPS) + ADAM_WD * w_ref[...])
        mo_ref[...] = mn
        vo_ref[...] = vn

    blk = pl.BlockSpec((rows, tc), lambda i: (0, i))
    return pl.pallas_call(
        body,
        name=name,
        grid=(cols // tc,),
        in_specs=[pl.BlockSpec((N_DEV, rows, tc), lambda i: (0, 0, i)), blk, blk, blk],
        out_specs=[blk] * 4,
        out_shape=[_sds((rows, cols), f32)] * 4,
        compiler_params=_params(("arbitrary",)),
    )(recv, w, m, v)


def _pad_cols(a, width=128):
    return jnp.pad(a, ((0, 0), (0, width - a.shape[1])))


def _pack_small(ln_g, ln_b, rel, b_f, sink):
    return jnp.concatenate([
        ln_g.reshape(8, 128), ln_b.reshape(8, 128), _pad_cols(rel),
        jnp.pad(_pad_cols(b_f), ((0, 7), (0, 0))), jnp.pad(_pad_cols(sink), ((0, 7), (0, 0)))], axis=0)


def _unpack_small(p):
    return (p[0:8].reshape(1, D_MODEL), p[8:16].reshape(1, D_MODEL), p[16:48, 0:8], p[48:49, 0:8], p[56:57, 0:8])


def kernel(x, w_in, b_f, rel_bias, sink, w_o, ln_g, ln_b, loss_target, m_w_in, m_b_f, m_rel_bias, m_sink, m_w_o, m_ln_g, m_ln_b, v_w_in, v_b_f, v_rel_bias, v_sink, v_w_o, v_ln_g, v_ln_b):
    x2 = x[0]
    tgt = loss_target[0]
    s_len = x2.shape[0]
    shard = w_in.shape[2]

    w_in_t = jnp.transpose(w_in[0])
    g_in, g_o = _gather_call([w_in_t.astype(bf16), w_o[0].astype(bf16)])
    wt_full = g_in.reshape(N_DEV * shard, D_MODEL)
    w_t = jnp.concatenate([wt_full[:O_FF0], wt_full[O_FF1:], wt_full[O_FF0:O_FF1],
                           jnp.zeros((A_W - D_IN, D_MODEL), bf16)], axis=0)
    wo_full = g_o.reshape(D_MODEL, D_MODEL)

    qft, kft, vf, fz, qst, ks, vs, sz, fft, vat, kst, vsta = _proj_call(x2, w_t, 512)
    cum, sgm = _cum_call(fft, b_f.reshape(FOX_HEADS, 1))
    qat, ka, kat, tile_stats = _augment_call(qft, kft, cum.reshape(FOX_HEADS, 1, s_len), 2048)
    npairs, pair_q, pair_k = _fox_prune_tables(tile_stats)
    o_ft, lse_f = _fox_fwd_call(qat, ka, vat, npairs, pair_q, pair_k)
    bucket_t = jnp.asarray(_t5_bucket_table().T)
    bias_t, bias0_t = _swa_bias_call(rel_bias, bucket_t)
    sink_v = sink.reshape(SWA_HEADS)
    o_st, lse_s = _swa_fwd_call(qst, ks, vsta, bias_t, bias0_t, sink_v)

    (dh, do_f, dfz, do_s, dsz, dl_f, dl_s, dwo, dg, db, loss_part) = _post_call(
        o_ft.reshape(FOX_HEADS * HEAD_DIM, s_len), fz, o_st.reshape(SWA_HEADS * HEAD_DIM, s_len), sz, x2, tgt,
        wo_full, ln_g, ln_b, jnp.asarray(_head_selector()).astype(bf16), 512)

    dqf, dkf, dvf, dcq, dck = _fox_bwd_call(ka, kat, vf, qat, do_f.reshape(FOX_HEADS, HEAD_DIM, s_len), lse_f,
                                            dl_f.reshape(FOX_HEADS, 1, s_len), npairs, pair_q, pair_k)
    dqf, dkf, dvf = (a.reshape(FOX_HEADS * HEAD_DIM, s_len) for a in (dqf, dkf, dvf))
    dfft, dbf = _cum_bwd_call(dcq.reshape(FOX_HEADS, s_len), dck.reshape(FOX_HEADS, s_len), sgm)
    dqs, dks, dvs, dbias, dsink = _swa_bwd_call(
        qst, ks, kst, vs, do_s.reshape(SWA_HEADS, HEAD_DIM, s_len), lse_s, dl_s.reshape(SWA_HEADS, 1, s_len),
        bias_t, bias0_t, sink_v)
    dqs = dqs.reshape(SWA_HEADS * HEAD_DIM, s_len)
    dks, dvs = (a.reshape(SWA_KV_HEADS * HEAD_DIM, s_len) for a in (dks, dvs))
    drel = _swa_bias_bwd_call(dbias, bucket_t)

    pieces = (dqf, dkf, dvf, dfz, dqs, dks, dvs, dsz, dfft)
    dw_t = _dw_call(x2, pieces, 1024)

    dwt_full = jnp.concatenate([dw_t[:O_FF0], dw_t[A_FF:A_FF + (O_FF1 - O_FF0)], dw_t[O_FF0:A_FF]], axis=0)
    dw_blocks = dwt_full.reshape(N_DEV, shard, D_MODEL)
    dwo_blocks = dwo.reshape(N_DEV, D_MODEL // N_DEV, D_MODEL).astype(bf16)
    small = _pack_small(dg, db, drel[:, 0:8], dbf[:, 0].reshape(1, 8), dsink[:, 0].reshape(1, 8))
    loss_slot = np.zeros((64, 128), bool)
    loss_slot[49, 0] = True
    small = jnp.where(jnp.asarray(loss_slot), loss_part[0, 0], small)
    small_blocks = jnp.broadcast_to(small[None], (N_DEV,) + small.shape)
    dx, (r_in, r_o, r_small) = _dx_exchange_call(dh, pieces, w_t, [dw_blocks, dwo_blocks, small_blocks], 256)

    win_t = [jnp.transpose(a) for a in _adam_call(
        r_in, w_in_t, jnp.transpose(m_w_in[0]), jnp.transpose(v_w_in[0]), 256, "adam_w_in")]
    g_win, d_win, nm_win, nv_win = win_t
    g_wo, d_wo, nm_wo, nv_wo = _adam_call(r_o, w_o[0], m_w_o[0], v_w_o[0], 256, "adam_w_o")
    p_w = _pack_small(ln_g, ln_b, rel_bias, b_f, sink)
    p_m = _pack_small(m_ln_g, m_ln_b, m_rel_bias, m_b_f, m_sink)
    p_v = _pack_small(v_ln_g, v_ln_b, v_rel_bias, v_b_f, v_sink)
    g_p, d_p, nm_p, nv_p = _adam_call(r_small, p_w, p_m, p_v, 128, "adam_small")

    loss = g_p[49, 0]
    g_lng, g_lnb, g_rel, g_bf, g_sink = _unpack_small(g_p)
    d_lng, d_lnb, d_rel, d_bf, d_sink = _unpack_small(d_p)
    m_lng, m_lnb, m_rel, m_bf, m_sk = _unpack_small(nm_p)
    v_lng, v_lnb, v_rel, v_bf, v_sk = _unpack_small(nv_p)
    return (loss, dx[None], g_win[None], g_bf, g_rel, g_sink, g_wo[None], g_lng, g_lnb,
            d_win[None], d_bf, d_rel, d_sink, d_wo[None], d_lng, d_lnb,
            nm_win[None], m_bf, m_rel, m_sk, nm_wo[None], m_lng, m_lnb,
            nv_win[None], v_bf, v_rel, v_sk, nv_wo[None], v_lng, v_lnb)
```

```python
import functools
import math

import numpy as np
import jax
import jax.numpy as jnp
from jax import lax
from jax.experimental import pallas as pl
from jax.experimental.pallas import tpu as pltpu

f32 = jnp.float32
bf16 = jnp.bfloat16

D_MODEL = 1024
HEAD_DIM = 64
FOX_HEADS = 8
SWA_HEADS = 8
SWA_KV_HEADS = 2
SWA_GROUP = 4
BLOCK = 128
NUM_BUCKETS = 32
MAX_DISTANCE = 128
LN_EPS = 1e-5
NEG_INF = -1e30
ALPHA = 2.0 ** 0.25
SCALE = 1.0 / math.sqrt(HEAD_DIM)
D_IN = 3336

ADAM_LR = 0.001
ADAM_B1 = 0.9
ADAM_B2 = 0.999
ADAM_EPS = 1e-08
ADAM_WD = 0.01
ADAM_STEP = 10

N_DEV = 8
A_FQ, A_FK, A_FV, A_FZ, A_SQ, A_SK, A_SV, A_SZ, A_FF, A_W = 0, 512, 1024, 1536, 2048, 2560, 2688, 2816, 3328, 3456
O_FF0, O_FF1 = 1536, 1544

VMEM_LIMIT = 48 * 1024 * 1024
HIGHEST = lax.Precision.HIGHEST
NT = (((1,), (1,)), ((), ()))
TN = (((0,), (0,)), ((), ()))
MESH = pl.DeviceIdType.MESH
RELS = [(0, 0, 1), (0, 1, 0), (0, 1, 1), (1, 0, 0), (1, 0, 1), (1, 1, 0), (1, 1, 1)]


VMEM_LIMIT_BIG = 60 * 1024 * 1024


def _params(sem=None, vmem=VMEM_LIMIT):
    return pltpu.CompilerParams(dimension_semantics=sem, vmem_limit_bytes=vmem)


def _sds(shape, dtype):
    return jax.ShapeDtypeStruct(shape, dtype)


def _t5_bucket_table():
    qi = np.arange(BLOCK)[:, None]
    kj = np.arange(2 * BLOCK)[None, :]
    rel = qi + BLOCK - kj
    band = (rel >= 0) & (rel < BLOCK)
    relc = np.maximum(rel, 0)
    max_exact = NUM_BUCKETS // 2
    relf = np.maximum(relc, 1).astype(np.float32)
    large = max_exact + (np.log(relf / np.float32(max_exact)) / np.float32(math.log(MAX_DISTANCE / max_exact))
                         * np.float32(NUM_BUCKETS - max_exact)).astype(np.int32)
    large = np.minimum(large, NUM_BUCKETS - 1)
    bucket = np.where(relc < max_exact, relc, large).astype(np.int32)
    bucket = np.where(band, bucket, -1).astype(np.int32)
    return bucket


def _mesh_pos():
    return lax.axis_index("x"), lax.axis_index("y"), lax.axis_index("c")


def _dev_index(p):
    return 4 * p[0] + 2 * p[1] + p[2]


def _gather_call(xs):
    n = len(xs)

    def body(*refs):
        x_refs, o_refs = refs[:n], refs[n:2 * n]
        send_sems, recv_sems, local_sems = refs[2 * n:]
        x, y, c = _mesh_pos()
        me, sib = (x, y, c), (x, y, 1 - c)
        chips = [(1 - x, y), (x, 1 - y), (1 - x, 1 - y)]

        def copy(a, k, block, to, src=None):
            slot = o_refs[a].at[_dev_index(block)]
            return pltpu.make_async_remote_copy(
                src_ref=slot if src is None else src, dst_ref=slot,
                send_sem=send_sems.at[a * 7 + k], recv_sem=recv_sems.at[a * 7 + k],
                device_id=to, device_id_type=MESH)

        mine = [pltpu.make_async_copy(x_refs[a], o_refs[a].at[_dev_index(me)], local_sems.at[a]) for a in range(n)]
        for cp in mine:
            cp.start()
        first = []
        for a in range(n):
            first.append(copy(a, 0, me, sib, src=x_refs[a]))
            first += [copy(a, 1 + j, me, (*chip, c), src=x_refs[a]) for j, chip in enumerate(chips)]
        for cp in first:
            cp.start()
        passed = []
        for j, chip in enumerate(chips):
            for a in range(n):
                copy(a, 1 + j, (*chip, c), me).wait_recv()
                fwd = copy(a, 4 + j, (*chip, c), sib)
                fwd.start()
                passed.append(fwd)
        for a in range(n):
            copy(a, 0, sib, me).wait_recv()
            for j, chip in enumerate(chips):
                copy(a, 4 + j, (*chip, 1 - c), me).wait_recv()
        for cp in first + passed:
            cp.wait_send()
        for cp in mine:
            cp.wait()

    any_spec = pl.BlockSpec(memory_space=pl.ANY)
    return pl.pallas_call(
        body,
        name="gather_weights",
        out_shape=[_sds((N_DEV,) + a.shape, a.dtype) for a in xs],
        in_specs=[any_spec] * n,
        out_specs=[any_spec] * n,
        scratch_shapes=[pltpu.SemaphoreType.DMA((7 * n,)), pltpu.SemaphoreType.DMA((7 * n,)),
                        pltpu.SemaphoreType.DMA((n,))],
    )(*xs)


def _exchange_copies(b_refs, r_refs, send_sems, recv_sems, local_sems, incoming):
    n = len(b_refs)
    x, y, c = _mesh_pos()
    me_idx = _dev_index((x, y, c))
    mine = [pltpu.make_async_copy(b_refs[a].at[me_idx], r_refs[a].at[me_idx], local_sems.at[a]) for a in range(n)]
    remote = []
    for k, r in enumerate(RELS):
        peer = ((1 - x) if r[0] else x, (1 - y) if r[1] else y, (1 - c) if r[2] else c)
        pidx = _dev_index(peer)
        for a in range(n):
            remote.append(pltpu.make_async_remote_copy(
                src_ref=b_refs[a].at[pidx], dst_ref=r_refs[a].at[pidx if incoming else me_idx],
                send_sem=send_sems.at[a * 7 + k], recv_sem=recv_sems.at[a * 7 + k],
                device_id=peer, device_id_type=MESH))
    return mine, remote


def _exchange_start(b_refs, r_refs, sems):
    mine, out = _exchange_copies(b_refs, r_refs, *sems, incoming=False)
    for cp in mine + out:
        cp.start()


def _exchange_wait(b_refs, r_refs, sems):
    mine, inc = _exchange_copies(b_refs, r_refs, *sems, incoming=True)
    for cp in inc:
        cp.wait_recv()
    for cp in inc:
        cp.wait_send()
    for cp in mine:
        cp.wait()


def _proj_call(x2, w_t, tm):
    s_len = x2.shape[0]

    def body(x_ref, w_ref, qft_ref, kft_ref, vf_ref, fz_ref, qst_ref, ks_ref, vs_ref, sz_ref, fft_ref, vat_ref,
             kst_ref, vsta_ref):
        xb = x_ref[...].astype(bf16)

        def seg_t(off, width):
            return lax.dot_general(w_ref[off:off + width, :], xb, NT, preferred_element_type=f32)

        def seg(off, width):
            return lax.dot_general(xb, w_ref[off:off + width, :], NT, preferred_element_type=f32)

        def put_heads(ref, acc, nheads):
            for h in range(nheads):
                ref[h] = acc[:, h * HEAD_DIM:(h + 1) * HEAD_DIM].astype(bf16)

        def put_heads_t(ref, acc_t, nheads, augment):
            for h in range(nheads):
                ref[h, 0:HEAD_DIM, :] = acc_t[h * HEAD_DIM:(h + 1) * HEAD_DIM, :].astype(bf16)
                if augment:
                    ref[h, HEAD_DIM:2 * HEAD_DIM, :] = ones_row

        ones_row = jnp.where(lax.broadcasted_iota(jnp.int32, (HEAD_DIM, tm), 0) == 0, 1.0, 0.0).astype(bf16)
        put_heads_t(vat_ref, seg_t(A_FV, 512), FOX_HEADS, True)
        put_heads_t(qft_ref, seg_t(A_FQ, 512) * SCALE, FOX_HEADS, False)
        put_heads_t(kft_ref, seg_t(A_FK, 512), FOX_HEADS, False)
        put_heads(vf_ref, seg(A_FV, 512), FOX_HEADS)
        fz_ref[...] = seg(A_FZ, 512)
        put_heads_t(qst_ref, seg_t(A_SQ, 512) * SCALE, SWA_HEADS, False)
        put_heads(ks_ref, seg(A_SK, 128), SWA_KV_HEADS)
        put_heads(vs_ref, seg(A_SV, 128), SWA_KV_HEADS)
        put_heads_t(kst_ref, seg_t(A_SK, 128), SWA_KV_HEADS, False)
        put_heads_t(vsta_ref, seg_t(A_SV, 128), SWA_KV_HEADS, True)
        sz_ref[...] = seg(A_SZ, 512)
        fft_ref[...] = seg(A_FF, 128).T[:FOX_HEADS, :]

    def heads(nh):
        return pl.BlockSpec((nh, tm, HEAD_DIM), lambda i: (0, i, 0))

    def feat(nh, rows):
        return pl.BlockSpec((nh, rows, tm), lambda i: (0, 0, i))

    wide = pl.BlockSpec((tm, 512), lambda i: (i, 0))
    return pl.pallas_call(
        body,
        name="proj_fwd",
        grid=(s_len // tm,),
        in_specs=[pl.BlockSpec((tm, D_MODEL), lambda i: (i, 0)), pl.BlockSpec((A_W, D_MODEL), lambda i: (0, 0))],
        out_specs=[feat(8, HEAD_DIM), feat(8, HEAD_DIM), heads(8), wide, feat(8, HEAD_DIM), heads(2), heads(2), wide,
                   pl.BlockSpec((FOX_HEADS, tm), lambda i: (0, i)),
                   feat(FOX_HEADS, 2 * HEAD_DIM), feat(2, HEAD_DIM), feat(2, 2 * HEAD_DIM)],
        out_shape=[_sds((8, HEAD_DIM, s_len), bf16)] * 2 + [_sds((8, s_len, HEAD_DIM), bf16)]
                  + [_sds((s_len, 512), f32), _sds((8, HEAD_DIM, s_len), bf16),
                     _sds((2, s_len, HEAD_DIM), bf16), _sds((2, s_len, HEAD_DIM), bf16), _sds((s_len, 512), f32),
                     _sds((FOX_HEADS, s_len), f32), _sds((FOX_HEADS, 2 * HEAD_DIM, s_len), bf16),
                     _sds((2, HEAD_DIM, s_len), bf16), _sds((2, 2 * HEAD_DIM, s_len), bf16)],
        compiler_params=_params(("arbitrary",)),
    )(x2, w_t)


AUG = 2 * HEAD_DIM


def _augment_call(q_t, k_t, cum_row, tm):
    nh, _, s_len = k_t.shape
    per_step = tm // FOX_T

    def body(qt_ref, kt_ref, c_ref, qat_ref, ka_ref, kat_ref, st_ref):
        c = c_ref[0]
        hi = c.astype(bf16).astype(f32)
        r1 = c - hi
        mid = r1.astype(bf16).astype(f32)
        lo = (r1 - mid).astype(bf16).astype(f32)
        row = lax.broadcasted_iota(jnp.int32, (HEAD_DIM, tm), 0)
        q_tail = jnp.where(row == 0, hi, jnp.where(row == 1, mid, jnp.where(row == 2, lo,
                           jnp.where(row < 6, 1.0, 0.0))))
        k_tail = jnp.where(row < 3, 1.0, jnp.where(row == 3, -hi, jnp.where(row == 4, -mid,
                           jnp.where(row == 5, -lo, 0.0))))
        qat_ref[0, 0:HEAD_DIM, :] = qt_ref[0]
        qat_ref[0, HEAD_DIM:AUG, :] = q_tail.astype(bf16)
        kat_ref[0, 0:HEAD_DIM, :] = kt_ref[0]
        kat_ref[0, HEAD_DIM:AUG, :] = k_tail.astype(bf16)
        qt = qt_ref[0].astype(f32)
        kt = kt_ref[0].astype(f32)
        ka_ref[0] = jnp.concatenate([kt, k_tail], axis=0).T.astype(bf16)
        qn2 = jnp.sum(qt * qt, axis=0, keepdims=True)
        kn2 = jnp.sum(kt * kt, axis=0, keepdims=True)
        sd = jnp.sum(qt * kt, axis=0, keepdims=True)
        srow = lax.broadcasted_iota(jnp.int32, (8, LANES), 0)
        for part in range(per_step):
            sl = slice(part * FOX_T, (part + 1) * FOX_T)
            vals = [jnp.sqrt(jnp.max(qn2[:, sl], axis=1, keepdims=True)),
                    jnp.sqrt(jnp.max(kn2[:, sl], axis=1, keepdims=True)),
                    jnp.min(sd[:, sl], axis=1, keepdims=True),
                    jnp.max(c[:, sl], axis=1, keepdims=True), jnp.min(c[:, sl], axis=1, keepdims=True)]
            out = jnp.zeros((8, LANES), f32)
            for r, val in enumerate(vals):
                out = jnp.where(srow == r, val, out)
            st_ref[0, part] = out

    tile_t = pl.BlockSpec((1, HEAD_DIM, tm), lambda h, i: (h, 0, i))
    return pl.pallas_call(
        body,
        name="fox_augment",
        grid=(nh, s_len // tm),
        in_specs=[tile_t, tile_t, pl.BlockSpec((1, 1, tm), lambda h, i: (h, 0, i))],
        out_specs=[pl.BlockSpec((1, AUG, tm), lambda h, i: (h, 0, i)),
                   pl.BlockSpec((1, tm, AUG), lambda h, i: (h, i, 0)),
                   pl.BlockSpec((1, AUG, tm), lambda h, i: (h, 0, i)),
                   pl.BlockSpec((1, per_step, 8, LANES), lambda h, i: (h, i, 0, 0))],
        out_shape=[_sds((nh, AUG, s_len), bf16), _sds((nh, s_len, AUG), bf16), _sds((nh, AUG, s_len), bf16),
                   _sds((nh, s_len // FOX_T, 8, LANES), f32)],
        compiler_params=_params(("arbitrary", "arbitrary")),
    )(q_t, k_t, cum_row)


FOX_PRUNE_GAP = 32.0


def _fox_prune_tables(stats):
    s = stats[:, :, :, 0]
    qn, kn, sd, cmx, cmn = (s[:, :, r] for r in range(5))
    nt = s.shape[1]
    bound = qn[:, :, None] * kn[:, None, :] + (cmx[:, :, None] - cmn[:, None, :])
    margin = 0.01 + 1e-5 * (jnp.abs(cmx)[:, :, None] + jnp.abs(cmn)[:, None, :])
    qi = lax.broadcasted_iota(jnp.int32, (nt, nt), 0)
    kj = lax.broadcasted_iota(jnp.int32, (nt, nt), 1)
    skip = (bound + margin < sd[:, :, None] - FOX_PRUNE_GAP) & (kj < qi)[None]
    first = jnp.sum(jnp.cumprod(skip.astype(jnp.int32), axis=2), axis=2)
    tiles = lax.broadcasted_iota(jnp.int32, (1, nt), 1)
    cnt = tiles - first
    ends = jnp.cumsum(cnt, axis=1)
    off = ends - cnt
    kmax = nt * (nt - 1) // 2
    k = lax.broadcasted_iota(jnp.int32, (1, kmax), 1)
    pair_q = jnp.minimum(jnp.sum((ends[:, None, :] <= k[:, :, None]).astype(jnp.int32), axis=2), nt - 1)
    hit = pair_q[:, :, None] == tiles[:, None, :]
    first_k = jnp.sum(jnp.where(hit, first[:, None, :], 0), axis=2)
    off_k = jnp.sum(jnp.where(hit, off[:, None, :], 0), axis=2)
    pair_k = jnp.clip(first_k + k - off_k, 0, nt - 1)
    return (ends[:, nt - 1].astype(jnp.int32), pair_q.reshape(-1).astype(jnp.int32),
            pair_k.reshape(-1).astype(jnp.int32))


CUM_CHUNK = 512


def _cum_call(fft, bf_col):
    s_len = fft.shape[1]
    ch = CUM_CHUNK

    def body(f_ref, b_ref, cum_ref, sg_ref):
        r = lax.broadcasted_iota(jnp.int32, (ch, ch), 0)
        c = lax.broadcasted_iota(jnp.int32, (ch, ch), 1)
        upper = (r <= c).astype(f32)
        carry = jnp.zeros((FOX_HEADS, 1), f32)
        for n in range(s_len // ch):
            z = f_ref[:, n * ch:(n + 1) * ch] + b_ref[...]
            logf = jnp.minimum(z, 0.0) - jnp.log1p(jnp.exp(-jnp.abs(z)))
            sg_ref[:, n * ch:(n + 1) * ch] = 1.0 / (1.0 + jnp.exp(z))
            cs = jnp.dot(logf, upper, precision=HIGHEST, preferred_element_type=f32) + carry
            cum_ref[:, n * ch:(n + 1) * ch] = cs
            carry = cs[:, ch - 1:ch]

    return pl.pallas_call(
        body,
        name="fox_cum_fwd",
        out_shape=[_sds((FOX_HEADS, s_len), f32)] * 2,
        compiler_params=_params(),
    )(fft, bf_col)


def _cum_bwd_call(dcq, dck, sg):
    s_len = sg.shape[1]
    ch = CUM_CHUNK
    nch = s_len // ch

    def body(q_ref, k_ref, sg_ref, dff_ref, dbf_ref):
        r = lax.broadcasted_iota(jnp.int32, (ch, ch), 0)
        c = lax.broadcasted_iota(jnp.int32, (ch, ch), 1)
        lower = (r >= c).astype(f32)
        dff_ref[...] = jnp.zeros_like(dff_ref)
        carry = jnp.zeros((FOX_HEADS, 1), f32)
        total = jnp.zeros((FOX_HEADS, 1), f32)
        for n in reversed(range(nch)):
            sl = slice(n * ch, (n + 1) * ch)
            dcum = q_ref[:, sl] - k_ref[:, sl]
            rs = jnp.dot(dcum, lower, precision=HIGHEST, preferred_element_type=f32) + carry
            carry = rs[:, 0:1]
            dff = rs * sg_ref[:, sl]
            dff_ref[0:FOX_HEADS, sl] = dff
            total = total + jnp.sum(dff, axis=1, keepdims=True)
        dbf_ref[...] = jnp.broadcast_to(total, (FOX_HEADS, 128))

    return pl.pallas_call(
        body,
        name="fox_cum_bwd",
        out_shape=[_sds((128, s_len), f32), _sds((FOX_HEADS, 128), f32)],
        compiler_params=_params(),
    )(dcq, dck, sg)


FOX_T = 512
LANES = 128


def _causal_keep(t):
    return lax.broadcasted_iota(jnp.int32, (t, t), 0) <= lax.broadcasted_iota(jnp.int32, (t, t), 1)


def _tile_cols(i, t):
    return pl.ds(pl.multiple_of(i * t, t), t)


def _fox_pair(n, nt, kmax, h, pq_ref, pk_ref):
    k = h * kmax + jnp.maximum(n - nt, 0)
    return jnp.where(n < nt, n, pq_ref[k]), jnp.where(n < nt, n, pk_ref[k])


def _fox_fwd_call(qat, ka, vat, npairs, pair_q, pair_k):
    nh, s_len, _ = ka.shape
    t = FOX_T
    nt = s_len // t
    kmax = nt * (nt - 1) // 2
    assert nt >= 2 and nt % 2 == 0

    def body(np_ref, pq_ref, pk_ref, qat_ref, ka_ref, vat_ref, o_ref, lse_ref, s0, s1, p0, p1, a0, a1, m_all, acc_all):
        h = pl.program_id(0)
        extra = np_ref[h]
        total = nt + extra
        m_all[...] = jnp.full(m_all.shape, NEG_INF, f32)
        acc_all[...] = jnp.zeros(acc_all.shape, f32)
        bufs = ((s0, p0, a0), (s1, p1, a1))

        def pair(n):
            return _fox_pair(n, nt, kmax, h, pq_ref, pk_ref)

        def scores(n, b, masked):
            i, j = pair(n)
            st = jnp.dot(ka_ref[0, _tile_cols(j, t), :], qat_ref[0, :, _tile_cols(i, t)], preferred_element_type=f32)
            if masked:
                st = jnp.where(_causal_keep(t), st, NEG_INF)
            bufs[b][0][...] = st

        def softmax(n, b):
            i, _ = pair(n)
            s_ref, p_ref, a_ref = bufs[b]
            for c in range(t // LANES):
                cols = slice(c * LANES, (c + 1) * LANES)
                mcols = pl.ds(pl.multiple_of(i * t + c * LANES, LANES), LANES)
                m_old = m_all[:, mcols]
                m_new = jnp.maximum(m_old, jnp.max(s_ref[:, cols], axis=0, keepdims=True))
                m_all[:, mcols] = m_new
                a_ref[:, cols] = jnp.exp(m_old - m_new)
                p_ref[:, cols] = jnp.exp(s_ref[:, cols] - m_new).astype(bf16)

        def accum(n, b):
            i, j = pair(n)
            cols = _tile_cols(i, t)
            acc_all[:, cols] = bufs[b][2][...] * acc_all[:, cols] + jnp.dot(
                vat_ref[0, :, _tile_cols(j, t)], bufs[b][1][...], preferred_element_type=f32)

        def step(n, b, masked):
            accum(n - 2, b)
            softmax(n - 1, 1 - b)
            scores(n, b, masked)

        scores(0, 0, True)
        scores(1, 1, True)
        softmax(0, 0)

        def diag_steps(d, _):
            n = 2 + 2 * d
            step(n, 0, True)
            step(n + 1, 1, True)
            return 0

        lax.fori_loop(0, (nt - 2) // 2, diag_steps, 0)

        def off_steps(d, _):
            n = nt + 2 * d
            step(n, 0, False)
            step(n + 1, 1, False)
            return 0

        lax.fori_loop(0, extra // 2, off_steps, 0)

        @pl.when(extra % 2 == 1)
        def _():
            step(total - 1, 0, False)
            softmax(total - 1, 0)
            accum(total - 2, 1)
            accum(total - 1, 0)

        @pl.when(extra % 2 == 0)
        def _():
            softmax(total - 1, 1)
            accum(total - 2, 0)
            accum(total - 1, 1)

        l = acc_all[HEAD_DIM:HEAD_DIM + 1, :]
        o_ref[0] = acc_all[0:HEAD_DIM, :] / l
        lse_ref[0] = m_all[...] + jnp.log(l)

    smem = pl.BlockSpec(memory_space=pltpu.SMEM)
    return pl.pallas_call(
        body,
        name="fox_fwd",
        grid=(nh,),
        in_specs=[smem, smem, smem,
                  pl.BlockSpec((1, AUG, s_len), lambda h: (h, 0, 0)),
                  pl.BlockSpec((1, s_len, AUG), lambda h: (h, 0, 0)),
                  pl.BlockSpec((1, AUG, s_len), lambda h: (h, 0, 0))],
        out_specs=[pl.BlockSpec((1, HEAD_DIM, s_len), lambda h: (h, 0, 0)),
                   pl.BlockSpec((1, 1, s_len), lambda h: (h, 0, 0))],
        out_shape=[_sds((nh, HEAD_DIM, s_len), f32), _sds((nh, 1, s_len), f32)],
        scratch_shapes=[pltpu.VMEM((t, t), f32), pltpu.VMEM((t, t), f32), pltpu.VMEM((t, t), bf16),
                        pltpu.VMEM((t, t), bf16), pltpu.VMEM((1, t), f32), pltpu.VMEM((1, t), f32),
                        pltpu.VMEM((1, s_len), f32), pltpu.VMEM((AUG, s_len), f32)],
        compiler_params=_params(("arbitrary",)),
    )(npairs, pair_q, pair_k, qat, ka, vat)


SWA_TS = 512


SWA_W = SWA_GROUP * BLOCK


def _swa_bias_call(rel_bias, bucket_t):
    def body(rb_ref, bk_ref, b_ref, b0_ref):
        bk = bk_ref[...]
        row = lax.broadcasted_iota(jnp.int32, (2 * BLOCK, BLOCK), 0)
        for h in range(SWA_HEADS):
            acc = jnp.full((2 * BLOCK, BLOCK), NEG_INF, f32)
            for b in range(NUM_BUCKETS):
                acc = jnp.where(bk == b, rb_ref[b, h], acc)
            g, hh = divmod(h, SWA_GROUP)
            b_ref[g, :, hh * BLOCK:(hh + 1) * BLOCK] = acc
            b0_ref[g, :, hh * BLOCK:(hh + 1) * BLOCK] = jnp.where(row < BLOCK, NEG_INF, acc)

    return pl.pallas_call(
        body,
        name="swa_bias",
        in_specs=[pl.BlockSpec(memory_space=pltpu.SMEM), pl.BlockSpec(memory_space=pltpu.VMEM)],
        out_shape=[_sds((SWA_KV_HEADS, 2 * BLOCK, SWA_W), f32)] * 2,
        compiler_params=_params(),
    )(rel_bias, bucket_t)


def _swa_bias_bwd_call(dbias, bucket_t):
    def body(d_ref, bk_ref, o_ref):
        bk = bk_ref[...]
        row = lax.broadcasted_iota(jnp.int32, (NUM_BUCKETS, 128), 0)
        col = lax.broadcasted_iota(jnp.int32, (NUM_BUCKETS, 128), 1)
        out = jnp.zeros((NUM_BUCKETS, 128), f32)
        for h in range(SWA_HEADS):
            g, hh = divmod(h, SWA_GROUP)
            d = d_ref[g, :, hh * BLOCK:(hh + 1) * BLOCK]
            for b in range(NUM_BUCKETS):
                val = jnp.sum(jnp.sum(jnp.where(bk == b, d, 0.0), axis=1, keepdims=True), axis=0, keepdims=True)
                out = jnp.where((row == b) & (col == h), val, out)
        o_ref[...] = out

    return pl.pallas_call(
        body,
        name="swa_bias_bwd",
        out_shape=_sds((NUM_BUCKETS, 128), f32),
        compiler_params=_params(),
    )(dbias, bucket_t)


def _sink_row(sink_ref, g):
    return jnp.concatenate([jnp.full((1, BLOCK), sink_ref[g * SWA_GROUP + hh], f32) for hh in range(SWA_GROUP)], axis=1)


def _group_lanes(ref, g, cols):
    return jnp.concatenate([ref[g * SWA_GROUP + hh, :, cols] for hh in range(SWA_GROUP)], axis=1)


def _swa_fwd_call(qt, k, vta, bias_t, bias0_t, sink):
    s_len = qt.shape[2]
    ts = SWA_TS
    nb = ts // BLOCK

    def body(qt_ref, kc_ref, kp_ref, vc_ref, vp_ref, b_ref, b0_ref, sink_ref, o_ref, lse_ref):
        first = pl.program_id(0) == 0
        kall = [jnp.concatenate([kp_ref[g], kc_ref[g]], axis=0) for g in range(SWA_KV_HEADS)]
        vall = [jnp.concatenate([vp_ref[g], vc_ref[g]], axis=1) for g in range(SWA_KV_HEADS)]
        sinks = [_sink_row(sink_ref, g) for g in range(SWA_KV_HEADS)]
        items = [(g, b) for g in range(SWA_KV_HEADS) for b in range(nb)]

        def scores(g, b):
            qg = _group_lanes(qt_ref, g, slice(b * BLOCK, (b + 1) * BLOCK))
            bias_b = b_ref[g]
            if b == 0:
                bias_b = jnp.where(first, b0_ref[g], bias_b)
            return jnp.dot(kall[g][b * BLOCK:(b + 2) * BLOCK], qg, preferred_element_type=f32) + bias_b

        def finish(g, b, st):
            m = jnp.maximum(jnp.max(st, axis=0, keepdims=True), sinks[g])
            pt = jnp.exp(st - m)
            acc = jnp.dot(vall[g][:, b * BLOCK:(b + 2) * BLOCK], pt.astype(bf16), preferred_element_type=f32)
            l = acc[HEAD_DIM:HEAD_DIM + 1, :] + jnp.exp(sinks[g] - m)
            return acc[0:HEAD_DIM, :] / l, m + jnp.log(l)

        outs, lses = {}, {}
        st_next = scores(*items[0])
        for idx, (g, b) in enumerate(items):
            st = st_next
            if idx + 1 < len(items):
                st_next = scores(*items[idx + 1])
            outs[g, b], lses[g, b] = finish(g, b, st)
        for g in range(SWA_KV_HEADS):
            for hh in range(SWA_GROUP):
                lanes = slice(hh * BLOCK, (hh + 1) * BLOCK)
                o_ref[g * SWA_GROUP + hh] = jnp.concatenate([outs[g, b][:, lanes] for b in range(nb)], axis=1)
                lse_ref[g * SWA_GROUP + hh] = jnp.concatenate([lses[g, b][:, lanes] for b in range(nb)], axis=1)

    def prev_blk(n):
        return jnp.maximum(n * nb - 1, 0)

    bspec = pl.BlockSpec((SWA_KV_HEADS, 2 * BLOCK, SWA_W), lambda n: (0, 0, 0))
    return pl.pallas_call(
        body,
        name="swa_fwd",
        grid=(s_len // ts,),
        in_specs=[pl.BlockSpec((SWA_HEADS, HEAD_DIM, ts), lambda n: (0, 0, n)),
                  pl.BlockSpec((SWA_KV_HEADS, ts, HEAD_DIM), lambda n: (0, n, 0)),
                  pl.BlockSpec((SWA_KV_HEADS, BLOCK, HEAD_DIM), lambda n: (0, prev_blk(n), 0)),
                  pl.BlockSpec((SWA_KV_HEADS, AUG, ts), lambda n: (0, 0, n)),
                  pl.BlockSpec((SWA_KV_HEADS, AUG, BLOCK), lambda n: (0, 0, prev_blk(n))),
                  bspec, bspec, pl.BlockSpec(memory_space=pltpu.SMEM)],
        out_specs=[pl.BlockSpec((SWA_HEADS, HEAD_DIM, ts), lambda n: (0, 0, n)),
                   pl.BlockSpec((SWA_HEADS, 1, ts), lambda n: (0, 0, n))],
        out_shape=[_sds((SWA_HEADS, HEAD_DIM, s_len), f32), _sds((SWA_HEADS, 1, s_len), f32)],
        compiler_params=_params(("arbitrary",)),
    )(qt, k, k, vta, vta, bias_t, bias0_t, sink)


def _head_selector():
    sel = np.zeros((512, 128), np.float32)
    for h in range(8):
        sel[h * HEAD_DIM:(h + 1) * HEAD_DIM, h] = 1.0
    return sel


def _post_call(of, fz, osw, sz, x2, tgt, wo, ln_g, ln_b, sel, tm):
    s_len = x2.shape[0]

    def body(of_ref, fz_ref, os_ref, sz_ref, x_ref, t_ref, wo_ref, g_ref, b_ref, sel_ref,
             dh_ref, dof_ref, dfz_ref, dos_ref, dsz_ref, dlf_ref, dls_ref, dwo_ref, dg_ref, db_ref, loss_ref):
        n = pl.program_id(0)

        @pl.when(n == 0)
        def _():
            dwo_ref[...] = jnp.zeros_like(dwo_ref)
            dg_ref[...] = jnp.zeros_like(dg_ref)
            db_ref[...] = jnp.zeros_like(db_ref)
            loss_ref[...] = jnp.zeros_like(loss_ref)

        gam = g_ref[...]
        sel_m = sel_ref[...]

        def forward(r):
            o_f = of_ref[:, r].T
            o_s = os_ref[:, r].T
            fz = fz_ref[r, :]
            sz = sz_ref[r, :]
            sg_f = jax.nn.sigmoid(fz)
            sg_s = jax.nn.sigmoid(sz)
            silu_f = fz * sg_f
            silu_s = sz * sg_s
            mixed = jnp.concatenate([o_f * silu_f, o_s * silu_s], axis=1).astype(bf16)
            y = jnp.dot(mixed, wo_ref[...], preferred_element_type=f32)
            return o_f, o_s, fz, sz, sg_f, sg_s, silu_f, silu_s, mixed, y

        def norm_and_back(r, fwd):
            mixed, y = fwd[8], fwd[9]
            h = ALPHA * x_ref[r, :] + y
            mu = jnp.mean(h, axis=1, keepdims=True)
            hc = h - mu
            var = jnp.mean(hc * hc, axis=1, keepdims=True)
            rstd = lax.rsqrt(var + LN_EPS)
            xhat = hc * rstd
            out = xhat * gam + b_ref[...]
            err = out - t_ref[r, :]
            tok_loss = jnp.mean(err * err, axis=1, keepdims=True)
            loss_ref[...] += 0.5 * jnp.sum(tok_loss, axis=0, keepdims=True)
            dout = err * (1.0 / D_MODEL)
            dg_ref[...] += jnp.sum(dout * xhat, axis=0, keepdims=True)
            db_ref[...] += jnp.sum(dout, axis=0, keepdims=True)
            dxh = dout * gam
            m1 = jnp.mean(dxh, axis=1, keepdims=True)
            m2 = jnp.mean(dxh * xhat, axis=1, keepdims=True)
            dh = rstd * (dxh - m1 - xhat * m2)
            dh_ref[r, :] = dh
            dyb = dh.astype(bf16)
            dmix = lax.dot_general(dyb, wo_ref[...], NT, preferred_element_type=f32)
            dwo_ref[...] += lax.dot_general(mixed, dyb, TN, preferred_element_type=f32)
            return dmix

        def head_sums(prod):
            hi = prod.astype(bf16)
            lo = (prod - hi.astype(f32)).astype(bf16)
            return (jnp.dot(hi, sel_m, preferred_element_type=f32) + jnp.dot(lo, sel_m, preferred_element_type=f32))

        def gates_back(r, fwd, dmix):
            o_f, o_s, fz, sz, sg_f, sg_s, silu_f, silu_s = fwd[:8]
            dm_f = dmix[:, :512]
            dm_s = dmix[:, 512:]
            do_f = dm_f * silu_f
            do_s = dm_s * silu_s
            dfz_ref[r, :] = (dm_f * o_f * (sg_f * (1.0 + fz * (1.0 - sg_f)))).astype(bf16)
            dsz_ref[r, :] = (dm_s * o_s * (sg_s * (1.0 + sz * (1.0 - sg_s)))).astype(bf16)
            dof_ref[:, r] = do_f.T.astype(bf16)
            dos_ref[:, r] = do_s.T.astype(bf16)
            dlf_ref[:, r] = head_sums(do_f * o_f).T[:FOX_HEADS, :]
            dls_ref[:, r] = head_sums(do_s * o_s).T[:SWA_HEADS, :]

        halves = [slice(k * (tm // 2), (k + 1) * (tm // 2)) for k in range(2)]
        fwds = [forward(r) for r in halves]
        dmixes = [norm_and_back(r, f) for r, f in zip(halves, fwds)]
        for r, f, d in zip(halves, fwds, dmixes):
            gates_back(r, f, d)

    feat = pl.BlockSpec((512, tm), lambda n: (0, n))
    rows8 = pl.BlockSpec((8, tm), lambda n: (0, n))
    half = pl.BlockSpec((tm, 512), lambda n: (n, 0))
    fullw = pl.BlockSpec((tm, D_MODEL), lambda n: (n, 0))
    vec = pl.BlockSpec((1, D_MODEL), lambda n: (0, 0))
    return pl.pallas_call(
        body,
        name="post_fwd_bwd",
        grid=(s_len // tm,),
        in_specs=[feat, half, feat, half, fullw, fullw,
                  pl.BlockSpec((D_MODEL, D_MODEL), lambda n: (0, 0)), vec, vec,
                  pl.BlockSpec((512, 128), lambda n: (0, 0))],
        out_specs=[fullw, feat, half, feat, half, rows8, rows8,
                   pl.BlockSpec((D_MODEL, D_MODEL), lambda n: (0, 0)), vec, vec,
                   pl.BlockSpec((1, 1), lambda n: (0, 0))],
        out_shape=[_sds((s_len, D_MODEL), f32), _sds((512, s_len), bf16), _sds((s_len, 512), bf16),
                   _sds((512, s_len), bf16), _sds((s_len, 512), bf16),
                   _sds((FOX_HEADS, s_len), f32), _sds((SWA_HEADS, s_len), f32),
                   _sds((D_MODEL, D_MODEL), f32), _sds((1, D_MODEL), f32), _sds((1, D_MODEL), f32),
                   _sds((1, 1), f32)],
        compiler_params=_params(("arbitrary",), VMEM_LIMIT_BIG),
    )(of, fz, osw, sz, x2, tgt, wo, ln_g, ln_b, sel)


def _fox_bwd_call(ka, kat, v, qat, dot, lse_row, dl_row, npairs, pair_q, pair_k):
    nh, s_len, _ = ka.shape
    t = FOX_T
    nt = s_len // t
    kmax = nt * (nt - 1) // 2
    assert nt >= 2 and nt % 2 == 0
    ck_slot = HEAD_DIM + 3
    cq_slot = HEAD_DIM

    def body(np_ref, pq_ref, pk_ref, ka_ref, kat_ref, v_ref, qat_ref, dot_ref, lse_ref, dl_ref,
             dq_ref, dk_ref, dv_ref, dcq_ref, dck_ref, dqt_all, dkat_all, dvt_all, p0, p1, ds0, ds1):
        h = pl.program_id(0)
        extra = np_ref[h]
        total = nt + extra
        dqt_all[...] = jnp.zeros(dqt_all.shape, f32)
        dkat_all[...] = jnp.zeros(dkat_all.shape, f32)
        dvt_all[...] = jnp.zeros(dvt_all.shape, f32)
        pbuf, dsbuf = (p0, p1), (ds0, ds1)

        def pair(n):
            return _fox_pair(n, nt, kmax, h, pq_ref, pk_ref)

        def probs(n, b, masked):
            i, j = pair(n)
            qc, kr = _tile_cols(i, t), _tile_cols(j, t)
            st = jnp.dot(ka_ref[0, kr, :], qat_ref[0, :, qc], preferred_element_type=f32)
            dpt = jnp.dot(v_ref[0, kr, :], dot_ref[0, :, qc], preferred_element_type=f32)
            if masked:
                st = jnp.where(_causal_keep(t), st, NEG_INF)
            pt = jnp.exp(st - lse_ref[0, :, qc])
            pbuf[b][...] = pt.astype(bf16)
            dsbuf[b][...] = (pt * (dpt - dl_ref[0, :, qc])).astype(bf16)

        def grads(n, b):
            i, j = pair(n)
            qc, kc = _tile_cols(i, t), _tile_cols(j, t)
            dvt_all[:, kc] += lax.dot_general(dot_ref[0, :, qc], pbuf[b][...], NT, preferred_element_type=f32)
            dkat_all[:, kc] += lax.dot_general(qat_ref[0, :, qc], dsbuf[b][...], NT, preferred_element_type=f32)
            dqt_all[:, qc] += jnp.dot(kat_ref[0, :, kc], dsbuf[b][...], preferred_element_type=f32)

        def step(n, b, masked):
            i, j = pair(n)
            qc, kr = _tile_cols(i, t), _tile_cols(j, t)
            i1, j1 = pair(n - 1)
            qc1, kc1 = _tile_cols(i1, t), _tile_cols(j1, t)
            c = 1 - b
            st = jnp.dot(ka_ref[0, kr, :], qat_ref[0, :, qc], preferred_element_type=f32)
            dvt_all[:, kc1] += lax.dot_general(dot_ref[0, :, qc1], pbuf[c][...], NT, preferred_element_type=f32)
            if masked:
                st = jnp.where(_causal_keep(t), st, NEG_INF)
            pt = jnp.exp(st - lse_ref[0, :, qc])
            pbuf[b][...] = pt.astype(bf16)
            dpt = jnp.dot(v_ref[0, kr, :], dot_ref[0, :, qc], preferred_element_type=f32)
            dkat_all[:, kc1] += lax.dot_general(qat_ref[0, :, qc1], dsbuf[c][...], NT, preferred_element_type=f32)
            dqt_all[:, qc1] += jnp.dot(kat_ref[0, :, kc1], dsbuf[c][...], preferred_element_type=f32)
            dsbuf[b][...] = (pt * (dpt - dl_ref[0, :, qc])).astype(bf16)

        probs(0, 0, True)
        step(1, 1, True)

        def four_steps(n, masked):
            step(n, 0, masked)
            step(n + 1, 1, masked)
            step(n + 2, 0, masked)
            step(n + 3, 1, masked)

        def diag_quads(d, _):
            four_steps(2 + 4 * d, True)
            return 0

        lax.fori_loop(0, (nt - 2) // 4, diag_quads, 0)
        if (nt - 2) % 4:
            step(nt - 2, 0, True)
            step(nt - 1, 1, True)

        def off_quads(d, _):
            four_steps(nt + 4 * d, False)
            return 0

        quads = extra // 4
        lax.fori_loop(0, quads, off_quads, 0)

        def off_steps(d, _):
            n = nt + 4 * quads + 2 * d
            step(n, 0, False)
            step(n + 1, 1, False)
            return 0

        lax.fori_loop(0, (extra % 4) // 2, off_steps, 0)

        @pl.when(extra % 2 == 1)
        def _():
            step(total - 1, 0, False)
            grads(total - 1, 0)

        @pl.when(extra % 2 == 0)
        def _():
            grads(total - 1, 1)

        dq_ref[0] = (dqt_all[0:HEAD_DIM, :] * SCALE).astype(bf16)
        dk_ref[0] = dkat_all[0:HEAD_DIM, :].astype(bf16)
        dv_ref[0] = dvt_all[...].astype(bf16)
        dcq_ref[0] = dqt_all[cq_slot:cq_slot + 1, :]
        dck_ref[0] = dkat_all[ck_slot:ck_slot + 1, :]

    smem = pl.BlockSpec(memory_space=pltpu.SMEM)
    rows = pl.BlockSpec((1, s_len, AUG), lambda h: (h, 0, 0))
    feat = pl.BlockSpec((1, AUG, s_len), lambda h: (h, 0, 0))
    feat64 = pl.BlockSpec((1, HEAD_DIM, s_len), lambda h: (h, 0, 0))
    rowv = pl.BlockSpec((1, 1, s_len), lambda h: (h, 0, 0))
    return pl.pallas_call(
        body,
        name="fox_bwd",
        grid=(nh,),
        in_specs=[smem, smem, smem, rows, feat, pl.BlockSpec((1, s_len, HEAD_DIM), lambda h: (h, 0, 0)), feat, feat64,
                  rowv, rowv],
        out_specs=[feat64, feat64, feat64, rowv, rowv],
        out_shape=[_sds((nh, HEAD_DIM, s_len), bf16)] * 3 + [_sds((nh, 1, s_len), f32)] * 2,
        scratch_shapes=[pltpu.VMEM((AUG, s_len), f32), pltpu.VMEM((AUG, s_len), f32), pltpu.VMEM((HEAD_DIM, s_len), f32)]
                       + [pltpu.VMEM((t, t), bf16)] * 4,
        compiler_params=_params(("arbitrary",)),
    )(npairs, pair_q, pair_k, ka, kat, v, qat, dot, lse_row, dl_row)


def _swa_bwd_call(qt, k, kt, v, dot, lse, dl, bias_t, bias0_t, sink):
    s_len = qt.shape[2]
    ts = SWA_TS
    nb = ts // BLOCK
    nsteps = s_len // ts

    def body(qt_ref, kc_ref, kp_ref, ktc_ref, ktp_ref, vc_ref, vp_ref, dot_ref, lse_ref, dl_ref, b_ref, b0_ref,
             sink_ref, dq_ref, dk_ref, dv_ref, dbias_ref, dsink_ref, dk_s, dv_s, tail_k, tail_v, sk_s):
        n = pl.program_id(0)

        @pl.when(n == 0)
        def _():
            dbias_ref[...] = jnp.zeros_like(dbias_ref)
            sk_s[...] = jnp.zeros_like(sk_s)

        @pl.when(n < nsteps)
        def _():
            first = n == 0
            dk_s[...] = jnp.zeros_like(dk_s)
            dv_s[...] = jnp.zeros_like(dv_s)
            groups = range(SWA_KV_HEADS)
            kall = [jnp.concatenate([kp_ref[g], kc_ref[g]], axis=0) for g in groups]
            vall = [jnp.concatenate([vp_ref[g], vc_ref[g]], axis=0) for g in groups]
            ktall = [jnp.concatenate([ktp_ref[g], ktc_ref[g]], axis=1) for g in groups]
            sinks = [_sink_row(sink_ref, g) for g in groups]
            items = [(g, b) for g in groups for b in range(nb)]

            def products(g, b):
                cols = slice(b * BLOCK, (b + 1) * BLOCK)
                win = slice(b * BLOCK, (b + 2) * BLOCK)
                qg = _group_lanes(qt_ref, g, cols)
                dog = _group_lanes(dot_ref, g, cols)
                bias_b = b_ref[g]
                if b == 0:
                    bias_b = jnp.where(first, b0_ref[g], bias_b)
                st = jnp.dot(kall[g][win], qg, preferred_element_type=f32) + bias_b
                dpt = jnp.dot(vall[g][win], dog, preferred_element_type=f32)
                return qg, dog, st, dpt

            def finish(g, b, qg, dog, st, dpt):
                cols = slice(b * BLOCK, (b + 1) * BLOCK)
                win = slice(b * BLOCK, (b + 2) * BLOCK)
                lse_r = _group_lanes(lse_ref, g, cols)
                dl_r = _group_lanes(dl_ref, g, cols)
                pt = jnp.exp(st - lse_r)
                dst = pt * (dpt - dl_r)
                dsb = dst.astype(bf16)
                dk_s[g, :, win] += lax.dot_general(qg, dsb, NT, preferred_element_type=f32)
                dv_s[g, :, win] += lax.dot_general(dog, pt.astype(bf16), NT, preferred_element_type=f32)
                dqg = jnp.dot(ktall[g][:, win], dsb, preferred_element_type=f32) * SCALE
                return dqg, dst, -jnp.exp(sinks[g] - lse_r) * dl_r

            dqs, dsts, sks = {}, {}, {}
            nxt = products(*items[0])
            for idx, (g, b) in enumerate(items):
                cur = nxt
                if idx + 1 < len(items):
                    nxt = products(*items[idx + 1])
                dqs[g, b], dsts[g, b], sks[g, b] = finish(g, b, *cur)
            for g in groups:
                dbias_ref[g] += functools.reduce(lambda a, c: a + c, [dsts[g, b] for b in range(nb)])
                sk_s[g] += functools.reduce(lambda a, c: a + c, [sks[g, b] for b in range(nb)])
                for hh in range(SWA_GROUP):
                    lanes = slice(hh * BLOCK, (hh + 1) * BLOCK)
                    dq_ref[g * SWA_GROUP + hh] = jnp.concatenate(
                        [dqs[g, b][:, lanes] for b in range(nb)], axis=1).astype(bf16)

        @pl.when(n > 0)
        def _():
            last = slice(ts - BLOCK, ts)
            for g in range(SWA_KV_HEADS):
                add_k = jnp.where(n < nsteps, dk_s[g, :, 0:BLOCK], 0.0)
                add_v = jnp.where(n < nsteps, dv_s[g, :, 0:BLOCK], 0.0)
                dk_ref[g, :, 0:ts - BLOCK] = tail_k[g, :, 0:ts - BLOCK].astype(bf16)
                dv_ref[g, :, 0:ts - BLOCK] = tail_v[g, :, 0:ts - BLOCK].astype(bf16)
                dk_ref[g, :, last] = (tail_k[g, :, last] + add_k).astype(bf16)
                dv_ref[g, :, last] = (tail_v[g, :, last] + add_v).astype(bf16)

        @pl.when(n < nsteps)
        def _():
            tail_k[...] = dk_s[:, :, BLOCK:]
            tail_v[...] = dv_s[:, :, BLOCK:]

        @pl.when(n == nsteps)
        def _():
            row = lax.broadcasted_iota(jnp.int32, (SWA_HEADS, 128), 0)
            out = jnp.zeros((SWA_HEADS, 128), f32)
            for h in range(SWA_HEADS):
                g, hh = divmod(h, SWA_GROUP)
                val = jnp.sum(sk_s[g, :, hh * BLOCK:(hh + 1) * BLOCK], axis=1, keepdims=True)
                out = jnp.where(row == h, val, out)
            dsink_ref[...] = out

    last_step = nsteps - 1

    def cl(n):
        return jnp.minimum(n, last_step)

    def prev_blk(n):
        return jnp.maximum(cl(n) * nb - 1, 0)

    feat8 = pl.BlockSpec((SWA_HEADS, HEAD_DIM, ts), lambda n: (0, 0, cl(n)))
    rows8 = pl.BlockSpec((SWA_HEADS, 1, ts), lambda n: (0, 0, cl(n)))
    cur = pl.BlockSpec((SWA_KV_HEADS, ts, HEAD_DIM), lambda n: (0, cl(n), 0))
    prev = pl.BlockSpec((SWA_KV_HEADS, BLOCK, HEAD_DIM), lambda n: (0, prev_blk(n), 0))
    curt = pl.BlockSpec((SWA_KV_HEADS, HEAD_DIM, ts), lambda n: (0, 0, cl(n)))
    prevt = pl.BlockSpec((SWA_KV_HEADS, HEAD_DIM, BLOCK), lambda n: (0, 0, prev_blk(n)))
    bspec = pl.BlockSpec((SWA_KV_HEADS, 2 * BLOCK, SWA_W), lambda n: (0, 0, 0))
    kvout = pl.BlockSpec((SWA_KV_HEADS, HEAD_DIM, ts), lambda n: (0, 0, jnp.maximum(n - 1, 0)))
    return pl.pallas_call(
        body,
        name="swa_bwd",
        grid=(nsteps + 1,),
        in_specs=[feat8, cur, prev, curt, prevt, cur, prev, feat8, rows8, rows8, bspec, bspec,
                  pl.BlockSpec(memory_space=pltpu.SMEM)],
        out_specs=[feat8, kvout, kvout, bspec, pl.BlockSpec((SWA_HEADS, 128), lambda n: (0, 0))],
        out_shape=[_sds((SWA_HEADS, HEAD_DIM, s_len), bf16), _sds((SWA_KV_HEADS, HEAD_DIM, s_len), bf16),
                   _sds((SWA_KV_HEADS, HEAD_DIM, s_len), bf16),
                   _sds((SWA_KV_HEADS, 2 * BLOCK, SWA_W), f32), _sds((SWA_HEADS, 128), f32)],
        scratch_shapes=[pltpu.VMEM((SWA_KV_HEADS, HEAD_DIM, ts + BLOCK), f32),
                        pltpu.VMEM((SWA_KV_HEADS, HEAD_DIM, ts + BLOCK), f32),
                        pltpu.VMEM((SWA_KV_HEADS, HEAD_DIM, ts), f32),
                        pltpu.VMEM((SWA_KV_HEADS, HEAD_DIM, ts), f32),
                        pltpu.VMEM((SWA_KV_HEADS, 1, SWA_W), f32)],
        compiler_params=_params(("arbitrary",)),
    )(qt, k, k, kt, kt, v, v, dot, lse, dl, bias_t, bias0_t, sink)


def _dproj_specs(tm):
    half = pl.BlockSpec((tm, 512), lambda i: (i, 0))
    feat = pl.BlockSpec((512, tm), lambda i: (0, i))
    feat_kv = pl.BlockSpec((128, tm), lambda i: (0, i))
    return [feat, feat, feat, half, feat, feat_kv, feat_kv, half, feat_kv]


def _dx_exchange_call(dh, pieces, w_t, bs, tm):
    s_len = dh.shape[0]
    n = len(bs)
    last = s_len // tm - 1

    def body(*refs):
        dh_ref, dqf_ref, dkf_ref, dvf_ref, dfz_ref, dqs_ref, dks_ref, dvs_ref, dsz_ref, dfft_ref, w_ref = refs[:11]
        b_refs = refs[11:11 + n]
        dx_ref = refs[11 + n]
        r_refs = refs[12 + n:12 + 2 * n]
        sems = refs[12 + 2 * n:]
        i = pl.program_id(0)

        @pl.when(i == 0)
        def _():
            _exchange_start(b_refs, r_refs, sems)

        def tr(ref):
            return ref[...].astype(f32).T.astype(bf16)

        dp = jnp.concatenate([tr(dqf_ref), tr(dkf_ref), tr(dvf_ref), dfz_ref[...], tr(dqs_ref), tr(dks_ref),
                              tr(dvs_ref), dsz_ref[...], tr(dfft_ref)], axis=1)
        dx_ref[...] = ALPHA * dh_ref[...] + jnp.dot(dp, w_ref[...], preferred_element_type=f32)

        @pl.when(i == last)
        def _():
            _exchange_wait(b_refs, r_refs, sems)

    fullw = pl.BlockSpec((tm, D_MODEL), lambda i: (i, 0))
    any_spec = pl.BlockSpec(memory_space=pl.ANY)
    out = pl.pallas_call(
        body,
        name="dx_bwd_exchange",
        grid=(s_len // tm,),
        in_specs=[fullw] + _dproj_specs(tm) + [pl.BlockSpec((A_W, D_MODEL), lambda i: (0, 0))] + [any_spec] * n,
        out_specs=[fullw] + [any_spec] * n,
        out_shape=[_sds((s_len, D_MODEL), f32)] + [_sds(b.shape, b.dtype) for b in bs],
        scratch_shapes=[pltpu.SemaphoreType.DMA((7 * n,)), pltpu.SemaphoreType.DMA((7 * n,)),
                        pltpu.SemaphoreType.DMA((n,))],
        compiler_params=_params(("arbitrary",)),
    )(dh, *pieces, w_t, *bs)
    return out[0], out[1:]


DW_STAGE_ROWS = 384


def _dw_call(x2, pieces, tm):
    s_len = x2.shape[0]
    nt = s_len // tm

    def body(x_ref, dqf_ref, dkf_ref, dvf_ref, dfz_ref, dqs_ref, dks_ref, dvs_ref, dsz_ref, dfft_ref, dw_ref,
             acc_ref, stage_ref, sem):
        i = pl.program_id(0)

        @pl.when(i == 0)
        def _():
            acc_ref[...] = jnp.zeros_like(acc_ref)

        xb = x_ref[...].astype(bf16)

        def add_feat(off, lhs):
            acc_ref[off:off + lhs.shape[0], :] += jnp.dot(lhs, xb, preferred_element_type=f32)

        def add_rows(off, piece):
            acc_ref[off:off + piece.shape[1], :] += lax.dot_general(piece, xb, TN, preferred_element_type=f32)

        add_feat(A_FQ, dqf_ref[...])
        add_feat(A_FK, dkf_ref[...])
        add_feat(A_FV, dvf_ref[...])
        add_rows(A_FZ, dfz_ref[...])
        add_feat(A_SQ, dqs_ref[...])
        add_feat(A_SK, dks_ref[...])
        add_feat(A_SV, dvs_ref[...])
        add_rows(A_SZ, dsz_ref[...])
        add_feat(A_FF, dfft_ref[...].astype(bf16))

        @pl.when(i == nt - 1)
        def _():
            for r in range(A_W // DW_STAGE_ROWS):
                rows = slice(r * DW_STAGE_ROWS, (r + 1) * DW_STAGE_ROWS)
                stage_ref[...] = acc_ref[rows, :].astype(bf16)
                cp = pltpu.make_async_copy(stage_ref, dw_ref.at[rows, :], sem)
                cp.start()
                cp.wait()

    return pl.pallas_call(
        body,
        name="dw_in_bwd",
        grid=(nt,),
        in_specs=[pl.BlockSpec((tm, D_MODEL), lambda i: (i, 0))] + _dproj_specs(tm),
        out_specs=pl.BlockSpec(memory_space=pl.ANY),
        out_shape=_sds((A_W, D_MODEL), bf16),
        scratch_shapes=[pltpu.VMEM((A_W, D_MODEL), f32), pltpu.VMEM((DW_STAGE_ROWS, D_MODEL), bf16),
                        pltpu.SemaphoreType.DMA],
        compiler_params=_params(("arbitrary",), VMEM_LIMIT_BIG),
    )(x2, *pieces)


def _adam_call(recv, w, m, v, tc, name):
    rows, cols = w.shape

    def body(r_ref, w_ref, m_ref, v_ref, g_ref, d_ref, mo_ref, vo_ref):
        g = r_ref[0].astype(f32)
        for p in range(1, N_DEV):
            g = g + r_ref[p].astype(f32)
        mn = ADAM_B1 * m_ref[...] + (1.0 - ADAM_B1) * g
        vn = ADAM_B2 * v_ref[...] + (1.0 - ADAM_B2) * (g * g)
        m_hat = mn / (1.0 - ADAM_B1 ** ADAM_STEP)
        v_hat = vn / (1.0 - ADAM_B2 ** ADAM_STEP)
        g_ref[...] = g
        d_ref[...] = -ADAM_LR * (m_hat / (jnp.sqrt(v_hat) + ADAM_EPS) + ADAM_WD * w_ref[...])
        mo_ref[...] = mn
        vo_ref[...] = vn

    blk = pl.BlockSpec((rows, tc), lambda i: (0, i))
    return pl.pallas_call(
        body,
        name=name,
        grid=(cols // tc,),
        in_specs=[pl.BlockSpec((N_DEV, rows, tc), lambda i: (0, 0, i)), blk, blk, blk],
        out_specs=[blk] * 4,
        out_shape=[_sds((rows, cols), f32)] * 4,
        compiler_params=_params(("arbitrary",)),
    )(recv, w, m, v)


def _pad_cols(a, width=128):
    return jnp.pad(a, ((0, 0), (0, width - a.shape[1])))


def _pack_small(ln_g, ln_b, rel, b_f, sink):
    return jnp.concatenate([
        ln_g.reshape(8, 128), ln_b.reshape(8, 128), _pad_cols(rel),
        jnp.pad(_pad_cols(b_f), ((0, 7), (0, 0))), jnp.pad(_pad_cols(sink), ((0, 7), (0, 0)))], axis=0)


def _unpack_small(p):
    return (p[0:8].reshape(1, D_MODEL), p[8:16].reshape(1, D_MODEL), p[16:48, 0:8], p[48:49, 0:8], p[56:57, 0:8])


def kernel(x, w_in, b_f, rel_bias, sink, w_o, ln_g, ln_b, loss_target, m_w_in, m_b_f, m_rel_bias, m_sink, m_w_o, m_ln_g, m_ln_b, v_w_in, v_b_f, v_rel_bias, v_sink, v_w_o, v_ln_g, v_ln_b):
    x2 = x[0]
    tgt = loss_target[0]
    s_len = x2.shape[0]
    shard = w_in.shape[2]

    w_in_t = jnp.transpose(w_in[0])
    g_in, g_o = _gather_call([w_in_t.astype(bf16), w_o[0].astype(bf16)])
    wt_full = g_in.reshape(N_DEV * shard, D_MODEL)
    w_t = jnp.concatenate([wt_full[:O_FF0], wt_full[O_FF1:], wt_full[O_FF0:O_FF1],
                           jnp.zeros((A_W - D_IN, D_MODEL), bf16)], axis=0)
    wo_full = g_o.reshape(D_MODEL, D_MODEL)

    qft, kft, vf, fz, qst, ks, vs, sz, fft, vat, kst, vsta = _proj_call(x2, w_t, 512)
    cum, sgm = _cum_call(fft, b_f.reshape(FOX_HEADS, 1))
    qat, ka, kat, tile_stats = _augment_call(qft, kft, cum.reshape(FOX_HEADS, 1, s_len), 2048)
    npairs, pair_q, pair_k = _fox_prune_tables(tile_stats)
    o_ft, lse_f = _fox_fwd_call(qat, ka, vat, npairs, pair_q, pair_k)
    bucket_t = jnp.asarray(_t5_bucket_table().T)
    bias_t, bias0_t = _swa_bias_call(rel_bias, bucket_t)
    sink_v = sink.reshape(SWA_HEADS)
    o_st, lse_s = _swa_fwd_call(qst, ks, vsta, bias_t, bias0_t, sink_v)

    (dh, do_f, dfz, do_s, dsz, dl_f, dl_s, dwo, dg, db, loss_part) = _post_call(
        o_ft.reshape(FOX_HEADS * HEAD_DIM, s_len), fz, o_st.reshape(SWA_HEADS * HEAD_DIM, s_len), sz, x2, tgt,
        wo_full, ln_g, ln_b, jnp.asarray(_head_selector()).astype(bf16), 512)

    dqf, dkf, dvf, dcq, dck = _fox_bwd_call(ka, kat, vf, qat, do_f.reshape(FOX_HEADS, HEAD_DIM, s_len), lse_f,
                                            dl_f.reshape(FOX_HEADS, 1, s_len), npairs, pair_q, pair_k)
    dqf, dkf, dvf = (a.reshape(FOX_HEADS * HEAD_DIM, s_len) for a in (dqf, dkf, dvf))
    dfft, dbf = _cum_bwd_call(dcq.reshape(FOX_HEADS, s_len), dck.reshape(FOX_HEADS, s_len), sgm)
    dqs, dks, dvs, dbias, dsink = _swa_bwd_call(
        qst, ks, kst, vs, do_s.reshape(SWA_HEADS, HEAD_DIM, s_len), lse_s, dl_s.reshape(SWA_HEADS, 1, s_len),
        bias_t, bias0_t, sink_v)
    dqs = dqs.reshape(SWA_HEADS * HEAD_DIM, s_len)
    dks, dvs = (a.reshape(SWA_KV_HEADS * HEAD_DIM, s_len) for a in (dks, dvs))
    drel = _swa_bias_bwd_call(dbias, bucket_t)

    pieces = (dqf, dkf, dvf, dfz, dqs, dks, dvs, dsz, dfft)
    dw_t = _dw_call(x2, pieces, 1024)

    dwt_full = jnp.concatenate([dw_t[:O_FF0], dw_t[A_FF:A_FF + (O_FF1 - O_FF0)], dw_t[O_FF0:A_FF]], axis=0)
    dw_blocks = dwt_full.reshape(N_DEV, shard, D_MODEL)
    dwo_blocks = dwo.reshape(N_DEV, D_MODEL // N_DEV, D_MODEL).astype(bf16)
    small = _pack_small(dg, db, drel[:, 0:8], dbf[:, 0].reshape(1, 8), dsink[:, 0].reshape(1, 8))
    loss_slot = np.zeros((64, 128), bool)
    loss_slot[49, 0] = True
    small = jnp.where(jnp.asarray(loss_slot), loss_part[0, 0], small)
    small_blocks = jnp.broadcast_to(small[None], (N_DEV,) + small.shape)
    dx, (r_in, r_o, r_small) = _dx_exchange_call(dh, pieces, w_t, [dw_blocks, dwo_blocks, small_blocks], 256)

    win_t = [jnp.transpose(a) for a in _adam_call(
        r_in, w_in_t, jnp.transpose(m_w_in[0]), jnp.transpose(v_w_in[0]), 256, "adam_w_in")]
    g_win, d_win, nm_win, nv_win = win_t
    g_wo, d_wo, nm_wo, nv_wo = _adam_call(r_o, w_o[0], m_w_o[0], v_w_o[0], 256, "adam_w_o")
    p_w = _pack_small(ln_g, ln_b, rel_bias, b_f, sink)
    p_m = _pack_small(m_ln_g, m_ln_b, m_rel_bias, m_b_f, m_sink)
    p_v = _pack_small(v_ln_g, v_ln_b, v_rel_bias, v_b_f, v_sink)
    g_p, d_p, nm_p, nv_p = _adam_call(r_small, p_w, p_m, p_v, 128, "adam_small")

    loss = g_p[49, 0]
    g_lng, g_lnb, g_rel, g_bf, g_sink = _unpack_small(g_p)
    d_lng, d_lnb, d_rel, d_bf, d_sink = _unpack_small(d_p)
    m_lng, m_lnb, m_rel, m_bf, m_sk = _unpack_small(nm_p)
    v_lng, v_lnb, v_rel, v_bf, v_sk = _unpack_small(nv_p)
    return (loss, dx[None], g_win[None], g_bf, g_rel, g_sink, g_wo[None], g_lng, g_lnb,
            d_win[None], d_bf, d_rel, d_sink, d_wo[None], d_lng, d_lnb,
            nm_win[None], m_bf, m_rel, m_sk, nm_wo[None], m_lng, m_lnb,
            nv_win[None], v_bf, v_rel, v_sk, nv_wo[None], v_lng, v_lnb)
```

```python
import functools
import math

import numpy as np
import jax
import jax.numpy as jnp
from jax import lax
from jax.experimental import pallas as pl
from jax.experimental.pallas import tpu as pltpu

f32 = jnp.float32
bf16 = jnp.bfloat16

D_MODEL = 1024
HEAD_DIM = 64
FOX_HEADS = 8
SWA_HEADS = 8
SWA_KV_HEADS = 2
SWA_GROUP = 4
BLOCK = 128
NUM_BUCKETS = 32
MAX_DISTANCE = 128
LN_EPS = 1e-5
NEG_INF = -1e30
ALPHA = 2.0 ** 0.25
SCALE = 1.0 / math.sqrt(HEAD_DIM)
D_IN = 3336

ADAM_LR = 0.001
ADAM_B1 = 0.9
ADAM_B2 = 0.999
ADAM_EPS = 1e-08
ADAM_WD = 0.01
ADAM_STEP = 10

N_DEV = 8
A_FQ, A_FK, A_FV, A_FZ, A_SQ, A_SK, A_SV, A_SZ, A_FF, A_W = 0, 512, 1024, 1536, 2048, 2560, 2688, 2816, 3328, 3456
O_FF0, O_FF1 = 1536, 1544

VMEM_LIMIT = 48 * 1024 * 1024
HIGHEST = lax.Precision.HIGHEST
NT = (((1,), (1,)), ((), ()))
TN = (((0,), (0,)), ((), ()))
MESH = pl.DeviceIdType.MESH
RELS = [(0, 0, 1), (0, 1, 0), (0, 1, 1), (1, 0, 0), (1, 0, 1), (1, 1, 0), (1, 1, 1)]


VMEM_LIMIT_BIG = 60 * 1024 * 1024


def _params(sem=None, vmem=VMEM_LIMIT):
    return pltpu.CompilerParams(dimension_semantics=sem, vmem_limit_bytes=vmem)


def _sds(shape, dtype):
    return jax.ShapeDtypeStruct(shape, dtype)


def _t5_bucket_table():
    qi = np.arange(BLOCK)[:, None]
    kj = np.arange(2 * BLOCK)[None, :]
    rel = qi + BLOCK - kj
    band = (rel >= 0) & (rel < BLOCK)
    relc = np.maximum(rel, 0)
    max_exact = NUM_BUCKETS // 2
    relf = np.maximum(relc, 1).astype(np.float32)
    large = max_exact + (np.log(relf / np.float32(max_exact)) / np.float32(math.log(MAX_DISTANCE / max_exact))
                         * np.float32(NUM_BUCKETS - max_exact)).astype(np.int32)
    large = np.minimum(large, NUM_BUCKETS - 1)
    bucket = np.where(relc < max_exact, relc, large).astype(np.int32)
    bucket = np.where(band, bucket, -1).astype(np.int32)
    return bucket


def _mesh_pos():
    return lax.axis_index("x"), lax.axis_index("y"), lax.axis_index("c")


def _dev_index(p):
    return 4 * p[0] + 2 * p[1] + p[2]


def _gather_call(xs):
    n = len(xs)

    def body(*refs):
        x_refs, o_refs = refs[:n], refs[n:2 * n]
        send_sems, recv_sems, local_sems = refs[2 * n:]
        x, y, c = _mesh_pos()
        me, sib = (x, y, c), (x, y, 1 - c)
        chips = [(1 - x, y), (x, 1 - y), (1 - x, 1 - y)]

        def copy(a, k, block, to, src=None):
            slot = o_refs[a].at[_dev_index(block)]
            return pltpu.make_async_remote_copy(
                src_ref=slot if src is None else src, dst_ref=slot,
                send_sem=send_sems.at[a * 7 + k], recv_sem=recv_sems.at[a * 7 + k],
                device_id=to, device_id_type=MESH)

        mine = [pltpu.make_async_copy(x_refs[a], o_refs[a].at[_dev_index(me)], local_sems.at[a]) for a in range(n)]
        for cp in mine:
            cp.start()
        first = []
        for a in range(n):
            first.append(copy(a, 0, me, sib, src=x_refs[a]))
            first += [copy(a, 1 + j, me, (*chip, c), src=x_refs[a]) for j, chip in enumerate(chips)]
        for cp in first:
            cp.start()
        passed = []
        for j, chip in enumerate(chips):
            for a in range(n):
                copy(a, 1 + j, (*chip, c), me).wait_recv()
                fwd = copy(a, 4 + j, (*chip, c), sib)
                fwd.start()
                passed.append(fwd)
        for a in range(n):
            copy(a, 0, sib, me).wait_recv()
            for j, chip in enumerate(chips):
                copy(a, 4 + j, (*chip, 1 - c), me).wait_recv()
        for cp in first + passed:
            cp.wait_send()
        for cp in mine:
            cp.wait()

    any_spec = pl.BlockSpec(memory_space=pl.ANY)
    return pl.pallas_call(
        body,
        name="gather_weights",
        out_shape=[_sds((N_DEV,) + a.shape, a.dtype) for a in xs],
        in_specs=[any_spec] * n,
        out_specs=[any_spec] * n,
        scratch_shapes=[pltpu.SemaphoreType.DMA((7 * n,)), pltpu.SemaphoreType.DMA((7 * n,)),
                        pltpu.SemaphoreType.DMA((n,))],
    )(*xs)


def _exchange_copies(b_refs, r_refs, send_sems, recv_sems, local_sems, incoming):
    n = len(b_refs)
    x, y, c = _mesh_pos()
    me_idx = _dev_index((x, y, c))
    mine = [pltpu.make_async_copy(b_refs[a].at[me_idx], r_refs[a].at[me_idx], local_sems.at[a]) for a in range(n)]
    remote = []
    for k, r in enumerate(RELS):
        peer = ((1 - x) if r[0] else x, (1 - y) if r[1] else y, (1 - c) if r[2] else c)
        pidx = _dev_index(peer)
        for a in range(n):
            remote.append(pltpu.make_async_remote_copy(
                src_ref=b_refs[a].at[pidx], dst_ref=r_refs[a].at[pidx if incoming else me_idx],
                send_sem=send_sems.at[a * 7 + k], recv_sem=recv_sems.at[a * 7 + k],
                device_id=peer, device_id_type=MESH))
    return mine, remote


def _exchange_start(b_refs, r_refs, sems):
    mine, out = _exchange_copies(b_refs, r_refs, *sems, incoming=False)
    for cp in mine + out:
        cp.start()


def _exchange_wait(b_refs, r_refs, sems):
    mine, inc = _exchange_copies(b_refs, r_refs, *sems, incoming=True)
    for cp in inc:
        cp.wait_recv()
    for cp in inc:
        cp.wait_send()
    for cp in mine:
        cp.wait()


def _proj_call(x2, w_t, tm):
    s_len = x2.shape[0]

    def body(x_ref, w_ref, qft_ref, kft_ref, vf_ref, fz_ref, qst_ref, ks_ref, vs_ref, sz_ref, fft_ref, vat_ref,
             kst_ref, vsta_ref):
        xb = x_ref[...].astype(bf16)

        def seg_t(off, width):
            return lax.dot_general(w_ref[off:off + width, :], xb, NT, preferred_element_type=f32)

        def seg(off, width):
            return lax.dot_general(xb, w_ref[off:off + width, :], NT, preferred_element_type=f32)

        def put_heads(ref, acc, nheads):
            for h in range(nheads):
                ref[h] = acc[:, h * HEAD_DIM:(h + 1) * HEAD_DIM].astype(bf16)

        def put_heads_t(ref, acc_t, nheads, augment):
            for h in range(nheads):
                ref[h, 0:HEAD_DIM, :] = acc_t[h * HEAD_DIM:(h + 1) * HEAD_DIM, :].astype(bf16)
                if augment:
                    ref[h, HEAD_DIM:2 * HEAD_DIM, :] = ones_row

        ones_row = jnp.where(lax.broadcasted_iota(jnp.int32, (HEAD_DIM, tm), 0) == 0, 1.0, 0.0).astype(bf16)
        put_heads_t(vat_ref, seg_t(A_FV, 512), FOX_HEADS, True)
        put_heads_t(qft_ref, seg_t(A_FQ, 512) * SCALE, FOX_HEADS, False)
        put_heads_t(kft_ref, seg_t(A_FK, 512), FOX_HEADS, False)
        put_heads(vf_ref, seg(A_FV, 512), FOX_HEADS)
        fz_ref[...] = seg(A_FZ, 512)
        put_heads_t(qst_ref, seg_t(A_SQ, 512) * SCALE, SWA_HEADS, False)
        put_heads(ks_ref, seg(A_SK, 128), SWA_KV_HEADS)
        put_heads(vs_ref, seg(A_SV, 128), SWA_KV_HEADS)
        put_heads_t(kst_ref, seg_t(A_SK, 128), SWA_KV_HEADS, False)
        put_heads_t(vsta_ref, seg_t(A_SV, 128), SWA_KV_HEADS, True)
        sz_ref[...] = seg(A_SZ, 512)
        fft_ref[...] = seg(A_FF, 128).T[:FOX_HEADS, :]

    def heads(nh):
        return pl.BlockSpec((nh, tm, HEAD_DIM), lambda i: (0, i, 0))

    def feat(nh, rows):
        return pl.BlockSpec((nh, rows, tm), lambda i: (0, 0, i))

    wide = pl.BlockSpec((tm, 512), lambda i: (i, 0))
    return pl.pallas_call(
        body,
        name="proj_fwd",
        grid=(s_len // tm,),
        in_specs=[pl.BlockSpec((tm, D_MODEL), lambda i: (i, 0)), pl.BlockSpec((A_W, D_MODEL), lambda i: (0, 0))],
        out_specs=[feat(8, HEAD_DIM), feat(8, HEAD_DIM), heads(8), wide, feat(8, HEAD_DIM), heads(2), heads(2), wide,
                   pl.BlockSpec((FOX_HEADS, tm), lambda i: (0, i)),
                   feat(FOX_HEADS, 2 * HEAD_DIM), feat(2, HEAD_DIM), feat(2, 2 * HEAD_DIM)],
        out_shape=[_sds((8, HEAD_DIM, s_len), bf16)] * 2 + [_sds((8, s_len, HEAD_DIM), bf16)]
                  + [_sds((s_len, 512), f32), _sds((8, HEAD_DIM, s_len), bf16),
                     _sds((2, s_len, HEAD_DIM), bf16), _sds((2, s_len, HEAD_DIM), bf16), _sds((s_len, 512), f32),
                     _sds((FOX_HEADS, s_len), f32), _sds((FOX_HEADS, 2 * HEAD_DIM, s_len), bf16),
                     _sds((2, HEAD_DIM, s_len), bf16), _sds((2, 2 * HEAD_DIM, s_len), bf16)],
        compiler_params=_params(("arbitrary",)),
    )(x2, w_t)


AUG = 2 * HEAD_DIM
NEAR_KEYS = 3


def _augment_call(q_t, k_t, cum_row, tm):
    nh, _, s_len = k_t.shape
    per_step = tm // FOX_T

    def body(qt_ref, kt_ref, c_ref, qat_ref, ka_ref, kat_ref, st_ref):
        c = c_ref[0]
        hi = c.astype(bf16).astype(f32)
        r1 = c - hi
        mid = r1.astype(bf16).astype(f32)
        lo = (r1 - mid).astype(bf16).astype(f32)
        row = lax.broadcasted_iota(jnp.int32, (HEAD_DIM, tm), 0)
        q_tail = jnp.where(row == 0, hi, jnp.where(row == 1, mid, jnp.where(row == 2, lo,
                           jnp.where(row < 6, 1.0, 0.0))))
        k_tail = jnp.where(row < 3, 1.0, jnp.where(row == 3, -hi, jnp.where(row == 4, -mid,
                           jnp.where(row == 5, -lo, 0.0))))
        qat_ref[0, 0:HEAD_DIM, :] = qt_ref[0]
        qat_ref[0, HEAD_DIM:AUG, :] = q_tail.astype(bf16)
        kat_ref[0, 0:HEAD_DIM, :] = kt_ref[0]
        kat_ref[0, HEAD_DIM:AUG, :] = k_tail.astype(bf16)
        qt = qt_ref[0].astype(f32)
        kt = kt_ref[0].astype(f32)
        ka_ref[0] = jnp.concatenate([kt, k_tail], axis=0).T.astype(bf16)
        qn2 = jnp.sum(qt * qt, axis=0, keepdims=True)
        kn2 = jnp.sum(kt * kt, axis=0, keepdims=True)
        sd = jnp.sum(qt * kt, axis=0, keepdims=True)
        k_and_c = jnp.concatenate([kt, jnp.broadcast_to(c, (8, tm))], axis=0)
        lane = lax.broadcasted_iota(jnp.int32, (1, tm), 1)
        for shift in range(1, NEAR_KEYS + 1):
            prev = pltpu.roll(k_and_c, shift, axis=1)
            near = jnp.sum(qt * prev[0:HEAD_DIM], axis=0, keepdims=True) + (c - prev[HEAD_DIM:HEAD_DIM + 1])
            sd = jnp.maximum(sd, jnp.where(lane >= shift, near, NEG_INF))
        srow = lax.broadcasted_iota(jnp.int32, (8, LANES), 0)
        for part in range(per_step):
            sl = slice(part * FOX_T, (part + 1) * FOX_T)
            vals = [jnp.sqrt(jnp.max(qn2[:, sl], axis=1, keepdims=True)),
                    jnp.sqrt(jnp.max(kn2[:, sl], axis=1, keepdims=True)),
                    jnp.min(sd[:, sl], axis=1, keepdims=True),
                    jnp.max(c[:, sl], axis=1, keepdims=True), jnp.min(c[:, sl], axis=1, keepdims=True)]
            out = jnp.zeros((8, LANES), f32)
            for r, val in enumerate(vals):
                out = jnp.where(srow == r, val, out)
            st_ref[0, part] = out

    tile_t = pl.BlockSpec((1, HEAD_DIM, tm), lambda h, i: (h, 0, i))
    return pl.pallas_call(
        body,
        name="fox_augment",
        grid=(nh, s_len // tm),
        in_specs=[tile_t, tile_t, pl.BlockSpec((1, 1, tm), lambda h, i: (h, 0, i))],
        out_specs=[pl.BlockSpec((1, AUG, tm), lambda h, i: (h, 0, i)),
                   pl.BlockSpec((1, tm, AUG), lambda h, i: (h, i, 0)),
                   pl.BlockSpec((1, AUG, tm), lambda h, i: (h, 0, i)),
                   pl.BlockSpec((1, per_step, 8, LANES), lambda h, i: (h, i, 0, 0))],
        out_shape=[_sds((nh, AUG, s_len), bf16), _sds((nh, s_len, AUG), bf16), _sds((nh, AUG, s_len), bf16),
                   _sds((nh, s_len // FOX_T, 8, LANES), f32)],
        compiler_params=_params(("arbitrary", "arbitrary")),
    )(q_t, k_t, cum_row)


FOX_PRUNE_GAP = 32.0


def _fox_prune_tables(stats):
    s = stats[:, :, :, 0]
    qn, kn, sd, cmx, cmn = (s[:, :, r] for r in range(5))
    nt = s.shape[1]
    bound = qn[:, :, None] * kn[:, None, :] + (cmx[:, :, None] - cmn[:, None, :])
    margin = 0.01 + 1e-5 * (jnp.abs(cmx)[:, :, None] + jnp.abs(cmn)[:, None, :])
    qi = lax.broadcasted_iota(jnp.int32, (nt, nt), 0)
    kj = lax.broadcasted_iota(jnp.int32, (nt, nt), 1)
    skip = (bound + margin < sd[:, :, None] - FOX_PRUNE_GAP) & (kj < qi)[None]
    first = jnp.sum(jnp.cumprod(skip.astype(jnp.int32), axis=2), axis=2)
    tiles = lax.broadcasted_iota(jnp.int32, (1, nt), 1)
    cnt = tiles - first
    ends = jnp.cumsum(cnt, axis=1)
    off = ends - cnt
    kmax = nt * (nt - 1) // 2
    k = lax.broadcasted_iota(jnp.int32, (1, kmax), 1)
    pair_q = jnp.minimum(jnp.sum((ends[:, None, :] <= k[:, :, None]).astype(jnp.int32), axis=2), nt - 1)
    hit = pair_q[:, :, None] == tiles[:, None, :]
    first_k = jnp.sum(jnp.where(hit, first[:, None, :], 0), axis=2)
    off_k = jnp.sum(jnp.where(hit, off[:, None, :], 0), axis=2)
    pair_k = jnp.clip(first_k + k - off_k, 0, nt - 1)
    return (ends[:, nt - 1].astype(jnp.int32), pair_q.reshape(-1).astype(jnp.int32),
            pair_k.reshape(-1).astype(jnp.int32))


CUM_CHUNK = 512


def _cum_call(fft, bf_col):
    s_len = fft.shape[1]
    ch = CUM_CHUNK

    def body(f_ref, b_ref, cum_ref, sg_ref):
        r = lax.broadcasted_iota(jnp.int32, (ch, ch), 0)
        c = lax.broadcasted_iota(jnp.int32, (ch, ch), 1)
        upper = (r <= c).astype(f32)
        carry = jnp.zeros((FOX_HEADS, 1), f32)
        for n in range(s_len // ch):
            z = f_ref[:, n * ch:(n + 1) * ch] + b_ref[...]
            logf = jnp.minimum(z, 0.0) - jnp.log1p(jnp.exp(-jnp.abs(z)))
            sg_ref[:, n * ch:(n + 1) * ch] = 1.0 / (1.0 + jnp.exp(z))
            cs = jnp.dot(logf, upper, precision=HIGHEST, preferred_element_type=f32) + carry
            cum_ref[:, n * ch:(n + 1) * ch] = cs
            carry = cs[:, ch - 1:ch]

    return pl.pallas_call(
        body,
        name="fox_cum_fwd",
        out_shape=[_sds((FOX_HEADS, s_len), f32)] * 2,
        compiler_params=_params(),
    )(fft, bf_col)


def _cum_bwd_call(dcq, dck, sg):
    s_len = sg.shape[1]
    ch = CUM_CHUNK
    nch = s_len // ch

    def body(q_ref, k_ref, sg_ref, dff_ref, dbf_ref):
        r = lax.broadcasted_iota(jnp.int32, (ch, ch), 0)
        c = lax.broadcasted_iota(jnp.int32, (ch, ch), 1)
        lower = (r >= c).astype(f32)
        dff_ref[...] = jnp.zeros_like(dff_ref)
        carry = jnp.zeros((FOX_HEADS, 1), f32)
        total = jnp.zeros((FOX_HEADS, 1), f32)
        for n in reversed(range(nch)):
            sl = slice(n * ch, (n + 1) * ch)
            dcum = q_ref[:, sl] - k_ref[:, sl]
            rs = jnp.dot(dcum, lower, precision=HIGHEST, preferred_element_type=f32) + carry
            carry = rs[:, 0:1]
            dff = rs * sg_ref[:, sl]
            dff_ref[0:FOX_HEADS, sl] = dff
            total = total + jnp.sum(dff, axis=1, keepdims=True)
        dbf_ref[...] = jnp.broadcast_to(total, (FOX_HEADS, 128))

    return pl.pallas_call(
        body,
        name="fox_cum_bwd",
        out_shape=[_sds((128, s_len), f32), _sds((FOX_HEADS, 128), f32)],
        compiler_params=_params(),
    )(dcq, dck, sg)


FOX_T = 512
LANES = 128


def _causal_keep(t):
    return lax.broadcasted_iota(jnp.int32, (t, t), 0) <= lax.broadcasted_iota(jnp.int32, (t, t), 1)


def _tile_cols(i, t):
    return pl.ds(pl.multiple_of(i * t, t), t)


def _fox_pair(n, nt, kmax, h, pq_ref, pk_ref):
    k = h * kmax + jnp.maximum(n - nt, 0)
    return jnp.where(n < nt, n, pq_ref[k]), jnp.where(n < nt, n, pk_ref[k])


def _fox_fwd_call(qat, ka, vat, npairs, pair_q, pair_k):
    nh, s_len, _ = ka.shape
    t = FOX_T
    nt = s_len // t
    kmax = nt * (nt - 1) // 2
    assert nt >= 2 and nt % 2 == 0

    def body(np_ref, pq_ref, pk_ref, qat_ref, ka_ref, vat_ref, o_ref, lse_ref, s0, s1, p0, p1, a0, a1, m_all, acc_all):
        h = pl.program_id(0)
        extra = np_ref[h]
        total = nt + extra
        m_all[...] = jnp.full(m_all.shape, NEG_INF, f32)
        acc_all[...] = jnp.zeros(acc_all.shape, f32)
        bufs = ((s0, p0, a0), (s1, p1, a1))

        def pair(n):
            return _fox_pair(n, nt, kmax, h, pq_ref, pk_ref)

        def scores(n, b, masked):
            i, j = pair(n)
            st = jnp.dot(ka_ref[0, _tile_cols(j, t), :], qat_ref[0, :, _tile_cols(i, t)], preferred_element_type=f32)
            if masked:
                st = jnp.where(_causal_keep(t), st, NEG_INF)
            bufs[b][0][...] = st

        def softmax(n, b):
            i, _ = pair(n)
            s_ref, p_ref, a_ref = bufs[b]
            for c in range(t // LANES):
                cols = slice(c * LANES, (c + 1) * LANES)
                mcols = pl.ds(pl.multiple_of(i * t + c * LANES, LANES), LANES)
                m_old = m_all[:, mcols]
                m_new = jnp.maximum(m_old, jnp.max(s_ref[:, cols], axis=0, keepdims=True))
                m_all[:, mcols] = m_new
                a_ref[:, cols] = jnp.exp(m_old - m_new)
                p_ref[:, cols] = jnp.exp(s_ref[:, cols] - m_new).astype(bf16)

        def accum(n, b):
            i, j = pair(n)
            cols = _tile_cols(i, t)
            acc_all[:, cols] = bufs[b][2][...] * acc_all[:, cols] + jnp.dot(
                vat_ref[0, :, _tile_cols(j, t)], bufs[b][1][...], preferred_element_type=f32)

        def step(n, b, masked):
            accum(n - 2, b)
            softmax(n - 1, 1 - b)
            scores(n, b, masked)

        scores(0, 0, True)
        scores(1, 1, True)
        softmax(0, 0)

        def diag_steps(d, _):
            n = 2 + 2 * d
            step(n, 0, True)
            step(n + 1, 1, True)
            return 0

        lax.fori_loop(0, (nt - 2) // 2, diag_steps, 0)

        def off_steps(d, _):
            n = nt + 2 * d
            step(n, 0, False)
            step(n + 1, 1, False)
            return 0

        lax.fori_loop(0, extra // 2, off_steps, 0)

        @pl.when(extra % 2 == 1)
        def _():
            step(total - 1, 0, False)
            softmax(total - 1, 0)
            accum(total - 2, 1)
            accum(total - 1, 0)

        @pl.when(extra % 2 == 0)
        def _():
            softmax(total - 1, 1)
            accum(total - 2, 0)
            accum(total - 1, 1)

        l = acc_all[HEAD_DIM:HEAD_DIM + 1, :]
        o_ref[0] = acc_all[0:HEAD_DIM, :] / l
        lse_ref[0] = m_all[...] + jnp.log(l)

    smem = pl.BlockSpec(memory_space=pltpu.SMEM)
    return pl.pallas_call(
        body,
        name="fox_fwd",
        grid=(nh,),
        in_specs=[smem, smem, smem,
                  pl.BlockSpec((1, AUG, s_len), lambda h: (h, 0, 0)),
                  pl.BlockSpec((1, s_len, AUG), lambda h: (h, 0, 0)),
                  pl.BlockSpec((1, AUG, s_len), lambda h: (h, 0, 0))],
        out_specs=[pl.BlockSpec((1, HEAD_DIM, s_len), lambda h: (h, 0, 0)),
                   pl.BlockSpec((1, 1, s_len), lambda h: (h, 0, 0))],
        out_shape=[_sds((nh, HEAD_DIM, s_len), f32), _sds((nh, 1, s_len), f32)],
        scratch_shapes=[pltpu.VMEM((t, t), f32), pltpu.VMEM((t, t), f32), pltpu.VMEM((t, t), bf16),
                        pltpu.VMEM((t, t), bf16), pltpu.VMEM((1, t), f32), pltpu.VMEM((1, t), f32),
                        pltpu.VMEM((1, s_len), f32), pltpu.VMEM((AUG, s_len), f32)],
        compiler_params=_params(("arbitrary",)),
    )(npairs, pair_q, pair_k, qat, ka, vat)


SWA_TS = 512


SWA_W = SWA_GROUP * BLOCK


def _swa_bias_call(rel_bias, bucket_t):
    def body(rb_ref, bk_ref, b_ref, b0_ref):
        bk = bk_ref[...]
        row = lax.broadcasted_iota(jnp.int32, (2 * BLOCK, BLOCK), 0)
        for h in range(SWA_HEADS):
            acc = jnp.full((2 * BLOCK, BLOCK), NEG_INF, f32)
            for b in range(NUM_BUCKETS):
                acc = jnp.where(bk == b, rb_ref[b, h], acc)
            g, hh = divmod(h, SWA_GROUP)
            b_ref[g, :, hh * BLOCK:(hh + 1) * BLOCK] = acc
            b0_ref[g, :, hh * BLOCK:(hh + 1) * BLOCK] = jnp.where(row < BLOCK, NEG_INF, acc)

    return pl.pallas_call(
        body,
        name="swa_bias",
        in_specs=[pl.BlockSpec(memory_space=pltpu.SMEM), pl.BlockSpec(memory_space=pltpu.VMEM)],
        out_shape=[_sds((SWA_KV_HEADS, 2 * BLOCK, SWA_W), f32)] * 2,
        compiler_params=_params(),
    )(rel_bias, bucket_t)


def _swa_bias_bwd_call(dbias, bucket_t):
    def body(d_ref, bk_ref, o_ref):
        bk = bk_ref[...]
        row = lax.broadcasted_iota(jnp.int32, (NUM_BUCKETS, 128), 0)
        col = lax.broadcasted_iota(jnp.int32, (NUM_BUCKETS, 128), 1)
        out = jnp.zeros((NUM_BUCKETS, 128), f32)
        for h in range(SWA_HEADS):
            g, hh = divmod(h, SWA_GROUP)
            d = d_ref[g, :, hh * BLOCK:(hh + 1) * BLOCK]
            for b in range(NUM_BUCKETS):
                val = jnp.sum(jnp.sum(jnp.where(bk == b, d, 0.0), axis=1, keepdims=True), axis=0, keepdims=True)
                out = jnp.where((row == b) & (col == h), val, out)
        o_ref[...] = out

    return pl.pallas_call(
        body,
        name="swa_bias_bwd",
        out_shape=_sds((NUM_BUCKETS, 128), f32),
        compiler_params=_params(),
    )(dbias, bucket_t)


def _sink_row(sink_ref, g):
    return jnp.concatenate([jnp.full((1, BLOCK), sink_ref[g * SWA_GROUP + hh], f32) for hh in range(SWA_GROUP)], axis=1)


def _group_lanes(ref, g, cols):
    return jnp.concatenate([ref[g * SWA_GROUP + hh, :, cols] for hh in range(SWA_GROUP)], axis=1)


def _swa_fwd_call(qt, k, vta, bias_t, bias0_t, sink):
    s_len = qt.shape[2]
    ts = SWA_TS
    nb = ts // BLOCK

    def body(qt_ref, kc_ref, kp_ref, vc_ref, vp_ref, b_ref, b0_ref, sink_ref, o_ref, lse_ref):
        first = pl.program_id(0) == 0
        kall = [jnp.concatenate([kp_ref[g], kc_ref[g]], axis=0) for g in range(SWA_KV_HEADS)]
        vall = [jnp.concatenate([vp_ref[g], vc_ref[g]], axis=1) for g in range(SWA_KV_HEADS)]
        sinks = [_sink_row(sink_ref, g) for g in range(SWA_KV_HEADS)]
        items = [(g, b) for g in range(SWA_KV_HEADS) for b in range(nb)]

        def scores(g, b):
            qg = _group_lanes(qt_ref, g, slice(b * BLOCK, (b + 1) * BLOCK))
            bias_b = b_ref[g]
            if b == 0:
                bias_b = jnp.where(first, b0_ref[g], bias_b)
            return jnp.dot(kall[g][b * BLOCK:(b + 2) * BLOCK], qg, preferred_element_type=f32) + bias_b

        def finish(g, b, st):
            m = jnp.maximum(jnp.max(st, axis=0, keepdims=True), sinks[g])
            pt = jnp.exp(st - m)
            acc = jnp.dot(vall[g][:, b * BLOCK:(b + 2) * BLOCK], pt.astype(bf16), preferred_element_type=f32)
            l = acc[HEAD_DIM:HEAD_DIM + 1, :] + jnp.exp(sinks[g] - m)
            return acc[0:HEAD_DIM, :] / l, m + jnp.log(l)

        outs, lses = {}, {}
        st_next = scores(*items[0])
        for idx, (g, b) in enumerate(items):
            st = st_next
            if idx + 1 < len(items):
                st_next = scores(*items[idx + 1])
            outs[g, b], lses[g, b] = finish(g, b, st)
        for g in range(SWA_KV_HEADS):
            for hh in range(SWA_GROUP):
                lanes = slice(hh * BLOCK, (hh + 1) * BLOCK)
                o_ref[g * SWA_GROUP + hh] = jnp.concatenate([outs[g, b][:, lanes] for b in range(nb)], axis=1)
                lse_ref[g * SWA_GROUP + hh] = jnp.concatenate([lses[g, b][:, lanes] for b in range(nb)], axis=1)

    def prev_blk(n):
        return jnp.maximum(n * nb - 1, 0)

    bspec = pl.BlockSpec((SWA_KV_HEADS, 2 * BLOCK, SWA_W), lambda n: (0, 0, 0))
    return pl.pallas_call(
        body,
        name="swa_fwd",
        grid=(s_len // ts,),
        in_specs=[pl.BlockSpec((SWA_HEADS, HEAD_DIM, ts), lambda n: (0, 0, n)),
                  pl.BlockSpec((SWA_KV_HEADS, ts, HEAD_DIM), lambda n: (0, n, 0)),
                  pl.BlockSpec((SWA_KV_HEADS, BLOCK, HEAD_DIM), lambda n: (0, prev_blk(n), 0)),
                  pl.BlockSpec((SWA_KV_HEADS, AUG, ts), lambda n: (0, 0, n)),
                  pl.BlockSpec((SWA_KV_HEADS, AUG, BLOCK), lambda n: (0, 0, prev_blk(n))),
                  bspec, bspec, pl.BlockSpec(memory_space=pltpu.SMEM)],
        out_specs=[pl.BlockSpec((SWA_HEADS, HEAD_DIM, ts), lambda n: (0, 0, n)),
                   pl.BlockSpec((SWA_HEADS, 1, ts), lambda n: (0, 0, n))],
        out_shape=[_sds((SWA_HEADS, HEAD_DIM, s_len), f32), _sds((SWA_HEADS, 1, s_len), f32)],
        compiler_params=_params(("arbitrary",)),
    )(qt, k, k, vta, vta, bias_t, bias0_t, sink)


def _head_selector():
    sel = np.zeros((512, 128), np.float32)
    for h in range(8):
        sel[h * HEAD_DIM:(h + 1) * HEAD_DIM, h] = 1.0
    return sel


def _post_call(of, fz, osw, sz, x2, tgt, wo, ln_g, ln_b, sel, tm):
    s_len = x2.shape[0]

    def body(of_ref, fz_ref, os_ref, sz_ref, x_ref, t_ref, wo_ref, g_ref, b_ref, sel_ref,
             dh_ref, dof_ref, dfz_ref, dos_ref, dsz_ref, dlf_ref, dls_ref, dwo_ref, dg_ref, db_ref, loss_ref):
        n = pl.program_id(0)

        @pl.when(n == 0)
        def _():
            dwo_ref[...] = jnp.zeros_like(dwo_ref)
            dg_ref[...] = jnp.zeros_like(dg_ref)
            db_ref[...] = jnp.zeros_like(db_ref)
            loss_ref[...] = jnp.zeros_like(loss_ref)

        gam = g_ref[...]
        sel_m = sel_ref[...]

        def forward(r):
            o_f = of_ref[:, r].T
            o_s = os_ref[:, r].T
            fz = fz_ref[r, :]
            sz = sz_ref[r, :]
            sg_f = jax.nn.sigmoid(fz)
            sg_s = jax.nn.sigmoid(sz)
            silu_f = fz * sg_f
            silu_s = sz * sg_s
            mixed = jnp.concatenate([o_f * silu_f, o_s * silu_s], axis=1).astype(bf16)
            y = jnp.dot(mixed, wo_ref[...], preferred_element_type=f32)
            return o_f, o_s, fz, sz, sg_f, sg_s, silu_f, silu_s, mixed, y

        def norm_and_back(r, fwd):
            mixed, y = fwd[8], fwd[9]
            h = ALPHA * x_ref[r, :] + y
            mu = jnp.mean(h, axis=1, keepdims=True)
            hc = h - mu
            var = jnp.mean(hc * hc, axis=1, keepdims=True)
            rstd = lax.rsqrt(var + LN_EPS)
            xhat = hc * rstd
            out = xhat * gam + b_ref[...]
            err = out - t_ref[r, :]
            tok_loss = jnp.mean(err * err, axis=1, keepdims=True)
            loss_ref[...] += 0.5 * jnp.sum(tok_loss, axis=0, keepdims=True)
            dout = err * (1.0 / D_MODEL)
            dg_ref[...] += jnp.sum(dout * xhat, axis=0, keepdims=True)
            db_ref[...] += jnp.sum(dout, axis=0, keepdims=True)
            dxh = dout * gam
            m1 = jnp.mean(dxh, axis=1, keepdims=True)
            m2 = jnp.mean(dxh * xhat, axis=1, keepdims=True)
            dh = rstd * (dxh - m1 - xhat * m2)
            dh_ref[r, :] = dh
            dyb = dh.astype(bf16)
            dmix = lax.dot_general(dyb, wo_ref[...], NT, preferred_element_type=f32)
            dwo_ref[...] += lax.dot_general(mixed, dyb, TN, preferred_element_type=f32)
            return dmix

        def head_sums(prod):
            hi = prod.astype(bf16)
            lo = (prod - hi.astype(f32)).astype(bf16)
            return (jnp.dot(hi, sel_m, preferred_element_type=f32) + jnp.dot(lo, sel_m, preferred_element_type=f32))

        def gates_back(r, fwd, dmix):
            o_f, o_s, fz, sz, sg_f, sg_s, silu_f, silu_s = fwd[:8]
            dm_f = dmix[:, :512]
            dm_s = dmix[:, 512:]
            do_f = dm_f * silu_f
            do_s = dm_s * silu_s
            dfz_ref[r, :] = (dm_f * o_f * (sg_f * (1.0 + fz * (1.0 - sg_f)))).astype(bf16)
            dsz_ref[r, :] = (dm_s * o_s * (sg_s * (1.0 + sz * (1.0 - sg_s)))).astype(bf16)
            dof_ref[:, r] = do_f.T.astype(bf16)
            dos_ref[:, r] = do_s.T.astype(bf16)
            dlf_ref[:, r] = head_sums(do_f * o_f).T[:FOX_HEADS, :]
            dls_ref[:, r] = head_sums(do_s * o_s).T[:SWA_HEADS, :]

        halves = [slice(k * (tm // 2), (k + 1) * (tm // 2)) for k in range(2)]
        fwds = [forward(r) for r in halves]
        dmixes = [norm_and_back(r, f) for r, f in zip(halves, fwds)]
        for r, f, d in zip(halves, fwds, dmixes):
            gates_back(r, f, d)

    feat = pl.BlockSpec((512, tm), lambda n: (0, n))
    rows8 = pl.BlockSpec((8, tm), lambda n: (0, n))
    half = pl.BlockSpec((tm, 512), lambda n: (n, 0))
    fullw = pl.BlockSpec((tm, D_MODEL), lambda n: (n, 0))
    vec = pl.BlockSpec((1, D_MODEL), lambda n: (0, 0))
    return pl.pallas_call(
        body,
        name="post_fwd_bwd",
        grid=(s_len // tm,),
        in_specs=[feat, half, feat, half, fullw, fullw,
                  pl.BlockSpec((D_MODEL, D_MODEL), lambda n: (0, 0)), vec, vec,
                  pl.BlockSpec((512, 128), lambda n: (0, 0))],
        out_specs=[fullw, feat, half, feat, half, rows8, rows8,
                   pl.BlockSpec((D_MODEL, D_MODEL), lambda n: (0, 0)), vec, vec,
                   pl.BlockSpec((1, 1), lambda n: (0, 0))],
        out_shape=[_sds((s_len, D_MODEL), f32), _sds((512, s_len), bf16), _sds((s_len, 512), bf16),
                   _sds((512, s_len), bf16), _sds((s_len, 512), bf16),
                   _sds((FOX_HEADS, s_len), f32), _sds((SWA_HEADS, s_len), f32),
                   _sds((D_MODEL, D_MODEL), f32), _sds((1, D_MODEL), f32), _sds((1, D_MODEL), f32),
                   _sds((1, 1), f32)],
        compiler_params=_params(("arbitrary",), VMEM_LIMIT_BIG),
    )(of, fz, osw, sz, x2, tgt, wo, ln_g, ln_b, sel)


def _fox_bwd_call(ka, kat, v, qat, dot, lse_row, dl_row, npairs, pair_q, pair_k):
    nh, s_len, _ = ka.shape
    t = FOX_T
    nt = s_len // t
    kmax = nt * (nt - 1) // 2
    assert nt >= 2 and nt % 2 == 0
    ck_slot = HEAD_DIM + 3
    cq_slot = HEAD_DIM

    def body(np_ref, pq_ref, pk_ref, ka_ref, kat_ref, v_ref, qat_ref, dot_ref, lse_ref, dl_ref,
             dq_ref, dk_ref, dv_ref, dcq_ref, dck_ref, dqt_all, dkat_all, dvt_all, p0, p1, ds0, ds1):
        h = pl.program_id(0)
        extra = np_ref[h]
        total = nt + extra
        dqt_all[...] = jnp.zeros(dqt_all.shape, f32)
        dkat_all[...] = jnp.zeros(dkat_all.shape, f32)
        dvt_all[...] = jnp.zeros(dvt_all.shape, f32)
        pbuf, dsbuf = (p0, p1), (ds0, ds1)

        def pair(n):
            return _fox_pair(n, nt, kmax, h, pq_ref, pk_ref)

        def probs(n, b, masked):
            i, j = pair(n)
            qc, kr = _tile_cols(i, t), _tile_cols(j, t)
            st = jnp.dot(ka_ref[0, kr, :], qat_ref[0, :, qc], preferred_element_type=f32)
            dpt = jnp.dot(v_ref[0, kr, :], dot_ref[0, :, qc], preferred_element_type=f32)
            if masked:
                st = jnp.where(_causal_keep(t), st, NEG_INF)
            pt = jnp.exp(st - lse_ref[0, :, qc])
            pbuf[b][...] = pt.astype(bf16)
            dsbuf[b][...] = (pt * (dpt - dl_ref[0, :, qc])).astype(bf16)

        def grads(n, b):
            i, j = pair(n)
            qc, kc = _tile_cols(i, t), _tile_cols(j, t)
            dvt_all[:, kc] += lax.dot_general(dot_ref[0, :, qc], pbuf[b][...], NT, preferred_element_type=f32)
            dkat_all[:, kc] += lax.dot_general(qat_ref[0, :, qc], dsbuf[b][...], NT, preferred_element_type=f32)
            dqt_all[:, qc] += jnp.dot(kat_ref[0, :, kc], dsbuf[b][...], preferred_element_type=f32)

        def step(n, b, masked):
            i, j = pair(n)
            qc, kr = _tile_cols(i, t), _tile_cols(j, t)
            i1, j1 = pair(n - 1)
            qc1, kc1 = _tile_cols(i1, t), _tile_cols(j1, t)
            c = 1 - b
            st = jnp.dot(ka_ref[0, kr, :], qat_ref[0, :, qc], preferred_element_type=f32)
            dvt_all[:, kc1] += lax.dot_general(dot_ref[0, :, qc1], pbuf[c][...], NT, preferred_element_type=f32)
            if masked:
                st = jnp.where(_causal_keep(t), st, NEG_INF)
            pt = jnp.exp(st - lse_ref[0, :, qc])
            pbuf[b][...] = pt.astype(bf16)
            dpt = jnp.dot(v_ref[0, kr, :], dot_ref[0, :, qc], preferred_element_type=f32)
            dkat_all[:, kc1] += lax.dot_general(qat_ref[0, :, qc1], dsbuf[c][...], NT, preferred_element_type=f32)
            dqt_all[:, qc1] += jnp.dot(kat_ref[0, :, kc1], dsbuf[c][...], preferred_element_type=f32)
            dsbuf[b][...] = (pt * (dpt - dl_ref[0, :, qc])).astype(bf16)

        probs(0, 0, True)
        step(1, 1, True)

        def four_steps(n, masked):
            step(n, 0, masked)
            step(n + 1, 1, masked)
            step(n + 2, 0, masked)
            step(n + 3, 1, masked)

        def diag_quads(d, _):
            four_steps(2 + 4 * d, True)
            return 0

        lax.fori_loop(0, (nt - 2) // 4, diag_quads, 0)
        if (nt - 2) % 4:
            step(nt - 2, 0, True)
            step(nt - 1, 1, True)

        def off_quads(d, _):
            four_steps(nt + 4 * d, False)
            return 0

        quads = extra // 4
        lax.fori_loop(0, quads, off_quads, 0)

        def off_steps(d, _):
            n = nt + 4 * quads + 2 * d
            step(n, 0, False)
            step(n + 1, 1, False)
            return 0

        lax.fori_loop(0, (extra % 4) // 2, off_steps, 0)

        @pl.when(extra % 2 == 1)
        def _():
            step(total - 1, 0, False)
            grads(total - 1, 0)

        @pl.when(extra % 2 == 0)
        def _():
            grads(total - 1, 1)

        dq_ref[0] = (dqt_all[0:HEAD_DIM, :] * SCALE).astype(bf16)
        dk_ref[0] = dkat_all[0:HEAD_DIM, :].astype(bf16)
        dv_ref[0] = dvt_all[...].astype(bf16)
        dcq_ref[0] = dqt_all[cq_slot:cq_slot + 1, :]
        dck_ref[0] = dkat_all[ck_slot:ck_slot + 1, :]

    smem = pl.BlockSpec(memory_space=pltpu.SMEM)
    rows = pl.BlockSpec((1, s_len, AUG), lambda h: (h, 0, 0))
    feat = pl.BlockSpec((1, AUG, s_len), lambda h: (h, 0, 0))
    feat64 = pl.BlockSpec((1, HEAD_DIM, s_len), lambda h: (h, 0, 0))
    rowv = pl.BlockSpec((1, 1, s_len), lambda h: (h, 0, 0))
    return pl.pallas_call(
        body,
        name="fox_bwd",
        grid=(nh,),
        in_specs=[smem, smem, smem, rows, feat, pl.BlockSpec((1, s_len, HEAD_DIM), lambda h: (h, 0, 0)), feat, feat64,
                  rowv, rowv],
        out_specs=[feat64, feat64, feat64, rowv, rowv],
        out_shape=[_sds((nh, HEAD_DIM, s_len), bf16)] * 3 + [_sds((nh, 1, s_len), f32)] * 2,
        scratch_shapes=[pltpu.VMEM((AUG, s_len), f32), pltpu.VMEM((AUG, s_len), f32), pltpu.VMEM((HEAD_DIM, s_len), f32)]
                       + [pltpu.VMEM((t, t), bf16)] * 4,
        compiler_params=_params(("arbitrary",)),
    )(npairs, pair_q, pair_k, ka, kat, v, qat, dot, lse_row, dl_row)


def _swa_bwd_call(qt, k, kt, v, dot, lse, dl, bias_t, bias0_t, sink):
    s_len = qt.shape[2]
    ts = SWA_TS
    nb = ts // BLOCK
    nsteps = s_len // ts

    def body(qt_ref, kc_ref, kp_ref, ktc_ref, ktp_ref, vc_ref, vp_ref, dot_ref, lse_ref, dl_ref, b_ref, b0_ref,
             sink_ref, dq_ref, dk_ref, dv_ref, dbias_ref, dsink_ref, dk_s, dv_s, tail_k, tail_v, sk_s):
        n = pl.program_id(0)

        @pl.when(n == 0)
        def _():
            dbias_ref[...] = jnp.zeros_like(dbias_ref)
            sk_s[...] = jnp.zeros_like(sk_s)

        @pl.when(n < nsteps)
        def _():
            first = n == 0
            dk_s[...] = jnp.zeros_like(dk_s)
            dv_s[...] = jnp.zeros_like(dv_s)
            groups = range(SWA_KV_HEADS)
            kall = [jnp.concatenate([kp_ref[g], kc_ref[g]], axis=0) for g in groups]
            vall = [jnp.concatenate([vp_ref[g], vc_ref[g]], axis=0) for g in groups]
            ktall = [jnp.concatenate([ktp_ref[g], ktc_ref[g]], axis=1) for g in groups]
            sinks = [_sink_row(sink_ref, g) for g in groups]
            items = [(g, b) for g in groups for b in range(nb)]

            def products(g, b):
                cols = slice(b * BLOCK, (b + 1) * BLOCK)
                win = slice(b * BLOCK, (b + 2) * BLOCK)
                qg = _group_lanes(qt_ref, g, cols)
                dog = _group_lanes(dot_ref, g, cols)
                bias_b = b_ref[g]
                if b == 0:
                    bias_b = jnp.where(first, b0_ref[g], bias_b)
                st = jnp.dot(kall[g][win], qg, preferred_element_type=f32) + bias_b
                dpt = jnp.dot(vall[g][win], dog, preferred_element_type=f32)
                return qg, dog, st, dpt

            def finish(g, b, qg, dog, st, dpt):
                cols = slice(b * BLOCK, (b + 1) * BLOCK)
                win = slice(b * BLOCK, (b + 2) * BLOCK)
                lse_r = _group_lanes(lse_ref, g, cols)
                dl_r = _group_lanes(dl_ref, g, cols)
                pt = jnp.exp(st - lse_r)
                dst = pt * (dpt - dl_r)
                dsb = dst.astype(bf16)
                dk_s[g, :, win] += lax.dot_general(qg, dsb, NT, preferred_element_type=f32)
                dv_s[g, :, win] += lax.dot_general(dog, pt.astype(bf16), NT, preferred_element_type=f32)
                dqg = jnp.dot(ktall[g][:, win], dsb, preferred_element_type=f32) * SCALE
                return dqg, dst, -jnp.exp(sinks[g] - lse_r) * dl_r

            dqs, dsts, sks = {}, {}, {}
            nxt = products(*items[0])
            for idx, (g, b) in enumerate(items):
                cur = nxt
                if idx + 1 < len(items):
                    nxt = products(*items[idx + 1])
                dqs[g, b], dsts[g, b], sks[g, b] = finish(g, b, *cur)
            for g in groups:
                dbias_ref[g] += functools.reduce(lambda a, c: a + c, [dsts[g, b] for b in range(nb)])
                sk_s[g] += functools.reduce(lambda a, c: a + c, [sks[g, b] for b in range(nb)])
                for hh in range(SWA_GROUP):
                    lanes = slice(hh * BLOCK, (hh + 1) * BLOCK)
                    dq_ref[g * SWA_GROUP + hh] = jnp.concatenate(
                        [dqs[g, b][:, lanes] for b in range(nb)], axis=1).astype(bf16)

        @pl.when(n > 0)
        def _():
            last = slice(ts - BLOCK, ts)
            for g in range(SWA_KV_HEADS):
                add_k = jnp.where(n < nsteps, dk_s[g, :, 0:BLOCK], 0.0)
                add_v = jnp.where(n < nsteps, dv_s[g, :, 0:BLOCK], 0.0)
                dk_ref[g, :, 0:ts - BLOCK] = tail_k[g, :, 0:ts - BLOCK].astype(bf16)
                dv_ref[g, :, 0:ts - BLOCK] = tail_v[g, :, 0:ts - BLOCK].astype(bf16)
                dk_ref[g, :, last] = (tail_k[g, :, last] + add_k).astype(bf16)
                dv_ref[g, :, last] = (tail_v[g, :, last] + add_v).astype(bf16)

        @pl.when(n < nsteps)
        def _():
            tail_k[...] = dk_s[:, :, BLOCK:]
            tail_v[...] = dv_s[:, :, BLOCK:]

        @pl.when(n == nsteps)
        def _():
            row = lax.broadcasted_iota(jnp.int32, (SWA_HEADS, 128), 0)
            out = jnp.zeros((SWA_HEADS, 128), f32)
            for h in range(SWA_HEADS):
                g, hh = divmod(h, SWA_GROUP)
                val = jnp.sum(sk_s[g, :, hh * BLOCK:(hh + 1) * BLOCK], axis=1, keepdims=True)
                out = jnp.where(row == h, val, out)
            dsink_ref[...] = out

    last_step = nsteps - 1

    def cl(n):
        return jnp.minimum(n, last_step)

    def prev_blk(n):
        return jnp.maximum(cl(n) * nb - 1, 0)

    feat8 = pl.BlockSpec((SWA_HEADS, HEAD_DIM, ts), lambda n: (0, 0, cl(n)))
    rows8 = pl.BlockSpec((SWA_HEADS, 1, ts), lambda n: (0, 0, cl(n)))
    cur = pl.BlockSpec((SWA_KV_HEADS, ts, HEAD_DIM), lambda n: (0, cl(n), 0))
    prev = pl.BlockSpec((SWA_KV_HEADS, BLOCK, HEAD_DIM), lambda n: (0, prev_blk(n), 0))
    curt = pl.BlockSpec((SWA_KV_HEADS, HEAD_DIM, ts), lambda n: (0, 0, cl(n)))
    prevt = pl.BlockSpec((SWA_KV_HEADS, HEAD_DIM, BLOCK), lambda n: (0, 0, prev_blk(n)))
    bspec = pl.BlockSpec((SWA_KV_HEADS, 2 * BLOCK, SWA_W), lambda n: (0, 0, 0))
    kvout = pl.BlockSpec((SWA_KV_HEADS, HEAD_DIM, ts), lambda n: (0, 0, jnp.maximum(n - 1, 0)))
    return pl.pallas_call(
        body,
        name="swa_bwd",
        grid=(nsteps + 1,),
        in_specs=[feat8, cur, prev, curt, prevt, cur, prev, feat8, rows8, rows8, bspec, bspec,
                  pl.BlockSpec(memory_space=pltpu.SMEM)],
        out_specs=[feat8, kvout, kvout, bspec, pl.BlockSpec((SWA_HEADS, 128), lambda n: (0, 0))],
        out_shape=[_sds((SWA_HEADS, HEAD_DIM, s_len), bf16), _sds((SWA_KV_HEADS, HEAD_DIM, s_len), bf16),
                   _sds((SWA_KV_HEADS, HEAD_DIM, s_len), bf16),
                   _sds((SWA_KV_HEADS, 2 * BLOCK, SWA_W), f32), _sds((SWA_HEADS, 128), f32)],
        scratch_shapes=[pltpu.VMEM((SWA_KV_HEADS, HEAD_DIM, ts + BLOCK), f32),
                        pltpu.VMEM((SWA_KV_HEADS, HEAD_DIM, ts + BLOCK), f32),
                        pltpu.VMEM((SWA_KV_HEADS, HEAD_DIM, ts), f32),
                        pltpu.VMEM((SWA_KV_HEADS, HEAD_DIM, ts), f32),
                        pltpu.VMEM((SWA_KV_HEADS, 1, SWA_W), f32)],
        compiler_params=_params(("arbitrary",)),
    )(qt, k, k, kt, kt, v, v, dot, lse, dl, bias_t, bias0_t, sink)


def _dproj_specs(tm):
    half = pl.BlockSpec((tm, 512), lambda i: (i, 0))
    feat = pl.BlockSpec((512, tm), lambda i: (0, i))
    feat_kv = pl.BlockSpec((128, tm), lambda i: (0, i))
    return [feat, feat, feat, half, feat, feat_kv, feat_kv, half, feat_kv]


def _dx_exchange_call(dh, pieces, w_t, bs, tm):
    s_len = dh.shape[0]
    n = len(bs)
    last = s_len // tm - 1

    def body(*refs):
        dh_ref, dqf_ref, dkf_ref, dvf_ref, dfz_ref, dqs_ref, dks_ref, dvs_ref, dsz_ref, dfft_ref, w_ref = refs[:11]
        b_refs = refs[11:11 + n]
        dx_ref = refs[11 + n]
        r_refs = refs[12 + n:12 + 2 * n]
        sems = refs[12 + 2 * n:]
        i = pl.program_id(0)

        @pl.when(i == 0)
        def _():
            _exchange_start(b_refs, r_refs, sems)

        def tr(ref):
            return ref[...].astype(f32).T.astype(bf16)

        dp = jnp.concatenate([tr(dqf_ref), tr(dkf_ref), tr(dvf_ref), dfz_ref[...], tr(dqs_ref), tr(dks_ref),
                              tr(dvs_ref), dsz_ref[...], tr(dfft_ref)], axis=1)
        dx_ref[...] = ALPHA * dh_ref[...] + jnp.dot(dp, w_ref[...], preferred_element_type=f32)

        @pl.when(i == last)
        def _():
            _exchange_wait(b_refs, r_refs, sems)

    fullw = pl.BlockSpec((tm, D_MODEL), lambda i: (i, 0))
    any_spec = pl.BlockSpec(memory_space=pl.ANY)
    out = pl.pallas_call(
        body,
        name="dx_bwd_exchange",
        grid=(s_len // tm,),
        in_specs=[fullw] + _dproj_specs(tm) + [pl.BlockSpec((A_W, D_MODEL), lambda i: (0, 0))] + [any_spec] * n,
        out_specs=[fullw] + [any_spec] * n,
        out_shape=[_sds((s_len, D_MODEL), f32)] + [_sds(b.shape, b.dtype) for b in bs],
        scratch_shapes=[pltpu.SemaphoreType.DMA((7 * n,)), pltpu.SemaphoreType.DMA((7 * n,)),
                        pltpu.SemaphoreType.DMA((n,))],
        compiler_params=_params(("arbitrary",)),
    )(dh, *pieces, w_t, *bs)
    return out[0], out[1:]


DW_STAGE_ROWS = 384


def _dw_call(x2, pieces, tm):
    s_len = x2.shape[0]
    nt = s_len // tm

    def body(x_ref, dqf_ref, dkf_ref, dvf_ref, dfz_ref, dqs_ref, dks_ref, dvs_ref, dsz_ref, dfft_ref, dw_ref,
             acc_ref, stage_ref, sem):
        i = pl.program_id(0)

        @pl.when(i == 0)
        def _():
            acc_ref[...] = jnp.zeros_like(acc_ref)

        xb = x_ref[...].astype(bf16)

        def add_feat(off, lhs):
            acc_ref[off:off + lhs.shape[0], :] += jnp.dot(lhs, xb, preferred_element_type=f32)

        def add_rows(off, piece):
            acc_ref[off:off + piece.shape[1], :] += lax.dot_general(piece, xb, TN, preferred_element_type=f32)

        add_feat(A_FQ, dqf_ref[...])
        add_feat(A_FK, dkf_ref[...])
        add_feat(A_FV, dvf_ref[...])
        add_rows(A_FZ, dfz_ref[...])
        add_feat(A_SQ, dqs_ref[...])
        add_feat(A_SK, dks_ref[...])
        add_feat(A_SV, dvs_ref[...])
        add_rows(A_SZ, dsz_ref[...])
        add_feat(A_FF, dfft_ref[...].astype(bf16))

        @pl.when(i == nt - 1)
        def _():
            for r in range(A_W // DW_STAGE_ROWS):
                rows = slice(r * DW_STAGE_ROWS, (r + 1) * DW_STAGE_ROWS)
                stage_ref[...] = acc_ref[rows, :].astype(bf16)
                cp = pltpu.make_async_copy(stage_ref, dw_ref.at[rows, :], sem)
                cp.start()
                cp.wait()

    return pl.pallas_call(
        body,
        name="dw_in_bwd",
        grid=(nt,),
        in_specs=[pl.BlockSpec((tm, D_MODEL), lambda i: (i, 0))] + _dproj_specs(tm),
        out_specs=pl.BlockSpec(memory_space=pl.ANY),
        out_shape=_sds((A_W, D_MODEL), bf16),
        scratch_shapes=[pltpu.VMEM((A_W, D_MODEL), f32), pltpu.VMEM((DW_STAGE_ROWS, D_MODEL), bf16),
                        pltpu.SemaphoreType.DMA],
        compiler_params=_params(("arbitrary",), VMEM_LIMIT_BIG),
    )(x2, *pieces)


def _adam_call(recv, w, m, v, tc, name):
    rows, cols = w.shape

    def body(r_ref, w_ref, m_ref, v_ref, g_ref, d_ref, mo_ref, vo_ref):
        g = r_ref[0].astype(f32)
        for p in range(1, N_DEV):
            g = g + r_ref[p].astype(f32)
        mn = ADAM_B1 * m_ref[...] + (1.0 - ADAM_B1) * g
        vn = ADAM_B2 * v_ref[...] + (1.0 - ADAM_B2) * (g * g)
        m_hat = mn / (1.0 - ADAM_B1 ** ADAM_STEP)
        v_hat = vn / (1.0 - ADAM_B2 ** ADAM_STEP)
        g_ref[...] = g
        d_ref[...] = -ADAM_LR * (m_hat / (jnp.sqrt(v_hat) + ADAM_EPS) + ADAM_WD * w_ref[...])
        mo_ref[...] = mn
        vo_ref[...] = vn

    blk = pl.BlockSpec((rows, tc), lambda i: (0, i))
    return pl.pallas_call(
        body,
        name=name,
        grid=(cols // tc,),
        in_specs=[pl.BlockSpec((N_DEV, rows, tc), lambda i: (0, 0, i)), blk, blk, blk],
        out_specs=[blk] * 4,
        out_shape=[_sds((rows, cols), f32)] * 4,
        compiler_params=_params(("arbitrary",)),
    )(recv, w, m, v)


def _pad_cols(a, width=128):
    return jnp.pad(a, ((0, 0), (0, width - a.shape[1])))


def _pack_small(ln_g, ln_b, rel, b_f, sink):
    return jnp.concatenate([
        ln_g.reshape(8, 128), ln_b.reshape(8, 128), _pad_cols(rel),
        jnp.pad(_pad_cols(b_f), ((0, 7), (0, 0))), jnp.pad(_pad_cols(sink), ((0, 7), (0, 0)))], axis=0)


def _unpack_small(p):
    return (p[0:8].reshape(1, D_MODEL), p[8:16].reshape(1, D_MODEL), p[16:48, 0:8], p[48:49, 0:8], p[56:57, 0:8])


def kernel(x, w_in, b_f, rel_bias, sink, w_o, ln_g, ln_b, loss_target, m_w_in, m_b_f, m_rel_bias, m_sink, m_w_o, m_ln_g, m_ln_b, v_w_in, v_b_f, v_rel_bias, v_sink, v_w_o, v_ln_g, v_ln_b):
    x2 = x[0]
    tgt = loss_target[0]
    s_len = x2.shape[0]
    shard = w_in.shape[2]

    w_in_t = jnp.transpose(w_in[0])
    g_in, g_o = _gather_call([w_in_t.astype(bf16), w_o[0].astype(bf16)])
    wt_full = g_in.reshape(N_DEV * shard, D_MODEL)
    w_t = jnp.concatenate([wt_full[:O_FF0], wt_full[O_FF1:], wt_full[O_FF0:O_FF1],
                           jnp.zeros((A_W - D_IN, D_MODEL), bf16)], axis=0)
    wo_full = g_o.reshape(D_MODEL, D_MODEL)

    qft, kft, vf, fz, qst, ks, vs, sz, fft, vat, kst, vsta = _proj_call(x2, w_t, 512)
    cum, sgm = _cum_call(fft, b_f.reshape(FOX_HEADS, 1))
    qat, ka, kat, tile_stats = _augment_call(qft, kft, cum.reshape(FOX_HEADS, 1, s_len), 2048)
    npairs, pair_q, pair_k = _fox_prune_tables(tile_stats)
    o_ft, lse_f = _fox_fwd_call(qat, ka, vat, npairs, pair_q, pair_k)
    bucket_t = jnp.asarray(_t5_bucket_table().T)
    bias_t, bias0_t = _swa_bias_call(rel_bias, bucket_t)
    sink_v = sink.reshape(SWA_HEADS)
    o_st, lse_s = _swa_fwd_call(qst, ks, vsta, bias_t, bias0_t, sink_v)

    (dh, do_f, dfz, do_s, dsz, dl_f, dl_s, dwo, dg, db, loss_part) = _post_call(
        o_ft.reshape(FOX_HEADS * HEAD_DIM, s_len), fz, o_st.reshape(SWA_HEADS * HEAD_DIM, s_len), sz, x2, tgt,
        wo_full, ln_g, ln_b, jnp.asarray(_head_selector()).astype(bf16), 512)

    dqf, dkf, dvf, dcq, dck = _fox_bwd_call(ka, kat, vf, qat, do_f.reshape(FOX_HEADS, HEAD_DIM, s_len), lse_f,
                                            dl_f.reshape(FOX_HEADS, 1, s_len), npairs, pair_q, pair_k)
    dqf, dkf, dvf = (a.reshape(FOX_HEADS * HEAD_DIM, s_len) for a in (dqf, dkf, dvf))
    dfft, dbf = _cum_bwd_call(dcq.reshape(FOX_HEADS, s_len), dck.reshape(FOX_HEADS, s_len), sgm)
    dqs, dks, dvs, dbias, dsink = _swa_bwd_call(
        qst, ks, kst, vs, do_s.reshape(SWA_HEADS, HEAD_DIM, s_len), lse_s, dl_s.reshape(SWA_HEADS, 1, s_len),
        bias_t, bias0_t, sink_v)
    dqs = dqs.reshape(SWA_HEADS * HEAD_DIM, s_len)
    dks, dvs = (a.reshape(SWA_KV_HEADS * HEAD_DIM, s_len) for a in (dks, dvs))
    drel = _swa_bias_bwd_call(dbias, bucket_t)

    pieces = (dqf, dkf, dvf, dfz, dqs, dks, dvs, dsz, dfft)
    dw_t = _dw_call(x2, pieces, 1024)

    dwt_full = jnp.concatenate([dw_t[:O_FF0], dw_t[A_FF:A_FF + (O_FF1 - O_FF0)], dw_t[O_FF0:A_FF]], axis=0)
    dw_blocks = dwt_full.reshape(N_DEV, shard, D_MODEL)
    dwo_blocks = dwo.reshape(N_DEV, D_MODEL // N_DEV, D_MODEL).astype(bf16)
    small = _pack_small(dg, db, drel[:, 0:8], dbf[:, 0].reshape(1, 8), dsink[:, 0].reshape(1, 8))
    loss_slot = np.zeros((64, 128), bool)
    loss_slot[49, 0] = True
    small = jnp.where(jnp.asarray(loss_slot), loss_part[0, 0], small)
    small_blocks = jnp.broadcast_to(small[None], (N_DEV,) + small.shape)
    dx, (r_in, r_o, r_small) = _dx_exchange_call(dh, pieces, w_t, [dw_blocks, dwo_blocks, small_blocks], 256)

    win_t = [jnp.transpose(a) for a in _adam_call(
        r_in, w_in_t, jnp.transpose(m_w_in[0]), jnp.transpose(v_w_in[0]), 256, "adam_w_in")]
    g_win, d_win, nm_win, nv_win = win_t
    g_wo, d_wo, nm_wo, nv_wo = _adam_call(r_o, w_o[0], m_w_o[0], v_w_o[0], 256, "adam_w_o")
    p_w = _pack_small(ln_g, ln_b, rel_bias, b_f, sink)
    p_m = _pack_small(m_ln_g, m_ln_b, m_rel_bias, m_b_f, m_sink)
    p_v = _pack_small(v_ln_g, v_ln_b, v_rel_bias, v_b_f, v_sink)
    g_p, d_p, nm_p, nv_p = _adam_call(r_small, p_w, p_m, p_v, 128, "adam_small")

    loss = g_p[49, 0]
    g_lng, g_lnb, g_rel, g_bf, g_sink = _unpack_small(g_p)
    d_lng, d_lnb, d_rel, d_bf, d_sink = _unpack_small(d_p)
    m_lng, m_lnb, m_rel, m_bf, m_sk = _unpack_small(nm_p)
    v_lng, v_lnb, v_rel, v_bf, v_sk = _unpack_small(nv_p)
    return (loss, dx[None], g_win[None], g_bf, g_rel, g_sink, g_wo[None], g_lng, g_lnb,
            d_win[None], d_bf, d_rel, d_sink, d_wo[None], d_lng, d_lnb,
            nm_win[None], m_bf, m_rel, m_sk, nm_wo[None], m_lng, m_lnb,
            nv_win[None], v_bf, v_rel, v_sk, nv_wo[None], v_lng, v_lnb)
```

```python
import functools
import math

import numpy as np
import jax
import jax.numpy as jnp
from jax import lax
from jax.experimental import pallas as pl
from jax.experimental.pallas import tpu as pltpu

f32 = jnp.float32
bf16 = jnp.bfloat16

D_MODEL = 1024
HEAD_DIM = 64
FOX_HEADS = 8
SWA_HEADS = 8
SWA_KV_HEADS = 2
SWA_GROUP = 4
BLOCK = 128
NUM_BUCKETS = 32
MAX_DISTANCE = 128
LN_EPS = 1e-5
NEG_INF = -1e30
ALPHA = 2.0 ** 0.25
SCALE = 1.0 / math.sqrt(HEAD_DIM)
D_IN = 3336

ADAM_LR = 0.001
ADAM_B1 = 0.9
ADAM_B2 = 0.999
ADAM_EPS = 1e-08
ADAM_WD = 0.01
ADAM_STEP = 10

N_DEV = 8
A_FQ, A_FK, A_FV, A_FZ, A_SQ, A_SK, A_SV, A_SZ, A_FF, A_W = 0, 512, 1024, 1536, 2048, 2560, 2688, 2816, 3328, 3456
O_FF0, O_FF1 = 1536, 1544

VMEM_LIMIT = 48 * 1024 * 1024
HIGHEST = lax.Precision.HIGHEST
NT = (((1,), (1,)), ((), ()))
TN = (((0,), (0,)), ((), ()))
MESH = pl.DeviceIdType.MESH
RELS = [(0, 0, 1), (0, 1, 0), (0, 1, 1), (1, 0, 0), (1, 0, 1), (1, 1, 0), (1, 1, 1)]


VMEM_LIMIT_BIG = 60 * 1024 * 1024


def _params(sem=None, vmem=VMEM_LIMIT):
    return pltpu.CompilerParams(dimension_semantics=sem, vmem_limit_bytes=vmem)


def _sds(shape, dtype):
    return jax.ShapeDtypeStruct(shape, dtype)


def _t5_bucket_table():
    qi = np.arange(BLOCK)[:, None]
    kj = np.arange(2 * BLOCK)[None, :]
    rel = qi + BLOCK - kj
    band = (rel >= 0) & (rel < BLOCK)
    relc = np.maximum(rel, 0)
    max_exact = NUM_BUCKETS // 2
    relf = np.maximum(relc, 1).astype(np.float32)
    large = max_exact + (np.log(relf / np.float32(max_exact)) / np.float32(math.log(MAX_DISTANCE / max_exact))
                         * np.float32(NUM_BUCKETS - max_exact)).astype(np.int32)
    large = np.minimum(large, NUM_BUCKETS - 1)
    bucket = np.where(relc < max_exact, relc, large).astype(np.int32)
    bucket = np.where(band, bucket, -1).astype(np.int32)
    return bucket


def _mesh_pos():
    return lax.axis_index("x"), lax.axis_index("y"), lax.axis_index("c")


def _dev_index(p):
    return 4 * p[0] + 2 * p[1] + p[2]


def _gather_call(xs):
    n = len(xs)

    def body(*refs):
        x_refs, o_refs = refs[:n], refs[n:2 * n]
        send_sems, recv_sems, local_sems = refs[2 * n:]
        x, y, c = _mesh_pos()
        me, sib = (x, y, c), (x, y, 1 - c)
        chips = [(1 - x, y), (x, 1 - y), (1 - x, 1 - y)]

        def copy(a, k, block, to, src=None):
            slot = o_refs[a].at[_dev_index(block)]
            return pltpu.make_async_remote_copy(
                src_ref=slot if src is None else src, dst_ref=slot,
                send_sem=send_sems.at[a * 7 + k], recv_sem=recv_sems.at[a * 7 + k],
                device_id=to, device_id_type=MESH)

        mine = [pltpu.make_async_copy(x_refs[a], o_refs[a].at[_dev_index(me)], local_sems.at[a]) for a in range(n)]
        for cp in mine:
            cp.start()
        first = []
        for a in range(n):
            first.append(copy(a, 0, me, sib, src=x_refs[a]))
            first += [copy(a, 1 + j, me, (*chip, c), src=x_refs[a]) for j, chip in enumerate(chips)]
        for cp in first:
            cp.start()
        passed = []
        for j, chip in enumerate(chips):
            for a in range(n):
                copy(a, 1 + j, (*chip, c), me).wait_recv()
                fwd = copy(a, 4 + j, (*chip, c), sib)
                fwd.start()
                passed.append(fwd)
        for a in range(n):
            copy(a, 0, sib, me).wait_recv()
            for j, chip in enumerate(chips):
                copy(a, 4 + j, (*chip, 1 - c), me).wait_recv()
        for cp in first + passed:
            cp.wait_send()
        for cp in mine:
            cp.wait()

    any_spec = pl.BlockSpec(memory_space=pl.ANY)
    return pl.pallas_call(
        body,
        name="gather_weights",
        out_shape=[_sds((N_DEV,) + a.shape, a.dtype) for a in xs],
        in_specs=[any_spec] * n,
        out_specs=[any_spec] * n,
        scratch_shapes=[pltpu.SemaphoreType.DMA((7 * n,)), pltpu.SemaphoreType.DMA((7 * n,)),
                        pltpu.SemaphoreType.DMA((n,))],
    )(*xs)


def _exchange_copies(b_refs, r_refs, send_sems, recv_sems, local_sems, incoming):
    n = len(b_refs)
    x, y, c = _mesh_pos()
    me_idx = _dev_index((x, y, c))
    mine = [pltpu.make_async_copy(b_refs[a].at[me_idx], r_refs[a].at[me_idx], local_sems.at[a]) for a in range(n)]
    remote = []
    for k, r in enumerate(RELS):
        peer = ((1 - x) if r[0] else x, (1 - y) if r[1] else y, (1 - c) if r[2] else c)
        pidx = _dev_index(peer)
        for a in range(n):
            remote.append(pltpu.make_async_remote_copy(
                src_ref=b_refs[a].at[pidx], dst_ref=r_refs[a].at[pidx if incoming else me_idx],
                send_sem=send_sems.at[a * 7 + k], recv_sem=recv_sems.at[a * 7 + k],
                device_id=peer, device_id_type=MESH))
    return mine, remote


def _exchange_start(b_refs, r_refs, sems):
    mine, out = _exchange_copies(b_refs, r_refs, *sems, incoming=False)
    for cp in mine + out:
        cp.start()


def _exchange_wait(b_refs, r_refs, sems):
    mine, inc = _exchange_copies(b_refs, r_refs, *sems, incoming=True)
    for cp in inc:
        cp.wait_recv()
    for cp in inc:
        cp.wait_send()
    for cp in mine:
        cp.wait()


def _proj_call(x2, w_t, tm):
    s_len = x2.shape[0]

    def body(x_ref, w_ref, qft_ref, kft_ref, vf_ref, fz_ref, qst_ref, ks_ref, vs_ref, sz_ref, fft_ref, vat_ref,
             kst_ref, vsta_ref):
        xb = x_ref[...].astype(bf16)

        def seg_t(off, width):
            return lax.dot_general(w_ref[off:off + width, :], xb, NT, preferred_element_type=f32)

        def seg(off, width):
            return lax.dot_general(xb, w_ref[off:off + width, :], NT, preferred_element_type=f32)

        def put_heads(ref, acc, nheads):
            for h in range(nheads):
                ref[h] = acc[:, h * HEAD_DIM:(h + 1) * HEAD_DIM].astype(bf16)

        def put_heads_t(ref, acc_t, nheads, augment):
            for h in range(nheads):
                ref[h, 0:HEAD_DIM, :] = acc_t[h * HEAD_DIM:(h + 1) * HEAD_DIM, :].astype(bf16)
                if augment:
                    ref[h, HEAD_DIM:2 * HEAD_DIM, :] = ones_row

        ones_row = jnp.where(lax.broadcasted_iota(jnp.int32, (HEAD_DIM, tm), 0) == 0, 1.0, 0.0).astype(bf16)
        put_heads_t(vat_ref, seg_t(A_FV, 512), FOX_HEADS, True)
        put_heads_t(qft_ref, seg_t(A_FQ, 512) * SCALE, FOX_HEADS, False)
        put_heads_t(kft_ref, seg_t(A_FK, 512), FOX_HEADS, False)
        put_heads(vf_ref, seg(A_FV, 512), FOX_HEADS)
        fz_ref[...] = seg(A_FZ, 512)
        put_heads_t(qst_ref, seg_t(A_SQ, 512) * SCALE, SWA_HEADS, False)
        put_heads(ks_ref, seg(A_SK, 128), SWA_KV_HEADS)
        put_heads(vs_ref, seg(A_SV, 128), SWA_KV_HEADS)
        put_heads_t(kst_ref, seg_t(A_SK, 128), SWA_KV_HEADS, False)
        put_heads_t(vsta_ref, seg_t(A_SV, 128), SWA_KV_HEADS, True)
        sz_ref[...] = seg(A_SZ, 512)
        fft_ref[...] = seg(A_FF, 128).T[:FOX_HEADS, :]

    def heads(nh):
        return pl.BlockSpec((nh, tm, HEAD_DIM), lambda i: (0, i, 0))

    def feat(nh, rows):
        return pl.BlockSpec((nh, rows, tm), lambda i: (0, 0, i))

    wide = pl.BlockSpec((tm, 512), lambda i: (i, 0))
    return pl.pallas_call(
        body,
        name="proj_fwd",
        grid=(s_len // tm,),
        in_specs=[pl.BlockSpec((tm, D_MODEL), lambda i: (i, 0)), pl.BlockSpec((A_W, D_MODEL), lambda i: (0, 0))],
        out_specs=[feat(8, HEAD_DIM), feat(8, HEAD_DIM), heads(8), wide, feat(8, HEAD_DIM), heads(2), heads(2), wide,
                   pl.BlockSpec((FOX_HEADS, tm), lambda i: (0, i)),
                   feat(FOX_HEADS, 2 * HEAD_DIM), feat(2, HEAD_DIM), feat(2, 2 * HEAD_DIM)],
        out_shape=[_sds((8, HEAD_DIM, s_len), bf16)] * 2 + [_sds((8, s_len, HEAD_DIM), bf16)]
                  + [_sds((s_len, 512), f32), _sds((8, HEAD_DIM, s_len), bf16),
                     _sds((2, s_len, HEAD_DIM), bf16), _sds((2, s_len, HEAD_DIM), bf16), _sds((s_len, 512), f32),
                     _sds((FOX_HEADS, s_len), f32), _sds((FOX_HEADS, 2 * HEAD_DIM, s_len), bf16),
                     _sds((2, HEAD_DIM, s_len), bf16), _sds((2, 2 * HEAD_DIM, s_len), bf16)],
        compiler_params=_params(("arbitrary",)),
    )(x2, w_t)


AUG = 2 * HEAD_DIM
NEAR_KEYS = 3


def _augment_call(q_t, k_t, cum_row, tm):
    nh, _, s_len = k_t.shape
    per_step = tm // FOX_T

    def body(qt_ref, kt_ref, c_ref, qat_ref, ka_ref, kat_ref, st_ref):
        c = c_ref[0]
        hi = c.astype(bf16).astype(f32)
        r1 = c - hi
        mid = r1.astype(bf16).astype(f32)
        lo = (r1 - mid).astype(bf16).astype(f32)
        row = lax.broadcasted_iota(jnp.int32, (HEAD_DIM, tm), 0)
        q_tail = jnp.where(row == 0, hi, jnp.where(row == 1, mid, jnp.where(row == 2, lo,
                           jnp.where(row < 6, 1.0, 0.0))))
        k_tail = jnp.where(row < 3, 1.0, jnp.where(row == 3, -hi, jnp.where(row == 4, -mid,
                           jnp.where(row == 5, -lo, 0.0))))
        qat_ref[0, 0:HEAD_DIM, :] = qt_ref[0]
        qat_ref[0, HEAD_DIM:AUG, :] = q_tail.astype(bf16)
        kat_ref[0, 0:HEAD_DIM, :] = kt_ref[0]
        kat_ref[0, HEAD_DIM:AUG, :] = k_tail.astype(bf16)
        qt = qt_ref[0].astype(f32)
        kt = kt_ref[0].astype(f32)
        ka_ref[0] = jnp.concatenate([kt, k_tail], axis=0).T.astype(bf16)
        qn2 = jnp.sum(qt * qt, axis=0, keepdims=True)
        kn2 = jnp.sum(kt * kt, axis=0, keepdims=True)
        sd = jnp.sum(qt * kt, axis=0, keepdims=True)
        k_and_c = jnp.concatenate([kt, jnp.broadcast_to(c, (8, tm))], axis=0)
        lane = lax.broadcasted_iota(jnp.int32, (1, tm), 1)
        for shift in range(1, NEAR_KEYS + 1):
            prev = pltpu.roll(k_and_c, shift, axis=1)
            near = jnp.sum(qt * prev[0:HEAD_DIM], axis=0, keepdims=True) + (c - prev[HEAD_DIM:HEAD_DIM + 1])
            sd = jnp.maximum(sd, jnp.where(lane >= shift, near, NEG_INF))
        srow = lax.broadcasted_iota(jnp.int32, (8, LANES), 0)
        for part in range(per_step):
            sl = slice(part * FOX_T, (part + 1) * FOX_T)
            vals = [jnp.sqrt(jnp.max(qn2[:, sl], axis=1, keepdims=True)),
                    jnp.sqrt(jnp.max(kn2[:, sl], axis=1, keepdims=True)),
                    jnp.min(sd[:, sl], axis=1, keepdims=True),
                    jnp.max(c[:, sl], axis=1, keepdims=True), jnp.min(c[:, sl], axis=1, keepdims=True)]
            out = jnp.zeros((8, LANES), f32)
            for r, val in enumerate(vals):
                out = jnp.where(srow == r, val, out)
            st_ref[0, part] = out

    tile_t = pl.BlockSpec((1, HEAD_DIM, tm), lambda h, i: (h, 0, i))
    return pl.pallas_call(
        body,
        name="fox_augment",
        grid=(nh, s_len // tm),
        in_specs=[tile_t, tile_t, pl.BlockSpec((1, 1, tm), lambda h, i: (h, 0, i))],
        out_specs=[pl.BlockSpec((1, AUG, tm), lambda h, i: (h, 0, i)),
                   pl.BlockSpec((1, tm, AUG), lambda h, i: (h, i, 0)),
                   pl.BlockSpec((1, AUG, tm), lambda h, i: (h, 0, i)),
                   pl.BlockSpec((1, per_step, 8, LANES), lambda h, i: (h, i, 0, 0))],
        out_shape=[_sds((nh, AUG, s_len), bf16), _sds((nh, s_len, AUG), bf16), _sds((nh, AUG, s_len), bf16),
                   _sds((nh, s_len // FOX_T, 8, LANES), f32)],
        compiler_params=_params(("arbitrary", "arbitrary")),
    )(q_t, k_t, cum_row)


FOX_PRUNE_GAP = 32.0


def _fox_prune_tables(stats):
    s = stats[:, :, :, 0]
    qn, kn, sd, cmx, cmn = (s[:, :, r] for r in range(5))
    nt = s.shape[1]
    bound = qn[:, :, None] * kn[:, None, :] + (cmx[:, :, None] - cmn[:, None, :])
    margin = 0.01 + 1e-5 * (jnp.abs(cmx)[:, :, None] + jnp.abs(cmn)[:, None, :])
    qi = lax.broadcasted_iota(jnp.int32, (nt, nt), 0)
    kj = lax.broadcasted_iota(jnp.int32, (nt, nt), 1)
    skip = (bound + margin < sd[:, :, None] - FOX_PRUNE_GAP) & (kj < qi)[None]
    first = jnp.sum(jnp.cumprod(skip.astype(jnp.int32), axis=2), axis=2)
    tiles = lax.broadcasted_iota(jnp.int32, (1, nt), 1)
    cnt = tiles - first
    ends = jnp.cumsum(cnt, axis=1)
    off = ends - cnt
    kmax = nt * (nt - 1) // 2
    k = lax.broadcasted_iota(jnp.int32, (1, kmax), 1)
    pair_q = jnp.minimum(jnp.sum((ends[:, None, :] <= k[:, :, None]).astype(jnp.int32), axis=2), nt - 1)
    hit = pair_q[:, :, None] == tiles[:, None, :]
    first_k = jnp.sum(jnp.where(hit, first[:, None, :], 0), axis=2)
    off_k = jnp.sum(jnp.where(hit, off[:, None, :], 0), axis=2)
    pair_k = jnp.clip(first_k + k - off_k, 0, nt - 1)
    return (ends[:, nt - 1].astype(jnp.int32), pair_q.reshape(-1).astype(jnp.int32),
            pair_k.reshape(-1).astype(jnp.int32))


CUM_CHUNK = 512


def _cum_call(fft, bf_col):
    s_len = fft.shape[1]
    ch = CUM_CHUNK

    def body(f_ref, b_ref, cum_ref, sg_ref):
        r = lax.broadcasted_iota(jnp.int32, (ch, ch), 0)
        c = lax.broadcasted_iota(jnp.int32, (ch, ch), 1)
        upper = (r <= c).astype(f32)
        carry = jnp.zeros((FOX_HEADS, 1), f32)
        for n in range(s_len // ch):
            z = f_ref[:, n * ch:(n + 1) * ch] + b_ref[...]
            logf = jnp.minimum(z, 0.0) - jnp.log1p(jnp.exp(-jnp.abs(z)))
            sg_ref[:, n * ch:(n + 1) * ch] = 1.0 / (1.0 + jnp.exp(z))
            cs = jnp.dot(logf, upper, precision=HIGHEST, preferred_element_type=f32) + carry
            cum_ref[:, n * ch:(n + 1) * ch] = cs
            carry = cs[:, ch - 1:ch]

    return pl.pallas_call(
        body,
        name="fox_cum_fwd",
        out_shape=[_sds((FOX_HEADS, s_len), f32)] * 2,
        compiler_params=_params(),
    )(fft, bf_col)


def _cum_bwd_call(dcq, dck, sg):
    s_len = sg.shape[1]
    ch = CUM_CHUNK
    nch = s_len // ch

    def body(q_ref, k_ref, sg_ref, dff_ref, dbf_ref):
        r = lax.broadcasted_iota(jnp.int32, (ch, ch), 0)
        c = lax.broadcasted_iota(jnp.int32, (ch, ch), 1)
        lower = (r >= c).astype(f32)
        dff_ref[...] = jnp.zeros_like(dff_ref)
        carry = jnp.zeros((FOX_HEADS, 1), f32)
        total = jnp.zeros((FOX_HEADS, 1), f32)
        for n in reversed(range(nch)):
            sl = slice(n * ch, (n + 1) * ch)
            dcum = q_ref[:, sl] - k_ref[:, sl]
            rs = jnp.dot(dcum, lower, precision=HIGHEST, preferred_element_type=f32) + carry
            carry = rs[:, 0:1]
            dff = rs * sg_ref[:, sl]
            dff_ref[0:FOX_HEADS, sl] = dff
            total = total + jnp.sum(dff, axis=1, keepdims=True)
        dbf_ref[...] = jnp.broadcast_to(total, (FOX_HEADS, 128))

    return pl.pallas_call(
        body,
        name="fox_cum_bwd",
        out_shape=[_sds((128, s_len), f32), _sds((FOX_HEADS, 128), f32)],
        compiler_params=_params(),
    )(dcq, dck, sg)


FOX_T = 512
LANES = 128


def _causal_keep(t):
    return lax.broadcasted_iota(jnp.int32, (t, t), 0) <= lax.broadcasted_iota(jnp.int32, (t, t), 1)


def _tile_cols(i, t):
    return pl.ds(pl.multiple_of(i * t, t), t)


def _fox_pair(n, nt, kmax, h, pq_ref, pk_ref):
    k = h * kmax + jnp.maximum(n - nt, 0)
    return jnp.where(n < nt, n, pq_ref[k]), jnp.where(n < nt, n, pk_ref[k])


def _fox_fwd_call(qat, ka, vat, npairs, pair_q, pair_k):
    nh, s_len, _ = ka.shape
    t = FOX_T
    nt = s_len // t
    kmax = nt * (nt - 1) // 2
    assert nt >= 2 and nt % 2 == 0

    def body(np_ref, pq_ref, pk_ref, qat_ref, ka_ref, vat_ref, o_ref, lse_ref, s0, s1, p0, p1, a0, a1, m_all, acc_all):
        h = pl.program_id(0)
        extra = np_ref[h]
        total = nt + extra
        m_all[...] = jnp.full(m_all.shape, NEG_INF, f32)
        acc_all[...] = jnp.zeros(acc_all.shape, f32)
        bufs = ((s0, p0, a0), (s1, p1, a1))

        def pair(n):
            return _fox_pair(n, nt, kmax, h, pq_ref, pk_ref)

        def scores(n, b, masked):
            i, j = pair(n)
            st = jnp.dot(ka_ref[0, _tile_cols(j, t), :], qat_ref[0, :, _tile_cols(i, t)], preferred_element_type=f32)
            if masked:
                st = jnp.where(_causal_keep(t), st, NEG_INF)
            bufs[b][0][...] = st

        def softmax(n, b):
            i, _ = pair(n)
            s_ref, p_ref, a_ref = bufs[b]
            for c in range(t // LANES):
                cols = slice(c * LANES, (c + 1) * LANES)
                mcols = pl.ds(pl.multiple_of(i * t + c * LANES, LANES), LANES)
                m_old = m_all[:, mcols]
                m_new = jnp.maximum(m_old, jnp.max(s_ref[:, cols], axis=0, keepdims=True))
                m_all[:, mcols] = m_new
                a_ref[:, cols] = jnp.exp(m_old - m_new)
                p_ref[:, cols] = jnp.exp(s_ref[:, cols] - m_new).astype(bf16)

        def accum(n, b):
            i, j = pair(n)
            cols = _tile_cols(i, t)
            acc_all[:, cols] = bufs[b][2][...] * acc_all[:, cols] + jnp.dot(
                vat_ref[0, :, _tile_cols(j, t)], bufs[b][1][...], preferred_element_type=f32)

        def step(n, b, masked):
            accum(n - 2, b)
            softmax(n - 1, 1 - b)
            scores(n, b, masked)

        scores(0, 0, True)
        scores(1, 1, True)
        softmax(0, 0)

        def diag_steps(d, _):
            n = 2 + 2 * d
            step(n, 0, True)
            step(n + 1, 1, True)
            return 0

        lax.fori_loop(0, (nt - 2) // 2, diag_steps, 0)

        def off_steps(d, _):
            n = nt + 2 * d
            step(n, 0, False)
            step(n + 1, 1, False)
            return 0

        lax.fori_loop(0, extra // 2, off_steps, 0)

        @pl.when(extra % 2 == 1)
        def _():
            step(total - 1, 0, False)
            softmax(total - 1, 0)
            accum(total - 2, 1)
            accum(total - 1, 0)

        @pl.when(extra % 2 == 0)
        def _():
            softmax(total - 1, 1)
            accum(total - 2, 0)
            accum(total - 1, 1)

        l = acc_all[HEAD_DIM:HEAD_DIM + 1, :]
        o_ref[0] = acc_all[0:HEAD_DIM, :] / l
        lse_ref[0] = m_all[...] + jnp.log(l)

    smem = pl.BlockSpec(memory_space=pltpu.SMEM)
    return pl.pallas_call(
        body,
        name="fox_fwd",
        grid=(nh,),
        in_specs=[smem, smem, smem,
                  pl.BlockSpec((1, AUG, s_len), lambda h: (h, 0, 0)),
                  pl.BlockSpec((1, s_len, AUG), lambda h: (h, 0, 0)),
                  pl.BlockSpec((1, AUG, s_len), lambda h: (h, 0, 0))],
        out_specs=[pl.BlockSpec((1, HEAD_DIM, s_len), lambda h: (h, 0, 0)),
                   pl.BlockSpec((1, 1, s_len), lambda h: (h, 0, 0))],
        out_shape=[_sds((nh, HEAD_DIM, s_len), f32), _sds((nh, 1, s_len), f32)],
        scratch_shapes=[pltpu.VMEM((t, t), f32), pltpu.VMEM((t, t), f32), pltpu.VMEM((t, t), bf16),
                        pltpu.VMEM((t, t), bf16), pltpu.VMEM((1, t), f32), pltpu.VMEM((1, t), f32),
                        pltpu.VMEM((1, s_len), f32), pltpu.VMEM((AUG, s_len), f32)],
        compiler_params=_params(("arbitrary",)),
    )(npairs, pair_q, pair_k, qat, ka, vat)


SWA_TS = 512


SWA_W = SWA_GROUP * BLOCK


def _swa_bias_call(rel_bias, bucket_t):
    def body(rb_ref, bk_ref, b_ref, b0_ref):
        bk = bk_ref[...]
        row = lax.broadcasted_iota(jnp.int32, (2 * BLOCK, BLOCK), 0)
        for h in range(SWA_HEADS):
            acc = jnp.full((2 * BLOCK, BLOCK), NEG_INF, f32)
            for b in range(NUM_BUCKETS):
                acc = jnp.where(bk == b, rb_ref[b, h], acc)
            g, hh = divmod(h, SWA_GROUP)
            b_ref[g, :, hh * BLOCK:(hh + 1) * BLOCK] = acc
            b0_ref[g, :, hh * BLOCK:(hh + 1) * BLOCK] = jnp.where(row < BLOCK, NEG_INF, acc)

    return pl.pallas_call(
        body,
        name="swa_bias",
        in_specs=[pl.BlockSpec(memory_space=pltpu.SMEM), pl.BlockSpec(memory_space=pltpu.VMEM)],
        out_shape=[_sds((SWA_KV_HEADS, 2 * BLOCK, SWA_W), f32)] * 2,
        compiler_params=_params(),
    )(rel_bias, bucket_t)


def _swa_bias_bwd_call(dbias, bucket_t):
    def body(d_ref, bk_ref, o_ref):
        bk = bk_ref[...]
        row = lax.broadcasted_iota(jnp.int32, (NUM_BUCKETS, 128), 0)
        col = lax.broadcasted_iota(jnp.int32, (NUM_BUCKETS, 128), 1)
        out = jnp.zeros((NUM_BUCKETS, 128), f32)
        for h in range(SWA_HEADS):
            g, hh = divmod(h, SWA_GROUP)
            d = d_ref[g, :, hh * BLOCK:(hh + 1) * BLOCK]
            for b in range(NUM_BUCKETS):
                val = jnp.sum(jnp.sum(jnp.where(bk == b, d, 0.0), axis=1, keepdims=True), axis=0, keepdims=True)
                out = jnp.where((row == b) & (col == h), val, out)
        o_ref[...] = out

    return pl.pallas_call(
        body,
        name="swa_bias_bwd",
        out_shape=_sds((NUM_BUCKETS, 128), f32),
        compiler_params=_params(),
    )(dbias, bucket_t)


def _sink_row(sink_ref, g):
    return jnp.concatenate([jnp.full((1, BLOCK), sink_ref[g * SWA_GROUP + hh], f32) for hh in range(SWA_GROUP)], axis=1)


def _group_lanes(ref, g, cols):
    return jnp.concatenate([ref[g * SWA_GROUP + hh, :, cols] for hh in range(SWA_GROUP)], axis=1)


def _swa_fwd_call(qt, k, vta, bias_t, bias0_t, sink):
    s_len = qt.shape[2]
    ts = SWA_TS
    nb = ts // BLOCK

    def body(qt_ref, kc_ref, kp_ref, vc_ref, vp_ref, b_ref, b0_ref, sink_ref, o_ref, lse_ref):
        first = pl.program_id(0) == 0
        kall = [jnp.concatenate([kp_ref[g], kc_ref[g]], axis=0) for g in range(SWA_KV_HEADS)]
        vall = [jnp.concatenate([vp_ref[g], vc_ref[g]], axis=1) for g in range(SWA_KV_HEADS)]
        sinks = [_sink_row(sink_ref, g) for g in range(SWA_KV_HEADS)]
        items = [(g, b) for g in range(SWA_KV_HEADS) for b in range(nb)]

        def scores(g, b):
            qg = _group_lanes(qt_ref, g, slice(b * BLOCK, (b + 1) * BLOCK))
            bias_b = b_ref[g]
            if b == 0:
                bias_b = jnp.where(first, b0_ref[g], bias_b)
            return jnp.dot(kall[g][b * BLOCK:(b + 2) * BLOCK], qg, preferred_element_type=f32) + bias_b

        def finish(g, b, st):
            m = jnp.maximum(jnp.max(st, axis=0, keepdims=True), sinks[g])
            pt = jnp.exp(st - m)
            acc = jnp.dot(vall[g][:, b * BLOCK:(b + 2) * BLOCK], pt.astype(bf16), preferred_element_type=f32)
            l = acc[HEAD_DIM:HEAD_DIM + 1, :] + jnp.exp(sinks[g] - m)
            return acc[0:HEAD_DIM, :] / l, m + jnp.log(l)

        outs, lses = {}, {}
        st_next = scores(*items[0])
        for idx, (g, b) in enumerate(items):
            st = st_next
            if idx + 1 < len(items):
                st_next = scores(*items[idx + 1])
            outs[g, b], lses[g, b] = finish(g, b, st)
        for g in range(SWA_KV_HEADS):
            for hh in range(SWA_GROUP):
                lanes = slice(hh * BLOCK, (hh + 1) * BLOCK)
                o_ref[g * SWA_GROUP + hh] = jnp.concatenate([outs[g, b][:, lanes] for b in range(nb)], axis=1)
                lse_ref[g * SWA_GROUP + hh] = jnp.concatenate([lses[g, b][:, lanes] for b in range(nb)], axis=1)

    def prev_blk(n):
        return jnp.maximum(n * nb - 1, 0)

    bspec = pl.BlockSpec((SWA_KV_HEADS, 2 * BLOCK, SWA_W), lambda n: (0, 0, 0))
    return pl.pallas_call(
        body,
        name="swa_fwd",
        grid=(s_len // ts,),
        in_specs=[pl.BlockSpec((SWA_HEADS, HEAD_DIM, ts), lambda n: (0, 0, n)),
                  pl.BlockSpec((SWA_KV_HEADS, ts, HEAD_DIM), lambda n: (0, n, 0)),
                  pl.BlockSpec((SWA_KV_HEADS, BLOCK, HEAD_DIM), lambda n: (0, prev_blk(n), 0)),
                  pl.BlockSpec((SWA_KV_HEADS, AUG, ts), lambda n: (0, 0, n)),
                  pl.BlockSpec((SWA_KV_HEADS, AUG, BLOCK), lambda n: (0, 0, prev_blk(n))),
                  bspec, bspec, pl.BlockSpec(memory_space=pltpu.SMEM)],
        out_specs=[pl.BlockSpec((SWA_HEADS, HEAD_DIM, ts), lambda n: (0, 0, n)),
                   pl.BlockSpec((SWA_HEADS, 1, ts), lambda n: (0, 0, n))],
        out_shape=[_sds((SWA_HEADS, HEAD_DIM, s_len), f32), _sds((SWA_HEADS, 1, s_len), f32)],
        compiler_params=_params(("arbitrary",)),
    )(qt, k, k, vta, vta, bias_t, bias0_t, sink)


def _head_selector():
    sel = np.zeros((512, 128), np.float32)
    for h in range(8):
        sel[h * HEAD_DIM:(h + 1) * HEAD_DIM, h] = 1.0
    return sel


def _post_call(of, fz, osw, sz, x2, tgt, wo, ln_g, ln_b, sel, tm):
    s_len = x2.shape[0]

    def body(of_ref, fz_ref, os_ref, sz_ref, x_ref, t_ref, wo_ref, g_ref, b_ref, sel_ref,
             dh_ref, dof_ref, dfz_ref, dos_ref, dsz_ref, dlf_ref, dls_ref, dwo_ref, dg_ref, db_ref, loss_ref):
        n = pl.program_id(0)

        @pl.when(n == 0)
        def _():
            dwo_ref[...] = jnp.zeros_like(dwo_ref)
            dg_ref[...] = jnp.zeros_like(dg_ref)
            db_ref[...] = jnp.zeros_like(db_ref)
            loss_ref[...] = jnp.zeros_like(loss_ref)

        gam = g_ref[...]
        sel_m = sel_ref[...]

        def forward(r):
            o_f = of_ref[:, r].T
            o_s = os_ref[:, r].T
            fz = fz_ref[r, :]
            sz = sz_ref[r, :]
            sg_f = jax.nn.sigmoid(fz)
            sg_s = jax.nn.sigmoid(sz)
            silu_f = fz * sg_f
            silu_s = sz * sg_s
            mixed = jnp.concatenate([o_f * silu_f, o_s * silu_s], axis=1).astype(bf16)
            y = jnp.dot(mixed, wo_ref[...], preferred_element_type=f32)
            return o_f, o_s, fz, sz, sg_f, sg_s, silu_f, silu_s, mixed, y

        def norm_and_back(r, fwd):
            mixed, y = fwd[8], fwd[9]
            h = ALPHA * x_ref[r, :] + y
            mu = jnp.mean(h, axis=1, keepdims=True)
            hc = h - mu
            var = jnp.mean(hc * hc, axis=1, keepdims=True)
            rstd = lax.rsqrt(var + LN_EPS)
            xhat = hc * rstd
            out = xhat * gam + b_ref[...]
            err = out - t_ref[r, :]
            tok_loss = jnp.mean(err * err, axis=1, keepdims=True)
            loss_ref[...] += 0.5 * jnp.sum(tok_loss, axis=0, keepdims=True)
            dout = err * (1.0 / D_MODEL)
            dg_ref[...] += jnp.sum(dout * xhat, axis=0, keepdims=True)
            db_ref[...] += jnp.sum(dout, axis=0, keepdims=True)
            dxh = dout * gam
            m1 = jnp.mean(dxh, axis=1, keepdims=True)
            m2 = jnp.mean(dxh * xhat, axis=1, keepdims=True)
            dh = rstd * (dxh - m1 - xhat * m2)
            dh_ref[r, :] = dh
            dyb = dh.astype(bf16)
            dmix = lax.dot_general(dyb, wo_ref[...], NT, preferred_element_type=f32)
            dwo_ref[...] += lax.dot_general(mixed, dyb, TN, preferred_element_type=f32)
            return dmix

        def head_sums(prod):
            hi = prod.astype(bf16)
            lo = (prod - hi.astype(f32)).astype(bf16)
            return (jnp.dot(hi, sel_m, preferred_element_type=f32) + jnp.dot(lo, sel_m, preferred_element_type=f32))

        def gates_back(r, fwd, dmix):
            o_f, o_s, fz, sz, sg_f, sg_s, silu_f, silu_s = fwd[:8]
            dm_f = dmix[:, :512]
            dm_s = dmix[:, 512:]
            do_f = dm_f * silu_f
            do_s = dm_s * silu_s
            dfz_ref[r, :] = (dm_f * o_f * (sg_f * (1.0 + fz * (1.0 - sg_f)))).astype(bf16)
            dsz_ref[r, :] = (dm_s * o_s * (sg_s * (1.0 + sz * (1.0 - sg_s)))).astype(bf16)
            dof_ref[:, r] = do_f.T.astype(bf16)
            dos_ref[:, r] = do_s.T.astype(bf16)
            dlf_ref[:, r] = head_sums(do_f * o_f).T[:FOX_HEADS, :]
            dls_ref[:, r] = head_sums(do_s * o_s).T[:SWA_HEADS, :]

        halves = [slice(k * (tm // 2), (k + 1) * (tm // 2)) for k in range(2)]
        fwds = [forward(r) for r in halves]
        dmixes = [norm_and_back(r, f) for r, f in zip(halves, fwds)]
        for r, f, d in zip(halves, fwds, dmixes):
            gates_back(r, f, d)

    feat = pl.BlockSpec((512, tm), lambda n: (0, n))
    rows8 = pl.BlockSpec((8, tm), lambda n: (0, n))
    half = pl.BlockSpec((tm, 512), lambda n: (n, 0))
    fullw = pl.BlockSpec((tm, D_MODEL), lambda n: (n, 0))
    vec = pl.BlockSpec((1, D_MODEL), lambda n: (0, 0))
    return pl.pallas_call(
        body,
        name="post_fwd_bwd",
        grid=(s_len // tm,),
        in_specs=[feat, half, feat, half, fullw, fullw,
                  pl.BlockSpec((D_MODEL, D_MODEL), lambda n: (0, 0)), vec, vec,
                  pl.BlockSpec((512, 128), lambda n: (0, 0))],
        out_specs=[fullw, feat, half, feat, half, rows8, rows8,
                   pl.BlockSpec((D_MODEL, D_MODEL), lambda n: (0, 0)), vec, vec,
                   pl.BlockSpec((1, 1), lambda n: (0, 0))],
        out_shape=[_sds((s_len, D_MODEL), f32), _sds((512, s_len), bf16), _sds((s_len, 512), bf16),
                   _sds((512, s_len), bf16), _sds((s_len, 512), bf16),
                   _sds((FOX_HEADS, s_len), f32), _sds((SWA_HEADS, s_len), f32),
                   _sds((D_MODEL, D_MODEL), f32), _sds((1, D_MODEL), f32), _sds((1, D_MODEL), f32),
                   _sds((1, 1), f32)],
        compiler_params=_params(("arbitrary",), VMEM_LIMIT_BIG),
    )(of, fz, osw, sz, x2, tgt, wo, ln_g, ln_b, sel)


def _fox_bwd_call(ka, kat, v, qat, dot, lse_row, dl_row, npairs, pair_q, pair_k):
    nh, s_len, _ = ka.shape
    t = FOX_T
    nt = s_len // t
    kmax = nt * (nt - 1) // 2
    assert nt >= 2 and nt % 2 == 0
    ck_slot = HEAD_DIM + 3
    cq_slot = HEAD_DIM

    def body(np_ref, pq_ref, pk_ref, ka_ref, kat_ref, v_ref, qat_ref, dot_ref, lse_ref, dl_ref,
             dq_ref, dk_ref, dv_ref, dcq_ref, dck_ref, dqt_all, dkat_all, dvt_all, p0, p1, ds0, ds1):
        h = pl.program_id(0)
        extra = np_ref[h]
        total = nt + extra
        dqt_all[...] = jnp.zeros(dqt_all.shape, f32)
        dkat_all[...] = jnp.zeros(dkat_all.shape, f32)
        dvt_all[...] = jnp.zeros(dvt_all.shape, f32)
        pbuf, dsbuf = (p0, p1), (ds0, ds1)

        def pair(n):
            return _fox_pair(n, nt, kmax, h, pq_ref, pk_ref)

        def probs(n, b, masked):
            i, j = pair(n)
            qc, kr = _tile_cols(i, t), _tile_cols(j, t)
            st = jnp.dot(ka_ref[0, kr, :], qat_ref[0, :, qc], preferred_element_type=f32)
            dpt = jnp.dot(v_ref[0, kr, :], dot_ref[0, :, qc], preferred_element_type=f32)
            if masked:
                st = jnp.where(_causal_keep(t), st, NEG_INF)
            pt = jnp.exp(st - lse_ref[0, :, qc])
            pbuf[b][...] = pt.astype(bf16)
            dsbuf[b][...] = (pt * (dpt - dl_ref[0, :, qc])).astype(bf16)

        def grads(n, b):
            i, j = pair(n)
            qc, kc = _tile_cols(i, t), _tile_cols(j, t)
            dvt_all[:, kc] += lax.dot_general(dot_ref[0, :, qc], pbuf[b][...], NT, preferred_element_type=f32)
            dkat_all[:, kc] += lax.dot_general(qat_ref[0, :, qc], dsbuf[b][...], NT, preferred_element_type=f32)
            dqt_all[:, qc] += jnp.dot(kat_ref[0, :, kc], dsbuf[b][...], preferred_element_type=f32)

        def step(n, b, masked):
            i, j = pair(n)
            qc, kr = _tile_cols(i, t), _tile_cols(j, t)
            i1, j1 = pair(n - 1)
            qc1, kc1 = _tile_cols(i1, t), _tile_cols(j1, t)
            c = 1 - b
            st = jnp.dot(ka_ref[0, kr, :], qat_ref[0, :, qc], preferred_element_type=f32)
            dvt_all[:, kc1] += lax.dot_general(dot_ref[0, :, qc1], pbuf[c][...], NT, preferred_element_type=f32)
            if masked:
                st = jnp.where(_causal_keep(t), st, NEG_INF)
            pt = jnp.exp(st - lse_ref[0, :, qc])
            pbuf[b][...] = pt.astype(bf16)
            dpt = jnp.dot(v_ref[0, kr, :], dot_ref[0, :, qc], preferred_element_type=f32)
            dkat_all[:, kc1] += lax.dot_general(qat_ref[0, :, qc1], dsbuf[c][...], NT, preferred_element_type=f32)
            dqt_all[:, qc1] += jnp.dot(kat_ref[0, :, kc1], dsbuf[c][...], preferred_element_type=f32)
            dsbuf[b][...] = (pt * (dpt - dl_ref[0, :, qc])).astype(bf16)

        probs(0, 0, True)
        step(1, 1, True)

        def four_steps(n, masked):
            step(n, 0, masked)
            step(n + 1, 1, masked)
            step(n + 2, 0, masked)
            step(n + 3, 1, masked)

        def diag_quads(d, _):
            four_steps(2 + 4 * d, True)
            return 0

        lax.fori_loop(0, (nt - 2) // 4, diag_quads, 0)
        if (nt - 2) % 4:
            step(nt - 2, 0, True)
            step(nt - 1, 1, True)

        def off_quads(d, _):
            four_steps(nt + 4 * d, False)
            return 0

        quads = extra // 4
        lax.fori_loop(0, quads, off_quads, 0)

        def off_steps(d, _):
            n = nt + 4 * quads + 2 * d
            step(n, 0, False)
            step(n + 1, 1, False)
            return 0

        lax.fori_loop(0, (extra % 4) // 2, off_steps, 0)

        @pl.when(extra % 2 == 1)
        def _():
            step(total - 1, 0, False)
            grads(total - 1, 0)

        @pl.when(extra % 2 == 0)
        def _():
            grads(total - 1, 1)

        dq_ref[0] = (dqt_all[0:HEAD_DIM, :] * SCALE).astype(bf16)
        dk_ref[0] = dkat_all[0:HEAD_DIM, :].astype(bf16)
        dv_ref[0] = dvt_all[...].astype(bf16)
        dcq_ref[0] = dqt_all[cq_slot:cq_slot + 1, :]
        dck_ref[0] = dkat_all[ck_slot:ck_slot + 1, :]

    smem = pl.BlockSpec(memory_space=pltpu.SMEM)
    rows = pl.BlockSpec((1, s_len, AUG), lambda h: (h, 0, 0))
    feat = pl.BlockSpec((1, AUG, s_len), lambda h: (h, 0, 0))
    feat64 = pl.BlockSpec((1, HEAD_DIM, s_len), lambda h: (h, 0, 0))
    rowv = pl.BlockSpec((1, 1, s_len), lambda h: (h, 0, 0))
    return pl.pallas_call(
        body,
        name="fox_bwd",
        grid=(nh,),
        in_specs=[smem, smem, smem, rows, feat, pl.BlockSpec((1, s_len, HEAD_DIM), lambda h: (h, 0, 0)), feat, feat64,
                  rowv, rowv],
        out_specs=[feat64, feat64, feat64, rowv, rowv],
        out_shape=[_sds((nh, HEAD_DIM, s_len), bf16)] * 3 + [_sds((nh, 1, s_len), f32)] * 2,
        scratch_shapes=[pltpu.VMEM((AUG, s_len), f32), pltpu.VMEM((AUG, s_len), f32), pltpu.VMEM((HEAD_DIM, s_len), f32)]
                       + [pltpu.VMEM((t, t), bf16)] * 4,
        compiler_params=_params(("arbitrary",)),
    )(npairs, pair_q, pair_k, ka, kat, v, qat, dot, lse_row, dl_row)


def _swa_bwd_call(qt, k, kt, v, dot, lse, dl, bias_t, bias0_t, sink):
    s_len = qt.shape[2]
    ts = SWA_TS
    nb = ts // BLOCK
    nsteps = s_len // ts

    def body(qt_ref, kc_ref, kp_ref, ktc_ref, ktp_ref, vc_ref, vp_ref, dot_ref, lse_ref, dl_ref, b_ref, b0_ref,
             sink_ref, dq_ref, dk_ref, dv_ref, dbias_ref, dsink_ref, dk_s, dv_s, tail_k, tail_v, sk_s):
        n = pl.program_id(0)

        @pl.when(n == 0)
        def _():
            dbias_ref[...] = jnp.zeros_like(dbias_ref)
            sk_s[...] = jnp.zeros_like(sk_s)

        @pl.when(n < nsteps)
        def _():
            first = n == 0
            dk_s[...] = jnp.zeros_like(dk_s)
            dv_s[...] = jnp.zeros_like(dv_s)
            groups = range(SWA_KV_HEADS)
            kall = [jnp.concatenate([kp_ref[g], kc_ref[g]], axis=0) for g in groups]
            vall = [jnp.concatenate([vp_ref[g], vc_ref[g]], axis=0) for g in groups]
            ktall = [jnp.concatenate([ktp_ref[g], ktc_ref[g]], axis=1) for g in groups]
            sinks = [_sink_row(sink_ref, g) for g in groups]
            items = [(g, b) for g in groups for b in range(nb)]

            def products(g, b):
                cols = slice(b * BLOCK, (b + 1) * BLOCK)
                win = slice(b * BLOCK, (b + 2) * BLOCK)
                qg = _group_lanes(qt_ref, g, cols)
                dog = _group_lanes(dot_ref, g, cols)
                bias_b = b_ref[g]
                if b == 0:
                    bias_b = jnp.where(first, b0_ref[g], bias_b)
                st = jnp.dot(kall[g][win], qg, preferred_element_type=f32) + bias_b
                dpt = jnp.dot(vall[g][win], dog, preferred_element_type=f32)
                return qg, dog, st, dpt

            def finish(g, b, qg, dog, st, dpt):
                cols = slice(b * BLOCK, (b + 1) * BLOCK)
                win = slice(b * BLOCK, (b + 2) * BLOCK)
                lse_r = _group_lanes(lse_ref, g, cols)
                dl_r = _group_lanes(dl_ref, g, cols)
                pt = jnp.exp(st - lse_r)
                dst = pt * (dpt - dl_r)
                dsb = dst.astype(bf16)
                dk_s[g, :, win] += lax.dot_general(qg, dsb, NT, preferred_element_type=f32)
                dv_s[g, :, win] += lax.dot_general(dog, pt.astype(bf16), NT, preferred_element_type=f32)
                dqg = jnp.dot(ktall[g][:, win], dsb, preferred_element_type=f32) * SCALE
                return dqg, dst, -jnp.exp(sinks[g] - lse_r) * dl_r

            dqs, dsts, sks = {}, {}, {}
            nxt = products(*items[0])
            for idx, (g, b) in enumerate(items):
                cur = nxt
                if idx + 1 < len(items):
                    nxt = products(*items[idx + 1])
                dqs[g, b], dsts[g, b], sks[g, b] = finish(g, b, *cur)
            for g in groups:
                dbias_ref[g] += functools.reduce(lambda a, c: a + c, [dsts[g, b] for b in range(nb)])
                sk_s[g] += functools.reduce(lambda a, c: a + c, [sks[g, b] for b in range(nb)])
                for hh in range(SWA_GROUP):
                    lanes = slice(hh * BLOCK, (hh + 1) * BLOCK)
                    dq_ref[g * SWA_GROUP + hh] = jnp.concatenate(
                        [dqs[g, b][:, lanes] for b in range(nb)], axis=1).astype(bf16)

        @pl.when(n > 0)
        def _():
            last = slice(ts - BLOCK, ts)
            for g in range(SWA_KV_HEADS):
                add_k = jnp.where(n < nsteps, dk_s[g, :, 0:BLOCK], 0.0)
                add_v = jnp.where(n < nsteps, dv_s[g, :, 0:BLOCK], 0.0)
                dk_ref[g, :, 0:ts - BLOCK] = tail_k[g, :, 0:ts - BLOCK].astype(bf16)
                dv_ref[g, :, 0:ts - BLOCK] = tail_v[g, :, 0:ts - BLOCK].astype(bf16)
                dk_ref[g, :, last] = (tail_k[g, :, last] + add_k).astype(bf16)
                dv_ref[g, :, last] = (tail_v[g, :, last] + add_v).astype(bf16)

        @pl.when(n < nsteps)
        def _():
            tail_k[...] = dk_s[:, :, BLOCK:]
            tail_v[...] = dv_s[:, :, BLOCK:]

        @pl.when(n == nsteps)
        def _():
            row = lax.broadcasted_iota(jnp.int32, (SWA_HEADS, 128), 0)
            out = jnp.zeros((SWA_HEADS, 128), f32)
            for h in range(SWA_HEADS):
                g, hh = divmod(h, SWA_GROUP)
                val = jnp.sum(sk_s[g, :, hh * BLOCK:(hh + 1) * BLOCK], axis=1, keepdims=True)
                out = jnp.where(row == h, val, out)
            dsink_ref[...] = out

    last_step = nsteps - 1

    def cl(n):
        return jnp.minimum(n, last_step)

    def prev_blk(n):
        return jnp.maximum(cl(n) * nb - 1, 0)

    feat8 = pl.BlockSpec((SWA_HEADS, HEAD_DIM, ts), lambda n: (0, 0, cl(n)))
    rows8 = pl.BlockSpec((SWA_HEADS, 1, ts), lambda n: (0, 0, cl(n)))
    cur = pl.BlockSpec((SWA_KV_HEADS, ts, HEAD_DIM), lambda n: (0, cl(n), 0))
    prev = pl.BlockSpec((SWA_KV_HEADS, BLOCK, HEAD_DIM), lambda n: (0, prev_blk(n), 0))
    curt = pl.BlockSpec((SWA_KV_HEADS, HEAD_DIM, ts), lambda n: (0, 0, cl(n)))
    prevt = pl.BlockSpec((SWA_KV_HEADS, HEAD_DIM, BLOCK), lambda n: (0, 0, prev_blk(n)))
    bspec = pl.BlockSpec((SWA_KV_HEADS, 2 * BLOCK, SWA_W), lambda n: (0, 0, 0))
    kvout = pl.BlockSpec((SWA_KV_HEADS, HEAD_DIM, ts), lambda n: (0, 0, jnp.maximum(n - 1, 0)))
    return pl.pallas_call(
        body,
        name="swa_bwd",
        grid=(nsteps + 1,),
        in_specs=[feat8, cur, prev, curt, prevt, cur, prev, feat8, rows8, rows8, bspec, bspec,
                  pl.BlockSpec(memory_space=pltpu.SMEM)],
        out_specs=[feat8, kvout, kvout, bspec, pl.BlockSpec((SWA_HEADS, 128), lambda n: (0, 0))],
        out_shape=[_sds((SWA_HEADS, HEAD_DIM, s_len), bf16), _sds((SWA_KV_HEADS, HEAD_DIM, s_len), bf16),
                   _sds((SWA_KV_HEADS, HEAD_DIM, s_len), bf16),
                   _sds((SWA_KV_HEADS, 2 * BLOCK, SWA_W), f32), _sds((SWA_HEADS, 128), f32)],
        scratch_shapes=[pltpu.VMEM((SWA_KV_HEADS, HEAD_DIM, ts + BLOCK), f32),
                        pltpu.VMEM((SWA_KV_HEADS, HEAD_DIM, ts + BLOCK), f32),
                        pltpu.VMEM((SWA_KV_HEADS, HEAD_DIM, ts), f32),
                        pltpu.VMEM((SWA_KV_HEADS, HEAD_DIM, ts), f32),
                        pltpu.VMEM((SWA_KV_HEADS, 1, SWA_W), f32)],
        compiler_params=_params(("arbitrary",)),
    )(qt, k, k, kt, kt, v, v, dot, lse, dl, bias_t, bias0_t, sink)


def _dproj_specs(tm):
    half = pl.BlockSpec((tm, 512), lambda i: (i, 0))
    feat = pl.BlockSpec((512, tm), lambda i: (0, i))
    feat_kv = pl.BlockSpec((128, tm), lambda i: (0, i))
    return [feat, feat, feat, half, feat, feat_kv, feat_kv, half, feat_kv]


def _dx_exchange_call(dh, pieces, w_t, bs, tm):
    s_len = dh.shape[0]
    n = len(bs)
    last = s_len // tm - 1

    def body(*refs):
        dh_ref, dqf_ref, dkf_ref, dvf_ref, dfz_ref, dqs_ref, dks_ref, dvs_ref, dsz_ref, dfft_ref, w_ref = refs[:11]
        b_refs = refs[11:11 + n]
        dx_ref = refs[11 + n]
        r_refs = refs[12 + n:12 + 2 * n]
        sems = refs[12 + 2 * n:]
        i = pl.program_id(0)

        @pl.when(i == 0)
        def _():
            _exchange_start(b_refs, r_refs, sems)

        def tr(ref):
            return ref[...].astype(f32).T.astype(bf16)

        dp = jnp.concatenate([tr(dqf_ref), tr(dkf_ref), tr(dvf_ref), dfz_ref[...], tr(dqs_ref), tr(dks_ref),
                              tr(dvs_ref), dsz_ref[...], tr(dfft_ref)], axis=1)
        dx_ref[...] = ALPHA * dh_ref[...] + jnp.dot(dp, w_ref[...], preferred_element_type=f32)

        @pl.when(i == last)
        def _():
            _exchange_wait(b_refs, r_refs, sems)

    fullw = pl.BlockSpec((tm, D_MODEL), lambda i: (i, 0))
    any_spec = pl.BlockSpec(memory_space=pl.ANY)
    out = pl.pallas_call(
        body,
        name="dx_bwd_exchange",
        grid=(s_len // tm,),
        in_specs=[fullw] + _dproj_specs(tm) + [pl.BlockSpec((A_W, D_MODEL), lambda i: (0, 0))] + [any_spec] * n,
        out_specs=[fullw] + [any_spec] * n,
        out_shape=[_sds((s_len, D_MODEL), f32)] + [_sds(b.shape, b.dtype) for b in bs],
        scratch_shapes=[pltpu.SemaphoreType.DMA((7 * n,)), pltpu.SemaphoreType.DMA((7 * n,)),
                        pltpu.SemaphoreType.DMA((n,))],
        compiler_params=_params(("arbitrary",)),
    )(dh, *pieces, w_t, *bs)
    return out[0], out[1:]


DW_STAGE_ROWS = 384


def _dw_call(x2, pieces, tm):
    s_len = x2.shape[0]
    nt = s_len // tm

    def body(x_ref, dqf_ref, dkf_ref, dvf_ref, dfz_ref, dqs_ref, dks_ref, dvs_ref, dsz_ref, dfft_ref, dw_ref,
             acc_ref, stage_ref, sem):
        i = pl.program_id(0)

        @pl.when(i == 0)
        def _():
            acc_ref[...] = jnp.zeros_like(acc_ref)

        xb = x_ref[...].astype(bf16)

        def add_feat(off, lhs):
            acc_ref[off:off + lhs.shape[0], :] += jnp.dot(lhs, xb, preferred_element_type=f32)

        def add_rows(off, piece):
            acc_ref[off:off + piece.shape[1], :] += lax.dot_general(piece, xb, TN, preferred_element_type=f32)

        add_feat(A_FQ, dqf_ref[...])
        add_feat(A_FK, dkf_ref[...])
        add_feat(A_FV, dvf_ref[...])
        add_rows(A_FZ, dfz_ref[...])
        add_feat(A_SQ, dqs_ref[...])
        add_feat(A_SK, dks_ref[...])
        add_feat(A_SV, dvs_ref[...])
        add_rows(A_SZ, dsz_ref[...])
        add_feat(A_FF, dfft_ref[...].astype(bf16))

        @pl.when(i == nt - 1)
        def _():
            for r in range(A_W // DW_STAGE_ROWS):
                rows = slice(r * DW_STAGE_ROWS, (r + 1) * DW_STAGE_ROWS)
                stage_ref[...] = acc_ref[rows, :].astype(bf16)
                cp = pltpu.make_async_copy(stage_ref, dw_ref.at[rows, :], sem)
                cp.start()
                cp.wait()

    return pl.pallas_call(
        body,
        name="dw_in_bwd",
        grid=(nt,),
        in_specs=[pl.BlockSpec((tm, D_MODEL), lambda i: (i, 0))] + _dproj_specs(tm),
        out_specs=pl.BlockSpec(memory_space=pl.ANY),
        out_shape=_sds((A_W, D_MODEL), bf16),
        scratch_shapes=[pltpu.VMEM((A_W, D_MODEL), f32), pltpu.VMEM((DW_STAGE_ROWS, D_MODEL), bf16),
                        pltpu.SemaphoreType.DMA],
        compiler_params=_params(("arbitrary",), VMEM_LIMIT_BIG),
    )(x2, *pieces)


def _adam_call(recv, w, m, v, tc, name):
    rows, cols = w.shape

    def body(r_ref, w_ref, m_ref, v_ref, g_ref, d_ref, mo_ref, vo_ref):
        g = r_ref[0].astype(f32)
        for p in range(1, N_DEV):
            g = g + r_ref[p].astype(f32)
        mn = ADAM_B1 * m_ref[...] + (1.0 - ADAM_B1) * g
        vn = ADAM_B2 * v_ref[...] + (1.0 - ADAM_B2) * (g * g)
        m_hat = mn / (1.0 - ADAM_B1 ** ADAM_STEP)
        v_hat = vn / (1.0 - ADAM_B2 ** ADAM_STEP)
        g_ref[...] = g
        d_ref[...] = -ADAM_LR * (m_hat / (jnp.sqrt(v_hat) + ADAM_EPS) + ADAM_WD * w_ref[...])
        mo_ref[...] = mn
        vo_ref[...] = vn

    blk = pl.BlockSpec((rows, tc), lambda i: (0, i))
    return pl.pallas_call(
        body,
        name=name,
        grid=(cols // tc,),
        in_specs=[pl.BlockSpec((N_DEV, rows, tc), lambda i: (0, 0, i)), blk, blk, blk],
        out_specs=[blk] * 4,
        out_shape=[_sds((rows, cols), f32)] * 4,
        compiler_params=_params(("arbitrary",)),
    )(recv, w, m, v)


def _shard_rows(blocks, lo, hi):
    shard = blocks.shape[1]
    out = []
    for d in range(blocks.shape[0]):
        a, b = max(lo, d * shard), min(hi, (d + 1) * shard)
        if a < b:
            out.append(blocks[d, a - d * shard:b - d * shard])
    return out


def _rows_to_shards(parts, shard):
    blocks = []
    for d in range(N_DEV):
        lo, hi, start, pieces = d * shard, (d + 1) * shard, 0, []
        for part in parts:
            a, b = max(lo, start), min(hi, start + part.shape[0])
            if a < b:
                pieces.append(part[a - start:b - start])
            start += part.shape[0]
        blocks.append(jnp.concatenate(pieces, axis=0))
    return jnp.stack(blocks)


def _pad_cols(a, width=128):
    return jnp.pad(a, ((0, 0), (0, width - a.shape[1])))


def _pack_small(ln_g, ln_b, rel, b_f, sink):
    return jnp.concatenate([
        ln_g.reshape(8, 128), ln_b.reshape(8, 128), _pad_cols(rel),
        jnp.pad(_pad_cols(b_f), ((0, 7), (0, 0))), jnp.pad(_pad_cols(sink), ((0, 7), (0, 0)))], axis=0)


def _unpack_small(p):
    return (p[0:8].reshape(1, D_MODEL), p[8:16].reshape(1, D_MODEL), p[16:48, 0:8], p[48:49, 0:8], p[56:57, 0:8])


def kernel(x, w_in, b_f, rel_bias, sink, w_o, ln_g, ln_b, loss_target, m_w_in, m_b_f, m_rel_bias, m_sink, m_w_o, m_ln_g, m_ln_b, v_w_in, v_b_f, v_rel_bias, v_sink, v_w_o, v_ln_g, v_ln_b):
    x2 = x[0]
    tgt = loss_target[0]
    s_len = x2.shape[0]
    shard = w_in.shape[2]

    w_in_t = jnp.transpose(w_in[0])
    g_in, g_o = _gather_call([w_in_t.astype(bf16), w_o[0].astype(bf16)])
    w_t = jnp.concatenate(_shard_rows(g_in, 0, O_FF0) + _shard_rows(g_in, O_FF1, D_IN) + _shard_rows(g_in, O_FF0, O_FF1)
                          + [jnp.zeros((A_W - D_IN, D_MODEL), bf16)], axis=0)
    wo_full = g_o.reshape(D_MODEL, D_MODEL)

    qft, kft, vf, fz, qst, ks, vs, sz, fft, vat, kst, vsta = _proj_call(x2, w_t, 512)
    cum, sgm = _cum_call(fft, b_f.reshape(FOX_HEADS, 1))
    qat, ka, kat, tile_stats = _augment_call(qft, kft, cum.reshape(FOX_HEADS, 1, s_len), 2048)
    npairs, pair_q, pair_k = _fox_prune_tables(tile_stats)
    o_ft, lse_f = _fox_fwd_call(qat, ka, vat, npairs, pair_q, pair_k)
    bucket_t = jnp.asarray(_t5_bucket_table().T)
    bias_t, bias0_t = _swa_bias_call(rel_bias, bucket_t)
    sink_v = sink.reshape(SWA_HEADS)
    o_st, lse_s = _swa_fwd_call(qst, ks, vsta, bias_t, bias0_t, sink_v)

    (dh, do_f, dfz, do_s, dsz, dl_f, dl_s, dwo, dg, db, loss_part) = _post_call(
        o_ft.reshape(FOX_HEADS * HEAD_DIM, s_len), fz, o_st.reshape(SWA_HEADS * HEAD_DIM, s_len), sz, x2, tgt,
        wo_full, ln_g, ln_b, jnp.asarray(_head_selector()).astype(bf16), 512)

    dqf, dkf, dvf, dcq, dck = _fox_bwd_call(ka, kat, vf, qat, do_f.reshape(FOX_HEADS, HEAD_DIM, s_len), lse_f,
                                            dl_f.reshape(FOX_HEADS, 1, s_len), npairs, pair_q, pair_k)
    dqf, dkf, dvf = (a.reshape(FOX_HEADS * HEAD_DIM, s_len) for a in (dqf, dkf, dvf))
    dfft, dbf = _cum_bwd_call(dcq.reshape(FOX_HEADS, s_len), dck.reshape(FOX_HEADS, s_len), sgm)
    dqs, dks, dvs, dbias, dsink = _swa_bwd_call(
        qst, ks, kst, vs, do_s.reshape(SWA_HEADS, HEAD_DIM, s_len), lse_s, dl_s.reshape(SWA_HEADS, 1, s_len),
        bias_t, bias0_t, sink_v)
    dqs = dqs.reshape(SWA_HEADS * HEAD_DIM, s_len)
    dks, dvs = (a.reshape(SWA_KV_HEADS * HEAD_DIM, s_len) for a in (dks, dvs))
    drel = _swa_bias_bwd_call(dbias, bucket_t)

    pieces = (dqf, dkf, dvf, dfz, dqs, dks, dvs, dsz, dfft)
    dw_t = _dw_call(x2, pieces, 1024)

    dw_blocks = _rows_to_shards([dw_t[:O_FF0], dw_t[A_FF:A_FF + (O_FF1 - O_FF0)], dw_t[O_FF0:A_FF]], shard)
    dwo_blocks = dwo.reshape(N_DEV, D_MODEL // N_DEV, D_MODEL).astype(bf16)
    small = _pack_small(dg, db, drel[:, 0:8], dbf[:, 0].reshape(1, 8), dsink[:, 0].reshape(1, 8))
    loss_slot = np.zeros((64, 128), bool)
    loss_slot[49, 0] = True
    small = jnp.where(jnp.asarray(loss_slot), loss_part[0, 0], small)
    small_blocks = jnp.broadcast_to(small[None], (N_DEV,) + small.shape)
    dx, (r_in, r_o, r_small) = _dx_exchange_call(dh, pieces, w_t, [dw_blocks, dwo_blocks, small_blocks], 256)

    win_t = [jnp.transpose(a) for a in _adam_call(
        r_in, w_in_t, jnp.transpose(m_w_in[0]), jnp.transpose(v_w_in[0]), 256, "adam_w_in")]
    g_win, d_win, nm_win, nv_win = win_t
    g_wo, d_wo, nm_wo, nv_wo = _adam_call(r_o, w_o[0], m_w_o[0], v_w_o[0], 256, "adam_w_o")
    p_w = _pack_small(ln_g, ln_b, rel_bias, b_f, sink)
    p_m = _pack_small(m_ln_g, m_ln_b, m_rel_bias, m_b_f, m_sink)
    p_v = _pack_small(v_ln_g, v_ln_b, v_rel_bias, v_b_f, v_sink)
    g_p, d_p, nm_p, nv_p = _adam_call(r_small, p_w, p_m, p_v, 128, "adam_small")

    loss = g_p[49, 0]
    g_lng, g_lnb, g_rel, g_bf, g_sink = _unpack_small(g_p)
    d_lng, d_lnb, d_rel, d_bf, d_sink = _unpack_small(d_p)
    m_lng, m_lnb, m_rel, m_bf, m_sk = _unpack_small(nm_p)
    v_lng, v_lnb, v_rel, v_bf, v_sk = _unpack_small(nv_p)
    return (loss, dx[None], g_win[None], g_bf, g_rel, g_sink, g_wo[None], g_lng, g_lnb,
            d_win[None], d_bf, d_rel, d_sink, d_wo[None], d_lng, d_lnb,
            nm_win[None], m_bf, m_rel, m_sk, nm_wo[None], m_lng, m_lnb,
            nv_win[None], v_bf, v_rel, v_sk, nv_wo[None], v_lng, v_lnb)
```

```python
import functools
import math

import numpy as np
import jax
import jax.numpy as jnp
from jax import lax
from jax.experimental import pallas as pl
from jax.experimental.pallas import tpu as pltpu

f32 = jnp.float32
bf16 = jnp.bfloat16

D_MODEL = 1024
HEAD_DIM = 64
FOX_HEADS = 8
SWA_HEADS = 8
SWA_KV_HEADS = 2
SWA_GROUP = 4
BLOCK = 128
NUM_BUCKETS = 32
MAX_DISTANCE = 128
LN_EPS = 1e-5
NEG_INF = -1e30
ALPHA = 2.0 ** 0.25
SCALE = 1.0 / math.sqrt(HEAD_DIM)
D_IN = 3336

ADAM_LR = 0.001
ADAM_B1 = 0.9
ADAM_B2 = 0.999
ADAM_EPS = 1e-08
ADAM_WD = 0.01
ADAM_STEP = 10

N_DEV = 8
A_FQ, A_FK, A_FV, A_FZ, A_SQ, A_SK, A_SV, A_SZ, A_FF, A_W = 0, 512, 1024, 1536, 2048, 2560, 2688, 2816, 3328, 3456
O_FF0, O_FF1 = 1536, 1544

VMEM_LIMIT = 48 * 1024 * 1024
HIGHEST = lax.Precision.HIGHEST
NT = (((1,), (1,)), ((), ()))
TN = (((0,), (0,)), ((), ()))
MESH = pl.DeviceIdType.MESH
RELS = [(0, 0, 1), (0, 1, 0), (0, 1, 1), (1, 0, 0), (1, 0, 1), (1, 1, 0), (1, 1, 1)]


VMEM_LIMIT_BIG = 60 * 1024 * 1024


def _params(sem=None, vmem=VMEM_LIMIT):
    return pltpu.CompilerParams(dimension_semantics=sem, vmem_limit_bytes=vmem)


def _sds(shape, dtype):
    return jax.ShapeDtypeStruct(shape, dtype)


def _t5_bucket_table():
    qi = np.arange(BLOCK)[:, None]
    kj = np.arange(2 * BLOCK)[None, :]
    rel = qi + BLOCK - kj
    band = (rel >= 0) & (rel < BLOCK)
    relc = np.maximum(rel, 0)
    max_exact = NUM_BUCKETS // 2
    relf = np.maximum(relc, 1).astype(np.float32)
    large = max_exact + (np.log(relf / np.float32(max_exact)) / np.float32(math.log(MAX_DISTANCE / max_exact))
                         * np.float32(NUM_BUCKETS - max_exact)).astype(np.int32)
    large = np.minimum(large, NUM_BUCKETS - 1)
    bucket = np.where(relc < max_exact, relc, large).astype(np.int32)
    bucket = np.where(band, bucket, -1).astype(np.int32)
    return bucket


def _mesh_pos():
    return lax.axis_index("x"), lax.axis_index("y"), lax.axis_index("c")


def _dev_index(p):
    return 4 * p[0] + 2 * p[1] + p[2]


def _gather_call(xs):
    n = len(xs)

    def body(*refs):
        x_refs, o_refs = refs[:n], refs[n:2 * n]
        send_sems, recv_sems, local_sems = refs[2 * n:]
        x, y, c = _mesh_pos()
        me, sib = (x, y, c), (x, y, 1 - c)
        chips = [(1 - x, y), (x, 1 - y), (1 - x, 1 - y)]

        def copy(a, k, block, to, src=None):
            slot = o_refs[a].at[_dev_index(block)]
            return pltpu.make_async_remote_copy(
                src_ref=slot if src is None else src, dst_ref=slot,
                send_sem=send_sems.at[a * 7 + k], recv_sem=recv_sems.at[a * 7 + k],
                device_id=to, device_id_type=MESH)

        mine = [pltpu.make_async_copy(x_refs[a], o_refs[a].at[_dev_index(me)], local_sems.at[a]) for a in range(n)]
        for cp in mine:
            cp.start()
        first = []
        for a in range(n):
            first.append(copy(a, 0, me, sib, src=x_refs[a]))
            first += [copy(a, 1 + j, me, (*chip, c), src=x_refs[a]) for j, chip in enumerate(chips)]
        for cp in first:
            cp.start()
        passed = []
        for j, chip in enumerate(chips):
            for a in range(n):
                copy(a, 1 + j, (*chip, c), me).wait_recv()
                fwd = copy(a, 4 + j, (*chip, c), sib)
                fwd.start()
                passed.append(fwd)
        for a in range(n):
            copy(a, 0, sib, me).wait_recv()
            for j, chip in enumerate(chips):
                copy(a, 4 + j, (*chip, 1 - c), me).wait_recv()
        for cp in first + passed:
            cp.wait_send()
        for cp in mine:
            cp.wait()

    any_spec = pl.BlockSpec(memory_space=pl.ANY)
    return pl.pallas_call(
        body,
        name="gather_weights",
        out_shape=[_sds((N_DEV,) + a.shape, a.dtype) for a in xs],
        in_specs=[any_spec] * n,
        out_specs=[any_spec] * n,
        scratch_shapes=[pltpu.SemaphoreType.DMA((7 * n,)), pltpu.SemaphoreType.DMA((7 * n,)),
                        pltpu.SemaphoreType.DMA((n,))],
    )(*xs)


def _exchange_copies(b_refs, r_refs, send_sems, recv_sems, local_sems, incoming):
    n = len(b_refs)
    x, y, c = _mesh_pos()
    me_idx = _dev_index((x, y, c))
    mine = [pltpu.make_async_copy(b_refs[a].at[me_idx], r_refs[a].at[me_idx], local_sems.at[a]) for a in range(n)]
    remote = []
    for k, r in enumerate(RELS):
        peer = ((1 - x) if r[0] else x, (1 - y) if r[1] else y, (1 - c) if r[2] else c)
        pidx = _dev_index(peer)
        for a in range(n):
            remote.append(pltpu.make_async_remote_copy(
                src_ref=b_refs[a].at[pidx], dst_ref=r_refs[a].at[pidx if incoming else me_idx],
                send_sem=send_sems.at[a * 7 + k], recv_sem=recv_sems.at[a * 7 + k],
                device_id=peer, device_id_type=MESH))
    return mine, remote


def _exchange_start(b_refs, r_refs, sems):
    mine, out = _exchange_copies(b_refs, r_refs, *sems, incoming=False)
    for cp in mine + out:
        cp.start()


def _exchange_wait(b_refs, r_refs, sems):
    mine, inc = _exchange_copies(b_refs, r_refs, *sems, incoming=True)
    for cp in inc:
        cp.wait_recv()
    for cp in inc:
        cp.wait_send()
    for cp in mine:
        cp.wait()


def _proj_call(x2, w_t, tm):
    s_len = x2.shape[0]

    def body(x_ref, w_ref, qft_ref, kft_ref, vf_ref, fz_ref, qst_ref, ks_ref, vs_ref, sz_ref, fft_ref, vat_ref,
             kst_ref, vsta_ref):
        xb = x_ref[...].astype(bf16)

        def seg_t(off, width):
            return lax.dot_general(w_ref[off:off + width, :], xb, NT, preferred_element_type=f32)

        def seg(off, width):
            return lax.dot_general(xb, w_ref[off:off + width, :], NT, preferred_element_type=f32)

        def put_heads(ref, acc, nheads):
            for h in range(nheads):
                ref[h] = acc[:, h * HEAD_DIM:(h + 1) * HEAD_DIM].astype(bf16)

        def put_heads_t(ref, acc_t, nheads, augment):
            for h in range(nheads):
                ref[h, 0:HEAD_DIM, :] = acc_t[h * HEAD_DIM:(h + 1) * HEAD_DIM, :].astype(bf16)
                if augment:
                    ref[h, HEAD_DIM:2 * HEAD_DIM, :] = ones_row

        ones_row = jnp.where(lax.broadcasted_iota(jnp.int32, (HEAD_DIM, tm), 0) == 0, 1.0, 0.0).astype(bf16)
        put_heads_t(vat_ref, seg_t(A_FV, 512), FOX_HEADS, True)
        put_heads_t(qft_ref, seg_t(A_FQ, 512) * SCALE, FOX_HEADS, False)
        put_heads_t(kft_ref, seg_t(A_FK, 512), FOX_HEADS, False)
        put_heads(vf_ref, seg(A_FV, 512), FOX_HEADS)
        fz_ref[...] = seg(A_FZ, 512)
        put_heads_t(qst_ref, seg_t(A_SQ, 512) * SCALE, SWA_HEADS, False)
        put_heads(ks_ref, seg(A_SK, 128), SWA_KV_HEADS)
        put_heads(vs_ref, seg(A_SV, 128), SWA_KV_HEADS)
        put_heads_t(kst_ref, seg_t(A_SK, 128), SWA_KV_HEADS, False)
        put_heads_t(vsta_ref, seg_t(A_SV, 128), SWA_KV_HEADS, True)
        sz_ref[...] = seg(A_SZ, 512)
        fft_ref[...] = seg(A_FF, 128).T[:FOX_HEADS, :]

    def heads(nh):
        return pl.BlockSpec((nh, tm, HEAD_DIM), lambda i: (0, i, 0))

    def feat(nh, rows):
        return pl.BlockSpec((nh, rows, tm), lambda i: (0, 0, i))

    wide = pl.BlockSpec((tm, 512), lambda i: (i, 0))
    return pl.pallas_call(
        body,
        name="proj_fwd",
        grid=(s_len // tm,),
        in_specs=[pl.BlockSpec((tm, D_MODEL), lambda i: (i, 0)), pl.BlockSpec((A_W, D_MODEL), lambda i: (0, 0))],
        out_specs=[feat(8, HEAD_DIM), feat(8, HEAD_DIM), heads(8), wide, feat(8, HEAD_DIM), heads(2), heads(2), wide,
                   pl.BlockSpec((FOX_HEADS, tm), lambda i: (0, i)),
                   feat(FOX_HEADS, 2 * HEAD_DIM), feat(2, HEAD_DIM), feat(2, 2 * HEAD_DIM)],
        out_shape=[_sds((8, HEAD_DIM, s_len), bf16)] * 2 + [_sds((8, s_len, HEAD_DIM), bf16)]
                  + [_sds((s_len, 512), f32), _sds((8, HEAD_DIM, s_len), bf16),
                     _sds((2, s_len, HEAD_DIM), bf16), _sds((2, s_len, HEAD_DIM), bf16), _sds((s_len, 512), f32),
                     _sds((FOX_HEADS, s_len), f32), _sds((FOX_HEADS, 2 * HEAD_DIM, s_len), bf16),
                     _sds((2, HEAD_DIM, s_len), bf16), _sds((2, 2 * HEAD_DIM, s_len), bf16)],
        compiler_params=_params(("arbitrary",)),
    )(x2, w_t)


AUG = 2 * HEAD_DIM
NEAR_KEYS = 3


def _augment_call(q_t, k_t, cum_row, tm):
    nh, _, s_len = k_t.shape
    per_step = tm // FOX_T

    def body(qt_ref, kt_ref, c_ref, qat_ref, ka_ref, kat_ref, st_ref):
        c = c_ref[0]
        hi = c.astype(bf16).astype(f32)
        r1 = c - hi
        mid = r1.astype(bf16).astype(f32)
        lo = (r1 - mid).astype(bf16).astype(f32)
        row = lax.broadcasted_iota(jnp.int32, (HEAD_DIM, tm), 0)
        q_tail = jnp.where(row == 0, hi, jnp.where(row == 1, mid, jnp.where(row == 2, lo,
                           jnp.where(row < 6, 1.0, 0.0))))
        k_tail = jnp.where(row < 3, 1.0, jnp.where(row == 3, -hi, jnp.where(row == 4, -mid,
                           jnp.where(row == 5, -lo, 0.0))))
        qat_ref[0, 0:HEAD_DIM, :] = qt_ref[0]
        qat_ref[0, HEAD_DIM:AUG, :] = q_tail.astype(bf16)
        kat_ref[0, 0:HEAD_DIM, :] = kt_ref[0]
        kat_ref[0, HEAD_DIM:AUG, :] = k_tail.astype(bf16)
        qt = qt_ref[0].astype(f32)
        kt = kt_ref[0].astype(f32)
        ka_ref[0] = jnp.concatenate([kt, k_tail], axis=0).T.astype(bf16)
        qn2 = jnp.sum(qt * qt, axis=0, keepdims=True)
        kn2 = jnp.sum(kt * kt, axis=0, keepdims=True)
        sd = jnp.sum(qt * kt, axis=0, keepdims=True)
        k_and_c = jnp.concatenate([kt, jnp.broadcast_to(c, (8, tm))], axis=0)
        lane = lax.broadcasted_iota(jnp.int32, (1, tm), 1)
        for shift in range(1, NEAR_KEYS + 1):
            prev = pltpu.roll(k_and_c, shift, axis=1)
            near = jnp.sum(qt * prev[0:HEAD_DIM], axis=0, keepdims=True) + (c - prev[HEAD_DIM:HEAD_DIM + 1])
            sd = jnp.maximum(sd, jnp.where(lane >= shift, near, NEG_INF))
        srow = lax.broadcasted_iota(jnp.int32, (8, LANES), 0)
        for part in range(per_step):
            sl = slice(part * FOX_T, (part + 1) * FOX_T)
            vals = [jnp.sqrt(jnp.max(qn2[:, sl], axis=1, keepdims=True)),
                    jnp.sqrt(jnp.max(kn2[:, sl], axis=1, keepdims=True)),
                    jnp.min(sd[:, sl], axis=1, keepdims=True),
                    jnp.max(c[:, sl], axis=1, keepdims=True), jnp.min(c[:, sl], axis=1, keepdims=True)]
            out = jnp.zeros((8, LANES), f32)
            for r, val in enumerate(vals):
                out = jnp.where(srow == r, val, out)
            st_ref[0, part] = out

    tile_t = pl.BlockSpec((1, HEAD_DIM, tm), lambda h, i: (h, 0, i))
    return pl.pallas_call(
        body,
        name="fox_augment",
        grid=(nh, s_len // tm),
        in_specs=[tile_t, tile_t, pl.BlockSpec((1, 1, tm), lambda h, i: (h, 0, i))],
        out_specs=[pl.BlockSpec((1, AUG, tm), lambda h, i: (h, 0, i)),
                   pl.BlockSpec((1, tm, AUG), lambda h, i: (h, i, 0)),
                   pl.BlockSpec((1, AUG, tm), lambda h, i: (h, 0, i)),
                   pl.BlockSpec((1, per_step, 8, LANES), lambda h, i: (h, i, 0, 0))],
        out_shape=[_sds((nh, AUG, s_len), bf16), _sds((nh, s_len, AUG), bf16), _sds((nh, AUG, s_len), bf16),
                   _sds((nh, s_len // FOX_T, 8, LANES), f32)],
        compiler_params=_params(("arbitrary", "arbitrary")),
    )(q_t, k_t, cum_row)


FOX_PRUNE_GAP = 32.0


def _fox_prune_tables(stats):
    s = stats[:, :, :, 0]
    qn, kn, sd, cmx, cmn = (s[:, :, r] for r in range(5))
    nt = s.shape[1]
    bound = qn[:, :, None] * kn[:, None, :] + (cmx[:, :, None] - cmn[:, None, :])
    margin = 0.01 + 1e-5 * (jnp.abs(cmx)[:, :, None] + jnp.abs(cmn)[:, None, :])
    qi = lax.broadcasted_iota(jnp.int32, (nt, nt), 0)
    kj = lax.broadcasted_iota(jnp.int32, (nt, nt), 1)
    skip = (bound + margin < sd[:, :, None] - FOX_PRUNE_GAP) & (kj < qi)[None]
    first = jnp.sum(jnp.cumprod(skip.astype(jnp.int32), axis=2), axis=2)
    tiles = lax.broadcasted_iota(jnp.int32, (1, nt), 1)
    cnt = tiles - first
    ends = jnp.cumsum(cnt, axis=1)
    off = ends - cnt
    kmax = nt * (nt - 1) // 2
    k = lax.broadcasted_iota(jnp.int32, (1, kmax), 1)
    pair_q = jnp.minimum(jnp.sum((ends[:, None, :] <= k[:, :, None]).astype(jnp.int32), axis=2), nt - 1)
    hit = pair_q[:, :, None] == tiles[:, None, :]
    first_k = jnp.sum(jnp.where(hit, first[:, None, :], 0), axis=2)
    off_k = jnp.sum(jnp.where(hit, off[:, None, :], 0), axis=2)
    pair_k = jnp.clip(first_k + k - off_k, 0, nt - 1)
    return (ends[:, nt - 1].astype(jnp.int32), pair_q.reshape(-1).astype(jnp.int32),
            pair_k.reshape(-1).astype(jnp.int32))


CUM_CHUNK = 512


def _cum_call(fft, bf_col):
    s_len = fft.shape[1]
    ch = CUM_CHUNK

    def body(f_ref, b_ref, cum_ref, sg_ref):
        r = lax.broadcasted_iota(jnp.int32, (ch, ch), 0)
        c = lax.broadcasted_iota(jnp.int32, (ch, ch), 1)
        upper = (r <= c).astype(f32)
        carry = jnp.zeros((FOX_HEADS, 1), f32)
        for n in range(s_len // ch):
            z = f_ref[:, n * ch:(n + 1) * ch] + b_ref[...]
            logf = jnp.minimum(z, 0.0) - jnp.log1p(jnp.exp(-jnp.abs(z)))
            sg_ref[:, n * ch:(n + 1) * ch] = 1.0 / (1.0 + jnp.exp(z))
            cs = jnp.dot(logf, upper, precision=HIGHEST, preferred_element_type=f32) + carry
            cum_ref[:, n * ch:(n + 1) * ch] = cs
            carry = cs[:, ch - 1:ch]

    return pl.pallas_call(
        body,
        name="fox_cum_fwd",
        out_shape=[_sds((FOX_HEADS, s_len), f32)] * 2,
        compiler_params=_params(),
    )(fft, bf_col)


def _cum_bwd_call(dcq, dck, sg):
    s_len = sg.shape[1]
    ch = CUM_CHUNK
    nch = s_len // ch

    def body(q_ref, k_ref, sg_ref, dff_ref, dbf_ref):
        r = lax.broadcasted_iota(jnp.int32, (ch, ch), 0)
        c = lax.broadcasted_iota(jnp.int32, (ch, ch), 1)
        lower = (r >= c).astype(f32)
        dff_ref[...] = jnp.zeros_like(dff_ref)
        carry = jnp.zeros((FOX_HEADS, 1), f32)
        total = jnp.zeros((FOX_HEADS, 1), f32)
        for n in reversed(range(nch)):
            sl = slice(n * ch, (n + 1) * ch)
            dcum = q_ref[:, sl] - k_ref[:, sl]
            rs = jnp.dot(dcum, lower, precision=HIGHEST, preferred_element_type=f32) + carry
            carry = rs[:, 0:1]
            dff = rs * sg_ref[:, sl]
            dff_ref[0:FOX_HEADS, sl] = dff
            total = total + jnp.sum(dff, axis=1, keepdims=True)
        dbf_ref[...] = jnp.broadcast_to(total, (FOX_HEADS, 128))

    return pl.pallas_call(
        body,
        name="fox_cum_bwd",
        out_shape=[_sds((128, s_len), f32), _sds((FOX_HEADS, 128), f32)],
        compiler_params=_params(),
    )(dcq, dck, sg)


FOX_T = 512
LANES = 128


def _causal_keep(t):
    return lax.broadcasted_iota(jnp.int32, (t, t), 0) <= lax.broadcasted_iota(jnp.int32, (t, t), 1)


def _tile_cols(i, t):
    return pl.ds(pl.multiple_of(i * t, t), t)


def _fox_pair(n, nt, kmax, h, pq_ref, pk_ref):
    k = h * kmax + jnp.maximum(n - nt, 0)
    return jnp.where(n < nt, n, pq_ref[k]), jnp.where(n < nt, n, pk_ref[k])


def _fox_fwd_call(qat, ka, vat, npairs, pair_q, pair_k):
    nh, s_len, _ = ka.shape
    t = FOX_T
    nt = s_len // t
    kmax = nt * (nt - 1) // 2
    assert nt >= 2 and nt % 2 == 0

    def body(np_ref, pq_ref, pk_ref, qat_ref, ka_ref, vat_ref, o_ref, lse_ref, s0, s1, p0, p1, a0, a1, m_all, acc_all):
        h = pl.program_id(0)
        extra = np_ref[h]
        total = nt + extra
        m_all[...] = jnp.full(m_all.shape, NEG_INF, f32)
        acc_all[...] = jnp.zeros(acc_all.shape, f32)
        bufs = ((s0, p0, a0), (s1, p1, a1))

        def pair(n):
            return _fox_pair(n, nt, kmax, h, pq_ref, pk_ref)

        def scores(n, b, masked):
            i, j = pair(n)
            st = jnp.dot(ka_ref[0, _tile_cols(j, t), :], qat_ref[0, :, _tile_cols(i, t)], preferred_element_type=f32)
            if masked:
                st = jnp.where(_causal_keep(t), st, NEG_INF)
            bufs[b][0][...] = st

        def softmax(n, b):
            i, _ = pair(n)
            s_ref, p_ref, a_ref = bufs[b]
            for c in range(t // LANES):
                cols = slice(c * LANES, (c + 1) * LANES)
                mcols = pl.ds(pl.multiple_of(i * t + c * LANES, LANES), LANES)
                m_old = m_all[:, mcols]
                m_new = jnp.maximum(m_old, jnp.max(s_ref[:, cols], axis=0, keepdims=True))
                m_all[:, mcols] = m_new
                a_ref[:, cols] = jnp.exp(m_old - m_new)
                p_ref[:, cols] = jnp.exp(s_ref[:, cols] - m_new).astype(bf16)

        def accum(n, b):
            i, j = pair(n)
            cols = _tile_cols(i, t)
            acc_all[:, cols] = bufs[b][2][...] * acc_all[:, cols] + jnp.dot(
                vat_ref[0, :, _tile_cols(j, t)], bufs[b][1][...], preferred_element_type=f32)

        def step(n, b, masked):
            accum(n - 2, b)
            softmax(n - 1, 1 - b)
            scores(n, b, masked)

        scores(0, 0, True)
        scores(1, 1, True)
        softmax(0, 0)

        def diag_steps(d, _):
            n = 2 + 2 * d
            step(n, 0, True)
            step(n + 1, 1, True)
            return 0

        lax.fori_loop(0, (nt - 2) // 2, diag_steps, 0)

        def off_steps(d, _):
            n = nt + 2 * d
            step(n, 0, False)
            step(n + 1, 1, False)
            return 0

        lax.fori_loop(0, extra // 2, off_steps, 0)

        @pl.when(extra % 2 == 1)
        def _():
            step(total - 1, 0, False)
            softmax(total - 1, 0)
            accum(total - 2, 1)
            accum(total - 1, 0)

        @pl.when(extra % 2 == 0)
        def _():
            softmax(total - 1, 1)
            accum(total - 2, 0)
            accum(total - 1, 1)

        l = acc_all[HEAD_DIM:HEAD_DIM + 1, :]
        o_ref[0] = acc_all[0:HEAD_DIM, :] / l
        lse_ref[0] = m_all[...] + jnp.log(l)

    smem = pl.BlockSpec(memory_space=pltpu.SMEM)
    return pl.pallas_call(
        body,
        name="fox_fwd",
        grid=(nh,),
        in_specs=[smem, smem, smem,
                  pl.BlockSpec((1, AUG, s_len), lambda h: (h, 0, 0)),
                  pl.BlockSpec((1, s_len, AUG), lambda h: (h, 0, 0)),
                  pl.BlockSpec((1, AUG, s_len), lambda h: (h, 0, 0))],
        out_specs=[pl.BlockSpec((1, HEAD_DIM, s_len), lambda h: (h, 0, 0)),
                   pl.BlockSpec((1, 1, s_len), lambda h: (h, 0, 0))],
        out_shape=[_sds((nh, HEAD_DIM, s_len), f32), _sds((nh, 1, s_len), f32)],
        scratch_shapes=[pltpu.VMEM((t, t), f32), pltpu.VMEM((t, t), f32), pltpu.VMEM((t, t), bf16),
                        pltpu.VMEM((t, t), bf16), pltpu.VMEM((1, t), f32), pltpu.VMEM((1, t), f32),
                        pltpu.VMEM((1, s_len), f32), pltpu.VMEM((AUG, s_len), f32)],
        compiler_params=_params(("arbitrary",)),
    )(npairs, pair_q, pair_k, qat, ka, vat)


SWA_TS = 512


SWA_W = SWA_GROUP * BLOCK


def _swa_bias_call(rel_bias, bucket_t):
    def body(rb_ref, bk_ref, b_ref, b0_ref):
        bk = bk_ref[...]
        row = lax.broadcasted_iota(jnp.int32, (2 * BLOCK, BLOCK), 0)
        for h in range(SWA_HEADS):
            acc = jnp.full((2 * BLOCK, BLOCK), NEG_INF, f32)
            for b in range(NUM_BUCKETS):
                acc = jnp.where(bk == b, rb_ref[b, h], acc)
            g, hh = divmod(h, SWA_GROUP)
            b_ref[g, :, hh * BLOCK:(hh + 1) * BLOCK] = acc
            b0_ref[g, :, hh * BLOCK:(hh + 1) * BLOCK] = jnp.where(row < BLOCK, NEG_INF, acc)

    return pl.pallas_call(
        body,
        name="swa_bias",
        in_specs=[pl.BlockSpec(memory_space=pltpu.SMEM), pl.BlockSpec(memory_space=pltpu.VMEM)],
        out_shape=[_sds((SWA_KV_HEADS, 2 * BLOCK, SWA_W), f32)] * 2,
        compiler_params=_params(),
    )(rel_bias, bucket_t)


def _swa_bias_bwd_call(dbias, bucket_t):
    def body(d_ref, bk_ref, o_ref):
        bk = bk_ref[...]
        row = lax.broadcasted_iota(jnp.int32, (NUM_BUCKETS, 128), 0)
        col = lax.broadcasted_iota(jnp.int32, (NUM_BUCKETS, 128), 1)
        out = jnp.zeros((NUM_BUCKETS, 128), f32)
        for h in range(SWA_HEADS):
            g, hh = divmod(h, SWA_GROUP)
            d = d_ref[g, :, hh * BLOCK:(hh + 1) * BLOCK]
            for b in range(NUM_BUCKETS):
                val = jnp.sum(jnp.sum(jnp.where(bk == b, d, 0.0), axis=1, keepdims=True), axis=0, keepdims=True)
                out = jnp.where((row == b) & (col == h), val, out)
        o_ref[...] = out

    return pl.pallas_call(
        body,
        name="swa_bias_bwd",
        out_shape=_sds((NUM_BUCKETS, 128), f32),
        compiler_params=_params(),
    )(dbias, bucket_t)


def _sink_row(sink_ref, g):
    return jnp.concatenate([jnp.full((1, BLOCK), sink_ref[g * SWA_GROUP + hh], f32) for hh in range(SWA_GROUP)], axis=1)


def _group_lanes(ref, g, cols):
    return jnp.concatenate([ref[g * SWA_GROUP + hh, :, cols] for hh in range(SWA_GROUP)], axis=1)


def _swa_fwd_call(qt, k, vta, bias_t, bias0_t, sink):
    s_len = qt.shape[2]
    ts = SWA_TS
    nb = ts // BLOCK

    def body(qt_ref, kc_ref, kp_ref, vc_ref, vp_ref, b_ref, b0_ref, sink_ref, o_ref, lse_ref):
        first = pl.program_id(0) == 0
        kall = [jnp.concatenate([kp_ref[g], kc_ref[g]], axis=0) for g in range(SWA_KV_HEADS)]
        vall = [jnp.concatenate([vp_ref[g], vc_ref[g]], axis=1) for g in range(SWA_KV_HEADS)]
        sinks = [_sink_row(sink_ref, g) for g in range(SWA_KV_HEADS)]
        items = [(g, b) for g in range(SWA_KV_HEADS) for b in range(nb)]

        def scores(g, b):
            qg = _group_lanes(qt_ref, g, slice(b * BLOCK, (b + 1) * BLOCK))
            bias_b = b_ref[g]
            if b == 0:
                bias_b = jnp.where(first, b0_ref[g], bias_b)
            return jnp.dot(kall[g][b * BLOCK:(b + 2) * BLOCK], qg, preferred_element_type=f32) + bias_b

        def finish(g, b, st):
            m = jnp.maximum(jnp.max(st, axis=0, keepdims=True), sinks[g])
            pt = jnp.exp(st - m)
            acc = jnp.dot(vall[g][:, b * BLOCK:(b + 2) * BLOCK], pt.astype(bf16), preferred_element_type=f32)
            l = acc[HEAD_DIM:HEAD_DIM + 1, :] + jnp.exp(sinks[g] - m)
            return acc[0:HEAD_DIM, :] / l, m + jnp.log(l)

        outs, lses = {}, {}
        st_next = scores(*items[0])
        for idx, (g, b) in enumerate(items):
            st = st_next
            if idx + 1 < len(items):
                st_next = scores(*items[idx + 1])
            outs[g, b], lses[g, b] = finish(g, b, st)
        for g in range(SWA_KV_HEADS):
            for hh in range(SWA_GROUP):
                lanes = slice(hh * BLOCK, (hh + 1) * BLOCK)
                o_ref[g * SWA_GROUP + hh] = jnp.concatenate([outs[g, b][:, lanes] for b in range(nb)], axis=1)
                lse_ref[g * SWA_GROUP + hh] = jnp.concatenate([lses[g, b][:, lanes] for b in range(nb)], axis=1)

    def prev_blk(n):
        return jnp.maximum(n * nb - 1, 0)

    bspec = pl.BlockSpec((SWA_KV_HEADS, 2 * BLOCK, SWA_W), lambda n: (0, 0, 0))
    return pl.pallas_call(
        body,
        name="swa_fwd",
        grid=(s_len // ts,),
        in_specs=[pl.BlockSpec((SWA_HEADS, HEAD_DIM, ts), lambda n: (0, 0, n)),
                  pl.BlockSpec((SWA_KV_HEADS, ts, HEAD_DIM), lambda n: (0, n, 0)),
                  pl.BlockSpec((SWA_KV_HEADS, BLOCK, HEAD_DIM), lambda n: (0, prev_blk(n), 0)),
                  pl.BlockSpec((SWA_KV_HEADS, AUG, ts), lambda n: (0, 0, n)),
                  pl.BlockSpec((SWA_KV_HEADS, AUG, BLOCK), lambda n: (0, 0, prev_blk(n))),
                  bspec, bspec, pl.BlockSpec(memory_space=pltpu.SMEM)],
        out_specs=[pl.BlockSpec((SWA_HEADS, HEAD_DIM, ts), lambda n: (0, 0, n)),
                   pl.BlockSpec((SWA_HEADS, 1, ts), lambda n: (0, 0, n))],
        out_shape=[_sds((SWA_HEADS, HEAD_DIM, s_len), f32), _sds((SWA_HEADS, 1, s_len), f32)],
        compiler_params=_params(("arbitrary",)),
    )(qt, k, k, vta, vta, bias_t, bias0_t, sink)


def _head_selector():
    sel = np.zeros((512, 128), np.float32)
    for h in range(8):
        sel[h * HEAD_DIM:(h + 1) * HEAD_DIM, h] = 1.0
    return sel


def _post_call(of, fz, osw, sz, x2, tgt, wo, ln_g, ln_b, sel, tm):
    s_len = x2.shape[0]

    def body(of_ref, fz_ref, os_ref, sz_ref, x_ref, t_ref, wo_ref, g_ref, b_ref, sel_ref,
             dh_ref, dof_ref, dfz_ref, dos_ref, dsz_ref, dlf_ref, dls_ref, dwo_ref, dg_ref, db_ref, loss_ref):
        n = pl.program_id(0)

        @pl.when(n == 0)
        def _():
            dwo_ref[...] = jnp.zeros_like(dwo_ref)
            dg_ref[...] = jnp.zeros_like(dg_ref)
            db_ref[...] = jnp.zeros_like(db_ref)
            loss_ref[...] = jnp.zeros_like(loss_ref)

        gam = g_ref[...]
        sel_m = sel_ref[...]

        def forward(r):
            o_f = of_ref[:, r].T
            o_s = os_ref[:, r].T
            fz = fz_ref[r, :]
            sz = sz_ref[r, :]
            sg_f = jax.nn.sigmoid(fz)
            sg_s = jax.nn.sigmoid(sz)
            silu_f = fz * sg_f
            silu_s = sz * sg_s
            mixed = jnp.concatenate([o_f * silu_f, o_s * silu_s], axis=1).astype(bf16)
            y = jnp.dot(mixed, wo_ref[...], preferred_element_type=f32)
            return o_f, o_s, fz, sz, sg_f, sg_s, silu_f, silu_s, mixed, y

        def norm_and_back(r, fwd):
            mixed, y = fwd[8], fwd[9]
            h = ALPHA * x_ref[r, :] + y
            mu = jnp.mean(h, axis=1, keepdims=True)
            hc = h - mu
            var = jnp.mean(hc * hc, axis=1, keepdims=True)
            rstd = lax.rsqrt(var + LN_EPS)
            xhat = hc * rstd
            out = xhat * gam + b_ref[...]
            err = out - t_ref[r, :]
            tok_loss = jnp.mean(err * err, axis=1, keepdims=True)
            loss_ref[...] += 0.5 * jnp.sum(tok_loss, axis=0, keepdims=True)
            dout = err * (1.0 / D_MODEL)
            dg_ref[...] += jnp.sum(dout * xhat, axis=0, keepdims=True)
            db_ref[...] += jnp.sum(dout, axis=0, keepdims=True)
            dxh = dout * gam
            m1 = jnp.mean(dxh, axis=1, keepdims=True)
            m2 = jnp.mean(dxh * xhat, axis=1, keepdims=True)
            dh = rstd * (dxh - m1 - xhat * m2)
            dh_ref[r, :] = dh
            dyb = dh.astype(bf16)
            dmix = lax.dot_general(dyb, wo_ref[...], NT, preferred_element_type=f32)
            dwo_ref[...] += lax.dot_general(mixed, dyb, TN, preferred_element_type=f32)
            return dmix

        def head_sums(prod):
            hi = prod.astype(bf16)
            lo = (prod - hi.astype(f32)).astype(bf16)
            return (jnp.dot(hi, sel_m, preferred_element_type=f32) + jnp.dot(lo, sel_m, preferred_element_type=f32))

        def gates_back(r, fwd, dmix):
            o_f, o_s, fz, sz, sg_f, sg_s, silu_f, silu_s = fwd[:8]
            dm_f = dmix[:, :512]
            dm_s = dmix[:, 512:]
            do_f = dm_f * silu_f
            do_s = dm_s * silu_s
            dfz_ref[r, :] = (dm_f * o_f * (sg_f * (1.0 + fz * (1.0 - sg_f)))).astype(bf16)
            dsz_ref[r, :] = (dm_s * o_s * (sg_s * (1.0 + sz * (1.0 - sg_s)))).astype(bf16)
            dof_ref[:, r] = do_f.T.astype(bf16)
            dos_ref[:, r] = do_s.T.astype(bf16)
            dlf_ref[:, r] = head_sums(do_f * o_f).T[:FOX_HEADS, :]
            dls_ref[:, r] = head_sums(do_s * o_s).T[:SWA_HEADS, :]

        halves = [slice(k * (tm // 2), (k + 1) * (tm // 2)) for k in range(2)]
        fwds = [forward(r) for r in halves]
        dmixes = [norm_and_back(r, f) for r, f in zip(halves, fwds)]
        for r, f, d in zip(halves, fwds, dmixes):
            gates_back(r, f, d)

    feat = pl.BlockSpec((512, tm), lambda n: (0, n))
    rows8 = pl.BlockSpec((8, tm), lambda n: (0, n))
    half = pl.BlockSpec((tm, 512), lambda n: (n, 0))
    fullw = pl.BlockSpec((tm, D_MODEL), lambda n: (n, 0))
    vec = pl.BlockSpec((1, D_MODEL), lambda n: (0, 0))
    return pl.pallas_call(
        body,
        name="post_fwd_bwd",
        grid=(s_len // tm,),
        in_specs=[feat, half, feat, half, fullw, fullw,
                  pl.BlockSpec((D_MODEL, D_MODEL), lambda n: (0, 0)), vec, vec,
                  pl.BlockSpec((512, 128), lambda n: (0, 0))],
        out_specs=[fullw, feat, half, feat, half, rows8, rows8,
                   pl.BlockSpec((D_MODEL, D_MODEL), lambda n: (0, 0)), vec, vec,
                   pl.BlockSpec((1, 1), lambda n: (0, 0))],
        out_shape=[_sds((s_len, D_MODEL), f32), _sds((512, s_len), bf16), _sds((s_len, 512), bf16),
                   _sds((512, s_len), bf16), _sds((s_len, 512), bf16),
                   _sds((FOX_HEADS, s_len), f32), _sds((SWA_HEADS, s_len), f32),
                   _sds((D_MODEL, D_MODEL), f32), _sds((1, D_MODEL), f32), _sds((1, D_MODEL), f32),
                   _sds((1, 1), f32)],
        compiler_params=_params(("arbitrary",), VMEM_LIMIT_BIG),
    )(of, fz, osw, sz, x2, tgt, wo, ln_g, ln_b, sel)


def _fox_bwd_call(ka, kat, v, qat, dot, lse_row, dl_row, npairs, pair_q, pair_k):
    nh, s_len, _ = ka.shape
    t = FOX_T
    nt = s_len // t
    kmax = nt * (nt - 1) // 2
    assert nt >= 2 and nt % 2 == 0
    ck_slot = HEAD_DIM + 3
    cq_slot = HEAD_DIM

    def body(np_ref, pq_ref, pk_ref, ka_ref, kat_ref, v_ref, qat_ref, dot_ref, lse_ref, dl_ref,
             dq_ref, dk_ref, dv_ref, dcq_ref, dck_ref, dqt_all, dkat_all, dvt_all, p0, p1, ds0, ds1):
        h = pl.program_id(0)
        extra = np_ref[h]
        total = nt + extra
        dqt_all[...] = jnp.zeros(dqt_all.shape, f32)
        dkat_all[...] = jnp.zeros(dkat_all.shape, f32)
        dvt_all[...] = jnp.zeros(dvt_all.shape, f32)
        pbuf, dsbuf = (p0, p1), (ds0, ds1)

        def pair(n):
            return _fox_pair(n, nt, kmax, h, pq_ref, pk_ref)

        def probs(n, b, masked):
            i, j = pair(n)
            qc, kr = _tile_cols(i, t), _tile_cols(j, t)
            st = jnp.dot(ka_ref[0, kr, :], qat_ref[0, :, qc], preferred_element_type=f32)
            dpt = jnp.dot(v_ref[0, kr, :], dot_ref[0, :, qc], preferred_element_type=f32)
            if masked:
                st = jnp.where(_causal_keep(t), st, NEG_INF)
            pt = jnp.exp(st - lse_ref[0, :, qc])
            pbuf[b][...] = pt.astype(bf16)
            dsbuf[b][...] = (pt * (dpt - dl_ref[0, :, qc])).astype(bf16)

        def grads(n, b):
            i, j = pair(n)
            qc, kc = _tile_cols(i, t), _tile_cols(j, t)
            dvt_all[:, kc] += lax.dot_general(dot_ref[0, :, qc], pbuf[b][...], NT, preferred_element_type=f32)
            dkat_all[:, kc] += lax.dot_general(qat_ref[0, :, qc], dsbuf[b][...], NT, preferred_element_type=f32)
            dqt_all[:, qc] += jnp.dot(kat_ref[0, :, kc], dsbuf[b][...], preferred_element_type=f32)

        def step(n, b, masked):
            i, j = pair(n)
            qc, kr = _tile_cols(i, t), _tile_cols(j, t)
            i1, j1 = pair(n - 1)
            qc1, kc1 = _tile_cols(i1, t), _tile_cols(j1, t)
            c = 1 - b
            st = jnp.dot(ka_ref[0, kr, :], qat_ref[0, :, qc], preferred_element_type=f32)
            dvt_all[:, kc1] += lax.dot_general(dot_ref[0, :, qc1], pbuf[c][...], NT, preferred_element_type=f32)
            if masked:
                st = jnp.where(_causal_keep(t), st, NEG_INF)
            pt = jnp.exp(st - lse_ref[0, :, qc])
            pbuf[b][...] = pt.astype(bf16)
            dpt = jnp.dot(v_ref[0, kr, :], dot_ref[0, :, qc], preferred_element_type=f32)
            dkat_all[:, kc1] += lax.dot_general(qat_ref[0, :, qc1], dsbuf[c][...], NT, preferred_element_type=f32)
            dqt_all[:, qc1] += jnp.dot(kat_ref[0, :, kc1], dsbuf[c][...], preferred_element_type=f32)
            dsbuf[b][...] = (pt * (dpt - dl_ref[0, :, qc])).astype(bf16)

        probs(0, 0, True)
        step(1, 1, True)

        def four_steps(n, masked):
            step(n, 0, masked)
            step(n + 1, 1, masked)
            step(n + 2, 0, masked)
            step(n + 3, 1, masked)

        def diag_quads(d, _):
            four_steps(2 + 4 * d, True)
            return 0

        lax.fori_loop(0, (nt - 2) // 4, diag_quads, 0)
        if (nt - 2) % 4:
            step(nt - 2, 0, True)
            step(nt - 1, 1, True)

        def off_quads(d, _):
            four_steps(nt + 4 * d, False)
            return 0

        quads = extra // 4
        lax.fori_loop(0, quads, off_quads, 0)

        def off_steps(d, _):
            n = nt + 4 * quads + 2 * d
            step(n, 0, False)
            step(n + 1, 1, False)
            return 0

        lax.fori_loop(0, (extra % 4) // 2, off_steps, 0)

        @pl.when(extra % 2 == 1)
        def _():
            step(total - 1, 0, False)
            grads(total - 1, 0)

        @pl.when(extra % 2 == 0)
        def _():
            grads(total - 1, 1)

        dq_ref[0] = (dqt_all[0:HEAD_DIM, :] * SCALE).astype(bf16)
        dk_ref[0] = dkat_all[0:HEAD_DIM, :].astype(bf16)
        dv_ref[0] = dvt_all[...].astype(bf16)
        dcq_ref[0] = dqt_all[cq_slot:cq_slot + 1, :]
        dck_ref[0] = dkat_all[ck_slot:ck_slot + 1, :]

    smem = pl.BlockSpec(memory_space=pltpu.SMEM)
    rows = pl.BlockSpec((1, s_len, AUG), lambda h: (h, 0, 0))
    feat = pl.BlockSpec((1, AUG, s_len), lambda h: (h, 0, 0))
    feat64 = pl.BlockSpec((1, HEAD_DIM, s_len), lambda h: (h, 0, 0))
    rowv = pl.BlockSpec((1, 1, s_len), lambda h: (h, 0, 0))
    return pl.pallas_call(
        body,
        name="fox_bwd",
        grid=(nh,),
        in_specs=[smem, smem, smem, rows, feat, pl.BlockSpec((1, s_len, HEAD_DIM), lambda h: (h, 0, 0)), feat, feat64,
                  rowv, rowv],
        out_specs=[feat64, feat64, feat64, rowv, rowv],
        out_shape=[_sds((nh, HEAD_DIM, s_len), bf16)] * 3 + [_sds((nh, 1, s_len), f32)] * 2,
        scratch_shapes=[pltpu.VMEM((AUG, s_len), f32), pltpu.VMEM((AUG, s_len), f32), pltpu.VMEM((HEAD_DIM, s_len), f32)]
                       + [pltpu.VMEM((t, t), bf16)] * 4,
        compiler_params=_params(("arbitrary",)),
    )(npairs, pair_q, pair_k, ka, kat, v, qat, dot, lse_row, dl_row)


def _swa_bwd_call(qt, k, kt, v, dot, lse, dl, bias_t, bias0_t, sink):
    s_len = qt.shape[2]
    ts = SWA_TS
    nb = ts // BLOCK
    nsteps = s_len // ts

    def body(qt_ref, kc_ref, kp_ref, ktc_ref, ktp_ref, vc_ref, vp_ref, dot_ref, lse_ref, dl_ref, b_ref, b0_ref,
             sink_ref, dq_ref, dk_ref, dv_ref, dbias_ref, dsink_ref, dk_s, dv_s, tail_k, tail_v, sk_s):
        n = pl.program_id(0)

        @pl.when(n == 0)
        def _():
            dbias_ref[...] = jnp.zeros_like(dbias_ref)
            sk_s[...] = jnp.zeros_like(sk_s)

        @pl.when(n < nsteps)
        def _():
            first = n == 0
            dk_s[...] = jnp.zeros_like(dk_s)
            dv_s[...] = jnp.zeros_like(dv_s)
            groups = range(SWA_KV_HEADS)
            kall = [jnp.concatenate([kp_ref[g], kc_ref[g]], axis=0) for g in groups]
            vall = [jnp.concatenate([vp_ref[g], vc_ref[g]], axis=0) for g in groups]
            ktall = [jnp.concatenate([ktp_ref[g], ktc_ref[g]], axis=1) for g in groups]
            sinks = [_sink_row(sink_ref, g) for g in groups]
            items = [(g, b) for g in groups for b in range(nb)]

            def products(g, b):
                cols = slice(b * BLOCK, (b + 1) * BLOCK)
                win = slice(b * BLOCK, (b + 2) * BLOCK)
                qg = _group_lanes(qt_ref, g, cols)
                dog = _group_lanes(dot_ref, g, cols)
                bias_b = b_ref[g]
                if b == 0:
                    bias_b = jnp.where(first, b0_ref[g], bias_b)
                st = jnp.dot(kall[g][win], qg, preferred_element_type=f32) + bias_b
                dpt = jnp.dot(vall[g][win], dog, preferred_element_type=f32)
                return qg, dog, st, dpt

            def finish(g, b, qg, dog, st, dpt):
                cols = slice(b * BLOCK, (b + 1) * BLOCK)
                win = slice(b * BLOCK, (b + 2) * BLOCK)
                lse_r = _group_lanes(lse_ref, g, cols)
                dl_r = _group_lanes(dl_ref, g, cols)
                pt = jnp.exp(st - lse_r)
                dst = pt * (dpt - dl_r)
                dsb = dst.astype(bf16)
                dk_s[g, :, win] += lax.dot_general(qg, dsb, NT, preferred_element_type=f32)
                dv_s[g, :, win] += lax.dot_general(dog, pt.astype(bf16), NT, preferred_element_type=f32)
                dqg = jnp.dot(ktall[g][:, win], dsb, preferred_element_type=f32) * SCALE
                return dqg, dst, -jnp.exp(sinks[g] - lse_r) * dl_r

            dqs, dsts, sks = {}, {}, {}
            nxt = products(*items[0])
            for idx, (g, b) in enumerate(items):
                cur = nxt
                if idx + 1 < len(items):
                    nxt = products(*items[idx + 1])
                dqs[g, b], dsts[g, b], sks[g, b] = finish(g, b, *cur)
            for g in groups:
                dbias_ref[g] += functools.reduce(lambda a, c: a + c, [dsts[g, b] for b in range(nb)])
                sk_s[g] += functools.reduce(lambda a, c: a + c, [sks[g, b] for b in range(nb)])
                for hh in range(SWA_GROUP):
                    lanes = slice(hh * BLOCK, (hh + 1) * BLOCK)
                    dq_ref[g * SWA_GROUP + hh] = jnp.concatenate(
                        [dqs[g, b][:, lanes] for b in range(nb)], axis=1).astype(bf16)

        @pl.when(n > 0)
        def _():
            last = slice(ts - BLOCK, ts)
            for g in range(SWA_KV_HEADS):
                add_k = jnp.where(n < nsteps, dk_s[g, :, 0:BLOCK], 0.0)
                add_v = jnp.where(n < nsteps, dv_s[g, :, 0:BLOCK], 0.0)
                dk_ref[g, :, 0:ts - BLOCK] = tail_k[g, :, 0:ts - BLOCK].astype(bf16)
                dv_ref[g, :, 0:ts - BLOCK] = tail_v[g, :, 0:ts - BLOCK].astype(bf16)
                dk_ref[g, :, last] = (tail_k[g, :, last] + add_k).astype(bf16)
                dv_ref[g, :, last] = (tail_v[g, :, last] + add_v).astype(bf16)

        @pl.when(n < nsteps)
        def _():
            tail_k[...] = dk_s[:, :, BLOCK:]
            tail_v[...] = dv_s[:, :, BLOCK:]

        @pl.when(n == nsteps)
        def _():
            row = lax.broadcasted_iota(jnp.int32, (SWA_HEADS, 128), 0)
            out = jnp.zeros((SWA_HEADS, 128), f32)
            for h in range(SWA_HEADS):
                g, hh = divmod(h, SWA_GROUP)
                val = jnp.sum(sk_s[g, :, hh * BLOCK:(hh + 1) * BLOCK], axis=1, keepdims=True)
                out = jnp.where(row == h, val, out)
            dsink_ref[...] = out

    last_step = nsteps - 1

    def cl(n):
        return jnp.minimum(n, last_step)

    def prev_blk(n):
        return jnp.maximum(cl(n) * nb - 1, 0)

    feat8 = pl.BlockSpec((SWA_HEADS, HEAD_DIM, ts), lambda n: (0, 0, cl(n)))
    rows8 = pl.BlockSpec((SWA_HEADS, 1, ts), lambda n: (0, 0, cl(n)))
    cur = pl.BlockSpec((SWA_KV_HEADS, ts, HEAD_DIM), lambda n: (0, cl(n), 0))
    prev = pl.BlockSpec((SWA_KV_HEADS, BLOCK, HEAD_DIM), lambda n: (0, prev_blk(n), 0))
    curt = pl.BlockSpec((SWA_KV_HEADS, HEAD_DIM, ts), lambda n: (0, 0, cl(n)))
    prevt = pl.BlockSpec((SWA_KV_HEADS, HEAD_DIM, BLOCK), lambda n: (0, 0, prev_blk(n)))
    bspec = pl.BlockSpec((SWA_KV_HEADS, 2 * BLOCK, SWA_W), lambda n: (0, 0, 0))
    kvout = pl.BlockSpec((SWA_KV_HEADS, HEAD_DIM, ts), lambda n: (0, 0, jnp.maximum(n - 1, 0)))
    return pl.pallas_call(
        body,
        name="swa_bwd",
        grid=(nsteps + 1,),
        in_specs=[feat8, cur, prev, curt, prevt, cur, prev, feat8, rows8, rows8, bspec, bspec,
                  pl.BlockSpec(memory_space=pltpu.SMEM)],
        out_specs=[feat8, kvout, kvout, bspec, pl.BlockSpec((SWA_HEADS, 128), lambda n: (0, 0))],
        out_shape=[_sds((SWA_HEADS, HEAD_DIM, s_len), bf16), _sds((SWA_KV_HEADS, HEAD_DIM, s_len), bf16),
                   _sds((SWA_KV_HEADS, HEAD_DIM, s_len), bf16),
                   _sds((SWA_KV_HEADS, 2 * BLOCK, SWA_W), f32), _sds((SWA_HEADS, 128), f32)],
        scratch_shapes=[pltpu.VMEM((SWA_KV_HEADS, HEAD_DIM, ts + BLOCK), f32),
                        pltpu.VMEM((SWA_KV_HEADS, HEAD_DIM, ts + BLOCK), f32),
                        pltpu.VMEM((SWA_KV_HEADS, HEAD_DIM, ts), f32),
                        pltpu.VMEM((SWA_KV_HEADS, HEAD_DIM, ts), f32),
                        pltpu.VMEM((SWA_KV_HEADS, 1, SWA_W), f32)],
        compiler_params=_params(("arbitrary",)),
    )(qt, k, k, kt, kt, v, v, dot, lse, dl, bias_t, bias0_t, sink)


def _dproj_specs(tm):
    half = pl.BlockSpec((tm, 512), lambda i: (i, 0))
    feat = pl.BlockSpec((512, tm), lambda i: (0, i))
    feat_kv = pl.BlockSpec((128, tm), lambda i: (0, i))
    return [feat, feat, feat, half, feat, feat_kv, feat_kv, half, feat_kv]


def _dx_exchange_call(dh, pieces, w_t, bs, tm):
    s_len = dh.shape[0]
    n = len(bs)
    last = s_len // tm - 1

    def body(*refs):
        dh_ref, dqf_ref, dkf_ref, dvf_ref, dfz_ref, dqs_ref, dks_ref, dvs_ref, dsz_ref, dfft_ref, w_ref = refs[:11]
        b_refs = refs[11:11 + n]
        dx_ref = refs[11 + n]
        r_refs = refs[12 + n:12 + 2 * n]
        sems = refs[12 + 2 * n:]
        i = pl.program_id(0)

        @pl.when(i == 0)
        def _():
            _exchange_start(b_refs, r_refs, sems)

        def tr(ref):
            return ref[...].astype(f32).T.astype(bf16)

        dp = jnp.concatenate([tr(dqf_ref), tr(dkf_ref), tr(dvf_ref), dfz_ref[...], tr(dqs_ref), tr(dks_ref),
                              tr(dvs_ref), dsz_ref[...], tr(dfft_ref)], axis=1)
        dx_ref[...] = ALPHA * dh_ref[...] + jnp.dot(dp, w_ref[...], preferred_element_type=f32)

        @pl.when(i == last)
        def _():
            _exchange_wait(b_refs, r_refs, sems)

    fullw = pl.BlockSpec((tm, D_MODEL), lambda i: (i, 0))
    any_spec = pl.BlockSpec(memory_space=pl.ANY)
    out = pl.pallas_call(
        body,
        name="dx_bwd_exchange",
        grid=(s_len // tm,),
        in_specs=[fullw] + _dproj_specs(tm) + [pl.BlockSpec((A_W, D_MODEL), lambda i: (0, 0))] + [any_spec] * n,
        out_specs=[fullw] + [any_spec] * n,
        out_shape=[_sds((s_len, D_MODEL), f32)] + [_sds(b.shape, b.dtype) for b in bs],
        scratch_shapes=[pltpu.SemaphoreType.DMA((7 * n,)), pltpu.SemaphoreType.DMA((7 * n,)),
                        pltpu.SemaphoreType.DMA((n,))],
        compiler_params=_params(("arbitrary",)),
    )(dh, *pieces, w_t, *bs)
    return out[0], out[1:]


DW_STAGE_ROWS = 384


def _dw_call(x2, pieces, tm):
    s_len = x2.shape[0]
    nt = s_len // tm

    def body(x_ref, dqf_ref, dkf_ref, dvf_ref, dfz_ref, dqs_ref, dks_ref, dvs_ref, dsz_ref, dfft_ref, dw_ref,
             acc_ref, stage_ref, sem):
        i = pl.program_id(0)

        @pl.when(i == 0)
        def _():
            acc_ref[...] = jnp.zeros_like(acc_ref)

        xb = x_ref[...].astype(bf16)

        def add_feat(off, lhs):
            acc_ref[off:off + lhs.shape[0], :] += jnp.dot(lhs, xb, preferred_element_type=f32)

        def add_rows(off, piece):
            acc_ref[off:off + piece.shape[1], :] += lax.dot_general(piece, xb, TN, preferred_element_type=f32)

        add_feat(A_FQ, dqf_ref[...])
        add_feat(A_FK, dkf_ref[...])
        add_feat(A_FV, dvf_ref[...])
        add_rows(A_FZ, dfz_ref[...])
        add_feat(A_SQ, dqs_ref[...])
        add_feat(A_SK, dks_ref[...])
        add_feat(A_SV, dvs_ref[...])
        add_rows(A_SZ, dsz_ref[...])
        add_feat(A_FF, dfft_ref[...].astype(bf16))

        @pl.when(i == nt - 1)
        def _():
            for r in range(A_W // DW_STAGE_ROWS):
                rows = slice(r * DW_STAGE_ROWS, (r + 1) * DW_STAGE_ROWS)
                stage_ref[...] = acc_ref[rows, :].astype(bf16)
                cp = pltpu.make_async_copy(stage_ref, dw_ref.at[rows, :], sem)
                cp.start()
                cp.wait()

    return pl.pallas_call(
        body,
        name="dw_in_bwd",
        grid=(nt,),
        in_specs=[pl.BlockSpec((tm, D_MODEL), lambda i: (i, 0))] + _dproj_specs(tm),
        out_specs=pl.BlockSpec(memory_space=pl.ANY),
        out_shape=_sds((A_W, D_MODEL), bf16),
        scratch_shapes=[pltpu.VMEM((A_W, D_MODEL), f32), pltpu.VMEM((DW_STAGE_ROWS, D_MODEL), bf16),
                        pltpu.SemaphoreType.DMA],
        compiler_params=_params(("arbitrary",), VMEM_LIMIT_BIG),
    )(x2, *pieces)


def _adam_call(recv, w, m, v, tc, name):
    rows, cols = w.shape

    def body(r_ref, w_ref, m_ref, v_ref, g_ref, d_ref, mo_ref, vo_ref):
        g = r_ref[0].astype(f32)
        for p in range(1, N_DEV):
            g = g + r_ref[p].astype(f32)
        mn = ADAM_B1 * m_ref[...] + (1.0 - ADAM_B1) * g
        vn = ADAM_B2 * v_ref[...] + (1.0 - ADAM_B2) * (g * g)
        m_hat = mn / (1.0 - ADAM_B1 ** ADAM_STEP)
        v_hat = vn / (1.0 - ADAM_B2 ** ADAM_STEP)
        g_ref[...] = g
        d_ref[...] = -ADAM_LR * (m_hat / (jnp.sqrt(v_hat) + ADAM_EPS) + ADAM_WD * w_ref[...])
        mo_ref[...] = mn
        vo_ref[...] = vn

    blk = pl.BlockSpec((rows, tc), lambda i: (0, i))
    return pl.pallas_call(
        body,
        name=name,
        grid=(cols // tc,),
        in_specs=[pl.BlockSpec((N_DEV, rows, tc), lambda i: (0, 0, i)), blk, blk, blk],
        out_specs=[blk] * 4,
        out_shape=[_sds((rows, cols), f32)] * 4,
        compiler_params=_params(("arbitrary",)),
    )(recv, w, m, v)


def _rows_to_shards(parts, shard):
    blocks = []
    for d in range(N_DEV):
        lo, hi, start, pieces = d * shard, (d + 1) * shard, 0, []
        for part in parts:
            a, b = max(lo, start), min(hi, start + part.shape[0])
            if a < b:
                pieces.append(part[a - start:b - start])
            start += part.shape[0]
        blocks.append(jnp.concatenate(pieces, axis=0))
    return jnp.stack(blocks)


def _pad_cols(a, width=128):
    return jnp.pad(a, ((0, 0), (0, width - a.shape[1])))


def _pack_small(ln_g, ln_b, rel, b_f, sink):
    return jnp.concatenate([
        ln_g.reshape(8, 128), ln_b.reshape(8, 128), _pad_cols(rel),
        jnp.pad(_pad_cols(b_f), ((0, 7), (0, 0))), jnp.pad(_pad_cols(sink), ((0, 7), (0, 0)))], axis=0)


def _unpack_small(p):
    return (p[0:8].reshape(1, D_MODEL), p[8:16].reshape(1, D_MODEL), p[16:48, 0:8], p[48:49, 0:8], p[56:57, 0:8])


def kernel(x, w_in, b_f, rel_bias, sink, w_o, ln_g, ln_b, loss_target, m_w_in, m_b_f, m_rel_bias, m_sink, m_w_o, m_ln_g, m_ln_b, v_w_in, v_b_f, v_rel_bias, v_sink, v_w_o, v_ln_g, v_ln_b):
    x2 = x[0]
    tgt = loss_target[0]
    s_len = x2.shape[0]
    shard = w_in.shape[2]

    w_in_t = jnp.transpose(w_in[0])
    g_in, g_o = _gather_call([w_in_t.astype(bf16), w_o[0].astype(bf16)])
    wt_full = g_in.reshape(N_DEV * shard, D_MODEL)
    w_t = jnp.concatenate([wt_full[:O_FF0], wt_full[O_FF1:], wt_full[O_FF0:O_FF1],
                           jnp.zeros((A_W - D_IN, D_MODEL), bf16)], axis=0)
    wo_full = g_o.reshape(D_MODEL, D_MODEL)

    qft, kft, vf, fz, qst, ks, vs, sz, fft, vat, kst, vsta = _proj_call(x2, w_t, 512)
    cum, sgm = _cum_call(fft, b_f.reshape(FOX_HEADS, 1))
    qat, ka, kat, tile_stats = _augment_call(qft, kft, cum.reshape(FOX_HEADS, 1, s_len), 2048)
    npairs, pair_q, pair_k = _fox_prune_tables(tile_stats)
    o_ft, lse_f = _fox_fwd_call(qat, ka, vat, npairs, pair_q, pair_k)
    bucket_t = jnp.asarray(_t5_bucket_table().T)
    bias_t, bias0_t = _swa_bias_call(rel_bias, bucket_t)
    sink_v = sink.reshape(SWA_HEADS)
    o_st, lse_s = _swa_fwd_call(qst, ks, vsta, bias_t, bias0_t, sink_v)

    (dh, do_f, dfz, do_s, dsz, dl_f, dl_s, dwo, dg, db, loss_part) = _post_call(
        o_ft.reshape(FOX_HEADS * HEAD_DIM, s_len), fz, o_st.reshape(SWA_HEADS * HEAD_DIM, s_len), sz, x2, tgt,
        wo_full, ln_g, ln_b, jnp.asarray(_head_selector()).astype(bf16), 512)

    dqf, dkf, dvf, dcq, dck = _fox_bwd_call(ka, kat, vf, qat, do_f.reshape(FOX_HEADS, HEAD_DIM, s_len), lse_f,
                                            dl_f.reshape(FOX_HEADS, 1, s_len), npairs, pair_q, pair_k)
    dqf, dkf, dvf = (a.reshape(FOX_HEADS * HEAD_DIM, s_len) for a in (dqf, dkf, dvf))
    dfft, dbf = _cum_bwd_call(dcq.reshape(FOX_HEADS, s_len), dck.reshape(FOX_HEADS, s_len), sgm)
    dqs, dks, dvs, dbias, dsink = _swa_bwd_call(
        qst, ks, kst, vs, do_s.reshape(SWA_HEADS, HEAD_DIM, s_len), lse_s, dl_s.reshape(SWA_HEADS, 1, s_len),
        bias_t, bias0_t, sink_v)
    dqs = dqs.reshape(SWA_HEADS * HEAD_DIM, s_len)
    dks, dvs = (a.reshape(SWA_KV_HEADS * HEAD_DIM, s_len) for a in (dks, dvs))
    drel = _swa_bias_bwd_call(dbias, bucket_t)

    pieces = (dqf, dkf, dvf, dfz, dqs, dks, dvs, dsz, dfft)
    dw_t = _dw_call(x2, pieces, 1024)

    dw_blocks = _rows_to_shards([dw_t[:O_FF0], dw_t[A_FF:A_FF + (O_FF1 - O_FF0)], dw_t[O_FF0:A_FF]], shard)
    dwo_blocks = dwo.reshape(N_DEV, D_MODEL // N_DEV, D_MODEL).astype(bf16)
    small = _pack_small(dg, db, drel[:, 0:8], dbf[:, 0].reshape(1, 8), dsink[:, 0].reshape(1, 8))
    loss_slot = np.zeros((64, 128), bool)
    loss_slot[49, 0] = True
    small = jnp.where(jnp.asarray(loss_slot), loss_part[0, 0], small)
    small_blocks = jnp.broadcast_to(small[None], (N_DEV,) + small.shape)
    dx, (r_in, r_o, r_small) = _dx_exchange_call(dh, pieces, w_t, [dw_blocks, dwo_blocks, small_blocks], 256)

    win_t = [jnp.transpose(a) for a in _adam_call(
        r_in, w_in_t, jnp.transpose(m_w_in[0]), jnp.transpose(v_w_in[0]), 256, "adam_w_in")]
    g_win, d_win, nm_win, nv_win = win_t
    g_wo, d_wo, nm_wo, nv_wo = _adam_call(r_o, w_o[0], m_w_o[0], v_w_o[0], 256, "adam_w_o")
    p_w = _pack_small(ln_g, ln_b, rel_bias, b_f, sink)
    p_m = _pack_small(m_ln_g, m_ln_b, m_rel_bias, m_b_f, m_sink)
    p_v = _pack_small(v_ln_g, v_ln_b, v_rel_bias, v_b_f, v_sink)
    g_p, d_p, nm_p, nv_p = _adam_call(r_small, p_w, p_m, p_v, 128, "adam_small")

    loss = g_p[49, 0]
    g_lng, g_lnb, g_rel, g_bf, g_sink = _unpack_small(g_p)
    d_lng, d_lnb, d_rel, d_bf, d_sink = _unpack_small(d_p)
    m_lng, m_lnb, m_rel, m_bf, m_sk = _unpack_small(nm_p)
    v_lng, v_lnb, v_rel, v_bf, v_sk = _unpack_small(nv_p)
    return (loss, dx[None], g_win[None], g_bf, g_rel, g_sink, g_wo[None], g_lng, g_lnb,
            d_win[None], d_bf, d_rel, d_sink, d_wo[None], d_lng, d_lnb,
            nm_win[None], m_bf, m_rel, m_sk, nm_wo[None], m_lng, m_lnb,
            nv_win[None], v_bf, v_rel, v_sk, nv_wo[None], v_lng, v_lnb)
```

```python
import functools
import math

import numpy as np
import jax
import jax.numpy as jnp
from jax import lax
from jax.experimental import pallas as pl
from jax.experimental.pallas import tpu as pltpu

f32 = jnp.float32
bf16 = jnp.bfloat16

D_MODEL = 1024
HEAD_DIM = 64
FOX_HEADS = 8
SWA_HEADS = 8
SWA_KV_HEADS = 2
SWA_GROUP = 4
BLOCK = 128
NUM_BUCKETS = 32
MAX_DISTANCE = 128
LN_EPS = 1e-5
NEG_INF = -1e30
ALPHA = 2.0 ** 0.25
SCALE = 1.0 / math.sqrt(HEAD_DIM)
D_IN = 3336

ADAM_LR = 0.001
ADAM_B1 = 0.9
ADAM_B2 = 0.999
ADAM_EPS = 1e-08
ADAM_WD = 0.01
ADAM_STEP = 10

N_DEV = 8
A_FQ, A_FK, A_FV, A_FZ, A_SQ, A_SK, A_SV, A_SZ, A_FF, A_W = 0, 512, 1024, 1536, 2048, 2560, 2688, 2816, 3328, 3456
O_FF0, O_FF1 = 1536, 1544

VMEM_LIMIT = 48 * 1024 * 1024
HIGHEST = lax.Precision.HIGHEST
NT = (((1,), (1,)), ((), ()))
TN = (((0,), (0,)), ((), ()))
MESH = pl.DeviceIdType.MESH
RELS = [(0, 0, 1), (0, 1, 0), (0, 1, 1), (1, 0, 0), (1, 0, 1), (1, 1, 0), (1, 1, 1)]


VMEM_LIMIT_BIG = 60 * 1024 * 1024


def _params(sem=None, vmem=VMEM_LIMIT):
    return pltpu.CompilerParams(dimension_semantics=sem, vmem_limit_bytes=vmem)


def _sds(shape, dtype):
    return jax.ShapeDtypeStruct(shape, dtype)


def _t5_bucket_table():
    qi = np.arange(BLOCK)[:, None]
    kj = np.arange(2 * BLOCK)[None, :]
    rel = qi + BLOCK - kj
    band = (rel >= 0) & (rel < BLOCK)
    relc = np.maximum(rel, 0)
    max_exact = NUM_BUCKETS // 2
    relf = np.maximum(relc, 1).astype(np.float32)
    large = max_exact + (np.log(relf / np.float32(max_exact)) / np.float32(math.log(MAX_DISTANCE / max_exact))
                         * np.float32(NUM_BUCKETS - max_exact)).astype(np.int32)
    large = np.minimum(large, NUM_BUCKETS - 1)
    bucket = np.where(relc < max_exact, relc, large).astype(np.int32)
    bucket = np.where(band, bucket, -1).astype(np.int32)
    return bucket


def _mesh_pos():
    return lax.axis_index("x"), lax.axis_index("y"), lax.axis_index("c")


def _dev_index(p):
    return 4 * p[0] + 2 * p[1] + p[2]


def _gather_call(xs):
    n = len(xs)

    def body(*refs):
        x_refs, o_refs = refs[:n], refs[n:2 * n]
        send_sems, recv_sems, local_sems = refs[2 * n:]
        x, y, c = _mesh_pos()
        me, sib = (x, y, c), (x, y, 1 - c)
        chips = [(1 - x, y), (x, 1 - y), (1 - x, 1 - y)]

        def copy(a, k, block, to, src=None):
            slot = o_refs[a].at[_dev_index(block)]
            return pltpu.make_async_remote_copy(
                src_ref=slot if src is None else src, dst_ref=slot,
                send_sem=send_sems.at[a * 7 + k], recv_sem=recv_sems.at[a * 7 + k],
                device_id=to, device_id_type=MESH)

        mine = [pltpu.make_async_copy(x_refs[a], o_refs[a].at[_dev_index(me)], local_sems.at[a]) for a in range(n)]
        for cp in mine:
            cp.start()
        first = []
        for a in range(n):
            first.append(copy(a, 0, me, sib, src=x_refs[a]))
            first += [copy(a, 1 + j, me, (*chip, c), src=x_refs[a]) for j, chip in enumerate(chips)]
        for cp in first:
            cp.start()
        passed = []
        for j, chip in enumerate(chips):
            for a in range(n):
                copy(a, 1 + j, (*chip, c), me).wait_recv()
                fwd = copy(a, 4 + j, (*chip, c), sib)
                fwd.start()
                passed.append(fwd)
        for a in range(n):
            copy(a, 0, sib, me).wait_recv()
            for j, chip in enumerate(chips):
                copy(a, 4 + j, (*chip, 1 - c), me).wait_recv()
        for cp in first + passed:
            cp.wait_send()
        for cp in mine:
            cp.wait()

    any_spec = pl.BlockSpec(memory_space=pl.ANY)
    return pl.pallas_call(
        body,
        name="gather_weights",
        out_shape=[_sds((N_DEV,) + a.shape, a.dtype) for a in xs],
        in_specs=[any_spec] * n,
        out_specs=[any_spec] * n,
        scratch_shapes=[pltpu.SemaphoreType.DMA((7 * n,)), pltpu.SemaphoreType.DMA((7 * n,)),
                        pltpu.SemaphoreType.DMA((n,))],
    )(*xs)


def _exchange_copies(b_refs, r_refs, send_sems, recv_sems, local_sems, incoming):
    n = len(b_refs)
    x, y, c = _mesh_pos()
    me_idx = _dev_index((x, y, c))
    mine = [pltpu.make_async_copy(b_refs[a].at[me_idx], r_refs[a].at[me_idx], local_sems.at[a]) for a in range(n)]
    remote = []
    for k, r in enumerate(RELS):
        peer = ((1 - x) if r[0] else x, (1 - y) if r[1] else y, (1 - c) if r[2] else c)
        pidx = _dev_index(peer)
        for a in range(n):
            remote.append(pltpu.make_async_remote_copy(
                src_ref=b_refs[a].at[pidx], dst_ref=r_refs[a].at[pidx if incoming else me_idx],
                send_sem=send_sems.at[a * 7 + k], recv_sem=recv_sems.at[a * 7 + k],
                device_id=peer, device_id_type=MESH))
    return mine, remote


def _exchange_start(b_refs, r_refs, sems):
    mine, out = _exchange_copies(b_refs, r_refs, *sems, incoming=False)
    for cp in mine + out:
        cp.start()


def _exchange_wait(b_refs, r_refs, sems):
    mine, inc = _exchange_copies(b_refs, r_refs, *sems, incoming=True)
    for cp in inc:
        cp.wait_recv()
    for cp in inc:
        cp.wait_send()
    for cp in mine:
        cp.wait()


def _proj_call(x2, w_t, tm):
    s_len = x2.shape[0]

    def body(x_ref, w_ref, qft_ref, kft_ref, vf_ref, fz_ref, qst_ref, ks_ref, vs_ref, sz_ref, fft_ref, vat_ref,
             kst_ref, vsta_ref):
        xb = x_ref[...].astype(bf16)

        def seg_t(off, width):
            return lax.dot_general(w_ref[off:off + width, :], xb, NT, preferred_element_type=f32)

        def seg(off, width):
            return lax.dot_general(xb, w_ref[off:off + width, :], NT, preferred_element_type=f32)

        def put_heads(ref, acc, nheads):
            for h in range(nheads):
                ref[h] = acc[:, h * HEAD_DIM:(h + 1) * HEAD_DIM].astype(bf16)

        def put_heads_t(ref, acc_t, nheads, augment):
            for h in range(nheads):
                ref[h, 0:HEAD_DIM, :] = acc_t[h * HEAD_DIM:(h + 1) * HEAD_DIM, :].astype(bf16)
                if augment:
                    ref[h, HEAD_DIM:2 * HEAD_DIM, :] = ones_row

        ones_row = jnp.where(lax.broadcasted_iota(jnp.int32, (HEAD_DIM, tm), 0) == 0, 1.0, 0.0).astype(bf16)
        put_heads_t(vat_ref, seg_t(A_FV, 512), FOX_HEADS, True)
        put_heads_t(qft_ref, seg_t(A_FQ, 512) * SCALE, FOX_HEADS, False)
        put_heads_t(kft_ref, seg_t(A_FK, 512), FOX_HEADS, False)
        put_heads(vf_ref, seg(A_FV, 512), FOX_HEADS)
        fz_ref[...] = seg(A_FZ, 512)
        put_heads_t(qst_ref, seg_t(A_SQ, 512) * SCALE, SWA_HEADS, False)
        put_heads(ks_ref, seg(A_SK, 128), SWA_KV_HEADS)
        put_heads(vs_ref, seg(A_SV, 128), SWA_KV_HEADS)
        put_heads_t(kst_ref, seg_t(A_SK, 128), SWA_KV_HEADS, False)
        put_heads_t(vsta_ref, seg_t(A_SV, 128), SWA_KV_HEADS, True)
        sz_ref[...] = seg(A_SZ, 512)
        fft_ref[...] = seg(A_FF, 128).T[:FOX_HEADS, :]

    def heads(nh):
        return pl.BlockSpec((nh, tm, HEAD_DIM), lambda i: (0, i, 0))

    def feat(nh, rows):
        return pl.BlockSpec((nh, rows, tm), lambda i: (0, 0, i))

    wide = pl.BlockSpec((tm, 512), lambda i: (i, 0))
    return pl.pallas_call(
        body,
        name="proj_fwd",
        grid=(s_len // tm,),
        in_specs=[pl.BlockSpec((tm, D_MODEL), lambda i: (i, 0)), pl.BlockSpec((A_W, D_MODEL), lambda i: (0, 0))],
        out_specs=[feat(8, HEAD_DIM), feat(8, HEAD_DIM), heads(8), wide, feat(8, HEAD_DIM), heads(2), heads(2), wide,
                   pl.BlockSpec((FOX_HEADS, tm), lambda i: (0, i)),
                   feat(FOX_HEADS, 2 * HEAD_DIM), feat(2, HEAD_DIM), feat(2, 2 * HEAD_DIM)],
        out_shape=[_sds((8, HEAD_DIM, s_len), bf16)] * 2 + [_sds((8, s_len, HEAD_DIM), bf16)]
                  + [_sds((s_len, 512), f32), _sds((8, HEAD_DIM, s_len), bf16),
                     _sds((2, s_len, HEAD_DIM), bf16), _sds((2, s_len, HEAD_DIM), bf16), _sds((s_len, 512), f32),
                     _sds((FOX_HEADS, s_len), f32), _sds((FOX_HEADS, 2 * HEAD_DIM, s_len), bf16),
                     _sds((2, HEAD_DIM, s_len), bf16), _sds((2, 2 * HEAD_DIM, s_len), bf16)],
        compiler_params=_params(("arbitrary",)),
    )(x2, w_t)


AUG = 2 * HEAD_DIM
NEAR_KEYS = 3


def _augment_call(q_t, k_t, cum_row, tm):
    nh, _, s_len = k_t.shape
    per_step = tm // FOX_T

    def body(qt_ref, kt_ref, c_ref, qat_ref, ka_ref, kat_ref, st_ref):
        c = c_ref[0]
        hi = c.astype(bf16).astype(f32)
        r1 = c - hi
        mid = r1.astype(bf16).astype(f32)
        lo = (r1 - mid).astype(bf16).astype(f32)
        row = lax.broadcasted_iota(jnp.int32, (HEAD_DIM, tm), 0)
        q_tail = jnp.where(row == 0, hi, jnp.where(row == 1, mid, jnp.where(row == 2, lo,
                           jnp.where(row < 6, 1.0, 0.0))))
        k_tail = jnp.where(row < 3, 1.0, jnp.where(row == 3, -hi, jnp.where(row == 4, -mid,
                           jnp.where(row == 5, -lo, 0.0))))
        qat_ref[0, 0:HEAD_DIM, :] = qt_ref[0]
        qat_ref[0, HEAD_DIM:AUG, :] = q_tail.astype(bf16)
        kat_ref[0, 0:HEAD_DIM, :] = kt_ref[0]
        kat_ref[0, HEAD_DIM:AUG, :] = k_tail.astype(bf16)
        qt = qt_ref[0].astype(f32)
        kt = kt_ref[0].astype(f32)
        ka_ref[0] = jnp.concatenate([kt, k_tail], axis=0).T.astype(bf16)
        qn2 = jnp.sum(qt * qt, axis=0, keepdims=True)
        kn2 = jnp.sum(kt * kt, axis=0, keepdims=True)
        sd = jnp.sum(qt * kt, axis=0, keepdims=True)
        k_and_c = jnp.concatenate([kt, jnp.broadcast_to(c, (8, tm))], axis=0)
        lane = lax.broadcasted_iota(jnp.int32, (1, tm), 1)
        for shift in range(1, NEAR_KEYS + 1):
            prev = pltpu.roll(k_and_c, shift, axis=1)
            near = jnp.sum(qt * prev[0:HEAD_DIM], axis=0, keepdims=True) + (c - prev[HEAD_DIM:HEAD_DIM + 1])
            sd = jnp.maximum(sd, jnp.where(lane >= shift, near, NEG_INF))
        srow = lax.broadcasted_iota(jnp.int32, (8, LANES), 0)
        for part in range(per_step):
            sl = slice(part * FOX_T, (part + 1) * FOX_T)
            vals = [jnp.sqrt(jnp.max(qn2[:, sl], axis=1, keepdims=True)),
                    jnp.sqrt(jnp.max(kn2[:, sl], axis=1, keepdims=True)),
                    jnp.min(sd[:, sl], axis=1, keepdims=True),
                    jnp.max(c[:, sl], axis=1, keepdims=True), jnp.min(c[:, sl], axis=1, keepdims=True)]
            out = jnp.zeros((8, LANES), f32)
            for r, val in enumerate(vals):
                out = jnp.where(srow == r, val, out)
            st_ref[0, part] = out

    tile_t = pl.BlockSpec((1, HEAD_DIM, tm), lambda h, i: (h, 0, i))
    return pl.pallas_call(
        body,
        name="fox_augment",
        grid=(nh, s_len // tm),
        in_specs=[tile_t, tile_t, pl.BlockSpec((1, 1, tm), lambda h, i: (h, 0, i))],
        out_specs=[pl.BlockSpec((1, AUG, tm), lambda h, i: (h, 0, i)),
                   pl.BlockSpec((1, tm, AUG), lambda h, i: (h, i, 0)),
                   pl.BlockSpec((1, AUG, tm), lambda h, i: (h, 0, i)),
                   pl.BlockSpec((1, per_step, 8, LANES), lambda h, i: (h, i, 0, 0))],
        out_shape=[_sds((nh, AUG, s_len), bf16), _sds((nh, s_len, AUG), bf16), _sds((nh, AUG, s_len), bf16),
                   _sds((nh, s_len // FOX_T, 8, LANES), f32)],
        compiler_params=_params(("arbitrary", "arbitrary")),
    )(q_t, k_t, cum_row)


FOX_PRUNE_GAP = 32.0


def _fox_prune_tables(stats):
    s = stats[:, :, :, 0]
    qn, kn, sd, cmx, cmn = (s[:, :, r] for r in range(5))
    nt = s.shape[1]
    bound = qn[:, :, None] * kn[:, None, :] + (cmx[:, :, None] - cmn[:, None, :])
    margin = 0.01 + 1e-5 * (jnp.abs(cmx)[:, :, None] + jnp.abs(cmn)[:, None, :])
    qi = lax.broadcasted_iota(jnp.int32, (nt, nt), 0)
    kj = lax.broadcasted_iota(jnp.int32, (nt, nt), 1)
    skip = (bound + margin < sd[:, :, None] - FOX_PRUNE_GAP) & (kj < qi)[None]
    first = jnp.sum(jnp.cumprod(skip.astype(jnp.int32), axis=2), axis=2)
    tiles = lax.broadcasted_iota(jnp.int32, (1, nt), 1)
    cnt = tiles - first
    ends = jnp.cumsum(cnt, axis=1)
    off = ends - cnt
    kmax = nt * (nt - 1) // 2
    k = lax.broadcasted_iota(jnp.int32, (1, kmax), 1)
    pair_q = jnp.minimum(jnp.sum((ends[:, None, :] <= k[:, :, None]).astype(jnp.int32), axis=2), nt - 1)
    hit = pair_q[:, :, None] == tiles[:, None, :]
    first_k = jnp.sum(jnp.where(hit, first[:, None, :], 0), axis=2)
    off_k = jnp.sum(jnp.where(hit, off[:, None, :], 0), axis=2)
    pair_k = jnp.clip(first_k + k - off_k, 0, nt - 1)
    return (ends[:, nt - 1].astype(jnp.int32), pair_q.reshape(-1).astype(jnp.int32),
            pair_k.reshape(-1).astype(jnp.int32))


CUM_CHUNK = 512


def _cum_call(fft, bf_col):
    s_len = fft.shape[1]
    ch = CUM_CHUNK

    def body(f_ref, b_ref, cum_ref, sg_ref):
        r = lax.broadcasted_iota(jnp.int32, (ch, ch), 0)
        c = lax.broadcasted_iota(jnp.int32, (ch, ch), 1)
        upper = (r <= c).astype(f32)
        carry = jnp.zeros((FOX_HEADS, 1), f32)
        for n in range(s_len // ch):
            z = f_ref[:, n * ch:(n + 1) * ch] + b_ref[...]
            logf = jnp.minimum(z, 0.0) - jnp.log1p(jnp.exp(-jnp.abs(z)))
            sg_ref[:, n * ch:(n + 1) * ch] = 1.0 / (1.0 + jnp.exp(z))
            cs = jnp.dot(logf, upper, precision=HIGHEST, preferred_element_type=f32) + carry
            cum_ref[:, n * ch:(n + 1) * ch] = cs
            carry = cs[:, ch - 1:ch]

    return pl.pallas_call(
        body,
        name="fox_cum_fwd",
        out_shape=[_sds((FOX_HEADS, s_len), f32)] * 2,
        compiler_params=_params(),
    )(fft, bf_col)


def _cum_bwd_call(dcq, dck, sg):
    s_len = sg.shape[1]
    ch = CUM_CHUNK
    nch = s_len // ch

    def body(q_ref, k_ref, sg_ref, dff_ref, dbf_ref):
        r = lax.broadcasted_iota(jnp.int32, (ch, ch), 0)
        c = lax.broadcasted_iota(jnp.int32, (ch, ch), 1)
        lower = (r >= c).astype(f32)
        dff_ref[...] = jnp.zeros_like(dff_ref)
        carry = jnp.zeros((FOX_HEADS, 1), f32)
        total = jnp.zeros((FOX_HEADS, 1), f32)
        for n in reversed(range(nch)):
            sl = slice(n * ch, (n + 1) * ch)
            dcum = q_ref[:, sl] - k_ref[:, sl]
            rs = jnp.dot(dcum, lower, precision=HIGHEST, preferred_element_type=f32) + carry
            carry = rs[:, 0:1]
            dff = rs * sg_ref[:, sl]
            dff_ref[0:FOX_HEADS, sl] = dff
            total = total + jnp.sum(dff, axis=1, keepdims=True)
        dbf_ref[...] = jnp.broadcast_to(total, (FOX_HEADS, 128))

    return pl.pallas_call(
        body,
        name="fox_cum_bwd",
        out_shape=[_sds((128, s_len), f32), _sds((FOX_HEADS, 128), f32)],
        compiler_params=_params(),
    )(dcq, dck, sg)


FOX_T = 512
LANES = 128


def _causal_keep(t):
    return lax.broadcasted_iota(jnp.int32, (t, t), 0) <= lax.broadcasted_iota(jnp.int32, (t, t), 1)


def _tile_cols(i, t):
    return pl.ds(pl.multiple_of(i * t, t), t)


def _fox_pair(n, nt, kmax, h, pq_ref, pk_ref):
    k = h * kmax + jnp.maximum(n - nt, 0)
    return jnp.where(n < nt, n, pq_ref[k]), jnp.where(n < nt, n, pk_ref[k])


def _fox_fwd_call(qat, ka, vat, npairs, pair_q, pair_k):
    nh, s_len, _ = ka.shape
    t = FOX_T
    nt = s_len // t
    kmax = nt * (nt - 1) // 2
    assert nt >= 2 and nt % 2 == 0

    def body(np_ref, pq_ref, pk_ref, qat_ref, ka_ref, vat_ref, o_ref, lse_ref, s0, s1, p0, p1, a0, a1, m_all, acc_all):
        h = pl.program_id(0)
        extra = np_ref[h]
        total = nt + extra
        m_all[...] = jnp.full(m_all.shape, NEG_INF, f32)
        acc_all[...] = jnp.zeros(acc_all.shape, f32)
        bufs = ((s0, p0, a0), (s1, p1, a1))

        def pair(n):
            return _fox_pair(n, nt, kmax, h, pq_ref, pk_ref)

        def scores(n, b, masked):
            i, j = pair(n)
            st = jnp.dot(ka_ref[0, _tile_cols(j, t), :], qat_ref[0, :, _tile_cols(i, t)], preferred_element_type=f32)
            if masked:
                st = jnp.where(_causal_keep(t), st, NEG_INF)
            bufs[b][0][...] = st

        def softmax(n, b):
            i, _ = pair(n)
            s_ref, p_ref, a_ref = bufs[b]
            for c in range(t // LANES):
                cols = slice(c * LANES, (c + 1) * LANES)
                mcols = pl.ds(pl.multiple_of(i * t + c * LANES, LANES), LANES)
                m_old = m_all[:, mcols]
                m_new = jnp.maximum(m_old, jnp.max(s_ref[:, cols], axis=0, keepdims=True))
                m_all[:, mcols] = m_new
                a_ref[:, cols] = jnp.exp(m_old - m_new)
                p_ref[:, cols] = jnp.exp(s_ref[:, cols] - m_new).astype(bf16)

        def accum(n, b):
            i, j = pair(n)
            cols = _tile_cols(i, t)
            acc_all[:, cols] = bufs[b][2][...] * acc_all[:, cols] + jnp.dot(
                vat_ref[0, :, _tile_cols(j, t)], bufs[b][1][...], preferred_element_type=f32)

        def step(n, b, masked):
            accum(n - 2, b)
            softmax(n - 1, 1 - b)
            scores(n, b, masked)

        scores(0, 0, True)
        scores(1, 1, True)
        softmax(0, 0)

        def diag_steps(d, _):
            n = 2 + 2 * d
            step(n, 0, True)
            step(n + 1, 1, True)
            return 0

        lax.fori_loop(0, (nt - 2) // 2, diag_steps, 0)

        def off_steps(d, _):
            n = nt + 2 * d
            step(n, 0, False)
            step(n + 1, 1, False)
            return 0

        lax.fori_loop(0, extra // 2, off_steps, 0)

        @pl.when(extra % 2 == 1)
        def _():
            step(total - 1, 0, False)
            softmax(total - 1, 0)
            accum(total - 2, 1)
            accum(total - 1, 0)

        @pl.when(extra % 2 == 0)
        def _():
            softmax(total - 1, 1)
            accum(total - 2, 0)
            accum(total - 1, 1)

        l = acc_all[HEAD_DIM:HEAD_DIM + 1, :]
        o_ref[0] = acc_all[0:HEAD_DIM, :] / l
        lse_ref[0] = m_all[...] + jnp.log(l)

    smem = pl.BlockSpec(memory_space=pltpu.SMEM)
    return pl.pallas_call(
        body,
        name="fox_fwd",
        grid=(nh,),
        in_specs=[smem, smem, smem,
                  pl.BlockSpec((1, AUG, s_len), lambda h: (h, 0, 0)),
                  pl.BlockSpec((1, s_len, AUG), lambda h: (h, 0, 0)),
                  pl.BlockSpec((1, AUG, s_len), lambda h: (h, 0, 0))],
        out_specs=[pl.BlockSpec((1, HEAD_DIM, s_len), lambda h: (h, 0, 0)),
                   pl.BlockSpec((1, 1, s_len), lambda h: (h, 0, 0))],
        out_shape=[_sds((nh, HEAD_DIM, s_len), f32), _sds((nh, 1, s_len), f32)],
        scratch_shapes=[pltpu.VMEM((t, t), f32), pltpu.VMEM((t, t), f32), pltpu.VMEM((t, t), bf16),
                        pltpu.VMEM((t, t), bf16), pltpu.VMEM((1, t), f32), pltpu.VMEM((1, t), f32),
                        pltpu.VMEM((1, s_len), f32), pltpu.VMEM((AUG, s_len), f32)],
        compiler_params=_params(("arbitrary",)),
    )(npairs, pair_q, pair_k, qat, ka, vat)


SWA_TS = 512


SWA_W = SWA_GROUP * BLOCK


def _swa_bias_call(rel_bias, bucket_t):
    def body(rb_ref, bk_ref, b_ref, b0_ref):
        bk = bk_ref[...]
        row = lax.broadcasted_iota(jnp.int32, (2 * BLOCK, BLOCK), 0)
        for h in range(SWA_HEADS):
            acc = jnp.full((2 * BLOCK, BLOCK), NEG_INF, f32)
            for b in range(NUM_BUCKETS):
                acc = jnp.where(bk == b, rb_ref[b, h], acc)
            g, hh = divmod(h, SWA_GROUP)
            b_ref[g, :, hh * BLOCK:(hh + 1) * BLOCK] = acc
            b0_ref[g, :, hh * BLOCK:(hh + 1) * BLOCK] = jnp.where(row < BLOCK, NEG_INF, acc)

    return pl.pallas_call(
        body,
        name="swa_bias",
        in_specs=[pl.BlockSpec(memory_space=pltpu.SMEM), pl.BlockSpec(memory_space=pltpu.VMEM)],
        out_shape=[_sds((SWA_KV_HEADS, 2 * BLOCK, SWA_W), f32)] * 2,
        compiler_params=_params(),
    )(rel_bias, bucket_t)


def _swa_bias_bwd_call(dbias, bucket_t):
    def body(d_ref, bk_ref, o_ref):
        bk = bk_ref[...]
        row = lax.broadcasted_iota(jnp.int32, (NUM_BUCKETS, 128), 0)
        col = lax.broadcasted_iota(jnp.int32, (NUM_BUCKETS, 128), 1)
        out = jnp.zeros((NUM_BUCKETS, 128), f32)
        for h in range(SWA_HEADS):
            g, hh = divmod(h, SWA_GROUP)
            d = d_ref[g, :, hh * BLOCK:(hh + 1) * BLOCK]
            for b in range(NUM_BUCKETS):
                val = jnp.sum(jnp.sum(jnp.where(bk == b, d, 0.0), axis=1, keepdims=True), axis=0, keepdims=True)
                out = jnp.where((row == b) & (col == h), val, out)
        o_ref[...] = out

    return pl.pallas_call(
        body,
        name="swa_bias_bwd",
        out_shape=_sds((NUM_BUCKETS, 128), f32),
        compiler_params=_params(),
    )(dbias, bucket_t)


def _sink_row(sink_ref, g):
    return jnp.concatenate([jnp.full((1, BLOCK), sink_ref[g * SWA_GROUP + hh], f32) for hh in range(SWA_GROUP)], axis=1)


def _group_lanes(ref, g, cols):
    return jnp.concatenate([ref[g * SWA_GROUP + hh, :, cols] for hh in range(SWA_GROUP)], axis=1)


def _swa_fwd_call(qt, k, vta, bias_t, bias0_t, sink):
    s_len = qt.shape[2]
    ts = SWA_TS
    nb = ts // BLOCK

    def body(qt_ref, kc_ref, kp_ref, vc_ref, vp_ref, b_ref, b0_ref, sink_ref, o_ref, lse_ref):
        first = pl.program_id(0) == 0
        kall = [jnp.concatenate([kp_ref[g], kc_ref[g]], axis=0) for g in range(SWA_KV_HEADS)]
        vall = [jnp.concatenate([vp_ref[g], vc_ref[g]], axis=1) for g in range(SWA_KV_HEADS)]
        sinks = [_sink_row(sink_ref, g) for g in range(SWA_KV_HEADS)]
        items = [(g, b) for g in range(SWA_KV_HEADS) for b in range(nb)]

        def scores(g, b):
            qg = _group_lanes(qt_ref, g, slice(b * BLOCK, (b + 1) * BLOCK))
            bias_b = b_ref[g]
            if b == 0:
                bias_b = jnp.where(first, b0_ref[g], bias_b)
            return jnp.dot(kall[g][b * BLOCK:(b + 2) * BLOCK], qg, preferred_element_type=f32) + bias_b

        def finish(g, b, st):
            m = jnp.maximum(jnp.max(st, axis=0, keepdims=True), sinks[g])
            pt = jnp.exp(st - m)
            acc = jnp.dot(vall[g][:, b * BLOCK:(b + 2) * BLOCK], pt.astype(bf16), preferred_element_type=f32)
            l = acc[HEAD_DIM:HEAD_DIM + 1, :] + jnp.exp(sinks[g] - m)
            return acc[0:HEAD_DIM, :] / l, m + jnp.log(l)

        outs, lses = {}, {}
        st_next = scores(*items[0])
        for idx, (g, b) in enumerate(items):
            st = st_next
            if idx + 1 < len(items):
                st_next = scores(*items[idx + 1])
            outs[g, b], lses[g, b] = finish(g, b, st)
        for g in range(SWA_KV_HEADS):
            for hh in range(SWA_GROUP):
                lanes = slice(hh * BLOCK, (hh + 1) * BLOCK)
                o_ref[g * SWA_GROUP + hh] = jnp.concatenate([outs[g, b][:, lanes] for b in range(nb)], axis=1)
                lse_ref[g * SWA_GROUP + hh] = jnp.concatenate([lses[g, b][:, lanes] for b in range(nb)], axis=1)

    def prev_blk(n):
        return jnp.maximum(n * nb - 1, 0)

    bspec = pl.BlockSpec((SWA_KV_HEADS, 2 * BLOCK, SWA_W), lambda n: (0, 0, 0))
    return pl.pallas_call(
        body,
        name="swa_fwd",
        grid=(s_len // ts,),
        in_specs=[pl.BlockSpec((SWA_HEADS, HEAD_DIM, ts), lambda n: (0, 0, n)),
                  pl.BlockSpec((SWA_KV_HEADS, ts, HEAD_DIM), lambda n: (0, n, 0)),
                  pl.BlockSpec((SWA_KV_HEADS, BLOCK, HEAD_DIM), lambda n: (0, prev_blk(n), 0)),
                  pl.BlockSpec((SWA_KV_HEADS, AUG, ts), lambda n: (0, 0, n)),
                  pl.BlockSpec((SWA_KV_HEADS, AUG, BLOCK), lambda n: (0, 0, prev_blk(n))),
                  bspec, bspec, pl.BlockSpec(memory_space=pltpu.SMEM)],
        out_specs=[pl.BlockSpec((SWA_HEADS, HEAD_DIM, ts), lambda n: (0, 0, n)),
                   pl.BlockSpec((SWA_HEADS, 1, ts), lambda n: (0, 0, n))],
        out_shape=[_sds((SWA_HEADS, HEAD_DIM, s_len), f32), _sds((SWA_HEADS, 1, s_len), f32)],
        compiler_params=_params(("arbitrary",)),
    )(qt, k, k, vta, vta, bias_t, bias0_t, sink)


def _head_selector():
    sel = np.zeros((512, 128), np.float32)
    for h in range(8):
        sel[h * HEAD_DIM:(h + 1) * HEAD_DIM, h] = 1.0
    return sel


def _post_call(of, fz, osw, sz, x2, tgt, wo, ln_g, ln_b, sel, tm):
    s_len = x2.shape[0]

    def body(of_ref, fz_ref, os_ref, sz_ref, x_ref, t_ref, wo_ref, g_ref, b_ref, sel_ref,
             dh_ref, dof_ref, dfz_ref, dos_ref, dsz_ref, dlf_ref, dls_ref, dwo_ref, dg_ref, db_ref, loss_ref):
        n = pl.program_id(0)

        @pl.when(n == 0)
        def _():
            dwo_ref[...] = jnp.zeros_like(dwo_ref)
            dg_ref[...] = jnp.zeros_like(dg_ref)
            db_ref[...] = jnp.zeros_like(db_ref)
            loss_ref[...] = jnp.zeros_like(loss_ref)

        gam = g_ref[...]
        sel_m = sel_ref[...]

        def forward(r):
            o_f = of_ref[:, r].T
            o_s = os_ref[:, r].T
            fz = fz_ref[r, :]
            sz = sz_ref[r, :]
            sg_f = jax.nn.sigmoid(fz)
            sg_s = jax.nn.sigmoid(sz)
            silu_f = fz * sg_f
            silu_s = sz * sg_s
            mixed = jnp.concatenate([o_f * silu_f, o_s * silu_s], axis=1).astype(bf16)
            y = jnp.dot(mixed, wo_ref[...], preferred_element_type=f32)
            return o_f, o_s, fz, sz, sg_f, sg_s, silu_f, silu_s, mixed, y

        def norm_and_back(r, fwd):
            mixed, y = fwd[8], fwd[9]
            h = ALPHA * x_ref[r, :] + y
            mu = jnp.mean(h, axis=1, keepdims=True)
            hc = h - mu
            var = jnp.mean(hc * hc, axis=1, keepdims=True)
            rstd = lax.rsqrt(var + LN_EPS)
            xhat = hc * rstd
            out = xhat * gam + b_ref[...]
            err = out - t_ref[r, :]
            tok_loss = jnp.mean(err * err, axis=1, keepdims=True)
            loss_ref[...] += 0.5 * jnp.sum(tok_loss, axis=0, keepdims=True)
            dout = err * (1.0 / D_MODEL)
            dg_ref[...] += jnp.sum(dout * xhat, axis=0, keepdims=True)
            db_ref[...] += jnp.sum(dout, axis=0, keepdims=True)
            dxh = dout * gam
            m1 = jnp.mean(dxh, axis=1, keepdims=True)
            m2 = jnp.mean(dxh * xhat, axis=1, keepdims=True)
            dh = rstd * (dxh - m1 - xhat * m2)
            dh_ref[r, :] = dh
            dyb = dh.astype(bf16)
            dmix = lax.dot_general(dyb, wo_ref[...], NT, preferred_element_type=f32)
            dwo_ref[...] += lax.dot_general(mixed, dyb, TN, preferred_element_type=f32)
            return dmix

        def head_sums(prod):
            hi = prod.astype(bf16)
            lo = (prod - hi.astype(f32)).astype(bf16)
            return (jnp.dot(hi, sel_m, preferred_element_type=f32) + jnp.dot(lo, sel_m, preferred_element_type=f32))

        def gates_back(r, fwd, dmix):
            o_f, o_s, fz, sz, sg_f, sg_s, silu_f, silu_s = fwd[:8]
            dm_f = dmix[:, :512]
            dm_s = dmix[:, 512:]
            do_f = dm_f * silu_f
            do_s = dm_s * silu_s
            dfz_ref[r, :] = (dm_f * o_f * (sg_f * (1.0 + fz * (1.0 - sg_f)))).astype(bf16)
            dsz_ref[r, :] = (dm_s * o_s * (sg_s * (1.0 + sz * (1.0 - sg_s)))).astype(bf16)
            dof_ref[:, r] = do_f.T.astype(bf16)
            dos_ref[:, r] = do_s.T.astype(bf16)
            dlf_ref[:, r] = head_sums(do_f * o_f).T[:FOX_HEADS, :]
            dls_ref[:, r] = head_sums(do_s * o_s).T[:SWA_HEADS, :]

        halves = [slice(k * (tm // 2), (k + 1) * (tm // 2)) for k in range(2)]
        fwds = [forward(r) for r in halves]
        dmixes = [norm_and_back(r, f) for r, f in zip(halves, fwds)]
        for r, f, d in zip(halves, fwds, dmixes):
            gates_back(r, f, d)

    feat = pl.BlockSpec((512, tm), lambda n: (0, n))
    rows8 = pl.BlockSpec((8, tm), lambda n: (0, n))
    half = pl.BlockSpec((tm, 512), lambda n: (n, 0))
    fullw = pl.BlockSpec((tm, D_MODEL), lambda n: (n, 0))
    vec = pl.BlockSpec((1, D_MODEL), lambda n: (0, 0))
    return pl.pallas_call(
        body,
        name="post_fwd_bwd",
        grid=(s_len // tm,),
        in_specs=[feat, half, feat, half, fullw, fullw,
                  pl.BlockSpec((D_MODEL, D_MODEL), lambda n: (0, 0)), vec, vec,
                  pl.BlockSpec((512, 128), lambda n: (0, 0))],
        out_specs=[fullw, feat, half, feat, half, rows8, rows8,
                   pl.BlockSpec((D_MODEL, D_MODEL), lambda n: (0, 0)), vec, vec,
                   pl.BlockSpec((1, 1), lambda n: (0, 0))],
        out_shape=[_sds((s_len, D_MODEL), f32), _sds((512, s_len), bf16), _sds((s_len, 512), bf16),
                   _sds((512, s_len), bf16), _sds((s_len, 512), bf16),
                   _sds((FOX_HEADS, s_len), f32), _sds((SWA_HEADS, s_len), f32),
                   _sds((D_MODEL, D_MODEL), f32), _sds((1, D_MODEL), f32), _sds((1, D_MODEL), f32),
                   _sds((1, 1), f32)],
        compiler_params=_params(("arbitrary",), VMEM_LIMIT_BIG),
    )(of, fz, osw, sz, x2, tgt, wo, ln_g, ln_b, sel)


def _fox_bwd_call(ka, kat, v, qat, dot, lse_row, dl_row, npairs, pair_q, pair_k):
    nh, s_len, _ = ka.shape
    t = FOX_T
    nt = s_len // t
    kmax = nt * (nt - 1) // 2
    assert nt >= 2 and nt % 2 == 0
    ck_slot = HEAD_DIM + 3
    cq_slot = HEAD_DIM

    def body(np_ref, pq_ref, pk_ref, ka_ref, kat_ref, v_ref, qat_ref, dot_ref, lse_ref, dl_ref,
             dq_ref, dk_ref, dv_ref, dcq_ref, dck_ref, dqt_all, dkat_all, dvt_all, p0, p1, ds0, ds1):
        h = pl.program_id(0)
        extra = np_ref[h]
        total = nt + extra
        dqt_all[...] = jnp.zeros(dqt_all.shape, f32)
        dkat_all[...] = jnp.zeros(dkat_all.shape, f32)
        dvt_all[...] = jnp.zeros(dvt_all.shape, f32)
        pbuf, dsbuf = (p0, p1), (ds0, ds1)

        def pair(n):
            return _fox_pair(n, nt, kmax, h, pq_ref, pk_ref)

        def probs(n, b, masked):
            i, j = pair(n)
            qc, kr = _tile_cols(i, t), _tile_cols(j, t)
            st = jnp.dot(ka_ref[0, kr, :], qat_ref[0, :, qc], preferred_element_type=f32)
            dpt = jnp.dot(v_ref[0, kr, :], dot_ref[0, :, qc], preferred_element_type=f32)
            if masked:
                st = jnp.where(_causal_keep(t), st, NEG_INF)
            pt = jnp.exp(st - lse_ref[0, :, qc])
            pbuf[b][...] = pt.astype(bf16)
            dsbuf[b][...] = (pt * (dpt - dl_ref[0, :, qc])).astype(bf16)

        def grads(n, b):
            i, j = pair(n)
            qc, kc = _tile_cols(i, t), _tile_cols(j, t)
            dvt_all[:, kc] += lax.dot_general(dot_ref[0, :, qc], pbuf[b][...], NT, preferred_element_type=f32)
            dkat_all[:, kc] += lax.dot_general(qat_ref[0, :, qc], dsbuf[b][...], NT, preferred_element_type=f32)
            dqt_all[:, qc] += jnp.dot(kat_ref[0, :, kc], dsbuf[b][...], preferred_element_type=f32)

        def step(n, b, masked):
            i, j = pair(n)
            qc, kr = _tile_cols(i, t), _tile_cols(j, t)
            i1, j1 = pair(n - 1)
            qc1, kc1 = _tile_cols(i1, t), _tile_cols(j1, t)
            c = 1 - b
            st = jnp.dot(ka_ref[0, kr, :], qat_ref[0, :, qc], preferred_element_type=f32)
            dvt_all[:, kc1] += lax.dot_general(dot_ref[0, :, qc1], pbuf[c][...], NT, preferred_element_type=f32)
            if masked:
                st = jnp.where(_causal_keep(t), st, NEG_INF)
            pt = jnp.exp(st - lse_ref[0, :, qc])
            pbuf[b][...] = pt.astype(bf16)
            dpt = jnp.dot(v_ref[0, kr, :], dot_ref[0, :, qc], preferred_element_type=f32)
            dkat_all[:, kc1] += lax.dot_general(qat_ref[0, :, qc1], dsbuf[c][...], NT, preferred_element_type=f32)
            dqt_all[:, qc1] += jnp.dot(kat_ref[0, :, kc1], dsbuf[c][...], preferred_element_type=f32)
            dsbuf[b][...] = (pt * (dpt - dl_ref[0, :, qc])).astype(bf16)

        probs(0, 0, True)
        step(1, 1, True)

        def four_steps(n, masked):
            step(n, 0, masked)
            step(n + 1, 1, masked)
            step(n + 2, 0, masked)
            step(n + 3, 1, masked)

        def diag_quads(d, _):
            four_steps(2 + 4 * d, True)
            return 0

        lax.fori_loop(0, (nt - 2) // 4, diag_quads, 0)
        if (nt - 2) % 4:
            step(nt - 2, 0, True)
            step(nt - 1, 1, True)

        def off_quads(d, _):
            four_steps(nt + 4 * d, False)
            return 0

        quads = extra // 4
        lax.fori_loop(0, quads, off_quads, 0)

        def off_steps(d, _):
            n = nt + 4 * quads + 2 * d
            step(n, 0, False)
            step(n + 1, 1, False)
            return 0

        lax.fori_loop(0, (extra % 4) // 2, off_steps, 0)

        @pl.when(extra % 2 == 1)
        def _():
            step(total - 1, 0, False)
            grads(total - 1, 0)

        @pl.when(extra % 2 == 0)
        def _():
            grads(total - 1, 1)

        dq_ref[0] = (dqt_all[0:HEAD_DIM, :] * SCALE).astype(bf16)
        dk_ref[0] = dkat_all[0:HEAD_DIM, :].astype(bf16)
        dv_ref[0] = dvt_all[...].astype(bf16)
        dcq_ref[0] = dqt_all[cq_slot:cq_slot + 1, :]
        dck_ref[0] = dkat_all[ck_slot:ck_slot + 1, :]

    smem = pl.BlockSpec(memory_space=pltpu.SMEM)
    rows = pl.BlockSpec((1, s_len, AUG), lambda h: (h, 0, 0))
    feat = pl.BlockSpec((1, AUG, s_len), lambda h: (h, 0, 0))
    feat64 = pl.BlockSpec((1, HEAD_DIM, s_len), lambda h: (h, 0, 0))
    rowv = pl.BlockSpec((1, 1, s_len), lambda h: (h, 0, 0))
    return pl.pallas_call(
        body,
        name="fox_bwd",
        grid=(nh,),
        in_specs=[smem, smem, smem, rows, feat, pl.BlockSpec((1, s_len, HEAD_DIM), lambda h: (h, 0, 0)), feat, feat64,
                  rowv, rowv],
        out_specs=[feat64, feat64, feat64, rowv, rowv],
        out_shape=[_sds((nh, HEAD_DIM, s_len), bf16)] * 3 + [_sds((nh, 1, s_len), f32)] * 2,
        scratch_shapes=[pltpu.VMEM((AUG, s_len), f32), pltpu.VMEM((AUG, s_len), f32), pltpu.VMEM((HEAD_DIM, s_len), f32)]
                       + [pltpu.VMEM((t, t), bf16)] * 4,
        compiler_params=_params(("arbitrary",)),
    )(npairs, pair_q, pair_k, ka, kat, v, qat, dot, lse_row, dl_row)


def _swa_bwd_call(qt, k, kt, v, dot, lse, dl, bias_t, bias0_t, sink):
    s_len = qt.shape[2]
    ts = SWA_TS
    nb = ts // BLOCK
    nsteps = s_len // ts

    def body(qt_ref, kc_ref, kp_ref, ktc_ref, ktp_ref, vc_ref, vp_ref, dot_ref, lse_ref, dl_ref, b_ref, b0_ref,
             sink_ref, dq_ref, dk_ref, dv_ref, dbias_ref, dsink_ref, dk_s, dv_s, tail_k, tail_v, sk_s):
        n = pl.program_id(0)

        @pl.when(n == 0)
        def _():
            dbias_ref[...] = jnp.zeros_like(dbias_ref)
            sk_s[...] = jnp.zeros_like(sk_s)

        @pl.when(n < nsteps)
        def _():
            first = n == 0
            dk_s[...] = jnp.zeros_like(dk_s)
            dv_s[...] = jnp.zeros_like(dv_s)
            groups = range(SWA_KV_HEADS)
            kall = [jnp.concatenate([kp_ref[g], kc_ref[g]], axis=0) for g in groups]
            vall = [jnp.concatenate([vp_ref[g], vc_ref[g]], axis=0) for g in groups]
            ktall = [jnp.concatenate([ktp_ref[g], ktc_ref[g]], axis=1) for g in groups]
            sinks = [_sink_row(sink_ref, g) for g in groups]
            items = [(g, b) for g in groups for b in range(nb)]

            def products(g, b):
                cols = slice(b * BLOCK, (b + 1) * BLOCK)
                win = slice(b * BLOCK, (b + 2) * BLOCK)
                qg = _group_lanes(qt_ref, g, cols)
                dog = _group_lanes(dot_ref, g, cols)
                bias_b = b_ref[g]
                if b == 0:
                    bias_b = jnp.where(first, b0_ref[g], bias_b)
                st = jnp.dot(kall[g][win], qg, preferred_element_type=f32) + bias_b
                dpt = jnp.dot(vall[g][win], dog, preferred_element_type=f32)
                return qg, dog, st, dpt

            def finish(g, b, qg, dog, st, dpt):
                cols = slice(b * BLOCK, (b + 1) * BLOCK)
                win = slice(b * BLOCK, (b + 2) * BLOCK)
                lse_r = _group_lanes(lse_ref, g, cols)
                dl_r = _group_lanes(dl_ref, g, cols)
                pt = jnp.exp(st - lse_r)
                dst = pt * (dpt - dl_r)
                dsb = dst.astype(bf16)
                dk_s[g, :, win] += lax.dot_general(qg, dsb, NT, preferred_element_type=f32)
                dv_s[g, :, win] += lax.dot_general(dog, pt.astype(bf16), NT, preferred_element_type=f32)
                dqg = jnp.dot(ktall[g][:, win], dsb, preferred_element_type=f32) * SCALE
                return dqg, dst, -jnp.exp(sinks[g] - lse_r) * dl_r

            dqs, dsts, sks = {}, {}, {}
            nxt = products(*items[0])
            for idx, (g, b) in enumerate(items):
                cur = nxt
                if idx + 1 < len(items):
                    nxt = products(*items[idx + 1])
                dqs[g, b], dsts[g, b], sks[g, b] = finish(g, b, *cur)
            for g in groups:
                dbias_ref[g] += functools.reduce(lambda a, c: a + c, [dsts[g, b] for b in range(nb)])
                sk_s[g] += functools.reduce(lambda a, c: a + c, [sks[g, b] for b in range(nb)])
                for hh in range(SWA_GROUP):
                    lanes = slice(hh * BLOCK, (hh + 1) * BLOCK)
                    dq_ref[g * SWA_GROUP + hh] = jnp.concatenate(
                        [dqs[g, b][:, lanes] for b in range(nb)], axis=1).astype(bf16)

        @pl.when(n > 0)
        def _():
            last = slice(ts - BLOCK, ts)
            for g in range(SWA_KV_HEADS):
                add_k = jnp.where(n < nsteps, dk_s[g, :, 0:BLOCK], 0.0)
                add_v = jnp.where(n < nsteps, dv_s[g, :, 0:BLOCK], 0.0)
                dk_ref[g, :, 0:ts - BLOCK] = tail_k[g, :, 0:ts - BLOCK].astype(bf16)
                dv_ref[g, :, 0:ts - BLOCK] = tail_v[g, :, 0:ts - BLOCK].astype(bf16)
                dk_ref[g, :, last] = (tail_k[g, :, last] + add_k).astype(bf16)
                dv_ref[g, :, last] = (tail_v[g, :, last] + add_v).astype(bf16)

        @pl.when(n < nsteps)
        def _():
            tail_k[...] = dk_s[:, :, BLOCK:]
            tail_v[...] = dv_s[:, :, BLOCK:]

        @pl.when(n == nsteps)
        def _():
            row = lax.broadcasted_iota(jnp.int32, (SWA_HEADS, 128), 0)
            out = jnp.zeros((SWA_HEADS, 128), f32)
            for h in range(SWA_HEADS):
                g, hh = divmod(h, SWA_GROUP)
                val = jnp.sum(sk_s[g, :, hh * BLOCK:(hh + 1) * BLOCK], axis=1, keepdims=True)
                out = jnp.where(row == h, val, out)
            dsink_ref[...] = out

    last_step = nsteps - 1

    def cl(n):
        return jnp.minimum(n, last_step)

    def prev_blk(n):
        return jnp.maximum(cl(n) * nb - 1, 0)

    feat8 = pl.BlockSpec((SWA_HEADS, HEAD_DIM, ts), lambda n: (0, 0, cl(n)))
    rows8 = pl.BlockSpec((SWA_HEADS, 1, ts), lambda n: (0, 0, cl(n)))
    cur = pl.BlockSpec((SWA_KV_HEADS, ts, HEAD_DIM), lambda n: (0, cl(n), 0))
    prev = pl.BlockSpec((SWA_KV_HEADS, BLOCK, HEAD_DIM), lambda n: (0, prev_blk(n), 0))
    curt = pl.BlockSpec((SWA_KV_HEADS, HEAD_DIM, ts), lambda n: (0, 0, cl(n)))
    prevt = pl.BlockSpec((SWA_KV_HEADS, HEAD_DIM, BLOCK), lambda n: (0, 0, prev_blk(n)))
    bspec = pl.BlockSpec((SWA_KV_HEADS, 2 * BLOCK, SWA_W), lambda n: (0, 0, 0))
    kvout = pl.BlockSpec((SWA_KV_HEADS, HEAD_DIM, ts), lambda n: (0, 0, jnp.maximum(n - 1, 0)))
    return pl.pallas_call(
        body,
        name="swa_bwd",
        grid=(nsteps + 1,),
        in_specs=[feat8, cur, prev, curt, prevt, cur, prev, feat8, rows8, rows8, bspec, bspec,
                  pl.BlockSpec(memory_space=pltpu.SMEM)],
        out_specs=[feat8, kvout, kvout, bspec, pl.BlockSpec((SWA_HEADS, 128), lambda n: (0, 0))],
        out_shape=[_sds((SWA_HEADS, HEAD_DIM, s_len), bf16), _sds((SWA_KV_HEADS, HEAD_DIM, s_len), bf16),
                   _sds((SWA_KV_HEADS, HEAD_DIM, s_len), bf16),
                   _sds((SWA_KV_HEADS, 2 * BLOCK, SWA_W), f32), _sds((SWA_HEADS, 128), f32)],
        scratch_shapes=[pltpu.VMEM((SWA_KV_HEADS, HEAD_DIM, ts + BLOCK), f32),
                        pltpu.VMEM((SWA_KV_HEADS, HEAD_DIM, ts + BLOCK), f32),
                        pltpu.VMEM((SWA_KV_HEADS, HEAD_DIM, ts), f32),
                        pltpu.VMEM((SWA_KV_HEADS, HEAD_DIM, ts), f32),
                        pltpu.VMEM((SWA_KV_HEADS, 1, SWA_W), f32)],
        compiler_params=_params(("arbitrary",)),
    )(qt, k, k, kt, kt, v, v, dot, lse, dl, bias_t, bias0_t, sink)


def _dproj_specs(tm):
    half = pl.BlockSpec((tm, 512), lambda i: (i, 0))
    feat = pl.BlockSpec((512, tm), lambda i: (0, i))
    feat_kv = pl.BlockSpec((128, tm), lambda i: (0, i))
    return [feat, feat, feat, half, feat, feat_kv, feat_kv, half, feat_kv]


def _dx_exchange_call(dh, pieces, w_t, bs, tm):
    s_len = dh.shape[0]
    n = len(bs)
    last = s_len // tm - 1

    def body(*refs):
        dh_ref, dqf_ref, dkf_ref, dvf_ref, dfz_ref, dqs_ref, dks_ref, dvs_ref, dsz_ref, dfft_ref, w_ref = refs[:11]
        b_refs = refs[11:11 + n]
        dx_ref = refs[11 + n]
        r_refs = refs[12 + n:12 + 2 * n]
        sems = refs[12 + 2 * n:]
        i = pl.program_id(0)

        @pl.when(i == 0)
        def _():
            _exchange_start(b_refs, r_refs, sems)

        def tr(ref):
            return ref[...].astype(f32).T.astype(bf16)

        dp = jnp.concatenate([tr(dqf_ref), tr(dkf_ref), tr(dvf_ref), dfz_ref[...], tr(dqs_ref), tr(dks_ref),
                              tr(dvs_ref), dsz_ref[...], tr(dfft_ref)], axis=1)
        dx_ref[...] = ALPHA * dh_ref[...] + jnp.dot(dp, w_ref[...], preferred_element_type=f32)

        @pl.when(i == last)
        def _():
            _exchange_wait(b_refs, r_refs, sems)

    fullw = pl.BlockSpec((tm, D_MODEL), lambda i: (i, 0))
    any_spec = pl.BlockSpec(memory_space=pl.ANY)
    out = pl.pallas_call(
        body,
        name="dx_bwd_exchange",
        grid=(s_len // tm,),
        in_specs=[fullw] + _dproj_specs(tm) + [pl.BlockSpec((A_W, D_MODEL), lambda i: (0, 0))] + [any_spec] * n,
        out_specs=[fullw] + [any_spec] * n,
        out_shape=[_sds((s_len, D_MODEL), f32)] + [_sds(b.shape, b.dtype) for b in bs],
        scratch_shapes=[pltpu.SemaphoreType.DMA((7 * n,)), pltpu.SemaphoreType.DMA((7 * n,)),
                        pltpu.SemaphoreType.DMA((n,))],
        compiler_params=_params(("arbitrary",)),
    )(dh, *pieces, w_t, *bs)
    return out[0], out[1:]


DW_STAGE_ROWS = 384


def _dw_exchange_call(x2, pieces, bs, tm):
    s_len = x2.shape[0]
    nt = s_len // tm
    n = len(bs)

    def body(*refs):
        x_ref, dqf_ref, dkf_ref, dvf_ref, dfz_ref, dqs_ref, dks_ref, dvs_ref, dsz_ref, dfft_ref = refs[:10]
        b_refs = refs[10:10 + n]
        dw_ref = refs[10 + n]
        r_refs = refs[11 + n:11 + 2 * n]
        acc_ref, stage_ref, sem = refs[11 + 2 * n:14 + 2 * n]
        sems = refs[14 + 2 * n:]
        i = pl.program_id(0)

        @pl.when(i == 0)
        def _():
            _exchange_start(b_refs, r_refs, sems)
            acc_ref[...] = jnp.zeros_like(acc_ref)

        xb = x_ref[...].astype(bf16)

        def add_feat(off, lhs):
            acc_ref[off:off + lhs.shape[0], :] += jnp.dot(lhs, xb, preferred_element_type=f32)

        def add_rows(off, piece):
            acc_ref[off:off + piece.shape[1], :] += lax.dot_general(piece, xb, TN, preferred_element_type=f32)

        add_feat(A_FQ, dqf_ref[...])
        add_feat(A_FK, dkf_ref[...])
        add_feat(A_FV, dvf_ref[...])
        add_rows(A_FZ, dfz_ref[...])
        add_feat(A_SQ, dqs_ref[...])
        add_feat(A_SK, dks_ref[...])
        add_feat(A_SV, dvs_ref[...])
        add_rows(A_SZ, dsz_ref[...])
        add_feat(A_FF, dfft_ref[...].astype(bf16))

        @pl.when(i == nt - 1)
        def _():
            for r in range(A_W // DW_STAGE_ROWS):
                rows = slice(r * DW_STAGE_ROWS, (r + 1) * DW_STAGE_ROWS)
                stage_ref[...] = acc_ref[rows, :].astype(bf16)
                cp = pltpu.make_async_copy(stage_ref, dw_ref.at[rows, :], sem)
                cp.start()
                cp.wait()
            _exchange_wait(b_refs, r_refs, sems)

    any_spec = pl.BlockSpec(memory_space=pl.ANY)
    out = pl.pallas_call(
        body,
        name="dw_bwd_exchange",
        grid=(nt,),
        in_specs=[pl.BlockSpec((tm, D_MODEL), lambda i: (i, 0))] + _dproj_specs(tm) + [any_spec] * n,
        out_specs=[any_spec] * (1 + n),
        out_shape=[_sds((A_W, D_MODEL), bf16)] + [_sds(b.shape, b.dtype) for b in bs],
        scratch_shapes=[pltpu.VMEM((A_W, D_MODEL), f32), pltpu.VMEM((DW_STAGE_ROWS, D_MODEL), bf16),
                        pltpu.SemaphoreType.DMA, pltpu.SemaphoreType.DMA((7 * n,)), pltpu.SemaphoreType.DMA((7 * n,)),
                        pltpu.SemaphoreType.DMA((n,))],
        compiler_params=_params(("arbitrary",), VMEM_LIMIT_BIG),
    )(x2, *pieces, *bs)
    return out[0], out[1:]


def _adam_call(recv, w, m, v, tc, name):
    rows, cols = w.shape

    def body(r_ref, w_ref, m_ref, v_ref, g_ref, d_ref, mo_ref, vo_ref):
        g = r_ref[0].astype(f32)
        for p in range(1, N_DEV):
            g = g + r_ref[p].astype(f32)
        mn = ADAM_B1 * m_ref[...] + (1.0 - ADAM_B1) * g
        vn = ADAM_B2 * v_ref[...] + (1.0 - ADAM_B2) * (g * g)
        m_hat = mn / (1.0 - ADAM_B1 ** ADAM_STEP)
        v_hat = vn / (1.0 - ADAM_B2 ** ADAM_STEP)
        g_ref[...] = g
        d_ref[...] = -ADAM_LR * (m_hat / (jnp.sqrt(v_hat) + ADAM_EPS) + ADAM_WD * w_ref[...])
        mo_ref[...] = mn
        vo_ref[...] = vn

    blk = pl.BlockSpec((rows, tc), lambda i: (0, i))
    return pl.pallas_call(
        body,
        name=name,
        grid=(cols // tc,),
        in_specs=[pl.BlockSpec((N_DEV, rows, tc), lambda i: (0, 0, i)), blk, blk, blk],
        out_specs=[blk] * 4,
        out_shape=[_sds((rows, cols), f32)] * 4,
        compiler_params=_params(("arbitrary",)),
    )(recv, w, m, v)


def _rows_to_shards(parts, shard):
    blocks = []
    for d in range(N_DEV):
        lo, hi, start, pieces = d * shard, (d + 1) * shard, 0, []
        for part in parts:
            a, b = max(lo, start), min(hi, start + part.shape[0])
            if a < b:
                pieces.append(part[a - start:b - start])
            start += part.shape[0]
        blocks.append(jnp.concatenate(pieces, axis=0))
    return jnp.stack(blocks)


def _pad_cols(a, width=128):
    return jnp.pad(a, ((0, 0), (0, width - a.shape[1])))


def _pack_small(ln_g, ln_b, rel, b_f, sink):
    return jnp.concatenate([
        ln_g.reshape(8, 128), ln_b.reshape(8, 128), _pad_cols(rel),
        jnp.pad(_pad_cols(b_f), ((0, 7), (0, 0))), jnp.pad(_pad_cols(sink), ((0, 7), (0, 0)))], axis=0)


def _unpack_small(p):
    return (p[0:8].reshape(1, D_MODEL), p[8:16].reshape(1, D_MODEL), p[16:48, 0:8], p[48:49, 0:8], p[56:57, 0:8])


def kernel(x, w_in, b_f, rel_bias, sink, w_o, ln_g, ln_b, loss_target, m_w_in, m_b_f, m_rel_bias, m_sink, m_w_o, m_ln_g, m_ln_b, v_w_in, v_b_f, v_rel_bias, v_sink, v_w_o, v_ln_g, v_ln_b):
    x2 = x[0]
    tgt = loss_target[0]
    s_len = x2.shape[0]
    shard = w_in.shape[2]

    w_in_t = jnp.transpose(w_in[0])
    g_in, g_o = _gather_call([w_in_t.astype(bf16), w_o[0].astype(bf16)])
    wt_full = g_in.reshape(N_DEV * shard, D_MODEL)
    w_t = jnp.concatenate([wt_full[:O_FF0], wt_full[O_FF1:], wt_full[O_FF0:O_FF1],
                           jnp.zeros((A_W - D_IN, D_MODEL), bf16)], axis=0)
    wo_full = g_o.reshape(D_MODEL, D_MODEL)

    qft, kft, vf, fz, qst, ks, vs, sz, fft, vat, kst, vsta = _proj_call(x2, w_t, 512)
    cum, sgm = _cum_call(fft, b_f.reshape(FOX_HEADS, 1))
    qat, ka, kat, tile_stats = _augment_call(qft, kft, cum.reshape(FOX_HEADS, 1, s_len), 2048)
    npairs, pair_q, pair_k = _fox_prune_tables(tile_stats)
    o_ft, lse_f = _fox_fwd_call(qat, ka, vat, npairs, pair_q, pair_k)
    bucket_t = jnp.asarray(_t5_bucket_table().T)
    bias_t, bias0_t = _swa_bias_call(rel_bias, bucket_t)
    sink_v = sink.reshape(SWA_HEADS)
    o_st, lse_s = _swa_fwd_call(qst, ks, vsta, bias_t, bias0_t, sink_v)

    (dh, do_f, dfz, do_s, dsz, dl_f, dl_s, dwo, dg, db, loss_part) = _post_call(
        o_ft.reshape(FOX_HEADS * HEAD_DIM, s_len), fz, o_st.reshape(SWA_HEADS * HEAD_DIM, s_len), sz, x2, tgt,
        wo_full, ln_g, ln_b, jnp.asarray(_head_selector()).astype(bf16), 512)

    dqf, dkf, dvf, dcq, dck = _fox_bwd_call(ka, kat, vf, qat, do_f.reshape(FOX_HEADS, HEAD_DIM, s_len), lse_f,
                                            dl_f.reshape(FOX_HEADS, 1, s_len), npairs, pair_q, pair_k)
    dqf, dkf, dvf = (a.reshape(FOX_HEADS * HEAD_DIM, s_len) for a in (dqf, dkf, dvf))
    dfft, dbf = _cum_bwd_call(dcq.reshape(FOX_HEADS, s_len), dck.reshape(FOX_HEADS, s_len), sgm)
    dqs, dks, dvs, dbias, dsink = _swa_bwd_call(
        qst, ks, kst, vs, do_s.reshape(SWA_HEADS, HEAD_DIM, s_len), lse_s, dl_s.reshape(SWA_HEADS, 1, s_len),
        bias_t, bias0_t, sink_v)
    dqs = dqs.reshape(SWA_HEADS * HEAD_DIM, s_len)
    dks, dvs = (a.reshape(SWA_KV_HEADS * HEAD_DIM, s_len) for a in (dks, dvs))
    drel = _swa_bias_bwd_call(dbias, bucket_t)

    dwo_blocks = dwo.reshape(N_DEV, D_MODEL // N_DEV, D_MODEL).astype(bf16)
    small = _pack_small(dg, db, drel[:, 0:8], dbf[:, 0].reshape(1, 8), dsink[:, 0].reshape(1, 8))
    loss_slot = np.zeros((64, 128), bool)
    loss_slot[49, 0] = True
    small = jnp.where(jnp.asarray(loss_slot), loss_part[0, 0], small)
    small_blocks = jnp.broadcast_to(small[None], (N_DEV,) + small.shape)
    pieces = (dqf, dkf, dvf, dfz, dqs, dks, dvs, dsz, dfft)
    dw_t, (r_o, r_small) = _dw_exchange_call(x2, pieces, [dwo_blocks, small_blocks], 1024)
    dw_blocks = _rows_to_shards([dw_t[:O_FF0], dw_t[A_FF:A_FF + (O_FF1 - O_FF0)], dw_t[O_FF0:A_FF]], shard)
    dx, (r_in,) = _dx_exchange_call(dh, pieces, w_t, [dw_blocks], 256)

    win_t = [jnp.transpose(a) for a in _adam_call(
        r_in, w_in_t, jnp.transpose(m_w_in[0]), jnp.transpose(v_w_in[0]), 256, "adam_w_in")]
    g_win, d_win, nm_win, nv_win = win_t
    g_wo, d_wo, nm_wo, nv_wo = _adam_call(r_o, w_o[0], m_w_o[0], v_w_o[0], 256, "adam_w_o")
    p_w = _pack_small(ln_g, ln_b, rel_bias, b_f, sink)
    p_m = _pack_small(m_ln_g, m_ln_b, m_rel_bias, m_b_f, m_sink)
    p_v = _pack_small(v_ln_g, v_ln_b, v_rel_bias, v_b_f, v_sink)
    g_p, d_p, nm_p, nv_p = _adam_call(r_small, p_w, p_m, p_v, 128, "adam_small")

    loss = g_p[49, 0]
    g_lng, g_lnb, g_rel, g_bf, g_sink = _unpack_small(g_p)
    d_lng, d_lnb, d_rel, d_bf, d_sink = _unpack_small(d_p)
    m_lng, m_lnb, m_rel, m_bf, m_sk = _unpack_small(nm_p)
    v_lng, v_lnb, v_rel, v_bf, v_sk = _unpack_small(nv_p)
    return (loss, dx[None], g_win[None], g_bf, g_rel, g_sink, g_wo[None], g_lng, g_lnb,
            d_win[None], d_bf, d_rel, d_sink, d_wo[None], d_lng, d_lnb,
            nm_win[None], m_bf, m_rel, m_sk, nm_wo[None], m_lng, m_lnb,
            nv_win[None], v_bf, v_rel, v_sk, nv_wo[None], v_lng, v_lnb)
```

```python
import functools
import math

import numpy as np
import jax
import jax.numpy as jnp
from jax import lax
from jax.experimental import pallas as pl
from jax.experimental.pallas import tpu as pltpu

f32 = jnp.float32
bf16 = jnp.bfloat16

D_MODEL = 1024
HEAD_DIM = 64
FOX_HEADS = 8
SWA_HEADS = 8
SWA_KV_HEADS = 2
SWA_GROUP = 4
BLOCK = 128
NUM_BUCKETS = 32
MAX_DISTANCE = 128
LN_EPS = 1e-5
NEG_INF = -1e30
ALPHA = 2.0 ** 0.25
SCALE = 1.0 / math.sqrt(HEAD_DIM)
D_IN = 3336

ADAM_LR = 0.001
ADAM_B1 = 0.9
ADAM_B2 = 0.999
ADAM_EPS = 1e-08
ADAM_WD = 0.01
ADAM_STEP = 10

N_DEV = 8
A_FQ, A_FK, A_FV, A_FZ, A_SQ, A_SK, A_SV, A_SZ, A_FF, A_W = 0, 512, 1024, 1536, 2048, 2560, 2688, 2816, 3328, 3456
O_FF0, O_FF1 = 1536, 1544

VMEM_LIMIT = 48 * 1024 * 1024
HIGHEST = lax.Precision.HIGHEST
NT = (((1,), (1,)), ((), ()))
TN = (((0,), (0,)), ((), ()))
MESH = pl.DeviceIdType.MESH
RELS = [(0, 0, 1), (0, 1, 0), (0, 1, 1), (1, 0, 0), (1, 0, 1), (1, 1, 0), (1, 1, 1)]


VMEM_LIMIT_BIG = 60 * 1024 * 1024


def _params(sem=None, vmem=VMEM_LIMIT):
    return pltpu.CompilerParams(dimension_semantics=sem, vmem_limit_bytes=vmem)


def _sds(shape, dtype):
    return jax.ShapeDtypeStruct(shape, dtype)


def _t5_bucket_table():
    qi = np.arange(BLOCK)[:, None]
    kj = np.arange(2 * BLOCK)[None, :]
    rel = qi + BLOCK - kj
    band = (rel >= 0) & (rel < BLOCK)
    relc = np.maximum(rel, 0)
    max_exact = NUM_BUCKETS // 2
    relf = np.maximum(relc, 1).astype(np.float32)
    large = max_exact + (np.log(relf / np.float32(max_exact)) / np.float32(math.log(MAX_DISTANCE / max_exact))
                         * np.float32(NUM_BUCKETS - max_exact)).astype(np.int32)
    large = np.minimum(large, NUM_BUCKETS - 1)
    bucket = np.where(relc < max_exact, relc, large).astype(np.int32)
    bucket = np.where(band, bucket, -1).astype(np.int32)
    return bucket


def _mesh_pos():
    return lax.axis_index("x"), lax.axis_index("y"), lax.axis_index("c")


def _dev_index(p):
    return 4 * p[0] + 2 * p[1] + p[2]


def _gather_call(xs):
    n = len(xs)

    def body(*refs):
        x_refs, o_refs = refs[:n], refs[n:2 * n]
        send_sems, recv_sems, local_sems = refs[2 * n:]
        x, y, c = _mesh_pos()
        me, sib = (x, y, c), (x, y, 1 - c)
        chips = [(1 - x, y), (x, 1 - y), (1 - x, 1 - y)]

        def copy(a, k, block, to, src=None):
            slot = o_refs[a].at[_dev_index(block)]
            return pltpu.make_async_remote_copy(
                src_ref=slot if src is None else src, dst_ref=slot,
                send_sem=send_sems.at[a * 7 + k], recv_sem=recv_sems.at[a * 7 + k],
                device_id=to, device_id_type=MESH)

        mine = [pltpu.make_async_copy(x_refs[a], o_refs[a].at[_dev_index(me)], local_sems.at[a]) for a in range(n)]
        for cp in mine:
            cp.start()
        first = []
        for a in range(n):
            first.append(copy(a, 0, me, sib, src=x_refs[a]))
            first += [copy(a, 1 + j, me, (*chip, c), src=x_refs[a]) for j, chip in enumerate(chips)]
        for cp in first:
            cp.start()
        passed = []
        for j, chip in enumerate(chips):
            for a in range(n):
                copy(a, 1 + j, (*chip, c), me).wait_recv()
                fwd = copy(a, 4 + j, (*chip, c), sib)
                fwd.start()
                passed.append(fwd)
        for a in range(n):
            copy(a, 0, sib, me).wait_recv()
            for j, chip in enumerate(chips):
                copy(a, 4 + j, (*chip, 1 - c), me).wait_recv()
        for cp in first + passed:
            cp.wait_send()
        for cp in mine:
            cp.wait()

    any_spec = pl.BlockSpec(memory_space=pl.ANY)
    return pl.pallas_call(
        body,
        name="gather_weights",
        out_shape=[_sds((N_DEV,) + a.shape, a.dtype) for a in xs],
        in_specs=[any_spec] * n,
        out_specs=[any_spec] * n,
        scratch_shapes=[pltpu.SemaphoreType.DMA((7 * n,)), pltpu.SemaphoreType.DMA((7 * n,)),
                        pltpu.SemaphoreType.DMA((n,))],
    )(*xs)


def _exchange_copies(b_refs, r_refs, send_sems, recv_sems, local_sems, incoming):
    n = len(b_refs)
    x, y, c = _mesh_pos()
    me_idx = _dev_index((x, y, c))
    mine = [pltpu.make_async_copy(b_refs[a].at[me_idx], r_refs[a].at[me_idx], local_sems.at[a]) for a in range(n)]
    remote = []
    for k, r in enumerate(RELS):
        peer = ((1 - x) if r[0] else x, (1 - y) if r[1] else y, (1 - c) if r[2] else c)
        pidx = _dev_index(peer)
        for a in range(n):
            remote.append(pltpu.make_async_remote_copy(
                src_ref=b_refs[a].at[pidx], dst_ref=r_refs[a].at[pidx if incoming else me_idx],
                send_sem=send_sems.at[a * 7 + k], recv_sem=recv_sems.at[a * 7 + k],
                device_id=peer, device_id_type=MESH))
    return mine, remote


def _exchange_start(b_refs, r_refs, sems):
    mine, out = _exchange_copies(b_refs, r_refs, *sems, incoming=False)
    for cp in mine + out:
        cp.start()


def _exchange_wait(b_refs, r_refs, sems):
    mine, inc = _exchange_copies(b_refs, r_refs, *sems, incoming=True)
    for cp in inc:
        cp.wait_recv()
    for cp in inc:
        cp.wait_send()
    for cp in mine:
        cp.wait()


def _proj_call(x2, w_t, bs, tm):
    s_len = x2.shape[0]
    n = len(bs)
    last = s_len // tm - 1

    def body(*refs):
        x_ref, w_ref = refs[:2]
        b_refs = refs[2:2 + n]
        (qft_ref, kft_ref, vf_ref, fz_ref, qst_ref, ks_ref, vs_ref, sz_ref, fft_ref, vat_ref,
         kst_ref, vsta_ref) = refs[2 + n:14 + n]
        r_refs = refs[14 + n:14 + 2 * n]
        sems = refs[14 + 2 * n:]

        @pl.when(pl.program_id(0) == 0)
        def _():
            _exchange_start(b_refs, r_refs, sems)

        xb = x_ref[...].astype(bf16)

        def seg_t(off, width):
            return lax.dot_general(w_ref[off:off + width, :], xb, NT, preferred_element_type=f32)

        def seg(off, width):
            return lax.dot_general(xb, w_ref[off:off + width, :], NT, preferred_element_type=f32)

        def put_heads(ref, acc, nheads):
            for h in range(nheads):
                ref[h] = acc[:, h * HEAD_DIM:(h + 1) * HEAD_DIM].astype(bf16)

        def put_heads_t(ref, acc_t, nheads, augment):
            for h in range(nheads):
                ref[h, 0:HEAD_DIM, :] = acc_t[h * HEAD_DIM:(h + 1) * HEAD_DIM, :].astype(bf16)
                if augment:
                    ref[h, HEAD_DIM:2 * HEAD_DIM, :] = ones_row

        ones_row = jnp.where(lax.broadcasted_iota(jnp.int32, (HEAD_DIM, tm), 0) == 0, 1.0, 0.0).astype(bf16)
        put_heads_t(vat_ref, seg_t(A_FV, 512), FOX_HEADS, True)
        put_heads_t(qft_ref, seg_t(A_FQ, 512) * SCALE, FOX_HEADS, False)
        put_heads_t(kft_ref, seg_t(A_FK, 512), FOX_HEADS, False)
        put_heads(vf_ref, seg(A_FV, 512), FOX_HEADS)
        fz_ref[...] = seg(A_FZ, 512)
        put_heads_t(qst_ref, seg_t(A_SQ, 512) * SCALE, SWA_HEADS, False)
        put_heads(ks_ref, seg(A_SK, 128), SWA_KV_HEADS)
        put_heads(vs_ref, seg(A_SV, 128), SWA_KV_HEADS)
        put_heads_t(kst_ref, seg_t(A_SK, 128), SWA_KV_HEADS, False)
        put_heads_t(vsta_ref, seg_t(A_SV, 128), SWA_KV_HEADS, True)
        sz_ref[...] = seg(A_SZ, 512)
        fft_ref[...] = seg(A_FF, 128).T[:FOX_HEADS, :]

        @pl.when(pl.program_id(0) == last)
        def _():
            _exchange_wait(b_refs, r_refs, sems)

    def heads(nh):
        return pl.BlockSpec((nh, tm, HEAD_DIM), lambda i: (0, i, 0))

    def feat(nh, rows):
        return pl.BlockSpec((nh, rows, tm), lambda i: (0, 0, i))

    wide = pl.BlockSpec((tm, 512), lambda i: (i, 0))
    any_spec = pl.BlockSpec(memory_space=pl.ANY)
    out = pl.pallas_call(
        body,
        name="proj_fwd_gather",
        grid=(s_len // tm,),
        in_specs=[pl.BlockSpec((tm, D_MODEL), lambda i: (i, 0)), pl.BlockSpec((A_W, D_MODEL), lambda i: (0, 0))]
                 + [any_spec] * n,
        out_specs=[feat(8, HEAD_DIM), feat(8, HEAD_DIM), heads(8), wide, feat(8, HEAD_DIM), heads(2), heads(2), wide,
                   pl.BlockSpec((FOX_HEADS, tm), lambda i: (0, i)),
                   feat(FOX_HEADS, 2 * HEAD_DIM), feat(2, HEAD_DIM), feat(2, 2 * HEAD_DIM)] + [any_spec] * n,
        out_shape=[_sds((8, HEAD_DIM, s_len), bf16)] * 2 + [_sds((8, s_len, HEAD_DIM), bf16)]
                  + [_sds((s_len, 512), f32), _sds((8, HEAD_DIM, s_len), bf16),
                     _sds((2, s_len, HEAD_DIM), bf16), _sds((2, s_len, HEAD_DIM), bf16), _sds((s_len, 512), f32),
                     _sds((FOX_HEADS, s_len), f32), _sds((FOX_HEADS, 2 * HEAD_DIM, s_len), bf16),
                     _sds((2, HEAD_DIM, s_len), bf16), _sds((2, 2 * HEAD_DIM, s_len), bf16)]
                  + [_sds(b.shape, b.dtype) for b in bs],
        scratch_shapes=[pltpu.SemaphoreType.DMA((7 * n,)), pltpu.SemaphoreType.DMA((7 * n,)),
                        pltpu.SemaphoreType.DMA((n,))],
        compiler_params=_params(("arbitrary",)),
    )(x2, w_t, *bs)
    return out[:12], out[12:]


AUG = 2 * HEAD_DIM
NEAR_KEYS = 3


def _augment_call(q_t, k_t, cum_row, tm):
    nh, _, s_len = k_t.shape
    per_step = tm // FOX_T

    def body(qt_ref, kt_ref, c_ref, qat_ref, ka_ref, kat_ref, st_ref):
        c = c_ref[0]
        hi = c.astype(bf16).astype(f32)
        r1 = c - hi
        mid = r1.astype(bf16).astype(f32)
        lo = (r1 - mid).astype(bf16).astype(f32)
        row = lax.broadcasted_iota(jnp.int32, (HEAD_DIM, tm), 0)
        q_tail = jnp.where(row == 0, hi, jnp.where(row == 1, mid, jnp.where(row == 2, lo,
                           jnp.where(row < 6, 1.0, 0.0))))
        k_tail = jnp.where(row < 3, 1.0, jnp.where(row == 3, -hi, jnp.where(row == 4, -mid,
                           jnp.where(row == 5, -lo, 0.0))))
        qat_ref[0, 0:HEAD_DIM, :] = qt_ref[0]
        qat_ref[0, HEAD_DIM:AUG, :] = q_tail.astype(bf16)
        kat_ref[0, 0:HEAD_DIM, :] = kt_ref[0]
        kat_ref[0, HEAD_DIM:AUG, :] = k_tail.astype(bf16)
        qt = qt_ref[0].astype(f32)
        kt = kt_ref[0].astype(f32)
        ka_ref[0] = jnp.concatenate([kt, k_tail], axis=0).T.astype(bf16)
        qn2 = jnp.sum(qt * qt, axis=0, keepdims=True)
        kn2 = jnp.sum(kt * kt, axis=0, keepdims=True)
        sd = jnp.sum(qt * kt, axis=0, keepdims=True)
        k_and_c = jnp.concatenate([kt, jnp.broadcast_to(c, (8, tm))], axis=0)
        lane = lax.broadcasted_iota(jnp.int32, (1, tm), 1)
        for shift in range(1, NEAR_KEYS + 1):
            prev = pltpu.roll(k_and_c, shift, axis=1)
            near = jnp.sum(qt * prev[0:HEAD_DIM], axis=0, keepdims=True) + (c - prev[HEAD_DIM:HEAD_DIM + 1])
            sd = jnp.maximum(sd, jnp.where(lane >= shift, near, NEG_INF))
        srow = lax.broadcasted_iota(jnp.int32, (8, LANES), 0)
        for part in range(per_step):
            sl = slice(part * FOX_T, (part + 1) * FOX_T)
            vals = [jnp.sqrt(jnp.max(qn2[:, sl], axis=1, keepdims=True)),
                    jnp.sqrt(jnp.max(kn2[:, sl], axis=1, keepdims=True)),
                    jnp.min(sd[:, sl], axis=1, keepdims=True),
                    jnp.max(c[:, sl], axis=1, keepdims=True), jnp.min(c[:, sl], axis=1, keepdims=True)]
            out = jnp.zeros((8, LANES), f32)
            for r, val in enumerate(vals):
                out = jnp.where(srow == r, val, out)
            st_ref[0, part] = out

    tile_t = pl.BlockSpec((1, HEAD_DIM, tm), lambda h, i: (h, 0, i))
    return pl.pallas_call(
        body,
        name="fox_augment",
        grid=(nh, s_len // tm),
        in_specs=[tile_t, tile_t, pl.BlockSpec((1, 1, tm), lambda h, i: (h, 0, i))],
        out_specs=[pl.BlockSpec((1, AUG, tm), lambda h, i: (h, 0, i)),
                   pl.BlockSpec((1, tm, AUG), lambda h, i: (h, i, 0)),
                   pl.BlockSpec((1, AUG, tm), lambda h, i: (h, 0, i)),
                   pl.BlockSpec((1, per_step, 8, LANES), lambda h, i: (h, i, 0, 0))],
        out_shape=[_sds((nh, AUG, s_len), bf16), _sds((nh, s_len, AUG), bf16), _sds((nh, AUG, s_len), bf16),
                   _sds((nh, s_len // FOX_T, 8, LANES), f32)],
        compiler_params=_params(("arbitrary", "arbitrary")),
    )(q_t, k_t, cum_row)


FOX_PRUNE_GAP = 32.0


def _fox_prune_tables(stats):
    s = stats[:, :, :, 0]
    qn, kn, sd, cmx, cmn = (s[:, :, r] for r in range(5))
    nt = s.shape[1]
    bound = qn[:, :, None] * kn[:, None, :] + (cmx[:, :, None] - cmn[:, None, :])
    margin = 0.01 + 1e-5 * (jnp.abs(cmx)[:, :, None] + jnp.abs(cmn)[:, None, :])
    qi = lax.broadcasted_iota(jnp.int32, (nt, nt), 0)
    kj = lax.broadcasted_iota(jnp.int32, (nt, nt), 1)
    skip = (bound + margin < sd[:, :, None] - FOX_PRUNE_GAP) & (kj < qi)[None]
    first = jnp.sum(jnp.cumprod(skip.astype(jnp.int32), axis=2), axis=2)
    tiles = lax.broadcasted_iota(jnp.int32, (1, nt), 1)
    cnt = tiles - first
    ends = jnp.cumsum(cnt, axis=1)
    off = ends - cnt
    kmax = nt * (nt - 1) // 2
    k = lax.broadcasted_iota(jnp.int32, (1, kmax), 1)
    pair_q = jnp.minimum(jnp.sum((ends[:, None, :] <= k[:, :, None]).astype(jnp.int32), axis=2), nt - 1)
    hit = pair_q[:, :, None] == tiles[:, None, :]
    first_k = jnp.sum(jnp.where(hit, first[:, None, :], 0), axis=2)
    off_k = jnp.sum(jnp.where(hit, off[:, None, :], 0), axis=2)
    pair_k = jnp.clip(first_k + k - off_k, 0, nt - 1)
    return (ends[:, nt - 1].astype(jnp.int32), pair_q.reshape(-1).astype(jnp.int32),
            pair_k.reshape(-1).astype(jnp.int32))


CUM_CHUNK = 512


def _cum_call(fft, bf_col):
    s_len = fft.shape[1]
    ch = CUM_CHUNK

    def body(f_ref, b_ref, cum_ref, sg_ref):
        r = lax.broadcasted_iota(jnp.int32, (ch, ch), 0)
        c = lax.broadcasted_iota(jnp.int32, (ch, ch), 1)
        upper = (r <= c).astype(f32)
        carry = jnp.zeros((FOX_HEADS, 1), f32)
        for n in range(s_len // ch):
            z = f_ref[:, n * ch:(n + 1) * ch] + b_ref[...]
            logf = jnp.minimum(z, 0.0) - jnp.log1p(jnp.exp(-jnp.abs(z)))
            sg_ref[:, n * ch:(n + 1) * ch] = 1.0 / (1.0 + jnp.exp(z))
            cs = jnp.dot(logf, upper, precision=HIGHEST, preferred_element_type=f32) + carry
            cum_ref[:, n * ch:(n + 1) * ch] = cs
            carry = cs[:, ch - 1:ch]

    return pl.pallas_call(
        body,
        name="fox_cum_fwd",
        out_shape=[_sds((FOX_HEADS, s_len), f32)] * 2,
        compiler_params=_params(),
    )(fft, bf_col)


def _cum_bwd_call(dcq, dck, sg):
    s_len = sg.shape[1]
    ch = CUM_CHUNK
    nch = s_len // ch

    def body(q_ref, k_ref, sg_ref, dff_ref, dbf_ref):
        r = lax.broadcasted_iota(jnp.int32, (ch, ch), 0)
        c = lax.broadcasted_iota(jnp.int32, (ch, ch), 1)
        lower = (r >= c).astype(f32)
        dff_ref[...] = jnp.zeros_like(dff_ref)
        carry = jnp.zeros((FOX_HEADS, 1), f32)
        total = jnp.zeros((FOX_HEADS, 1), f32)
        for n in reversed(range(nch)):
            sl = slice(n * ch, (n + 1) * ch)
            dcum = q_ref[:, sl] - k_ref[:, sl]
            rs = jnp.dot(dcum, lower, precision=HIGHEST, preferred_element_type=f32) + carry
            carry = rs[:, 0:1]
            dff = rs * sg_ref[:, sl]
            dff_ref[0:FOX_HEADS, sl] = dff
            total = total + jnp.sum(dff, axis=1, keepdims=True)
        dbf_ref[...] = jnp.broadcast_to(total, (FOX_HEADS, 128))

    return pl.pallas_call(
        body,
        name="fox_cum_bwd",
        out_shape=[_sds((128, s_len), f32), _sds((FOX_HEADS, 128), f32)],
        compiler_params=_params(),
    )(dcq, dck, sg)


FOX_T = 512
LANES = 128


def _causal_keep(t):
    return lax.broadcasted_iota(jnp.int32, (t, t), 0) <= lax.broadcasted_iota(jnp.int32, (t, t), 1)


def _tile_cols(i, t):
    return pl.ds(pl.multiple_of(i * t, t), t)


def _fox_pair(n, nt, kmax, h, pq_ref, pk_ref):
    k = h * kmax + jnp.maximum(n - nt, 0)
    return jnp.where(n < nt, n, pq_ref[k]), jnp.where(n < nt, n, pk_ref[k])


def _fox_fwd_call(qat, ka, vat, npairs, pair_q, pair_k):
    nh, s_len, _ = ka.shape
    t = FOX_T
    nt = s_len // t
    kmax = nt * (nt - 1) // 2
    assert nt >= 2 and nt % 2 == 0

    def body(np_ref, pq_ref, pk_ref, qat_ref, ka_ref, vat_ref, o_ref, lse_ref, s0, s1, p0, p1, a0, a1, m_all, acc_all):
        h = pl.program_id(0)
        extra = np_ref[h]
        total = nt + extra
        m_all[...] = jnp.full(m_all.shape, NEG_INF, f32)
        acc_all[...] = jnp.zeros(acc_all.shape, f32)
        bufs = ((s0, p0, a0), (s1, p1, a1))

        def pair(n):
            return _fox_pair(n, nt, kmax, h, pq_ref, pk_ref)

        def scores(n, b, masked):
            i, j = pair(n)
            st = jnp.dot(ka_ref[0, _tile_cols(j, t), :], qat_ref[0, :, _tile_cols(i, t)], preferred_element_type=f32)
            if masked:
                st = jnp.where(_causal_keep(t), st, NEG_INF)
            bufs[b][0][...] = st

        def softmax(n, b):
            i, _ = pair(n)
            s_ref, p_ref, a_ref = bufs[b]
            for c in range(t // LANES):
                cols = slice(c * LANES, (c + 1) * LANES)
                mcols = pl.ds(pl.multiple_of(i * t + c * LANES, LANES), LANES)
                m_old = m_all[:, mcols]
                m_new = jnp.maximum(m_old, jnp.max(s_ref[:, cols], axis=0, keepdims=True))
                m_all[:, mcols] = m_new
                a_ref[:, cols] = jnp.exp(m_old - m_new)
                p_ref[:, cols] = jnp.exp(s_ref[:, cols] - m_new).astype(bf16)

        def accum(n, b):
            i, j = pair(n)
            cols = _tile_cols(i, t)
            acc_all[:, cols] = bufs[b][2][...] * acc_all[:, cols] + jnp.dot(
                vat_ref[0, :, _tile_cols(j, t)], bufs[b][1][...], preferred_element_type=f32)

        def step(n, b, masked):
            accum(n - 2, b)
            softmax(n - 1, 1 - b)
            scores(n, b, masked)

        scores(0, 0, True)
        scores(1, 1, True)
        softmax(0, 0)

        def diag_steps(d, _):
            n = 2 + 2 * d
            step(n, 0, True)
            step(n + 1, 1, True)
            return 0

        lax.fori_loop(0, (nt - 2) // 2, diag_steps, 0)

        def off_steps(d, _):
            n = nt + 2 * d
            step(n, 0, False)
            step(n + 1, 1, False)
            return 0

        lax.fori_loop(0, extra // 2, off_steps, 0)

        @pl.when(extra % 2 == 1)
        def _():
            step(total - 1, 0, False)
            softmax(total - 1, 0)
            accum(total - 2, 1)
            accum(total - 1, 0)

        @pl.when(extra % 2 == 0)
        def _():
            softmax(total - 1, 1)
            accum(total - 2, 0)
            accum(total - 1, 1)

        l = acc_all[HEAD_DIM:HEAD_DIM + 1, :]
        o_ref[0] = acc_all[0:HEAD_DIM, :] / l
        lse_ref[0] = m_all[...] + jnp.log(l)

    smem = pl.BlockSpec(memory_space=pltpu.SMEM)
    return pl.pallas_call(
        body,
        name="fox_fwd",
        grid=(nh,),
        in_specs=[smem, smem, smem,
                  pl.BlockSpec((1, AUG, s_len), lambda h: (h, 0, 0)),
                  pl.BlockSpec((1, s_len, AUG), lambda h: (h, 0, 0)),
                  pl.BlockSpec((1, AUG, s_len), lambda h: (h, 0, 0))],
        out_specs=[pl.BlockSpec((1, HEAD_DIM, s_len), lambda h: (h, 0, 0)),
                   pl.BlockSpec((1, 1, s_len), lambda h: (h, 0, 0))],
        out_shape=[_sds((nh, HEAD_DIM, s_len), f32), _sds((nh, 1, s_len), f32)],
        scratch_shapes=[pltpu.VMEM((t, t), f32), pltpu.VMEM((t, t), f32), pltpu.VMEM((t, t), bf16),
                        pltpu.VMEM((t, t), bf16), pltpu.VMEM((1, t), f32), pltpu.VMEM((1, t), f32),
                        pltpu.VMEM((1, s_len), f32), pltpu.VMEM((AUG, s_len), f32)],
        compiler_params=_params(("arbitrary",)),
    )(npairs, pair_q, pair_k, qat, ka, vat)


SWA_TS = 512


SWA_W = SWA_GROUP * BLOCK


def _swa_bias_call(rel_bias, bucket_t):
    def body(rb_ref, bk_ref, b_ref, b0_ref):
        bk = bk_ref[...]
        row = lax.broadcasted_iota(jnp.int32, (2 * BLOCK, BLOCK), 0)
        for h in range(SWA_HEADS):
            acc = jnp.full((2 * BLOCK, BLOCK), NEG_INF, f32)
            for b in range(NUM_BUCKETS):
                acc = jnp.where(bk == b, rb_ref[b, h], acc)
            g, hh = divmod(h, SWA_GROUP)
            b_ref[g, :, hh * BLOCK:(hh + 1) * BLOCK] = acc
            b0_ref[g, :, hh * BLOCK:(hh + 1) * BLOCK] = jnp.where(row < BLOCK, NEG_INF, acc)

    return pl.pallas_call(
        body,
        name="swa_bias",
        in_specs=[pl.BlockSpec(memory_space=pltpu.SMEM), pl.BlockSpec(memory_space=pltpu.VMEM)],
        out_shape=[_sds((SWA_KV_HEADS, 2 * BLOCK, SWA_W), f32)] * 2,
        compiler_params=_params(),
    )(rel_bias, bucket_t)


def _swa_bias_bwd_call(dbias, bucket_t):
    def body(d_ref, bk_ref, o_ref):
        bk = bk_ref[...]
        row = lax.broadcasted_iota(jnp.int32, (NUM_BUCKETS, 128), 0)
        col = lax.broadcasted_iota(jnp.int32, (NUM_BUCKETS, 128), 1)
        out = jnp.zeros((NUM_BUCKETS, 128), f32)
        for h in range(SWA_HEADS):
            g, hh = divmod(h, SWA_GROUP)
            d = d_ref[g, :, hh * BLOCK:(hh + 1) * BLOCK]
            for b in range(NUM_BUCKETS):
                val = jnp.sum(jnp.sum(jnp.where(bk == b, d, 0.0), axis=1, keepdims=True), axis=0, keepdims=True)
                out = jnp.where((row == b) & (col == h), val, out)
        o_ref[...] = out

    return pl.pallas_call(
        body,
        name="swa_bias_bwd",
        out_shape=_sds((NUM_BUCKETS, 128), f32),
        compiler_params=_params(),
    )(dbias, bucket_t)


def _sink_row(sink_ref, g):
    return jnp.concatenate([jnp.full((1, BLOCK), sink_ref[g * SWA_GROUP + hh], f32) for hh in range(SWA_GROUP)], axis=1)


def _group_lanes(ref, g, cols):
    return jnp.concatenate([ref[g * SWA_GROUP + hh, :, cols] for hh in range(SWA_GROUP)], axis=1)


def _swa_fwd_call(qt, k, vta, bias_t, bias0_t, sink):
    s_len = qt.shape[2]
    ts = SWA_TS
    nb = ts // BLOCK

    def body(qt_ref, kc_ref, kp_ref, vc_ref, vp_ref, b_ref, b0_ref, sink_ref, o_ref, lse_ref):
        first = pl.program_id(0) == 0
        kall = [jnp.concatenate([kp_ref[g], kc_ref[g]], axis=0) for g in range(SWA_KV_HEADS)]
        vall = [jnp.concatenate([vp_ref[g], vc_ref[g]], axis=1) for g in range(SWA_KV_HEADS)]
        sinks = [_sink_row(sink_ref, g) for g in range(SWA_KV_HEADS)]
        items = [(g, b) for g in range(SWA_KV_HEADS) for b in range(nb)]

        def scores(g, b):
            qg = _group_lanes(qt_ref, g, slice(b * BLOCK, (b + 1) * BLOCK))
            bias_b = b_ref[g]
            if b == 0:
                bias_b = jnp.where(first, b0_ref[g], bias_b)
            return jnp.dot(kall[g][b * BLOCK:(b + 2) * BLOCK], qg, preferred_element_type=f32) + bias_b

        def finish(g, b, st):
            m = jnp.maximum(jnp.max(st, axis=0, keepdims=True), sinks[g])
            pt = jnp.exp(st - m)
            acc = jnp.dot(vall[g][:, b * BLOCK:(b + 2) * BLOCK], pt.astype(bf16), preferred_element_type=f32)
            l = acc[HEAD_DIM:HEAD_DIM + 1, :] + jnp.exp(sinks[g] - m)
            return acc[0:HEAD_DIM, :] / l, m + jnp.log(l)

        outs, lses = {}, {}
        st_next = scores(*items[0])
        for idx, (g, b) in enumerate(items):
            st = st_next
            if idx + 1 < len(items):
                st_next = scores(*items[idx + 1])
            outs[g, b], lses[g, b] = finish(g, b, st)
        for g in range(SWA_KV_HEADS):
            for hh in range(SWA_GROUP):
                lanes = slice(hh * BLOCK, (hh + 1) * BLOCK)
                o_ref[g * SWA_GROUP + hh] = jnp.concatenate([outs[g, b][:, lanes] for b in range(nb)], axis=1)
                lse_ref[g * SWA_GROUP + hh] = jnp.concatenate([lses[g, b][:, lanes] for b in range(nb)], axis=1)

    def prev_blk(n):
        return jnp.maximum(n * nb - 1, 0)

    bspec = pl.BlockSpec((SWA_KV_HEADS, 2 * BLOCK, SWA_W), lambda n: (0, 0, 0))
    return pl.pallas_call(
        body,
        name="swa_fwd",
        grid=(s_len // ts,),
        in_specs=[pl.BlockSpec((SWA_HEADS, HEAD_DIM, ts), lambda n: (0, 0, n)),
                  pl.BlockSpec((SWA_KV_HEADS, ts, HEAD_DIM), lambda n: (0, n, 0)),
                  pl.BlockSpec((SWA_KV_HEADS, BLOCK, HEAD_DIM), lambda n: (0, prev_blk(n), 0)),
                  pl.BlockSpec((SWA_KV_HEADS, AUG, ts), lambda n: (0, 0, n)),
                  pl.BlockSpec((SWA_KV_HEADS, AUG, BLOCK), lambda n: (0, 0, prev_blk(n))),
                  bspec, bspec, pl.BlockSpec(memory_space=pltpu.SMEM)],
        out_specs=[pl.BlockSpec((SWA_HEADS, HEAD_DIM, ts), lambda n: (0, 0, n)),
                   pl.BlockSpec((SWA_HEADS, 1, ts), lambda n: (0, 0, n))],
        out_shape=[_sds((SWA_HEADS, HEAD_DIM, s_len), f32), _sds((SWA_HEADS, 1, s_len), f32)],
        compiler_params=_params(("arbitrary",)),
    )(qt, k, k, vta, vta, bias_t, bias0_t, sink)


def _head_selector():
    sel = np.zeros((512, 128), np.float32)
    for h in range(8):
        sel[h * HEAD_DIM:(h + 1) * HEAD_DIM, h] = 1.0
    return sel


def _post_call(of, fz, osw, sz, x2, tgt, wo, ln_g, ln_b, sel, tm):
    s_len = x2.shape[0]

    def body(of_ref, fz_ref, os_ref, sz_ref, x_ref, t_ref, wo_ref, g_ref, b_ref, sel_ref,
             dh_ref, dof_ref, dfz_ref, dos_ref, dsz_ref, dlf_ref, dls_ref, dwo_ref, dg_ref, db_ref, loss_ref):
        n = pl.program_id(0)

        @pl.when(n == 0)
        def _():
            dwo_ref[...] = jnp.zeros_like(dwo_ref)
            dg_ref[...] = jnp.zeros_like(dg_ref)
            db_ref[...] = jnp.zeros_like(db_ref)
            loss_ref[...] = jnp.zeros_like(loss_ref)

        gam = g_ref[...]
        sel_m = sel_ref[...]

        def forward(r):
            o_f = of_ref[:, r].T
            o_s = os_ref[:, r].T
            fz = fz_ref[r, :]
            sz = sz_ref[r, :]
            sg_f = jax.nn.sigmoid(fz)
            sg_s = jax.nn.sigmoid(sz)
            silu_f = fz * sg_f
            silu_s = sz * sg_s
            mixed = jnp.concatenate([o_f * silu_f, o_s * silu_s], axis=1).astype(bf16)
            y = jnp.dot(mixed, wo_ref[...], preferred_element_type=f32)
            return o_f, o_s, fz, sz, sg_f, sg_s, silu_f, silu_s, mixed, y

        def norm_and_back(r, fwd):
            mixed, y = fwd[8], fwd[9]
            h = ALPHA * x_ref[r, :] + y
            mu = jnp.mean(h, axis=1, keepdims=True)
            hc = h - mu
            var = jnp.mean(hc * hc, axis=1, keepdims=True)
            rstd = lax.rsqrt(var + LN_EPS)
            xhat = hc * rstd
            out = xhat * gam + b_ref[...]
            err = out - t_ref[r, :]
            tok_loss = jnp.mean(err * err, axis=1, keepdims=True)
            loss_ref[...] += 0.5 * jnp.sum(tok_loss, axis=0, keepdims=True)
            dout = err * (1.0 / D_MODEL)
            dg_ref[...] += jnp.sum(dout * xhat, axis=0, keepdims=True)
            db_ref[...] += jnp.sum(dout, axis=0, keepdims=True)
            dxh = dout * gam
            m1 = jnp.mean(dxh, axis=1, keepdims=True)
            m2 = jnp.mean(dxh * xhat, axis=1, keepdims=True)
            dh = rstd * (dxh - m1 - xhat * m2)
            dh_ref[r, :] = dh
            dyb = dh.astype(bf16)
            dmix = lax.dot_general(dyb, wo_ref[...], NT, preferred_element_type=f32)
            dwo_ref[...] += lax.dot_general(mixed, dyb, TN, preferred_element_type=f32)
            return dmix

        def head_sums(prod):
            hi = prod.astype(bf16)
            lo = (prod - hi.astype(f32)).astype(bf16)
            return (jnp.dot(hi, sel_m, preferred_element_type=f32) + jnp.dot(lo, sel_m, preferred_element_type=f32))

        def gates_back(r, fwd, dmix):
            o_f, o_s, fz, sz, sg_f, sg_s, silu_f, silu_s = fwd[:8]
            dm_f = dmix[:, :512]
            dm_s = dmix[:, 512:]
            do_f = dm_f * silu_f
            do_s = dm_s * silu_s
            dfz_ref[r, :] = (dm_f * o_f * (sg_f * (1.0 + fz * (1.0 - sg_f)))).astype(bf16)
            dsz_ref[r, :] = (dm_s * o_s * (sg_s * (1.0 + sz * (1.0 - sg_s)))).astype(bf16)
            dof_ref[:, r] = do_f.T.astype(bf16)
            dos_ref[:, r] = do_s.T.astype(bf16)
            dlf_ref[:, r] = head_sums(do_f * o_f).T[:FOX_HEADS, :]
            dls_ref[:, r] = head_sums(do_s * o_s).T[:SWA_HEADS, :]

        halves = [slice(k * (tm // 2), (k + 1) * (tm // 2)) for k in range(2)]
        fwds = [forward(r) for r in halves]
        dmixes = [norm_and_back(r, f) for r, f in zip(halves, fwds)]
        for r, f, d in zip(halves, fwds, dmixes):
            gates_back(r, f, d)

    feat = pl.BlockSpec((512, tm), lambda n: (0, n))
    rows8 = pl.BlockSpec((8, tm), lambda n: (0, n))
    half = pl.BlockSpec((tm, 512), lambda n: (n, 0))
    fullw = pl.BlockSpec((tm, D_MODEL), lambda n: (n, 0))
    vec = pl.BlockSpec((1, D_MODEL), lambda n: (0, 0))
    return pl.pallas_call(
        body,
        name="post_fwd_bwd",
        grid=(s_len // tm,),
        in_specs=[feat, half, feat, half, fullw, fullw,
                  pl.BlockSpec((D_MODEL, D_MODEL), lambda n: (0, 0)), vec, vec,
                  pl.BlockSpec((512, 128), lambda n: (0, 0))],
        out_specs=[fullw, feat, half, feat, half, rows8, rows8,
                   pl.BlockSpec((D_MODEL, D_MODEL), lambda n: (0, 0)), vec, vec,
                   pl.BlockSpec((1, 1), lambda n: (0, 0))],
        out_shape=[_sds((s_len, D_MODEL), f32), _sds((512, s_len), bf16), _sds((s_len, 512), bf16),
                   _sds((512, s_len), bf16), _sds((s_len, 512), bf16),
                   _sds((FOX_HEADS, s_len), f32), _sds((SWA_HEADS, s_len), f32),
                   _sds((D_MODEL, D_MODEL), f32), _sds((1, D_MODEL), f32), _sds((1, D_MODEL), f32),
                   _sds((1, 1), f32)],
        compiler_params=_params(("arbitrary",), VMEM_LIMIT_BIG),
    )(of, fz, osw, sz, x2, tgt, wo, ln_g, ln_b, sel)


def _fox_bwd_call(ka, kat, v, qat, dot, lse_row, dl_row, npairs, pair_q, pair_k):
    nh, s_len, _ = ka.shape
    t = FOX_T
    nt = s_len // t
    kmax = nt * (nt - 1) // 2
    assert nt >= 2 and nt % 2 == 0
    ck_slot = HEAD_DIM + 3
    cq_slot = HEAD_DIM

    def body(np_ref, pq_ref, pk_ref, ka_ref, kat_ref, v_ref, qat_ref, dot_ref, lse_ref, dl_ref,
             dq_ref, dk_ref, dv_ref, dcq_ref, dck_ref, dqt_all, dkat_all, dvt_all, p0, p1, ds0, ds1):
        h = pl.program_id(0)
        extra = np_ref[h]
        total = nt + extra
        dqt_all[...] = jnp.zeros(dqt_all.shape, f32)
        dkat_all[...] = jnp.zeros(dkat_all.shape, f32)
        dvt_all[...] = jnp.zeros(dvt_all.shape, f32)
        pbuf, dsbuf = (p0, p1), (ds0, ds1)

        def pair(n):
            return _fox_pair(n, nt, kmax, h, pq_ref, pk_ref)

        def probs(n, b, masked):
            i, j = pair(n)
            qc, kr = _tile_cols(i, t), _tile_cols(j, t)
            st = jnp.dot(ka_ref[0, kr, :], qat_ref[0, :, qc], preferred_element_type=f32)
            dpt = jnp.dot(v_ref[0, kr, :], dot_ref[0, :, qc], preferred_element_type=f32)
            if masked:
                st = jnp.where(_causal_keep(t), st, NEG_INF)
            pt = jnp.exp(st - lse_ref[0, :, qc])
            pbuf[b][...] = pt.astype(bf16)
            dsbuf[b][...] = (pt * (dpt - dl_ref[0, :, qc])).astype(bf16)

        def grads(n, b):
            i, j = pair(n)
            qc, kc = _tile_cols(i, t), _tile_cols(j, t)
            dvt_all[:, kc] += lax.dot_general(dot_ref[0, :, qc], pbuf[b][...], NT, preferred_element_type=f32)
            dkat_all[:, kc] += lax.dot_general(qat_ref[0, :, qc], dsbuf[b][...], NT, preferred_element_type=f32)
            dqt_all[:, qc] += jnp.dot(kat_ref[0, :, kc], dsbuf[b][...], preferred_element_type=f32)

        def step(n, b, masked):
            i, j = pair(n)
            qc, kr = _tile_cols(i, t), _tile_cols(j, t)
            i1, j1 = pair(n - 1)
            qc1, kc1 = _tile_cols(i1, t), _tile_cols(j1, t)
            c = 1 - b
            st = jnp.dot(ka_ref[0, kr, :], qat_ref[0, :, qc], preferred_element_type=f32)
            dvt_all[:, kc1] += lax.dot_general(dot_ref[0, :, qc1], pbuf[c][...], NT, preferred_element_type=f32)
            if masked:
                st = jnp.where(_causal_keep(t), st, NEG_INF)
            pt = jnp.exp(st - lse_ref[0, :, qc])
            pbuf[b][...] = pt.astype(bf16)
            dpt = jnp.dot(v_ref[0, kr, :], dot_ref[0, :, qc], preferred_element_type=f32)
            dkat_all[:, kc1] += lax.dot_general(qat_ref[0, :, qc1], dsbuf[c][...], NT, preferred_element_type=f32)
            dqt_all[:, qc1] += jnp.dot(kat_ref[0, :, kc1], dsbuf[c][...], preferred_element_type=f32)
            dsbuf[b][...] = (pt * (dpt - dl_ref[0, :, qc])).astype(bf16)

        probs(0, 0, True)
        step(1, 1, True)

        def four_steps(n, masked):
            step(n, 0, masked)
            step(n + 1, 1, masked)
            step(n + 2, 0, masked)
            step(n + 3, 1, masked)

        def diag_quads(d, _):
            four_steps(2 + 4 * d, True)
            return 0

        lax.fori_loop(0, (nt - 2) // 4, diag_quads, 0)
        if (nt - 2) % 4:
            step(nt - 2, 0, True)
            step(nt - 1, 1, True)

        def off_quads(d, _):
            four_steps(nt + 4 * d, False)
            return 0

        quads = extra // 4
        lax.fori_loop(0, quads, off_quads, 0)

        def off_steps(d, _):
            n = nt + 4 * quads + 2 * d
            step(n, 0, False)
            step(n + 1, 1, False)
            return 0

        lax.fori_loop(0, (extra % 4) // 2, off_steps, 0)

        @pl.when(extra % 2 == 1)
        def _():
            step(total - 1, 0, False)
            grads(total - 1, 0)

        @pl.when(extra % 2 == 0)
        def _():
            grads(total - 1, 1)

        dq_ref[0] = (dqt_all[0:HEAD_DIM, :] * SCALE).astype(bf16)
        dk_ref[0] = dkat_all[0:HEAD_DIM, :].astype(bf16)
        dv_ref[0] = dvt_all[...].astype(bf16)
        dcq_ref[0] = dqt_all[cq_slot:cq_slot + 1, :]
        dck_ref[0] = dkat_all[ck_slot:ck_slot + 1, :]

    smem = pl.BlockSpec(memory_space=pltpu.SMEM)
    rows = pl.BlockSpec((1, s_len, AUG), lambda h: (h, 0, 0))
    feat = pl.BlockSpec((1, AUG, s_len), lambda h: (h, 0, 0))
    feat64 = pl.BlockSpec((1, HEAD_DIM, s_len), lambda h: (h, 0, 0))
    rowv = pl.BlockSpec((1, 1, s_len), lambda h: (h, 0, 0))
    return pl.pallas_call(
        body,
        name="fox_bwd",
        grid=(nh,),
        in_specs=[smem, smem, smem, rows, feat, pl.BlockSpec((1, s_len, HEAD_DIM), lambda h: (h, 0, 0)), feat, feat64,
                  rowv, rowv],
        out_specs=[feat64, feat64, feat64, rowv, rowv],
        out_shape=[_sds((nh, HEAD_DIM, s_len), bf16)] * 3 + [_sds((nh, 1, s_len), f32)] * 2,
        scratch_shapes=[pltpu.VMEM((AUG, s_len), f32), pltpu.VMEM((AUG, s_len), f32), pltpu.VMEM((HEAD_DIM, s_len), f32)]
                       + [pltpu.VMEM((t, t), bf16)] * 4,
        compiler_params=_params(("arbitrary",)),
    )(npairs, pair_q, pair_k, ka, kat, v, qat, dot, lse_row, dl_row)


def _swa_bwd_call(qt, k, kt, v, dot, lse, dl, bias_t, bias0_t, sink):
    s_len = qt.shape[2]
    ts = SWA_TS
    nb = ts // BLOCK
    nsteps = s_len // ts

    def body(qt_ref, kc_ref, kp_ref, ktc_ref, ktp_ref, vc_ref, vp_ref, dot_ref, lse_ref, dl_ref, b_ref, b0_ref,
             sink_ref, dq_ref, dk_ref, dv_ref, dbias_ref, dsink_ref, dk_s, dv_s, tail_k, tail_v, sk_s):
        n = pl.program_id(0)

        @pl.when(n == 0)
        def _():
            dbias_ref[...] = jnp.zeros_like(dbias_ref)
            sk_s[...] = jnp.zeros_like(sk_s)

        @pl.when(n < nsteps)
        def _():
            first = n == 0
            dk_s[...] = jnp.zeros_like(dk_s)
            dv_s[...] = jnp.zeros_like(dv_s)
            groups = range(SWA_KV_HEADS)
            kall = [jnp.concatenate([kp_ref[g], kc_ref[g]], axis=0) for g in groups]
            vall = [jnp.concatenate([vp_ref[g], vc_ref[g]], axis=0) for g in groups]
            ktall = [jnp.concatenate([ktp_ref[g], ktc_ref[g]], axis=1) for g in groups]
            sinks = [_sink_row(sink_ref, g) for g in groups]
            items = [(g, b) for g in groups for b in range(nb)]

            def products(g, b):
                cols = slice(b * BLOCK, (b + 1) * BLOCK)
                win = slice(b * BLOCK, (b + 2) * BLOCK)
                qg = _group_lanes(qt_ref, g, cols)
                dog = _group_lanes(dot_ref, g, cols)
                bias_b = b_ref[g]
                if b == 0:
                    bias_b = jnp.where(first, b0_ref[g], bias_b)
                st = jnp.dot(kall[g][win], qg, preferred_element_type=f32) + bias_b
                dpt = jnp.dot(vall[g][win], dog, preferred_element_type=f32)
                return qg, dog, st, dpt

            def finish(g, b, qg, dog, st, dpt):
                cols = slice(b * BLOCK, (b + 1) * BLOCK)
                win = slice(b * BLOCK, (b + 2) * BLOCK)
                lse_r = _group_lanes(lse_ref, g, cols)
                dl_r = _group_lanes(dl_ref, g, cols)
                pt = jnp.exp(st - lse_r)
                dst = pt * (dpt - dl_r)
                dsb = dst.astype(bf16)
                dk_s[g, :, win] += lax.dot_general(qg, dsb, NT, preferred_element_type=f32)
                dv_s[g, :, win] += lax.dot_general(dog, pt.astype(bf16), NT, preferred_element_type=f32)
                dqg = jnp.dot(ktall[g][:, win], dsb, preferred_element_type=f32) * SCALE
                return dqg, dst, -jnp.exp(sinks[g] - lse_r) * dl_r

            dqs, dsts, sks = {}, {}, {}
            nxt = products(*items[0])
            for idx, (g, b) in enumerate(items):
                cur = nxt
                if idx + 1 < len(items):
                    nxt = products(*items[idx + 1])
                dqs[g, b], dsts[g, b], sks[g, b] = finish(g, b, *cur)
            for g in groups:
                dbias_ref[g] += functools.reduce(lambda a, c: a + c, [dsts[g, b] for b in range(nb)])
                sk_s[g] += functools.reduce(lambda a, c: a + c, [sks[g, b] for b in range(nb)])
                for hh in range(SWA_GROUP):
                    lanes = slice(hh * BLOCK, (hh + 1) * BLOCK)
                    dq_ref[g * SWA_GROUP + hh] = jnp.concatenate(
                        [dqs[g, b][:, lanes] for b in range(nb)], axis=1).astype(bf16)

        @pl.when(n > 0)
        def _():
            last = slice(ts - BLOCK, ts)
            for g in range(SWA_KV_HEADS):
                add_k = jnp.where(n < nsteps, dk_s[g, :, 0:BLOCK], 0.0)
                add_v = jnp.where(n < nsteps, dv_s[g, :, 0:BLOCK], 0.0)
                dk_ref[g, :, 0:ts - BLOCK] = tail_k[g, :, 0:ts - BLOCK].astype(bf16)
                dv_ref[g, :, 0:ts - BLOCK] = tail_v[g, :, 0:ts - BLOCK].astype(bf16)
                dk_ref[g, :, last] = (tail_k[g, :, last] + add_k).astype(bf16)
                dv_ref[g, :, last] = (tail_v[g, :, last] + add_v).astype(bf16)

        @pl.when(n < nsteps)
        def _():
            tail_k[...] = dk_s[:, :, BLOCK:]
            tail_v[...] = dv_s[:, :, BLOCK:]

        @pl.when(n == nsteps)
        def _():
            row = lax.broadcasted_iota(jnp.int32, (SWA_HEADS, 128), 0)
            out = jnp.zeros((SWA_HEADS, 128), f32)
            for h in range(SWA_HEADS):
                g, hh = divmod(h, SWA_GROUP)
                val = jnp.sum(sk_s[g, :, hh * BLOCK:(hh + 1) * BLOCK], axis=1, keepdims=True)
                out = jnp.where(row == h, val, out)
            dsink_ref[...] = out

    last_step = nsteps - 1

    def cl(n):
        return jnp.minimum(n, last_step)

    def prev_blk(n):
        return jnp.maximum(cl(n) * nb - 1, 0)

    feat8 = pl.BlockSpec((SWA_HEADS, HEAD_DIM, ts), lambda n: (0, 0, cl(n)))
    rows8 = pl.BlockSpec((SWA_HEADS, 1, ts), lambda n: (0, 0, cl(n)))
    cur = pl.BlockSpec((SWA_KV_HEADS, ts, HEAD_DIM), lambda n: (0, cl(n), 0))
    prev = pl.BlockSpec((SWA_KV_HEADS, BLOCK, HEAD_DIM), lambda n: (0, prev_blk(n), 0))
    curt = pl.BlockSpec((SWA_KV_HEADS, HEAD_DIM, ts), lambda n: (0, 0, cl(n)))
    prevt = pl.BlockSpec((SWA_KV_HEADS, HEAD_DIM, BLOCK), lambda n: (0, 0, prev_blk(n)))
    bspec = pl.BlockSpec((SWA_KV_HEADS, 2 * BLOCK, SWA_W), lambda n: (0, 0, 0))
    kvout = pl.BlockSpec((SWA_KV_HEADS, HEAD_DIM, ts), lambda n: (0, 0, jnp.maximum(n - 1, 0)))
    return pl.pallas_call(
        body,
        name="swa_bwd",
        grid=(nsteps + 1,),
        in_specs=[feat8, cur, prev, curt, prevt, cur, prev, feat8, rows8, rows8, bspec, bspec,
                  pl.BlockSpec(memory_space=pltpu.SMEM)],
        out_specs=[feat8, kvout, kvout, bspec, pl.BlockSpec((SWA_HEADS, 128), lambda n: (0, 0))],
        out_shape=[_sds((SWA_HEADS, HEAD_DIM, s_len), bf16), _sds((SWA_KV_HEADS, HEAD_DIM, s_len), bf16),
                   _sds((SWA_KV_HEADS, HEAD_DIM, s_len), bf16),
                   _sds((SWA_KV_HEADS, 2 * BLOCK, SWA_W), f32), _sds((SWA_HEADS, 128), f32)],
        scratch_shapes=[pltpu.VMEM((SWA_KV_HEADS, HEAD_DIM, ts + BLOCK), f32),
                        pltpu.VMEM((SWA_KV_HEADS, HEAD_DIM, ts + BLOCK), f32),
                        pltpu.VMEM((SWA_KV_HEADS, HEAD_DIM, ts), f32),
                        pltpu.VMEM((SWA_KV_HEADS, HEAD_DIM, ts), f32),
                        pltpu.VMEM((SWA_KV_HEADS, 1, SWA_W), f32)],
        compiler_params=_params(("arbitrary",)),
    )(qt, k, k, kt, kt, v, v, dot, lse, dl, bias_t, bias0_t, sink)


def _dproj_specs(tm):
    half = pl.BlockSpec((tm, 512), lambda i: (i, 0))
    feat = pl.BlockSpec((512, tm), lambda i: (0, i))
    feat_kv = pl.BlockSpec((128, tm), lambda i: (0, i))
    return [feat, feat, feat, half, feat, feat_kv, feat_kv, half, feat_kv]


def _dx_exchange_call(dh, pieces, w_t, bs, tm):
    s_len = dh.shape[0]
    n = len(bs)
    last = s_len // tm - 1

    def body(*refs):
        dh_ref, dqf_ref, dkf_ref, dvf_ref, dfz_ref, dqs_ref, dks_ref, dvs_ref, dsz_ref, dfft_ref, w_ref = refs[:11]
        b_refs = refs[11:11 + n]
        dx_ref = refs[11 + n]
        r_refs = refs[12 + n:12 + 2 * n]
        sems = refs[12 + 2 * n:]
        i = pl.program_id(0)

        @pl.when(i == 0)
        def _():
            _exchange_start(b_refs, r_refs, sems)

        def tr(ref):
            return ref[...].astype(f32).T.astype(bf16)

        dp = jnp.concatenate([tr(dqf_ref), tr(dkf_ref), tr(dvf_ref), dfz_ref[...], tr(dqs_ref), tr(dks_ref),
                              tr(dvs_ref), dsz_ref[...], tr(dfft_ref)], axis=1)
        dx_ref[...] = ALPHA * dh_ref[...] + jnp.dot(dp, w_ref[...], preferred_element_type=f32)

        @pl.when(i == last)
        def _():
            _exchange_wait(b_refs, r_refs, sems)

    fullw = pl.BlockSpec((tm, D_MODEL), lambda i: (i, 0))
    any_spec = pl.BlockSpec(memory_space=pl.ANY)
    out = pl.pallas_call(
        body,
        name="dx_bwd_exchange",
        grid=(s_len // tm,),
        in_specs=[fullw] + _dproj_specs(tm) + [pl.BlockSpec((A_W, D_MODEL), lambda i: (0, 0))] + [any_spec] * n,
        out_specs=[fullw] + [any_spec] * n,
        out_shape=[_sds((s_len, D_MODEL), f32)] + [_sds(b.shape, b.dtype) for b in bs],
        scratch_shapes=[pltpu.SemaphoreType.DMA((7 * n,)), pltpu.SemaphoreType.DMA((7 * n,)),
                        pltpu.SemaphoreType.DMA((n,))],
        compiler_params=_params(("arbitrary",)),
    )(dh, *pieces, w_t, *bs)
    return out[0], out[1:]


DW_STAGE_ROWS = 384


def _dw_exchange_call(x2, pieces, bs, tm):
    s_len = x2.shape[0]
    nt = s_len // tm
    n = len(bs)

    def body(*refs):
        x_ref, dqf_ref, dkf_ref, dvf_ref, dfz_ref, dqs_ref, dks_ref, dvs_ref, dsz_ref, dfft_ref = refs[:10]
        b_refs = refs[10:10 + n]
        dw_ref = refs[10 + n]
        r_refs = refs[11 + n:11 + 2 * n]
        acc_ref, stage_ref, sem = refs[11 + 2 * n:14 + 2 * n]
        sems = refs[14 + 2 * n:]
        i = pl.program_id(0)

        @pl.when(i == 0)
        def _():
            _exchange_start(b_refs, r_refs, sems)
            acc_ref[...] = jnp.zeros_like(acc_ref)

        xb = x_ref[...].astype(bf16)

        def add_feat(off, lhs):
            acc_ref[off:off + lhs.shape[0], :] += jnp.dot(lhs, xb, preferred_element_type=f32)

        def add_rows(off, piece):
            acc_ref[off:off + piece.shape[1], :] += lax.dot_general(piece, xb, TN, preferred_element_type=f32)

        add_feat(A_FQ, dqf_ref[...])
        add_feat(A_FK, dkf_ref[...])
        add_feat(A_FV, dvf_ref[...])
        add_rows(A_FZ, dfz_ref[...])
        add_feat(A_SQ, dqs_ref[...])
        add_feat(A_SK, dks_ref[...])
        add_feat(A_SV, dvs_ref[...])
        add_rows(A_SZ, dsz_ref[...])
        add_feat(A_FF, dfft_ref[...].astype(bf16))

        @pl.when(i == nt - 1)
        def _():
            for r in range(A_W // DW_STAGE_ROWS):
                rows = slice(r * DW_STAGE_ROWS, (r + 1) * DW_STAGE_ROWS)
                stage_ref[...] = acc_ref[rows, :].astype(bf16)
                cp = pltpu.make_async_copy(stage_ref, dw_ref.at[rows, :], sem)
                cp.start()
                cp.wait()
            _exchange_wait(b_refs, r_refs, sems)

    any_spec = pl.BlockSpec(memory_space=pl.ANY)
    out = pl.pallas_call(
        body,
        name="dw_bwd_exchange",
        grid=(nt,),
        in_specs=[pl.BlockSpec((tm, D_MODEL), lambda i: (i, 0))] + _dproj_specs(tm) + [any_spec] * n,
        out_specs=[any_spec] * (1 + n),
        out_shape=[_sds((A_W, D_MODEL), bf16)] + [_sds(b.shape, b.dtype) for b in bs],
        scratch_shapes=[pltpu.VMEM((A_W, D_MODEL), f32), pltpu.VMEM((DW_STAGE_ROWS, D_MODEL), bf16),
                        pltpu.SemaphoreType.DMA, pltpu.SemaphoreType.DMA((7 * n,)), pltpu.SemaphoreType.DMA((7 * n,)),
                        pltpu.SemaphoreType.DMA((n,))],
        compiler_params=_params(("arbitrary",), VMEM_LIMIT_BIG),
    )(x2, *pieces, *bs)
    return out[0], out[1:]


def _adam_call(recv, w, m, v, tc, name):
    rows, cols = w.shape

    def body(r_ref, w_ref, m_ref, v_ref, g_ref, d_ref, mo_ref, vo_ref):
        g = r_ref[0].astype(f32)
        for p in range(1, N_DEV):
            g = g + r_ref[p].astype(f32)
        mn = ADAM_B1 * m_ref[...] + (1.0 - ADAM_B1) * g
        vn = ADAM_B2 * v_ref[...] + (1.0 - ADAM_B2) * (g * g)
        m_hat = mn / (1.0 - ADAM_B1 ** ADAM_STEP)
        v_hat = vn / (1.0 - ADAM_B2 ** ADAM_STEP)
        g_ref[...] = g
        d_ref[...] = -ADAM_LR * (m_hat / (jnp.sqrt(v_hat) + ADAM_EPS) + ADAM_WD * w_ref[...])
        mo_ref[...] = mn
        vo_ref[...] = vn

    blk = pl.BlockSpec((rows, tc), lambda i: (0, i))
    return pl.pallas_call(
        body,
        name=name,
        grid=(cols // tc,),
        in_specs=[pl.BlockSpec((N_DEV, rows, tc), lambda i: (0, 0, i)), blk, blk, blk],
        out_specs=[blk] * 4,
        out_shape=[_sds((rows, cols), f32)] * 4,
        compiler_params=_params(("arbitrary",)),
    )(recv, w, m, v)


def _rows_to_shards(parts, shard):
    blocks = []
    for d in range(N_DEV):
        lo, hi, start, pieces = d * shard, (d + 1) * shard, 0, []
        for part in parts:
            a, b = max(lo, start), min(hi, start + part.shape[0])
            if a < b:
                pieces.append(part[a - start:b - start])
            start += part.shape[0]
        blocks.append(jnp.concatenate(pieces, axis=0))
    return jnp.stack(blocks)


def _pad_cols(a, width=128):
    return jnp.pad(a, ((0, 0), (0, width - a.shape[1])))


def _pack_small(ln_g, ln_b, rel, b_f, sink):
    return jnp.concatenate([
        ln_g.reshape(8, 128), ln_b.reshape(8, 128), _pad_cols(rel),
        jnp.pad(_pad_cols(b_f), ((0, 7), (0, 0))), jnp.pad(_pad_cols(sink), ((0, 7), (0, 0)))], axis=0)


def _unpack_small(p):
    return (p[0:8].reshape(1, D_MODEL), p[8:16].reshape(1, D_MODEL), p[16:48, 0:8], p[48:49, 0:8], p[56:57, 0:8])


def kernel(x, w_in, b_f, rel_bias, sink, w_o, ln_g, ln_b, loss_target, m_w_in, m_b_f, m_rel_bias, m_sink, m_w_o, m_ln_g, m_ln_b, v_w_in, v_b_f, v_rel_bias, v_sink, v_w_o, v_ln_g, v_ln_b):
    x2 = x[0]
    tgt = loss_target[0]
    s_len = x2.shape[0]
    shard = w_in.shape[2]

    w_in_t = jnp.transpose(w_in[0])
    (g_in,) = _gather_call([w_in_t.astype(bf16)])
    wt_full = g_in.reshape(N_DEV * shard, D_MODEL)
    w_t = jnp.concatenate([wt_full[:O_FF0], wt_full[O_FF1:], wt_full[O_FF0:O_FF1],
                           jnp.zeros((A_W - D_IN, D_MODEL), bf16)], axis=0)

    wo_blocks = jnp.broadcast_to(w_o[0].astype(bf16)[None], (N_DEV,) + w_o.shape[1:])
    (qft, kft, vf, fz, qst, ks, vs, sz, fft, vat, kst, vsta), (g_o,) = _proj_call(x2, w_t, [wo_blocks], 512)
    wo_full = g_o.reshape(D_MODEL, D_MODEL)
    cum, sgm = _cum_call(fft, b_f.reshape(FOX_HEADS, 1))
    qat, ka, kat, tile_stats = _augment_call(qft, kft, cum.reshape(FOX_HEADS, 1, s_len), 2048)
    npairs, pair_q, pair_k = _fox_prune_tables(tile_stats)
    o_ft, lse_f = _fox_fwd_call(qat, ka, vat, npairs, pair_q, pair_k)
    bucket_t = jnp.asarray(_t5_bucket_table().T)
    bias_t, bias0_t = _swa_bias_call(rel_bias, bucket_t)
    sink_v = sink.reshape(SWA_HEADS)
    o_st, lse_s = _swa_fwd_call(qst, ks, vsta, bias_t, bias0_t, sink_v)

    (dh, do_f, dfz, do_s, dsz, dl_f, dl_s, dwo, dg, db, loss_part) = _post_call(
        o_ft.reshape(FOX_HEADS * HEAD_DIM, s_len), fz, o_st.reshape(SWA_HEADS * HEAD_DIM, s_len), sz, x2, tgt,
        wo_full, ln_g, ln_b, jnp.asarray(_head_selector()).astype(bf16), 512)

    dqf, dkf, dvf, dcq, dck = _fox_bwd_call(ka, kat, vf, qat, do_f.reshape(FOX_HEADS, HEAD_DIM, s_len), lse_f,
                                            dl_f.reshape(FOX_HEADS, 1, s_len), npairs, pair_q, pair_k)
    dqf, dkf, dvf = (a.reshape(FOX_HEADS * HEAD_DIM, s_len) for a in (dqf, dkf, dvf))
    dfft, dbf = _cum_bwd_call(dcq.reshape(FOX_HEADS, s_len), dck.reshape(FOX_HEADS, s_len), sgm)
    dqs, dks, dvs, dbias, dsink = _swa_bwd_call(
        qst, ks, kst, vs, do_s.reshape(SWA_HEADS, HEAD_DIM, s_len), lse_s, dl_s.reshape(SWA_HEADS, 1, s_len),
        bias_t, bias0_t, sink_v)
    dqs = dqs.reshape(SWA_HEADS * HEAD_DIM, s_len)
    dks, dvs = (a.reshape(SWA_KV_HEADS * HEAD_DIM, s_len) for a in (dks, dvs))
    drel = _swa_bias_bwd_call(dbias, bucket_t)

    dwo_blocks = dwo.reshape(N_DEV, D_MODEL // N_DEV, D_MODEL).astype(bf16)
    small = _pack_small(dg, db, drel[:, 0:8], dbf[:, 0].reshape(1, 8), dsink[:, 0].reshape(1, 8))
    loss_slot = np.zeros((64, 128), bool)
    loss_slot[49, 0] = True
    small = jnp.where(jnp.asarray(loss_slot), loss_part[0, 0], small)
    small_blocks = jnp.broadcast_to(small[None], (N_DEV,) + small.shape)
    pieces = (dqf, dkf, dvf, dfz, dqs, dks, dvs, dsz, dfft)
    dw_t, (r_o, r_small) = _dw_exchange_call(x2, pieces, [dwo_blocks, small_blocks], 1024)
    dw_blocks = _rows_to_shards([dw_t[:O_FF0], dw_t[A_FF:A_FF + (O_FF1 - O_FF0)], dw_t[O_FF0:A_FF]], shard)
    dx, (r_in,) = _dx_exchange_call(dh, pieces, w_t, [dw_blocks], 256)

    win_t = [jnp.transpose(a) for a in _adam_call(
        r_in, w_in_t, jnp.transpose(m_w_in[0]), jnp.transpose(v_w_in[0]), 256, "adam_w_in")]
    g_win, d_win, nm_win, nv_win = win_t
    g_wo, d_wo, nm_wo, nv_wo = _adam_call(r_o, w_o[0], m_w_o[0], v_w_o[0], 256, "adam_w_o")
    p_w = _pack_small(ln_g, ln_b, rel_bias, b_f, sink)
    p_m = _pack_small(m_ln_g, m_ln_b, m_rel_bias, m_b_f, m_sink)
    p_v = _pack_small(v_ln_g, v_ln_b, v_rel_bias, v_b_f, v_sink)
    g_p, d_p, nm_p, nv_p = _adam_call(r_small, p_w, p_m, p_v, 128, "adam_small")

    loss = g_p[49, 0]
    g_lng, g_lnb, g_rel, g_bf, g_sink = _unpack_small(g_p)
    d_lng, d_lnb, d_rel, d_bf, d_sink = _unpack_small(d_p)
    m_lng, m_lnb, m_rel, m_bf, m_sk = _unpack_small(nm_p)
    v_lng, v_lnb, v_rel, v_bf, v_sk = _unpack_small(nv_p)
    return (loss, dx[None], g_win[None], g_bf, g_rel, g_sink, g_wo[None], g_lng, g_lnb,
            d_win[None], d_bf, d_rel, d_sink, d_wo[None], d_lng, d_lnb,
            nm_win[None], m_bf, m_rel, m_sk, nm_wo[None], m_lng, m_lnb,
            nv_win[None], v_bf, v_rel, v_sk, nv_wo[None], v_lng, v_lnb)
```

```python
import functools
import math

import numpy as np
import jax
import jax.numpy as jnp
from jax import lax
from jax.experimental import pallas as pl
from jax.experimental.pallas import tpu as pltpu

f32 = jnp.float32
bf16 = jnp.bfloat16

D_MODEL = 1024
HEAD_DIM = 64
FOX_HEADS = 8
SWA_HEADS = 8
SWA_KV_HEADS = 2
SWA_GROUP = 4
BLOCK = 128
NUM_BUCKETS = 32
MAX_DISTANCE = 128
LN_EPS = 1e-5
NEG_INF = -1e30
ALPHA = 2.0 ** 0.25
SCALE = 1.0 / math.sqrt(HEAD_DIM)
D_IN = 3336

ADAM_LR = 0.001
ADAM_B1 = 0.9
ADAM_B2 = 0.999
ADAM_EPS = 1e-08
ADAM_WD = 0.01
ADAM_STEP = 10

N_DEV = 8
A_FQ, A_FK, A_FV, A_FZ, A_SQ, A_SK, A_SV, A_SZ, A_FF, A_W = 0, 512, 1024, 1536, 2048, 2560, 2688, 2816, 3328, 3456
O_FF0, O_FF1 = 1536, 1544

VMEM_LIMIT = 48 * 1024 * 1024
HIGHEST = lax.Precision.HIGHEST
NT = (((1,), (1,)), ((), ()))
TN = (((0,), (0,)), ((), ()))
MESH = pl.DeviceIdType.MESH
RELS = [(0, 0, 1), (0, 1, 0), (0, 1, 1), (1, 0, 0), (1, 0, 1), (1, 1, 0), (1, 1, 1)]


VMEM_LIMIT_BIG = 60 * 1024 * 1024


def _params(sem=None, vmem=VMEM_LIMIT):
    return pltpu.CompilerParams(dimension_semantics=sem, vmem_limit_bytes=vmem)


def _sds(shape, dtype):
    return jax.ShapeDtypeStruct(shape, dtype)


def _t5_bucket_table():
    qi = np.arange(BLOCK)[:, None]
    kj = np.arange(2 * BLOCK)[None, :]
    rel = qi + BLOCK - kj
    band = (rel >= 0) & (rel < BLOCK)
    relc = np.maximum(rel, 0)
    max_exact = NUM_BUCKETS // 2
    relf = np.maximum(relc, 1).astype(np.float32)
    large = max_exact + (np.log(relf / np.float32(max_exact)) / np.float32(math.log(MAX_DISTANCE / max_exact))
                         * np.float32(NUM_BUCKETS - max_exact)).astype(np.int32)
    large = np.minimum(large, NUM_BUCKETS - 1)
    bucket = np.where(relc < max_exact, relc, large).astype(np.int32)
    bucket = np.where(band, bucket, -1).astype(np.int32)
    return bucket


def _mesh_pos():
    return lax.axis_index("x"), lax.axis_index("y"), lax.axis_index("c")


def _dev_index(p):
    return 4 * p[0] + 2 * p[1] + p[2]


def _gather_call(xs):
    n = len(xs)

    def body(*refs):
        x_refs, o_refs = refs[:n], refs[n:2 * n]
        send_sems, recv_sems, local_sems = refs[2 * n:]
        x, y, c = _mesh_pos()
        me, sib = (x, y, c), (x, y, 1 - c)
        chips = [(1 - x, y), (x, 1 - y), (1 - x, 1 - y)]

        def copy(a, k, block, to, src=None):
            slot = o_refs[a].at[_dev_index(block)]
            return pltpu.make_async_remote_copy(
                src_ref=slot if src is None else src, dst_ref=slot,
                send_sem=send_sems.at[a * 7 + k], recv_sem=recv_sems.at[a * 7 + k],
                device_id=to, device_id_type=MESH)

        mine = [pltpu.make_async_copy(x_refs[a], o_refs[a].at[_dev_index(me)], local_sems.at[a]) for a in range(n)]
        for cp in mine:
            cp.start()
        first = []
        for a in range(n):
            first.append(copy(a, 0, me, sib, src=x_refs[a]))
            first += [copy(a, 1 + j, me, (*chip, c), src=x_refs[a]) for j, chip in enumerate(chips)]
        for cp in first:
            cp.start()
        passed = []
        for j, chip in enumerate(chips):
            for a in range(n):
                copy(a, 1 + j, (*chip, c), me).wait_recv()
                fwd = copy(a, 4 + j, (*chip, c), sib)
                fwd.start()
                passed.append(fwd)
        for a in range(n):
            copy(a, 0, sib, me).wait_recv()
            for j, chip in enumerate(chips):
                copy(a, 4 + j, (*chip, 1 - c), me).wait_recv()
        for cp in first + passed:
            cp.wait_send()
        for cp in mine:
            cp.wait()

    any_spec = pl.BlockSpec(memory_space=pl.ANY)
    return pl.pallas_call(
        body,
        name="gather_weights",
        out_shape=[_sds((N_DEV,) + a.shape, a.dtype) for a in xs],
        in_specs=[any_spec] * n,
        out_specs=[any_spec] * n,
        scratch_shapes=[pltpu.SemaphoreType.DMA((7 * n,)), pltpu.SemaphoreType.DMA((7 * n,)),
                        pltpu.SemaphoreType.DMA((n,))],
    )(*xs)


def _exchange_copies(b_refs, r_refs, send_sems, recv_sems, local_sems, incoming):
    n = len(b_refs)
    x, y, c = _mesh_pos()
    me_idx = _dev_index((x, y, c))
    mine = [pltpu.make_async_copy(b_refs[a].at[me_idx], r_refs[a].at[me_idx], local_sems.at[a]) for a in range(n)]
    remote = []
    for k, r in enumerate(RELS):
        peer = ((1 - x) if r[0] else x, (1 - y) if r[1] else y, (1 - c) if r[2] else c)
        pidx = _dev_index(peer)
        for a in range(n):
            remote.append(pltpu.make_async_remote_copy(
                src_ref=b_refs[a].at[pidx], dst_ref=r_refs[a].at[pidx if incoming else me_idx],
                send_sem=send_sems.at[a * 7 + k], recv_sem=recv_sems.at[a * 7 + k],
                device_id=peer, device_id_type=MESH))
    return mine, remote


def _exchange_start(b_refs, r_refs, sems):
    mine, out = _exchange_copies(b_refs, r_refs, *sems, incoming=False)
    for cp in mine + out:
        cp.start()


def _exchange_wait(b_refs, r_refs, sems):
    mine, inc = _exchange_copies(b_refs, r_refs, *sems, incoming=True)
    for cp in inc:
        cp.wait_recv()
    for cp in inc:
        cp.wait_send()
    for cp in mine:
        cp.wait()


def _proj_call(x2, w_t, bs, tm):
    s_len = x2.shape[0]
    n = len(bs)
    last = s_len // tm - 1

    def body(*refs):
        x_ref, w_ref = refs[:2]
        b_refs = refs[2:2 + n]
        (qft_ref, kft_ref, vf_ref, fz_ref, qst_ref, ks_ref, vs_ref, sz_ref, fft_ref, vat_ref,
         kst_ref, vsta_ref) = refs[2 + n:14 + n]
        r_refs = refs[14 + n:14 + 2 * n]
        sems = refs[14 + 2 * n:]

        @pl.when(pl.program_id(0) == 0)
        def _():
            _exchange_start(b_refs, r_refs, sems)

        xb = x_ref[...].astype(bf16)

        def seg_t(off, width):
            return lax.dot_general(w_ref[off:off + width, :], xb, NT, preferred_element_type=f32)

        def seg(off, width):
            return lax.dot_general(xb, w_ref[off:off + width, :], NT, preferred_element_type=f32)

        def put_heads(ref, acc, nheads):
            for h in range(nheads):
                ref[h] = acc[:, h * HEAD_DIM:(h + 1) * HEAD_DIM].astype(bf16)

        def put_heads_t(ref, acc_t, nheads, augment):
            for h in range(nheads):
                ref[h, 0:HEAD_DIM, :] = acc_t[h * HEAD_DIM:(h + 1) * HEAD_DIM, :].astype(bf16)
                if augment:
                    ref[h, HEAD_DIM:2 * HEAD_DIM, :] = ones_row

        ones_row = jnp.where(lax.broadcasted_iota(jnp.int32, (HEAD_DIM, tm), 0) == 0, 1.0, 0.0).astype(bf16)
        put_heads_t(vat_ref, seg_t(A_FV, 512), FOX_HEADS, True)
        put_heads_t(qft_ref, seg_t(A_FQ, 512) * SCALE, FOX_HEADS, False)
        put_heads_t(kft_ref, seg_t(A_FK, 512), FOX_HEADS, False)
        put_heads(vf_ref, seg(A_FV, 512), FOX_HEADS)
        fz_ref[...] = seg(A_FZ, 512)
        put_heads_t(qst_ref, seg_t(A_SQ, 512) * SCALE, SWA_HEADS, False)
        put_heads(ks_ref, seg(A_SK, 128), SWA_KV_HEADS)
        put_heads(vs_ref, seg(A_SV, 128), SWA_KV_HEADS)
        put_heads_t(kst_ref, seg_t(A_SK, 128), SWA_KV_HEADS, False)
        put_heads_t(vsta_ref, seg_t(A_SV, 128), SWA_KV_HEADS, True)
        sz_ref[...] = seg(A_SZ, 512)
        fft_ref[...] = seg(A_FF, 128).T[:FOX_HEADS, :]

        @pl.when(pl.program_id(0) == last)
        def _():
            _exchange_wait(b_refs, r_refs, sems)

    def heads(nh):
        return pl.BlockSpec((nh, tm, HEAD_DIM), lambda i: (0, i, 0))

    def feat(nh, rows):
        return pl.BlockSpec((nh, rows, tm), lambda i: (0, 0, i))

    wide = pl.BlockSpec((tm, 512), lambda i: (i, 0))
    any_spec = pl.BlockSpec(memory_space=pl.ANY)
    out = pl.pallas_call(
        body,
        name="proj_fwd_gather",
        grid=(s_len // tm,),
        in_specs=[pl.BlockSpec((tm, D_MODEL), lambda i: (i, 0)), pl.BlockSpec((A_W, D_MODEL), lambda i: (0, 0))]
                 + [any_spec] * n,
        out_specs=[feat(8, HEAD_DIM), feat(8, HEAD_DIM), heads(8), wide, feat(8, HEAD_DIM), heads(2), heads(2), wide,
                   pl.BlockSpec((FOX_HEADS, tm), lambda i: (0, i)),
                   feat(FOX_HEADS, 2 * HEAD_DIM), feat(2, HEAD_DIM), feat(2, 2 * HEAD_DIM)] + [any_spec] * n,
        out_shape=[_sds((8, HEAD_DIM, s_len), bf16)] * 2 + [_sds((8, s_len, HEAD_DIM), bf16)]
                  + [_sds((s_len, 512), f32), _sds((8, HEAD_DIM, s_len), bf16),
                     _sds((2, s_len, HEAD_DIM), bf16), _sds((2, s_len, HEAD_DIM), bf16), _sds((s_len, 512), f32),
                     _sds((FOX_HEADS, s_len), f32), _sds((FOX_HEADS, 2 * HEAD_DIM, s_len), bf16),
                     _sds((2, HEAD_DIM, s_len), bf16), _sds((2, 2 * HEAD_DIM, s_len), bf16)]
                  + [_sds(b.shape, b.dtype) for b in bs],
        scratch_shapes=[pltpu.SemaphoreType.DMA((7 * n,)), pltpu.SemaphoreType.DMA((7 * n,)),
                        pltpu.SemaphoreType.DMA((n,))],
        compiler_params=_params(("arbitrary",)),
    )(x2, w_t, *bs)
    return out[:12], out[12:]


AUG = 2 * HEAD_DIM
NEAR_KEYS = 3


def _augment_call(q_t, k_t, cum_row, tm):
    nh, _, s_len = k_t.shape
    per_step = tm // FOX_T

    def body(qt_ref, kt_ref, c_ref, qat_ref, ka_ref, kat_ref, st_ref):
        c = c_ref[0]
        hi = c.astype(bf16).astype(f32)
        r1 = c - hi
        mid = r1.astype(bf16).astype(f32)
        lo = (r1 - mid).astype(bf16).astype(f32)
        row = lax.broadcasted_iota(jnp.int32, (HEAD_DIM, tm), 0)
        q_tail = jnp.where(row == 0, hi, jnp.where(row == 1, mid, jnp.where(row == 2, lo,
                           jnp.where(row < 6, 1.0, 0.0))))
        k_tail = jnp.where(row < 3, 1.0, jnp.where(row == 3, -hi, jnp.where(row == 4, -mid,
                           jnp.where(row == 5, -lo, 0.0))))
        qat_ref[0, 0:HEAD_DIM, :] = qt_ref[0]
        qat_ref[0, HEAD_DIM:AUG, :] = q_tail.astype(bf16)
        kat_ref[0, 0:HEAD_DIM, :] = kt_ref[0]
        kat_ref[0, HEAD_DIM:AUG, :] = k_tail.astype(bf16)
        qt = qt_ref[0].astype(f32)
        kt = kt_ref[0].astype(f32)
        ka_ref[0] = jnp.concatenate([kt, k_tail], axis=0).T.astype(bf16)
        qn2 = jnp.sum(qt * qt, axis=0, keepdims=True)
        kn2 = jnp.sum(kt * kt, axis=0, keepdims=True)
        sd = jnp.sum(qt * kt, axis=0, keepdims=True)
        k_and_c = jnp.concatenate([kt, jnp.broadcast_to(c, (8, tm))], axis=0)
        lane = lax.broadcasted_iota(jnp.int32, (1, tm), 1)
        for shift in range(1, NEAR_KEYS + 1):
            prev = pltpu.roll(k_and_c, shift, axis=1)
            near = jnp.sum(qt * prev[0:HEAD_DIM], axis=0, keepdims=True) + (c - prev[HEAD_DIM:HEAD_DIM + 1])
            sd = jnp.maximum(sd, jnp.where(lane >= shift, near, NEG_INF))
        srow = lax.broadcasted_iota(jnp.int32, (8, LANES), 0)
        for part in range(per_step):
            sl = slice(part * FOX_T, (part + 1) * FOX_T)
            vals = [jnp.sqrt(jnp.max(qn2[:, sl], axis=1, keepdims=True)),
                    jnp.sqrt(jnp.max(kn2[:, sl], axis=1, keepdims=True)),
                    jnp.min(sd[:, sl], axis=1, keepdims=True),
                    jnp.max(c[:, sl], axis=1, keepdims=True), jnp.min(c[:, sl], axis=1, keepdims=True)]
            out = jnp.zeros((8, LANES), f32)
            for r, val in enumerate(vals):
                out = jnp.where(srow == r, val, out)
            st_ref[0, part] = out

    tile_t = pl.BlockSpec((1, HEAD_DIM, tm), lambda h, i: (h, 0, i))
    return pl.pallas_call(
        body,
        name="fox_augment",
        grid=(nh, s_len // tm),
        in_specs=[tile_t, tile_t, pl.BlockSpec((1, 1, tm), lambda h, i: (h, 0, i))],
        out_specs=[pl.BlockSpec((1, AUG, tm), lambda h, i: (h, 0, i)),
                   pl.BlockSpec((1, tm, AUG), lambda h, i: (h, i, 0)),
                   pl.BlockSpec((1, AUG, tm), lambda h, i: (h, 0, i)),
                   pl.BlockSpec((1, per_step, 8, LANES), lambda h, i: (h, i, 0, 0))],
        out_shape=[_sds((nh, AUG, s_len), bf16), _sds((nh, s_len, AUG), bf16), _sds((nh, AUG, s_len), bf16),
                   _sds((nh, s_len // FOX_T, 8, LANES), f32)],
        compiler_params=_params(("arbitrary", "arbitrary")),
    )(q_t, k_t, cum_row)


FOX_PRUNE_GAP = 32.0


def _fox_prune_tables(stats):
    s = stats[:, :, :, 0]
    qn, kn, sd, cmx, cmn = (s[:, :, r] for r in range(5))
    nt = s.shape[1]
    bound = qn[:, :, None] * kn[:, None, :] + (cmx[:, :, None] - cmn[:, None, :])
    margin = 0.01 + 1e-5 * (jnp.abs(cmx)[:, :, None] + jnp.abs(cmn)[:, None, :])
    qi = lax.broadcasted_iota(jnp.int32, (nt, nt), 0)
    kj = lax.broadcasted_iota(jnp.int32, (nt, nt), 1)
    skip = (bound + margin < sd[:, :, None] - FOX_PRUNE_GAP) & (kj < qi)[None]
    first = jnp.sum(jnp.cumprod(skip.astype(jnp.int32), axis=2), axis=2)
    tiles = lax.broadcasted_iota(jnp.int32, (1, nt), 1)
    cnt = tiles - first
    ends = jnp.cumsum(cnt, axis=1)
    off = ends - cnt
    kmax = nt * (nt - 1) // 2
    k = lax.broadcasted_iota(jnp.int32, (1, kmax), 1)
    pair_q = jnp.minimum(jnp.sum((ends[:, None, :] <= k[:, :, None]).astype(jnp.int32), axis=2), nt - 1)
    hit = pair_q[:, :, None] == tiles[:, None, :]
    first_k = jnp.sum(jnp.where(hit, first[:, None, :], 0), axis=2)
    off_k = jnp.sum(jnp.where(hit, off[:, None, :], 0), axis=2)
    pair_k = jnp.clip(first_k + k - off_k, 0, nt - 1)
    return (ends[:, nt - 1].astype(jnp.int32), pair_q.reshape(-1).astype(jnp.int32),
            pair_k.reshape(-1).astype(jnp.int32))


CUM_CHUNK = 512


def _cum_call(fft, bf_col):
    s_len = fft.shape[1]
    ch = CUM_CHUNK

    def body(f_ref, b_ref, cum_ref, sg_ref):
        r = lax.broadcasted_iota(jnp.int32, (ch, ch), 0)
        c = lax.broadcasted_iota(jnp.int32, (ch, ch), 1)
        upper = (r <= c).astype(f32)
        carry = jnp.zeros((FOX_HEADS, 1), f32)
        for n in range(s_len // ch):
            z = f_ref[:, n * ch:(n + 1) * ch] + b_ref[...]
            logf = jnp.minimum(z, 0.0) - jnp.log1p(jnp.exp(-jnp.abs(z)))
            sg_ref[:, n * ch:(n + 1) * ch] = 1.0 / (1.0 + jnp.exp(z))
            cs = jnp.dot(logf, upper, precision=HIGHEST, preferred_element_type=f32) + carry
            cum_ref[:, n * ch:(n + 1) * ch] = cs
            carry = cs[:, ch - 1:ch]

    return pl.pallas_call(
        body,
        name="fox_cum_fwd",
        out_shape=[_sds((FOX_HEADS, s_len), f32)] * 2,
        compiler_params=_params(),
    )(fft, bf_col)


def _cum_bwd_call(dcq, dck, sg):
    s_len = sg.shape[1]
    ch = CUM_CHUNK
    nch = s_len // ch

    def body(q_ref, k_ref, sg_ref, dff_ref, dbf_ref):
        r = lax.broadcasted_iota(jnp.int32, (ch, ch), 0)
        c = lax.broadcasted_iota(jnp.int32, (ch, ch), 1)
        lower = (r >= c).astype(f32)
        dff_ref[...] = jnp.zeros_like(dff_ref)
        carry = jnp.zeros((FOX_HEADS, 1), f32)
        total = jnp.zeros((FOX_HEADS, 1), f32)
        for n in reversed(range(nch)):
            sl = slice(n * ch, (n + 1) * ch)
            dcum = q_ref[:, sl] - k_ref[:, sl]
            rs = jnp.dot(dcum, lower, precision=HIGHEST, preferred_element_type=f32) + carry
            carry = rs[:, 0:1]
            dff = rs * sg_ref[:, sl]
            dff_ref[0:FOX_HEADS, sl] = dff
            total = total + jnp.sum(dff, axis=1, keepdims=True)
        dbf_ref[...] = jnp.broadcast_to(total, (FOX_HEADS, 128))

    return pl.pallas_call(
        body,
        name="fox_cum_bwd",
        out_shape=[_sds((128, s_len), f32), _sds((FOX_HEADS, 128), f32)],
        compiler_params=_params(),
    )(dcq, dck, sg)


FOX_T = 512
ACC_ROWS = HEAD_DIM + 16
LANES = 128


def _causal_keep(t):
    return lax.broadcasted_iota(jnp.int32, (t, t), 0) <= lax.broadcasted_iota(jnp.int32, (t, t), 1)


def _tile_cols(i, t):
    return pl.ds(pl.multiple_of(i * t, t), t)


def _fox_pair(n, nt, kmax, h, pq_ref, pk_ref):
    k = h * kmax + jnp.maximum(n - nt, 0)
    return jnp.where(n < nt, n, pq_ref[k]), jnp.where(n < nt, n, pk_ref[k])


def _fox_fwd_call(qat, ka, vat, npairs, pair_q, pair_k):
    nh, s_len, _ = ka.shape
    t = FOX_T
    nt = s_len // t
    kmax = nt * (nt - 1) // 2
    assert nt >= 2 and nt % 2 == 0

    def body(np_ref, pq_ref, pk_ref, qat_ref, ka_ref, vat_ref, o_ref, lse_ref, s0, s1, p0, p1, a0, a1, m_all, acc_all):
        h = pl.program_id(0)
        extra = np_ref[h]
        total = nt + extra
        m_all[...] = jnp.full(m_all.shape, NEG_INF, f32)
        acc_all[...] = jnp.zeros(acc_all.shape, f32)
        bufs = ((s0, p0, a0), (s1, p1, a1))

        def pair(n):
            return _fox_pair(n, nt, kmax, h, pq_ref, pk_ref)

        def scores(n, b, masked):
            i, j = pair(n)
            st = jnp.dot(ka_ref[0, _tile_cols(j, t), :], qat_ref[0, :, _tile_cols(i, t)], preferred_element_type=f32)
            if masked:
                st = jnp.where(_causal_keep(t), st, NEG_INF)
            bufs[b][0][...] = st

        def softmax(n, b):
            i, _ = pair(n)
            s_ref, p_ref, a_ref = bufs[b]
            for c in range(t // LANES):
                cols = slice(c * LANES, (c + 1) * LANES)
                mcols = pl.ds(pl.multiple_of(i * t + c * LANES, LANES), LANES)
                m_old = m_all[:, mcols]
                m_new = jnp.maximum(m_old, jnp.max(s_ref[:, cols], axis=0, keepdims=True))
                m_all[:, mcols] = m_new
                a_ref[:, cols] = jnp.exp(m_old - m_new)
                p_ref[:, cols] = jnp.exp(s_ref[:, cols] - m_new).astype(bf16)

        def accum(n, b):
            i, j = pair(n)
            cols = _tile_cols(i, t)
            acc_all[:, cols] = bufs[b][2][...] * acc_all[:, cols] + jnp.dot(
                vat_ref[0, 0:ACC_ROWS, _tile_cols(j, t)], bufs[b][1][...], preferred_element_type=f32)

        def step(n, b, masked):
            accum(n - 2, b)
            softmax(n - 1, 1 - b)
            scores(n, b, masked)

        scores(0, 0, True)
        scores(1, 1, True)
        softmax(0, 0)

        def diag_steps(d, _):
            n = 2 + 2 * d
            step(n, 0, True)
            step(n + 1, 1, True)
            return 0

        lax.fori_loop(0, (nt - 2) // 2, diag_steps, 0)

        def off_steps(d, _):
            n = nt + 2 * d
            step(n, 0, False)
            step(n + 1, 1, False)
            return 0

        lax.fori_loop(0, extra // 2, off_steps, 0)

        @pl.when(extra % 2 == 1)
        def _():
            step(total - 1, 0, False)
            softmax(total - 1, 0)
            accum(total - 2, 1)
            accum(total - 1, 0)

        @pl.when(extra % 2 == 0)
        def _():
            softmax(total - 1, 1)
            accum(total - 2, 0)
            accum(total - 1, 1)

        l = acc_all[HEAD_DIM:HEAD_DIM + 1, :]
        o_ref[0] = acc_all[0:HEAD_DIM, :] / l
        lse_ref[0] = m_all[...] + jnp.log(l)

    smem = pl.BlockSpec(memory_space=pltpu.SMEM)
    return pl.pallas_call(
        body,
        name="fox_fwd",
        grid=(nh,),
        in_specs=[smem, smem, smem,
                  pl.BlockSpec((1, AUG, s_len), lambda h: (h, 0, 0)),
                  pl.BlockSpec((1, s_len, AUG), lambda h: (h, 0, 0)),
                  pl.BlockSpec((1, AUG, s_len), lambda h: (h, 0, 0))],
        out_specs=[pl.BlockSpec((1, HEAD_DIM, s_len), lambda h: (h, 0, 0)),
                   pl.BlockSpec((1, 1, s_len), lambda h: (h, 0, 0))],
        out_shape=[_sds((nh, HEAD_DIM, s_len), f32), _sds((nh, 1, s_len), f32)],
        scratch_shapes=[pltpu.VMEM((t, t), f32), pltpu.VMEM((t, t), f32), pltpu.VMEM((t, t), bf16),
                        pltpu.VMEM((t, t), bf16), pltpu.VMEM((1, t), f32), pltpu.VMEM((1, t), f32),
                        pltpu.VMEM((1, s_len), f32), pltpu.VMEM((ACC_ROWS, s_len), f32)],
        compiler_params=_params(("arbitrary",)),
    )(npairs, pair_q, pair_k, qat, ka, vat)


SWA_TS = 512


SWA_W = SWA_GROUP * BLOCK


def _swa_bias_call(rel_bias, bucket_t):
    def body(rb_ref, bk_ref, b_ref, b0_ref):
        bk = bk_ref[...]
        row = lax.broadcasted_iota(jnp.int32, (2 * BLOCK, BLOCK), 0)
        for h in range(SWA_HEADS):
            acc = jnp.full((2 * BLOCK, BLOCK), NEG_INF, f32)
            for b in range(NUM_BUCKETS):
                acc = jnp.where(bk == b, rb_ref[b, h], acc)
            g, hh = divmod(h, SWA_GROUP)
            b_ref[g, :, hh * BLOCK:(hh + 1) * BLOCK] = acc
            b0_ref[g, :, hh * BLOCK:(hh + 1) * BLOCK] = jnp.where(row < BLOCK, NEG_INF, acc)

    return pl.pallas_call(
        body,
        name="swa_bias",
        in_specs=[pl.BlockSpec(memory_space=pltpu.SMEM), pl.BlockSpec(memory_space=pltpu.VMEM)],
        out_shape=[_sds((SWA_KV_HEADS, 2 * BLOCK, SWA_W), f32)] * 2,
        compiler_params=_params(),
    )(rel_bias, bucket_t)


def _swa_bias_bwd_call(dbias, bucket_t):
    def body(d_ref, bk_ref, o_ref):
        bk = bk_ref[...]
        row = lax.broadcasted_iota(jnp.int32, (NUM_BUCKETS, 128), 0)
        col = lax.broadcasted_iota(jnp.int32, (NUM_BUCKETS, 128), 1)
        out = jnp.zeros((NUM_BUCKETS, 128), f32)
        for h in range(SWA_HEADS):
            g, hh = divmod(h, SWA_GROUP)
            d = d_ref[g, :, hh * BLOCK:(hh + 1) * BLOCK]
            for b in range(NUM_BUCKETS):
                val = jnp.sum(jnp.sum(jnp.where(bk == b, d, 0.0), axis=1, keepdims=True), axis=0, keepdims=True)
                out = jnp.where((row == b) & (col == h), val, out)
        o_ref[...] = out

    return pl.pallas_call(
        body,
        name="swa_bias_bwd",
        out_shape=_sds((NUM_BUCKETS, 128), f32),
        compiler_params=_params(),
    )(dbias, bucket_t)


def _sink_row(sink_ref, g):
    return jnp.concatenate([jnp.full((1, BLOCK), sink_ref[g * SWA_GROUP + hh], f32) for hh in range(SWA_GROUP)], axis=1)


def _group_lanes(ref, g, cols):
    return jnp.concatenate([ref[g * SWA_GROUP + hh, :, cols] for hh in range(SWA_GROUP)], axis=1)


def _swa_fwd_call(qt, k, vta, bias_t, bias0_t, sink):
    s_len = qt.shape[2]
    ts = SWA_TS
    nb = ts // BLOCK

    def body(qt_ref, kc_ref, kp_ref, vc_ref, vp_ref, b_ref, b0_ref, sink_ref, o_ref, lse_ref):
        first = pl.program_id(0) == 0
        kall = [jnp.concatenate([kp_ref[g], kc_ref[g]], axis=0) for g in range(SWA_KV_HEADS)]
        vall = [jnp.concatenate([vp_ref[g], vc_ref[g]], axis=1) for g in range(SWA_KV_HEADS)]
        sinks = [_sink_row(sink_ref, g) for g in range(SWA_KV_HEADS)]
        items = [(g, b) for g in range(SWA_KV_HEADS) for b in range(nb)]

        def scores(g, b):
            qg = _group_lanes(qt_ref, g, slice(b * BLOCK, (b + 1) * BLOCK))
            bias_b = b_ref[g]
            if b == 0:
                bias_b = jnp.where(first, b0_ref[g], bias_b)
            return jnp.dot(kall[g][b * BLOCK:(b + 2) * BLOCK], qg, preferred_element_type=f32) + bias_b

        def finish(g, b, st):
            m = jnp.maximum(jnp.max(st, axis=0, keepdims=True), sinks[g])
            pt = jnp.exp(st - m)
            acc = jnp.dot(vall[g][:, b * BLOCK:(b + 2) * BLOCK], pt.astype(bf16), preferred_element_type=f32)
            l = acc[HEAD_DIM:HEAD_DIM + 1, :] + jnp.exp(sinks[g] - m)
            return acc[0:HEAD_DIM, :] / l, m + jnp.log(l)

        outs, lses = {}, {}
        st_next = scores(*items[0])
        for idx, (g, b) in enumerate(items):
            st = st_next
            if idx + 1 < len(items):
                st_next = scores(*items[idx + 1])
            outs[g, b], lses[g, b] = finish(g, b, st)
        for g in range(SWA_KV_HEADS):
            for hh in range(SWA_GROUP):
                lanes = slice(hh * BLOCK, (hh + 1) * BLOCK)
                o_ref[g * SWA_GROUP + hh] = jnp.concatenate([outs[g, b][:, lanes] for b in range(nb)], axis=1)
                lse_ref[g * SWA_GROUP + hh] = jnp.concatenate([lses[g, b][:, lanes] for b in range(nb)], axis=1)

    def prev_blk(n):
        return jnp.maximum(n * nb - 1, 0)

    bspec = pl.BlockSpec((SWA_KV_HEADS, 2 * BLOCK, SWA_W), lambda n: (0, 0, 0))
    return pl.pallas_call(
        body,
        name="swa_fwd",
        grid=(s_len // ts,),
        in_specs=[pl.BlockSpec((SWA_HEADS, HEAD_DIM, ts), lambda n: (0, 0, n)),
                  pl.BlockSpec((SWA_KV_HEADS, ts, HEAD_DIM), lambda n: (0, n, 0)),
                  pl.BlockSpec((SWA_KV_HEADS, BLOCK, HEAD_DIM), lambda n: (0, prev_blk(n), 0)),
                  pl.BlockSpec((SWA_KV_HEADS, AUG, ts), lambda n: (0, 0, n)),
                  pl.BlockSpec((SWA_KV_HEADS, AUG, BLOCK), lambda n: (0, 0, prev_blk(n))),
                  bspec, bspec, pl.BlockSpec(memory_space=pltpu.SMEM)],
        out_specs=[pl.BlockSpec((SWA_HEADS, HEAD_DIM, ts), lambda n: (0, 0, n)),
                   pl.BlockSpec((SWA_HEADS, 1, ts), lambda n: (0, 0, n))],
        out_shape=[_sds((SWA_HEADS, HEAD_DIM, s_len), f32), _sds((SWA_HEADS, 1, s_len), f32)],
        compiler_params=_params(("arbitrary",)),
    )(qt, k, k, vta, vta, bias_t, bias0_t, sink)


def _head_selector():
    sel = np.zeros((512, 128), np.float32)
    for h in range(8):
        sel[h * HEAD_DIM:(h + 1) * HEAD_DIM, h] = 1.0
    return sel


def _post_call(of, fz, osw, sz, x2, tgt, wo, ln_g, ln_b, sel, tm):
    s_len = x2.shape[0]

    def body(of_ref, fz_ref, os_ref, sz_ref, x_ref, t_ref, wo_ref, g_ref, b_ref, sel_ref,
             dh_ref, dof_ref, dfz_ref, dos_ref, dsz_ref, dlf_ref, dls_ref, dwo_ref, dg_ref, db_ref, loss_ref):
        n = pl.program_id(0)

        @pl.when(n == 0)
        def _():
            dwo_ref[...] = jnp.zeros_like(dwo_ref)
            dg_ref[...] = jnp.zeros_like(dg_ref)
            db_ref[...] = jnp.zeros_like(db_ref)
            loss_ref[...] = jnp.zeros_like(loss_ref)

        gam = g_ref[...]
        sel_m = sel_ref[...]

        def forward(r):
            o_f = of_ref[:, r].T
            o_s = os_ref[:, r].T
            fz = fz_ref[r, :]
            sz = sz_ref[r, :]
            sg_f = jax.nn.sigmoid(fz)
            sg_s = jax.nn.sigmoid(sz)
            silu_f = fz * sg_f
            silu_s = sz * sg_s
            mixed = jnp.concatenate([o_f * silu_f, o_s * silu_s], axis=1).astype(bf16)
            y = jnp.dot(mixed, wo_ref[...], preferred_element_type=f32)
            return o_f, o_s, fz, sz, sg_f, sg_s, silu_f, silu_s, mixed, y

        def norm_and_back(r, fwd):
            mixed, y = fwd[8], fwd[9]
            h = ALPHA * x_ref[r, :] + y
            mu = jnp.mean(h, axis=1, keepdims=True)
            hc = h - mu
            var = jnp.mean(hc * hc, axis=1, keepdims=True)
            rstd = lax.rsqrt(var + LN_EPS)
            xhat = hc * rstd
            out = xhat * gam + b_ref[...]
            err = out - t_ref[r, :]
            tok_loss = jnp.mean(err * err, axis=1, keepdims=True)
            loss_ref[...] += 0.5 * jnp.sum(tok_loss, axis=0, keepdims=True)
            dout = err * (1.0 / D_MODEL)
            dg_ref[...] += jnp.sum(dout * xhat, axis=0, keepdims=True)
            db_ref[...] += jnp.sum(dout, axis=0, keepdims=True)
            dxh = dout * gam
            m1 = jnp.mean(dxh, axis=1, keepdims=True)
            m2 = jnp.mean(dxh * xhat, axis=1, keepdims=True)
            dh = rstd * (dxh - m1 - xhat * m2)
            dh_ref[r, :] = dh
            dyb = dh.astype(bf16)
            dmix = lax.dot_general(dyb, wo_ref[...], NT, preferred_element_type=f32)
            dwo_ref[...] += lax.dot_general(mixed, dyb, TN, preferred_element_type=f32)
            return dmix

        def head_sums(prod):
            hi = prod.astype(bf16)
            lo = (prod - hi.astype(f32)).astype(bf16)
            return (jnp.dot(hi, sel_m, preferred_element_type=f32) + jnp.dot(lo, sel_m, preferred_element_type=f32))

        def gates_back(r, fwd, dmix):
            o_f, o_s, fz, sz, sg_f, sg_s, silu_f, silu_s = fwd[:8]
            dm_f = dmix[:, :512]
            dm_s = dmix[:, 512:]
            do_f = dm_f * silu_f
            do_s = dm_s * silu_s
            dfz_ref[r, :] = (dm_f * o_f * (sg_f * (1.0 + fz * (1.0 - sg_f)))).astype(bf16)
            dsz_ref[r, :] = (dm_s * o_s * (sg_s * (1.0 + sz * (1.0 - sg_s)))).astype(bf16)
            dof_ref[:, r] = do_f.T.astype(bf16)
            dos_ref[:, r] = do_s.T.astype(bf16)
            dlf_ref[:, r] = head_sums(do_f * o_f).T[:FOX_HEADS, :]
            dls_ref[:, r] = head_sums(do_s * o_s).T[:SWA_HEADS, :]

        halves = [slice(k * (tm // 2), (k + 1) * (tm // 2)) for k in range(2)]
        fwds = [forward(r) for r in halves]
        dmixes = [norm_and_back(r, f) for r, f in zip(halves, fwds)]
        for r, f, d in zip(halves, fwds, dmixes):
            gates_back(r, f, d)

    feat = pl.BlockSpec((512, tm), lambda n: (0, n))
    rows8 = pl.BlockSpec((8, tm), lambda n: (0, n))
    half = pl.BlockSpec((tm, 512), lambda n: (n, 0))
    fullw = pl.BlockSpec((tm, D_MODEL), lambda n: (n, 0))
    vec = pl.BlockSpec((1, D_MODEL), lambda n: (0, 0))
    return pl.pallas_call(
        body,
        name="post_fwd_bwd",
        grid=(s_len // tm,),
        in_specs=[feat, half, feat, half, fullw, fullw,
                  pl.BlockSpec((D_MODEL, D_MODEL), lambda n: (0, 0)), vec, vec,
                  pl.BlockSpec((512, 128), lambda n: (0, 0))],
        out_specs=[fullw, feat, half, feat, half, rows8, rows8,
                   pl.BlockSpec((D_MODEL, D_MODEL), lambda n: (0, 0)), vec, vec,
                   pl.BlockSpec((1, 1), lambda n: (0, 0))],
        out_shape=[_sds((s_len, D_MODEL), f32), _sds((512, s_len), bf16), _sds((s_len, 512), bf16),
                   _sds((512, s_len), bf16), _sds((s_len, 512), bf16),
                   _sds((FOX_HEADS, s_len), f32), _sds((SWA_HEADS, s_len), f32),
                   _sds((D_MODEL, D_MODEL), f32), _sds((1, D_MODEL), f32), _sds((1, D_MODEL), f32),
                   _sds((1, 1), f32)],
        compiler_params=_params(("arbitrary",), VMEM_LIMIT_BIG),
    )(of, fz, osw, sz, x2, tgt, wo, ln_g, ln_b, sel)


def _fox_bwd_call(ka, kat, v, qat, dot, lse_row, dl_row, npairs, pair_q, pair_k):
    nh, s_len, _ = ka.shape
    t = FOX_T
    nt = s_len // t
    kmax = nt * (nt - 1) // 2
    assert nt >= 2 and nt % 2 == 0
    ck_slot = HEAD_DIM + 3
    cq_slot = HEAD_DIM

    def body(np_ref, pq_ref, pk_ref, ka_ref, kat_ref, v_ref, qat_ref, dot_ref, lse_ref, dl_ref,
             dq_ref, dk_ref, dv_ref, dcq_ref, dck_ref, dqt_all, dkat_all, dvt_all, p0, p1, ds0, ds1):
        h = pl.program_id(0)
        extra = np_ref[h]
        total = nt + extra
        dqt_all[...] = jnp.zeros(dqt_all.shape, f32)
        dkat_all[...] = jnp.zeros(dkat_all.shape, f32)
        dvt_all[...] = jnp.zeros(dvt_all.shape, f32)
        pbuf, dsbuf = (p0, p1), (ds0, ds1)

        def pair(n):
            return _fox_pair(n, nt, kmax, h, pq_ref, pk_ref)

        def probs(n, b, masked):
            i, j = pair(n)
            qc, kr = _tile_cols(i, t), _tile_cols(j, t)
            st = jnp.dot(ka_ref[0, kr, :], qat_ref[0, :, qc], preferred_element_type=f32)
            dpt = jnp.dot(v_ref[0, kr, :], dot_ref[0, :, qc], preferred_element_type=f32)
            if masked:
                st = jnp.where(_causal_keep(t), st, NEG_INF)
            pt = jnp.exp(st - lse_ref[0, :, qc])
            pbuf[b][...] = pt.astype(bf16)
            dsbuf[b][...] = (pt * (dpt - dl_ref[0, :, qc])).astype(bf16)

        def grads(n, b):
            i, j = pair(n)
            qc, kc = _tile_cols(i, t), _tile_cols(j, t)
            dvt_all[:, kc] += lax.dot_general(dot_ref[0, :, qc], pbuf[b][...], NT, preferred_element_type=f32)
            dkat_all[:, kc] += lax.dot_general(qat_ref[0, :, qc], dsbuf[b][...], NT, preferred_element_type=f32)
            dqt_all[:, qc] += jnp.dot(kat_ref[0, :, kc], dsbuf[b][...], preferred_element_type=f32)

        def step(n, b, masked):
            i, j = pair(n)
            qc, kr = _tile_cols(i, t), _tile_cols(j, t)
            i1, j1 = pair(n - 1)
            qc1, kc1 = _tile_cols(i1, t), _tile_cols(j1, t)
            c = 1 - b
            st = jnp.dot(ka_ref[0, kr, :], qat_ref[0, :, qc], preferred_element_type=f32)
            dvt_all[:, kc1] += lax.dot_general(dot_ref[0, :, qc1], pbuf[c][...], NT, preferred_element_type=f32)
            if masked:
                st = jnp.where(_causal_keep(t), st, NEG_INF)
            pt = jnp.exp(st - lse_ref[0, :, qc])
            pbuf[b][...] = pt.astype(bf16)
            dpt = jnp.dot(v_ref[0, kr, :], dot_ref[0, :, qc], preferred_element_type=f32)
            dkat_all[:, kc1] += lax.dot_general(qat_ref[0, :, qc1], dsbuf[c][...], NT, preferred_element_type=f32)
            dqt_all[:, qc1] += jnp.dot(kat_ref[0, :, kc1], dsbuf[c][...], preferred_element_type=f32)
            dsbuf[b][...] = (pt * (dpt - dl_ref[0, :, qc])).astype(bf16)

        probs(0, 0, True)
        step(1, 1, True)

        def four_steps(n, masked):
            step(n, 0, masked)
            step(n + 1, 1, masked)
            step(n + 2, 0, masked)
            step(n + 3, 1, masked)

        def diag_quads(d, _):
            four_steps(2 + 4 * d, True)
            return 0

        lax.fori_loop(0, (nt - 2) // 4, diag_quads, 0)
        if (nt - 2) % 4:
            step(nt - 2, 0, True)
            step(nt - 1, 1, True)

        def off_quads(d, _):
            four_steps(nt + 4 * d, False)
            return 0

        quads = extra // 4
        lax.fori_loop(0, quads, off_quads, 0)

        def off_steps(d, _):
            n = nt + 4 * quads + 2 * d
            step(n, 0, False)
            step(n + 1, 1, False)
            return 0

        lax.fori_loop(0, (extra % 4) // 2, off_steps, 0)

        @pl.when(extra % 2 == 1)
        def _():
            step(total - 1, 0, False)
            grads(total - 1, 0)

        @pl.when(extra % 2 == 0)
        def _():
            grads(total - 1, 1)

        dq_ref[0] = (dqt_all[0:HEAD_DIM, :] * SCALE).astype(bf16)
        dk_ref[0] = dkat_all[0:HEAD_DIM, :].astype(bf16)
        dv_ref[0] = dvt_all[...].astype(bf16)
        dcq_ref[0] = dqt_all[cq_slot:cq_slot + 1, :]
        dck_ref[0] = dkat_all[ck_slot:ck_slot + 1, :]

    smem = pl.BlockSpec(memory_space=pltpu.SMEM)
    rows = pl.BlockSpec((1, s_len, AUG), lambda h: (h, 0, 0))
    feat = pl.BlockSpec((1, AUG, s_len), lambda h: (h, 0, 0))
    feat64 = pl.BlockSpec((1, HEAD_DIM, s_len), lambda h: (h, 0, 0))
    rowv = pl.BlockSpec((1, 1, s_len), lambda h: (h, 0, 0))
    return pl.pallas_call(
        body,
        name="fox_bwd",
        grid=(nh,),
        in_specs=[smem, smem, smem, rows, feat, pl.BlockSpec((1, s_len, HEAD_DIM), lambda h: (h, 0, 0)), feat, feat64,
                  rowv, rowv],
        out_specs=[feat64, feat64, feat64, rowv, rowv],
        out_shape=[_sds((nh, HEAD_DIM, s_len), bf16)] * 3 + [_sds((nh, 1, s_len), f32)] * 2,
        scratch_shapes=[pltpu.VMEM((AUG, s_len), f32), pltpu.VMEM((AUG, s_len), f32), pltpu.VMEM((HEAD_DIM, s_len), f32)]
                       + [pltpu.VMEM((t, t), bf16)] * 4,
        compiler_params=_params(("arbitrary",)),
    )(npairs, pair_q, pair_k, ka, kat, v, qat, dot, lse_row, dl_row)


def _swa_bwd_call(qt, k, kt, v, dot, lse, dl, bias_t, bias0_t, sink):
    s_len = qt.shape[2]
    ts = SWA_TS
    nb = ts // BLOCK
    nsteps = s_len // ts

    def body(qt_ref, kc_ref, kp_ref, ktc_ref, ktp_ref, vc_ref, vp_ref, dot_ref, lse_ref, dl_ref, b_ref, b0_ref,
             sink_ref, dq_ref, dk_ref, dv_ref, dbias_ref, dsink_ref, dk_s, dv_s, tail_k, tail_v, sk_s):
        n = pl.program_id(0)

        @pl.when(n == 0)
        def _():
            dbias_ref[...] = jnp.zeros_like(dbias_ref)
            sk_s[...] = jnp.zeros_like(sk_s)

        @pl.when(n < nsteps)
        def _():
            first = n == 0
            dk_s[...] = jnp.zeros_like(dk_s)
            dv_s[...] = jnp.zeros_like(dv_s)
            groups = range(SWA_KV_HEADS)
            kall = [jnp.concatenate([kp_ref[g], kc_ref[g]], axis=0) for g in groups]
            vall = [jnp.concatenate([vp_ref[g], vc_ref[g]], axis=0) for g in groups]
            ktall = [jnp.concatenate([ktp_ref[g], ktc_ref[g]], axis=1) for g in groups]
            sinks = [_sink_row(sink_ref, g) for g in groups]
            items = [(g, b) for g in groups for b in range(nb)]

            def products(g, b):
                cols = slice(b * BLOCK, (b + 1) * BLOCK)
                win = slice(b * BLOCK, (b + 2) * BLOCK)
                qg = _group_lanes(qt_ref, g, cols)
                dog = _group_lanes(dot_ref, g, cols)
                bias_b = b_ref[g]
                if b == 0:
                    bias_b = jnp.where(first, b0_ref[g], bias_b)
                st = jnp.dot(kall[g][win], qg, preferred_element_type=f32) + bias_b
                dpt = jnp.dot(vall[g][win], dog, preferred_element_type=f32)
                return qg, dog, st, dpt

            def finish(g, b, qg, dog, st, dpt):
                cols = slice(b * BLOCK, (b + 1) * BLOCK)
                win = slice(b * BLOCK, (b + 2) * BLOCK)
                lse_r = _group_lanes(lse_ref, g, cols)
                dl_r = _group_lanes(dl_ref, g, cols)
                pt = jnp.exp(st - lse_r)
                dst = pt * (dpt - dl_r)
                dsb = dst.astype(bf16)
                dk_s[g, :, win] += lax.dot_general(qg, dsb, NT, preferred_element_type=f32)
                dv_s[g, :, win] += lax.dot_general(dog, pt.astype(bf16), NT, preferred_element_type=f32)
                dqg = jnp.dot(ktall[g][:, win], dsb, preferred_element_type=f32) * SCALE
                return dqg, dst, -jnp.exp(sinks[g] - lse_r) * dl_r

            dqs, dsts, sks = {}, {}, {}
            nxt = products(*items[0])
            for idx, (g, b) in enumerate(items):
                cur = nxt
                if idx + 1 < len(items):
                    nxt = products(*items[idx + 1])
                dqs[g, b], dsts[g, b], sks[g, b] = finish(g, b, *cur)
            for g in groups:
                dbias_ref[g] += functools.reduce(lambda a, c: a + c, [dsts[g, b] for b in range(nb)])
                sk_s[g] += functools.reduce(lambda a, c: a + c, [sks[g, b] for b in range(nb)])
                for hh in range(SWA_GROUP):
                    lanes = slice(hh * BLOCK, (hh + 1) * BLOCK)
                    dq_ref[g * SWA_GROUP + hh] = jnp.concatenate(
                        [dqs[g, b][:, lanes] for b in range(nb)], axis=1).astype(bf16)

        @pl.when(n > 0)
        def _():
            last = slice(ts - BLOCK, ts)
            for g in range(SWA_KV_HEADS):
                add_k = jnp.where(n < nsteps, dk_s[g, :, 0:BLOCK], 0.0)
                add_v = jnp.where(n < nsteps, dv_s[g, :, 0:BLOCK], 0.0)
                dk_ref[g, :, 0:ts - BLOCK] = tail_k[g, :, 0:ts - BLOCK].astype(bf16)
                dv_ref[g, :, 0:ts - BLOCK] = tail_v[g, :, 0:ts - BLOCK].astype(bf16)
                dk_ref[g, :, last] = (tail_k[g, :, last] + add_k).astype(bf16)
                dv_ref[g, :, last] = (tail_v[g, :, last] + add_v).astype(bf16)

        @pl.when(n < nsteps)
        def _():
            tail_k[...] = dk_s[:, :, BLOCK:]
            tail_v[...] = dv_s[:, :, BLOCK:]

        @pl.when(n == nsteps)
        def _():
            row = lax.broadcasted_iota(jnp.int32, (SWA_HEADS, 128), 0)
            out = jnp.zeros((SWA_HEADS, 128), f32)
            for h in range(SWA_HEADS):
                g, hh = divmod(h, SWA_GROUP)
                val = jnp.sum(sk_s[g, :, hh * BLOCK:(hh + 1) * BLOCK], axis=1, keepdims=True)
                out = jnp.where(row == h, val, out)
            dsink_ref[...] = out

    last_step = nsteps - 1

    def cl(n):
        return jnp.minimum(n, last_step)

    def prev_blk(n):
        return jnp.maximum(cl(n) * nb - 1, 0)

    feat8 = pl.BlockSpec((SWA_HEADS, HEAD_DIM, ts), lambda n: (0, 0, cl(n)))
    rows8 = pl.BlockSpec((SWA_HEADS, 1, ts), lambda n: (0, 0, cl(n)))
    cur = pl.BlockSpec((SWA_KV_HEADS, ts, HEAD_DIM), lambda n: (0, cl(n), 0))
    prev = pl.BlockSpec((SWA_KV_HEADS, BLOCK, HEAD_DIM), lambda n: (0, prev_blk(n), 0))
    curt = pl.BlockSpec((SWA_KV_HEADS, HEAD_DIM, ts), lambda n: (0, 0, cl(n)))
    prevt = pl.BlockSpec((SWA_KV_HEADS, HEAD_DIM, BLOCK), lambda n: (0, 0, prev_blk(n)))
    bspec = pl.BlockSpec((SWA_KV_HEADS, 2 * BLOCK, SWA_W), lambda n: (0, 0, 0))
    kvout = pl.BlockSpec((SWA_KV_HEADS, HEAD_DIM, ts), lambda n: (0, 0, jnp.maximum(n - 1, 0)))
    return pl.pallas_call(
        body,
        name="swa_bwd",
        grid=(nsteps + 1,),
        in_specs=[feat8, cur, prev, curt, prevt, cur, prev, feat8, rows8, rows8, bspec, bspec,
                  pl.BlockSpec(memory_space=pltpu.SMEM)],
        out_specs=[feat8, kvout, kvout, bspec, pl.BlockSpec((SWA_HEADS, 128), lambda n: (0, 0))],
        out_shape=[_sds((SWA_HEADS, HEAD_DIM, s_len), bf16), _sds((SWA_KV_HEADS, HEAD_DIM, s_len), bf16),
                   _sds((SWA_KV_HEADS, HEAD_DIM, s_len), bf16),
                   _sds((SWA_KV_HEADS, 2 * BLOCK, SWA_W), f32), _sds((SWA_HEADS, 128), f32)],
        scratch_shapes=[pltpu.VMEM((SWA_KV_HEADS, HEAD_DIM, ts + BLOCK), f32),
                        pltpu.VMEM((SWA_KV_HEADS, HEAD_DIM, ts + BLOCK), f32),
                        pltpu.VMEM((SWA_KV_HEADS, HEAD_DIM, ts), f32),
                        pltpu.VMEM((SWA_KV_HEADS, HEAD_DIM, ts), f32),
                        pltpu.VMEM((SWA_KV_HEADS, 1, SWA_W), f32)],
        compiler_params=_params(("arbitrary",)),
    )(qt, k, k, kt, kt, v, v, dot, lse, dl, bias_t, bias0_t, sink)


def _dproj_specs(tm):
    half = pl.BlockSpec((tm, 512), lambda i: (i, 0))
    feat = pl.BlockSpec((512, tm), lambda i: (0, i))
    feat_kv = pl.BlockSpec((128, tm), lambda i: (0, i))
    return [feat, feat, feat, half, feat, feat_kv, feat_kv, half, feat_kv]


def _dx_exchange_call(dh, pieces, w_t, bs, tm):
    s_len = dh.shape[0]
    n = len(bs)
    last = s_len // tm - 1

    def body(*refs):
        dh_ref, dqf_ref, dkf_ref, dvf_ref, dfz_ref, dqs_ref, dks_ref, dvs_ref, dsz_ref, dfft_ref, w_ref = refs[:11]
        b_refs = refs[11:11 + n]
        dx_ref = refs[11 + n]
        r_refs = refs[12 + n:12 + 2 * n]
        sems = refs[12 + 2 * n:]
        i = pl.program_id(0)

        @pl.when(i == 0)
        def _():
            _exchange_start(b_refs, r_refs, sems)

        def tr(ref):
            return ref[...].astype(f32).T.astype(bf16)

        dp = jnp.concatenate([tr(dqf_ref), tr(dkf_ref), tr(dvf_ref), dfz_ref[...], tr(dqs_ref), tr(dks_ref),
                              tr(dvs_ref), dsz_ref[...], tr(dfft_ref)], axis=1)
        dx_ref[...] = ALPHA * dh_ref[...] + jnp.dot(dp, w_ref[...], preferred_element_type=f32)

        @pl.when(i == last)
        def _():
            _exchange_wait(b_refs, r_refs, sems)

    fullw = pl.BlockSpec((tm, D_MODEL), lambda i: (i, 0))
    any_spec = pl.BlockSpec(memory_space=pl.ANY)
    out = pl.pallas_call(
        body,
        name="dx_bwd_exchange",
        grid=(s_len // tm,),
        in_specs=[fullw] + _dproj_specs(tm) + [pl.BlockSpec((A_W, D_MODEL), lambda i: (0, 0))] + [any_spec] * n,
        out_specs=[fullw] + [any_spec] * n,
        out_shape=[_sds((s_len, D_MODEL), f32)] + [_sds(b.shape, b.dtype) for b in bs],
        scratch_shapes=[pltpu.SemaphoreType.DMA((7 * n,)), pltpu.SemaphoreType.DMA((7 * n,)),
                        pltpu.SemaphoreType.DMA((n,))],
        compiler_params=_params(("arbitrary",)),
    )(dh, *pieces, w_t, *bs)
    return out[0], out[1:]


DW_STAGE_ROWS = 384


def _dw_exchange_call(x2, pieces, bs, tm):
    s_len = x2.shape[0]
    nt = s_len // tm
    n = len(bs)

    def body(*refs):
        x_ref, dqf_ref, dkf_ref, dvf_ref, dfz_ref, dqs_ref, dks_ref, dvs_ref, dsz_ref, dfft_ref = refs[:10]
        b_refs = refs[10:10 + n]
        dw_ref = refs[10 + n]
        r_refs = refs[11 + n:11 + 2 * n]
        acc_ref, stage_ref, sem = refs[11 + 2 * n:14 + 2 * n]
        sems = refs[14 + 2 * n:]
        i = pl.program_id(0)

        @pl.when(i == 0)
        def _():
            _exchange_start(b_refs, r_refs, sems)
            acc_ref[...] = jnp.zeros_like(acc_ref)

        xb = x_ref[...].astype(bf16)

        def add_feat(off, lhs):
            acc_ref[off:off + lhs.shape[0], :] += jnp.dot(lhs, xb, preferred_element_type=f32)

        def add_rows(off, piece):
            acc_ref[off:off + piece.shape[1], :] += lax.dot_general(piece, xb, TN, preferred_element_type=f32)

        add_feat(A_FQ, dqf_ref[...])
        add_feat(A_FK, dkf_ref[...])
        add_feat(A_FV, dvf_ref[...])
        add_rows(A_FZ, dfz_ref[...])
        add_feat(A_SQ, dqs_ref[...])
        add_feat(A_SK, dks_ref[...])
        add_feat(A_SV, dvs_ref[...])
        add_rows(A_SZ, dsz_ref[...])
        add_feat(A_FF, dfft_ref[...].astype(bf16))

        @pl.when(i == nt - 1)
        def _():
            for r in range(A_W // DW_STAGE_ROWS):
                rows = slice(r * DW_STAGE_ROWS, (r + 1) * DW_STAGE_ROWS)
                stage_ref[...] = acc_ref[rows, :].astype(bf16)
                cp = pltpu.make_async_copy(stage_ref, dw_ref.at[rows, :], sem)
                cp.start()
                cp.wait()
            _exchange_wait(b_refs, r_refs, sems)

    any_spec = pl.BlockSpec(memory_space=pl.ANY)
    out = pl.pallas_call(
        body,
        name="dw_bwd_exchange",
        grid=(nt,),
        in_specs=[pl.BlockSpec((tm, D_MODEL), lambda i: (i, 0))] + _dproj_specs(tm) + [any_spec] * n,
        out_specs=[any_spec] * (1 + n),
        out_shape=[_sds((A_W, D_MODEL), bf16)] + [_sds(b.shape, b.dtype) for b in bs],
        scratch_shapes=[pltpu.VMEM((A_W, D_MODEL), f32), pltpu.VMEM((DW_STAGE_ROWS, D_MODEL), bf16),
                        pltpu.SemaphoreType.DMA, pltpu.SemaphoreType.DMA((7 * n,)), pltpu.SemaphoreType.DMA((7 * n,)),
                        pltpu.SemaphoreType.DMA((n,))],
        compiler_params=_params(("arbitrary",), VMEM_LIMIT_BIG),
    )(x2, *pieces, *bs)
    return out[0], out[1:]


def _adam_call(recv, w, m, v, tc, name):
    rows, cols = w.shape

    def body(r_ref, w_ref, m_ref, v_ref, g_ref, d_ref, mo_ref, vo_ref):
        g = r_ref[0].astype(f32)
        for p in range(1, N_DEV):
            g = g + r_ref[p].astype(f32)
        mn = ADAM_B1 * m_ref[...] + (1.0 - ADAM_B1) * g
        vn = ADAM_B2 * v_ref[...] + (1.0 - ADAM_B2) * (g * g)
        m_hat = mn / (1.0 - ADAM_B1 ** ADAM_STEP)
        v_hat = vn / (1.0 - ADAM_B2 ** ADAM_STEP)
        g_ref[...] = g
        d_ref[...] = -ADAM_LR * (m_hat / (jnp.sqrt(v_hat) + ADAM_EPS) + ADAM_WD * w_ref[...])
        mo_ref[...] = mn
        vo_ref[...] = vn

    blk = pl.BlockSpec((rows, tc), lambda i: (0, i))
    return pl.pallas_call(
        body,
        name=name,
        grid=(cols // tc,),
        in_specs=[pl.BlockSpec((N_DEV, rows, tc), lambda i: (0, 0, i)), blk, blk, blk],
        out_specs=[blk] * 4,
        out_shape=[_sds((rows, cols), f32)] * 4,
        compiler_params=_params(("arbitrary",)),
    )(recv, w, m, v)


def _rows_to_shards(parts, shard):
    blocks = []
    for d in range(N_DEV):
        lo, hi, start, pieces = d * shard, (d + 1) * shard, 0, []
        for part in parts:
            a, b = max(lo, start), min(hi, start + part.shape[0])
            if a < b:
                pieces.append(part[a - start:b - start])
            start += part.shape[0]
        blocks.append(jnp.concatenate(pieces, axis=0))
    return jnp.stack(blocks)


def _pad_cols(a, width=128):
    return jnp.pad(a, ((0, 0), (0, width - a.shape[1])))


def _pack_small(ln_g, ln_b, rel, b_f, sink):
    return jnp.concatenate([
        ln_g.reshape(8, 128), ln_b.reshape(8, 128), _pad_cols(rel),
        jnp.pad(_pad_cols(b_f), ((0, 7), (0, 0))), jnp.pad(_pad_cols(sink), ((0, 7), (0, 0)))], axis=0)


def _unpack_small(p):
    return (p[0:8].reshape(1, D_MODEL), p[8:16].reshape(1, D_MODEL), p[16:48, 0:8], p[48:49, 0:8], p[56:57, 0:8])


def kernel(x, w_in, b_f, rel_bias, sink, w_o, ln_g, ln_b, loss_target, m_w_in, m_b_f, m_rel_bias, m_sink, m_w_o, m_ln_g, m_ln_b, v_w_in, v_b_f, v_rel_bias, v_sink, v_w_o, v_ln_g, v_ln_b):
    x2 = x[0]
    tgt = loss_target[0]
    s_len = x2.shape[0]
    shard = w_in.shape[2]

    w_in_t = jnp.transpose(w_in[0])
    (g_in,) = _gather_call([w_in_t.astype(bf16)])
    wt_full = g_in.reshape(N_DEV * shard, D_MODEL)
    w_t = jnp.concatenate([wt_full[:O_FF0], wt_full[O_FF1:], wt_full[O_FF0:O_FF1],
                           jnp.zeros((A_W - D_IN, D_MODEL), bf16)], axis=0)

    wo_blocks = jnp.broadcast_to(w_o[0].astype(bf16)[None], (N_DEV,) + w_o.shape[1:])
    (qft, kft, vf, fz, qst, ks, vs, sz, fft, vat, kst, vsta), (g_o,) = _proj_call(x2, w_t, [wo_blocks], 512)
    wo_full = g_o.reshape(D_MODEL, D_MODEL)
    cum, sgm = _cum_call(fft, b_f.reshape(FOX_HEADS, 1))
    qat, ka, kat, tile_stats = _augment_call(qft, kft, cum.reshape(FOX_HEADS, 1, s_len), 2048)
    npairs, pair_q, pair_k = _fox_prune_tables(tile_stats)
    o_ft, lse_f = _fox_fwd_call(qat, ka, vat, npairs, pair_q, pair_k)
    bucket_t = jnp.asarray(_t5_bucket_table().T)
    bias_t, bias0_t = _swa_bias_call(rel_bias, bucket_t)
    sink_v = sink.reshape(SWA_HEADS)
    o_st, lse_s = _swa_fwd_call(qst, ks, vsta, bias_t, bias0_t, sink_v)

    (dh, do_f, dfz, do_s, dsz, dl_f, dl_s, dwo, dg, db, loss_part) = _post_call(
        o_ft.reshape(FOX_HEADS * HEAD_DIM, s_len), fz, o_st.reshape(SWA_HEADS * HEAD_DIM, s_len), sz, x2, tgt,
        wo_full, ln_g, ln_b, jnp.asarray(_head_selector()).astype(bf16), 512)

    dqf, dkf, dvf, dcq, dck = _fox_bwd_call(ka, kat, vf, qat, do_f.reshape(FOX_HEADS, HEAD_DIM, s_len), lse_f,
                                            dl_f.reshape(FOX_HEADS, 1, s_len), npairs, pair_q, pair_k)
    dqf, dkf, dvf = (a.reshape(FOX_HEADS * HEAD_DIM, s_len) for a in (dqf, dkf, dvf))
    dfft, dbf = _cum_bwd_call(dcq.reshape(FOX_HEADS, s_len), dck.reshape(FOX_HEADS, s_len), sgm)
    dqs, dks, dvs, dbias, dsink = _swa_bwd_call(
        qst, ks, kst, vs, do_s.reshape(SWA_HEADS, HEAD_DIM, s_len), lse_s, dl_s.reshape(SWA_HEADS, 1, s_len),
        bias_t, bias0_t, sink_v)
    dqs = dqs.reshape(SWA_HEADS * HEAD_DIM, s_len)
    dks, dvs = (a.reshape(SWA_KV_HEADS * HEAD_DIM, s_len) for a in (dks, dvs))
    drel = _swa_bias_bwd_call(dbias, bucket_t)

    dwo_blocks = dwo.reshape(N_DEV, D_MODEL // N_DEV, D_MODEL).astype(bf16)
    small = _pack_small(dg, db, drel[:, 0:8], dbf[:, 0].reshape(1, 8), dsink[:, 0].reshape(1, 8))
    loss_slot = np.zeros((64, 128), bool)
    loss_slot[49, 0] = True
    small = jnp.where(jnp.asarray(loss_slot), loss_part[0, 0], small)
    small_blocks = jnp.broadcast_to(small[None], (N_DEV,) + small.shape)
    pieces = (dqf, dkf, dvf, dfz, dqs, dks, dvs, dsz, dfft)
    dw_t, (r_o, r_small) = _dw_exchange_call(x2, pieces, [dwo_blocks, small_blocks], 1024)
    dw_blocks = _rows_to_shards([dw_t[:O_FF0], dw_t[A_FF:A_FF + (O_FF1 - O_FF0)], dw_t[O_FF0:A_FF]], shard)
    dx, (r_in,) = _dx_exchange_call(dh, pieces, w_t, [dw_blocks], 256)

    win_t = [jnp.transpose(a) for a in _adam_call(
        r_in, w_in_t, jnp.transpose(m_w_in[0]), jnp.transpose(v_w_in[0]), 256, "adam_w_in")]
    g_win, d_win, nm_win, nv_win = win_t
    g_wo, d_wo, nm_wo, nv_wo = _adam_call(r_o, w_o[0], m_w_o[0], v_w_o[0], 256, "adam_w_o")
    p_w = _pack_small(ln_g, ln_b, rel_bias, b_f, sink)
    p_m = _pack_small(m_ln_g, m_ln_b, m_rel_bias, m_b_f, m_sink)
    p_v = _pack_small(v_ln_g, v_ln_b, v_rel_bias, v_b_f, v_sink)
    g_p, d_p, nm_p, nv_p = _adam_call(r_small, p_w, p_m, p_v, 128, "adam_small")

    loss = g_p[49, 0]
    g_lng, g_lnb, g_rel, g_bf, g_sink = _unpack_small(g_p)
    d_lng, d_lnb, d_rel, d_bf, d_sink = _unpack_small(d_p)
    m_lng, m_lnb, m_rel, m_bf, m_sk = _unpack_small(nm_p)
    v_lng, v_lnb, v_rel, v_bf, v_sk = _unpack_small(nv_p)
    return (loss, dx[None], g_win[None], g_bf, g_rel, g_sink, g_wo[None], g_lng, g_lnb,
            d_win[None], d_bf, d_rel, d_sink, d_wo[None], d_lng, d_lnb,
            nm_win[None], m_bf, m_rel, m_sk, nm_wo[None], m_lng, m_lnb,
            nv_win[None], v_bf, v_rel, v_sk, nv_wo[None], v_lng, v_lnb)
```

```python
import functools
import math

import numpy as np
import jax
import jax.numpy as jnp
from jax import lax
from jax.experimental import pallas as pl
from jax.experimental.pallas import tpu as pltpu

f32 = jnp.float32
bf16 = jnp.bfloat16

D_MODEL = 1024
HEAD_DIM = 64
FOX_HEADS = 8
SWA_HEADS = 8
SWA_KV_HEADS = 2
SWA_GROUP = 4
BLOCK = 128
NUM_BUCKETS = 32
MAX_DISTANCE = 128
LN_EPS = 1e-5
NEG_INF = -1e30
ALPHA = 2.0 ** 0.25
SCALE = 1.0 / math.sqrt(HEAD_DIM)
D_IN = 3336

ADAM_LR = 0.001
ADAM_B1 = 0.9
ADAM_B2 = 0.999
ADAM_EPS = 1e-08
ADAM_WD = 0.01
ADAM_STEP = 10

N_DEV = 8
A_FQ, A_FK, A_FV, A_FZ, A_SQ, A_SK, A_SV, A_SZ, A_FF, A_W = 0, 512, 1024, 1536, 2048, 2560, 2688, 2816, 3328, 3456
O_FF0, O_FF1 = 1536, 1544

VMEM_LIMIT = 48 * 1024 * 1024
HIGHEST = lax.Precision.HIGHEST
NT = (((1,), (1,)), ((), ()))
TN = (((0,), (0,)), ((), ()))
MESH = pl.DeviceIdType.MESH
RELS = [(0, 0, 1), (0, 1, 0), (0, 1, 1), (1, 0, 0), (1, 0, 1), (1, 1, 0), (1, 1, 1)]


VMEM_LIMIT_BIG = 60 * 1024 * 1024


def _params(sem=None, vmem=VMEM_LIMIT):
    return pltpu.CompilerParams(dimension_semantics=sem, vmem_limit_bytes=vmem)


def _sds(shape, dtype):
    return jax.ShapeDtypeStruct(shape, dtype)


def _t5_bucket_table():
    qi = np.arange(BLOCK)[:, None]
    kj = np.arange(2 * BLOCK)[None, :]
    rel = qi + BLOCK - kj
    band = (rel >= 0) & (rel < BLOCK)
    relc = np.maximum(rel, 0)
    max_exact = NUM_BUCKETS // 2
    relf = np.maximum(relc, 1).astype(np.float32)
    large = max_exact + (np.log(relf / np.float32(max_exact)) / np.float32(math.log(MAX_DISTANCE / max_exact))
                         * np.float32(NUM_BUCKETS - max_exact)).astype(np.int32)
    large = np.minimum(large, NUM_BUCKETS - 1)
    bucket = np.where(relc < max_exact, relc, large).astype(np.int32)
    bucket = np.where(band, bucket, -1).astype(np.int32)
    return bucket


def _mesh_pos():
    return lax.axis_index("x"), lax.axis_index("y"), lax.axis_index("c")


def _dev_index(p):
    return 4 * p[0] + 2 * p[1] + p[2]


def _gather_call(xs):
    n = len(xs)

    def body(*refs):
        x_refs, o_refs = refs[:n], refs[n:2 * n]
        send_sems, recv_sems, local_sems = refs[2 * n:]
        x, y, c = _mesh_pos()
        me, sib = (x, y, c), (x, y, 1 - c)
        chips = [(1 - x, y), (x, 1 - y), (1 - x, 1 - y)]

        def copy(a, k, block, to, src=None):
            slot = o_refs[a].at[_dev_index(block)]
            return pltpu.make_async_remote_copy(
                src_ref=slot if src is None else src, dst_ref=slot,
                send_sem=send_sems.at[a * 7 + k], recv_sem=recv_sems.at[a * 7 + k],
                device_id=to, device_id_type=MESH)

        mine = [pltpu.make_async_copy(x_refs[a], o_refs[a].at[_dev_index(me)], local_sems.at[a]) for a in range(n)]
        for cp in mine:
            cp.start()
        first = []
        for a in range(n):
            first.append(copy(a, 0, me, sib, src=x_refs[a]))
            first += [copy(a, 1 + j, me, (*chip, c), src=x_refs[a]) for j, chip in enumerate(chips)]
        for cp in first:
            cp.start()
        passed = []
        for j, chip in enumerate(chips):
            for a in range(n):
                copy(a, 1 + j, (*chip, c), me).wait_recv()
                fwd = copy(a, 4 + j, (*chip, c), sib)
                fwd.start()
                passed.append(fwd)
        for a in range(n):
            copy(a, 0, sib, me).wait_recv()
            for j, chip in enumerate(chips):
                copy(a, 4 + j, (*chip, 1 - c), me).wait_recv()
        for cp in first + passed:
            cp.wait_send()
        for cp in mine:
            cp.wait()

    any_spec = pl.BlockSpec(memory_space=pl.ANY)
    return pl.pallas_call(
        body,
        name="gather_weights",
        out_shape=[_sds((N_DEV,) + a.shape, a.dtype) for a in xs],
        in_specs=[any_spec] * n,
        out_specs=[any_spec] * n,
        scratch_shapes=[pltpu.SemaphoreType.DMA((7 * n,)), pltpu.SemaphoreType.DMA((7 * n,)),
                        pltpu.SemaphoreType.DMA((n,))],
    )(*xs)


def _exchange_copies(b_refs, r_refs, send_sems, recv_sems, local_sems, incoming):
    n = len(b_refs)
    x, y, c = _mesh_pos()
    me_idx = _dev_index((x, y, c))
    mine = [pltpu.make_async_copy(b_refs[a].at[me_idx], r_refs[a].at[me_idx], local_sems.at[a]) for a in range(n)]
    remote = []
    for k, r in enumerate(RELS):
        peer = ((1 - x) if r[0] else x, (1 - y) if r[1] else y, (1 - c) if r[2] else c)
        pidx = _dev_index(peer)
        for a in range(n):
            remote.append(pltpu.make_async_remote_copy(
                src_ref=b_refs[a].at[pidx], dst_ref=r_refs[a].at[pidx if incoming else me_idx],
                send_sem=send_sems.at[a * 7 + k], recv_sem=recv_sems.at[a * 7 + k],
                device_id=peer, device_id_type=MESH))
    return mine, remote


def _exchange_start(b_refs, r_refs, sems):
    mine, out = _exchange_copies(b_refs, r_refs, *sems, incoming=False)
    for cp in mine + out:
        cp.start()


def _exchange_wait(b_refs, r_refs, sems):
    mine, inc = _exchange_copies(b_refs, r_refs, *sems, incoming=True)
    for cp in inc:
        cp.wait_recv()
    for cp in inc:
        cp.wait_send()
    for cp in mine:
        cp.wait()


def _proj_call(x2, w_t, bs, tm):
    s_len = x2.shape[0]
    n = len(bs)
    last = s_len // tm - 1

    def body(*refs):
        x_ref, w_ref = refs[:2]
        b_refs = refs[2:2 + n]
        (qft_ref, kft_ref, vf_ref, fz_ref, qst_ref, ks_ref, vs_ref, sz_ref, fft_ref, vat_ref,
         kst_ref, vsta_ref) = refs[2 + n:14 + n]
        r_refs = refs[14 + n:14 + 2 * n]
        sems = refs[14 + 2 * n:]

        @pl.when(pl.program_id(0) == 0)
        def _():
            _exchange_start(b_refs, r_refs, sems)

        xb = x_ref[...].astype(bf16)

        def seg_t(off, width):
            return lax.dot_general(w_ref[off:off + width, :], xb, NT, preferred_element_type=f32)

        def seg(off, width):
            return lax.dot_general(xb, w_ref[off:off + width, :], NT, preferred_element_type=f32)

        def put_heads(ref, acc, nheads):
            for h in range(nheads):
                ref[h] = acc[:, h * HEAD_DIM:(h + 1) * HEAD_DIM].astype(bf16)

        def put_heads_t(ref, acc_t, nheads, augment):
            for h in range(nheads):
                ref[h, 0:HEAD_DIM, :] = acc_t[h * HEAD_DIM:(h + 1) * HEAD_DIM, :].astype(bf16)
                if augment:
                    ref[h, HEAD_DIM:2 * HEAD_DIM, :] = ones_row

        ones_row = jnp.where(lax.broadcasted_iota(jnp.int32, (HEAD_DIM, tm), 0) == 0, 1.0, 0.0).astype(bf16)
        put_heads_t(vat_ref, seg_t(A_FV, 512), FOX_HEADS, True)
        put_heads_t(qft_ref, seg_t(A_FQ, 512) * SCALE, FOX_HEADS, False)
        put_heads_t(kft_ref, seg_t(A_FK, 512), FOX_HEADS, False)
        put_heads(vf_ref, seg(A_FV, 512), FOX_HEADS)
        fz_ref[...] = seg(A_FZ, 512)
        put_heads_t(qst_ref, seg_t(A_SQ, 512) * SCALE, SWA_HEADS, False)
        put_heads(ks_ref, seg(A_SK, 128), SWA_KV_HEADS)
        put_heads(vs_ref, seg(A_SV, 128), SWA_KV_HEADS)
        put_heads_t(kst_ref, seg_t(A_SK, 128), SWA_KV_HEADS, False)
        put_heads_t(vsta_ref, seg_t(A_SV, 128), SWA_KV_HEADS, True)
        sz_ref[...] = seg(A_SZ, 512)
        fft_ref[...] = seg(A_FF, 128).T[:FOX_HEADS, :]

        @pl.when(pl.program_id(0) == last)
        def _():
            _exchange_wait(b_refs, r_refs, sems)

    def heads(nh):
        return pl.BlockSpec((nh, tm, HEAD_DIM), lambda i: (0, i, 0))

    def feat(nh, rows):
        return pl.BlockSpec((nh, rows, tm), lambda i: (0, 0, i))

    wide = pl.BlockSpec((tm, 512), lambda i: (i, 0))
    any_spec = pl.BlockSpec(memory_space=pl.ANY)
    out = pl.pallas_call(
        body,
        name="proj_fwd_gather",
        grid=(s_len // tm,),
        in_specs=[pl.BlockSpec((tm, D_MODEL), lambda i: (i, 0)), pl.BlockSpec((A_W, D_MODEL), lambda i: (0, 0))]
                 + [any_spec] * n,
        out_specs=[feat(8, HEAD_DIM), feat(8, HEAD_DIM), heads(8), wide, feat(8, HEAD_DIM), heads(2), heads(2), wide,
                   pl.BlockSpec((FOX_HEADS, tm), lambda i: (0, i)),
                   feat(FOX_HEADS, 2 * HEAD_DIM), feat(2, HEAD_DIM), feat(2, 2 * HEAD_DIM)] + [any_spec] * n,
        out_shape=[_sds((8, HEAD_DIM, s_len), bf16)] * 2 + [_sds((8, s_len, HEAD_DIM), bf16)]
                  + [_sds((s_len, 512), f32), _sds((8, HEAD_DIM, s_len), bf16),
                     _sds((2, s_len, HEAD_DIM), bf16), _sds((2, s_len, HEAD_DIM), bf16), _sds((s_len, 512), f32),
                     _sds((FOX_HEADS, s_len), f32), _sds((FOX_HEADS, 2 * HEAD_DIM, s_len), bf16),
                     _sds((2, HEAD_DIM, s_len), bf16), _sds((2, 2 * HEAD_DIM, s_len), bf16)]
                  + [_sds(b.shape, b.dtype) for b in bs],
        scratch_shapes=[pltpu.SemaphoreType.DMA((7 * n,)), pltpu.SemaphoreType.DMA((7 * n,)),
                        pltpu.SemaphoreType.DMA((n,))],
        compiler_params=_params(("arbitrary",)),
    )(x2, w_t, *bs)
    return out[:12], out[12:]


AUG = 2 * HEAD_DIM
NEAR_KEYS = 3


def _augment_call(q_t, k_t, cum_row, tm):
    nh, _, s_len = k_t.shape
    per_step = tm // FOX_T

    def body(qt_ref, kt_ref, c_ref, qat_ref, ka_ref, kat_ref, st_ref):
        c = c_ref[0]
        hi = c.astype(bf16).astype(f32)
        r1 = c - hi
        mid = r1.astype(bf16).astype(f32)
        lo = (r1 - mid).astype(bf16).astype(f32)
        row = lax.broadcasted_iota(jnp.int32, (HEAD_DIM, tm), 0)
        q_tail = jnp.where(row == 0, hi, jnp.where(row == 1, mid, jnp.where(row == 2, lo,
                           jnp.where(row < 6, 1.0, 0.0))))
        k_tail = jnp.where(row < 3, 1.0, jnp.where(row == 3, -hi, jnp.where(row == 4, -mid,
                           jnp.where(row == 5, -lo, 0.0))))
        qat_ref[0, 0:HEAD_DIM, :] = qt_ref[0]
        qat_ref[0, HEAD_DIM:AUG, :] = q_tail.astype(bf16)
        kat_ref[0, 0:HEAD_DIM, :] = kt_ref[0]
        kat_ref[0, HEAD_DIM:AUG, :] = k_tail.astype(bf16)
        qt = qt_ref[0].astype(f32)
        kt = kt_ref[0].astype(f32)
        ka_ref[0] = jnp.concatenate([kt, k_tail], axis=0).T.astype(bf16)
        qn2 = jnp.sum(qt * qt, axis=0, keepdims=True)
        kn2 = jnp.sum(kt * kt, axis=0, keepdims=True)
        sd = jnp.sum(qt * kt, axis=0, keepdims=True)
        k_and_c = jnp.concatenate([kt, jnp.broadcast_to(c, (8, tm))], axis=0)
        lane = lax.broadcasted_iota(jnp.int32, (1, tm), 1)
        for shift in range(1, NEAR_KEYS + 1):
            prev = pltpu.roll(k_and_c, shift, axis=1)
            near = jnp.sum(qt * prev[0:HEAD_DIM], axis=0, keepdims=True) + (c - prev[HEAD_DIM:HEAD_DIM + 1])
            sd = jnp.maximum(sd, jnp.where(lane >= shift, near, NEG_INF))
        srow = lax.broadcasted_iota(jnp.int32, (8, LANES), 0)
        for part in range(per_step):
            sl = slice(part * FOX_T, (part + 1) * FOX_T)
            vals = [jnp.sqrt(jnp.max(qn2[:, sl], axis=1, keepdims=True)),
                    jnp.sqrt(jnp.max(kn2[:, sl], axis=1, keepdims=True)),
                    jnp.min(sd[:, sl], axis=1, keepdims=True),
                    jnp.max(c[:, sl], axis=1, keepdims=True), jnp.min(c[:, sl], axis=1, keepdims=True)]
            out = jnp.zeros((8, LANES), f32)
            for r, val in enumerate(vals):
                out = jnp.where(srow == r, val, out)
            st_ref[0, part] = out

    tile_t = pl.BlockSpec((1, HEAD_DIM, tm), lambda h, i: (h, 0, i))
    return pl.pallas_call(
        body,
        name="fox_augment",
        grid=(nh, s_len // tm),
        in_specs=[tile_t, tile_t, pl.BlockSpec((1, 1, tm), lambda h, i: (h, 0, i))],
        out_specs=[pl.BlockSpec((1, AUG, tm), lambda h, i: (h, 0, i)),
                   pl.BlockSpec((1, tm, AUG), lambda h, i: (h, i, 0)),
                   pl.BlockSpec((1, AUG, tm), lambda h, i: (h, 0, i)),
                   pl.BlockSpec((1, per_step, 8, LANES), lambda h, i: (h, i, 0, 0))],
        out_shape=[_sds((nh, AUG, s_len), bf16), _sds((nh, s_len, AUG), bf16), _sds((nh, AUG, s_len), bf16),
                   _sds((nh, s_len // FOX_T, 8, LANES), f32)],
        compiler_params=_params(("arbitrary", "arbitrary")),
    )(q_t, k_t, cum_row)


FOX_PRUNE_GAP = 32.0


def _fox_prune_tables(stats):
    s = stats[:, :, :, 0]
    qn, kn, sd, cmx, cmn = (s[:, :, r] for r in range(5))
    nt = s.shape[1]
    bound = qn[:, :, None] * kn[:, None, :] + (cmx[:, :, None] - cmn[:, None, :])
    margin = 0.01 + 1e-5 * (jnp.abs(cmx)[:, :, None] + jnp.abs(cmn)[:, None, :])
    qi = lax.broadcasted_iota(jnp.int32, (nt, nt), 0)
    kj = lax.broadcasted_iota(jnp.int32, (nt, nt), 1)
    skip = (bound + margin < sd[:, :, None] - FOX_PRUNE_GAP) & (kj < qi)[None]
    first = jnp.sum(jnp.cumprod(skip.astype(jnp.int32), axis=2), axis=2)
    tiles = lax.broadcasted_iota(jnp.int32, (1, nt), 1)
    cnt = tiles - first
    ends = jnp.cumsum(cnt, axis=1)
    off = ends - cnt
    kmax = nt * (nt - 1) // 2
    k = lax.broadcasted_iota(jnp.int32, (1, kmax), 1)
    pair_q = jnp.minimum(jnp.sum((ends[:, None, :] <= k[:, :, None]).astype(jnp.int32), axis=2), nt - 1)
    hit = pair_q[:, :, None] == tiles[:, None, :]
    first_k = jnp.sum(jnp.where(hit, first[:, None, :], 0), axis=2)
    off_k = jnp.sum(jnp.where(hit, off[:, None, :], 0), axis=2)
    pair_k = jnp.clip(first_k + k - off_k, 0, nt - 1)
    return (ends[:, nt - 1].astype(jnp.int32), pair_q.reshape(-1).astype(jnp.int32),
            pair_k.reshape(-1).astype(jnp.int32))


CUM_CHUNK = 512


def _cum_call(fft, bf_col):
    s_len = fft.shape[1]
    ch = CUM_CHUNK

    def body(f_ref, b_ref, cum_ref, sg_ref):
        r = lax.broadcasted_iota(jnp.int32, (ch, ch), 0)
        c = lax.broadcasted_iota(jnp.int32, (ch, ch), 1)
        upper = (r <= c).astype(f32)
        carry = jnp.zeros((FOX_HEADS, 1), f32)
        for n in range(s_len // ch):
            z = f_ref[:, n * ch:(n + 1) * ch] + b_ref[...]
            logf = jnp.minimum(z, 0.0) - jnp.log1p(jnp.exp(-jnp.abs(z)))
            sg_ref[:, n * ch:(n + 1) * ch] = 1.0 / (1.0 + jnp.exp(z))
            cs = jnp.dot(logf, upper, precision=HIGHEST, preferred_element_type=f32) + carry
            cum_ref[:, n * ch:(n + 1) * ch] = cs
            carry = cs[:, ch - 1:ch]

    return pl.pallas_call(
        body,
        name="fox_cum_fwd",
        out_shape=[_sds((FOX_HEADS, s_len), f32)] * 2,
        compiler_params=_params(),
    )(fft, bf_col)


def _cum_bwd_call(dcq, dck, sg):
    s_len = sg.shape[1]
    ch = CUM_CHUNK
    nch = s_len // ch

    def body(q_ref, k_ref, sg_ref, dff_ref, dbf_ref):
        r = lax.broadcasted_iota(jnp.int32, (ch, ch), 0)
        c = lax.broadcasted_iota(jnp.int32, (ch, ch), 1)
        lower = (r >= c).astype(f32)
        dff_ref[...] = jnp.zeros_like(dff_ref)
        carry = jnp.zeros((FOX_HEADS, 1), f32)
        total = jnp.zeros((FOX_HEADS, 1), f32)
        for n in reversed(range(nch)):
            sl = slice(n * ch, (n + 1) * ch)
            dcum = q_ref[:, sl] - k_ref[:, sl]
            rs = jnp.dot(dcum, lower, precision=HIGHEST, preferred_element_type=f32) + carry
            carry = rs[:, 0:1]
            dff = rs * sg_ref[:, sl]
            dff_ref[0:FOX_HEADS, sl] = dff
            total = total + jnp.sum(dff, axis=1, keepdims=True)
        dbf_ref[...] = jnp.broadcast_to(total, (FOX_HEADS, 128))

    return pl.pallas_call(
        body,
        name="fox_cum_bwd",
        out_shape=[_sds((128, s_len), f32), _sds((FOX_HEADS, 128), f32)],
        compiler_params=_params(),
    )(dcq, dck, sg)


FOX_T = 512
ACC_ROWS = HEAD_DIM + 16
LANES = 128


def _causal_keep(t):
    return lax.broadcasted_iota(jnp.int32, (t, t), 0) <= lax.broadcasted_iota(jnp.int32, (t, t), 1)


def _tile_cols(i, t):
    return pl.ds(pl.multiple_of(i * t, t), t)


def _fox_pair(n, nt, kmax, h, pq_ref, pk_ref):
    k = h * kmax + jnp.maximum(n - nt, 0)
    return jnp.where(n < nt, n, pq_ref[k]), jnp.where(n < nt, n, pk_ref[k])


def _fox_fwd_call(qat, ka, vat, npairs, pair_q, pair_k):
    nh, s_len, _ = ka.shape
    t = FOX_T
    nt = s_len // t
    kmax = nt * (nt - 1) // 2
    assert nt >= 2 and nt % 2 == 0

    def body(np_ref, pq_ref, pk_ref, qat_ref, ka_ref, vat_ref, o_ref, lse_ref, s0, s1, p0, p1, a0, a1, m_all, acc_all):
        h = pl.program_id(0)
        extra = np_ref[h]
        total = nt + extra
        m_all[...] = jnp.full(m_all.shape, NEG_INF, f32)
        acc_all[...] = jnp.zeros(acc_all.shape, f32)
        bufs = ((s0, p0, a0), (s1, p1, a1))

        def pair(n):
            return _fox_pair(n, nt, kmax, h, pq_ref, pk_ref)

        def scores(n, b, masked):
            i, j = pair(n)
            st = jnp.dot(ka_ref[0, _tile_cols(j, t), :], qat_ref[0, :, _tile_cols(i, t)], preferred_element_type=f32)
            if masked:
                st = jnp.where(_causal_keep(t), st, NEG_INF)
            bufs[b][0][...] = st

        def softmax(n, b):
            i, _ = pair(n)
            s_ref, p_ref, a_ref = bufs[b]
            for c in range(t // LANES):
                cols = slice(c * LANES, (c + 1) * LANES)
                mcols = pl.ds(pl.multiple_of(i * t + c * LANES, LANES), LANES)
                m_old = m_all[:, mcols]
                m_new = jnp.maximum(m_old, jnp.max(s_ref[:, cols], axis=0, keepdims=True))
                m_all[:, mcols] = m_new
                a_ref[:, cols] = jnp.exp(m_old - m_new)
                p_ref[:, cols] = jnp.exp(s_ref[:, cols] - m_new).astype(bf16)

        def accum(n, b):
            i, j = pair(n)
            cols = _tile_cols(i, t)
            acc_all[:, cols] = bufs[b][2][...] * acc_all[:, cols] + jnp.dot(
                vat_ref[0, 0:ACC_ROWS, _tile_cols(j, t)], bufs[b][1][...], preferred_element_type=f32)

        def step(n, b, masked):
            accum(n - 2, b)
            softmax(n - 1, 1 - b)
            scores(n, b, masked)

        scores(0, 0, True)
        scores(1, 1, True)
        softmax(0, 0)

        def diag_steps(d, _):
            n = 2 + 2 * d
            step(n, 0, True)
            step(n + 1, 1, True)
            return 0

        lax.fori_loop(0, (nt - 2) // 2, diag_steps, 0)

        def off_steps(d, _):
            n = nt + 2 * d
            step(n, 0, False)
            step(n + 1, 1, False)
            return 0

        lax.fori_loop(0, extra // 2, off_steps, 0)

        @pl.when(extra % 2 == 1)
        def _():
            step(total - 1, 0, False)
            softmax(total - 1, 0)
            accum(total - 2, 1)
            accum(total - 1, 0)

        @pl.when(extra % 2 == 0)
        def _():
            softmax(total - 1, 1)
            accum(total - 2, 0)
            accum(total - 1, 1)

        l = acc_all[HEAD_DIM:HEAD_DIM + 1, :]
        o_ref[0] = acc_all[0:HEAD_DIM, :] / l
        lse_ref[0] = m_all[...] + jnp.log(l)

    smem = pl.BlockSpec(memory_space=pltpu.SMEM)
    return pl.pallas_call(
        body,
        name="fox_fwd",
        grid=(nh,),
        in_specs=[smem, smem, smem,
                  pl.BlockSpec((1, AUG, s_len), lambda h: (h, 0, 0)),
                  pl.BlockSpec((1, s_len, AUG), lambda h: (h, 0, 0)),
                  pl.BlockSpec((1, AUG, s_len), lambda h: (h, 0, 0))],
        out_specs=[pl.BlockSpec((1, HEAD_DIM, s_len), lambda h: (h, 0, 0)),
                   pl.BlockSpec((1, 1, s_len), lambda h: (h, 0, 0))],
        out_shape=[_sds((nh, HEAD_DIM, s_len), f32), _sds((nh, 1, s_len), f32)],
        scratch_shapes=[pltpu.VMEM((t, t), f32), pltpu.VMEM((t, t), f32), pltpu.VMEM((t, t), bf16),
                        pltpu.VMEM((t, t), bf16), pltpu.VMEM((1, t), f32), pltpu.VMEM((1, t), f32),
                        pltpu.VMEM((1, s_len), f32), pltpu.VMEM((ACC_ROWS, s_len), f32)],
        compiler_params=_params(("arbitrary",)),
    )(npairs, pair_q, pair_k, qat, ka, vat)


SWA_TS = 512


SWA_W = SWA_GROUP * BLOCK


def _swa_bias_call(rel_bias, bucket_t):
    def body(rb_ref, bk_ref, b_ref, b0_ref):
        bk = bk_ref[...]
        row = lax.broadcasted_iota(jnp.int32, (2 * BLOCK, BLOCK), 0)
        for h in range(SWA_HEADS):
            acc = jnp.full((2 * BLOCK, BLOCK), NEG_INF, f32)
            for b in range(NUM_BUCKETS):
                acc = jnp.where(bk == b, rb_ref[b, h], acc)
            g, hh = divmod(h, SWA_GROUP)
            b_ref[g, :, hh * BLOCK:(hh + 1) * BLOCK] = acc
            b0_ref[g, :, hh * BLOCK:(hh + 1) * BLOCK] = jnp.where(row < BLOCK, NEG_INF, acc)

    return pl.pallas_call(
        body,
        name="swa_bias",
        in_specs=[pl.BlockSpec(memory_space=pltpu.SMEM), pl.BlockSpec(memory_space=pltpu.VMEM)],
        out_shape=[_sds((SWA_KV_HEADS, 2 * BLOCK, SWA_W), f32)] * 2,
        compiler_params=_params(),
    )(rel_bias, bucket_t)


def _swa_bias_bwd_call(dbias, bucket_t):
    def body(d_ref, bk_ref, o_ref):
        bk = bk_ref[...]
        row = lax.broadcasted_iota(jnp.int32, (NUM_BUCKETS, 128), 0)
        col = lax.broadcasted_iota(jnp.int32, (NUM_BUCKETS, 128), 1)
        out = jnp.zeros((NUM_BUCKETS, 128), f32)
        for h in range(SWA_HEADS):
            g, hh = divmod(h, SWA_GROUP)
            d = d_ref[g, :, hh * BLOCK:(hh + 1) * BLOCK]
            for b in range(NUM_BUCKETS):
                val = jnp.sum(jnp.sum(jnp.where(bk == b, d, 0.0), axis=1, keepdims=True), axis=0, keepdims=True)
                out = jnp.where((row == b) & (col == h), val, out)
        o_ref[...] = out

    return pl.pallas_call(
        body,
        name="swa_bias_bwd",
        out_shape=_sds((NUM_BUCKETS, 128), f32),
        compiler_params=_params(),
    )(dbias, bucket_t)


def _sink_row(sink_ref, g):
    return jnp.concatenate([jnp.full((1, BLOCK), sink_ref[g * SWA_GROUP + hh], f32) for hh in range(SWA_GROUP)], axis=1)


def _group_lanes(ref, g, cols):
    return jnp.concatenate([ref[g * SWA_GROUP + hh, :, cols] for hh in range(SWA_GROUP)], axis=1)


def _swa_fwd_call(qt, k, vta, bias_t, bias0_t, sink):
    s_len = qt.shape[2]
    ts = SWA_TS
    nb = ts // BLOCK

    def body(qt_ref, kc_ref, kp_ref, vc_ref, vp_ref, b_ref, b0_ref, sink_ref, o_ref, lse_ref):
        first = pl.program_id(0) == 0
        kall = [jnp.concatenate([kp_ref[g], kc_ref[g]], axis=0) for g in range(SWA_KV_HEADS)]
        vall = [jnp.concatenate([vp_ref[g], vc_ref[g]], axis=1) for g in range(SWA_KV_HEADS)]
        sinks = [_sink_row(sink_ref, g) for g in range(SWA_KV_HEADS)]
        items = [(g, b) for g in range(SWA_KV_HEADS) for b in range(nb)]

        def scores(g, b):
            qg = _group_lanes(qt_ref, g, slice(b * BLOCK, (b + 1) * BLOCK))
            bias_b = b_ref[g]
            if b == 0:
                bias_b = jnp.where(first, b0_ref[g], bias_b)
            return jnp.dot(kall[g][b * BLOCK:(b + 2) * BLOCK], qg, preferred_element_type=f32) + bias_b

        def finish(g, b, st):
            m = jnp.maximum(jnp.max(st, axis=0, keepdims=True), sinks[g])
            pt = jnp.exp(st - m)
            acc = jnp.dot(vall[g][:, b * BLOCK:(b + 2) * BLOCK], pt.astype(bf16), preferred_element_type=f32)
            l = acc[HEAD_DIM:HEAD_DIM + 1, :] + jnp.exp(sinks[g] - m)
            return acc[0:HEAD_DIM, :] / l, m + jnp.log(l)

        outs, lses = {}, {}
        st_next = scores(*items[0])
        for idx, (g, b) in enumerate(items):
            st = st_next
            if idx + 1 < len(items):
                st_next = scores(*items[idx + 1])
            outs[g, b], lses[g, b] = finish(g, b, st)
        for g in range(SWA_KV_HEADS):
            for hh in range(SWA_GROUP):
                lanes = slice(hh * BLOCK, (hh + 1) * BLOCK)
                o_ref[g * SWA_GROUP + hh] = jnp.concatenate([outs[g, b][:, lanes] for b in range(nb)], axis=1)
                lse_ref[g * SWA_GROUP + hh] = jnp.concatenate([lses[g, b][:, lanes] for b in range(nb)], axis=1)

    def prev_blk(n):
        return jnp.maximum(n * nb - 1, 0)

    bspec = pl.BlockSpec((SWA_KV_HEADS, 2 * BLOCK, SWA_W), lambda n: (0, 0, 0))
    return pl.pallas_call(
        body,
        name="swa_fwd",
        grid=(s_len // ts,),
        in_specs=[pl.BlockSpec((SWA_HEADS, HEAD_DIM, ts), lambda n: (0, 0, n)),
                  pl.BlockSpec((SWA_KV_HEADS, ts, HEAD_DIM), lambda n: (0, n, 0)),
                  pl.BlockSpec((SWA_KV_HEADS, BLOCK, HEAD_DIM), lambda n: (0, prev_blk(n), 0)),
                  pl.BlockSpec((SWA_KV_HEADS, AUG, ts), lambda n: (0, 0, n)),
                  pl.BlockSpec((SWA_KV_HEADS, AUG, BLOCK), lambda n: (0, 0, prev_blk(n))),
                  bspec, bspec, pl.BlockSpec(memory_space=pltpu.SMEM)],
        out_specs=[pl.BlockSpec((SWA_HEADS, HEAD_DIM, ts), lambda n: (0, 0, n)),
                   pl.BlockSpec((SWA_HEADS, 1, ts), lambda n: (0, 0, n))],
        out_shape=[_sds((SWA_HEADS, HEAD_DIM, s_len), f32), _sds((SWA_HEADS, 1, s_len), f32)],
        compiler_params=_params(("arbitrary",)),
    )(qt, k, k, vta, vta, bias_t, bias0_t, sink)


def _head_selector():
    sel = np.zeros((512, 128), np.float32)
    for h in range(8):
        sel[h * HEAD_DIM:(h + 1) * HEAD_DIM, h] = 1.0
    return sel


def _post_call(of, fz, osw, sz, x2, tgt, wo, ln_g, ln_b, sel, tm):
    s_len = x2.shape[0]

    def body(of_ref, fz_ref, os_ref, sz_ref, x_ref, t_ref, wo_ref, g_ref, b_ref, sel_ref,
             dh_ref, dof_ref, dfz_ref, dos_ref, dsz_ref, dlf_ref, dls_ref, dwo_ref, dg_ref, db_ref, loss_ref):
        n = pl.program_id(0)

        @pl.when(n == 0)
        def _():
            dwo_ref[...] = jnp.zeros_like(dwo_ref)
            dg_ref[...] = jnp.zeros_like(dg_ref)
            db_ref[...] = jnp.zeros_like(db_ref)
            loss_ref[...] = jnp.zeros_like(loss_ref)

        gam = g_ref[...]
        sel_m = sel_ref[...]

        def forward(r):
            o_f = of_ref[:, r].T
            o_s = os_ref[:, r].T
            fz = fz_ref[r, :]
            sz = sz_ref[r, :]
            sg_f = jax.nn.sigmoid(fz)
            sg_s = jax.nn.sigmoid(sz)
            silu_f = fz * sg_f
            silu_s = sz * sg_s
            mixed = jnp.concatenate([o_f * silu_f, o_s * silu_s], axis=1).astype(bf16)
            y = jnp.dot(mixed, wo_ref[...], preferred_element_type=f32)
            return o_f, o_s, fz, sz, sg_f, sg_s, silu_f, silu_s, mixed, y

        def norm_and_back(r, fwd):
            mixed, y = fwd[8], fwd[9]
            h = ALPHA * x_ref[r, :] + y
            mu = jnp.mean(h, axis=1, keepdims=True)
            hc = h - mu
            var = jnp.mean(hc * hc, axis=1, keepdims=True)
            rstd = lax.rsqrt(var + LN_EPS)
            xhat = hc * rstd
            out = xhat * gam + b_ref[...]
            err = out - t_ref[r, :]
            tok_loss = jnp.mean(err * err, axis=1, keepdims=True)
            loss_ref[...] += 0.5 * jnp.sum(tok_loss, axis=0, keepdims=True)
            dout = err * (1.0 / D_MODEL)
            dg_ref[...] += jnp.sum(dout * xhat, axis=0, keepdims=True)
            db_ref[...] += jnp.sum(dout, axis=0, keepdims=True)
            dxh = dout * gam
            m1 = jnp.mean(dxh, axis=1, keepdims=True)
            m2 = jnp.mean(dxh * xhat, axis=1, keepdims=True)
            dh = rstd * (dxh - m1 - xhat * m2)
            dh_ref[r, :] = dh
            dyb = dh.astype(bf16)
            dmix = lax.dot_general(dyb, wo_ref[...], NT, preferred_element_type=f32)
            dwo_ref[...] += lax.dot_general(mixed, dyb, TN, preferred_element_type=f32)
            return dmix

        def head_sums(prod):
            hi = prod.astype(bf16)
            lo = (prod - hi.astype(f32)).astype(bf16)
            return (jnp.dot(hi, sel_m, preferred_element_type=f32) + jnp.dot(lo, sel_m, preferred_element_type=f32))

        def gates_back(r, fwd, dmix):
            o_f, o_s, fz, sz, sg_f, sg_s, silu_f, silu_s = fwd[:8]
            dm_f = dmix[:, :512]
            dm_s = dmix[:, 512:]
            do_f = dm_f * silu_f
            do_s = dm_s * silu_s
            dfz_ref[r, :] = (dm_f * o_f * (sg_f * (1.0 + fz * (1.0 - sg_f)))).astype(bf16)
            dsz_ref[r, :] = (dm_s * o_s * (sg_s * (1.0 + sz * (1.0 - sg_s)))).astype(bf16)
            dof_ref[:, r] = do_f.T.astype(bf16)
            dos_ref[:, r] = do_s.T.astype(bf16)
            dlf_ref[:, r] = head_sums(do_f * o_f).T[:FOX_HEADS, :]
            dls_ref[:, r] = head_sums(do_s * o_s).T[:SWA_HEADS, :]

        halves = [slice(k * (tm // 2), (k + 1) * (tm // 2)) for k in range(2)]
        fwds = [forward(r) for r in halves]
        dmixes = [norm_and_back(r, f) for r, f in zip(halves, fwds)]
        for r, f, d in zip(halves, fwds, dmixes):
            gates_back(r, f, d)

    feat = pl.BlockSpec((512, tm), lambda n: (0, n))
    rows8 = pl.BlockSpec((8, tm), lambda n: (0, n))
    half = pl.BlockSpec((tm, 512), lambda n: (n, 0))
    fullw = pl.BlockSpec((tm, D_MODEL), lambda n: (n, 0))
    vec = pl.BlockSpec((1, D_MODEL), lambda n: (0, 0))
    return pl.pallas_call(
        body,
        name="post_fwd_bwd",
        grid=(s_len // tm,),
        in_specs=[feat, half, feat, half, fullw, fullw,
                  pl.BlockSpec((D_MODEL, D_MODEL), lambda n: (0, 0)), vec, vec,
                  pl.BlockSpec((512, 128), lambda n: (0, 0))],
        out_specs=[fullw, feat, half, feat, half, rows8, rows8,
                   pl.BlockSpec((D_MODEL, D_MODEL), lambda n: (0, 0)), vec, vec,
                   pl.BlockSpec((1, 1), lambda n: (0, 0))],
        out_shape=[_sds((s_len, D_MODEL), f32), _sds((512, s_len), bf16), _sds((s_len, 512), bf16),
                   _sds((512, s_len), bf16), _sds((s_len, 512), bf16),
                   _sds((FOX_HEADS, s_len), f32), _sds((SWA_HEADS, s_len), f32),
                   _sds((D_MODEL, D_MODEL), f32), _sds((1, D_MODEL), f32), _sds((1, D_MODEL), f32),
                   _sds((1, 1), f32)],
        compiler_params=_params(("arbitrary",), VMEM_LIMIT_BIG),
    )(of, fz, osw, sz, x2, tgt, wo, ln_g, ln_b, sel)


def _fox_bwd_call(ka, kat, v, qat, dot, lse_row, dl_row, npairs, pair_q, pair_k):
    nh, s_len, _ = ka.shape
    t = FOX_T
    nt = s_len // t
    kmax = nt * (nt - 1) // 2
    assert nt >= 2 and nt % 2 == 0
    ck_slot = HEAD_DIM + 3
    cq_slot = HEAD_DIM

    def body(np_ref, pq_ref, pk_ref, ka_ref, kat_ref, v_ref, qat_ref, dot_ref, lse_ref, dl_ref,
             dq_ref, dk_ref, dv_ref, dcq_ref, dck_ref, dqt_all, dkat_all, dvt_all, p0, p1, ds0, ds1):
        h = pl.program_id(0)
        extra = np_ref[h]
        total = nt + extra
        dqt_all[...] = jnp.zeros(dqt_all.shape, f32)
        dkat_all[...] = jnp.zeros(dkat_all.shape, f32)
        dvt_all[...] = jnp.zeros(dvt_all.shape, f32)
        pbuf, dsbuf = (p0, p1), (ds0, ds1)

        def pair(n):
            return _fox_pair(n, nt, kmax, h, pq_ref, pk_ref)

        def probs(n, b, masked):
            i, j = pair(n)
            qc, kr = _tile_cols(i, t), _tile_cols(j, t)
            st = jnp.dot(ka_ref[0, kr, :], qat_ref[0, :, qc], preferred_element_type=f32)
            dpt = jnp.dot(v_ref[0, kr, :], dot_ref[0, :, qc], preferred_element_type=f32)
            if masked:
                st = jnp.where(_causal_keep(t), st, NEG_INF)
            pt = jnp.exp(st - lse_ref[0, :, qc])
            pbuf[b][...] = pt.astype(bf16)
            dsbuf[b][...] = (pt * (dpt - dl_ref[0, :, qc])).astype(bf16)

        def grads(n, b):
            i, j = pair(n)
            qc, kc = _tile_cols(i, t), _tile_cols(j, t)
            dvt_all[:, kc] += lax.dot_general(dot_ref[0, :, qc], pbuf[b][...], NT, preferred_element_type=f32)
            dkat_all[:, kc] += lax.dot_general(qat_ref[0, 0:ACC_ROWS, qc], dsbuf[b][...], NT, preferred_element_type=f32)
            dqt_all[:, qc] += jnp.dot(kat_ref[0, 0:ACC_ROWS, kc], dsbuf[b][...], preferred_element_type=f32)

        def step(n, b, masked):
            i, j = pair(n)
            qc, kr = _tile_cols(i, t), _tile_cols(j, t)
            i1, j1 = pair(n - 1)
            qc1, kc1 = _tile_cols(i1, t), _tile_cols(j1, t)
            c = 1 - b
            st = jnp.dot(ka_ref[0, kr, :], qat_ref[0, :, qc], preferred_element_type=f32)
            dvt_all[:, kc1] += lax.dot_general(dot_ref[0, :, qc1], pbuf[c][...], NT, preferred_element_type=f32)
            if masked:
                st = jnp.where(_causal_keep(t), st, NEG_INF)
            pt = jnp.exp(st - lse_ref[0, :, qc])
            pbuf[b][...] = pt.astype(bf16)
            dpt = jnp.dot(v_ref[0, kr, :], dot_ref[0, :, qc], preferred_element_type=f32)
            dkat_all[:, kc1] += lax.dot_general(qat_ref[0, 0:ACC_ROWS, qc1], dsbuf[c][...], NT, preferred_element_type=f32)
            dqt_all[:, qc1] += jnp.dot(kat_ref[0, 0:ACC_ROWS, kc1], dsbuf[c][...], preferred_element_type=f32)
            dsbuf[b][...] = (pt * (dpt - dl_ref[0, :, qc])).astype(bf16)

        probs(0, 0, True)
        step(1, 1, True)

        def four_steps(n, masked):
            step(n, 0, masked)
            step(n + 1, 1, masked)
            step(n + 2, 0, masked)
            step(n + 3, 1, masked)

        def diag_quads(d, _):
            four_steps(2 + 4 * d, True)
            return 0

        lax.fori_loop(0, (nt - 2) // 4, diag_quads, 0)
        if (nt - 2) % 4:
            step(nt - 2, 0, True)
            step(nt - 1, 1, True)

        def off_quads(d, _):
            four_steps(nt + 4 * d, False)
            return 0

        quads = extra // 4
        lax.fori_loop(0, quads, off_quads, 0)

        def off_steps(d, _):
            n = nt + 4 * quads + 2 * d
            step(n, 0, False)
            step(n + 1, 1, False)
            return 0

        lax.fori_loop(0, (extra % 4) // 2, off_steps, 0)

        @pl.when(extra % 2 == 1)
        def _():
            step(total - 1, 0, False)
            grads(total - 1, 0)

        @pl.when(extra % 2 == 0)
        def _():
            grads(total - 1, 1)

        dq_ref[0] = (dqt_all[0:HEAD_DIM, :] * SCALE).astype(bf16)
        dk_ref[0] = dkat_all[0:HEAD_DIM, :].astype(bf16)
        dv_ref[0] = dvt_all[...].astype(bf16)
        dcq_ref[0] = dqt_all[cq_slot:cq_slot + 1, :]
        dck_ref[0] = dkat_all[ck_slot:ck_slot + 1, :]

    smem = pl.BlockSpec(memory_space=pltpu.SMEM)
    rows = pl.BlockSpec((1, s_len, AUG), lambda h: (h, 0, 0))
    feat = pl.BlockSpec((1, AUG, s_len), lambda h: (h, 0, 0))
    feat64 = pl.BlockSpec((1, HEAD_DIM, s_len), lambda h: (h, 0, 0))
    rowv = pl.BlockSpec((1, 1, s_len), lambda h: (h, 0, 0))
    return pl.pallas_call(
        body,
        name="fox_bwd",
        grid=(nh,),
        in_specs=[smem, smem, smem, rows, feat, pl.BlockSpec((1, s_len, HEAD_DIM), lambda h: (h, 0, 0)), feat, feat64,
                  rowv, rowv],
        out_specs=[feat64, feat64, feat64, rowv, rowv],
        out_shape=[_sds((nh, HEAD_DIM, s_len), bf16)] * 3 + [_sds((nh, 1, s_len), f32)] * 2,
        scratch_shapes=[pltpu.VMEM((ACC_ROWS, s_len), f32), pltpu.VMEM((ACC_ROWS, s_len), f32),
                        pltpu.VMEM((HEAD_DIM, s_len), f32)]
                       + [pltpu.VMEM((t, t), bf16)] * 4,
        compiler_params=_params(("arbitrary",)),
    )(npairs, pair_q, pair_k, ka, kat, v, qat, dot, lse_row, dl_row)


def _swa_bwd_call(qt, k, kt, v, dot, lse, dl, bias_t, bias0_t, sink):
    s_len = qt.shape[2]
    ts = SWA_TS
    nb = ts // BLOCK
    nsteps = s_len // ts

    def body(qt_ref, kc_ref, kp_ref, ktc_ref, ktp_ref, vc_ref, vp_ref, dot_ref, lse_ref, dl_ref, b_ref, b0_ref,
             sink_ref, dq_ref, dk_ref, dv_ref, dbias_ref, dsink_ref, dk_s, dv_s, tail_k, tail_v, sk_s):
        n = pl.program_id(0)

        @pl.when(n == 0)
        def _():
            dbias_ref[...] = jnp.zeros_like(dbias_ref)
            sk_s[...] = jnp.zeros_like(sk_s)

        @pl.when(n < nsteps)
        def _():
            first = n == 0
            dk_s[...] = jnp.zeros_like(dk_s)
            dv_s[...] = jnp.zeros_like(dv_s)
            groups = range(SWA_KV_HEADS)
            kall = [jnp.concatenate([kp_ref[g], kc_ref[g]], axis=0) for g in groups]
            vall = [jnp.concatenate([vp_ref[g], vc_ref[g]], axis=0) for g in groups]
            ktall = [jnp.concatenate([ktp_ref[g], ktc_ref[g]], axis=1) for g in groups]
            sinks = [_sink_row(sink_ref, g) for g in groups]
            items = [(g, b) for g in groups for b in range(nb)]

            def products(g, b):
                cols = slice(b * BLOCK, (b + 1) * BLOCK)
                win = slice(b * BLOCK, (b + 2) * BLOCK)
                qg = _group_lanes(qt_ref, g, cols)
                dog = _group_lanes(dot_ref, g, cols)
                bias_b = b_ref[g]
                if b == 0:
                    bias_b = jnp.where(first, b0_ref[g], bias_b)
                st = jnp.dot(kall[g][win], qg, preferred_element_type=f32) + bias_b
                dpt = jnp.dot(vall[g][win], dog, preferred_element_type=f32)
                return qg, dog, st, dpt

            def finish(g, b, qg, dog, st, dpt):
                cols = slice(b * BLOCK, (b + 1) * BLOCK)
                win = slice(b * BLOCK, (b + 2) * BLOCK)
                lse_r = _group_lanes(lse_ref, g, cols)
                dl_r = _group_lanes(dl_ref, g, cols)
                pt = jnp.exp(st - lse_r)
                dst = pt * (dpt - dl_r)
                dsb = dst.astype(bf16)
                dk_s[g, :, win] += lax.dot_general(qg, dsb, NT, preferred_element_type=f32)
                dv_s[g, :, win] += lax.dot_general(dog, pt.astype(bf16), NT, preferred_element_type=f32)
                dqg = jnp.dot(ktall[g][:, win], dsb, preferred_element_type=f32) * SCALE
                return dqg, dst, -jnp.exp(sinks[g] - lse_r) * dl_r

            dqs, dsts, sks = {}, {}, {}
            nxt = products(*items[0])
            for idx, (g, b) in enumerate(items):
                cur = nxt
                if idx + 1 < len(items):
                    nxt = products(*items[idx + 1])
                dqs[g, b], dsts[g, b], sks[g, b] = finish(g, b, *cur)
            for g in groups:
                dbias_ref[g] += functools.reduce(lambda a, c: a + c, [dsts[g, b] for b in range(nb)])
                sk_s[g] += functools.reduce(lambda a, c: a + c, [sks[g, b] for b in range(nb)])
                for hh in range(SWA_GROUP):
                    lanes = slice(hh * BLOCK, (hh + 1) * BLOCK)
                    dq_ref[g * SWA_GROUP + hh] = jnp.concatenate(
                        [dqs[g, b][:, lanes] for b in range(nb)], axis=1).astype(bf16)

        @pl.when(n > 0)
        def _():
            last = slice(ts - BLOCK, ts)
            for g in range(SWA_KV_HEADS):
                add_k = jnp.where(n < nsteps, dk_s[g, :, 0:BLOCK], 0.0)
                add_v = jnp.where(n < nsteps, dv_s[g, :, 0:BLOCK], 0.0)
                dk_ref[g, :, 0:ts - BLOCK] = tail_k[g, :, 0:ts - BLOCK].astype(bf16)
                dv_ref[g, :, 0:ts - BLOCK] = tail_v[g, :, 0:ts - BLOCK].astype(bf16)
                dk_ref[g, :, last] = (tail_k[g, :, last] + add_k).astype(bf16)
                dv_ref[g, :, last] = (tail_v[g, :, last] + add_v).astype(bf16)

        @pl.when(n < nsteps)
        def _():
            tail_k[...] = dk_s[:, :, BLOCK:]
            tail_v[...] = dv_s[:, :, BLOCK:]

        @pl.when(n == nsteps)
        def _():
            row = lax.broadcasted_iota(jnp.int32, (SWA_HEADS, 128), 0)
            out = jnp.zeros((SWA_HEADS, 128), f32)
            for h in range(SWA_HEADS):
                g, hh = divmod(h, SWA_GROUP)
                val = jnp.sum(sk_s[g, :, hh * BLOCK:(hh + 1) * BLOCK], axis=1, keepdims=True)
                out = jnp.where(row == h, val, out)
            dsink_ref[...] = out

    last_step = nsteps - 1

    def cl(n):
        return jnp.minimum(n, last_step)

    def prev_blk(n):
        return jnp.maximum(cl(n) * nb - 1, 0)

    feat8 = pl.BlockSpec((SWA_HEADS, HEAD_DIM, ts), lambda n: (0, 0, cl(n)))
    rows8 = pl.BlockSpec((SWA_HEADS, 1, ts), lambda n: (0, 0, cl(n)))
    cur = pl.BlockSpec((SWA_KV_HEADS, ts, HEAD_DIM), lambda n: (0, cl(n), 0))
    prev = pl.BlockSpec((SWA_KV_HEADS, BLOCK, HEAD_DIM), lambda n: (0, prev_blk(n), 0))
    curt = pl.BlockSpec((SWA_KV_HEADS, HEAD_DIM, ts), lambda n: (0, 0, cl(n)))
    prevt = pl.BlockSpec((SWA_KV_HEADS, HEAD_DIM, BLOCK), lambda n: (0, 0, prev_blk(n)))
    bspec = pl.BlockSpec((SWA_KV_HEADS, 2 * BLOCK, SWA_W), lambda n: (0, 0, 0))
    kvout = pl.BlockSpec((SWA_KV_HEADS, HEAD_DIM, ts), lambda n: (0, 0, jnp.maximum(n - 1, 0)))
    return pl.pallas_call(
        body,
        name="swa_bwd",
        grid=(nsteps + 1,),
        in_specs=[feat8, cur, prev, curt, prevt, cur, prev, feat8, rows8, rows8, bspec, bspec,
                  pl.BlockSpec(memory_space=pltpu.SMEM)],
        out_specs=[feat8, kvout, kvout, bspec, pl.BlockSpec((SWA_HEADS, 128), lambda n: (0, 0))],
        out_shape=[_sds((SWA_HEADS, HEAD_DIM, s_len), bf16), _sds((SWA_KV_HEADS, HEAD_DIM, s_len), bf16),
                   _sds((SWA_KV_HEADS, HEAD_DIM, s_len), bf16),
                   _sds((SWA_KV_HEADS, 2 * BLOCK, SWA_W), f32), _sds((SWA_HEADS, 128), f32)],
        scratch_shapes=[pltpu.VMEM((SWA_KV_HEADS, HEAD_DIM, ts + BLOCK), f32),
                        pltpu.VMEM((SWA_KV_HEADS, HEAD_DIM, ts + BLOCK), f32),
                        pltpu.VMEM((SWA_KV_HEADS, HEAD_DIM, ts), f32),
                        pltpu.VMEM((SWA_KV_HEADS, HEAD_DIM, ts), f32),
                        pltpu.VMEM((SWA_KV_HEADS, 1, SWA_W), f32)],
        compiler_params=_params(("arbitrary",)),
    )(qt, k, k, kt, kt, v, v, dot, lse, dl, bias_t, bias0_t, sink)


def _dproj_specs(tm):
    half = pl.BlockSpec((tm, 512), lambda i: (i, 0))
    feat = pl.BlockSpec((512, tm), lambda i: (0, i))
    feat_kv = pl.BlockSpec((128, tm), lambda i: (0, i))
    return [feat, feat, feat, half, feat, feat_kv, feat_kv, half, feat_kv]


def _dx_exchange_call(dh, pieces, w_t, bs, tm):
    s_len = dh.shape[0]
    n = len(bs)
    last = s_len // tm - 1

    def body(*refs):
        dh_ref, dqf_ref, dkf_ref, dvf_ref, dfz_ref, dqs_ref, dks_ref, dvs_ref, dsz_ref, dfft_ref, w_ref = refs[:11]
        b_refs = refs[11:11 + n]
        dx_ref = refs[11 + n]
        r_refs = refs[12 + n:12 + 2 * n]
        sems = refs[12 + 2 * n:]
        i = pl.program_id(0)

        @pl.when(i == 0)
        def _():
            _exchange_start(b_refs, r_refs, sems)

        def tr(ref):
            return ref[...].astype(f32).T.astype(bf16)

        dp = jnp.concatenate([tr(dqf_ref), tr(dkf_ref), tr(dvf_ref), dfz_ref[...], tr(dqs_ref), tr(dks_ref),
                              tr(dvs_ref), dsz_ref[...], tr(dfft_ref)], axis=1)
        dx_ref[...] = ALPHA * dh_ref[...] + jnp.dot(dp, w_ref[...], preferred_element_type=f32)

        @pl.when(i == last)
        def _():
            _exchange_wait(b_refs, r_refs, sems)

    fullw = pl.BlockSpec((tm, D_MODEL), lambda i: (i, 0))
    any_spec = pl.BlockSpec(memory_space=pl.ANY)
    out = pl.pallas_call(
        body,
        name="dx_bwd_exchange",
        grid=(s_len // tm,),
        in_specs=[fullw] + _dproj_specs(tm) + [pl.BlockSpec((A_W, D_MODEL), lambda i: (0, 0))] + [any_spec] * n,
        out_specs=[fullw] + [any_spec] * n,
        out_shape=[_sds((s_len, D_MODEL), f32)] + [_sds(b.shape, b.dtype) for b in bs],
        scratch_shapes=[pltpu.SemaphoreType.DMA((7 * n,)), pltpu.SemaphoreType.DMA((7 * n,)),
                        pltpu.SemaphoreType.DMA((n,))],
        compiler_params=_params(("arbitrary",)),
    )(dh, *pieces, w_t, *bs)
    return out[0], out[1:]


DW_STAGE_ROWS = 384


def _dw_exchange_call(x2, pieces, bs, tm):
    s_len = x2.shape[0]
    nt = s_len // tm
    n = len(bs)

    def body(*refs):
        x_ref, dqf_ref, dkf_ref, dvf_ref, dfz_ref, dqs_ref, dks_ref, dvs_ref, dsz_ref, dfft_ref = refs[:10]
        b_refs = refs[10:10 + n]
        dw_ref = refs[10 + n]
        r_refs = refs[11 + n:11 + 2 * n]
        acc_ref, stage_ref, sem = refs[11 + 2 * n:14 + 2 * n]
        sems = refs[14 + 2 * n:]
        i = pl.program_id(0)

        @pl.when(i == 0)
        def _():
            _exchange_start(b_refs, r_refs, sems)
            acc_ref[...] = jnp.zeros_like(acc_ref)

        xb = x_ref[...].astype(bf16)

        def add_feat(off, lhs):
            acc_ref[off:off + lhs.shape[0], :] += jnp.dot(lhs, xb, preferred_element_type=f32)

        def add_rows(off, piece):
            acc_ref[off:off + piece.shape[1], :] += lax.dot_general(piece, xb, TN, preferred_element_type=f32)

        add_feat(A_FQ, dqf_ref[...])
        add_feat(A_FK, dkf_ref[...])
        add_feat(A_FV, dvf_ref[...])
        add_rows(A_FZ, dfz_ref[...])
        add_feat(A_SQ, dqs_ref[...])
        add_feat(A_SK, dks_ref[...])
        add_feat(A_SV, dvs_ref[...])
        add_rows(A_SZ, dsz_ref[...])
        add_feat(A_FF, dfft_ref[...].astype(bf16))

        @pl.when(i == nt - 1)
        def _():
            for r in range(A_W // DW_STAGE_ROWS):
                rows = slice(r * DW_STAGE_ROWS, (r + 1) * DW_STAGE_ROWS)
                stage_ref[...] = acc_ref[rows, :].astype(bf16)
                cp = pltpu.make_async_copy(stage_ref, dw_ref.at[rows, :], sem)
                cp.start()
                cp.wait()
            _exchange_wait(b_refs, r_refs, sems)

    any_spec = pl.BlockSpec(memory_space=pl.ANY)
    out = pl.pallas_call(
        body,
        name="dw_bwd_exchange",
        grid=(nt,),
        in_specs=[pl.BlockSpec((tm, D_MODEL), lambda i: (i, 0))] + _dproj_specs(tm) + [any_spec] * n,
        out_specs=[any_spec] * (1 + n),
        out_shape=[_sds((A_W, D_MODEL), bf16)] + [_sds(b.shape, b.dtype) for b in bs],
        scratch_shapes=[pltpu.VMEM((A_W, D_MODEL), f32), pltpu.VMEM((DW_STAGE_ROWS, D_MODEL), bf16),
                        pltpu.SemaphoreType.DMA, pltpu.SemaphoreType.DMA((7 * n,)), pltpu.SemaphoreType.DMA((7 * n,)),
                        pltpu.SemaphoreType.DMA((n,))],
        compiler_params=_params(("arbitrary",), VMEM_LIMIT_BIG),
    )(x2, *pieces, *bs)
    return out[0], out[1:]


def _adam_call(recv, w, m, v, tc, name):
    rows, cols = w.shape

    def body(r_ref, w_ref, m_ref, v_ref, g_ref, d_ref, mo_ref, vo_ref):
        g = r_ref[0].astype(f32)
        for p in range(1, N_DEV):
            g = g + r_ref[p].astype(f32)
        mn = ADAM_B1 * m_ref[...] + (1.0 - ADAM_B1) * g
        vn = ADAM_B2 * v_ref[...] + (1.0 - ADAM_B2) * (g * g)
        m_hat = mn / (1.0 - ADAM_B1 ** ADAM_STEP)
        v_hat = vn / (1.0 - ADAM_B2 ** ADAM_STEP)
        g_ref[...] = g
        d_ref[...] = -ADAM_LR * (m_hat / (jnp.sqrt(v_hat) + ADAM_EPS) + ADAM_WD * w_ref[...])
        mo_ref[...] = mn
        vo_ref[...] = vn

    blk = pl.BlockSpec((rows, tc), lambda i: (0, i))
    return pl.pallas_call(
        body,
        name=name,
        grid=(cols // tc,),
        in_specs=[pl.BlockSpec((N_DEV, rows, tc), lambda i: (0, 0, i)), blk, blk, blk],
        out_specs=[blk] * 4,
        out_shape=[_sds((rows, cols), f32)] * 4,
        compiler_params=_params(("arbitrary",)),
    )(recv, w, m, v)


def _rows_to_shards(parts, shard):
    blocks = []
    for d in range(N_DEV):
        lo, hi, start, pieces = d * shard, (d + 1) * shard, 0, []
        for part in parts:
            a, b = max(lo, start), min(hi, start + part.shape[0])
            if a < b:
                pieces.append(part[a - start:b - start])
            start += part.shape[0]
        blocks.append(jnp.concatenate(pieces, axis=0))
    return jnp.stack(blocks)


def _pad_cols(a, width=128):
    return jnp.pad(a, ((0, 0), (0, width - a.shape[1])))


def _pack_small(ln_g, ln_b, rel, b_f, sink):
    return jnp.concatenate([
        ln_g.reshape(8, 128), ln_b.reshape(8, 128), _pad_cols(rel),
        jnp.pad(_pad_cols(b_f), ((0, 7), (0, 0))), jnp.pad(_pad_cols(sink), ((0, 7), (0, 0)))], axis=0)


def _unpack_small(p):
    return (p[0:8].reshape(1, D_MODEL), p[8:16].reshape(1, D_MODEL), p[16:48, 0:8], p[48:49, 0:8], p[56:57, 0:8])


def kernel(x, w_in, b_f, rel_bias, sink, w_o, ln_g, ln_b, loss_target, m_w_in, m_b_f, m_rel_bias, m_sink, m_w_o, m_ln_g, m_ln_b, v_w_in, v_b_f, v_rel_bias, v_sink, v_w_o, v_ln_g, v_ln_b):
    x2 = x[0]
    tgt = loss_target[0]
    s_len = x2.shape[0]
    shard = w_in.shape[2]

    w_in_t = jnp.transpose(w_in[0])
    (g_in,) = _gather_call([w_in_t.astype(bf16)])
    wt_full = g_in.reshape(N_DEV * shard, D_MODEL)
    w_t = jnp.concatenate([wt_full[:O_FF0], wt_full[O_FF1:], wt_full[O_FF0:O_FF1],
                           jnp.zeros((A_W - D_IN, D_MODEL), bf16)], axis=0)

    wo_blocks = jnp.broadcast_to(w_o[0].astype(bf16)[None], (N_DEV,) + w_o.shape[1:])
    (qft, kft, vf, fz, qst, ks, vs, sz, fft, vat, kst, vsta), (g_o,) = _proj_call(x2, w_t, [wo_blocks], 512)
    wo_full = g_o.reshape(D_MODEL, D_MODEL)
    cum, sgm = _cum_call(fft, b_f.reshape(FOX_HEADS, 1))
    qat, ka, kat, tile_stats = _augment_call(qft, kft, cum.reshape(FOX_HEADS, 1, s_len), 2048)
    npairs, pair_q, pair_k = _fox_prune_tables(tile_stats)
    o_ft, lse_f = _fox_fwd_call(qat, ka, vat, npairs, pair_q, pair_k)
    bucket_t = jnp.asarray(_t5_bucket_table().T)
    bias_t, bias0_t = _swa_bias_call(rel_bias, bucket_t)
    sink_v = sink.reshape(SWA_HEADS)
    o_st, lse_s = _swa_fwd_call(qst, ks, vsta, bias_t, bias0_t, sink_v)

    (dh, do_f, dfz, do_s, dsz, dl_f, dl_s, dwo, dg, db, loss_part) = _post_call(
        o_ft.reshape(FOX_HEADS * HEAD_DIM, s_len), fz, o_st.reshape(SWA_HEADS * HEAD_DIM, s_len), sz, x2, tgt,
        wo_full, ln_g, ln_b, jnp.asarray(_head_selector()).astype(bf16), 512)

    dqf, dkf, dvf, dcq, dck = _fox_bwd_call(ka, kat, vf, qat, do_f.reshape(FOX_HEADS, HEAD_DIM, s_len), lse_f,
                                            dl_f.reshape(FOX_HEADS, 1, s_len), npairs, pair_q, pair_k)
    dqf, dkf, dvf = (a.reshape(FOX_HEADS * HEAD_DIM, s_len) for a in (dqf, dkf, dvf))
    dfft, dbf = _cum_bwd_call(dcq.reshape(FOX_HEADS, s_len), dck.reshape(FOX_HEADS, s_len), sgm)
    dqs, dks, dvs, dbias, dsink = _swa_bwd_call(
        qst, ks, kst, vs, do_s.reshape(SWA_HEADS, HEAD_DIM, s_len), lse_s, dl_s.reshape(SWA_HEADS, 1, s_len),
        bias_t, bias0_t, sink_v)
    dqs = dqs.reshape(SWA_HEADS * HEAD_DIM, s_len)
    dks, dvs = (a.reshape(SWA_KV_HEADS * HEAD_DIM, s_len) for a in (dks, dvs))
    drel = _swa_bias_bwd_call(dbias, bucket_t)

    dwo_blocks = dwo.reshape(N_DEV, D_MODEL // N_DEV, D_MODEL).astype(bf16)
    small = _pack_small(dg, db, drel[:, 0:8], dbf[:, 0].reshape(1, 8), dsink[:, 0].reshape(1, 8))
    loss_slot = np.zeros((64, 128), bool)
    loss_slot[49, 0] = True
    small = jnp.where(jnp.asarray(loss_slot), loss_part[0, 0], small)
    small_blocks = jnp.broadcast_to(small[None], (N_DEV,) + small.shape)
    pieces = (dqf, dkf, dvf, dfz, dqs, dks, dvs, dsz, dfft)
    dw_t, (r_o, r_small) = _dw_exchange_call(x2, pieces, [dwo_blocks, small_blocks], 1024)
    dw_blocks = _rows_to_shards([dw_t[:O_FF0], dw_t[A_FF:A_FF + (O_FF1 - O_FF0)], dw_t[O_FF0:A_FF]], shard)
    dx, (r_in,) = _dx_exchange_call(dh, pieces, w_t, [dw_blocks], 256)

    win_t = [jnp.transpose(a) for a in _adam_call(
        r_in, w_in_t, jnp.transpose(m_w_in[0]), jnp.transpose(v_w_in[0]), 256, "adam_w_in")]
    g_win, d_win, nm_win, nv_win = win_t
    g_wo, d_wo, nm_wo, nv_wo = _adam_call(r_o, w_o[0], m_w_o[0], v_w_o[0], 256, "adam_w_o")
    p_w = _pack_small(ln_g, ln_b, rel_bias, b_f, sink)
    p_m = _pack_small(m_ln_g, m_ln_b, m_rel_bias, m_b_f, m_sink)
    p_v = _pack_small(v_ln_g, v_ln_b, v_rel_bias, v_b_f, v_sink)
    g_p, d_p, nm_p, nv_p = _adam_call(r_small, p_w, p_m, p_v, 128, "adam_small")

    loss = g_p[49, 0]
    g_lng, g_lnb, g_rel, g_bf, g_sink = _unpack_small(g_p)
    d_lng, d_lnb, d_rel, d_bf, d_sink = _unpack_small(d_p)
    m_lng, m_lnb, m_rel, m_bf, m_sk = _unpack_small(nm_p)
    v_lng, v_lnb, v_rel, v_bf, v_sk = _unpack_small(nv_p)
    return (loss, dx[None], g_win[None], g_bf, g_rel, g_sink, g_wo[None], g_lng, g_lnb,
            d_win[None], d_bf, d_rel, d_sink, d_wo[None], d_lng, d_lnb,
            nm_win[None], m_bf, m_rel, m_sk, nm_wo[None], m_lng, m_lnb,
            nv_win[None], v_bf, v_rel, v_sk, nv_wo[None], v_lng, v_lnb)
```

```python
import functools
import math

import numpy as np
import jax
import jax.numpy as jnp
from jax import lax
from jax.experimental import pallas as pl
from jax.experimental.pallas import tpu as pltpu

f32 = jnp.float32
bf16 = jnp.bfloat16

D_MODEL = 1024
HEAD_DIM = 64
FOX_HEADS = 8
SWA_HEADS = 8
SWA_KV_HEADS = 2
SWA_GROUP = 4
BLOCK = 128
NUM_BUCKETS = 32
MAX_DISTANCE = 128
LN_EPS = 1e-5
NEG_INF = -1e30
ALPHA = 2.0 ** 0.25
SCALE = 1.0 / math.sqrt(HEAD_DIM)
D_IN = 3336

ADAM_LR = 0.001
ADAM_B1 = 0.9
ADAM_B2 = 0.999
ADAM_EPS = 1e-08
ADAM_WD = 0.01
ADAM_STEP = 10

N_DEV = 8
A_FQ, A_FK, A_FV, A_FZ, A_SQ, A_SK, A_SV, A_SZ, A_FF, A_W = 0, 512, 1024, 1536, 2048, 2560, 2688, 2816, 3328, 3456
O_FF0, O_FF1 = 1536, 1544

VMEM_LIMIT = 48 * 1024 * 1024
HIGHEST = lax.Precision.HIGHEST
NT = (((1,), (1,)), ((), ()))
TN = (((0,), (0,)), ((), ()))
MESH = pl.DeviceIdType.MESH
RELS = [(0, 0, 1), (0, 1, 0), (0, 1, 1), (1, 0, 0), (1, 0, 1), (1, 1, 0), (1, 1, 1)]


VMEM_LIMIT_BIG = 60 * 1024 * 1024


def _params(sem=None, vmem=VMEM_LIMIT):
    return pltpu.CompilerParams(dimension_semantics=sem, vmem_limit_bytes=vmem)


def _sds(shape, dtype):
    return jax.ShapeDtypeStruct(shape, dtype)


def _t5_bucket_table():
    qi = np.arange(BLOCK)[:, None]
    kj = np.arange(2 * BLOCK)[None, :]
    rel = qi + BLOCK - kj
    band = (rel >= 0) & (rel < BLOCK)
    relc = np.maximum(rel, 0)
    max_exact = NUM_BUCKETS // 2
    relf = np.maximum(relc, 1).astype(np.float32)
    large = max_exact + (np.log(relf / np.float32(max_exact)) / np.float32(math.log(MAX_DISTANCE / max_exact))
                         * np.float32(NUM_BUCKETS - max_exact)).astype(np.int32)
    large = np.minimum(large, NUM_BUCKETS - 1)
    bucket = np.where(relc < max_exact, relc, large).astype(np.int32)
    bucket = np.where(band, bucket, -1).astype(np.int32)
    return bucket


def _mesh_pos():
    return lax.axis_index("x"), lax.axis_index("y"), lax.axis_index("c")


def _dev_index(p):
    return 4 * p[0] + 2 * p[1] + p[2]


def _gather_call(xs):
    n = len(xs)

    def body(*refs):
        x_refs, o_refs = refs[:n], refs[n:2 * n]
        send_sems, recv_sems, local_sems = refs[2 * n:]
        x, y, c = _mesh_pos()
        me, sib = (x, y, c), (x, y, 1 - c)
        chips = [(1 - x, y), (x, 1 - y), (1 - x, 1 - y)]

        def copy(a, k, block, to, src=None):
            slot = o_refs[a].at[_dev_index(block)]
            return pltpu.make_async_remote_copy(
                src_ref=slot if src is None else src, dst_ref=slot,
                send_sem=send_sems.at[a * 7 + k], recv_sem=recv_sems.at[a * 7 + k],
                device_id=to, device_id_type=MESH)

        mine = [pltpu.make_async_copy(x_refs[a], o_refs[a].at[_dev_index(me)], local_sems.at[a]) for a in range(n)]
        for cp in mine:
            cp.start()
        first = []
        for a in range(n):
            first.append(copy(a, 0, me, sib, src=x_refs[a]))
            first += [copy(a, 1 + j, me, (*chip, c), src=x_refs[a]) for j, chip in enumerate(chips)]
        for cp in first:
            cp.start()
        passed = []
        for j, chip in enumerate(chips):
            for a in range(n):
                copy(a, 1 + j, (*chip, c), me).wait_recv()
                fwd = copy(a, 4 + j, (*chip, c), sib)
                fwd.start()
                passed.append(fwd)
        for a in range(n):
            copy(a, 0, sib, me).wait_recv()
            for j, chip in enumerate(chips):
                copy(a, 4 + j, (*chip, 1 - c), me).wait_recv()
        for cp in first + passed:
            cp.wait_send()
        for cp in mine:
            cp.wait()

    any_spec = pl.BlockSpec(memory_space=pl.ANY)
    return pl.pallas_call(
        body,
        name="gather_weights",
        out_shape=[_sds((N_DEV,) + a.shape, a.dtype) for a in xs],
        in_specs=[any_spec] * n,
        out_specs=[any_spec] * n,
        scratch_shapes=[pltpu.SemaphoreType.DMA((7 * n,)), pltpu.SemaphoreType.DMA((7 * n,)),
                        pltpu.SemaphoreType.DMA((n,))],
    )(*xs)


def _exchange_copies(b_refs, r_refs, send_sems, recv_sems, local_sems, incoming):
    n = len(b_refs)
    x, y, c = _mesh_pos()
    me_idx = _dev_index((x, y, c))
    mine = [pltpu.make_async_copy(b_refs[a].at[me_idx], r_refs[a].at[me_idx], local_sems.at[a]) for a in range(n)]
    remote = []
    for k, r in enumerate(RELS):
        peer = ((1 - x) if r[0] else x, (1 - y) if r[1] else y, (1 - c) if r[2] else c)
        pidx = _dev_index(peer)
        for a in range(n):
            remote.append(pltpu.make_async_remote_copy(
                src_ref=b_refs[a].at[pidx], dst_ref=r_refs[a].at[pidx if incoming else me_idx],
                send_sem=send_sems.at[a * 7 + k], recv_sem=recv_sems.at[a * 7 + k],
                device_id=peer, device_id_type=MESH))
    return mine, remote


def _exchange_start(b_refs, r_refs, sems):
    mine, out = _exchange_copies(b_refs, r_refs, *sems, incoming=False)
    for cp in mine + out:
        cp.start()


def _exchange_wait(b_refs, r_refs, sems):
    mine, inc = _exchange_copies(b_refs, r_refs, *sems, incoming=True)
    for cp in inc:
        cp.wait_recv()
    for cp in inc:
        cp.wait_send()
    for cp in mine:
        cp.wait()


def _proj_call(x2, w_t, bs, tm):
    s_len = x2.shape[0]
    n = len(bs)
    last = s_len // tm - 1

    def body(*refs):
        x_ref, w_ref = refs[:2]
        b_refs = refs[2:2 + n]
        (qft_ref, kft_ref, vf_ref, fz_ref, qst_ref, ks_ref, vs_ref, sz_ref, fft_ref, vat_ref,
         kst_ref, vsta_ref) = refs[2 + n:14 + n]
        r_refs = refs[14 + n:14 + 2 * n]
        sems = refs[14 + 2 * n:]

        @pl.when(pl.program_id(0) == 0)
        def _():
            _exchange_start(b_refs, r_refs, sems)

        xb = x_ref[...].astype(bf16)

        def seg_t(off, width):
            return lax.dot_general(w_ref[off:off + width, :], xb, NT, preferred_element_type=f32)

        def seg(off, width):
            return lax.dot_general(xb, w_ref[off:off + width, :], NT, preferred_element_type=f32)

        def put_heads(ref, acc, nheads):
            for h in range(nheads):
                ref[h] = acc[:, h * HEAD_DIM:(h + 1) * HEAD_DIM].astype(bf16)

        def put_heads_t(ref, acc_t, nheads, augment):
            for h in range(nheads):
                ref[h, 0:HEAD_DIM, :] = acc_t[h * HEAD_DIM:(h + 1) * HEAD_DIM, :].astype(bf16)
                if augment:
                    ref[h, HEAD_DIM:2 * HEAD_DIM, :] = ones_row

        ones_row = jnp.where(lax.broadcasted_iota(jnp.int32, (HEAD_DIM, tm), 0) == 0, 1.0, 0.0).astype(bf16)
        put_heads_t(vat_ref, seg_t(A_FV, 512), FOX_HEADS, True)
        put_heads_t(qft_ref, seg_t(A_FQ, 512) * SCALE, FOX_HEADS, False)
        put_heads_t(kft_ref, seg_t(A_FK, 512), FOX_HEADS, False)
        put_heads(vf_ref, seg(A_FV, 512), FOX_HEADS)
        fz_ref[...] = seg(A_FZ, 512)
        put_heads_t(qst_ref, seg_t(A_SQ, 512) * SCALE, SWA_HEADS, False)
        put_heads(ks_ref, seg(A_SK, 128), SWA_KV_HEADS)
        put_heads(vs_ref, seg(A_SV, 128), SWA_KV_HEADS)
        put_heads_t(kst_ref, seg_t(A_SK, 128), SWA_KV_HEADS, False)
        put_heads_t(vsta_ref, seg_t(A_SV, 128), SWA_KV_HEADS, True)
        sz_ref[...] = seg(A_SZ, 512)
        fft_ref[...] = seg(A_FF, 128).T[:FOX_HEADS, :]

        @pl.when(pl.program_id(0) == last)
        def _():
            _exchange_wait(b_refs, r_refs, sems)

    def heads(nh):
        return pl.BlockSpec((nh, tm, HEAD_DIM), lambda i: (0, i, 0))

    def feat(nh, rows):
        return pl.BlockSpec((nh, rows, tm), lambda i: (0, 0, i))

    wide = pl.BlockSpec((tm, 512), lambda i: (i, 0))
    any_spec = pl.BlockSpec(memory_space=pl.ANY)
    out = pl.pallas_call(
        body,
        name="proj_fwd_gather",
        grid=(s_len // tm,),
        in_specs=[pl.BlockSpec((tm, D_MODEL), lambda i: (i, 0)), pl.BlockSpec((A_W, D_MODEL), lambda i: (0, 0))]
                 + [any_spec] * n,
        out_specs=[feat(8, HEAD_DIM), feat(8, HEAD_DIM), heads(8), wide, feat(8, HEAD_DIM), heads(2), heads(2), wide,
                   pl.BlockSpec((FOX_HEADS, tm), lambda i: (0, i)),
                   feat(FOX_HEADS, 2 * HEAD_DIM), feat(2, HEAD_DIM), feat(2, 2 * HEAD_DIM)] + [any_spec] * n,
        out_shape=[_sds((8, HEAD_DIM, s_len), bf16)] * 2 + [_sds((8, s_len, HEAD_DIM), bf16)]
                  + [_sds((s_len, 512), f32), _sds((8, HEAD_DIM, s_len), bf16),
                     _sds((2, s_len, HEAD_DIM), bf16), _sds((2, s_len, HEAD_DIM), bf16), _sds((s_len, 512), f32),
                     _sds((FOX_HEADS, s_len), f32), _sds((FOX_HEADS, 2 * HEAD_DIM, s_len), bf16),
                     _sds((2, HEAD_DIM, s_len), bf16), _sds((2, 2 * HEAD_DIM, s_len), bf16)]
                  + [_sds(b.shape, b.dtype) for b in bs],
        scratch_shapes=[pltpu.SemaphoreType.DMA((7 * n,)), pltpu.SemaphoreType.DMA((7 * n,)),
                        pltpu.SemaphoreType.DMA((n,))],
        compiler_params=_params(("arbitrary",)),
    )(x2, w_t, *bs)
    return out[:12], out[12:]


AUG = 2 * HEAD_DIM
NEAR_KEYS = 3


def _augment_call(q_t, k_t, cum_row, tm):
    nh, _, s_len = k_t.shape
    per_step = tm // FOX_T

    def body(qt_ref, kt_ref, c_ref, qat_ref, ka_ref, kat_ref, st_ref):
        c = c_ref[0]
        hi = c.astype(bf16).astype(f32)
        r1 = c - hi
        mid = r1.astype(bf16).astype(f32)
        lo = (r1 - mid).astype(bf16).astype(f32)
        row = lax.broadcasted_iota(jnp.int32, (HEAD_DIM, tm), 0)
        q_tail = jnp.where(row == 0, hi, jnp.where(row == 1, mid, jnp.where(row == 2, lo,
                           jnp.where(row < 6, 1.0, 0.0))))
        k_tail = jnp.where(row < 3, 1.0, jnp.where(row == 3, -hi, jnp.where(row == 4, -mid,
                           jnp.where(row == 5, -lo, 0.0))))
        qat_ref[0, 0:HEAD_DIM, :] = qt_ref[0]
        qat_ref[0, HEAD_DIM:AUG, :] = q_tail.astype(bf16)
        kat_ref[0, 0:HEAD_DIM, :] = kt_ref[0]
        kat_ref[0, HEAD_DIM:AUG, :] = k_tail.astype(bf16)
        qt = qt_ref[0].astype(f32)
        kt = kt_ref[0].astype(f32)
        ka_ref[0] = jnp.concatenate([kt, k_tail], axis=0).T.astype(bf16)
        qn2 = jnp.sum(qt * qt, axis=0, keepdims=True)
        kn2 = jnp.sum(kt * kt, axis=0, keepdims=True)
        sd = jnp.sum(qt * kt, axis=0, keepdims=True)
        k_and_c = jnp.concatenate([kt, jnp.broadcast_to(c, (8, tm))], axis=0)
        lane = lax.broadcasted_iota(jnp.int32, (1, tm), 1)
        for shift in range(1, NEAR_KEYS + 1):
            prev = pltpu.roll(k_and_c, shift, axis=1)
            near = jnp.sum(qt * prev[0:HEAD_DIM], axis=0, keepdims=True) + (c - prev[HEAD_DIM:HEAD_DIM + 1])
            sd = jnp.maximum(sd, jnp.where(lane >= shift, near, NEG_INF))
        srow = lax.broadcasted_iota(jnp.int32, (8, LANES), 0)
        for part in range(per_step):
            sl = slice(part * FOX_T, (part + 1) * FOX_T)
            vals = [jnp.sqrt(jnp.max(qn2[:, sl], axis=1, keepdims=True)),
                    jnp.sqrt(jnp.max(kn2[:, sl], axis=1, keepdims=True)),
                    jnp.min(sd[:, sl], axis=1, keepdims=True),
                    jnp.max(c[:, sl], axis=1, keepdims=True), jnp.min(c[:, sl], axis=1, keepdims=True)]
            out = jnp.zeros((8, LANES), f32)
            for r, val in enumerate(vals):
                out = jnp.where(srow == r, val, out)
            st_ref[0, part] = out

    tile_t = pl.BlockSpec((1, HEAD_DIM, tm), lambda h, i: (h, 0, i))
    return pl.pallas_call(
        body,
        name="fox_augment",
        grid=(nh, s_len // tm),
        in_specs=[tile_t, tile_t, pl.BlockSpec((1, 1, tm), lambda h, i: (h, 0, i))],
        out_specs=[pl.BlockSpec((1, AUG, tm), lambda h, i: (h, 0, i)),
                   pl.BlockSpec((1, tm, AUG), lambda h, i: (h, i, 0)),
                   pl.BlockSpec((1, AUG, tm), lambda h, i: (h, 0, i)),
                   pl.BlockSpec((1, per_step, 8, LANES), lambda h, i: (h, i, 0, 0))],
        out_shape=[_sds((nh, AUG, s_len), bf16), _sds((nh, s_len, AUG), bf16), _sds((nh, AUG, s_len), bf16),
                   _sds((nh, s_len // FOX_T, 8, LANES), f32)],
        compiler_params=_params(("arbitrary", "arbitrary")),
    )(q_t, k_t, cum_row)


FOX_PRUNE_GAP = 32.0


def _fox_prune_tables(stats):
    s = stats[:, :, :, 0]
    qn, kn, sd, cmx, cmn = (s[:, :, r] for r in range(5))
    nt = s.shape[1]
    bound = qn[:, :, None] * kn[:, None, :] + (cmx[:, :, None] - cmn[:, None, :])
    margin = 0.01 + 1e-5 * (jnp.abs(cmx)[:, :, None] + jnp.abs(cmn)[:, None, :])
    qi = lax.broadcasted_iota(jnp.int32, (nt, nt), 0)
    kj = lax.broadcasted_iota(jnp.int32, (nt, nt), 1)
    skip = (bound + margin < sd[:, :, None] - FOX_PRUNE_GAP) & (kj < qi)[None]
    first = jnp.sum(jnp.cumprod(skip.astype(jnp.int32), axis=2), axis=2)
    tiles = lax.broadcasted_iota(jnp.int32, (1, nt), 1)
    cnt = tiles - first
    ends = jnp.cumsum(cnt, axis=1)
    off = ends - cnt
    kmax = nt * (nt - 1) // 2
    k = lax.broadcasted_iota(jnp.int32, (1, kmax), 1)
    pair_q = jnp.minimum(jnp.sum((ends[:, None, :] <= k[:, :, None]).astype(jnp.int32), axis=2), nt - 1)
    hit = pair_q[:, :, None] == tiles[:, None, :]
    first_k = jnp.sum(jnp.where(hit, first[:, None, :], 0), axis=2)
    off_k = jnp.sum(jnp.where(hit, off[:, None, :], 0), axis=2)
    pair_k = jnp.clip(first_k + k - off_k, 0, nt - 1)
    return (ends[:, nt - 1].astype(jnp.int32), pair_q.reshape(-1).astype(jnp.int32),
            pair_k.reshape(-1).astype(jnp.int32))


CUM_CHUNK = 512


def _cum_call(fft, bf_col):
    s_len = fft.shape[1]
    ch = CUM_CHUNK

    def body(f_ref, b_ref, cum_ref, sg_ref):
        r = lax.broadcasted_iota(jnp.int32, (ch, ch), 0)
        c = lax.broadcasted_iota(jnp.int32, (ch, ch), 1)
        upper = (r <= c).astype(f32)
        carry = jnp.zeros((FOX_HEADS, 1), f32)
        for n in range(s_len // ch):
            z = f_ref[:, n * ch:(n + 1) * ch] + b_ref[...]
            logf = jnp.minimum(z, 0.0) - jnp.log1p(jnp.exp(-jnp.abs(z)))
            sg_ref[:, n * ch:(n + 1) * ch] = 1.0 / (1.0 + jnp.exp(z))
            cs = jnp.dot(logf, upper, precision=HIGHEST, preferred_element_type=f32) + carry
            cum_ref[:, n * ch:(n + 1) * ch] = cs
            carry = cs[:, ch - 1:ch]

    return pl.pallas_call(
        body,
        name="fox_cum_fwd",
        out_shape=[_sds((FOX_HEADS, s_len), f32)] * 2,
        compiler_params=_params(),
    )(fft, bf_col)


def _cum_bwd_call(dcq, dck, sg):
    s_len = sg.shape[1]
    ch = CUM_CHUNK
    nch = s_len // ch

    def body(q_ref, k_ref, sg_ref, dff_ref, dbf_ref):
        r = lax.broadcasted_iota(jnp.int32, (ch, ch), 0)
        c = lax.broadcasted_iota(jnp.int32, (ch, ch), 1)
        lower = (r >= c).astype(f32)
        dff_ref[...] = jnp.zeros_like(dff_ref)
        carry = jnp.zeros((FOX_HEADS, 1), f32)
        total = jnp.zeros((FOX_HEADS, 1), f32)
        for n in reversed(range(nch)):
            sl = slice(n * ch, (n + 1) * ch)
            dcum = q_ref[:, sl] - k_ref[:, sl]
            rs = jnp.dot(dcum, lower, precision=HIGHEST, preferred_element_type=f32) + carry
            carry = rs[:, 0:1]
            dff = rs * sg_ref[:, sl]
            dff_ref[0:FOX_HEADS, sl] = dff
            total = total + jnp.sum(dff, axis=1, keepdims=True)
        dbf_ref[...] = jnp.broadcast_to(total, (FOX_HEADS, 128))

    return pl.pallas_call(
        body,
        name="fox_cum_bwd",
        out_shape=[_sds((128, s_len), f32), _sds((FOX_HEADS, 128), f32)],
        compiler_params=_params(),
    )(dcq, dck, sg)


FOX_T = 512
ACC_ROWS = HEAD_DIM + 16
LANES = 128


def _causal_keep(t):
    return lax.broadcasted_iota(jnp.int32, (t, t), 0) <= lax.broadcasted_iota(jnp.int32, (t, t), 1)


def _tile_cols(i, t):
    return pl.ds(pl.multiple_of(i * t, t), t)


def _fox_pair(n, nt, kmax, h, pq_ref, pk_ref):
    k = h * kmax + jnp.maximum(n - nt, 0)
    return jnp.where(n < nt, n, pq_ref[k]), jnp.where(n < nt, n, pk_ref[k])


def _fox_fwd_call(qat, ka, vat, npairs, pair_q, pair_k):
    nh, s_len, _ = ka.shape
    t = FOX_T
    nt = s_len // t
    kmax = nt * (nt - 1) // 2
    assert nt >= 2 and nt % 2 == 0

    def body(np_ref, pq_ref, pk_ref, qat_ref, ka_ref, vat_ref, o_ref, lse_ref, s0, s1, p0, p1, a0, a1, m_all, acc_all):
        h = pl.program_id(0)
        extra = np_ref[h]
        total = nt + extra
        m_all[...] = jnp.full(m_all.shape, NEG_INF, f32)
        acc_all[...] = jnp.zeros(acc_all.shape, f32)
        bufs = ((s0, p0, a0), (s1, p1, a1))

        def pair(n):
            return _fox_pair(n, nt, kmax, h, pq_ref, pk_ref)

        def scores(n, b, masked):
            i, j = pair(n)
            st = jnp.dot(ka_ref[0, _tile_cols(j, t), :], qat_ref[0, :, _tile_cols(i, t)], preferred_element_type=f32)
            if masked:
                st = jnp.where(_causal_keep(t), st, NEG_INF)
            bufs[b][0][...] = st

        def softmax(n, b):
            i, _ = pair(n)
            s_ref, p_ref, a_ref = bufs[b]
            for c in range(t // LANES):
                cols = slice(c * LANES, (c + 1) * LANES)
                mcols = pl.ds(pl.multiple_of(i * t + c * LANES, LANES), LANES)
                m_old = m_all[:, mcols]
                m_new = jnp.maximum(m_old, jnp.max(s_ref[:, cols], axis=0, keepdims=True))
                m_all[:, mcols] = m_new
                a_ref[:, cols] = jnp.exp(m_old - m_new)
                p_ref[:, cols] = jnp.exp(s_ref[:, cols] - m_new).astype(bf16)

        def accum(n, b):
            i, j = pair(n)
            cols = _tile_cols(i, t)
            acc_all[:, cols] = bufs[b][2][...] * acc_all[:, cols] + jnp.dot(
                vat_ref[0, 0:ACC_ROWS, _tile_cols(j, t)], bufs[b][1][...], preferred_element_type=f32)

        def step(n, b, masked):
            accum(n - 2, b)
            softmax(n - 1, 1 - b)
            scores(n, b, masked)

        scores(0, 0, True)
        scores(1, 1, True)
        softmax(0, 0)

        def diag_steps(d, _):
            n = 2 + 2 * d
            step(n, 0, True)
            step(n + 1, 1, True)
            return 0

        lax.fori_loop(0, (nt - 2) // 2, diag_steps, 0)

        def off_steps(d, _):
            n = nt + 2 * d
            step(n, 0, False)
            step(n + 1, 1, False)
            return 0

        lax.fori_loop(0, extra // 2, off_steps, 0)

        @pl.when(extra % 2 == 1)
        def _():
            step(total - 1, 0, False)
            softmax(total - 1, 0)
            accum(total - 2, 1)
            accum(total - 1, 0)

        @pl.when(extra % 2 == 0)
        def _():
            softmax(total - 1, 1)
            accum(total - 2, 0)
            accum(total - 1, 1)

        l = acc_all[HEAD_DIM:HEAD_DIM + 1, :]
        o_ref[0] = acc_all[0:HEAD_DIM, :] / l
        lse_ref[0] = m_all[...] + jnp.log(l)

    smem = pl.BlockSpec(memory_space=pltpu.SMEM)
    return pl.pallas_call(
        body,
        name="fox_fwd",
        grid=(nh,),
        in_specs=[smem, smem, smem,
                  pl.BlockSpec((1, AUG, s_len), lambda h: (h, 0, 0)),
                  pl.BlockSpec((1, s_len, AUG), lambda h: (h, 0, 0)),
                  pl.BlockSpec((1, AUG, s_len), lambda h: (h, 0, 0))],
        out_specs=[pl.BlockSpec((1, HEAD_DIM, s_len), lambda h: (h, 0, 0)),
                   pl.BlockSpec((1, 1, s_len), lambda h: (h, 0, 0))],
        out_shape=[_sds((nh, HEAD_DIM, s_len), f32), _sds((nh, 1, s_len), f32)],
        scratch_shapes=[pltpu.VMEM((t, t), f32), pltpu.VMEM((t, t), f32), pltpu.VMEM((t, t), bf16),
                        pltpu.VMEM((t, t), bf16), pltpu.VMEM((1, t), f32), pltpu.VMEM((1, t), f32),
                        pltpu.VMEM((1, s_len), f32), pltpu.VMEM((ACC_ROWS, s_len), f32)],
        compiler_params=_params(("arbitrary",)),
    )(npairs, pair_q, pair_k, qat, ka, vat)


SWA_TS = 512


SWA_W = SWA_GROUP * BLOCK


def _swa_bias_call(rel_bias, bucket_t):
    def body(rb_ref, bk_ref, b_ref, b0_ref):
        bk = bk_ref[...]
        row = lax.broadcasted_iota(jnp.int32, (2 * BLOCK, BLOCK), 0)
        for h in range(SWA_HEADS):
            acc = jnp.full((2 * BLOCK, BLOCK), NEG_INF, f32)
            for b in range(NUM_BUCKETS):
                acc = jnp.where(bk == b, rb_ref[b, h], acc)
            g, hh = divmod(h, SWA_GROUP)
            b_ref[g, :, hh * BLOCK:(hh + 1) * BLOCK] = acc
            b0_ref[g, :, hh * BLOCK:(hh + 1) * BLOCK] = jnp.where(row < BLOCK, NEG_INF, acc)

    return pl.pallas_call(
        body,
        name="swa_bias",
        in_specs=[pl.BlockSpec(memory_space=pltpu.SMEM), pl.BlockSpec(memory_space=pltpu.VMEM)],
        out_shape=[_sds((SWA_KV_HEADS, 2 * BLOCK, SWA_W), f32)] * 2,
        compiler_params=_params(),
    )(rel_bias, bucket_t)


def _swa_bias_bwd_call(dbias, bucket_t):
    def body(d_ref, bk_ref, o_ref):
        bk = bk_ref[...]
        row = lax.broadcasted_iota(jnp.int32, (NUM_BUCKETS, 128), 0)
        col = lax.broadcasted_iota(jnp.int32, (NUM_BUCKETS, 128), 1)
        out = jnp.zeros((NUM_BUCKETS, 128), f32)
        for h in range(SWA_HEADS):
            g, hh = divmod(h, SWA_GROUP)
            d = d_ref[g, :, hh * BLOCK:(hh + 1) * BLOCK]
            for b in range(NUM_BUCKETS):
                val = jnp.sum(jnp.sum(jnp.where(bk == b, d, 0.0), axis=1, keepdims=True), axis=0, keepdims=True)
                out = jnp.where((row == b) & (col == h), val, out)
        o_ref[...] = out

    return pl.pallas_call(
        body,
        name="swa_bias_bwd",
        out_shape=_sds((NUM_BUCKETS, 128), f32),
        compiler_params=_params(),
    )(dbias, bucket_t)


def _sink_row(sink_ref, g):
    return jnp.concatenate([jnp.full((1, BLOCK), sink_ref[g * SWA_GROUP + hh], f32) for hh in range(SWA_GROUP)], axis=1)


def _group_lanes(ref, g, cols):
    return jnp.concatenate([ref[g * SWA_GROUP + hh, :, cols] for hh in range(SWA_GROUP)], axis=1)


def _swa_fwd_call(qt, k, vta, bias_t, bias0_t, sink):
    s_len = qt.shape[2]
    ts = SWA_TS
    nb = ts // BLOCK

    def body(qt_ref, kc_ref, kp_ref, vc_ref, vp_ref, b_ref, b0_ref, sink_ref, o_ref, lse_ref):
        first = pl.program_id(0) == 0
        kall = [jnp.concatenate([kp_ref[g], kc_ref[g]], axis=0) for g in range(SWA_KV_HEADS)]
        vall = [jnp.concatenate([vp_ref[g], vc_ref[g]], axis=1) for g in range(SWA_KV_HEADS)]
        sinks = [_sink_row(sink_ref, g) for g in range(SWA_KV_HEADS)]
        items = [(g, b) for g in range(SWA_KV_HEADS) for b in range(nb)]

        def scores(g, b):
            qg = _group_lanes(qt_ref, g, slice(b * BLOCK, (b + 1) * BLOCK))
            bias_b = b_ref[g]
            if b == 0:
                bias_b = jnp.where(first, b0_ref[g], bias_b)
            return jnp.dot(kall[g][b * BLOCK:(b + 2) * BLOCK], qg, preferred_element_type=f32) + bias_b

        def finish(g, b, st):
            m = jnp.maximum(jnp.max(st, axis=0, keepdims=True), sinks[g])
            pt = jnp.exp(st - m)
            acc = jnp.dot(vall[g][:, b * BLOCK:(b + 2) * BLOCK], pt.astype(bf16), preferred_element_type=f32)
            l = acc[HEAD_DIM:HEAD_DIM + 1, :] + jnp.exp(sinks[g] - m)
            return acc[0:HEAD_DIM, :] / l, m + jnp.log(l)

        outs, lses = {}, {}
        st_next = scores(*items[0])
        for idx, (g, b) in enumerate(items):
            st = st_next
            if idx + 1 < len(items):
                st_next = scores(*items[idx + 1])
            outs[g, b], lses[g, b] = finish(g, b, st)
        for g in range(SWA_KV_HEADS):
            for hh in range(SWA_GROUP):
                lanes = slice(hh * BLOCK, (hh + 1) * BLOCK)
                o_ref[g * SWA_GROUP + hh] = jnp.concatenate([outs[g, b][:, lanes] for b in range(nb)], axis=1)
                lse_ref[g * SWA_GROUP + hh] = jnp.concatenate([lses[g, b][:, lanes] for b in range(nb)], axis=1)

    def prev_blk(n):
        return jnp.maximum(n * nb - 1, 0)

    bspec = pl.BlockSpec((SWA_KV_HEADS, 2 * BLOCK, SWA_W), lambda n: (0, 0, 0))
    return pl.pallas_call(
        body,
        name="swa_fwd",
        grid=(s_len // ts,),
        in_specs=[pl.BlockSpec((SWA_HEADS, HEAD_DIM, ts), lambda n: (0, 0, n)),
                  pl.BlockSpec((SWA_KV_HEADS, ts, HEAD_DIM), lambda n: (0, n, 0)),
                  pl.BlockSpec((SWA_KV_HEADS, BLOCK, HEAD_DIM), lambda n: (0, prev_blk(n), 0)),
                  pl.BlockSpec((SWA_KV_HEADS, AUG, ts), lambda n: (0, 0, n)),
                  pl.BlockSpec((SWA_KV_HEADS, AUG, BLOCK), lambda n: (0, 0, prev_blk(n))),
                  bspec, bspec, pl.BlockSpec(memory_space=pltpu.SMEM)],
        out_specs=[pl.BlockSpec((SWA_HEADS, HEAD_DIM, ts), lambda n: (0, 0, n)),
                   pl.BlockSpec((SWA_HEADS, 1, ts), lambda n: (0, 0, n))],
        out_shape=[_sds((SWA_HEADS, HEAD_DIM, s_len), f32), _sds((SWA_HEADS, 1, s_len), f32)],
        compiler_params=_params(("arbitrary",)),
    )(qt, k, k, vta, vta, bias_t, bias0_t, sink)


def _head_selector():
    sel = np.zeros((512, 128), np.float32)
    for h in range(8):
        sel[h * HEAD_DIM:(h + 1) * HEAD_DIM, h] = 1.0
    return sel


def _post_call(of, fz, osw, sz, x2, tgt, wo, ln_g, ln_b, sel, tm):
    s_len = x2.shape[0]

    def body(of_ref, fz_ref, os_ref, sz_ref, x_ref, t_ref, wo_ref, g_ref, b_ref, sel_ref,
             dh_ref, dof_ref, dfz_ref, dos_ref, dsz_ref, dlf_ref, dls_ref, dwo_ref, dg_ref, db_ref, loss_ref):
        n = pl.program_id(0)

        @pl.when(n == 0)
        def _():
            dwo_ref[...] = jnp.zeros_like(dwo_ref)
            dg_ref[...] = jnp.zeros_like(dg_ref)
            db_ref[...] = jnp.zeros_like(db_ref)
            loss_ref[...] = jnp.zeros_like(loss_ref)

        gam = g_ref[...]
        sel_m = sel_ref[...]

        def forward(r):
            o_f = of_ref[:, r].T
            o_s = os_ref[:, r].T
            fz = fz_ref[r, :]
            sz = sz_ref[r, :]
            sg_f = jax.nn.sigmoid(fz)
            sg_s = jax.nn.sigmoid(sz)
            silu_f = fz * sg_f
            silu_s = sz * sg_s
            mixed = jnp.concatenate([o_f * silu_f, o_s * silu_s], axis=1).astype(bf16)
            y = jnp.dot(mixed, wo_ref[...], preferred_element_type=f32)
            return o_f, o_s, fz, sz, sg_f, sg_s, silu_f, silu_s, mixed, y

        def norm_and_back(r, fwd):
            mixed, y = fwd[8], fwd[9]
            h = ALPHA * x_ref[r, :] + y
            mu = jnp.mean(h, axis=1, keepdims=True)
            hc = h - mu
            var = jnp.mean(hc * hc, axis=1, keepdims=True)
            rstd = lax.rsqrt(var + LN_EPS)
            xhat = hc * rstd
            out = xhat * gam + b_ref[...]
            err = out - t_ref[r, :]
            tok_loss = jnp.mean(err * err, axis=1, keepdims=True)
            loss_ref[...] += 0.5 * jnp.sum(tok_loss, axis=0, keepdims=True)
            dout = err * (1.0 / D_MODEL)
            dg_ref[...] += jnp.sum(dout * xhat, axis=0, keepdims=True)
            db_ref[...] += jnp.sum(dout, axis=0, keepdims=True)
            dxh = dout * gam
            m1 = jnp.mean(dxh, axis=1, keepdims=True)
            m2 = jnp.mean(dxh * xhat, axis=1, keepdims=True)
            dh = rstd * (dxh - m1 - xhat * m2)
            dh_ref[r, :] = dh
            dyb = dh.astype(bf16)
            dmix = lax.dot_general(dyb, wo_ref[...], NT, preferred_element_type=f32)
            dwo_ref[...] += lax.dot_general(mixed, dyb, TN, preferred_element_type=f32)
            return dmix

        def head_sums(prod):
            hi = prod.astype(bf16)
            lo = (prod - hi.astype(f32)).astype(bf16)
            return (jnp.dot(hi, sel_m, preferred_element_type=f32) + jnp.dot(lo, sel_m, preferred_element_type=f32))

        def gates_back(r, fwd, dmix):
            o_f, o_s, fz, sz, sg_f, sg_s, silu_f, silu_s = fwd[:8]
            dm_f = dmix[:, :512]
            dm_s = dmix[:, 512:]
            do_f = dm_f * silu_f
            do_s = dm_s * silu_s
            dfz_ref[r, :] = (dm_f * o_f * (sg_f * (1.0 + fz * (1.0 - sg_f)))).astype(bf16)
            dsz_ref[r, :] = (dm_s * o_s * (sg_s * (1.0 + sz * (1.0 - sg_s)))).astype(bf16)
            dof_ref[:, r] = do_f.T.astype(bf16)
            dos_ref[:, r] = do_s.T.astype(bf16)
            dlf_ref[:, r] = head_sums(do_f * o_f).T[:FOX_HEADS, :]
            dls_ref[:, r] = head_sums(do_s * o_s).T[:SWA_HEADS, :]

        halves = [slice(k * (tm // 2), (k + 1) * (tm // 2)) for k in range(2)]
        fwds = [forward(r) for r in halves]
        dmixes = [norm_and_back(r, f) for r, f in zip(halves, fwds)]
        for r, f, d in zip(halves, fwds, dmixes):
            gates_back(r, f, d)

    feat = pl.BlockSpec((512, tm), lambda n: (0, n))
    rows8 = pl.BlockSpec((8, tm), lambda n: (0, n))
    half = pl.BlockSpec((tm, 512), lambda n: (n, 0))
    fullw = pl.BlockSpec((tm, D_MODEL), lambda n: (n, 0))
    vec = pl.BlockSpec((1, D_MODEL), lambda n: (0, 0))
    return pl.pallas_call(
        body,
        name="post_fwd_bwd",
        grid=(s_len // tm,),
        in_specs=[feat, half, feat, half, fullw, fullw,
                  pl.BlockSpec((D_MODEL, D_MODEL), lambda n: (0, 0)), vec, vec,
                  pl.BlockSpec((512, 128), lambda n: (0, 0))],
        out_specs=[fullw, feat, half, feat, half, rows8, rows8,
                   pl.BlockSpec((D_MODEL, D_MODEL), lambda n: (0, 0)), vec, vec,
                   pl.BlockSpec((1, 1), lambda n: (0, 0))],
        out_shape=[_sds((s_len, D_MODEL), f32), _sds((512, s_len), bf16), _sds((s_len, 512), bf16),
                   _sds((512, s_len), bf16), _sds((s_len, 512), bf16),
                   _sds((FOX_HEADS, s_len), f32), _sds((SWA_HEADS, s_len), f32),
                   _sds((D_MODEL, D_MODEL), f32), _sds((1, D_MODEL), f32), _sds((1, D_MODEL), f32),
                   _sds((1, 1), f32)],
        compiler_params=_params(("arbitrary",), VMEM_LIMIT_BIG),
    )(of, fz, osw, sz, x2, tgt, wo, ln_g, ln_b, sel)


def _fox_bwd_call(ka, kat, v, qat, dot, lse_row, dl_row, npairs, pair_q, pair_k):
    nh, s_len, _ = ka.shape
    t = FOX_T
    nt = s_len // t
    kmax = nt * (nt - 1) // 2
    assert nt >= 2 and nt % 2 == 0
    ck_slot = HEAD_DIM + 3
    cq_slot = HEAD_DIM

    def body(np_ref, pq_ref, pk_ref, ka_ref, kat_ref, v_ref, qat_ref, dot_ref, lse_ref, dl_ref,
             dq_ref, dk_ref, dv_ref, dcq_ref, dck_ref, dqt_all, dkat_all, dvt_all, p0, p1, ds0, ds1):
        h = pl.program_id(0)
        extra = np_ref[h]
        total = nt + extra
        dqt_all[...] = jnp.zeros(dqt_all.shape, f32)
        dkat_all[...] = jnp.zeros(dkat_all.shape, f32)
        dvt_all[...] = jnp.zeros(dvt_all.shape, f32)
        pbuf, dsbuf = (p0, p1), (ds0, ds1)

        def pair(n):
            return _fox_pair(n, nt, kmax, h, pq_ref, pk_ref)

        def probs(n, b, masked):
            i, j = pair(n)
            qc, kr = _tile_cols(i, t), _tile_cols(j, t)
            st = jnp.dot(ka_ref[0, kr, :], qat_ref[0, :, qc], preferred_element_type=f32)
            dpt = jnp.dot(v_ref[0, kr, :], dot_ref[0, :, qc], preferred_element_type=f32)
            if masked:
                st = jnp.where(_causal_keep(t), st, NEG_INF)
            pt = jnp.exp(st - lse_ref[0, :, qc])
            pbuf[b][...] = pt.astype(bf16)
            dsbuf[b][...] = (pbuf[b][...].astype(f32) * (dpt - dl_ref[0, :, qc])).astype(bf16)

        def grads(n, b):
            i, j = pair(n)
            qc, kc = _tile_cols(i, t), _tile_cols(j, t)
            dvt_all[:, kc] += lax.dot_general(dot_ref[0, :, qc], pbuf[b][...], NT, preferred_element_type=f32)
            dkat_all[:, kc] += lax.dot_general(qat_ref[0, 0:ACC_ROWS, qc], dsbuf[b][...], NT, preferred_element_type=f32)
            dqt_all[:, qc] += jnp.dot(kat_ref[0, 0:ACC_ROWS, kc], dsbuf[b][...], preferred_element_type=f32)

        def step(n, b, masked):
            i, j = pair(n)
            qc, kr = _tile_cols(i, t), _tile_cols(j, t)
            i1, j1 = pair(n - 1)
            qc1, kc1 = _tile_cols(i1, t), _tile_cols(j1, t)
            c = 1 - b
            st = jnp.dot(ka_ref[0, kr, :], qat_ref[0, :, qc], preferred_element_type=f32)
            dvt_all[:, kc1] += lax.dot_general(dot_ref[0, :, qc1], pbuf[c][...], NT, preferred_element_type=f32)
            if masked:
                st = jnp.where(_causal_keep(t), st, NEG_INF)
            pt = jnp.exp(st - lse_ref[0, :, qc])
            pbuf[b][...] = pt.astype(bf16)
            dpt = jnp.dot(v_ref[0, kr, :], dot_ref[0, :, qc], preferred_element_type=f32)
            dkat_all[:, kc1] += lax.dot_general(qat_ref[0, 0:ACC_ROWS, qc1], dsbuf[c][...], NT, preferred_element_type=f32)
            dqt_all[:, qc1] += jnp.dot(kat_ref[0, 0:ACC_ROWS, kc1], dsbuf[c][...], preferred_element_type=f32)
            dsbuf[b][...] = (pbuf[b][...].astype(f32) * (dpt - dl_ref[0, :, qc])).astype(bf16)

        probs(0, 0, True)
        step(1, 1, True)

        def four_steps(n, masked):
            step(n, 0, masked)
            step(n + 1, 1, masked)
            step(n + 2, 0, masked)
            step(n + 3, 1, masked)

        def diag_quads(d, _):
            four_steps(2 + 4 * d, True)
            return 0

        lax.fori_loop(0, (nt - 2) // 4, diag_quads, 0)
        if (nt - 2) % 4:
            step(nt - 2, 0, True)
            step(nt - 1, 1, True)

        def off_quads(d, _):
            four_steps(nt + 4 * d, False)
            return 0

        quads = extra // 4
        lax.fori_loop(0, quads, off_quads, 0)

        def off_steps(d, _):
            n = nt + 4 * quads + 2 * d
            step(n, 0, False)
            step(n + 1, 1, False)
            return 0

        lax.fori_loop(0, (extra % 4) // 2, off_steps, 0)

        @pl.when(extra % 2 == 1)
        def _():
            step(total - 1, 0, False)
            grads(total - 1, 0)

        @pl.when(extra % 2 == 0)
        def _():
            grads(total - 1, 1)

        dq_ref[0] = (dqt_all[0:HEAD_DIM, :] * SCALE).astype(bf16)
        dk_ref[0] = dkat_all[0:HEAD_DIM, :].astype(bf16)
        dv_ref[0] = dvt_all[...].astype(bf16)
        dcq_ref[0] = dqt_all[cq_slot:cq_slot + 1, :]
        dck_ref[0] = dkat_all[ck_slot:ck_slot + 1, :]

    smem = pl.BlockSpec(memory_space=pltpu.SMEM)
    rows = pl.BlockSpec((1, s_len, AUG), lambda h: (h, 0, 0))
    feat = pl.BlockSpec((1, AUG, s_len), lambda h: (h, 0, 0))
    feat64 = pl.BlockSpec((1, HEAD_DIM, s_len), lambda h: (h, 0, 0))
    rowv = pl.BlockSpec((1, 1, s_len), lambda h: (h, 0, 0))
    return pl.pallas_call(
        body,
        name="fox_bwd",
        grid=(nh,),
        in_specs=[smem, smem, smem, rows, feat, pl.BlockSpec((1, s_len, HEAD_DIM), lambda h: (h, 0, 0)), feat, feat64,
                  rowv, rowv],
        out_specs=[feat64, feat64, feat64, rowv, rowv],
        out_shape=[_sds((nh, HEAD_DIM, s_len), bf16)] * 3 + [_sds((nh, 1, s_len), f32)] * 2,
        scratch_shapes=[pltpu.VMEM((ACC_ROWS, s_len), f32), pltpu.VMEM((ACC_ROWS, s_len), f32),
                        pltpu.VMEM((HEAD_DIM, s_len), f32)]
                       + [pltpu.VMEM((t, t), bf16)] * 4,
        compiler_params=_params(("arbitrary",)),
    )(npairs, pair_q, pair_k, ka, kat, v, qat, dot, lse_row, dl_row)


def _swa_bwd_call(qt, k, kt, v, dot, lse, dl, bias_t, bias0_t, sink):
    s_len = qt.shape[2]
    ts = SWA_TS
    nb = ts // BLOCK
    nsteps = s_len // ts

    def body(qt_ref, kc_ref, kp_ref, ktc_ref, ktp_ref, vc_ref, vp_ref, dot_ref, lse_ref, dl_ref, b_ref, b0_ref,
             sink_ref, dq_ref, dk_ref, dv_ref, dbias_ref, dsink_ref, dk_s, dv_s, tail_k, tail_v, sk_s):
        n = pl.program_id(0)

        @pl.when(n == 0)
        def _():
            dbias_ref[...] = jnp.zeros_like(dbias_ref)
            sk_s[...] = jnp.zeros_like(sk_s)

        @pl.when(n < nsteps)
        def _():
            first = n == 0
            dk_s[...] = jnp.zeros_like(dk_s)
            dv_s[...] = jnp.zeros_like(dv_s)
            groups = range(SWA_KV_HEADS)
            kall = [jnp.concatenate([kp_ref[g], kc_ref[g]], axis=0) for g in groups]
            vall = [jnp.concatenate([vp_ref[g], vc_ref[g]], axis=0) for g in groups]
            ktall = [jnp.concatenate([ktp_ref[g], ktc_ref[g]], axis=1) for g in groups]
            sinks = [_sink_row(sink_ref, g) for g in groups]
            items = [(g, b) for g in groups for b in range(nb)]

            def products(g, b):
                cols = slice(b * BLOCK, (b + 1) * BLOCK)
                win = slice(b * BLOCK, (b + 2) * BLOCK)
                qg = _group_lanes(qt_ref, g, cols)
                dog = _group_lanes(dot_ref, g, cols)
                bias_b = b_ref[g]
                if b == 0:
                    bias_b = jnp.where(first, b0_ref[g], bias_b)
                st = jnp.dot(kall[g][win], qg, preferred_element_type=f32) + bias_b
                dpt = jnp.dot(vall[g][win], dog, preferred_element_type=f32)
                return qg, dog, st, dpt

            def finish(g, b, qg, dog, st, dpt):
                cols = slice(b * BLOCK, (b + 1) * BLOCK)
                win = slice(b * BLOCK, (b + 2) * BLOCK)
                lse_r = _group_lanes(lse_ref, g, cols)
                dl_r = _group_lanes(dl_ref, g, cols)
                pt = jnp.exp(st - lse_r)
                dst = pt * (dpt - dl_r)
                dsb = dst.astype(bf16)
                dk_s[g, :, win] += lax.dot_general(qg, dsb, NT, preferred_element_type=f32)
                dv_s[g, :, win] += lax.dot_general(dog, pt.astype(bf16), NT, preferred_element_type=f32)
                dqg = jnp.dot(ktall[g][:, win], dsb, preferred_element_type=f32) * SCALE
                return dqg, dst, -jnp.exp(sinks[g] - lse_r) * dl_r

            dqs, dsts, sks = {}, {}, {}
            nxt = products(*items[0])
            for idx, (g, b) in enumerate(items):
                cur = nxt
                if idx + 1 < len(items):
                    nxt = products(*items[idx + 1])
                dqs[g, b], dsts[g, b], sks[g, b] = finish(g, b, *cur)
            for g in groups:
                dbias_ref[g] += functools.reduce(lambda a, c: a + c, [dsts[g, b] for b in range(nb)])
                sk_s[g] += functools.reduce(lambda a, c: a + c, [sks[g, b] for b in range(nb)])
                for hh in range(SWA_GROUP):
                    lanes = slice(hh * BLOCK, (hh + 1) * BLOCK)
                    dq_ref[g * SWA_GROUP + hh] = jnp.concatenate(
                        [dqs[g, b][:, lanes] for b in range(nb)], axis=1).astype(bf16)

        @pl.when(n > 0)
        def _():
            last = slice(ts - BLOCK, ts)
            for g in range(SWA_KV_HEADS):
                add_k = jnp.where(n < nsteps, dk_s[g, :, 0:BLOCK], 0.0)
                add_v = jnp.where(n < nsteps, dv_s[g, :, 0:BLOCK], 0.0)
                dk_ref[g, :, 0:ts - BLOCK] = tail_k[g, :, 0:ts - BLOCK].astype(bf16)
                dv_ref[g, :, 0:ts - BLOCK] = tail_v[g, :, 0:ts - BLOCK].astype(bf16)
                dk_ref[g, :, last] = (tail_k[g, :, last] + add_k).astype(bf16)
                dv_ref[g, :, last] = (tail_v[g, :, last] + add_v).astype(bf16)

        @pl.when(n < nsteps)
        def _():
            tail_k[...] = dk_s[:, :, BLOCK:]
            tail_v[...] = dv_s[:, :, BLOCK:]

        @pl.when(n == nsteps)
        def _():
            row = lax.broadcasted_iota(jnp.int32, (SWA_HEADS, 128), 0)
            out = jnp.zeros((SWA_HEADS, 128), f32)
            for h in range(SWA_HEADS):
                g, hh = divmod(h, SWA_GROUP)
                val = jnp.sum(sk_s[g, :, hh * BLOCK:(hh + 1) * BLOCK], axis=1, keepdims=True)
                out = jnp.where(row == h, val, out)
            dsink_ref[...] = out

    last_step = nsteps - 1

    def cl(n):
        return jnp.minimum(n, last_step)

    def prev_blk(n):
        return jnp.maximum(cl(n) * nb - 1, 0)

    feat8 = pl.BlockSpec((SWA_HEADS, HEAD_DIM, ts), lambda n: (0, 0, cl(n)))
    rows8 = pl.BlockSpec((SWA_HEADS, 1, ts), lambda n: (0, 0, cl(n)))
    cur = pl.BlockSpec((SWA_KV_HEADS, ts, HEAD_DIM), lambda n: (0, cl(n), 0))
    prev = pl.BlockSpec((SWA_KV_HEADS, BLOCK, HEAD_DIM), lambda n: (0, prev_blk(n), 0))
    curt = pl.BlockSpec((SWA_KV_HEADS, HEAD_DIM, ts), lambda n: (0, 0, cl(n)))
    prevt = pl.BlockSpec((SWA_KV_HEADS, HEAD_DIM, BLOCK), lambda n: (0, 0, prev_blk(n)))
    bspec = pl.BlockSpec((SWA_KV_HEADS, 2 * BLOCK, SWA_W), lambda n: (0, 0, 0))
    kvout = pl.BlockSpec((SWA_KV_HEADS, HEAD_DIM, ts), lambda n: (0, 0, jnp.maximum(n - 1, 0)))
    return pl.pallas_call(
        body,
        name="swa_bwd",
        grid=(nsteps + 1,),
        in_specs=[feat8, cur, prev, curt, prevt, cur, prev, feat8, rows8, rows8, bspec, bspec,
                  pl.BlockSpec(memory_space=pltpu.SMEM)],
        out_specs=[feat8, kvout, kvout, bspec, pl.BlockSpec((SWA_HEADS, 128), lambda n: (0, 0))],
        out_shape=[_sds((SWA_HEADS, HEAD_DIM, s_len), bf16), _sds((SWA_KV_HEADS, HEAD_DIM, s_len), bf16),
                   _sds((SWA_KV_HEADS, HEAD_DIM, s_len), bf16),
                   _sds((SWA_KV_HEADS, 2 * BLOCK, SWA_W), f32), _sds((SWA_HEADS, 128), f32)],
        scratch_shapes=[pltpu.VMEM((SWA_KV_HEADS, HEAD_DIM, ts + BLOCK), f32),
                        pltpu.VMEM((SWA_KV_HEADS, HEAD_DIM, ts + BLOCK), f32),
                        pltpu.VMEM((SWA_KV_HEADS, HEAD_DIM, ts), f32),
                        pltpu.VMEM((SWA_KV_HEADS, HEAD_DIM, ts), f32),
                        pltpu.VMEM((SWA_KV_HEADS, 1, SWA_W), f32)],
        compiler_params=_params(("arbitrary",)),
    )(qt, k, k, kt, kt, v, v, dot, lse, dl, bias_t, bias0_t, sink)


def _dproj_specs(tm):
    half = pl.BlockSpec((tm, 512), lambda i: (i, 0))
    feat = pl.BlockSpec((512, tm), lambda i: (0, i))
    feat_kv = pl.BlockSpec((128, tm), lambda i: (0, i))
    return [feat, feat, feat, half, feat, feat_kv, feat_kv, half, feat_kv]


def _dx_exchange_call(dh, pieces, w_t, bs, tm):
    s_len = dh.shape[0]
    n = len(bs)
    last = s_len // tm - 1

    def body(*refs):
        dh_ref, dqf_ref, dkf_ref, dvf_ref, dfz_ref, dqs_ref, dks_ref, dvs_ref, dsz_ref, dfft_ref, w_ref = refs[:11]
        b_refs = refs[11:11 + n]
        dx_ref = refs[11 + n]
        r_refs = refs[12 + n:12 + 2 * n]
        sems = refs[12 + 2 * n:]
        i = pl.program_id(0)

        @pl.when(i == 0)
        def _():
            _exchange_start(b_refs, r_refs, sems)

        def tr(ref):
            return ref[...].astype(f32).T.astype(bf16)

        dp = jnp.concatenate([tr(dqf_ref), tr(dkf_ref), tr(dvf_ref), dfz_ref[...], tr(dqs_ref), tr(dks_ref),
                              tr(dvs_ref), dsz_ref[...], tr(dfft_ref)], axis=1)
        dx_ref[...] = ALPHA * dh_ref[...] + jnp.dot(dp, w_ref[...], preferred_element_type=f32)

        @pl.when(i == last)
        def _():
            _exchange_wait(b_refs, r_refs, sems)

    fullw = pl.BlockSpec((tm, D_MODEL), lambda i: (i, 0))
    any_spec = pl.BlockSpec(memory_space=pl.ANY)
    out = pl.pallas_call(
        body,
        name="dx_bwd_exchange",
        grid=(s_len // tm,),
        in_specs=[fullw] + _dproj_specs(tm) + [pl.BlockSpec((A_W, D_MODEL), lambda i: (0, 0))] + [any_spec] * n,
        out_specs=[fullw] + [any_spec] * n,
        out_shape=[_sds((s_len, D_MODEL), f32)] + [_sds(b.shape, b.dtype) for b in bs],
        scratch_shapes=[pltpu.SemaphoreType.DMA((7 * n,)), pltpu.SemaphoreType.DMA((7 * n,)),
                        pltpu.SemaphoreType.DMA((n,))],
        compiler_params=_params(("arbitrary",)),
    )(dh, *pieces, w_t, *bs)
    return out[0], out[1:]


DW_STAGE_ROWS = 384


def _dw_exchange_call(x2, pieces, bs, tm):
    s_len = x2.shape[0]
    nt = s_len // tm
    n = len(bs)

    def body(*refs):
        x_ref, dqf_ref, dkf_ref, dvf_ref, dfz_ref, dqs_ref, dks_ref, dvs_ref, dsz_ref, dfft_ref = refs[:10]
        b_refs = refs[10:10 + n]
        dw_ref = refs[10 + n]
        r_refs = refs[11 + n:11 + 2 * n]
        acc_ref, stage_ref, sem = refs[11 + 2 * n:14 + 2 * n]
        sems = refs[14 + 2 * n:]
        i = pl.program_id(0)

        @pl.when(i == 0)
        def _():
            _exchange_start(b_refs, r_refs, sems)
            acc_ref[...] = jnp.zeros_like(acc_ref)

        xb = x_ref[...].astype(bf16)

        def add_feat(off, lhs):
            acc_ref[off:off + lhs.shape[0], :] += jnp.dot(lhs, xb, preferred_element_type=f32)

        def add_rows(off, piece):
            acc_ref[off:off + piece.shape[1], :] += lax.dot_general(piece, xb, TN, preferred_element_type=f32)

        add_feat(A_FQ, dqf_ref[...])
        add_feat(A_FK, dkf_ref[...])
        add_feat(A_FV, dvf_ref[...])
        add_rows(A_FZ, dfz_ref[...])
        add_feat(A_SQ, dqs_ref[...])
        add_feat(A_SK, dks_ref[...])
        add_feat(A_SV, dvs_ref[...])
        add_rows(A_SZ, dsz_ref[...])
        add_feat(A_FF, dfft_ref[...].astype(bf16))

        @pl.when(i == nt - 1)
        def _():
            for r in range(A_W // DW_STAGE_ROWS):
                rows = slice(r * DW_STAGE_ROWS, (r + 1) * DW_STAGE_ROWS)
                stage_ref[...] = acc_ref[rows, :].astype(bf16)
                cp = pltpu.make_async_copy(stage_ref, dw_ref.at[rows, :], sem)
                cp.start()
                cp.wait()
            _exchange_wait(b_refs, r_refs, sems)

    any_spec = pl.BlockSpec(memory_space=pl.ANY)
    out = pl.pallas_call(
        body,
        name="dw_bwd_exchange",
        grid=(nt,),
        in_specs=[pl.BlockSpec((tm, D_MODEL), lambda i: (i, 0))] + _dproj_specs(tm) + [any_spec] * n,
        out_specs=[any_spec] * (1 + n),
        out_shape=[_sds((A_W, D_MODEL), bf16)] + [_sds(b.shape, b.dtype) for b in bs],
        scratch_shapes=[pltpu.VMEM((A_W, D_MODEL), f32), pltpu.VMEM((DW_STAGE_ROWS, D_MODEL), bf16),
                        pltpu.SemaphoreType.DMA, pltpu.SemaphoreType.DMA((7 * n,)), pltpu.SemaphoreType.DMA((7 * n,)),
                        pltpu.SemaphoreType.DMA((n,))],
        compiler_params=_params(("arbitrary",), VMEM_LIMIT_BIG),
    )(x2, *pieces, *bs)
    return out[0], out[1:]


def _adam_call(recv, w, m, v, tc, name):
    rows, cols = w.shape

    def body(r_ref, w_ref, m_ref, v_ref, g_ref, d_ref, mo_ref, vo_ref):
        g = r_ref[0].astype(f32)
        for p in range(1, N_DEV):
            g = g + r_ref[p].astype(f32)
        mn = ADAM_B1 * m_ref[...] + (1.0 - ADAM_B1) * g
        vn = ADAM_B2 * v_ref[...] + (1.0 - ADAM_B2) * (g * g)
        m_hat = mn / (1.0 - ADAM_B1 ** ADAM_STEP)
        v_hat = vn / (1.0 - ADAM_B2 ** ADAM_STEP)
        g_ref[...] = g
        d_ref[...] = -ADAM_LR * (m_hat / (jnp.sqrt(v_hat) + ADAM_EPS) + ADAM_WD * w_ref[...])
        mo_ref[...] = mn
        vo_ref[...] = vn

    blk = pl.BlockSpec((rows, tc), lambda i: (0, i))
    return pl.pallas_call(
        body,
        name=name,
        grid=(cols // tc,),
        in_specs=[pl.BlockSpec((N_DEV, rows, tc), lambda i: (0, 0, i)), blk, blk, blk],
        out_specs=[blk] * 4,
        out_shape=[_sds((rows, cols), f32)] * 4,
        compiler_params=_params(("arbitrary",)),
    )(recv, w, m, v)


def _rows_to_shards(parts, shard):
    blocks = []
    for d in range(N_DEV):
        lo, hi, start, pieces = d * shard, (d + 1) * shard, 0, []
        for part in parts:
            a, b = max(lo, start), min(hi, start + part.shape[0])
            if a < b:
                pieces.append(part[a - start:b - start])
            start += part.shape[0]
        blocks.append(jnp.concatenate(pieces, axis=0))
    return jnp.stack(blocks)


def _pad_cols(a, width=128):
    return jnp.pad(a, ((0, 0), (0, width - a.shape[1])))


def _pack_small(ln_g, ln_b, rel, b_f, sink):
    return jnp.concatenate([
        ln_g.reshape(8, 128), ln_b.reshape(8, 128), _pad_cols(rel),
        jnp.pad(_pad_cols(b_f), ((0, 7), (0, 0))), jnp.pad(_pad_cols(sink), ((0, 7), (0, 0)))], axis=0)


def _unpack_small(p):
    return (p[0:8].reshape(1, D_MODEL), p[8:16].reshape(1, D_MODEL), p[16:48, 0:8], p[48:49, 0:8], p[56:57, 0:8])


def kernel(x, w_in, b_f, rel_bias, sink, w_o, ln_g, ln_b, loss_target, m_w_in, m_b_f, m_rel_bias, m_sink, m_w_o, m_ln_g, m_ln_b, v_w_in, v_b_f, v_rel_bias, v_sink, v_w_o, v_ln_g, v_ln_b):
    x2 = x[0]
    tgt = loss_target[0]
    s_len = x2.shape[0]
    shard = w_in.shape[2]

    w_in_t = jnp.transpose(w_in[0])
    (g_in,) = _gather_call([w_in_t.astype(bf16)])
    wt_full = g_in.reshape(N_DEV * shard, D_MODEL)
    w_t = jnp.concatenate([wt_full[:O_FF0], wt_full[O_FF1:], wt_full[O_FF0:O_FF1],
                           jnp.zeros((A_W - D_IN, D_MODEL), bf16)], axis=0)

    wo_blocks = jnp.broadcast_to(w_o[0].astype(bf16)[None], (N_DEV,) + w_o.shape[1:])
    (qft, kft, vf, fz, qst, ks, vs, sz, fft, vat, kst, vsta), (g_o,) = _proj_call(x2, w_t, [wo_blocks], 512)
    wo_full = g_o.reshape(D_MODEL, D_MODEL)
    cum, sgm = _cum_call(fft, b_f.reshape(FOX_HEADS, 1))
    qat, ka, kat, tile_stats = _augment_call(qft, kft, cum.reshape(FOX_HEADS, 1, s_len), 2048)
    npairs, pair_q, pair_k = _fox_prune_tables(tile_stats)
    o_ft, lse_f = _fox_fwd_call(qat, ka, vat, npairs, pair_q, pair_k)
    bucket_t = jnp.asarray(_t5_bucket_table().T)
    bias_t, bias0_t = _swa_bias_call(rel_bias, bucket_t)
    sink_v = sink.reshape(SWA_HEADS)
    o_st, lse_s = _swa_fwd_call(qst, ks, vsta, bias_t, bias0_t, sink_v)

    (dh, do_f, dfz, do_s, dsz, dl_f, dl_s, dwo, dg, db, loss_part) = _post_call(
        o_ft.reshape(FOX_HEADS * HEAD_DIM, s_len), fz, o_st.reshape(SWA_HEADS * HEAD_DIM, s_len), sz, x2, tgt,
        wo_full, ln_g, ln_b, jnp.asarray(_head_selector()).astype(bf16), 512)

    dqf, dkf, dvf, dcq, dck = _fox_bwd_call(ka, kat, vf, qat, do_f.reshape(FOX_HEADS, HEAD_DIM, s_len), lse_f,
                                            dl_f.reshape(FOX_HEADS, 1, s_len), npairs, pair_q, pair_k)
    dqf, dkf, dvf = (a.reshape(FOX_HEADS * HEAD_DIM, s_len) for a in (dqf, dkf, dvf))
    dfft, dbf = _cum_bwd_call(dcq.reshape(FOX_HEADS, s_len), dck.reshape(FOX_HEADS, s_len), sgm)
    dqs, dks, dvs, dbias, dsink = _swa_bwd_call(
        qst, ks, kst, vs, do_s.reshape(SWA_HEADS, HEAD_DIM, s_len), lse_s, dl_s.reshape(SWA_HEADS, 1, s_len),
        bias_t, bias0_t, sink_v)
    dqs = dqs.reshape(SWA_HEADS * HEAD_DIM, s_len)
    dks, dvs = (a.reshape(SWA_KV_HEADS * HEAD_DIM, s_len) for a in (dks, dvs))
    drel = _swa_bias_bwd_call(dbias, bucket_t)

    dwo_blocks = dwo.reshape(N_DEV, D_MODEL // N_DEV, D_MODEL).astype(bf16)
    small = _pack_small(dg, db, drel[:, 0:8], dbf[:, 0].reshape(1, 8), dsink[:, 0].reshape(1, 8))
    loss_slot = np.zeros((64, 128), bool)
    loss_slot[49, 0] = True
    small = jnp.where(jnp.asarray(loss_slot), loss_part[0, 0], small)
    small_blocks = jnp.broadcast_to(small[None], (N_DEV,) + small.shape)
    pieces = (dqf, dkf, dvf, dfz, dqs, dks, dvs, dsz, dfft)
    dw_t, (r_o, r_small) = _dw_exchange_call(x2, pieces, [dwo_blocks, small_blocks], 1024)
    dw_blocks = _rows_to_shards([dw_t[:O_FF0], dw_t[A_FF:A_FF + (O_FF1 - O_FF0)], dw_t[O_FF0:A_FF]], shard)
    dx, (r_in,) = _dx_exchange_call(dh, pieces, w_t, [dw_blocks], 256)

    win_t = [jnp.transpose(a) for a in _adam_call(
        r_in, w_in_t, jnp.transpose(m_w_in[0]), jnp.transpose(v_w_in[0]), 256, "adam_w_in")]
    g_win, d_win, nm_win, nv_win = win_t
    g_wo, d_wo, nm_wo, nv_wo = _adam_call(r_o, w_o[0], m_w_o[0], v_w_o[0], 256, "adam_w_o")
    p_w = _pack_small(ln_g, ln_b, rel_bias, b_f, sink)
    p_m = _pack_small(m_ln_g, m_ln_b, m_rel_bias, m_b_f, m_sink)
    p_v = _pack_small(v_ln_g, v_ln_b, v_rel_bias, v_b_f, v_sink)
    g_p, d_p, nm_p, nv_p = _adam_call(r_small, p_w, p_m, p_v, 128, "adam_small")

    loss = g_p[49, 0]
    g_lng, g_lnb, g_rel, g_bf, g_sink = _unpack_small(g_p)
    d_lng, d_lnb, d_rel, d_bf, d_sink = _unpack_small(d_p)
    m_lng, m_lnb, m_rel, m_bf, m_sk = _unpack_small(nm_p)
    v_lng, v_lnb, v_rel, v_bf, v_sk = _unpack_small(nv_p)
    return (loss, dx[None], g_win[None], g_bf, g_rel, g_sink, g_wo[None], g_lng, g_lnb,
            d_win[None], d_bf, d_rel, d_sink, d_wo[None], d_lng, d_lnb,
            nm_win[None], m_bf, m_rel, m_sk, nm_wo[None], m_lng, m_lnb,
            nv_win[None], v_bf, v_rel, v_sk, nv_wo[None], v_lng, v_lnb)
```

```python
import functools
import math

import numpy as np
import jax
import jax.numpy as jnp
from jax import lax
from jax.experimental import pallas as pl
from jax.experimental.pallas import tpu as pltpu

f32 = jnp.float32
bf16 = jnp.bfloat16

D_MODEL = 1024
HEAD_DIM = 64
FOX_HEADS = 8
SWA_HEADS = 8
SWA_KV_HEADS = 2
SWA_GROUP = 4
BLOCK = 128
NUM_BUCKETS = 32
MAX_DISTANCE = 128
LN_EPS = 1e-5
NEG_INF = -1e30
ALPHA = 2.0 ** 0.25
SCALE = 1.0 / math.sqrt(HEAD_DIM)
D_IN = 3336

ADAM_LR = 0.001
ADAM_B1 = 0.9
ADAM_B2 = 0.999
ADAM_EPS = 1e-08
ADAM_WD = 0.01
ADAM_STEP = 10

N_DEV = 8
A_FQ, A_FK, A_FV, A_FZ, A_SQ, A_SK, A_SV, A_SZ, A_FF, A_W = 0, 512, 1024, 1536, 2048, 2560, 2688, 2816, 3328, 3456
O_FF0, O_FF1 = 1536, 1544

VMEM_LIMIT = 48 * 1024 * 1024
HIGHEST = lax.Precision.HIGHEST
NT = (((1,), (1,)), ((), ()))
TN = (((0,), (0,)), ((), ()))
MESH = pl.DeviceIdType.MESH
RELS = [(0, 0, 1), (0, 1, 0), (0, 1, 1), (1, 0, 0), (1, 0, 1), (1, 1, 0), (1, 1, 1)]


VMEM_LIMIT_BIG = 60 * 1024 * 1024


def _params(sem=None, vmem=VMEM_LIMIT):
    return pltpu.CompilerParams(dimension_semantics=sem, vmem_limit_bytes=vmem)


def _sds(shape, dtype):
    return jax.ShapeDtypeStruct(shape, dtype)


def _t5_bucket_table():
    qi = np.arange(BLOCK)[:, None]
    kj = np.arange(2 * BLOCK)[None, :]
    rel = qi + BLOCK - kj
    band = (rel >= 0) & (rel < BLOCK)
    relc = np.maximum(rel, 0)
    max_exact = NUM_BUCKETS // 2
    relf = np.maximum(relc, 1).astype(np.float32)
    large = max_exact + (np.log(relf / np.float32(max_exact)) / np.float32(math.log(MAX_DISTANCE / max_exact))
                         * np.float32(NUM_BUCKETS - max_exact)).astype(np.int32)
    large = np.minimum(large, NUM_BUCKETS - 1)
    bucket = np.where(relc < max_exact, relc, large).astype(np.int32)
    bucket = np.where(band, bucket, -1).astype(np.int32)
    return bucket


def _mesh_pos():
    return lax.axis_index("x"), lax.axis_index("y"), lax.axis_index("c")


def _dev_index(p):
    return 4 * p[0] + 2 * p[1] + p[2]


def _gather_call(xs):
    n = len(xs)

    def body(*refs):
        x_refs, o_refs = refs[:n], refs[n:2 * n]
        send_sems, recv_sems, local_sems = refs[2 * n:]
        x, y, c = _mesh_pos()
        me, sib = (x, y, c), (x, y, 1 - c)
        chips = [(1 - x, y), (x, 1 - y), (1 - x, 1 - y)]

        def copy(a, k, block, to, src=None):
            slot = o_refs[a].at[_dev_index(block)]
            return pltpu.make_async_remote_copy(
                src_ref=slot if src is None else src, dst_ref=slot,
                send_sem=send_sems.at[a * 7 + k], recv_sem=recv_sems.at[a * 7 + k],
                device_id=to, device_id_type=MESH)

        mine = [pltpu.make_async_copy(x_refs[a], o_refs[a].at[_dev_index(me)], local_sems.at[a]) for a in range(n)]
        for cp in mine:
            cp.start()
        first = []
        for a in range(n):
            first.append(copy(a, 0, me, sib, src=x_refs[a]))
            first += [copy(a, 1 + j, me, (*chip, c), src=x_refs[a]) for j, chip in enumerate(chips)]
        for cp in first:
            cp.start()
        passed = []
        for j, chip in enumerate(chips):
            for a in range(n):
                copy(a, 1 + j, (*chip, c), me).wait_recv()
                fwd = copy(a, 4 + j, (*chip, c), sib)
                fwd.start()
                passed.append(fwd)
        for a in range(n):
            copy(a, 0, sib, me).wait_recv()
            for j, chip in enumerate(chips):
                copy(a, 4 + j, (*chip, 1 - c), me).wait_recv()
        for cp in first + passed:
            cp.wait_send()
        for cp in mine:
            cp.wait()

    any_spec = pl.BlockSpec(memory_space=pl.ANY)
    return pl.pallas_call(
        body,
        name="gather_weights",
        out_shape=[_sds((N_DEV,) + a.shape, a.dtype) for a in xs],
        in_specs=[any_spec] * n,
        out_specs=[any_spec] * n,
        scratch_shapes=[pltpu.SemaphoreType.DMA((7 * n,)), pltpu.SemaphoreType.DMA((7 * n,)),
                        pltpu.SemaphoreType.DMA((n,))],
    )(*xs)


def _exchange_copies(b_refs, r_refs, send_sems, recv_sems, local_sems, incoming):
    n = len(b_refs)
    x, y, c = _mesh_pos()
    me_idx = _dev_index((x, y, c))
    mine = [pltpu.make_async_copy(b_refs[a].at[me_idx], r_refs[a].at[me_idx], local_sems.at[a]) for a in range(n)]
    remote = []
    for k, r in enumerate(RELS):
        peer = ((1 - x) if r[0] else x, (1 - y) if r[1] else y, (1 - c) if r[2] else c)
        pidx = _dev_index(peer)
        for a in range(n):
            remote.append(pltpu.make_async_remote_copy(
                src_ref=b_refs[a].at[pidx], dst_ref=r_refs[a].at[pidx if incoming else me_idx],
                send_sem=send_sems.at[a * 7 + k], recv_sem=recv_sems.at[a * 7 + k],
                device_id=peer, device_id_type=MESH))
    return mine, remote


def _exchange_start(b_refs, r_refs, sems):
    mine, out = _exchange_copies(b_refs, r_refs, *sems, incoming=False)
    for cp in mine + out:
        cp.start()


def _exchange_wait(b_refs, r_refs, sems):
    mine, inc = _exchange_copies(b_refs, r_refs, *sems, incoming=True)
    for cp in inc:
        cp.wait_recv()
    for cp in inc:
        cp.wait_send()
    for cp in mine:
        cp.wait()


def _proj_call(x2, w_t, bs, tm):
    s_len = x2.shape[0]
    n = len(bs)
    last = s_len // tm - 1

    def body(*refs):
        x_ref, w_ref = refs[:2]
        b_refs = refs[2:2 + n]
        (qft_ref, kft_ref, vf_ref, fz_ref, qst_ref, ks_ref, vs_ref, sz_ref, fft_ref, vat_ref,
         kst_ref, vsta_ref) = refs[2 + n:14 + n]
        r_refs = refs[14 + n:14 + 2 * n]
        sems = refs[14 + 2 * n:]

        @pl.when(pl.program_id(0) == 0)
        def _():
            _exchange_start(b_refs, r_refs, sems)

        xb = x_ref[...].astype(bf16)

        def seg_t(off, width):
            return lax.dot_general(w_ref[off:off + width, :], xb, NT, preferred_element_type=f32)

        def seg(off, width):
            return lax.dot_general(xb, w_ref[off:off + width, :], NT, preferred_element_type=f32)

        def put_heads(ref, acc, nheads):
            for h in range(nheads):
                ref[h] = acc[:, h * HEAD_DIM:(h + 1) * HEAD_DIM].astype(bf16)

        def put_heads_t(ref, acc_t, nheads, augment):
            for h in range(nheads):
                ref[h, 0:HEAD_DIM, :] = acc_t[h * HEAD_DIM:(h + 1) * HEAD_DIM, :].astype(bf16)
                if augment:
                    ref[h, HEAD_DIM:2 * HEAD_DIM, :] = ones_row

        ones_row = jnp.where(lax.broadcasted_iota(jnp.int32, (HEAD_DIM, tm), 0) == 0, 1.0, 0.0).astype(bf16)
        put_heads_t(vat_ref, seg_t(A_FV, 512), FOX_HEADS, True)
        put_heads_t(qft_ref, seg_t(A_FQ, 512) * SCALE, FOX_HEADS, False)
        put_heads_t(kft_ref, seg_t(A_FK, 512), FOX_HEADS, False)
        put_heads(vf_ref, seg(A_FV, 512), FOX_HEADS)
        fz_ref[...] = seg(A_FZ, 512)
        put_heads_t(qst_ref, seg_t(A_SQ, 512) * SCALE, SWA_HEADS, False)
        put_heads(ks_ref, seg(A_SK, 128), SWA_KV_HEADS)
        put_heads(vs_ref, seg(A_SV, 128), SWA_KV_HEADS)
        put_heads_t(kst_ref, seg_t(A_SK, 128), SWA_KV_HEADS, False)
        put_heads_t(vsta_ref, seg_t(A_SV, 128), SWA_KV_HEADS, True)
        sz_ref[...] = seg(A_SZ, 512)
        fft_ref[...] = seg(A_FF, 128).T[:FOX_HEADS, :]

        @pl.when(pl.program_id(0) == last)
        def _():
            _exchange_wait(b_refs, r_refs, sems)

    def heads(nh):
        return pl.BlockSpec((nh, tm, HEAD_DIM), lambda i: (0, i, 0))

    def feat(nh, rows):
        return pl.BlockSpec((nh, rows, tm), lambda i: (0, 0, i))

    wide = pl.BlockSpec((tm, 512), lambda i: (i, 0))
    any_spec = pl.BlockSpec(memory_space=pl.ANY)
    out = pl.pallas_call(
        body,
        name="proj_fwd_gather",
        grid=(s_len // tm,),
        in_specs=[pl.BlockSpec((tm, D_MODEL), lambda i: (i, 0)), pl.BlockSpec((A_W, D_MODEL), lambda i: (0, 0))]
                 + [any_spec] * n,
        out_specs=[feat(8, HEAD_DIM), feat(8, HEAD_DIM), heads(8), wide, feat(8, HEAD_DIM), heads(2), heads(2), wide,
                   pl.BlockSpec((FOX_HEADS, tm), lambda i: (0, i)),
                   feat(FOX_HEADS, 2 * HEAD_DIM), feat(2, HEAD_DIM), feat(2, 2 * HEAD_DIM)] + [any_spec] * n,
        out_shape=[_sds((8, HEAD_DIM, s_len), bf16)] * 2 + [_sds((8, s_len, HEAD_DIM), bf16)]
                  + [_sds((s_len, 512), f32), _sds((8, HEAD_DIM, s_len), bf16),
                     _sds((2, s_len, HEAD_DIM), bf16), _sds((2, s_len, HEAD_DIM), bf16), _sds((s_len, 512), f32),
                     _sds((FOX_HEADS, s_len), f32), _sds((FOX_HEADS, 2 * HEAD_DIM, s_len), bf16),
                     _sds((2, HEAD_DIM, s_len), bf16), _sds((2, 2 * HEAD_DIM, s_len), bf16)]
                  + [_sds(b.shape, b.dtype) for b in bs],
        scratch_shapes=[pltpu.SemaphoreType.DMA((7 * n,)), pltpu.SemaphoreType.DMA((7 * n,)),
                        pltpu.SemaphoreType.DMA((n,))],
        compiler_params=_params(("arbitrary",)),
    )(x2, w_t, *bs)
    return out[:12], out[12:]


AUG = 2 * HEAD_DIM
NEAR_KEYS = 3


def _augment_call(q_t, k_t, cum_row, tm):
    nh, _, s_len = k_t.shape
    per_step = tm // FOX_T

    def body(qt_ref, kt_ref, c_ref, qat_ref, ka_ref, kat_ref, st_ref):
        c = c_ref[0]
        hi = c.astype(bf16).astype(f32)
        r1 = c - hi
        mid = r1.astype(bf16).astype(f32)
        lo = (r1 - mid).astype(bf16).astype(f32)
        row = lax.broadcasted_iota(jnp.int32, (HEAD_DIM, tm), 0)
        q_tail = jnp.where(row == 0, hi, jnp.where(row == 1, mid, jnp.where(row == 2, lo,
                           jnp.where(row < 6, 1.0, 0.0))))
        k_tail = jnp.where(row < 3, 1.0, jnp.where(row == 3, -hi, jnp.where(row == 4, -mid,
                           jnp.where(row == 5, -lo, 0.0))))
        qat_ref[0, 0:HEAD_DIM, :] = qt_ref[0]
        qat_ref[0, HEAD_DIM:AUG, :] = q_tail.astype(bf16)
        kat_ref[0, 0:HEAD_DIM, :] = kt_ref[0]
        kat_ref[0, HEAD_DIM:AUG, :] = k_tail.astype(bf16)
        qt = qt_ref[0].astype(f32)
        kt = kt_ref[0].astype(f32)
        ka_ref[0] = jnp.concatenate([kt, k_tail], axis=0).T.astype(bf16)
        qn2 = jnp.sum(qt * qt, axis=0, keepdims=True)
        kn2 = jnp.sum(kt * kt, axis=0, keepdims=True)
        sd = jnp.sum(qt * kt, axis=0, keepdims=True)
        k_and_c = jnp.concatenate([kt, jnp.broadcast_to(c, (8, tm))], axis=0)
        lane = lax.broadcasted_iota(jnp.int32, (1, tm), 1)
        for shift in range(1, NEAR_KEYS + 1):
            prev = pltpu.roll(k_and_c, shift, axis=1)
            near = jnp.sum(qt * prev[0:HEAD_DIM], axis=0, keepdims=True) + (c - prev[HEAD_DIM:HEAD_DIM + 1])
            sd = jnp.maximum(sd, jnp.where(lane >= shift, near, NEG_INF))
        srow = lax.broadcasted_iota(jnp.int32, (8, LANES), 0)
        for part in range(per_step):
            sl = slice(part * FOX_T, (part + 1) * FOX_T)
            vals = [jnp.sqrt(jnp.max(qn2[:, sl], axis=1, keepdims=True)),
                    jnp.sqrt(jnp.max(kn2[:, sl], axis=1, keepdims=True)),
                    jnp.min(sd[:, sl], axis=1, keepdims=True),
                    jnp.max(c[:, sl], axis=1, keepdims=True), jnp.min(c[:, sl], axis=1, keepdims=True)]
            out = jnp.zeros((8, LANES), f32)
            for r, val in enumerate(vals):
                out = jnp.where(srow == r, val, out)
            st_ref[0, part] = out

    tile_t = pl.BlockSpec((1, HEAD_DIM, tm), lambda h, i: (h, 0, i))
    return pl.pallas_call(
        body,
        name="fox_augment",
        grid=(nh, s_len // tm),
        in_specs=[tile_t, tile_t, pl.BlockSpec((1, 1, tm), lambda h, i: (h, 0, i))],
        out_specs=[pl.BlockSpec((1, AUG, tm), lambda h, i: (h, 0, i)),
                   pl.BlockSpec((1, tm, AUG), lambda h, i: (h, i, 0)),
                   pl.BlockSpec((1, AUG, tm), lambda h, i: (h, 0, i)),
                   pl.BlockSpec((1, per_step, 8, LANES), lambda h, i: (h, i, 0, 0))],
        out_shape=[_sds((nh, AUG, s_len), bf16), _sds((nh, s_len, AUG), bf16), _sds((nh, AUG, s_len), bf16),
                   _sds((nh, s_len // FOX_T, 8, LANES), f32)],
        compiler_params=_params(("arbitrary", "arbitrary")),
    )(q_t, k_t, cum_row)


FOX_PRUNE_GAP = 32.0


def _fox_prune_tables(stats):
    s = stats[:, :, :, 0]
    qn, kn, sd, cmx, cmn = (s[:, :, r] for r in range(5))
    nt = s.shape[1]
    bound = qn[:, :, None] * kn[:, None, :] + (cmx[:, :, None] - cmn[:, None, :])
    margin = 0.01 + 1e-5 * (jnp.abs(cmx)[:, :, None] + jnp.abs(cmn)[:, None, :])
    qi = lax.broadcasted_iota(jnp.int32, (nt, nt), 0)
    kj = lax.broadcasted_iota(jnp.int32, (nt, nt), 1)
    skip = (bound + margin < sd[:, :, None] - FOX_PRUNE_GAP) & (kj < qi)[None]
    first = jnp.sum(jnp.cumprod(skip.astype(jnp.int32), axis=2), axis=2)
    tiles = lax.broadcasted_iota(jnp.int32, (1, nt), 1)
    cnt = tiles - first
    ends = jnp.cumsum(cnt, axis=1)
    off = ends - cnt
    kmax = nt * (nt - 1) // 2
    k = lax.broadcasted_iota(jnp.int32, (1, kmax), 1)
    pair_q = jnp.minimum(jnp.sum((ends[:, None, :] <= k[:, :, None]).astype(jnp.int32), axis=2), nt - 1)
    hit = pair_q[:, :, None] == tiles[:, None, :]
    first_k = jnp.sum(jnp.where(hit, first[:, None, :], 0), axis=2)
    off_k = jnp.sum(jnp.where(hit, off[:, None, :], 0), axis=2)
    pair_k = jnp.clip(first_k + k - off_k, 0, nt - 1)
    return (ends[:, nt - 1].astype(jnp.int32), pair_q.reshape(-1).astype(jnp.int32),
            pair_k.reshape(-1).astype(jnp.int32))


CUM_CHUNK = 512


def _cum_call(fft, bf_col):
    s_len = fft.shape[1]
    ch = CUM_CHUNK

    def body(f_ref, b_ref, cum_ref, sg_ref):
        r = lax.broadcasted_iota(jnp.int32, (ch, ch), 0)
        c = lax.broadcasted_iota(jnp.int32, (ch, ch), 1)
        upper = (r <= c).astype(f32)
        carry = jnp.zeros((FOX_HEADS, 1), f32)
        for n in range(s_len // ch):
            z = f_ref[:, n * ch:(n + 1) * ch] + b_ref[...]
            logf = jnp.minimum(z, 0.0) - jnp.log1p(jnp.exp(-jnp.abs(z)))
            sg_ref[:, n * ch:(n + 1) * ch] = 1.0 / (1.0 + jnp.exp(z))
            cs = jnp.dot(logf, upper, precision=HIGHEST, preferred_element_type=f32) + carry
            cum_ref[:, n * ch:(n + 1) * ch] = cs
            carry = cs[:, ch - 1:ch]

    return pl.pallas_call(
        body,
        name="fox_cum_fwd",
        out_shape=[_sds((FOX_HEADS, s_len), f32)] * 2,
        compiler_params=_params(),
    )(fft, bf_col)


def _cum_bwd_call(dcq, dck, sg):
    s_len = sg.shape[1]
    ch = CUM_CHUNK
    nch = s_len // ch

    def body(q_ref, k_ref, sg_ref, dff_ref, dbf_ref):
        r = lax.broadcasted_iota(jnp.int32, (ch, ch), 0)
        c = lax.broadcasted_iota(jnp.int32, (ch, ch), 1)
        lower = (r >= c).astype(f32)
        dff_ref[...] = jnp.zeros_like(dff_ref)
        carry = jnp.zeros((FOX_HEADS, 1), f32)
        total = jnp.zeros((FOX_HEADS, 1), f32)
        for n in reversed(range(nch)):
            sl = slice(n * ch, (n + 1) * ch)
            dcum = q_ref[:, sl] - k_ref[:, sl]
            rs = jnp.dot(dcum, lower, precision=HIGHEST, preferred_element_type=f32) + carry
            carry = rs[:, 0:1]
            dff = rs * sg_ref[:, sl]
            dff_ref[0:FOX_HEADS, sl] = dff
            total = total + jnp.sum(dff, axis=1, keepdims=True)
        dbf_ref[...] = jnp.broadcast_to(total, (FOX_HEADS, 128))

    return pl.pallas_call(
        body,
        name="fox_cum_bwd",
        out_shape=[_sds((128, s_len), f32), _sds((FOX_HEADS, 128), f32)],
        compiler_params=_params(),
    )(dcq, dck, sg)


FOX_T = 512
ACC_ROWS = HEAD_DIM + 16
LANES = 128


def _causal_keep(t):
    return lax.broadcasted_iota(jnp.int32, (t, t), 0) <= lax.broadcasted_iota(jnp.int32, (t, t), 1)


def _tile_cols(i, t):
    return pl.ds(pl.multiple_of(i * t, t), t)


def _fox_pair(n, nt, kmax, h, pq_ref, pk_ref):
    k = h * kmax + jnp.maximum(n - nt, 0)
    return jnp.where(n < nt, n, pq_ref[k]), jnp.where(n < nt, n, pk_ref[k])


def _fox_fwd_call(qat, ka, vat, npairs, pair_q, pair_k):
    nh, s_len, _ = ka.shape
    t = FOX_T
    nt = s_len // t
    kmax = nt * (nt - 1) // 2
    assert nt >= 2 and nt % 2 == 0

    def body(np_ref, pq_ref, pk_ref, qat_ref, ka_ref, vat_ref, o_ref, lse_ref, s0, s1, p0, p1, a0, a1, m_all, acc_all):
        h = pl.program_id(0)
        extra = np_ref[h]
        total = nt + extra
        m_all[...] = jnp.full(m_all.shape, NEG_INF, f32)
        acc_all[...] = jnp.zeros(acc_all.shape, f32)
        bufs = ((s0, p0, a0), (s1, p1, a1))

        def pair(n):
            return _fox_pair(n, nt, kmax, h, pq_ref, pk_ref)

        def scores(n, b, masked):
            i, j = pair(n)
            st = jnp.dot(ka_ref[0, _tile_cols(j, t), :], qat_ref[0, :, _tile_cols(i, t)], preferred_element_type=f32)
            if masked:
                st = jnp.where(_causal_keep(t), st, NEG_INF)
            bufs[b][0][...] = st

        def softmax(n, b):
            i, _ = pair(n)
            s_ref, p_ref, a_ref = bufs[b]
            for c in range(t // LANES):
                cols = slice(c * LANES, (c + 1) * LANES)
                mcols = pl.ds(pl.multiple_of(i * t + c * LANES, LANES), LANES)
                m_old = m_all[:, mcols]
                m_new = jnp.maximum(m_old, jnp.max(s_ref[:, cols], axis=0, keepdims=True))
                m_all[:, mcols] = m_new
                a_ref[:, cols] = jnp.exp(m_old - m_new)
                p_ref[:, cols] = jnp.exp(s_ref[:, cols] - m_new).astype(bf16)

        def accum(n, b):
            i, j = pair(n)
            cols = _tile_cols(i, t)
            acc_all[:, cols] = bufs[b][2][...] * acc_all[:, cols] + jnp.dot(
                vat_ref[0, 0:ACC_ROWS, _tile_cols(j, t)], bufs[b][1][...], preferred_element_type=f32)

        def step(n, b, masked):
            accum(n - 2, b)
            softmax(n - 1, 1 - b)
            scores(n, b, masked)

        scores(0, 0, True)
        scores(1, 1, True)
        softmax(0, 0)

        def diag_steps(d, _):
            n = 2 + 2 * d
            step(n, 0, True)
            step(n + 1, 1, True)
            return 0

        lax.fori_loop(0, (nt - 2) // 2, diag_steps, 0)

        def off_steps(d, _):
            n = nt + 2 * d
            step(n, 0, False)
            step(n + 1, 1, False)
            return 0

        lax.fori_loop(0, extra // 2, off_steps, 0)

        @pl.when(extra % 2 == 1)
        def _():
            step(total - 1, 0, False)
            softmax(total - 1, 0)
            accum(total - 2, 1)
            accum(total - 1, 0)

        @pl.when(extra % 2 == 0)
        def _():
            softmax(total - 1, 1)
            accum(total - 2, 0)
            accum(total - 1, 1)

        l = acc_all[HEAD_DIM:HEAD_DIM + 1, :]
        o_ref[0] = acc_all[0:HEAD_DIM, :] / l
        lse_ref[0] = m_all[...] + jnp.log(l)

    smem = pl.BlockSpec(memory_space=pltpu.SMEM)
    return pl.pallas_call(
        body,
        name="fox_fwd",
        grid=(nh,),
        in_specs=[smem, smem, smem,
                  pl.BlockSpec((1, AUG, s_len), lambda h: (h, 0, 0)),
                  pl.BlockSpec((1, s_len, AUG), lambda h: (h, 0, 0)),
                  pl.BlockSpec((1, AUG, s_len), lambda h: (h, 0, 0))],
        out_specs=[pl.BlockSpec((1, HEAD_DIM, s_len), lambda h: (h, 0, 0)),
                   pl.BlockSpec((1, 1, s_len), lambda h: (h, 0, 0))],
        out_shape=[_sds((nh, HEAD_DIM, s_len), f32), _sds((nh, 1, s_len), f32)],
        scratch_shapes=[pltpu.VMEM((t, t), f32), pltpu.VMEM((t, t), f32), pltpu.VMEM((t, t), bf16),
                        pltpu.VMEM((t, t), bf16), pltpu.VMEM((1, t), f32), pltpu.VMEM((1, t), f32),
                        pltpu.VMEM((1, s_len), f32), pltpu.VMEM((ACC_ROWS, s_len), f32)],
        compiler_params=_params(("arbitrary",)),
    )(npairs, pair_q, pair_k, qat, ka, vat)


SWA_TS = 512


SWA_W = SWA_GROUP * BLOCK


def _swa_bias_call(rel_bias, bucket_t):
    def body(rb_ref, bk_ref, b_ref, b0_ref):
        bk = bk_ref[...]
        row = lax.broadcasted_iota(jnp.int32, (2 * BLOCK, BLOCK), 0)
        for h in range(SWA_HEADS):
            acc = jnp.full((2 * BLOCK, BLOCK), NEG_INF, f32)
            for b in range(NUM_BUCKETS):
                acc = jnp.where(bk == b, rb_ref[b, h], acc)
            g, hh = divmod(h, SWA_GROUP)
            b_ref[g, :, hh * BLOCK:(hh + 1) * BLOCK] = acc
            b0_ref[g, :, hh * BLOCK:(hh + 1) * BLOCK] = jnp.where(row < BLOCK, NEG_INF, acc)

    return pl.pallas_call(
        body,
        name="swa_bias",
        in_specs=[pl.BlockSpec(memory_space=pltpu.SMEM), pl.BlockSpec(memory_space=pltpu.VMEM)],
        out_shape=[_sds((SWA_KV_HEADS, 2 * BLOCK, SWA_W), f32)] * 2,
        compiler_params=_params(),
    )(rel_bias, bucket_t)


def _swa_bias_bwd_call(dbias, bucket_t):
    def body(d_ref, bk_ref, o_ref):
        bk = bk_ref[...]
        row = lax.broadcasted_iota(jnp.int32, (NUM_BUCKETS, 128), 0)
        col = lax.broadcasted_iota(jnp.int32, (NUM_BUCKETS, 128), 1)
        out = jnp.zeros((NUM_BUCKETS, 128), f32)
        for h in range(SWA_HEADS):
            g, hh = divmod(h, SWA_GROUP)
            d = d_ref[g, :, hh * BLOCK:(hh + 1) * BLOCK]
            for b in range(NUM_BUCKETS):
                val = jnp.sum(jnp.sum(jnp.where(bk == b, d, 0.0), axis=1, keepdims=True), axis=0, keepdims=True)
                out = jnp.where((row == b) & (col == h), val, out)
        o_ref[...] = out

    return pl.pallas_call(
        body,
        name="swa_bias_bwd",
        out_shape=_sds((NUM_BUCKETS, 128), f32),
        compiler_params=_params(),
    )(dbias, bucket_t)


def _sink_row(sink_ref, g):
    return jnp.concatenate([jnp.full((1, BLOCK), sink_ref[g * SWA_GROUP + hh], f32) for hh in range(SWA_GROUP)], axis=1)


def _group_lanes(ref, g, cols):
    return jnp.concatenate([ref[g * SWA_GROUP + hh, :, cols] for hh in range(SWA_GROUP)], axis=1)


def _swa_fwd_call(qt, k, vta, bias_t, bias0_t, sink):
    s_len = qt.shape[2]
    ts = SWA_TS
    nb = ts // BLOCK

    def body(qt_ref, kc_ref, kp_ref, vc_ref, vp_ref, b_ref, b0_ref, sink_ref, o_ref, lse_ref):
        first = pl.program_id(0) == 0
        kall = [jnp.concatenate([kp_ref[g], kc_ref[g]], axis=0) for g in range(SWA_KV_HEADS)]
        vall = [jnp.concatenate([vp_ref[g], vc_ref[g]], axis=1) for g in range(SWA_KV_HEADS)]
        sinks = [_sink_row(sink_ref, g) for g in range(SWA_KV_HEADS)]
        items = [(g, b) for g in range(SWA_KV_HEADS) for b in range(nb)]

        def scores(g, b):
            qg = _group_lanes(qt_ref, g, slice(b * BLOCK, (b + 1) * BLOCK))
            bias_b = b_ref[g]
            if b == 0:
                bias_b = jnp.where(first, b0_ref[g], bias_b)
            return jnp.dot(kall[g][b * BLOCK:(b + 2) * BLOCK], qg, preferred_element_type=f32) + bias_b

        def finish(g, b, st):
            m = jnp.maximum(jnp.max(st, axis=0, keepdims=True), sinks[g])
            pt = jnp.exp(st - m)
            acc = jnp.dot(vall[g][:, b * BLOCK:(b + 2) * BLOCK], pt.astype(bf16), preferred_element_type=f32)
            l = acc[HEAD_DIM:HEAD_DIM + 1, :] + jnp.exp(sinks[g] - m)
            return acc[0:HEAD_DIM, :] / l, m + jnp.log(l)

        outs, lses = {}, {}
        st_next = scores(*items[0])
        for idx, (g, b) in enumerate(items):
            st = st_next
            if idx + 1 < len(items):
                st_next = scores(*items[idx + 1])
            outs[g, b], lses[g, b] = finish(g, b, st)
        for g in range(SWA_KV_HEADS):
            for hh in range(SWA_GROUP):
                lanes = slice(hh * BLOCK, (hh + 1) * BLOCK)
                o_ref[g * SWA_GROUP + hh] = jnp.concatenate([outs[g, b][:, lanes] for b in range(nb)], axis=1)
                lse_ref[g * SWA_GROUP + hh] = jnp.concatenate([lses[g, b][:, lanes] for b in range(nb)], axis=1)

    def prev_blk(n):
        return jnp.maximum(n * nb - 1, 0)

    bspec = pl.BlockSpec((SWA_KV_HEADS, 2 * BLOCK, SWA_W), lambda n: (0, 0, 0))
    return pl.pallas_call(
        body,
        name="swa_fwd",
        grid=(s_len // ts,),
        in_specs=[pl.BlockSpec((SWA_HEADS, HEAD_DIM, ts), lambda n: (0, 0, n)),
                  pl.BlockSpec((SWA_KV_HEADS, ts, HEAD_DIM), lambda n: (0, n, 0)),
                  pl.BlockSpec((SWA_KV_HEADS, BLOCK, HEAD_DIM), lambda n: (0, prev_blk(n), 0)),
                  pl.BlockSpec((SWA_KV_HEADS, AUG, ts), lambda n: (0, 0, n)),
                  pl.BlockSpec((SWA_KV_HEADS, AUG, BLOCK), lambda n: (0, 0, prev_blk(n))),
                  bspec, bspec, pl.BlockSpec(memory_space=pltpu.SMEM)],
        out_specs=[pl.BlockSpec((SWA_HEADS, HEAD_DIM, ts), lambda n: (0, 0, n)),
                   pl.BlockSpec((SWA_HEADS, 1, ts), lambda n: (0, 0, n))],
        out_shape=[_sds((SWA_HEADS, HEAD_DIM, s_len), f32), _sds((SWA_HEADS, 1, s_len), f32)],
        compiler_params=_params(("arbitrary",)),
    )(qt, k, k, vta, vta, bias_t, bias0_t, sink)


def _head_selector():
    sel = np.zeros((512, 128), np.float32)
    for h in range(8):
        sel[h * HEAD_DIM:(h + 1) * HEAD_DIM, h] = 1.0
    return sel


def _post_call(of, fz, osw, sz, x2, tgt, wo, ln_g, ln_b, sel, tm):
    s_len = x2.shape[0]

    def body(of_ref, fz_ref, os_ref, sz_ref, x_ref, t_ref, wo_ref, g_ref, b_ref, sel_ref,
             dh_ref, dof_ref, dfz_ref, dos_ref, dsz_ref, dlf_ref, dls_ref, dwo_ref, dg_ref, db_ref, loss_ref):
        n = pl.program_id(0)

        @pl.when(n == 0)
        def _():
            dwo_ref[...] = jnp.zeros_like(dwo_ref)
            dg_ref[...] = jnp.zeros_like(dg_ref)
            db_ref[...] = jnp.zeros_like(db_ref)
            loss_ref[...] = jnp.zeros_like(loss_ref)

        gam = g_ref[...]
        sel_m = sel_ref[...]

        def forward(r):
            o_f = of_ref[:, r].T
            o_s = os_ref[:, r].T
            fz = fz_ref[r, :]
            sz = sz_ref[r, :]
            sg_f = jax.nn.sigmoid(fz)
            sg_s = jax.nn.sigmoid(sz)
            silu_f = fz * sg_f
            silu_s = sz * sg_s
            mixed = jnp.concatenate([o_f * silu_f, o_s * silu_s], axis=1).astype(bf16)
            y = jnp.dot(mixed, wo_ref[...], preferred_element_type=f32)
            return o_f, o_s, fz, sz, sg_f, sg_s, silu_f, silu_s, mixed, y

        def norm_and_back(r, fwd):
            mixed, y = fwd[8], fwd[9]
            h = ALPHA * x_ref[r, :] + y
            mu = jnp.mean(h, axis=1, keepdims=True)
            hc = h - mu
            var = jnp.mean(hc * hc, axis=1, keepdims=True)
            rstd = lax.rsqrt(var + LN_EPS)
            xhat = hc * rstd
            out = xhat * gam + b_ref[...]
            err = out - t_ref[r, :]
            tok_loss = jnp.mean(err * err, axis=1, keepdims=True)
            loss_ref[...] += 0.5 * jnp.sum(tok_loss, axis=0, keepdims=True)
            dout = err * (1.0 / D_MODEL)
            dg_ref[...] += jnp.sum(dout * xhat, axis=0, keepdims=True)
            db_ref[...] += jnp.sum(dout, axis=0, keepdims=True)
            dxh = dout * gam
            m1 = jnp.mean(dxh, axis=1, keepdims=True)
            m2 = jnp.mean(dxh * xhat, axis=1, keepdims=True)
            dh = rstd * (dxh - m1 - xhat * m2)
            dh_ref[r, :] = dh
            dyb = dh.astype(bf16)
            dmix = lax.dot_general(dyb, wo_ref[...], NT, preferred_element_type=f32)
            dwo_ref[...] += lax.dot_general(mixed, dyb, TN, preferred_element_type=f32)
            return dmix

        def head_sums(prod):
            hi = prod.astype(bf16)
            lo = (prod - hi.astype(f32)).astype(bf16)
            return (jnp.dot(hi, sel_m, preferred_element_type=f32) + jnp.dot(lo, sel_m, preferred_element_type=f32))

        def gates_back(r, fwd, dmix):
            o_f, o_s, fz, sz, sg_f, sg_s, silu_f, silu_s = fwd[:8]
            dm_f = dmix[:, :512]
            dm_s = dmix[:, 512:]
            do_f = dm_f * silu_f
            do_s = dm_s * silu_s
            dfz_ref[r, :] = (dm_f * o_f * (sg_f * (1.0 + fz * (1.0 - sg_f)))).astype(bf16)
            dsz_ref[r, :] = (dm_s * o_s * (sg_s * (1.0 + sz * (1.0 - sg_s)))).astype(bf16)
            dof_ref[:, r] = do_f.T.astype(bf16)
            dos_ref[:, r] = do_s.T.astype(bf16)
            dlf_ref[:, r] = head_sums(do_f * o_f).T[:FOX_HEADS, :]
            dls_ref[:, r] = head_sums(do_s * o_s).T[:SWA_HEADS, :]

        halves = [slice(k * (tm // 2), (k + 1) * (tm // 2)) for k in range(2)]
        fwds = [forward(r) for r in halves]
        dmixes = [norm_and_back(r, f) for r, f in zip(halves, fwds)]
        for r, f, d in zip(halves, fwds, dmixes):
            gates_back(r, f, d)

    feat = pl.BlockSpec((512, tm), lambda n: (0, n))
    rows8 = pl.BlockSpec((8, tm), lambda n: (0, n))
    half = pl.BlockSpec((tm, 512), lambda n: (n, 0))
    fullw = pl.BlockSpec((tm, D_MODEL), lambda n: (n, 0))
    vec = pl.BlockSpec((1, D_MODEL), lambda n: (0, 0))
    return pl.pallas_call(
        body,
        name="post_fwd_bwd",
        grid=(s_len // tm,),
        in_specs=[feat, half, feat, half, fullw, fullw,
                  pl.BlockSpec((D_MODEL, D_MODEL), lambda n: (0, 0)), vec, vec,
                  pl.BlockSpec((512, 128), lambda n: (0, 0))],
        out_specs=[fullw, feat, half, feat, half, rows8, rows8,
                   pl.BlockSpec((D_MODEL, D_MODEL), lambda n: (0, 0)), vec, vec,
                   pl.BlockSpec((1, 1), lambda n: (0, 0))],
        out_shape=[_sds((s_len, D_MODEL), f32), _sds((512, s_len), bf16), _sds((s_len, 512), bf16),
                   _sds((512, s_len), bf16), _sds((s_len, 512), bf16),
                   _sds((FOX_HEADS, s_len), f32), _sds((SWA_HEADS, s_len), f32),
                   _sds((D_MODEL, D_MODEL), f32), _sds((1, D_MODEL), f32), _sds((1, D_MODEL), f32),
                   _sds((1, 1), f32)],
        compiler_params=_params(("arbitrary",), VMEM_LIMIT_BIG),
    )(of, fz, osw, sz, x2, tgt, wo, ln_g, ln_b, sel)


def _fox_bwd_call(ka, kat, v, qat, dot, lse_row, dl_row, npairs, pair_q, pair_k):
    nh, s_len, _ = ka.shape
    t = FOX_T
    nt = s_len // t
    kmax = nt * (nt - 1) // 2
    assert nt >= 2 and nt % 2 == 0
    ck_slot = HEAD_DIM + 3
    cq_slot = HEAD_DIM

    def body(np_ref, pq_ref, pk_ref, ka_ref, kat_ref, v_ref, qat_ref, dot_ref, lse_ref, dl_ref,
             dq_ref, dk_ref, dv_ref, dcq_ref, dck_ref, dqt_all, dkat_all, dvt_all, p0, p1, ds0, ds1):
        h = pl.program_id(0)
        extra = np_ref[h]
        total = nt + extra
        dqt_all[...] = jnp.zeros(dqt_all.shape, f32)
        dkat_all[...] = jnp.zeros(dkat_all.shape, f32)
        dvt_all[...] = jnp.zeros(dvt_all.shape, f32)
        pbuf, dsbuf = (p0, p1), (ds0, ds1)

        def pair(n):
            return _fox_pair(n, nt, kmax, h, pq_ref, pk_ref)

        def probs(n, b, masked):
            i, j = pair(n)
            qc, kr = _tile_cols(i, t), _tile_cols(j, t)
            st = jnp.dot(ka_ref[0, kr, :], qat_ref[0, :, qc], preferred_element_type=f32)
            dpt = jnp.dot(v_ref[0, kr, :], dot_ref[0, :, qc], preferred_element_type=f32)
            if masked:
                st = jnp.where(_causal_keep(t), st, NEG_INF)
            pt = jnp.exp(st - lse_ref[0, :, qc])
            pbuf[b][...] = pt.astype(bf16)
            dsbuf[b][...] = (pbuf[b][...].astype(f32) * (dpt - dl_ref[0, :, qc])).astype(bf16)

        def grads(n, b):
            i, j = pair(n)
            qc, kc = _tile_cols(i, t), _tile_cols(j, t)
            dvt_all[:, kc] += lax.dot_general(dot_ref[0, :, qc], pbuf[b][...], NT, preferred_element_type=f32)
            dkat_all[:, kc] += lax.dot_general(qat_ref[0, 0:ACC_ROWS, qc], dsbuf[b][...], NT, preferred_element_type=f32)
            dqt_all[:, qc] += jnp.dot(kat_ref[0, 0:ACC_ROWS, kc], dsbuf[b][...], preferred_element_type=f32)

        def step(n, b, masked):
            i, j = pair(n)
            qc, kr = _tile_cols(i, t), _tile_cols(j, t)
            i1, j1 = pair(n - 1)
            qc1, kc1 = _tile_cols(i1, t), _tile_cols(j1, t)
            c = 1 - b
            st = jnp.dot(ka_ref[0, kr, :], qat_ref[0, :, qc], preferred_element_type=f32)
            dvt_all[:, kc1] += lax.dot_general(dot_ref[0, :, qc1], pbuf[c][...], NT, preferred_element_type=f32)
            if masked:
                st = jnp.where(_causal_keep(t), st, NEG_INF)
            pt = jnp.exp(st - lse_ref[0, :, qc])
            pbuf[b][...] = pt.astype(bf16)
            dpt = jnp.dot(v_ref[0, kr, :], dot_ref[0, :, qc], preferred_element_type=f32)
            dkat_all[:, kc1] += lax.dot_general(qat_ref[0, 0:ACC_ROWS, qc1], dsbuf[c][...], NT, preferred_element_type=f32)
            dqt_all[:, qc1] += jnp.dot(kat_ref[0, 0:ACC_ROWS, kc1], dsbuf[c][...], preferred_element_type=f32)
            dsbuf[b][...] = (pbuf[b][...].astype(f32) * (dpt - dl_ref[0, :, qc])).astype(bf16)

        probs(0, 0, True)
        step(1, 1, True)

        def four_steps(n, masked):
            step(n, 0, masked)
            step(n + 1, 1, masked)
            step(n + 2, 0, masked)
            step(n + 3, 1, masked)

        def diag_quads(d, _):
            four_steps(2 + 4 * d, True)
            return 0

        lax.fori_loop(0, (nt - 2) // 4, diag_quads, 0)
        if (nt - 2) % 4:
            step(nt - 2, 0, True)
            step(nt - 1, 1, True)

        def off_quads(d, _):
            four_steps(nt + 4 * d, False)
            return 0

        quads = extra // 4
        lax.fori_loop(0, quads, off_quads, 0)

        def off_steps(d, _):
            n = nt + 4 * quads + 2 * d
            step(n, 0, False)
            step(n + 1, 1, False)
            return 0

        lax.fori_loop(0, (extra % 4) // 2, off_steps, 0)

        @pl.when(extra % 2 == 1)
        def _():
            step(total - 1, 0, False)
            grads(total - 1, 0)

        @pl.when(extra % 2 == 0)
        def _():
            grads(total - 1, 1)

        dq_ref[0] = (dqt_all[0:HEAD_DIM, :] * SCALE).astype(bf16)
        dk_ref[0] = dkat_all[0:HEAD_DIM, :].astype(bf16)
        dv_ref[0] = dvt_all[...].astype(bf16)
        dcq_ref[0] = dqt_all[cq_slot:cq_slot + 1, :]
        dck_ref[0] = dkat_all[ck_slot:ck_slot + 1, :]

    smem = pl.BlockSpec(memory_space=pltpu.SMEM)
    rows = pl.BlockSpec((1, s_len, AUG), lambda h: (h, 0, 0))
    feat = pl.BlockSpec((1, AUG, s_len), lambda h: (h, 0, 0))
    feat64 = pl.BlockSpec((1, HEAD_DIM, s_len), lambda h: (h, 0, 0))
    rowv = pl.BlockSpec((1, 1, s_len), lambda h: (h, 0, 0))
    return pl.pallas_call(
        body,
        name="fox_bwd",
        grid=(nh,),
        in_specs=[smem, smem, smem, rows, feat, pl.BlockSpec((1, s_len, HEAD_DIM), lambda h: (h, 0, 0)), feat, feat64,
                  rowv, rowv],
        out_specs=[feat64, feat64, feat64, rowv, rowv],
        out_shape=[_sds((nh, HEAD_DIM, s_len), bf16)] * 3 + [_sds((nh, 1, s_len), f32)] * 2,
        scratch_shapes=[pltpu.VMEM((ACC_ROWS, s_len), f32), pltpu.VMEM((ACC_ROWS, s_len), f32),
                        pltpu.VMEM((HEAD_DIM, s_len), f32)]
                       + [pltpu.VMEM((t, t), bf16)] * 4,
        compiler_params=_params(("arbitrary",)),
    )(npairs, pair_q, pair_k, ka, kat, v, qat, dot, lse_row, dl_row)


def _swa_bwd_call(qt, k, kt, v, dot, lse, dl, bias_t, bias0_t, sink):
    s_len = qt.shape[2]
    ts = SWA_TS
    nb = ts // BLOCK
    nsteps = s_len // ts

    def body(qt_ref, kc_ref, kp_ref, ktc_ref, ktp_ref, vc_ref, vp_ref, dot_ref, lse_ref, dl_ref, b_ref, b0_ref,
             sink_ref, dq_ref, dk_ref, dv_ref, dbias_ref, dsink_ref, dk_s, dv_s, tail_k, tail_v, sk_s):
        n = pl.program_id(0)

        @pl.when(n == 0)
        def _():
            dbias_ref[...] = jnp.zeros_like(dbias_ref)
            sk_s[...] = jnp.zeros_like(sk_s)

        @pl.when(n < nsteps)
        def _():
            first = n == 0
            dk_s[...] = jnp.zeros_like(dk_s)
            dv_s[...] = jnp.zeros_like(dv_s)
            groups = range(SWA_KV_HEADS)
            kall = [jnp.concatenate([kp_ref[g], kc_ref[g]], axis=0) for g in groups]
            vall = [jnp.concatenate([vp_ref[g], vc_ref[g]], axis=0) for g in groups]
            ktall = [jnp.concatenate([ktp_ref[g], ktc_ref[g]], axis=1) for g in groups]
            sinks = [_sink_row(sink_ref, g) for g in groups]
            items = [(g, b) for g in groups for b in range(nb)]

            def products(g, b):
                cols = slice(b * BLOCK, (b + 1) * BLOCK)
                win = slice(b * BLOCK, (b + 2) * BLOCK)
                qg = _group_lanes(qt_ref, g, cols)
                dog = _group_lanes(dot_ref, g, cols)
                bias_b = b_ref[g]
                if b == 0:
                    bias_b = jnp.where(first, b0_ref[g], bias_b)
                st = jnp.dot(kall[g][win], qg, preferred_element_type=f32) + bias_b
                dpt = jnp.dot(vall[g][win], dog, preferred_element_type=f32)
                return qg, dog, st, dpt

            def finish(g, b, qg, dog, st, dpt):
                cols = slice(b * BLOCK, (b + 1) * BLOCK)
                win = slice(b * BLOCK, (b + 2) * BLOCK)
                lse_r = _group_lanes(lse_ref, g, cols)
                dl_r = _group_lanes(dl_ref, g, cols)
                pt = jnp.exp(st - lse_r)
                dst = pt * (dpt - dl_r)
                dsb = dst.astype(bf16)
                dk_s[g, :, win] += lax.dot_general(qg, dsb, NT, preferred_element_type=f32)
                dv_s[g, :, win] += lax.dot_general(dog, pt.astype(bf16), NT, preferred_element_type=f32)
                dqg = jnp.dot(ktall[g][:, win], dsb, preferred_element_type=f32) * SCALE
                return dqg, dst, -jnp.exp(sinks[g] - lse_r) * dl_r

            nxt = products(*items[0])
            for idx, (g, b) in enumerate(items):
                cur = nxt
                if idx + 1 < len(items):
                    nxt = products(*items[idx + 1])
                dqg, dst, sk = finish(g, b, *cur)
                dbias_ref[g] += dst
                sk_s[g] += sk
                for hh in range(SWA_GROUP):
                    dq_ref[g * SWA_GROUP + hh, :, b * BLOCK:(b + 1) * BLOCK] = (
                        dqg[:, hh * BLOCK:(hh + 1) * BLOCK].astype(bf16))

        @pl.when(n > 0)
        def _():
            last = slice(ts - BLOCK, ts)
            for g in range(SWA_KV_HEADS):
                add_k = jnp.where(n < nsteps, dk_s[g, :, 0:BLOCK], 0.0)
                add_v = jnp.where(n < nsteps, dv_s[g, :, 0:BLOCK], 0.0)
                dk_ref[g, :, 0:ts - BLOCK] = tail_k[g, :, 0:ts - BLOCK].astype(bf16)
                dv_ref[g, :, 0:ts - BLOCK] = tail_v[g, :, 0:ts - BLOCK].astype(bf16)
                dk_ref[g, :, last] = (tail_k[g, :, last] + add_k).astype(bf16)
                dv_ref[g, :, last] = (tail_v[g, :, last] + add_v).astype(bf16)

        @pl.when(n < nsteps)
        def _():
            tail_k[...] = dk_s[:, :, BLOCK:]
            tail_v[...] = dv_s[:, :, BLOCK:]

        @pl.when(n == nsteps)
        def _():
            row = lax.broadcasted_iota(jnp.int32, (SWA_HEADS, 128), 0)
            out = jnp.zeros((SWA_HEADS, 128), f32)
            for h in range(SWA_HEADS):
                g, hh = divmod(h, SWA_GROUP)
                val = jnp.sum(sk_s[g, :, hh * BLOCK:(hh + 1) * BLOCK], axis=1, keepdims=True)
                out = jnp.where(row == h, val, out)
            dsink_ref[...] = out

    last_step = nsteps - 1

    def cl(n):
        return jnp.minimum(n, last_step)

    def prev_blk(n):
        return jnp.maximum(cl(n) * nb - 1, 0)

    feat8 = pl.BlockSpec((SWA_HEADS, HEAD_DIM, ts), lambda n: (0, 0, cl(n)))
    rows8 = pl.BlockSpec((SWA_HEADS, 1, ts), lambda n: (0, 0, cl(n)))
    cur = pl.BlockSpec((SWA_KV_HEADS, ts, HEAD_DIM), lambda n: (0, cl(n), 0))
    prev = pl.BlockSpec((SWA_KV_HEADS, BLOCK, HEAD_DIM), lambda n: (0, prev_blk(n), 0))
    curt = pl.BlockSpec((SWA_KV_HEADS, HEAD_DIM, ts), lambda n: (0, 0, cl(n)))
    prevt = pl.BlockSpec((SWA_KV_HEADS, HEAD_DIM, BLOCK), lambda n: (0, 0, prev_blk(n)))
    bspec = pl.BlockSpec((SWA_KV_HEADS, 2 * BLOCK, SWA_W), lambda n: (0, 0, 0))
    kvout = pl.BlockSpec((SWA_KV_HEADS, HEAD_DIM, ts), lambda n: (0, 0, jnp.maximum(n - 1, 0)))
    return pl.pallas_call(
        body,
        name="swa_bwd",
        grid=(nsteps + 1,),
        in_specs=[feat8, cur, prev, curt, prevt, cur, prev, feat8, rows8, rows8, bspec, bspec,
                  pl.BlockSpec(memory_space=pltpu.SMEM)],
        out_specs=[feat8, kvout, kvout, bspec, pl.BlockSpec((SWA_HEADS, 128), lambda n: (0, 0))],
        out_shape=[_sds((SWA_HEADS, HEAD_DIM, s_len), bf16), _sds((SWA_KV_HEADS, HEAD_DIM, s_len), bf16),
                   _sds((SWA_KV_HEADS, HEAD_DIM, s_len), bf16),
                   _sds((SWA_KV_HEADS, 2 * BLOCK, SWA_W), f32), _sds((SWA_HEADS, 128), f32)],
        scratch_shapes=[pltpu.VMEM((SWA_KV_HEADS, HEAD_DIM, ts + BLOCK), f32),
                        pltpu.VMEM((SWA_KV_HEADS, HEAD_DIM, ts + BLOCK), f32),
                        pltpu.VMEM((SWA_KV_HEADS, HEAD_DIM, ts), f32),
                        pltpu.VMEM((SWA_KV_HEADS, HEAD_DIM, ts), f32),
                        pltpu.VMEM((SWA_KV_HEADS, 1, SWA_W), f32)],
        compiler_params=_params(("arbitrary",)),
    )(qt, k, k, kt, kt, v, v, dot, lse, dl, bias_t, bias0_t, sink)


def _dproj_specs(tm):
    half = pl.BlockSpec((tm, 512), lambda i: (i, 0))
    feat = pl.BlockSpec((512, tm), lambda i: (0, i))
    feat_kv = pl.BlockSpec((128, tm), lambda i: (0, i))
    return [feat, feat, feat, half, feat, feat_kv, feat_kv, half, feat_kv]


def _dx_exchange_call(dh, pieces, w_t, bs, tm):
    s_len = dh.shape[0]
    n = len(bs)
    last = s_len // tm - 1

    def body(*refs):
        dh_ref, dqf_ref, dkf_ref, dvf_ref, dfz_ref, dqs_ref, dks_ref, dvs_ref, dsz_ref, dfft_ref, w_ref = refs[:11]
        b_refs = refs[11:11 + n]
        dx_ref = refs[11 + n]
        r_refs = refs[12 + n:12 + 2 * n]
        sems = refs[12 + 2 * n:]
        i = pl.program_id(0)

        @pl.when(i == 0)
        def _():
            _exchange_start(b_refs, r_refs, sems)

        def tr(ref):
            return ref[...].astype(f32).T.astype(bf16)

        dp = jnp.concatenate([tr(dqf_ref), tr(dkf_ref), tr(dvf_ref), dfz_ref[...], tr(dqs_ref), tr(dks_ref),
                              tr(dvs_ref), dsz_ref[...], tr(dfft_ref)], axis=1)
        dx_ref[...] = ALPHA * dh_ref[...] + jnp.dot(dp, w_ref[...], preferred_element_type=f32)

        @pl.when(i == last)
        def _():
            _exchange_wait(b_refs, r_refs, sems)

    fullw = pl.BlockSpec((tm, D_MODEL), lambda i: (i, 0))
    any_spec = pl.BlockSpec(memory_space=pl.ANY)
    out = pl.pallas_call(
        body,
        name="dx_bwd_exchange",
        grid=(s_len // tm,),
        in_specs=[fullw] + _dproj_specs(tm) + [pl.BlockSpec((A_W, D_MODEL), lambda i: (0, 0))] + [any_spec] * n,
        out_specs=[fullw] + [any_spec] * n,
        out_shape=[_sds((s_len, D_MODEL), f32)] + [_sds(b.shape, b.dtype) for b in bs],
        scratch_shapes=[pltpu.SemaphoreType.DMA((7 * n,)), pltpu.SemaphoreType.DMA((7 * n,)),
                        pltpu.SemaphoreType.DMA((n,))],
        compiler_params=_params(("arbitrary",)),
    )(dh, *pieces, w_t, *bs)
    return out[0], out[1:]


DW_STAGE_ROWS = 384


def _dw_exchange_call(x2, pieces, bs, tm):
    s_len = x2.shape[0]
    nt = s_len // tm
    n = len(bs)

    def body(*refs):
        x_ref, dqf_ref, dkf_ref, dvf_ref, dfz_ref, dqs_ref, dks_ref, dvs_ref, dsz_ref, dfft_ref = refs[:10]
        b_refs = refs[10:10 + n]
        dw_ref = refs[10 + n]
        r_refs = refs[11 + n:11 + 2 * n]
        acc_ref, stage_ref, sem = refs[11 + 2 * n:14 + 2 * n]
        sems = refs[14 + 2 * n:]
        i = pl.program_id(0)

        @pl.when(i == 0)
        def _():
            _exchange_start(b_refs, r_refs, sems)
            acc_ref[...] = jnp.zeros_like(acc_ref)

        xb = x_ref[...].astype(bf16)

        def add_feat(off, lhs):
            acc_ref[off:off + lhs.shape[0], :] += jnp.dot(lhs, xb, preferred_element_type=f32)

        def add_rows(off, piece):
            acc_ref[off:off + piece.shape[1], :] += lax.dot_general(piece, xb, TN, preferred_element_type=f32)

        add_feat(A_FQ, dqf_ref[...])
        add_feat(A_FK, dkf_ref[...])
        add_feat(A_FV, dvf_ref[...])
        add_rows(A_FZ, dfz_ref[...])
        add_feat(A_SQ, dqs_ref[...])
        add_feat(A_SK, dks_ref[...])
        add_feat(A_SV, dvs_ref[...])
        add_rows(A_SZ, dsz_ref[...])
        add_feat(A_FF, dfft_ref[...].astype(bf16))

        @pl.when(i == nt - 1)
        def _():
            for r in range(A_W // DW_STAGE_ROWS):
                rows = slice(r * DW_STAGE_ROWS, (r + 1) * DW_STAGE_ROWS)
                stage_ref[...] = acc_ref[rows, :].astype(bf16)
                cp = pltpu.make_async_copy(stage_ref, dw_ref.at[rows, :], sem)
                cp.start()
                cp.wait()
            _exchange_wait(b_refs, r_refs, sems)

    any_spec = pl.BlockSpec(memory_space=pl.ANY)
    out = pl.pallas_call(
        body,
        name="dw_bwd_exchange",
        grid=(nt,),
        in_specs=[pl.BlockSpec((tm, D_MODEL), lambda i: (i, 0))] + _dproj_specs(tm) + [any_spec] * n,
        out_specs=[any_spec] * (1 + n),
        out_shape=[_sds((A_W, D_MODEL), bf16)] + [_sds(b.shape, b.dtype) for b in bs],
        scratch_shapes=[pltpu.VMEM((A_W, D_MODEL), f32), pltpu.VMEM((DW_STAGE_ROWS, D_MODEL), bf16),
                        pltpu.SemaphoreType.DMA, pltpu.SemaphoreType.DMA((7 * n,)), pltpu.SemaphoreType.DMA((7 * n,)),
                        pltpu.SemaphoreType.DMA((n,))],
        compiler_params=_params(("arbitrary",), VMEM_LIMIT_BIG),
    )(x2, *pieces, *bs)
    return out[0], out[1:]


def _adam_call(recv, w, m, v, tc, name):
    rows, cols = w.shape

    def body(r_ref, w_ref, m_ref, v_ref, g_ref, d_ref, mo_ref, vo_ref):
        g = r_ref[0].astype(f32)
        for p in range(1, N_DEV):
            g = g + r_ref[p].astype(f32)
        mn = ADAM_B1 * m_ref[...] + (1.0 - ADAM_B1) * g
        vn = ADAM_B2 * v_ref[...] + (1.0 - ADAM_B2) * (g * g)
        m_hat = mn / (1.0 - ADAM_B1 ** ADAM_STEP)
        v_hat = vn / (1.0 - ADAM_B2 ** ADAM_STEP)
        g_ref[...] = g
        d_ref[...] = -ADAM_LR * (m_hat / (jnp.sqrt(v_hat) + ADAM_EPS) + ADAM_WD * w_ref[...])
        mo_ref[...] = mn
        vo_ref[...] = vn

    blk = pl.BlockSpec((rows, tc), lambda i: (0, i))
    return pl.pallas_call(
        body,
        name=name,
        grid=(cols // tc,),
        in_specs=[pl.BlockSpec((N_DEV, rows, tc), lambda i: (0, 0, i)), blk, blk, blk],
        out_specs=[blk] * 4,
        out_shape=[_sds((rows, cols), f32)] * 4,
        compiler_params=_params(("arbitrary",)),
    )(recv, w, m, v)


def _rows_to_shards(parts, shard):
    blocks = []
    for d in range(N_DEV):
        lo, hi, start, pieces = d * shard, (d + 1) * shard, 0, []
        for part in parts:
            a, b = max(lo, start), min(hi, start + part.shape[0])
            if a < b:
                pieces.append(part[a - start:b - start])
            start += part.shape[0]
        blocks.append(jnp.concatenate(pieces, axis=0))
    return jnp.stack(blocks)


def _pad_cols(a, width=128):
    return jnp.pad(a, ((0, 0), (0, width - a.shape[1])))


def _pack_small(ln_g, ln_b, rel, b_f, sink):
    return jnp.concatenate([
        ln_g.reshape(8, 128), ln_b.reshape(8, 128), _pad_cols(rel),
        jnp.pad(_pad_cols(b_f), ((0, 7), (0, 0))), jnp.pad(_pad_cols(sink), ((0, 7), (0, 0)))], axis=0)


def _unpack_small(p):
    return (p[0:8].reshape(1, D_MODEL), p[8:16].reshape(1, D_MODEL), p[16:48, 0:8], p[48:49, 0:8], p[56:57, 0:8])


def kernel(x, w_in, b_f, rel_bias, sink, w_o, ln_g, ln_b, loss_target, m_w_in, m_b_f, m_rel_bias, m_sink, m_w_o, m_ln_g, m_ln_b, v_w_in, v_b_f, v_rel_bias, v_sink, v_w_o, v_ln_g, v_ln_b):
    x2 = x[0]
    tgt = loss_target[0]
    s_len = x2.shape[0]
    shard = w_in.shape[2]

    w_in_t = jnp.transpose(w_in[0])
    (g_in,) = _gather_call([w_in_t.astype(bf16)])
    wt_full = g_in.reshape(N_DEV * shard, D_MODEL)
    w_t = jnp.concatenate([wt_full[:O_FF0], wt_full[O_FF1:], wt_full[O_FF0:O_FF1],
                           jnp.zeros((A_W - D_IN, D_MODEL), bf16)], axis=0)

    wo_blocks = jnp.broadcast_to(w_o[0].astype(bf16)[None], (N_DEV,) + w_o.shape[1:])
    (qft, kft, vf, fz, qst, ks, vs, sz, fft, vat, kst, vsta), (g_o,) = _proj_call(x2, w_t, [wo_blocks], 512)
    wo_full = g_o.reshape(D_MODEL, D_MODEL)
    cum, sgm = _cum_call(fft, b_f.reshape(FOX_HEADS, 1))
    qat, ka, kat, tile_stats = _augment_call(qft, kft, cum.reshape(FOX_HEADS, 1, s_len), 2048)
    npairs, pair_q, pair_k = _fox_prune_tables(tile_stats)
    o_ft, lse_f = _fox_fwd_call(qat, ka, vat, npairs, pair_q, pair_k)
    bucket_t = jnp.asarray(_t5_bucket_table().T)
    bias_t, bias0_t = _swa_bias_call(rel_bias, bucket_t)
    sink_v = sink.reshape(SWA_HEADS)
    o_st, lse_s = _swa_fwd_call(qst, ks, vsta, bias_t, bias0_t, sink_v)

    (dh, do_f, dfz, do_s, dsz, dl_f, dl_s, dwo, dg, db, loss_part) = _post_call(
        o_ft.reshape(FOX_HEADS * HEAD_DIM, s_len), fz, o_st.reshape(SWA_HEADS * HEAD_DIM, s_len), sz, x2, tgt,
        wo_full, ln_g, ln_b, jnp.asarray(_head_selector()).astype(bf16), 512)

    dqf, dkf, dvf, dcq, dck = _fox_bwd_call(ka, kat, vf, qat, do_f.reshape(FOX_HEADS, HEAD_DIM, s_len), lse_f,
                                            dl_f.reshape(FOX_HEADS, 1, s_len), npairs, pair_q, pair_k)
    dqf, dkf, dvf = (a.reshape(FOX_HEADS * HEAD_DIM, s_len) for a in (dqf, dkf, dvf))
    dfft, dbf = _cum_bwd_call(dcq.reshape(FOX_HEADS, s_len), dck.reshape(FOX_HEADS, s_len), sgm)
    dqs, dks, dvs, dbias, dsink = _swa_bwd_call(
        qst, ks, kst, vs, do_s.reshape(SWA_HEADS, HEAD_DIM, s_len), lse_s, dl_s.reshape(SWA_HEADS, 1, s_len),
        bias_t, bias0_t, sink_v)
    dqs = dqs.reshape(SWA_HEADS * HEAD_DIM, s_len)
    dks, dvs = (a.reshape(SWA_KV_HEADS * HEAD_DIM, s_len) for a in (dks, dvs))
    drel = _swa_bias_bwd_call(dbias, bucket_t)

    dwo_blocks = dwo.reshape(N_DEV, D_MODEL // N_DEV, D_MODEL).astype(bf16)
    small = _pack_small(dg, db, drel[:, 0:8], dbf[:, 0].reshape(1, 8), dsink[:, 0].reshape(1, 8))
    loss_slot = np.zeros((64, 128), bool)
    loss_slot[49, 0] = True
    small = jnp.where(jnp.asarray(loss_slot), loss_part[0, 0], small)
    small_blocks = jnp.broadcast_to(small[None], (N_DEV,) + small.shape)
    pieces = (dqf, dkf, dvf, dfz, dqs, dks, dvs, dsz, dfft)
    dw_t, (r_o, r_small) = _dw_exchange_call(x2, pieces, [dwo_blocks, small_blocks], 1024)
    dw_blocks = _rows_to_shards([dw_t[:O_FF0], dw_t[A_FF:A_FF + (O_FF1 - O_FF0)], dw_t[O_FF0:A_FF]], shard)
    dx, (r_in,) = _dx_exchange_call(dh, pieces, w_t, [dw_blocks], 256)

    win_t = [jnp.transpose(a) for a in _adam_call(
        r_in, w_in_t, jnp.transpose(m_w_in[0]), jnp.transpose(v_w_in[0]), 256, "adam_w_in")]
    g_win, d_win, nm_win, nv_win = win_t
    g_wo, d_wo, nm_wo, nv_wo = _adam_call(r_o, w_o[0], m_w_o[0], v_w_o[0], 256, "adam_w_o")
    p_w = _pack_small(ln_g, ln_b, rel_bias, b_f, sink)
    p_m = _pack_small(m_ln_g, m_ln_b, m_rel_bias, m_b_f, m_sink)
    p_v = _pack_small(v_ln_g, v_ln_b, v_rel_bias, v_b_f, v_sink)
    g_p, d_p, nm_p, nv_p = _adam_call(r_small, p_w, p_m, p_v, 128, "adam_small")

    loss = g_p[49, 0]
    g_lng, g_lnb, g_rel, g_bf, g_sink = _unpack_small(g_p)
    d_lng, d_lnb, d_rel, d_bf, d_sink = _unpack_small(d_p)
    m_lng, m_lnb, m_rel, m_bf, m_sk = _unpack_small(nm_p)
    v_lng, v_lnb, v_rel, v_bf, v_sk = _unpack_small(nv_p)
    return (loss, dx[None], g_win[None], g_bf, g_rel, g_sink, g_wo[None], g_lng, g_lnb,
            d_win[None], d_bf, d_rel, d_sink, d_wo[None], d_lng, d_lnb,
            nm_win[None], m_bf, m_rel, m_sk, nm_wo[None], m_lng, m_lnb,
            nv_win[None], v_bf, v_rel, v_sk, nv_wo[None], v_lng, v_lnb)
```
